```python
import math
import jax, jax.numpy as jnp
from jax import lax
import numpy as np

D_MODEL = 1024
BATCH = 8
SEQ = 4096
DEPTH = 2

N_EVEN = (DEPTH + 1) // 2
N_ODD = DEPTH // 2

POOL_WINDOWS = (2, 4, 8, 16)
N_POOL_GROUPS = len(POOL_WINDOWS)
POOL_DIM = D_MODEL // 2
POOL_GROUP = POOL_DIM // N_POOL_GROUPS

HEAD_DIM = 64
N_HEADS = (D_MODEL // 2) // HEAD_DIM
N_KV_HEADS = 2
GQ = N_HEADS // N_KV_HEADS
Q_DIM = N_HEADS * HEAD_DIM
KV_DIM = N_KV_HEADS * HEAD_DIM
WINDOW = 128
BLOCK = 128
ROPE_THETA = 10000.0
MIX_IN_DIM = POOL_DIM + Q_DIM + 2 * KV_DIM
MIX_OUT_DIM = POOL_DIM + Q_DIM
MAX_POS_OFFSET = 1024

SSM_EXPAND = 2
SSM_D_INNER = SSM_EXPAND * D_MODEL
SSM_HEAD_DIM = 64
SSM_HEADS = SSM_D_INNER // SSM_HEAD_DIM
SSM_GROUPS = 8
SSM_STATE = 128
SSM_CONV = 4
SSM_CHUNK = 128
SSM_CONV_DIM = SSM_D_INNER + 2 * SSM_GROUPS * SSM_STATE
SSM_IN_DIM = SSM_D_INNER + SSM_CONV_DIM + SSM_HEADS

D_FF = 2816
FFN_CONV = 3

NORM_EPS = 1e-6
SSM_NORM_EPS = 1e-5

kernel_name = "hybrid_pool_swa_ssd_convffn"


def rms_norm(x, w, eps=NORM_EPS):
    xf = x.astype(jnp.float32)
    y = xf * lax.rsqrt(jnp.mean(xf * xf, axis=-1, keepdims=True) + eps)
    return (y * w.astype(jnp.float32)).astype(x.dtype)


def causal_dwconv(x, w, b):
    K, C = w.shape
    y = lax.conv_general_dilated(
        x, w.astype(x.dtype)[:, None, :], window_strides=(1,), padding=[(K - 1, 0)],
        dimension_numbers=("NWC", "WIO", "NWC"), feature_group_count=C)
    return y + b.astype(x.dtype)


def rope_tables(positions):
    inv_freq = ROPE_THETA ** (-jnp.arange(0, HEAD_DIM, 2, dtype=jnp.float32) / HEAD_DIM)
    ang = positions.astype(jnp.float32)[..., None] * inv_freq
    ang = jnp.concatenate([ang, ang], axis=-1)
    return jnp.cos(ang)[:, :, None, :], jnp.sin(ang)[:, :, None, :]


def apply_rope(t, cos, sin):
    tf = t.astype(jnp.float32)
    half = HEAD_DIM // 2
    rot = jnp.concatenate([-tf[..., half:], tf[..., :half]], axis=-1)
    return (tf * cos + rot * sin).astype(t.dtype)


def multiscale_pool(u):
    B, S, _ = u.shape
    ug = u.reshape(B, S, N_POOL_GROUPS, POOL_GROUP).astype(jnp.float32)
    cs = jnp.pad(jnp.cumsum(ug, axis=1), ((0, 0), (1, 0), (0, 0), (0, 0)))
    t1 = jnp.arange(1, S + 1)
    means = []
    for g, w in enumerate(POOL_WINDOWS):
        upper = cs[:, 1:, g]
        lower = jnp.pad(cs[:, :S + 1 - w, g], ((0, 0), (w - 1, 0), (0, 0)))
        cnt = jnp.minimum(t1, w).astype(jnp.float32)[None, :, None]
        means.append((upper - lower) / cnt)
    return jnp.stack(means, axis=2) - ug


def sliding_window_attention(q, k, v, sinks):
    B, S, _, _ = q.shape
    nb = S // BLOCK
    qb = q.reshape(B, nb, BLOCK, N_KV_HEADS, GQ, HEAD_DIM)

    def with_prev(t):
        t = t.reshape(B, nb, BLOCK, N_KV_HEADS, HEAD_DIM)
        prev = jnp.pad(t[:, :-1], ((0, 0), (1, 0), (0, 0), (0, 0), (0, 0)))
        return jnp.concatenate([prev, t], axis=2)

    kk, vv = with_prev(k), with_prev(v)
    s = jnp.einsum("bnqkgd,bnskd->bkgnqs", qb, kk).astype(jnp.float32) * (HEAD_DIM ** -0.5)
    qi = jnp.arange(BLOCK)[:, None]
    kj = jnp.arange(2 * BLOCK)[None, :]
    rel = qi + BLOCK - kj
    band = (rel >= 0) & (rel < WINDOW)
    valid = (jnp.arange(nb)[:, None, None] > 0) | (kj[None] >= BLOCK)
    mask = band[None] & valid
    s = jnp.where(mask, s, -jnp.inf)
    sink = jnp.broadcast_to(sinks.astype(jnp.float32).reshape(1, N_KV_HEADS, GQ, 1, 1, 1),
                            s.shape[:-1] + (1,))
    p = jax.nn.softmax(jnp.concatenate([s, sink], axis=-1), axis=-1)[..., :-1]
    o = jnp.einsum("bkgnqs,bnskd->bnqkgd", p.astype(v.dtype), vv)
    return o.reshape(B, S, Q_DIM)


def pool_attention_mixer(h, cos, sin, w_in, pool_w, pool_scale, sinks, w_out):
    B, S, _ = h.shape
    proj = h @ w_in
    u, q, k, v = jnp.split(proj, [POOL_DIM, POOL_DIM + Q_DIM, POOL_DIM + Q_DIM + KV_DIM], axis=-1)
    pooled = multiscale_pool(u).astype(h.dtype)
    pooled = jnp.einsum("bsgc,gcd->bsgd", pooled, pool_w).reshape(B, S, POOL_DIM) * pool_scale
    q = apply_rope(q.reshape(B, S, N_HEADS, HEAD_DIM), cos, sin)
    k = apply_rope(k.reshape(B, S, N_KV_HEADS, HEAD_DIM), cos, sin)
    v = v.reshape(B, S, N_KV_HEADS, HEAD_DIM)
    attn = sliding_window_attention(q, k, v, sinks)
    return jnp.concatenate([pooled, attn], axis=-1) @ w_out


def ssd_scan(x, dt, A, Bm, Cm):
    b, s, h, p = x.shape
    g, n = Bm.shape[2:]
    r = h // g
    c = s // SSM_CHUNK
    X = (x * dt[..., None]).reshape(b, c, SSM_CHUNK, g, r, p)
    a = (dt * A).reshape(b, c, SSM_CHUNK, g, r).transpose(0, 3, 4, 1, 2)
    a_cs = jnp.cumsum(a, axis=-1)
    Bc = Bm.reshape(b, c, SSM_CHUNK, g, n)
    Cc = Cm.reshape(b, c, SSM_CHUNK, g, n)
    tril = jnp.tril(jnp.ones((SSM_CHUNK, SSM_CHUNK), dtype=bool))
    seg = a_cs[..., :, None] - a_cs[..., None, :]
    Lmat = jnp.exp(jnp.where(tril, seg, -jnp.inf))
    CB = jnp.einsum("bclgn,bcsgn->bgcls", Cc, Bc)
    y_diag = jnp.einsum("bgcls,bgrcls,bcsgrp->bclgrp", CB, Lmat, X)
    decay = jnp.exp(a_cs[..., -1:] - a_cs)
    states = jnp.einsum("bclgn,bgrcl,bclgrp->bcgrpn", Bc, decay, X)
    chunk_decay = jnp.exp(a_cs[..., -1])

    def step(state, inp):
        st, dec = inp
        return state * dec[..., None, None] + st, state

    h0 = jnp.zeros((b, g, r, p, n), jnp.float32)
    _, prev = lax.scan(step, h0, (jnp.moveaxis(states, 1, 0), jnp.moveaxis(chunk_decay, 3, 0)))
    y_off = jnp.einsum("bclgn,cbgrpn,bgrcl->bclgrp", Cc, prev, jnp.exp(a_cs))
    return (y_diag + y_off).reshape(b, s, h, p)


def ssd_mixer(h, w_in, conv_w, conv_b, dt_bias, A_log, D_skip, norm_w, w_out):
    B, S, _ = h.shape
    proj = h @ w_in
    z, xbc, dt = jnp.split(proj, [SSM_D_INNER, SSM_D_INNER + SSM_CONV_DIM], axis=-1)
    xbc = jax.nn.silu(causal_dwconv(xbc, conv_w, conv_b))
    xs, Bm, Cm = jnp.split(xbc, [SSM_D_INNER, SSM_D_INNER + SSM_GROUPS * SSM_STATE], axis=-1)
    xs = xs.reshape(B, S, SSM_HEADS, SSM_HEAD_DIM).astype(jnp.float32)
    Bm = Bm.reshape(B, S, SSM_GROUPS, SSM_STATE).astype(jnp.float32)
    Cm = Cm.reshape(B, S, SSM_GROUPS, SSM_STATE).astype(jnp.float32)
    dt = jax.nn.softplus(dt.astype(jnp.float32) + dt_bias.astype(jnp.float32))
    A = -jnp.exp(A_log.astype(jnp.float32))
    y = ssd_scan(xs, dt, A, Bm, Cm) + D_skip.astype(jnp.float32)[:, None] * xs
    y = y.reshape(B, S, SSM_D_INNER) * jax.nn.silu(z.astype(jnp.float32))
    y = rms_norm(y, norm_w, SSM_NORM_EPS).astype(h.dtype)
    return y @ w_out


def conv_ffn(h, w_up, conv_w, conv_b, w_down):
    hid = causal_dwconv(h @ w_up, conv_w, conv_b)
    u, g = jnp.split(hid, 2, axis=-1)
    return (jax.nn.silu(g) * u) @ w_down


def _fwd_setup_inputs(seed: int = 0) -> dict:
    key = jax.random.key(seed)
    ks = jax.random.split(key, 24)
    f32 = jnp.float32
    nrm = lambda k, shape, scale: jax.random.normal(k, shape, f32) * scale
    x = jax.random.normal(ks[0], (BATCH, SEQ, D_MODEL), f32)
    offsets = jax.random.randint(ks[1], (BATCH, 1), 0, MAX_POS_OFFSET, dtype=jnp.int32)
    positions = (offsets + jnp.arange(SEQ, dtype=jnp.int32)[None, :]).astype(jnp.int32)
    dt0 = jnp.exp(jax.random.uniform(ks[13], (N_ODD, SSM_HEADS), f32, math.log(1e-3), math.log(1e-1)))
    return {
        "x": x,
        "positions": positions,
        "norm_mix": 1.0 + nrm(ks[2], (DEPTH, D_MODEL), 0.02),
        "norm_ffn": 1.0 + nrm(ks[3], (DEPTH, D_MODEL), 0.02),
        "norm_final": 1.0 + nrm(ks[4], (D_MODEL,), 0.02),
        "mix_w_in": nrm(ks[5], (N_EVEN, D_MODEL, MIX_IN_DIM), D_MODEL ** -0.5),
        "pool_w": nrm(ks[6], (N_EVEN, N_POOL_GROUPS, POOL_GROUP, POOL_GROUP), POOL_GROUP ** -0.5),
        "pool_scale": 1.0 + nrm(ks[7], (N_EVEN, POOL_DIM), 0.1),
        "attn_sinks": nrm(ks[8], (N_EVEN, N_HEADS), 1.0),
        "mix_w_out": nrm(ks[9], (N_EVEN, MIX_OUT_DIM, D_MODEL), MIX_OUT_DIM ** -0.5),
        "ssm_w_in": nrm(ks[10], (N_ODD, D_MODEL, SSM_IN_DIM), D_MODEL ** -0.5),
        "ssm_conv_w": nrm(ks[11], (N_ODD, SSM_CONV, SSM_CONV_DIM), SSM_CONV ** -0.5),
        "ssm_conv_b": nrm(ks[12], (N_ODD, SSM_CONV_DIM), 0.02),
        "ssm_dt_bias": dt0 + jnp.log(-jnp.expm1(-dt0)),
        "ssm_A_log": jnp.log(jax.random.uniform(ks[14], (N_ODD, SSM_HEADS), f32, 1.0, 16.0)),
        "ssm_D": 1.0 + nrm(ks[15], (N_ODD, SSM_HEADS), 0.1),
        "ssm_norm": 1.0 + nrm(ks[16], (N_ODD, SSM_D_INNER), 0.02),
        "ssm_w_out": nrm(ks[17], (N_ODD, SSM_D_INNER, D_MODEL), SSM_D_INNER ** -0.5),
        "ffn_w_up": nrm(ks[18], (DEPTH, D_MODEL, 2 * D_FF), D_MODEL ** -0.5),
        "ffn_conv_w": nrm(ks[19], (DEPTH, FFN_CONV, 2 * D_FF), FFN_CONV ** -0.5),
        "ffn_conv_b": nrm(ks[20], (DEPTH, 2 * D_FF), 0.02),
        "ffn_w_down": nrm(ks[21], (DEPTH, D_FF, D_MODEL), D_FF ** -0.5),
    }


def _fwd_reference(x, positions, norm_mix, norm_ffn, norm_final, mix_w_in, pool_w, pool_scale,
              attn_sinks, mix_w_out, ssm_w_in, ssm_conv_w, ssm_conv_b, ssm_dt_bias, ssm_A_log,
              ssm_D, ssm_norm, ssm_w_out, ffn_w_up, ffn_conv_w, ffn_conv_b, ffn_w_down):
    cos, sin = rope_tables(positions)
    for i in range(DEPTH):
        j = i // 2
        h = rms_norm(x, norm_mix[i])
        if i % 2 == 0:
            x = x + pool_attention_mixer(h, cos, sin, mix_w_in[j], pool_w[j], pool_scale[j],
                                         attn_sinks[j], mix_w_out[j])
        else:
            x = x + ssd_mixer(h, ssm_w_in[j], ssm_conv_w[j], ssm_conv_b[j], ssm_dt_bias[j],
                              ssm_A_log[j], ssm_D[j], ssm_norm[j], ssm_w_out[j])
        x = x + conv_ffn(rms_norm(x, norm_ffn[i]), ffn_w_up[i], ffn_conv_w[i], ffn_conv_b[i],
                         ffn_w_down[i])
    return rms_norm(x, norm_final)


import jax as _jax
import jax.numpy as _jnp

TWIN_FORMAT = 'train_step'
FWD_PARAMS = ['x', 'positions', 'norm_mix', 'norm_ffn', 'norm_final', 'mix_w_in', 'pool_w', 'pool_scale', 'attn_sinks', 'mix_w_out', 'ssm_w_in', 'ssm_conv_w', 'ssm_conv_b', 'ssm_dt_bias', 'ssm_A_log', 'ssm_D', 'ssm_norm', 'ssm_w_out', 'ffn_w_up', 'ffn_conv_w', 'ffn_conv_b', 'ffn_w_down']
TWIN_WEIGHTS = ['norm_mix', 'norm_ffn', 'norm_final', 'mix_w_in', 'pool_w', 'pool_scale', 'attn_sinks', 'mix_w_out', 'ssm_w_in', 'ssm_conv_w', 'ssm_conv_b', 'ssm_dt_bias', 'ssm_A_log', 'ssm_D', 'ssm_norm', 'ssm_w_out', 'ffn_w_up', 'ffn_conv_w', 'ffn_conv_b', 'ffn_w_down']
TWIN_DIFF_INPUT = 'x'
TWIN_INPUTS = ['x', 'positions', 'norm_mix', 'norm_ffn', 'norm_final', 'mix_w_in', 'pool_w', 'pool_scale', 'attn_sinks', 'mix_w_out', 'ssm_w_in', 'ssm_conv_w', 'ssm_conv_b', 'ssm_dt_bias', 'ssm_A_log', 'ssm_D', 'ssm_norm', 'ssm_w_out', 'ffn_w_up', 'ffn_conv_w', 'ffn_conv_b', 'ffn_w_down', 'loss_target', 'm_norm_mix', 'm_norm_ffn', 'm_norm_final', 'm_mix_w_in', 'm_pool_w', 'm_pool_scale', 'm_attn_sinks', 'm_mix_w_out', 'm_ssm_w_in', 'm_ssm_conv_w', 'm_ssm_conv_b', 'm_ssm_dt_bias', 'm_ssm_A_log', 'm_ssm_D', 'm_ssm_norm', 'm_ssm_w_out', 'm_ffn_w_up', 'm_ffn_conv_w', 'm_ffn_conv_b', 'm_ffn_w_down', 'v_norm_mix', 'v_norm_ffn', 'v_norm_final', 'v_mix_w_in', 'v_pool_w', 'v_pool_scale', 'v_attn_sinks', 'v_mix_w_out', 'v_ssm_w_in', 'v_ssm_conv_w', 'v_ssm_conv_b', 'v_ssm_dt_bias', 'v_ssm_A_log', 'v_ssm_D', 'v_ssm_norm', 'v_ssm_w_out', 'v_ffn_w_up', 'v_ffn_conv_w', 'v_ffn_conv_b', 'v_ffn_w_down']
TWIN_OUTPUTS = ['loss', 'grad_x', 'grad_norm_mix', 'grad_norm_ffn', 'grad_norm_final', 'grad_mix_w_in', 'grad_pool_w', 'grad_pool_scale', 'grad_attn_sinks', 'grad_mix_w_out', 'grad_ssm_w_in', 'grad_ssm_conv_w', 'grad_ssm_conv_b', 'grad_ssm_dt_bias', 'grad_ssm_A_log', 'grad_ssm_D', 'grad_ssm_norm', 'grad_ssm_w_out', 'grad_ffn_w_up', 'grad_ffn_conv_w', 'grad_ffn_conv_b', 'grad_ffn_w_down', 'delta_norm_mix', 'delta_norm_ffn', 'delta_norm_final', 'delta_mix_w_in', 'delta_pool_w', 'delta_pool_scale', 'delta_attn_sinks', 'delta_mix_w_out', 'delta_ssm_w_in', 'delta_ssm_conv_w', 'delta_ssm_conv_b', 'delta_ssm_dt_bias', 'delta_ssm_A_log', 'delta_ssm_D', 'delta_ssm_norm', 'delta_ssm_w_out', 'delta_ffn_w_up', 'delta_ffn_conv_w', 'delta_ffn_conv_b', 'delta_ffn_w_down', 'new_m_norm_mix', 'new_m_norm_ffn', 'new_m_norm_final', 'new_m_mix_w_in', 'new_m_pool_w', 'new_m_pool_scale', 'new_m_attn_sinks', 'new_m_mix_w_out', 'new_m_ssm_w_in', 'new_m_ssm_conv_w', 'new_m_ssm_conv_b', 'new_m_ssm_dt_bias', 'new_m_ssm_A_log', 'new_m_ssm_D', 'new_m_ssm_norm', 'new_m_ssm_w_out', 'new_m_ffn_w_up', 'new_m_ffn_conv_w', 'new_m_ffn_conv_b', 'new_m_ffn_w_down', 'new_v_norm_mix', 'new_v_norm_ffn', 'new_v_norm_final', 'new_v_mix_w_in', 'new_v_pool_w', 'new_v_pool_scale', 'new_v_attn_sinks', 'new_v_mix_w_out', 'new_v_ssm_w_in', 'new_v_ssm_conv_w', 'new_v_ssm_conv_b', 'new_v_ssm_dt_bias', 'new_v_ssm_A_log', 'new_v_ssm_D', 'new_v_ssm_norm', 'new_v_ssm_w_out', 'new_v_ffn_w_up', 'new_v_ffn_conv_w', 'new_v_ffn_conv_b', 'new_v_ffn_w_down']
TWIN_LEAF_KINDS = {'loss': 'loss', 'grad_x': 'grad_x', 'grad_norm_mix': 'grad_w', 'grad_norm_ffn': 'grad_w', 'grad_norm_final': 'grad_w', 'grad_mix_w_in': 'grad_w', 'grad_pool_w': 'grad_w', 'grad_pool_scale': 'grad_w', 'grad_attn_sinks': 'grad_w', 'grad_mix_w_out': 'grad_w', 'grad_ssm_w_in': 'grad_w', 'grad_ssm_conv_w': 'grad_w', 'grad_ssm_conv_b': 'grad_w', 'grad_ssm_dt_bias': 'grad_w', 'grad_ssm_A_log': 'grad_w', 'grad_ssm_D': 'grad_w', 'grad_ssm_norm': 'grad_w', 'grad_ssm_w_out': 'grad_w', 'grad_ffn_w_up': 'grad_w', 'grad_ffn_conv_w': 'grad_w', 'grad_ffn_conv_b': 'grad_w', 'grad_ffn_w_down': 'grad_w', 'delta_norm_mix': 'delta_w', 'delta_norm_ffn': 'delta_w', 'delta_norm_final': 'delta_w', 'delta_mix_w_in': 'delta_w', 'delta_pool_w': 'delta_w', 'delta_pool_scale': 'delta_w', 'delta_attn_sinks': 'delta_w', 'delta_mix_w_out': 'delta_w', 'delta_ssm_w_in': 'delta_w', 'delta_ssm_conv_w': 'delta_w', 'delta_ssm_conv_b': 'delta_w', 'delta_ssm_dt_bias': 'delta_w', 'delta_ssm_A_log': 'delta_w', 'delta_ssm_D': 'delta_w', 'delta_ssm_norm': 'delta_w', 'delta_ssm_w_out': 'delta_w', 'delta_ffn_w_up': 'delta_w', 'delta_ffn_conv_w': 'delta_w', 'delta_ffn_conv_b': 'delta_w', 'delta_ffn_w_down': 'delta_w', 'new_m_norm_mix': 'new_m', 'new_m_norm_ffn': 'new_m', 'new_m_norm_final': 'new_m', 'new_m_mix_w_in': 'new_m', 'new_m_pool_w': 'new_m', 'new_m_pool_scale': 'new_m', 'new_m_attn_sinks': 'new_m', 'new_m_mix_w_out': 'new_m', 'new_m_ssm_w_in': 'new_m', 'new_m_ssm_conv_w': 'new_m', 'new_m_ssm_conv_b': 'new_m', 'new_m_ssm_dt_bias': 'new_m', 'new_m_ssm_A_log': 'new_m', 'new_m_ssm_D': 'new_m', 'new_m_ssm_norm': 'new_m', 'new_m_ssm_w_out': 'new_m', 'new_m_ffn_w_up': 'new_m', 'new_m_ffn_conv_w': 'new_m', 'new_m_ffn_conv_b': 'new_m', 'new_m_ffn_w_down': 'new_m', 'new_v_norm_mix': 'new_v', 'new_v_norm_ffn': 'new_v', 'new_v_norm_final': 'new_v', 'new_v_mix_w_in': 'new_v', 'new_v_pool_w': 'new_v', 'new_v_pool_scale': 'new_v', 'new_v_attn_sinks': 'new_v', 'new_v_mix_w_out': 'new_v', 'new_v_ssm_w_in': 'new_v', 'new_v_ssm_conv_w': 'new_v', 'new_v_ssm_conv_b': 'new_v', 'new_v_ssm_dt_bias': 'new_v', 'new_v_ssm_A_log': 'new_v', 'new_v_ssm_D': 'new_v', 'new_v_ssm_norm': 'new_v', 'new_v_ssm_w_out': 'new_v', 'new_v_ffn_w_up': 'new_v', 'new_v_ffn_conv_w': 'new_v', 'new_v_ffn_conv_b': 'new_v', 'new_v_ffn_w_down': 'new_v'}


def _forward(args):
    return _fwd_reference(*[args[k] for k in FWD_PARAMS])


def _output_shape():
    def fwd():
        inp = _fwd_setup_inputs(0)
        return _fwd_reference(*[inp[k] for k in FWD_PARAMS])
    out = _jax.eval_shape(fwd)
    return out.shape, out.dtype

N_MICROBATCH = 1
ADAM_LR = 0.001
ADAM_B1 = 0.9
ADAM_B2 = 0.999
ADAM_EPS = 1e-08
ADAM_WD = 0.01
ADAM_STEP = 10
PER_EXAMPLE_BATCH_AXIS = {'x': 0, 'positions': 0, 'loss_target': 0}
SHARED_INPUTS = []
_WEIGHT_DTYPES = {'norm_mix': _jnp.float32, 'norm_ffn': _jnp.float32, 'norm_final': _jnp.float32, 'mix_w_in': _jnp.float32, 'pool_w': _jnp.float32, 'pool_scale': _jnp.float32, 'attn_sinks': _jnp.float32, 'mix_w_out': _jnp.float32, 'ssm_w_in': _jnp.float32, 'ssm_conv_w': _jnp.float32, 'ssm_conv_b': _jnp.float32, 'ssm_dt_bias': _jnp.float32, 'ssm_A_log': _jnp.float32, 'ssm_D': _jnp.float32, 'ssm_norm': _jnp.float32, 'ssm_w_out': _jnp.float32, 'ffn_w_up': _jnp.float32, 'ffn_conv_w': _jnp.float32, 'ffn_conv_b': _jnp.float32, 'ffn_w_down': _jnp.float32}
MOMENT_SCALE = {'norm_mix': 1.636311e-01, 'norm_ffn': 1.231859e-01, 'norm_final': 3.201274e+01, 'mix_w_in': 1.273076e-01, 'pool_w': 1.893070e-01, 'pool_scale': 2.130817e-01, 'attn_sinks': 4.260492e-02, 'mix_w_out': 1.359747e-01, 'ssm_w_in': 7.156801e-02, 'ssm_conv_w': 6.312421e-02, 'ssm_conv_b': 8.182030e-02, 'ssm_dt_bias': 2.201204e-01, 'ssm_A_log': 1.897572e-01, 'ssm_D': 3.862165e-01, 'ssm_norm': 8.293236e-02, 'ssm_w_out': 1.140060e-01, 'ffn_w_up': 5.303187e-02, 'ffn_conv_w': 5.367721e-02, 'ffn_conv_b': 5.400095e-02, 'ffn_w_down': 8.644645e-02}


def _to_microbatches(a, axis):
    t = _jnp.moveaxis(a, axis, 0)
    t = t.reshape((N_MICROBATCH, t.shape[0] // N_MICROBATCH) + t.shape[1:])
    return _jnp.moveaxis(t, 1, axis + 1)


def setup_inputs(seed: int = 0) -> dict:
    inp = _fwd_setup_inputs(seed)
    key = _jax.random.fold_in(_jax.random.key(seed), 7919)
    shape, _ = _output_shape()
    out = dict(inp)
    out["loss_target"] = _jax.random.normal(_jax.random.fold_in(key, 0), shape, _jnp.float32)
    for i, name in enumerate(TWIN_WEIGHTS):
        w = inp[name].astype(_jnp.float32)
        if MOMENT_SCALE is None:
            s = _jnp.sqrt(_jnp.mean(_jnp.square(w)) + 1e-30)
        else:
            s = MOMENT_SCALE[name]
        km, kv = _jax.random.split(_jax.random.fold_in(key, i + 1))
        out[name] = w
        out["m_" + name] = s * _jax.random.normal(km, w.shape, _jnp.float32)
        out["v_" + name] = (s * s) * _jax.random.uniform(kv, w.shape, _jnp.float32, 0.5, 1.5)
    if N_MICROBATCH > 1:
        for name, axis in PER_EXAMPLE_BATCH_AXIS.items():
            out[name] = _to_microbatches(out[name], axis)
    return {'x': out['x'], 'positions': out['positions'], 'norm_mix': out['norm_mix'], 'norm_ffn': out['norm_ffn'], 'norm_final': out['norm_final'], 'mix_w_in': out['mix_w_in'], 'pool_w': out['pool_w'], 'pool_scale': out['pool_scale'], 'attn_sinks': out['attn_sinks'], 'mix_w_out': out['mix_w_out'], 'ssm_w_in': out['ssm_w_in'], 'ssm_conv_w': out['ssm_conv_w'], 'ssm_conv_b': out['ssm_conv_b'], 'ssm_dt_bias': out['ssm_dt_bias'], 'ssm_A_log': out['ssm_A_log'], 'ssm_D': out['ssm_D'], 'ssm_norm': out['ssm_norm'], 'ssm_w_out': out['ssm_w_out'], 'ffn_w_up': out['ffn_w_up'], 'ffn_conv_w': out['ffn_conv_w'], 'ffn_conv_b': out['ffn_conv_b'], 'ffn_w_down': out['ffn_w_down'], 'loss_target': out['loss_target'], 'm_norm_mix': out['m_norm_mix'], 'm_norm_ffn': out['m_norm_ffn'], 'm_norm_final': out['m_norm_final'], 'm_mix_w_in': out['m_mix_w_in'], 'm_pool_w': out['m_pool_w'], 'm_pool_scale': out['m_pool_scale'], 'm_attn_sinks': out['m_attn_sinks'], 'm_mix_w_out': out['m_mix_w_out'], 'm_ssm_w_in': out['m_ssm_w_in'], 'm_ssm_conv_w': out['m_ssm_conv_w'], 'm_ssm_conv_b': out['m_ssm_conv_b'], 'm_ssm_dt_bias': out['m_ssm_dt_bias'], 'm_ssm_A_log': out['m_ssm_A_log'], 'm_ssm_D': out['m_ssm_D'], 'm_ssm_norm': out['m_ssm_norm'], 'm_ssm_w_out': out['m_ssm_w_out'], 'm_ffn_w_up': out['m_ffn_w_up'], 'm_ffn_conv_w': out['m_ffn_conv_w'], 'm_ffn_conv_b': out['m_ffn_conv_b'], 'm_ffn_w_down': out['m_ffn_w_down'], 'v_norm_mix': out['v_norm_mix'], 'v_norm_ffn': out['v_norm_ffn'], 'v_norm_final': out['v_norm_final'], 'v_mix_w_in': out['v_mix_w_in'], 'v_pool_w': out['v_pool_w'], 'v_pool_scale': out['v_pool_scale'], 'v_attn_sinks': out['v_attn_sinks'], 'v_mix_w_out': out['v_mix_w_out'], 'v_ssm_w_in': out['v_ssm_w_in'], 'v_ssm_conv_w': out['v_ssm_conv_w'], 'v_ssm_conv_b': out['v_ssm_conv_b'], 'v_ssm_dt_bias': out['v_ssm_dt_bias'], 'v_ssm_A_log': out['v_ssm_A_log'], 'v_ssm_D': out['v_ssm_D'], 'v_ssm_norm': out['v_ssm_norm'], 'v_ssm_w_out': out['v_ssm_w_out'], 'v_ffn_w_up': out['v_ffn_w_up'], 'v_ffn_conv_w': out['v_ffn_conv_w'], 'v_ffn_conv_b': out['v_ffn_conv_b'], 'v_ffn_w_down': out['v_ffn_w_down']}


def _loss(weights, diff, rest, loss_target):
    with _jax.named_scope("forward"):
        args = {**rest, TWIN_DIFF_INPUT: diff, **{k: w.astype(_WEIGHT_DTYPES[k]) for k, w in weights.items()}}
        y = _forward(args)
    with _jax.named_scope("loss_head"):
        err = _jnp.square(y.astype(_jnp.float32) - loss_target)
        return 0.5 * _jnp.sum(_jnp.mean(err, axis=-1)) if err.ndim else 0.5 * err


def _adamw(w, g, m, v):
    m = ADAM_B1 * m + (1.0 - ADAM_B1) * g
    v = ADAM_B2 * v + (1.0 - ADAM_B2) * _jnp.square(g)
    m_hat = m / (1.0 - ADAM_B1 ** ADAM_STEP)
    v_hat = v / (1.0 - ADAM_B2 ** ADAM_STEP)
    delta = -ADAM_LR * (m_hat / (_jnp.sqrt(v_hat) + ADAM_EPS) + ADAM_WD * w)
    return delta, m, v


def reference(x, positions, norm_mix, norm_ffn, norm_final, mix_w_in, pool_w, pool_scale, attn_sinks, mix_w_out, ssm_w_in, ssm_conv_w, ssm_conv_b, ssm_dt_bias, ssm_A_log, ssm_D, ssm_norm, ssm_w_out, ffn_w_up, ffn_conv_w, ffn_conv_b, ffn_w_down, loss_target, m_norm_mix, m_norm_ffn, m_norm_final, m_mix_w_in, m_pool_w, m_pool_scale, m_attn_sinks, m_mix_w_out, m_ssm_w_in, m_ssm_conv_w, m_ssm_conv_b, m_ssm_dt_bias, m_ssm_A_log, m_ssm_D, m_ssm_norm, m_ssm_w_out, m_ffn_w_up, m_ffn_conv_w, m_ffn_conv_b, m_ffn_w_down, v_norm_mix, v_norm_ffn, v_norm_final, v_mix_w_in, v_pool_w, v_pool_scale, v_attn_sinks, v_mix_w_out, v_ssm_w_in, v_ssm_conv_w, v_ssm_conv_b, v_ssm_dt_bias, v_ssm_A_log, v_ssm_D, v_ssm_norm, v_ssm_w_out, v_ffn_w_up, v_ffn_conv_w, v_ffn_conv_b, v_ffn_w_down):
    given = dict(x=x, positions=positions, norm_mix=norm_mix, norm_ffn=norm_ffn, norm_final=norm_final, mix_w_in=mix_w_in, pool_w=pool_w, pool_scale=pool_scale, attn_sinks=attn_sinks, mix_w_out=mix_w_out, ssm_w_in=ssm_w_in, ssm_conv_w=ssm_conv_w, ssm_conv_b=ssm_conv_b, ssm_dt_bias=ssm_dt_bias, ssm_A_log=ssm_A_log, ssm_D=ssm_D, ssm_norm=ssm_norm, ssm_w_out=ssm_w_out, ffn_w_up=ffn_w_up, ffn_conv_w=ffn_conv_w, ffn_conv_b=ffn_conv_b, ffn_w_down=ffn_w_down, loss_target=loss_target, m_norm_mix=m_norm_mix, m_norm_ffn=m_norm_ffn, m_norm_final=m_norm_final, m_mix_w_in=m_mix_w_in, m_pool_w=m_pool_w, m_pool_scale=m_pool_scale, m_attn_sinks=m_attn_sinks, m_mix_w_out=m_mix_w_out, m_ssm_w_in=m_ssm_w_in, m_ssm_conv_w=m_ssm_conv_w, m_ssm_conv_b=m_ssm_conv_b, m_ssm_dt_bias=m_ssm_dt_bias, m_ssm_A_log=m_ssm_A_log, m_ssm_D=m_ssm_D, m_ssm_norm=m_ssm_norm, m_ssm_w_out=m_ssm_w_out, m_ffn_w_up=m_ffn_w_up, m_ffn_conv_w=m_ffn_conv_w, m_ffn_conv_b=m_ffn_conv_b, m_ffn_w_down=m_ffn_w_down, v_norm_mix=v_norm_mix, v_norm_ffn=v_norm_ffn, v_norm_final=v_norm_final, v_mix_w_in=v_mix_w_in, v_pool_w=v_pool_w, v_pool_scale=v_pool_scale, v_attn_sinks=v_attn_sinks, v_mix_w_out=v_mix_w_out, v_ssm_w_in=v_ssm_w_in, v_ssm_conv_w=v_ssm_conv_w, v_ssm_conv_b=v_ssm_conv_b, v_ssm_dt_bias=v_ssm_dt_bias, v_ssm_A_log=v_ssm_A_log, v_ssm_D=v_ssm_D, v_ssm_norm=v_ssm_norm, v_ssm_w_out=v_ssm_w_out, v_ffn_w_up=v_ffn_w_up, v_ffn_conv_w=v_ffn_conv_w, v_ffn_conv_b=v_ffn_conv_b, v_ffn_w_down=v_ffn_w_down)
    weights = {n: given[n] for n in TWIN_WEIGHTS}
    shared = {n: given[n] for n in SHARED_INPUTS}
    per_example = {n: given[n] for n in ['x', 'positions']}
    grad_fn = _jax.value_and_grad(_loss, argnums=(0, 1))

    def one_microbatch(ex, loss_target):
        ex = dict(ex)
        diff = ex.pop(TWIN_DIFF_INPUT)
        return grad_fn(weights, diff, {**shared, **ex}, loss_target)

    if N_MICROBATCH == 1:
        loss, (grad_w, grad_x) = one_microbatch(per_example, given["loss_target"])
    else:
        def body(carry, xs):
            loss_sum, grad_sum = carry
            l_k, (gw_k, gx_k) = one_microbatch(xs[0], xs[1])
            with _jax.named_scope("update"):
                return (loss_sum + l_k, _jax.tree.map(_jnp.add, grad_sum, gw_k)), gx_k

        init = (_jnp.zeros((), _jnp.float32), _jax.tree.map(_jnp.zeros_like, weights))
        (loss, grad_w), grad_x = _jax.lax.scan(body, init, (per_example, given["loss_target"]))
    with _jax.named_scope("update"):
        delta_w, new_m, new_v = {}, {}, {}
        for n in TWIN_WEIGHTS:
            delta_w[n], new_m[n], new_v[n] = _adamw(weights[n], grad_w[n], given["m_" + n], given["v_" + n])
    return (loss, grad_x, *[grad_w[n] for n in TWIN_WEIGHTS], *[delta_w[n] for n in TWIN_WEIGHTS],
            *[new_m[n] for n in TWIN_WEIGHTS], *[new_v[n] for n in TWIN_WEIGHTS])
```

```python
import functools

import jax
import jax.numpy as jnp
from jax import lax
from jax.experimental import pallas as pl
from jax.experimental.pallas import tpu as pltpu

F32 = jnp.float32
BF16 = jnp.bfloat16
MXU = BF16
HI = lax.Precision.HIGHEST

D_MODEL = 1024
POOL_WINDOWS = (2, 4, 8, 16)
POOL_DIM = 512
POOL_GROUP = 128
HEAD_DIM = 64
N_HEADS = 8
N_KV_HEADS = 2
GQ = 4
Q_DIM = 512
KV_DIM = 128
BLOCK = 128
ROPE_THETA = 10000.0
MIX_IN_DIM = 1280
SSM_D_INNER = 2048
SSM_HEADS = 32
SSM_GROUPS = 8
SSM_STATE = 128
SSM_CONV = 4
SSM_CHUNK = 128
SSM_CONV_DIM = 4096
SSM_IN_DIM = 6176
D_FF = 2816
FFN_CONV = 3
NORM_EPS = 1e-6
SSM_NORM_EPS = 1e-5
ADAM_LR = 0.001
ADAM_B1 = 0.9
ADAM_B2 = 0.999
ADAM_EPS = 1e-08
ADAM_WD = 0.01
ADAM_STEP = 10

N_CHIPS = 4
N_DEV = 8
LANES = 128
SUBLANES = 8
V7X_VMEM_LIMIT = 56 * 1024 * 1024
NEG = -1e30
MESH = pl.DeviceIdType.MESH


def _cp(*sem):
    return pltpu.CompilerParams(dimension_semantics=sem if sem else None, vmem_limit_bytes=V7X_VMEM_LIMIT)


def _sds(shape, dtype=F32):
    return jax.ShapeDtypeStruct(tuple(shape), dtype)


def _iota(shape, dim):
    return lax.broadcasted_iota(jnp.int32, shape, dim)


def _silu(x):
    return x * (1.0 / (1.0 + jnp.exp(-x)))


def _dsilu(x):
    s = 1.0 / (1.0 + jnp.exp(-x))
    return s * (1.0 + x * (1.0 - s))


def _mm(a, b, *, ta=False, tb=False, tm, tn, tk, res=None, out_dtype=F32, out_shard_perm=None, name):
    M, K = (a.shape[1], a.shape[0]) if ta else a.shape
    N = b.shape[0] if tb else b.shape[1]
    tm, tn, tk = min(tm, M), min(tn, N), min(tk, K)
    gm, gn, gk = M // tm, N // tn, K // tk
    assert gm * tm == M and gn * tn == N and gk * tk == K, (name, M, N, K, tm, tn, tk)
    a_spec = pl.BlockSpec((tk, tm), lambda i, j, k: (k, i)) if ta else pl.BlockSpec((tm, tk), lambda i, j, k: (i, k))
    b_spec = pl.BlockSpec((tn, tk), lambda i, j, k: (j, k)) if tb else pl.BlockSpec((tk, tn), lambda i, j, k: (k, j))
    dims = (((0 if ta else 1,), (1 if tb else 0,)), ((), ()))
    has_res = res is not None

    def body(*refs):
        a_ref, b_ref = refs[0], refs[1]
        r_ref = refs[2] if has_res else None
        o_ref = refs[3] if has_res else refs[2]
        p = lax.dot_general(a_ref[...].astype(MXU), b_ref[...].astype(MXU), dims, preferred_element_type=F32)
        if gk == 1:
            if has_res:
                p = p + r_ref[...]
            o_ref[...] = p.astype(out_dtype)
        else:
            acc = refs[-1]
            k = pl.program_id(2)

            @pl.when(k == 0)
            def _():
                acc[...] = p

            @pl.when(k > 0)
            def _():
                acc[...] += p

            @pl.when(k == gk - 1)
            def _():
                r = acc[...]
                if has_res:
                    r = r + r_ref[...]
                o_ref[...] = r.astype(out_dtype)

    in_specs = [a_spec, b_spec]
    args = [a, b]
    if has_res:
        in_specs.append(pl.BlockSpec((tm, tn), lambda i, j, k: (i, j)))
        args.append(res)
    if out_shard_perm is None:
        out_spec = pl.BlockSpec((tm, tn), lambda i, j, k: (i, j))
        out_shape = _sds((M, N), out_dtype)
    else:
        assert gn == len(out_shard_perm) == 4 and tuple(out_shard_perm) == (0, 2, 1, 3)
        out_spec = pl.BlockSpec((None, tm, tn), lambda i, j, k: ((j % 2) * 2 + j // 2, i, 0))
        out_shape = _sds((gn, M, tn), out_dtype)
    return pl.pallas_call(
        body, grid=(gm, gn, gk), in_specs=in_specs, out_specs=out_spec, out_shape=out_shape,
        scratch_shapes=[pltpu.VMEM((tm, tn), F32)] if gk > 1 else [],
        compiler_params=_cp("parallel", "parallel", "arbitrary"), name=name)(*args)


def _rmsnorm_fwd(x, w, name):
    T, D = x.shape
    tm = min(T, 512)

    def body(x_ref, w_ref, o_ref):
        xv = x_ref[...]
        r = lax.rsqrt(jnp.mean(xv * xv, axis=-1, keepdims=True) + NORM_EPS)
        o_ref[...] = (xv * r * w_ref[...]).astype(o_ref.dtype)

    return pl.pallas_call(
        body, grid=(T // tm,),
        in_specs=[pl.BlockSpec((tm, D), lambda i: (i, 0)), pl.BlockSpec((1, D), lambda i: (0, 0))],
        out_specs=pl.BlockSpec((tm, D), lambda i: (i, 0)), out_shape=_sds((T, D), MXU),
        compiler_params=_cp("parallel"), name=name)(x, w.reshape(1, D))


def _rmsnorm_bwd(x, w, dh, dres, name):
    T, D = x.shape
    tm = min(T, 512)

    def body(x_ref, w_ref, dh_ref, dr_ref, dx_ref, dw_ref):
        xv = x_ref[...]
        r = lax.rsqrt(jnp.mean(xv * xv, axis=-1, keepdims=True) + NORM_EPS)
        xh = xv * r
        dh = dh_ref[...]
        g = dh * w_ref[...]
        dx_ref[...] = dr_ref[...] + r * (g - xh * jnp.mean(g * xh, axis=-1, keepdims=True))
        part = jnp.sum(dh * xh, axis=0, keepdims=True)

        @pl.when(pl.program_id(0) == 0)
        def _():
            dw_ref[...] = part

        @pl.when(pl.program_id(0) > 0)
        def _():
            dw_ref[...] += part

    row = pl.BlockSpec((tm, D), lambda i: (i, 0))
    vec = pl.BlockSpec((1, D), lambda i: (0, 0))
    return pl.pallas_call(
        body, grid=(T // tm,), in_specs=[row, vec, row, row], out_specs=[row, vec],
        out_shape=[_sds((T, D)), _sds((1, D))], compiler_params=_cp("arbitrary"), name=name)(x, w.reshape(1, D), dh, dres)


def _loss_head(x, w, target, name):
    T, D = x.shape
    tm = min(T, 512)

    def body(x_ref, w_ref, t_ref, loss_ref, dx_ref, dw_ref):
        xv = x_ref[...]
        r = lax.rsqrt(jnp.mean(xv * xv, axis=-1, keepdims=True) + NORM_EPS)
        xh = xv * r
        wv = w_ref[...]
        e = xh * wv - t_ref[...]
        lpart = 0.5 * jnp.sum(jnp.mean(e * e, axis=-1, keepdims=True), axis=0, keepdims=True)
        dy = e * (1.0 / D)
        g = dy * wv
        dx_ref[...] = r * (g - xh * jnp.mean(g * xh, axis=-1, keepdims=True))
        part = jnp.sum(dy * xh, axis=0, keepdims=True)
        lrow = jnp.broadcast_to(lpart, (1, LANES))

        @pl.when(pl.program_id(0) == 0)
        def _():
            dw_ref[...] = part
            loss_ref[...] = lrow

        @pl.when(pl.program_id(0) > 0)
        def _():
            dw_ref[...] += part
            loss_ref[...] += lrow

    row = pl.BlockSpec((tm, D), lambda i: (i, 0))
    vec = pl.BlockSpec((1, D), lambda i: (0, 0))
    return pl.pallas_call(
        body, grid=(T // tm,), in_specs=[row, vec, row],
        out_specs=[pl.BlockSpec((1, LANES), lambda i: (0, 0)), row, vec],
        out_shape=[_sds((1, LANES)), _sds((T, D)), _sds((1, D))],
        compiler_params=_cp("arbitrary"), name=name)(x, w.reshape(1, D), target)


def _shift_down(cur, prev8, s):
    if s == 0:
        return cur
    tm = cur.shape[0]
    rc = pltpu.roll(cur, s, 0)
    top = jnp.where(_iota((SUBLANES, cur.shape[1]), 0) < s, pltpu.roll(prev8, s, 0), rc[:SUBLANES])
    return jnp.concatenate([top, rc[SUBLANES:]], axis=0) if tm > SUBLANES else top


def _shift_up(cur, next8, s):
    if s == 0:
        return cur
    tm = cur.shape[0]
    rc = pltpu.roll(cur, tm - s, 0)
    bot = jnp.where(_iota((SUBLANES, cur.shape[1]), 0) >= SUBLANES - s, pltpu.roll(next8, SUBLANES - s, 0), rc[tm - SUBLANES:])
    return jnp.concatenate([rc[:tm - SUBLANES], bot], axis=0) if tm > SUBLANES else bot


def _conv_rows(cur, prev8, w, b, K):
    acc = cur * w[K - 1:K, :] + b
    for s in range(1, K):
        acc = acc + _shift_down(cur, prev8, s) * w[K - 1 - s:K - s, :]
    return acc


def _halo_specs(tm, tc, col_of):
    q = tm // SUBLANES

    def prev_map(i, j):
        return (jnp.maximum(i * q - 1, 0), col_of(j))

    def make_next(n_row_tiles):
        def next_map(i, j):
            return (jnp.minimum((i + 1) * q, n_row_tiles * q - 1), col_of(j))
        return next_map

    return (lambda: pl.BlockSpec((SUBLANES, tc), prev_map)), (lambda n: pl.BlockSpec((SUBLANES, tc), make_next(n)))


FFN_TC = 1408


def _ffn_mid_fwd(hid, cw, cb, name):
    T = hid.shape[0]
    tm = min(T, 256)
    nt, nj = T // tm, D_FF // FFN_TC
    K = FFN_CONV

    def body(h_ref, hp_ref, w_ref, b_ref, o_ref):
        i = pl.program_id(0)
        cur = h_ref[...]
        prev8 = jnp.where(i > 0, hp_ref[...], 0.0)
        hc = _conv_rows(cur, prev8, w_ref[...], b_ref[...], K)
        o_ref[...] = (_silu(hc[:, FFN_TC:]) * hc[:, :FFN_TC]).astype(o_ref.dtype)

    mk_prev, _ = _halo_specs(tm, 2 * FFN_TC, lambda j: j)
    return pl.pallas_call(
        body, grid=(nt, nj),
        in_specs=[pl.BlockSpec((tm, 2 * FFN_TC), lambda i, j: (i, j)), mk_prev(),
                  pl.BlockSpec((K, 2 * FFN_TC), lambda i, j: (0, j)), pl.BlockSpec((1, 2 * FFN_TC), lambda i, j: (0, j))],
        out_specs=pl.BlockSpec((tm, FFN_TC), lambda i, j: (i, j)), out_shape=_sds((T, D_FF), MXU),
        compiler_params=_cp("parallel", "parallel"), name=name)(hid, hid, cw, cb)


def _ffn_mid_bwd(hid, cw, cb, da, name):
    T = hid.shape[0]
    tm = min(T, 256)
    nt, nj = T // tm, D_FF // FFN_TC
    K = FFN_CONV
    W2 = 2 * FFN_TC

    def body(h_ref, hp_ref, hn_ref, da_ref, dan_ref, w_ref, b_ref, dh_ref, dw_ref, db_ref):
        i = pl.program_id(1)
        w = w_ref[...]
        b = b_ref[...]
        cur = h_ref[...]
        prev8 = jnp.where(i > 0, hp_ref[...], 0.0)
        nxt8 = hn_ref[...]
        last = i == nt - 1

        def dpre(hc, dav):
            u, g = hc[:, :FFN_TC], hc[:, FFN_TC:]
            return jnp.concatenate([dav * _silu(g), dav * u * _dsilu(g)], axis=1)

        hc = _conv_rows(cur, prev8, w, b, K)
        d_cur = dpre(hc, da_ref[...])
        hc_n = _conv_rows(nxt8, cur[tm - SUBLANES:], w, b, K)
        d_nxt = jnp.where(last, 0.0, dpre(hc_n, dan_ref[...]))
        dh = d_cur * w[K - 1:K, :]
        for s in range(1, K):
            dh = dh + _shift_up(d_cur, d_nxt, s) * w[K - 1 - s:K - s, :]
        dh_ref[...] = dh.astype(dh_ref.dtype)
        rows = [jnp.sum(d_cur * _shift_down(cur, prev8, K - 1 - k), axis=0, keepdims=True) for k in range(K)]
        dwp = jnp.concatenate(rows, axis=0)
        dbp = jnp.sum(d_cur, axis=0, keepdims=True)

        @pl.when(i == 0)
        def _():
            dw_ref[...] = dwp
            db_ref[...] = dbp

        @pl.when(i > 0)
        def _():
            dw_ref[...] += dwp
            db_ref[...] += dbp

    q = tm // SUBLANES
    blk = pl.BlockSpec((tm, W2), lambda j, i: (i, j))
    prv = pl.BlockSpec((SUBLANES, W2), lambda j, i: (jnp.maximum(i * q - 1, 0), j))
    nxt = pl.BlockSpec((SUBLANES, W2), lambda j, i: (jnp.minimum((i + 1) * q, nt * q - 1), j))
    dab = pl.BlockSpec((tm, FFN_TC), lambda j, i: (i, j))
    dan = pl.BlockSpec((SUBLANES, FFN_TC), lambda j, i: (jnp.minimum((i + 1) * q, nt * q - 1), j))
    return pl.pallas_call(
        body, grid=(nj, nt),
        in_specs=[blk, prv, nxt, dab, dan, pl.BlockSpec((K, W2), lambda j, i: (0, j)), pl.BlockSpec((1, W2), lambda j, i: (0, j))],
        out_specs=[blk, pl.BlockSpec((K, W2), lambda j, i: (0, j)), pl.BlockSpec((1, W2), lambda j, i: (0, j))],
        out_shape=[_sds((T, 2 * D_FF), MXU), _sds((K, 2 * D_FF)), _sds((1, 2 * D_FF))],
        compiler_params=_cp("parallel", "arbitrary"), name=name)(hid, hid, hid, da, da, cw, cb)


def _rope(t, cos, sin_s, inverse=False):
    n = t.shape[1] // LANES
    c = jnp.concatenate([cos] * n, axis=1) if n > 1 else cos
    s = jnp.concatenate([sin_s] * n, axis=1) if n > 1 else sin_s
    a = pltpu.roll(t, HEAD_DIM // 2, 1)
    b = pltpu.roll(t, t.shape[1] - HEAD_DIM // 2, 1)
    first = (_iota(t.shape, 1) % HEAD_DIM) < HEAD_DIM // 2
    rot = jnp.where(first, b, a) * s
    return t * c - rot if inverse else t * c + rot


def _stack_heads(t, g):
    return jnp.concatenate([t[:, (GQ * g + r) * HEAD_DIM:(GQ * g + r + 1) * HEAD_DIM] for r in range(GQ)], axis=0)


def _stack_cols(t, g):
    return jnp.concatenate([t[:, GQ * g + r:GQ * g + r + 1] for r in range(GQ)], axis=0)


def _pool_sums(prev, cur, w):
    s = jnp.concatenate([prev, cur], axis=0)
    sh = 1
    while sh < w:
        s = s + pltpu.roll(s, sh, 0)
        sh *= 2
    return s[BLOCK:]


def _nt(a, b):
    return lax.dot_general(a.astype(MXU), b.astype(MXU), (((1,), (1,)), ((), ())), preferred_element_type=F32)


def _tn(a, b):
    return lax.dot_general(a.astype(MXU), b.astype(MXU), (((0,), (0,)), ((), ())), preferred_element_type=F32)


def _nn(a, b):
    return jnp.dot(a.astype(MXU), b.astype(MXU), preferred_element_type=F32)


def _mixcore_fwd(proj, cos, sin_s, pool_w, pool_scale, sinks, name):
    T = proj.shape[0]
    nb = T // BLOCK
    scale = HEAD_DIM ** -0.5

    def body(p_ref, pp_ref, c_ref, s_ref, cp_ref, sp_ref, pw_ref, ps_ref, sk_ref, cat_ref, at_ref, lse_ref):
        i = pl.program_id(0)
        has_prev = i > 0
        cur = p_ref[...]
        prv = jnp.where(has_prev, pp_ref[...], 0.0)
        tpos = (i * BLOCK + _iota((BLOCK, 1), 0) + 1).astype(F32)
        for g, w in enumerate(POOL_WINDOWS):
            sl = slice(g * POOL_GROUP, (g + 1) * POOL_GROUP)
            pooled = _pool_sums(prv[:, sl], cur[:, sl], w) / jnp.minimum(tpos, float(w)) - cur[:, sl]
            cat_ref[:, sl] = (_nn(pooled, pw_ref[g]) * ps_ref[:, sl]).astype(cat_ref.dtype)
        q = _rope(cur[:, POOL_DIM:POOL_DIM + Q_DIM], c_ref[...], s_ref[...])
        kc = _rope(cur[:, POOL_DIM + Q_DIM:POOL_DIM + Q_DIM + KV_DIM], c_ref[...], s_ref[...])
        kp = _rope(prv[:, POOL_DIM + Q_DIM:POOL_DIM + Q_DIM + KV_DIM], cp_ref[...], sp_ref[...])
        vc = cur[:, POOL_DIM + Q_DIM + KV_DIM:]
        vp = prv[:, POOL_DIM + Q_DIM + KV_DIM:]
        ri = _iota((GQ * BLOCK, BLOCK), 0) % BLOCK
        cj = _iota((GQ * BLOCK, BLOCK), 1)
        mc = cj <= ri
        mp = jnp.logical_and(cj > ri, has_prev)
        outs, lses = [], []
        for g in range(N_KV_HEADS):
            hs = slice(g * HEAD_DIM, (g + 1) * HEAD_DIM)
            qg = _stack_heads(q, g) * scale
            sc = jnp.where(mc, _nt(qg, kc[:, hs]), NEG)
            sp = jnp.where(mp, _nt(qg, kp[:, hs]), NEG)
            sink = jnp.concatenate([jnp.full((BLOCK, 1), sk_ref[GQ * g + r], F32) for r in range(GQ)], axis=0)
            m = jnp.maximum(jnp.maximum(jnp.max(sc, axis=1, keepdims=True), jnp.max(sp, axis=1, keepdims=True)), sink)
            pc = jnp.exp(sc - m)
            pp = jnp.exp(sp - m)
            den = jnp.sum(pc, axis=1, keepdims=True) + jnp.sum(pp, axis=1, keepdims=True) + jnp.exp(sink - m)
            o = (_nn(pc, vc[:, hs]) + _nn(pp, vp[:, hs])) / den
            lse = m + jnp.log(den)
            for r in range(GQ):
                outs.append(o[r * BLOCK:(r + 1) * BLOCK])
                lses.append(lse[r * BLOCK:(r + 1) * BLOCK])
        attn = jnp.concatenate(outs, axis=1)
        at_ref[...] = attn
        cat_ref[:, POOL_DIM:] = attn.astype(cat_ref.dtype)
        lane = _iota((BLOCK, LANES), 1)
        lrow = jnp.zeros((BLOCK, LANES), F32)
        for h in range(N_HEADS):
            lrow = jnp.where(lane == h, lses[h], lrow)
        lse_ref[...] = lrow

    cur = lambda w: pl.BlockSpec((BLOCK, w), lambda i: (i, 0))
    prv = lambda w: pl.BlockSpec((BLOCK, w), lambda i: (jnp.maximum(i - 1, 0), 0))
    return pl.pallas_call(
        body, grid=(nb,),
        in_specs=[cur(MIX_IN_DIM), prv(MIX_IN_DIM), cur(LANES), cur(LANES), prv(LANES), prv(LANES),
                  pl.BlockSpec((4, POOL_GROUP, POOL_GROUP), lambda i: (0, 0, 0)), pl.BlockSpec((1, POOL_DIM), lambda i: (0, 0)),
                  pl.BlockSpec(memory_space=pltpu.SMEM)],
        out_specs=[cur(2 * POOL_DIM), cur(Q_DIM), cur(LANES)],
        out_shape=[_sds((T, 2 * POOL_DIM), MXU), _sds((T, Q_DIM)), _sds((T, LANES))],
        compiler_params=_cp("parallel"), name=name)(proj, proj, cos, sin_s, cos, sin_s, pool_w, pool_scale, sinks)


def _mixcore_bwd(proj, cos, sin_s, pool_w, pool_scale, sinks, attn, lse, dcat, name):
    T = proj.shape[0]
    nb = T // BLOCK
    scale = HEAD_DIM ** -0.5
    QO, KO, VO = POOL_DIM, POOL_DIM + Q_DIM, POOL_DIM + Q_DIM + KV_DIM

    def body(p_ref, pp_ref, pn_ref, c_ref, s_ref, cp_ref, sp_ref, cn_ref, sn_ref, pw_ref, ps_ref, sk_ref,
             at_ref, atn_ref, l_ref, ln_ref, d_ref, dn_ref, dp_ref, dpw_ref, dps_ref, dsk_ref):
        i = pl.program_id(0)
        has_prev = i > 0
        has_next = i < nb - 1
        cur = p_ref[...]
        prv = jnp.where(has_prev, pp_ref[...], 0.0)
        d_cur = d_ref[...]
        d_nxt = jnp.where(has_next, dn_ref[...], 0.0)

        tpos = (i * BLOCK + _iota((BLOCK, 1), 0) + 1).astype(F32)
        tpos2 = (i * BLOCK + _iota((2 * BLOCK, 1), 0) + 1).astype(F32)
        ps = ps_ref[...]
        dps_parts, dpw_parts = [], []
        for g, w in enumerate(POOL_WINDOWS):
            sl = slice(g * POOL_GROUP, (g + 1) * POOL_GROUP)
            pooled = _pool_sums(prv[:, sl], cur[:, sl], w) / jnp.minimum(tpos, float(w)) - cur[:, sl]
            mixed = _nn(pooled, pw_ref[g])
            dps_parts.append(jnp.sum(d_cur[:, sl] * mixed, axis=0, keepdims=True))
            dm2 = jnp.concatenate([d_cur[:, sl], d_nxt[:, sl]], axis=0) * ps[:, sl]
            dpw_parts.append(_tn(pooled, dm2[:BLOCK]))
            dpool2 = _nt(dm2, pw_ref[g])
            e = dpool2 / jnp.minimum(tpos2, float(w))
            sh = 1
            while sh < w:
                e = e + pltpu.roll(e, 2 * BLOCK - sh, 0)
                sh *= 2
            dp_ref[:, sl] = (e[:BLOCK] - dpool2[:BLOCK]).astype(dp_ref.dtype)
        dpsp = jnp.concatenate(dps_parts, axis=1)

        nxt = pn_ref[...]
        q = _rope(cur[:, QO:KO], c_ref[...], s_ref[...])
        qn = _rope(nxt[:, QO:KO], cn_ref[...], sn_ref[...])
        kc = _rope(cur[:, KO:VO], c_ref[...], s_ref[...])
        kp = _rope(prv[:, KO:VO], cp_ref[...], sp_ref[...])
        vc, vp = cur[:, VO:], prv[:, VO:]
        do, don = d_cur[:, POOL_DIM:], d_nxt[:, POOL_DIM:]
        dl = do * at_ref[...]
        dln = don * atn_ref[...]
        lse, lsen = l_ref[...], ln_ref[...]
        ri = _iota((GQ * BLOCK, BLOCK), 0) % BLOCK
        cj = _iota((GQ * BLOCK, BLOCK), 1)
        mc = cj <= ri
        mp = jnp.logical_and(cj > ri, has_prev)
        mn = jnp.logical_and(cj > ri, has_next)
        dq_parts, dk_parts, dv_parts, dsk_vals = [], [], [], []
        for g in range(N_KV_HEADS):
            hs = slice(g * HEAD_DIM, (g + 1) * HEAD_DIM)
            qg, qng = _stack_heads(q, g) * scale, _stack_heads(qn, g) * scale
            dog, dong = _stack_heads(do, g), _stack_heads(don, g)
            delta = jnp.sum(_stack_heads(dl, g), axis=1, keepdims=True)
            deltan = jnp.sum(_stack_heads(dln, g), axis=1, keepdims=True)
            lg, lng = _stack_cols(lse, g), _stack_cols(lsen, g)
            pc = jnp.where(mc, jnp.exp(_nt(qg, kc[:, hs]) - lg), 0.0)
            pp = jnp.where(mp, jnp.exp(_nt(qg, kp[:, hs]) - lg), 0.0)
            pn = jnp.where(mn, jnp.exp(_nt(qng, kc[:, hs]) - lng), 0.0)
            dsc = pc * (_nt(dog, vc[:, hs]) - delta)
            dsp = pp * (_nt(dog, vp[:, hs]) - delta)
            dsn = pn * (_nt(dong, vc[:, hs]) - deltan)
            dqg = (_nn(dsc, kc[:, hs]) + _nn(dsp, kp[:, hs])) * scale
            dq_parts += [dqg[r * BLOCK:(r + 1) * BLOCK] for r in range(GQ)]
            dk_parts.append(_tn(dsc, qg) + _tn(dsn, qng))
            dv_parts.append(_tn(pc, dog) + _tn(pn, dong))
            sink = jnp.concatenate([jnp.full((BLOCK, 1), sk_ref[GQ * g + r], F32) for r in range(GQ)], axis=0)
            dsk = -jnp.exp(sink - lg) * delta
            dsk_vals += [jnp.sum(dsk[r * BLOCK:(r + 1) * BLOCK], axis=0, keepdims=True) for r in range(GQ)]
        dq = _rope(jnp.concatenate(dq_parts, axis=1), c_ref[...], s_ref[...], inverse=True)
        dk = _rope(jnp.concatenate(dk_parts, axis=1), c_ref[...], s_ref[...], inverse=True)
        dp_ref[:, QO:KO] = dq.astype(dp_ref.dtype)
        dp_ref[:, KO:VO] = dk.astype(dp_ref.dtype)
        dp_ref[:, VO:] = jnp.concatenate(dv_parts, axis=1).astype(dp_ref.dtype)
        lane = _iota((1, LANES), 1)
        dskp = jnp.zeros((1, LANES), F32)
        for h in range(N_HEADS):
            dskp = jnp.where(lane == h, dsk_vals[h], dskp)

        @pl.when(i == 0)
        def _():
            dps_ref[...] = dpsp
            dsk_ref[...] = dskp
            for g in range(4):
                dpw_ref[g] = dpw_parts[g]

        @pl.when(i > 0)
        def _():
            dps_ref[...] += dpsp
            dsk_ref[...] += dskp
            for g in range(4):
                dpw_ref[g] += dpw_parts[g]

    cur = lambda w: pl.BlockSpec((BLOCK, w), lambda i: (i, 0))
    prv = lambda w: pl.BlockSpec((BLOCK, w), lambda i: (jnp.maximum(i - 1, 0), 0))
    nxt = lambda w: pl.BlockSpec((BLOCK, w), lambda i: (jnp.minimum(i + 1, nb - 1), 0))
    return pl.pallas_call(
        body, grid=(nb,),
        in_specs=[cur(MIX_IN_DIM), prv(MIX_IN_DIM), nxt(MIX_IN_DIM),
                  cur(LANES), cur(LANES), prv(LANES), prv(LANES), nxt(LANES), nxt(LANES),
                  pl.BlockSpec((4, POOL_GROUP, POOL_GROUP), lambda i: (0, 0, 0)), pl.BlockSpec((1, POOL_DIM), lambda i: (0, 0)),
                  pl.BlockSpec(memory_space=pltpu.SMEM),
                  cur(Q_DIM), nxt(Q_DIM), cur(LANES), nxt(LANES), cur(2 * POOL_DIM), nxt(2 * POOL_DIM)],
        out_specs=[cur(MIX_IN_DIM), pl.BlockSpec((4, POOL_GROUP, POOL_GROUP), lambda i: (0, 0, 0)),
                   pl.BlockSpec((1, POOL_DIM), lambda i: (0, 0)), pl.BlockSpec((1, LANES), lambda i: (0, 0))],
        out_shape=[_sds((T, MIX_IN_DIM), MXU), _sds((4, POOL_GROUP, POOL_GROUP)), _sds((1, POOL_DIM)), _sds((1, LANES))],
        compiler_params=_cp("arbitrary"), name=name)(
            proj, proj, proj, cos, sin_s, cos, sin_s, cos, sin_s, pool_w, pool_scale, sinks, attn, attn, lse, lse, dcat, dcat)


SSM_TC = 128
GROUP_W = SSM_D_INNER // SSM_GROUPS
PERM_W = GROUP_W + 2 * SSM_STATE


def _perm_col(n):
    nx = SSM_D_INNER // SSM_TC
    nbt = SSM_GROUPS
    x_idx = (n // 2) * 4 + n % 2
    b_idx = (n - nx) * 4 + 2
    c_idx = (n - nx - nbt) * 4 + 3
    return jnp.where(n < nx, x_idx, jnp.where(n < nx + nbt, b_idx, c_idx))


def _ssm_pre_fwd(xbc, cw, cb, name):
    T = xbc.shape[0]
    tm = min(T, 1024)
    K = SSM_CONV
    q = tm // SUBLANES

    def body(x_ref, xp_ref, w_ref, b_ref, o_ref):
        prev8 = jnp.where(pl.program_id(0) > 0, xp_ref[...], 0.0)
        o_ref[...] = _silu(_conv_rows(x_ref[...], prev8, w_ref[...], b_ref[...], K))

    tc = 512
    return pl.pallas_call(
        body, grid=(T // tm, SSM_CONV_DIM // tc),
        in_specs=[pl.BlockSpec((tm, tc), lambda i, j: (i, j)),
                  pl.BlockSpec((SUBLANES, tc), lambda i, j: (jnp.maximum(i * q - 1, 0), j)),
                  pl.BlockSpec((K, tc), lambda i, j: (0, j)), pl.BlockSpec((1, tc), lambda i, j: (0, j))],
        out_specs=pl.BlockSpec((tm, tc), lambda i, j: (i, j)), out_shape=_sds((T, SSM_CONV_DIM)),
        compiler_params=_cp("parallel", "parallel"), name=name)(xbc, xbc, cw, cb)


def _ssm_pre_bwd(xbc, cw, cb, dact_perm, name):
    T = xbc.shape[0]
    tm = min(T, 1024)
    nt = T // tm
    K = SSM_CONV
    q = tm // SUBLANES
    tc = SSM_TC

    def body(x_ref, xp_ref, xn_ref, d_ref, dn_ref, w_ref, b_ref, dx_ref, dw_ref, db_ref):
        i = pl.program_id(1)
        w = w_ref[...]
        b = b_ref[...]
        cur = x_ref[...]
        prev8 = jnp.where(i > 0, xp_ref[...], 0.0)
        nxt8 = xn_ref[...]
        d_cur = d_ref[...] * _dsilu(_conv_rows(cur, prev8, w, b, K))
        d_nxt = jnp.where(i == nt - 1, 0.0, dn_ref[...] * _dsilu(_conv_rows(nxt8, cur[tm - SUBLANES:], w, b, K)))
        dx = d_cur * w[K - 1:K, :]
        for s in range(1, K):
            dx = dx + _shift_up(d_cur, d_nxt, s) * w[K - 1 - s:K - s, :]
        dx_ref[...] = dx.astype(dx_ref.dtype)
        dwp = jnp.concatenate([jnp.sum(d_cur * _shift_down(cur, prev8, K - 1 - k), axis=0, keepdims=True) for k in range(K)], axis=0)
        dbp = jnp.sum(d_cur, axis=0, keepdims=True)

        @pl.when(i == 0)
        def _():
            dw_ref[...] = dwp
            db_ref[...] = dbp

        @pl.when(i > 0)
        def _():
            dw_ref[...] += dwp
            db_ref[...] += dbp

    nxt_row = lambda i: jnp.minimum((i + 1) * q, nt * q - 1)
    return pl.pallas_call(
        body, grid=(SSM_CONV_DIM // tc, nt),
        in_specs=[pl.BlockSpec((tm, tc), lambda j, i: (i, j)),
                  pl.BlockSpec((SUBLANES, tc), lambda j, i: (jnp.maximum(i * q - 1, 0), j)),
                  pl.BlockSpec((SUBLANES, tc), lambda j, i: (nxt_row(i), j)),
                  pl.BlockSpec((tm, tc), lambda j, i: (i, _perm_col(j))),
                  pl.BlockSpec((SUBLANES, tc), lambda j, i: (nxt_row(i), _perm_col(j))),
                  pl.BlockSpec((K, tc), lambda j, i: (0, j)), pl.BlockSpec((1, tc), lambda j, i: (0, j))],
        out_specs=[pl.BlockSpec((tm, tc), lambda j, i: (i, j)), pl.BlockSpec((K, tc), lambda j, i: (0, j)),
                   pl.BlockSpec((1, tc), lambda j, i: (0, j))],
        out_shape=[_sds((T, SSM_CONV_DIM), MXU), _sds((K, SSM_CONV_DIM)), _sds((1, SSM_CONV_DIM))],
        compiler_params=_cp("parallel", "arbitrary"), name=name)(xbc, xbc, xbc, dact_perm, dact_perm, cw, cb)


def _dot_hi(a, b):
    return jnp.dot(a, b, precision=HI, preferred_element_type=F32)


def _ssd_common(g, dtraw, bias, alog):
    L = SSM_CHUNK
    xb = dtraw + bias
    dt = jnp.maximum(xb, 0.0) + jnp.log1p(jnp.exp(-jnp.abs(xb)))
    A = -jnp.exp(alog)
    a = dt * A
    tril = (_iota((L, L), 1) <= _iota((L, L), 0)).astype(F32)
    acs = _dot_hi(tril, a)
    head_of_lane = 4 * g + _iota((LANES, GROUP_W), 1) // HEAD_DIM
    esel = (_iota((LANES, GROUP_W), 0) == head_of_lane).astype(F32)
    eselT = (_iota((GROUP_W, LANES), 1) == 4 * g + _iota((GROUP_W, LANES), 0) // HEAD_DIM).astype(F32)
    return xb, dt, A, a, tril, acs, esel, eselT


def _ssd_fwd(xact, dtraw, dt_bias, a_log, name):
    T = xact.shape[0]
    nc = T // SSM_CHUNK
    L = SSM_CHUNK
    nxg = SSM_D_INNER // GROUP_W

    def body(x_ref, b_ref, c_ref, dt_ref, bias_ref, al_ref, y_ref, st_ref, state):
        c, g = pl.program_id(0), pl.program_id(1)

        @pl.when(c == 0)
        def _():
            state[g] = jnp.zeros((GROUP_W, SSM_STATE), F32)

        _, dt, A, a, tril, acs, esel, eselT = _ssd_common(g, dt_ref[...], bias_ref[...], al_ref[...])
        x, B, C = x_ref[...], b_ref[...], c_ref[...]
        dtx = _dot_hi(dt, esel)
        acs_x = _dot_hi(acs, esel)
        acs_xT = lax.dot_general(eselT, acs, (((1,), (1,)), ((), ())), precision=HI, preferred_element_type=F32)
        X = x * dtx
        CB = _nt(C, B)
        yd = []
        for r in range(GQ):
            col = acs_x[:, r * HEAD_DIM:r * HEAD_DIM + 1]
            row = acs_xT[r * HEAD_DIM:r * HEAD_DIM + 1, :]
            Lm = jnp.exp(jnp.where(tril > 0, col - row, NEG))
            yd.append(_nn(CB * Lm, X[:, r * HEAD_DIM:(r + 1) * HEAD_DIM]))
        S = state[g]
        st_ref[...] = S
        yo = _nt(C, S) * jnp.exp(acs_x)
        y_ref[...] = jnp.concatenate(yd, axis=1) + yo
        last = acs_x[L - 1:L, :]
        contrib = _tn(X * jnp.exp(last - acs_x), B)
        state[g] = S * jnp.exp(acs_xT[:, L - 1:L]) + contrib

    return pl.pallas_call(
        body, grid=(nc, SSM_GROUPS),
        in_specs=[pl.BlockSpec((L, GROUP_W), lambda c, g: (c, g)),
                  pl.BlockSpec((L, SSM_STATE), lambda c, g: (c, 2 * nxg + g)),
                  pl.BlockSpec((L, SSM_STATE), lambda c, g: (c, 3 * nxg + g)),
                  pl.BlockSpec((L, LANES), lambda c, g: (c, 0)),
                  pl.BlockSpec((1, LANES), lambda c, g: (0, 0)), pl.BlockSpec((1, LANES), lambda c, g: (0, 0))],
        out_specs=[pl.BlockSpec((L, GROUP_W), lambda c, g: (c, g)),
                   pl.BlockSpec((None, None, GROUP_W, SSM_STATE), lambda c, g: (c, g, 0, 0))],
        out_shape=[_sds((T, SSM_D_INNER)), _sds((nc, SSM_GROUPS, GROUP_W, SSM_STATE))],
        scratch_shapes=[pltpu.VMEM((SSM_GROUPS, GROUP_W, SSM_STATE), F32)],
        compiler_params=_cp("arbitrary", "arbitrary"), name=name)(xact, xact, xact, dtraw, dt_bias, a_log)


def _ssd_bwd(xact, dtraw, dt_bias, a_log, d_skip, states, dy, name):
    T = xact.shape[0]
    nc = T // SSM_CHUNK
    L = SSM_CHUNK
    nxg = SSM_D_INNER // GROUP_W

    def body(x_ref, b_ref, c_ref, dt_ref, bias_ref, al_ref, dsk_ref, st_ref, dy_ref,
             dxp_ref, ddt_ref, dbias_ref, dal_ref, dd_ref, dstate):
        cc, g = pl.program_id(0), pl.program_id(1)

        @pl.when(cc == 0)
        def _():
            dstate[g] = jnp.zeros((GROUP_W, SSM_STATE), F32)

        xb, dt, A, a, tril, acs, esel, eselT = _ssd_common(g, dt_ref[...], bias_ref[...], al_ref[...])
        x, B, C, dY = x_ref[...], b_ref[...], c_ref[...], dy_ref[...]
        dtx = _dot_hi(dt, esel)
        acs_x = _dot_hi(acs, esel)
        acs_xT = lax.dot_general(eselT, acs, (((1,), (1,)), ((), ())), precision=HI, preferred_element_type=F32)
        X = x * dtx
        CB = _nt(C, B)
        S = st_ref[...]
        dS_out = dstate[g]
        lane = _iota((L, LANES), 1)
        dacs = jnp.zeros((L, LANES), F32)
        dcb_sum = jnp.zeros((L, L), F32)
        dX_parts = []
        for r in range(GQ):
            hs = slice(r * HEAD_DIM, (r + 1) * HEAD_DIM)
            col = acs_x[:, r * HEAD_DIM:r * HEAD_DIM + 1]
            row = acs_xT[r * HEAD_DIM:r * HEAD_DIM + 1, :]
            Lm = jnp.exp(jnp.where(tril > 0, col - row, NEG))
            M = CB * Lm
            dM = _nt(dY[:, hs], X[:, hs])
            dX_parts.append(_tn(M, dY[:, hs]))
            Wm = dM * M
            dcb_sum = dcb_sum + dM * Lm
            dacs = dacs + jnp.where(lane == 4 * g + r, jnp.sum(Wm - Wm.T, axis=1, keepdims=True), 0.0)
        E_x = jnp.exp(acs_x)
        G = _nt(C, S)
        dG = dY * E_x
        E = jnp.exp(acs)
        dacs = dacs + _dot_hi(dY * G, eselT) * E
        last_x = acs_x[L - 1:L, :]
        dec_x = jnp.exp(last_x - acs_x)
        dDX = _nt(B, dS_out)
        last = acs[L - 1:L, :]
        dec = jnp.exp(last - acs)
        ddec = _dot_hi(dDX * X, eselT) * dec
        dacs = dacs - ddec
        zrow = jnp.broadcast_to(jnp.sum(dS_out * S, axis=1, keepdims=True), (GROUP_W, LANES))
        ztot = lax.dot_general(zrow, eselT, (((0,), (0,)), ((), ())), precision=HI, preferred_element_type=F32)[0:1, :]
        dlast = jnp.sum(ddec, axis=0, keepdims=True) + jnp.exp(last) * ztot
        dacs = dacs + jnp.where(_iota((L, LANES), 0) == L - 1, dlast, 0.0)
        triu = (_iota((L, L), 0) <= _iota((L, L), 1)).astype(F32)
        da = _dot_hi(triu, dacs)
        dX = jnp.concatenate(dX_parts, axis=1) + dec_x * dDX
        dsk = dsk_ref[...]
        ddt = da * A + _dot_hi(dX * x, eselT)
        sig = 1.0 / (1.0 + jnp.exp(-xb))
        ddtraw = ddt * sig
        dal = jnp.sum(da * dt, axis=0, keepdims=True) * A
        ddp = jnp.sum(_dot_hi(dY * x, eselT), axis=0, keepdims=True)
        dbp = jnp.sum(ddtraw, axis=0, keepdims=True)
        dxp_ref[:, :GROUP_W] = dX * dtx + dY * dsk
        dxp_ref[:, GROUP_W:GROUP_W + SSM_STATE] = _tn(dcb_sum, C) + _nn(X * dec_x, dS_out)
        dxp_ref[:, GROUP_W + SSM_STATE:] = _nn(dcb_sum, B) + _nn(dG, S)
        dstate[g] = dS_out * jnp.exp(acs_xT[:, L - 1:L]) + _tn(dG, C)

        @pl.when(g == 0)
        def _():
            ddt_ref[...] = ddtraw

        @pl.when(g > 0)
        def _():
            ddt_ref[...] += ddtraw

        first = jnp.logical_and(cc == 0, g == 0)

        @pl.when(first)
        def _():
            dbias_ref[...] = dbp
            dal_ref[...] = dal
            dd_ref[...] = ddp

        @pl.when(jnp.logical_not(first))
        def _():
            dbias_ref[...] += dbp
            dal_ref[...] += dal
            dd_ref[...] += ddp

    rc = lambda c: nc - 1 - c
    vec = pl.BlockSpec((1, LANES), lambda c, g: (0, 0))
    return pl.pallas_call(
        body, grid=(nc, SSM_GROUPS),
        in_specs=[pl.BlockSpec((L, GROUP_W), lambda c, g: (rc(c), g)),
                  pl.BlockSpec((L, SSM_STATE), lambda c, g: (rc(c), 2 * nxg + g)),
                  pl.BlockSpec((L, SSM_STATE), lambda c, g: (rc(c), 3 * nxg + g)),
                  pl.BlockSpec((L, LANES), lambda c, g: (rc(c), 0)), vec, vec,
                  pl.BlockSpec((1, GROUP_W), lambda c, g: (0, g)),
                  pl.BlockSpec((None, None, GROUP_W, SSM_STATE), lambda c, g: (rc(c), g, 0, 0)),
                  pl.BlockSpec((L, GROUP_W), lambda c, g: (rc(c), g))],
        out_specs=[pl.BlockSpec((L, PERM_W), lambda c, g: (rc(c), g)),
                   pl.BlockSpec((L, LANES), lambda c, g: (rc(c), 0)), vec, vec, vec],
        out_shape=[_sds((T, SSM_GROUPS * PERM_W)), _sds((T, LANES)), _sds((1, LANES)), _sds((1, LANES)), _sds((1, LANES))],
        scratch_shapes=[pltpu.VMEM((SSM_GROUPS, GROUP_W, SSM_STATE), F32)],
        compiler_params=_cp("arbitrary", "arbitrary"), name=name)(xact, xact, xact, dtraw, dt_bias, a_log, d_skip, states, dy)


def _ssm_post_fwd(y, xact, z, d_skip, nw, name):
    T = y.shape[0]
    tm = min(T, 256)
    W = SSM_D_INNER

    def body(y_ref, x_ref, z_ref, d_ref, w_ref, o_ref):
        y2 = (y_ref[...] + d_ref[...] * x_ref[...]) * _silu(z_ref[...])
        r = lax.rsqrt(jnp.mean(y2 * y2, axis=-1, keepdims=True) + SSM_NORM_EPS)
        o_ref[...] = (y2 * r * w_ref[...]).astype(o_ref.dtype)

    row = pl.BlockSpec((tm, W), lambda i: (i, 0))
    vec = pl.BlockSpec((1, W), lambda i: (0, 0))
    return pl.pallas_call(
        body, grid=(T // tm,), in_specs=[row, row, row, vec, vec], out_specs=row, out_shape=_sds((T, W), MXU),
        compiler_params=_cp("parallel"), name=name)(y, xact, z, d_skip, nw)


def _ssm_post_bwd(y, xact, z, d_skip, nw, dyn, name):
    T = y.shape[0]
    tm = min(T, 256)
    W = SSM_D_INNER

    def body(y_ref, x_ref, z_ref, d_ref, w_ref, dn_ref, dyg_ref, dz_ref, dw_ref):
        zv = z_ref[...]
        sz = _silu(zv)
        yg = y_ref[...] + d_ref[...] * x_ref[...]
        y2 = yg * sz
        r = lax.rsqrt(jnp.mean(y2 * y2, axis=-1, keepdims=True) + SSM_NORM_EPS)
        y2h = y2 * r
        dn = dn_ref[...]
        gy = dn * w_ref[...]
        dy2 = r * (gy - y2h * jnp.mean(gy * y2h, axis=-1, keepdims=True))
        dyg_ref[...] = dy2 * sz
        dz_ref[...] = (dy2 * yg * _dsilu(zv)).astype(dz_ref.dtype)
        part = jnp.sum(dn * y2h, axis=0, keepdims=True)

        @pl.when(pl.program_id(0) == 0)
        def _():
            dw_ref[...] = part

        @pl.when(pl.program_id(0) > 0)
        def _():
            dw_ref[...] += part

    row = pl.BlockSpec((tm, W), lambda i: (i, 0))
    vec = pl.BlockSpec((1, W), lambda i: (0, 0))
    return pl.pallas_call(
        body, grid=(T // tm,), in_specs=[row, row, row, vec, vec, row], out_specs=[row, row, vec],
        out_shape=[_sds((T, W)), _sds((T, W), MXU), _sds((1, W))],
        compiler_params=_cp("arbitrary"), name=name)(y, xact, z, d_skip, nw, dyn)


def _local_step(x0, cos, sin_s, target, P):
    mmf = functools.partial(_mm, tm=512)
    big, small = {}, {}
    h0 = _rmsnorm_fwd(x0, P["nm"][0], "norm_mix0")
    proj0 = mmf(h0, P["wmi"], tn=1280, tk=1024, name="mix_in")
    cat, attn, lse = _mixcore_fwd(proj0, cos, sin_s, P["pool_w"], P["pool_scale"], P["sinks"], "mixcore_fwd")
    x1 = mmf(cat, P["wmo"], tn=1024, tk=1024, res=x0, name="mix_out")

    def ffn_fwd(xin, i):
        hf = _rmsnorm_fwd(xin, P["nf"][i], f"norm_ffn{i}")
        hid = mmf(hf, P["wup"][i], tn=1408, tk=1024, name=f"ffn_up{i}")
        act = _ffn_mid_fwd(hid, P["fcw"][i], P["fcb"][i], f"ffn_mid_fwd{i}")
        xout = mmf(act, P["wdn"][i], tn=1024, tk=D_FF, res=xin, name=f"ffn_down{i}")
        return hf, hid, act, xout

    hf0, hid0, act0, x2 = ffn_fwd(x1, 0)
    h1 = _rmsnorm_fwd(x2, P["nm"][1], "norm_mix1")
    z = mmf(h1, P["wz"], tn=1024, tk=1024, name="ssm_in_z")
    xbc = mmf(h1, P["wxbc"], tn=1024, tk=1024, name="ssm_in_xbc")
    dtraw = mmf(h1, P["wdt"], tn=128, tk=1024, name="ssm_in_dt")
    xact = _ssm_pre_fwd(xbc, P["scw"], P["scb"], "ssm_pre_fwd")
    y, states = _ssd_fwd(xact, dtraw, P["dt_bias"], P["a_log"], "ssd_fwd")
    yn = _ssm_post_fwd(y, xact, z, P["d_exp"], P["snorm"], "ssm_post_fwd")
    x3 = mmf(yn, P["wso"], tn=1024, tk=SSM_D_INNER, res=x2, name="ssm_out")
    hf1, hid1, act1, x4 = ffn_fwd(x3, 1)
    loss_row, dx4, d_nfin = _loss_head(x4, P["nfin"], target, "loss_head")
    small["norm_final"] = d_nfin

    def ffn_bwd(xin, dxo, hf, hid, act, i):
        da = mmf(dxo, P["wdn"][i], tb=True, tn=1408, tk=1024, name=f"ffn_down_dx{i}")
        big[f"ffn_w_down{i}"] = _mm(act, dxo, ta=True, tm=1408, tn=1024, tk=512, name=f"ffn_down_dw{i}").reshape(N_CHIPS, D_FF // N_CHIPS, D_MODEL)
        dhid, dcw, dcb = _ffn_mid_bwd(hid, P["fcw"][i], P["fcb"][i], da, f"ffn_mid_bwd{i}")
        dhf = mmf(dhid, P["wup"][i], tb=True, tn=1024, tk=1408, name=f"ffn_up_dx{i}")
        big[f"ffn_w_up{i}"] = _mm(hf, dhid, ta=True, tm=1024, tn=1408, tk=512, out_shard_perm=(0, 2, 1, 3), name=f"ffn_up_dw{i}")
        dxi, dnf = _rmsnorm_bwd(xin, P["nf"][i], dhf, dxo, f"norm_ffn_bwd{i}")
        return dxi, dnf, dcw, dcb

    dx3, dnf1, dfcw1, dfcb1 = ffn_bwd(x3, dx4, hf1, hid1, act1, 1)
    dyn = mmf(dx3, P["wso"], tb=True, tn=1024, tk=1024, name="ssm_out_dx")
    big["ssm_w_out"] = _mm(yn, dx3, ta=True, tm=1024, tn=1024, tk=512, name="ssm_out_dw").reshape(N_CHIPS, SSM_D_INNER // N_CHIPS, D_MODEL)
    dyg, dz, d_snorm = _ssm_post_bwd(y, xact, z, P["d_exp"], P["snorm"], dyn, "ssm_post_bwd")
    dxact_p, ddtraw, d_dtb, d_alog, d_dskip = _ssd_bwd(xact, dtraw, P["dt_bias"], P["a_log"], P["d_exp"], states, dyg, "ssd_bwd")
    dxbc, d_scw, d_scb = _ssm_pre_bwd(xbc, P["scw"], P["scb"], dxact_p, "ssm_pre_bwd")
    dh1 = mmf(dz, P["wz"], tb=True, tn=1024, tk=1024, name="ssm_in_dx_z")
    dh1 = mmf(dxbc, P["wxbc"], tb=True, tn=1024, tk=1024, res=dh1, name="ssm_in_dx_xbc")
    dh1 = mmf(ddtraw, P["wdt"], tb=True, tn=1024, tk=128, res=dh1, name="ssm_in_dx_dt")
    dwz = _mm(h1, dz, ta=True, tm=1024, tn=1024, tk=512, name="ssm_in_dw_z")
    dwxbc = _mm(h1, dxbc, ta=True, tm=1024, tn=1024, tk=512, name="ssm_in_dw_xbc")
    dwdt = _mm(h1, ddtraw, ta=True, tm=1024, tn=128, tk=512, name="ssm_in_dw_dt")
    dwsi = jnp.concatenate([dwz, dwxbc, dwdt[:, :SSM_HEADS]], axis=1)
    big["ssm_w_in"] = dwsi.reshape(D_MODEL, N_CHIPS, SSM_IN_DIM // N_CHIPS).transpose(1, 0, 2)
    dx2, dnm1 = _rmsnorm_bwd(x2, P["nm"][1], dh1, dx3, "norm_mix_bwd1")
    dx1, dnf0, dfcw0, dfcb0 = ffn_bwd(x1, dx2, hf0, hid0, act0, 0)
    dcat = mmf(dx1, P["wmo"], tb=True, tn=1024, tk=1024, name="mix_out_dx")
    big["mix_w_out"] = _mm(cat, dx1, ta=True, tm=1024, tn=1024, tk=512, name="mix_out_dw").reshape(N_CHIPS, D_MODEL // N_CHIPS, D_MODEL)
    dproj0, d_pw, d_ps, d_sk = _mixcore_bwd(proj0, cos, sin_s, P["pool_w"], P["pool_scale"], P["sinks"], attn, lse, dcat, "mixcore_bwd")
    dh0 = mmf(dproj0, P["wmi"], tb=True, tn=1024, tk=1280, name="mix_in_dx")
    dwmi = _mm(h0, dproj0, ta=True, tm=1024, tn=1280, tk=512, name="mix_in_dw")
    big["mix_w_in"] = dwmi.reshape(D_MODEL, N_CHIPS, MIX_IN_DIM // N_CHIPS).transpose(1, 0, 2)
    dx0, dnm0 = _rmsnorm_bwd(x0, P["nm"][0], dh0, dx1, "norm_mix_bwd0")

    def unperm_cols(a):
        r = a.shape[0]
        t = a.reshape(r, N_CHIPS, FFN_TC)
        return jnp.stack([t[:, p] for p in _PERM], axis=0)

    small["norm_mix"] = jnp.concatenate([dnm0, dnm1], axis=0)
    small["norm_ffn"] = jnp.concatenate([dnf0, dnf1], axis=0)
    small["pool_w"] = d_pw.reshape(4 * POOL_GROUP, POOL_GROUP)
    small["pool_scale"] = d_ps
    small["attn_sinks"] = d_sk
    small["ssm_dt_bias"] = d_dtb
    small["ssm_A_log"] = d_alog
    small["ssm_D"] = d_dskip
    fcb = jnp.stack([unperm_cols(dfcb0), unperm_cols(dfcb1)], axis=0)
    small["ffn_conv_b"] = fcb.reshape(2, 2 * D_FF)
    small["ssm_conv_w"] = d_scw.reshape(SSM_CONV, N_CHIPS, SSM_CONV_DIM // N_CHIPS).transpose(1, 0, 2)
    small["ssm_conv_b"] = d_scb.reshape(N_CHIPS, 1, SSM_CONV_DIM // N_CHIPS)
    small["ssm_norm"] = d_snorm.reshape(N_CHIPS, 1, SSM_D_INNER // N_CHIPS)
    small["ffn_conv_w"] = jnp.concatenate([unperm_cols(dfcw0), unperm_cols(dfcw1)], axis=1)
    return loss_row, dx0, big, small


ANY = pl.BlockSpec(memory_space=pl.ANY)


def _place():
    return lax.axis_index("x"), lax.axis_index("y"), lax.axis_index("c")


def _gather_shards(shards, name):
    n = len(shards)

    def body(*refs):
        ins, outs = refs[:n], refs[n:2 * n]
        send, recv, loc = refs[2 * n:]
        x, y, c = _place()
        k = 2 * x + y
        chips = [(1 - x, y), (x, 1 - y), (1 - x, 1 - y)]
        local = [pltpu.make_async_copy(ins[a], outs[a].at[k], loc.at[a]) for a in range(n)]
        for cp in local:
            cp.start()
        sends = [pltpu.make_async_remote_copy(ins[a], outs[a].at[k], send.at[a, j], recv.at[a, j],
                                              device_id=(px, py, c), device_id_type=MESH)
                 for a in range(n) for j, (px, py) in enumerate(chips)]
        for cp in sends:
            cp.start()
        for a in range(n):
            for j, (px, py) in enumerate(chips):
                pltpu.make_async_remote_copy(ins[a], outs[a].at[2 * px + py], send.at[a, j], recv.at[a, j],
                                             device_id=(px, py, c), device_id_type=MESH).wait_recv()
        for cp in sends:
            cp.wait_send()
        for cp in local:
            cp.wait()

    return pl.pallas_call(
        body, in_specs=[ANY] * n, out_specs=[ANY] * n,
        out_shape=[_sds((N_CHIPS,) + s.shape, s.dtype) for s in shards],
        scratch_shapes=[pltpu.SemaphoreType.DMA((n, 3)), pltpu.SemaphoreType.DMA((n, 3)), pltpu.SemaphoreType.DMA((n,))],
        compiler_params=pltpu.CompilerParams(has_side_effects=True), name=name)(*shards)


def _pair_exchange(gs, name):
    n = len(gs)

    def body(*refs):
        ins, outs = refs[:n], refs[n:2 * n]
        send, recv = refs[2 * n:]
        x, y, c = _place()
        cps = []
        for a in range(n):
            r2 = gs[a].shape[1] // 2
            src = ins[a].at[:, pl.ds(pl.multiple_of((1 - c) * r2, SUBLANES), r2), :]
            cps.append(pltpu.make_async_remote_copy(src, outs[a], send.at[a], recv.at[a],
                                                    device_id=(x, y, 1 - c), device_id_type=MESH))
        for cp in cps:
            cp.start()
        for cp in cps:
            cp.wait()

    return pl.pallas_call(
        body, in_specs=[ANY] * n, out_specs=[ANY] * n,
        out_shape=[_sds((N_CHIPS, g.shape[1] // 2, g.shape[2]), g.dtype) for g in gs],
        scratch_shapes=[pltpu.SemaphoreType.DMA((n,)), pltpu.SemaphoreType.DMA((n,))],
        compiler_params=pltpu.CompilerParams(has_side_effects=True), name=name)(*gs)


def _chip_exchange(ps, name):
    n = len(ps)

    def body(*refs):
        ins, outs = refs[:n], refs[n:2 * n]
        send, recv, loc = refs[2 * n:]
        x, y, c = _place()
        k = 2 * x + y
        chips = [(1 - x, y), (x, 1 - y), (1 - x, 1 - y)]
        local = [pltpu.make_async_copy(ins[a].at[k], outs[a].at[k], loc.at[a]) for a in range(n)]
        for cp in local:
            cp.start()
        sends = [pltpu.make_async_remote_copy(ins[a].at[2 * px + py], outs[a].at[k], send.at[a, j], recv.at[a, j],
                                              device_id=(px, py, c), device_id_type=MESH)
                 for a in range(n) for j, (px, py) in enumerate(chips)]
        for cp in sends:
            cp.start()
        for a in range(n):
            for j, (px, py) in enumerate(chips):
                pltpu.make_async_remote_copy(ins[a].at[k], outs[a].at[2 * px + py], send.at[a, j], recv.at[a, j],
                                             device_id=(px, py, c), device_id_type=MESH).wait_recv()
        for cp in sends:
            cp.wait_send()
        for cp in local:
            cp.wait()

    return pl.pallas_call(
        body, in_specs=[ANY] * n, out_specs=[ANY] * n, out_shape=[_sds(p.shape, p.dtype) for p in ps],
        scratch_shapes=[pltpu.SemaphoreType.DMA((n, 3)), pltpu.SemaphoreType.DMA((n, 3)), pltpu.SemaphoreType.DMA((n,))],
        compiler_params=pltpu.CompilerParams(has_side_effects=True), name=name)(*ps)


def _half_exchange(fs, name):
    n = len(fs)

    def body(*refs):
        ins, outs = refs[:n], refs[n:2 * n]
        send, recv, loc = refs[2 * n:]
        x, y, c = _place()
        local = [pltpu.make_async_copy(ins[a], outs[a].at[c], loc.at[a]) for a in range(n)]
        for cp in local:
            cp.start()
        sends = [pltpu.make_async_remote_copy(ins[a], outs[a].at[c], send.at[a], recv.at[a],
                                              device_id=(x, y, 1 - c), device_id_type=MESH) for a in range(n)]
        for cp in sends:
            cp.start()
        for a in range(n):
            pltpu.make_async_remote_copy(ins[a], outs[a].at[1 - c], send.at[a], recv.at[a],
                                         device_id=(x, y, 1 - c), device_id_type=MESH).wait_recv()
        for cp in sends:
            cp.wait_send()
        for cp in local:
            cp.wait()

    return pl.pallas_call(
        body, in_specs=[ANY] * n, out_specs=[ANY] * n, out_shape=[_sds((2,) + f.shape, f.dtype) for f in fs],
        scratch_shapes=[pltpu.SemaphoreType.DMA((n,)), pltpu.SemaphoreType.DMA((n,)), pltpu.SemaphoreType.DMA((n,))],
        compiler_params=pltpu.CompilerParams(has_side_effects=True), name=name)(*fs)


def _row_tile(rows, cols, budget=2 * 1024 * 1024):
    best = SUBLANES
    for t in range(SUBLANES, rows + 1, SUBLANES):
        if rows % t == 0 and t * cols * 4 <= budget:
            best = t
    return best


def _pair_sum(g, got, cidx, name):
    _, R, C = g.shape
    r2 = R // 2
    tr = _row_tile(r2, C)
    nr = r2 // tr

    def body(c_ref, g_ref, o_ref_in, o_ref):
        o_ref[...] = g_ref[...] + o_ref_in[...]

    return pl.pallas_call(
        body,
        grid_spec=pltpu.PrefetchScalarGridSpec(
            num_scalar_prefetch=1, grid=(N_CHIPS, nr),
            in_specs=[pl.BlockSpec((None, tr, C), lambda k, i, c: (k, c[0] * nr + i, 0)),
                      pl.BlockSpec((None, tr, C), lambda k, i, c: (k, i, 0))],
            out_specs=pl.BlockSpec((None, tr, C), lambda k, i, c: (k, i, 0))),
        out_shape=_sds((N_CHIPS, r2, C)), compiler_params=_cp("parallel", "parallel"), name=name)(cidx, g, got)


def _chip_sum(parts, name):
    _, r2, C = parts.shape
    tr = _row_tile(r2, C, budget=1024 * 1024)

    def body(p_ref, o_ref):
        o_ref[...] = ((p_ref[0] + p_ref[1]) + p_ref[2]) + p_ref[3]

    return pl.pallas_call(
        body, grid=(r2 // tr,), in_specs=[pl.BlockSpec((N_CHIPS, tr, C), lambda i: (0, i, 0))],
        out_specs=pl.BlockSpec((tr, C), lambda i: (i, 0)), out_shape=_sds((r2, C)),
        compiler_params=_cp("parallel"), name=name)(parts)


def _adamw_math(w, g, m, v):
    m2 = ADAM_B1 * m + (1.0 - ADAM_B1) * g
    v2 = ADAM_B2 * v + (1.0 - ADAM_B2) * (g * g)
    m_hat = m2 / (1.0 - ADAM_B1 ** ADAM_STEP)
    v_hat = v2 / (1.0 - ADAM_B2 ** ADAM_STEP)
    delta = -ADAM_LR * (m_hat / (jnp.sqrt(v_hat) + ADAM_EPS) + ADAM_WD * w)
    return delta, m2, v2


def _adamw(w, m, v, gparts, name):
    Lw, R, C = w.shape
    tr = _row_tile(R, C, budget=1024 * 1024)

    def body(*refs):
        w_ref, m_ref, v_ref = refs[:3]
        g_refs = refs[3:3 + Lw]
        go_ref, d_ref, mo_ref, vo_ref = refs[3 + Lw:]
        g = g_refs[0][...]
        for l in range(1, Lw):
            g = jnp.where(pl.program_id(0) == l, g_refs[l][...], g)
        d, m2, v2 = _adamw_math(w_ref[...], g, m_ref[...], v_ref[...])
        go_ref[...] = g
        d_ref[...] = d
        mo_ref[...] = m2
        vo_ref[...] = v2

    blk = pl.BlockSpec((None, tr, C), lambda l, i: (l, i, 0))
    gblk = pl.BlockSpec((tr, C), lambda l, i: (i, 0))
    return pl.pallas_call(
        body, grid=(Lw, R // tr), in_specs=[blk, blk, blk] + [gblk] * Lw, out_specs=[blk] * 4,
        out_shape=[_sds((Lw, R, C))] * 4, compiler_params=_cp("parallel", "parallel"), name=name)(w, m, v, *gparts)


def _small_reduce_adamw(items, loss_row, name):
    n = len(items)
    gshapes = [it[0].shape for it in items] + [loss_row.shape]
    pshapes = [it[1].shape for it in items]
    ng = n + 1

    def body(*refs):
        g_in = refs[:ng]
        wmv = refs[ng:ng + 3 * n]
        outs = refs[ng + 3 * n:ng + 3 * n + 4 * n + 1]
        bufs = refs[ng + 7 * n + 1:ng + 7 * n + 1 + ng]
        send, recv = refs[-2:]
        x, y, c = _place()
        me = 4 * x + 2 * y + c
        k = 2 * x + y
        flips = [(fx, fy, fc) for fx in (0, 1) for fy in (0, 1) for fc in (0, 1)][1:]

        def peer(f):
            return (x ^ f[0], y ^ f[1], c ^ f[2])

        def slot(p):
            return 4 * p[0] + 2 * p[1] + p[2]

        for a in range(ng):
            bufs[a][me] = g_in[a][...]
        sends = [pltpu.make_async_remote_copy(g_in[a], bufs[a].at[me], send.at[a, j], recv.at[a, j],
                                              device_id=peer(f), device_id_type=MESH)
                 for a in range(ng) for j, f in enumerate(flips)]
        for cp in sends:
            cp.start()
        for a in range(ng):
            for j, f in enumerate(flips):
                pltpu.make_async_remote_copy(g_in[a], bufs[a].at[slot(peer(f))], send.at[a, j], recv.at[a, j],
                                             device_id=peer(f), device_id_type=MESH).wait_recv()
        for cp in sends:
            cp.wait_send()
        for a in range(ng):
            sharded = len(gshapes[a]) == 3

            def part(d):
                return bufs[a][d, k] if sharded else bufs[a][d]

            tot = part(0)
            for d in range(1, N_DEV):
                tot = tot + part(d)
            if a == n:
                outs[4 * n][...] = tot
                continue
            pr, pc = pshapes[a]
            g = tot[:pr, :pc]
            w_ref, m_ref, v_ref = wmv[3 * a:3 * a + 3]
            d_, m2, v2 = _adamw_math(w_ref[...], g, m_ref[...], v_ref[...])
            outs[4 * a][...] = g
            outs[4 * a + 1][...] = d_
            outs[4 * a + 2][...] = m2
            outs[4 * a + 3][...] = v2

    vm = pl.BlockSpec(memory_space=pltpu.VMEM)
    args = [it[0] for it in items] + [loss_row]
    for it in items:
        args += [it[1], it[2], it[3]]
    out_shape = []
    for ps in pshapes:
        out_shape += [_sds(ps)] * 4
    out_shape.append(_sds(loss_row.shape))
    return pl.pallas_call(
        body, in_specs=[vm] * len(args), out_specs=[vm] * len(out_shape), out_shape=out_shape,
        scratch_shapes=[pltpu.VMEM((N_DEV,) + tuple(s), F32) for s in gshapes]
        + [pltpu.SemaphoreType.DMA((ng, N_DEV - 1)), pltpu.SemaphoreType.DMA((ng, N_DEV - 1))],
        compiler_params=pltpu.CompilerParams(has_side_effects=True, vmem_limit_bytes=V7X_VMEM_LIMIT), name=name)(*args)


_PERM = (0, 2, 1, 3)


def _cols_from_shards(g):
    return g.transpose(1, 0, 2).reshape(g.shape[1], N_CHIPS * g.shape[2])


def _rope_tables(positions):
    inv_freq = ROPE_THETA ** (-jnp.arange(0, HEAD_DIM, 2, dtype=F32) / HEAD_DIM)
    ang = positions.astype(F32).reshape(-1, 1) * inv_freq
    cos, sin = jnp.cos(ang), jnp.sin(ang)
    cos = jnp.concatenate([cos, cos, cos, cos], axis=-1)
    sin_s = jnp.concatenate([-sin, sin, -sin, sin], axis=-1)
    return cos, sin_s


def kernel(x, positions, norm_mix, norm_ffn, norm_final, mix_w_in, pool_w, pool_scale, attn_sinks, mix_w_out, ssm_w_in, ssm_conv_w, ssm_conv_b, ssm_dt_bias, ssm_A_log, ssm_D, ssm_norm, ssm_w_out, ffn_w_up, ffn_conv_w, ffn_conv_b, ffn_w_down, loss_target, m_norm_mix, m_norm_ffn, m_norm_final, m_mix_w_in, m_pool_w, m_pool_scale, m_attn_sinks, m_mix_w_out, m_ssm_w_in, m_ssm_conv_w, m_ssm_conv_b, m_ssm_dt_bias, m_ssm_A_log, m_ssm_D, m_ssm_norm, m_ssm_w_out, m_ffn_w_up, m_ffn_conv_w, m_ffn_conv_b, m_ffn_w_down, v_norm_mix, v_norm_ffn, v_norm_final, v_mix_w_in, v_pool_w, v_pool_scale, v_attn_sinks, v_mix_w_out, v_ssm_w_in, v_ssm_conv_w, v_ssm_conv_b, v_ssm_dt_bias, v_ssm_A_log, v_ssm_D, v_ssm_norm, v_ssm_w_out, v_ffn_w_up, v_ffn_conv_w, v_ffn_conv_b, v_ffn_w_down):
    W = dict(norm_mix=norm_mix, norm_ffn=norm_ffn, norm_final=norm_final, mix_w_in=mix_w_in, pool_w=pool_w, pool_scale=pool_scale, attn_sinks=attn_sinks, mix_w_out=mix_w_out, ssm_w_in=ssm_w_in, ssm_conv_w=ssm_conv_w, ssm_conv_b=ssm_conv_b, ssm_dt_bias=ssm_dt_bias, ssm_A_log=ssm_A_log, ssm_D=ssm_D, ssm_norm=ssm_norm, ssm_w_out=ssm_w_out, ffn_w_up=ffn_w_up, ffn_conv_w=ffn_conv_w, ffn_conv_b=ffn_conv_b, ffn_w_down=ffn_w_down)
    Mo = dict(norm_mix=m_norm_mix, norm_ffn=m_norm_ffn, norm_final=m_norm_final, mix_w_in=m_mix_w_in, pool_w=m_pool_w, pool_scale=m_pool_scale, attn_sinks=m_attn_sinks, mix_w_out=m_mix_w_out, ssm_w_in=m_ssm_w_in, ssm_conv_w=m_ssm_conv_w, ssm_conv_b=m_ssm_conv_b, ssm_dt_bias=m_ssm_dt_bias, ssm_A_log=m_ssm_A_log, ssm_D=m_ssm_D, ssm_norm=m_ssm_norm, ssm_w_out=m_ssm_w_out, ffn_w_up=m_ffn_w_up, ffn_conv_w=m_ffn_conv_w, ffn_conv_b=m_ffn_conv_b, ffn_w_down=m_ffn_w_down)
    Vo = dict(norm_mix=v_norm_mix, norm_ffn=v_norm_ffn, norm_final=v_norm_final, mix_w_in=v_mix_w_in, pool_w=v_pool_w, pool_scale=v_pool_scale, attn_sinks=v_attn_sinks, mix_w_out=v_mix_w_out, ssm_w_in=v_ssm_w_in, ssm_conv_w=v_ssm_conv_w, ssm_conv_b=v_ssm_conv_b, ssm_dt_bias=v_ssm_dt_bias, ssm_A_log=v_ssm_A_log, ssm_D=v_ssm_D, ssm_norm=v_ssm_norm, ssm_w_out=v_ssm_w_out, ffn_w_up=v_ffn_w_up, ffn_conv_w=v_ffn_conv_w, ffn_conv_b=v_ffn_conv_b, ffn_w_down=v_ffn_w_down)

    sh = [mix_w_in[0].astype(MXU), mix_w_out[0].astype(MXU), ssm_w_in[0].astype(MXU), ssm_w_out[0].astype(MXU),
          ffn_w_up.astype(MXU), ffn_w_down.astype(MXU), ssm_conv_w[0], ssm_conv_b, ssm_norm, ffn_conv_w]
    g_mi, g_mo, g_si, g_so, g_up, g_dn, g_scw, g_scb, g_sn, g_fcw = _gather_shards(sh, "gather_weights")
    wsi = _cols_from_shards(g_si)
    zx = SSM_D_INNER + SSM_CONV_DIM
    P = dict(
        nm=norm_mix, nf=norm_ffn, nfin=norm_final,
        wmi=_cols_from_shards(g_mi), wmo=g_mo.reshape(D_MODEL, D_MODEL),
        pool_w=pool_w[0], pool_scale=pool_scale, sinks=attn_sinks[0],
        wz=wsi[:, :SSM_D_INNER], wxbc=wsi[:, SSM_D_INNER:zx],
        wdt=jnp.pad(wsi[:, zx:], ((0, 0), (0, LANES - SSM_HEADS))),
        scw=_cols_from_shards(g_scw), scb=g_scb.reshape(1, SSM_CONV_DIM), snorm=g_sn.reshape(1, SSM_D_INNER),
        dt_bias=jnp.pad(ssm_dt_bias, ((0, 0), (0, LANES - SSM_HEADS))), a_log=jnp.pad(ssm_A_log, ((0, 0), (0, LANES - SSM_HEADS))),
        d_exp=jnp.repeat(ssm_D, SSM_D_INNER // SSM_HEADS, axis=1),
        wso=g_so.reshape(SSM_D_INNER, D_MODEL),
        wup=[jnp.concatenate([g_up[p, i] for p in _PERM], axis=1) for i in range(2)],
        fcw=[jnp.concatenate([g_fcw[p, i] for p in _PERM], axis=1) for i in range(2)],
        fcb=[jnp.concatenate([ffn_conv_b[i:i + 1, p * FFN_TC:(p + 1) * FFN_TC] for p in _PERM], axis=1) for i in range(2)],
        wdn=[g_dn[:, i].reshape(D_FF, D_MODEL) for i in range(2)],
    )
    cos, sin_s = _rope_tables(positions)
    loss_row, grad_x, big, small = _local_step(x[0], cos, sin_s, loss_target[0], P)

    names = ["mix_w_in", "mix_w_out", "ssm_w_in", "ssm_w_out", "ffn_w_up0", "ffn_w_up1", "ffn_w_down0", "ffn_w_down1"]
    gs = [big[nm] for nm in names]
    cidx = lax.axis_index("c").astype(jnp.int32).reshape(1)
    got = _pair_exchange(gs, "pair_exchange")
    ps = [_pair_sum(g, o, cidx, f"pair_sum_{nm}") for g, o, nm in zip(gs, got, names)]
    parts = _chip_exchange(ps, "chip_exchange")
    fs = [_chip_sum(p, f"chip_sum_{nm}") for p, nm in zip(parts, names)]
    halves = _half_exchange(fs, "half_exchange")
    red = {nm: h.reshape(2 * h.shape[1], h.shape[2]) for nm, h in zip(names, halves)}

    out = {}

    def big_update(pname, gparts):
        w = W[pname]
        lw = len(gparts)
        shp = w.shape
        w3, m3, v3 = (t.reshape((lw,) + gparts[0].shape) for t in (w, Mo[pname], Vo[pname]))
        res = _adamw(w3, m3, v3, gparts, f"adamw_{pname}")
        out[pname] = tuple(r.reshape(shp) for r in res)

    big_update("mix_w_in", [red["mix_w_in"]])
    big_update("mix_w_out", [red["mix_w_out"]])
    big_update("ssm_w_in", [red["ssm_w_in"]])
    big_update("ssm_w_out", [red["ssm_w_out"]])
    big_update("ffn_w_up", [red["ffn_w_up0"], red["ffn_w_up1"]])
    big_update("ffn_w_down", [red["ffn_w_down0"], red["ffn_w_down1"]])

    small_names = ["norm_mix", "norm_ffn", "norm_final", "pool_w", "pool_scale", "attn_sinks", "ssm_dt_bias", "ssm_A_log",
                   "ssm_D", "ffn_conv_b", "ssm_conv_w", "ssm_conv_b", "ssm_norm", "ffn_conv_w"]

    def as2d(t):
        if t.ndim == 1:
            return t.reshape(1, -1)
        return t.reshape(-1, t.shape[-1])

    items = [(small[nm], as2d(W[nm]), as2d(Mo[nm]), as2d(Vo[nm])) for nm in small_names]
    res = _small_reduce_adamw(items, loss_row, "small_reduce_adamw")
    for a, nm in enumerate(small_names):
        out[nm] = tuple(r.reshape(W[nm].shape) for r in res[4 * a:4 * a + 4])
    loss = res[-1][0, 0]

    order = ["norm_mix", "norm_ffn", "norm_final", "mix_w_in", "pool_w", "pool_scale", "attn_sinks", "mix_w_out", "ssm_w_in",
             "ssm_conv_w", "ssm_conv_b", "ssm_dt_bias", "ssm_A_log", "ssm_D", "ssm_norm", "ssm_w_out", "ffn_w_up", "ffn_conv_w",
             "ffn_conv_b", "ffn_w_down"]
    return (loss, grad_x.reshape(x.shape), *[out[nm][0] for nm in order], *[out[nm][1] for nm in order],
            *[out[nm][2] for nm in order], *[out[nm][3] for nm in order])
```

```python
import functools

import jax
import jax.numpy as jnp
from jax import lax
from jax.experimental import pallas as pl
from jax.experimental.pallas import tpu as pltpu

F32 = jnp.float32
BF16 = jnp.bfloat16
MXU = BF16
HI = lax.Precision.HIGHEST

D_MODEL = 1024
POOL_WINDOWS = (2, 4, 8, 16)
POOL_DIM = 512
POOL_GROUP = 128
HEAD_DIM = 64
N_HEADS = 8
N_KV_HEADS = 2
GQ = 4
Q_DIM = 512
KV_DIM = 128
BLOCK = 128
ROPE_THETA = 10000.0
MIX_IN_DIM = 1280
SSM_D_INNER = 2048
SSM_HEADS = 32
SSM_GROUPS = 8
SSM_STATE = 128
SSM_CONV = 4
SSM_CHUNK = 128
SSM_CONV_DIM = 4096
SSM_IN_DIM = 6176
D_FF = 2816
FFN_CONV = 3
NORM_EPS = 1e-6
SSM_NORM_EPS = 1e-5
ADAM_LR = 0.001
ADAM_B1 = 0.9
ADAM_B2 = 0.999
ADAM_EPS = 1e-08
ADAM_WD = 0.01
ADAM_STEP = 10

N_CHIPS = 4
N_DEV = 8
LANES = 128
SUBLANES = 8
V7X_VMEM_LIMIT = 56 * 1024 * 1024
NEG = -1e30
MESH = pl.DeviceIdType.MESH


def _cp(*sem):
    return pltpu.CompilerParams(dimension_semantics=sem if sem else None, vmem_limit_bytes=V7X_VMEM_LIMIT)


def _sds(shape, dtype=F32):
    return jax.ShapeDtypeStruct(tuple(shape), dtype)


def _iota(shape, dim):
    return lax.broadcasted_iota(jnp.int32, shape, dim)


def _silu(x):
    return x * (1.0 / (1.0 + jnp.exp(-x)))


def _dsilu(x):
    s = 1.0 / (1.0 + jnp.exp(-x))
    return s * (1.0 + x * (1.0 - s))


def _mm(a, b, *, ta=False, tb=False, tm, tn, tk, res=None, out_dtype=F32, out_shard_perm=None, name):
    M, K = (a.shape[1], a.shape[0]) if ta else a.shape
    N = b.shape[0] if tb else b.shape[1]
    tm, tn, tk = min(tm, M), min(tn, N), min(tk, K)
    gm, gn, gk = M // tm, N // tn, K // tk
    assert gm * tm == M and gn * tn == N and gk * tk == K, (name, M, N, K, tm, tn, tk)
    a_spec = pl.BlockSpec((tk, tm), lambda i, j, k: (k, i)) if ta else pl.BlockSpec((tm, tk), lambda i, j, k: (i, k))
    b_spec = pl.BlockSpec((tn, tk), lambda i, j, k: (j, k)) if tb else pl.BlockSpec((tk, tn), lambda i, j, k: (k, j))
    dims = (((0 if ta else 1,), (1 if tb else 0,)), ((), ()))
    has_res = res is not None

    def body(*refs):
        a_ref, b_ref = refs[0], refs[1]
        r_ref = refs[2] if has_res else None
        o_ref = refs[3] if has_res else refs[2]
        p = lax.dot_general(a_ref[...].astype(MXU), b_ref[...].astype(MXU), dims, preferred_element_type=F32)
        if gk == 1:
            if has_res:
                p = p + r_ref[...]
            o_ref[...] = p.astype(out_dtype)
        else:
            acc = refs[-1]
            k = pl.program_id(2)

            @pl.when(k == 0)
            def _():
                acc[...] = p

            @pl.when(k > 0)
            def _():
                acc[...] += p

            @pl.when(k == gk - 1)
            def _():
                r = acc[...]
                if has_res:
                    r = r + r_ref[...]
                o_ref[...] = r.astype(out_dtype)

    in_specs = [a_spec, b_spec]
    args = [a, b]
    if has_res:
        in_specs.append(pl.BlockSpec((tm, tn), lambda i, j, k: (i, j)))
        args.append(res)
    if out_shard_perm is None:
        out_spec = pl.BlockSpec((tm, tn), lambda i, j, k: (i, j))
        out_shape = _sds((M, N), out_dtype)
    else:
        assert gn == len(out_shard_perm) == 4 and tuple(out_shard_perm) == (0, 2, 1, 3)
        out_spec = pl.BlockSpec((None, tm, tn), lambda i, j, k: ((j % 2) * 2 + j // 2, i, 0))
        out_shape = _sds((gn, M, tn), out_dtype)
    return pl.pallas_call(
        body, grid=(gm, gn, gk), in_specs=in_specs, out_specs=out_spec, out_shape=out_shape,
        scratch_shapes=[pltpu.VMEM((tm, tn), F32)] if gk > 1 else [],
        compiler_params=_cp("parallel", "parallel", "arbitrary"), name=name)(*args)


def _rmsnorm_fwd(x, w, name):
    T, D = x.shape
    tm = min(T, 512)

    def body(x_ref, w_ref, o_ref):
        xv = x_ref[...]
        r = lax.rsqrt(jnp.mean(xv * xv, axis=-1, keepdims=True) + NORM_EPS)
        o_ref[...] = (xv * r * w_ref[...]).astype(o_ref.dtype)

    return pl.pallas_call(
        body, grid=(T // tm,),
        in_specs=[pl.BlockSpec((tm, D), lambda i: (i, 0)), pl.BlockSpec((1, D), lambda i: (0, 0))],
        out_specs=pl.BlockSpec((tm, D), lambda i: (i, 0)), out_shape=_sds((T, D), MXU),
        compiler_params=_cp("parallel"), name=name)(x, w.reshape(1, D))


def _rmsnorm_bwd(x, w, dh, dres, name):
    T, D = x.shape
    tm = min(T, 512)

    def body(x_ref, w_ref, dh_ref, dr_ref, dx_ref, dw_ref):
        xv = x_ref[...]
        r = lax.rsqrt(jnp.mean(xv * xv, axis=-1, keepdims=True) + NORM_EPS)
        xh = xv * r
        dh = dh_ref[...]
        g = dh * w_ref[...]
        dx_ref[...] = dr_ref[...] + r * (g - xh * jnp.mean(g * xh, axis=-1, keepdims=True))
        part = jnp.sum(dh * xh, axis=0, keepdims=True)

        @pl.when(pl.program_id(0) == 0)
        def _():
            dw_ref[...] = part

        @pl.when(pl.program_id(0) > 0)
        def _():
            dw_ref[...] += part

    row = pl.BlockSpec((tm, D), lambda i: (i, 0))
    vec = pl.BlockSpec((1, D), lambda i: (0, 0))
    return pl.pallas_call(
        body, grid=(T // tm,), in_specs=[row, vec, row, row], out_specs=[row, vec],
        out_shape=[_sds((T, D)), _sds((1, D))], compiler_params=_cp("arbitrary"), name=name)(x, w.reshape(1, D), dh, dres)


def _loss_head(x, w, target, name):
    T, D = x.shape
    tm = min(T, 512)

    def body(x_ref, w_ref, t_ref, loss_ref, dx_ref, dw_ref):
        xv = x_ref[...]
        r = lax.rsqrt(jnp.mean(xv * xv, axis=-1, keepdims=True) + NORM_EPS)
        xh = xv * r
        wv = w_ref[...]
        e = xh * wv - t_ref[...]
        lpart = 0.5 * jnp.sum(jnp.mean(e * e, axis=-1, keepdims=True), axis=0, keepdims=True)
        dy = e * (1.0 / D)
        g = dy * wv
        dx_ref[...] = r * (g - xh * jnp.mean(g * xh, axis=-1, keepdims=True))
        part = jnp.sum(dy * xh, axis=0, keepdims=True)
        lrow = jnp.broadcast_to(lpart, (1, LANES))

        @pl.when(pl.program_id(0) == 0)
        def _():
            dw_ref[...] = part
            loss_ref[...] = lrow

        @pl.when(pl.program_id(0) > 0)
        def _():
            dw_ref[...] += part
            loss_ref[...] += lrow

    row = pl.BlockSpec((tm, D), lambda i: (i, 0))
    vec = pl.BlockSpec((1, D), lambda i: (0, 0))
    return pl.pallas_call(
        body, grid=(T // tm,), in_specs=[row, vec, row],
        out_specs=[pl.BlockSpec((1, LANES), lambda i: (0, 0)), row, vec],
        out_shape=[_sds((1, LANES)), _sds((T, D)), _sds((1, D))],
        compiler_params=_cp("arbitrary"), name=name)(x, w.reshape(1, D), target)


def _shift_down(cur, prev8, s):
    if s == 0:
        return cur
    tm = cur.shape[0]
    rc = pltpu.roll(cur, s, 0)
    top = jnp.where(_iota((SUBLANES, cur.shape[1]), 0) < s, pltpu.roll(prev8, s, 0), rc[:SUBLANES])
    return jnp.concatenate([top, rc[SUBLANES:]], axis=0) if tm > SUBLANES else top


def _shift_up(cur, next8, s):
    if s == 0:
        return cur
    tm = cur.shape[0]
    rc = pltpu.roll(cur, tm - s, 0)
    bot = jnp.where(_iota((SUBLANES, cur.shape[1]), 0) >= SUBLANES - s, pltpu.roll(next8, SUBLANES - s, 0), rc[tm - SUBLANES:])
    return jnp.concatenate([rc[:tm - SUBLANES], bot], axis=0) if tm > SUBLANES else bot


def _conv_rows(cur, prev8, w, b, K):
    acc = cur * w[K - 1:K, :] + b
    for s in range(1, K):
        acc = acc + _shift_down(cur, prev8, s) * w[K - 1 - s:K - s, :]
    return acc


def _halo_specs(tm, tc, col_of):
    q = tm // SUBLANES

    def prev_map(i, j):
        return (jnp.maximum(i * q - 1, 0), col_of(j))

    def make_next(n_row_tiles):
        def next_map(i, j):
            return (jnp.minimum((i + 1) * q, n_row_tiles * q - 1), col_of(j))
        return next_map

    return (lambda: pl.BlockSpec((SUBLANES, tc), prev_map)), (lambda n: pl.BlockSpec((SUBLANES, tc), make_next(n)))


FFN_TC = 1408


def _ffn_mid_fwd(hid, cw, cb, name):
    T = hid.shape[0]
    tm = min(T, 256)
    nt, nj = T // tm, D_FF // FFN_TC
    K = FFN_CONV

    def body(h_ref, hp_ref, w_ref, b_ref, o_ref):
        i = pl.program_id(0)
        cur = h_ref[...]
        prev8 = jnp.where(i > 0, hp_ref[...], 0.0)
        hc = _conv_rows(cur, prev8, w_ref[...], b_ref[...], K)
        o_ref[...] = (_silu(hc[:, FFN_TC:]) * hc[:, :FFN_TC]).astype(o_ref.dtype)

    mk_prev, _ = _halo_specs(tm, 2 * FFN_TC, lambda j: j)
    return pl.pallas_call(
        body, grid=(nt, nj),
        in_specs=[pl.BlockSpec((tm, 2 * FFN_TC), lambda i, j: (i, j)), mk_prev(),
                  pl.BlockSpec((K, 2 * FFN_TC), lambda i, j: (0, j)), pl.BlockSpec((1, 2 * FFN_TC), lambda i, j: (0, j))],
        out_specs=pl.BlockSpec((tm, FFN_TC), lambda i, j: (i, j)), out_shape=_sds((T, D_FF), MXU),
        compiler_params=_cp("parallel", "parallel"), name=name)(hid, hid, cw, cb)


def _ffn_mid_bwd(hid, cw, cb, da, name):
    T = hid.shape[0]
    tm = min(T, 256)
    nt, nj = T // tm, D_FF // FFN_TC
    K = FFN_CONV
    W2 = 2 * FFN_TC

    def body(h_ref, hp_ref, hn_ref, da_ref, dan_ref, w_ref, b_ref, dh_ref, dw_ref, db_ref):
        i = pl.program_id(1)
        w = w_ref[...]
        b = b_ref[...]
        cur = h_ref[...]
        prev8 = jnp.where(i > 0, hp_ref[...], 0.0)
        nxt8 = hn_ref[...]
        last = i == nt - 1

        def dpre(hc, dav):
            u, g = hc[:, :FFN_TC], hc[:, FFN_TC:]
            return jnp.concatenate([dav * _silu(g), dav * u * _dsilu(g)], axis=1)

        hc = _conv_rows(cur, prev8, w, b, K)
        d_cur = dpre(hc, da_ref[...])
        hc_n = _conv_rows(nxt8, cur[tm - SUBLANES:], w, b, K)
        d_nxt = jnp.where(last, 0.0, dpre(hc_n, dan_ref[...]))
        dh = d_cur * w[K - 1:K, :]
        for s in range(1, K):
            dh = dh + _shift_up(d_cur, d_nxt, s) * w[K - 1 - s:K - s, :]
        dh_ref[...] = dh.astype(dh_ref.dtype)
        rows = [jnp.sum(d_cur * _shift_down(cur, prev8, K - 1 - k), axis=0, keepdims=True) for k in range(K)]
        dwp = jnp.concatenate(rows, axis=0)
        dbp = jnp.sum(d_cur, axis=0, keepdims=True)

        @pl.when(i == 0)
        def _():
            dw_ref[...] = dwp
            db_ref[...] = dbp

        @pl.when(i > 0)
        def _():
            dw_ref[...] += dwp
            db_ref[...] += dbp

    q = tm // SUBLANES
    blk = pl.BlockSpec((tm, W2), lambda j, i: (i, j))
    prv = pl.BlockSpec((SUBLANES, W2), lambda j, i: (jnp.maximum(i * q - 1, 0), j))
    nxt = pl.BlockSpec((SUBLANES, W2), lambda j, i: (jnp.minimum((i + 1) * q, nt * q - 1), j))
    dab = pl.BlockSpec((tm, FFN_TC), lambda j, i: (i, j))
    dan = pl.BlockSpec((SUBLANES, FFN_TC), lambda j, i: (jnp.minimum((i + 1) * q, nt * q - 1), j))
    return pl.pallas_call(
        body, grid=(nj, nt),
        in_specs=[blk, prv, nxt, dab, dan, pl.BlockSpec((K, W2), lambda j, i: (0, j)), pl.BlockSpec((1, W2), lambda j, i: (0, j))],
        out_specs=[blk, pl.BlockSpec((K, W2), lambda j, i: (0, j)), pl.BlockSpec((1, W2), lambda j, i: (0, j))],
        out_shape=[_sds((T, 2 * D_FF), MXU), _sds((K, 2 * D_FF)), _sds((1, 2 * D_FF))],
        compiler_params=_cp("parallel", "arbitrary"), name=name)(hid, hid, hid, da, da, cw, cb)


def _rope(t, cos, sin_s, inverse=False):
    n = t.shape[1] // LANES
    c = jnp.concatenate([cos] * n, axis=1) if n > 1 else cos
    s = jnp.concatenate([sin_s] * n, axis=1) if n > 1 else sin_s
    a = pltpu.roll(t, HEAD_DIM // 2, 1)
    b = pltpu.roll(t, t.shape[1] - HEAD_DIM // 2, 1)
    first = (_iota(t.shape, 1) % HEAD_DIM) < HEAD_DIM // 2
    rot = jnp.where(first, b, a) * s
    return t * c - rot if inverse else t * c + rot


def _stack_heads(t, g):
    return jnp.concatenate([t[:, (GQ * g + r) * HEAD_DIM:(GQ * g + r + 1) * HEAD_DIM] for r in range(GQ)], axis=0)


def _stack_cols(t, g):
    return jnp.concatenate([t[:, GQ * g + r:GQ * g + r + 1] for r in range(GQ)], axis=0)


def _pool_sums(prev, cur, w):
    s = jnp.concatenate([prev, cur], axis=0)
    sh = 1
    while sh < w:
        s = s + pltpu.roll(s, sh, 0)
        sh *= 2
    return s[BLOCK:]


def _nt(a, b):
    return lax.dot_general(a.astype(MXU), b.astype(MXU), (((1,), (1,)), ((), ())), preferred_element_type=F32)


def _tn(a, b):
    return lax.dot_general(a.astype(MXU), b.astype(MXU), (((0,), (0,)), ((), ())), preferred_element_type=F32)


def _nn(a, b):
    return jnp.dot(a.astype(MXU), b.astype(MXU), preferred_element_type=F32)


def _mixcore_fwd(proj, cos, sin_s, pool_w, pool_scale, sinks, name):
    T = proj.shape[0]
    nb = T // BLOCK
    scale = HEAD_DIM ** -0.5

    def body(p_ref, pp_ref, c_ref, s_ref, cp_ref, sp_ref, pw_ref, ps_ref, sk_ref, cat_ref, at_ref, lse_ref):
        i = pl.program_id(0)
        has_prev = i > 0
        cur = p_ref[...]
        prv = jnp.where(has_prev, pp_ref[...], 0.0)
        tpos = (i * BLOCK + _iota((BLOCK, 1), 0) + 1).astype(F32)
        for g, w in enumerate(POOL_WINDOWS):
            sl = slice(g * POOL_GROUP, (g + 1) * POOL_GROUP)
            pooled = _pool_sums(prv[:, sl], cur[:, sl], w) / jnp.minimum(tpos, float(w)) - cur[:, sl]
            cat_ref[:, sl] = (_nn(pooled, pw_ref[g]) * ps_ref[:, sl]).astype(cat_ref.dtype)
        q = _rope(cur[:, POOL_DIM:POOL_DIM + Q_DIM], c_ref[...], s_ref[...])
        kc = _rope(cur[:, POOL_DIM + Q_DIM:POOL_DIM + Q_DIM + KV_DIM], c_ref[...], s_ref[...])
        kp = _rope(prv[:, POOL_DIM + Q_DIM:POOL_DIM + Q_DIM + KV_DIM], cp_ref[...], sp_ref[...])
        vc = cur[:, POOL_DIM + Q_DIM + KV_DIM:]
        vp = prv[:, POOL_DIM + Q_DIM + KV_DIM:]
        ri = _iota((GQ * BLOCK, BLOCK), 0) % BLOCK
        cj = _iota((GQ * BLOCK, BLOCK), 1)
        mc = cj <= ri
        mp = jnp.logical_and(cj > ri, has_prev)
        outs, lses = [], []
        for g in range(N_KV_HEADS):
            hs = slice(g * HEAD_DIM, (g + 1) * HEAD_DIM)
            qg = _stack_heads(q, g) * scale
            sc = jnp.where(mc, _nt(qg, kc[:, hs]), NEG)
            sp = jnp.where(mp, _nt(qg, kp[:, hs]), NEG)
            sink = jnp.concatenate([jnp.full((BLOCK, 1), sk_ref[GQ * g + r], F32) for r in range(GQ)], axis=0)
            m = jnp.maximum(jnp.maximum(jnp.max(sc, axis=1, keepdims=True), jnp.max(sp, axis=1, keepdims=True)), sink)
            pc = jnp.exp(sc - m)
            pp = jnp.exp(sp - m)
            den = jnp.sum(pc, axis=1, keepdims=True) + jnp.sum(pp, axis=1, keepdims=True) + jnp.exp(sink - m)
            o = (_nn(pc, vc[:, hs]) + _nn(pp, vp[:, hs])) / den
            lse = m + jnp.log(den)
            for r in range(GQ):
                outs.append(o[r * BLOCK:(r + 1) * BLOCK])
                lses.append(lse[r * BLOCK:(r + 1) * BLOCK])
        attn = jnp.concatenate(outs, axis=1)
        at_ref[...] = attn
        cat_ref[:, POOL_DIM:] = attn.astype(cat_ref.dtype)
        lane = _iota((BLOCK, LANES), 1)
        lrow = jnp.zeros((BLOCK, LANES), F32)
        for h in range(N_HEADS):
            lrow = jnp.where(lane == h, lses[h], lrow)
        lse_ref[...] = lrow

    cur = lambda w: pl.BlockSpec((BLOCK, w), lambda i: (i, 0))
    prv = lambda w: pl.BlockSpec((BLOCK, w), lambda i: (jnp.maximum(i - 1, 0), 0))
    return pl.pallas_call(
        body, grid=(nb,),
        in_specs=[cur(MIX_IN_DIM), prv(MIX_IN_DIM), cur(LANES), cur(LANES), prv(LANES), prv(LANES),
                  pl.BlockSpec((4, POOL_GROUP, POOL_GROUP), lambda i: (0, 0, 0)), pl.BlockSpec((1, POOL_DIM), lambda i: (0, 0)),
                  pl.BlockSpec(memory_space=pltpu.SMEM)],
        out_specs=[cur(2 * POOL_DIM), cur(Q_DIM), cur(LANES)],
        out_shape=[_sds((T, 2 * POOL_DIM), MXU), _sds((T, Q_DIM)), _sds((T, LANES))],
        compiler_params=_cp("parallel"), name=name)(proj, proj, cos, sin_s, cos, sin_s, pool_w, pool_scale, sinks)


def _mixcore_bwd(proj, cos, sin_s, pool_w, pool_scale, sinks, attn, lse, dcat, name):
    T = proj.shape[0]
    nb = T // BLOCK
    scale = HEAD_DIM ** -0.5
    QO, KO, VO = POOL_DIM, POOL_DIM + Q_DIM, POOL_DIM + Q_DIM + KV_DIM

    def body(p_ref, pp_ref, pn_ref, c_ref, s_ref, cp_ref, sp_ref, cn_ref, sn_ref, pw_ref, ps_ref, sk_ref,
             at_ref, atn_ref, l_ref, ln_ref, d_ref, dn_ref, dp_ref, dpw_ref, dps_ref, dsk_ref):
        i = pl.program_id(0)
        has_prev = i > 0
        has_next = i < nb - 1
        cur = p_ref[...]
        prv = jnp.where(has_prev, pp_ref[...], 0.0)
        d_cur = d_ref[...]
        d_nxt = jnp.where(has_next, dn_ref[...], 0.0)

        tpos = (i * BLOCK + _iota((BLOCK, 1), 0) + 1).astype(F32)
        tpos2 = (i * BLOCK + _iota((2 * BLOCK, 1), 0) + 1).astype(F32)
        ps = ps_ref[...]
        dps_parts, dpw_parts = [], []
        for g, w in enumerate(POOL_WINDOWS):
            sl = slice(g * POOL_GROUP, (g + 1) * POOL_GROUP)
            pooled = _pool_sums(prv[:, sl], cur[:, sl], w) / jnp.minimum(tpos, float(w)) - cur[:, sl]
            mixed = _nn(pooled, pw_ref[g])
            dps_parts.append(jnp.sum(d_cur[:, sl] * mixed, axis=0, keepdims=True))
            dm2 = jnp.concatenate([d_cur[:, sl], d_nxt[:, sl]], axis=0) * ps[:, sl]
            dpw_parts.append(_tn(pooled, dm2[:BLOCK]))
            dpool2 = _nt(dm2, pw_ref[g])
            e = dpool2 / jnp.minimum(tpos2, float(w))
            sh = 1
            while sh < w:
                e = e + pltpu.roll(e, 2 * BLOCK - sh, 0)
                sh *= 2
            dp_ref[:, sl] = (e[:BLOCK] - dpool2[:BLOCK]).astype(dp_ref.dtype)
        dpsp = jnp.concatenate(dps_parts, axis=1)

        nxt = pn_ref[...]
        q = _rope(cur[:, QO:KO], c_ref[...], s_ref[...])
        qn = _rope(nxt[:, QO:KO], cn_ref[...], sn_ref[...])
        kc = _rope(cur[:, KO:VO], c_ref[...], s_ref[...])
        kp = _rope(prv[:, KO:VO], cp_ref[...], sp_ref[...])
        vc, vp = cur[:, VO:], prv[:, VO:]
        do, don = d_cur[:, POOL_DIM:], d_nxt[:, POOL_DIM:]
        dl = do * at_ref[...]
        dln = don * atn_ref[...]
        lse, lsen = l_ref[...], ln_ref[...]
        ri = _iota((GQ * BLOCK, BLOCK), 0) % BLOCK
        cj = _iota((GQ * BLOCK, BLOCK), 1)
        mc = cj <= ri
        mp = jnp.logical_and(cj > ri, has_prev)
        mn = jnp.logical_and(cj > ri, has_next)
        dq_parts, dk_parts, dv_parts, dsk_vals = [], [], [], []
        for g in range(N_KV_HEADS):
            hs = slice(g * HEAD_DIM, (g + 1) * HEAD_DIM)
            qg, qng = _stack_heads(q, g) * scale, _stack_heads(qn, g) * scale
            dog, dong = _stack_heads(do, g), _stack_heads(don, g)
            delta = jnp.sum(_stack_heads(dl, g), axis=1, keepdims=True)
            deltan = jnp.sum(_stack_heads(dln, g), axis=1, keepdims=True)
            lg, lng = _stack_cols(lse, g), _stack_cols(lsen, g)
            pc = jnp.where(mc, jnp.exp(_nt(qg, kc[:, hs]) - lg), 0.0)
            pp = jnp.where(mp, jnp.exp(_nt(qg, kp[:, hs]) - lg), 0.0)
            pn = jnp.where(mn, jnp.exp(_nt(qng, kc[:, hs]) - lng), 0.0)
            dsc = pc * (_nt(dog, vc[:, hs]) - delta)
            dsp = pp * (_nt(dog, vp[:, hs]) - delta)
            dsn = pn * (_nt(dong, vc[:, hs]) - deltan)
            dqg = (_nn(dsc, kc[:, hs]) + _nn(dsp, kp[:, hs])) * scale
            dq_parts += [dqg[r * BLOCK:(r + 1) * BLOCK] for r in range(GQ)]
            dk_parts.append(_tn(dsc, qg) + _tn(dsn, qng))
            dv_parts.append(_tn(pc, dog) + _tn(pn, dong))
            sink = jnp.concatenate([jnp.full((BLOCK, 1), sk_ref[GQ * g + r], F32) for r in range(GQ)], axis=0)
            dsk = -jnp.exp(sink - lg) * delta
            dsk_vals += [jnp.sum(dsk[r * BLOCK:(r + 1) * BLOCK], axis=0, keepdims=True) for r in range(GQ)]
        dq = _rope(jnp.concatenate(dq_parts, axis=1), c_ref[...], s_ref[...], inverse=True)
        dk = _rope(jnp.concatenate(dk_parts, axis=1), c_ref[...], s_ref[...], inverse=True)
        dp_ref[:, QO:KO] = dq.astype(dp_ref.dtype)
        dp_ref[:, KO:VO] = dk.astype(dp_ref.dtype)
        dp_ref[:, VO:] = jnp.concatenate(dv_parts, axis=1).astype(dp_ref.dtype)
        lane = _iota((1, LANES), 1)
        dskp = jnp.zeros((1, LANES), F32)
        for h in range(N_HEADS):
            dskp = jnp.where(lane == h, dsk_vals[h], dskp)

        @pl.when(i == 0)
        def _():
            dps_ref[...] = dpsp
            dsk_ref[...] = dskp
            for g in range(4):
                dpw_ref[g] = dpw_parts[g]

        @pl.when(i > 0)
        def _():
            dps_ref[...] += dpsp
            dsk_ref[...] += dskp
            for g in range(4):
                dpw_ref[g] += dpw_parts[g]

    cur = lambda w: pl.BlockSpec((BLOCK, w), lambda i: (i, 0))
    prv = lambda w: pl.BlockSpec((BLOCK, w), lambda i: (jnp.maximum(i - 1, 0), 0))
    nxt = lambda w: pl.BlockSpec((BLOCK, w), lambda i: (jnp.minimum(i + 1, nb - 1), 0))
    return pl.pallas_call(
        body, grid=(nb,),
        in_specs=[cur(MIX_IN_DIM), prv(MIX_IN_DIM), nxt(MIX_IN_DIM),
                  cur(LANES), cur(LANES), prv(LANES), prv(LANES), nxt(LANES), nxt(LANES),
                  pl.BlockSpec((4, POOL_GROUP, POOL_GROUP), lambda i: (0, 0, 0)), pl.BlockSpec((1, POOL_DIM), lambda i: (0, 0)),
                  pl.BlockSpec(memory_space=pltpu.SMEM),
                  cur(Q_DIM), nxt(Q_DIM), cur(LANES), nxt(LANES), cur(2 * POOL_DIM), nxt(2 * POOL_DIM)],
        out_specs=[cur(MIX_IN_DIM), pl.BlockSpec((4, POOL_GROUP, POOL_GROUP), lambda i: (0, 0, 0)),
                   pl.BlockSpec((1, POOL_DIM), lambda i: (0, 0)), pl.BlockSpec((1, LANES), lambda i: (0, 0))],
        out_shape=[_sds((T, MIX_IN_DIM), MXU), _sds((4, POOL_GROUP, POOL_GROUP)), _sds((1, POOL_DIM)), _sds((1, LANES))],
        compiler_params=_cp("arbitrary"), name=name)(
            proj, proj, proj, cos, sin_s, cos, sin_s, cos, sin_s, pool_w, pool_scale, sinks, attn, attn, lse, lse, dcat, dcat)


SSM_TC = 128
GROUP_W = SSM_D_INNER // SSM_GROUPS
PERM_W = GROUP_W + 2 * SSM_STATE


def _perm_col(n):
    nx = SSM_D_INNER // SSM_TC
    nbt = SSM_GROUPS
    x_idx = (n // 2) * 4 + n % 2
    b_idx = (n - nx) * 4 + 2
    c_idx = (n - nx - nbt) * 4 + 3
    return jnp.where(n < nx, x_idx, jnp.where(n < nx + nbt, b_idx, c_idx))


def _ssm_pre_fwd(xbc, cw, cb, name):
    T = xbc.shape[0]
    tm = min(T, 1024)
    K = SSM_CONV
    q = tm // SUBLANES

    def body(x_ref, xp_ref, w_ref, b_ref, o_ref):
        prev8 = jnp.where(pl.program_id(0) > 0, xp_ref[...], 0.0)
        o_ref[...] = _silu(_conv_rows(x_ref[...], prev8, w_ref[...], b_ref[...], K))

    tc = 512
    return pl.pallas_call(
        body, grid=(T // tm, SSM_CONV_DIM // tc),
        in_specs=[pl.BlockSpec((tm, tc), lambda i, j: (i, j)),
                  pl.BlockSpec((SUBLANES, tc), lambda i, j: (jnp.maximum(i * q - 1, 0), j)),
                  pl.BlockSpec((K, tc), lambda i, j: (0, j)), pl.BlockSpec((1, tc), lambda i, j: (0, j))],
        out_specs=pl.BlockSpec((tm, tc), lambda i, j: (i, j)), out_shape=_sds((T, SSM_CONV_DIM)),
        compiler_params=_cp("parallel", "parallel"), name=name)(xbc, xbc, cw, cb)


def _ssm_pre_bwd(xbc, cw, cb, dact_perm, name):
    T = xbc.shape[0]
    tm = min(T, 1024)
    nt = T // tm
    K = SSM_CONV
    q = tm // SUBLANES
    tc = SSM_TC

    def body(x_ref, xp_ref, xn_ref, d_ref, dn_ref, w_ref, b_ref, dx_ref, dw_ref, db_ref):
        i = pl.program_id(1)
        w = w_ref[...]
        b = b_ref[...]
        cur = x_ref[...]
        prev8 = jnp.where(i > 0, xp_ref[...], 0.0)
        nxt8 = xn_ref[...]
        d_cur = d_ref[...] * _dsilu(_conv_rows(cur, prev8, w, b, K))
        d_nxt = jnp.where(i == nt - 1, 0.0, dn_ref[...] * _dsilu(_conv_rows(nxt8, cur[tm - SUBLANES:], w, b, K)))
        dx = d_cur * w[K - 1:K, :]
        for s in range(1, K):
            dx = dx + _shift_up(d_cur, d_nxt, s) * w[K - 1 - s:K - s, :]
        dx_ref[...] = dx.astype(dx_ref.dtype)
        dwp = jnp.concatenate([jnp.sum(d_cur * _shift_down(cur, prev8, K - 1 - k), axis=0, keepdims=True) for k in range(K)], axis=0)
        dbp = jnp.sum(d_cur, axis=0, keepdims=True)

        @pl.when(i == 0)
        def _():
            dw_ref[...] = dwp
            db_ref[...] = dbp

        @pl.when(i > 0)
        def _():
            dw_ref[...] += dwp
            db_ref[...] += dbp

    nxt_row = lambda i: jnp.minimum((i + 1) * q, nt * q - 1)
    return pl.pallas_call(
        body, grid=(SSM_CONV_DIM // tc, nt),
        in_specs=[pl.BlockSpec((tm, tc), lambda j, i: (i, j)),
                  pl.BlockSpec((SUBLANES, tc), lambda j, i: (jnp.maximum(i * q - 1, 0), j)),
                  pl.BlockSpec((SUBLANES, tc), lambda j, i: (nxt_row(i), j)),
                  pl.BlockSpec((tm, tc), lambda j, i: (i, _perm_col(j))),
                  pl.BlockSpec((SUBLANES, tc), lambda j, i: (nxt_row(i), _perm_col(j))),
                  pl.BlockSpec((K, tc), lambda j, i: (0, j)), pl.BlockSpec((1, tc), lambda j, i: (0, j))],
        out_specs=[pl.BlockSpec((tm, tc), lambda j, i: (i, j)), pl.BlockSpec((K, tc), lambda j, i: (0, j)),
                   pl.BlockSpec((1, tc), lambda j, i: (0, j))],
        out_shape=[_sds((T, SSM_CONV_DIM), MXU), _sds((K, SSM_CONV_DIM)), _sds((1, SSM_CONV_DIM))],
        compiler_params=_cp("parallel", "arbitrary"), name=name)(xbc, xbc, xbc, dact_perm, dact_perm, cw, cb)


def _dot_hi(a, b):
    return jnp.dot(a, b, precision=HI, preferred_element_type=F32)


def _ssd_common(g, dtraw, bias, alog):
    L = SSM_CHUNK
    xb = dtraw + bias
    dt = jnp.maximum(xb, 0.0) + jnp.log1p(jnp.exp(-jnp.abs(xb)))
    A = -jnp.exp(alog)
    a = dt * A
    tril = (_iota((L, L), 1) <= _iota((L, L), 0)).astype(F32)
    acs = _dot_hi(tril, a)
    head_of_lane = 4 * g + _iota((LANES, GROUP_W), 1) // HEAD_DIM
    esel = (_iota((LANES, GROUP_W), 0) == head_of_lane).astype(F32)
    eselT = (_iota((GROUP_W, LANES), 1) == 4 * g + _iota((GROUP_W, LANES), 0) // HEAD_DIM).astype(F32)
    return xb, dt, A, a, tril, acs, esel, eselT


def _ssd_fwd(xact, dtraw, dt_bias, a_log, name):
    T = xact.shape[0]
    nc = T // SSM_CHUNK
    L = SSM_CHUNK
    nxg = SSM_D_INNER // GROUP_W

    def body(x_ref, b_ref, c_ref, dt_ref, bias_ref, al_ref, y_ref, st_ref, state):
        c, g = pl.program_id(0), pl.program_id(1)

        @pl.when(c == 0)
        def _():
            state[g] = jnp.zeros((GROUP_W, SSM_STATE), F32)

        _, dt, A, a, tril, acs, esel, eselT = _ssd_common(g, dt_ref[...], bias_ref[...], al_ref[...])
        x, B, C = x_ref[...], b_ref[...], c_ref[...]
        dtx = _dot_hi(dt, esel)
        acs_x = _dot_hi(acs, esel)
        acs_xT = lax.dot_general(eselT, acs, (((1,), (1,)), ((), ())), precision=HI, preferred_element_type=F32)
        X = x * dtx
        CB = _nt(C, B)
        yd = []
        for r in range(GQ):
            col = acs_x[:, r * HEAD_DIM:r * HEAD_DIM + 1]
            row = acs_xT[r * HEAD_DIM:r * HEAD_DIM + 1, :]
            Lm = jnp.exp(jnp.where(tril > 0, col - row, NEG))
            yd.append(_nn(CB * Lm, X[:, r * HEAD_DIM:(r + 1) * HEAD_DIM]))
        S = state[g]
        st_ref[...] = S
        yo = _nt(C, S) * jnp.exp(acs_x)
        y_ref[...] = jnp.concatenate(yd, axis=1) + yo
        last = acs_x[L - 1:L, :]
        contrib = _tn(X * jnp.exp(last - acs_x), B)
        state[g] = S * jnp.exp(acs_xT[:, L - 1:L]) + contrib

    return pl.pallas_call(
        body, grid=(nc, SSM_GROUPS),
        in_specs=[pl.BlockSpec((L, GROUP_W), lambda c, g: (c, g)),
                  pl.BlockSpec((L, SSM_STATE), lambda c, g: (c, 2 * nxg + g)),
                  pl.BlockSpec((L, SSM_STATE), lambda c, g: (c, 3 * nxg + g)),
                  pl.BlockSpec((L, LANES), lambda c, g: (c, 0)),
                  pl.BlockSpec((1, LANES), lambda c, g: (0, 0)), pl.BlockSpec((1, LANES), lambda c, g: (0, 0))],
        out_specs=[pl.BlockSpec((L, GROUP_W), lambda c, g: (c, g)),
                   pl.BlockSpec((None, None, GROUP_W, SSM_STATE), lambda c, g: (c, g, 0, 0))],
        out_shape=[_sds((T, SSM_D_INNER)), _sds((nc, SSM_GROUPS, GROUP_W, SSM_STATE))],
        scratch_shapes=[pltpu.VMEM((SSM_GROUPS, GROUP_W, SSM_STATE), F32)],
        compiler_params=_cp("arbitrary", "arbitrary"), name=name)(xact, xact, xact, dtraw, dt_bias, a_log)


def _ssd_bwd(xact, dtraw, dt_bias, a_log, d_skip, states, dy, name):
    T = xact.shape[0]
    nc = T // SSM_CHUNK
    L = SSM_CHUNK
    nxg = SSM_D_INNER // GROUP_W

    def body(x_ref, b_ref, c_ref, dt_ref, bias_ref, al_ref, dsk_ref, st_ref, dy_ref,
             dxp_ref, ddt_ref, dbias_ref, dal_ref, dd_ref, dstate):
        cc, g = pl.program_id(0), pl.program_id(1)

        @pl.when(cc == 0)
        def _():
            dstate[g] = jnp.zeros((GROUP_W, SSM_STATE), F32)

        xb, dt, A, a, tril, acs, esel, eselT = _ssd_common(g, dt_ref[...], bias_ref[...], al_ref[...])
        x, B, C, dY = x_ref[...], b_ref[...], c_ref[...], dy_ref[...]
        dtx = _dot_hi(dt, esel)
        acs_x = _dot_hi(acs, esel)
        acs_xT = lax.dot_general(eselT, acs, (((1,), (1,)), ((), ())), precision=HI, preferred_element_type=F32)
        X = x * dtx
        CB = _nt(C, B)
        S = st_ref[...]
        dS_out = dstate[g]
        lane = _iota((L, LANES), 1)
        dacs = jnp.zeros((L, LANES), F32)
        dcb_sum = jnp.zeros((L, L), F32)
        dX_parts = []
        for r in range(GQ):
            hs = slice(r * HEAD_DIM, (r + 1) * HEAD_DIM)
            col = acs_x[:, r * HEAD_DIM:r * HEAD_DIM + 1]
            row = acs_xT[r * HEAD_DIM:r * HEAD_DIM + 1, :]
            Lm = jnp.exp(jnp.where(tril > 0, col - row, NEG))
            M = CB * Lm
            dM = _nt(dY[:, hs], X[:, hs])
            dX_parts.append(_tn(M, dY[:, hs]))
            Wm = dM * M
            dcb_sum = dcb_sum + dM * Lm
            dacs = dacs + jnp.where(lane == 4 * g + r, jnp.sum(Wm - Wm.T, axis=1, keepdims=True), 0.0)
        E_x = jnp.exp(acs_x)
        G = _nt(C, S)
        dG = dY * E_x
        E = jnp.exp(acs)
        dacs = dacs + _dot_hi(dY * G, eselT) * E
        last_x = acs_x[L - 1:L, :]
        dec_x = jnp.exp(last_x - acs_x)
        dDX = _nt(B, dS_out)
        last = acs[L - 1:L, :]
        dec = jnp.exp(last - acs)
        ddec = _dot_hi(dDX * X, eselT) * dec
        dacs = dacs - ddec
        zrow = jnp.broadcast_to(jnp.sum(dS_out * S, axis=1, keepdims=True), (GROUP_W, LANES))
        ztot = lax.dot_general(zrow, eselT, (((0,), (0,)), ((), ())), precision=HI, preferred_element_type=F32)[0:1, :]
        dlast = jnp.sum(ddec, axis=0, keepdims=True) + jnp.exp(last) * ztot
        dacs = dacs + jnp.where(_iota((L, LANES), 0) == L - 1, dlast, 0.0)
        triu = (_iota((L, L), 0) <= _iota((L, L), 1)).astype(F32)
        da = _dot_hi(triu, dacs)
        dX = jnp.concatenate(dX_parts, axis=1) + dec_x * dDX
        dsk = dsk_ref[...]
        ddt = da * A + _dot_hi(dX * x, eselT)
        sig = 1.0 / (1.0 + jnp.exp(-xb))
        ddtraw = ddt * sig
        dal = jnp.sum(da * dt, axis=0, keepdims=True) * A
        ddp = jnp.sum(_dot_hi(dY * x, eselT), axis=0, keepdims=True)
        dbp = jnp.sum(ddtraw, axis=0, keepdims=True)
        dxp_ref[:, :GROUP_W] = dX * dtx + dY * dsk
        dxp_ref[:, GROUP_W:GROUP_W + SSM_STATE] = _tn(dcb_sum, C) + _nn(X * dec_x, dS_out)
        dxp_ref[:, GROUP_W + SSM_STATE:] = _nn(dcb_sum, B) + _nn(dG, S)
        dstate[g] = dS_out * jnp.exp(acs_xT[:, L - 1:L]) + _tn(dG, C)

        @pl.when(g == 0)
        def _():
            ddt_ref[...] = ddtraw

        @pl.when(g > 0)
        def _():
            ddt_ref[...] += ddtraw

        first = jnp.logical_and(cc == 0, g == 0)

        @pl.when(first)
        def _():
            dbias_ref[...] = dbp
            dal_ref[...] = dal
            dd_ref[...] = ddp

        @pl.when(jnp.logical_not(first))
        def _():
            dbias_ref[...] += dbp
            dal_ref[...] += dal
            dd_ref[...] += ddp

    rc = lambda c: nc - 1 - c
    vec = pl.BlockSpec((1, LANES), lambda c, g: (0, 0))
    return pl.pallas_call(
        body, grid=(nc, SSM_GROUPS),
        in_specs=[pl.BlockSpec((L, GROUP_W), lambda c, g: (rc(c), g)),
                  pl.BlockSpec((L, SSM_STATE), lambda c, g: (rc(c), 2 * nxg + g)),
                  pl.BlockSpec((L, SSM_STATE), lambda c, g: (rc(c), 3 * nxg + g)),
                  pl.BlockSpec((L, LANES), lambda c, g: (rc(c), 0)), vec, vec,
                  pl.BlockSpec((1, GROUP_W), lambda c, g: (0, g)),
                  pl.BlockSpec((None, None, GROUP_W, SSM_STATE), lambda c, g: (rc(c), g, 0, 0)),
                  pl.BlockSpec((L, GROUP_W), lambda c, g: (rc(c), g))],
        out_specs=[pl.BlockSpec((L, PERM_W), lambda c, g: (rc(c), g)),
                   pl.BlockSpec((L, LANES), lambda c, g: (rc(c), 0)), vec, vec, vec],
        out_shape=[_sds((T, SSM_GROUPS * PERM_W)), _sds((T, LANES)), _sds((1, LANES)), _sds((1, LANES)), _sds((1, LANES))],
        scratch_shapes=[pltpu.VMEM((SSM_GROUPS, GROUP_W, SSM_STATE), F32)],
        compiler_params=_cp("arbitrary", "arbitrary"), name=name)(xact, xact, xact, dtraw, dt_bias, a_log, d_skip, states, dy)


def _ssm_post_fwd(y, xact, z, d_skip, nw, name):
    T = y.shape[0]
    tm = min(T, 256)
    W = SSM_D_INNER

    def body(y_ref, x_ref, z_ref, d_ref, w_ref, o_ref):
        y2 = (y_ref[...] + d_ref[...] * x_ref[...]) * _silu(z_ref[...])
        r = lax.rsqrt(jnp.mean(y2 * y2, axis=-1, keepdims=True) + SSM_NORM_EPS)
        o_ref[...] = (y2 * r * w_ref[...]).astype(o_ref.dtype)

    row = pl.BlockSpec((tm, W), lambda i: (i, 0))
    vec = pl.BlockSpec((1, W), lambda i: (0, 0))
    return pl.pallas_call(
        body, grid=(T // tm,), in_specs=[row, row, row, vec, vec], out_specs=row, out_shape=_sds((T, W), MXU),
        compiler_params=_cp("parallel"), name=name)(y, xact, z, d_skip, nw)


def _ssm_post_bwd(y, xact, z, d_skip, nw, dyn, name):
    T = y.shape[0]
    tm = min(T, 256)
    W = SSM_D_INNER

    def body(y_ref, x_ref, z_ref, d_ref, w_ref, dn_ref, dyg_ref, dz_ref, dw_ref):
        zv = z_ref[...]
        sz = _silu(zv)
        yg = y_ref[...] + d_ref[...] * x_ref[...]
        y2 = yg * sz
        r = lax.rsqrt(jnp.mean(y2 * y2, axis=-1, keepdims=True) + SSM_NORM_EPS)
        y2h = y2 * r
        dn = dn_ref[...]
        gy = dn * w_ref[...]
        dy2 = r * (gy - y2h * jnp.mean(gy * y2h, axis=-1, keepdims=True))
        dyg_ref[...] = dy2 * sz
        dz_ref[...] = (dy2 * yg * _dsilu(zv)).astype(dz_ref.dtype)
        part = jnp.sum(dn * y2h, axis=0, keepdims=True)

        @pl.when(pl.program_id(0) == 0)
        def _():
            dw_ref[...] = part

        @pl.when(pl.program_id(0) > 0)
        def _():
            dw_ref[...] += part

    row = pl.BlockSpec((tm, W), lambda i: (i, 0))
    vec = pl.BlockSpec((1, W), lambda i: (0, 0))
    return pl.pallas_call(
        body, grid=(T // tm,), in_specs=[row, row, row, vec, vec, row], out_specs=[row, row, vec],
        out_shape=[_sds((T, W)), _sds((T, W), MXU), _sds((1, W))],
        compiler_params=_cp("arbitrary"), name=name)(y, xact, z, d_skip, nw, dyn)


def _local_step(x0, cos, sin_s, target, P):
    mmf = functools.partial(_mm, tm=512)
    big, small = {}, {}
    h0 = _rmsnorm_fwd(x0, P["nm"][0], "norm_mix0")
    proj0 = mmf(h0, P["wmi"], tn=1280, tk=1024, name="mix_in")
    cat, attn, lse = _mixcore_fwd(proj0, cos, sin_s, P["pool_w"], P["pool_scale"], P["sinks"], "mixcore_fwd")
    x1 = mmf(cat, P["wmo"], tn=1024, tk=1024, res=x0, name="mix_out")

    def ffn_fwd(xin, i):
        hf = _rmsnorm_fwd(xin, P["nf"][i], f"norm_ffn{i}")
        hid = mmf(hf, P["wup"][i], tn=1408, tk=1024, name=f"ffn_up{i}")
        act = _ffn_mid_fwd(hid, P["fcw"][i], P["fcb"][i], f"ffn_mid_fwd{i}")
        xout = mmf(act, P["wdn"][i], tn=1024, tk=D_FF, res=xin, name=f"ffn_down{i}")
        return hf, hid, act, xout

    hf0, hid0, act0, x2 = ffn_fwd(x1, 0)
    h1 = _rmsnorm_fwd(x2, P["nm"][1], "norm_mix1")
    z = mmf(h1, P["wz"], tn=1024, tk=1024, name="ssm_in_z")
    xbc = mmf(h1, P["wxbc"], tn=1024, tk=1024, name="ssm_in_xbc")
    dtraw = mmf(h1, P["wdt"], tn=128, tk=1024, name="ssm_in_dt")
    xact = _ssm_pre_fwd(xbc, P["scw"], P["scb"], "ssm_pre_fwd")
    y, states = _ssd_fwd(xact, dtraw, P["dt_bias"], P["a_log"], "ssd_fwd")
    yn = _ssm_post_fwd(y, xact, z, P["d_exp"], P["snorm"], "ssm_post_fwd")
    x3 = mmf(yn, P["wso"], tn=1024, tk=SSM_D_INNER, res=x2, name="ssm_out")
    hf1, hid1, act1, x4 = ffn_fwd(x3, 1)
    loss_row, dx4, d_nfin = _loss_head(x4, P["nfin"], target, "loss_head")
    small["norm_final"] = d_nfin

    def ffn_bwd(xin, dxo, hf, hid, act, i):
        da = mmf(dxo, P["wdn"][i], tb=True, tn=1408, tk=1024, name=f"ffn_down_dx{i}")
        big[f"ffn_w_down{i}"] = _mm(act, dxo, ta=True, tm=1408, tn=1024, tk=512, name=f"ffn_down_dw{i}").reshape(N_CHIPS, D_FF // N_CHIPS, D_MODEL)
        dhid, dcw, dcb = _ffn_mid_bwd(hid, P["fcw"][i], P["fcb"][i], da, f"ffn_mid_bwd{i}")
        dhf = mmf(dhid, P["wup"][i], tb=True, tn=1024, tk=1408, name=f"ffn_up_dx{i}")
        big[f"ffn_w_up{i}"] = _mm(hf, dhid, ta=True, tm=1024, tn=1408, tk=512, out_shard_perm=(0, 2, 1, 3), name=f"ffn_up_dw{i}")
        dxi, dnf = _rmsnorm_bwd(xin, P["nf"][i], dhf, dxo, f"norm_ffn_bwd{i}")
        return dxi, dnf, dcw, dcb

    dx3, dnf1, dfcw1, dfcb1 = ffn_bwd(x3, dx4, hf1, hid1, act1, 1)
    dyn = mmf(dx3, P["wso"], tb=True, tn=1024, tk=1024, name="ssm_out_dx")
    big["ssm_w_out"] = _mm(yn, dx3, ta=True, tm=1024, tn=1024, tk=512, name="ssm_out_dw").reshape(N_CHIPS, SSM_D_INNER // N_CHIPS, D_MODEL)
    dyg, dz, d_snorm = _ssm_post_bwd(y, xact, z, P["d_exp"], P["snorm"], dyn, "ssm_post_bwd")
    dxact_p, ddtraw, d_dtb, d_alog, d_dskip = _ssd_bwd(xact, dtraw, P["dt_bias"], P["a_log"], P["d_exp"], states, dyg, "ssd_bwd")
    dxbc, d_scw, d_scb = _ssm_pre_bwd(xbc, P["scw"], P["scb"], dxact_p, "ssm_pre_bwd")
    dh1 = mmf(dz, P["wz"], tb=True, tn=1024, tk=1024, name="ssm_in_dx_z")
    dh1 = mmf(dxbc, P["wxbc"], tb=True, tn=1024, tk=1024, res=dh1, name="ssm_in_dx_xbc")
    dh1 = mmf(ddtraw, P["wdt"], tb=True, tn=1024, tk=128, res=dh1, name="ssm_in_dx_dt")
    dwz = _mm(h1, dz, ta=True, tm=1024, tn=1024, tk=512, name="ssm_in_dw_z")
    dwxbc = _mm(h1, dxbc, ta=True, tm=1024, tn=1024, tk=512, name="ssm_in_dw_xbc")
    dwdt = _mm(h1, ddtraw, ta=True, tm=1024, tn=128, tk=512, name="ssm_in_dw_dt")
    dwsi = jnp.concatenate([dwz, dwxbc, dwdt[:, :SSM_HEADS]], axis=1)
    big["ssm_w_in"] = dwsi.reshape(D_MODEL, N_CHIPS, SSM_IN_DIM // N_CHIPS).transpose(1, 0, 2)
    dx2, dnm1 = _rmsnorm_bwd(x2, P["nm"][1], dh1, dx3, "norm_mix_bwd1")
    dx1, dnf0, dfcw0, dfcb0 = ffn_bwd(x1, dx2, hf0, hid0, act0, 0)
    dcat = mmf(dx1, P["wmo"], tb=True, tn=1024, tk=1024, name="mix_out_dx")
    big["mix_w_out"] = _mm(cat, dx1, ta=True, tm=1024, tn=1024, tk=512, name="mix_out_dw").reshape(N_CHIPS, D_MODEL // N_CHIPS, D_MODEL)
    dproj0, d_pw, d_ps, d_sk = _mixcore_bwd(proj0, cos, sin_s, P["pool_w"], P["pool_scale"], P["sinks"], attn, lse, dcat, "mixcore_bwd")
    dh0 = mmf(dproj0, P["wmi"], tb=True, tn=1024, tk=1280, name="mix_in_dx")
    dwmi = _mm(h0, dproj0, ta=True, tm=1024, tn=1280, tk=512, name="mix_in_dw")
    big["mix_w_in"] = dwmi.reshape(D_MODEL, N_CHIPS, MIX_IN_DIM // N_CHIPS).transpose(1, 0, 2)
    dx0, dnm0 = _rmsnorm_bwd(x0, P["nm"][0], dh0, dx1, "norm_mix_bwd0")

    def unperm_cols(a):
        r = a.shape[0]
        t = a.reshape(r, N_CHIPS, FFN_TC)
        return jnp.stack([t[:, p] for p in _PERM], axis=0)

    small["norm_mix"] = jnp.concatenate([dnm0, dnm1], axis=0)
    small["norm_ffn"] = jnp.concatenate([dnf0, dnf1], axis=0)
    small["pool_w"] = d_pw.reshape(4 * POOL_GROUP, POOL_GROUP)
    small["pool_scale"] = d_ps
    small["attn_sinks"] = d_sk
    small["ssm_dt_bias"] = d_dtb
    small["ssm_A_log"] = d_alog
    small["ssm_D"] = d_dskip
    fcb = jnp.stack([unperm_cols(dfcb0), unperm_cols(dfcb1)], axis=0)
    small["ffn_conv_b"] = fcb.reshape(2, 2 * D_FF)
    small["ssm_conv_w"] = d_scw.reshape(SSM_CONV, N_CHIPS, SSM_CONV_DIM // N_CHIPS).transpose(1, 0, 2)
    small["ssm_conv_b"] = d_scb.reshape(N_CHIPS, 1, SSM_CONV_DIM // N_CHIPS)
    small["ssm_norm"] = d_snorm.reshape(N_CHIPS, 1, SSM_D_INNER // N_CHIPS)
    small["ffn_conv_w"] = jnp.concatenate([unperm_cols(dfcw0), unperm_cols(dfcw1)], axis=1)
    return loss_row, dx0, big, small


ANY = pl.BlockSpec(memory_space=pl.ANY)


def _place():
    return lax.axis_index("x"), lax.axis_index("y"), lax.axis_index("c")


def _gather_shards(shards, name):
    n = len(shards)
    split = [s.size >= (1 << 16) for s in shards]

    def half(ref, a, h):
        shp = shards[a].shape
        if len(shp) == 3:
            return ref.at[h]
        r2 = shp[0] // 2
        return ref.at[pl.ds(pl.multiple_of(h * r2, 2 * SUBLANES), r2), :]

    def body(*refs):
        ins, outs = refs[:n], refs[n:2 * n]
        send, recv, fsend, frecv = refs[2 * n:]
        x, y, c = _place()
        k = 2 * x + y
        chips = [(1 - x, y), (x, 1 - y), (1 - x, 1 - y)]

        def ici(a, j, src_slot_ref, dst_slot):
            px, py = chips[j]
            src = half(src_slot_ref, a, c) if split[a] else src_slot_ref
            dst = half(outs[a].at[dst_slot], a, c) if split[a] else outs[a].at[dst_slot]
            return pltpu.make_async_remote_copy(src, dst, send.at[a, j], recv.at[a, j], device_id=(px, py, c), device_id_type=MESH)

        def d2d(a, j, h):
            px, py = chips[j]
            part = half(outs[a].at[2 * px + py], a, h)
            return pltpu.make_async_remote_copy(part, part, fsend.at[a, j], frecv.at[a, j], device_id=(x, y, 1 - c), device_id_type=MESH)

        sends = [ici(a, j, ins[a], k) for a in range(n) for j in range(3)]
        for cp in sends:
            cp.start()
        passed = []
        for a in range(n):
            for j, (px, py) in enumerate(chips):
                ici(a, j, ins[a], 2 * px + py).wait_recv()
                if split[a]:
                    passed.append(d2d(a, j, c))
                    passed[-1].start()
        for a in range(n):
            if split[a]:
                for j in range(3):
                    d2d(a, j, 1 - c).wait_recv()
        for cp in sends + passed:
            cp.wait_send()

    return pl.pallas_call(
        body, in_specs=[ANY] * n, out_specs=[ANY] * n,
        out_shape=[_sds((N_CHIPS,) + s.shape, s.dtype) for s in shards],
        scratch_shapes=[pltpu.SemaphoreType.DMA((n, 3))] * 4,
        compiler_params=pltpu.CompilerParams(has_side_effects=True), name=name)(*shards)


def _pair_exchange(gs, name):
    n = len(gs)

    def body(*refs):
        ins, outs = refs[:n], refs[n:2 * n]
        send, recv = refs[2 * n:]
        x, y, c = _place()
        cps = []
        for a in range(n):
            r2 = gs[a].shape[1] // 2
            src = ins[a].at[:, pl.ds(pl.multiple_of((1 - c) * r2, SUBLANES), r2), :]
            cps.append(pltpu.make_async_remote_copy(src, outs[a], send.at[a], recv.at[a],
                                                    device_id=(x, y, 1 - c), device_id_type=MESH))
        for cp in cps:
            cp.start()
        for cp in cps:
            cp.wait()

    return pl.pallas_call(
        body, in_specs=[ANY] * n, out_specs=[ANY] * n,
        out_shape=[_sds((N_CHIPS, g.shape[1] // 2, g.shape[2]), g.dtype) for g in gs],
        scratch_shapes=[pltpu.SemaphoreType.DMA((n,)), pltpu.SemaphoreType.DMA((n,))],
        compiler_params=pltpu.CompilerParams(has_side_effects=True), name=name)(*gs)


def _chip_exchange(ps, name):
    n = len(ps)

    def body(*refs):
        ins, outs = refs[:n], refs[n:2 * n]
        send, recv = refs[2 * n:]
        x, y, c = _place()
        k = 2 * x + y
        chips = [(1 - x, y), (x, 1 - y), (1 - x, 1 - y)]
        sends = [pltpu.make_async_remote_copy(ins[a].at[2 * px + py], outs[a].at[k], send.at[a, j], recv.at[a, j],
                                              device_id=(px, py, c), device_id_type=MESH)
                 for a in range(n) for j, (px, py) in enumerate(chips)]
        for cp in sends:
            cp.start()
        for a in range(n):
            for j, (px, py) in enumerate(chips):
                pltpu.make_async_remote_copy(ins[a].at[k], outs[a].at[2 * px + py], send.at[a, j], recv.at[a, j],
                                             device_id=(px, py, c), device_id_type=MESH).wait_recv()
        for cp in sends:
            cp.wait_send()

    return pl.pallas_call(
        body, in_specs=[ANY] * n, out_specs=[ANY] * n, out_shape=[_sds(p.shape, p.dtype) for p in ps],
        scratch_shapes=[pltpu.SemaphoreType.DMA((n, 3)), pltpu.SemaphoreType.DMA((n, 3))],
        compiler_params=pltpu.CompilerParams(has_side_effects=True), name=name)(*ps)


def _half_exchange(fs, name):
    n = len(fs)

    def body(*refs):
        ins, outs = refs[:n], refs[n:2 * n]
        send, recv = refs[2 * n:]
        x, y, c = _place()
        cps = [pltpu.make_async_remote_copy(ins[a], outs[a], send.at[a], recv.at[a],
                                            device_id=(x, y, 1 - c), device_id_type=MESH) for a in range(n)]
        for cp in cps:
            cp.start()
        for cp in cps:
            cp.wait()

    return pl.pallas_call(
        body, in_specs=[ANY] * n, out_specs=[ANY] * n, out_shape=[_sds(f.shape, f.dtype) for f in fs],
        scratch_shapes=[pltpu.SemaphoreType.DMA((n,)), pltpu.SemaphoreType.DMA((n,))],
        compiler_params=pltpu.CompilerParams(has_side_effects=True), name=name)(*fs)


def _row_tile(rows, cols, budget=2 * 1024 * 1024, step=2 * SUBLANES):
    best = step
    for t in range(step, rows + 1, step):
        if rows % t == 0 and t * cols * 4 <= budget:
            best = t
    assert rows % best == 0, (rows, best)
    return best


def _pair_sum(g, got, cidx, name):
    _, R, C = g.shape
    r2 = R // 2
    tr = _row_tile(r2, C)
    nr = r2 // tr

    def body(c_ref, g_ref, o_ref_in, o_ref):
        o_ref[...] = (g_ref[...] + o_ref_in[...]).astype(o_ref.dtype)

    return pl.pallas_call(
        body,
        grid_spec=pltpu.PrefetchScalarGridSpec(
            num_scalar_prefetch=1, grid=(N_CHIPS, nr),
            in_specs=[pl.BlockSpec((None, tr, C), lambda k, i, c: (k, c[0] * nr + i, 0)),
                      pl.BlockSpec((None, tr, C), lambda k, i, c: (k, i, 0))],
            out_specs=pl.BlockSpec((None, tr, C), lambda k, i, c: (k, i, 0))),
        out_shape=_sds((N_CHIPS, r2, C), BF16), compiler_params=_cp("parallel", "parallel"), name=name)(cidx, g, got)


def _chip_sum(own, parts, kidx, name):
    _, r2, C = parts.shape
    tr = _row_tile(r2, C, budget=1024 * 1024)

    def body(k_ref, o_ref_in, p1_ref, p2_ref, p3_ref, o_ref):
        o_ref[...] = ((o_ref_in[...].astype(F32) + p1_ref[...].astype(F32)) + p2_ref[...].astype(F32)) + p3_ref[...].astype(F32)

    def slot(d):
        return pl.BlockSpec((None, tr, C), lambda i, k: ((k[0] + d) % N_CHIPS, i, 0))

    return pl.pallas_call(
        body,
        grid_spec=pltpu.PrefetchScalarGridSpec(
            num_scalar_prefetch=1, grid=(r2 // tr,), in_specs=[slot(0), slot(1), slot(2), slot(3)],
            out_specs=pl.BlockSpec((tr, C), lambda i, k: (i, 0))),
        out_shape=_sds((r2, C)), compiler_params=_cp("parallel"), name=name)(kidx, own, parts, parts, parts)


def _adamw_math(w, g, m, v):
    m2 = ADAM_B1 * m + (1.0 - ADAM_B1) * g
    v2 = ADAM_B2 * v + (1.0 - ADAM_B2) * (g * g)
    m_hat = m2 / (1.0 - ADAM_B1 ** ADAM_STEP)
    v_hat = v2 / (1.0 - ADAM_B2 ** ADAM_STEP)
    delta = -ADAM_LR * (m_hat / (jnp.sqrt(v_hat) + ADAM_EPS) + ADAM_WD * w)
    return delta, m2, v2


def _adamw(w, m, v, gparts, cidx, name):
    Lw, R, C = w.shape
    r2 = R // 2
    tr = _row_tile(r2, C, budget=1024 * 1024)
    nr = r2 // tr
    flat = [h for pair in gparts for h in pair]

    def body(*refs):
        c_ref = refs[0]
        w_ref, m_ref, v_ref = refs[1:4]
        g_refs = refs[4:4 + 2 * Lw]
        go_ref, d_ref, mo_ref, vo_ref = refs[4 + 2 * Lw:]
        mine = (pl.program_id(1) // nr) == c_ref[0]
        g = jnp.where(mine, g_refs[0][...], g_refs[1][...])
        for l in range(1, Lw):
            g = jnp.where(pl.program_id(0) == l, jnp.where(mine, g_refs[2 * l][...], g_refs[2 * l + 1][...]), g)
        d, m2, v2 = _adamw_math(w_ref[...], g, m_ref[...], v_ref[...])
        go_ref[...] = g
        d_ref[...] = d
        mo_ref[...] = m2
        vo_ref[...] = v2

    blk = pl.BlockSpec((None, tr, C), lambda l, i, c: (l, i, 0))
    gblk = pl.BlockSpec((tr, C), lambda l, i, c: (i % nr, 0))
    return pl.pallas_call(
        body,
        grid_spec=pltpu.PrefetchScalarGridSpec(
            num_scalar_prefetch=1, grid=(Lw, 2 * nr), in_specs=[blk, blk, blk] + [gblk] * (2 * Lw), out_specs=[blk] * 4),
        out_shape=[_sds((Lw, R, C))] * 4, compiler_params=_cp("parallel", "parallel"), name=name)(cidx, w, m, v, *flat)


def _small_reduce_adamw(items, loss_row, name):
    n = len(items)
    gshapes = [it[0].shape for it in items] + [loss_row.shape]
    pshapes = [it[1].shape for it in items]
    ng = n + 1

    def body(*refs):
        g_in = refs[:ng]
        wmv = refs[ng:ng + 3 * n]
        outs = refs[ng + 3 * n:ng + 3 * n + 4 * n + 1]
        bufs = refs[ng + 7 * n + 1:ng + 7 * n + 1 + ng]
        send, recv = refs[-2:]
        x, y, c = _place()
        me = 4 * x + 2 * y + c
        k = 2 * x + y
        flips = [(fx, fy, fc) for fx in (0, 1) for fy in (0, 1) for fc in (0, 1)][1:]

        def peer(f):
            return (x ^ f[0], y ^ f[1], c ^ f[2])

        def slot(p):
            return 4 * p[0] + 2 * p[1] + p[2]

        for a in range(ng):
            bufs[a][me] = g_in[a][...]
        sends = [pltpu.make_async_remote_copy(g_in[a], bufs[a].at[me], send.at[a, j], recv.at[a, j],
                                              device_id=peer(f), device_id_type=MESH)
                 for a in range(ng) for j, f in enumerate(flips)]
        for cp in sends:
            cp.start()
        for a in range(ng):
            for j, f in enumerate(flips):
                pltpu.make_async_remote_copy(g_in[a], bufs[a].at[slot(peer(f))], send.at[a, j], recv.at[a, j],
                                             device_id=peer(f), device_id_type=MESH).wait_recv()
        for cp in sends:
            cp.wait_send()
        for a in range(ng):
            sharded = len(gshapes[a]) == 3

            def part(d):
                return bufs[a][d, k] if sharded else bufs[a][d]

            tot = part(0)
            for d in range(1, N_DEV):
                tot = tot + part(d)
            if a == n:
                outs[4 * n][...] = tot
                continue
            pr, pc = pshapes[a]
            g = tot[:pr, :pc]
            w_ref, m_ref, v_ref = wmv[3 * a:3 * a + 3]
            d_, m2, v2 = _adamw_math(w_ref[...], g, m_ref[...], v_ref[...])
            outs[4 * a][...] = g
            outs[4 * a + 1][...] = d_
            outs[4 * a + 2][...] = m2
            outs[4 * a + 3][...] = v2

    vm = pl.BlockSpec(memory_space=pltpu.VMEM)
    args = [it[0] for it in items] + [loss_row]
    for it in items:
        args += [it[1], it[2], it[3]]
    out_shape = []
    for ps in pshapes:
        out_shape += [_sds(ps)] * 4
    out_shape.append(_sds(loss_row.shape))
    return pl.pallas_call(
        body, in_specs=[vm] * len(args), out_specs=[vm] * len(out_shape), out_shape=out_shape,
        scratch_shapes=[pltpu.VMEM((N_DEV,) + tuple(s), F32) for s in gshapes]
        + [pltpu.SemaphoreType.DMA((ng, N_DEV - 1)), pltpu.SemaphoreType.DMA((ng, N_DEV - 1))],
        compiler_params=pltpu.CompilerParams(has_side_effects=True, vmem_limit_bytes=V7X_VMEM_LIMIT), name=name)(*args)


_PERM = (0, 2, 1, 3)


def _cols_from_shards(g):
    return g.transpose(1, 0, 2).reshape(g.shape[1], N_CHIPS * g.shape[2])


def _rope_tables(positions):
    inv_freq = ROPE_THETA ** (-jnp.arange(0, HEAD_DIM, 2, dtype=F32) / HEAD_DIM)
    ang = positions.astype(F32).reshape(-1, 1) * inv_freq
    cos, sin = jnp.cos(ang), jnp.sin(ang)
    cos = jnp.concatenate([cos, cos, cos, cos], axis=-1)
    sin_s = jnp.concatenate([-sin, sin, -sin, sin], axis=-1)
    return cos, sin_s


def kernel(x, positions, norm_mix, norm_ffn, norm_final, mix_w_in, pool_w, pool_scale, attn_sinks, mix_w_out, ssm_w_in, ssm_conv_w, ssm_conv_b, ssm_dt_bias, ssm_A_log, ssm_D, ssm_norm, ssm_w_out, ffn_w_up, ffn_conv_w, ffn_conv_b, ffn_w_down, loss_target, m_norm_mix, m_norm_ffn, m_norm_final, m_mix_w_in, m_pool_w, m_pool_scale, m_attn_sinks, m_mix_w_out, m_ssm_w_in, m_ssm_conv_w, m_ssm_conv_b, m_ssm_dt_bias, m_ssm_A_log, m_ssm_D, m_ssm_norm, m_ssm_w_out, m_ffn_w_up, m_ffn_conv_w, m_ffn_conv_b, m_ffn_w_down, v_norm_mix, v_norm_ffn, v_norm_final, v_mix_w_in, v_pool_w, v_pool_scale, v_attn_sinks, v_mix_w_out, v_ssm_w_in, v_ssm_conv_w, v_ssm_conv_b, v_ssm_dt_bias, v_ssm_A_log, v_ssm_D, v_ssm_norm, v_ssm_w_out, v_ffn_w_up, v_ffn_conv_w, v_ffn_conv_b, v_ffn_w_down):
    W = dict(norm_mix=norm_mix, norm_ffn=norm_ffn, norm_final=norm_final, mix_w_in=mix_w_in, pool_w=pool_w, pool_scale=pool_scale, attn_sinks=attn_sinks, mix_w_out=mix_w_out, ssm_w_in=ssm_w_in, ssm_conv_w=ssm_conv_w, ssm_conv_b=ssm_conv_b, ssm_dt_bias=ssm_dt_bias, ssm_A_log=ssm_A_log, ssm_D=ssm_D, ssm_norm=ssm_norm, ssm_w_out=ssm_w_out, ffn_w_up=ffn_w_up, ffn_conv_w=ffn_conv_w, ffn_conv_b=ffn_conv_b, ffn_w_down=ffn_w_down)
    Mo = dict(norm_mix=m_norm_mix, norm_ffn=m_norm_ffn, norm_final=m_norm_final, mix_w_in=m_mix_w_in, pool_w=m_pool_w, pool_scale=m_pool_scale, attn_sinks=m_attn_sinks, mix_w_out=m_mix_w_out, ssm_w_in=m_ssm_w_in, ssm_conv_w=m_ssm_conv_w, ssm_conv_b=m_ssm_conv_b, ssm_dt_bias=m_ssm_dt_bias, ssm_A_log=m_ssm_A_log, ssm_D=m_ssm_D, ssm_norm=m_ssm_norm, ssm_w_out=m_ssm_w_out, ffn_w_up=m_ffn_w_up, ffn_conv_w=m_ffn_conv_w, ffn_conv_b=m_ffn_conv_b, ffn_w_down=m_ffn_w_down)
    Vo = dict(norm_mix=v_norm_mix, norm_ffn=v_norm_ffn, norm_final=v_norm_final, mix_w_in=v_mix_w_in, pool_w=v_pool_w, pool_scale=v_pool_scale, attn_sinks=v_attn_sinks, mix_w_out=v_mix_w_out, ssm_w_in=v_ssm_w_in, ssm_conv_w=v_ssm_conv_w, ssm_conv_b=v_ssm_conv_b, ssm_dt_bias=v_ssm_dt_bias, ssm_A_log=v_ssm_A_log, ssm_D=v_ssm_D, ssm_norm=v_ssm_norm, ssm_w_out=v_ssm_w_out, ffn_w_up=v_ffn_w_up, ffn_conv_w=v_ffn_conv_w, ffn_conv_b=v_ffn_conv_b, ffn_w_down=v_ffn_w_down)

    sh = [mix_w_in[0].astype(MXU), mix_w_out[0].astype(MXU), ssm_w_in[0].astype(MXU), ssm_w_out[0].astype(MXU),
          ffn_w_up.astype(MXU), ffn_w_down.astype(MXU), ssm_conv_w[0], ssm_conv_b, ssm_norm, ffn_conv_w]
    kchip = 2 * lax.axis_index("x") + lax.axis_index("y")
    gathered = _gather_shards(sh, "gather_weights")
    g_mi, g_mo, g_si, g_so, g_up, g_dn, g_scw, g_scb, g_sn, g_fcw = [
        lax.dynamic_update_slice_in_dim(g, own[None], kchip, axis=0) for g, own in zip(gathered, sh)]
    wsi = _cols_from_shards(g_si)
    zx = SSM_D_INNER + SSM_CONV_DIM
    P = dict(
        nm=norm_mix, nf=norm_ffn, nfin=norm_final,
        wmi=_cols_from_shards(g_mi), wmo=g_mo.reshape(D_MODEL, D_MODEL),
        pool_w=pool_w[0], pool_scale=pool_scale, sinks=attn_sinks[0],
        wz=wsi[:, :SSM_D_INNER], wxbc=wsi[:, SSM_D_INNER:zx],
        wdt=jnp.pad(wsi[:, zx:], ((0, 0), (0, LANES - SSM_HEADS))),
        scw=_cols_from_shards(g_scw), scb=g_scb.reshape(1, SSM_CONV_DIM), snorm=g_sn.reshape(1, SSM_D_INNER),
        dt_bias=jnp.pad(ssm_dt_bias, ((0, 0), (0, LANES - SSM_HEADS))), a_log=jnp.pad(ssm_A_log, ((0, 0), (0, LANES - SSM_HEADS))),
        d_exp=jnp.repeat(ssm_D, SSM_D_INNER // SSM_HEADS, axis=1),
        wso=g_so.reshape(SSM_D_INNER, D_MODEL),
        wup=[jnp.concatenate([g_up[p, i] for p in _PERM], axis=1) for i in range(2)],
        fcw=[jnp.concatenate([g_fcw[p, i] for p in _PERM], axis=1) for i in range(2)],
        fcb=[jnp.concatenate([ffn_conv_b[i:i + 1, p * FFN_TC:(p + 1) * FFN_TC] for p in _PERM], axis=1) for i in range(2)],
        wdn=[g_dn[:, i].reshape(D_FF, D_MODEL) for i in range(2)],
    )
    cos, sin_s = _rope_tables(positions)
    loss_row, grad_x, big, small = _local_step(x[0], cos, sin_s, loss_target[0], P)

    names = ["mix_w_in", "mix_w_out", "ssm_w_in", "ssm_w_out", "ffn_w_up0", "ffn_w_up1", "ffn_w_down0", "ffn_w_down1"]
    gs = [big[nm] for nm in names]
    cidx = lax.axis_index("c").astype(jnp.int32).reshape(1)
    got = _pair_exchange(gs, "pair_exchange")
    ps = [_pair_sum(g, o, cidx, f"pair_sum_{nm}") for g, o, nm in zip(gs, got, names)]
    parts = _chip_exchange(ps, "chip_exchange")
    kidx = kchip.astype(jnp.int32).reshape(1)
    fs = [_chip_sum(o, p, kidx, f"chip_sum_{nm}") for o, p, nm in zip(ps, parts, names)]
    others = _half_exchange(fs, "half_exchange")
    red = {nm: (f, o) for nm, f, o in zip(names, fs, others)}

    out = {}

    def big_update(pname, gparts):
        w = W[pname]
        lw = len(gparts)
        shp = w.shape
        r2, cc = gparts[0][0].shape
        w3, m3, v3 = (t.reshape(lw, 2 * r2, cc) for t in (w, Mo[pname], Vo[pname]))
        res = _adamw(w3, m3, v3, gparts, cidx, f"adamw_{pname}")
        out[pname] = tuple(r.reshape(shp) for r in res)

    big_update("mix_w_in", [red["mix_w_in"]])
    big_update("mix_w_out", [red["mix_w_out"]])
    big_update("ssm_w_in", [red["ssm_w_in"]])
    big_update("ssm_w_out", [red["ssm_w_out"]])
    big_update("ffn_w_up", [red["ffn_w_up0"], red["ffn_w_up1"]])
    big_update("ffn_w_down", [red["ffn_w_down0"], red["ffn_w_down1"]])

    small_names = ["norm_mix", "norm_ffn", "norm_final", "pool_w", "pool_scale", "attn_sinks", "ssm_dt_bias", "ssm_A_log",
                   "ssm_D", "ffn_conv_b", "ssm_conv_w", "ssm_conv_b", "ssm_norm", "ffn_conv_w"]

    def as2d(t):
        if t.ndim == 1:
            return t.reshape(1, -1)
        return t.reshape(-1, t.shape[-1])

    items = [(small[nm], as2d(W[nm]), as2d(Mo[nm]), as2d(Vo[nm])) for nm in small_names]
    res = _small_reduce_adamw(items, loss_row, "small_reduce_adamw")
    for a, nm in enumerate(small_names):
        out[nm] = tuple(r.reshape(W[nm].shape) for r in res[4 * a:4 * a + 4])
    loss = res[-1][0, 0]

    order = ["norm_mix", "norm_ffn", "norm_final", "mix_w_in", "pool_w", "pool_scale", "attn_sinks", "mix_w_out", "ssm_w_in",
             "ssm_conv_w", "ssm_conv_b", "ssm_dt_bias", "ssm_A_log", "ssm_D", "ssm_norm", "ssm_w_out", "ffn_w_up", "ffn_conv_w",
             "ffn_conv_b", "ffn_w_down"]
    return (loss, grad_x.reshape(x.shape), *[out[nm][0] for nm in order], *[out[nm][1] for nm in order],
            *[out[nm][2] for nm in order], *[out[nm][3] for nm in order])
```

```python
import functools

import jax
import jax.numpy as jnp
from jax import lax
from jax.experimental import pallas as pl
from jax.experimental.pallas import tpu as pltpu

F32 = jnp.float32
BF16 = jnp.bfloat16
MXU = BF16
HI = lax.Precision.HIGHEST

D_MODEL = 1024
POOL_WINDOWS = (2, 4, 8, 16)
POOL_DIM = 512
POOL_GROUP = 128
HEAD_DIM = 64
N_HEADS = 8
N_KV_HEADS = 2
GQ = 4
Q_DIM = 512
KV_DIM = 128
BLOCK = 128
ROPE_THETA = 10000.0
MIX_IN_DIM = 1280
SSM_D_INNER = 2048
SSM_HEADS = 32
SSM_GROUPS = 8
SSM_STATE = 128
SSM_CONV = 4
SSM_CHUNK = 128
SSM_CONV_DIM = 4096
SSM_IN_DIM = 6176
D_FF = 2816
FFN_CONV = 3
NORM_EPS = 1e-6
SSM_NORM_EPS = 1e-5
ADAM_LR = 0.001
ADAM_B1 = 0.9
ADAM_B2 = 0.999
ADAM_EPS = 1e-08
ADAM_WD = 0.01
ADAM_STEP = 10

N_CHIPS = 4
N_DEV = 8
LANES = 128
SUBLANES = 8
V7X_VMEM_LIMIT = 56 * 1024 * 1024
NEG = -1e30
MESH = pl.DeviceIdType.MESH


def _cp(*sem):
    return pltpu.CompilerParams(dimension_semantics=sem if sem else None, vmem_limit_bytes=V7X_VMEM_LIMIT)


def _sds(shape, dtype=F32):
    return jax.ShapeDtypeStruct(tuple(shape), dtype)


def _iota(shape, dim):
    return lax.broadcasted_iota(jnp.int32, shape, dim)


def _silu(x):
    return x * (1.0 / (1.0 + jnp.exp(-x)))


def _dsilu(x):
    s = 1.0 / (1.0 + jnp.exp(-x))
    return s * (1.0 + x * (1.0 - s))


def _mm(a, b, *, ta=False, tb=False, tm, tn, tk, res=None, out_dtype=F32, out_shard_perm=None, name):
    M, K = (a.shape[1], a.shape[0]) if ta else a.shape
    N = b.shape[0] if tb else b.shape[1]
    tm, tn, tk = min(tm, M), min(tn, N), min(tk, K)
    gm, gn, gk = M // tm, N // tn, K // tk
    assert gm * tm == M and gn * tn == N and gk * tk == K, (name, M, N, K, tm, tn, tk)
    a_spec = pl.BlockSpec((tk, tm), lambda i, j, k: (k, i)) if ta else pl.BlockSpec((tm, tk), lambda i, j, k: (i, k))
    b_spec = pl.BlockSpec((tn, tk), lambda i, j, k: (j, k)) if tb else pl.BlockSpec((tk, tn), lambda i, j, k: (k, j))
    dims = (((0 if ta else 1,), (1 if tb else 0,)), ((), ()))
    has_res = res is not None

    def body(*refs):
        a_ref, b_ref = refs[0], refs[1]
        r_ref = refs[2] if has_res else None
        o_ref = refs[3] if has_res else refs[2]
        p = lax.dot_general(a_ref[...].astype(MXU), b_ref[...].astype(MXU), dims, preferred_element_type=F32)
        if gk == 1:
            if has_res:
                p = p + r_ref[...]
            o_ref[...] = p.astype(out_dtype)
        else:
            acc = refs[-1]
            k = pl.program_id(2)

            @pl.when(k == 0)
            def _():
                acc[...] = p

            @pl.when(k > 0)
            def _():
                acc[...] += p

            @pl.when(k == gk - 1)
            def _():
                r = acc[...]
                if has_res:
                    r = r + r_ref[...]
                o_ref[...] = r.astype(out_dtype)

    in_specs = [a_spec, b_spec]
    args = [a, b]
    if has_res:
        in_specs.append(pl.BlockSpec((tm, tn), lambda i, j, k: (i, j)))
        args.append(res)
    if out_shard_perm is None:
        out_spec = pl.BlockSpec((tm, tn), lambda i, j, k: (i, j))
        out_shape = _sds((M, N), out_dtype)
    else:
        assert gn == len(out_shard_perm) == 4 and tuple(out_shard_perm) == (0, 2, 1, 3)
        out_spec = pl.BlockSpec((None, tm, tn), lambda i, j, k: ((j % 2) * 2 + j // 2, i, 0))
        out_shape = _sds((gn, M, tn), out_dtype)
    return pl.pallas_call(
        body, grid=(gm, gn, gk), in_specs=in_specs, out_specs=out_spec, out_shape=out_shape,
        scratch_shapes=[pltpu.VMEM((tm, tn), F32)] if gk > 1 else [],
        compiler_params=_cp("parallel", "parallel", "arbitrary"), name=name)(*args)


def _rmsnorm_fwd(x, w, name):
    T, D = x.shape
    tm = min(T, 512)

    def body(x_ref, w_ref, o_ref):
        xv = x_ref[...]
        r = lax.rsqrt(jnp.mean(xv * xv, axis=-1, keepdims=True) + NORM_EPS)
        o_ref[...] = (xv * r * w_ref[...]).astype(o_ref.dtype)

    return pl.pallas_call(
        body, grid=(T // tm,),
        in_specs=[pl.BlockSpec((tm, D), lambda i: (i, 0)), pl.BlockSpec((1, D), lambda i: (0, 0))],
        out_specs=pl.BlockSpec((tm, D), lambda i: (i, 0)), out_shape=_sds((T, D), MXU),
        compiler_params=_cp("parallel"), name=name)(x, w.reshape(1, D))


def _rmsnorm_bwd(x, w, dh, dres, name):
    T, D = x.shape
    tm = min(T, 512)

    def body(x_ref, w_ref, dh_ref, dr_ref, dx_ref, dw_ref):
        xv = x_ref[...]
        r = lax.rsqrt(jnp.mean(xv * xv, axis=-1, keepdims=True) + NORM_EPS)
        xh = xv * r
        dh = dh_ref[...]
        g = dh * w_ref[...]
        dx_ref[...] = dr_ref[...] + r * (g - xh * jnp.mean(g * xh, axis=-1, keepdims=True))
        part = jnp.sum(dh * xh, axis=0, keepdims=True)

        @pl.when(pl.program_id(0) == 0)
        def _():
            dw_ref[...] = part

        @pl.when(pl.program_id(0) > 0)
        def _():
            dw_ref[...] += part

    row = pl.BlockSpec((tm, D), lambda i: (i, 0))
    vec = pl.BlockSpec((1, D), lambda i: (0, 0))
    return pl.pallas_call(
        body, grid=(T // tm,), in_specs=[row, vec, row, row], out_specs=[row, vec],
        out_shape=[_sds((T, D)), _sds((1, D))], compiler_params=_cp("arbitrary"), name=name)(x, w.reshape(1, D), dh, dres)


def _loss_head(x, w, target, name):
    T, D = x.shape
    tm = min(T, 512)

    def body(x_ref, w_ref, t_ref, loss_ref, dx_ref, dw_ref):
        xv = x_ref[...]
        r = lax.rsqrt(jnp.mean(xv * xv, axis=-1, keepdims=True) + NORM_EPS)
        xh = xv * r
        wv = w_ref[...]
        e = xh * wv - t_ref[...]
        lpart = 0.5 * jnp.sum(jnp.mean(e * e, axis=-1, keepdims=True), axis=0, keepdims=True)
        dy = e * (1.0 / D)
        g = dy * wv
        dx_ref[...] = r * (g - xh * jnp.mean(g * xh, axis=-1, keepdims=True))
        part = jnp.sum(dy * xh, axis=0, keepdims=True)
        lrow = jnp.broadcast_to(lpart, (1, LANES))

        @pl.when(pl.program_id(0) == 0)
        def _():
            dw_ref[...] = part
            loss_ref[...] = lrow

        @pl.when(pl.program_id(0) > 0)
        def _():
            dw_ref[...] += part
            loss_ref[...] += lrow

    row = pl.BlockSpec((tm, D), lambda i: (i, 0))
    vec = pl.BlockSpec((1, D), lambda i: (0, 0))
    return pl.pallas_call(
        body, grid=(T // tm,), in_specs=[row, vec, row],
        out_specs=[pl.BlockSpec((1, LANES), lambda i: (0, 0)), row, vec],
        out_shape=[_sds((1, LANES)), _sds((T, D)), _sds((1, D))],
        compiler_params=_cp("arbitrary"), name=name)(x, w.reshape(1, D), target)


def _shift_down(cur, prev8, s):
    if s == 0:
        return cur
    tm = cur.shape[0]
    rc = pltpu.roll(cur, s, 0)
    top = jnp.where(_iota((SUBLANES, cur.shape[1]), 0) < s, pltpu.roll(prev8, s, 0), rc[:SUBLANES])
    return jnp.concatenate([top, rc[SUBLANES:]], axis=0) if tm > SUBLANES else top


def _shift_up(cur, next8, s):
    if s == 0:
        return cur
    tm = cur.shape[0]
    rc = pltpu.roll(cur, tm - s, 0)
    bot = jnp.where(_iota((SUBLANES, cur.shape[1]), 0) >= SUBLANES - s, pltpu.roll(next8, SUBLANES - s, 0), rc[tm - SUBLANES:])
    return jnp.concatenate([rc[:tm - SUBLANES], bot], axis=0) if tm > SUBLANES else bot


def _conv_rows(cur, prev8, w, b, K):
    acc = cur * w[K - 1:K, :] + b
    for s in range(1, K):
        acc = acc + _shift_down(cur, prev8, s) * w[K - 1 - s:K - s, :]
    return acc


def _halo_specs(tm, tc, col_of):
    q = tm // SUBLANES

    def prev_map(i, j):
        return (jnp.maximum(i * q - 1, 0), col_of(j))

    def make_next(n_row_tiles):
        def next_map(i, j):
            return (jnp.minimum((i + 1) * q, n_row_tiles * q - 1), col_of(j))
        return next_map

    return (lambda: pl.BlockSpec((SUBLANES, tc), prev_map)), (lambda n: pl.BlockSpec((SUBLANES, tc), make_next(n)))


FFN_TC = 1408


def _ffn_mid_fwd(hid, cw, cb, name):
    T = hid.shape[0]
    tm = min(T, 256)
    nt, nj = T // tm, D_FF // FFN_TC
    K = FFN_CONV

    def body(h_ref, hp_ref, w_ref, b_ref, o_ref):
        i = pl.program_id(0)
        cur = h_ref[...]
        prev8 = jnp.where(i > 0, hp_ref[...], 0.0)
        hc = _conv_rows(cur, prev8, w_ref[...], b_ref[...], K)
        o_ref[...] = (_silu(hc[:, FFN_TC:]) * hc[:, :FFN_TC]).astype(o_ref.dtype)

    mk_prev, _ = _halo_specs(tm, 2 * FFN_TC, lambda j: j)
    return pl.pallas_call(
        body, grid=(nt, nj),
        in_specs=[pl.BlockSpec((tm, 2 * FFN_TC), lambda i, j: (i, j)), mk_prev(),
                  pl.BlockSpec((K, 2 * FFN_TC), lambda i, j: (0, j)), pl.BlockSpec((1, 2 * FFN_TC), lambda i, j: (0, j))],
        out_specs=pl.BlockSpec((tm, FFN_TC), lambda i, j: (i, j)), out_shape=_sds((T, D_FF), MXU),
        compiler_params=_cp("parallel", "parallel"), name=name)(hid, hid, cw, cb)


def _ffn_mid_bwd(hid, cw, cb, da, name):
    T = hid.shape[0]
    tm = min(T, 256)
    nt, nj = T // tm, D_FF // FFN_TC
    K = FFN_CONV
    W2 = 2 * FFN_TC

    def body(h_ref, hp_ref, hn_ref, da_ref, dan_ref, w_ref, b_ref, dh_ref, dw_ref, db_ref):
        i = pl.program_id(1)
        w = w_ref[...]
        b = b_ref[...]
        cur = h_ref[...]
        prev8 = jnp.where(i > 0, hp_ref[...], 0.0)
        nxt8 = hn_ref[...]
        last = i == nt - 1

        def dpre(hc, dav):
            u, g = hc[:, :FFN_TC], hc[:, FFN_TC:]
            return jnp.concatenate([dav * _silu(g), dav * u * _dsilu(g)], axis=1)

        hc = _conv_rows(cur, prev8, w, b, K)
        d_cur = dpre(hc, da_ref[...])
        hc_n = _conv_rows(nxt8, cur[tm - SUBLANES:], w, b, K)
        d_nxt = jnp.where(last, 0.0, dpre(hc_n, dan_ref[...]))
        dh = d_cur * w[K - 1:K, :]
        for s in range(1, K):
            dh = dh + _shift_up(d_cur, d_nxt, s) * w[K - 1 - s:K - s, :]
        dh_ref[...] = dh.astype(dh_ref.dtype)
        rows = [jnp.sum(d_cur * _shift_down(cur, prev8, K - 1 - k), axis=0, keepdims=True) for k in range(K)]
        dwp = jnp.concatenate(rows, axis=0)
        dbp = jnp.sum(d_cur, axis=0, keepdims=True)

        @pl.when(i == 0)
        def _():
            dw_ref[...] = dwp
            db_ref[...] = dbp

        @pl.when(i > 0)
        def _():
            dw_ref[...] += dwp
            db_ref[...] += dbp

    q = tm // SUBLANES
    blk = pl.BlockSpec((tm, W2), lambda j, i: (i, j))
    prv = pl.BlockSpec((SUBLANES, W2), lambda j, i: (jnp.maximum(i * q - 1, 0), j))
    nxt = pl.BlockSpec((SUBLANES, W2), lambda j, i: (jnp.minimum((i + 1) * q, nt * q - 1), j))
    dab = pl.BlockSpec((tm, FFN_TC), lambda j, i: (i, j))
    dan = pl.BlockSpec((SUBLANES, FFN_TC), lambda j, i: (jnp.minimum((i + 1) * q, nt * q - 1), j))
    return pl.pallas_call(
        body, grid=(nj, nt),
        in_specs=[blk, prv, nxt, dab, dan, pl.BlockSpec((K, W2), lambda j, i: (0, j)), pl.BlockSpec((1, W2), lambda j, i: (0, j))],
        out_specs=[blk, pl.BlockSpec((K, W2), lambda j, i: (0, j)), pl.BlockSpec((1, W2), lambda j, i: (0, j))],
        out_shape=[_sds((T, 2 * D_FF), MXU), _sds((K, 2 * D_FF)), _sds((1, 2 * D_FF))],
        compiler_params=_cp("parallel", "arbitrary"), name=name)(hid, hid, hid, da, da, cw, cb)


def _rope(t, cos, sin_s, inverse=False):
    n = t.shape[1] // LANES
    c = jnp.concatenate([cos] * n, axis=1) if n > 1 else cos
    s = jnp.concatenate([sin_s] * n, axis=1) if n > 1 else sin_s
    a = pltpu.roll(t, HEAD_DIM // 2, 1)
    b = pltpu.roll(t, t.shape[1] - HEAD_DIM // 2, 1)
    first = (_iota(t.shape, 1) % HEAD_DIM) < HEAD_DIM // 2
    rot = jnp.where(first, b, a) * s
    return t * c - rot if inverse else t * c + rot


def _stack_heads(t, g):
    return jnp.concatenate([t[:, (GQ * g + r) * HEAD_DIM:(GQ * g + r + 1) * HEAD_DIM] for r in range(GQ)], axis=0)


def _stack_cols(t, g):
    return jnp.concatenate([t[:, GQ * g + r:GQ * g + r + 1] for r in range(GQ)], axis=0)


def _pool_sums(prev, cur, w):
    s = jnp.concatenate([prev, cur], axis=0)
    sh = 1
    while sh < w:
        s = s + pltpu.roll(s, sh, 0)
        sh *= 2
    return s[BLOCK:]


def _nt(a, b):
    return lax.dot_general(a.astype(MXU), b.astype(MXU), (((1,), (1,)), ((), ())), preferred_element_type=F32)


def _tn(a, b):
    return lax.dot_general(a.astype(MXU), b.astype(MXU), (((0,), (0,)), ((), ())), preferred_element_type=F32)


def _nn(a, b):
    return jnp.dot(a.astype(MXU), b.astype(MXU), preferred_element_type=F32)


def _mixcore_fwd(proj, cos, sin_s, pool_w, pool_scale, sinks, name):
    T = proj.shape[0]
    nb = T // BLOCK
    scale = HEAD_DIM ** -0.5

    def body(p_ref, pp_ref, c_ref, s_ref, cp_ref, sp_ref, pw_ref, ps_ref, sk_ref, cat_ref, at_ref, lse_ref):
        i = pl.program_id(0)
        has_prev = i > 0
        cur = p_ref[...]
        prv = jnp.where(has_prev, pp_ref[...], 0.0)
        tpos = (i * BLOCK + _iota((BLOCK, 1), 0) + 1).astype(F32)
        for g, w in enumerate(POOL_WINDOWS):
            sl = slice(g * POOL_GROUP, (g + 1) * POOL_GROUP)
            pooled = _pool_sums(prv[:, sl], cur[:, sl], w) / jnp.minimum(tpos, float(w)) - cur[:, sl]
            cat_ref[:, sl] = (_nn(pooled, pw_ref[g]) * ps_ref[:, sl]).astype(cat_ref.dtype)
        q = _rope(cur[:, POOL_DIM:POOL_DIM + Q_DIM], c_ref[...], s_ref[...])
        kc = _rope(cur[:, POOL_DIM + Q_DIM:POOL_DIM + Q_DIM + KV_DIM], c_ref[...], s_ref[...])
        kp = _rope(prv[:, POOL_DIM + Q_DIM:POOL_DIM + Q_DIM + KV_DIM], cp_ref[...], sp_ref[...])
        vc = cur[:, POOL_DIM + Q_DIM + KV_DIM:]
        vp = prv[:, POOL_DIM + Q_DIM + KV_DIM:]
        ri = _iota((GQ * BLOCK, BLOCK), 0) % BLOCK
        cj = _iota((GQ * BLOCK, BLOCK), 1)
        mc = cj <= ri
        mp = jnp.logical_and(cj > ri, has_prev)
        outs, lses = [], []
        for g in range(N_KV_HEADS):
            hs = slice(g * HEAD_DIM, (g + 1) * HEAD_DIM)
            qg = _stack_heads(q, g) * scale
            sc = jnp.where(mc, _nt(qg, kc[:, hs]), NEG)
            sp = jnp.where(mp, _nt(qg, kp[:, hs]), NEG)
            sink = jnp.concatenate([jnp.full((BLOCK, 1), sk_ref[GQ * g + r], F32) for r in range(GQ)], axis=0)
            m = jnp.maximum(jnp.maximum(jnp.max(sc, axis=1, keepdims=True), jnp.max(sp, axis=1, keepdims=True)), sink)
            pc = jnp.exp(sc - m)
            pp = jnp.exp(sp - m)
            den = jnp.sum(pc, axis=1, keepdims=True) + jnp.sum(pp, axis=1, keepdims=True) + jnp.exp(sink - m)
            o = (_nn(pc, vc[:, hs]) + _nn(pp, vp[:, hs])) / den
            lse = m + jnp.log(den)
            for r in range(GQ):
                outs.append(o[r * BLOCK:(r + 1) * BLOCK])
                lses.append(lse[r * BLOCK:(r + 1) * BLOCK])
        attn = jnp.concatenate(outs, axis=1)
        at_ref[...] = attn
        cat_ref[:, POOL_DIM:] = attn.astype(cat_ref.dtype)
        lane = _iota((BLOCK, LANES), 1)
        lrow = jnp.zeros((BLOCK, LANES), F32)
        for h in range(N_HEADS):
            lrow = jnp.where(lane == h, lses[h], lrow)
        lse_ref[...] = lrow

    cur = lambda w: pl.BlockSpec((BLOCK, w), lambda i: (i, 0))
    prv = lambda w: pl.BlockSpec((BLOCK, w), lambda i: (jnp.maximum(i - 1, 0), 0))
    return pl.pallas_call(
        body, grid=(nb,),
        in_specs=[cur(MIX_IN_DIM), prv(MIX_IN_DIM), cur(LANES), cur(LANES), prv(LANES), prv(LANES),
                  pl.BlockSpec((4, POOL_GROUP, POOL_GROUP), lambda i: (0, 0, 0)), pl.BlockSpec((1, POOL_DIM), lambda i: (0, 0)),
                  pl.BlockSpec(memory_space=pltpu.SMEM)],
        out_specs=[cur(2 * POOL_DIM), cur(Q_DIM), cur(LANES)],
        out_shape=[_sds((T, 2 * POOL_DIM), MXU), _sds((T, Q_DIM)), _sds((T, LANES))],
        compiler_params=_cp("parallel"), name=name)(proj, proj, cos, sin_s, cos, sin_s, pool_w, pool_scale, sinks)


def _mixcore_bwd(proj, cos, sin_s, pool_w, pool_scale, sinks, attn, lse, dcat, name):
    T = proj.shape[0]
    nb = T // BLOCK
    scale = HEAD_DIM ** -0.5
    QO, KO, VO = POOL_DIM, POOL_DIM + Q_DIM, POOL_DIM + Q_DIM + KV_DIM

    def body(p_ref, pp_ref, pn_ref, c_ref, s_ref, cp_ref, sp_ref, cn_ref, sn_ref, pw_ref, ps_ref, sk_ref,
             at_ref, atn_ref, l_ref, ln_ref, d_ref, dn_ref, dp_ref, dpw_ref, dps_ref, dsk_ref):
        i = pl.program_id(0)
        has_prev = i > 0
        has_next = i < nb - 1
        cur = p_ref[...]
        prv = jnp.where(has_prev, pp_ref[...], 0.0)
        d_cur = d_ref[...]
        d_nxt = jnp.where(has_next, dn_ref[...], 0.0)

        tpos = (i * BLOCK + _iota((BLOCK, 1), 0) + 1).astype(F32)
        tpos2 = (i * BLOCK + _iota((2 * BLOCK, 1), 0) + 1).astype(F32)
        ps = ps_ref[...]
        dps_parts, dpw_parts = [], []
        for g, w in enumerate(POOL_WINDOWS):
            sl = slice(g * POOL_GROUP, (g + 1) * POOL_GROUP)
            pooled = _pool_sums(prv[:, sl], cur[:, sl], w) / jnp.minimum(tpos, float(w)) - cur[:, sl]
            mixed = _nn(pooled, pw_ref[g])
            dps_parts.append(jnp.sum(d_cur[:, sl] * mixed, axis=0, keepdims=True))
            dm2 = jnp.concatenate([d_cur[:, sl], d_nxt[:, sl]], axis=0) * ps[:, sl]
            dpw_parts.append(_tn(pooled, dm2[:BLOCK]))
            dpool2 = _nt(dm2, pw_ref[g])
            e = dpool2 / jnp.minimum(tpos2, float(w))
            sh = 1
            while sh < w:
                e = e + pltpu.roll(e, 2 * BLOCK - sh, 0)
                sh *= 2
            dp_ref[:, sl] = (e[:BLOCK] - dpool2[:BLOCK]).astype(dp_ref.dtype)
        dpsp = jnp.concatenate(dps_parts, axis=1)

        nxt = pn_ref[...]
        q = _rope(cur[:, QO:KO], c_ref[...], s_ref[...])
        qn = _rope(nxt[:, QO:KO], cn_ref[...], sn_ref[...])
        kc = _rope(cur[:, KO:VO], c_ref[...], s_ref[...])
        kp = _rope(prv[:, KO:VO], cp_ref[...], sp_ref[...])
        vc, vp = cur[:, VO:], prv[:, VO:]
        do, don = d_cur[:, POOL_DIM:], d_nxt[:, POOL_DIM:]
        dl = do * at_ref[...]
        dln = don * atn_ref[...]
        lse, lsen = l_ref[...], ln_ref[...]
        ri = _iota((GQ * BLOCK, BLOCK), 0) % BLOCK
        cj = _iota((GQ * BLOCK, BLOCK), 1)
        mc = cj <= ri
        mp = jnp.logical_and(cj > ri, has_prev)
        mn = jnp.logical_and(cj > ri, has_next)
        dq_parts, dk_parts, dv_parts, dsk_vals = [], [], [], []
        for g in range(N_KV_HEADS):
            hs = slice(g * HEAD_DIM, (g + 1) * HEAD_DIM)
            qg, qng = _stack_heads(q, g) * scale, _stack_heads(qn, g) * scale
            dog, dong = _stack_heads(do, g), _stack_heads(don, g)
            delta = jnp.sum(_stack_heads(dl, g), axis=1, keepdims=True)
            deltan = jnp.sum(_stack_heads(dln, g), axis=1, keepdims=True)
            lg, lng = _stack_cols(lse, g), _stack_cols(lsen, g)
            pc = jnp.where(mc, jnp.exp(_nt(qg, kc[:, hs]) - lg), 0.0)
            pp = jnp.where(mp, jnp.exp(_nt(qg, kp[:, hs]) - lg), 0.0)
            pn = jnp.where(mn, jnp.exp(_nt(qng, kc[:, hs]) - lng), 0.0)
            dsc = pc * (_nt(dog, vc[:, hs]) - delta)
            dsp = pp * (_nt(dog, vp[:, hs]) - delta)
            dsn = pn * (_nt(dong, vc[:, hs]) - deltan)
            dqg = (_nn(dsc, kc[:, hs]) + _nn(dsp, kp[:, hs])) * scale
            dq_parts += [dqg[r * BLOCK:(r + 1) * BLOCK] for r in range(GQ)]
            dk_parts.append(_tn(dsc, qg) + _tn(dsn, qng))
            dv_parts.append(_tn(pc, dog) + _tn(pn, dong))
            sink = jnp.concatenate([jnp.full((BLOCK, 1), sk_ref[GQ * g + r], F32) for r in range(GQ)], axis=0)
            dsk = -jnp.exp(sink - lg) * delta
            dsk_vals += [jnp.sum(dsk[r * BLOCK:(r + 1) * BLOCK], axis=0, keepdims=True) for r in range(GQ)]
        dq = _rope(jnp.concatenate(dq_parts, axis=1), c_ref[...], s_ref[...], inverse=True)
        dk = _rope(jnp.concatenate(dk_parts, axis=1), c_ref[...], s_ref[...], inverse=True)
        dp_ref[:, QO:KO] = dq.astype(dp_ref.dtype)
        dp_ref[:, KO:VO] = dk.astype(dp_ref.dtype)
        dp_ref[:, VO:] = jnp.concatenate(dv_parts, axis=1).astype(dp_ref.dtype)
        lane = _iota((1, LANES), 1)
        dskp = jnp.zeros((1, LANES), F32)
        for h in range(N_HEADS):
            dskp = jnp.where(lane == h, dsk_vals[h], dskp)

        @pl.when(i == 0)
        def _():
            dps_ref[...] = dpsp
            dsk_ref[...] = dskp
            for g in range(4):
                dpw_ref[g] = dpw_parts[g]

        @pl.when(i > 0)
        def _():
            dps_ref[...] += dpsp
            dsk_ref[...] += dskp
            for g in range(4):
                dpw_ref[g] += dpw_parts[g]

    cur = lambda w: pl.BlockSpec((BLOCK, w), lambda i: (i, 0))
    prv = lambda w: pl.BlockSpec((BLOCK, w), lambda i: (jnp.maximum(i - 1, 0), 0))
    nxt = lambda w: pl.BlockSpec((BLOCK, w), lambda i: (jnp.minimum(i + 1, nb - 1), 0))
    return pl.pallas_call(
        body, grid=(nb,),
        in_specs=[cur(MIX_IN_DIM), prv(MIX_IN_DIM), nxt(MIX_IN_DIM),
                  cur(LANES), cur(LANES), prv(LANES), prv(LANES), nxt(LANES), nxt(LANES),
                  pl.BlockSpec((4, POOL_GROUP, POOL_GROUP), lambda i: (0, 0, 0)), pl.BlockSpec((1, POOL_DIM), lambda i: (0, 0)),
                  pl.BlockSpec(memory_space=pltpu.SMEM),
                  cur(Q_DIM), nxt(Q_DIM), cur(LANES), nxt(LANES), cur(2 * POOL_DIM), nxt(2 * POOL_DIM)],
        out_specs=[cur(MIX_IN_DIM), pl.BlockSpec((4, POOL_GROUP, POOL_GROUP), lambda i: (0, 0, 0)),
                   pl.BlockSpec((1, POOL_DIM), lambda i: (0, 0)), pl.BlockSpec((1, LANES), lambda i: (0, 0))],
        out_shape=[_sds((T, MIX_IN_DIM), MXU), _sds((4, POOL_GROUP, POOL_GROUP)), _sds((1, POOL_DIM)), _sds((1, LANES))],
        compiler_params=_cp("arbitrary"), name=name)(
            proj, proj, proj, cos, sin_s, cos, sin_s, cos, sin_s, pool_w, pool_scale, sinks, attn, attn, lse, lse, dcat, dcat)


SSM_TC = 128
GROUP_W = SSM_D_INNER // SSM_GROUPS
PERM_W = GROUP_W + 2 * SSM_STATE


def _perm_col(n):
    nx = SSM_D_INNER // SSM_TC
    nbt = SSM_GROUPS
    x_idx = (n // 2) * 4 + n % 2
    b_idx = (n - nx) * 4 + 2
    c_idx = (n - nx - nbt) * 4 + 3
    return jnp.where(n < nx, x_idx, jnp.where(n < nx + nbt, b_idx, c_idx))


def _ssm_pre_fwd(xbc, cw, cb, name):
    T = xbc.shape[0]
    tm = min(T, 1024)
    K = SSM_CONV
    q = tm // SUBLANES

    def body(x_ref, xp_ref, w_ref, b_ref, o_ref):
        prev8 = jnp.where(pl.program_id(0) > 0, xp_ref[...], 0.0)
        o_ref[...] = _silu(_conv_rows(x_ref[...], prev8, w_ref[...], b_ref[...], K))

    tc = 512
    return pl.pallas_call(
        body, grid=(T // tm, SSM_CONV_DIM // tc),
        in_specs=[pl.BlockSpec((tm, tc), lambda i, j: (i, j)),
                  pl.BlockSpec((SUBLANES, tc), lambda i, j: (jnp.maximum(i * q - 1, 0), j)),
                  pl.BlockSpec((K, tc), lambda i, j: (0, j)), pl.BlockSpec((1, tc), lambda i, j: (0, j))],
        out_specs=pl.BlockSpec((tm, tc), lambda i, j: (i, j)), out_shape=_sds((T, SSM_CONV_DIM)),
        compiler_params=_cp("parallel", "parallel"), name=name)(xbc, xbc, cw, cb)


def _ssm_pre_bwd(xbc, cw, cb, dact_perm, name):
    T = xbc.shape[0]
    tm = min(T, 1024)
    nt = T // tm
    K = SSM_CONV
    q = tm // SUBLANES
    tc = SSM_TC

    def body(x_ref, xp_ref, xn_ref, d_ref, dn_ref, w_ref, b_ref, dx_ref, dw_ref, db_ref):
        i = pl.program_id(1)
        w = w_ref[...]
        b = b_ref[...]
        cur = x_ref[...]
        prev8 = jnp.where(i > 0, xp_ref[...], 0.0)
        nxt8 = xn_ref[...]
        d_cur = d_ref[...] * _dsilu(_conv_rows(cur, prev8, w, b, K))
        d_nxt = jnp.where(i == nt - 1, 0.0, dn_ref[...] * _dsilu(_conv_rows(nxt8, cur[tm - SUBLANES:], w, b, K)))
        dx = d_cur * w[K - 1:K, :]
        for s in range(1, K):
            dx = dx + _shift_up(d_cur, d_nxt, s) * w[K - 1 - s:K - s, :]
        dx_ref[...] = dx.astype(dx_ref.dtype)
        dwp = jnp.concatenate([jnp.sum(d_cur * _shift_down(cur, prev8, K - 1 - k), axis=0, keepdims=True) for k in range(K)], axis=0)
        dbp = jnp.sum(d_cur, axis=0, keepdims=True)

        @pl.when(i == 0)
        def _():
            dw_ref[...] = dwp
            db_ref[...] = dbp

        @pl.when(i > 0)
        def _():
            dw_ref[...] += dwp
            db_ref[...] += dbp

    nxt_row = lambda i: jnp.minimum((i + 1) * q, nt * q - 1)
    return pl.pallas_call(
        body, grid=(SSM_CONV_DIM // tc, nt),
        in_specs=[pl.BlockSpec((tm, tc), lambda j, i: (i, j)),
                  pl.BlockSpec((SUBLANES, tc), lambda j, i: (jnp.maximum(i * q - 1, 0), j)),
                  pl.BlockSpec((SUBLANES, tc), lambda j, i: (nxt_row(i), j)),
                  pl.BlockSpec((tm, tc), lambda j, i: (i, _perm_col(j))),
                  pl.BlockSpec((SUBLANES, tc), lambda j, i: (nxt_row(i), _perm_col(j))),
                  pl.BlockSpec((K, tc), lambda j, i: (0, j)), pl.BlockSpec((1, tc), lambda j, i: (0, j))],
        out_specs=[pl.BlockSpec((tm, tc), lambda j, i: (i, j)), pl.BlockSpec((K, tc), lambda j, i: (0, j)),
                   pl.BlockSpec((1, tc), lambda j, i: (0, j))],
        out_shape=[_sds((T, SSM_CONV_DIM), MXU), _sds((K, SSM_CONV_DIM)), _sds((1, SSM_CONV_DIM))],
        compiler_params=_cp("parallel", "arbitrary"), name=name)(xbc, xbc, xbc, dact_perm, dact_perm, cw, cb)


def _dot_hi(a, b):
    return jnp.dot(a, b, precision=HI, preferred_element_type=F32)


def _ssd_common(dtraw, bias, alog):
    L = SSM_CHUNK
    xb = dtraw + bias
    dt = jnp.maximum(xb, 0.0) + jnp.log1p(jnp.exp(-jnp.abs(xb)))
    A = -jnp.exp(alog)
    tril = (_iota((L, L), 1) <= _iota((L, L), 0)).astype(F32)
    acs = _dot_hi(tril, dt * A)
    return xb, dt, A, tril, acs


def _head_selectors():
    es = (_iota((LANES, SSM_D_INNER), 0) == _iota((LANES, SSM_D_INNER), 1) // HEAD_DIM).astype(BF16)
    est = (_iota((SSM_D_INNER, LANES), 1) == _iota((SSM_D_INNER, LANES), 0) // HEAD_DIM).astype(BF16)
    return es, est


def _dot_sel(v, sel):
    hi = v.astype(BF16)
    r1 = v - hi.astype(F32)
    mid = r1.astype(BF16)
    lo = (r1 - mid.astype(F32)).astype(BF16)
    d = lambda a: jnp.dot(a, sel, preferred_element_type=F32)
    return (d(hi) + d(mid)) + d(lo)


def _expand_heads(v, es):
    return _dot_sel(v, es)


def _reduce_heads(q, est):
    return _dot_sel(q, est)


def _per_state_row(v, g):
    return jnp.concatenate([jnp.broadcast_to(v[:, GQ * g + r:GQ * g + r + 1], (HEAD_DIM, 1)) for r in range(GQ)], axis=0)


def _ssd_fwd(xact, dtraw, dt_bias, a_log, name):
    T = xact.shape[0]
    nc = T // SSM_CHUNK
    L = SSM_CHUNK
    BO, CO = SSM_D_INNER, SSM_D_INNER + SSM_GROUPS * SSM_STATE

    def body(x_ref, dt_ref, bias_ref, al_ref, es_ref, y_ref, st_ref, state):
        @pl.when(pl.program_id(0) == 0)
        def _():
            state[...] = jnp.zeros(state.shape, F32)

        _, dt, A, tril, acs = _ssd_common(dt_ref[...], bias_ref[...], al_ref[...])
        acsT = acs.T
        last = acs[L - 1:L, :]
        cd = jnp.exp(last)
        es = es_ref[...]
        dtX = _expand_heads(dt, es)
        EX = _expand_heads(jnp.exp(acs), es)
        decX = _expand_heads(jnp.exp(last - acs), es)
        for g in range(SSM_GROUPS):
            gs = slice(g * GROUP_W, (g + 1) * GROUP_W)
            B = x_ref[:, BO + g * SSM_STATE:BO + (g + 1) * SSM_STATE]
            C = x_ref[:, CO + g * SSM_STATE:CO + (g + 1) * SSM_STATE]
            X = x_ref[:, gs] * dtX[:, gs]
            CB = _nt(C, B)
            yd = []
            for r in range(GQ):
                h = GQ * g + r
                Lm = jnp.exp(jnp.where(tril > 0, acs[:, h:h + 1] - acsT[h:h + 1, :], NEG))
                yd.append(_nn(CB * Lm, X[:, r * HEAD_DIM:(r + 1) * HEAD_DIM]))
            S = state[g]
            st_ref[g] = S
            y_ref[:, gs] = jnp.concatenate(yd, axis=1) + _nt(C, S) * EX[:, gs]
            state[g] = S * _per_state_row(cd, g) + _tn(X * decX[:, gs], B)

    es, _ = _head_selectors()
    return pl.pallas_call(
        body, grid=(nc,),
        in_specs=[pl.BlockSpec((L, SSM_CONV_DIM), lambda c: (c, 0)), pl.BlockSpec((L, LANES), lambda c: (c, 0)),
                  pl.BlockSpec((1, LANES), lambda c: (0, 0)), pl.BlockSpec((1, LANES), lambda c: (0, 0)),
                  pl.BlockSpec((LANES, SSM_D_INNER), lambda c: (0, 0))],
        out_specs=[pl.BlockSpec((L, SSM_D_INNER), lambda c: (c, 0)),
                   pl.BlockSpec((None, SSM_GROUPS, GROUP_W, SSM_STATE), lambda c: (c, 0, 0, 0))],
        out_shape=[_sds((T, SSM_D_INNER)), _sds((nc, SSM_GROUPS, GROUP_W, SSM_STATE))],
        scratch_shapes=[pltpu.VMEM((SSM_GROUPS, GROUP_W, SSM_STATE), F32)],
        compiler_params=_cp("arbitrary"), name=name)(xact, dtraw, dt_bias, a_log, es)


def _ssd_bwd(xact, dtraw, dt_bias, a_log, d_skip, states, dy, name):
    T = xact.shape[0]
    nc = T // SSM_CHUNK
    L = SSM_CHUNK
    BO, CO = SSM_D_INNER, SSM_D_INNER + SSM_GROUPS * SSM_STATE

    def body(x_ref, dt_ref, bias_ref, al_ref, dsk_ref, es_ref, est_ref, st_ref, dy_ref,
             dxp_ref, ddt_ref, dbias_ref, dal_ref, dd_ref, dstate, qa, qx):
        cc = pl.program_id(0)

        @pl.when(cc == 0)
        def _():
            dstate[...] = jnp.zeros(dstate.shape, F32)

        xb, dt, A, tril, acs = _ssd_common(dt_ref[...], bias_ref[...], al_ref[...])
        acsT = acs.T
        last = acs[L - 1:L, :]
        cd = jnp.exp(last)
        es, est = es_ref[...], est_ref[...]
        dtX = _expand_heads(dt, es)
        EX = _expand_heads(jnp.exp(acs), es)
        decX = _expand_heads(jnp.exp(last - acs), es)
        lane1 = _iota((1, LANES), 1)
        lane = _iota((L, LANES), 1)
        sub = _iota((L, LANES), 0)
        ztot = jnp.zeros((1, LANES), F32)
        wrow = jnp.zeros((L, LANES), F32)
        wcolT = jnp.zeros((LANES, L), F32)
        rows_dec, rows_dd = [], []
        for g in range(SSM_GROUPS):
            gs = slice(g * GROUP_W, (g + 1) * GROUP_W)
            x = x_ref[:, gs]
            B = x_ref[:, BO + g * SSM_STATE:BO + (g + 1) * SSM_STATE]
            C = x_ref[:, CO + g * SSM_STATE:CO + (g + 1) * SSM_STATE]
            dY = dy_ref[:, gs]
            dtx, e_x, dec_x = dtX[:, gs], EX[:, gs], decX[:, gs]
            X = x * dtx
            CB = _nt(C, B)
            S = st_ref[g]
            dS_out = dstate[g]
            dcb_sum = jnp.zeros((L, L), F32)
            dxd = []
            for r in range(GQ):
                h = GQ * g + r
                hs = slice(r * HEAD_DIM, (r + 1) * HEAD_DIM)
                Lm = jnp.exp(jnp.where(tril > 0, acs[:, h:h + 1] - acsT[h:h + 1, :], NEG))
                M = CB * Lm
                dM = _nt(dY[:, hs], X[:, hs])
                dxd.append(_tn(M, dY[:, hs]))
                dcb_sum = dcb_sum + dM * Lm
                Wm = dM * M
                wrow = jnp.where(lane == h, jnp.sum(Wm, axis=1, keepdims=True), wrow)
                wcolT = jnp.where(sub == h, jnp.sum(Wm, axis=0, keepdims=True), wcolT)
            dXd = jnp.concatenate(dxd, axis=1)
            G = _nt(C, S)
            dG = dY * e_x
            dDX = _nt(B, dS_out)
            dX = dXd + dec_x * dDX
            t_dec = dDX * X * dec_x
            qa[:, gs] = dG * G - t_dec
            qx[:, gs] = dX * x
            rows_dec.append(jnp.sum(t_dec, axis=0, keepdims=True))
            rows_dd.append(jnp.sum(dY * x, axis=0, keepdims=True))
            zc = jnp.sum(dS_out * S, axis=1, keepdims=True)
            for r in range(GQ):
                ztot = jnp.where(lane1 == GQ * g + r, jnp.sum(zc[r * HEAD_DIM:(r + 1) * HEAD_DIM], axis=0, keepdims=True), ztot)
            dxp_ref[:, g * PERM_W:g * PERM_W + GROUP_W] = dX * dtx + dY * dsk_ref[:, gs]
            dxp_ref[:, g * PERM_W + GROUP_W:g * PERM_W + GROUP_W + SSM_STATE] = _tn(dcb_sum, C) + _nn(X * dec_x, dS_out)
            dxp_ref[:, g * PERM_W + GROUP_W + SSM_STATE:(g + 1) * PERM_W] = _nn(dcb_sum, B) + _nn(dG, S)
            dstate[g] = dS_out * _per_state_row(cd, g) + _tn(dG, C)
        rows = jnp.concatenate([jnp.concatenate(rows_dec, axis=1), jnp.concatenate(rows_dd, axis=1)]
                               + [jnp.zeros((SUBLANES - 2, SSM_D_INNER), F32)], axis=0)
        rsum = _reduce_heads(rows, est)
        dlast = rsum[0:1, :] + cd * ztot
        dacs = (wrow - wcolT.T) + _reduce_heads(qa[...], est) + jnp.where(sub == L - 1, dlast, 0.0)
        triu = (_iota((L, L), 0) <= _iota((L, L), 1)).astype(F32)
        da = _dot_hi(triu, dacs)
        ddtraw = (da * A + _reduce_heads(qx[...], est)) * (1.0 / (1.0 + jnp.exp(-xb)))
        ddt_ref[...] = ddtraw
        dal = jnp.sum(da * dt, axis=0, keepdims=True) * A
        ddp = rsum[1:2, :]
        dbp = jnp.sum(ddtraw, axis=0, keepdims=True)

        @pl.when(cc == 0)
        def _():
            dbias_ref[...] = dbp
            dal_ref[...] = dal
            dd_ref[...] = ddp

        @pl.when(cc > 0)
        def _():
            dbias_ref[...] += dbp
            dal_ref[...] += dal
            dd_ref[...] += ddp

    rc = lambda c: nc - 1 - c
    vec = pl.BlockSpec((1, LANES), lambda c: (0, 0))
    es, est = _head_selectors()
    return pl.pallas_call(
        body, grid=(nc,),
        in_specs=[pl.BlockSpec((L, SSM_CONV_DIM), lambda c: (rc(c), 0)), pl.BlockSpec((L, LANES), lambda c: (rc(c), 0)), vec, vec,
                  pl.BlockSpec((1, SSM_D_INNER), lambda c: (0, 0)),
                  pl.BlockSpec((LANES, SSM_D_INNER), lambda c: (0, 0)), pl.BlockSpec((SSM_D_INNER, LANES), lambda c: (0, 0)),
                  pl.BlockSpec((None, SSM_GROUPS, GROUP_W, SSM_STATE), lambda c: (rc(c), 0, 0, 0)),
                  pl.BlockSpec((L, SSM_D_INNER), lambda c: (rc(c), 0))],
        out_specs=[pl.BlockSpec((L, SSM_GROUPS * PERM_W), lambda c: (rc(c), 0)),
                   pl.BlockSpec((L, LANES), lambda c: (rc(c), 0)), vec, vec, vec],
        out_shape=[_sds((T, SSM_GROUPS * PERM_W)), _sds((T, LANES)), _sds((1, LANES)), _sds((1, LANES)), _sds((1, LANES))],
        scratch_shapes=[pltpu.VMEM((SSM_GROUPS, GROUP_W, SSM_STATE), F32), pltpu.VMEM((L, SSM_D_INNER), F32),
                        pltpu.VMEM((L, SSM_D_INNER), F32)],
        compiler_params=_cp("arbitrary"), name=name)(xact, dtraw, dt_bias, a_log, d_skip, es, est, states, dy)


def _ssm_post_fwd(y, xact, z, d_skip, nw, name):
    T = y.shape[0]
    tm = min(T, 256)
    W = SSM_D_INNER

    def body(y_ref, x_ref, z_ref, d_ref, w_ref, o_ref):
        y2 = (y_ref[...] + d_ref[...] * x_ref[...]) * _silu(z_ref[...])
        r = lax.rsqrt(jnp.mean(y2 * y2, axis=-1, keepdims=True) + SSM_NORM_EPS)
        o_ref[...] = (y2 * r * w_ref[...]).astype(o_ref.dtype)

    row = pl.BlockSpec((tm, W), lambda i: (i, 0))
    vec = pl.BlockSpec((1, W), lambda i: (0, 0))
    return pl.pallas_call(
        body, grid=(T // tm,), in_specs=[row, row, row, vec, vec], out_specs=row, out_shape=_sds((T, W), MXU),
        compiler_params=_cp("parallel"), name=name)(y, xact, z, d_skip, nw)


def _ssm_post_bwd(y, xact, z, d_skip, nw, dyn, name):
    T = y.shape[0]
    tm = min(T, 256)
    W = SSM_D_INNER

    def body(y_ref, x_ref, z_ref, d_ref, w_ref, dn_ref, dyg_ref, dz_ref, dw_ref):
        zv = z_ref[...]
        sz = _silu(zv)
        yg = y_ref[...] + d_ref[...] * x_ref[...]
        y2 = yg * sz
        r = lax.rsqrt(jnp.mean(y2 * y2, axis=-1, keepdims=True) + SSM_NORM_EPS)
        y2h = y2 * r
        dn = dn_ref[...]
        gy = dn * w_ref[...]
        dy2 = r * (gy - y2h * jnp.mean(gy * y2h, axis=-1, keepdims=True))
        dyg_ref[...] = dy2 * sz
        dz_ref[...] = (dy2 * yg * _dsilu(zv)).astype(dz_ref.dtype)
        part = jnp.sum(dn * y2h, axis=0, keepdims=True)

        @pl.when(pl.program_id(0) == 0)
        def _():
            dw_ref[...] = part

        @pl.when(pl.program_id(0) > 0)
        def _():
            dw_ref[...] += part

    row = pl.BlockSpec((tm, W), lambda i: (i, 0))
    vec = pl.BlockSpec((1, W), lambda i: (0, 0))
    return pl.pallas_call(
        body, grid=(T // tm,), in_specs=[row, row, row, vec, vec, row], out_specs=[row, row, vec],
        out_shape=[_sds((T, W)), _sds((T, W), MXU), _sds((1, W))],
        compiler_params=_cp("arbitrary"), name=name)(y, xact, z, d_skip, nw, dyn)


def _local_step(x0, cos, sin_s, target, P):
    mmf = functools.partial(_mm, tm=1024)
    big, small = {}, {}
    h0 = _rmsnorm_fwd(x0, P["nm"][0], "norm_mix0")
    proj0 = mmf(h0, P["wmi"], tn=1280, tk=1024, name="mix_in")
    cat, attn, lse = _mixcore_fwd(proj0, cos, sin_s, P["pool_w"], P["pool_scale"], P["sinks"], "mixcore_fwd")
    x1 = mmf(cat, P["wmo"], tn=1024, tk=1024, res=x0, name="mix_out")

    def ffn_fwd(xin, i):
        hf = _rmsnorm_fwd(xin, P["nf"][i], f"norm_ffn{i}")
        hid = mmf(hf, P["wup"][i], tn=1408, tk=1024, name=f"ffn_up{i}")
        act = _ffn_mid_fwd(hid, P["fcw"][i], P["fcb"][i], f"ffn_mid_fwd{i}")
        xout = mmf(act, P["wdn"][i], tn=1024, tk=D_FF, res=xin, name=f"ffn_down{i}")
        return hf, hid, act, xout

    hf0, hid0, act0, x2 = ffn_fwd(x1, 0)
    h1 = _rmsnorm_fwd(x2, P["nm"][1], "norm_mix1")
    z = mmf(h1, P["wz"], tn=1024, tk=1024, name="ssm_in_z")
    xbc = mmf(h1, P["wxbc"], tn=1024, tk=1024, name="ssm_in_xbc")
    dtraw = mmf(h1, P["wdt"], tn=128, tk=1024, name="ssm_in_dt")
    xact = _ssm_pre_fwd(xbc, P["scw"], P["scb"], "ssm_pre_fwd")
    y, states = _ssd_fwd(xact, dtraw, P["dt_bias"], P["a_log"], "ssd_fwd")
    yn = _ssm_post_fwd(y, xact, z, P["d_exp"], P["snorm"], "ssm_post_fwd")
    x3 = mmf(yn, P["wso"], tn=1024, tk=SSM_D_INNER, res=x2, name="ssm_out")
    hf1, hid1, act1, x4 = ffn_fwd(x3, 1)
    loss_row, dx4, d_nfin = _loss_head(x4, P["nfin"], target, "loss_head")
    small["norm_final"] = d_nfin

    def ffn_bwd(xin, dxo, hf, hid, act, i):
        da = mmf(dxo, P["wdn"][i], tb=True, tn=1408, tk=1024, name=f"ffn_down_dx{i}")
        big[f"ffn_w_down{i}"] = _mm(act, dxo, ta=True, tm=1408, tn=1024, tk=1024, name=f"ffn_down_dw{i}").reshape(N_CHIPS, D_FF // N_CHIPS, D_MODEL)
        dhid, dcw, dcb = _ffn_mid_bwd(hid, P["fcw"][i], P["fcb"][i], da, f"ffn_mid_bwd{i}")
        dhf = mmf(dhid, P["wup"][i], tb=True, tn=1024, tk=1408, name=f"ffn_up_dx{i}")
        big[f"ffn_w_up{i}"] = _mm(hf, dhid, ta=True, tm=1024, tn=1408, tk=1024, out_shard_perm=(0, 2, 1, 3), name=f"ffn_up_dw{i}")
        dxi, dnf = _rmsnorm_bwd(xin, P["nf"][i], dhf, dxo, f"norm_ffn_bwd{i}")
        return dxi, dnf, dcw, dcb

    dx3, dnf1, dfcw1, dfcb1 = ffn_bwd(x3, dx4, hf1, hid1, act1, 1)
    dyn = mmf(dx3, P["wso"], tb=True, tn=1024, tk=1024, name="ssm_out_dx")
    big["ssm_w_out"] = _mm(yn, dx3, ta=True, tm=1024, tn=1024, tk=1024, name="ssm_out_dw").reshape(N_CHIPS, SSM_D_INNER // N_CHIPS, D_MODEL)
    dyg, dz, d_snorm = _ssm_post_bwd(y, xact, z, P["d_exp"], P["snorm"], dyn, "ssm_post_bwd")
    dxact_p, ddtraw, d_dtb, d_alog, d_dskip = _ssd_bwd(xact, dtraw, P["dt_bias"], P["a_log"], P["d_exp"], states, dyg, "ssd_bwd")
    dxbc, d_scw, d_scb = _ssm_pre_bwd(xbc, P["scw"], P["scb"], dxact_p, "ssm_pre_bwd")
    dh1 = mmf(dz, P["wz"], tb=True, tn=1024, tk=1024, name="ssm_in_dx_z")
    dh1 = mmf(dxbc, P["wxbc"], tb=True, tn=1024, tk=1024, res=dh1, name="ssm_in_dx_xbc")
    dh1 = mmf(ddtraw, P["wdt"], tb=True, tn=1024, tk=128, res=dh1, name="ssm_in_dx_dt")
    dwz = _mm(h1, dz, ta=True, tm=1024, tn=1024, tk=1024, name="ssm_in_dw_z")
    dwxbc = _mm(h1, dxbc, ta=True, tm=1024, tn=1024, tk=1024, name="ssm_in_dw_xbc")
    dwdt = _mm(h1, ddtraw, ta=True, tm=1024, tn=128, tk=1024, name="ssm_in_dw_dt")
    dwsi = jnp.concatenate([dwz, dwxbc, dwdt[:, :SSM_HEADS]], axis=1)
    big["ssm_w_in"] = dwsi.reshape(D_MODEL, N_CHIPS, SSM_IN_DIM // N_CHIPS).transpose(1, 0, 2)
    dx2, dnm1 = _rmsnorm_bwd(x2, P["nm"][1], dh1, dx3, "norm_mix_bwd1")
    dx1, dnf0, dfcw0, dfcb0 = ffn_bwd(x1, dx2, hf0, hid0, act0, 0)
    dcat = mmf(dx1, P["wmo"], tb=True, tn=1024, tk=1024, name="mix_out_dx")
    big["mix_w_out"] = _mm(cat, dx1, ta=True, tm=1024, tn=1024, tk=1024, name="mix_out_dw").reshape(N_CHIPS, D_MODEL // N_CHIPS, D_MODEL)
    dproj0, d_pw, d_ps, d_sk = _mixcore_bwd(proj0, cos, sin_s, P["pool_w"], P["pool_scale"], P["sinks"], attn, lse, dcat, "mixcore_bwd")
    dh0 = mmf(dproj0, P["wmi"], tb=True, tn=1024, tk=1280, name="mix_in_dx")
    dwmi = _mm(h0, dproj0, ta=True, tm=1024, tn=1280, tk=1024, name="mix_in_dw")
    big["mix_w_in"] = dwmi.reshape(D_MODEL, N_CHIPS, MIX_IN_DIM // N_CHIPS).transpose(1, 0, 2)
    dx0, dnm0 = _rmsnorm_bwd(x0, P["nm"][0], dh0, dx1, "norm_mix_bwd0")

    def unperm_cols(a):
        r = a.shape[0]
        t = a.reshape(r, N_CHIPS, FFN_TC)
        return jnp.stack([t[:, p] for p in _PERM], axis=0)

    small["norm_mix"] = jnp.concatenate([dnm0, dnm1], axis=0)
    small["norm_ffn"] = jnp.concatenate([dnf0, dnf1], axis=0)
    small["pool_w"] = d_pw.reshape(4 * POOL_GROUP, POOL_GROUP)
    small["pool_scale"] = d_ps
    small["attn_sinks"] = d_sk
    small["ssm_dt_bias"] = d_dtb
    small["ssm_A_log"] = d_alog
    small["ssm_D"] = d_dskip
    fcb = jnp.stack([unperm_cols(dfcb0), unperm_cols(dfcb1)], axis=0)
    small["ffn_conv_b"] = fcb.reshape(2, 2 * D_FF)
    small["ssm_conv_w"] = d_scw.reshape(SSM_CONV, N_CHIPS, SSM_CONV_DIM // N_CHIPS).transpose(1, 0, 2)
    small["ssm_conv_b"] = d_scb.reshape(N_CHIPS, 1, SSM_CONV_DIM // N_CHIPS)
    small["ssm_norm"] = d_snorm.reshape(N_CHIPS, 1, SSM_D_INNER // N_CHIPS)
    small["ffn_conv_w"] = jnp.concatenate([unperm_cols(dfcw0), unperm_cols(dfcw1)], axis=1)
    return loss_row, dx0, big, small


ANY = pl.BlockSpec(memory_space=pl.ANY)


def _place():
    return lax.axis_index("x"), lax.axis_index("y"), lax.axis_index("c")


def _gather_shards(shards, name):
    n = len(shards)
    split = [s.size >= (1 << 16) for s in shards]

    def half(ref, a, h):
        shp = shards[a].shape
        if len(shp) == 3:
            return ref.at[h]
        r2 = shp[0] // 2
        return ref.at[pl.ds(pl.multiple_of(h * r2, 2 * SUBLANES), r2), :]

    def body(*refs):
        ins, outs = refs[:n], refs[n:2 * n]
        send, recv, fsend, frecv = refs[2 * n:]
        x, y, c = _place()
        k = 2 * x + y
        chips = [(1 - x, y), (x, 1 - y), (1 - x, 1 - y)]

        def ici(a, j, src_slot_ref, dst_slot):
            px, py = chips[j]
            src = half(src_slot_ref, a, c) if split[a] else src_slot_ref
            dst = half(outs[a].at[dst_slot], a, c) if split[a] else outs[a].at[dst_slot]
            return pltpu.make_async_remote_copy(src, dst, send.at[a, j], recv.at[a, j], device_id=(px, py, c), device_id_type=MESH)

        def d2d(a, j, h):
            px, py = chips[j]
            part = half(outs[a].at[2 * px + py], a, h)
            return pltpu.make_async_remote_copy(part, part, fsend.at[a, j], frecv.at[a, j], device_id=(x, y, 1 - c), device_id_type=MESH)

        sends = [ici(a, j, ins[a], k) for a in range(n) for j in range(3)]
        for cp in sends:
            cp.start()
        passed = []
        for a in range(n):
            for j, (px, py) in enumerate(chips):
                ici(a, j, ins[a], 2 * px + py).wait_recv()
                if split[a]:
                    passed.append(d2d(a, j, c))
                    passed[-1].start()
        for a in range(n):
            if split[a]:
                for j in range(3):
                    d2d(a, j, 1 - c).wait_recv()
        for cp in sends + passed:
            cp.wait_send()

    return pl.pallas_call(
        body, in_specs=[ANY] * n, out_specs=[ANY] * n,
        out_shape=[_sds((N_CHIPS,) + s.shape, s.dtype) for s in shards],
        scratch_shapes=[pltpu.SemaphoreType.DMA((n, 3))] * 4,
        compiler_params=pltpu.CompilerParams(has_side_effects=True), name=name)(*shards)


def _pair_exchange(gs, name):
    n = len(gs)

    def body(*refs):
        ins, outs = refs[:n], refs[n:2 * n]
        send, recv = refs[2 * n:]
        x, y, c = _place()
        cps = []
        for a in range(n):
            r2 = gs[a].shape[1] // 2
            src = ins[a].at[:, pl.ds(pl.multiple_of((1 - c) * r2, SUBLANES), r2), :]
            cps.append(pltpu.make_async_remote_copy(src, outs[a], send.at[a], recv.at[a],
                                                    device_id=(x, y, 1 - c), device_id_type=MESH))
        for cp in cps:
            cp.start()
        for cp in cps:
            cp.wait()

    return pl.pallas_call(
        body, in_specs=[ANY] * n, out_specs=[ANY] * n,
        out_shape=[_sds((N_CHIPS, g.shape[1] // 2, g.shape[2]), g.dtype) for g in gs],
        scratch_shapes=[pltpu.SemaphoreType.DMA((n,)), pltpu.SemaphoreType.DMA((n,))],
        compiler_params=pltpu.CompilerParams(has_side_effects=True), name=name)(*gs)


def _chip_exchange(ps, name):
    n = len(ps)

    def body(*refs):
        ins, outs = refs[:n], refs[n:2 * n]
        send, recv = refs[2 * n:]
        x, y, c = _place()
        k = 2 * x + y
        chips = [(1 - x, y), (x, 1 - y), (1 - x, 1 - y)]
        sends = [pltpu.make_async_remote_copy(ins[a].at[2 * px + py], outs[a].at[k], send.at[a, j], recv.at[a, j],
                                              device_id=(px, py, c), device_id_type=MESH)
                 for a in range(n) for j, (px, py) in enumerate(chips)]
        for cp in sends:
            cp.start()
        for a in range(n):
            for j, (px, py) in enumerate(chips):
                pltpu.make_async_remote_copy(ins[a].at[k], outs[a].at[2 * px + py], send.at[a, j], recv.at[a, j],
                                             device_id=(px, py, c), device_id_type=MESH).wait_recv()
        for cp in sends:
            cp.wait_send()

    return pl.pallas_call(
        body, in_specs=[ANY] * n, out_specs=[ANY] * n, out_shape=[_sds(p.shape, p.dtype) for p in ps],
        scratch_shapes=[pltpu.SemaphoreType.DMA((n, 3)), pltpu.SemaphoreType.DMA((n, 3))],
        compiler_params=pltpu.CompilerParams(has_side_effects=True), name=name)(*ps)


def _half_exchange(fs, name):
    n = len(fs)

    def body(*refs):
        ins, outs = refs[:n], refs[n:2 * n]
        send, recv = refs[2 * n:]
        x, y, c = _place()
        cps = [pltpu.make_async_remote_copy(ins[a], outs[a], send.at[a], recv.at[a],
                                            device_id=(x, y, 1 - c), device_id_type=MESH) for a in range(n)]
        for cp in cps:
            cp.start()
        for cp in cps:
            cp.wait()

    return pl.pallas_call(
        body, in_specs=[ANY] * n, out_specs=[ANY] * n, out_shape=[_sds(f.shape, f.dtype) for f in fs],
        scratch_shapes=[pltpu.SemaphoreType.DMA((n,)), pltpu.SemaphoreType.DMA((n,))],
        compiler_params=pltpu.CompilerParams(has_side_effects=True), name=name)(*fs)


def _row_tile(rows, cols, budget=2 * 1024 * 1024, step=2 * SUBLANES):
    best = step
    for t in range(step, rows + 1, step):
        if rows % t == 0 and t * cols * 4 <= budget:
            best = t
    assert rows % best == 0, (rows, best)
    return best


def _pair_sum(g, got, cidx, name):
    _, R, C = g.shape
    r2 = R // 2
    tr = _row_tile(r2, C)
    nr = r2 // tr

    def body(c_ref, g_ref, o_ref_in, o_ref):
        o_ref[...] = (g_ref[...] + o_ref_in[...]).astype(o_ref.dtype)

    return pl.pallas_call(
        body,
        grid_spec=pltpu.PrefetchScalarGridSpec(
            num_scalar_prefetch=1, grid=(N_CHIPS, nr),
            in_specs=[pl.BlockSpec((None, tr, C), lambda k, i, c: (k, c[0] * nr + i, 0)),
                      pl.BlockSpec((None, tr, C), lambda k, i, c: (k, i, 0))],
            out_specs=pl.BlockSpec((None, tr, C), lambda k, i, c: (k, i, 0))),
        out_shape=_sds((N_CHIPS, r2, C), BF16), compiler_params=_cp("parallel", "parallel"), name=name)(cidx, g, got)


def _chip_sum(own, parts, kidx, name):
    _, r2, C = parts.shape
    tr = _row_tile(r2, C, budget=1024 * 1024)

    def body(k_ref, o_ref_in, p1_ref, p2_ref, p3_ref, o_ref):
        o_ref[...] = ((o_ref_in[...].astype(F32) + p1_ref[...].astype(F32)) + p2_ref[...].astype(F32)) + p3_ref[...].astype(F32)

    def slot(d):
        return pl.BlockSpec((None, tr, C), lambda i, k: ((k[0] + d) % N_CHIPS, i, 0))

    return pl.pallas_call(
        body,
        grid_spec=pltpu.PrefetchScalarGridSpec(
            num_scalar_prefetch=1, grid=(r2 // tr,), in_specs=[slot(0), slot(1), slot(2), slot(3)],
            out_specs=pl.BlockSpec((tr, C), lambda i, k: (i, 0))),
        out_shape=_sds((r2, C)), compiler_params=_cp("parallel"), name=name)(kidx, own, parts, parts, parts)


def _adamw_math(w, g, m, v):
    m2 = ADAM_B1 * m + (1.0 - ADAM_B1) * g
    v2 = ADAM_B2 * v + (1.0 - ADAM_B2) * (g * g)
    m_hat = m2 / (1.0 - ADAM_B1 ** ADAM_STEP)
    v_hat = v2 / (1.0 - ADAM_B2 ** ADAM_STEP)
    delta = -ADAM_LR * (m_hat / (jnp.sqrt(v_hat) + ADAM_EPS) + ADAM_WD * w)
    return delta, m2, v2


def _adamw(w, m, v, gparts, cidx, name):
    Lw, R, C = w.shape
    r2 = R // 2
    tr = _row_tile(r2, C, budget=1024 * 1024)
    nr = r2 // tr
    flat = [h for pair in gparts for h in pair]

    def body(*refs):
        c_ref = refs[0]
        w_ref, m_ref, v_ref = refs[1:4]
        g_refs = refs[4:4 + 2 * Lw]
        go_ref, d_ref, mo_ref, vo_ref = refs[4 + 2 * Lw:]
        mine = (pl.program_id(1) // nr) == c_ref[0]
        g = jnp.where(mine, g_refs[0][...], g_refs[1][...])
        for l in range(1, Lw):
            g = jnp.where(pl.program_id(0) == l, jnp.where(mine, g_refs[2 * l][...], g_refs[2 * l + 1][...]), g)
        d, m2, v2 = _adamw_math(w_ref[...], g, m_ref[...], v_ref[...])
        go_ref[...] = g
        d_ref[...] = d
        mo_ref[...] = m2
        vo_ref[...] = v2

    blk = pl.BlockSpec((None, tr, C), lambda l, i, c: (l, i, 0))
    gblk = pl.BlockSpec((tr, C), lambda l, i, c: (i % nr, 0))
    return pl.pallas_call(
        body,
        grid_spec=pltpu.PrefetchScalarGridSpec(
            num_scalar_prefetch=1, grid=(Lw, 2 * nr), in_specs=[blk, blk, blk] + [gblk] * (2 * Lw), out_specs=[blk] * 4),
        out_shape=[_sds((Lw, R, C))] * 4, compiler_params=_cp("parallel", "parallel"), name=name)(cidx, w, m, v, *flat)


def _small_reduce_adamw(items, loss_row, name):
    n = len(items)
    gshapes = [it[0].shape for it in items] + [loss_row.shape]
    pshapes = [it[1].shape for it in items]
    ng = n + 1

    def body(*refs):
        g_in = refs[:ng]
        wmv = refs[ng:ng + 3 * n]
        outs = refs[ng + 3 * n:ng + 3 * n + 4 * n + 1]
        bufs = refs[ng + 7 * n + 1:ng + 7 * n + 1 + ng]
        send, recv = refs[-2:]
        x, y, c = _place()
        me = 4 * x + 2 * y + c
        k = 2 * x + y
        flips = [(fx, fy, fc) for fx in (0, 1) for fy in (0, 1) for fc in (0, 1)][1:]

        def peer(f):
            return (x ^ f[0], y ^ f[1], c ^ f[2])

        def slot(p):
            return 4 * p[0] + 2 * p[1] + p[2]

        for a in range(ng):
            bufs[a][me] = g_in[a][...]
        sends = [pltpu.make_async_remote_copy(g_in[a], bufs[a].at[me], send.at[a, j], recv.at[a, j],
                                              device_id=peer(f), device_id_type=MESH)
                 for a in range(ng) for j, f in enumerate(flips)]
        for cp in sends:
            cp.start()
        for a in range(ng):
            for j, f in enumerate(flips):
                pltpu.make_async_remote_copy(g_in[a], bufs[a].at[slot(peer(f))], send.at[a, j], recv.at[a, j],
                                             device_id=peer(f), device_id_type=MESH).wait_recv()
        for cp in sends:
            cp.wait_send()
        for a in range(ng):
            sharded = len(gshapes[a]) == 3

            def part(d):
                return bufs[a][d, k] if sharded else bufs[a][d]

            tot = part(0)
            for d in range(1, N_DEV):
                tot = tot + part(d)
            if a == n:
                outs[4 * n][...] = tot
                continue
            pr, pc = pshapes[a]
            g = tot[:pr, :pc]
            w_ref, m_ref, v_ref = wmv[3 * a:3 * a + 3]
            d_, m2, v2 = _adamw_math(w_ref[...], g, m_ref[...], v_ref[...])
            outs[4 * a][...] = g
            outs[4 * a + 1][...] = d_
            outs[4 * a + 2][...] = m2
            outs[4 * a + 3][...] = v2

    vm = pl.BlockSpec(memory_space=pltpu.VMEM)
    args = [it[0] for it in items] + [loss_row]
    for it in items:
        args += [it[1], it[2], it[3]]
    out_shape = []
    for ps in pshapes:
        out_shape += [_sds(ps)] * 4
    out_shape.append(_sds(loss_row.shape))
    return pl.pallas_call(
        body, in_specs=[vm] * len(args), out_specs=[vm] * len(out_shape), out_shape=out_shape,
        scratch_shapes=[pltpu.VMEM((N_DEV,) + tuple(s), F32) for s in gshapes]
        + [pltpu.SemaphoreType.DMA((ng, N_DEV - 1)), pltpu.SemaphoreType.DMA((ng, N_DEV - 1))],
        compiler_params=pltpu.CompilerParams(has_side_effects=True, vmem_limit_bytes=V7X_VMEM_LIMIT), name=name)(*args)


_PERM = (0, 2, 1, 3)


def _cols_from_shards(g):
    return g.transpose(1, 0, 2).reshape(g.shape[1], N_CHIPS * g.shape[2])


def _rope_tables(positions):
    inv_freq = ROPE_THETA ** (-jnp.arange(0, HEAD_DIM, 2, dtype=F32) / HEAD_DIM)
    ang = positions.astype(F32).reshape(-1, 1) * inv_freq
    cos, sin = jnp.cos(ang), jnp.sin(ang)
    cos = jnp.concatenate([cos, cos, cos, cos], axis=-1)
    sin_s = jnp.concatenate([-sin, sin, -sin, sin], axis=-1)
    return cos, sin_s


def kernel(x, positions, norm_mix, norm_ffn, norm_final, mix_w_in, pool_w, pool_scale, attn_sinks, mix_w_out, ssm_w_in, ssm_conv_w, ssm_conv_b, ssm_dt_bias, ssm_A_log, ssm_D, ssm_norm, ssm_w_out, ffn_w_up, ffn_conv_w, ffn_conv_b, ffn_w_down, loss_target, m_norm_mix, m_norm_ffn, m_norm_final, m_mix_w_in, m_pool_w, m_pool_scale, m_attn_sinks, m_mix_w_out, m_ssm_w_in, m_ssm_conv_w, m_ssm_conv_b, m_ssm_dt_bias, m_ssm_A_log, m_ssm_D, m_ssm_norm, m_ssm_w_out, m_ffn_w_up, m_ffn_conv_w, m_ffn_conv_b, m_ffn_w_down, v_norm_mix, v_norm_ffn, v_norm_final, v_mix_w_in, v_pool_w, v_pool_scale, v_attn_sinks, v_mix_w_out, v_ssm_w_in, v_ssm_conv_w, v_ssm_conv_b, v_ssm_dt_bias, v_ssm_A_log, v_ssm_D, v_ssm_norm, v_ssm_w_out, v_ffn_w_up, v_ffn_conv_w, v_ffn_conv_b, v_ffn_w_down):
    W = dict(norm_mix=norm_mix, norm_ffn=norm_ffn, norm_final=norm_final, mix_w_in=mix_w_in, pool_w=pool_w, pool_scale=pool_scale, attn_sinks=attn_sinks, mix_w_out=mix_w_out, ssm_w_in=ssm_w_in, ssm_conv_w=ssm_conv_w, ssm_conv_b=ssm_conv_b, ssm_dt_bias=ssm_dt_bias, ssm_A_log=ssm_A_log, ssm_D=ssm_D, ssm_norm=ssm_norm, ssm_w_out=ssm_w_out, ffn_w_up=ffn_w_up, ffn_conv_w=ffn_conv_w, ffn_conv_b=ffn_conv_b, ffn_w_down=ffn_w_down)
    Mo = dict(norm_mix=m_norm_mix, norm_ffn=m_norm_ffn, norm_final=m_norm_final, mix_w_in=m_mix_w_in, pool_w=m_pool_w, pool_scale=m_pool_scale, attn_sinks=m_attn_sinks, mix_w_out=m_mix_w_out, ssm_w_in=m_ssm_w_in, ssm_conv_w=m_ssm_conv_w, ssm_conv_b=m_ssm_conv_b, ssm_dt_bias=m_ssm_dt_bias, ssm_A_log=m_ssm_A_log, ssm_D=m_ssm_D, ssm_norm=m_ssm_norm, ssm_w_out=m_ssm_w_out, ffn_w_up=m_ffn_w_up, ffn_conv_w=m_ffn_conv_w, ffn_conv_b=m_ffn_conv_b, ffn_w_down=m_ffn_w_down)
    Vo = dict(norm_mix=v_norm_mix, norm_ffn=v_norm_ffn, norm_final=v_norm_final, mix_w_in=v_mix_w_in, pool_w=v_pool_w, pool_scale=v_pool_scale, attn_sinks=v_attn_sinks, mix_w_out=v_mix_w_out, ssm_w_in=v_ssm_w_in, ssm_conv_w=v_ssm_conv_w, ssm_conv_b=v_ssm_conv_b, ssm_dt_bias=v_ssm_dt_bias, ssm_A_log=v_ssm_A_log, ssm_D=v_ssm_D, ssm_norm=v_ssm_norm, ssm_w_out=v_ssm_w_out, ffn_w_up=v_ffn_w_up, ffn_conv_w=v_ffn_conv_w, ffn_conv_b=v_ffn_conv_b, ffn_w_down=v_ffn_w_down)

    sh = [mix_w_in[0].astype(MXU), mix_w_out[0].astype(MXU), ssm_w_in[0].astype(MXU), ssm_w_out[0].astype(MXU),
          ffn_w_up.astype(MXU), ffn_w_down.astype(MXU), ssm_conv_w[0], ssm_conv_b, ssm_norm, ffn_conv_w]
    kchip = 2 * lax.axis_index("x") + lax.axis_index("y")
    gathered = _gather_shards(sh, "gather_weights")
    g_mi, g_mo, g_si, g_so, g_up, g_dn, g_scw, g_scb, g_sn, g_fcw = [
        lax.dynamic_update_slice_in_dim(g, own[None], kchip, axis=0) for g, own in zip(gathered, sh)]
    wsi = _cols_from_shards(g_si)
    zx = SSM_D_INNER + SSM_CONV_DIM
    P = dict(
        nm=norm_mix, nf=norm_ffn, nfin=norm_final,
        wmi=_cols_from_shards(g_mi), wmo=g_mo.reshape(D_MODEL, D_MODEL),
        pool_w=pool_w[0], pool_scale=pool_scale, sinks=attn_sinks[0],
        wz=wsi[:, :SSM_D_INNER], wxbc=wsi[:, SSM_D_INNER:zx],
        wdt=jnp.pad(wsi[:, zx:], ((0, 0), (0, LANES - SSM_HEADS))),
        scw=_cols_from_shards(g_scw), scb=g_scb.reshape(1, SSM_CONV_DIM), snorm=g_sn.reshape(1, SSM_D_INNER),
        dt_bias=jnp.pad(ssm_dt_bias, ((0, 0), (0, LANES - SSM_HEADS))), a_log=jnp.pad(ssm_A_log, ((0, 0), (0, LANES - SSM_HEADS))),
        d_exp=jnp.repeat(ssm_D, SSM_D_INNER // SSM_HEADS, axis=1),
        wso=g_so.reshape(SSM_D_INNER, D_MODEL),
        wup=[jnp.concatenate([g_up[p, i] for p in _PERM], axis=1) for i in range(2)],
        fcw=[jnp.concatenate([g_fcw[p, i] for p in _PERM], axis=1) for i in range(2)],
        fcb=[jnp.concatenate([ffn_conv_b[i:i + 1, p * FFN_TC:(p + 1) * FFN_TC] for p in _PERM], axis=1) for i in range(2)],
        wdn=[g_dn[:, i].reshape(D_FF, D_MODEL) for i in range(2)],
    )
    cos, sin_s = _rope_tables(positions)
    loss_row, grad_x, big, small = _local_step(x[0], cos, sin_s, loss_target[0], P)

    names = ["mix_w_in", "mix_w_out", "ssm_w_in", "ssm_w_out", "ffn_w_up0", "ffn_w_up1", "ffn_w_down0", "ffn_w_down1"]
    gs = [big[nm] for nm in names]
    cidx = lax.axis_index("c").astype(jnp.int32).reshape(1)
    got = _pair_exchange(gs, "pair_exchange")
    ps = [_pair_sum(g, o, cidx, f"pair_sum_{nm}") for g, o, nm in zip(gs, got, names)]
    parts = _chip_exchange(ps, "chip_exchange")
    kidx = kchip.astype(jnp.int32).reshape(1)
    fs = [_chip_sum(o, p, kidx, f"chip_sum_{nm}") for o, p, nm in zip(ps, parts, names)]
    others = _half_exchange(fs, "half_exchange")
    red = {nm: (f, o) for nm, f, o in zip(names, fs, others)}

    out = {}

    def big_update(pname, gparts):
        w = W[pname]
        lw = len(gparts)
        shp = w.shape
        r2, cc = gparts[0][0].shape
        w3, m3, v3 = (t.reshape(lw, 2 * r2, cc) for t in (w, Mo[pname], Vo[pname]))
        res = _adamw(w3, m3, v3, gparts, cidx, f"adamw_{pname}")
        out[pname] = tuple(r.reshape(shp) for r in res)

    big_update("mix_w_in", [red["mix_w_in"]])
    big_update("mix_w_out", [red["mix_w_out"]])
    big_update("ssm_w_in", [red["ssm_w_in"]])
    big_update("ssm_w_out", [red["ssm_w_out"]])
    big_update("ffn_w_up", [red["ffn_w_up0"], red["ffn_w_up1"]])
    big_update("ffn_w_down", [red["ffn_w_down0"], red["ffn_w_down1"]])

    small_names = ["norm_mix", "norm_ffn", "norm_final", "pool_w", "pool_scale", "attn_sinks", "ssm_dt_bias", "ssm_A_log",
                   "ssm_D", "ffn_conv_b", "ssm_conv_w", "ssm_conv_b", "ssm_norm", "ffn_conv_w"]

    def as2d(t):
        if t.ndim == 1:
            return t.reshape(1, -1)
        return t.reshape(-1, t.shape[-1])

    items = [(small[nm], as2d(W[nm]), as2d(Mo[nm]), as2d(Vo[nm])) for nm in small_names]
    res = _small_reduce_adamw(items, loss_row, "small_reduce_adamw")
    for a, nm in enumerate(small_names):
        out[nm] = tuple(r.reshape(W[nm].shape) for r in res[4 * a:4 * a + 4])
    loss = res[-1][0, 0]

    order = ["norm_mix", "norm_ffn", "norm_final", "mix_w_in", "pool_w", "pool_scale", "attn_sinks", "mix_w_out", "ssm_w_in",
             "ssm_conv_w", "ssm_conv_b", "ssm_dt_bias", "ssm_A_log", "ssm_D", "ssm_norm", "ssm_w_out", "ffn_w_up", "ffn_conv_w",
             "ffn_conv_b", "ffn_w_down"]
    return (loss, grad_x.reshape(x.shape), *[out[nm][0] for nm in order], *[out[nm][1] for nm in order],
            *[out[nm][2] for nm in order], *[out[nm][3] for nm in order])
```

```python
import functools

import jax
import jax.numpy as jnp
from jax import lax
from jax.experimental import pallas as pl
from jax.experimental.pallas import tpu as pltpu

F32 = jnp.float32
BF16 = jnp.bfloat16
MXU = BF16
HI = lax.Precision.HIGHEST

D_MODEL = 1024
POOL_WINDOWS = (2, 4, 8, 16)
POOL_DIM = 512
POOL_GROUP = 128
HEAD_DIM = 64
N_HEADS = 8
N_KV_HEADS = 2
GQ = 4
Q_DIM = 512
KV_DIM = 128
BLOCK = 128
ROPE_THETA = 10000.0
MIX_IN_DIM = 1280
SSM_D_INNER = 2048
SSM_HEADS = 32
SSM_GROUPS = 8
SSM_STATE = 128
SSM_CONV = 4
SSM_CHUNK = 128
SSM_CONV_DIM = 4096
SSM_IN_DIM = 6176
D_FF = 2816
FFN_CONV = 3
NORM_EPS = 1e-6
SSM_NORM_EPS = 1e-5
ADAM_LR = 0.001
ADAM_B1 = 0.9
ADAM_B2 = 0.999
ADAM_EPS = 1e-08
ADAM_WD = 0.01
ADAM_STEP = 10

N_CHIPS = 4
N_DEV = 8
LANES = 128
SUBLANES = 8
V7X_VMEM_LIMIT = 56 * 1024 * 1024
NEG = -1e30
MESH = pl.DeviceIdType.MESH


def _cp(*sem):
    return pltpu.CompilerParams(dimension_semantics=sem if sem else None, vmem_limit_bytes=V7X_VMEM_LIMIT)


def _sds(shape, dtype=F32):
    return jax.ShapeDtypeStruct(tuple(shape), dtype)


def _iota(shape, dim):
    return lax.broadcasted_iota(jnp.int32, shape, dim)


def _silu(x):
    return x * (1.0 / (1.0 + jnp.exp(-x)))


def _dsilu(x):
    s = 1.0 / (1.0 + jnp.exp(-x))
    return s * (1.0 + x * (1.0 - s))


def _mm(a, b, *, ta=False, tb=False, tm, tn, tk, res=None, out_dtype=F32, out_shard_perm=None, name):
    M, K = (a.shape[1], a.shape[0]) if ta else a.shape
    N = b.shape[0] if tb else b.shape[1]
    tm, tn, tk = min(tm, M), min(tn, N), min(tk, K)
    gm, gn, gk = M // tm, N // tn, K // tk
    assert gm * tm == M and gn * tn == N and gk * tk == K, (name, M, N, K, tm, tn, tk)
    a_spec = pl.BlockSpec((tk, tm), lambda i, j, k: (k, i)) if ta else pl.BlockSpec((tm, tk), lambda i, j, k: (i, k))
    b_spec = pl.BlockSpec((tn, tk), lambda i, j, k: (j, k)) if tb else pl.BlockSpec((tk, tn), lambda i, j, k: (k, j))
    dims = (((0 if ta else 1,), (1 if tb else 0,)), ((), ()))
    has_res = res is not None

    def body(*refs):
        a_ref, b_ref = refs[0], refs[1]
        r_ref = refs[2] if has_res else None
        o_ref = refs[3] if has_res else refs[2]
        p = lax.dot_general(a_ref[...].astype(MXU), b_ref[...].astype(MXU), dims, preferred_element_type=F32)
        if gk == 1:
            if has_res:
                p = p + r_ref[...]
            o_ref[...] = p.astype(out_dtype)
        else:
            acc = refs[-1]
            k = pl.program_id(2)

            @pl.when(k == 0)
            def _():
                acc[...] = p

            @pl.when(k > 0)
            def _():
                acc[...] += p

            @pl.when(k == gk - 1)
            def _():
                r = acc[...]
                if has_res:
                    r = r + r_ref[...]
                o_ref[...] = r.astype(out_dtype)

    in_specs = [a_spec, b_spec]
    args = [a, b]
    if has_res:
        in_specs.append(pl.BlockSpec((tm, tn), lambda i, j, k: (i, j)))
        args.append(res)
    if out_shard_perm is None:
        out_spec = pl.BlockSpec((tm, tn), lambda i, j, k: (i, j))
        out_shape = _sds((M, N), out_dtype)
    else:
        assert gn == len(out_shard_perm) == 4 and tuple(out_shard_perm) == (0, 2, 1, 3)
        out_spec = pl.BlockSpec((None, tm, tn), lambda i, j, k: ((j % 2) * 2 + j // 2, i, 0))
        out_shape = _sds((gn, M, tn), out_dtype)
    return pl.pallas_call(
        body, grid=(gm, gn, gk), in_specs=in_specs, out_specs=out_spec, out_shape=out_shape,
        scratch_shapes=[pltpu.VMEM((tm, tn), F32)] if gk > 1 else [],
        compiler_params=_cp("parallel", "parallel", "arbitrary"), name=name)(*args)


def _rmsnorm_fwd(x, w, name, token=None):
    T, D = x.shape
    tm = min(T, 512)
    has_token = token is not None

    def body(*refs):
        x_ref, w_ref, o_ref = refs[0], refs[1], refs[-1]
        xv = x_ref[...]
        if has_token:
            xv = xv + refs[2][0:1, 0:1]
        r = lax.rsqrt(jnp.mean(xv * xv, axis=-1, keepdims=True) + NORM_EPS)
        o_ref[...] = (xv * r * w_ref[...]).astype(o_ref.dtype)

    in_specs = [pl.BlockSpec((tm, D), lambda i: (i, 0)), pl.BlockSpec((1, D), lambda i: (0, 0))]
    args = [x, w.reshape(1, D)]
    if has_token:
        in_specs.append(pl.BlockSpec((SUBLANES, LANES), lambda i: (0, 0)))
        args.append(token)
    return pl.pallas_call(
        body, grid=(T // tm,), in_specs=in_specs,
        out_specs=pl.BlockSpec((tm, D), lambda i: (i, 0)), out_shape=_sds((T, D), MXU),
        compiler_params=_cp("parallel"), name=name)(*args)


def _rmsnorm_bwd(x, w, dh, dres, name):
    T, D = x.shape
    tm = min(T, 512)

    def body(x_ref, w_ref, dh_ref, dr_ref, dx_ref, dw_ref):
        xv = x_ref[...]
        r = lax.rsqrt(jnp.mean(xv * xv, axis=-1, keepdims=True) + NORM_EPS)
        xh = xv * r
        dh = dh_ref[...]
        g = dh * w_ref[...]
        dx_ref[...] = dr_ref[...] + r * (g - xh * jnp.mean(g * xh, axis=-1, keepdims=True))
        part = jnp.sum(dh * xh, axis=0, keepdims=True)

        @pl.when(pl.program_id(0) == 0)
        def _():
            dw_ref[...] = part

        @pl.when(pl.program_id(0) > 0)
        def _():
            dw_ref[...] += part

    row = pl.BlockSpec((tm, D), lambda i: (i, 0))
    vec = pl.BlockSpec((1, D), lambda i: (0, 0))
    return pl.pallas_call(
        body, grid=(T // tm,), in_specs=[row, vec, row, row], out_specs=[row, vec],
        out_shape=[_sds((T, D)), _sds((1, D))], compiler_params=_cp("arbitrary"), name=name)(x, w.reshape(1, D), dh, dres)


def _loss_head(x, w, target, name):
    T, D = x.shape
    tm = min(T, 512)

    def body(x_ref, w_ref, t_ref, loss_ref, dx_ref, dw_ref):
        xv = x_ref[...]
        r = lax.rsqrt(jnp.mean(xv * xv, axis=-1, keepdims=True) + NORM_EPS)
        xh = xv * r
        wv = w_ref[...]
        e = xh * wv - t_ref[...]
        lpart = 0.5 * jnp.sum(jnp.mean(e * e, axis=-1, keepdims=True), axis=0, keepdims=True)
        dy = e * (1.0 / D)
        g = dy * wv
        dx_ref[...] = r * (g - xh * jnp.mean(g * xh, axis=-1, keepdims=True))
        part = jnp.sum(dy * xh, axis=0, keepdims=True)
        lrow = jnp.broadcast_to(lpart, (1, LANES))

        @pl.when(pl.program_id(0) == 0)
        def _():
            dw_ref[...] = part
            loss_ref[...] = lrow

        @pl.when(pl.program_id(0) > 0)
        def _():
            dw_ref[...] += part
            loss_ref[...] += lrow

    row = pl.BlockSpec((tm, D), lambda i: (i, 0))
    vec = pl.BlockSpec((1, D), lambda i: (0, 0))
    return pl.pallas_call(
        body, grid=(T // tm,), in_specs=[row, vec, row],
        out_specs=[pl.BlockSpec((1, LANES), lambda i: (0, 0)), row, vec],
        out_shape=[_sds((1, LANES)), _sds((T, D)), _sds((1, D))],
        compiler_params=_cp("arbitrary"), name=name)(x, w.reshape(1, D), target)


def _shift_down(cur, prev8, s):
    if s == 0:
        return cur
    tm = cur.shape[0]
    rc = pltpu.roll(cur, s, 0)
    top = jnp.where(_iota((SUBLANES, cur.shape[1]), 0) < s, pltpu.roll(prev8, s, 0), rc[:SUBLANES])
    return jnp.concatenate([top, rc[SUBLANES:]], axis=0) if tm > SUBLANES else top


def _shift_up(cur, next8, s):
    if s == 0:
        return cur
    tm = cur.shape[0]
    rc = pltpu.roll(cur, tm - s, 0)
    bot = jnp.where(_iota((SUBLANES, cur.shape[1]), 0) >= SUBLANES - s, pltpu.roll(next8, SUBLANES - s, 0), rc[tm - SUBLANES:])
    return jnp.concatenate([rc[:tm - SUBLANES], bot], axis=0) if tm > SUBLANES else bot


def _conv_rows(cur, prev8, w, b, K):
    acc = cur * w[K - 1:K, :] + b
    for s in range(1, K):
        acc = acc + _shift_down(cur, prev8, s) * w[K - 1 - s:K - s, :]
    return acc


def _halo_specs(tm, tc, col_of):
    q = tm // SUBLANES

    def prev_map(i, j):
        return (jnp.maximum(i * q - 1, 0), col_of(j))

    def make_next(n_row_tiles):
        def next_map(i, j):
            return (jnp.minimum((i + 1) * q, n_row_tiles * q - 1), col_of(j))
        return next_map

    return (lambda: pl.BlockSpec((SUBLANES, tc), prev_map)), (lambda n: pl.BlockSpec((SUBLANES, tc), make_next(n)))


FFN_TC = 1408


def _ffn_mid_fwd(hid, cw, cb, name):
    T = hid.shape[0]
    tm = min(T, 256)
    nt, nj = T // tm, D_FF // FFN_TC
    K = FFN_CONV

    def body(h_ref, hp_ref, w_ref, b_ref, o_ref):
        i = pl.program_id(0)
        cur = h_ref[...]
        prev8 = jnp.where(i > 0, hp_ref[...], 0.0)
        hc = _conv_rows(cur, prev8, w_ref[...], b_ref[...], K)
        o_ref[...] = (_silu(hc[:, FFN_TC:]) * hc[:, :FFN_TC]).astype(o_ref.dtype)

    mk_prev, _ = _halo_specs(tm, 2 * FFN_TC, lambda j: j)
    return pl.pallas_call(
        body, grid=(nt, nj),
        in_specs=[pl.BlockSpec((tm, 2 * FFN_TC), lambda i, j: (i, j)), mk_prev(),
                  pl.BlockSpec((K, 2 * FFN_TC), lambda i, j: (0, j)), pl.BlockSpec((1, 2 * FFN_TC), lambda i, j: (0, j))],
        out_specs=pl.BlockSpec((tm, FFN_TC), lambda i, j: (i, j)), out_shape=_sds((T, D_FF), MXU),
        compiler_params=_cp("parallel", "parallel"), name=name)(hid, hid, cw, cb)


def _ffn_mid_bwd(hid, cw, cb, da, name):
    T = hid.shape[0]
    tm = min(T, 256)
    nt, nj = T // tm, D_FF // FFN_TC
    K = FFN_CONV
    W2 = 2 * FFN_TC

    def body(h_ref, hp_ref, hn_ref, da_ref, dan_ref, w_ref, b_ref, dh_ref, dw_ref, db_ref):
        i = pl.program_id(1)
        w = w_ref[...]
        b = b_ref[...]
        cur = h_ref[...]
        prev8 = jnp.where(i > 0, hp_ref[...], 0.0)
        nxt8 = hn_ref[...]
        last = i == nt - 1

        def dpre(hc, dav):
            u, g = hc[:, :FFN_TC], hc[:, FFN_TC:]
            return jnp.concatenate([dav * _silu(g), dav * u * _dsilu(g)], axis=1)

        hc = _conv_rows(cur, prev8, w, b, K)
        d_cur = dpre(hc, da_ref[...])
        hc_n = _conv_rows(nxt8, cur[tm - SUBLANES:], w, b, K)
        d_nxt = jnp.where(last, 0.0, dpre(hc_n, dan_ref[...]))
        dh = d_cur * w[K - 1:K, :]
        for s in range(1, K):
            dh = dh + _shift_up(d_cur, d_nxt, s) * w[K - 1 - s:K - s, :]
        dh_ref[...] = dh.astype(dh_ref.dtype)
        rows = [jnp.sum(d_cur * _shift_down(cur, prev8, K - 1 - k), axis=0, keepdims=True) for k in range(K)]
        dwp = jnp.concatenate(rows, axis=0)
        dbp = jnp.sum(d_cur, axis=0, keepdims=True)

        @pl.when(i == 0)
        def _():
            dw_ref[...] = dwp
            db_ref[...] = dbp

        @pl.when(i > 0)
        def _():
            dw_ref[...] += dwp
            db_ref[...] += dbp

    q = tm // SUBLANES
    blk = pl.BlockSpec((tm, W2), lambda j, i: (i, j))
    prv = pl.BlockSpec((SUBLANES, W2), lambda j, i: (jnp.maximum(i * q - 1, 0), j))
    nxt = pl.BlockSpec((SUBLANES, W2), lambda j, i: (jnp.minimum((i + 1) * q, nt * q - 1), j))
    dab = pl.BlockSpec((tm, FFN_TC), lambda j, i: (i, j))
    dan = pl.BlockSpec((SUBLANES, FFN_TC), lambda j, i: (jnp.minimum((i + 1) * q, nt * q - 1), j))
    return pl.pallas_call(
        body, grid=(nj, nt),
        in_specs=[blk, prv, nxt, dab, dan, pl.BlockSpec((K, W2), lambda j, i: (0, j)), pl.BlockSpec((1, W2), lambda j, i: (0, j))],
        out_specs=[blk, pl.BlockSpec((K, W2), lambda j, i: (0, j)), pl.BlockSpec((1, W2), lambda j, i: (0, j))],
        out_shape=[_sds((T, 2 * D_FF), MXU), _sds((K, 2 * D_FF)), _sds((1, 2 * D_FF))],
        compiler_params=_cp("parallel", "arbitrary"), name=name)(hid, hid, hid, da, da, cw, cb)


def _rope(t, cos, sin_s, inverse=False):
    n = t.shape[1] // LANES
    c = jnp.concatenate([cos] * n, axis=1) if n > 1 else cos
    s = jnp.concatenate([sin_s] * n, axis=1) if n > 1 else sin_s
    a = pltpu.roll(t, HEAD_DIM // 2, 1)
    b = pltpu.roll(t, t.shape[1] - HEAD_DIM // 2, 1)
    first = (_iota(t.shape, 1) % HEAD_DIM) < HEAD_DIM // 2
    rot = jnp.where(first, b, a) * s
    return t * c - rot if inverse else t * c + rot


def _stack_heads(t, g):
    return jnp.concatenate([t[:, (GQ * g + r) * HEAD_DIM:(GQ * g + r + 1) * HEAD_DIM] for r in range(GQ)], axis=0)


def _stack_cols(t, g):
    return jnp.concatenate([t[:, GQ * g + r:GQ * g + r + 1] for r in range(GQ)], axis=0)


def _pool_sums(prev, cur, w):
    s = jnp.concatenate([prev, cur], axis=0)
    sh = 1
    while sh < w:
        s = s + pltpu.roll(s, sh, 0)
        sh *= 2
    return s[BLOCK:]


def _nt(a, b):
    return lax.dot_general(a.astype(MXU), b.astype(MXU), (((1,), (1,)), ((), ())), preferred_element_type=F32)


def _tn(a, b):
    return lax.dot_general(a.astype(MXU), b.astype(MXU), (((0,), (0,)), ((), ())), preferred_element_type=F32)


def _nn(a, b):
    return jnp.dot(a.astype(MXU), b.astype(MXU), preferred_element_type=F32)


def _mixcore_fwd(proj, cos, sin_s, pool_w, pool_scale, sinks, name):
    T = proj.shape[0]
    nb = T // BLOCK
    scale = HEAD_DIM ** -0.5

    def body(p_ref, pp_ref, c_ref, s_ref, cp_ref, sp_ref, pw_ref, ps_ref, sk_ref, cat_ref, at_ref, lse_ref):
        i = pl.program_id(0)
        has_prev = i > 0
        cur = p_ref[...]
        prv = jnp.where(has_prev, pp_ref[...], 0.0)
        tpos = (i * BLOCK + _iota((BLOCK, 1), 0) + 1).astype(F32)
        for g, w in enumerate(POOL_WINDOWS):
            sl = slice(g * POOL_GROUP, (g + 1) * POOL_GROUP)
            pooled = _pool_sums(prv[:, sl], cur[:, sl], w) / jnp.minimum(tpos, float(w)) - cur[:, sl]
            cat_ref[:, sl] = (_nn(pooled, pw_ref[g]) * ps_ref[:, sl]).astype(cat_ref.dtype)
        q = _rope(cur[:, POOL_DIM:POOL_DIM + Q_DIM], c_ref[...], s_ref[...])
        kc = _rope(cur[:, POOL_DIM + Q_DIM:POOL_DIM + Q_DIM + KV_DIM], c_ref[...], s_ref[...])
        kp = _rope(prv[:, POOL_DIM + Q_DIM:POOL_DIM + Q_DIM + KV_DIM], cp_ref[...], sp_ref[...])
        vc = cur[:, POOL_DIM + Q_DIM + KV_DIM:]
        vp = prv[:, POOL_DIM + Q_DIM + KV_DIM:]
        ri = _iota((GQ * BLOCK, BLOCK), 0) % BLOCK
        cj = _iota((GQ * BLOCK, BLOCK), 1)
        mc = cj <= ri
        mp = jnp.logical_and(cj > ri, has_prev)
        outs, lses = [], []
        for g in range(N_KV_HEADS):
            hs = slice(g * HEAD_DIM, (g + 1) * HEAD_DIM)
            qg = _stack_heads(q, g) * scale
            sc = jnp.where(mc, _nt(qg, kc[:, hs]), NEG)
            sp = jnp.where(mp, _nt(qg, kp[:, hs]), NEG)
            sink = jnp.concatenate([jnp.full((BLOCK, 1), sk_ref[GQ * g + r], F32) for r in range(GQ)], axis=0)
            m = jnp.maximum(jnp.maximum(jnp.max(sc, axis=1, keepdims=True), jnp.max(sp, axis=1, keepdims=True)), sink)
            pc = jnp.exp(sc - m)
            pp = jnp.exp(sp - m)
            den = jnp.sum(pc, axis=1, keepdims=True) + jnp.sum(pp, axis=1, keepdims=True) + jnp.exp(sink - m)
            o = (_nn(pc, vc[:, hs]) + _nn(pp, vp[:, hs])) / den
            lse = m + jnp.log(den)
            for r in range(GQ):
                outs.append(o[r * BLOCK:(r + 1) * BLOCK])
                lses.append(lse[r * BLOCK:(r + 1) * BLOCK])
        attn = jnp.concatenate(outs, axis=1)
        at_ref[...] = attn
        cat_ref[:, POOL_DIM:] = attn.astype(cat_ref.dtype)
        lane = _iota((BLOCK, LANES), 1)
        lrow = jnp.zeros((BLOCK, LANES), F32)
        for h in range(N_HEADS):
            lrow = jnp.where(lane == h, lses[h], lrow)
        lse_ref[...] = lrow

    cur = lambda w: pl.BlockSpec((BLOCK, w), lambda i: (i, 0))
    prv = lambda w: pl.BlockSpec((BLOCK, w), lambda i: (jnp.maximum(i - 1, 0), 0))
    return pl.pallas_call(
        body, grid=(nb,),
        in_specs=[cur(MIX_IN_DIM), prv(MIX_IN_DIM), cur(LANES), cur(LANES), prv(LANES), prv(LANES),
                  pl.BlockSpec((4, POOL_GROUP, POOL_GROUP), lambda i: (0, 0, 0)), pl.BlockSpec((1, POOL_DIM), lambda i: (0, 0)),
                  pl.BlockSpec(memory_space=pltpu.SMEM)],
        out_specs=[cur(2 * POOL_DIM), cur(Q_DIM), cur(LANES)],
        out_shape=[_sds((T, 2 * POOL_DIM), MXU), _sds((T, Q_DIM)), _sds((T, LANES))],
        compiler_params=_cp("parallel"), name=name)(proj, proj, cos, sin_s, cos, sin_s, pool_w, pool_scale, sinks)


def _mixcore_bwd(proj, cos, sin_s, pool_w, pool_scale, sinks, attn, lse, dcat, name):
    T = proj.shape[0]
    nb = T // BLOCK
    scale = HEAD_DIM ** -0.5
    QO, KO, VO = POOL_DIM, POOL_DIM + Q_DIM, POOL_DIM + Q_DIM + KV_DIM

    def body(p_ref, pp_ref, pn_ref, c_ref, s_ref, cp_ref, sp_ref, cn_ref, sn_ref, pw_ref, ps_ref, sk_ref,
             at_ref, atn_ref, l_ref, ln_ref, d_ref, dn_ref, dp_ref, dpw_ref, dps_ref, dsk_ref):
        i = pl.program_id(0)
        has_prev = i > 0
        has_next = i < nb - 1
        cur = p_ref[...]
        prv = jnp.where(has_prev, pp_ref[...], 0.0)
        d_cur = d_ref[...]
        d_nxt = jnp.where(has_next, dn_ref[...], 0.0)

        tpos = (i * BLOCK + _iota((BLOCK, 1), 0) + 1).astype(F32)
        tpos2 = (i * BLOCK + _iota((2 * BLOCK, 1), 0) + 1).astype(F32)
        ps = ps_ref[...]
        dps_parts, dpw_parts = [], []
        for g, w in enumerate(POOL_WINDOWS):
            sl = slice(g * POOL_GROUP, (g + 1) * POOL_GROUP)
            pooled = _pool_sums(prv[:, sl], cur[:, sl], w) / jnp.minimum(tpos, float(w)) - cur[:, sl]
            mixed = _nn(pooled, pw_ref[g])
            dps_parts.append(jnp.sum(d_cur[:, sl] * mixed, axis=0, keepdims=True))
            dm2 = jnp.concatenate([d_cur[:, sl], d_nxt[:, sl]], axis=0) * ps[:, sl]
            dpw_parts.append(_tn(pooled, dm2[:BLOCK]))
            dpool2 = _nt(dm2, pw_ref[g])
            e = dpool2 / jnp.minimum(tpos2, float(w))
            sh = 1
            while sh < w:
                e = e + pltpu.roll(e, 2 * BLOCK - sh, 0)
                sh *= 2
            dp_ref[:, sl] = (e[:BLOCK] - dpool2[:BLOCK]).astype(dp_ref.dtype)
        dpsp = jnp.concatenate(dps_parts, axis=1)

        nxt = pn_ref[...]
        q = _rope(cur[:, QO:KO], c_ref[...], s_ref[...])
        qn = _rope(nxt[:, QO:KO], cn_ref[...], sn_ref[...])
        kc = _rope(cur[:, KO:VO], c_ref[...], s_ref[...])
        kp = _rope(prv[:, KO:VO], cp_ref[...], sp_ref[...])
        vc, vp = cur[:, VO:], prv[:, VO:]
        do, don = d_cur[:, POOL_DIM:], d_nxt[:, POOL_DIM:]
        dl = do * at_ref[...]
        dln = don * atn_ref[...]
        lse, lsen = l_ref[...], ln_ref[...]
        ri = _iota((GQ * BLOCK, BLOCK), 0) % BLOCK
        cj = _iota((GQ * BLOCK, BLOCK), 1)
        mc = cj <= ri
        mp = jnp.logical_and(cj > ri, has_prev)
        mn = jnp.logical_and(cj > ri, has_next)
        dq_parts, dk_parts, dv_parts, dsk_vals = [], [], [], []
        for g in range(N_KV_HEADS):
            hs = slice(g * HEAD_DIM, (g + 1) * HEAD_DIM)
            qg, qng = _stack_heads(q, g) * scale, _stack_heads(qn, g) * scale
            dog, dong = _stack_heads(do, g), _stack_heads(don, g)
            delta = jnp.sum(_stack_heads(dl, g), axis=1, keepdims=True)
            deltan = jnp.sum(_stack_heads(dln, g), axis=1, keepdims=True)
            lg, lng = _stack_cols(lse, g), _stack_cols(lsen, g)
            pc = jnp.where(mc, jnp.exp(_nt(qg, kc[:, hs]) - lg), 0.0)
            pp = jnp.where(mp, jnp.exp(_nt(qg, kp[:, hs]) - lg), 0.0)
            pn = jnp.where(mn, jnp.exp(_nt(qng, kc[:, hs]) - lng), 0.0)
            dsc = pc * (_nt(dog, vc[:, hs]) - delta)
            dsp = pp * (_nt(dog, vp[:, hs]) - delta)
            dsn = pn * (_nt(dong, vc[:, hs]) - deltan)
            dqg = (_nn(dsc, kc[:, hs]) + _nn(dsp, kp[:, hs])) * scale
            dq_parts += [dqg[r * BLOCK:(r + 1) * BLOCK] for r in range(GQ)]
            dk_parts.append(_tn(dsc, qg) + _tn(dsn, qng))
            dv_parts.append(_tn(pc, dog) + _tn(pn, dong))
            sink = jnp.concatenate([jnp.full((BLOCK, 1), sk_ref[GQ * g + r], F32) for r in range(GQ)], axis=0)
            dsk = -jnp.exp(sink - lg) * delta
            dsk_vals += [jnp.sum(dsk[r * BLOCK:(r + 1) * BLOCK], axis=0, keepdims=True) for r in range(GQ)]
        dq = _rope(jnp.concatenate(dq_parts, axis=1), c_ref[...], s_ref[...], inverse=True)
        dk = _rope(jnp.concatenate(dk_parts, axis=1), c_ref[...], s_ref[...], inverse=True)
        dp_ref[:, QO:KO] = dq.astype(dp_ref.dtype)
        dp_ref[:, KO:VO] = dk.astype(dp_ref.dtype)
        dp_ref[:, VO:] = jnp.concatenate(dv_parts, axis=1).astype(dp_ref.dtype)
        lane = _iota((1, LANES), 1)
        dskp = jnp.zeros((1, LANES), F32)
        for h in range(N_HEADS):
            dskp = jnp.where(lane == h, dsk_vals[h], dskp)

        @pl.when(i == 0)
        def _():
            dps_ref[...] = dpsp
            dsk_ref[...] = dskp
            for g in range(4):
                dpw_ref[g] = dpw_parts[g]

        @pl.when(i > 0)
        def _():
            dps_ref[...] += dpsp
            dsk_ref[...] += dskp
            for g in range(4):
                dpw_ref[g] += dpw_parts[g]

    cur = lambda w: pl.BlockSpec((BLOCK, w), lambda i: (i, 0))
    prv = lambda w: pl.BlockSpec((BLOCK, w), lambda i: (jnp.maximum(i - 1, 0), 0))
    nxt = lambda w: pl.BlockSpec((BLOCK, w), lambda i: (jnp.minimum(i + 1, nb - 1), 0))
    return pl.pallas_call(
        body, grid=(nb,),
        in_specs=[cur(MIX_IN_DIM), prv(MIX_IN_DIM), nxt(MIX_IN_DIM),
                  cur(LANES), cur(LANES), prv(LANES), prv(LANES), nxt(LANES), nxt(LANES),
                  pl.BlockSpec((4, POOL_GROUP, POOL_GROUP), lambda i: (0, 0, 0)), pl.BlockSpec((1, POOL_DIM), lambda i: (0, 0)),
                  pl.BlockSpec(memory_space=pltpu.SMEM),
                  cur(Q_DIM), nxt(Q_DIM), cur(LANES), nxt(LANES), cur(2 * POOL_DIM), nxt(2 * POOL_DIM)],
        out_specs=[cur(MIX_IN_DIM), pl.BlockSpec((4, POOL_GROUP, POOL_GROUP), lambda i: (0, 0, 0)),
                   pl.BlockSpec((1, POOL_DIM), lambda i: (0, 0)), pl.BlockSpec((1, LANES), lambda i: (0, 0))],
        out_shape=[_sds((T, MIX_IN_DIM), MXU), _sds((4, POOL_GROUP, POOL_GROUP)), _sds((1, POOL_DIM)), _sds((1, LANES))],
        compiler_params=_cp("arbitrary"), name=name)(
            proj, proj, proj, cos, sin_s, cos, sin_s, cos, sin_s, pool_w, pool_scale, sinks, attn, attn, lse, lse, dcat, dcat)


SSM_TC = 128
GROUP_W = SSM_D_INNER // SSM_GROUPS
PERM_W = GROUP_W + 2 * SSM_STATE


def _perm_col(n):
    nx = SSM_D_INNER // SSM_TC
    nbt = SSM_GROUPS
    x_idx = (n // 2) * 4 + n % 2
    b_idx = (n - nx) * 4 + 2
    c_idx = (n - nx - nbt) * 4 + 3
    return jnp.where(n < nx, x_idx, jnp.where(n < nx + nbt, b_idx, c_idx))


def _ssm_pre_fwd(xbc, cw, cb, name):
    T = xbc.shape[0]
    tm = min(T, 1024)
    K = SSM_CONV
    q = tm // SUBLANES

    def body(x_ref, xp_ref, w_ref, b_ref, o_ref):
        prev8 = jnp.where(pl.program_id(0) > 0, xp_ref[...], 0.0)
        o_ref[...] = _silu(_conv_rows(x_ref[...], prev8, w_ref[...], b_ref[...], K))

    tc = 512
    return pl.pallas_call(
        body, grid=(T // tm, SSM_CONV_DIM // tc),
        in_specs=[pl.BlockSpec((tm, tc), lambda i, j: (i, j)),
                  pl.BlockSpec((SUBLANES, tc), lambda i, j: (jnp.maximum(i * q - 1, 0), j)),
                  pl.BlockSpec((K, tc), lambda i, j: (0, j)), pl.BlockSpec((1, tc), lambda i, j: (0, j))],
        out_specs=pl.BlockSpec((tm, tc), lambda i, j: (i, j)), out_shape=_sds((T, SSM_CONV_DIM)),
        compiler_params=_cp("parallel", "parallel"), name=name)(xbc, xbc, cw, cb)


def _ssm_pre_bwd(xbc, cw, cb, dact_perm, name):
    T = xbc.shape[0]
    tm = min(T, 1024)
    nt = T // tm
    K = SSM_CONV
    q = tm // SUBLANES
    tc = SSM_TC

    def body(x_ref, xp_ref, xn_ref, d_ref, dn_ref, w_ref, b_ref, dx_ref, dw_ref, db_ref):
        i = pl.program_id(1)
        w = w_ref[...]
        b = b_ref[...]
        cur = x_ref[...]
        prev8 = jnp.where(i > 0, xp_ref[...], 0.0)
        nxt8 = xn_ref[...]
        d_cur = d_ref[...] * _dsilu(_conv_rows(cur, prev8, w, b, K))
        d_nxt = jnp.where(i == nt - 1, 0.0, dn_ref[...] * _dsilu(_conv_rows(nxt8, cur[tm - SUBLANES:], w, b, K)))
        dx = d_cur * w[K - 1:K, :]
        for s in range(1, K):
            dx = dx + _shift_up(d_cur, d_nxt, s) * w[K - 1 - s:K - s, :]
        dx_ref[...] = dx.astype(dx_ref.dtype)
        dwp = jnp.concatenate([jnp.sum(d_cur * _shift_down(cur, prev8, K - 1 - k), axis=0, keepdims=True) for k in range(K)], axis=0)
        dbp = jnp.sum(d_cur, axis=0, keepdims=True)

        @pl.when(i == 0)
        def _():
            dw_ref[...] = dwp
            db_ref[...] = dbp

        @pl.when(i > 0)
        def _():
            dw_ref[...] += dwp
            db_ref[...] += dbp

    nxt_row = lambda i: jnp.minimum((i + 1) * q, nt * q - 1)
    return pl.pallas_call(
        body, grid=(SSM_CONV_DIM // tc, nt),
        in_specs=[pl.BlockSpec((tm, tc), lambda j, i: (i, j)),
                  pl.BlockSpec((SUBLANES, tc), lambda j, i: (jnp.maximum(i * q - 1, 0), j)),
                  pl.BlockSpec((SUBLANES, tc), lambda j, i: (nxt_row(i), j)),
                  pl.BlockSpec((tm, tc), lambda j, i: (i, _perm_col(j))),
                  pl.BlockSpec((SUBLANES, tc), lambda j, i: (nxt_row(i), _perm_col(j))),
                  pl.BlockSpec((K, tc), lambda j, i: (0, j)), pl.BlockSpec((1, tc), lambda j, i: (0, j))],
        out_specs=[pl.BlockSpec((tm, tc), lambda j, i: (i, j)), pl.BlockSpec((K, tc), lambda j, i: (0, j)),
                   pl.BlockSpec((1, tc), lambda j, i: (0, j))],
        out_shape=[_sds((T, SSM_CONV_DIM), MXU), _sds((K, SSM_CONV_DIM)), _sds((1, SSM_CONV_DIM))],
        compiler_params=_cp("parallel", "arbitrary"), name=name)(xbc, xbc, xbc, dact_perm, dact_perm, cw, cb)


def _dot_hi(a, b):
    return jnp.dot(a, b, precision=HI, preferred_element_type=F32)


def _ssd_common(dtraw, bias, alog):
    L = SSM_CHUNK
    xb = dtraw + bias
    dt = jnp.maximum(xb, 0.0) + jnp.log1p(jnp.exp(-jnp.abs(xb)))
    A = -jnp.exp(alog)
    tril = (_iota((L, L), 1) <= _iota((L, L), 0)).astype(F32)
    acs = _dot_hi(tril, dt * A)
    return xb, dt, A, tril, acs


def _head_selectors():
    es = (_iota((LANES, SSM_D_INNER), 0) == _iota((LANES, SSM_D_INNER), 1) // HEAD_DIM).astype(BF16)
    est = (_iota((SSM_D_INNER, LANES), 1) == _iota((SSM_D_INNER, LANES), 0) // HEAD_DIM).astype(BF16)
    return es, est


def _dot_sel(v, sel):
    hi = v.astype(BF16)
    r1 = v - hi.astype(F32)
    mid = r1.astype(BF16)
    lo = (r1 - mid.astype(F32)).astype(BF16)
    d = lambda a: jnp.dot(a, sel, preferred_element_type=F32)
    return (d(hi) + d(mid)) + d(lo)


def _expand_heads(v, es):
    return _dot_sel(v, es)


def _reduce_heads(q, est):
    return _dot_sel(q, est)


def _per_state_row(v, g):
    return jnp.concatenate([jnp.broadcast_to(v[:, GQ * g + r:GQ * g + r + 1], (HEAD_DIM, 1)) for r in range(GQ)], axis=0)


def _ssd_fwd(xact, dtraw, dt_bias, a_log, name):
    T = xact.shape[0]
    nc = T // SSM_CHUNK
    L = SSM_CHUNK
    BO, CO = SSM_D_INNER, SSM_D_INNER + SSM_GROUPS * SSM_STATE

    def body(x_ref, dt_ref, bias_ref, al_ref, es_ref, y_ref, st_ref, state):
        @pl.when(pl.program_id(0) == 0)
        def _():
            state[...] = jnp.zeros(state.shape, F32)

        _, dt, A, tril, acs = _ssd_common(dt_ref[...], bias_ref[...], al_ref[...])
        acsT = acs.T
        last = acs[L - 1:L, :]
        cd = jnp.exp(last)
        es = es_ref[...]
        dtX = _expand_heads(dt, es)
        EX = _expand_heads(jnp.exp(acs), es)
        decX = _expand_heads(jnp.exp(last - acs), es)
        for g in range(SSM_GROUPS):
            gs = slice(g * GROUP_W, (g + 1) * GROUP_W)
            B = x_ref[:, BO + g * SSM_STATE:BO + (g + 1) * SSM_STATE]
            C = x_ref[:, CO + g * SSM_STATE:CO + (g + 1) * SSM_STATE]
            X = x_ref[:, gs] * dtX[:, gs]
            CB = _nt(C, B)
            yd = []
            for r in range(GQ):
                h = GQ * g + r
                Lm = jnp.exp(jnp.where(tril > 0, acs[:, h:h + 1] - acsT[h:h + 1, :], NEG))
                yd.append(_nn(CB * Lm, X[:, r * HEAD_DIM:(r + 1) * HEAD_DIM]))
            S = state[g]
            st_ref[g] = S
            y_ref[:, gs] = jnp.concatenate(yd, axis=1) + _nt(C, S) * EX[:, gs]
            state[g] = S * _per_state_row(cd, g) + _tn(X * decX[:, gs], B)

    es, _ = _head_selectors()
    return pl.pallas_call(
        body, grid=(nc,),
        in_specs=[pl.BlockSpec((L, SSM_CONV_DIM), lambda c: (c, 0)), pl.BlockSpec((L, LANES), lambda c: (c, 0)),
                  pl.BlockSpec((1, LANES), lambda c: (0, 0)), pl.BlockSpec((1, LANES), lambda c: (0, 0)),
                  pl.BlockSpec((LANES, SSM_D_INNER), lambda c: (0, 0))],
        out_specs=[pl.BlockSpec((L, SSM_D_INNER), lambda c: (c, 0)),
                   pl.BlockSpec((None, SSM_GROUPS, GROUP_W, SSM_STATE), lambda c: (c, 0, 0, 0))],
        out_shape=[_sds((T, SSM_D_INNER)), _sds((nc, SSM_GROUPS, GROUP_W, SSM_STATE))],
        scratch_shapes=[pltpu.VMEM((SSM_GROUPS, GROUP_W, SSM_STATE), F32)],
        compiler_params=_cp("arbitrary"), name=name)(xact, dtraw, dt_bias, a_log, es)


def _ssd_bwd(xact, dtraw, dt_bias, a_log, d_skip, states, dy, name):
    T = xact.shape[0]
    nc = T // SSM_CHUNK
    L = SSM_CHUNK
    BO, CO = SSM_D_INNER, SSM_D_INNER + SSM_GROUPS * SSM_STATE

    def body(x_ref, dt_ref, bias_ref, al_ref, dsk_ref, es_ref, est_ref, st_ref, dy_ref,
             dxp_ref, ddt_ref, dbias_ref, dal_ref, dd_ref, dstate, qa, qx):
        cc = pl.program_id(0)

        @pl.when(cc == 0)
        def _():
            dstate[...] = jnp.zeros(dstate.shape, F32)

        xb, dt, A, tril, acs = _ssd_common(dt_ref[...], bias_ref[...], al_ref[...])
        acsT = acs.T
        last = acs[L - 1:L, :]
        cd = jnp.exp(last)
        es, est = es_ref[...], est_ref[...]
        dtX = _expand_heads(dt, es)
        EX = _expand_heads(jnp.exp(acs), es)
        decX = _expand_heads(jnp.exp(last - acs), es)
        lane1 = _iota((1, LANES), 1)
        lane = _iota((L, LANES), 1)
        sub = _iota((L, LANES), 0)
        ztot = jnp.zeros((1, LANES), F32)
        wrow = jnp.zeros((L, LANES), F32)
        wcolT = jnp.zeros((LANES, L), F32)
        rows_dec, rows_dd = [], []
        for g in range(SSM_GROUPS):
            gs = slice(g * GROUP_W, (g + 1) * GROUP_W)
            x = x_ref[:, gs]
            B = x_ref[:, BO + g * SSM_STATE:BO + (g + 1) * SSM_STATE]
            C = x_ref[:, CO + g * SSM_STATE:CO + (g + 1) * SSM_STATE]
            dY = dy_ref[:, gs]
            dtx, e_x, dec_x = dtX[:, gs], EX[:, gs], decX[:, gs]
            X = x * dtx
            CB = _nt(C, B)
            S = st_ref[g]
            dS_out = dstate[g]
            dcb_sum = jnp.zeros((L, L), F32)
            dxd = []
            for r in range(GQ):
                h = GQ * g + r
                hs = slice(r * HEAD_DIM, (r + 1) * HEAD_DIM)
                Lm = jnp.exp(jnp.where(tril > 0, acs[:, h:h + 1] - acsT[h:h + 1, :], NEG))
                M = CB * Lm
                dM = _nt(dY[:, hs], X[:, hs])
                dxd.append(_tn(M, dY[:, hs]))
                dcb_sum = dcb_sum + dM * Lm
                Wm = dM * M
                wrow = jnp.where(lane == h, jnp.sum(Wm, axis=1, keepdims=True), wrow)
                wcolT = jnp.where(sub == h, jnp.sum(Wm, axis=0, keepdims=True), wcolT)
            dXd = jnp.concatenate(dxd, axis=1)
            G = _nt(C, S)
            dG = dY * e_x
            dDX = _nt(B, dS_out)
            dX = dXd + dec_x * dDX
            t_dec = dDX * X * dec_x
            qa[:, gs] = dG * G - t_dec
            qx[:, gs] = dX * x
            rows_dec.append(jnp.sum(t_dec, axis=0, keepdims=True))
            rows_dd.append(jnp.sum(dY * x, axis=0, keepdims=True))
            zc = jnp.sum(dS_out * S, axis=1, keepdims=True)
            for r in range(GQ):
                ztot = jnp.where(lane1 == GQ * g + r, jnp.sum(zc[r * HEAD_DIM:(r + 1) * HEAD_DIM], axis=0, keepdims=True), ztot)
            dxp_ref[:, g * PERM_W:g * PERM_W + GROUP_W] = dX * dtx + dY * dsk_ref[:, gs]
            dxp_ref[:, g * PERM_W + GROUP_W:g * PERM_W + GROUP_W + SSM_STATE] = _tn(dcb_sum, C) + _nn(X * dec_x, dS_out)
            dxp_ref[:, g * PERM_W + GROUP_W + SSM_STATE:(g + 1) * PERM_W] = _nn(dcb_sum, B) + _nn(dG, S)
            dstate[g] = dS_out * _per_state_row(cd, g) + _tn(dG, C)
        rows = jnp.concatenate([jnp.concatenate(rows_dec, axis=1), jnp.concatenate(rows_dd, axis=1)]
                               + [jnp.zeros((SUBLANES - 2, SSM_D_INNER), F32)], axis=0)
        rsum = _reduce_heads(rows, est)
        dlast = rsum[0:1, :] + cd * ztot
        dacs = (wrow - wcolT.T) + _reduce_heads(qa[...], est) + jnp.where(sub == L - 1, dlast, 0.0)
        triu = (_iota((L, L), 0) <= _iota((L, L), 1)).astype(F32)
        da = _dot_hi(triu, dacs)
        ddtraw = (da * A + _reduce_heads(qx[...], est)) * (1.0 / (1.0 + jnp.exp(-xb)))
        ddt_ref[...] = ddtraw
        dal = jnp.sum(da * dt, axis=0, keepdims=True) * A
        ddp = rsum[1:2, :]
        dbp = jnp.sum(ddtraw, axis=0, keepdims=True)

        @pl.when(cc == 0)
        def _():
            dbias_ref[...] = dbp
            dal_ref[...] = dal
            dd_ref[...] = ddp

        @pl.when(cc > 0)
        def _():
            dbias_ref[...] += dbp
            dal_ref[...] += dal
            dd_ref[...] += ddp

    rc = lambda c: nc - 1 - c
    vec = pl.BlockSpec((1, LANES), lambda c: (0, 0))
    es, est = _head_selectors()
    return pl.pallas_call(
        body, grid=(nc,),
        in_specs=[pl.BlockSpec((L, SSM_CONV_DIM), lambda c: (rc(c), 0)), pl.BlockSpec((L, LANES), lambda c: (rc(c), 0)), vec, vec,
                  pl.BlockSpec((1, SSM_D_INNER), lambda c: (0, 0)),
                  pl.BlockSpec((LANES, SSM_D_INNER), lambda c: (0, 0)), pl.BlockSpec((SSM_D_INNER, LANES), lambda c: (0, 0)),
                  pl.BlockSpec((None, SSM_GROUPS, GROUP_W, SSM_STATE), lambda c: (rc(c), 0, 0, 0)),
                  pl.BlockSpec((L, SSM_D_INNER), lambda c: (rc(c), 0))],
        out_specs=[pl.BlockSpec((L, SSM_GROUPS * PERM_W), lambda c: (rc(c), 0)),
                   pl.BlockSpec((L, LANES), lambda c: (rc(c), 0)), vec, vec, vec],
        out_shape=[_sds((T, SSM_GROUPS * PERM_W)), _sds((T, LANES)), _sds((1, LANES)), _sds((1, LANES)), _sds((1, LANES))],
        scratch_shapes=[pltpu.VMEM((SSM_GROUPS, GROUP_W, SSM_STATE), F32), pltpu.VMEM((L, SSM_D_INNER), F32),
                        pltpu.VMEM((L, SSM_D_INNER), F32)],
        compiler_params=_cp("arbitrary"), name=name)(xact, dtraw, dt_bias, a_log, d_skip, es, est, states, dy)


def _ssm_post_fwd(y, xact, z, d_skip, nw, name):
    T = y.shape[0]
    tm = min(T, 256)
    W = SSM_D_INNER

    def body(y_ref, x_ref, z_ref, d_ref, w_ref, o_ref):
        y2 = (y_ref[...] + d_ref[...] * x_ref[...]) * _silu(z_ref[...])
        r = lax.rsqrt(jnp.mean(y2 * y2, axis=-1, keepdims=True) + SSM_NORM_EPS)
        o_ref[...] = (y2 * r * w_ref[...]).astype(o_ref.dtype)

    row = pl.BlockSpec((tm, W), lambda i: (i, 0))
    vec = pl.BlockSpec((1, W), lambda i: (0, 0))
    return pl.pallas_call(
        body, grid=(T // tm,), in_specs=[row, row, row, vec, vec], out_specs=row, out_shape=_sds((T, W), MXU),
        compiler_params=_cp("parallel"), name=name)(y, xact, z, d_skip, nw)


def _ssm_post_bwd(y, xact, z, d_skip, nw, dyn, name):
    T = y.shape[0]
    tm = min(T, 256)
    W = SSM_D_INNER

    def body(y_ref, x_ref, z_ref, d_ref, w_ref, dn_ref, dyg_ref, dz_ref, dw_ref):
        zv = z_ref[...]
        sz = _silu(zv)
        yg = y_ref[...] + d_ref[...] * x_ref[...]
        y2 = yg * sz
        r = lax.rsqrt(jnp.mean(y2 * y2, axis=-1, keepdims=True) + SSM_NORM_EPS)
        y2h = y2 * r
        dn = dn_ref[...]
        gy = dn * w_ref[...]
        dy2 = r * (gy - y2h * jnp.mean(gy * y2h, axis=-1, keepdims=True))
        dyg_ref[...] = dy2 * sz
        dz_ref[...] = (dy2 * yg * _dsilu(zv)).astype(dz_ref.dtype)
        part = jnp.sum(dn * y2h, axis=0, keepdims=True)

        @pl.when(pl.program_id(0) == 0)
        def _():
            dw_ref[...] = part

        @pl.when(pl.program_id(0) > 0)
        def _():
            dw_ref[...] += part

    row = pl.BlockSpec((tm, W), lambda i: (i, 0))
    vec = pl.BlockSpec((1, W), lambda i: (0, 0))
    return pl.pallas_call(
        body, grid=(T // tm,), in_specs=[row, row, row, vec, vec, row], out_specs=[row, row, vec],
        out_shape=[_sds((T, W)), _sds((T, W), MXU), _sds((1, W))],
        compiler_params=_cp("arbitrary"), name=name)(y, xact, z, d_skip, nw, dyn)


def _local_step(x0, cos, sin_s, target, P, fetch, token):
    mmf = functools.partial(_mm, tm=1024)
    big, small = {}, {}
    P = dict(P, wup={}, wdn={}, fcw={})
    h0 = _rmsnorm_fwd(x0, P["nm"][0], "norm_mix0", token=token)
    proj0 = mmf(h0, P["wmi"], tn=1280, tk=1024, name="mix_in")
    cat, attn, lse = _mixcore_fwd(proj0, cos, sin_s, P["pool_w"], P["pool_scale"], P["sinks"], "mixcore_fwd")
    x1 = mmf(cat, P["wmo"], tn=1024, tk=1024, res=x0, name="mix_out")

    def ffn_fwd(xin, i):
        hf = _rmsnorm_fwd(xin, P["nf"][i], f"norm_ffn{i}")
        got = fetch(f"ffn{i}", hf)
        P["wup"][i], P["wdn"][i], P["fcw"][i] = got["wup"], got["wdn"], got["fcw"]
        hid = mmf(hf, P["wup"][i], tn=1408, tk=1024, name=f"ffn_up{i}")
        act = _ffn_mid_fwd(hid, P["fcw"][i], P["fcb"][i], f"ffn_mid_fwd{i}")
        xout = mmf(act, P["wdn"][i], tn=1024, tk=D_FF, res=xin, name=f"ffn_down{i}")
        return hf, hid, act, xout

    hf0, hid0, act0, x2 = ffn_fwd(x1, 0)
    h1 = _rmsnorm_fwd(x2, P["nm"][1], "norm_mix1")
    P.update(fetch("ssm", h1))
    z = mmf(h1, P["wz"], tn=1024, tk=1024, name="ssm_in_z")
    xbc = mmf(h1, P["wxbc"], tn=1024, tk=1024, name="ssm_in_xbc")
    dtraw = mmf(h1, P["wdt"], tn=128, tk=1024, name="ssm_in_dt")
    xact = _ssm_pre_fwd(xbc, P["scw"], P["scb"], "ssm_pre_fwd")
    y, states = _ssd_fwd(xact, dtraw, P["dt_bias"], P["a_log"], "ssd_fwd")
    yn = _ssm_post_fwd(y, xact, z, P["d_exp"], P["snorm"], "ssm_post_fwd")
    x3 = mmf(yn, P["wso"], tn=1024, tk=SSM_D_INNER, res=x2, name="ssm_out")
    hf1, hid1, act1, x4 = ffn_fwd(x3, 1)
    loss_row, dx4, d_nfin = _loss_head(x4, P["nfin"], target, "loss_head")
    small["norm_final"] = d_nfin

    def ffn_bwd(xin, dxo, hf, hid, act, i):
        da = mmf(dxo, P["wdn"][i], tb=True, tn=1408, tk=1024, name=f"ffn_down_dx{i}")
        big[f"ffn_w_down{i}"] = _mm(act, dxo, ta=True, tm=1408, tn=1024, tk=1024, name=f"ffn_down_dw{i}").reshape(N_CHIPS, D_FF // N_CHIPS, D_MODEL)
        dhid, dcw, dcb = _ffn_mid_bwd(hid, P["fcw"][i], P["fcb"][i], da, f"ffn_mid_bwd{i}")
        dhf = mmf(dhid, P["wup"][i], tb=True, tn=1024, tk=1408, name=f"ffn_up_dx{i}")
        big[f"ffn_w_up{i}"] = _mm(hf, dhid, ta=True, tm=1024, tn=1408, tk=1024, out_shard_perm=(0, 2, 1, 3), name=f"ffn_up_dw{i}")
        dxi, dnf = _rmsnorm_bwd(xin, P["nf"][i], dhf, dxo, f"norm_ffn_bwd{i}")
        return dxi, dnf, dcw, dcb

    dx3, dnf1, dfcw1, dfcb1 = ffn_bwd(x3, dx4, hf1, hid1, act1, 1)
    dyn = mmf(dx3, P["wso"], tb=True, tn=1024, tk=1024, name="ssm_out_dx")
    big["ssm_w_out"] = _mm(yn, dx3, ta=True, tm=1024, tn=1024, tk=1024, name="ssm_out_dw").reshape(N_CHIPS, SSM_D_INNER // N_CHIPS, D_MODEL)
    dyg, dz, d_snorm = _ssm_post_bwd(y, xact, z, P["d_exp"], P["snorm"], dyn, "ssm_post_bwd")
    dxact_p, ddtraw, d_dtb, d_alog, d_dskip = _ssd_bwd(xact, dtraw, P["dt_bias"], P["a_log"], P["d_exp"], states, dyg, "ssd_bwd")
    dxbc, d_scw, d_scb = _ssm_pre_bwd(xbc, P["scw"], P["scb"], dxact_p, "ssm_pre_bwd")
    dh1 = mmf(dz, P["wz"], tb=True, tn=1024, tk=1024, name="ssm_in_dx_z")
    dh1 = mmf(dxbc, P["wxbc"], tb=True, tn=1024, tk=1024, res=dh1, name="ssm_in_dx_xbc")
    dh1 = mmf(ddtraw, P["wdt"], tb=True, tn=1024, tk=128, res=dh1, name="ssm_in_dx_dt")
    dwz = _mm(h1, dz, ta=True, tm=1024, tn=1024, tk=1024, name="ssm_in_dw_z")
    dwxbc = _mm(h1, dxbc, ta=True, tm=1024, tn=1024, tk=1024, name="ssm_in_dw_xbc")
    dwdt = _mm(h1, ddtraw, ta=True, tm=1024, tn=128, tk=1024, name="ssm_in_dw_dt")
    dwsi = jnp.concatenate([dwz, dwxbc, dwdt[:, :SSM_HEADS]], axis=1)
    big["ssm_w_in"] = dwsi.reshape(D_MODEL, N_CHIPS, SSM_IN_DIM // N_CHIPS).transpose(1, 0, 2)
    dx2, dnm1 = _rmsnorm_bwd(x2, P["nm"][1], dh1, dx3, "norm_mix_bwd1")
    dx1, dnf0, dfcw0, dfcb0 = ffn_bwd(x1, dx2, hf0, hid0, act0, 0)
    dcat = mmf(dx1, P["wmo"], tb=True, tn=1024, tk=1024, name="mix_out_dx")
    big["mix_w_out"] = _mm(cat, dx1, ta=True, tm=1024, tn=1024, tk=1024, name="mix_out_dw").reshape(N_CHIPS, D_MODEL // N_CHIPS, D_MODEL)
    dproj0, d_pw, d_ps, d_sk = _mixcore_bwd(proj0, cos, sin_s, P["pool_w"], P["pool_scale"], P["sinks"], attn, lse, dcat, "mixcore_bwd")
    dh0 = mmf(dproj0, P["wmi"], tb=True, tn=1024, tk=1280, name="mix_in_dx")
    dwmi = _mm(h0, dproj0, ta=True, tm=1024, tn=1280, tk=1024, name="mix_in_dw")
    big["mix_w_in"] = dwmi.reshape(D_MODEL, N_CHIPS, MIX_IN_DIM // N_CHIPS).transpose(1, 0, 2)
    dx0, dnm0 = _rmsnorm_bwd(x0, P["nm"][0], dh0, dx1, "norm_mix_bwd0")

    def unperm_cols(a):
        r = a.shape[0]
        t = a.reshape(r, N_CHIPS, FFN_TC)
        return jnp.stack([t[:, p] for p in _PERM], axis=0)

    small["norm_mix"] = jnp.concatenate([dnm0, dnm1], axis=0)
    small["norm_ffn"] = jnp.concatenate([dnf0, dnf1], axis=0)
    small["pool_w"] = d_pw.reshape(4 * POOL_GROUP, POOL_GROUP)
    small["pool_scale"] = d_ps
    small["attn_sinks"] = d_sk
    small["ssm_dt_bias"] = d_dtb
    small["ssm_A_log"] = d_alog
    small["ssm_D"] = d_dskip
    fcb = jnp.stack([unperm_cols(dfcb0), unperm_cols(dfcb1)], axis=0)
    small["ffn_conv_b"] = fcb.reshape(2, 2 * D_FF)
    small["ssm_conv_w"] = d_scw.reshape(SSM_CONV, N_CHIPS, SSM_CONV_DIM // N_CHIPS).transpose(1, 0, 2)
    small["ssm_conv_b"] = d_scb.reshape(N_CHIPS, 1, SSM_CONV_DIM // N_CHIPS)
    small["ssm_norm"] = d_snorm.reshape(N_CHIPS, 1, SSM_D_INNER // N_CHIPS)
    small["ffn_conv_w"] = jnp.concatenate([unperm_cols(dfcw0), unperm_cols(dfcw1)], axis=1)
    return loss_row, dx0, big, small


ANY = pl.BlockSpec(memory_space=pl.ANY)


def _place():
    return lax.axis_index("x"), lax.axis_index("y"), lax.axis_index("c")


def _gather_shards(shards, name):
    n = len(shards)
    split = [s.size >= (1 << 16) for s in shards]

    def half(ref, a, h):
        shp = shards[a].shape
        if len(shp) == 3:
            return ref.at[h]
        r2 = shp[0] // 2
        return ref.at[pl.ds(pl.multiple_of(h * r2, 2 * SUBLANES), r2), :]

    def body(*refs):
        ins, outs = refs[:n], refs[n:2 * n]
        send, recv, fsend, frecv = refs[2 * n:]
        x, y, c = _place()
        k = 2 * x + y
        chips = [(1 - x, y), (x, 1 - y), (1 - x, 1 - y)]

        def ici(a, j, src_slot_ref, dst_slot):
            px, py = chips[j]
            src = half(src_slot_ref, a, c) if split[a] else src_slot_ref
            dst = half(outs[a].at[dst_slot], a, c) if split[a] else outs[a].at[dst_slot]
            return pltpu.make_async_remote_copy(src, dst, send.at[a, j], recv.at[a, j], device_id=(px, py, c), device_id_type=MESH)

        def d2d(a, j, h):
            px, py = chips[j]
            part = half(outs[a].at[2 * px + py], a, h)
            return pltpu.make_async_remote_copy(part, part, fsend.at[a, j], frecv.at[a, j], device_id=(x, y, 1 - c), device_id_type=MESH)

        sends = [ici(a, j, ins[a], k) for a in range(n) for j in range(3)]
        for cp in sends:
            cp.start()
        passed = []
        for a in range(n):
            for j, (px, py) in enumerate(chips):
                ici(a, j, ins[a], 2 * px + py).wait_recv()
                if split[a]:
                    passed.append(d2d(a, j, c))
                    passed[-1].start()
        for a in range(n):
            if split[a]:
                for j in range(3):
                    d2d(a, j, 1 - c).wait_recv()
        for cp in sends + passed:
            cp.wait_send()

    return pl.pallas_call(
        body, in_specs=[ANY] * n, out_specs=[ANY] * n,
        out_shape=[_sds((N_CHIPS,) + s.shape, s.dtype) for s in shards],
        scratch_shapes=[pltpu.SemaphoreType.DMA((n, 3))] * 4,
        compiler_params=pltpu.CompilerParams(has_side_effects=True), name=name)(*shards)


HBM = pl.BlockSpec(memory_space=pltpu.HBM)
SEM = pl.BlockSpec(memory_space=pltpu.SEMAPHORE)
DATAFLOW = pltpu.SideEffectType.DATAFLOW_SIDE_EFFECTING


def _spread_start(groups, slot_src, after, name):
    flat = [a for grp in groups for a in grp]
    n = len(flat)
    ng = len(groups)
    offs = [sum(len(g) for g in groups[:i]) for i in range(ng)]
    lshape = [(a.shape if slot_src else (N_CHIPS,) + a.shape) for a in flat]

    nsem = 6 * n

    def body(*refs):
        src, land = refs[:n], refs[n:2 * n]
        sems = refs[2 * n + 1:2 * n + 1 + nsem]
        token = refs[-1]
        x, y, c = _place()
        k = 2 * x + y
        chips = [(1 - x, y), (x, 1 - y), (1 - x, 1 - y)]
        for a in range(n):
            for j, (px, py) in enumerate(chips):
                s = src[a].at[2 * px + py] if slot_src else src[a]
                pltpu.make_async_remote_copy(s, land[a].at[k], sems[6 * a + 2 * j], sems[6 * a + 2 * j + 1],
                                             device_id=(px, py, c), device_id_type=MESH).start()
        token[...] = jnp.zeros(token.shape, token.dtype)

    out_shape = [pltpu.SemaphoreType.DMA(())] * nsem
    out_shape += [pltpu.HBM(a.shape, a.dtype) for a in flat] + [pltpu.HBM(s, a.dtype) for s, a in zip(lshape, flat)]
    out_shape.append(_sds((SUBLANES, LANES)))
    args = [pltpu.with_memory_space_constraint(a, pltpu.HBM) for a in flat]
    args += [pltpu.with_memory_space_constraint(lax.empty(s, a.dtype), pltpu.HBM) for s, a in zip(lshape, flat)]
    res = pl.pallas_call(
        body, name=name, out_shape=tuple(out_shape), in_specs=[HBM] * (2 * n) + [pl.BlockSpec(memory_space=pl.ANY)],
        out_specs=tuple([SEM] * nsem + [HBM] * (2 * n) + [pl.BlockSpec(memory_space=pltpu.VMEM)]),
        input_output_aliases={i: nsem + i for i in range(2 * n)},
        compiler_params=pltpu.CompilerParams(has_side_effects=DATAFLOW))(*args, after)
    sems, thru, token = res[:nsem], res[nsem:nsem + 2 * n], res[-1]
    out = []
    for gi, grp in enumerate(groups):
        sl = slice(offs[gi], offs[gi] + len(grp))
        out.append((list(sems[6 * offs[gi]:6 * (offs[gi] + len(grp))]), list(thru[:n][sl]), list(thru[n:][sl])))
    return out, token


def _spread_wait(started, slot_src, after, name):
    sems, srcs, lands = started
    n = len(srcs)

    def body(*refs):
        src, land = refs[:n], refs[n:2 * n]
        sem = refs[2 * n:2 * n + 6 * n]
        x, y, c = _place()
        chips = [(1 - x, y), (x, 1 - y), (1 - x, 1 - y)]
        for a in range(n):
            for j, (px, py) in enumerate(chips):
                s = src[a].at[2 * px + py] if slot_src else src[a]
                cp = pltpu.make_async_remote_copy(s, land[a].at[2 * px + py], sem[6 * a + 2 * j], sem[6 * a + 2 * j + 1],
                                                  device_id=(px, py, c), device_id_type=MESH)
                cp.wait_send()
                cp.wait_recv()

    res = pl.pallas_call(
        body, name=name, out_shape=tuple([pltpu.HBM(a.shape, a.dtype) for a in srcs] + [pltpu.HBM(a.shape, a.dtype) for a in lands]),
        in_specs=[HBM] * (2 * n) + [SEM] * (6 * n) + [pl.BlockSpec(memory_space=pl.ANY)], out_specs=tuple([HBM] * (2 * n)),
        input_output_aliases={i: i for i in range(2 * n)},
        compiler_params=pltpu.CompilerParams(has_side_effects=DATAFLOW))(*srcs, *lands, *sems, after)
    return list(res[n:])


def _pair_exchange(gs, name):
    n = len(gs)

    def body(*refs):
        ins, outs = refs[:n], refs[n:2 * n]
        send, recv = refs[2 * n:]
        x, y, c = _place()
        cps = []
        for a in range(n):
            r2 = gs[a].shape[1] // 2
            src = ins[a].at[:, pl.ds(pl.multiple_of((1 - c) * r2, SUBLANES), r2), :]
            cps.append(pltpu.make_async_remote_copy(src, outs[a], send.at[a], recv.at[a],
                                                    device_id=(x, y, 1 - c), device_id_type=MESH))
        for cp in cps:
            cp.start()
        for cp in cps:
            cp.wait()

    return pl.pallas_call(
        body, in_specs=[ANY] * n, out_specs=[ANY] * n,
        out_shape=[_sds((N_CHIPS, g.shape[1] // 2, g.shape[2]), g.dtype) for g in gs],
        scratch_shapes=[pltpu.SemaphoreType.DMA((n,)), pltpu.SemaphoreType.DMA((n,))],
        compiler_params=pltpu.CompilerParams(has_side_effects=True), name=name)(*gs)


def _chip_exchange(ps, name):
    n = len(ps)

    def body(*refs):
        ins, outs = refs[:n], refs[n:2 * n]
        send, recv = refs[2 * n:]
        x, y, c = _place()
        k = 2 * x + y
        chips = [(1 - x, y), (x, 1 - y), (1 - x, 1 - y)]
        sends = [pltpu.make_async_remote_copy(ins[a].at[2 * px + py], outs[a].at[k], send.at[a, j], recv.at[a, j],
                                              device_id=(px, py, c), device_id_type=MESH)
                 for a in range(n) for j, (px, py) in enumerate(chips)]
        for cp in sends:
            cp.start()
        for a in range(n):
            for j, (px, py) in enumerate(chips):
                pltpu.make_async_remote_copy(ins[a].at[k], outs[a].at[2 * px + py], send.at[a, j], recv.at[a, j],
                                             device_id=(px, py, c), device_id_type=MESH).wait_recv()
        for cp in sends:
            cp.wait_send()

    return pl.pallas_call(
        body, in_specs=[ANY] * n, out_specs=[ANY] * n, out_shape=[_sds(p.shape, p.dtype) for p in ps],
        scratch_shapes=[pltpu.SemaphoreType.DMA((n, 3)), pltpu.SemaphoreType.DMA((n, 3))],
        compiler_params=pltpu.CompilerParams(has_side_effects=True), name=name)(*ps)


def _half_exchange(fs, name):
    n = len(fs)

    def body(*refs):
        ins, outs = refs[:n], refs[n:2 * n]
        send, recv = refs[2 * n:]
        x, y, c = _place()
        cps = [pltpu.make_async_remote_copy(ins[a], outs[a], send.at[a], recv.at[a],
                                            device_id=(x, y, 1 - c), device_id_type=MESH) for a in range(n)]
        for cp in cps:
            cp.start()
        for cp in cps:
            cp.wait()

    return pl.pallas_call(
        body, in_specs=[ANY] * n, out_specs=[ANY] * n, out_shape=[_sds(f.shape, f.dtype) for f in fs],
        scratch_shapes=[pltpu.SemaphoreType.DMA((n,)), pltpu.SemaphoreType.DMA((n,))],
        compiler_params=pltpu.CompilerParams(has_side_effects=True), name=name)(*fs)


def _row_tile(rows, cols, budget=2 * 1024 * 1024, step=2 * SUBLANES):
    best = step
    for t in range(step, rows + 1, step):
        if rows % t == 0 and t * cols * 4 <= budget:
            best = t
    assert rows % best == 0, (rows, best)
    return best


def _pair_sum(g, got, cidx, name):
    _, R, C = g.shape
    r2 = R // 2
    tr = _row_tile(r2, C)
    nr = r2 // tr

    def body(c_ref, g_ref, o_ref_in, o_ref):
        o_ref[...] = (g_ref[...] + o_ref_in[...]).astype(o_ref.dtype)

    return pl.pallas_call(
        body,
        grid_spec=pltpu.PrefetchScalarGridSpec(
            num_scalar_prefetch=1, grid=(N_CHIPS, nr),
            in_specs=[pl.BlockSpec((None, tr, C), lambda k, i, c: (k, c[0] * nr + i, 0)),
                      pl.BlockSpec((None, tr, C), lambda k, i, c: (k, i, 0))],
            out_specs=pl.BlockSpec((None, tr, C), lambda k, i, c: (k, i, 0))),
        out_shape=_sds((N_CHIPS, r2, C), BF16), compiler_params=_cp("parallel", "parallel"), name=name)(cidx, g, got)


def _chip_sum(own, parts, kidx, name):
    _, r2, C = parts.shape
    tr = _row_tile(r2, C, budget=1024 * 1024)

    def body(k_ref, o_ref_in, p1_ref, p2_ref, p3_ref, o_ref):
        o_ref[...] = ((o_ref_in[...].astype(F32) + p1_ref[...].astype(F32)) + p2_ref[...].astype(F32)) + p3_ref[...].astype(F32)

    def slot(d):
        return pl.BlockSpec((None, tr, C), lambda i, k: ((k[0] + d) % N_CHIPS, i, 0))

    return pl.pallas_call(
        body,
        grid_spec=pltpu.PrefetchScalarGridSpec(
            num_scalar_prefetch=1, grid=(r2 // tr,), in_specs=[slot(0), slot(1), slot(2), slot(3)],
            out_specs=pl.BlockSpec((tr, C), lambda i, k: (i, 0))),
        out_shape=_sds((r2, C)), compiler_params=_cp("parallel"), name=name)(kidx, own, parts, parts, parts)


def _adamw_math(w, g, m, v):
    m2 = ADAM_B1 * m + (1.0 - ADAM_B1) * g
    v2 = ADAM_B2 * v + (1.0 - ADAM_B2) * (g * g)
    m_hat = m2 / (1.0 - ADAM_B1 ** ADAM_STEP)
    v_hat = v2 / (1.0 - ADAM_B2 ** ADAM_STEP)
    delta = -ADAM_LR * (m_hat / (jnp.sqrt(v_hat) + ADAM_EPS) + ADAM_WD * w)
    return delta, m2, v2


def _adamw(w, m, v, gparts, cidx, name):
    Lw, R, C = w.shape
    r2 = R // 2
    tr = _row_tile(r2, C, budget=1024 * 1024)
    nr = r2 // tr
    flat = [h for pair in gparts for h in pair]

    def body(*refs):
        c_ref = refs[0]
        w_ref, m_ref, v_ref = refs[1:4]
        g_refs = refs[4:4 + 2 * Lw]
        go_ref, d_ref, mo_ref, vo_ref = refs[4 + 2 * Lw:]
        mine = (pl.program_id(1) // nr) == c_ref[0]
        g = jnp.where(mine, g_refs[0][...], g_refs[1][...])
        for l in range(1, Lw):
            g = jnp.where(pl.program_id(0) == l, jnp.where(mine, g_refs[2 * l][...], g_refs[2 * l + 1][...]), g)
        d, m2, v2 = _adamw_math(w_ref[...], g, m_ref[...], v_ref[...])
        go_ref[...] = g
        d_ref[...] = d
        mo_ref[...] = m2
        vo_ref[...] = v2

    blk = pl.BlockSpec((None, tr, C), lambda l, i, c: (l, i, 0))
    gblk = pl.BlockSpec((tr, C), lambda l, i, c: (i % nr, 0))
    return pl.pallas_call(
        body,
        grid_spec=pltpu.PrefetchScalarGridSpec(
            num_scalar_prefetch=1, grid=(Lw, 2 * nr), in_specs=[blk, blk, blk] + [gblk] * (2 * Lw), out_specs=[blk] * 4),
        out_shape=[_sds((Lw, R, C))] * 4, compiler_params=_cp("parallel", "parallel"), name=name)(cidx, w, m, v, *flat)


def _small_reduce_adamw(items, loss_row, name):
    n = len(items)
    gshapes = [it[0].shape for it in items] + [loss_row.shape]
    pshapes = [it[1].shape for it in items]
    ng = n + 1

    def body(*refs):
        g_in = refs[:ng]
        wmv = refs[ng:ng + 3 * n]
        outs = refs[ng + 3 * n:ng + 3 * n + 4 * n + 1]
        bufs = refs[ng + 7 * n + 1:ng + 7 * n + 1 + ng]
        send, recv = refs[-2:]
        x, y, c = _place()
        me = 4 * x + 2 * y + c
        k = 2 * x + y
        flips = [(fx, fy, fc) for fx in (0, 1) for fy in (0, 1) for fc in (0, 1)][1:]

        def peer(f):
            return (x ^ f[0], y ^ f[1], c ^ f[2])

        def slot(p):
            return 4 * p[0] + 2 * p[1] + p[2]

        for a in range(ng):
            bufs[a][me] = g_in[a][...]
        sends = [pltpu.make_async_remote_copy(g_in[a], bufs[a].at[me], send.at[a, j], recv.at[a, j],
                                              device_id=peer(f), device_id_type=MESH)
                 for a in range(ng) for j, f in enumerate(flips)]
        for cp in sends:
            cp.start()
        for a in range(ng):
            for j, f in enumerate(flips):
                pltpu.make_async_remote_copy(g_in[a], bufs[a].at[slot(peer(f))], send.at[a, j], recv.at[a, j],
                                             device_id=peer(f), device_id_type=MESH).wait_recv()
        for cp in sends:
            cp.wait_send()
        for a in range(ng):
            sharded = len(gshapes[a]) == 3

            def part(d):
                return bufs[a][d, k] if sharded else bufs[a][d]

            tot = part(0)
            for d in range(1, N_DEV):
                tot = tot + part(d)
            if a == n:
                outs[4 * n][...] = tot
                continue
            pr, pc = pshapes[a]
            g = tot[:pr, :pc]
            w_ref, m_ref, v_ref = wmv[3 * a:3 * a + 3]
            d_, m2, v2 = _adamw_math(w_ref[...], g, m_ref[...], v_ref[...])
            outs[4 * a][...] = g
            outs[4 * a + 1][...] = d_
            outs[4 * a + 2][...] = m2
            outs[4 * a + 3][...] = v2

    vm = pl.BlockSpec(memory_space=pltpu.VMEM)
    args = [it[0] for it in items] + [loss_row]
    for it in items:
        args += [it[1], it[2], it[3]]
    out_shape = []
    for ps in pshapes:
        out_shape += [_sds(ps)] * 4
    out_shape.append(_sds(loss_row.shape))
    return pl.pallas_call(
        body, in_specs=[vm] * len(args), out_specs=[vm] * len(out_shape), out_shape=out_shape,
        scratch_shapes=[pltpu.VMEM((N_DEV,) + tuple(s), F32) for s in gshapes]
        + [pltpu.SemaphoreType.DMA((ng, N_DEV - 1)), pltpu.SemaphoreType.DMA((ng, N_DEV - 1))],
        compiler_params=pltpu.CompilerParams(has_side_effects=True, vmem_limit_bytes=V7X_VMEM_LIMIT), name=name)(*args)


_PERM = (0, 2, 1, 3)


def _cols_from_shards(g):
    return g.transpose(1, 0, 2).reshape(g.shape[1], N_CHIPS * g.shape[2])


def _rope_tables(positions):
    inv_freq = ROPE_THETA ** (-jnp.arange(0, HEAD_DIM, 2, dtype=F32) / HEAD_DIM)
    ang = positions.astype(F32).reshape(-1, 1) * inv_freq
    cos, sin = jnp.cos(ang), jnp.sin(ang)
    cos = jnp.concatenate([cos, cos, cos, cos], axis=-1)
    sin_s = jnp.concatenate([-sin, sin, -sin, sin], axis=-1)
    return cos, sin_s


def kernel(x, positions, norm_mix, norm_ffn, norm_final, mix_w_in, pool_w, pool_scale, attn_sinks, mix_w_out, ssm_w_in, ssm_conv_w, ssm_conv_b, ssm_dt_bias, ssm_A_log, ssm_D, ssm_norm, ssm_w_out, ffn_w_up, ffn_conv_w, ffn_conv_b, ffn_w_down, loss_target, m_norm_mix, m_norm_ffn, m_norm_final, m_mix_w_in, m_pool_w, m_pool_scale, m_attn_sinks, m_mix_w_out, m_ssm_w_in, m_ssm_conv_w, m_ssm_conv_b, m_ssm_dt_bias, m_ssm_A_log, m_ssm_D, m_ssm_norm, m_ssm_w_out, m_ffn_w_up, m_ffn_conv_w, m_ffn_conv_b, m_ffn_w_down, v_norm_mix, v_norm_ffn, v_norm_final, v_mix_w_in, v_pool_w, v_pool_scale, v_attn_sinks, v_mix_w_out, v_ssm_w_in, v_ssm_conv_w, v_ssm_conv_b, v_ssm_dt_bias, v_ssm_A_log, v_ssm_D, v_ssm_norm, v_ssm_w_out, v_ffn_w_up, v_ffn_conv_w, v_ffn_conv_b, v_ffn_w_down):
    W = dict(norm_mix=norm_mix, norm_ffn=norm_ffn, norm_final=norm_final, mix_w_in=mix_w_in, pool_w=pool_w, pool_scale=pool_scale, attn_sinks=attn_sinks, mix_w_out=mix_w_out, ssm_w_in=ssm_w_in, ssm_conv_w=ssm_conv_w, ssm_conv_b=ssm_conv_b, ssm_dt_bias=ssm_dt_bias, ssm_A_log=ssm_A_log, ssm_D=ssm_D, ssm_norm=ssm_norm, ssm_w_out=ssm_w_out, ffn_w_up=ffn_w_up, ffn_conv_w=ffn_conv_w, ffn_conv_b=ffn_conv_b, ffn_w_down=ffn_w_down)
    Mo = dict(norm_mix=m_norm_mix, norm_ffn=m_norm_ffn, norm_final=m_norm_final, mix_w_in=m_mix_w_in, pool_w=m_pool_w, pool_scale=m_pool_scale, attn_sinks=m_attn_sinks, mix_w_out=m_mix_w_out, ssm_w_in=m_ssm_w_in, ssm_conv_w=m_ssm_conv_w, ssm_conv_b=m_ssm_conv_b, ssm_dt_bias=m_ssm_dt_bias, ssm_A_log=m_ssm_A_log, ssm_D=m_ssm_D, ssm_norm=m_ssm_norm, ssm_w_out=m_ssm_w_out, ffn_w_up=m_ffn_w_up, ffn_conv_w=m_ffn_conv_w, ffn_conv_b=m_ffn_conv_b, ffn_w_down=m_ffn_w_down)
    Vo = dict(norm_mix=v_norm_mix, norm_ffn=v_norm_ffn, norm_final=v_norm_final, mix_w_in=v_mix_w_in, pool_w=v_pool_w, pool_scale=v_pool_scale, attn_sinks=v_attn_sinks, mix_w_out=v_mix_w_out, ssm_w_in=v_ssm_w_in, ssm_conv_w=v_ssm_conv_w, ssm_conv_b=v_ssm_conv_b, ssm_dt_bias=v_ssm_dt_bias, ssm_A_log=v_ssm_A_log, ssm_D=v_ssm_D, ssm_norm=v_ssm_norm, ssm_w_out=v_ssm_w_out, ffn_w_up=v_ffn_w_up, ffn_conv_w=v_ffn_conv_w, ffn_conv_b=v_ffn_conv_b, ffn_w_down=v_ffn_w_down)

    kchip = 2 * lax.axis_index("x") + lax.axis_index("y")

    def own_slot(g, own):
        return lax.dynamic_update_slice_in_dim(g, own[None], kchip, axis=0)

    later = dict(ffn0=[ffn_w_up[0].astype(MXU), ffn_w_down[0].astype(MXU)],
                 ssm=[ssm_w_in[0].astype(MXU), ssm_w_out[0].astype(MXU)],
                 ffn1=[ffn_w_up[1].astype(MXU), ffn_w_down[1].astype(MXU)])
    sh = [mix_w_in[0].astype(MXU), mix_w_out[0].astype(MXU), ssm_conv_w[0], ssm_conv_b, ssm_norm, ffn_conv_w]
    first = _gather_shards(sh, "gather_first")
    g_mi, g_mo, g_scw, g_scb, g_sn, g_fcw = [own_slot(g, own) for g, own in zip(first, sh)]
    started, token = _spread_start(list(later.values()), False, first[0], "gather_start")
    started = dict(zip(later.keys(), started))
    fcw = [jnp.concatenate([g_fcw[p, i] for p in _PERM], axis=1) for i in range(2)]
    P = dict(
        nm=norm_mix, nf=norm_ffn, nfin=norm_final,
        wmi=_cols_from_shards(g_mi), wmo=g_mo.reshape(D_MODEL, D_MODEL),
        pool_w=pool_w[0], pool_scale=pool_scale, sinks=attn_sinks[0],
        scw=_cols_from_shards(g_scw), scb=g_scb.reshape(1, SSM_CONV_DIM), snorm=g_sn.reshape(1, SSM_D_INNER),
        dt_bias=jnp.pad(ssm_dt_bias, ((0, 0), (0, LANES - SSM_HEADS))), a_log=jnp.pad(ssm_A_log, ((0, 0), (0, LANES - SSM_HEADS))),
        d_exp=jnp.repeat(ssm_D, SSM_D_INNER // SSM_HEADS, axis=1),
        fcb=[jnp.concatenate([ffn_conv_b[i:i + 1, p * FFN_TC:(p + 1) * FFN_TC] for p in _PERM], axis=1) for i in range(2)],
    )

    def fetch(group, after):
        lands = _spread_wait(started[group], False, after, f"gather_wait_{group}")
        a, b = [own_slot(g, own) for g, own in zip(lands, later[group])]
        if group == "ssm":
            wsi = _cols_from_shards(a)
            zx = SSM_D_INNER + SSM_CONV_DIM
            return dict(wz=wsi[:, :SSM_D_INNER], wxbc=wsi[:, SSM_D_INNER:zx],
                        wdt=jnp.pad(wsi[:, zx:], ((0, 0), (0, LANES - SSM_HEADS))), wso=b.reshape(SSM_D_INNER, D_MODEL))
        i = int(group[-1])
        return dict(wup=jnp.concatenate([a[p] for p in _PERM], axis=1), wdn=b.reshape(D_FF, D_MODEL), fcw=fcw[i])

    cos, sin_s = _rope_tables(positions)
    loss_row, grad_x, big, small = _local_step(x[0], cos, sin_s, loss_target[0], P, fetch, token)

    names = ["mix_w_in", "mix_w_out", "ssm_w_in", "ssm_w_out", "ffn_w_up0", "ffn_w_up1", "ffn_w_down0", "ffn_w_down1"]
    gs = [big[nm] for nm in names]
    cidx = lax.axis_index("c").astype(jnp.int32).reshape(1)
    got = _pair_exchange(gs, "pair_exchange")
    ps = [_pair_sum(g, o, cidx, f"pair_sum_{nm}") for g, o, nm in zip(gs, got, names)]
    parts = _chip_exchange(ps, "chip_exchange")
    kidx = kchip.astype(jnp.int32).reshape(1)
    fs = [_chip_sum(o, p, kidx, f"chip_sum_{nm}") for o, p, nm in zip(ps, parts, names)]
    others = _half_exchange(fs, "half_exchange")
    red = {nm: (f, o) for nm, f, o in zip(names, fs, others)}

    out = {}

    def big_update(pname, gparts):
        w = W[pname]
        lw = len(gparts)
        shp = w.shape
        r2, cc = gparts[0][0].shape
        w3, m3, v3 = (t.reshape(lw, 2 * r2, cc) for t in (w, Mo[pname], Vo[pname]))
        res = _adamw(w3, m3, v3, gparts, cidx, f"adamw_{pname}")
        out[pname] = tuple(r.reshape(shp) for r in res)

    big_update("mix_w_in", [red["mix_w_in"]])
    big_update("mix_w_out", [red["mix_w_out"]])
    big_update("ssm_w_in", [red["ssm_w_in"]])
    big_update("ssm_w_out", [red["ssm_w_out"]])
    big_update("ffn_w_up", [red["ffn_w_up0"], red["ffn_w_up1"]])
    big_update("ffn_w_down", [red["ffn_w_down0"], red["ffn_w_down1"]])

    small_names = ["norm_mix", "norm_ffn", "norm_final", "pool_w", "pool_scale", "attn_sinks", "ssm_dt_bias", "ssm_A_log",
                   "ssm_D", "ffn_conv_b", "ssm_conv_w", "ssm_conv_b", "ssm_norm", "ffn_conv_w"]

    def as2d(t):
        if t.ndim == 1:
            return t.reshape(1, -1)
        return t.reshape(-1, t.shape[-1])

    items = [(small[nm], as2d(W[nm]), as2d(Mo[nm]), as2d(Vo[nm])) for nm in small_names]
    res = _small_reduce_adamw(items, loss_row, "small_reduce_adamw")
    for a, nm in enumerate(small_names):
        out[nm] = tuple(r.reshape(W[nm].shape) for r in res[4 * a:4 * a + 4])
    loss = res[-1][0, 0]

    order = ["norm_mix", "norm_ffn", "norm_final", "mix_w_in", "pool_w", "pool_scale", "attn_sinks", "mix_w_out", "ssm_w_in",
             "ssm_conv_w", "ssm_conv_b", "ssm_dt_bias", "ssm_A_log", "ssm_D", "ssm_norm", "ssm_w_out", "ffn_w_up", "ffn_conv_w",
             "ffn_conv_b", "ffn_w_down"]
    return (loss, grad_x.reshape(x.shape), *[out[nm][0] for nm in order], *[out[nm][1] for nm in order],
            *[out[nm][2] for nm in order], *[out[nm][3] for nm in order])
```

```python
import functools

import jax
import jax.numpy as jnp
from jax import lax
from jax.experimental import pallas as pl
from jax.experimental.pallas import tpu as pltpu

F32 = jnp.float32
BF16 = jnp.bfloat16
MXU = BF16
HI = lax.Precision.HIGHEST

D_MODEL = 1024
POOL_WINDOWS = (2, 4, 8, 16)
POOL_DIM = 512
POOL_GROUP = 128
HEAD_DIM = 64
N_HEADS = 8
N_KV_HEADS = 2
GQ = 4
Q_DIM = 512
KV_DIM = 128
BLOCK = 128
ROPE_THETA = 10000.0
MIX_IN_DIM = 1280
SSM_D_INNER = 2048
SSM_HEADS = 32
SSM_GROUPS = 8
SSM_STATE = 128
SSM_CONV = 4
SSM_CHUNK = 128
SSM_CONV_DIM = 4096
SSM_IN_DIM = 6176
D_FF = 2816
FFN_CONV = 3
NORM_EPS = 1e-6
SSM_NORM_EPS = 1e-5
ADAM_LR = 0.001
ADAM_B1 = 0.9
ADAM_B2 = 0.999
ADAM_EPS = 1e-08
ADAM_WD = 0.01
ADAM_STEP = 10

N_CHIPS = 4
N_DEV = 8
LANES = 128
SUBLANES = 8
V7X_VMEM_LIMIT = 56 * 1024 * 1024
NEG = -1e30
MESH = pl.DeviceIdType.MESH


def _cp(*sem):
    return pltpu.CompilerParams(dimension_semantics=sem if sem else None, vmem_limit_bytes=V7X_VMEM_LIMIT)


def _sds(shape, dtype=F32):
    return jax.ShapeDtypeStruct(tuple(shape), dtype)


def _iota(shape, dim):
    return lax.broadcasted_iota(jnp.int32, shape, dim)


def _silu(x):
    return x * (1.0 / (1.0 + jnp.exp(-x)))


def _dsilu(x):
    s = 1.0 / (1.0 + jnp.exp(-x))
    return s * (1.0 + x * (1.0 - s))


def _mm(a, b, *, ta=False, tb=False, tm, tn, tk, res=None, out_dtype=F32, out_shard_perm=None, name):
    M, K = (a.shape[1], a.shape[0]) if ta else a.shape
    N = b.shape[0] if tb else b.shape[1]
    tm, tn, tk = min(tm, M), min(tn, N), min(tk, K)
    gm, gn, gk = M // tm, N // tn, K // tk
    assert gm * tm == M and gn * tn == N and gk * tk == K, (name, M, N, K, tm, tn, tk)
    a_spec = pl.BlockSpec((tk, tm), lambda i, j, k: (k, i)) if ta else pl.BlockSpec((tm, tk), lambda i, j, k: (i, k))
    b_spec = pl.BlockSpec((tn, tk), lambda i, j, k: (j, k)) if tb else pl.BlockSpec((tk, tn), lambda i, j, k: (k, j))
    dims = (((0 if ta else 1,), (1 if tb else 0,)), ((), ()))
    has_res = res is not None

    def body(*refs):
        a_ref, b_ref = refs[0], refs[1]
        r_ref = refs[2] if has_res else None
        o_ref = refs[3] if has_res else refs[2]
        p = lax.dot_general(a_ref[...].astype(MXU), b_ref[...].astype(MXU), dims, preferred_element_type=F32)
        if gk == 1:
            if has_res:
                p = p + r_ref[...]
            o_ref[...] = p.astype(out_dtype)
        else:
            acc = refs[-1]
            k = pl.program_id(2)

            @pl.when(k == 0)
            def _():
                acc[...] = p

            @pl.when(k > 0)
            def _():
                acc[...] += p

            @pl.when(k == gk - 1)
            def _():
                r = acc[...]
                if has_res:
                    r = r + r_ref[...]
                o_ref[...] = r.astype(out_dtype)

    in_specs = [a_spec, b_spec]
    args = [a, b]
    if has_res:
        in_specs.append(pl.BlockSpec((tm, tn), lambda i, j, k: (i, j)))
        args.append(res)
    if out_shard_perm is None:
        out_spec = pl.BlockSpec((tm, tn), lambda i, j, k: (i, j))
        out_shape = _sds((M, N), out_dtype)
    else:
        assert gn == len(out_shard_perm) == 4 and tuple(out_shard_perm) == (0, 2, 1, 3)
        out_spec = pl.BlockSpec((None, tm, tn), lambda i, j, k: ((j % 2) * 2 + j // 2, i, 0))
        out_shape = _sds((gn, M, tn), out_dtype)
    return pl.pallas_call(
        body, grid=(gm, gn, gk), in_specs=in_specs, out_specs=out_spec, out_shape=out_shape,
        scratch_shapes=[pltpu.VMEM((tm, tn), F32)] if gk > 1 else [],
        compiler_params=_cp("parallel", "parallel", "arbitrary"), name=name)(*args)


def _rmsnorm_fwd(x, w, name, token=None):
    T, D = x.shape
    tm = min(T, 512)
    has_token = token is not None

    def body(*refs):
        x_ref, w_ref, o_ref = refs[0], refs[1], refs[-1]
        xv = x_ref[...]
        if has_token:
            xv = xv + refs[2][0:1, 0:1]
        r = lax.rsqrt(jnp.mean(xv * xv, axis=-1, keepdims=True) + NORM_EPS)
        o_ref[...] = (xv * r * w_ref[...]).astype(o_ref.dtype)

    in_specs = [pl.BlockSpec((tm, D), lambda i: (i, 0)), pl.BlockSpec((1, D), lambda i: (0, 0))]
    args = [x, w.reshape(1, D)]
    if has_token:
        in_specs.append(pl.BlockSpec((SUBLANES, LANES), lambda i: (0, 0)))
        args.append(token)
    return pl.pallas_call(
        body, grid=(T // tm,), in_specs=in_specs,
        out_specs=pl.BlockSpec((tm, D), lambda i: (i, 0)), out_shape=_sds((T, D), MXU),
        compiler_params=_cp("parallel"), name=name)(*args)


def _rmsnorm_bwd(x, w, dh, dres, name, token=None):
    T, D = x.shape
    tm = min(T, 512)
    has_token = token is not None

    def body(*refs):
        x_ref, w_ref, dh_ref, dr_ref = refs[:4]
        dx_ref, dw_ref = refs[-2:]
        xv = x_ref[...]
        r = lax.rsqrt(jnp.mean(xv * xv, axis=-1, keepdims=True) + NORM_EPS)
        xh = xv * r
        dh = dh_ref[...]
        g = dh * w_ref[...]
        dr = dr_ref[...] + refs[4][0:1, 0:1] if has_token else dr_ref[...]
        dx_ref[...] = dr + r * (g - xh * jnp.mean(g * xh, axis=-1, keepdims=True))
        part = jnp.sum(dh * xh, axis=0, keepdims=True)

        @pl.when(pl.program_id(0) == 0)
        def _():
            dw_ref[...] = part

        @pl.when(pl.program_id(0) > 0)
        def _():
            dw_ref[...] += part

    row = pl.BlockSpec((tm, D), lambda i: (i, 0))
    vec = pl.BlockSpec((1, D), lambda i: (0, 0))
    in_specs = [row, vec, row, row]
    args = [x, w.reshape(1, D), dh, dres]
    if has_token:
        in_specs.append(pl.BlockSpec((SUBLANES, LANES), lambda i: (0, 0)))
        args.append(token)
    return pl.pallas_call(
        body, grid=(T // tm,), in_specs=in_specs, out_specs=[row, vec],
        out_shape=[_sds((T, D)), _sds((1, D))], compiler_params=_cp("arbitrary"), name=name)(*args)


def _loss_head(x, w, target, name):
    T, D = x.shape
    tm = min(T, 512)

    def body(x_ref, w_ref, t_ref, loss_ref, dx_ref, dw_ref):
        xv = x_ref[...]
        r = lax.rsqrt(jnp.mean(xv * xv, axis=-1, keepdims=True) + NORM_EPS)
        xh = xv * r
        wv = w_ref[...]
        e = xh * wv - t_ref[...]
        lpart = 0.5 * jnp.sum(jnp.mean(e * e, axis=-1, keepdims=True), axis=0, keepdims=True)
        dy = e * (1.0 / D)
        g = dy * wv
        dx_ref[...] = r * (g - xh * jnp.mean(g * xh, axis=-1, keepdims=True))
        part = jnp.sum(dy * xh, axis=0, keepdims=True)
        lrow = jnp.broadcast_to(lpart, (1, LANES))

        @pl.when(pl.program_id(0) == 0)
        def _():
            dw_ref[...] = part
            loss_ref[...] = lrow

        @pl.when(pl.program_id(0) > 0)
        def _():
            dw_ref[...] += part
            loss_ref[...] += lrow

    row = pl.BlockSpec((tm, D), lambda i: (i, 0))
    vec = pl.BlockSpec((1, D), lambda i: (0, 0))
    return pl.pallas_call(
        body, grid=(T // tm,), in_specs=[row, vec, row],
        out_specs=[pl.BlockSpec((1, LANES), lambda i: (0, 0)), row, vec],
        out_shape=[_sds((1, LANES)), _sds((T, D)), _sds((1, D))],
        compiler_params=_cp("arbitrary"), name=name)(x, w.reshape(1, D), target)


def _shift_down(cur, prev8, s):
    if s == 0:
        return cur
    tm = cur.shape[0]
    rc = pltpu.roll(cur, s, 0)
    top = jnp.where(_iota((SUBLANES, cur.shape[1]), 0) < s, pltpu.roll(prev8, s, 0), rc[:SUBLANES])
    return jnp.concatenate([top, rc[SUBLANES:]], axis=0) if tm > SUBLANES else top


def _shift_up(cur, next8, s):
    if s == 0:
        return cur
    tm = cur.shape[0]
    rc = pltpu.roll(cur, tm - s, 0)
    bot = jnp.where(_iota((SUBLANES, cur.shape[1]), 0) >= SUBLANES - s, pltpu.roll(next8, SUBLANES - s, 0), rc[tm - SUBLANES:])
    return jnp.concatenate([rc[:tm - SUBLANES], bot], axis=0) if tm > SUBLANES else bot


def _conv_rows(cur, prev8, w, b, K):
    acc = cur * w[K - 1:K, :] + b
    for s in range(1, K):
        acc = acc + _shift_down(cur, prev8, s) * w[K - 1 - s:K - s, :]
    return acc


def _halo_specs(tm, tc, col_of):
    q = tm // SUBLANES

    def prev_map(i, j):
        return (jnp.maximum(i * q - 1, 0), col_of(j))

    def make_next(n_row_tiles):
        def next_map(i, j):
            return (jnp.minimum((i + 1) * q, n_row_tiles * q - 1), col_of(j))
        return next_map

    return (lambda: pl.BlockSpec((SUBLANES, tc), prev_map)), (lambda n: pl.BlockSpec((SUBLANES, tc), make_next(n)))


FFN_TC = 1408


def _ffn_mid_fwd(hid, cw, cb, name):
    T = hid.shape[0]
    tm = min(T, 256)
    nt, nj = T // tm, D_FF // FFN_TC
    K = FFN_CONV

    def body(h_ref, hp_ref, w_ref, b_ref, o_ref):
        i = pl.program_id(0)
        cur = h_ref[...]
        prev8 = jnp.where(i > 0, hp_ref[...], 0.0)
        hc = _conv_rows(cur, prev8, w_ref[...], b_ref[...], K)
        o_ref[...] = (_silu(hc[:, FFN_TC:]) * hc[:, :FFN_TC]).astype(o_ref.dtype)

    mk_prev, _ = _halo_specs(tm, 2 * FFN_TC, lambda j: j)
    return pl.pallas_call(
        body, grid=(nt, nj),
        in_specs=[pl.BlockSpec((tm, 2 * FFN_TC), lambda i, j: (i, j)), mk_prev(),
                  pl.BlockSpec((K, 2 * FFN_TC), lambda i, j: (0, j)), pl.BlockSpec((1, 2 * FFN_TC), lambda i, j: (0, j))],
        out_specs=pl.BlockSpec((tm, FFN_TC), lambda i, j: (i, j)), out_shape=_sds((T, D_FF), MXU),
        compiler_params=_cp("parallel", "parallel"), name=name)(hid, hid, cw, cb)


def _ffn_mid_bwd(hid, cw, cb, da, name):
    T = hid.shape[0]
    tm = min(T, 256)
    nt, nj = T // tm, D_FF // FFN_TC
    K = FFN_CONV
    W2 = 2 * FFN_TC

    def body(h_ref, hp_ref, hn_ref, da_ref, dan_ref, w_ref, b_ref, dh_ref, dw_ref, db_ref):
        i = pl.program_id(1)
        w = w_ref[...]
        b = b_ref[...]
        cur = h_ref[...]
        prev8 = jnp.where(i > 0, hp_ref[...], 0.0)
        nxt8 = hn_ref[...]
        last = i == nt - 1

        def dpre(hc, dav):
            u, g = hc[:, :FFN_TC], hc[:, FFN_TC:]
            return jnp.concatenate([dav * _silu(g), dav * u * _dsilu(g)], axis=1)

        hc = _conv_rows(cur, prev8, w, b, K)
        d_cur = dpre(hc, da_ref[...])
        hc_n = _conv_rows(nxt8, cur[tm - SUBLANES:], w, b, K)
        d_nxt = jnp.where(last, 0.0, dpre(hc_n, dan_ref[...]))
        dh = d_cur * w[K - 1:K, :]
        for s in range(1, K):
            dh = dh + _shift_up(d_cur, d_nxt, s) * w[K - 1 - s:K - s, :]
        dh_ref[...] = dh.astype(dh_ref.dtype)
        rows = [jnp.sum(d_cur * _shift_down(cur, prev8, K - 1 - k), axis=0, keepdims=True) for k in range(K)]
        dwp = jnp.concatenate(rows, axis=0)
        dbp = jnp.sum(d_cur, axis=0, keepdims=True)

        @pl.when(i == 0)
        def _():
            dw_ref[...] = dwp
            db_ref[...] = dbp

        @pl.when(i > 0)
        def _():
            dw_ref[...] += dwp
            db_ref[...] += dbp

    q = tm // SUBLANES
    blk = pl.BlockSpec((tm, W2), lambda j, i: (i, j))
    prv = pl.BlockSpec((SUBLANES, W2), lambda j, i: (jnp.maximum(i * q - 1, 0), j))
    nxt = pl.BlockSpec((SUBLANES, W2), lambda j, i: (jnp.minimum((i + 1) * q, nt * q - 1), j))
    dab = pl.BlockSpec((tm, FFN_TC), lambda j, i: (i, j))
    dan = pl.BlockSpec((SUBLANES, FFN_TC), lambda j, i: (jnp.minimum((i + 1) * q, nt * q - 1), j))
    return pl.pallas_call(
        body, grid=(nj, nt),
        in_specs=[blk, prv, nxt, dab, dan, pl.BlockSpec((K, W2), lambda j, i: (0, j)), pl.BlockSpec((1, W2), lambda j, i: (0, j))],
        out_specs=[blk, pl.BlockSpec((K, W2), lambda j, i: (0, j)), pl.BlockSpec((1, W2), lambda j, i: (0, j))],
        out_shape=[_sds((T, 2 * D_FF), MXU), _sds((K, 2 * D_FF)), _sds((1, 2 * D_FF))],
        compiler_params=_cp("parallel", "arbitrary"), name=name)(hid, hid, hid, da, da, cw, cb)


def _rope(t, cos, sin_s, inverse=False):
    n = t.shape[1] // LANES
    c = jnp.concatenate([cos] * n, axis=1) if n > 1 else cos
    s = jnp.concatenate([sin_s] * n, axis=1) if n > 1 else sin_s
    a = pltpu.roll(t, HEAD_DIM // 2, 1)
    b = pltpu.roll(t, t.shape[1] - HEAD_DIM // 2, 1)
    first = (_iota(t.shape, 1) % HEAD_DIM) < HEAD_DIM // 2
    rot = jnp.where(first, b, a) * s
    return t * c - rot if inverse else t * c + rot


def _stack_heads(t, g):
    return jnp.concatenate([t[:, (GQ * g + r) * HEAD_DIM:(GQ * g + r + 1) * HEAD_DIM] for r in range(GQ)], axis=0)


def _stack_cols(t, g):
    return jnp.concatenate([t[:, GQ * g + r:GQ * g + r + 1] for r in range(GQ)], axis=0)


def _pool_sums(prev, cur, w):
    s = jnp.concatenate([prev, cur], axis=0)
    sh = 1
    while sh < w:
        s = s + pltpu.roll(s, sh, 0)
        sh *= 2
    return s[BLOCK:]


def _nt(a, b):
    return lax.dot_general(a.astype(MXU), b.astype(MXU), (((1,), (1,)), ((), ())), preferred_element_type=F32)


def _tn(a, b):
    return lax.dot_general(a.astype(MXU), b.astype(MXU), (((0,), (0,)), ((), ())), preferred_element_type=F32)


def _nn(a, b):
    return jnp.dot(a.astype(MXU), b.astype(MXU), preferred_element_type=F32)


def _mixcore_fwd(proj, cos, sin_s, pool_w, pool_scale, sinks, name):
    T = proj.shape[0]
    nb = T // BLOCK
    scale = HEAD_DIM ** -0.5

    def body(p_ref, pp_ref, c_ref, s_ref, cp_ref, sp_ref, pw_ref, ps_ref, sk_ref, cat_ref, at_ref, lse_ref):
        i = pl.program_id(0)
        has_prev = i > 0
        cur = p_ref[...]
        prv = jnp.where(has_prev, pp_ref[...], 0.0)
        tpos = (i * BLOCK + _iota((BLOCK, 1), 0) + 1).astype(F32)
        for g, w in enumerate(POOL_WINDOWS):
            sl = slice(g * POOL_GROUP, (g + 1) * POOL_GROUP)
            pooled = _pool_sums(prv[:, sl], cur[:, sl], w) / jnp.minimum(tpos, float(w)) - cur[:, sl]
            cat_ref[:, sl] = (_nn(pooled, pw_ref[g]) * ps_ref[:, sl]).astype(cat_ref.dtype)
        q = _rope(cur[:, POOL_DIM:POOL_DIM + Q_DIM], c_ref[...], s_ref[...])
        kc = _rope(cur[:, POOL_DIM + Q_DIM:POOL_DIM + Q_DIM + KV_DIM], c_ref[...], s_ref[...])
        kp = _rope(prv[:, POOL_DIM + Q_DIM:POOL_DIM + Q_DIM + KV_DIM], cp_ref[...], sp_ref[...])
        vc = cur[:, POOL_DIM + Q_DIM + KV_DIM:]
        vp = prv[:, POOL_DIM + Q_DIM + KV_DIM:]
        ri = _iota((GQ * BLOCK, BLOCK), 0) % BLOCK
        cj = _iota((GQ * BLOCK, BLOCK), 1)
        mc = cj <= ri
        mp = jnp.logical_and(cj > ri, has_prev)
        outs, lses = [], []
        for g in range(N_KV_HEADS):
            hs = slice(g * HEAD_DIM, (g + 1) * HEAD_DIM)
            qg = _stack_heads(q, g) * scale
            sc = jnp.where(mc, _nt(qg, kc[:, hs]), NEG)
            sp = jnp.where(mp, _nt(qg, kp[:, hs]), NEG)
            sink = jnp.concatenate([jnp.full((BLOCK, 1), sk_ref[GQ * g + r], F32) for r in range(GQ)], axis=0)
            m = jnp.maximum(jnp.maximum(jnp.max(sc, axis=1, keepdims=True), jnp.max(sp, axis=1, keepdims=True)), sink)
            pc = jnp.exp(sc - m)
            pp = jnp.exp(sp - m)
            den = jnp.sum(pc, axis=1, keepdims=True) + jnp.sum(pp, axis=1, keepdims=True) + jnp.exp(sink - m)
            o = (_nn(pc, vc[:, hs]) + _nn(pp, vp[:, hs])) / den
            lse = m + jnp.log(den)
            for r in range(GQ):
                outs.append(o[r * BLOCK:(r + 1) * BLOCK])
                lses.append(lse[r * BLOCK:(r + 1) * BLOCK])
        attn = jnp.concatenate(outs, axis=1)
        at_ref[...] = attn
        cat_ref[:, POOL_DIM:] = attn.astype(cat_ref.dtype)
        lane = _iota((BLOCK, LANES), 1)
        lrow = jnp.zeros((BLOCK, LANES), F32)
        for h in range(N_HEADS):
            lrow = jnp.where(lane == h, lses[h], lrow)
        lse_ref[...] = lrow

    cur = lambda w: pl.BlockSpec((BLOCK, w), lambda i: (i, 0))
    prv = lambda w: pl.BlockSpec((BLOCK, w), lambda i: (jnp.maximum(i - 1, 0), 0))
    return pl.pallas_call(
        body, grid=(nb,),
        in_specs=[cur(MIX_IN_DIM), prv(MIX_IN_DIM), cur(LANES), cur(LANES), prv(LANES), prv(LANES),
                  pl.BlockSpec((4, POOL_GROUP, POOL_GROUP), lambda i: (0, 0, 0)), pl.BlockSpec((1, POOL_DIM), lambda i: (0, 0)),
                  pl.BlockSpec(memory_space=pltpu.SMEM)],
        out_specs=[cur(2 * POOL_DIM), cur(Q_DIM), cur(LANES)],
        out_shape=[_sds((T, 2 * POOL_DIM), MXU), _sds((T, Q_DIM)), _sds((T, LANES))],
        compiler_params=_cp("parallel"), name=name)(proj, proj, cos, sin_s, cos, sin_s, pool_w, pool_scale, sinks)


def _mixcore_bwd(proj, cos, sin_s, pool_w, pool_scale, sinks, attn, lse, dcat, name):
    T = proj.shape[0]
    nb = T // BLOCK
    scale = HEAD_DIM ** -0.5
    QO, KO, VO = POOL_DIM, POOL_DIM + Q_DIM, POOL_DIM + Q_DIM + KV_DIM

    def body(p_ref, pp_ref, pn_ref, c_ref, s_ref, cp_ref, sp_ref, cn_ref, sn_ref, pw_ref, ps_ref, sk_ref,
             at_ref, atn_ref, l_ref, ln_ref, d_ref, dn_ref, dp_ref, dpw_ref, dps_ref, dsk_ref):
        i = pl.program_id(0)
        has_prev = i > 0
        has_next = i < nb - 1
        cur = p_ref[...]
        prv = jnp.where(has_prev, pp_ref[...], 0.0)
        d_cur = d_ref[...]
        d_nxt = jnp.where(has_next, dn_ref[...], 0.0)

        tpos = (i * BLOCK + _iota((BLOCK, 1), 0) + 1).astype(F32)
        tpos2 = (i * BLOCK + _iota((2 * BLOCK, 1), 0) + 1).astype(F32)
        ps = ps_ref[...]
        dps_parts, dpw_parts = [], []
        for g, w in enumerate(POOL_WINDOWS):
            sl = slice(g * POOL_GROUP, (g + 1) * POOL_GROUP)
            pooled = _pool_sums(prv[:, sl], cur[:, sl], w) / jnp.minimum(tpos, float(w)) - cur[:, sl]
            mixed = _nn(pooled, pw_ref[g])
            dps_parts.append(jnp.sum(d_cur[:, sl] * mixed, axis=0, keepdims=True))
            dm2 = jnp.concatenate([d_cur[:, sl], d_nxt[:, sl]], axis=0) * ps[:, sl]
            dpw_parts.append(_tn(pooled, dm2[:BLOCK]))
            dpool2 = _nt(dm2, pw_ref[g])
            e = dpool2 / jnp.minimum(tpos2, float(w))
            sh = 1
            while sh < w:
                e = e + pltpu.roll(e, 2 * BLOCK - sh, 0)
                sh *= 2
            dp_ref[:, sl] = (e[:BLOCK] - dpool2[:BLOCK]).astype(dp_ref.dtype)
        dpsp = jnp.concatenate(dps_parts, axis=1)

        nxt = pn_ref[...]
        q = _rope(cur[:, QO:KO], c_ref[...], s_ref[...])
        qn = _rope(nxt[:, QO:KO], cn_ref[...], sn_ref[...])
        kc = _rope(cur[:, KO:VO], c_ref[...], s_ref[...])
        kp = _rope(prv[:, KO:VO], cp_ref[...], sp_ref[...])
        vc, vp = cur[:, VO:], prv[:, VO:]
        do, don = d_cur[:, POOL_DIM:], d_nxt[:, POOL_DIM:]
        dl = do * at_ref[...]
        dln = don * atn_ref[...]
        lse, lsen = l_ref[...], ln_ref[...]
        ri = _iota((GQ * BLOCK, BLOCK), 0) % BLOCK
        cj = _iota((GQ * BLOCK, BLOCK), 1)
        mc = cj <= ri
        mp = jnp.logical_and(cj > ri, has_prev)
        mn = jnp.logical_and(cj > ri, has_next)
        dq_parts, dk_parts, dv_parts, dsk_vals = [], [], [], []
        for g in range(N_KV_HEADS):
            hs = slice(g * HEAD_DIM, (g + 1) * HEAD_DIM)
            qg, qng = _stack_heads(q, g) * scale, _stack_heads(qn, g) * scale
            dog, dong = _stack_heads(do, g), _stack_heads(don, g)
            delta = jnp.sum(_stack_heads(dl, g), axis=1, keepdims=True)
            deltan = jnp.sum(_stack_heads(dln, g), axis=1, keepdims=True)
            lg, lng = _stack_cols(lse, g), _stack_cols(lsen, g)
            pc = jnp.where(mc, jnp.exp(_nt(qg, kc[:, hs]) - lg), 0.0)
            pp = jnp.where(mp, jnp.exp(_nt(qg, kp[:, hs]) - lg), 0.0)
            pn = jnp.where(mn, jnp.exp(_nt(qng, kc[:, hs]) - lng), 0.0)
            dsc = pc * (_nt(dog, vc[:, hs]) - delta)
            dsp = pp * (_nt(dog, vp[:, hs]) - delta)
            dsn = pn * (_nt(dong, vc[:, hs]) - deltan)
            dqg = (_nn(dsc, kc[:, hs]) + _nn(dsp, kp[:, hs])) * scale
            dq_parts += [dqg[r * BLOCK:(r + 1) * BLOCK] for r in range(GQ)]
            dk_parts.append(_tn(dsc, qg) + _tn(dsn, qng))
            dv_parts.append(_tn(pc, dog) + _tn(pn, dong))
            sink = jnp.concatenate([jnp.full((BLOCK, 1), sk_ref[GQ * g + r], F32) for r in range(GQ)], axis=0)
            dsk = -jnp.exp(sink - lg) * delta
            dsk_vals += [jnp.sum(dsk[r * BLOCK:(r + 1) * BLOCK], axis=0, keepdims=True) for r in range(GQ)]
        dq = _rope(jnp.concatenate(dq_parts, axis=1), c_ref[...], s_ref[...], inverse=True)
        dk = _rope(jnp.concatenate(dk_parts, axis=1), c_ref[...], s_ref[...], inverse=True)
        dp_ref[:, QO:KO] = dq.astype(dp_ref.dtype)
        dp_ref[:, KO:VO] = dk.astype(dp_ref.dtype)
        dp_ref[:, VO:] = jnp.concatenate(dv_parts, axis=1).astype(dp_ref.dtype)
        lane = _iota((1, LANES), 1)
        dskp = jnp.zeros((1, LANES), F32)
        for h in range(N_HEADS):
            dskp = jnp.where(lane == h, dsk_vals[h], dskp)

        @pl.when(i == 0)
        def _():
            dps_ref[...] = dpsp
            dsk_ref[...] = dskp
            for g in range(4):
                dpw_ref[g] = dpw_parts[g]

        @pl.when(i > 0)
        def _():
            dps_ref[...] += dpsp
            dsk_ref[...] += dskp
            for g in range(4):
                dpw_ref[g] += dpw_parts[g]

    cur = lambda w: pl.BlockSpec((BLOCK, w), lambda i: (i, 0))
    prv = lambda w: pl.BlockSpec((BLOCK, w), lambda i: (jnp.maximum(i - 1, 0), 0))
    nxt = lambda w: pl.BlockSpec((BLOCK, w), lambda i: (jnp.minimum(i + 1, nb - 1), 0))
    return pl.pallas_call(
        body, grid=(nb,),
        in_specs=[cur(MIX_IN_DIM), prv(MIX_IN_DIM), nxt(MIX_IN_DIM),
                  cur(LANES), cur(LANES), prv(LANES), prv(LANES), nxt(LANES), nxt(LANES),
                  pl.BlockSpec((4, POOL_GROUP, POOL_GROUP), lambda i: (0, 0, 0)), pl.BlockSpec((1, POOL_DIM), lambda i: (0, 0)),
                  pl.BlockSpec(memory_space=pltpu.SMEM),
                  cur(Q_DIM), nxt(Q_DIM), cur(LANES), nxt(LANES), cur(2 * POOL_DIM), nxt(2 * POOL_DIM)],
        out_specs=[cur(MIX_IN_DIM), pl.BlockSpec((4, POOL_GROUP, POOL_GROUP), lambda i: (0, 0, 0)),
                   pl.BlockSpec((1, POOL_DIM), lambda i: (0, 0)), pl.BlockSpec((1, LANES), lambda i: (0, 0))],
        out_shape=[_sds((T, MIX_IN_DIM), MXU), _sds((4, POOL_GROUP, POOL_GROUP)), _sds((1, POOL_DIM)), _sds((1, LANES))],
        compiler_params=_cp("arbitrary"), name=name)(
            proj, proj, proj, cos, sin_s, cos, sin_s, cos, sin_s, pool_w, pool_scale, sinks, attn, attn, lse, lse, dcat, dcat)


SSM_TC = 128
GROUP_W = SSM_D_INNER // SSM_GROUPS
PERM_W = GROUP_W + 2 * SSM_STATE


def _perm_col(n):
    nx = SSM_D_INNER // SSM_TC
    nbt = SSM_GROUPS
    x_idx = (n // 2) * 4 + n % 2
    b_idx = (n - nx) * 4 + 2
    c_idx = (n - nx - nbt) * 4 + 3
    return jnp.where(n < nx, x_idx, jnp.where(n < nx + nbt, b_idx, c_idx))


def _ssm_pre_fwd(xbc, cw, cb, name):
    T = xbc.shape[0]
    tm = min(T, 1024)
    K = SSM_CONV
    q = tm // SUBLANES

    def body(x_ref, xp_ref, w_ref, b_ref, o_ref):
        prev8 = jnp.where(pl.program_id(0) > 0, xp_ref[...], 0.0)
        o_ref[...] = _silu(_conv_rows(x_ref[...], prev8, w_ref[...], b_ref[...], K))

    tc = 512
    return pl.pallas_call(
        body, grid=(T // tm, SSM_CONV_DIM // tc),
        in_specs=[pl.BlockSpec((tm, tc), lambda i, j: (i, j)),
                  pl.BlockSpec((SUBLANES, tc), lambda i, j: (jnp.maximum(i * q - 1, 0), j)),
                  pl.BlockSpec((K, tc), lambda i, j: (0, j)), pl.BlockSpec((1, tc), lambda i, j: (0, j))],
        out_specs=pl.BlockSpec((tm, tc), lambda i, j: (i, j)), out_shape=_sds((T, SSM_CONV_DIM)),
        compiler_params=_cp("parallel", "parallel"), name=name)(xbc, xbc, cw, cb)


def _ssm_pre_bwd(xbc, cw, cb, dact_perm, name):
    T = xbc.shape[0]
    tm = min(T, 1024)
    nt = T // tm
    K = SSM_CONV
    q = tm // SUBLANES
    tc = SSM_TC

    def body(x_ref, xp_ref, xn_ref, d_ref, dn_ref, w_ref, b_ref, dx_ref, dw_ref, db_ref):
        i = pl.program_id(1)
        w = w_ref[...]
        b = b_ref[...]
        cur = x_ref[...]
        prev8 = jnp.where(i > 0, xp_ref[...], 0.0)
        nxt8 = xn_ref[...]
        d_cur = d_ref[...] * _dsilu(_conv_rows(cur, prev8, w, b, K))
        d_nxt = jnp.where(i == nt - 1, 0.0, dn_ref[...] * _dsilu(_conv_rows(nxt8, cur[tm - SUBLANES:], w, b, K)))
        dx = d_cur * w[K - 1:K, :]
        for s in range(1, K):
            dx = dx + _shift_up(d_cur, d_nxt, s) * w[K - 1 - s:K - s, :]
        dx_ref[...] = dx.astype(dx_ref.dtype)
        dwp = jnp.concatenate([jnp.sum(d_cur * _shift_down(cur, prev8, K - 1 - k), axis=0, keepdims=True) for k in range(K)], axis=0)
        dbp = jnp.sum(d_cur, axis=0, keepdims=True)

        @pl.when(i == 0)
        def _():
            dw_ref[...] = dwp
            db_ref[...] = dbp

        @pl.when(i > 0)
        def _():
            dw_ref[...] += dwp
            db_ref[...] += dbp

    nxt_row = lambda i: jnp.minimum((i + 1) * q, nt * q - 1)
    return pl.pallas_call(
        body, grid=(SSM_CONV_DIM // tc, nt),
        in_specs=[pl.BlockSpec((tm, tc), lambda j, i: (i, j)),
                  pl.BlockSpec((SUBLANES, tc), lambda j, i: (jnp.maximum(i * q - 1, 0), j)),
                  pl.BlockSpec((SUBLANES, tc), lambda j, i: (nxt_row(i), j)),
                  pl.BlockSpec((tm, tc), lambda j, i: (i, _perm_col(j))),
                  pl.BlockSpec((SUBLANES, tc), lambda j, i: (nxt_row(i), _perm_col(j))),
                  pl.BlockSpec((K, tc), lambda j, i: (0, j)), pl.BlockSpec((1, tc), lambda j, i: (0, j))],
        out_specs=[pl.BlockSpec((tm, tc), lambda j, i: (i, j)), pl.BlockSpec((K, tc), lambda j, i: (0, j)),
                   pl.BlockSpec((1, tc), lambda j, i: (0, j))],
        out_shape=[_sds((T, SSM_CONV_DIM), MXU), _sds((K, SSM_CONV_DIM)), _sds((1, SSM_CONV_DIM))],
        compiler_params=_cp("parallel", "arbitrary"), name=name)(xbc, xbc, xbc, dact_perm, dact_perm, cw, cb)


def _dot_hi(a, b):
    return jnp.dot(a, b, precision=HI, preferred_element_type=F32)


def _ssd_common(dtraw, bias, alog):
    L = SSM_CHUNK
    xb = dtraw + bias
    dt = jnp.maximum(xb, 0.0) + jnp.log1p(jnp.exp(-jnp.abs(xb)))
    A = -jnp.exp(alog)
    tril = (_iota((L, L), 1) <= _iota((L, L), 0)).astype(F32)
    acs = _dot_hi(tril, dt * A)
    return xb, dt, A, tril, acs


def _head_selectors():
    es = (_iota((LANES, SSM_D_INNER), 0) == _iota((LANES, SSM_D_INNER), 1) // HEAD_DIM).astype(BF16)
    est = (_iota((SSM_D_INNER, LANES), 1) == _iota((SSM_D_INNER, LANES), 0) // HEAD_DIM).astype(BF16)
    return es, est


def _dot_sel(v, sel):
    hi = v.astype(BF16)
    r1 = v - hi.astype(F32)
    mid = r1.astype(BF16)
    lo = (r1 - mid.astype(F32)).astype(BF16)
    d = lambda a: jnp.dot(a, sel, preferred_element_type=F32)
    return (d(hi) + d(mid)) + d(lo)


def _expand_heads(v, es):
    return _dot_sel(v, es)


def _reduce_heads(q, est):
    return _dot_sel(q, est)


def _per_state_row(v, g):
    return jnp.concatenate([jnp.broadcast_to(v[:, GQ * g + r:GQ * g + r + 1], (HEAD_DIM, 1)) for r in range(GQ)], axis=0)


def _ssd_fwd(xact, dtraw, dt_bias, a_log, name):
    T = xact.shape[0]
    nc = T // SSM_CHUNK
    L = SSM_CHUNK
    BO, CO = SSM_D_INNER, SSM_D_INNER + SSM_GROUPS * SSM_STATE

    def body(x_ref, dt_ref, bias_ref, al_ref, es_ref, y_ref, st_ref, state):
        @pl.when(pl.program_id(0) == 0)
        def _():
            state[...] = jnp.zeros(state.shape, F32)

        _, dt, A, tril, acs = _ssd_common(dt_ref[...], bias_ref[...], al_ref[...])
        acsT = acs.T
        last = acs[L - 1:L, :]
        cd = jnp.exp(last)
        es = es_ref[...]
        dtX = _expand_heads(dt, es)
        EX = _expand_heads(jnp.exp(acs), es)
        decX = _expand_heads(jnp.exp(last - acs), es)
        for g in range(SSM_GROUPS):
            gs = slice(g * GROUP_W, (g + 1) * GROUP_W)
            B = x_ref[:, BO + g * SSM_STATE:BO + (g + 1) * SSM_STATE]
            C = x_ref[:, CO + g * SSM_STATE:CO + (g + 1) * SSM_STATE]
            X = x_ref[:, gs] * dtX[:, gs]
            CB = _nt(C, B)
            yd = []
            for r in range(GQ):
                h = GQ * g + r
                Lm = jnp.exp(jnp.where(tril > 0, acs[:, h:h + 1] - acsT[h:h + 1, :], NEG))
                yd.append(_nn(CB * Lm, X[:, r * HEAD_DIM:(r + 1) * HEAD_DIM]))
            S = state[g]
            st_ref[g] = S
            y_ref[:, gs] = jnp.concatenate(yd, axis=1) + _nt(C, S) * EX[:, gs]
            state[g] = S * _per_state_row(cd, g) + _tn(X * decX[:, gs], B)

    es, _ = _head_selectors()
    return pl.pallas_call(
        body, grid=(nc,),
        in_specs=[pl.BlockSpec((L, SSM_CONV_DIM), lambda c: (c, 0)), pl.BlockSpec((L, LANES), lambda c: (c, 0)),
                  pl.BlockSpec((1, LANES), lambda c: (0, 0)), pl.BlockSpec((1, LANES), lambda c: (0, 0)),
                  pl.BlockSpec((LANES, SSM_D_INNER), lambda c: (0, 0))],
        out_specs=[pl.BlockSpec((L, SSM_D_INNER), lambda c: (c, 0)),
                   pl.BlockSpec((None, SSM_GROUPS, GROUP_W, SSM_STATE), lambda c: (c, 0, 0, 0))],
        out_shape=[_sds((T, SSM_D_INNER)), _sds((nc, SSM_GROUPS, GROUP_W, SSM_STATE))],
        scratch_shapes=[pltpu.VMEM((SSM_GROUPS, GROUP_W, SSM_STATE), F32)],
        compiler_params=_cp("arbitrary"), name=name)(xact, dtraw, dt_bias, a_log, es)


def _ssd_bwd(xact, dtraw, dt_bias, a_log, d_skip, states, dy, name):
    T = xact.shape[0]
    nc = T // SSM_CHUNK
    L = SSM_CHUNK
    BO, CO = SSM_D_INNER, SSM_D_INNER + SSM_GROUPS * SSM_STATE

    def body(x_ref, dt_ref, bias_ref, al_ref, dsk_ref, es_ref, est_ref, st_ref, dy_ref,
             dxp_ref, ddt_ref, dbias_ref, dal_ref, dd_ref, dstate, qa, qx):
        cc = pl.program_id(0)

        @pl.when(cc == 0)
        def _():
            dstate[...] = jnp.zeros(dstate.shape, F32)

        xb, dt, A, tril, acs = _ssd_common(dt_ref[...], bias_ref[...], al_ref[...])
        acsT = acs.T
        last = acs[L - 1:L, :]
        cd = jnp.exp(last)
        es, est = es_ref[...], est_ref[...]
        dtX = _expand_heads(dt, es)
        EX = _expand_heads(jnp.exp(acs), es)
        decX = _expand_heads(jnp.exp(last - acs), es)
        lane1 = _iota((1, LANES), 1)
        lane = _iota((L, LANES), 1)
        sub = _iota((L, LANES), 0)
        ztot = jnp.zeros((1, LANES), F32)
        wrow = jnp.zeros((L, LANES), F32)
        wcolT = jnp.zeros((LANES, L), F32)
        rows_dec, rows_dd = [], []
        for g in range(SSM_GROUPS):
            gs = slice(g * GROUP_W, (g + 1) * GROUP_W)
            x = x_ref[:, gs]
            B = x_ref[:, BO + g * SSM_STATE:BO + (g + 1) * SSM_STATE]
            C = x_ref[:, CO + g * SSM_STATE:CO + (g + 1) * SSM_STATE]
            dY = dy_ref[:, gs]
            dtx, e_x, dec_x = dtX[:, gs], EX[:, gs], decX[:, gs]
            X = x * dtx
            CB = _nt(C, B)
            S = st_ref[g]
            dS_out = dstate[g]
            dcb_sum = jnp.zeros((L, L), F32)
            dxd = []
            for r in range(GQ):
                h = GQ * g + r
                hs = slice(r * HEAD_DIM, (r + 1) * HEAD_DIM)
                Lm = jnp.exp(jnp.where(tril > 0, acs[:, h:h + 1] - acsT[h:h + 1, :], NEG))
                M = CB * Lm
                dM = _nt(dY[:, hs], X[:, hs])
                dxd.append(_tn(M, dY[:, hs]))
                dcb_sum = dcb_sum + dM * Lm
                Wm = dM * M
                wrow = jnp.where(lane == h, jnp.sum(Wm, axis=1, keepdims=True), wrow)
                wcolT = jnp.where(sub == h, jnp.sum(Wm, axis=0, keepdims=True), wcolT)
            dXd = jnp.concatenate(dxd, axis=1)
            G = _nt(C, S)
            dG = dY * e_x
            dDX = _nt(B, dS_out)
            dX = dXd + dec_x * dDX
            t_dec = dDX * X * dec_x
            qa[:, gs] = dG * G - t_dec
            qx[:, gs] = dX * x
            rows_dec.append(jnp.sum(t_dec, axis=0, keepdims=True))
            rows_dd.append(jnp.sum(dY * x, axis=0, keepdims=True))
            zc = jnp.sum(dS_out * S, axis=1, keepdims=True)
            for r in range(GQ):
                ztot = jnp.where(lane1 == GQ * g + r, jnp.sum(zc[r * HEAD_DIM:(r + 1) * HEAD_DIM], axis=0, keepdims=True), ztot)
            dxp_ref[:, g * PERM_W:g * PERM_W + GROUP_W] = dX * dtx + dY * dsk_ref[:, gs]
            dxp_ref[:, g * PERM_W + GROUP_W:g * PERM_W + GROUP_W + SSM_STATE] = _tn(dcb_sum, C) + _nn(X * dec_x, dS_out)
            dxp_ref[:, g * PERM_W + GROUP_W + SSM_STATE:(g + 1) * PERM_W] = _nn(dcb_sum, B) + _nn(dG, S)
            dstate[g] = dS_out * _per_state_row(cd, g) + _tn(dG, C)
        rows = jnp.concatenate([jnp.concatenate(rows_dec, axis=1), jnp.concatenate(rows_dd, axis=1)]
                               + [jnp.zeros((SUBLANES - 2, SSM_D_INNER), F32)], axis=0)
        rsum = _reduce_heads(rows, est)
        dlast = rsum[0:1, :] + cd * ztot
        dacs = (wrow - wcolT.T) + _reduce_heads(qa[...], est) + jnp.where(sub == L - 1, dlast, 0.0)
        triu = (_iota((L, L), 0) <= _iota((L, L), 1)).astype(F32)
        da = _dot_hi(triu, dacs)
        ddtraw = (da * A + _reduce_heads(qx[...], est)) * (1.0 / (1.0 + jnp.exp(-xb)))
        ddt_ref[...] = ddtraw
        dal = jnp.sum(da * dt, axis=0, keepdims=True) * A
        ddp = rsum[1:2, :]
        dbp = jnp.sum(ddtraw, axis=0, keepdims=True)

        @pl.when(cc == 0)
        def _():
            dbias_ref[...] = dbp
            dal_ref[...] = dal
            dd_ref[...] = ddp

        @pl.when(cc > 0)
        def _():
            dbias_ref[...] += dbp
            dal_ref[...] += dal
            dd_ref[...] += ddp

    rc = lambda c: nc - 1 - c
    vec = pl.BlockSpec((1, LANES), lambda c: (0, 0))
    es, est = _head_selectors()
    return pl.pallas_call(
        body, grid=(nc,),
        in_specs=[pl.BlockSpec((L, SSM_CONV_DIM), lambda c: (rc(c), 0)), pl.BlockSpec((L, LANES), lambda c: (rc(c), 0)), vec, vec,
                  pl.BlockSpec((1, SSM_D_INNER), lambda c: (0, 0)),
                  pl.BlockSpec((LANES, SSM_D_INNER), lambda c: (0, 0)), pl.BlockSpec((SSM_D_INNER, LANES), lambda c: (0, 0)),
                  pl.BlockSpec((None, SSM_GROUPS, GROUP_W, SSM_STATE), lambda c: (rc(c), 0, 0, 0)),
                  pl.BlockSpec((L, SSM_D_INNER), lambda c: (rc(c), 0))],
        out_specs=[pl.BlockSpec((L, SSM_GROUPS * PERM_W), lambda c: (rc(c), 0)),
                   pl.BlockSpec((L, LANES), lambda c: (rc(c), 0)), vec, vec, vec],
        out_shape=[_sds((T, SSM_GROUPS * PERM_W)), _sds((T, LANES)), _sds((1, LANES)), _sds((1, LANES)), _sds((1, LANES))],
        scratch_shapes=[pltpu.VMEM((SSM_GROUPS, GROUP_W, SSM_STATE), F32), pltpu.VMEM((L, SSM_D_INNER), F32),
                        pltpu.VMEM((L, SSM_D_INNER), F32)],
        compiler_params=_cp("arbitrary"), name=name)(xact, dtraw, dt_bias, a_log, d_skip, es, est, states, dy)


def _ssm_post_fwd(y, xact, z, d_skip, nw, name):
    T = y.shape[0]
    tm = min(T, 256)
    W = SSM_D_INNER

    def body(y_ref, x_ref, z_ref, d_ref, w_ref, o_ref):
        y2 = (y_ref[...] + d_ref[...] * x_ref[...]) * _silu(z_ref[...])
        r = lax.rsqrt(jnp.mean(y2 * y2, axis=-1, keepdims=True) + SSM_NORM_EPS)
        o_ref[...] = (y2 * r * w_ref[...]).astype(o_ref.dtype)

    row = pl.BlockSpec((tm, W), lambda i: (i, 0))
    vec = pl.BlockSpec((1, W), lambda i: (0, 0))
    return pl.pallas_call(
        body, grid=(T // tm,), in_specs=[row, row, row, vec, vec], out_specs=row, out_shape=_sds((T, W), MXU),
        compiler_params=_cp("parallel"), name=name)(y, xact, z, d_skip, nw)


def _ssm_post_bwd(y, xact, z, d_skip, nw, dyn, name):
    T = y.shape[0]
    tm = min(T, 256)
    W = SSM_D_INNER

    def body(y_ref, x_ref, z_ref, d_ref, w_ref, dn_ref, dyg_ref, dz_ref, dw_ref):
        zv = z_ref[...]
        sz = _silu(zv)
        yg = y_ref[...] + d_ref[...] * x_ref[...]
        y2 = yg * sz
        r = lax.rsqrt(jnp.mean(y2 * y2, axis=-1, keepdims=True) + SSM_NORM_EPS)
        y2h = y2 * r
        dn = dn_ref[...]
        gy = dn * w_ref[...]
        dy2 = r * (gy - y2h * jnp.mean(gy * y2h, axis=-1, keepdims=True))
        dyg_ref[...] = dy2 * sz
        dz_ref[...] = (dy2 * yg * _dsilu(zv)).astype(dz_ref.dtype)
        part = jnp.sum(dn * y2h, axis=0, keepdims=True)

        @pl.when(pl.program_id(0) == 0)
        def _():
            dw_ref[...] = part

        @pl.when(pl.program_id(0) > 0)
        def _():
            dw_ref[...] += part

    row = pl.BlockSpec((tm, W), lambda i: (i, 0))
    vec = pl.BlockSpec((1, W), lambda i: (0, 0))
    return pl.pallas_call(
        body, grid=(T // tm,), in_specs=[row, row, row, vec, vec, row], out_specs=[row, row, vec],
        out_shape=[_sds((T, W)), _sds((T, W), MXU), _sds((1, W))],
        compiler_params=_cp("arbitrary"), name=name)(y, xact, z, d_skip, nw, dyn)


def _local_step(x0, cos, sin_s, target, P, fetch, token, send):
    mmf = functools.partial(_mm, tm=1024)
    big, small = {}, {}
    P = dict(P, wup={}, wdn={}, fcw={})
    h0 = _rmsnorm_fwd(x0, P["nm"][0], "norm_mix0", token=token)
    proj0 = mmf(h0, P["wmi"], tn=1280, tk=1024, name="mix_in")
    cat, attn, lse = _mixcore_fwd(proj0, cos, sin_s, P["pool_w"], P["pool_scale"], P["sinks"], "mixcore_fwd")
    x1 = mmf(cat, P["wmo"], tn=1024, tk=1024, res=x0, name="mix_out")

    def ffn_fwd(xin, i):
        hf = _rmsnorm_fwd(xin, P["nf"][i], f"norm_ffn{i}")
        got = fetch(f"ffn{i}", hf)
        P["wup"][i], P["wdn"][i], P["fcw"][i] = got["wup"], got["wdn"], got["fcw"]
        hid = mmf(hf, P["wup"][i], tn=1408, tk=1024, name=f"ffn_up{i}")
        act = _ffn_mid_fwd(hid, P["fcw"][i], P["fcb"][i], f"ffn_mid_fwd{i}")
        xout = mmf(act, P["wdn"][i], tn=1024, tk=D_FF, res=xin, name=f"ffn_down{i}")
        return hf, hid, act, xout

    hf0, hid0, act0, x2 = ffn_fwd(x1, 0)
    h1 = _rmsnorm_fwd(x2, P["nm"][1], "norm_mix1")
    P.update(fetch("ssm", h1))
    z = mmf(h1, P["wz"], tn=1024, tk=1024, name="ssm_in_z")
    xbc = mmf(h1, P["wxbc"], tn=1024, tk=1024, name="ssm_in_xbc")
    dtraw = mmf(h1, P["wdt"], tn=128, tk=1024, name="ssm_in_dt")
    xact = _ssm_pre_fwd(xbc, P["scw"], P["scb"], "ssm_pre_fwd")
    y, states = _ssd_fwd(xact, dtraw, P["dt_bias"], P["a_log"], "ssd_fwd")
    yn = _ssm_post_fwd(y, xact, z, P["d_exp"], P["snorm"], "ssm_post_fwd")
    x3 = mmf(yn, P["wso"], tn=1024, tk=SSM_D_INNER, res=x2, name="ssm_out")
    hf1, hid1, act1, x4 = ffn_fwd(x3, 1)
    loss_row, dx4, d_nfin = _loss_head(x4, P["nfin"], target, "loss_head")
    small["norm_final"] = d_nfin

    def ffn_bwd(xin, dxo, hf, hid, act, i):
        da = mmf(dxo, P["wdn"][i], tb=True, tn=1408, tk=1024, name=f"ffn_down_dx{i}")
        big[f"ffn_w_down{i}"] = dwf(act, dxo, tm=1408, tn=1024, name=f"ffn_down_dw{i}").reshape(N_CHIPS, D_FF // N_CHIPS, D_MODEL)
        dhid, dcw, dcb = _ffn_mid_bwd(hid, P["fcw"][i], P["fcb"][i], da, f"ffn_mid_bwd{i}")
        dhf = mmf(dhid, P["wup"][i], tb=True, tn=1024, tk=1408, name=f"ffn_up_dx{i}")
        big[f"ffn_w_up{i}"] = dwf(hf, dhid, tm=1024, tn=1408, out_shard_perm=(0, 2, 1, 3), name=f"ffn_up_dw{i}")
        tok = send(f"ffn{i}", [big[f"ffn_w_up{i}"], big[f"ffn_w_down{i}"]])
        dxi, dnf = _rmsnorm_bwd(xin, P["nf"][i], dhf, dxo, f"norm_ffn_bwd{i}", token=tok)
        return dxi, dnf, dcw, dcb

    dwf = functools.partial(_mm, ta=True, tk=1024, out_dtype=BF16)
    dx3, dnf1, dfcw1, dfcb1 = ffn_bwd(x3, dx4, hf1, hid1, act1, 1)
    dyn = mmf(dx3, P["wso"], tb=True, tn=1024, tk=1024, name="ssm_out_dx")
    big["ssm_w_out"] = dwf(yn, dx3, tm=1024, tn=1024, name="ssm_out_dw").reshape(N_CHIPS, SSM_D_INNER // N_CHIPS, D_MODEL)
    dyg, dz, d_snorm = _ssm_post_bwd(y, xact, z, P["d_exp"], P["snorm"], dyn, "ssm_post_bwd")
    dxact_p, ddtraw, d_dtb, d_alog, d_dskip = _ssd_bwd(xact, dtraw, P["dt_bias"], P["a_log"], P["d_exp"], states, dyg, "ssd_bwd")
    dxbc, d_scw, d_scb = _ssm_pre_bwd(xbc, P["scw"], P["scb"], dxact_p, "ssm_pre_bwd")
    dh1 = mmf(dz, P["wz"], tb=True, tn=1024, tk=1024, name="ssm_in_dx_z")
    dh1 = mmf(dxbc, P["wxbc"], tb=True, tn=1024, tk=1024, res=dh1, name="ssm_in_dx_xbc")
    dh1 = mmf(ddtraw, P["wdt"], tb=True, tn=1024, tk=128, res=dh1, name="ssm_in_dx_dt")
    dwz = dwf(h1, dz, tm=1024, tn=1024, name="ssm_in_dw_z")
    dwxbc = dwf(h1, dxbc, tm=1024, tn=1024, name="ssm_in_dw_xbc")
    dwdt = dwf(h1, ddtraw, tm=1024, tn=128, name="ssm_in_dw_dt")
    dwsi = jnp.concatenate([dwz, dwxbc, dwdt[:, :SSM_HEADS]], axis=1)
    big["ssm_w_in"] = dwsi.reshape(D_MODEL, N_CHIPS, SSM_IN_DIM // N_CHIPS).transpose(1, 0, 2)
    tok = send("ssm", [big["ssm_w_in"], big["ssm_w_out"]])
    dx2, dnm1 = _rmsnorm_bwd(x2, P["nm"][1], dh1, dx3, "norm_mix_bwd1", token=tok)
    dx1, dnf0, dfcw0, dfcb0 = ffn_bwd(x1, dx2, hf0, hid0, act0, 0)
    dcat = mmf(dx1, P["wmo"], tb=True, tn=1024, tk=1024, name="mix_out_dx")
    big["mix_w_out"] = dwf(cat, dx1, tm=1024, tn=1024, name="mix_out_dw").reshape(N_CHIPS, D_MODEL // N_CHIPS, D_MODEL)
    dproj0, d_pw, d_ps, d_sk = _mixcore_bwd(proj0, cos, sin_s, P["pool_w"], P["pool_scale"], P["sinks"], attn, lse, dcat, "mixcore_bwd")
    dh0 = mmf(dproj0, P["wmi"], tb=True, tn=1024, tk=1280, name="mix_in_dx")
    dwmi = dwf(h0, dproj0, tm=1024, tn=1280, name="mix_in_dw")
    big["mix_w_in"] = dwmi.reshape(D_MODEL, N_CHIPS, MIX_IN_DIM // N_CHIPS).transpose(1, 0, 2)
    tok = send("mix", [big["mix_w_in"], big["mix_w_out"]])
    dx0, dnm0 = _rmsnorm_bwd(x0, P["nm"][0], dh0, dx1, "norm_mix_bwd0", token=tok)

    def unperm_cols(a):
        r = a.shape[0]
        t = a.reshape(r, N_CHIPS, FFN_TC)
        return jnp.stack([t[:, p] for p in _PERM], axis=0)

    small["norm_mix"] = jnp.concatenate([dnm0, dnm1], axis=0)
    small["norm_ffn"] = jnp.concatenate([dnf0, dnf1], axis=0)
    small["pool_w"] = d_pw.reshape(4 * POOL_GROUP, POOL_GROUP)
    small["pool_scale"] = d_ps
    small["attn_sinks"] = d_sk
    small["ssm_dt_bias"] = d_dtb
    small["ssm_A_log"] = d_alog
    small["ssm_D"] = d_dskip
    fcb = jnp.stack([unperm_cols(dfcb0), unperm_cols(dfcb1)], axis=0)
    small["ffn_conv_b"] = fcb.reshape(2, 2 * D_FF)
    small["ssm_conv_w"] = d_scw.reshape(SSM_CONV, N_CHIPS, SSM_CONV_DIM // N_CHIPS).transpose(1, 0, 2)
    small["ssm_conv_b"] = d_scb.reshape(N_CHIPS, 1, SSM_CONV_DIM // N_CHIPS)
    small["ssm_norm"] = d_snorm.reshape(N_CHIPS, 1, SSM_D_INNER // N_CHIPS)
    small["ffn_conv_w"] = jnp.concatenate([unperm_cols(dfcw0), unperm_cols(dfcw1)], axis=1)
    return loss_row, dx0, big, small


ANY = pl.BlockSpec(memory_space=pl.ANY)


def _place():
    return lax.axis_index("x"), lax.axis_index("y"), lax.axis_index("c")


def _gather_shards(shards, name):
    n = len(shards)
    split = [s.size >= (1 << 16) for s in shards]

    def half(ref, a, h):
        shp = shards[a].shape
        if len(shp) == 3:
            return ref.at[h]
        r2 = shp[0] // 2
        return ref.at[pl.ds(pl.multiple_of(h * r2, 2 * SUBLANES), r2), :]

    def body(*refs):
        ins, outs = refs[:n], refs[n:2 * n]
        send, recv, fsend, frecv = refs[2 * n:]
        x, y, c = _place()
        k = 2 * x + y
        chips = [(1 - x, y), (x, 1 - y), (1 - x, 1 - y)]

        def ici(a, j, src_slot_ref, dst_slot):
            px, py = chips[j]
            src = half(src_slot_ref, a, c) if split[a] else src_slot_ref
            dst = half(outs[a].at[dst_slot], a, c) if split[a] else outs[a].at[dst_slot]
            return pltpu.make_async_remote_copy(src, dst, send.at[a, j], recv.at[a, j], device_id=(px, py, c), device_id_type=MESH)

        def d2d(a, j, h):
            px, py = chips[j]
            part = half(outs[a].at[2 * px + py], a, h)
            return pltpu.make_async_remote_copy(part, part, fsend.at[a, j], frecv.at[a, j], device_id=(x, y, 1 - c), device_id_type=MESH)

        sends = [ici(a, j, ins[a], k) for a in range(n) for j in range(3)]
        for cp in sends:
            cp.start()
        passed = []
        for a in range(n):
            for j, (px, py) in enumerate(chips):
                ici(a, j, ins[a], 2 * px + py).wait_recv()
                if split[a]:
                    passed.append(d2d(a, j, c))
                    passed[-1].start()
        for a in range(n):
            if split[a]:
                for j in range(3):
                    d2d(a, j, 1 - c).wait_recv()
        for cp in sends + passed:
            cp.wait_send()

    return pl.pallas_call(
        body, in_specs=[ANY] * n, out_specs=[ANY] * n,
        out_shape=[_sds((N_CHIPS,) + s.shape, s.dtype) for s in shards],
        scratch_shapes=[pltpu.SemaphoreType.DMA((n, 3))] * 4,
        compiler_params=pltpu.CompilerParams(has_side_effects=True), name=name)(*shards)


HBM = pl.BlockSpec(memory_space=pltpu.HBM)
SEM = pl.BlockSpec(memory_space=pltpu.SEMAPHORE)
DATAFLOW = pltpu.SideEffectType.DATAFLOW_SIDE_EFFECTING


def _spread_start(groups, slot_src, after, name):
    flat = [a for grp in groups for a in grp]
    n = len(flat)
    ng = len(groups)
    offs = [sum(len(g) for g in groups[:i]) for i in range(ng)]
    lshape = [(a.shape if slot_src else (N_CHIPS,) + a.shape) for a in flat]

    nsem = 6 * n

    def body(*refs):
        src, land = refs[:n], refs[n:2 * n]
        sems = refs[2 * n + 1:2 * n + 1 + nsem]
        token = refs[-1]
        x, y, c = _place()
        k = 2 * x + y
        chips = [(1 - x, y), (x, 1 - y), (1 - x, 1 - y)]
        for a in range(n):
            for j, (px, py) in enumerate(chips):
                s = src[a].at[2 * px + py] if slot_src else src[a]
                pltpu.make_async_remote_copy(s, land[a].at[k], sems[6 * a + 2 * j], sems[6 * a + 2 * j + 1],
                                             device_id=(px, py, c), device_id_type=MESH).start()
        token[...] = jnp.zeros(token.shape, token.dtype)

    out_shape = [pltpu.SemaphoreType.DMA(())] * nsem
    out_shape += [pltpu.HBM(a.shape, a.dtype) for a in flat] + [pltpu.HBM(s, a.dtype) for s, a in zip(lshape, flat)]
    out_shape.append(_sds((SUBLANES, LANES)))
    args = [pltpu.with_memory_space_constraint(a, pltpu.HBM) for a in flat]
    args += [pltpu.with_memory_space_constraint(lax.empty(s, a.dtype), pltpu.HBM) for s, a in zip(lshape, flat)]
    res = pl.pallas_call(
        body, name=name, out_shape=tuple(out_shape), in_specs=[HBM] * (2 * n) + [pl.BlockSpec(memory_space=pl.ANY)],
        out_specs=tuple([SEM] * nsem + [HBM] * (2 * n) + [pl.BlockSpec(memory_space=pltpu.VMEM)]),
        input_output_aliases={i: nsem + i for i in range(2 * n)},
        compiler_params=pltpu.CompilerParams(has_side_effects=DATAFLOW))(*args, after)
    sems, thru, token = res[:nsem], res[nsem:nsem + 2 * n], res[-1]
    out = []
    for gi, grp in enumerate(groups):
        sl = slice(offs[gi], offs[gi] + len(grp))
        out.append((list(sems[6 * offs[gi]:6 * (offs[gi] + len(grp))]), list(thru[:n][sl]), list(thru[n:][sl])))
    return out, token


def _spread_wait(started, slot_src, after, name):
    sems, srcs, lands = started
    n = len(srcs)

    def body(*refs):
        src, land = refs[:n], refs[n:2 * n]
        sem = refs[2 * n:2 * n + 6 * n]
        x, y, c = _place()
        chips = [(1 - x, y), (x, 1 - y), (1 - x, 1 - y)]
        for a in range(n):
            for j, (px, py) in enumerate(chips):
                s = src[a].at[2 * px + py] if slot_src else src[a]
                cp = pltpu.make_async_remote_copy(s, land[a].at[2 * px + py], sem[6 * a + 2 * j], sem[6 * a + 2 * j + 1],
                                                  device_id=(px, py, c), device_id_type=MESH)
                cp.wait_send()
                cp.wait_recv()

    res = pl.pallas_call(
        body, name=name, out_shape=tuple([pltpu.HBM(a.shape, a.dtype) for a in srcs] + [pltpu.HBM(a.shape, a.dtype) for a in lands]),
        in_specs=[HBM] * (2 * n) + [SEM] * (6 * n) + [pl.BlockSpec(memory_space=pl.ANY)], out_specs=tuple([HBM] * (2 * n)),
        input_output_aliases={i: i for i in range(2 * n)},
        compiler_params=pltpu.CompilerParams(has_side_effects=DATAFLOW))(*srcs, *lands, *sems, after)
    return list(res[:n]), list(res[n:])


def _sibling_exchange(fs, name):
    n = len(fs)

    def body(*refs):
        ins, outs = refs[:n], refs[n:2 * n]
        send, recv = refs[2 * n:]
        x, y, c = _place()
        cps = [pltpu.make_async_remote_copy(ins[a], outs[a], send.at[a], recv.at[a],
                                            device_id=(x, y, 1 - c), device_id_type=MESH) for a in range(n)]
        for cp in cps:
            cp.start()
        for cp in cps:
            cp.wait()

    return pl.pallas_call(
        body, in_specs=[ANY] * n, out_specs=[ANY] * n, out_shape=[_sds(f.shape, f.dtype) for f in fs],
        scratch_shapes=[pltpu.SemaphoreType.DMA((n,)), pltpu.SemaphoreType.DMA((n,))],
        compiler_params=pltpu.CompilerParams(has_side_effects=True), name=name)(*fs)


def _row_tile(rows, cols, budget=2 * 1024 * 1024, step=2 * SUBLANES):
    best = step
    for t in range(step, rows + 1, step):
        if rows % t == 0 and t * cols * 4 <= budget:
            best = t
    assert rows % best == 0, (rows, best)
    return best


def _chip_sum(own, parts, kidx, name):
    _, r2, C = parts.shape
    tr = _row_tile(r2, C, budget=1024 * 1024)

    def body(k_ref, o_ref_in, p1_ref, p2_ref, p3_ref, o_ref):
        o_ref[...] = ((o_ref_in[...].astype(F32) + p1_ref[...].astype(F32)) + p2_ref[...].astype(F32)) + p3_ref[...].astype(F32)

    def slot(d):
        return pl.BlockSpec((None, tr, C), lambda i, k: ((k[0] + d) % N_CHIPS, i, 0))

    return pl.pallas_call(
        body,
        grid_spec=pltpu.PrefetchScalarGridSpec(
            num_scalar_prefetch=1, grid=(r2 // tr,), in_specs=[slot(0), slot(1), slot(2), slot(3)],
            out_specs=pl.BlockSpec((tr, C), lambda i, k: (i, 0))),
        out_shape=_sds((r2, C)), compiler_params=_cp("parallel"), name=name)(kidx, own, parts, parts, parts)


def _adamw_math(w, g, m, v):
    m2 = ADAM_B1 * m + (1.0 - ADAM_B1) * g
    v2 = ADAM_B2 * v + (1.0 - ADAM_B2) * (g * g)
    m_hat = m2 / (1.0 - ADAM_B1 ** ADAM_STEP)
    v_hat = v2 / (1.0 - ADAM_B2 ** ADAM_STEP)
    delta = -ADAM_LR * (m_hat / (jnp.sqrt(v_hat) + ADAM_EPS) + ADAM_WD * w)
    return delta, m2, v2


def _adamw(w, m, v, gparts, name):
    Lw, R, C = w.shape
    tr = _row_tile(R, C, budget=1024 * 1024)
    flat = [h for pair in gparts for h in pair]

    def body(*refs):
        w_ref, m_ref, v_ref = refs[:3]
        g_refs = refs[3:3 + 2 * Lw]
        go_ref, d_ref, mo_ref, vo_ref = refs[3 + 2 * Lw:]
        g = g_refs[0][...] + g_refs[1][...]
        for l in range(1, Lw):
            g = jnp.where(pl.program_id(0) == l, g_refs[2 * l][...] + g_refs[2 * l + 1][...], g)
        d, m2, v2 = _adamw_math(w_ref[...], g, m_ref[...], v_ref[...])
        go_ref[...] = g
        d_ref[...] = d
        mo_ref[...] = m2
        vo_ref[...] = v2

    blk = pl.BlockSpec((None, tr, C), lambda l, i: (l, i, 0))
    gblk = pl.BlockSpec((tr, C), lambda l, i: (i, 0))
    return pl.pallas_call(
        body, grid=(Lw, R // tr), in_specs=[blk, blk, blk] + [gblk] * (2 * Lw), out_specs=[blk] * 4,
        out_shape=[_sds((Lw, R, C))] * 4, compiler_params=_cp("parallel", "parallel"), name=name)(w, m, v, *flat)


def _small_reduce_adamw(items, loss_row, name):
    n = len(items)
    gshapes = [it[0].shape for it in items] + [loss_row.shape]
    pshapes = [it[1].shape for it in items]
    ng = n + 1

    def body(*refs):
        g_in = refs[:ng]
        wmv = refs[ng:ng + 3 * n]
        outs = refs[ng + 3 * n:ng + 3 * n + 4 * n + 1]
        bufs = refs[ng + 7 * n + 1:ng + 7 * n + 1 + ng]
        send, recv = refs[-2:]
        x, y, c = _place()
        me = 4 * x + 2 * y + c
        k = 2 * x + y
        flips = [(fx, fy, fc) for fx in (0, 1) for fy in (0, 1) for fc in (0, 1)][1:]

        def peer(f):
            return (x ^ f[0], y ^ f[1], c ^ f[2])

        def slot(p):
            return 4 * p[0] + 2 * p[1] + p[2]

        for a in range(ng):
            bufs[a][me] = g_in[a][...]
        sends = [pltpu.make_async_remote_copy(g_in[a], bufs[a].at[me], send.at[a, j], recv.at[a, j],
                                              device_id=peer(f), device_id_type=MESH)
                 for a in range(ng) for j, f in enumerate(flips)]
        for cp in sends:
            cp.start()
        for a in range(ng):
            for j, f in enumerate(flips):
                pltpu.make_async_remote_copy(g_in[a], bufs[a].at[slot(peer(f))], send.at[a, j], recv.at[a, j],
                                             device_id=peer(f), device_id_type=MESH).wait_recv()
        for cp in sends:
            cp.wait_send()
        for a in range(ng):
            sharded = len(gshapes[a]) == 3

            def part(d):
                return bufs[a][d, k] if sharded else bufs[a][d]

            tot = part(0)
            for d in range(1, N_DEV):
                tot = tot + part(d)
            if a == n:
                outs[4 * n][...] = tot
                continue
            pr, pc = pshapes[a]
            g = tot[:pr, :pc]
            w_ref, m_ref, v_ref = wmv[3 * a:3 * a + 3]
            d_, m2, v2 = _adamw_math(w_ref[...], g, m_ref[...], v_ref[...])
            outs[4 * a][...] = g
            outs[4 * a + 1][...] = d_
            outs[4 * a + 2][...] = m2
            outs[4 * a + 3][...] = v2

    vm = pl.BlockSpec(memory_space=pltpu.VMEM)
    args = [it[0] for it in items] + [loss_row]
    for it in items:
        args += [it[1], it[2], it[3]]
    out_shape = []
    for ps in pshapes:
        out_shape += [_sds(ps)] * 4
    out_shape.append(_sds(loss_row.shape))
    return pl.pallas_call(
        body, in_specs=[vm] * len(args), out_specs=[vm] * len(out_shape), out_shape=out_shape,
        scratch_shapes=[pltpu.VMEM((N_DEV,) + tuple(s), F32) for s in gshapes]
        + [pltpu.SemaphoreType.DMA((ng, N_DEV - 1)), pltpu.SemaphoreType.DMA((ng, N_DEV - 1))],
        compiler_params=pltpu.CompilerParams(has_side_effects=True, vmem_limit_bytes=V7X_VMEM_LIMIT), name=name)(*args)


_PERM = (0, 2, 1, 3)


def _cols_from_shards(g):
    return g.transpose(1, 0, 2).reshape(g.shape[1], N_CHIPS * g.shape[2])


def _rope_tables(positions):
    inv_freq = ROPE_THETA ** (-jnp.arange(0, HEAD_DIM, 2, dtype=F32) / HEAD_DIM)
    ang = positions.astype(F32).reshape(-1, 1) * inv_freq
    cos, sin = jnp.cos(ang), jnp.sin(ang)
    cos = jnp.concatenate([cos, cos, cos, cos], axis=-1)
    sin_s = jnp.concatenate([-sin, sin, -sin, sin], axis=-1)
    return cos, sin_s


def kernel(x, positions, norm_mix, norm_ffn, norm_final, mix_w_in, pool_w, pool_scale, attn_sinks, mix_w_out, ssm_w_in, ssm_conv_w, ssm_conv_b, ssm_dt_bias, ssm_A_log, ssm_D, ssm_norm, ssm_w_out, ffn_w_up, ffn_conv_w, ffn_conv_b, ffn_w_down, loss_target, m_norm_mix, m_norm_ffn, m_norm_final, m_mix_w_in, m_pool_w, m_pool_scale, m_attn_sinks, m_mix_w_out, m_ssm_w_in, m_ssm_conv_w, m_ssm_conv_b, m_ssm_dt_bias, m_ssm_A_log, m_ssm_D, m_ssm_norm, m_ssm_w_out, m_ffn_w_up, m_ffn_conv_w, m_ffn_conv_b, m_ffn_w_down, v_norm_mix, v_norm_ffn, v_norm_final, v_mix_w_in, v_pool_w, v_pool_scale, v_attn_sinks, v_mix_w_out, v_ssm_w_in, v_ssm_conv_w, v_ssm_conv_b, v_ssm_dt_bias, v_ssm_A_log, v_ssm_D, v_ssm_norm, v_ssm_w_out, v_ffn_w_up, v_ffn_conv_w, v_ffn_conv_b, v_ffn_w_down):
    W = dict(norm_mix=norm_mix, norm_ffn=norm_ffn, norm_final=norm_final, mix_w_in=mix_w_in, pool_w=pool_w, pool_scale=pool_scale, attn_sinks=attn_sinks, mix_w_out=mix_w_out, ssm_w_in=ssm_w_in, ssm_conv_w=ssm_conv_w, ssm_conv_b=ssm_conv_b, ssm_dt_bias=ssm_dt_bias, ssm_A_log=ssm_A_log, ssm_D=ssm_D, ssm_norm=ssm_norm, ssm_w_out=ssm_w_out, ffn_w_up=ffn_w_up, ffn_conv_w=ffn_conv_w, ffn_conv_b=ffn_conv_b, ffn_w_down=ffn_w_down)
    Mo = dict(norm_mix=m_norm_mix, norm_ffn=m_norm_ffn, norm_final=m_norm_final, mix_w_in=m_mix_w_in, pool_w=m_pool_w, pool_scale=m_pool_scale, attn_sinks=m_attn_sinks, mix_w_out=m_mix_w_out, ssm_w_in=m_ssm_w_in, ssm_conv_w=m_ssm_conv_w, ssm_conv_b=m_ssm_conv_b, ssm_dt_bias=m_ssm_dt_bias, ssm_A_log=m_ssm_A_log, ssm_D=m_ssm_D, ssm_norm=m_ssm_norm, ssm_w_out=m_ssm_w_out, ffn_w_up=m_ffn_w_up, ffn_conv_w=m_ffn_conv_w, ffn_conv_b=m_ffn_conv_b, ffn_w_down=m_ffn_w_down)
    Vo = dict(norm_mix=v_norm_mix, norm_ffn=v_norm_ffn, norm_final=v_norm_final, mix_w_in=v_mix_w_in, pool_w=v_pool_w, pool_scale=v_pool_scale, attn_sinks=v_attn_sinks, mix_w_out=v_mix_w_out, ssm_w_in=v_ssm_w_in, ssm_conv_w=v_ssm_conv_w, ssm_conv_b=v_ssm_conv_b, ssm_dt_bias=v_ssm_dt_bias, ssm_A_log=v_ssm_A_log, ssm_D=v_ssm_D, ssm_norm=v_ssm_norm, ssm_w_out=v_ssm_w_out, ffn_w_up=v_ffn_w_up, ffn_conv_w=v_ffn_conv_w, ffn_conv_b=v_ffn_conv_b, ffn_w_down=v_ffn_w_down)

    kchip = 2 * lax.axis_index("x") + lax.axis_index("y")

    def own_slot(g, own):
        return lax.dynamic_update_slice_in_dim(g, own[None], kchip, axis=0)

    later = dict(ffn0=[ffn_w_up[0].astype(MXU), ffn_w_down[0].astype(MXU)],
                 ssm=[ssm_w_in[0].astype(MXU), ssm_w_out[0].astype(MXU)],
                 ffn1=[ffn_w_up[1].astype(MXU), ffn_w_down[1].astype(MXU)])
    sh = [mix_w_in[0].astype(MXU), mix_w_out[0].astype(MXU), ssm_conv_w[0], ssm_conv_b, ssm_norm, ffn_conv_w]
    first = _gather_shards(sh, "gather_first")
    g_mi, g_mo, g_scw, g_scb, g_sn, g_fcw = [own_slot(g, own) for g, own in zip(first, sh)]
    started, token = _spread_start(list(later.values()), False, first[0], "gather_start")
    started = dict(zip(later.keys(), started))
    fcw = [jnp.concatenate([g_fcw[p, i] for p in _PERM], axis=1) for i in range(2)]
    P = dict(
        nm=norm_mix, nf=norm_ffn, nfin=norm_final,
        wmi=_cols_from_shards(g_mi), wmo=g_mo.reshape(D_MODEL, D_MODEL),
        pool_w=pool_w[0], pool_scale=pool_scale, sinks=attn_sinks[0],
        scw=_cols_from_shards(g_scw), scb=g_scb.reshape(1, SSM_CONV_DIM), snorm=g_sn.reshape(1, SSM_D_INNER),
        dt_bias=jnp.pad(ssm_dt_bias, ((0, 0), (0, LANES - SSM_HEADS))), a_log=jnp.pad(ssm_A_log, ((0, 0), (0, LANES - SSM_HEADS))),
        d_exp=jnp.repeat(ssm_D, SSM_D_INNER // SSM_HEADS, axis=1),
        fcb=[jnp.concatenate([ffn_conv_b[i:i + 1, p * FFN_TC:(p + 1) * FFN_TC] for p in _PERM], axis=1) for i in range(2)],
    )

    def fetch(group, after):
        owns, lands = _spread_wait(started[group], False, after, f"gather_wait_{group}")
        a, b = [own_slot(g, own) for g, own in zip(lands, owns)]
        if group == "ssm":
            wsi = _cols_from_shards(a)
            zx = SSM_D_INNER + SSM_CONV_DIM
            return dict(wz=wsi[:, :SSM_D_INNER], wxbc=wsi[:, SSM_D_INNER:zx],
                        wdt=jnp.pad(wsi[:, zx:], ((0, 0), (0, LANES - SSM_HEADS))), wso=b.reshape(SSM_D_INNER, D_MODEL))
        i = int(group[-1])
        return dict(wup=jnp.concatenate([a[p] for p in _PERM], axis=1), wdn=b.reshape(D_FF, D_MODEL), fcw=fcw[i])

    cos, sin_s = _rope_tables(positions)
    sent = {}

    def send(group, grads):
        res, tok = _spread_start([grads], True, jnp.zeros((SUBLANES, LANES), F32), f"grad_start_{group}")
        sent[group] = res[0]
        return tok

    loss_row, grad_x, big, small = _local_step(x[0], cos, sin_s, loss_target[0], P, fetch, token, send)

    kidx = kchip.astype(jnp.int32).reshape(1)
    group_names = dict(ffn1=["ffn_w_up1", "ffn_w_down1"], ssm=["ssm_w_in", "ssm_w_out"], ffn0=["ffn_w_up0", "ffn_w_down0"],
                       mix=["mix_w_in", "mix_w_out"])
    names, mine = [], []
    for group, started_g in sent.items():
        grads, lands = _spread_wait(started_g, True, grad_x, f"grad_wait_{group}")
        for nm, g, land in zip(group_names[group], grads, lands):
            names.append(nm)
            mine.append(_chip_sum(g, land, kidx, f"chip_sum_{nm}"))
    theirs = _sibling_exchange(mine, "sibling_exchange")
    red = {nm: (a, b) for nm, a, b in zip(names, mine, theirs)}

    out = {}

    def big_update(pname, gparts):
        w = W[pname]
        lw = len(gparts)
        shp = w.shape
        rr, cc = gparts[0][0].shape
        w3, m3, v3 = (t.reshape(lw, rr, cc) for t in (w, Mo[pname], Vo[pname]))
        res = _adamw(w3, m3, v3, gparts, f"adamw_{pname}")
        out[pname] = tuple(r.reshape(shp) for r in res)

    big_update("mix_w_in", [red["mix_w_in"]])
    big_update("mix_w_out", [red["mix_w_out"]])
    big_update("ssm_w_in", [red["ssm_w_in"]])
    big_update("ssm_w_out", [red["ssm_w_out"]])
    big_update("ffn_w_up", [red["ffn_w_up0"], red["ffn_w_up1"]])
    big_update("ffn_w_down", [red["ffn_w_down0"], red["ffn_w_down1"]])

    small_names = ["norm_mix", "norm_ffn", "norm_final", "pool_w", "pool_scale", "attn_sinks", "ssm_dt_bias", "ssm_A_log",
                   "ssm_D", "ffn_conv_b", "ssm_conv_w", "ssm_conv_b", "ssm_norm", "ffn_conv_w"]

    def as2d(t):
        if t.ndim == 1:
            return t.reshape(1, -1)
        return t.reshape(-1, t.shape[-1])

    items = [(small[nm], as2d(W[nm]), as2d(Mo[nm]), as2d(Vo[nm])) for nm in small_names]
    res = _small_reduce_adamw(items, loss_row, "small_reduce_adamw")
    for a, nm in enumerate(small_names):
        out[nm] = tuple(r.reshape(W[nm].shape) for r in res[4 * a:4 * a + 4])
    loss = res[-1][0, 0]

    order = ["norm_mix", "norm_ffn", "norm_final", "mix_w_in", "pool_w", "pool_scale", "attn_sinks", "mix_w_out", "ssm_w_in",
             "ssm_conv_w", "ssm_conv_b", "ssm_dt_bias", "ssm_A_log", "ssm_D", "ssm_norm", "ssm_w_out", "ffn_w_up", "ffn_conv_w",
             "ffn_conv_b", "ffn_w_down"]
    return (loss, grad_x.reshape(x.shape), *[out[nm][0] for nm in order], *[out[nm][1] for nm in order],
            *[out[nm][2] for nm in order], *[out[nm][3] for nm in order])
```

```python
import functools

import jax
import jax.numpy as jnp
from jax import lax
from jax.experimental import pallas as pl
from jax.experimental.pallas import tpu as pltpu

F32 = jnp.float32
BF16 = jnp.bfloat16
MXU = BF16
HI = lax.Precision.HIGHEST

D_MODEL = 1024
POOL_WINDOWS = (2, 4, 8, 16)
POOL_DIM = 512
POOL_GROUP = 128
HEAD_DIM = 64
N_HEADS = 8
N_KV_HEADS = 2
GQ = 4
Q_DIM = 512
KV_DIM = 128
BLOCK = 128
ROPE_THETA = 10000.0
MIX_IN_DIM = 1280
SSM_D_INNER = 2048
SSM_HEADS = 32
SSM_GROUPS = 8
SSM_STATE = 128
SSM_CONV = 4
SSM_CHUNK = 128
SSM_CONV_DIM = 4096
SSM_IN_DIM = 6176
D_FF = 2816
FFN_CONV = 3
NORM_EPS = 1e-6
SSM_NORM_EPS = 1e-5
ADAM_LR = 0.001
ADAM_B1 = 0.9
ADAM_B2 = 0.999
ADAM_EPS = 1e-08
ADAM_WD = 0.01
ADAM_STEP = 10

N_CHIPS = 4
N_DEV = 8
LANES = 128
SUBLANES = 8
V7X_VMEM_LIMIT = 56 * 1024 * 1024
NEG = -1e30
MESH = pl.DeviceIdType.MESH


def _cp(*sem):
    return pltpu.CompilerParams(dimension_semantics=sem if sem else None, vmem_limit_bytes=V7X_VMEM_LIMIT)


def _sds(shape, dtype=F32):
    return jax.ShapeDtypeStruct(tuple(shape), dtype)


def _iota(shape, dim):
    return lax.broadcasted_iota(jnp.int32, shape, dim)


def _silu(x):
    return x * (1.0 / (1.0 + jnp.exp(-x)))


def _dsilu(x):
    s = 1.0 / (1.0 + jnp.exp(-x))
    return s * (1.0 + x * (1.0 - s))


def _mm(a, b, *, ta=False, tb=False, tm, tn, tk, res=None, out_dtype=F32, out_shard_perm=None, name):
    M, K = (a.shape[1], a.shape[0]) if ta else a.shape
    N = b.shape[0] if tb else b.shape[1]
    tm, tn, tk = min(tm, M), min(tn, N), min(tk, K)
    gm, gn, gk = M // tm, N // tn, K // tk
    assert gm * tm == M and gn * tn == N and gk * tk == K, (name, M, N, K, tm, tn, tk)
    a_spec = pl.BlockSpec((tk, tm), lambda i, j, k: (k, i)) if ta else pl.BlockSpec((tm, tk), lambda i, j, k: (i, k))
    b_spec = pl.BlockSpec((tn, tk), lambda i, j, k: (j, k)) if tb else pl.BlockSpec((tk, tn), lambda i, j, k: (k, j))
    dims = (((0 if ta else 1,), (1 if tb else 0,)), ((), ()))
    has_res = res is not None

    def body(*refs):
        a_ref, b_ref = refs[0], refs[1]
        r_ref = refs[2] if has_res else None
        o_ref = refs[3] if has_res else refs[2]
        p = lax.dot_general(a_ref[...].astype(MXU), b_ref[...].astype(MXU), dims, preferred_element_type=F32)
        if gk == 1:
            if has_res:
                p = p + r_ref[...]
            o_ref[...] = p.astype(out_dtype)
        else:
            acc = refs[-1]
            k = pl.program_id(2)

            @pl.when(k == 0)
            def _():
                acc[...] = p

            @pl.when(k > 0)
            def _():
                acc[...] += p

            @pl.when(k == gk - 1)
            def _():
                r = acc[...]
                if has_res:
                    r = r + r_ref[...]
                o_ref[...] = r.astype(out_dtype)

    in_specs = [a_spec, b_spec]
    args = [a, b]
    if has_res:
        in_specs.append(pl.BlockSpec((tm, tn), lambda i, j, k: (i, j)))
        args.append(res)
    if out_shard_perm is None:
        out_spec = pl.BlockSpec((tm, tn), lambda i, j, k: (i, j))
        out_shape = _sds((M, N), out_dtype)
    else:
        assert gn == len(out_shard_perm) == 4 and tuple(out_shard_perm) == (0, 2, 1, 3)
        out_spec = pl.BlockSpec((None, tm, tn), lambda i, j, k: ((j % 2) * 2 + j // 2, i, 0))
        out_shape = _sds((gn, M, tn), out_dtype)
    return pl.pallas_call(
        body, grid=(gm, gn, gk), in_specs=in_specs, out_specs=out_spec, out_shape=out_shape,
        scratch_shapes=[pltpu.VMEM((tm, tn), F32)] if gk > 1 else [],
        compiler_params=_cp("parallel", "parallel", "arbitrary"), name=name)(*args)


def _rmsnorm_fwd(x, w, name, token=None):
    T, D = x.shape
    tm = min(T, 512)
    has_token = token is not None

    def body(*refs):
        x_ref, w_ref, o_ref = refs[0], refs[1], refs[-1]
        xv = x_ref[...]
        if has_token:
            xv = xv + refs[2][0:1, 0:1]
        r = lax.rsqrt(jnp.mean(xv * xv, axis=-1, keepdims=True) + NORM_EPS)
        o_ref[...] = (xv * r * w_ref[...]).astype(o_ref.dtype)

    in_specs = [pl.BlockSpec((tm, D), lambda i: (i, 0)), pl.BlockSpec((1, D), lambda i: (0, 0))]
    args = [x, w.reshape(1, D)]
    if has_token:
        in_specs.append(pl.BlockSpec((SUBLANES, LANES), lambda i: (0, 0)))
        args.append(token)
    return pl.pallas_call(
        body, grid=(T // tm,), in_specs=in_specs,
        out_specs=pl.BlockSpec((tm, D), lambda i: (i, 0)), out_shape=_sds((T, D), MXU),
        compiler_params=_cp("parallel"), name=name)(*args)


def _rmsnorm_bwd(x, w, dh, dres, name, token=None):
    T, D = x.shape
    tm = min(T, 512)
    has_token = token is not None

    def body(*refs):
        x_ref, w_ref, dh_ref, dr_ref = refs[:4]
        dx_ref, dw_ref = refs[-2:]
        xv = x_ref[...]
        r = lax.rsqrt(jnp.mean(xv * xv, axis=-1, keepdims=True) + NORM_EPS)
        xh = xv * r
        dh = dh_ref[...]
        g = dh * w_ref[...]
        dr = dr_ref[...] + refs[4][0:1, 0:1] if has_token else dr_ref[...]
        dx_ref[...] = dr + r * (g - xh * jnp.mean(g * xh, axis=-1, keepdims=True))
        part = jnp.sum(dh * xh, axis=0, keepdims=True)

        @pl.when(pl.program_id(0) == 0)
        def _():
            dw_ref[...] = part

        @pl.when(pl.program_id(0) > 0)
        def _():
            dw_ref[...] += part

    row = pl.BlockSpec((tm, D), lambda i: (i, 0))
    vec = pl.BlockSpec((1, D), lambda i: (0, 0))
    in_specs = [row, vec, row, row]
    args = [x, w.reshape(1, D), dh, dres]
    if has_token:
        in_specs.append(pl.BlockSpec((SUBLANES, LANES), lambda i: (0, 0)))
        args.append(token)
    return pl.pallas_call(
        body, grid=(T // tm,), in_specs=in_specs, out_specs=[row, vec],
        out_shape=[_sds((T, D)), _sds((1, D))], compiler_params=_cp("arbitrary"), name=name)(*args)


def _loss_head(x, w, target, name):
    T, D = x.shape
    tm = min(T, 512)

    def body(x_ref, w_ref, t_ref, loss_ref, dx_ref, dw_ref):
        xv = x_ref[...]
        r = lax.rsqrt(jnp.mean(xv * xv, axis=-1, keepdims=True) + NORM_EPS)
        xh = xv * r
        wv = w_ref[...]
        e = xh * wv - t_ref[...]
        lpart = 0.5 * jnp.sum(jnp.mean(e * e, axis=-1, keepdims=True), axis=0, keepdims=True)
        dy = e * (1.0 / D)
        g = dy * wv
        dx_ref[...] = r * (g - xh * jnp.mean(g * xh, axis=-1, keepdims=True))
        part = jnp.sum(dy * xh, axis=0, keepdims=True)
        lrow = jnp.broadcast_to(lpart, (1, LANES))

        @pl.when(pl.program_id(0) == 0)
        def _():
            dw_ref[...] = part
            loss_ref[...] = lrow

        @pl.when(pl.program_id(0) > 0)
        def _():
            dw_ref[...] += part
            loss_ref[...] += lrow

    row = pl.BlockSpec((tm, D), lambda i: (i, 0))
    vec = pl.BlockSpec((1, D), lambda i: (0, 0))
    return pl.pallas_call(
        body, grid=(T // tm,), in_specs=[row, vec, row],
        out_specs=[pl.BlockSpec((1, LANES), lambda i: (0, 0)), row, vec],
        out_shape=[_sds((1, LANES)), _sds((T, D)), _sds((1, D))],
        compiler_params=_cp("arbitrary"), name=name)(x, w.reshape(1, D), target)


def _shift_down(cur, prev8, s):
    if s == 0:
        return cur
    tm = cur.shape[0]
    rc = pltpu.roll(cur, s, 0)
    top = jnp.where(_iota((SUBLANES, cur.shape[1]), 0) < s, pltpu.roll(prev8, s, 0), rc[:SUBLANES])
    return jnp.concatenate([top, rc[SUBLANES:]], axis=0) if tm > SUBLANES else top


def _shift_up(cur, next8, s):
    if s == 0:
        return cur
    tm = cur.shape[0]
    rc = pltpu.roll(cur, tm - s, 0)
    bot = jnp.where(_iota((SUBLANES, cur.shape[1]), 0) >= SUBLANES - s, pltpu.roll(next8, SUBLANES - s, 0), rc[tm - SUBLANES:])
    return jnp.concatenate([rc[:tm - SUBLANES], bot], axis=0) if tm > SUBLANES else bot


def _conv_rows(cur, prev8, w, b, K):
    acc = cur * w[K - 1:K, :] + b
    for s in range(1, K):
        acc = acc + _shift_down(cur, prev8, s) * w[K - 1 - s:K - s, :]
    return acc


def _halo_specs(tm, tc, col_of):
    q = tm // SUBLANES

    def prev_map(i, j):
        return (jnp.maximum(i * q - 1, 0), col_of(j))

    def make_next(n_row_tiles):
        def next_map(i, j):
            return (jnp.minimum((i + 1) * q, n_row_tiles * q - 1), col_of(j))
        return next_map

    return (lambda: pl.BlockSpec((SUBLANES, tc), prev_map)), (lambda n: pl.BlockSpec((SUBLANES, tc), make_next(n)))


FFN_TC = 1408


def _ffn_mid_fwd(hid, cw, cb, name):
    T = hid.shape[0]
    tm = min(T, 256)
    nt, nj = T // tm, D_FF // FFN_TC
    K = FFN_CONV

    def body(h_ref, hp_ref, w_ref, b_ref, o_ref):
        i = pl.program_id(0)
        cur = h_ref[...]
        prev8 = jnp.where(i > 0, hp_ref[...], 0.0)
        hc = _conv_rows(cur, prev8, w_ref[...], b_ref[...], K)
        o_ref[...] = (_silu(hc[:, FFN_TC:]) * hc[:, :FFN_TC]).astype(o_ref.dtype)

    mk_prev, _ = _halo_specs(tm, 2 * FFN_TC, lambda j: j)
    return pl.pallas_call(
        body, grid=(nt, nj),
        in_specs=[pl.BlockSpec((tm, 2 * FFN_TC), lambda i, j: (i, j)), mk_prev(),
                  pl.BlockSpec((K, 2 * FFN_TC), lambda i, j: (0, j)), pl.BlockSpec((1, 2 * FFN_TC), lambda i, j: (0, j))],
        out_specs=pl.BlockSpec((tm, FFN_TC), lambda i, j: (i, j)), out_shape=_sds((T, D_FF), MXU),
        compiler_params=_cp("parallel", "parallel"), name=name)(hid, hid, cw, cb)


def _ffn_mid_bwd(hid, cw, cb, da, name):
    T = hid.shape[0]
    tm = min(T, 256)
    nt, nj = T // tm, D_FF // FFN_TC
    K = FFN_CONV
    W2 = 2 * FFN_TC

    def body(h_ref, hp_ref, hn_ref, da_ref, dan_ref, w_ref, b_ref, dh_ref, dw_ref, db_ref):
        i = pl.program_id(1)
        w = w_ref[...]
        b = b_ref[...]
        cur = h_ref[...]
        prev8 = jnp.where(i > 0, hp_ref[...], 0.0)
        nxt8 = hn_ref[...]
        last = i == nt - 1

        def dpre(hc, dav):
            u, g = hc[:, :FFN_TC], hc[:, FFN_TC:]
            return jnp.concatenate([dav * _silu(g), dav * u * _dsilu(g)], axis=1)

        hc = _conv_rows(cur, prev8, w, b, K)
        d_cur = dpre(hc, da_ref[...])
        hc_n = _conv_rows(nxt8, cur[tm - SUBLANES:], w, b, K)
        d_nxt = jnp.where(last, 0.0, dpre(hc_n, dan_ref[...]))
        ups = [d_cur] + [_shift_up(d_cur, d_nxt, s) for s in range(1, K)]
        dh = ups[0] * w[K - 1:K, :]
        for s in range(1, K):
            dh = dh + ups[s] * w[K - 1 - s:K - s, :]
        dh_ref[...] = dh.astype(dh_ref.dtype)
        dwp = jnp.concatenate([jnp.sum(ups[K - 1 - k] * cur, axis=0, keepdims=True) for k in range(K)], axis=0)
        dbp = jnp.sum(d_cur, axis=0, keepdims=True)

        @pl.when(i == 0)
        def _():
            dw_ref[...] = dwp
            db_ref[...] = dbp

        @pl.when(i > 0)
        def _():
            dw_ref[...] += dwp
            db_ref[...] += dbp

    q = tm // SUBLANES
    blk = pl.BlockSpec((tm, W2), lambda j, i: (i, j))
    prv = pl.BlockSpec((SUBLANES, W2), lambda j, i: (jnp.maximum(i * q - 1, 0), j))
    nxt = pl.BlockSpec((SUBLANES, W2), lambda j, i: (jnp.minimum((i + 1) * q, nt * q - 1), j))
    dab = pl.BlockSpec((tm, FFN_TC), lambda j, i: (i, j))
    dan = pl.BlockSpec((SUBLANES, FFN_TC), lambda j, i: (jnp.minimum((i + 1) * q, nt * q - 1), j))
    return pl.pallas_call(
        body, grid=(nj, nt),
        in_specs=[blk, prv, nxt, dab, dan, pl.BlockSpec((K, W2), lambda j, i: (0, j)), pl.BlockSpec((1, W2), lambda j, i: (0, j))],
        out_specs=[blk, pl.BlockSpec((K, W2), lambda j, i: (0, j)), pl.BlockSpec((1, W2), lambda j, i: (0, j))],
        out_shape=[_sds((T, 2 * D_FF), MXU), _sds((K, 2 * D_FF)), _sds((1, 2 * D_FF))],
        compiler_params=_cp("parallel", "arbitrary"), name=name)(hid, hid, hid, da, da, cw, cb)


def _rope(t, cos, sin_s, inverse=False):
    n = t.shape[1] // LANES
    c = jnp.concatenate([cos] * n, axis=1) if n > 1 else cos
    s = jnp.concatenate([sin_s] * n, axis=1) if n > 1 else sin_s
    a = pltpu.roll(t, HEAD_DIM // 2, 1)
    b = pltpu.roll(t, t.shape[1] - HEAD_DIM // 2, 1)
    first = (_iota(t.shape, 1) % HEAD_DIM) < HEAD_DIM // 2
    rot = jnp.where(first, b, a) * s
    return t * c - rot if inverse else t * c + rot


def _stack_heads(t, g):
    return jnp.concatenate([t[:, (GQ * g + r) * HEAD_DIM:(GQ * g + r + 1) * HEAD_DIM] for r in range(GQ)], axis=0)


def _stack_cols(t, g):
    return jnp.concatenate([t[:, GQ * g + r:GQ * g + r + 1] for r in range(GQ)], axis=0)


def _pool_sums(prev, cur, w):
    s = jnp.concatenate([prev, cur], axis=0)
    sh = 1
    while sh < w:
        s = s + pltpu.roll(s, sh, 0)
        sh *= 2
    return s[BLOCK:]


def _nt(a, b):
    return lax.dot_general(a.astype(MXU), b.astype(MXU), (((1,), (1,)), ((), ())), preferred_element_type=F32)


def _tn(a, b):
    return lax.dot_general(a.astype(MXU), b.astype(MXU), (((0,), (0,)), ((), ())), preferred_element_type=F32)


def _nn(a, b):
    return jnp.dot(a.astype(MXU), b.astype(MXU), preferred_element_type=F32)


def _mixcore_fwd(proj, cos, sin_s, pool_w, pool_scale, sinks, name):
    T = proj.shape[0]
    nb = T // BLOCK
    scale = HEAD_DIM ** -0.5

    def body(p_ref, pp_ref, c_ref, s_ref, cp_ref, sp_ref, pw_ref, ps_ref, sk_ref, cat_ref, at_ref, lse_ref):
        i = pl.program_id(0)
        has_prev = i > 0
        cur = p_ref[...]
        prv = jnp.where(has_prev, pp_ref[...], 0.0)
        tpos = (i * BLOCK + _iota((BLOCK, 1), 0) + 1).astype(F32)
        for g, w in enumerate(POOL_WINDOWS):
            sl = slice(g * POOL_GROUP, (g + 1) * POOL_GROUP)
            pooled = _pool_sums(prv[:, sl], cur[:, sl], w) / jnp.minimum(tpos, float(w)) - cur[:, sl]
            cat_ref[:, sl] = (_nn(pooled, pw_ref[g]) * ps_ref[:, sl]).astype(cat_ref.dtype)
        q = _rope(cur[:, POOL_DIM:POOL_DIM + Q_DIM], c_ref[...], s_ref[...])
        kc = _rope(cur[:, POOL_DIM + Q_DIM:POOL_DIM + Q_DIM + KV_DIM], c_ref[...], s_ref[...])
        kp = _rope(prv[:, POOL_DIM + Q_DIM:POOL_DIM + Q_DIM + KV_DIM], cp_ref[...], sp_ref[...])
        vc = cur[:, POOL_DIM + Q_DIM + KV_DIM:]
        vp = prv[:, POOL_DIM + Q_DIM + KV_DIM:]
        ri = _iota((GQ * BLOCK, BLOCK), 0) % BLOCK
        cj = _iota((GQ * BLOCK, BLOCK), 1)
        mc = cj <= ri
        mp = jnp.logical_and(cj > ri, has_prev)
        outs, lses = [], []
        for g in range(N_KV_HEADS):
            hs = slice(g * HEAD_DIM, (g + 1) * HEAD_DIM)
            qg = _stack_heads(q, g) * scale
            sc = jnp.where(mc, _nt(qg, kc[:, hs]), NEG)
            sp = jnp.where(mp, _nt(qg, kp[:, hs]), NEG)
            sink = jnp.concatenate([jnp.full((BLOCK, 1), sk_ref[GQ * g + r], F32) for r in range(GQ)], axis=0)
            m = jnp.maximum(jnp.maximum(jnp.max(sc, axis=1, keepdims=True), jnp.max(sp, axis=1, keepdims=True)), sink)
            pc = jnp.exp(sc - m)
            pp = jnp.exp(sp - m)
            den = jnp.sum(pc, axis=1, keepdims=True) + jnp.sum(pp, axis=1, keepdims=True) + jnp.exp(sink - m)
            o = (_nn(pc, vc[:, hs]) + _nn(pp, vp[:, hs])) / den
            lse = m + jnp.log(den)
            for r in range(GQ):
                outs.append(o[r * BLOCK:(r + 1) * BLOCK])
                lses.append(lse[r * BLOCK:(r + 1) * BLOCK])
        attn = jnp.concatenate(outs, axis=1)
        at_ref[...] = attn
        cat_ref[:, POOL_DIM:] = attn.astype(cat_ref.dtype)
        lane = _iota((BLOCK, LANES), 1)
        lrow = jnp.zeros((BLOCK, LANES), F32)
        for h in range(N_HEADS):
            lrow = jnp.where(lane == h, lses[h], lrow)
        lse_ref[...] = lrow

    cur = lambda w: pl.BlockSpec((BLOCK, w), lambda i: (i, 0))
    prv = lambda w: pl.BlockSpec((BLOCK, w), lambda i: (jnp.maximum(i - 1, 0), 0))
    return pl.pallas_call(
        body, grid=(nb,),
        in_specs=[cur(MIX_IN_DIM), prv(MIX_IN_DIM), cur(LANES), cur(LANES), prv(LANES), prv(LANES),
                  pl.BlockSpec((4, POOL_GROUP, POOL_GROUP), lambda i: (0, 0, 0)), pl.BlockSpec((1, POOL_DIM), lambda i: (0, 0)),
                  pl.BlockSpec(memory_space=pltpu.SMEM)],
        out_specs=[cur(2 * POOL_DIM), cur(Q_DIM), cur(LANES)],
        out_shape=[_sds((T, 2 * POOL_DIM), MXU), _sds((T, Q_DIM)), _sds((T, LANES))],
        compiler_params=_cp("parallel"), name=name)(proj, proj, cos, sin_s, cos, sin_s, pool_w, pool_scale, sinks)


def _mixcore_bwd(proj, cos, sin_s, pool_w, pool_scale, sinks, attn, lse, dcat, name):
    T = proj.shape[0]
    nb = T // BLOCK
    scale = HEAD_DIM ** -0.5
    QO, KO, VO = POOL_DIM, POOL_DIM + Q_DIM, POOL_DIM + Q_DIM + KV_DIM

    def body(p_ref, pp_ref, pn_ref, c_ref, s_ref, cp_ref, sp_ref, cn_ref, sn_ref, pw_ref, ps_ref, sk_ref,
             at_ref, atn_ref, l_ref, ln_ref, d_ref, dn_ref, dp_ref, dpw_ref, dps_ref, dsk_ref):
        i = pl.program_id(0)
        has_prev = i > 0
        has_next = i < nb - 1
        cur = p_ref[...]
        prv = jnp.where(has_prev, pp_ref[...], 0.0)
        d_cur = d_ref[...]
        d_nxt = jnp.where(has_next, dn_ref[...], 0.0)

        tpos = (i * BLOCK + _iota((BLOCK, 1), 0) + 1).astype(F32)
        tpos2 = (i * BLOCK + _iota((2 * BLOCK, 1), 0) + 1).astype(F32)
        ps = ps_ref[...]
        dps_parts, dpw_parts = [], []
        for g, w in enumerate(POOL_WINDOWS):
            sl = slice(g * POOL_GROUP, (g + 1) * POOL_GROUP)
            pooled = _pool_sums(prv[:, sl], cur[:, sl], w) / jnp.minimum(tpos, float(w)) - cur[:, sl]
            mixed = _nn(pooled, pw_ref[g])
            dps_parts.append(jnp.sum(d_cur[:, sl] * mixed, axis=0, keepdims=True))
            dm2 = jnp.concatenate([d_cur[:, sl], d_nxt[:, sl]], axis=0) * ps[:, sl]
            dpw_parts.append(_tn(pooled, dm2[:BLOCK]))
            dpool2 = _nt(dm2, pw_ref[g])
            e = dpool2 / jnp.minimum(tpos2, float(w))
            sh = 1
            while sh < w:
                e = e + pltpu.roll(e, 2 * BLOCK - sh, 0)
                sh *= 2
            dp_ref[:, sl] = (e[:BLOCK] - dpool2[:BLOCK]).astype(dp_ref.dtype)
        dpsp = jnp.concatenate(dps_parts, axis=1)

        nxt = pn_ref[...]
        q = _rope(cur[:, QO:KO], c_ref[...], s_ref[...])
        qn = _rope(nxt[:, QO:KO], cn_ref[...], sn_ref[...])
        kc = _rope(cur[:, KO:VO], c_ref[...], s_ref[...])
        kp = _rope(prv[:, KO:VO], cp_ref[...], sp_ref[...])
        vc, vp = cur[:, VO:], prv[:, VO:]
        do, don = d_cur[:, POOL_DIM:], d_nxt[:, POOL_DIM:]
        dl = do * at_ref[...]
        dln = don * atn_ref[...]
        lse, lsen = l_ref[...], ln_ref[...]
        ri = _iota((GQ * BLOCK, BLOCK), 0) % BLOCK
        cj = _iota((GQ * BLOCK, BLOCK), 1)
        mc = cj <= ri
        mp = jnp.logical_and(cj > ri, has_prev)
        mn = jnp.logical_and(cj > ri, has_next)
        dq_parts, dk_parts, dv_parts, dsk_vals = [], [], [], []
        for g in range(N_KV_HEADS):
            hs = slice(g * HEAD_DIM, (g + 1) * HEAD_DIM)
            qg, qng = _stack_heads(q, g) * scale, _stack_heads(qn, g) * scale
            dog, dong = _stack_heads(do, g), _stack_heads(don, g)
            delta = jnp.sum(_stack_heads(dl, g), axis=1, keepdims=True)
            deltan = jnp.sum(_stack_heads(dln, g), axis=1, keepdims=True)
            lg, lng = _stack_cols(lse, g), _stack_cols(lsen, g)
            pc = jnp.where(mc, jnp.exp(_nt(qg, kc[:, hs]) - lg), 0.0)
            pp = jnp.where(mp, jnp.exp(_nt(qg, kp[:, hs]) - lg), 0.0)
            pn = jnp.where(mn, jnp.exp(_nt(qng, kc[:, hs]) - lng), 0.0)
            dsc = pc * (_nt(dog, vc[:, hs]) - delta)
            dsp = pp * (_nt(dog, vp[:, hs]) - delta)
            dsn = pn * (_nt(dong, vc[:, hs]) - deltan)
            dqg = (_nn(dsc, kc[:, hs]) + _nn(dsp, kp[:, hs])) * scale
            dq_parts += [dqg[r * BLOCK:(r + 1) * BLOCK] for r in range(GQ)]
            dk_parts.append(_tn(dsc, qg) + _tn(dsn, qng))
            dv_parts.append(_tn(pc, dog) + _tn(pn, dong))
            sink = jnp.concatenate([jnp.full((BLOCK, 1), sk_ref[GQ * g + r], F32) for r in range(GQ)], axis=0)
            dsk = -jnp.exp(sink - lg) * delta
            dsk_vals += [jnp.sum(dsk[r * BLOCK:(r + 1) * BLOCK], axis=0, keepdims=True) for r in range(GQ)]
        dq = _rope(jnp.concatenate(dq_parts, axis=1), c_ref[...], s_ref[...], inverse=True)
        dk = _rope(jnp.concatenate(dk_parts, axis=1), c_ref[...], s_ref[...], inverse=True)
        dp_ref[:, QO:KO] = dq.astype(dp_ref.dtype)
        dp_ref[:, KO:VO] = dk.astype(dp_ref.dtype)
        dp_ref[:, VO:] = jnp.concatenate(dv_parts, axis=1).astype(dp_ref.dtype)
        lane = _iota((1, LANES), 1)
        dskp = jnp.zeros((1, LANES), F32)
        for h in range(N_HEADS):
            dskp = jnp.where(lane == h, dsk_vals[h], dskp)

        @pl.when(i == 0)
        def _():
            dps_ref[...] = dpsp
            dsk_ref[...] = dskp
            for g in range(4):
                dpw_ref[g] = dpw_parts[g]

        @pl.when(i > 0)
        def _():
            dps_ref[...] += dpsp
            dsk_ref[...] += dskp
            for g in range(4):
                dpw_ref[g] += dpw_parts[g]

    cur = lambda w: pl.BlockSpec((BLOCK, w), lambda i: (i, 0))
    prv = lambda w: pl.BlockSpec((BLOCK, w), lambda i: (jnp.maximum(i - 1, 0), 0))
    nxt = lambda w: pl.BlockSpec((BLOCK, w), lambda i: (jnp.minimum(i + 1, nb - 1), 0))
    return pl.pallas_call(
        body, grid=(nb,),
        in_specs=[cur(MIX_IN_DIM), prv(MIX_IN_DIM), nxt(MIX_IN_DIM),
                  cur(LANES), cur(LANES), prv(LANES), prv(LANES), nxt(LANES), nxt(LANES),
                  pl.BlockSpec((4, POOL_GROUP, POOL_GROUP), lambda i: (0, 0, 0)), pl.BlockSpec((1, POOL_DIM), lambda i: (0, 0)),
                  pl.BlockSpec(memory_space=pltpu.SMEM),
                  cur(Q_DIM), nxt(Q_DIM), cur(LANES), nxt(LANES), cur(2 * POOL_DIM), nxt(2 * POOL_DIM)],
        out_specs=[cur(MIX_IN_DIM), pl.BlockSpec((4, POOL_GROUP, POOL_GROUP), lambda i: (0, 0, 0)),
                   pl.BlockSpec((1, POOL_DIM), lambda i: (0, 0)), pl.BlockSpec((1, LANES), lambda i: (0, 0))],
        out_shape=[_sds((T, MIX_IN_DIM), MXU), _sds((4, POOL_GROUP, POOL_GROUP)), _sds((1, POOL_DIM)), _sds((1, LANES))],
        compiler_params=_cp("arbitrary"), name=name)(
            proj, proj, proj, cos, sin_s, cos, sin_s, cos, sin_s, pool_w, pool_scale, sinks, attn, attn, lse, lse, dcat, dcat)


SSM_TC = 128
GROUP_W = SSM_D_INNER // SSM_GROUPS
PERM_W = GROUP_W + 2 * SSM_STATE


def _perm_col(n):
    nx = SSM_D_INNER // SSM_TC
    nbt = SSM_GROUPS
    x_idx = (n // 2) * 4 + n % 2
    b_idx = (n - nx) * 4 + 2
    c_idx = (n - nx - nbt) * 4 + 3
    return jnp.where(n < nx, x_idx, jnp.where(n < nx + nbt, b_idx, c_idx))


def _ssm_pre_fwd(xbc, cw, cb, name):
    T = xbc.shape[0]
    tm = min(T, 1024)
    K = SSM_CONV
    q = tm // SUBLANES

    def body(x_ref, xp_ref, w_ref, b_ref, o_ref):
        prev8 = jnp.where(pl.program_id(0) > 0, xp_ref[...], 0.0)
        o_ref[...] = _silu(_conv_rows(x_ref[...], prev8, w_ref[...], b_ref[...], K))

    tc = 512
    return pl.pallas_call(
        body, grid=(T // tm, SSM_CONV_DIM // tc),
        in_specs=[pl.BlockSpec((tm, tc), lambda i, j: (i, j)),
                  pl.BlockSpec((SUBLANES, tc), lambda i, j: (jnp.maximum(i * q - 1, 0), j)),
                  pl.BlockSpec((K, tc), lambda i, j: (0, j)), pl.BlockSpec((1, tc), lambda i, j: (0, j))],
        out_specs=pl.BlockSpec((tm, tc), lambda i, j: (i, j)), out_shape=_sds((T, SSM_CONV_DIM)),
        compiler_params=_cp("parallel", "parallel"), name=name)(xbc, xbc, cw, cb)


def _ssm_pre_bwd(xbc, cw, cb, dact_perm, name):
    T = xbc.shape[0]
    tm = min(T, 1024)
    nt = T // tm
    K = SSM_CONV
    q = tm // SUBLANES
    tc = SSM_TC

    def body(x_ref, xp_ref, xn_ref, d_ref, dn_ref, w_ref, b_ref, dx_ref, dw_ref, db_ref):
        i = pl.program_id(1)
        w = w_ref[...]
        b = b_ref[...]
        cur = x_ref[...]
        prev8 = jnp.where(i > 0, xp_ref[...], 0.0)
        nxt8 = xn_ref[...]
        d_cur = d_ref[...] * _dsilu(_conv_rows(cur, prev8, w, b, K))
        d_nxt = jnp.where(i == nt - 1, 0.0, dn_ref[...] * _dsilu(_conv_rows(nxt8, cur[tm - SUBLANES:], w, b, K)))
        ups = [d_cur] + [_shift_up(d_cur, d_nxt, s) for s in range(1, K)]
        dx = ups[0] * w[K - 1:K, :]
        for s in range(1, K):
            dx = dx + ups[s] * w[K - 1 - s:K - s, :]
        dx_ref[...] = dx.astype(dx_ref.dtype)
        dwp = jnp.concatenate([jnp.sum(ups[K - 1 - k] * cur, axis=0, keepdims=True) for k in range(K)], axis=0)
        dbp = jnp.sum(d_cur, axis=0, keepdims=True)

        @pl.when(i == 0)
        def _():
            dw_ref[...] = dwp
            db_ref[...] = dbp

        @pl.when(i > 0)
        def _():
            dw_ref[...] += dwp
            db_ref[...] += dbp

    nxt_row = lambda i: jnp.minimum((i + 1) * q, nt * q - 1)
    return pl.pallas_call(
        body, grid=(SSM_CONV_DIM // tc, nt),
        in_specs=[pl.BlockSpec((tm, tc), lambda j, i: (i, j)),
                  pl.BlockSpec((SUBLANES, tc), lambda j, i: (jnp.maximum(i * q - 1, 0), j)),
                  pl.BlockSpec((SUBLANES, tc), lambda j, i: (nxt_row(i), j)),
                  pl.BlockSpec((tm, tc), lambda j, i: (i, _perm_col(j))),
                  pl.BlockSpec((SUBLANES, tc), lambda j, i: (nxt_row(i), _perm_col(j))),
                  pl.BlockSpec((K, tc), lambda j, i: (0, j)), pl.BlockSpec((1, tc), lambda j, i: (0, j))],
        out_specs=[pl.BlockSpec((tm, tc), lambda j, i: (i, j)), pl.BlockSpec((K, tc), lambda j, i: (0, j)),
                   pl.BlockSpec((1, tc), lambda j, i: (0, j))],
        out_shape=[_sds((T, SSM_CONV_DIM), MXU), _sds((K, SSM_CONV_DIM)), _sds((1, SSM_CONV_DIM))],
        compiler_params=_cp("parallel", "arbitrary"), name=name)(xbc, xbc, xbc, dact_perm, dact_perm, cw, cb)


def _dot_hi(a, b):
    return jnp.dot(a, b, precision=HI, preferred_element_type=F32)


def _ssd_common(dtraw, bias, alog):
    L = SSM_CHUNK
    xb = dtraw + bias
    dt = jnp.maximum(xb, 0.0) + jnp.log1p(jnp.exp(-jnp.abs(xb)))
    A = -jnp.exp(alog)
    tril = (_iota((L, L), 1) <= _iota((L, L), 0)).astype(F32)
    acs = _dot_hi(tril, dt * A)
    return xb, dt, A, tril, acs


def _head_selectors():
    es = (_iota((LANES, SSM_D_INNER), 0) == _iota((LANES, SSM_D_INNER), 1) // HEAD_DIM).astype(BF16)
    est = (_iota((SSM_D_INNER, LANES), 1) == _iota((SSM_D_INNER, LANES), 0) // HEAD_DIM).astype(BF16)
    return es, est


def _dot_sel(v, sel):
    hi = v.astype(BF16)
    r1 = v - hi.astype(F32)
    mid = r1.astype(BF16)
    lo = (r1 - mid.astype(F32)).astype(BF16)
    d = lambda a: jnp.dot(a, sel, preferred_element_type=F32)
    return (d(hi) + d(mid)) + d(lo)


def _expand_heads(v, es):
    return _dot_sel(v, es)


def _reduce_heads(q, est):
    return _dot_sel(q, est)


def _per_state_row(v, g):
    return jnp.concatenate([jnp.broadcast_to(v[:, GQ * g + r:GQ * g + r + 1], (HEAD_DIM, 1)) for r in range(GQ)], axis=0)


def _ssd_fwd(xact, dtraw, dt_bias, a_log, name):
    T = xact.shape[0]
    nc = T // SSM_CHUNK
    L = SSM_CHUNK
    BO, CO = SSM_D_INNER, SSM_D_INNER + SSM_GROUPS * SSM_STATE

    def body(x_ref, dt_ref, bias_ref, al_ref, es_ref, y_ref, st_ref, state):
        @pl.when(pl.program_id(0) == 0)
        def _():
            state[...] = jnp.zeros(state.shape, F32)

        _, dt, A, tril, acs = _ssd_common(dt_ref[...], bias_ref[...], al_ref[...])
        acsT = acs.T
        last = acs[L - 1:L, :]
        cd = jnp.exp(last)
        es = es_ref[...]
        dtX = _expand_heads(dt, es)
        EX = _expand_heads(jnp.exp(acs), es)
        decX = _expand_heads(jnp.exp(last - acs), es)
        for g in range(SSM_GROUPS):
            gs = slice(g * GROUP_W, (g + 1) * GROUP_W)
            B = x_ref[:, BO + g * SSM_STATE:BO + (g + 1) * SSM_STATE]
            C = x_ref[:, CO + g * SSM_STATE:CO + (g + 1) * SSM_STATE]
            X = x_ref[:, gs] * dtX[:, gs]
            CB = _nt(C, B)
            yd = []
            for r in range(GQ):
                h = GQ * g + r
                Lm = jnp.exp(jnp.where(tril > 0, acs[:, h:h + 1] - acsT[h:h + 1, :], NEG))
                yd.append(_nn(CB * Lm, X[:, r * HEAD_DIM:(r + 1) * HEAD_DIM]))
            S = state[g]
            st_ref[g] = S
            y_ref[:, gs] = jnp.concatenate(yd, axis=1) + _nt(C, S) * EX[:, gs]
            state[g] = S * _per_state_row(cd, g) + _tn(X * decX[:, gs], B)

    es, _ = _head_selectors()
    return pl.pallas_call(
        body, grid=(nc,),
        in_specs=[pl.BlockSpec((L, SSM_CONV_DIM), lambda c: (c, 0)), pl.BlockSpec((L, LANES), lambda c: (c, 0)),
                  pl.BlockSpec((1, LANES), lambda c: (0, 0)), pl.BlockSpec((1, LANES), lambda c: (0, 0)),
                  pl.BlockSpec((LANES, SSM_D_INNER), lambda c: (0, 0))],
        out_specs=[pl.BlockSpec((L, SSM_D_INNER), lambda c: (c, 0)),
                   pl.BlockSpec((None, SSM_GROUPS, GROUP_W, SSM_STATE), lambda c: (c, 0, 0, 0))],
        out_shape=[_sds((T, SSM_D_INNER)), _sds((nc, SSM_GROUPS, GROUP_W, SSM_STATE))],
        scratch_shapes=[pltpu.VMEM((SSM_GROUPS, GROUP_W, SSM_STATE), F32)],
        compiler_params=_cp("arbitrary"), name=name)(xact, dtraw, dt_bias, a_log, es)


def _ssd_bwd(xact, dtraw, dt_bias, a_log, d_skip, states, dy, name):
    T = xact.shape[0]
    nc = T // SSM_CHUNK
    L = SSM_CHUNK
    BO, CO = SSM_D_INNER, SSM_D_INNER + SSM_GROUPS * SSM_STATE

    def body(x_ref, dt_ref, bias_ref, al_ref, dsk_ref, es_ref, est_ref, st_ref, dy_ref,
             dxp_ref, ddt_ref, dbias_ref, dal_ref, dd_ref, dstate, qa, qx):
        cc = pl.program_id(0)

        @pl.when(cc == 0)
        def _():
            dstate[...] = jnp.zeros(dstate.shape, F32)

        xb, dt, A, tril, acs = _ssd_common(dt_ref[...], bias_ref[...], al_ref[...])
        acsT = acs.T
        last = acs[L - 1:L, :]
        cd = jnp.exp(last)
        es, est = es_ref[...], est_ref[...]
        dtX = _expand_heads(dt, es)
        EX = _expand_heads(jnp.exp(acs), es)
        decX = _expand_heads(jnp.exp(last - acs), es)
        lane1 = _iota((1, LANES), 1)
        lane = _iota((L, LANES), 1)
        sub = _iota((L, LANES), 0)
        ztot = jnp.zeros((1, LANES), F32)
        wrow = jnp.zeros((L, LANES), F32)
        wcolT = jnp.zeros((LANES, L), F32)
        rows_dec, rows_dd = [], []
        for g in range(SSM_GROUPS):
            gs = slice(g * GROUP_W, (g + 1) * GROUP_W)
            x = x_ref[:, gs]
            B = x_ref[:, BO + g * SSM_STATE:BO + (g + 1) * SSM_STATE]
            C = x_ref[:, CO + g * SSM_STATE:CO + (g + 1) * SSM_STATE]
            dY = dy_ref[:, gs]
            dtx, e_x, dec_x = dtX[:, gs], EX[:, gs], decX[:, gs]
            X = x * dtx
            CB = _nt(C, B)
            S = st_ref[g]
            dS_out = dstate[g]
            dcb_sum = jnp.zeros((L, L), F32)
            dxd = []
            for r in range(GQ):
                h = GQ * g + r
                hs = slice(r * HEAD_DIM, (r + 1) * HEAD_DIM)
                Lm = jnp.exp(jnp.where(tril > 0, acs[:, h:h + 1] - acsT[h:h + 1, :], NEG))
                M = CB * Lm
                dM = _nt(dY[:, hs], X[:, hs])
                dxd.append(_tn(M, dY[:, hs]))
                dcb_sum = dcb_sum + dM * Lm
                Wm = dM * M
                wrow = jnp.where(lane == h, jnp.sum(Wm, axis=1, keepdims=True), wrow)
                wcolT = jnp.where(sub == h, jnp.sum(Wm, axis=0, keepdims=True), wcolT)
            dXd = jnp.concatenate(dxd, axis=1)
            G = _nt(C, S)
            dG = dY * e_x
            dDX = _nt(B, dS_out)
            dX = dXd + dec_x * dDX
            t_dec = dDX * X * dec_x
            qa[:, gs] = dG * G - t_dec
            qx[:, gs] = dX * x
            rows_dec.append(jnp.sum(t_dec, axis=0, keepdims=True))
            rows_dd.append(jnp.sum(dY * x, axis=0, keepdims=True))
            zc = jnp.sum(dS_out * S, axis=1, keepdims=True)
            for r in range(GQ):
                ztot = jnp.where(lane1 == GQ * g + r, jnp.sum(zc[r * HEAD_DIM:(r + 1) * HEAD_DIM], axis=0, keepdims=True), ztot)
            dxp_ref[:, g * PERM_W:g * PERM_W + GROUP_W] = dX * dtx + dY * dsk_ref[:, gs]
            dxp_ref[:, g * PERM_W + GROUP_W:g * PERM_W + GROUP_W + SSM_STATE] = _tn(dcb_sum, C) + _nn(X * dec_x, dS_out)
            dxp_ref[:, g * PERM_W + GROUP_W + SSM_STATE:(g + 1) * PERM_W] = _nn(dcb_sum, B) + _nn(dG, S)
            dstate[g] = dS_out * _per_state_row(cd, g) + _tn(dG, C)
        rows = jnp.concatenate([jnp.concatenate(rows_dec, axis=1), jnp.concatenate(rows_dd, axis=1)]
                               + [jnp.zeros((SUBLANES - 2, SSM_D_INNER), F32)], axis=0)
        rsum = _reduce_heads(rows, est)
        dlast = rsum[0:1, :] + cd * ztot
        dacs = (wrow - wcolT.T) + _reduce_heads(qa[...], est) + jnp.where(sub == L - 1, dlast, 0.0)
        triu = (_iota((L, L), 0) <= _iota((L, L), 1)).astype(F32)
        da = _dot_hi(triu, dacs)
        ddtraw = (da * A + _reduce_heads(qx[...], est)) * (1.0 / (1.0 + jnp.exp(-xb)))
        ddt_ref[...] = ddtraw
        dal = jnp.sum(da * dt, axis=0, keepdims=True) * A
        ddp = rsum[1:2, :]
        dbp = jnp.sum(ddtraw, axis=0, keepdims=True)

        @pl.when(cc == 0)
        def _():
            dbias_ref[...] = dbp
            dal_ref[...] = dal
            dd_ref[...] = ddp

        @pl.when(cc > 0)
        def _():
            dbias_ref[...] += dbp
            dal_ref[...] += dal
            dd_ref[...] += ddp

    rc = lambda c: nc - 1 - c
    vec = pl.BlockSpec((1, LANES), lambda c: (0, 0))
    es, est = _head_selectors()
    return pl.pallas_call(
        body, grid=(nc,),
        in_specs=[pl.BlockSpec((L, SSM_CONV_DIM), lambda c: (rc(c), 0)), pl.BlockSpec((L, LANES), lambda c: (rc(c), 0)), vec, vec,
                  pl.BlockSpec((1, SSM_D_INNER), lambda c: (0, 0)),
                  pl.BlockSpec((LANES, SSM_D_INNER), lambda c: (0, 0)), pl.BlockSpec((SSM_D_INNER, LANES), lambda c: (0, 0)),
                  pl.BlockSpec((None, SSM_GROUPS, GROUP_W, SSM_STATE), lambda c: (rc(c), 0, 0, 0)),
                  pl.BlockSpec((L, SSM_D_INNER), lambda c: (rc(c), 0))],
        out_specs=[pl.BlockSpec((L, SSM_GROUPS * PERM_W), lambda c: (rc(c), 0)),
                   pl.BlockSpec((L, LANES), lambda c: (rc(c), 0)), vec, vec, vec],
        out_shape=[_sds((T, SSM_GROUPS * PERM_W)), _sds((T, LANES)), _sds((1, LANES)), _sds((1, LANES)), _sds((1, LANES))],
        scratch_shapes=[pltpu.VMEM((SSM_GROUPS, GROUP_W, SSM_STATE), F32), pltpu.VMEM((L, SSM_D_INNER), F32),
                        pltpu.VMEM((L, SSM_D_INNER), F32)],
        compiler_params=_cp("arbitrary"), name=name)(xact, dtraw, dt_bias, a_log, d_skip, es, est, states, dy)


def _ssm_post_fwd(y, xact, z, d_skip, nw, name):
    T = y.shape[0]
    tm = min(T, 256)
    W = SSM_D_INNER

    def body(y_ref, x_ref, z_ref, d_ref, w_ref, o_ref):
        y2 = (y_ref[...] + d_ref[...] * x_ref[...]) * _silu(z_ref[...])
        r = lax.rsqrt(jnp.mean(y2 * y2, axis=-1, keepdims=True) + SSM_NORM_EPS)
        o_ref[...] = (y2 * r * w_ref[...]).astype(o_ref.dtype)

    row = pl.BlockSpec((tm, W), lambda i: (i, 0))
    vec = pl.BlockSpec((1, W), lambda i: (0, 0))
    return pl.pallas_call(
        body, grid=(T // tm,), in_specs=[row, row, row, vec, vec], out_specs=row, out_shape=_sds((T, W), MXU),
        compiler_params=_cp("parallel"), name=name)(y, xact, z, d_skip, nw)


def _ssm_post_bwd(y, xact, z, d_skip, nw, dyn, name):
    T = y.shape[0]
    tm = min(T, 256)
    W = SSM_D_INNER

    def body(y_ref, x_ref, z_ref, d_ref, w_ref, dn_ref, dyg_ref, dz_ref, dw_ref):
        zv = z_ref[...]
        sz = _silu(zv)
        yg = y_ref[...] + d_ref[...] * x_ref[...]
        y2 = yg * sz
        r = lax.rsqrt(jnp.mean(y2 * y2, axis=-1, keepdims=True) + SSM_NORM_EPS)
        y2h = y2 * r
        dn = dn_ref[...]
        gy = dn * w_ref[...]
        dy2 = r * (gy - y2h * jnp.mean(gy * y2h, axis=-1, keepdims=True))
        dyg_ref[...] = dy2 * sz
        dz_ref[...] = (dy2 * yg * _dsilu(zv)).astype(dz_ref.dtype)
        part = jnp.sum(dn * y2h, axis=0, keepdims=True)

        @pl.when(pl.program_id(0) == 0)
        def _():
            dw_ref[...] = part

        @pl.when(pl.program_id(0) > 0)
        def _():
            dw_ref[...] += part

    row = pl.BlockSpec((tm, W), lambda i: (i, 0))
    vec = pl.BlockSpec((1, W), lambda i: (0, 0))
    return pl.pallas_call(
        body, grid=(T // tm,), in_specs=[row, row, row, vec, vec, row], out_specs=[row, row, vec],
        out_shape=[_sds((T, W)), _sds((T, W), MXU), _sds((1, W))],
        compiler_params=_cp("arbitrary"), name=name)(y, xact, z, d_skip, nw, dyn)


def _local_step(x0, cos, sin_s, target, P, fetch, token, send):
    mmf = functools.partial(_mm, tm=1024)
    big, small = {}, {}
    P = dict(P, wup={}, wdn={}, fcw={})
    h0 = _rmsnorm_fwd(x0, P["nm"][0], "norm_mix0", token=token)
    proj0 = mmf(h0, P["wmi"], tn=1280, tk=1024, name="mix_in")
    cat, attn, lse = _mixcore_fwd(proj0, cos, sin_s, P["pool_w"], P["pool_scale"], P["sinks"], "mixcore_fwd")
    x1 = mmf(cat, P["wmo"], tn=1024, tk=1024, res=x0, name="mix_out")

    def ffn_fwd(xin, i):
        hf = _rmsnorm_fwd(xin, P["nf"][i], f"norm_ffn{i}")
        got = fetch(f"ffn{i}", hf)
        P["wup"][i], P["wdn"][i], P["fcw"][i] = got["wup"], got["wdn"], got["fcw"]
        hid = mmf(hf, P["wup"][i], tn=1408, tk=1024, name=f"ffn_up{i}")
        act = _ffn_mid_fwd(hid, P["fcw"][i], P["fcb"][i], f"ffn_mid_fwd{i}")
        xout = mmf(act, P["wdn"][i], tn=1024, tk=D_FF, res=xin, name=f"ffn_down{i}")
        return hf, hid, act, xout

    hf0, hid0, act0, x2 = ffn_fwd(x1, 0)
    h1 = _rmsnorm_fwd(x2, P["nm"][1], "norm_mix1")
    P.update(fetch("ssm", h1))
    z = mmf(h1, P["wz"], tn=1024, tk=1024, name="ssm_in_z")
    xbc = mmf(h1, P["wxbc"], tn=1024, tk=1024, name="ssm_in_xbc")
    dtraw = mmf(h1, P["wdt"], tn=128, tk=1024, name="ssm_in_dt")
    xact = _ssm_pre_fwd(xbc, P["scw"], P["scb"], "ssm_pre_fwd")
    y, states = _ssd_fwd(xact, dtraw, P["dt_bias"], P["a_log"], "ssd_fwd")
    yn = _ssm_post_fwd(y, xact, z, P["d_exp"], P["snorm"], "ssm_post_fwd")
    x3 = mmf(yn, P["wso"], tn=1024, tk=SSM_D_INNER, res=x2, name="ssm_out")
    hf1, hid1, act1, x4 = ffn_fwd(x3, 1)
    loss_row, dx4, d_nfin = _loss_head(x4, P["nfin"], target, "loss_head")
    small["norm_final"] = d_nfin

    def ffn_bwd(xin, dxo, hf, hid, act, i):
        da = mmf(dxo, P["wdn"][i], tb=True, tn=1408, tk=1024, name=f"ffn_down_dx{i}")
        big[f"ffn_w_down{i}"] = dwf(act, dxo, tm=1408, tn=1024, name=f"ffn_down_dw{i}").reshape(N_CHIPS, D_FF // N_CHIPS, D_MODEL)
        dhid, dcw, dcb = _ffn_mid_bwd(hid, P["fcw"][i], P["fcb"][i], da, f"ffn_mid_bwd{i}")
        dhf = mmf(dhid, P["wup"][i], tb=True, tn=1024, tk=1408, name=f"ffn_up_dx{i}")
        big[f"ffn_w_up{i}"] = dwf(hf, dhid, tm=1024, tn=1408, out_shard_perm=(0, 2, 1, 3), name=f"ffn_up_dw{i}")
        tok = send(f"ffn{i}", [big[f"ffn_w_up{i}"], big[f"ffn_w_down{i}"]])
        dxi, dnf = _rmsnorm_bwd(xin, P["nf"][i], dhf, dxo, f"norm_ffn_bwd{i}", token=tok)
        return dxi, dnf, dcw, dcb

    dwf = functools.partial(_mm, ta=True, tk=1024, out_dtype=BF16)
    dx3, dnf1, dfcw1, dfcb1 = ffn_bwd(x3, dx4, hf1, hid1, act1, 1)
    dyn = mmf(dx3, P["wso"], tb=True, tn=1024, tk=1024, name="ssm_out_dx")
    big["ssm_w_out"] = dwf(yn, dx3, tm=1024, tn=1024, name="ssm_out_dw").reshape(N_CHIPS, SSM_D_INNER // N_CHIPS, D_MODEL)
    dyg, dz, d_snorm = _ssm_post_bwd(y, xact, z, P["d_exp"], P["snorm"], dyn, "ssm_post_bwd")
    dxact_p, ddtraw, d_dtb, d_alog, d_dskip = _ssd_bwd(xact, dtraw, P["dt_bias"], P["a_log"], P["d_exp"], states, dyg, "ssd_bwd")
    dxbc, d_scw, d_scb = _ssm_pre_bwd(xbc, P["scw"], P["scb"], dxact_p, "ssm_pre_bwd")
    dh1 = mmf(dz, P["wz"], tb=True, tn=1024, tk=1024, name="ssm_in_dx_z")
    dh1 = mmf(dxbc, P["wxbc"], tb=True, tn=1024, tk=1024, res=dh1, name="ssm_in_dx_xbc")
    dh1 = mmf(ddtraw, P["wdt"], tb=True, tn=1024, tk=128, res=dh1, name="ssm_in_dx_dt")
    dwz = dwf(h1, dz, tm=1024, tn=1024, name="ssm_in_dw_z")
    dwxbc = dwf(h1, dxbc, tm=1024, tn=1024, name="ssm_in_dw_xbc")
    dwdt = dwf(h1, ddtraw, tm=1024, tn=128, name="ssm_in_dw_dt")
    dwsi = jnp.concatenate([dwz, dwxbc, dwdt[:, :SSM_HEADS]], axis=1)
    big["ssm_w_in"] = dwsi.reshape(D_MODEL, N_CHIPS, SSM_IN_DIM // N_CHIPS).transpose(1, 0, 2)
    tok = send("ssm", [big["ssm_w_in"], big["ssm_w_out"]])
    dx2, dnm1 = _rmsnorm_bwd(x2, P["nm"][1], dh1, dx3, "norm_mix_bwd1", token=tok)
    dx1, dnf0, dfcw0, dfcb0 = ffn_bwd(x1, dx2, hf0, hid0, act0, 0)
    dcat = mmf(dx1, P["wmo"], tb=True, tn=1024, tk=1024, name="mix_out_dx")
    big["mix_w_out"] = dwf(cat, dx1, tm=1024, tn=1024, name="mix_out_dw").reshape(N_CHIPS, D_MODEL // N_CHIPS, D_MODEL)
    dproj0, d_pw, d_ps, d_sk = _mixcore_bwd(proj0, cos, sin_s, P["pool_w"], P["pool_scale"], P["sinks"], attn, lse, dcat, "mixcore_bwd")
    dh0 = mmf(dproj0, P["wmi"], tb=True, tn=1024, tk=1280, name="mix_in_dx")
    dwmi = dwf(h0, dproj0, tm=1024, tn=1280, name="mix_in_dw")
    big["mix_w_in"] = dwmi.reshape(D_MODEL, N_CHIPS, MIX_IN_DIM // N_CHIPS).transpose(1, 0, 2)
    tok = send("mix", [big["mix_w_in"], big["mix_w_out"]])
    dx0, dnm0 = _rmsnorm_bwd(x0, P["nm"][0], dh0, dx1, "norm_mix_bwd0", token=tok)

    def unperm_cols(a):
        r = a.shape[0]
        t = a.reshape(r, N_CHIPS, FFN_TC)
        return jnp.stack([t[:, p] for p in _PERM], axis=0)

    small["norm_mix"] = jnp.concatenate([dnm0, dnm1], axis=0)
    small["norm_ffn"] = jnp.concatenate([dnf0, dnf1], axis=0)
    small["pool_w"] = d_pw.reshape(4 * POOL_GROUP, POOL_GROUP)
    small["pool_scale"] = d_ps
    small["attn_sinks"] = d_sk
    small["ssm_dt_bias"] = d_dtb
    small["ssm_A_log"] = d_alog
    small["ssm_D"] = d_dskip
    fcb = jnp.stack([unperm_cols(dfcb0), unperm_cols(dfcb1)], axis=0)
    small["ffn_conv_b"] = fcb.reshape(2, 2 * D_FF)
    small["ssm_conv_w"] = d_scw.reshape(SSM_CONV, N_CHIPS, SSM_CONV_DIM // N_CHIPS).transpose(1, 0, 2)
    small["ssm_conv_b"] = d_scb.reshape(N_CHIPS, 1, SSM_CONV_DIM // N_CHIPS)
    small["ssm_norm"] = d_snorm.reshape(N_CHIPS, 1, SSM_D_INNER // N_CHIPS)
    small["ffn_conv_w"] = jnp.concatenate([unperm_cols(dfcw0), unperm_cols(dfcw1)], axis=1)
    return loss_row, dx0, big, small


ANY = pl.BlockSpec(memory_space=pl.ANY)


def _place():
    return lax.axis_index("x"), lax.axis_index("y"), lax.axis_index("c")


def _gather_shards(shards, name):
    n = len(shards)
    split = [s.size >= (1 << 16) for s in shards]

    def half(ref, a, h):
        shp = shards[a].shape
        if len(shp) == 3:
            return ref.at[h]
        r2 = shp[0] // 2
        return ref.at[pl.ds(pl.multiple_of(h * r2, 2 * SUBLANES), r2), :]

    def body(*refs):
        ins, outs = refs[:n], refs[n:2 * n]
        send, recv, fsend, frecv = refs[2 * n:]
        x, y, c = _place()
        k = 2 * x + y
        chips = [(1 - x, y), (x, 1 - y), (1 - x, 1 - y)]

        def ici(a, j, src_slot_ref, dst_slot):
            px, py = chips[j]
            src = half(src_slot_ref, a, c) if split[a] else src_slot_ref
            dst = half(outs[a].at[dst_slot], a, c) if split[a] else outs[a].at[dst_slot]
            return pltpu.make_async_remote_copy(src, dst, send.at[a, j], recv.at[a, j], device_id=(px, py, c), device_id_type=MESH)

        def d2d(a, j, h):
            px, py = chips[j]
            part = half(outs[a].at[2 * px + py], a, h)
            return pltpu.make_async_remote_copy(part, part, fsend.at[a, j], frecv.at[a, j], device_id=(x, y, 1 - c), device_id_type=MESH)

        sends = [ici(a, j, ins[a], k) for a in range(n) for j in range(3)]
        for cp in sends:
            cp.start()
        passed = []
        for a in range(n):
            for j, (px, py) in enumerate(chips):
                ici(a, j, ins[a], 2 * px + py).wait_recv()
                if split[a]:
                    passed.append(d2d(a, j, c))
                    passed[-1].start()
        for a in range(n):
            if split[a]:
                for j in range(3):
                    d2d(a, j, 1 - c).wait_recv()
        for cp in sends + passed:
            cp.wait_send()

    return pl.pallas_call(
        body, in_specs=[ANY] * n, out_specs=[ANY] * n,
        out_shape=[_sds((N_CHIPS,) + s.shape, s.dtype) for s in shards],
        scratch_shapes=[pltpu.SemaphoreType.DMA((n, 3))] * 4,
        compiler_params=pltpu.CompilerParams(has_side_effects=True), name=name)(*shards)


HBM = pl.BlockSpec(memory_space=pltpu.HBM)
SEM = pl.BlockSpec(memory_space=pltpu.SEMAPHORE)
DATAFLOW = pltpu.SideEffectType.DATAFLOW_SIDE_EFFECTING


def _spread_start(groups, slot_src, after, name):
    flat = [a for grp in groups for a in grp]
    n = len(flat)
    ng = len(groups)
    offs = [sum(len(g) for g in groups[:i]) for i in range(ng)]
    lshape = [(a.shape if slot_src else (N_CHIPS,) + a.shape) for a in flat]

    nsem = 6 * n

    def body(*refs):
        src, land = refs[:n], refs[n:2 * n]
        sems = refs[2 * n + 1:2 * n + 1 + nsem]
        token = refs[-1]
        x, y, c = _place()
        k = 2 * x + y
        chips = [(1 - x, y), (x, 1 - y), (1 - x, 1 - y)]
        for a in range(n):
            for j, (px, py) in enumerate(chips):
                s = src[a].at[2 * px + py] if slot_src else src[a]
                pltpu.make_async_remote_copy(s, land[a].at[k], sems[6 * a + 2 * j], sems[6 * a + 2 * j + 1],
                                             device_id=(px, py, c), device_id_type=MESH).start()
        token[...] = jnp.zeros(token.shape, token.dtype)

    out_shape = [pltpu.SemaphoreType.DMA(())] * nsem
    out_shape += [pltpu.HBM(a.shape, a.dtype) for a in flat] + [pltpu.HBM(s, a.dtype) for s, a in zip(lshape, flat)]
    out_shape.append(_sds((SUBLANES, LANES)))
    args = [pltpu.with_memory_space_constraint(a, pltpu.HBM) for a in flat]
    args += [pltpu.with_memory_space_constraint(lax.empty(s, a.dtype), pltpu.HBM) for s, a in zip(lshape, flat)]
    res = pl.pallas_call(
        body, name=name, out_shape=tuple(out_shape), in_specs=[HBM] * (2 * n) + [pl.BlockSpec(memory_space=pl.ANY)],
        out_specs=tuple([SEM] * nsem + [HBM] * (2 * n) + [pl.BlockSpec(memory_space=pltpu.VMEM)]),
        input_output_aliases={i: nsem + i for i in range(2 * n)},
        compiler_params=pltpu.CompilerParams(has_side_effects=DATAFLOW))(*args, after)
    sems, thru, token = res[:nsem], res[nsem:nsem + 2 * n], res[-1]
    out = []
    for gi, grp in enumerate(groups):
        sl = slice(offs[gi], offs[gi] + len(grp))
        out.append((list(sems[6 * offs[gi]:6 * (offs[gi] + len(grp))]), list(thru[:n][sl]), list(thru[n:][sl])))
    return out, token


def _spread_wait(started, slot_src, after, name):
    sems, srcs, lands = started
    n = len(srcs)

    def body(*refs):
        src, land = refs[:n], refs[n:2 * n]
        sem = refs[2 * n:2 * n + 6 * n]
        x, y, c = _place()
        chips = [(1 - x, y), (x, 1 - y), (1 - x, 1 - y)]
        for a in range(n):
            for j, (px, py) in enumerate(chips):
                s = src[a].at[2 * px + py] if slot_src else src[a]
                cp = pltpu.make_async_remote_copy(s, land[a].at[2 * px + py], sem[6 * a + 2 * j], sem[6 * a + 2 * j + 1],
                                                  device_id=(px, py, c), device_id_type=MESH)
                cp.wait_send()
                cp.wait_recv()

    res = pl.pallas_call(
        body, name=name, out_shape=tuple([pltpu.HBM(a.shape, a.dtype) for a in srcs] + [pltpu.HBM(a.shape, a.dtype) for a in lands]),
        in_specs=[HBM] * (2 * n) + [SEM] * (6 * n) + [pl.BlockSpec(memory_space=pl.ANY)], out_specs=tuple([HBM] * (2 * n)),
        input_output_aliases={i: i for i in range(2 * n)},
        compiler_params=pltpu.CompilerParams(has_side_effects=DATAFLOW))(*srcs, *lands, *sems, after)
    return list(res[:n]), list(res[n:])


def _sibling_exchange(fs, name):
    n = len(fs)

    def body(*refs):
        ins, outs = refs[:n], refs[n:2 * n]
        send, recv = refs[2 * n:]
        x, y, c = _place()
        cps = [pltpu.make_async_remote_copy(ins[a], outs[a], send.at[a], recv.at[a],
                                            device_id=(x, y, 1 - c), device_id_type=MESH) for a in range(n)]
        for cp in cps:
            cp.start()
        for cp in cps:
            cp.wait()

    return pl.pallas_call(
        body, in_specs=[ANY] * n, out_specs=[ANY] * n, out_shape=[_sds(f.shape, f.dtype) for f in fs],
        scratch_shapes=[pltpu.SemaphoreType.DMA((n,)), pltpu.SemaphoreType.DMA((n,))],
        compiler_params=pltpu.CompilerParams(has_side_effects=True), name=name)(*fs)


def _row_tile(rows, cols, budget=2 * 1024 * 1024, step=2 * SUBLANES):
    best = step
    for t in range(step, rows + 1, step):
        if rows % t == 0 and t * cols * 4 <= budget:
            best = t
    assert rows % best == 0, (rows, best)
    return best


def _chip_sum(own, parts, kidx, name):
    _, r2, C = parts.shape
    tr = _row_tile(r2, C, budget=1024 * 1024)

    def body(k_ref, o_ref_in, p1_ref, p2_ref, p3_ref, o_ref):
        o_ref[...] = ((o_ref_in[...].astype(F32) + p1_ref[...].astype(F32)) + p2_ref[...].astype(F32)) + p3_ref[...].astype(F32)

    def slot(d):
        return pl.BlockSpec((None, tr, C), lambda i, k: ((k[0] + d) % N_CHIPS, i, 0))

    return pl.pallas_call(
        body,
        grid_spec=pltpu.PrefetchScalarGridSpec(
            num_scalar_prefetch=1, grid=(r2 // tr,), in_specs=[slot(0), slot(1), slot(2), slot(3)],
            out_specs=pl.BlockSpec((tr, C), lambda i, k: (i, 0))),
        out_shape=_sds((r2, C)), compiler_params=_cp("parallel"), name=name)(kidx, own, parts, parts, parts)


def _adamw_math(w, g, m, v):
    m2 = ADAM_B1 * m + (1.0 - ADAM_B1) * g
    v2 = ADAM_B2 * v + (1.0 - ADAM_B2) * (g * g)
    m_hat = m2 / (1.0 - ADAM_B1 ** ADAM_STEP)
    v_hat = v2 / (1.0 - ADAM_B2 ** ADAM_STEP)
    delta = -ADAM_LR * (m_hat / (jnp.sqrt(v_hat) + ADAM_EPS) + ADAM_WD * w)
    return delta, m2, v2


def _adamw(w, m, v, gparts, name):
    Lw, R, C = w.shape
    tr = _row_tile(R, C, budget=1024 * 1024)
    flat = [h for pair in gparts for h in pair]

    def body(*refs):
        w_ref, m_ref, v_ref = refs[:3]
        g_refs = refs[3:3 + 2 * Lw]
        go_ref, d_ref, mo_ref, vo_ref = refs[3 + 2 * Lw:]
        g = g_refs[0][...] + g_refs[1][...]
        for l in range(1, Lw):
            g = jnp.where(pl.program_id(0) == l, g_refs[2 * l][...] + g_refs[2 * l + 1][...], g)
        d, m2, v2 = _adamw_math(w_ref[...], g, m_ref[...], v_ref[...])
        go_ref[...] = g
        d_ref[...] = d
        mo_ref[...] = m2
        vo_ref[...] = v2

    blk = pl.BlockSpec((None, tr, C), lambda l, i: (l, i, 0))
    gblk = pl.BlockSpec((tr, C), lambda l, i: (i, 0))
    return pl.pallas_call(
        body, grid=(Lw, R // tr), in_specs=[blk, blk, blk] + [gblk] * (2 * Lw), out_specs=[blk] * 4,
        out_shape=[_sds((Lw, R, C))] * 4, compiler_params=_cp("parallel", "parallel"), name=name)(w, m, v, *flat)


def _small_adamw(grads, wmv, name):
    n = len(grads)

    def body(*refs):
        g_in, p_in, outs = refs[:n], refs[n:4 * n], refs[4 * n:]
        for a in range(n):
            g = g_in[a][...]
            d_, m2, v2 = _adamw_math(p_in[3 * a][...], g, p_in[3 * a + 1][...], p_in[3 * a + 2][...])
            outs[4 * a][...] = g
            outs[4 * a + 1][...] = d_
            outs[4 * a + 2][...] = m2
            outs[4 * a + 3][...] = v2

    vm = pl.BlockSpec(memory_space=pltpu.VMEM)
    args = list(grads) + [t for tri in wmv for t in tri]
    out_shape = [_sds(g.shape) for g in grads for _ in range(4)]
    return pl.pallas_call(body, in_specs=[vm] * len(args), out_specs=[vm] * len(out_shape), out_shape=out_shape,
                          compiler_params=pltpu.CompilerParams(vmem_limit_bytes=V7X_VMEM_LIMIT), name=name)(*args)


def _small_allreduce(partials, pshapes, loss_row, name):
    n = len(partials)
    gshapes = [p.shape for p in partials] + [loss_row.shape]
    ng = n + 1

    def body(*refs):
        g_in = refs[:ng]
        outs = refs[ng:2 * ng]
        bufs = refs[2 * ng:3 * ng]
        send, recv = refs[-2:]
        x, y, c = _place()
        me = 4 * x + 2 * y + c
        k = 2 * x + y
        flips = [(fx, fy, fc) for fx in (0, 1) for fy in (0, 1) for fc in (0, 1)][1:]

        def peer(f):
            return (x ^ f[0], y ^ f[1], c ^ f[2])

        def slot(p):
            return 4 * p[0] + 2 * p[1] + p[2]

        for a in range(ng):
            bufs[a][me] = g_in[a][...]
        sends = [pltpu.make_async_remote_copy(g_in[a], bufs[a].at[me], send.at[a, j], recv.at[a, j],
                                              device_id=peer(f), device_id_type=MESH)
                 for a in range(ng) for j, f in enumerate(flips)]
        for cp in sends:
            cp.start()
        for a in range(ng):
            for j, f in enumerate(flips):
                pltpu.make_async_remote_copy(g_in[a], bufs[a].at[slot(peer(f))], send.at[a, j], recv.at[a, j],
                                             device_id=peer(f), device_id_type=MESH).wait_recv()
        for cp in sends:
            cp.wait_send()
        for a in range(ng):
            sharded = len(gshapes[a]) == 3

            def part(d):
                return bufs[a][d, k] if sharded else bufs[a][d]

            tot = part(0)
            for d in range(1, N_DEV):
                tot = tot + part(d)
            if a == n:
                outs[n][...] = tot
            else:
                pr, pc = pshapes[a]
                outs[a][...] = tot[:pr, :pc]

    vm = pl.BlockSpec(memory_space=pltpu.VMEM)
    args = list(partials) + [loss_row]
    out_shape = [_sds(ps) for ps in pshapes] + [_sds(loss_row.shape)]
    return pl.pallas_call(
        body, in_specs=[vm] * len(args), out_specs=[vm] * len(out_shape), out_shape=out_shape,
        scratch_shapes=[pltpu.VMEM((N_DEV,) + tuple(s), F32) for s in gshapes]
        + [pltpu.SemaphoreType.DMA((ng, N_DEV - 1)), pltpu.SemaphoreType.DMA((ng, N_DEV - 1))],
        compiler_params=pltpu.CompilerParams(has_side_effects=True, vmem_limit_bytes=V7X_VMEM_LIMIT), name=name)(*args)


_PERM = (0, 2, 1, 3)


def _cols_from_shards(g):
    return g.transpose(1, 0, 2).reshape(g.shape[1], N_CHIPS * g.shape[2])


def _rope_tables(positions):
    inv_freq = ROPE_THETA ** (-jnp.arange(0, HEAD_DIM, 2, dtype=F32) / HEAD_DIM)
    ang = positions.astype(F32).reshape(-1, 1) * inv_freq
    cos, sin = jnp.cos(ang), jnp.sin(ang)
    cos = jnp.concatenate([cos, cos, cos, cos], axis=-1)
    sin_s = jnp.concatenate([-sin, sin, -sin, sin], axis=-1)
    return cos, sin_s


def kernel(x, positions, norm_mix, norm_ffn, norm_final, mix_w_in, pool_w, pool_scale, attn_sinks, mix_w_out, ssm_w_in, ssm_conv_w, ssm_conv_b, ssm_dt_bias, ssm_A_log, ssm_D, ssm_norm, ssm_w_out, ffn_w_up, ffn_conv_w, ffn_conv_b, ffn_w_down, loss_target, m_norm_mix, m_norm_ffn, m_norm_final, m_mix_w_in, m_pool_w, m_pool_scale, m_attn_sinks, m_mix_w_out, m_ssm_w_in, m_ssm_conv_w, m_ssm_conv_b, m_ssm_dt_bias, m_ssm_A_log, m_ssm_D, m_ssm_norm, m_ssm_w_out, m_ffn_w_up, m_ffn_conv_w, m_ffn_conv_b, m_ffn_w_down, v_norm_mix, v_norm_ffn, v_norm_final, v_mix_w_in, v_pool_w, v_pool_scale, v_attn_sinks, v_mix_w_out, v_ssm_w_in, v_ssm_conv_w, v_ssm_conv_b, v_ssm_dt_bias, v_ssm_A_log, v_ssm_D, v_ssm_norm, v_ssm_w_out, v_ffn_w_up, v_ffn_conv_w, v_ffn_conv_b, v_ffn_w_down):
    W = dict(norm_mix=norm_mix, norm_ffn=norm_ffn, norm_final=norm_final, mix_w_in=mix_w_in, pool_w=pool_w, pool_scale=pool_scale, attn_sinks=attn_sinks, mix_w_out=mix_w_out, ssm_w_in=ssm_w_in, ssm_conv_w=ssm_conv_w, ssm_conv_b=ssm_conv_b, ssm_dt_bias=ssm_dt_bias, ssm_A_log=ssm_A_log, ssm_D=ssm_D, ssm_norm=ssm_norm, ssm_w_out=ssm_w_out, ffn_w_up=ffn_w_up, ffn_conv_w=ffn_conv_w, ffn_conv_b=ffn_conv_b, ffn_w_down=ffn_w_down)
    Mo = dict(norm_mix=m_norm_mix, norm_ffn=m_norm_ffn, norm_final=m_norm_final, mix_w_in=m_mix_w_in, pool_w=m_pool_w, pool_scale=m_pool_scale, attn_sinks=m_attn_sinks, mix_w_out=m_mix_w_out, ssm_w_in=m_ssm_w_in, ssm_conv_w=m_ssm_conv_w, ssm_conv_b=m_ssm_conv_b, ssm_dt_bias=m_ssm_dt_bias, ssm_A_log=m_ssm_A_log, ssm_D=m_ssm_D, ssm_norm=m_ssm_norm, ssm_w_out=m_ssm_w_out, ffn_w_up=m_ffn_w_up, ffn_conv_w=m_ffn_conv_w, ffn_conv_b=m_ffn_conv_b, ffn_w_down=m_ffn_w_down)
    Vo = dict(norm_mix=v_norm_mix, norm_ffn=v_norm_ffn, norm_final=v_norm_final, mix_w_in=v_mix_w_in, pool_w=v_pool_w, pool_scale=v_pool_scale, attn_sinks=v_attn_sinks, mix_w_out=v_mix_w_out, ssm_w_in=v_ssm_w_in, ssm_conv_w=v_ssm_conv_w, ssm_conv_b=v_ssm_conv_b, ssm_dt_bias=v_ssm_dt_bias, ssm_A_log=v_ssm_A_log, ssm_D=v_ssm_D, ssm_norm=v_ssm_norm, ssm_w_out=v_ssm_w_out, ffn_w_up=v_ffn_w_up, ffn_conv_w=v_ffn_conv_w, ffn_conv_b=v_ffn_conv_b, ffn_w_down=v_ffn_w_down)

    kchip = 2 * lax.axis_index("x") + lax.axis_index("y")

    def own_slot(g, own):
        return lax.dynamic_update_slice_in_dim(g, own[None], kchip, axis=0)

    later = dict(ffn0=[ffn_w_up[0].astype(MXU), ffn_w_down[0].astype(MXU)],
                 ssm=[ssm_w_in[0].astype(MXU), ssm_w_out[0].astype(MXU)],
                 ffn1=[ffn_w_up[1].astype(MXU), ffn_w_down[1].astype(MXU)])
    sh = [mix_w_in[0].astype(MXU), mix_w_out[0].astype(MXU), ssm_conv_w[0], ssm_conv_b, ssm_norm, ffn_conv_w]
    first = _gather_shards(sh, "gather_first")
    g_mi, g_mo, g_scw, g_scb, g_sn, g_fcw = [own_slot(g, own) for g, own in zip(first, sh)]
    started, token = _spread_start(list(later.values()), False, first[0], "gather_start")
    started = dict(zip(later.keys(), started))
    fcw = [jnp.concatenate([g_fcw[p, i] for p in _PERM], axis=1) for i in range(2)]
    P = dict(
        nm=norm_mix, nf=norm_ffn, nfin=norm_final,
        wmi=_cols_from_shards(g_mi), wmo=g_mo.reshape(D_MODEL, D_MODEL),
        pool_w=pool_w[0], pool_scale=pool_scale, sinks=attn_sinks[0],
        scw=_cols_from_shards(g_scw), scb=g_scb.reshape(1, SSM_CONV_DIM), snorm=g_sn.reshape(1, SSM_D_INNER),
        dt_bias=jnp.pad(ssm_dt_bias, ((0, 0), (0, LANES - SSM_HEADS))), a_log=jnp.pad(ssm_A_log, ((0, 0), (0, LANES - SSM_HEADS))),
        d_exp=jnp.repeat(ssm_D, SSM_D_INNER // SSM_HEADS, axis=1),
        fcb=[jnp.concatenate([ffn_conv_b[i:i + 1, p * FFN_TC:(p + 1) * FFN_TC] for p in _PERM], axis=1) for i in range(2)],
    )

    def fetch(group, after):
        owns, lands = _spread_wait(started[group], False, after, f"gather_wait_{group}")
        a, b = [own_slot(g, own) for g, own in zip(lands, owns)]
        if group == "ssm":
            wsi = _cols_from_shards(a)
            zx = SSM_D_INNER + SSM_CONV_DIM
            return dict(wz=wsi[:, :SSM_D_INNER], wxbc=wsi[:, SSM_D_INNER:zx],
                        wdt=jnp.pad(wsi[:, zx:], ((0, 0), (0, LANES - SSM_HEADS))), wso=b.reshape(SSM_D_INNER, D_MODEL))
        i = int(group[-1])
        return dict(wup=jnp.concatenate([a[p] for p in _PERM], axis=1), wdn=b.reshape(D_FF, D_MODEL), fcw=fcw[i])

    cos, sin_s = _rope_tables(positions)
    sent = {}

    def send(group, grads):
        res, tok = _spread_start([grads], True, jnp.zeros((SUBLANES, LANES), F32), f"grad_start_{group}")
        sent[group] = res[0]
        return tok

    loss_row, grad_x, big, small = _local_step(x[0], cos, sin_s, loss_target[0], P, fetch, token, send)

    kidx = kchip.astype(jnp.int32).reshape(1)
    group_names = dict(ffn1=["ffn_w_up1", "ffn_w_down1"], ssm=["ssm_w_in", "ssm_w_out"], ffn0=["ffn_w_up0", "ffn_w_down0"],
                       mix=["mix_w_in", "mix_w_out"])
    names, mine = [], []
    for group, started_g in sent.items():
        grads, lands = _spread_wait(started_g, True, grad_x, f"grad_wait_{group}")
        for nm, g, land in zip(group_names[group], grads, lands):
            names.append(nm)
            mine.append(_chip_sum(g, land, kidx, f"chip_sum_{nm}"))
    theirs = _sibling_exchange(mine, "sibling_exchange")
    red = {nm: (a, b) for nm, a, b in zip(names, mine, theirs)}

    out = {}

    def big_update(pname, gparts):
        w = W[pname]
        lw = len(gparts)
        shp = w.shape
        rr, cc = gparts[0][0].shape
        w3, m3, v3 = (t.reshape(lw, rr, cc) for t in (w, Mo[pname], Vo[pname]))
        res = _adamw(w3, m3, v3, gparts, f"adamw_{pname}")
        out[pname] = tuple(r.reshape(shp) for r in res)

    big_update("mix_w_in", [red["mix_w_in"]])
    big_update("mix_w_out", [red["mix_w_out"]])
    big_update("ssm_w_in", [red["ssm_w_in"]])
    big_update("ssm_w_out", [red["ssm_w_out"]])
    big_update("ffn_w_up", [red["ffn_w_up0"], red["ffn_w_up1"]])
    big_update("ffn_w_down", [red["ffn_w_down0"], red["ffn_w_down1"]])

    small_names = ["norm_mix", "norm_ffn", "norm_final", "pool_w", "pool_scale", "attn_sinks", "ssm_dt_bias", "ssm_A_log",
                   "ssm_D", "ffn_conv_b", "ssm_conv_w", "ssm_conv_b", "ssm_norm", "ffn_conv_w"]

    def as2d(t):
        if t.ndim == 1:
            return t.reshape(1, -1)
        return t.reshape(-1, t.shape[-1])

    wmv = [(as2d(W[nm]), as2d(Mo[nm]), as2d(Vo[nm])) for nm in small_names]
    summed = _small_allreduce([small[nm] for nm in small_names], [t[0].shape for t in wmv], loss_row, "small_allreduce")
    res = _small_adamw(summed[:-1], wmv, "small_adamw")
    for a, nm in enumerate(small_names):
        out[nm] = tuple(r.reshape(W[nm].shape) for r in res[4 * a:4 * a + 4])
    loss = summed[-1][0, 0]

    order = ["norm_mix", "norm_ffn", "norm_final", "mix_w_in", "pool_w", "pool_scale", "attn_sinks", "mix_w_out", "ssm_w_in",
             "ssm_conv_w", "ssm_conv_b", "ssm_dt_bias", "ssm_A_log", "ssm_D", "ssm_norm", "ssm_w_out", "ffn_w_up", "ffn_conv_w",
             "ffn_conv_b", "ffn_w_down"]
    return (loss, grad_x.reshape(x.shape), *[out[nm][0] for nm in order], *[out[nm][1] for nm in order],
            *[out[nm][2] for nm in order], *[out[nm][3] for nm in order])
```

```python
import functools

import jax
import jax.numpy as jnp
from jax import lax
from jax.experimental import pallas as pl
from jax.experimental.pallas import tpu as pltpu

F32 = jnp.float32
BF16 = jnp.bfloat16
MXU = BF16
HI = lax.Precision.HIGHEST

D_MODEL = 1024
POOL_WINDOWS = (2, 4, 8, 16)
POOL_DIM = 512
POOL_GROUP = 128
HEAD_DIM = 64
N_HEADS = 8
N_KV_HEADS = 2
GQ = 4
Q_DIM = 512
KV_DIM = 128
BLOCK = 128
ROPE_THETA = 10000.0
MIX_IN_DIM = 1280
SSM_D_INNER = 2048
SSM_HEADS = 32
SSM_GROUPS = 8
SSM_STATE = 128
SSM_CONV = 4
SSM_CHUNK = 128
SSM_CONV_DIM = 4096
SSM_IN_DIM = 6176
D_FF = 2816
FFN_CONV = 3
NORM_EPS = 1e-6
SSM_NORM_EPS = 1e-5
ADAM_LR = 0.001
ADAM_B1 = 0.9
ADAM_B2 = 0.999
ADAM_EPS = 1e-08
ADAM_WD = 0.01
ADAM_STEP = 10

N_CHIPS = 4
N_DEV = 8
LANES = 128
SUBLANES = 8
V7X_VMEM_LIMIT = 56 * 1024 * 1024
NEG = -1e30
MESH = pl.DeviceIdType.MESH


def _cp(*sem):
    return pltpu.CompilerParams(dimension_semantics=sem if sem else None, vmem_limit_bytes=V7X_VMEM_LIMIT)


def _sds(shape, dtype=F32):
    return jax.ShapeDtypeStruct(tuple(shape), dtype)


def _iota(shape, dim):
    return lax.broadcasted_iota(jnp.int32, shape, dim)


def _silu(x):
    return x * (1.0 / (1.0 + jnp.exp(-x)))


def _dsilu(x):
    s = 1.0 / (1.0 + jnp.exp(-x))
    return s * (1.0 + x * (1.0 - s))


def _mm(a, b, *, ta=False, tb=False, tm, tn, tk, res=None, out_dtype=F32, out_shard_perm=None, name):
    M, K = (a.shape[1], a.shape[0]) if ta else a.shape
    N = b.shape[0] if tb else b.shape[1]
    tm, tn, tk = min(tm, M), min(tn, N), min(tk, K)
    gm, gn, gk = M // tm, N // tn, K // tk
    assert gm * tm == M and gn * tn == N and gk * tk == K, (name, M, N, K, tm, tn, tk)
    a_spec = pl.BlockSpec((tk, tm), lambda i, j, k: (k, i)) if ta else pl.BlockSpec((tm, tk), lambda i, j, k: (i, k))
    b_spec = pl.BlockSpec((tn, tk), lambda i, j, k: (j, k)) if tb else pl.BlockSpec((tk, tn), lambda i, j, k: (k, j))
    dims = (((0 if ta else 1,), (1 if tb else 0,)), ((), ()))
    has_res = res is not None

    def body(*refs):
        a_ref, b_ref = refs[0], refs[1]
        r_ref = refs[2] if has_res else None
        o_ref = refs[3] if has_res else refs[2]
        p = lax.dot_general(a_ref[...].astype(MXU), b_ref[...].astype(MXU), dims, preferred_element_type=F32)
        if gk == 1:
            if has_res:
                p = p + r_ref[...]
            o_ref[...] = p.astype(out_dtype)
        else:
            acc = refs[-1]
            k = pl.program_id(2)

            @pl.when(k == 0)
            def _():
                acc[...] = p

            @pl.when(k > 0)
            def _():
                acc[...] += p

            @pl.when(k == gk - 1)
            def _():
                r = acc[...]
                if has_res:
                    r = r + r_ref[...]
                o_ref[...] = r.astype(out_dtype)

    in_specs = [a_spec, b_spec]
    args = [a, b]
    if has_res:
        in_specs.append(pl.BlockSpec((tm, tn), lambda i, j, k: (i, j)))
        args.append(res)
    if out_shard_perm is None:
        out_spec = pl.BlockSpec((tm, tn), lambda i, j, k: (i, j))
        out_shape = _sds((M, N), out_dtype)
    else:
        assert gn == len(out_shard_perm) == 4 and tuple(out_shard_perm) == (0, 2, 1, 3)
        out_spec = pl.BlockSpec((None, tm, tn), lambda i, j, k: ((j % 2) * 2 + j // 2, i, 0))
        out_shape = _sds((gn, M, tn), out_dtype)
    return pl.pallas_call(
        body, grid=(gm, gn, gk), in_specs=in_specs, out_specs=out_spec, out_shape=out_shape,
        scratch_shapes=[pltpu.VMEM((tm, tn), F32)] if gk > 1 else [],
        compiler_params=_cp("parallel", "parallel", "arbitrary"), name=name)(*args)


def _rmsnorm_fwd(x, w, name, token=None):
    T, D = x.shape
    tm = min(T, 512)
    has_token = token is not None

    def body(*refs):
        x_ref, w_ref, o_ref = refs[0], refs[1], refs[-1]
        xv = x_ref[...]
        if has_token:
            xv = xv + refs[2][0:1, 0:1]
        r = lax.rsqrt(jnp.mean(xv * xv, axis=-1, keepdims=True) + NORM_EPS)
        o_ref[...] = (xv * r * w_ref[...]).astype(o_ref.dtype)

    in_specs = [pl.BlockSpec((tm, D), lambda i: (i, 0)), pl.BlockSpec((1, D), lambda i: (0, 0))]
    args = [x, w.reshape(1, D)]
    if has_token:
        in_specs.append(pl.BlockSpec((SUBLANES, LANES), lambda i: (0, 0)))
        args.append(token)
    return pl.pallas_call(
        body, grid=(T // tm,), in_specs=in_specs,
        out_specs=pl.BlockSpec((tm, D), lambda i: (i, 0)), out_shape=_sds((T, D), MXU),
        compiler_params=_cp("parallel"), name=name)(*args)


def _rmsnorm_bwd(x, w, dh, dres, name, token=None):
    T, D = x.shape
    tm = min(T, 512)
    has_token = token is not None

    def body(*refs):
        x_ref, w_ref, dh_ref, dr_ref = refs[:4]
        dx_ref, dw_ref = refs[-2:]
        xv = x_ref[...]
        r = lax.rsqrt(jnp.mean(xv * xv, axis=-1, keepdims=True) + NORM_EPS)
        xh = xv * r
        dh = dh_ref[...]
        g = dh * w_ref[...]
        dr = dr_ref[...] + refs[4][0:1, 0:1] if has_token else dr_ref[...]
        dx_ref[...] = dr + r * (g - xh * jnp.mean(g * xh, axis=-1, keepdims=True))
        part = jnp.sum(dh * xh, axis=0, keepdims=True)

        @pl.when(pl.program_id(0) == 0)
        def _():
            dw_ref[...] = part

        @pl.when(pl.program_id(0) > 0)
        def _():
            dw_ref[...] += part

    row = pl.BlockSpec((tm, D), lambda i: (i, 0))
    vec = pl.BlockSpec((1, D), lambda i: (0, 0))
    in_specs = [row, vec, row, row]
    args = [x, w.reshape(1, D), dh, dres]
    if has_token:
        in_specs.append(pl.BlockSpec((SUBLANES, LANES), lambda i: (0, 0)))
        args.append(token)
    return pl.pallas_call(
        body, grid=(T // tm,), in_specs=in_specs, out_specs=[row, vec],
        out_shape=[_sds((T, D)), _sds((1, D))], compiler_params=_cp("arbitrary"), name=name)(*args)


def _loss_head(x, w, target, name):
    T, D = x.shape
    tm = min(T, 512)

    def body(x_ref, w_ref, t_ref, loss_ref, dx_ref, dw_ref):
        xv = x_ref[...]
        r = lax.rsqrt(jnp.mean(xv * xv, axis=-1, keepdims=True) + NORM_EPS)
        xh = xv * r
        wv = w_ref[...]
        e = xh * wv - t_ref[...]
        lpart = 0.5 * jnp.sum(jnp.mean(e * e, axis=-1, keepdims=True), axis=0, keepdims=True)
        dy = e * (1.0 / D)
        g = dy * wv
        dx_ref[...] = r * (g - xh * jnp.mean(g * xh, axis=-1, keepdims=True))
        part = jnp.sum(dy * xh, axis=0, keepdims=True)
        lrow = jnp.broadcast_to(lpart, (1, LANES))

        @pl.when(pl.program_id(0) == 0)
        def _():
            dw_ref[...] = part
            loss_ref[...] = lrow

        @pl.when(pl.program_id(0) > 0)
        def _():
            dw_ref[...] += part
            loss_ref[...] += lrow

    row = pl.BlockSpec((tm, D), lambda i: (i, 0))
    vec = pl.BlockSpec((1, D), lambda i: (0, 0))
    return pl.pallas_call(
        body, grid=(T // tm,), in_specs=[row, vec, row],
        out_specs=[pl.BlockSpec((1, LANES), lambda i: (0, 0)), row, vec],
        out_shape=[_sds((1, LANES)), _sds((T, D)), _sds((1, D))],
        compiler_params=_cp("arbitrary"), name=name)(x, w.reshape(1, D), target)


def _shift_down(cur, prev8, s):
    if s == 0:
        return cur
    tm = cur.shape[0]
    rc = pltpu.roll(cur, s, 0)
    top = jnp.where(_iota((SUBLANES, cur.shape[1]), 0) < s, pltpu.roll(prev8, s, 0), rc[:SUBLANES])
    return jnp.concatenate([top, rc[SUBLANES:]], axis=0) if tm > SUBLANES else top


def _shift_up(cur, next8, s):
    if s == 0:
        return cur
    tm = cur.shape[0]
    rc = pltpu.roll(cur, tm - s, 0)
    bot = jnp.where(_iota((SUBLANES, cur.shape[1]), 0) >= SUBLANES - s, pltpu.roll(next8, SUBLANES - s, 0), rc[tm - SUBLANES:])
    return jnp.concatenate([rc[:tm - SUBLANES], bot], axis=0) if tm > SUBLANES else bot


def _conv_rows(cur, prev8, w, b, K):
    acc = cur * w[K - 1:K, :] + b
    for s in range(1, K):
        acc = acc + _shift_down(cur, prev8, s) * w[K - 1 - s:K - s, :]
    return acc


def _halo_specs(tm, tc, col_of):
    q = tm // SUBLANES

    def prev_map(i, j):
        return (jnp.maximum(i * q - 1, 0), col_of(j))

    def make_next(n_row_tiles):
        def next_map(i, j):
            return (jnp.minimum((i + 1) * q, n_row_tiles * q - 1), col_of(j))
        return next_map

    return (lambda: pl.BlockSpec((SUBLANES, tc), prev_map)), (lambda n: pl.BlockSpec((SUBLANES, tc), make_next(n)))


FFN_TC = 1408


def _ffn_mid_fwd(hid, cw, cb, name):
    T = hid.shape[0]
    tm = min(T, 256)
    nt, nj = T // tm, D_FF // FFN_TC
    K = FFN_CONV

    def body(h_ref, hp_ref, w_ref, b_ref, o_ref):
        i = pl.program_id(0)
        cur = h_ref[...]
        prev8 = jnp.where(i > 0, hp_ref[...], 0.0)
        hc = _conv_rows(cur, prev8, w_ref[...], b_ref[...], K)
        o_ref[...] = (_silu(hc[:, FFN_TC:]) * hc[:, :FFN_TC]).astype(o_ref.dtype)

    mk_prev, _ = _halo_specs(tm, 2 * FFN_TC, lambda j: j)
    return pl.pallas_call(
        body, grid=(nt, nj),
        in_specs=[pl.BlockSpec((tm, 2 * FFN_TC), lambda i, j: (i, j)), mk_prev(),
                  pl.BlockSpec((K, 2 * FFN_TC), lambda i, j: (0, j)), pl.BlockSpec((1, 2 * FFN_TC), lambda i, j: (0, j))],
        out_specs=pl.BlockSpec((tm, FFN_TC), lambda i, j: (i, j)), out_shape=_sds((T, D_FF), MXU),
        compiler_params=_cp("parallel", "parallel"), name=name)(hid, hid, cw, cb)


def _ffn_mid_bwd(hid, cw, cb, da, name):
    T = hid.shape[0]
    tm = min(T, 256)
    nt, nj = T // tm, D_FF // FFN_TC
    K = FFN_CONV
    W2 = 2 * FFN_TC

    def body(h_ref, hp_ref, hn_ref, da_ref, dan_ref, w_ref, b_ref, dh_ref, dw_ref, db_ref):
        i = pl.program_id(1)
        w = w_ref[...]
        b = b_ref[...]
        cur = h_ref[...]
        prev8 = jnp.where(i > 0, hp_ref[...], 0.0)
        nxt8 = hn_ref[...]
        last = i == nt - 1

        def dpre(hc, dav):
            u, g = hc[:, :FFN_TC], hc[:, FFN_TC:]
            return jnp.concatenate([dav * _silu(g), dav * u * _dsilu(g)], axis=1)

        hc = _conv_rows(cur, prev8, w, b, K)
        d_cur = dpre(hc, da_ref[...])
        hc_n = _conv_rows(nxt8, cur[tm - SUBLANES:], w, b, K)
        d_nxt = jnp.where(last, 0.0, dpre(hc_n, dan_ref[...]))
        ups = [d_cur] + [_shift_up(d_cur, d_nxt, s) for s in range(1, K)]
        dh = ups[0] * w[K - 1:K, :]
        for s in range(1, K):
            dh = dh + ups[s] * w[K - 1 - s:K - s, :]
        dh_ref[...] = dh.astype(dh_ref.dtype)
        dwp = jnp.concatenate([jnp.sum(ups[K - 1 - k] * cur, axis=0, keepdims=True) for k in range(K)], axis=0)
        dbp = jnp.sum(d_cur, axis=0, keepdims=True)

        @pl.when(i == 0)
        def _():
            dw_ref[...] = dwp
            db_ref[...] = dbp

        @pl.when(i > 0)
        def _():
            dw_ref[...] += dwp
            db_ref[...] += dbp

    q = tm // SUBLANES
    blk = pl.BlockSpec((tm, W2), lambda j, i: (i, j))
    prv = pl.BlockSpec((SUBLANES, W2), lambda j, i: (jnp.maximum(i * q - 1, 0), j))
    nxt = pl.BlockSpec((SUBLANES, W2), lambda j, i: (jnp.minimum((i + 1) * q, nt * q - 1), j))
    dab = pl.BlockSpec((tm, FFN_TC), lambda j, i: (i, j))
    dan = pl.BlockSpec((SUBLANES, FFN_TC), lambda j, i: (jnp.minimum((i + 1) * q, nt * q - 1), j))
    return pl.pallas_call(
        body, grid=(nj, nt),
        in_specs=[blk, prv, nxt, dab, dan, pl.BlockSpec((K, W2), lambda j, i: (0, j)), pl.BlockSpec((1, W2), lambda j, i: (0, j))],
        out_specs=[blk, pl.BlockSpec((K, W2), lambda j, i: (0, j)), pl.BlockSpec((1, W2), lambda j, i: (0, j))],
        out_shape=[_sds((T, 2 * D_FF), MXU), _sds((K, 2 * D_FF)), _sds((1, 2 * D_FF))],
        compiler_params=_cp("parallel", "arbitrary"), name=name)(hid, hid, hid, da, da, cw, cb)


def _rope(t, cos, sin_s, inverse=False):
    n = t.shape[1] // LANES
    c = jnp.concatenate([cos] * n, axis=1) if n > 1 else cos
    s = jnp.concatenate([sin_s] * n, axis=1) if n > 1 else sin_s
    a = pltpu.roll(t, HEAD_DIM // 2, 1)
    b = pltpu.roll(t, t.shape[1] - HEAD_DIM // 2, 1)
    first = (_iota(t.shape, 1) % HEAD_DIM) < HEAD_DIM // 2
    rot = jnp.where(first, b, a) * s
    return t * c - rot if inverse else t * c + rot


def _stack_heads(t, g):
    return jnp.concatenate([t[:, (GQ * g + r) * HEAD_DIM:(GQ * g + r + 1) * HEAD_DIM] for r in range(GQ)], axis=0)


def _stack_cols(t, g):
    return jnp.concatenate([t[:, GQ * g + r:GQ * g + r + 1] for r in range(GQ)], axis=0)


def _pool_sums(prev, cur, w):
    s = jnp.concatenate([prev, cur], axis=0)
    sh = 1
    while sh < w:
        s = s + pltpu.roll(s, sh, 0)
        sh *= 2
    return s[BLOCK:]


def _nt(a, b):
    return lax.dot_general(a.astype(MXU), b.astype(MXU), (((1,), (1,)), ((), ())), preferred_element_type=F32)


def _tn(a, b):
    return lax.dot_general(a.astype(MXU), b.astype(MXU), (((0,), (0,)), ((), ())), preferred_element_type=F32)


def _nn(a, b):
    return jnp.dot(a.astype(MXU), b.astype(MXU), preferred_element_type=F32)


def _mixcore_fwd(proj, cos, sin_s, pool_w, pool_scale, sinks, name):
    T = proj.shape[0]
    nb = T // BLOCK
    scale = HEAD_DIM ** -0.5

    def body(p_ref, pp_ref, c_ref, s_ref, cp_ref, sp_ref, pw_ref, ps_ref, sk_ref, cat_ref, at_ref, lse_ref):
        i = pl.program_id(0)
        has_prev = i > 0
        cur = p_ref[...]
        prv = jnp.where(has_prev, pp_ref[...], 0.0)
        tpos = (i * BLOCK + _iota((BLOCK, 1), 0) + 1).astype(F32)
        for g, w in enumerate(POOL_WINDOWS):
            sl = slice(g * POOL_GROUP, (g + 1) * POOL_GROUP)
            pooled = _pool_sums(prv[:, sl], cur[:, sl], w) / jnp.minimum(tpos, float(w)) - cur[:, sl]
            cat_ref[:, sl] = (_nn(pooled, pw_ref[g]) * ps_ref[:, sl]).astype(cat_ref.dtype)
        q = _rope(cur[:, POOL_DIM:POOL_DIM + Q_DIM], c_ref[...], s_ref[...])
        kc = _rope(cur[:, POOL_DIM + Q_DIM:POOL_DIM + Q_DIM + KV_DIM], c_ref[...], s_ref[...])
        kp = _rope(prv[:, POOL_DIM + Q_DIM:POOL_DIM + Q_DIM + KV_DIM], cp_ref[...], sp_ref[...])
        vc = cur[:, POOL_DIM + Q_DIM + KV_DIM:]
        vp = prv[:, POOL_DIM + Q_DIM + KV_DIM:]
        ri = _iota((GQ * BLOCK, BLOCK), 0) % BLOCK
        cj = _iota((GQ * BLOCK, BLOCK), 1)
        mc = cj <= ri
        mp = jnp.logical_and(cj > ri, has_prev)
        outs, lses = [], []
        for g in range(N_KV_HEADS):
            hs = slice(g * HEAD_DIM, (g + 1) * HEAD_DIM)
            qg = _stack_heads(q, g) * scale
            sc = jnp.where(mc, _nt(qg, kc[:, hs]), NEG)
            sp = jnp.where(mp, _nt(qg, kp[:, hs]), NEG)
            sink = jnp.concatenate([jnp.full((BLOCK, 1), sk_ref[GQ * g + r], F32) for r in range(GQ)], axis=0)
            m = jnp.maximum(jnp.maximum(jnp.max(sc, axis=1, keepdims=True), jnp.max(sp, axis=1, keepdims=True)), sink)
            pc = jnp.exp(sc - m)
            pp = jnp.exp(sp - m)
            den = jnp.sum(pc, axis=1, keepdims=True) + jnp.sum(pp, axis=1, keepdims=True) + jnp.exp(sink - m)
            o = (_nn(pc, vc[:, hs]) + _nn(pp, vp[:, hs])) / den
            lse = m + jnp.log(den)
            for r in range(GQ):
                outs.append(o[r * BLOCK:(r + 1) * BLOCK])
                lses.append(lse[r * BLOCK:(r + 1) * BLOCK])
        attn = jnp.concatenate(outs, axis=1)
        at_ref[...] = attn
        cat_ref[:, POOL_DIM:] = attn.astype(cat_ref.dtype)
        lane = _iota((BLOCK, LANES), 1)
        lrow = jnp.zeros((BLOCK, LANES), F32)
        for h in range(N_HEADS):
            lrow = jnp.where(lane == h, lses[h], lrow)
        lse_ref[...] = lrow

    cur = lambda w: pl.BlockSpec((BLOCK, w), lambda i: (i, 0))
    prv = lambda w: pl.BlockSpec((BLOCK, w), lambda i: (jnp.maximum(i - 1, 0), 0))
    return pl.pallas_call(
        body, grid=(nb,),
        in_specs=[cur(MIX_IN_DIM), prv(MIX_IN_DIM), cur(LANES), cur(LANES), prv(LANES), prv(LANES),
                  pl.BlockSpec((4, POOL_GROUP, POOL_GROUP), lambda i: (0, 0, 0)), pl.BlockSpec((1, POOL_DIM), lambda i: (0, 0)),
                  pl.BlockSpec(memory_space=pltpu.SMEM)],
        out_specs=[cur(2 * POOL_DIM), cur(Q_DIM), cur(LANES)],
        out_shape=[_sds((T, 2 * POOL_DIM), MXU), _sds((T, Q_DIM)), _sds((T, LANES))],
        compiler_params=_cp("parallel"), name=name)(proj, proj, cos, sin_s, cos, sin_s, pool_w, pool_scale, sinks)


def _mixcore_bwd(proj, cos, sin_s, pool_w, pool_scale, sinks, attn, lse, dcat, name):
    T = proj.shape[0]
    nb = T // BLOCK
    scale = HEAD_DIM ** -0.5
    QO, KO, VO = POOL_DIM, POOL_DIM + Q_DIM, POOL_DIM + Q_DIM + KV_DIM

    def body(p_ref, pp_ref, pn_ref, c_ref, s_ref, cp_ref, sp_ref, cn_ref, sn_ref, pw_ref, ps_ref, sk_ref,
             at_ref, atn_ref, l_ref, ln_ref, d_ref, dn_ref, dp_ref, dpw_ref, dps_ref, dsk_ref):
        i = pl.program_id(0)
        has_prev = i > 0
        has_next = i < nb - 1
        cur = p_ref[...]
        prv = jnp.where(has_prev, pp_ref[...], 0.0)
        d_cur = d_ref[...]
        d_nxt = jnp.where(has_next, dn_ref[...], 0.0)

        tpos = (i * BLOCK + _iota((BLOCK, 1), 0) + 1).astype(F32)
        tpos2 = (i * BLOCK + _iota((2 * BLOCK, 1), 0) + 1).astype(F32)
        ps = ps_ref[...]
        dps_parts, dpw_parts = [], []
        for g, w in enumerate(POOL_WINDOWS):
            sl = slice(g * POOL_GROUP, (g + 1) * POOL_GROUP)
            pooled = _pool_sums(prv[:, sl], cur[:, sl], w) / jnp.minimum(tpos, float(w)) - cur[:, sl]
            mixed = _nn(pooled, pw_ref[g])
            dps_parts.append(jnp.sum(d_cur[:, sl] * mixed, axis=0, keepdims=True))
            dm2 = jnp.concatenate([d_cur[:, sl], d_nxt[:, sl]], axis=0) * ps[:, sl]
            dpw_parts.append(_tn(pooled, dm2[:BLOCK]))
            dpool2 = _nt(dm2, pw_ref[g])
            e = dpool2 / jnp.minimum(tpos2, float(w))
            sh = 1
            while sh < w:
                e = e + pltpu.roll(e, 2 * BLOCK - sh, 0)
                sh *= 2
            dp_ref[:, sl] = (e[:BLOCK] - dpool2[:BLOCK]).astype(dp_ref.dtype)
        dpsp = jnp.concatenate(dps_parts, axis=1)

        nxt = pn_ref[...]
        q = _rope(cur[:, QO:KO], c_ref[...], s_ref[...])
        qn = _rope(nxt[:, QO:KO], cn_ref[...], sn_ref[...])
        kc = _rope(cur[:, KO:VO], c_ref[...], s_ref[...])
        kp = _rope(prv[:, KO:VO], cp_ref[...], sp_ref[...])
        vc, vp = cur[:, VO:], prv[:, VO:]
        do, don = d_cur[:, POOL_DIM:], d_nxt[:, POOL_DIM:]
        dl = do * at_ref[...]
        dln = don * atn_ref[...]
        lse, lsen = l_ref[...], ln_ref[...]
        ri = _iota((GQ * BLOCK, BLOCK), 0) % BLOCK
        cj = _iota((GQ * BLOCK, BLOCK), 1)
        mc = cj <= ri
        mp = jnp.logical_and(cj > ri, has_prev)
        mn = jnp.logical_and(cj > ri, has_next)
        dq_parts, dk_parts, dv_parts, dsk_vals = [], [], [], []
        for g in range(N_KV_HEADS):
            hs = slice(g * HEAD_DIM, (g + 1) * HEAD_DIM)
            qg, qng = _stack_heads(q, g) * scale, _stack_heads(qn, g) * scale
            dog, dong = _stack_heads(do, g), _stack_heads(don, g)
            delta = jnp.sum(_stack_heads(dl, g), axis=1, keepdims=True)
            deltan = jnp.sum(_stack_heads(dln, g), axis=1, keepdims=True)
            lg, lng = _stack_cols(lse, g), _stack_cols(lsen, g)
            pc = jnp.where(mc, jnp.exp(_nt(qg, kc[:, hs]) - lg), 0.0)
            pp = jnp.where(mp, jnp.exp(_nt(qg, kp[:, hs]) - lg), 0.0)
            pn = jnp.where(mn, jnp.exp(_nt(qng, kc[:, hs]) - lng), 0.0)
            dsc = pc * (_nt(dog, vc[:, hs]) - delta)
            dsp = pp * (_nt(dog, vp[:, hs]) - delta)
            dsn = pn * (_nt(dong, vc[:, hs]) - deltan)
            dqg = (_nn(dsc, kc[:, hs]) + _nn(dsp, kp[:, hs])) * scale
            dq_parts += [dqg[r * BLOCK:(r + 1) * BLOCK] for r in range(GQ)]
            dk_parts.append(_tn(dsc, qg) + _tn(dsn, qng))
            dv_parts.append(_tn(pc, dog) + _tn(pn, dong))
            sink = jnp.concatenate([jnp.full((BLOCK, 1), sk_ref[GQ * g + r], F32) for r in range(GQ)], axis=0)
            dsk = -jnp.exp(sink - lg) * delta
            dsk_vals += [jnp.sum(dsk[r * BLOCK:(r + 1) * BLOCK], axis=0, keepdims=True) for r in range(GQ)]
        dq = _rope(jnp.concatenate(dq_parts, axis=1), c_ref[...], s_ref[...], inverse=True)
        dk = _rope(jnp.concatenate(dk_parts, axis=1), c_ref[...], s_ref[...], inverse=True)
        dp_ref[:, QO:KO] = dq.astype(dp_ref.dtype)
        dp_ref[:, KO:VO] = dk.astype(dp_ref.dtype)
        dp_ref[:, VO:] = jnp.concatenate(dv_parts, axis=1).astype(dp_ref.dtype)
        lane = _iota((1, LANES), 1)
        dskp = jnp.zeros((1, LANES), F32)
        for h in range(N_HEADS):
            dskp = jnp.where(lane == h, dsk_vals[h], dskp)

        @pl.when(i == 0)
        def _():
            dps_ref[...] = dpsp
            dsk_ref[...] = dskp
            for g in range(4):
                dpw_ref[g] = dpw_parts[g]

        @pl.when(i > 0)
        def _():
            dps_ref[...] += dpsp
            dsk_ref[...] += dskp
            for g in range(4):
                dpw_ref[g] += dpw_parts[g]

    cur = lambda w: pl.BlockSpec((BLOCK, w), lambda i: (i, 0))
    prv = lambda w: pl.BlockSpec((BLOCK, w), lambda i: (jnp.maximum(i - 1, 0), 0))
    nxt = lambda w: pl.BlockSpec((BLOCK, w), lambda i: (jnp.minimum(i + 1, nb - 1), 0))
    return pl.pallas_call(
        body, grid=(nb,),
        in_specs=[cur(MIX_IN_DIM), prv(MIX_IN_DIM), nxt(MIX_IN_DIM),
                  cur(LANES), cur(LANES), prv(LANES), prv(LANES), nxt(LANES), nxt(LANES),
                  pl.BlockSpec((4, POOL_GROUP, POOL_GROUP), lambda i: (0, 0, 0)), pl.BlockSpec((1, POOL_DIM), lambda i: (0, 0)),
                  pl.BlockSpec(memory_space=pltpu.SMEM),
                  cur(Q_DIM), nxt(Q_DIM), cur(LANES), nxt(LANES), cur(2 * POOL_DIM), nxt(2 * POOL_DIM)],
        out_specs=[cur(MIX_IN_DIM), pl.BlockSpec((4, POOL_GROUP, POOL_GROUP), lambda i: (0, 0, 0)),
                   pl.BlockSpec((1, POOL_DIM), lambda i: (0, 0)), pl.BlockSpec((1, LANES), lambda i: (0, 0))],
        out_shape=[_sds((T, MIX_IN_DIM), MXU), _sds((4, POOL_GROUP, POOL_GROUP)), _sds((1, POOL_DIM)), _sds((1, LANES))],
        compiler_params=_cp("arbitrary"), name=name)(
            proj, proj, proj, cos, sin_s, cos, sin_s, cos, sin_s, pool_w, pool_scale, sinks, attn, attn, lse, lse, dcat, dcat)


SSM_TC = 128
GROUP_W = SSM_D_INNER // SSM_GROUPS
PERM_W = GROUP_W + 2 * SSM_STATE


def _perm_col(n):
    nx = SSM_D_INNER // SSM_TC
    nbt = SSM_GROUPS
    x_idx = (n // 2) * 4 + n % 2
    b_idx = (n - nx) * 4 + 2
    c_idx = (n - nx - nbt) * 4 + 3
    return jnp.where(n < nx, x_idx, jnp.where(n < nx + nbt, b_idx, c_idx))


def _ssm_pre_fwd(xbc, cw, cb, name):
    T = xbc.shape[0]
    tm = min(T, 1024)
    K = SSM_CONV
    q = tm // SUBLANES

    def body(x_ref, xp_ref, w_ref, b_ref, o_ref):
        prev8 = jnp.where(pl.program_id(0) > 0, xp_ref[...], 0.0)
        o_ref[...] = _silu(_conv_rows(x_ref[...], prev8, w_ref[...], b_ref[...], K))

    tc = 512
    return pl.pallas_call(
        body, grid=(T // tm, SSM_CONV_DIM // tc),
        in_specs=[pl.BlockSpec((tm, tc), lambda i, j: (i, j)),
                  pl.BlockSpec((SUBLANES, tc), lambda i, j: (jnp.maximum(i * q - 1, 0), j)),
                  pl.BlockSpec((K, tc), lambda i, j: (0, j)), pl.BlockSpec((1, tc), lambda i, j: (0, j))],
        out_specs=pl.BlockSpec((tm, tc), lambda i, j: (i, j)), out_shape=_sds((T, SSM_CONV_DIM)),
        compiler_params=_cp("parallel", "parallel"), name=name)(xbc, xbc, cw, cb)


def _ssm_pre_bwd(xbc, cw, cb, dact_perm, name):
    T = xbc.shape[0]
    tm = min(T, 1024)
    nt = T // tm
    K = SSM_CONV
    q = tm // SUBLANES
    tc = SSM_TC

    def body(x_ref, xp_ref, xn_ref, d_ref, dn_ref, w_ref, b_ref, dx_ref, dw_ref, db_ref):
        i = pl.program_id(1)
        w = w_ref[...]
        b = b_ref[...]
        cur = x_ref[...]
        prev8 = jnp.where(i > 0, xp_ref[...], 0.0)
        nxt8 = xn_ref[...]
        d_cur = d_ref[...] * _dsilu(_conv_rows(cur, prev8, w, b, K))
        d_nxt = jnp.where(i == nt - 1, 0.0, dn_ref[...] * _dsilu(_conv_rows(nxt8, cur[tm - SUBLANES:], w, b, K)))
        ups = [d_cur] + [_shift_up(d_cur, d_nxt, s) for s in range(1, K)]
        dx = ups[0] * w[K - 1:K, :]
        for s in range(1, K):
            dx = dx + ups[s] * w[K - 1 - s:K - s, :]
        dx_ref[...] = dx.astype(dx_ref.dtype)
        dwp = jnp.concatenate([jnp.sum(ups[K - 1 - k] * cur, axis=0, keepdims=True) for k in range(K)], axis=0)
        dbp = jnp.sum(d_cur, axis=0, keepdims=True)

        @pl.when(i == 0)
        def _():
            dw_ref[...] = dwp
            db_ref[...] = dbp

        @pl.when(i > 0)
        def _():
            dw_ref[...] += dwp
            db_ref[...] += dbp

    nxt_row = lambda i: jnp.minimum((i + 1) * q, nt * q - 1)
    return pl.pallas_call(
        body, grid=(SSM_CONV_DIM // tc, nt),
        in_specs=[pl.BlockSpec((tm, tc), lambda j, i: (i, j)),
                  pl.BlockSpec((SUBLANES, tc), lambda j, i: (jnp.maximum(i * q - 1, 0), j)),
                  pl.BlockSpec((SUBLANES, tc), lambda j, i: (nxt_row(i), j)),
                  pl.BlockSpec((tm, tc), lambda j, i: (i, _perm_col(j))),
                  pl.BlockSpec((SUBLANES, tc), lambda j, i: (nxt_row(i), _perm_col(j))),
                  pl.BlockSpec((K, tc), lambda j, i: (0, j)), pl.BlockSpec((1, tc), lambda j, i: (0, j))],
        out_specs=[pl.BlockSpec((tm, tc), lambda j, i: (i, j)), pl.BlockSpec((K, tc), lambda j, i: (0, j)),
                   pl.BlockSpec((1, tc), lambda j, i: (0, j))],
        out_shape=[_sds((T, SSM_CONV_DIM), MXU), _sds((K, SSM_CONV_DIM)), _sds((1, SSM_CONV_DIM))],
        compiler_params=_cp("parallel", "arbitrary"), name=name)(xbc, xbc, xbc, dact_perm, dact_perm, cw, cb)


def _dot_hi(a, b):
    return jnp.dot(a, b, precision=HI, preferred_element_type=F32)


def _ssd_common(dtraw, bias, alog):
    L = SSM_CHUNK
    xb = dtraw + bias
    dt = jnp.maximum(xb, 0.0) + jnp.log1p(jnp.exp(-jnp.abs(xb)))
    A = -jnp.exp(alog)
    tril = (_iota((L, L), 1) <= _iota((L, L), 0)).astype(F32)
    acs = _dot_hi(tril, dt * A)
    return xb, dt, A, tril, acs


def _head_selectors():
    es = (_iota((LANES, SSM_D_INNER), 0) == _iota((LANES, SSM_D_INNER), 1) // HEAD_DIM).astype(BF16)
    est = (_iota((SSM_D_INNER, LANES), 1) == _iota((SSM_D_INNER, LANES), 0) // HEAD_DIM).astype(BF16)
    return es, est


def _dot_sel(v, sel):
    hi = v.astype(BF16)
    r1 = v - hi.astype(F32)
    mid = r1.astype(BF16)
    lo = (r1 - mid.astype(F32)).astype(BF16)
    d = lambda a: jnp.dot(a, sel, preferred_element_type=F32)
    return (d(hi) + d(mid)) + d(lo)


def _expand_heads(v, es):
    return _dot_sel(v, es)


def _reduce_heads(q, est):
    return _dot_sel(q, est)


def _per_state_row(v, g):
    return jnp.concatenate([jnp.broadcast_to(v[:, GQ * g + r:GQ * g + r + 1], (HEAD_DIM, 1)) for r in range(GQ)], axis=0)


def _ssd_fwd(xact, dtraw, dt_bias, a_log, name):
    T = xact.shape[0]
    nc = T // SSM_CHUNK
    L = SSM_CHUNK
    BO, CO = SSM_D_INNER, SSM_D_INNER + SSM_GROUPS * SSM_STATE

    def body(x_ref, dt_ref, bias_ref, al_ref, es_ref, y_ref, st_ref, state):
        @pl.when(pl.program_id(0) == 0)
        def _():
            state[...] = jnp.zeros(state.shape, F32)

        _, dt, A, tril, acs = _ssd_common(dt_ref[...], bias_ref[...], al_ref[...])
        acsT = acs.T
        last = acs[L - 1:L, :]
        cd = jnp.exp(last)
        es = es_ref[...]
        dtX = _expand_heads(dt, es)
        EX = _expand_heads(jnp.exp(acs), es)
        decX = _expand_heads(jnp.exp(last - acs), es)
        for g in range(SSM_GROUPS):
            gs = slice(g * GROUP_W, (g + 1) * GROUP_W)
            B = x_ref[:, BO + g * SSM_STATE:BO + (g + 1) * SSM_STATE]
            C = x_ref[:, CO + g * SSM_STATE:CO + (g + 1) * SSM_STATE]
            X = x_ref[:, gs] * dtX[:, gs]
            CB = _nt(C, B)
            yd = []
            for r in range(GQ):
                h = GQ * g + r
                Lm = jnp.exp(jnp.where(tril > 0, acs[:, h:h + 1] - acsT[h:h + 1, :], NEG))
                yd.append(_nn(CB * Lm, X[:, r * HEAD_DIM:(r + 1) * HEAD_DIM]))
            S = state[g]
            st_ref[g] = S
            y_ref[:, gs] = jnp.concatenate(yd, axis=1) + _nt(C, S) * EX[:, gs]
            state[g] = S * _per_state_row(cd, g) + _tn(X * decX[:, gs], B)

    es, _ = _head_selectors()
    return pl.pallas_call(
        body, grid=(nc,),
        in_specs=[pl.BlockSpec((L, SSM_CONV_DIM), lambda c: (c, 0)), pl.BlockSpec((L, LANES), lambda c: (c, 0)),
                  pl.BlockSpec((1, LANES), lambda c: (0, 0)), pl.BlockSpec((1, LANES), lambda c: (0, 0)),
                  pl.BlockSpec((LANES, SSM_D_INNER), lambda c: (0, 0))],
        out_specs=[pl.BlockSpec((L, SSM_D_INNER), lambda c: (c, 0)),
                   pl.BlockSpec((None, SSM_GROUPS, GROUP_W, SSM_STATE), lambda c: (c, 0, 0, 0))],
        out_shape=[_sds((T, SSM_D_INNER)), _sds((nc, SSM_GROUPS, GROUP_W, SSM_STATE))],
        scratch_shapes=[pltpu.VMEM((SSM_GROUPS, GROUP_W, SSM_STATE), F32)],
        compiler_params=_cp("arbitrary"), name=name)(xact, dtraw, dt_bias, a_log, es)


def _ssd_bwd(xact, dtraw, dt_bias, a_log, d_skip, states, dy, name):
    T = xact.shape[0]
    nc = T // SSM_CHUNK
    L = SSM_CHUNK
    BO, CO = SSM_D_INNER, SSM_D_INNER + SSM_GROUPS * SSM_STATE

    def body(x_ref, dt_ref, bias_ref, al_ref, dsk_ref, es_ref, est_ref, st_ref, dy_ref,
             dxp_ref, ddt_ref, dbias_ref, dal_ref, dd_ref, dstate, qa, qx):
        cc = pl.program_id(0)

        @pl.when(cc == 0)
        def _():
            dstate[...] = jnp.zeros(dstate.shape, F32)

        xb, dt, A, tril, acs = _ssd_common(dt_ref[...], bias_ref[...], al_ref[...])
        acsT = acs.T
        last = acs[L - 1:L, :]
        cd = jnp.exp(last)
        es, est = es_ref[...], est_ref[...]
        dtX = _expand_heads(dt, es)
        EX = _expand_heads(jnp.exp(acs), es)
        decX = _expand_heads(jnp.exp(last - acs), es)
        lane1 = _iota((1, LANES), 1)
        lane = _iota((L, LANES), 1)
        sub = _iota((L, LANES), 0)
        ztot = jnp.zeros((1, LANES), F32)
        wrow = jnp.zeros((L, LANES), F32)
        wcolT = jnp.zeros((LANES, L), F32)
        rows_dec, rows_dd = [], []
        for g in range(SSM_GROUPS):
            gs = slice(g * GROUP_W, (g + 1) * GROUP_W)
            x = x_ref[:, gs]
            B = x_ref[:, BO + g * SSM_STATE:BO + (g + 1) * SSM_STATE]
            C = x_ref[:, CO + g * SSM_STATE:CO + (g + 1) * SSM_STATE]
            dY = dy_ref[:, gs]
            dtx, e_x, dec_x = dtX[:, gs], EX[:, gs], decX[:, gs]
            X = x * dtx
            CB = _nt(C, B)
            S = st_ref[g]
            dS_out = dstate[g]
            dcb_sum = jnp.zeros((L, L), F32)
            dxd = []
            for r in range(GQ):
                h = GQ * g + r
                hs = slice(r * HEAD_DIM, (r + 1) * HEAD_DIM)
                Lm = jnp.exp(jnp.where(tril > 0, acs[:, h:h + 1] - acsT[h:h + 1, :], NEG))
                M = CB * Lm
                dM = _nt(dY[:, hs], X[:, hs])
                dxd.append(_tn(M, dY[:, hs]))
                dcb_sum = dcb_sum + dM * Lm
                Wm = dM * M
                wrow = jnp.where(lane == h, jnp.sum(Wm, axis=1, keepdims=True), wrow)
                wcolT = jnp.where(sub == h, jnp.sum(Wm, axis=0, keepdims=True), wcolT)
            dXd = jnp.concatenate(dxd, axis=1)
            G = _nt(C, S)
            dG = dY * e_x
            dDX = _nt(B, dS_out)
            dX = dXd + dec_x * dDX
            t_dec = dDX * X * dec_x
            qa[:, gs] = dG * G - t_dec
            qx[:, gs] = dX * x
            rows_dec.append(jnp.sum(t_dec, axis=0, keepdims=True))
            rows_dd.append(jnp.sum(dY * x, axis=0, keepdims=True))
            zc = jnp.sum(dS_out * S, axis=1, keepdims=True)
            for r in range(GQ):
                ztot = jnp.where(lane1 == GQ * g + r, jnp.sum(zc[r * HEAD_DIM:(r + 1) * HEAD_DIM], axis=0, keepdims=True), ztot)
            dxp_ref[:, g * PERM_W:g * PERM_W + GROUP_W] = dX * dtx + dY * dsk_ref[:, gs]
            dxp_ref[:, g * PERM_W + GROUP_W:g * PERM_W + GROUP_W + SSM_STATE] = _tn(dcb_sum, C) + _nn(X * dec_x, dS_out)
            dxp_ref[:, g * PERM_W + GROUP_W + SSM_STATE:(g + 1) * PERM_W] = _nn(dcb_sum, B) + _nn(dG, S)
            dstate[g] = dS_out * _per_state_row(cd, g) + _tn(dG, C)
        rows = jnp.concatenate([jnp.concatenate(rows_dec, axis=1), jnp.concatenate(rows_dd, axis=1)]
                               + [jnp.zeros((SUBLANES - 2, SSM_D_INNER), F32)], axis=0)
        rsum = _reduce_heads(rows, est)
        dlast = rsum[0:1, :] + cd * ztot
        dacs = (wrow - wcolT.T) + _reduce_heads(qa[...], est) + jnp.where(sub == L - 1, dlast, 0.0)
        triu = (_iota((L, L), 0) <= _iota((L, L), 1)).astype(F32)
        da = _dot_hi(triu, dacs)
        ddtraw = (da * A + _reduce_heads(qx[...], est)) * (1.0 / (1.0 + jnp.exp(-xb)))
        ddt_ref[...] = ddtraw
        dal = jnp.sum(da * dt, axis=0, keepdims=True) * A
        ddp = rsum[1:2, :]
        dbp = jnp.sum(ddtraw, axis=0, keepdims=True)

        @pl.when(cc == 0)
        def _():
            dbias_ref[...] = dbp
            dal_ref[...] = dal
            dd_ref[...] = ddp

        @pl.when(cc > 0)
        def _():
            dbias_ref[...] += dbp
            dal_ref[...] += dal
            dd_ref[...] += ddp

    rc = lambda c: nc - 1 - c
    vec = pl.BlockSpec((1, LANES), lambda c: (0, 0))
    es, est = _head_selectors()
    return pl.pallas_call(
        body, grid=(nc,),
        in_specs=[pl.BlockSpec((L, SSM_CONV_DIM), lambda c: (rc(c), 0)), pl.BlockSpec((L, LANES), lambda c: (rc(c), 0)), vec, vec,
                  pl.BlockSpec((1, SSM_D_INNER), lambda c: (0, 0)),
                  pl.BlockSpec((LANES, SSM_D_INNER), lambda c: (0, 0)), pl.BlockSpec((SSM_D_INNER, LANES), lambda c: (0, 0)),
                  pl.BlockSpec((None, SSM_GROUPS, GROUP_W, SSM_STATE), lambda c: (rc(c), 0, 0, 0)),
                  pl.BlockSpec((L, SSM_D_INNER), lambda c: (rc(c), 0))],
        out_specs=[pl.BlockSpec((L, SSM_GROUPS * PERM_W), lambda c: (rc(c), 0)),
                   pl.BlockSpec((L, LANES), lambda c: (rc(c), 0)), vec, vec, vec],
        out_shape=[_sds((T, SSM_GROUPS * PERM_W)), _sds((T, LANES)), _sds((1, LANES)), _sds((1, LANES)), _sds((1, LANES))],
        scratch_shapes=[pltpu.VMEM((SSM_GROUPS, GROUP_W, SSM_STATE), F32), pltpu.VMEM((L, SSM_D_INNER), F32),
                        pltpu.VMEM((L, SSM_D_INNER), F32)],
        compiler_params=_cp("arbitrary"), name=name)(xact, dtraw, dt_bias, a_log, d_skip, es, est, states, dy)


def _ssm_post_fwd(y, xact, z, d_skip, nw, name):
    T = y.shape[0]
    tm = min(T, 256)
    W = SSM_D_INNER

    def body(y_ref, x_ref, z_ref, d_ref, w_ref, o_ref):
        y2 = (y_ref[...] + d_ref[...] * x_ref[...]) * _silu(z_ref[...])
        r = lax.rsqrt(jnp.mean(y2 * y2, axis=-1, keepdims=True) + SSM_NORM_EPS)
        o_ref[...] = (y2 * r * w_ref[...]).astype(o_ref.dtype)

    row = pl.BlockSpec((tm, W), lambda i: (i, 0))
    vec = pl.BlockSpec((1, W), lambda i: (0, 0))
    return pl.pallas_call(
        body, grid=(T // tm,), in_specs=[row, row, row, vec, vec], out_specs=row, out_shape=_sds((T, W), MXU),
        compiler_params=_cp("parallel"), name=name)(y, xact, z, d_skip, nw)


def _ssm_post_bwd(y, xact, z, d_skip, nw, dyn, name):
    T = y.shape[0]
    tm = min(T, 256)
    W = SSM_D_INNER

    def body(y_ref, x_ref, z_ref, d_ref, w_ref, dn_ref, dyg_ref, dz_ref, dw_ref):
        zv = z_ref[...]
        sz = _silu(zv)
        yg = y_ref[...] + d_ref[...] * x_ref[...]
        y2 = yg * sz
        r = lax.rsqrt(jnp.mean(y2 * y2, axis=-1, keepdims=True) + SSM_NORM_EPS)
        y2h = y2 * r
        dn = dn_ref[...]
        gy = dn * w_ref[...]
        dy2 = r * (gy - y2h * jnp.mean(gy * y2h, axis=-1, keepdims=True))
        dyg_ref[...] = dy2 * sz
        dz_ref[...] = (dy2 * yg * _dsilu(zv)).astype(dz_ref.dtype)
        part = jnp.sum(dn * y2h, axis=0, keepdims=True)

        @pl.when(pl.program_id(0) == 0)
        def _():
            dw_ref[...] = part

        @pl.when(pl.program_id(0) > 0)
        def _():
            dw_ref[...] += part

    row = pl.BlockSpec((tm, W), lambda i: (i, 0))
    vec = pl.BlockSpec((1, W), lambda i: (0, 0))
    return pl.pallas_call(
        body, grid=(T // tm,), in_specs=[row, row, row, vec, vec, row], out_specs=[row, row, vec],
        out_shape=[_sds((T, W)), _sds((T, W), MXU), _sds((1, W))],
        compiler_params=_cp("arbitrary"), name=name)(y, xact, z, d_skip, nw, dyn)


def _local_step(x0, cos, sin_s, target, P, fetch, token, send):
    mmf = functools.partial(_mm, tm=1024)
    big, small = {}, {}
    P = dict(P, wup={}, wdn={}, fcw={})
    h0 = _rmsnorm_fwd(x0, P["nm"][0], "norm_mix0", token=token)
    proj0 = mmf(h0, P["wmiT"], tb=True, tn=1280, tk=1024, name="mix_in")
    cat, attn, lse = _mixcore_fwd(proj0, cos, sin_s, P["pool_w"], P["pool_scale"], P["sinks"], "mixcore_fwd")
    x1 = mmf(cat, P["wmo"], tn=1024, tk=1024, res=x0, name="mix_out")

    def ffn_fwd(xin, i):
        hf = _rmsnorm_fwd(xin, P["nf"][i], f"norm_ffn{i}")
        got = fetch(f"ffn{i}", hf)
        P["wup"][i], P["wdn"][i], P["fcw"][i] = got["wup"], got["wdn"], got["fcw"]
        hid = mmf(hf, P["wup"][i], tn=1408, tk=1024, name=f"ffn_up{i}")
        act = _ffn_mid_fwd(hid, P["fcw"][i], P["fcb"][i], f"ffn_mid_fwd{i}")
        xout = mmf(act, P["wdn"][i], tn=1024, tk=D_FF, res=xin, name=f"ffn_down{i}")
        return hf, hid, act, xout

    hf0, hid0, act0, x2 = ffn_fwd(x1, 0)
    h1 = _rmsnorm_fwd(x2, P["nm"][1], "norm_mix1")
    P.update(fetch("ssm", h1))
    z = mmf(h1, P["wzT"], tb=True, tn=1024, tk=1024, name="ssm_in_z")
    xbc = mmf(h1, P["wxbcT"], tb=True, tn=1024, tk=1024, name="ssm_in_xbc")
    dtraw = mmf(h1, P["wdtT"], tb=True, tn=128, tk=1024, name="ssm_in_dt")
    xact = _ssm_pre_fwd(xbc, P["scw"], P["scb"], "ssm_pre_fwd")
    y, states = _ssd_fwd(xact, dtraw, P["dt_bias"], P["a_log"], "ssd_fwd")
    yn = _ssm_post_fwd(y, xact, z, P["d_exp"], P["snorm"], "ssm_post_fwd")
    x3 = mmf(yn, P["wso"], tn=1024, tk=SSM_D_INNER, res=x2, name="ssm_out")
    hf1, hid1, act1, x4 = ffn_fwd(x3, 1)
    loss_row, dx4, d_nfin = _loss_head(x4, P["nfin"], target, "loss_head")
    small["norm_final"] = d_nfin

    def ffn_bwd(xin, dxo, hf, hid, act, i):
        da = mmf(dxo, P["wdn"][i], tb=True, tn=1408, tk=1024, name=f"ffn_down_dx{i}")
        big[f"ffn_w_down{i}"] = dwf(act, dxo, tm=1408, tn=1024, name=f"ffn_down_dw{i}").reshape(N_CHIPS, D_FF // N_CHIPS, D_MODEL)
        dhid, dcw, dcb = _ffn_mid_bwd(hid, P["fcw"][i], P["fcb"][i], da, f"ffn_mid_bwd{i}")
        dhf = mmf(dhid, P["wup"][i], tb=True, tn=1024, tk=1408, name=f"ffn_up_dx{i}")
        big[f"ffn_w_up{i}"] = dwf(hf, dhid, tm=1024, tn=1408, out_shard_perm=(0, 2, 1, 3), name=f"ffn_up_dw{i}")
        tok = send(f"ffn{i}", [big[f"ffn_w_up{i}"], big[f"ffn_w_down{i}"]])
        dxi, dnf = _rmsnorm_bwd(xin, P["nf"][i], dhf, dxo, f"norm_ffn_bwd{i}", token=tok)
        return dxi, dnf, dcw, dcb

    dwf = functools.partial(_mm, ta=True, tk=1024, out_dtype=BF16)
    dx3, dnf1, dfcw1, dfcb1 = ffn_bwd(x3, dx4, hf1, hid1, act1, 1)
    dyn = mmf(dx3, P["wso"], tb=True, tn=1024, tk=1024, name="ssm_out_dx")
    big["ssm_w_out"] = dwf(yn, dx3, tm=1024, tn=1024, name="ssm_out_dw").reshape(N_CHIPS, SSM_D_INNER // N_CHIPS, D_MODEL)
    dyg, dz, d_snorm = _ssm_post_bwd(y, xact, z, P["d_exp"], P["snorm"], dyn, "ssm_post_bwd")
    dxact_p, ddtraw, d_dtb, d_alog, d_dskip = _ssd_bwd(xact, dtraw, P["dt_bias"], P["a_log"], P["d_exp"], states, dyg, "ssd_bwd")
    dxbc, d_scw, d_scb = _ssm_pre_bwd(xbc, P["scw"], P["scb"], dxact_p, "ssm_pre_bwd")
    dh1 = mmf(dz, P["wzT"], tn=1024, tk=1024, name="ssm_in_dx_z")
    dh1 = mmf(dxbc, P["wxbcT"], tn=1024, tk=1024, res=dh1, name="ssm_in_dx_xbc")
    dh1 = mmf(ddtraw, P["wdtT"], tn=1024, tk=128, res=dh1, name="ssm_in_dx_dt")
    dwz = dwf(dz, h1, tm=1024, tn=1024, name="ssm_in_dw_z")
    dwxbc = dwf(dxbc, h1, tm=1024, tn=1024, name="ssm_in_dw_xbc")
    dwdt = dwf(ddtraw, h1, tm=128, tn=1024, name="ssm_in_dw_dt")
    dwsi = jnp.concatenate([dwz, dwxbc, dwdt[:SSM_HEADS]], axis=0)
    big["ssm_w_in"] = dwsi.reshape(N_CHIPS, SSM_IN_DIM // N_CHIPS, D_MODEL)
    tok = send("ssm", [big["ssm_w_in"], big["ssm_w_out"]])
    dx2, dnm1 = _rmsnorm_bwd(x2, P["nm"][1], dh1, dx3, "norm_mix_bwd1", token=tok)
    dx1, dnf0, dfcw0, dfcb0 = ffn_bwd(x1, dx2, hf0, hid0, act0, 0)
    dcat = mmf(dx1, P["wmo"], tb=True, tn=1024, tk=1024, name="mix_out_dx")
    big["mix_w_out"] = dwf(cat, dx1, tm=1024, tn=1024, name="mix_out_dw").reshape(N_CHIPS, D_MODEL // N_CHIPS, D_MODEL)
    dproj0, d_pw, d_ps, d_sk = _mixcore_bwd(proj0, cos, sin_s, P["pool_w"], P["pool_scale"], P["sinks"], attn, lse, dcat, "mixcore_bwd")
    dh0 = mmf(dproj0, P["wmiT"], tn=1024, tk=1280, name="mix_in_dx")
    big["mix_w_in"] = dwf(dproj0, h0, tm=1280, tn=1024, name="mix_in_dw").reshape(N_CHIPS, MIX_IN_DIM // N_CHIPS, D_MODEL)
    tok = send("mix", [big["mix_w_in"], big["mix_w_out"]])
    dx0, dnm0 = _rmsnorm_bwd(x0, P["nm"][0], dh0, dx1, "norm_mix_bwd0", token=tok)

    def unperm_cols(a):
        r = a.shape[0]
        t = a.reshape(r, N_CHIPS, FFN_TC)
        return jnp.stack([t[:, p] for p in _PERM], axis=0)

    small["norm_mix"] = jnp.concatenate([dnm0, dnm1], axis=0)
    small["norm_ffn"] = jnp.concatenate([dnf0, dnf1], axis=0)
    small["pool_w"] = d_pw.reshape(4 * POOL_GROUP, POOL_GROUP)
    small["pool_scale"] = d_ps
    small["attn_sinks"] = d_sk
    small["ssm_dt_bias"] = d_dtb
    small["ssm_A_log"] = d_alog
    small["ssm_D"] = d_dskip
    fcb = jnp.stack([unperm_cols(dfcb0), unperm_cols(dfcb1)], axis=0)
    small["ffn_conv_b"] = fcb.reshape(2, 2 * D_FF)
    small["ssm_conv_w"] = d_scw.reshape(SSM_CONV, N_CHIPS, SSM_CONV_DIM // N_CHIPS).transpose(1, 0, 2)
    small["ssm_conv_b"] = d_scb.reshape(N_CHIPS, 1, SSM_CONV_DIM // N_CHIPS)
    small["ssm_norm"] = d_snorm.reshape(N_CHIPS, 1, SSM_D_INNER // N_CHIPS)
    small["ffn_conv_w"] = jnp.concatenate([unperm_cols(dfcw0), unperm_cols(dfcw1)], axis=1)
    return loss_row, dx0, big, small


ANY = pl.BlockSpec(memory_space=pl.ANY)


def _place():
    return lax.axis_index("x"), lax.axis_index("y"), lax.axis_index("c")


def _gather_shards(shards, name):
    n = len(shards)
    split = [s.size >= (1 << 16) for s in shards]

    def half(ref, a, h):
        shp = shards[a].shape
        if len(shp) == 3:
            return ref.at[h]
        r2 = shp[0] // 2
        return ref.at[pl.ds(pl.multiple_of(h * r2, 2 * SUBLANES), r2), :]

    def body(*refs):
        ins, outs = refs[:n], refs[n:2 * n]
        send, recv, fsend, frecv = refs[2 * n:]
        x, y, c = _place()
        k = 2 * x + y
        chips = [(1 - x, y), (x, 1 - y), (1 - x, 1 - y)]

        def ici(a, j, src_slot_ref, dst_slot):
            px, py = chips[j]
            src = half(src_slot_ref, a, c) if split[a] else src_slot_ref
            dst = half(outs[a].at[dst_slot], a, c) if split[a] else outs[a].at[dst_slot]
            return pltpu.make_async_remote_copy(src, dst, send.at[a, j], recv.at[a, j], device_id=(px, py, c), device_id_type=MESH)

        def d2d(a, j, h):
            px, py = chips[j]
            part = half(outs[a].at[2 * px + py], a, h)
            return pltpu.make_async_remote_copy(part, part, fsend.at[a, j], frecv.at[a, j], device_id=(x, y, 1 - c), device_id_type=MESH)

        sends = [ici(a, j, ins[a], k) for a in range(n) for j in range(3)]
        for cp in sends:
            cp.start()
        passed = []
        for a in range(n):
            for j, (px, py) in enumerate(chips):
                ici(a, j, ins[a], 2 * px + py).wait_recv()
                if split[a]:
                    passed.append(d2d(a, j, c))
                    passed[-1].start()
        for a in range(n):
            if split[a]:
                for j in range(3):
                    d2d(a, j, 1 - c).wait_recv()
        for cp in sends + passed:
            cp.wait_send()

    return pl.pallas_call(
        body, in_specs=[ANY] * n, out_specs=[ANY] * n,
        out_shape=[_sds((N_CHIPS,) + s.shape, s.dtype) for s in shards],
        scratch_shapes=[pltpu.SemaphoreType.DMA((n, 3))] * 4,
        compiler_params=pltpu.CompilerParams(has_side_effects=True), name=name)(*shards)


HBM = pl.BlockSpec(memory_space=pltpu.HBM)
SEM = pl.BlockSpec(memory_space=pltpu.SEMAPHORE)
DATAFLOW = pltpu.SideEffectType.DATAFLOW_SIDE_EFFECTING


def _spread_start(groups, slot_src, after, name):
    flat = [a for grp in groups for a in grp]
    n = len(flat)
    ng = len(groups)
    offs = [sum(len(g) for g in groups[:i]) for i in range(ng)]
    lshape = [(a.shape if slot_src else (N_CHIPS,) + a.shape) for a in flat]

    nsem = 6 * n

    def body(*refs):
        src, land = refs[:n], refs[n:2 * n]
        sems = refs[2 * n + 1:2 * n + 1 + nsem]
        token = refs[-1]
        x, y, c = _place()
        k = 2 * x + y
        chips = [(1 - x, y), (x, 1 - y), (1 - x, 1 - y)]
        for a in range(n):
            for j, (px, py) in enumerate(chips):
                s = src[a].at[2 * px + py] if slot_src else src[a]
                pltpu.make_async_remote_copy(s, land[a].at[k], sems[6 * a + 2 * j], sems[6 * a + 2 * j + 1],
                                             device_id=(px, py, c), device_id_type=MESH).start()
        token[...] = jnp.zeros(token.shape, token.dtype)

    out_shape = [pltpu.SemaphoreType.DMA(())] * nsem
    out_shape += [pltpu.HBM(a.shape, a.dtype) for a in flat] + [pltpu.HBM(s, a.dtype) for s, a in zip(lshape, flat)]
    out_shape.append(_sds((SUBLANES, LANES)))
    args = [pltpu.with_memory_space_constraint(a, pltpu.HBM) for a in flat]
    args += [pltpu.with_memory_space_constraint(lax.empty(s, a.dtype), pltpu.HBM) for s, a in zip(lshape, flat)]
    res = pl.pallas_call(
        body, name=name, out_shape=tuple(out_shape), in_specs=[HBM] * (2 * n) + [pl.BlockSpec(memory_space=pl.ANY)],
        out_specs=tuple([SEM] * nsem + [HBM] * (2 * n) + [pl.BlockSpec(memory_space=pltpu.VMEM)]),
        input_output_aliases={i: nsem + i for i in range(2 * n)},
        compiler_params=pltpu.CompilerParams(has_side_effects=DATAFLOW))(*args, after)
    sems, thru, token = res[:nsem], res[nsem:nsem + 2 * n], res[-1]
    out = []
    for gi, grp in enumerate(groups):
        sl = slice(offs[gi], offs[gi] + len(grp))
        out.append((list(sems[6 * offs[gi]:6 * (offs[gi] + len(grp))]), list(thru[:n][sl]), list(thru[n:][sl])))
    return out, token


def _spread_wait(started, slot_src, after, name):
    sems, srcs, lands = started
    n = len(srcs)

    def body(*refs):
        src, land = refs[:n], refs[n:2 * n]
        sem = refs[2 * n:2 * n + 6 * n]
        x, y, c = _place()
        chips = [(1 - x, y), (x, 1 - y), (1 - x, 1 - y)]
        for a in range(n):
            for j, (px, py) in enumerate(chips):
                s = src[a].at[2 * px + py] if slot_src else src[a]
                cp = pltpu.make_async_remote_copy(s, land[a].at[2 * px + py], sem[6 * a + 2 * j], sem[6 * a + 2 * j + 1],
                                                  device_id=(px, py, c), device_id_type=MESH)
                cp.wait_send()
                cp.wait_recv()

    res = pl.pallas_call(
        body, name=name, out_shape=tuple([pltpu.HBM(a.shape, a.dtype) for a in srcs] + [pltpu.HBM(a.shape, a.dtype) for a in lands]),
        in_specs=[HBM] * (2 * n) + [SEM] * (6 * n) + [pl.BlockSpec(memory_space=pl.ANY)], out_specs=tuple([HBM] * (2 * n)),
        input_output_aliases={i: i for i in range(2 * n)},
        compiler_params=pltpu.CompilerParams(has_side_effects=DATAFLOW))(*srcs, *lands, *sems, after)
    return list(res[:n]), list(res[n:])


def _sibling_exchange(fs, name):
    n = len(fs)

    def body(*refs):
        ins, outs = refs[:n], refs[n:2 * n]
        send, recv = refs[2 * n:]
        x, y, c = _place()
        cps = [pltpu.make_async_remote_copy(ins[a], outs[a], send.at[a], recv.at[a],
                                            device_id=(x, y, 1 - c), device_id_type=MESH) for a in range(n)]
        for cp in cps:
            cp.start()
        for cp in cps:
            cp.wait()

    return pl.pallas_call(
        body, in_specs=[ANY] * n, out_specs=[ANY] * n, out_shape=[_sds(f.shape, f.dtype) for f in fs],
        scratch_shapes=[pltpu.SemaphoreType.DMA((n,)), pltpu.SemaphoreType.DMA((n,))],
        compiler_params=pltpu.CompilerParams(has_side_effects=True), name=name)(*fs)


def _tile2d(rows, cols, budget=1024 * 1024, step=2 * SUBLANES):
    fits = [t for t in range(step, rows + 1, step) if rows % t == 0 and t * cols * 4 <= budget]
    if fits:
        return fits[-1], cols
    fits = [t for t in range(LANES, cols + 1, LANES) if cols % t == 0 and rows * t * 4 <= budget]
    assert fits, (rows, cols)
    return rows, fits[-1]


def _chip_sum(own, parts, kidx, name):
    _, R, C = parts.shape
    tr, tc = _tile2d(R, C)

    def body(k_ref, o_ref_in, p1_ref, p2_ref, p3_ref, o_ref):
        o_ref[...] = ((o_ref_in[...].astype(F32) + p1_ref[...].astype(F32)) + p2_ref[...].astype(F32)) + p3_ref[...].astype(F32)

    def slot(d):
        return pl.BlockSpec((None, tr, tc), lambda i, j, k: ((k[0] + d) % N_CHIPS, i, j))

    return pl.pallas_call(
        body,
        grid_spec=pltpu.PrefetchScalarGridSpec(
            num_scalar_prefetch=1, grid=(R // tr, C // tc), in_specs=[slot(0), slot(1), slot(2), slot(3)],
            out_specs=pl.BlockSpec((tr, tc), lambda i, j, k: (i, j))),
        out_shape=_sds((R, C)), compiler_params=_cp("parallel", "parallel"), name=name)(kidx, own, parts, parts, parts)


def _adamw_math(w, g, m, v):
    m2 = ADAM_B1 * m + (1.0 - ADAM_B1) * g
    v2 = ADAM_B2 * v + (1.0 - ADAM_B2) * (g * g)
    m_hat = m2 / (1.0 - ADAM_B1 ** ADAM_STEP)
    v_hat = v2 / (1.0 - ADAM_B2 ** ADAM_STEP)
    delta = -ADAM_LR * (m_hat / (jnp.sqrt(v_hat) + ADAM_EPS) + ADAM_WD * w)
    return delta, m2, v2


def _adamw(w, m, v, gparts, name):
    Lw, R, C = w.shape
    tr, tc = _tile2d(R, C)
    flat = [h for pair in gparts for h in pair]

    def body(*refs):
        w_ref, m_ref, v_ref = refs[:3]
        g_refs = refs[3:3 + 2 * Lw]
        go_ref, d_ref, mo_ref, vo_ref = refs[3 + 2 * Lw:]
        g = g_refs[0][...] + g_refs[1][...]
        for l in range(1, Lw):
            g = jnp.where(pl.program_id(0) == l, g_refs[2 * l][...] + g_refs[2 * l + 1][...], g)
        d, m2, v2 = _adamw_math(w_ref[...], g, m_ref[...], v_ref[...])
        go_ref[...] = g
        d_ref[...] = d
        mo_ref[...] = m2
        vo_ref[...] = v2

    blk = pl.BlockSpec((None, tr, tc), lambda l, i, j: (l, i, j))
    gblk = pl.BlockSpec((tr, tc), lambda l, i, j: (i, j))
    return pl.pallas_call(
        body, grid=(Lw, R // tr, C // tc), in_specs=[blk, blk, blk] + [gblk] * (2 * Lw), out_specs=[blk] * 4,
        out_shape=[_sds((Lw, R, C))] * 4, compiler_params=_cp("parallel", "parallel", "parallel"), name=name)(w, m, v, *flat)


def _small_adamw(grads, wmv, name):
    n = len(grads)

    def body(*refs):
        g_in, p_in, outs = refs[:n], refs[n:4 * n], refs[4 * n:]
        for a in range(n):
            g = g_in[a][...]
            d_, m2, v2 = _adamw_math(p_in[3 * a][...], g, p_in[3 * a + 1][...], p_in[3 * a + 2][...])
            outs[4 * a][...] = g
            outs[4 * a + 1][...] = d_
            outs[4 * a + 2][...] = m2
            outs[4 * a + 3][...] = v2

    vm = pl.BlockSpec(memory_space=pltpu.VMEM)
    args = list(grads) + [t for tri in wmv for t in tri]
    out_shape = [_sds(g.shape) for g in grads for _ in range(4)]
    return pl.pallas_call(body, in_specs=[vm] * len(args), out_specs=[vm] * len(out_shape), out_shape=out_shape,
                          compiler_params=pltpu.CompilerParams(vmem_limit_bytes=V7X_VMEM_LIMIT), name=name)(*args)


def _small_allreduce(partials, pshapes, loss_row, name):
    n = len(partials)
    gshapes = [p.shape for p in partials] + [loss_row.shape]
    ng = n + 1

    def body(*refs):
        g_in = refs[:ng]
        outs = refs[ng:2 * ng]
        bufs = refs[2 * ng:3 * ng]
        send, recv = refs[-2:]
        x, y, c = _place()
        me = 4 * x + 2 * y + c
        k = 2 * x + y
        flips = [(fx, fy, fc) for fx in (0, 1) for fy in (0, 1) for fc in (0, 1)][1:]

        def peer(f):
            return (x ^ f[0], y ^ f[1], c ^ f[2])

        def slot(p):
            return 4 * p[0] + 2 * p[1] + p[2]

        for a in range(ng):
            bufs[a][me] = g_in[a][...]
        sends = [pltpu.make_async_remote_copy(g_in[a], bufs[a].at[me], send.at[a, j], recv.at[a, j],
                                              device_id=peer(f), device_id_type=MESH)
                 for a in range(ng) for j, f in enumerate(flips)]
        for cp in sends:
            cp.start()
        for a in range(ng):
            for j, f in enumerate(flips):
                pltpu.make_async_remote_copy(g_in[a], bufs[a].at[slot(peer(f))], send.at[a, j], recv.at[a, j],
                                             device_id=peer(f), device_id_type=MESH).wait_recv()
        for cp in sends:
            cp.wait_send()
        for a in range(ng):
            sharded = len(gshapes[a]) == 3

            def part(d):
                return bufs[a][d, k] if sharded else bufs[a][d]

            tot = part(0)
            for d in range(1, N_DEV):
                tot = tot + part(d)
            if a == n:
                outs[n][...] = tot
            else:
                pr, pc = pshapes[a]
                outs[a][...] = tot[:pr, :pc]

    vm = pl.BlockSpec(memory_space=pltpu.VMEM)
    args = list(partials) + [loss_row]
    out_shape = [_sds(ps) for ps in pshapes] + [_sds(loss_row.shape)]
    return pl.pallas_call(
        body, in_specs=[vm] * len(args), out_specs=[vm] * len(out_shape), out_shape=out_shape,
        scratch_shapes=[pltpu.VMEM((N_DEV,) + tuple(s), F32) for s in gshapes]
        + [pltpu.SemaphoreType.DMA((ng, N_DEV - 1)), pltpu.SemaphoreType.DMA((ng, N_DEV - 1))],
        compiler_params=pltpu.CompilerParams(has_side_effects=True, vmem_limit_bytes=V7X_VMEM_LIMIT), name=name)(*args)


_PERM = (0, 2, 1, 3)


def _cols_from_shards(g):
    return g.transpose(1, 0, 2).reshape(g.shape[1], N_CHIPS * g.shape[2])


def _rope_tables(positions):
    inv_freq = ROPE_THETA ** (-jnp.arange(0, HEAD_DIM, 2, dtype=F32) / HEAD_DIM)
    ang = positions.astype(F32).reshape(-1, 1) * inv_freq
    cos, sin = jnp.cos(ang), jnp.sin(ang)
    cos = jnp.concatenate([cos, cos, cos, cos], axis=-1)
    sin_s = jnp.concatenate([-sin, sin, -sin, sin], axis=-1)
    return cos, sin_s


def kernel(x, positions, norm_mix, norm_ffn, norm_final, mix_w_in, pool_w, pool_scale, attn_sinks, mix_w_out, ssm_w_in, ssm_conv_w, ssm_conv_b, ssm_dt_bias, ssm_A_log, ssm_D, ssm_norm, ssm_w_out, ffn_w_up, ffn_conv_w, ffn_conv_b, ffn_w_down, loss_target, m_norm_mix, m_norm_ffn, m_norm_final, m_mix_w_in, m_pool_w, m_pool_scale, m_attn_sinks, m_mix_w_out, m_ssm_w_in, m_ssm_conv_w, m_ssm_conv_b, m_ssm_dt_bias, m_ssm_A_log, m_ssm_D, m_ssm_norm, m_ssm_w_out, m_ffn_w_up, m_ffn_conv_w, m_ffn_conv_b, m_ffn_w_down, v_norm_mix, v_norm_ffn, v_norm_final, v_mix_w_in, v_pool_w, v_pool_scale, v_attn_sinks, v_mix_w_out, v_ssm_w_in, v_ssm_conv_w, v_ssm_conv_b, v_ssm_dt_bias, v_ssm_A_log, v_ssm_D, v_ssm_norm, v_ssm_w_out, v_ffn_w_up, v_ffn_conv_w, v_ffn_conv_b, v_ffn_w_down):
    W = dict(norm_mix=norm_mix, norm_ffn=norm_ffn, norm_final=norm_final, mix_w_in=mix_w_in, pool_w=pool_w, pool_scale=pool_scale, attn_sinks=attn_sinks, mix_w_out=mix_w_out, ssm_w_in=ssm_w_in, ssm_conv_w=ssm_conv_w, ssm_conv_b=ssm_conv_b, ssm_dt_bias=ssm_dt_bias, ssm_A_log=ssm_A_log, ssm_D=ssm_D, ssm_norm=ssm_norm, ssm_w_out=ssm_w_out, ffn_w_up=ffn_w_up, ffn_conv_w=ffn_conv_w, ffn_conv_b=ffn_conv_b, ffn_w_down=ffn_w_down)
    Mo = dict(norm_mix=m_norm_mix, norm_ffn=m_norm_ffn, norm_final=m_norm_final, mix_w_in=m_mix_w_in, pool_w=m_pool_w, pool_scale=m_pool_scale, attn_sinks=m_attn_sinks, mix_w_out=m_mix_w_out, ssm_w_in=m_ssm_w_in, ssm_conv_w=m_ssm_conv_w, ssm_conv_b=m_ssm_conv_b, ssm_dt_bias=m_ssm_dt_bias, ssm_A_log=m_ssm_A_log, ssm_D=m_ssm_D, ssm_norm=m_ssm_norm, ssm_w_out=m_ssm_w_out, ffn_w_up=m_ffn_w_up, ffn_conv_w=m_ffn_conv_w, ffn_conv_b=m_ffn_conv_b, ffn_w_down=m_ffn_w_down)
    Vo = dict(norm_mix=v_norm_mix, norm_ffn=v_norm_ffn, norm_final=v_norm_final, mix_w_in=v_mix_w_in, pool_w=v_pool_w, pool_scale=v_pool_scale, attn_sinks=v_attn_sinks, mix_w_out=v_mix_w_out, ssm_w_in=v_ssm_w_in, ssm_conv_w=v_ssm_conv_w, ssm_conv_b=v_ssm_conv_b, ssm_dt_bias=v_ssm_dt_bias, ssm_A_log=v_ssm_A_log, ssm_D=v_ssm_D, ssm_norm=v_ssm_norm, ssm_w_out=v_ssm_w_out, ffn_w_up=v_ffn_w_up, ffn_conv_w=v_ffn_conv_w, ffn_conv_b=v_ffn_conv_b, ffn_w_down=v_ffn_w_down)

    kchip = 2 * lax.axis_index("x") + lax.axis_index("y")

    def own_slot(g, own):
        return lax.dynamic_update_slice_in_dim(g, own[None], kchip, axis=0)

    def tr(t):
        return jnp.swapaxes(t[0], 0, 1)

    later = dict(ffn0=[ffn_w_up[0].astype(MXU), ffn_w_down[0].astype(MXU)],
                 ssm=[tr(ssm_w_in).astype(MXU), ssm_w_out[0].astype(MXU)],
                 ffn1=[ffn_w_up[1].astype(MXU), ffn_w_down[1].astype(MXU)])
    sh = [tr(mix_w_in).astype(MXU), mix_w_out[0].astype(MXU), ssm_conv_w[0], ssm_conv_b, ssm_norm, ffn_conv_w]
    first = _gather_shards(sh, "gather_first")
    g_mi, g_mo, g_scw, g_scb, g_sn, g_fcw = [own_slot(g, own) for g, own in zip(first, sh)]
    started, token = _spread_start(list(later.values()), False, first[0], "gather_start")
    started = dict(zip(later.keys(), started))
    fcw = [jnp.concatenate([g_fcw[p, i] for p in _PERM], axis=1) for i in range(2)]
    P = dict(
        nm=norm_mix, nf=norm_ffn, nfin=norm_final,
        wmiT=g_mi.reshape(MIX_IN_DIM, D_MODEL), wmo=g_mo.reshape(D_MODEL, D_MODEL),
        pool_w=pool_w[0], pool_scale=pool_scale, sinks=attn_sinks[0],
        scw=_cols_from_shards(g_scw), scb=g_scb.reshape(1, SSM_CONV_DIM), snorm=g_sn.reshape(1, SSM_D_INNER),
        dt_bias=jnp.pad(ssm_dt_bias, ((0, 0), (0, LANES - SSM_HEADS))), a_log=jnp.pad(ssm_A_log, ((0, 0), (0, LANES - SSM_HEADS))),
        d_exp=jnp.repeat(ssm_D, SSM_D_INNER // SSM_HEADS, axis=1),
        fcb=[jnp.concatenate([ffn_conv_b[i:i + 1, p * FFN_TC:(p + 1) * FFN_TC] for p in _PERM], axis=1) for i in range(2)],
    )

    def fetch(group, after):
        owns, lands = _spread_wait(started[group], False, after, f"gather_wait_{group}")
        a, b = [own_slot(g, own) for g, own in zip(lands, owns)]
        if group == "ssm":
            wsi = a.reshape(SSM_IN_DIM, D_MODEL)
            zx = SSM_D_INNER + SSM_CONV_DIM
            return dict(wzT=wsi[:SSM_D_INNER], wxbcT=wsi[SSM_D_INNER:zx],
                        wdtT=jnp.pad(wsi[zx:], ((0, LANES - SSM_HEADS), (0, 0))), wso=b.reshape(SSM_D_INNER, D_MODEL))
        i = int(group[-1])
        return dict(wup=jnp.concatenate([a[p] for p in _PERM], axis=1), wdn=b.reshape(D_FF, D_MODEL), fcw=fcw[i])

    cos, sin_s = _rope_tables(positions)
    sent = {}

    def send(group, grads):
        res, tok = _spread_start([grads], True, jnp.zeros((SUBLANES, LANES), F32), f"grad_start_{group}")
        sent[group] = res[0]
        return tok

    loss_row, grad_x, big, small = _local_step(x[0], cos, sin_s, loss_target[0], P, fetch, token, send)

    kidx = kchip.astype(jnp.int32).reshape(1)
    group_names = dict(ffn1=["ffn_w_up1", "ffn_w_down1"], ssm=["ssm_w_in", "ssm_w_out"], ffn0=["ffn_w_up0", "ffn_w_down0"],
                       mix=["mix_w_in", "mix_w_out"])
    names, mine = [], []
    for group, started_g in sent.items():
        grads, lands = _spread_wait(started_g, True, grad_x, f"grad_wait_{group}")
        for nm, g, land in zip(group_names[group], grads, lands):
            names.append(nm)
            mine.append(_chip_sum(g, land, kidx, f"chip_sum_{nm}"))
    theirs = _sibling_exchange(mine, "sibling_exchange")
    red = {nm: (a, b) for nm, a, b in zip(names, mine, theirs)}

    out = {}

    def big_update(pname, gparts, transposed=False):
        w = W[pname]
        lw = len(gparts)
        shp = w.shape
        rr, cc = gparts[0][0].shape
        fix = (lambda t: tr(t)[None]) if transposed else (lambda t: t.reshape(lw, rr, cc))
        res = _adamw(fix(w), fix(Mo[pname]), fix(Vo[pname]), gparts, f"adamw_{pname}")
        out[pname] = tuple((tr(r)[None] if transposed else r.reshape(shp)) for r in res)

    big_update("mix_w_in", [red["mix_w_in"]], transposed=True)
    big_update("mix_w_out", [red["mix_w_out"]])
    big_update("ssm_w_in", [red["ssm_w_in"]], transposed=True)
    big_update("ssm_w_out", [red["ssm_w_out"]])
    big_update("ffn_w_up", [red["ffn_w_up0"], red["ffn_w_up1"]])
    big_update("ffn_w_down", [red["ffn_w_down0"], red["ffn_w_down1"]])

    small_names = ["norm_mix", "norm_ffn", "norm_final", "pool_w", "pool_scale", "attn_sinks", "ssm_dt_bias", "ssm_A_log",
                   "ssm_D", "ffn_conv_b", "ssm_conv_w", "ssm_conv_b", "ssm_norm", "ffn_conv_w"]

    def as2d(t):
        if t.ndim == 1:
            return t.reshape(1, -1)
        return t.reshape(-1, t.shape[-1])

    wmv = [(as2d(W[nm]), as2d(Mo[nm]), as2d(Vo[nm])) for nm in small_names]
    summed = _small_allreduce([small[nm] for nm in small_names], [t[0].shape for t in wmv], loss_row, "small_allreduce")
    res = _small_adamw(summed[:-1], wmv, "small_adamw")
    for a, nm in enumerate(small_names):
        out[nm] = tuple(r.reshape(W[nm].shape) for r in res[4 * a:4 * a + 4])
    loss = summed[-1][0, 0]

    order = ["norm_mix", "norm_ffn", "norm_final", "mix_w_in", "pool_w", "pool_scale", "attn_sinks", "mix_w_out", "ssm_w_in",
             "ssm_conv_w", "ssm_conv_b", "ssm_dt_bias", "ssm_A_log", "ssm_D", "ssm_norm", "ssm_w_out", "ffn_w_up", "ffn_conv_w",
             "ffn_conv_b", "ffn_w_down"]
    return (loss, grad_x.reshape(x.shape), *[out[nm][0] for nm in order], *[out[nm][1] for nm in order],
            *[out[nm][2] for nm in order], *[out[nm][3] for nm in order])
```

```python
import functools

import jax
import jax.numpy as jnp
from jax import lax
from jax.experimental import pallas as pl
from jax.experimental.pallas import tpu as pltpu

F32 = jnp.float32
BF16 = jnp.bfloat16
MXU = BF16
HI = lax.Precision.HIGHEST

D_MODEL = 1024
POOL_WINDOWS = (2, 4, 8, 16)
POOL_DIM = 512
POOL_GROUP = 128
HEAD_DIM = 64
N_HEADS = 8
N_KV_HEADS = 2
GQ = 4
Q_DIM = 512
KV_DIM = 128
BLOCK = 128
ROPE_THETA = 10000.0
MIX_IN_DIM = 1280
SSM_D_INNER = 2048
SSM_HEADS = 32
SSM_GROUPS = 8
SSM_STATE = 128
SSM_CONV = 4
SSM_CHUNK = 128
SSM_CONV_DIM = 4096
SSM_IN_DIM = 6176
D_FF = 2816
FFN_CONV = 3
NORM_EPS = 1e-6
SSM_NORM_EPS = 1e-5
ADAM_LR = 0.001
ADAM_B1 = 0.9
ADAM_B2 = 0.999
ADAM_EPS = 1e-08
ADAM_WD = 0.01
ADAM_STEP = 10

N_CHIPS = 4
N_DEV = 8
LANES = 128
SUBLANES = 8
V7X_VMEM_LIMIT = 56 * 1024 * 1024
NEG = -1e30
MESH = pl.DeviceIdType.MESH


def _cp(*sem):
    return pltpu.CompilerParams(dimension_semantics=sem if sem else None, vmem_limit_bytes=V7X_VMEM_LIMIT)


def _sds(shape, dtype=F32):
    return jax.ShapeDtypeStruct(tuple(shape), dtype)


def _iota(shape, dim):
    return lax.broadcasted_iota(jnp.int32, shape, dim)


def _silu(x):
    return x * (1.0 / (1.0 + jnp.exp(-x)))


def _dsilu(x):
    s = 1.0 / (1.0 + jnp.exp(-x))
    return s * (1.0 + x * (1.0 - s))


def _mm(a, b, *, ta=False, tb=False, tm, tn, tk, res=None, out_dtype=F32, out_shard_perm=None, name):
    M, K = (a.shape[1], a.shape[0]) if ta else a.shape
    N = b.shape[0] if tb else b.shape[1]
    tm, tn, tk = min(tm, M), min(tn, N), min(tk, K)
    gm, gn, gk = M // tm, N // tn, K // tk
    assert gm * tm == M and gn * tn == N and gk * tk == K, (name, M, N, K, tm, tn, tk)
    a_spec = pl.BlockSpec((tk, tm), lambda i, j, k: (k, i)) if ta else pl.BlockSpec((tm, tk), lambda i, j, k: (i, k))
    b_spec = pl.BlockSpec((tn, tk), lambda i, j, k: (j, k)) if tb else pl.BlockSpec((tk, tn), lambda i, j, k: (k, j))
    dims = (((0 if ta else 1,), (1 if tb else 0,)), ((), ()))
    has_res = res is not None

    def body(*refs):
        a_ref, b_ref = refs[0], refs[1]
        r_ref = refs[2] if has_res else None
        o_ref = refs[3] if has_res else refs[2]
        p = lax.dot_general(a_ref[...].astype(MXU), b_ref[...].astype(MXU), dims, preferred_element_type=F32)
        if gk == 1:
            if has_res:
                p = p + r_ref[...]
            o_ref[...] = p.astype(out_dtype)
        else:
            acc = refs[-1]
            k = pl.program_id(2)

            @pl.when(k == 0)
            def _():
                acc[...] = p

            @pl.when(k > 0)
            def _():
                acc[...] += p

            @pl.when(k == gk - 1)
            def _():
                r = acc[...]
                if has_res:
                    r = r + r_ref[...]
                o_ref[...] = r.astype(out_dtype)

    in_specs = [a_spec, b_spec]
    args = [a, b]
    if has_res:
        in_specs.append(pl.BlockSpec((tm, tn), lambda i, j, k: (i, j)))
        args.append(res)
    if out_shard_perm is None:
        out_spec = pl.BlockSpec((tm, tn), lambda i, j, k: (i, j))
        out_shape = _sds((M, N), out_dtype)
    else:
        assert gn == len(out_shard_perm) == 4 and tuple(out_shard_perm) == (0, 2, 1, 3)
        out_spec = pl.BlockSpec((None, tm, tn), lambda i, j, k: ((j % 2) * 2 + j // 2, i, 0))
        out_shape = _sds((gn, M, tn), out_dtype)
    return pl.pallas_call(
        body, grid=(gm, gn, gk), in_specs=in_specs, out_specs=out_spec, out_shape=out_shape,
        scratch_shapes=[pltpu.VMEM((tm, tn), F32)] if gk > 1 else [],
        compiler_params=_cp("parallel", "parallel", "arbitrary"), name=name)(*args)


def _rmsnorm_fwd(x, w, name, token=None):
    T, D = x.shape
    tm = min(T, 512)
    has_token = token is not None

    def body(*refs):
        x_ref, w_ref, o_ref = refs[0], refs[1], refs[-1]
        xv = x_ref[...]
        if has_token:
            xv = xv + refs[2][0:1, 0:1]
        r = lax.rsqrt(jnp.mean(xv * xv, axis=-1, keepdims=True) + NORM_EPS)
        o_ref[...] = (xv * r * w_ref[...]).astype(o_ref.dtype)

    in_specs = [pl.BlockSpec((tm, D), lambda i: (i, 0)), pl.BlockSpec((1, D), lambda i: (0, 0))]
    args = [x, w.reshape(1, D)]
    if has_token:
        in_specs.append(pl.BlockSpec((SUBLANES, LANES), lambda i: (0, 0)))
        args.append(token)
    return pl.pallas_call(
        body, grid=(T // tm,), in_specs=in_specs,
        out_specs=pl.BlockSpec((tm, D), lambda i: (i, 0)), out_shape=_sds((T, D), MXU),
        compiler_params=_cp("parallel"), name=name)(*args)


def _rmsnorm_bwd(x, w, dh, dres, name, token=None):
    T, D = x.shape
    tm = min(T, 512)
    has_token = token is not None

    def body(*refs):
        x_ref, w_ref, dh_ref, dr_ref = refs[:4]
        dx_ref, dw_ref = refs[-2:]
        xv = x_ref[...]
        r = lax.rsqrt(jnp.mean(xv * xv, axis=-1, keepdims=True) + NORM_EPS)
        xh = xv * r
        dh = dh_ref[...]
        g = dh * w_ref[...]
        dr = dr_ref[...] + refs[4][0:1, 0:1] if has_token else dr_ref[...]
        dx_ref[...] = dr + r * (g - xh * jnp.mean(g * xh, axis=-1, keepdims=True))
        part = jnp.sum(dh * xh, axis=0, keepdims=True)

        @pl.when(pl.program_id(0) == 0)
        def _():
            dw_ref[...] = part

        @pl.when(pl.program_id(0) > 0)
        def _():
            dw_ref[...] += part

    row = pl.BlockSpec((tm, D), lambda i: (i, 0))
    vec = pl.BlockSpec((1, D), lambda i: (0, 0))
    in_specs = [row, vec, row, row]
    args = [x, w.reshape(1, D), dh, dres]
    if has_token:
        in_specs.append(pl.BlockSpec((SUBLANES, LANES), lambda i: (0, 0)))
        args.append(token)
    return pl.pallas_call(
        body, grid=(T // tm,), in_specs=in_specs, out_specs=[row, vec],
        out_shape=[_sds((T, D)), _sds((1, D))], compiler_params=_cp("arbitrary"), name=name)(*args)


def _loss_head(x, w, target, name):
    T, D = x.shape
    tm = min(T, 512)

    def body(x_ref, w_ref, t_ref, loss_ref, dx_ref, dw_ref):
        xv = x_ref[...]
        r = lax.rsqrt(jnp.mean(xv * xv, axis=-1, keepdims=True) + NORM_EPS)
        xh = xv * r
        wv = w_ref[...]
        e = xh * wv - t_ref[...]
        lpart = 0.5 * jnp.sum(jnp.mean(e * e, axis=-1, keepdims=True), axis=0, keepdims=True)
        dy = e * (1.0 / D)
        g = dy * wv
        dx_ref[...] = r * (g - xh * jnp.mean(g * xh, axis=-1, keepdims=True))
        part = jnp.sum(dy * xh, axis=0, keepdims=True)
        lrow = jnp.broadcast_to(lpart, (1, LANES))

        @pl.when(pl.program_id(0) == 0)
        def _():
            dw_ref[...] = part
            loss_ref[...] = lrow

        @pl.when(pl.program_id(0) > 0)
        def _():
            dw_ref[...] += part
            loss_ref[...] += lrow

    row = pl.BlockSpec((tm, D), lambda i: (i, 0))
    vec = pl.BlockSpec((1, D), lambda i: (0, 0))
    return pl.pallas_call(
        body, grid=(T // tm,), in_specs=[row, vec, row],
        out_specs=[pl.BlockSpec((1, LANES), lambda i: (0, 0)), row, vec],
        out_shape=[_sds((1, LANES)), _sds((T, D)), _sds((1, D))],
        compiler_params=_cp("arbitrary"), name=name)(x, w.reshape(1, D), target)


def _shift_down(cur, prev8, s):
    if s == 0:
        return cur
    tm = cur.shape[0]
    rc = pltpu.roll(cur, s, 0)
    top = jnp.where(_iota((SUBLANES, cur.shape[1]), 0) < s, pltpu.roll(prev8, s, 0), rc[:SUBLANES])
    return jnp.concatenate([top, rc[SUBLANES:]], axis=0) if tm > SUBLANES else top


def _shift_up(cur, next8, s):
    if s == 0:
        return cur
    tm = cur.shape[0]
    rc = pltpu.roll(cur, tm - s, 0)
    bot = jnp.where(_iota((SUBLANES, cur.shape[1]), 0) >= SUBLANES - s, pltpu.roll(next8, SUBLANES - s, 0), rc[tm - SUBLANES:])
    return jnp.concatenate([rc[:tm - SUBLANES], bot], axis=0) if tm > SUBLANES else bot


def _conv_rows(cur, prev8, w, b, K):
    acc = cur * w[K - 1:K, :] + b
    for s in range(1, K):
        acc = acc + _shift_down(cur, prev8, s) * w[K - 1 - s:K - s, :]
    return acc


FFN_TC = 1408
HALO16 = 2 * SUBLANES


def _ffn_mid_fwd(hid, cw, cb, name):
    T = hid.shape[0]
    tm = min(T, 256)
    nt, nj = T // tm, D_FF // FFN_TC
    K = FFN_CONV

    q = tm // HALO16

    def body(h_ref, hp_ref, w_ref, b_ref, o_ref):
        i = pl.program_id(0)
        cur = h_ref[...].astype(F32)
        prev8 = jnp.where(i > 0, hp_ref[...].astype(F32)[HALO16 - SUBLANES:], 0.0)
        hc = _conv_rows(cur, prev8, w_ref[...], b_ref[...], K)
        o_ref[...] = (_silu(hc[:, FFN_TC:]) * hc[:, :FFN_TC]).astype(o_ref.dtype)

    return pl.pallas_call(
        body, grid=(nt, nj),
        in_specs=[pl.BlockSpec((tm, 2 * FFN_TC), lambda i, j: (i, j)),
                  pl.BlockSpec((HALO16, 2 * FFN_TC), lambda i, j: (jnp.maximum(i * q - 1, 0), j)),
                  pl.BlockSpec((K, 2 * FFN_TC), lambda i, j: (0, j)), pl.BlockSpec((1, 2 * FFN_TC), lambda i, j: (0, j))],
        out_specs=pl.BlockSpec((tm, FFN_TC), lambda i, j: (i, j)), out_shape=_sds((T, D_FF), MXU),
        compiler_params=_cp("parallel", "parallel"), name=name)(hid, hid, cw, cb)


def _ffn_mid_bwd(hid, cw, cb, da, name):
    T = hid.shape[0]
    tm = min(T, 256)
    nt, nj = T // tm, D_FF // FFN_TC
    K = FFN_CONV
    W2 = 2 * FFN_TC

    def body(h_ref, hp_ref, hn_ref, da_ref, dan_ref, w_ref, b_ref, dh_ref, dw_ref, db_ref):
        i = pl.program_id(1)
        w = w_ref[...]
        b = b_ref[...]
        cur = h_ref[...].astype(F32)
        prev8 = jnp.where(i > 0, hp_ref[...].astype(F32)[HALO16 - SUBLANES:], 0.0)
        nxt8 = hn_ref[...].astype(F32)[:SUBLANES]
        last = i == nt - 1

        def dpre(hc, dav):
            u, g = hc[:, :FFN_TC], hc[:, FFN_TC:]
            return jnp.concatenate([dav * _silu(g), dav * u * _dsilu(g)], axis=1)

        hc = _conv_rows(cur, prev8, w, b, K)
        d_cur = dpre(hc, da_ref[...])
        hc_n = _conv_rows(nxt8, cur[tm - SUBLANES:], w, b, K)
        d_nxt = jnp.where(last, 0.0, dpre(hc_n, dan_ref[...]))
        ups = [d_cur] + [_shift_up(d_cur, d_nxt, s) for s in range(1, K)]
        dh = ups[0] * w[K - 1:K, :]
        for s in range(1, K):
            dh = dh + ups[s] * w[K - 1 - s:K - s, :]
        dh_ref[...] = dh.astype(dh_ref.dtype)
        dwp = jnp.concatenate([jnp.sum(ups[K - 1 - k] * cur, axis=0, keepdims=True) for k in range(K)], axis=0)
        dbp = jnp.sum(d_cur, axis=0, keepdims=True)

        @pl.when(i == 0)
        def _():
            dw_ref[...] = dwp
            db_ref[...] = dbp

        @pl.when(i > 0)
        def _():
            dw_ref[...] += dwp
            db_ref[...] += dbp

    q = tm // SUBLANES
    qh = tm // HALO16
    blk = pl.BlockSpec((tm, W2), lambda j, i: (i, j))
    prv = pl.BlockSpec((HALO16, W2), lambda j, i: (jnp.maximum(i * qh - 1, 0), j))
    nxt = pl.BlockSpec((HALO16, W2), lambda j, i: (jnp.minimum((i + 1) * qh, nt * qh - 1), j))
    dab = pl.BlockSpec((tm, FFN_TC), lambda j, i: (i, j))
    dan = pl.BlockSpec((SUBLANES, FFN_TC), lambda j, i: (jnp.minimum((i + 1) * q, nt * q - 1), j))
    return pl.pallas_call(
        body, grid=(nj, nt),
        in_specs=[blk, prv, nxt, dab, dan, pl.BlockSpec((K, W2), lambda j, i: (0, j)), pl.BlockSpec((1, W2), lambda j, i: (0, j))],
        out_specs=[blk, pl.BlockSpec((K, W2), lambda j, i: (0, j)), pl.BlockSpec((1, W2), lambda j, i: (0, j))],
        out_shape=[_sds((T, 2 * D_FF), MXU), _sds((K, 2 * D_FF)), _sds((1, 2 * D_FF))],
        compiler_params=_cp("parallel", "arbitrary"), name=name)(hid, hid, hid, da, da, cw, cb)


def _rope(t, cos, sin_s, inverse=False):
    n = t.shape[1] // LANES
    c = jnp.concatenate([cos] * n, axis=1) if n > 1 else cos
    s = jnp.concatenate([sin_s] * n, axis=1) if n > 1 else sin_s
    a = pltpu.roll(t, HEAD_DIM // 2, 1)
    b = pltpu.roll(t, t.shape[1] - HEAD_DIM // 2, 1)
    first = (_iota(t.shape, 1) % HEAD_DIM) < HEAD_DIM // 2
    rot = jnp.where(first, b, a) * s
    return t * c - rot if inverse else t * c + rot


def _stack_heads(t, g):
    return jnp.concatenate([t[:, (GQ * g + r) * HEAD_DIM:(GQ * g + r + 1) * HEAD_DIM] for r in range(GQ)], axis=0)


def _stack_cols(t, g):
    return jnp.concatenate([t[:, GQ * g + r:GQ * g + r + 1] for r in range(GQ)], axis=0)


def _pool_sums(prev, cur, w):
    s = jnp.concatenate([prev, cur], axis=0)
    sh = 1
    while sh < w:
        s = s + pltpu.roll(s, sh, 0)
        sh *= 2
    return s[BLOCK:]


def _nt(a, b):
    return lax.dot_general(a.astype(MXU), b.astype(MXU), (((1,), (1,)), ((), ())), preferred_element_type=F32)


def _tn(a, b):
    return lax.dot_general(a.astype(MXU), b.astype(MXU), (((0,), (0,)), ((), ())), preferred_element_type=F32)


def _nn(a, b):
    return jnp.dot(a.astype(MXU), b.astype(MXU), preferred_element_type=F32)


def _mixcore_fwd(proj, cos, sin_s, pool_w, pool_scale, sinks, name):
    T = proj.shape[0]
    nb = T // BLOCK
    scale = HEAD_DIM ** -0.5

    def body(p_ref, pp_ref, c_ref, s_ref, cp_ref, sp_ref, pw_ref, ps_ref, sk_ref, cat_ref, at_ref, lse_ref):
        i = pl.program_id(0)
        has_prev = i > 0
        cur = p_ref[...]
        prv = jnp.where(has_prev, pp_ref[...], 0.0)
        tpos = (i * BLOCK + _iota((BLOCK, 1), 0) + 1).astype(F32)
        for g, w in enumerate(POOL_WINDOWS):
            sl = slice(g * POOL_GROUP, (g + 1) * POOL_GROUP)
            pooled = _pool_sums(prv[:, sl], cur[:, sl], w) / jnp.minimum(tpos, float(w)) - cur[:, sl]
            cat_ref[:, sl] = (_nn(pooled, pw_ref[g]) * ps_ref[:, sl]).astype(cat_ref.dtype)
        q = _rope(cur[:, POOL_DIM:POOL_DIM + Q_DIM], c_ref[...], s_ref[...])
        kc = _rope(cur[:, POOL_DIM + Q_DIM:POOL_DIM + Q_DIM + KV_DIM], c_ref[...], s_ref[...])
        kp = _rope(prv[:, POOL_DIM + Q_DIM:POOL_DIM + Q_DIM + KV_DIM], cp_ref[...], sp_ref[...])
        vc = cur[:, POOL_DIM + Q_DIM + KV_DIM:]
        vp = prv[:, POOL_DIM + Q_DIM + KV_DIM:]
        ri = _iota((GQ * BLOCK, BLOCK), 0) % BLOCK
        cj = _iota((GQ * BLOCK, BLOCK), 1)
        mc = cj <= ri
        mp = jnp.logical_and(cj > ri, has_prev)
        outs, lses = [], []
        for g in range(N_KV_HEADS):
            hs = slice(g * HEAD_DIM, (g + 1) * HEAD_DIM)
            qg = _stack_heads(q, g) * scale
            sc = jnp.where(mc, _nt(qg, kc[:, hs]), NEG)
            sp = jnp.where(mp, _nt(qg, kp[:, hs]), NEG)
            sink = jnp.concatenate([jnp.full((BLOCK, 1), sk_ref[GQ * g + r], F32) for r in range(GQ)], axis=0)
            m = jnp.maximum(jnp.maximum(jnp.max(sc, axis=1, keepdims=True), jnp.max(sp, axis=1, keepdims=True)), sink)
            pc = jnp.exp(sc - m)
            pp = jnp.exp(sp - m)
            den = jnp.sum(pc, axis=1, keepdims=True) + jnp.sum(pp, axis=1, keepdims=True) + jnp.exp(sink - m)
            o = (_nn(pc, vc[:, hs]) + _nn(pp, vp[:, hs])) / den
            lse = m + jnp.log(den)
            for r in range(GQ):
                outs.append(o[r * BLOCK:(r + 1) * BLOCK])
                lses.append(lse[r * BLOCK:(r + 1) * BLOCK])
        attn = jnp.concatenate(outs, axis=1)
        at_ref[...] = attn
        cat_ref[:, POOL_DIM:] = attn.astype(cat_ref.dtype)
        lane = _iota((BLOCK, LANES), 1)
        lrow = jnp.zeros((BLOCK, LANES), F32)
        for h in range(N_HEADS):
            lrow = jnp.where(lane == h, lses[h], lrow)
        lse_ref[...] = lrow

    cur = lambda w: pl.BlockSpec((BLOCK, w), lambda i: (i, 0))
    prv = lambda w: pl.BlockSpec((BLOCK, w), lambda i: (jnp.maximum(i - 1, 0), 0))
    return pl.pallas_call(
        body, grid=(nb,),
        in_specs=[cur(MIX_IN_DIM), prv(MIX_IN_DIM), cur(LANES), cur(LANES), prv(LANES), prv(LANES),
                  pl.BlockSpec((4, POOL_GROUP, POOL_GROUP), lambda i: (0, 0, 0)), pl.BlockSpec((1, POOL_DIM), lambda i: (0, 0)),
                  pl.BlockSpec(memory_space=pltpu.SMEM)],
        out_specs=[cur(2 * POOL_DIM), cur(Q_DIM), cur(LANES)],
        out_shape=[_sds((T, 2 * POOL_DIM), MXU), _sds((T, Q_DIM)), _sds((T, LANES))],
        compiler_params=_cp("parallel"), name=name)(proj, proj, cos, sin_s, cos, sin_s, pool_w, pool_scale, sinks)


def _mixcore_bwd(proj, cos, sin_s, pool_w, pool_scale, sinks, attn, lse, dcat, name):
    T = proj.shape[0]
    nb = T // BLOCK
    scale = HEAD_DIM ** -0.5
    QO, KO, VO = POOL_DIM, POOL_DIM + Q_DIM, POOL_DIM + Q_DIM + KV_DIM

    def body(p_ref, pp_ref, pn_ref, c_ref, s_ref, cp_ref, sp_ref, cn_ref, sn_ref, pw_ref, ps_ref, sk_ref,
             at_ref, atn_ref, l_ref, ln_ref, d_ref, dn_ref, dp_ref, dpw_ref, dps_ref, dsk_ref):
        i = pl.program_id(0)
        has_prev = i > 0
        has_next = i < nb - 1
        cur = p_ref[...]
        prv = jnp.where(has_prev, pp_ref[...], 0.0)
        d_cur = d_ref[...]
        d_nxt = jnp.where(has_next, dn_ref[...], 0.0)

        tpos = (i * BLOCK + _iota((BLOCK, 1), 0) + 1).astype(F32)
        tpos2 = (i * BLOCK + _iota((2 * BLOCK, 1), 0) + 1).astype(F32)
        ps = ps_ref[...]
        dps_parts, dpw_parts = [], []
        for g, w in enumerate(POOL_WINDOWS):
            sl = slice(g * POOL_GROUP, (g + 1) * POOL_GROUP)
            pooled = _pool_sums(prv[:, sl], cur[:, sl], w) / jnp.minimum(tpos, float(w)) - cur[:, sl]
            mixed = _nn(pooled, pw_ref[g])
            dps_parts.append(jnp.sum(d_cur[:, sl] * mixed, axis=0, keepdims=True))
            dm2 = jnp.concatenate([d_cur[:, sl], d_nxt[:, sl]], axis=0) * ps[:, sl]
            dpw_parts.append(_tn(pooled, dm2[:BLOCK]))
            dpool2 = _nt(dm2, pw_ref[g])
            e = dpool2 / jnp.minimum(tpos2, float(w))
            sh = 1
            while sh < w:
                e = e + pltpu.roll(e, 2 * BLOCK - sh, 0)
                sh *= 2
            dp_ref[:, sl] = (e[:BLOCK] - dpool2[:BLOCK]).astype(dp_ref.dtype)
        dpsp = jnp.concatenate(dps_parts, axis=1)

        nxt = pn_ref[...]
        q = _rope(cur[:, QO:KO], c_ref[...], s_ref[...])
        qn = _rope(nxt[:, QO:KO], cn_ref[...], sn_ref[...])
        kc = _rope(cur[:, KO:VO], c_ref[...], s_ref[...])
        kp = _rope(prv[:, KO:VO], cp_ref[...], sp_ref[...])
        vc, vp = cur[:, VO:], prv[:, VO:]
        do, don = d_cur[:, POOL_DIM:], d_nxt[:, POOL_DIM:]
        dl = do * at_ref[...]
        dln = don * atn_ref[...]
        lse, lsen = l_ref[...], ln_ref[...]
        ri = _iota((GQ * BLOCK, BLOCK), 0) % BLOCK
        cj = _iota((GQ * BLOCK, BLOCK), 1)
        mc = cj <= ri
        mp = jnp.logical_and(cj > ri, has_prev)
        mn = jnp.logical_and(cj > ri, has_next)
        dq_parts, dk_parts, dv_parts, dsk_vals = [], [], [], []
        for g in range(N_KV_HEADS):
            hs = slice(g * HEAD_DIM, (g + 1) * HEAD_DIM)
            qg, qng = _stack_heads(q, g) * scale, _stack_heads(qn, g) * scale
            dog, dong = _stack_heads(do, g), _stack_heads(don, g)
            delta = jnp.sum(_stack_heads(dl, g), axis=1, keepdims=True)
            deltan = jnp.sum(_stack_heads(dln, g), axis=1, keepdims=True)
            lg, lng = _stack_cols(lse, g), _stack_cols(lsen, g)
            pc = jnp.where(mc, jnp.exp(_nt(qg, kc[:, hs]) - lg), 0.0)
            pp = jnp.where(mp, jnp.exp(_nt(qg, kp[:, hs]) - lg), 0.0)
            pn = jnp.where(mn, jnp.exp(_nt(qng, kc[:, hs]) - lng), 0.0)
            dsc = pc * (_nt(dog, vc[:, hs]) - delta)
            dsp = pp * (_nt(dog, vp[:, hs]) - delta)
            dsn = pn * (_nt(dong, vc[:, hs]) - deltan)
            dqg = (_nn(dsc, kc[:, hs]) + _nn(dsp, kp[:, hs])) * scale
            dq_parts += [dqg[r * BLOCK:(r + 1) * BLOCK] for r in range(GQ)]
            dk_parts.append(_tn(dsc, qg) + _tn(dsn, qng))
            dv_parts.append(_tn(pc, dog) + _tn(pn, dong))
            sink = jnp.concatenate([jnp.full((BLOCK, 1), sk_ref[GQ * g + r], F32) for r in range(GQ)], axis=0)
            dsk = -jnp.exp(sink - lg) * delta
            dsk_vals += [jnp.sum(dsk[r * BLOCK:(r + 1) * BLOCK], axis=0, keepdims=True) for r in range(GQ)]
        dq = _rope(jnp.concatenate(dq_parts, axis=1), c_ref[...], s_ref[...], inverse=True)
        dk = _rope(jnp.concatenate(dk_parts, axis=1), c_ref[...], s_ref[...], inverse=True)
        dp_ref[:, QO:KO] = dq.astype(dp_ref.dtype)
        dp_ref[:, KO:VO] = dk.astype(dp_ref.dtype)
        dp_ref[:, VO:] = jnp.concatenate(dv_parts, axis=1).astype(dp_ref.dtype)
        lane = _iota((1, LANES), 1)
        dskp = jnp.zeros((1, LANES), F32)
        for h in range(N_HEADS):
            dskp = jnp.where(lane == h, dsk_vals[h], dskp)

        @pl.when(i == 0)
        def _():
            dps_ref[...] = dpsp
            dsk_ref[...] = dskp
            for g in range(4):
                dpw_ref[g] = dpw_parts[g]

        @pl.when(i > 0)
        def _():
            dps_ref[...] += dpsp
            dsk_ref[...] += dskp
            for g in range(4):
                dpw_ref[g] += dpw_parts[g]

    cur = lambda w: pl.BlockSpec((BLOCK, w), lambda i: (i, 0))
    prv = lambda w: pl.BlockSpec((BLOCK, w), lambda i: (jnp.maximum(i - 1, 0), 0))
    nxt = lambda w: pl.BlockSpec((BLOCK, w), lambda i: (jnp.minimum(i + 1, nb - 1), 0))
    return pl.pallas_call(
        body, grid=(nb,),
        in_specs=[cur(MIX_IN_DIM), prv(MIX_IN_DIM), nxt(MIX_IN_DIM),
                  cur(LANES), cur(LANES), prv(LANES), prv(LANES), nxt(LANES), nxt(LANES),
                  pl.BlockSpec((4, POOL_GROUP, POOL_GROUP), lambda i: (0, 0, 0)), pl.BlockSpec((1, POOL_DIM), lambda i: (0, 0)),
                  pl.BlockSpec(memory_space=pltpu.SMEM),
                  cur(Q_DIM), nxt(Q_DIM), cur(LANES), nxt(LANES), cur(2 * POOL_DIM), nxt(2 * POOL_DIM)],
        out_specs=[cur(MIX_IN_DIM), pl.BlockSpec((4, POOL_GROUP, POOL_GROUP), lambda i: (0, 0, 0)),
                   pl.BlockSpec((1, POOL_DIM), lambda i: (0, 0)), pl.BlockSpec((1, LANES), lambda i: (0, 0))],
        out_shape=[_sds((T, MIX_IN_DIM), MXU), _sds((4, POOL_GROUP, POOL_GROUP)), _sds((1, POOL_DIM)), _sds((1, LANES))],
        compiler_params=_cp("arbitrary"), name=name)(
            proj, proj, proj, cos, sin_s, cos, sin_s, cos, sin_s, pool_w, pool_scale, sinks, attn, attn, lse, lse, dcat, dcat)


SSM_TC = 128
GROUP_W = SSM_D_INNER // SSM_GROUPS
PERM_W = GROUP_W + 2 * SSM_STATE


def _perm_col(n):
    nx = SSM_D_INNER // SSM_TC
    nbt = SSM_GROUPS
    x_idx = (n // 2) * 4 + n % 2
    b_idx = (n - nx) * 4 + 2
    c_idx = (n - nx - nbt) * 4 + 3
    return jnp.where(n < nx, x_idx, jnp.where(n < nx + nbt, b_idx, c_idx))


def _ssm_pre_fwd(xbc, cw, cb, name):
    T = xbc.shape[0]
    tm = min(T, 1024)
    K = SSM_CONV
    q = tm // SUBLANES

    def body(x_ref, xp_ref, w_ref, b_ref, o_ref):
        prev8 = jnp.where(pl.program_id(0) > 0, xp_ref[...], 0.0)
        o_ref[...] = _silu(_conv_rows(x_ref[...], prev8, w_ref[...], b_ref[...], K))

    tc = 512
    return pl.pallas_call(
        body, grid=(T // tm, SSM_CONV_DIM // tc),
        in_specs=[pl.BlockSpec((tm, tc), lambda i, j: (i, j)),
                  pl.BlockSpec((SUBLANES, tc), lambda i, j: (jnp.maximum(i * q - 1, 0), j)),
                  pl.BlockSpec((K, tc), lambda i, j: (0, j)), pl.BlockSpec((1, tc), lambda i, j: (0, j))],
        out_specs=pl.BlockSpec((tm, tc), lambda i, j: (i, j)), out_shape=_sds((T, SSM_CONV_DIM)),
        compiler_params=_cp("parallel", "parallel"), name=name)(xbc, xbc, cw, cb)


def _ssm_pre_bwd(xbc, cw, cb, dact_perm, name):
    T = xbc.shape[0]
    tm = min(T, 1024)
    nt = T // tm
    K = SSM_CONV
    q = tm // SUBLANES
    tc = SSM_TC

    def body(x_ref, xp_ref, xn_ref, d_ref, dn_ref, w_ref, b_ref, dx_ref, dw_ref, db_ref):
        i = pl.program_id(1)
        w = w_ref[...]
        b = b_ref[...]
        cur = x_ref[...]
        prev8 = jnp.where(i > 0, xp_ref[...], 0.0)
        nxt8 = xn_ref[...]
        d_cur = d_ref[...] * _dsilu(_conv_rows(cur, prev8, w, b, K))
        d_nxt = jnp.where(i == nt - 1, 0.0, dn_ref[...] * _dsilu(_conv_rows(nxt8, cur[tm - SUBLANES:], w, b, K)))
        ups = [d_cur] + [_shift_up(d_cur, d_nxt, s) for s in range(1, K)]
        dx = ups[0] * w[K - 1:K, :]
        for s in range(1, K):
            dx = dx + ups[s] * w[K - 1 - s:K - s, :]
        dx_ref[...] = dx.astype(dx_ref.dtype)
        dwp = jnp.concatenate([jnp.sum(ups[K - 1 - k] * cur, axis=0, keepdims=True) for k in range(K)], axis=0)
        dbp = jnp.sum(d_cur, axis=0, keepdims=True)

        @pl.when(i == 0)
        def _():
            dw_ref[...] = dwp
            db_ref[...] = dbp

        @pl.when(i > 0)
        def _():
            dw_ref[...] += dwp
            db_ref[...] += dbp

    nxt_row = lambda i: jnp.minimum((i + 1) * q, nt * q - 1)
    return pl.pallas_call(
        body, grid=(SSM_CONV_DIM // tc, nt),
        in_specs=[pl.BlockSpec((tm, tc), lambda j, i: (i, j)),
                  pl.BlockSpec((SUBLANES, tc), lambda j, i: (jnp.maximum(i * q - 1, 0), j)),
                  pl.BlockSpec((SUBLANES, tc), lambda j, i: (nxt_row(i), j)),
                  pl.BlockSpec((tm, tc), lambda j, i: (i, _perm_col(j))),
                  pl.BlockSpec((SUBLANES, tc), lambda j, i: (nxt_row(i), _perm_col(j))),
                  pl.BlockSpec((K, tc), lambda j, i: (0, j)), pl.BlockSpec((1, tc), lambda j, i: (0, j))],
        out_specs=[pl.BlockSpec((tm, tc), lambda j, i: (i, j)), pl.BlockSpec((K, tc), lambda j, i: (0, j)),
                   pl.BlockSpec((1, tc), lambda j, i: (0, j))],
        out_shape=[_sds((T, SSM_CONV_DIM), MXU), _sds((K, SSM_CONV_DIM)), _sds((1, SSM_CONV_DIM))],
        compiler_params=_cp("parallel", "arbitrary"), name=name)(xbc, xbc, xbc, dact_perm, dact_perm, cw, cb)


def _dot_hi(a, b):
    return jnp.dot(a, b, precision=HI, preferred_element_type=F32)


def _ssd_common(dtraw, bias, alog):
    L = SSM_CHUNK
    xb = dtraw + bias
    dt = jnp.maximum(xb, 0.0) + jnp.log1p(jnp.exp(-jnp.abs(xb)))
    A = -jnp.exp(alog)
    tril = (_iota((L, L), 1) <= _iota((L, L), 0)).astype(F32)
    acs = _dot_hi(tril, dt * A)
    return xb, dt, A, tril, acs


def _head_selectors():
    es = (_iota((LANES, SSM_D_INNER), 0) == _iota((LANES, SSM_D_INNER), 1) // HEAD_DIM).astype(BF16)
    est = (_iota((SSM_D_INNER, LANES), 1) == _iota((SSM_D_INNER, LANES), 0) // HEAD_DIM).astype(BF16)
    return es, est


def _dot_sel(v, sel):
    hi = v.astype(BF16)
    r1 = v - hi.astype(F32)
    mid = r1.astype(BF16)
    lo = (r1 - mid.astype(F32)).astype(BF16)
    d = lambda a: jnp.dot(a, sel, preferred_element_type=F32)
    return (d(hi) + d(mid)) + d(lo)


def _expand_heads(v, es):
    return _dot_sel(v, es)


def _reduce_heads(q, est):
    return _dot_sel(q, est)


def _per_state_row(v, g):
    return jnp.concatenate([jnp.broadcast_to(v[:, GQ * g + r:GQ * g + r + 1], (HEAD_DIM, 1)) for r in range(GQ)], axis=0)


def _ssd_fwd(xact, dtraw, dt_bias, a_log, name):
    T = xact.shape[0]
    nc = T // SSM_CHUNK
    L = SSM_CHUNK
    BO, CO = SSM_D_INNER, SSM_D_INNER + SSM_GROUPS * SSM_STATE

    def body(x_ref, dt_ref, bias_ref, al_ref, es_ref, y_ref, st_ref, state):
        @pl.when(pl.program_id(0) == 0)
        def _():
            state[...] = jnp.zeros(state.shape, F32)

        _, dt, A, tril, acs = _ssd_common(dt_ref[...], bias_ref[...], al_ref[...])
        acsT = acs.T
        last = acs[L - 1:L, :]
        cd = jnp.exp(last)
        es = es_ref[...]
        dtX = _expand_heads(dt, es)
        EX = _expand_heads(jnp.exp(acs), es)
        decX = _expand_heads(jnp.exp(last - acs), es)
        for g in range(SSM_GROUPS):
            gs = slice(g * GROUP_W, (g + 1) * GROUP_W)
            B = x_ref[:, BO + g * SSM_STATE:BO + (g + 1) * SSM_STATE]
            C = x_ref[:, CO + g * SSM_STATE:CO + (g + 1) * SSM_STATE]
            X = x_ref[:, gs] * dtX[:, gs]
            CB = _nt(C, B)
            yd = []
            for r in range(GQ):
                h = GQ * g + r
                Lm = jnp.exp(jnp.where(tril > 0, acs[:, h:h + 1] - acsT[h:h + 1, :], NEG))
                yd.append(_nn(CB * Lm, X[:, r * HEAD_DIM:(r + 1) * HEAD_DIM]))
            S = state[g]
            st_ref[g] = S
            y_ref[:, gs] = jnp.concatenate(yd, axis=1) + _nt(C, S) * EX[:, gs]
            state[g] = S * _per_state_row(cd, g) + _tn(X * decX[:, gs], B)

    es, _ = _head_selectors()
    return pl.pallas_call(
        body, grid=(nc,),
        in_specs=[pl.BlockSpec((L, SSM_CONV_DIM), lambda c: (c, 0)), pl.BlockSpec((L, LANES), lambda c: (c, 0)),
                  pl.BlockSpec((1, LANES), lambda c: (0, 0)), pl.BlockSpec((1, LANES), lambda c: (0, 0)),
                  pl.BlockSpec((LANES, SSM_D_INNER), lambda c: (0, 0))],
        out_specs=[pl.BlockSpec((L, SSM_D_INNER), lambda c: (c, 0)),
                   pl.BlockSpec((None, SSM_GROUPS, GROUP_W, SSM_STATE), lambda c: (c, 0, 0, 0))],
        out_shape=[_sds((T, SSM_D_INNER)), _sds((nc, SSM_GROUPS, GROUP_W, SSM_STATE))],
        scratch_shapes=[pltpu.VMEM((SSM_GROUPS, GROUP_W, SSM_STATE), F32)],
        compiler_params=_cp("arbitrary"), name=name)(xact, dtraw, dt_bias, a_log, es)


def _ssd_bwd(xact, dtraw, dt_bias, a_log, d_skip, states, dy, name):
    T = xact.shape[0]
    nc = T // SSM_CHUNK
    L = SSM_CHUNK
    BO, CO = SSM_D_INNER, SSM_D_INNER + SSM_GROUPS * SSM_STATE

    def body(x_ref, dt_ref, bias_ref, al_ref, dsk_ref, es_ref, est_ref, st_ref, dy_ref,
             dxp_ref, ddt_ref, dbias_ref, dal_ref, dd_ref, dstate, qa, qx):
        cc = pl.program_id(0)

        @pl.when(cc == 0)
        def _():
            dstate[...] = jnp.zeros(dstate.shape, F32)

        xb, dt, A, tril, acs = _ssd_common(dt_ref[...], bias_ref[...], al_ref[...])
        acsT = acs.T
        last = acs[L - 1:L, :]
        cd = jnp.exp(last)
        es, est = es_ref[...], est_ref[...]
        dtX = _expand_heads(dt, es)
        EX = _expand_heads(jnp.exp(acs), es)
        decX = _expand_heads(jnp.exp(last - acs), es)
        lane1 = _iota((1, LANES), 1)
        lane = _iota((L, LANES), 1)
        sub = _iota((L, LANES), 0)
        ztot = jnp.zeros((1, LANES), F32)
        wrow = jnp.zeros((L, LANES), F32)
        wcolT = jnp.zeros((LANES, L), F32)
        rows_dec, rows_dd = [], []
        for g in range(SSM_GROUPS):
            gs = slice(g * GROUP_W, (g + 1) * GROUP_W)
            x = x_ref[:, gs]
            B = x_ref[:, BO + g * SSM_STATE:BO + (g + 1) * SSM_STATE]
            C = x_ref[:, CO + g * SSM_STATE:CO + (g + 1) * SSM_STATE]
            dY = dy_ref[:, gs]
            dtx, e_x, dec_x = dtX[:, gs], EX[:, gs], decX[:, gs]
            X = x * dtx
            CB = _nt(C, B)
            S = st_ref[g]
            dS_out = dstate[g]
            dcb_sum = jnp.zeros((L, L), F32)
            dxd = []
            for r in range(GQ):
                h = GQ * g + r
                hs = slice(r * HEAD_DIM, (r + 1) * HEAD_DIM)
                Lm = jnp.exp(jnp.where(tril > 0, acs[:, h:h + 1] - acsT[h:h + 1, :], NEG))
                M = CB * Lm
                dM = _nt(dY[:, hs], X[:, hs])
                dxd.append(_tn(M, dY[:, hs]))
                dcb_sum = dcb_sum + dM * Lm
                Wm = dM * M
                wrow = jnp.where(lane == h, jnp.sum(Wm, axis=1, keepdims=True), wrow)
                wcolT = jnp.where(sub == h, jnp.sum(Wm, axis=0, keepdims=True), wcolT)
            dXd = jnp.concatenate(dxd, axis=1)
            G = _nt(C, S)
            dG = dY * e_x
            dDX = _nt(B, dS_out)
            dX = dXd + dec_x * dDX
            t_dec = dDX * X * dec_x
            qa[:, gs] = dG * G - t_dec
            qx[:, gs] = dX * x
            rows_dec.append(jnp.sum(t_dec, axis=0, keepdims=True))
            rows_dd.append(jnp.sum(dY * x, axis=0, keepdims=True))
            zc = jnp.sum(dS_out * S, axis=1, keepdims=True)
            for r in range(GQ):
                ztot = jnp.where(lane1 == GQ * g + r, jnp.sum(zc[r * HEAD_DIM:(r + 1) * HEAD_DIM], axis=0, keepdims=True), ztot)
            dxp_ref[:, g * PERM_W:g * PERM_W + GROUP_W] = dX * dtx + dY * dsk_ref[:, gs]
            dxp_ref[:, g * PERM_W + GROUP_W:g * PERM_W + GROUP_W + SSM_STATE] = _tn(dcb_sum, C) + _nn(X * dec_x, dS_out)
            dxp_ref[:, g * PERM_W + GROUP_W + SSM_STATE:(g + 1) * PERM_W] = _nn(dcb_sum, B) + _nn(dG, S)
            dstate[g] = dS_out * _per_state_row(cd, g) + _tn(dG, C)
        rows = jnp.concatenate([jnp.concatenate(rows_dec, axis=1), jnp.concatenate(rows_dd, axis=1)]
                               + [jnp.zeros((SUBLANES - 2, SSM_D_INNER), F32)], axis=0)
        rsum = _reduce_heads(rows, est)
        dlast = rsum[0:1, :] + cd * ztot
        dacs = (wrow - wcolT.T) + _reduce_heads(qa[...], est) + jnp.where(sub == L - 1, dlast, 0.0)
        triu = (_iota((L, L), 0) <= _iota((L, L), 1)).astype(F32)
        da = _dot_hi(triu, dacs)
        ddtraw = (da * A + _reduce_heads(qx[...], est)) * (1.0 / (1.0 + jnp.exp(-xb)))
        ddt_ref[...] = ddtraw
        dal = jnp.sum(da * dt, axis=0, keepdims=True) * A
        ddp = rsum[1:2, :]
        dbp = jnp.sum(ddtraw, axis=0, keepdims=True)

        @pl.when(cc == 0)
        def _():
            dbias_ref[...] = dbp
            dal_ref[...] = dal
            dd_ref[...] = ddp

        @pl.when(cc > 0)
        def _():
            dbias_ref[...] += dbp
            dal_ref[...] += dal
            dd_ref[...] += ddp

    rc = lambda c: nc - 1 - c
    vec = pl.BlockSpec((1, LANES), lambda c: (0, 0))
    es, est = _head_selectors()
    return pl.pallas_call(
        body, grid=(nc,),
        in_specs=[pl.BlockSpec((L, SSM_CONV_DIM), lambda c: (rc(c), 0)), pl.BlockSpec((L, LANES), lambda c: (rc(c), 0)), vec, vec,
                  pl.BlockSpec((1, SSM_D_INNER), lambda c: (0, 0)),
                  pl.BlockSpec((LANES, SSM_D_INNER), lambda c: (0, 0)), pl.BlockSpec((SSM_D_INNER, LANES), lambda c: (0, 0)),
                  pl.BlockSpec((None, SSM_GROUPS, GROUP_W, SSM_STATE), lambda c: (rc(c), 0, 0, 0)),
                  pl.BlockSpec((L, SSM_D_INNER), lambda c: (rc(c), 0))],
        out_specs=[pl.BlockSpec((L, SSM_GROUPS * PERM_W), lambda c: (rc(c), 0)),
                   pl.BlockSpec((L, LANES), lambda c: (rc(c), 0)), vec, vec, vec],
        out_shape=[_sds((T, SSM_GROUPS * PERM_W)), _sds((T, LANES)), _sds((1, LANES)), _sds((1, LANES)), _sds((1, LANES))],
        scratch_shapes=[pltpu.VMEM((SSM_GROUPS, GROUP_W, SSM_STATE), F32), pltpu.VMEM((L, SSM_D_INNER), F32),
                        pltpu.VMEM((L, SSM_D_INNER), F32)],
        compiler_params=_cp("arbitrary"), name=name)(xact, dtraw, dt_bias, a_log, d_skip, es, est, states, dy)


def _ssm_post_fwd(y, xact, z, d_skip, nw, name):
    T = y.shape[0]
    tm = min(T, 256)
    W = SSM_D_INNER

    def body(y_ref, x_ref, z_ref, d_ref, w_ref, o_ref):
        y2 = (y_ref[...] + d_ref[...] * x_ref[...]) * _silu(z_ref[...])
        r = lax.rsqrt(jnp.mean(y2 * y2, axis=-1, keepdims=True) + SSM_NORM_EPS)
        o_ref[...] = (y2 * r * w_ref[...]).astype(o_ref.dtype)

    row = pl.BlockSpec((tm, W), lambda i: (i, 0))
    vec = pl.BlockSpec((1, W), lambda i: (0, 0))
    return pl.pallas_call(
        body, grid=(T // tm,), in_specs=[row, row, row, vec, vec], out_specs=row, out_shape=_sds((T, W), MXU),
        compiler_params=_cp("parallel"), name=name)(y, xact, z, d_skip, nw)


def _ssm_post_bwd(y, xact, z, d_skip, nw, dyn, name):
    T = y.shape[0]
    tm = min(T, 256)
    W = SSM_D_INNER

    def body(y_ref, x_ref, z_ref, d_ref, w_ref, dn_ref, dyg_ref, dz_ref, dw_ref):
        zv = z_ref[...]
        sz = _silu(zv)
        yg = y_ref[...] + d_ref[...] * x_ref[...]
        y2 = yg * sz
        r = lax.rsqrt(jnp.mean(y2 * y2, axis=-1, keepdims=True) + SSM_NORM_EPS)
        y2h = y2 * r
        dn = dn_ref[...]
        gy = dn * w_ref[...]
        dy2 = r * (gy - y2h * jnp.mean(gy * y2h, axis=-1, keepdims=True))
        dyg_ref[...] = dy2 * sz
        dz_ref[...] = (dy2 * yg * _dsilu(zv)).astype(dz_ref.dtype)
        part = jnp.sum(dn * y2h, axis=0, keepdims=True)

        @pl.when(pl.program_id(0) == 0)
        def _():
            dw_ref[...] = part

        @pl.when(pl.program_id(0) > 0)
        def _():
            dw_ref[...] += part

    row = pl.BlockSpec((tm, W), lambda i: (i, 0))
    vec = pl.BlockSpec((1, W), lambda i: (0, 0))
    return pl.pallas_call(
        body, grid=(T // tm,), in_specs=[row, row, row, vec, vec, row], out_specs=[row, row, vec],
        out_shape=[_sds((T, W)), _sds((T, W), MXU), _sds((1, W))],
        compiler_params=_cp("arbitrary"), name=name)(y, xact, z, d_skip, nw, dyn)


def _local_step(x0, cos, sin_s, target, P, fetch, token, send):
    mmf = functools.partial(_mm, tm=1024)
    big, small = {}, {}
    P = dict(P, wup={}, wdn={}, fcw={})
    h0 = _rmsnorm_fwd(x0, P["nm"][0], "norm_mix0", token=token)
    proj0 = mmf(h0, P["wmiT"], tb=True, tn=1280, tk=1024, name="mix_in")
    cat, attn, lse = _mixcore_fwd(proj0, cos, sin_s, P["pool_w"], P["pool_scale"], P["sinks"], "mixcore_fwd")
    x1 = mmf(cat, P["wmo"], tn=1024, tk=1024, res=x0, name="mix_out")

    def ffn_fwd(xin, i):
        hf = _rmsnorm_fwd(xin, P["nf"][i], f"norm_ffn{i}")
        got = fetch(f"ffn{i}", hf)
        P["wup"][i], P["wdn"][i], P["fcw"][i] = got["wup"], got["wdn"], got["fcw"]
        hid = mmf(hf, P["wup"][i], tn=1408, tk=1024, out_dtype=MXU, name=f"ffn_up{i}")
        act = _ffn_mid_fwd(hid, P["fcw"][i], P["fcb"][i], f"ffn_mid_fwd{i}")
        xout = mmf(act, P["wdn"][i], tn=1024, tk=D_FF, res=xin, name=f"ffn_down{i}")
        return hf, hid, act, xout

    hf0, hid0, act0, x2 = ffn_fwd(x1, 0)
    h1 = _rmsnorm_fwd(x2, P["nm"][1], "norm_mix1")
    P.update(fetch("ssm", h1))
    z = mmf(h1, P["wzT"], tb=True, tn=1024, tk=1024, name="ssm_in_z")
    xbc = mmf(h1, P["wxbcT"], tb=True, tn=1024, tk=1024, name="ssm_in_xbc")
    dtraw = mmf(h1, P["wdtT"], tb=True, tn=128, tk=1024, name="ssm_in_dt")
    xact = _ssm_pre_fwd(xbc, P["scw"], P["scb"], "ssm_pre_fwd")
    y, states = _ssd_fwd(xact, dtraw, P["dt_bias"], P["a_log"], "ssd_fwd")
    yn = _ssm_post_fwd(y, xact, z, P["d_exp"], P["snorm"], "ssm_post_fwd")
    x3 = mmf(yn, P["wso"], tn=1024, tk=SSM_D_INNER, res=x2, name="ssm_out")
    hf1, hid1, act1, x4 = ffn_fwd(x3, 1)
    loss_row, dx4, d_nfin = _loss_head(x4, P["nfin"], target, "loss_head")
    small["norm_final"] = d_nfin

    def ffn_bwd(xin, dxo, hf, hid, act, i):
        da = mmf(dxo, P["wdn"][i], tb=True, tn=1408, tk=1024, name=f"ffn_down_dx{i}")
        big[f"ffn_w_down{i}"] = dwf(act, dxo, tm=1408, tn=1024, name=f"ffn_down_dw{i}").reshape(N_CHIPS, D_FF // N_CHIPS, D_MODEL)
        dhid, dcw, dcb = _ffn_mid_bwd(hid, P["fcw"][i], P["fcb"][i], da, f"ffn_mid_bwd{i}")
        dhf = mmf(dhid, P["wup"][i], tb=True, tn=1024, tk=1408, name=f"ffn_up_dx{i}")
        big[f"ffn_w_up{i}"] = dwf(hf, dhid, tm=1024, tn=1408, out_shard_perm=(0, 2, 1, 3), name=f"ffn_up_dw{i}")
        tok = send(f"ffn{i}", [big[f"ffn_w_up{i}"], big[f"ffn_w_down{i}"]])
        dxi, dnf = _rmsnorm_bwd(xin, P["nf"][i], dhf, dxo, f"norm_ffn_bwd{i}", token=tok)
        return dxi, dnf, dcw, dcb

    dwf = functools.partial(_mm, ta=True, tk=1024, out_dtype=BF16)
    dx3, dnf1, dfcw1, dfcb1 = ffn_bwd(x3, dx4, hf1, hid1, act1, 1)
    dyn = mmf(dx3, P["wso"], tb=True, tn=1024, tk=1024, name="ssm_out_dx")
    big["ssm_w_out"] = dwf(yn, dx3, tm=1024, tn=1024, name="ssm_out_dw").reshape(N_CHIPS, SSM_D_INNER // N_CHIPS, D_MODEL)
    dyg, dz, d_snorm = _ssm_post_bwd(y, xact, z, P["d_exp"], P["snorm"], dyn, "ssm_post_bwd")
    dxact_p, ddtraw, d_dtb, d_alog, d_dskip = _ssd_bwd(xact, dtraw, P["dt_bias"], P["a_log"], P["d_exp"], states, dyg, "ssd_bwd")
    dxbc, d_scw, d_scb = _ssm_pre_bwd(xbc, P["scw"], P["scb"], dxact_p, "ssm_pre_bwd")
    dh1 = mmf(dz, P["wzT"], tn=1024, tk=1024, name="ssm_in_dx_z")
    dh1 = mmf(dxbc, P["wxbcT"], tn=1024, tk=1024, res=dh1, name="ssm_in_dx_xbc")
    dh1 = mmf(ddtraw, P["wdtT"], tn=1024, tk=128, res=dh1, name="ssm_in_dx_dt")
    dwz = dwf(dz, h1, tm=1024, tn=1024, name="ssm_in_dw_z")
    dwxbc = dwf(dxbc, h1, tm=1024, tn=1024, name="ssm_in_dw_xbc")
    dwdt = dwf(ddtraw, h1, tm=128, tn=1024, name="ssm_in_dw_dt")
    dwsi = jnp.concatenate([dwz, dwxbc, dwdt[:SSM_HEADS]], axis=0)
    big["ssm_w_in"] = dwsi.reshape(N_CHIPS, SSM_IN_DIM // N_CHIPS, D_MODEL)
    tok = send("ssm", [big["ssm_w_in"], big["ssm_w_out"]])
    dx2, dnm1 = _rmsnorm_bwd(x2, P["nm"][1], dh1, dx3, "norm_mix_bwd1", token=tok)
    dx1, dnf0, dfcw0, dfcb0 = ffn_bwd(x1, dx2, hf0, hid0, act0, 0)
    dcat = mmf(dx1, P["wmo"], tb=True, tn=1024, tk=1024, name="mix_out_dx")
    big["mix_w_out"] = dwf(cat, dx1, tm=1024, tn=1024, name="mix_out_dw").reshape(N_CHIPS, D_MODEL // N_CHIPS, D_MODEL)
    dproj0, d_pw, d_ps, d_sk = _mixcore_bwd(proj0, cos, sin_s, P["pool_w"], P["pool_scale"], P["sinks"], attn, lse, dcat, "mixcore_bwd")
    dh0 = mmf(dproj0, P["wmiT"], tn=1024, tk=1280, name="mix_in_dx")
    big["mix_w_in"] = dwf(dproj0, h0, tm=1280, tn=1024, name="mix_in_dw").reshape(N_CHIPS, MIX_IN_DIM // N_CHIPS, D_MODEL)
    tok = send("mix", [big["mix_w_in"], big["mix_w_out"]])
    dx0, dnm0 = _rmsnorm_bwd(x0, P["nm"][0], dh0, dx1, "norm_mix_bwd0", token=tok)

    def unperm_cols(a):
        r = a.shape[0]
        t = a.reshape(r, N_CHIPS, FFN_TC)
        return jnp.stack([t[:, p] for p in _PERM], axis=0)

    small["norm_mix"] = jnp.concatenate([dnm0, dnm1], axis=0)
    small["norm_ffn"] = jnp.concatenate([dnf0, dnf1], axis=0)
    small["pool_w"] = d_pw.reshape(4 * POOL_GROUP, POOL_GROUP)
    small["pool_scale"] = d_ps
    small["attn_sinks"] = d_sk
    small["ssm_dt_bias"] = d_dtb
    small["ssm_A_log"] = d_alog
    small["ssm_D"] = d_dskip
    fcb = jnp.stack([unperm_cols(dfcb0), unperm_cols(dfcb1)], axis=0)
    small["ffn_conv_b"] = fcb.reshape(2, 2 * D_FF)
    small["ssm_conv_w"] = d_scw.reshape(SSM_CONV, N_CHIPS, SSM_CONV_DIM // N_CHIPS).transpose(1, 0, 2)
    small["ssm_conv_b"] = d_scb.reshape(N_CHIPS, 1, SSM_CONV_DIM // N_CHIPS)
    small["ssm_norm"] = d_snorm.reshape(N_CHIPS, 1, SSM_D_INNER // N_CHIPS)
    small["ffn_conv_w"] = jnp.concatenate([unperm_cols(dfcw0), unperm_cols(dfcw1)], axis=1)
    return loss_row, dx0, big, small


ANY = pl.BlockSpec(memory_space=pl.ANY)


def _place():
    return lax.axis_index("x"), lax.axis_index("y"), lax.axis_index("c")


def _gather_shards(shards, name):
    n = len(shards)
    split = [s.size >= (1 << 16) for s in shards]

    def half(ref, a, h):
        shp = shards[a].shape
        if len(shp) == 3:
            return ref.at[h]
        r2 = shp[0] // 2
        return ref.at[pl.ds(pl.multiple_of(h * r2, 2 * SUBLANES), r2), :]

    def body(*refs):
        ins, outs = refs[:n], refs[n:2 * n]
        send, recv, fsend, frecv = refs[2 * n:]
        x, y, c = _place()
        k = 2 * x + y
        chips = [(1 - x, y), (x, 1 - y), (1 - x, 1 - y)]

        def ici(a, j, src_slot_ref, dst_slot):
            px, py = chips[j]
            src = half(src_slot_ref, a, c) if split[a] else src_slot_ref
            dst = half(outs[a].at[dst_slot], a, c) if split[a] else outs[a].at[dst_slot]
            return pltpu.make_async_remote_copy(src, dst, send.at[a, j], recv.at[a, j], device_id=(px, py, c), device_id_type=MESH)

        def d2d(a, j, h):
            px, py = chips[j]
            part = half(outs[a].at[2 * px + py], a, h)
            return pltpu.make_async_remote_copy(part, part, fsend.at[a, j], frecv.at[a, j], device_id=(x, y, 1 - c), device_id_type=MESH)

        sends = [ici(a, j, ins[a], k) for a in range(n) for j in range(3)]
        for cp in sends:
            cp.start()
        passed = []
        for a in range(n):
            for j, (px, py) in enumerate(chips):
                ici(a, j, ins[a], 2 * px + py).wait_recv()
                if split[a]:
                    passed.append(d2d(a, j, c))
                    passed[-1].start()
        for a in range(n):
            if split[a]:
                for j in range(3):
                    d2d(a, j, 1 - c).wait_recv()
        for cp in sends + passed:
            cp.wait_send()

    return pl.pallas_call(
        body, in_specs=[ANY] * n, out_specs=[ANY] * n,
        out_shape=[_sds((N_CHIPS,) + s.shape, s.dtype) for s in shards],
        scratch_shapes=[pltpu.SemaphoreType.DMA((n, 3))] * 4,
        compiler_params=pltpu.CompilerParams(has_side_effects=True), name=name)(*shards)


HBM = pl.BlockSpec(memory_space=pltpu.HBM)
SEM = pl.BlockSpec(memory_space=pltpu.SEMAPHORE)
DATAFLOW = pltpu.SideEffectType.DATAFLOW_SIDE_EFFECTING


def _spread_start(groups, slot_src, after, name):
    flat = [a for grp in groups for a in grp]
    n = len(flat)
    ng = len(groups)
    offs = [sum(len(g) for g in groups[:i]) for i in range(ng)]
    lshape = [(a.shape if slot_src else (N_CHIPS,) + a.shape) for a in flat]

    nsem = 6 * n

    def body(*refs):
        src, land = refs[:n], refs[n:2 * n]
        sems = refs[2 * n + 1:2 * n + 1 + nsem]
        token = refs[-1]
        x, y, c = _place()
        k = 2 * x + y
        chips = [(1 - x, y), (x, 1 - y), (1 - x, 1 - y)]
        for a in range(n):
            for j, (px, py) in enumerate(chips):
                s = src[a].at[2 * px + py] if slot_src else src[a]
                pltpu.make_async_remote_copy(s, land[a].at[k], sems[6 * a + 2 * j], sems[6 * a + 2 * j + 1],
                                             device_id=(px, py, c), device_id_type=MESH).start()
        token[...] = jnp.zeros(token.shape, token.dtype)

    out_shape = [pltpu.SemaphoreType.DMA(())] * nsem
    out_shape += [pltpu.HBM(a.shape, a.dtype) for a in flat] + [pltpu.HBM(s, a.dtype) for s, a in zip(lshape, flat)]
    out_shape.append(_sds((SUBLANES, LANES)))
    args = [pltpu.with_memory_space_constraint(a, pltpu.HBM) for a in flat]
    args += [pltpu.with_memory_space_constraint(lax.empty(s, a.dtype), pltpu.HBM) for s, a in zip(lshape, flat)]
    res = pl.pallas_call(
        body, name=name, out_shape=tuple(out_shape), in_specs=[HBM] * (2 * n) + [pl.BlockSpec(memory_space=pl.ANY)],
        out_specs=tuple([SEM] * nsem + [HBM] * (2 * n) + [pl.BlockSpec(memory_space=pltpu.VMEM)]),
        input_output_aliases={i: nsem + i for i in range(2 * n)},
        compiler_params=pltpu.CompilerParams(has_side_effects=DATAFLOW))(*args, after)
    sems, thru, token = res[:nsem], res[nsem:nsem + 2 * n], res[-1]
    out = []
    for gi, grp in enumerate(groups):
        sl = slice(offs[gi], offs[gi] + len(grp))
        out.append((list(sems[6 * offs[gi]:6 * (offs[gi] + len(grp))]), list(thru[:n][sl]), list(thru[n:][sl])))
    return out, token


def _spread_wait(started, slot_src, after, name):
    sems, srcs, lands = started
    n = len(srcs)

    def body(*refs):
        src, land = refs[:n], refs[n:2 * n]
        sem = refs[2 * n:2 * n + 6 * n]
        x, y, c = _place()
        chips = [(1 - x, y), (x, 1 - y), (1 - x, 1 - y)]
        for a in range(n):
            for j, (px, py) in enumerate(chips):
                s = src[a].at[2 * px + py] if slot_src else src[a]
                cp = pltpu.make_async_remote_copy(s, land[a].at[2 * px + py], sem[6 * a + 2 * j], sem[6 * a + 2 * j + 1],
                                                  device_id=(px, py, c), device_id_type=MESH)
                cp.wait_send()
                cp.wait_recv()

    res = pl.pallas_call(
        body, name=name, out_shape=tuple([pltpu.HBM(a.shape, a.dtype) for a in srcs] + [pltpu.HBM(a.shape, a.dtype) for a in lands]),
        in_specs=[HBM] * (2 * n) + [SEM] * (6 * n) + [pl.BlockSpec(memory_space=pl.ANY)], out_specs=tuple([HBM] * (2 * n)),
        input_output_aliases={i: i for i in range(2 * n)},
        compiler_params=pltpu.CompilerParams(has_side_effects=DATAFLOW))(*srcs, *lands, *sems, after)
    return list(res[:n]), list(res[n:])


def _sibling_exchange(fs, name):
    n = len(fs)

    def body(*refs):
        ins, outs = refs[:n], refs[n:2 * n]
        send, recv = refs[2 * n:]
        x, y, c = _place()
        cps = [pltpu.make_async_remote_copy(ins[a], outs[a], send.at[a], recv.at[a],
                                            device_id=(x, y, 1 - c), device_id_type=MESH) for a in range(n)]
        for cp in cps:
            cp.start()
        for cp in cps:
            cp.wait()

    return pl.pallas_call(
        body, in_specs=[ANY] * n, out_specs=[ANY] * n, out_shape=[_sds(f.shape, f.dtype) for f in fs],
        scratch_shapes=[pltpu.SemaphoreType.DMA((n,)), pltpu.SemaphoreType.DMA((n,))],
        compiler_params=pltpu.CompilerParams(has_side_effects=True), name=name)(*fs)


def _tile2d(rows, cols, budget=1024 * 1024, step=2 * SUBLANES):
    fits = [t for t in range(step, rows + 1, step) if rows % t == 0 and t * cols * 4 <= budget]
    if fits:
        return fits[-1], cols
    fits = [t for t in range(LANES, cols + 1, LANES) if cols % t == 0 and rows * t * 4 <= budget]
    assert fits, (rows, cols)
    return rows, fits[-1]


def _chip_sum(own, parts, kidx, name):
    _, R, C = parts.shape
    tr, tc = _tile2d(R, C)

    def body(k_ref, o_ref_in, p1_ref, p2_ref, p3_ref, o_ref):
        o_ref[...] = ((o_ref_in[...].astype(F32) + p1_ref[...].astype(F32)) + p2_ref[...].astype(F32)) + p3_ref[...].astype(F32)

    def slot(d):
        return pl.BlockSpec((None, tr, tc), lambda i, j, k: ((k[0] + d) % N_CHIPS, i, j))

    return pl.pallas_call(
        body,
        grid_spec=pltpu.PrefetchScalarGridSpec(
            num_scalar_prefetch=1, grid=(R // tr, C // tc), in_specs=[slot(0), slot(1), slot(2), slot(3)],
            out_specs=pl.BlockSpec((tr, tc), lambda i, j, k: (i, j))),
        out_shape=_sds((R, C)), compiler_params=_cp("parallel", "parallel"), name=name)(kidx, own, parts, parts, parts)


def _adamw_math(w, g, m, v):
    m2 = ADAM_B1 * m + (1.0 - ADAM_B1) * g
    v2 = ADAM_B2 * v + (1.0 - ADAM_B2) * (g * g)
    m_hat = m2 / (1.0 - ADAM_B1 ** ADAM_STEP)
    v_hat = v2 / (1.0 - ADAM_B2 ** ADAM_STEP)
    delta = -ADAM_LR * (m_hat / (jnp.sqrt(v_hat) + ADAM_EPS) + ADAM_WD * w)
    return delta, m2, v2


def _adamw(w, m, v, gparts, name):
    Lw, R, C = w.shape
    tr, tc = _tile2d(R, C)
    flat = [h for pair in gparts for h in pair]

    def body(*refs):
        w_ref, m_ref, v_ref = refs[:3]
        g_refs = refs[3:3 + 2 * Lw]
        go_ref, d_ref, mo_ref, vo_ref = refs[3 + 2 * Lw:]
        g = g_refs[0][...] + g_refs[1][...]
        for l in range(1, Lw):
            g = jnp.where(pl.program_id(0) == l, g_refs[2 * l][...] + g_refs[2 * l + 1][...], g)
        d, m2, v2 = _adamw_math(w_ref[...], g, m_ref[...], v_ref[...])
        go_ref[...] = g
        d_ref[...] = d
        mo_ref[...] = m2
        vo_ref[...] = v2

    blk = pl.BlockSpec((None, tr, tc), lambda l, i, j: (l, i, j))
    gblk = pl.BlockSpec((tr, tc), lambda l, i, j: (i, j))
    return pl.pallas_call(
        body, grid=(Lw, R // tr, C // tc), in_specs=[blk, blk, blk] + [gblk] * (2 * Lw), out_specs=[blk] * 4,
        out_shape=[_sds((Lw, R, C))] * 4, compiler_params=_cp("parallel", "parallel", "parallel"), name=name)(w, m, v, *flat)


def _small_adamw(grads, wmv, name):
    n = len(grads)

    def body(*refs):
        g_in, p_in, outs = refs[:n], refs[n:4 * n], refs[4 * n:]
        for a in range(n):
            g = g_in[a][...]
            d_, m2, v2 = _adamw_math(p_in[3 * a][...], g, p_in[3 * a + 1][...], p_in[3 * a + 2][...])
            outs[4 * a][...] = g
            outs[4 * a + 1][...] = d_
            outs[4 * a + 2][...] = m2
            outs[4 * a + 3][...] = v2

    vm = pl.BlockSpec(memory_space=pltpu.VMEM)
    args = list(grads) + [t for tri in wmv for t in tri]
    out_shape = [_sds(g.shape) for g in grads for _ in range(4)]
    return pl.pallas_call(body, in_specs=[vm] * len(args), out_specs=[vm] * len(out_shape), out_shape=out_shape,
                          compiler_params=pltpu.CompilerParams(vmem_limit_bytes=V7X_VMEM_LIMIT), name=name)(*args)


def _small_allreduce(partials, pshapes, loss_row, name):
    n = len(partials)
    gshapes = [p.shape for p in partials] + [loss_row.shape]
    ng = n + 1

    def body(*refs):
        g_in = refs[:ng]
        outs = refs[ng:2 * ng]
        bufs = refs[2 * ng:3 * ng]
        send, recv = refs[-2:]
        x, y, c = _place()
        me = 4 * x + 2 * y + c
        k = 2 * x + y
        flips = [(fx, fy, fc) for fx in (0, 1) for fy in (0, 1) for fc in (0, 1)][1:]

        def peer(f):
            return (x ^ f[0], y ^ f[1], c ^ f[2])

        def slot(p):
            return 4 * p[0] + 2 * p[1] + p[2]

        for a in range(ng):
            bufs[a][me] = g_in[a][...]
        sends = [pltpu.make_async_remote_copy(g_in[a], bufs[a].at[me], send.at[a, j], recv.at[a, j],
                                              device_id=peer(f), device_id_type=MESH)
                 for a in range(ng) for j, f in enumerate(flips)]
        for cp in sends:
            cp.start()
        for a in range(ng):
            for j, f in enumerate(flips):
                pltpu.make_async_remote_copy(g_in[a], bufs[a].at[slot(peer(f))], send.at[a, j], recv.at[a, j],
                                             device_id=peer(f), device_id_type=MESH).wait_recv()
        for cp in sends:
            cp.wait_send()
        for a in range(ng):
            sharded = len(gshapes[a]) == 3

            def part(d):
                return bufs[a][d, k] if sharded else bufs[a][d]

            tot = part(0)
            for d in range(1, N_DEV):
                tot = tot + part(d)
            if a == n:
                outs[n][...] = tot
            else:
                pr, pc = pshapes[a]
                outs[a][...] = tot[:pr, :pc]

    vm = pl.BlockSpec(memory_space=pltpu.VMEM)
    args = list(partials) + [loss_row]
    out_shape = [_sds(ps) for ps in pshapes] + [_sds(loss_row.shape)]
    return pl.pallas_call(
        body, in_specs=[vm] * len(args), out_specs=[vm] * len(out_shape), out_shape=out_shape,
        scratch_shapes=[pltpu.VMEM((N_DEV,) + tuple(s), F32) for s in gshapes]
        + [pltpu.SemaphoreType.DMA((ng, N_DEV - 1)), pltpu.SemaphoreType.DMA((ng, N_DEV - 1))],
        compiler_params=pltpu.CompilerParams(has_side_effects=True, vmem_limit_bytes=V7X_VMEM_LIMIT), name=name)(*args)


_PERM = (0, 2, 1, 3)


def _cols_from_shards(g):
    return g.transpose(1, 0, 2).reshape(g.shape[1], N_CHIPS * g.shape[2])


def _rope_tables(positions):
    inv_freq = ROPE_THETA ** (-jnp.arange(0, HEAD_DIM, 2, dtype=F32) / HEAD_DIM)
    ang = positions.astype(F32).reshape(-1, 1) * inv_freq
    cos, sin = jnp.cos(ang), jnp.sin(ang)
    cos = jnp.concatenate([cos, cos, cos, cos], axis=-1)
    sin_s = jnp.concatenate([-sin, sin, -sin, sin], axis=-1)
    return cos, sin_s


def kernel(x, positions, norm_mix, norm_ffn, norm_final, mix_w_in, pool_w, pool_scale, attn_sinks, mix_w_out, ssm_w_in, ssm_conv_w, ssm_conv_b, ssm_dt_bias, ssm_A_log, ssm_D, ssm_norm, ssm_w_out, ffn_w_up, ffn_conv_w, ffn_conv_b, ffn_w_down, loss_target, m_norm_mix, m_norm_ffn, m_norm_final, m_mix_w_in, m_pool_w, m_pool_scale, m_attn_sinks, m_mix_w_out, m_ssm_w_in, m_ssm_conv_w, m_ssm_conv_b, m_ssm_dt_bias, m_ssm_A_log, m_ssm_D, m_ssm_norm, m_ssm_w_out, m_ffn_w_up, m_ffn_conv_w, m_ffn_conv_b, m_ffn_w_down, v_norm_mix, v_norm_ffn, v_norm_final, v_mix_w_in, v_pool_w, v_pool_scale, v_attn_sinks, v_mix_w_out, v_ssm_w_in, v_ssm_conv_w, v_ssm_conv_b, v_ssm_dt_bias, v_ssm_A_log, v_ssm_D, v_ssm_norm, v_ssm_w_out, v_ffn_w_up, v_ffn_conv_w, v_ffn_conv_b, v_ffn_w_down):
    W = dict(norm_mix=norm_mix, norm_ffn=norm_ffn, norm_final=norm_final, mix_w_in=mix_w_in, pool_w=pool_w, pool_scale=pool_scale, attn_sinks=attn_sinks, mix_w_out=mix_w_out, ssm_w_in=ssm_w_in, ssm_conv_w=ssm_conv_w, ssm_conv_b=ssm_conv_b, ssm_dt_bias=ssm_dt_bias, ssm_A_log=ssm_A_log, ssm_D=ssm_D, ssm_norm=ssm_norm, ssm_w_out=ssm_w_out, ffn_w_up=ffn_w_up, ffn_conv_w=ffn_conv_w, ffn_conv_b=ffn_conv_b, ffn_w_down=ffn_w_down)
    Mo = dict(norm_mix=m_norm_mix, norm_ffn=m_norm_ffn, norm_final=m_norm_final, mix_w_in=m_mix_w_in, pool_w=m_pool_w, pool_scale=m_pool_scale, attn_sinks=m_attn_sinks, mix_w_out=m_mix_w_out, ssm_w_in=m_ssm_w_in, ssm_conv_w=m_ssm_conv_w, ssm_conv_b=m_ssm_conv_b, ssm_dt_bias=m_ssm_dt_bias, ssm_A_log=m_ssm_A_log, ssm_D=m_ssm_D, ssm_norm=m_ssm_norm, ssm_w_out=m_ssm_w_out, ffn_w_up=m_ffn_w_up, ffn_conv_w=m_ffn_conv_w, ffn_conv_b=m_ffn_conv_b, ffn_w_down=m_ffn_w_down)
    Vo = dict(norm_mix=v_norm_mix, norm_ffn=v_norm_ffn, norm_final=v_norm_final, mix_w_in=v_mix_w_in, pool_w=v_pool_w, pool_scale=v_pool_scale, attn_sinks=v_attn_sinks, mix_w_out=v_mix_w_out, ssm_w_in=v_ssm_w_in, ssm_conv_w=v_ssm_conv_w, ssm_conv_b=v_ssm_conv_b, ssm_dt_bias=v_ssm_dt_bias, ssm_A_log=v_ssm_A_log, ssm_D=v_ssm_D, ssm_norm=v_ssm_norm, ssm_w_out=v_ssm_w_out, ffn_w_up=v_ffn_w_up, ffn_conv_w=v_ffn_conv_w, ffn_conv_b=v_ffn_conv_b, ffn_w_down=v_ffn_w_down)

    kchip = 2 * lax.axis_index("x") + lax.axis_index("y")

    def own_slot(g, own):
        return lax.dynamic_update_slice_in_dim(g, own[None], kchip, axis=0)

    def tr(t):
        return jnp.swapaxes(t[0], 0, 1)

    later = dict(ffn0=[ffn_w_up[0].astype(MXU), ffn_w_down[0].astype(MXU)],
                 ssm=[tr(ssm_w_in).astype(MXU), ssm_w_out[0].astype(MXU)],
                 ffn1=[ffn_w_up[1].astype(MXU), ffn_w_down[1].astype(MXU)])
    sh = [tr(mix_w_in).astype(MXU), mix_w_out[0].astype(MXU), ssm_conv_w[0], ssm_conv_b, ssm_norm, ffn_conv_w]
    first = _gather_shards(sh, "gather_first")
    g_mi, g_mo, g_scw, g_scb, g_sn, g_fcw = [own_slot(g, own) for g, own in zip(first, sh)]
    started, token = _spread_start(list(later.values()), False, first[0], "gather_start")
    started = dict(zip(later.keys(), started))
    fcw = [jnp.concatenate([g_fcw[p, i] for p in _PERM], axis=1) for i in range(2)]
    P = dict(
        nm=norm_mix, nf=norm_ffn, nfin=norm_final,
        wmiT=g_mi.reshape(MIX_IN_DIM, D_MODEL), wmo=g_mo.reshape(D_MODEL, D_MODEL),
        pool_w=pool_w[0], pool_scale=pool_scale, sinks=attn_sinks[0],
        scw=_cols_from_shards(g_scw), scb=g_scb.reshape(1, SSM_CONV_DIM), snorm=g_sn.reshape(1, SSM_D_INNER),
        dt_bias=jnp.pad(ssm_dt_bias, ((0, 0), (0, LANES - SSM_HEADS))), a_log=jnp.pad(ssm_A_log, ((0, 0), (0, LANES - SSM_HEADS))),
        d_exp=jnp.repeat(ssm_D, SSM_D_INNER // SSM_HEADS, axis=1),
        fcb=[jnp.concatenate([ffn_conv_b[i:i + 1, p * FFN_TC:(p + 1) * FFN_TC] for p in _PERM], axis=1) for i in range(2)],
    )

    def fetch(group, after):
        owns, lands = _spread_wait(started[group], False, after, f"gather_wait_{group}")
        a, b = [own_slot(g, own) for g, own in zip(lands, owns)]
        if group == "ssm":
            wsi = a.reshape(SSM_IN_DIM, D_MODEL)
            zx = SSM_D_INNER + SSM_CONV_DIM
            return dict(wzT=wsi[:SSM_D_INNER], wxbcT=wsi[SSM_D_INNER:zx],
                        wdtT=jnp.pad(wsi[zx:], ((0, LANES - SSM_HEADS), (0, 0))), wso=b.reshape(SSM_D_INNER, D_MODEL))
        i = int(group[-1])
        return dict(wup=jnp.concatenate([a[p] for p in _PERM], axis=1), wdn=b.reshape(D_FF, D_MODEL), fcw=fcw[i])

    cos, sin_s = _rope_tables(positions)
    sent = {}

    def send(group, grads):
        res, tok = _spread_start([grads], True, jnp.zeros((SUBLANES, LANES), F32), f"grad_start_{group}")
        sent[group] = res[0]
        return tok

    loss_row, grad_x, big, small = _local_step(x[0], cos, sin_s, loss_target[0], P, fetch, token, send)

    kidx = kchip.astype(jnp.int32).reshape(1)
    group_names = dict(ffn1=["ffn_w_up1", "ffn_w_down1"], ssm=["ssm_w_in", "ssm_w_out"], ffn0=["ffn_w_up0", "ffn_w_down0"],
                       mix=["mix_w_in", "mix_w_out"])
    names, mine = [], []
    for group, started_g in sent.items():
        grads, lands = _spread_wait(started_g, True, grad_x, f"grad_wait_{group}")
        for nm, g, land in zip(group_names[group], grads, lands):
            names.append(nm)
            mine.append(_chip_sum(g, land, kidx, f"chip_sum_{nm}"))
    theirs = _sibling_exchange(mine, "sibling_exchange")
    red = {nm: (a, b) for nm, a, b in zip(names, mine, theirs)}

    out = {}

    def big_update(pname, gparts, transposed=False):
        w = W[pname]
        lw = len(gparts)
        shp = w.shape
        rr, cc = gparts[0][0].shape
        fix = (lambda t: tr(t)[None]) if transposed else (lambda t: t.reshape(lw, rr, cc))
        res = _adamw(fix(w), fix(Mo[pname]), fix(Vo[pname]), gparts, f"adamw_{pname}")
        out[pname] = tuple((tr(r)[None] if transposed else r.reshape(shp)) for r in res)

    big_update("mix_w_in", [red["mix_w_in"]], transposed=True)
    big_update("mix_w_out", [red["mix_w_out"]])
    big_update("ssm_w_in", [red["ssm_w_in"]], transposed=True)
    big_update("ssm_w_out", [red["ssm_w_out"]])
    big_update("ffn_w_up", [red["ffn_w_up0"], red["ffn_w_up1"]])
    big_update("ffn_w_down", [red["ffn_w_down0"], red["ffn_w_down1"]])

    small_names = ["norm_mix", "norm_ffn", "norm_final", "pool_w", "pool_scale", "attn_sinks", "ssm_dt_bias", "ssm_A_log",
                   "ssm_D", "ffn_conv_b", "ssm_conv_w", "ssm_conv_b", "ssm_norm", "ffn_conv_w"]

    def as2d(t):
        if t.ndim == 1:
            return t.reshape(1, -1)
        return t.reshape(-1, t.shape[-1])

    wmv = [(as2d(W[nm]), as2d(Mo[nm]), as2d(Vo[nm])) for nm in small_names]
    summed = _small_allreduce([small[nm] for nm in small_names], [t[0].shape for t in wmv], loss_row, "small_allreduce")
    res = _small_adamw(summed[:-1], wmv, "small_adamw")
    for a, nm in enumerate(small_names):
        out[nm] = tuple(r.reshape(W[nm].shape) for r in res[4 * a:4 * a + 4])
    loss = summed[-1][0, 0]

    order = ["norm_mix", "norm_ffn", "norm_final", "mix_w_in", "pool_w", "pool_scale", "attn_sinks", "mix_w_out", "ssm_w_in",
             "ssm_conv_w", "ssm_conv_b", "ssm_dt_bias", "ssm_A_log", "ssm_D", "ssm_norm", "ssm_w_out", "ffn_w_up", "ffn_conv_w",
             "ffn_conv_b", "ffn_w_down"]
    return (loss, grad_x.reshape(x.shape), *[out[nm][0] for nm in order], *[out[nm][1] for nm in order],
            *[out[nm][2] for nm in order], *[out[nm][3] for nm in order])
```

```python
import functools

import jax
import jax.numpy as jnp
from jax import lax
from jax.experimental import pallas as pl
from jax.experimental.pallas import tpu as pltpu

F32 = jnp.float32
BF16 = jnp.bfloat16
MXU = BF16
HI = lax.Precision.HIGHEST

D_MODEL = 1024
POOL_WINDOWS = (2, 4, 8, 16)
POOL_DIM = 512
POOL_GROUP = 128
HEAD_DIM = 64
N_HEADS = 8
N_KV_HEADS = 2
GQ = 4
Q_DIM = 512
KV_DIM = 128
BLOCK = 128
ROPE_THETA = 10000.0
MIX_IN_DIM = 1280
SSM_D_INNER = 2048
SSM_HEADS = 32
SSM_GROUPS = 8
SSM_STATE = 128
SSM_CONV = 4
SSM_CHUNK = 128
SSM_CONV_DIM = 4096
SSM_IN_DIM = 6176
D_FF = 2816
FFN_CONV = 3
NORM_EPS = 1e-6
SSM_NORM_EPS = 1e-5
ADAM_LR = 0.001
ADAM_B1 = 0.9
ADAM_B2 = 0.999
ADAM_EPS = 1e-08
ADAM_WD = 0.01
ADAM_STEP = 10

N_CHIPS = 4
N_DEV = 8
LANES = 128
SUBLANES = 8
V7X_VMEM_LIMIT = 56 * 1024 * 1024
NEG = -1e30
MESH = pl.DeviceIdType.MESH


def _cp(*sem):
    return pltpu.CompilerParams(dimension_semantics=sem if sem else None, vmem_limit_bytes=V7X_VMEM_LIMIT)


def _sds(shape, dtype=F32):
    return jax.ShapeDtypeStruct(tuple(shape), dtype)


def _iota(shape, dim):
    return lax.broadcasted_iota(jnp.int32, shape, dim)


def _silu(x):
    return x * (1.0 / (1.0 + jnp.exp(-x)))


def _dsilu(x):
    s = 1.0 / (1.0 + jnp.exp(-x))
    return s * (1.0 + x * (1.0 - s))


def _mm(a, b, *, ta=False, tb=False, tm, tn, tk, res=None, out_dtype=F32, out_shard_perm=None, name):
    M, K = (a.shape[1], a.shape[0]) if ta else a.shape
    N = b.shape[0] if tb else b.shape[1]
    tm, tn, tk = min(tm, M), min(tn, N), min(tk, K)
    gm, gn, gk = M // tm, N // tn, K // tk
    assert gm * tm == M and gn * tn == N and gk * tk == K, (name, M, N, K, tm, tn, tk)
    a_spec = pl.BlockSpec((tk, tm), lambda i, j, k: (k, i)) if ta else pl.BlockSpec((tm, tk), lambda i, j, k: (i, k))
    b_spec = pl.BlockSpec((tn, tk), lambda i, j, k: (j, k)) if tb else pl.BlockSpec((tk, tn), lambda i, j, k: (k, j))
    dims = (((0 if ta else 1,), (1 if tb else 0,)), ((), ()))
    has_res = res is not None

    def body(*refs):
        a_ref, b_ref = refs[0], refs[1]
        r_ref = refs[2] if has_res else None
        o_ref = refs[3] if has_res else refs[2]
        def dot():
            return lax.dot_general(a_ref[...].astype(MXU), b_ref[...].astype(MXU), dims, preferred_element_type=F32)

        if gk == 1:
            p = dot()
            if has_res:
                p = p + r_ref[...]
            o_ref[...] = p.astype(out_dtype)
        else:
            acc = refs[-1]
            k = pl.program_id(2)

            @pl.when(k == 0)
            def _():
                acc[...] = dot()

            if gk > 2:
                @pl.when(jnp.logical_and(k > 0, k < gk - 1))
                def _():
                    acc[...] += dot()

            @pl.when(k == gk - 1)
            def _():
                r = acc[...] + dot()
                if has_res:
                    r = r + r_ref[...]
                o_ref[...] = r.astype(out_dtype)

    in_specs = [a_spec, b_spec]
    args = [a, b]
    if has_res:
        in_specs.append(pl.BlockSpec((tm, tn), lambda i, j, k: (i, j)))
        args.append(res)
    if out_shard_perm is None:
        out_spec = pl.BlockSpec((tm, tn), lambda i, j, k: (i, j))
        out_shape = _sds((M, N), out_dtype)
    else:
        assert gn == len(out_shard_perm) == 4 and tuple(out_shard_perm) == (0, 2, 1, 3)
        out_spec = pl.BlockSpec((None, tm, tn), lambda i, j, k: ((j % 2) * 2 + j // 2, i, 0))
        out_shape = _sds((gn, M, tn), out_dtype)
    return pl.pallas_call(
        body, grid=(gm, gn, gk), in_specs=in_specs, out_specs=out_spec, out_shape=out_shape,
        scratch_shapes=[pltpu.VMEM((tm, tn), F32)] if gk > 1 else [],
        compiler_params=_cp("parallel", "parallel", "arbitrary"), name=name)(*args)


def _rmsnorm_fwd(x, w, name, token=None):
    T, D = x.shape
    tm = min(T, 512)
    has_token = token is not None

    def body(*refs):
        x_ref, w_ref, o_ref = refs[0], refs[1], refs[-1]
        xv = x_ref[...]
        if has_token:
            xv = xv + refs[2][0:1, 0:1]
        r = lax.rsqrt(jnp.mean(xv * xv, axis=-1, keepdims=True) + NORM_EPS)
        o_ref[...] = (xv * r * w_ref[...]).astype(o_ref.dtype)

    in_specs = [pl.BlockSpec((tm, D), lambda i: (i, 0)), pl.BlockSpec((1, D), lambda i: (0, 0))]
    args = [x, w.reshape(1, D)]
    if has_token:
        in_specs.append(pl.BlockSpec((SUBLANES, LANES), lambda i: (0, 0)))
        args.append(token)
    return pl.pallas_call(
        body, grid=(T // tm,), in_specs=in_specs,
        out_specs=pl.BlockSpec((tm, D), lambda i: (i, 0)), out_shape=_sds((T, D), MXU),
        compiler_params=_cp("parallel"), name=name)(*args)


def _rmsnorm_bwd(x, w, dh, dres, name, token=None):
    T, D = x.shape
    tm = min(T, 512)
    has_token = token is not None

    def body(*refs):
        x_ref, w_ref, dh_ref, dr_ref = refs[:4]
        dx_ref, dw_ref = refs[-2:]
        xv = x_ref[...]
        r = lax.rsqrt(jnp.mean(xv * xv, axis=-1, keepdims=True) + NORM_EPS)
        xh = xv * r
        dh = dh_ref[...]
        g = dh * w_ref[...]
        dr = dr_ref[...] + refs[4][0:1, 0:1] if has_token else dr_ref[...]
        dx_ref[...] = dr + r * (g - xh * jnp.mean(g * xh, axis=-1, keepdims=True))
        part = jnp.sum(dh * xh, axis=0, keepdims=True)

        @pl.when(pl.program_id(0) == 0)
        def _():
            dw_ref[...] = part

        @pl.when(pl.program_id(0) > 0)
        def _():
            dw_ref[...] += part

    row = pl.BlockSpec((tm, D), lambda i: (i, 0))
    vec = pl.BlockSpec((1, D), lambda i: (0, 0))
    in_specs = [row, vec, row, row]
    args = [x, w.reshape(1, D), dh, dres]
    if has_token:
        in_specs.append(pl.BlockSpec((SUBLANES, LANES), lambda i: (0, 0)))
        args.append(token)
    return pl.pallas_call(
        body, grid=(T // tm,), in_specs=in_specs, out_specs=[row, vec],
        out_shape=[_sds((T, D)), _sds((1, D))], compiler_params=_cp("arbitrary"), name=name)(*args)


def _loss_head(x, w, target, name):
    T, D = x.shape
    tm = min(T, 512)

    def body(x_ref, w_ref, t_ref, loss_ref, dx_ref, dw_ref):
        xv = x_ref[...]
        r = lax.rsqrt(jnp.mean(xv * xv, axis=-1, keepdims=True) + NORM_EPS)
        xh = xv * r
        wv = w_ref[...]
        e = xh * wv - t_ref[...]
        lpart = 0.5 * jnp.sum(jnp.mean(e * e, axis=-1, keepdims=True), axis=0, keepdims=True)
        dy = e * (1.0 / D)
        g = dy * wv
        dx_ref[...] = r * (g - xh * jnp.mean(g * xh, axis=-1, keepdims=True))
        part = jnp.sum(dy * xh, axis=0, keepdims=True)
        lrow = jnp.broadcast_to(lpart, (1, LANES))

        @pl.when(pl.program_id(0) == 0)
        def _():
            dw_ref[...] = part
            loss_ref[...] = lrow

        @pl.when(pl.program_id(0) > 0)
        def _():
            dw_ref[...] += part
            loss_ref[...] += lrow

    row = pl.BlockSpec((tm, D), lambda i: (i, 0))
    vec = pl.BlockSpec((1, D), lambda i: (0, 0))
    return pl.pallas_call(
        body, grid=(T // tm,), in_specs=[row, vec, row],
        out_specs=[pl.BlockSpec((1, LANES), lambda i: (0, 0)), row, vec],
        out_shape=[_sds((1, LANES)), _sds((T, D)), _sds((1, D))],
        compiler_params=_cp("arbitrary"), name=name)(x, w.reshape(1, D), target)


def _shift_down(cur, prev8, s):
    if s == 0:
        return cur
    tm = cur.shape[0]
    rc = pltpu.roll(cur, s, 0)
    top = jnp.where(_iota((SUBLANES, cur.shape[1]), 0) < s, pltpu.roll(prev8, s, 0), rc[:SUBLANES])
    return jnp.concatenate([top, rc[SUBLANES:]], axis=0) if tm > SUBLANES else top


def _shift_up(cur, next8, s):
    if s == 0:
        return cur
    tm = cur.shape[0]
    rc = pltpu.roll(cur, tm - s, 0)
    bot = jnp.where(_iota((SUBLANES, cur.shape[1]), 0) >= SUBLANES - s, pltpu.roll(next8, SUBLANES - s, 0), rc[tm - SUBLANES:])
    return jnp.concatenate([rc[:tm - SUBLANES], bot], axis=0) if tm > SUBLANES else bot


def _conv_rows(cur, prev8, w, b, K):
    acc = cur * w[K - 1:K, :] + b
    for s in range(1, K):
        acc = acc + _shift_down(cur, prev8, s) * w[K - 1 - s:K - s, :]
    return acc


FFN_TC = 1408
HALO16 = 2 * SUBLANES


def _ffn_mid_fwd(hid, cw, cb, name):
    T = hid.shape[0]
    tm = min(T, 256)
    nt, nj = T // tm, D_FF // FFN_TC
    K = FFN_CONV

    q = tm // HALO16

    def body(h_ref, hp_ref, w_ref, b_ref, o_ref):
        i = pl.program_id(0)
        cur = h_ref[...].astype(F32)
        prev8 = jnp.where(i > 0, hp_ref[...].astype(F32)[HALO16 - SUBLANES:], 0.0)
        hc = _conv_rows(cur, prev8, w_ref[...], b_ref[...], K)
        o_ref[...] = (_silu(hc[:, FFN_TC:]) * hc[:, :FFN_TC]).astype(o_ref.dtype)

    return pl.pallas_call(
        body, grid=(nt, nj),
        in_specs=[pl.BlockSpec((tm, 2 * FFN_TC), lambda i, j: (i, j)),
                  pl.BlockSpec((HALO16, 2 * FFN_TC), lambda i, j: (jnp.maximum(i * q - 1, 0), j)),
                  pl.BlockSpec((K, 2 * FFN_TC), lambda i, j: (0, j)), pl.BlockSpec((1, 2 * FFN_TC), lambda i, j: (0, j))],
        out_specs=pl.BlockSpec((tm, FFN_TC), lambda i, j: (i, j)), out_shape=_sds((T, D_FF), MXU),
        compiler_params=_cp("parallel", "parallel"), name=name)(hid, hid, cw, cb)


def _ffn_mid_bwd(hid, cw, cb, da, name):
    T = hid.shape[0]
    tm = min(T, 256)
    nt, nj = T // tm, D_FF // FFN_TC
    K = FFN_CONV
    W2 = 2 * FFN_TC

    def body(h_ref, hp_ref, hn_ref, da_ref, dan_ref, w_ref, b_ref, dh_ref, dw_ref, db_ref):
        i = pl.program_id(1)
        w = w_ref[...]
        b = b_ref[...]
        cur = h_ref[...].astype(F32)
        prev8 = jnp.where(i > 0, hp_ref[...].astype(F32)[HALO16 - SUBLANES:], 0.0)
        nxt8 = hn_ref[...].astype(F32)[:SUBLANES]
        last = i == nt - 1

        def dpre(hc, dav):
            u, g = hc[:, :FFN_TC], hc[:, FFN_TC:]
            return jnp.concatenate([dav * _silu(g), dav * u * _dsilu(g)], axis=1)

        hc = _conv_rows(cur, prev8, w, b, K)
        d_cur = dpre(hc, da_ref[...])
        hc_n = _conv_rows(nxt8, cur[tm - SUBLANES:], w, b, K)
        d_nxt = jnp.where(last, 0.0, dpre(hc_n, dan_ref[...]))
        ups = [d_cur] + [_shift_up(d_cur, d_nxt, s) for s in range(1, K)]
        dh = ups[0] * w[K - 1:K, :]
        for s in range(1, K):
            dh = dh + ups[s] * w[K - 1 - s:K - s, :]
        dh_ref[...] = dh.astype(dh_ref.dtype)
        dwp = jnp.concatenate([jnp.sum(ups[K - 1 - k] * cur, axis=0, keepdims=True) for k in range(K)], axis=0)
        dbp = jnp.sum(d_cur, axis=0, keepdims=True)

        @pl.when(i == 0)
        def _():
            dw_ref[...] = dwp
            db_ref[...] = dbp

        @pl.when(i > 0)
        def _():
            dw_ref[...] += dwp
            db_ref[...] += dbp

    q = tm // SUBLANES
    qh = tm // HALO16
    blk = pl.BlockSpec((tm, W2), lambda j, i: (i, j))
    prv = pl.BlockSpec((HALO16, W2), lambda j, i: (jnp.maximum(i * qh - 1, 0), j))
    nxt = pl.BlockSpec((HALO16, W2), lambda j, i: (jnp.minimum((i + 1) * qh, nt * qh - 1), j))
    dab = pl.BlockSpec((tm, FFN_TC), lambda j, i: (i, j))
    dan = pl.BlockSpec((SUBLANES, FFN_TC), lambda j, i: (jnp.minimum((i + 1) * q, nt * q - 1), j))
    return pl.pallas_call(
        body, grid=(nj, nt),
        in_specs=[blk, prv, nxt, dab, dan, pl.BlockSpec((K, W2), lambda j, i: (0, j)), pl.BlockSpec((1, W2), lambda j, i: (0, j))],
        out_specs=[blk, pl.BlockSpec((K, W2), lambda j, i: (0, j)), pl.BlockSpec((1, W2), lambda j, i: (0, j))],
        out_shape=[_sds((T, 2 * D_FF), MXU), _sds((K, 2 * D_FF)), _sds((1, 2 * D_FF))],
        compiler_params=_cp("parallel", "arbitrary"), name=name)(hid, hid, hid, da, da, cw, cb)


def _rope(t, cos, sin_s, inverse=False):
    n = t.shape[1] // LANES
    c = jnp.concatenate([cos] * n, axis=1) if n > 1 else cos
    s = jnp.concatenate([sin_s] * n, axis=1) if n > 1 else sin_s
    a = pltpu.roll(t, HEAD_DIM // 2, 1)
    b = pltpu.roll(t, t.shape[1] - HEAD_DIM // 2, 1)
    first = (_iota(t.shape, 1) % HEAD_DIM) < HEAD_DIM // 2
    rot = jnp.where(first, b, a) * s
    return t * c - rot if inverse else t * c + rot


def _stack_heads(t, g):
    return jnp.concatenate([t[:, (GQ * g + r) * HEAD_DIM:(GQ * g + r + 1) * HEAD_DIM] for r in range(GQ)], axis=0)


def _stack_cols(t, g):
    return jnp.concatenate([t[:, GQ * g + r:GQ * g + r + 1] for r in range(GQ)], axis=0)


def _pool_sums(prev, cur, w):
    s = jnp.concatenate([prev, cur], axis=0)
    sh = 1
    while sh < w:
        s = s + pltpu.roll(s, sh, 0)
        sh *= 2
    return s[BLOCK:]


def _nt(a, b):
    return lax.dot_general(a.astype(MXU), b.astype(MXU), (((1,), (1,)), ((), ())), preferred_element_type=F32)


def _tn(a, b):
    return lax.dot_general(a.astype(MXU), b.astype(MXU), (((0,), (0,)), ((), ())), preferred_element_type=F32)


def _nn(a, b):
    return jnp.dot(a.astype(MXU), b.astype(MXU), preferred_element_type=F32)


def _mixcore_fwd(proj, cos, sin_s, pool_w, pool_scale, sinks, name):
    T = proj.shape[0]
    nb = T // BLOCK
    scale = HEAD_DIM ** -0.5

    def body(p_ref, pp_ref, c_ref, s_ref, cp_ref, sp_ref, pw_ref, ps_ref, sk_ref, cat_ref, at_ref, lse_ref):
        i = pl.program_id(0)
        has_prev = i > 0
        cur = p_ref[...]
        prv = jnp.where(has_prev, pp_ref[...], 0.0)
        tpos = (i * BLOCK + _iota((BLOCK, 1), 0) + 1).astype(F32)
        for g, w in enumerate(POOL_WINDOWS):
            sl = slice(g * POOL_GROUP, (g + 1) * POOL_GROUP)
            pooled = _pool_sums(prv[:, sl], cur[:, sl], w) / jnp.minimum(tpos, float(w)) - cur[:, sl]
            cat_ref[:, sl] = (_nn(pooled, pw_ref[g]) * ps_ref[:, sl]).astype(cat_ref.dtype)
        q = _rope(cur[:, POOL_DIM:POOL_DIM + Q_DIM], c_ref[...], s_ref[...])
        kc = _rope(cur[:, POOL_DIM + Q_DIM:POOL_DIM + Q_DIM + KV_DIM], c_ref[...], s_ref[...])
        kp = _rope(prv[:, POOL_DIM + Q_DIM:POOL_DIM + Q_DIM + KV_DIM], cp_ref[...], sp_ref[...])
        vc = cur[:, POOL_DIM + Q_DIM + KV_DIM:]
        vp = prv[:, POOL_DIM + Q_DIM + KV_DIM:]
        ri = _iota((GQ * BLOCK, BLOCK), 0) % BLOCK
        cj = _iota((GQ * BLOCK, BLOCK), 1)
        mc = cj <= ri
        mp = jnp.logical_and(cj > ri, has_prev)
        outs, lses = [], []
        for g in range(N_KV_HEADS):
            hs = slice(g * HEAD_DIM, (g + 1) * HEAD_DIM)
            qg = _stack_heads(q, g) * scale
            sc = jnp.where(mc, _nt(qg, kc[:, hs]), NEG)
            sp = jnp.where(mp, _nt(qg, kp[:, hs]), NEG)
            sink = jnp.concatenate([jnp.full((BLOCK, 1), sk_ref[GQ * g + r], F32) for r in range(GQ)], axis=0)
            m = jnp.maximum(jnp.maximum(jnp.max(sc, axis=1, keepdims=True), jnp.max(sp, axis=1, keepdims=True)), sink)
            pc = jnp.exp(sc - m)
            pp = jnp.exp(sp - m)
            den = jnp.sum(pc, axis=1, keepdims=True) + jnp.sum(pp, axis=1, keepdims=True) + jnp.exp(sink - m)
            o = (_nn(pc, vc[:, hs]) + _nn(pp, vp[:, hs])) / den
            lse = m + jnp.log(den)
            for r in range(GQ):
                outs.append(o[r * BLOCK:(r + 1) * BLOCK])
                lses.append(lse[r * BLOCK:(r + 1) * BLOCK])
        attn = jnp.concatenate(outs, axis=1)
        at_ref[...] = attn
        cat_ref[:, POOL_DIM:] = attn.astype(cat_ref.dtype)
        lane = _iota((BLOCK, LANES), 1)
        lrow = jnp.zeros((BLOCK, LANES), F32)
        for h in range(N_HEADS):
            lrow = jnp.where(lane == h, lses[h], lrow)
        lse_ref[...] = lrow

    cur = lambda w: pl.BlockSpec((BLOCK, w), lambda i: (i, 0))
    prv = lambda w: pl.BlockSpec((BLOCK, w), lambda i: (jnp.maximum(i - 1, 0), 0))
    return pl.pallas_call(
        body, grid=(nb,),
        in_specs=[cur(MIX_IN_DIM), prv(MIX_IN_DIM), cur(LANES), cur(LANES), prv(LANES), prv(LANES),
                  pl.BlockSpec((4, POOL_GROUP, POOL_GROUP), lambda i: (0, 0, 0)), pl.BlockSpec((1, POOL_DIM), lambda i: (0, 0)),
                  pl.BlockSpec(memory_space=pltpu.SMEM)],
        out_specs=[cur(2 * POOL_DIM), cur(Q_DIM), cur(LANES)],
        out_shape=[_sds((T, 2 * POOL_DIM), MXU), _sds((T, Q_DIM)), _sds((T, LANES))],
        compiler_params=_cp("parallel"), name=name)(proj, proj, cos, sin_s, cos, sin_s, pool_w, pool_scale, sinks)


def _mixcore_bwd(proj, cos, sin_s, pool_w, pool_scale, sinks, attn, lse, dcat, name):
    T = proj.shape[0]
    nb = T // BLOCK
    scale = HEAD_DIM ** -0.5
    QO, KO, VO = POOL_DIM, POOL_DIM + Q_DIM, POOL_DIM + Q_DIM + KV_DIM

    def body(p_ref, pp_ref, pn_ref, c_ref, s_ref, cp_ref, sp_ref, cn_ref, sn_ref, pw_ref, ps_ref, sk_ref,
             at_ref, atn_ref, l_ref, ln_ref, d_ref, dn_ref, dp_ref, dpw_ref, dps_ref, dsk_ref):
        i = pl.program_id(0)
        has_prev = i > 0
        has_next = i < nb - 1
        cur = p_ref[...]
        prv = jnp.where(has_prev, pp_ref[...], 0.0)
        d_cur = d_ref[...]
        d_nxt = jnp.where(has_next, dn_ref[...], 0.0)

        tpos = (i * BLOCK + _iota((BLOCK, 1), 0) + 1).astype(F32)
        tpos2 = (i * BLOCK + _iota((2 * BLOCK, 1), 0) + 1).astype(F32)
        ps = ps_ref[...]
        dps_parts, dpw_parts = [], []
        for g, w in enumerate(POOL_WINDOWS):
            sl = slice(g * POOL_GROUP, (g + 1) * POOL_GROUP)
            pooled = _pool_sums(prv[:, sl], cur[:, sl], w) / jnp.minimum(tpos, float(w)) - cur[:, sl]
            mixed = _nn(pooled, pw_ref[g])
            dps_parts.append(jnp.sum(d_cur[:, sl] * mixed, axis=0, keepdims=True))
            dm2 = jnp.concatenate([d_cur[:, sl], d_nxt[:, sl]], axis=0) * ps[:, sl]
            dpw_parts.append(_tn(pooled, dm2[:BLOCK]))
            dpool2 = _nt(dm2, pw_ref[g])
            e = dpool2 / jnp.minimum(tpos2, float(w))
            sh = 1
            while sh < w:
                e = e + pltpu.roll(e, 2 * BLOCK - sh, 0)
                sh *= 2
            dp_ref[:, sl] = (e[:BLOCK] - dpool2[:BLOCK]).astype(dp_ref.dtype)
        dpsp = jnp.concatenate(dps_parts, axis=1)

        nxt = pn_ref[...]
        q = _rope(cur[:, QO:KO], c_ref[...], s_ref[...])
        qn = _rope(nxt[:, QO:KO], cn_ref[...], sn_ref[...])
        kc = _rope(cur[:, KO:VO], c_ref[...], s_ref[...])
        kp = _rope(prv[:, KO:VO], cp_ref[...], sp_ref[...])
        vc, vp = cur[:, VO:], prv[:, VO:]
        do, don = d_cur[:, POOL_DIM:], d_nxt[:, POOL_DIM:]
        dl = do * at_ref[...]
        dln = don * atn_ref[...]
        lse, lsen = l_ref[...], ln_ref[...]
        ri = _iota((GQ * BLOCK, BLOCK), 0) % BLOCK
        cj = _iota((GQ * BLOCK, BLOCK), 1)
        mc = cj <= ri
        mp = jnp.logical_and(cj > ri, has_prev)
        mn = jnp.logical_and(cj > ri, has_next)
        dq_parts, dk_parts, dv_parts, dsk_vals = [], [], [], []
        for g in range(N_KV_HEADS):
            hs = slice(g * HEAD_DIM, (g + 1) * HEAD_DIM)
            qg, qng = _stack_heads(q, g) * scale, _stack_heads(qn, g) * scale
            dog, dong = _stack_heads(do, g), _stack_heads(don, g)
            delta = jnp.sum(_stack_heads(dl, g), axis=1, keepdims=True)
            deltan = jnp.sum(_stack_heads(dln, g), axis=1, keepdims=True)
            lg, lng = _stack_cols(lse, g), _stack_cols(lsen, g)
            pc = jnp.where(mc, jnp.exp(_nt(qg, kc[:, hs]) - lg), 0.0)
            pp = jnp.where(mp, jnp.exp(_nt(qg, kp[:, hs]) - lg), 0.0)
            pn = jnp.where(mn, jnp.exp(_nt(qng, kc[:, hs]) - lng), 0.0)
            dsc = pc * (_nt(dog, vc[:, hs]) - delta)
            dsp = pp * (_nt(dog, vp[:, hs]) - delta)
            dsn = pn * (_nt(dong, vc[:, hs]) - deltan)
            dqg = (_nn(dsc, kc[:, hs]) + _nn(dsp, kp[:, hs])) * scale
            dq_parts += [dqg[r * BLOCK:(r + 1) * BLOCK] for r in range(GQ)]
            dk_parts.append(_tn(dsc, qg) + _tn(dsn, qng))
            dv_parts.append(_tn(pc, dog) + _tn(pn, dong))
            sink = jnp.concatenate([jnp.full((BLOCK, 1), sk_ref[GQ * g + r], F32) for r in range(GQ)], axis=0)
            dsk = -jnp.exp(sink - lg) * delta
            dsk_vals += [jnp.sum(dsk[r * BLOCK:(r + 1) * BLOCK], axis=0, keepdims=True) for r in range(GQ)]
        dq = _rope(jnp.concatenate(dq_parts, axis=1), c_ref[...], s_ref[...], inverse=True)
        dk = _rope(jnp.concatenate(dk_parts, axis=1), c_ref[...], s_ref[...], inverse=True)
        dp_ref[:, QO:KO] = dq.astype(dp_ref.dtype)
        dp_ref[:, KO:VO] = dk.astype(dp_ref.dtype)
        dp_ref[:, VO:] = jnp.concatenate(dv_parts, axis=1).astype(dp_ref.dtype)
        lane = _iota((1, LANES), 1)
        dskp = jnp.zeros((1, LANES), F32)
        for h in range(N_HEADS):
            dskp = jnp.where(lane == h, dsk_vals[h], dskp)

        @pl.when(i == 0)
        def _():
            dps_ref[...] = dpsp
            dsk_ref[...] = dskp
            for g in range(4):
                dpw_ref[g] = dpw_parts[g]

        @pl.when(i > 0)
        def _():
            dps_ref[...] += dpsp
            dsk_ref[...] += dskp
            for g in range(4):
                dpw_ref[g] += dpw_parts[g]

    cur = lambda w: pl.BlockSpec((BLOCK, w), lambda i: (i, 0))
    prv = lambda w: pl.BlockSpec((BLOCK, w), lambda i: (jnp.maximum(i - 1, 0), 0))
    nxt = lambda w: pl.BlockSpec((BLOCK, w), lambda i: (jnp.minimum(i + 1, nb - 1), 0))
    return pl.pallas_call(
        body, grid=(nb,),
        in_specs=[cur(MIX_IN_DIM), prv(MIX_IN_DIM), nxt(MIX_IN_DIM),
                  cur(LANES), cur(LANES), prv(LANES), prv(LANES), nxt(LANES), nxt(LANES),
                  pl.BlockSpec((4, POOL_GROUP, POOL_GROUP), lambda i: (0, 0, 0)), pl.BlockSpec((1, POOL_DIM), lambda i: (0, 0)),
                  pl.BlockSpec(memory_space=pltpu.SMEM),
                  cur(Q_DIM), nxt(Q_DIM), cur(LANES), nxt(LANES), cur(2 * POOL_DIM), nxt(2 * POOL_DIM)],
        out_specs=[cur(MIX_IN_DIM), pl.BlockSpec((4, POOL_GROUP, POOL_GROUP), lambda i: (0, 0, 0)),
                   pl.BlockSpec((1, POOL_DIM), lambda i: (0, 0)), pl.BlockSpec((1, LANES), lambda i: (0, 0))],
        out_shape=[_sds((T, MIX_IN_DIM), MXU), _sds((4, POOL_GROUP, POOL_GROUP)), _sds((1, POOL_DIM)), _sds((1, LANES))],
        compiler_params=_cp("arbitrary"), name=name)(
            proj, proj, proj, cos, sin_s, cos, sin_s, cos, sin_s, pool_w, pool_scale, sinks, attn, attn, lse, lse, dcat, dcat)


SSM_TC = 128
GROUP_W = SSM_D_INNER // SSM_GROUPS
PERM_W = GROUP_W + 2 * SSM_STATE


def _perm_col(n):
    nx = SSM_D_INNER // SSM_TC
    nbt = SSM_GROUPS
    x_idx = (n // 2) * 4 + n % 2
    b_idx = (n - nx) * 4 + 2
    c_idx = (n - nx - nbt) * 4 + 3
    return jnp.where(n < nx, x_idx, jnp.where(n < nx + nbt, b_idx, c_idx))


def _ssm_pre_fwd(xbc, cw, cb, name):
    T = xbc.shape[0]
    tm = min(T, 1024)
    K = SSM_CONV
    q = tm // SUBLANES

    def body(x_ref, xp_ref, w_ref, b_ref, o_ref):
        prev8 = jnp.where(pl.program_id(0) > 0, xp_ref[...], 0.0)
        o_ref[...] = _silu(_conv_rows(x_ref[...], prev8, w_ref[...], b_ref[...], K))

    tc = 512
    return pl.pallas_call(
        body, grid=(T // tm, SSM_CONV_DIM // tc),
        in_specs=[pl.BlockSpec((tm, tc), lambda i, j: (i, j)),
                  pl.BlockSpec((SUBLANES, tc), lambda i, j: (jnp.maximum(i * q - 1, 0), j)),
                  pl.BlockSpec((K, tc), lambda i, j: (0, j)), pl.BlockSpec((1, tc), lambda i, j: (0, j))],
        out_specs=pl.BlockSpec((tm, tc), lambda i, j: (i, j)), out_shape=_sds((T, SSM_CONV_DIM)),
        compiler_params=_cp("parallel", "parallel"), name=name)(xbc, xbc, cw, cb)


def _ssm_pre_bwd(xbc, cw, cb, dact_perm, name):
    T = xbc.shape[0]
    tm = min(T, 1024)
    nt = T // tm
    K = SSM_CONV
    q = tm // SUBLANES
    tc = SSM_TC

    def body(x_ref, xp_ref, xn_ref, d_ref, dn_ref, w_ref, b_ref, dx_ref, dw_ref, db_ref):
        i = pl.program_id(1)
        w = w_ref[...]
        b = b_ref[...]
        cur = x_ref[...]
        prev8 = jnp.where(i > 0, xp_ref[...], 0.0)
        nxt8 = xn_ref[...]
        d_cur = d_ref[...] * _dsilu(_conv_rows(cur, prev8, w, b, K))
        d_nxt = jnp.where(i == nt - 1, 0.0, dn_ref[...] * _dsilu(_conv_rows(nxt8, cur[tm - SUBLANES:], w, b, K)))
        ups = [d_cur] + [_shift_up(d_cur, d_nxt, s) for s in range(1, K)]
        dx = ups[0] * w[K - 1:K, :]
        for s in range(1, K):
            dx = dx + ups[s] * w[K - 1 - s:K - s, :]
        dx_ref[...] = dx.astype(dx_ref.dtype)
        dwp = jnp.concatenate([jnp.sum(ups[K - 1 - k] * cur, axis=0, keepdims=True) for k in range(K)], axis=0)
        dbp = jnp.sum(d_cur, axis=0, keepdims=True)

        @pl.when(i == 0)
        def _():
            dw_ref[...] = dwp
            db_ref[...] = dbp

        @pl.when(i > 0)
        def _():
            dw_ref[...] += dwp
            db_ref[...] += dbp

    nxt_row = lambda i: jnp.minimum((i + 1) * q, nt * q - 1)
    return pl.pallas_call(
        body, grid=(SSM_CONV_DIM // tc, nt),
        in_specs=[pl.BlockSpec((tm, tc), lambda j, i: (i, j)),
                  pl.BlockSpec((SUBLANES, tc), lambda j, i: (jnp.maximum(i * q - 1, 0), j)),
                  pl.BlockSpec((SUBLANES, tc), lambda j, i: (nxt_row(i), j)),
                  pl.BlockSpec((tm, tc), lambda j, i: (i, _perm_col(j))),
                  pl.BlockSpec((SUBLANES, tc), lambda j, i: (nxt_row(i), _perm_col(j))),
                  pl.BlockSpec((K, tc), lambda j, i: (0, j)), pl.BlockSpec((1, tc), lambda j, i: (0, j))],
        out_specs=[pl.BlockSpec((tm, tc), lambda j, i: (i, j)), pl.BlockSpec((K, tc), lambda j, i: (0, j)),
                   pl.BlockSpec((1, tc), lambda j, i: (0, j))],
        out_shape=[_sds((T, SSM_CONV_DIM), MXU), _sds((K, SSM_CONV_DIM)), _sds((1, SSM_CONV_DIM))],
        compiler_params=_cp("parallel", "arbitrary"), name=name)(xbc, xbc, xbc, dact_perm, dact_perm, cw, cb)


def _dot_hi(a, b):
    return jnp.dot(a, b, precision=HI, preferred_element_type=F32)


def _ssd_common(dtraw, bias, alog):
    L = SSM_CHUNK
    xb = dtraw + bias
    dt = jnp.maximum(xb, 0.0) + jnp.log1p(jnp.exp(-jnp.abs(xb)))
    A = -jnp.exp(alog)
    tril = (_iota((L, L), 1) <= _iota((L, L), 0)).astype(F32)
    acs = _dot_hi(tril, dt * A)
    return xb, dt, A, tril, acs


def _head_selectors():
    es = (_iota((LANES, SSM_D_INNER), 0) == _iota((LANES, SSM_D_INNER), 1) // HEAD_DIM).astype(BF16)
    est = (_iota((SSM_D_INNER, LANES), 1) == _iota((SSM_D_INNER, LANES), 0) // HEAD_DIM).astype(BF16)
    return es, est


def _dot_sel(v, sel):
    hi = v.astype(BF16)
    r1 = v - hi.astype(F32)
    mid = r1.astype(BF16)
    lo = (r1 - mid.astype(F32)).astype(BF16)
    d = lambda a: jnp.dot(a, sel, preferred_element_type=F32)
    return (d(hi) + d(mid)) + d(lo)


def _expand_heads(v, es):
    return _dot_sel(v, es)


def _reduce_heads(q, est):
    return _dot_sel(q, est)


def _per_state_row(v, g):
    return jnp.concatenate([jnp.broadcast_to(v[:, GQ * g + r:GQ * g + r + 1], (HEAD_DIM, 1)) for r in range(GQ)], axis=0)


def _ssd_fwd(xact, dtraw, dt_bias, a_log, name):
    T = xact.shape[0]
    nc = T // SSM_CHUNK
    L = SSM_CHUNK
    BO, CO = SSM_D_INNER, SSM_D_INNER + SSM_GROUPS * SSM_STATE

    def body(x_ref, dt_ref, bias_ref, al_ref, es_ref, y_ref, st_ref, state):
        @pl.when(pl.program_id(0) == 0)
        def _():
            state[...] = jnp.zeros(state.shape, F32)

        _, dt, A, tril, acs = _ssd_common(dt_ref[...], bias_ref[...], al_ref[...])
        acsT = acs.T
        last = acs[L - 1:L, :]
        cd = jnp.exp(last)
        es = es_ref[...]
        dtX = _expand_heads(dt, es)
        EX = _expand_heads(jnp.exp(acs), es)
        decX = _expand_heads(jnp.exp(last - acs), es)
        for g in range(SSM_GROUPS):
            gs = slice(g * GROUP_W, (g + 1) * GROUP_W)
            B = x_ref[:, BO + g * SSM_STATE:BO + (g + 1) * SSM_STATE]
            C = x_ref[:, CO + g * SSM_STATE:CO + (g + 1) * SSM_STATE]
            X = x_ref[:, gs] * dtX[:, gs]
            CB = _nt(C, B)
            yd = []
            for r in range(GQ):
                h = GQ * g + r
                Lm = jnp.exp(jnp.where(tril > 0, acs[:, h:h + 1] - acsT[h:h + 1, :], NEG))
                yd.append(_nn(CB * Lm, X[:, r * HEAD_DIM:(r + 1) * HEAD_DIM]))
            S = state[g]
            st_ref[g] = S
            y_ref[:, gs] = jnp.concatenate(yd, axis=1) + _nt(C, S) * EX[:, gs]
            state[g] = S * _per_state_row(cd, g) + _tn(X * decX[:, gs], B)

    es, _ = _head_selectors()
    return pl.pallas_call(
        body, grid=(nc,),
        in_specs=[pl.BlockSpec((L, SSM_CONV_DIM), lambda c: (c, 0)), pl.BlockSpec((L, LANES), lambda c: (c, 0)),
                  pl.BlockSpec((1, LANES), lambda c: (0, 0)), pl.BlockSpec((1, LANES), lambda c: (0, 0)),
                  pl.BlockSpec((LANES, SSM_D_INNER), lambda c: (0, 0))],
        out_specs=[pl.BlockSpec((L, SSM_D_INNER), lambda c: (c, 0)),
                   pl.BlockSpec((None, SSM_GROUPS, GROUP_W, SSM_STATE), lambda c: (c, 0, 0, 0))],
        out_shape=[_sds((T, SSM_D_INNER)), _sds((nc, SSM_GROUPS, GROUP_W, SSM_STATE))],
        scratch_shapes=[pltpu.VMEM((SSM_GROUPS, GROUP_W, SSM_STATE), F32)],
        compiler_params=_cp("arbitrary"), name=name)(xact, dtraw, dt_bias, a_log, es)


def _ssd_bwd(xact, dtraw, dt_bias, a_log, d_skip, states, dy, name):
    T = xact.shape[0]
    nc = T // SSM_CHUNK
    L = SSM_CHUNK
    BO, CO = SSM_D_INNER, SSM_D_INNER + SSM_GROUPS * SSM_STATE

    def body(x_ref, dt_ref, bias_ref, al_ref, dsk_ref, es_ref, est_ref, st_ref, dy_ref,
             dxp_ref, ddt_ref, dbias_ref, dal_ref, dd_ref, dstate, qa, qx):
        cc = pl.program_id(0)

        @pl.when(cc == 0)
        def _():
            dstate[...] = jnp.zeros(dstate.shape, F32)

        xb, dt, A, tril, acs = _ssd_common(dt_ref[...], bias_ref[...], al_ref[...])
        acsT = acs.T
        last = acs[L - 1:L, :]
        cd = jnp.exp(last)
        es, est = es_ref[...], est_ref[...]
        dtX = _expand_heads(dt, es)
        EX = _expand_heads(jnp.exp(acs), es)
        decX = _expand_heads(jnp.exp(last - acs), es)
        lane1 = _iota((1, LANES), 1)
        lane = _iota((L, LANES), 1)
        sub = _iota((L, LANES), 0)
        ztot = jnp.zeros((1, LANES), F32)
        wrow = jnp.zeros((L, LANES), F32)
        wcolT = jnp.zeros((LANES, L), F32)
        rows_dec, rows_dd = [], []
        for g in range(SSM_GROUPS):
            gs = slice(g * GROUP_W, (g + 1) * GROUP_W)
            x = x_ref[:, gs]
            B = x_ref[:, BO + g * SSM_STATE:BO + (g + 1) * SSM_STATE]
            C = x_ref[:, CO + g * SSM_STATE:CO + (g + 1) * SSM_STATE]
            dY = dy_ref[:, gs]
            dtx, e_x, dec_x = dtX[:, gs], EX[:, gs], decX[:, gs]
            X = x * dtx
            CB = _nt(C, B)
            S = st_ref[g]
            dS_out = dstate[g]
            dcb_sum = jnp.zeros((L, L), F32)
            dxd = []
            for r in range(GQ):
                h = GQ * g + r
                hs = slice(r * HEAD_DIM, (r + 1) * HEAD_DIM)
                Lm = jnp.exp(jnp.where(tril > 0, acs[:, h:h + 1] - acsT[h:h + 1, :], NEG))
                M = CB * Lm
                dM = _nt(dY[:, hs], X[:, hs])
                dxd.append(_tn(M, dY[:, hs]))
                dcb_sum = dcb_sum + dM * Lm
                Wm = dM * M
                wrow = jnp.where(lane == h, jnp.sum(Wm, axis=1, keepdims=True), wrow)
                wcolT = jnp.where(sub == h, jnp.sum(Wm, axis=0, keepdims=True), wcolT)
            dXd = jnp.concatenate(dxd, axis=1)
            G = _nt(C, S)
            dG = dY * e_x
            dDX = _nt(B, dS_out)
            dX = dXd + dec_x * dDX
            t_dec = dDX * X * dec_x
            qa[:, gs] = dG * G - t_dec
            qx[:, gs] = dX * x
            rows_dec.append(jnp.sum(t_dec, axis=0, keepdims=True))
            rows_dd.append(jnp.sum(dY * x, axis=0, keepdims=True))
            zc = jnp.sum(dS_out * S, axis=1, keepdims=True)
            for r in range(GQ):
                ztot = jnp.where(lane1 == GQ * g + r, jnp.sum(zc[r * HEAD_DIM:(r + 1) * HEAD_DIM], axis=0, keepdims=True), ztot)
            dxp_ref[:, g * PERM_W:g * PERM_W + GROUP_W] = dX * dtx + dY * dsk_ref[:, gs]
            dxp_ref[:, g * PERM_W + GROUP_W:g * PERM_W + GROUP_W + SSM_STATE] = _tn(dcb_sum, C) + _nn(X * dec_x, dS_out)
            dxp_ref[:, g * PERM_W + GROUP_W + SSM_STATE:(g + 1) * PERM_W] = _nn(dcb_sum, B) + _nn(dG, S)
            dstate[g] = dS_out * _per_state_row(cd, g) + _tn(dG, C)
        rows = jnp.concatenate([jnp.concatenate(rows_dec, axis=1), jnp.concatenate(rows_dd, axis=1)]
                               + [jnp.zeros((SUBLANES - 2, SSM_D_INNER), F32)], axis=0)
        rsum = _reduce_heads(rows, est)
        dlast = rsum[0:1, :] + cd * ztot
        dacs = (wrow - wcolT.T) + _reduce_heads(qa[...], est) + jnp.where(sub == L - 1, dlast, 0.0)
        triu = (_iota((L, L), 0) <= _iota((L, L), 1)).astype(F32)
        da = _dot_hi(triu, dacs)
        ddtraw = (da * A + _reduce_heads(qx[...], est)) * (1.0 / (1.0 + jnp.exp(-xb)))
        ddt_ref[...] = ddtraw
        dal = jnp.sum(da * dt, axis=0, keepdims=True) * A
        ddp = rsum[1:2, :]
        dbp = jnp.sum(ddtraw, axis=0, keepdims=True)

        @pl.when(cc == 0)
        def _():
            dbias_ref[...] = dbp
            dal_ref[...] = dal
            dd_ref[...] = ddp

        @pl.when(cc > 0)
        def _():
            dbias_ref[...] += dbp
            dal_ref[...] += dal
            dd_ref[...] += ddp

    rc = lambda c: nc - 1 - c
    vec = pl.BlockSpec((1, LANES), lambda c: (0, 0))
    es, est = _head_selectors()
    return pl.pallas_call(
        body, grid=(nc,),
        in_specs=[pl.BlockSpec((L, SSM_CONV_DIM), lambda c: (rc(c), 0)), pl.BlockSpec((L, LANES), lambda c: (rc(c), 0)), vec, vec,
                  pl.BlockSpec((1, SSM_D_INNER), lambda c: (0, 0)),
                  pl.BlockSpec((LANES, SSM_D_INNER), lambda c: (0, 0)), pl.BlockSpec((SSM_D_INNER, LANES), lambda c: (0, 0)),
                  pl.BlockSpec((None, SSM_GROUPS, GROUP_W, SSM_STATE), lambda c: (rc(c), 0, 0, 0)),
                  pl.BlockSpec((L, SSM_D_INNER), lambda c: (rc(c), 0))],
        out_specs=[pl.BlockSpec((L, SSM_GROUPS * PERM_W), lambda c: (rc(c), 0)),
                   pl.BlockSpec((L, LANES), lambda c: (rc(c), 0)), vec, vec, vec],
        out_shape=[_sds((T, SSM_GROUPS * PERM_W)), _sds((T, LANES)), _sds((1, LANES)), _sds((1, LANES)), _sds((1, LANES))],
        scratch_shapes=[pltpu.VMEM((SSM_GROUPS, GROUP_W, SSM_STATE), F32), pltpu.VMEM((L, SSM_D_INNER), F32),
                        pltpu.VMEM((L, SSM_D_INNER), F32)],
        compiler_params=_cp("arbitrary"), name=name)(xact, dtraw, dt_bias, a_log, d_skip, es, est, states, dy)


def _ssm_post_fwd(y, xact, z, d_skip, nw, name):
    T = y.shape[0]
    tm = min(T, 256)
    W = SSM_D_INNER

    def body(y_ref, x_ref, z_ref, d_ref, w_ref, o_ref):
        y2 = (y_ref[...] + d_ref[...] * x_ref[...]) * _silu(z_ref[...])
        r = lax.rsqrt(jnp.mean(y2 * y2, axis=-1, keepdims=True) + SSM_NORM_EPS)
        o_ref[...] = (y2 * r * w_ref[...]).astype(o_ref.dtype)

    row = pl.BlockSpec((tm, W), lambda i: (i, 0))
    vec = pl.BlockSpec((1, W), lambda i: (0, 0))
    return pl.pallas_call(
        body, grid=(T // tm,), in_specs=[row, row, row, vec, vec], out_specs=row, out_shape=_sds((T, W), MXU),
        compiler_params=_cp("parallel"), name=name)(y, xact, z, d_skip, nw)


def _ssm_post_bwd(y, xact, z, d_skip, nw, dyn, name):
    T = y.shape[0]
    tm = min(T, 256)
    W = SSM_D_INNER

    def body(y_ref, x_ref, z_ref, d_ref, w_ref, dn_ref, dyg_ref, dz_ref, dw_ref):
        zv = z_ref[...]
        sz = _silu(zv)
        yg = y_ref[...] + d_ref[...] * x_ref[...]
        y2 = yg * sz
        r = lax.rsqrt(jnp.mean(y2 * y2, axis=-1, keepdims=True) + SSM_NORM_EPS)
        y2h = y2 * r
        dn = dn_ref[...]
        gy = dn * w_ref[...]
        dy2 = r * (gy - y2h * jnp.mean(gy * y2h, axis=-1, keepdims=True))
        dyg_ref[...] = dy2 * sz
        dz_ref[...] = (dy2 * yg * _dsilu(zv)).astype(dz_ref.dtype)
        part = jnp.sum(dn * y2h, axis=0, keepdims=True)

        @pl.when(pl.program_id(0) == 0)
        def _():
            dw_ref[...] = part

        @pl.when(pl.program_id(0) > 0)
        def _():
            dw_ref[...] += part

    row = pl.BlockSpec((tm, W), lambda i: (i, 0))
    vec = pl.BlockSpec((1, W), lambda i: (0, 0))
    return pl.pallas_call(
        body, grid=(T // tm,), in_specs=[row, row, row, vec, vec, row], out_specs=[row, row, vec],
        out_shape=[_sds((T, W)), _sds((T, W), MXU), _sds((1, W))],
        compiler_params=_cp("arbitrary"), name=name)(y, xact, z, d_skip, nw, dyn)


def _local_step(x0, cos, sin_s, target, P, fetch, token, send):
    mmf = functools.partial(_mm, tm=1024)
    big, small = {}, {}
    P = dict(P, wup={}, wdn={}, fcw={})
    h0 = _rmsnorm_fwd(x0, P["nm"][0], "norm_mix0", token=token)
    proj0 = mmf(h0, P["wmiT"], tb=True, tn=1280, tk=1024, name="mix_in")
    cat, attn, lse = _mixcore_fwd(proj0, cos, sin_s, P["pool_w"], P["pool_scale"], P["sinks"], "mixcore_fwd")
    x1 = mmf(cat, P["wmo"], tn=1024, tk=1024, res=x0, name="mix_out")

    def ffn_fwd(xin, i):
        hf = _rmsnorm_fwd(xin, P["nf"][i], f"norm_ffn{i}")
        got = fetch(f"ffn{i}", hf)
        P["wup"][i], P["wdn"][i], P["fcw"][i] = got["wup"], got["wdn"], got["fcw"]
        hid = mmf(hf, P["wup"][i], tn=1408, tk=1024, out_dtype=MXU, name=f"ffn_up{i}")
        act = _ffn_mid_fwd(hid, P["fcw"][i], P["fcb"][i], f"ffn_mid_fwd{i}")
        xout = mmf(act, P["wdn"][i], tn=1024, tk=D_FF, res=xin, name=f"ffn_down{i}")
        return hf, hid, act, xout

    hf0, hid0, act0, x2 = ffn_fwd(x1, 0)
    h1 = _rmsnorm_fwd(x2, P["nm"][1], "norm_mix1")
    P.update(fetch("ssm", h1))
    z = mmf(h1, P["wzT"], tb=True, tn=1024, tk=1024, name="ssm_in_z")
    xbc = mmf(h1, P["wxbcT"], tb=True, tn=1024, tk=1024, name="ssm_in_xbc")
    dtraw = mmf(h1, P["wdtT"], tb=True, tn=128, tk=1024, name="ssm_in_dt")
    xact = _ssm_pre_fwd(xbc, P["scw"], P["scb"], "ssm_pre_fwd")
    y, states = _ssd_fwd(xact, dtraw, P["dt_bias"], P["a_log"], "ssd_fwd")
    yn = _ssm_post_fwd(y, xact, z, P["d_exp"], P["snorm"], "ssm_post_fwd")
    x3 = mmf(yn, P["wso"], tn=1024, tk=SSM_D_INNER, res=x2, name="ssm_out")
    hf1, hid1, act1, x4 = ffn_fwd(x3, 1)
    loss_row, dx4, d_nfin = _loss_head(x4, P["nfin"], target, "loss_head")
    small["norm_final"] = d_nfin

    def ffn_bwd(xin, dxo, hf, hid, act, i):
        da = mmf(dxo, P["wdn"][i], tb=True, tn=1408, tk=1024, name=f"ffn_down_dx{i}")
        big[f"ffn_w_down{i}"] = dwf(act, dxo, tm=1408, tn=1024, name=f"ffn_down_dw{i}").reshape(N_CHIPS, D_FF // N_CHIPS, D_MODEL)
        dhid, dcw, dcb = _ffn_mid_bwd(hid, P["fcw"][i], P["fcb"][i], da, f"ffn_mid_bwd{i}")
        dhf = mmf(dhid, P["wup"][i], tb=True, tn=1024, tk=2816, name=f"ffn_up_dx{i}")
        big[f"ffn_w_up{i}"] = dwf(hf, dhid, tm=1024, tn=1408, out_shard_perm=(0, 2, 1, 3), name=f"ffn_up_dw{i}")
        tok = send(f"ffn{i}", [big[f"ffn_w_up{i}"], big[f"ffn_w_down{i}"]])
        dxi, dnf = _rmsnorm_bwd(xin, P["nf"][i], dhf, dxo, f"norm_ffn_bwd{i}", token=tok)
        return dxi, dnf, dcw, dcb

    dwf = functools.partial(_mm, ta=True, tk=2048, out_dtype=BF16)
    dx3, dnf1, dfcw1, dfcb1 = ffn_bwd(x3, dx4, hf1, hid1, act1, 1)
    dyn = mmf(dx3, P["wso"], tb=True, tn=1024, tk=1024, name="ssm_out_dx")
    big["ssm_w_out"] = dwf(yn, dx3, tm=1024, tn=1024, name="ssm_out_dw").reshape(N_CHIPS, SSM_D_INNER // N_CHIPS, D_MODEL)
    dyg, dz, d_snorm = _ssm_post_bwd(y, xact, z, P["d_exp"], P["snorm"], dyn, "ssm_post_bwd")
    dxact_p, ddtraw, d_dtb, d_alog, d_dskip = _ssd_bwd(xact, dtraw, P["dt_bias"], P["a_log"], P["d_exp"], states, dyg, "ssd_bwd")
    dxbc, d_scw, d_scb = _ssm_pre_bwd(xbc, P["scw"], P["scb"], dxact_p, "ssm_pre_bwd")
    dh1 = mmf(dz, P["wzT"], tn=1024, tk=2048, name="ssm_in_dx_z")
    dh1 = mmf(dxbc, P["wxbcT"], tn=1024, tk=2048, res=dh1, name="ssm_in_dx_xbc")
    dh1 = mmf(ddtraw, P["wdtT"], tn=1024, tk=128, res=dh1, name="ssm_in_dx_dt")
    dwz = dwf(dz, h1, tm=1024, tn=1024, name="ssm_in_dw_z")
    dwxbc = dwf(dxbc, h1, tm=1024, tn=1024, name="ssm_in_dw_xbc")
    dwdt = dwf(ddtraw, h1, tm=128, tn=1024, name="ssm_in_dw_dt")
    dwsi = jnp.concatenate([dwz, dwxbc, dwdt[:SSM_HEADS]], axis=0)
    big["ssm_w_in"] = dwsi.reshape(N_CHIPS, SSM_IN_DIM // N_CHIPS, D_MODEL)
    tok = send("ssm", [big["ssm_w_in"], big["ssm_w_out"]])
    dx2, dnm1 = _rmsnorm_bwd(x2, P["nm"][1], dh1, dx3, "norm_mix_bwd1", token=tok)
    dx1, dnf0, dfcw0, dfcb0 = ffn_bwd(x1, dx2, hf0, hid0, act0, 0)
    dcat = mmf(dx1, P["wmo"], tb=True, tn=1024, tk=1024, name="mix_out_dx")
    big["mix_w_out"] = dwf(cat, dx1, tm=1024, tn=1024, name="mix_out_dw").reshape(N_CHIPS, D_MODEL // N_CHIPS, D_MODEL)
    dproj0, d_pw, d_ps, d_sk = _mixcore_bwd(proj0, cos, sin_s, P["pool_w"], P["pool_scale"], P["sinks"], attn, lse, dcat, "mixcore_bwd")
    dh0 = mmf(dproj0, P["wmiT"], tn=1024, tk=1280, name="mix_in_dx")
    big["mix_w_in"] = dwf(dproj0, h0, tm=1280, tn=1024, name="mix_in_dw").reshape(N_CHIPS, MIX_IN_DIM // N_CHIPS, D_MODEL)
    tok = send("mix", [big["mix_w_in"], big["mix_w_out"]])
    dx0, dnm0 = _rmsnorm_bwd(x0, P["nm"][0], dh0, dx1, "norm_mix_bwd0", token=tok)

    def unperm_cols(a):
        r = a.shape[0]
        t = a.reshape(r, N_CHIPS, FFN_TC)
        return jnp.stack([t[:, p] for p in _PERM], axis=0)

    small["norm_mix"] = jnp.concatenate([dnm0, dnm1], axis=0)
    small["norm_ffn"] = jnp.concatenate([dnf0, dnf1], axis=0)
    small["pool_w"] = d_pw.reshape(4 * POOL_GROUP, POOL_GROUP)
    small["pool_scale"] = d_ps
    small["attn_sinks"] = d_sk
    small["ssm_dt_bias"] = d_dtb
    small["ssm_A_log"] = d_alog
    small["ssm_D"] = d_dskip
    fcb = jnp.stack([unperm_cols(dfcb0), unperm_cols(dfcb1)], axis=0)
    small["ffn_conv_b"] = fcb.reshape(2, 2 * D_FF)
    small["ssm_conv_w"] = d_scw.reshape(SSM_CONV, N_CHIPS, SSM_CONV_DIM // N_CHIPS).transpose(1, 0, 2)
    small["ssm_conv_b"] = d_scb.reshape(N_CHIPS, 1, SSM_CONV_DIM // N_CHIPS)
    small["ssm_norm"] = d_snorm.reshape(N_CHIPS, 1, SSM_D_INNER // N_CHIPS)
    small["ffn_conv_w"] = jnp.concatenate([unperm_cols(dfcw0), unperm_cols(dfcw1)], axis=1)
    return loss_row, dx0, big, small


ANY = pl.BlockSpec(memory_space=pl.ANY)


def _place():
    return lax.axis_index("x"), lax.axis_index("y"), lax.axis_index("c")


def _gather_shards(shards, name):
    n = len(shards)
    split = [s.size >= (1 << 16) for s in shards]

    def half(ref, a, h):
        shp = shards[a].shape
        if len(shp) == 3:
            return ref.at[h]
        r2 = shp[0] // 2
        return ref.at[pl.ds(pl.multiple_of(h * r2, 2 * SUBLANES), r2), :]

    def body(*refs):
        ins, outs = refs[:n], refs[n:2 * n]
        send, recv, fsend, frecv = refs[2 * n:]
        x, y, c = _place()
        k = 2 * x + y
        chips = [(1 - x, y), (x, 1 - y), (1 - x, 1 - y)]

        def ici(a, j, src_slot_ref, dst_slot):
            px, py = chips[j]
            src = half(src_slot_ref, a, c) if split[a] else src_slot_ref
            dst = half(outs[a].at[dst_slot], a, c) if split[a] else outs[a].at[dst_slot]
            return pltpu.make_async_remote_copy(src, dst, send.at[a, j], recv.at[a, j], device_id=(px, py, c), device_id_type=MESH)

        def d2d(a, j, h):
            px, py = chips[j]
            part = half(outs[a].at[2 * px + py], a, h)
            return pltpu.make_async_remote_copy(part, part, fsend.at[a, j], frecv.at[a, j], device_id=(x, y, 1 - c), device_id_type=MESH)

        sends = [ici(a, j, ins[a], k) for a in range(n) for j in range(3)]
        for cp in sends:
            cp.start()
        passed = []
        for a in range(n):
            for j, (px, py) in enumerate(chips):
                ici(a, j, ins[a], 2 * px + py).wait_recv()
                if split[a]:
                    passed.append(d2d(a, j, c))
                    passed[-1].start()
        for a in range(n):
            if split[a]:
                for j in range(3):
                    d2d(a, j, 1 - c).wait_recv()
        for cp in sends + passed:
            cp.wait_send()

    return pl.pallas_call(
        body, in_specs=[ANY] * n, out_specs=[ANY] * n,
        out_shape=[_sds((N_CHIPS,) + s.shape, s.dtype) for s in shards],
        scratch_shapes=[pltpu.SemaphoreType.DMA((n, 3))] * 4,
        compiler_params=pltpu.CompilerParams(has_side_effects=True), name=name)(*shards)


HBM = pl.BlockSpec(memory_space=pltpu.HBM)
SEM = pl.BlockSpec(memory_space=pltpu.SEMAPHORE)
DATAFLOW = pltpu.SideEffectType.DATAFLOW_SIDE_EFFECTING


def _spread_start(groups, slot_src, after, name):
    flat = [a for grp in groups for a in grp]
    n = len(flat)
    ng = len(groups)
    offs = [sum(len(g) for g in groups[:i]) for i in range(ng)]
    lshape = [(a.shape if slot_src else (N_CHIPS,) + a.shape) for a in flat]

    nsem = 6 * n

    def body(*refs):
        src, land = refs[:n], refs[n:2 * n]
        sems = refs[2 * n + 1:2 * n + 1 + nsem]
        token = refs[-1]
        x, y, c = _place()
        k = 2 * x + y
        chips = [(1 - x, y), (x, 1 - y), (1 - x, 1 - y)]
        for a in range(n):
            for j, (px, py) in enumerate(chips):
                s = src[a].at[2 * px + py] if slot_src else src[a]
                pltpu.make_async_remote_copy(s, land[a].at[k], sems[6 * a + 2 * j], sems[6 * a + 2 * j + 1],
                                             device_id=(px, py, c), device_id_type=MESH).start()
        token[...] = jnp.zeros(token.shape, token.dtype)

    out_shape = [pltpu.SemaphoreType.DMA(())] * nsem
    out_shape += [pltpu.HBM(a.shape, a.dtype) for a in flat] + [pltpu.HBM(s, a.dtype) for s, a in zip(lshape, flat)]
    out_shape.append(_sds((SUBLANES, LANES)))
    args = [pltpu.with_memory_space_constraint(a, pltpu.HBM) for a in flat]
    args += [pltpu.with_memory_space_constraint(lax.empty(s, a.dtype), pltpu.HBM) for s, a in zip(lshape, flat)]
    res = pl.pallas_call(
        body, name=name, out_shape=tuple(out_shape), in_specs=[HBM] * (2 * n) + [pl.BlockSpec(memory_space=pl.ANY)],
        out_specs=tuple([SEM] * nsem + [HBM] * (2 * n) + [pl.BlockSpec(memory_space=pltpu.VMEM)]),
        input_output_aliases={i: nsem + i for i in range(2 * n)},
        compiler_params=pltpu.CompilerParams(has_side_effects=DATAFLOW))(*args, after)
    sems, thru, token = res[:nsem], res[nsem:nsem + 2 * n], res[-1]
    out = []
    for gi, grp in enumerate(groups):
        sl = slice(offs[gi], offs[gi] + len(grp))
        out.append((list(sems[6 * offs[gi]:6 * (offs[gi] + len(grp))]), list(thru[:n][sl]), list(thru[n:][sl])))
    return out, token


def _spread_wait(started, slot_src, after, name):
    sems, srcs, lands = started
    n = len(srcs)

    def body(*refs):
        src, land = refs[:n], refs[n:2 * n]
        sem = refs[2 * n:2 * n + 6 * n]
        x, y, c = _place()
        chips = [(1 - x, y), (x, 1 - y), (1 - x, 1 - y)]
        for a in range(n):
            for j, (px, py) in enumerate(chips):
                s = src[a].at[2 * px + py] if slot_src else src[a]
                cp = pltpu.make_async_remote_copy(s, land[a].at[2 * px + py], sem[6 * a + 2 * j], sem[6 * a + 2 * j + 1],
                                                  device_id=(px, py, c), device_id_type=MESH)
                cp.wait_send()
                cp.wait_recv()

    res = pl.pallas_call(
        body, name=name, out_shape=tuple([pltpu.HBM(a.shape, a.dtype) for a in srcs] + [pltpu.HBM(a.shape, a.dtype) for a in lands]),
        in_specs=[HBM] * (2 * n) + [SEM] * (6 * n) + [pl.BlockSpec(memory_space=pl.ANY)], out_specs=tuple([HBM] * (2 * n)),
        input_output_aliases={i: i for i in range(2 * n)},
        compiler_params=pltpu.CompilerParams(has_side_effects=DATAFLOW))(*srcs, *lands, *sems, after)
    return list(res[:n]), list(res[n:])


def _sibling_exchange(fs, name):
    n = len(fs)

    def body(*refs):
        ins, outs = refs[:n], refs[n:2 * n]
        send, recv = refs[2 * n:]
        x, y, c = _place()
        cps = [pltpu.make_async_remote_copy(ins[a], outs[a], send.at[a], recv.at[a],
                                            device_id=(x, y, 1 - c), device_id_type=MESH) for a in range(n)]
        for cp in cps:
            cp.start()
        for cp in cps:
            cp.wait()

    return pl.pallas_call(
        body, in_specs=[ANY] * n, out_specs=[ANY] * n, out_shape=[_sds(f.shape, f.dtype) for f in fs],
        scratch_shapes=[pltpu.SemaphoreType.DMA((n,)), pltpu.SemaphoreType.DMA((n,))],
        compiler_params=pltpu.CompilerParams(has_side_effects=True), name=name)(*fs)


def _tile2d(rows, cols, budget=1024 * 1024, step=2 * SUBLANES):
    fits = [t for t in range(step, rows + 1, step) if rows % t == 0 and t * cols * 4 <= budget]
    if fits:
        return fits[-1], cols
    fits = [t for t in range(LANES, cols + 1, LANES) if cols % t == 0 and rows * t * 4 <= budget]
    assert fits, (rows, cols)
    return rows, fits[-1]


def _chip_sum(own, parts, kidx, name):
    _, R, C = parts.shape
    tr, tc = _tile2d(R, C)

    def body(k_ref, o_ref_in, p1_ref, p2_ref, p3_ref, o_ref):
        o_ref[...] = ((o_ref_in[...].astype(F32) + p1_ref[...].astype(F32)) + p2_ref[...].astype(F32)) + p3_ref[...].astype(F32)

    def slot(d):
        return pl.BlockSpec((None, tr, tc), lambda i, j, k: ((k[0] + d) % N_CHIPS, i, j))

    return pl.pallas_call(
        body,
        grid_spec=pltpu.PrefetchScalarGridSpec(
            num_scalar_prefetch=1, grid=(R // tr, C // tc), in_specs=[slot(0), slot(1), slot(2), slot(3)],
            out_specs=pl.BlockSpec((tr, tc), lambda i, j, k: (i, j))),
        out_shape=_sds((R, C)), compiler_params=_cp("parallel", "parallel"), name=name)(kidx, own, parts, parts, parts)


def _adamw_math(w, g, m, v):
    m2 = ADAM_B1 * m + (1.0 - ADAM_B1) * g
    v2 = ADAM_B2 * v + (1.0 - ADAM_B2) * (g * g)
    m_hat = m2 / (1.0 - ADAM_B1 ** ADAM_STEP)
    v_hat = v2 / (1.0 - ADAM_B2 ** ADAM_STEP)
    delta = -ADAM_LR * (m_hat / (jnp.sqrt(v_hat) + ADAM_EPS) + ADAM_WD * w)
    return delta, m2, v2


def _adamw(w, m, v, gparts, name):
    Lw, R, C = w.shape
    tr, tc = _tile2d(R, C)
    flat = [h for pair in gparts for h in pair]

    def body(*refs):
        w_ref, m_ref, v_ref = refs[:3]
        g_refs = refs[3:3 + 2 * Lw]
        go_ref, d_ref, mo_ref, vo_ref = refs[3 + 2 * Lw:]
        g = g_refs[0][...] + g_refs[1][...]
        for l in range(1, Lw):
            g = jnp.where(pl.program_id(0) == l, g_refs[2 * l][...] + g_refs[2 * l + 1][...], g)
        d, m2, v2 = _adamw_math(w_ref[...], g, m_ref[...], v_ref[...])
        go_ref[...] = g
        d_ref[...] = d
        mo_ref[...] = m2
        vo_ref[...] = v2

    blk = pl.BlockSpec((None, tr, tc), lambda l, i, j: (l, i, j))
    gblk = pl.BlockSpec((tr, tc), lambda l, i, j: (i, j))
    return pl.pallas_call(
        body, grid=(Lw, R // tr, C // tc), in_specs=[blk, blk, blk] + [gblk] * (2 * Lw), out_specs=[blk] * 4,
        out_shape=[_sds((Lw, R, C))] * 4, compiler_params=_cp("parallel", "parallel", "parallel"), name=name)(w, m, v, *flat)


def _small_adamw(grads, wmv, name):
    n = len(grads)

    def body(*refs):
        g_in, p_in, outs = refs[:n], refs[n:4 * n], refs[4 * n:]
        for a in range(n):
            g = g_in[a][...]
            d_, m2, v2 = _adamw_math(p_in[3 * a][...], g, p_in[3 * a + 1][...], p_in[3 * a + 2][...])
            outs[4 * a][...] = g
            outs[4 * a + 1][...] = d_
            outs[4 * a + 2][...] = m2
            outs[4 * a + 3][...] = v2

    vm = pl.BlockSpec(memory_space=pltpu.VMEM)
    args = list(grads) + [t for tri in wmv for t in tri]
    out_shape = [_sds(g.shape) for g in grads for _ in range(4)]
    return pl.pallas_call(body, in_specs=[vm] * len(args), out_specs=[vm] * len(out_shape), out_shape=out_shape,
                          compiler_params=pltpu.CompilerParams(vmem_limit_bytes=V7X_VMEM_LIMIT), name=name)(*args)


def _small_allreduce(partials, pshapes, loss_row, name):
    n = len(partials)
    gshapes = [p.shape for p in partials] + [loss_row.shape]
    ng = n + 1

    def body(*refs):
        g_in = refs[:ng]
        outs = refs[ng:2 * ng]
        bufs = refs[2 * ng:3 * ng]
        send, recv = refs[-2:]
        x, y, c = _place()
        me = 4 * x + 2 * y + c
        k = 2 * x + y
        flips = [(fx, fy, fc) for fx in (0, 1) for fy in (0, 1) for fc in (0, 1)][1:]

        def peer(f):
            return (x ^ f[0], y ^ f[1], c ^ f[2])

        def slot(p):
            return 4 * p[0] + 2 * p[1] + p[2]

        for a in range(ng):
            bufs[a][me] = g_in[a][...]
        sends = [pltpu.make_async_remote_copy(g_in[a], bufs[a].at[me], send.at[a, j], recv.at[a, j],
                                              device_id=peer(f), device_id_type=MESH)
                 for a in range(ng) for j, f in enumerate(flips)]
        for cp in sends:
            cp.start()
        for a in range(ng):
            for j, f in enumerate(flips):
                pltpu.make_async_remote_copy(g_in[a], bufs[a].at[slot(peer(f))], send.at[a, j], recv.at[a, j],
                                             device_id=peer(f), device_id_type=MESH).wait_recv()
        for cp in sends:
            cp.wait_send()
        for a in range(ng):
            sharded = len(gshapes[a]) == 3

            def part(d):
                return bufs[a][d, k] if sharded else bufs[a][d]

            tot = part(0)
            for d in range(1, N_DEV):
                tot = tot + part(d)
            if a == n:
                outs[n][...] = tot
            else:
                pr, pc = pshapes[a]
                outs[a][...] = tot[:pr, :pc]

    vm = pl.BlockSpec(memory_space=pltpu.VMEM)
    args = list(partials) + [loss_row]
    out_shape = [_sds(ps) for ps in pshapes] + [_sds(loss_row.shape)]
    return pl.pallas_call(
        body, in_specs=[vm] * len(args), out_specs=[vm] * len(out_shape), out_shape=out_shape,
        scratch_shapes=[pltpu.VMEM((N_DEV,) + tuple(s), F32) for s in gshapes]
        + [pltpu.SemaphoreType.DMA((ng, N_DEV - 1)), pltpu.SemaphoreType.DMA((ng, N_DEV - 1))],
        compiler_params=pltpu.CompilerParams(has_side_effects=True, vmem_limit_bytes=V7X_VMEM_LIMIT), name=name)(*args)


_PERM = (0, 2, 1, 3)


def _cols_from_shards(g):
    return g.transpose(1, 0, 2).reshape(g.shape[1], N_CHIPS * g.shape[2])


def _rope_tables(positions):
    inv_freq = ROPE_THETA ** (-jnp.arange(0, HEAD_DIM, 2, dtype=F32) / HEAD_DIM)
    ang = positions.astype(F32).reshape(-1, 1) * inv_freq
    cos, sin = jnp.cos(ang), jnp.sin(ang)
    cos = jnp.concatenate([cos, cos, cos, cos], axis=-1)
    sin_s = jnp.concatenate([-sin, sin, -sin, sin], axis=-1)
    return cos, sin_s


def kernel(x, positions, norm_mix, norm_ffn, norm_final, mix_w_in, pool_w, pool_scale, attn_sinks, mix_w_out, ssm_w_in, ssm_conv_w, ssm_conv_b, ssm_dt_bias, ssm_A_log, ssm_D, ssm_norm, ssm_w_out, ffn_w_up, ffn_conv_w, ffn_conv_b, ffn_w_down, loss_target, m_norm_mix, m_norm_ffn, m_norm_final, m_mix_w_in, m_pool_w, m_pool_scale, m_attn_sinks, m_mix_w_out, m_ssm_w_in, m_ssm_conv_w, m_ssm_conv_b, m_ssm_dt_bias, m_ssm_A_log, m_ssm_D, m_ssm_norm, m_ssm_w_out, m_ffn_w_up, m_ffn_conv_w, m_ffn_conv_b, m_ffn_w_down, v_norm_mix, v_norm_ffn, v_norm_final, v_mix_w_in, v_pool_w, v_pool_scale, v_attn_sinks, v_mix_w_out, v_ssm_w_in, v_ssm_conv_w, v_ssm_conv_b, v_ssm_dt_bias, v_ssm_A_log, v_ssm_D, v_ssm_norm, v_ssm_w_out, v_ffn_w_up, v_ffn_conv_w, v_ffn_conv_b, v_ffn_w_down):
    W = dict(norm_mix=norm_mix, norm_ffn=norm_ffn, norm_final=norm_final, mix_w_in=mix_w_in, pool_w=pool_w, pool_scale=pool_scale, attn_sinks=attn_sinks, mix_w_out=mix_w_out, ssm_w_in=ssm_w_in, ssm_conv_w=ssm_conv_w, ssm_conv_b=ssm_conv_b, ssm_dt_bias=ssm_dt_bias, ssm_A_log=ssm_A_log, ssm_D=ssm_D, ssm_norm=ssm_norm, ssm_w_out=ssm_w_out, ffn_w_up=ffn_w_up, ffn_conv_w=ffn_conv_w, ffn_conv_b=ffn_conv_b, ffn_w_down=ffn_w_down)
    Mo = dict(norm_mix=m_norm_mix, norm_ffn=m_norm_ffn, norm_final=m_norm_final, mix_w_in=m_mix_w_in, pool_w=m_pool_w, pool_scale=m_pool_scale, attn_sinks=m_attn_sinks, mix_w_out=m_mix_w_out, ssm_w_in=m_ssm_w_in, ssm_conv_w=m_ssm_conv_w, ssm_conv_b=m_ssm_conv_b, ssm_dt_bias=m_ssm_dt_bias, ssm_A_log=m_ssm_A_log, ssm_D=m_ssm_D, ssm_norm=m_ssm_norm, ssm_w_out=m_ssm_w_out, ffn_w_up=m_ffn_w_up, ffn_conv_w=m_ffn_conv_w, ffn_conv_b=m_ffn_conv_b, ffn_w_down=m_ffn_w_down)
    Vo = dict(norm_mix=v_norm_mix, norm_ffn=v_norm_ffn, norm_final=v_norm_final, mix_w_in=v_mix_w_in, pool_w=v_pool_w, pool_scale=v_pool_scale, attn_sinks=v_attn_sinks, mix_w_out=v_mix_w_out, ssm_w_in=v_ssm_w_in, ssm_conv_w=v_ssm_conv_w, ssm_conv_b=v_ssm_conv_b, ssm_dt_bias=v_ssm_dt_bias, ssm_A_log=v_ssm_A_log, ssm_D=v_ssm_D, ssm_norm=v_ssm_norm, ssm_w_out=v_ssm_w_out, ffn_w_up=v_ffn_w_up, ffn_conv_w=v_ffn_conv_w, ffn_conv_b=v_ffn_conv_b, ffn_w_down=v_ffn_w_down)

    kchip = 2 * lax.axis_index("x") + lax.axis_index("y")

    def own_slot(g, own):
        return lax.dynamic_update_slice_in_dim(g, own[None], kchip, axis=0)

    def tr(t):
        return jnp.swapaxes(t[0], 0, 1)

    later = dict(ffn0=[ffn_w_up[0].astype(MXU), ffn_w_down[0].astype(MXU)],
                 ssm=[tr(ssm_w_in).astype(MXU), ssm_w_out[0].astype(MXU)],
                 ffn1=[ffn_w_up[1].astype(MXU), ffn_w_down[1].astype(MXU)])
    sh = [tr(mix_w_in).astype(MXU), mix_w_out[0].astype(MXU), ssm_conv_w[0], ssm_conv_b, ssm_norm, ffn_conv_w]
    first = _gather_shards(sh, "gather_first")
    g_mi, g_mo, g_scw, g_scb, g_sn, g_fcw = [own_slot(g, own) for g, own in zip(first, sh)]
    started, token = _spread_start(list(later.values()), False, first[0], "gather_start")
    started = dict(zip(later.keys(), started))
    fcw = [jnp.concatenate([g_fcw[p, i] for p in _PERM], axis=1) for i in range(2)]
    P = dict(
        nm=norm_mix, nf=norm_ffn, nfin=norm_final,
        wmiT=g_mi.reshape(MIX_IN_DIM, D_MODEL), wmo=g_mo.reshape(D_MODEL, D_MODEL),
        pool_w=pool_w[0], pool_scale=pool_scale, sinks=attn_sinks[0],
        scw=_cols_from_shards(g_scw), scb=g_scb.reshape(1, SSM_CONV_DIM), snorm=g_sn.reshape(1, SSM_D_INNER),
        dt_bias=jnp.pad(ssm_dt_bias, ((0, 0), (0, LANES - SSM_HEADS))), a_log=jnp.pad(ssm_A_log, ((0, 0), (0, LANES - SSM_HEADS))),
        d_exp=jnp.repeat(ssm_D, SSM_D_INNER // SSM_HEADS, axis=1),
        fcb=[jnp.concatenate([ffn_conv_b[i:i + 1, p * FFN_TC:(p + 1) * FFN_TC] for p in _PERM], axis=1) for i in range(2)],
    )

    def fetch(group, after):
        owns, lands = _spread_wait(started[group], False, after, f"gather_wait_{group}")
        a, b = [own_slot(g, own) for g, own in zip(lands, owns)]
        if group == "ssm":
            wsi = a.reshape(SSM_IN_DIM, D_MODEL)
            zx = SSM_D_INNER + SSM_CONV_DIM
            return dict(wzT=wsi[:SSM_D_INNER], wxbcT=wsi[SSM_D_INNER:zx],
                        wdtT=jnp.pad(wsi[zx:], ((0, LANES - SSM_HEADS), (0, 0))), wso=b.reshape(SSM_D_INNER, D_MODEL))
        i = int(group[-1])
        return dict(wup=jnp.concatenate([a[p] for p in _PERM], axis=1), wdn=b.reshape(D_FF, D_MODEL), fcw=fcw[i])

    cos, sin_s = _rope_tables(positions)
    sent = {}

    def send(group, grads):
        res, tok = _spread_start([grads], True, jnp.zeros((SUBLANES, LANES), F32), f"grad_start_{group}")
        sent[group] = res[0]
        return tok

    loss_row, grad_x, big, small = _local_step(x[0], cos, sin_s, loss_target[0], P, fetch, token, send)

    kidx = kchip.astype(jnp.int32).reshape(1)
    group_names = dict(ffn1=["ffn_w_up1", "ffn_w_down1"], ssm=["ssm_w_in", "ssm_w_out"], ffn0=["ffn_w_up0", "ffn_w_down0"],
                       mix=["mix_w_in", "mix_w_out"])
    names, mine = [], []
    for group, started_g in sent.items():
        grads, lands = _spread_wait(started_g, True, grad_x, f"grad_wait_{group}")
        for nm, g, land in zip(group_names[group], grads, lands):
            names.append(nm)
            mine.append(_chip_sum(g, land, kidx, f"chip_sum_{nm}"))
    theirs = _sibling_exchange(mine, "sibling_exchange")
    red = {nm: (a, b) for nm, a, b in zip(names, mine, theirs)}

    out = {}

    def big_update(pname, gparts, transposed=False):
        w = W[pname]
        lw = len(gparts)
        shp = w.shape
        rr, cc = gparts[0][0].shape
        fix = (lambda t: tr(t)[None]) if transposed else (lambda t: t.reshape(lw, rr, cc))
        res = _adamw(fix(w), fix(Mo[pname]), fix(Vo[pname]), gparts, f"adamw_{pname}")
        out[pname] = tuple((tr(r)[None] if transposed else r.reshape(shp)) for r in res)

    big_update("mix_w_in", [red["mix_w_in"]], transposed=True)
    big_update("mix_w_out", [red["mix_w_out"]])
    big_update("ssm_w_in", [red["ssm_w_in"]], transposed=True)
    big_update("ssm_w_out", [red["ssm_w_out"]])
    big_update("ffn_w_up", [red["ffn_w_up0"], red["ffn_w_up1"]])
    big_update("ffn_w_down", [red["ffn_w_down0"], red["ffn_w_down1"]])

    small_names = ["norm_mix", "norm_ffn", "norm_final", "pool_w", "pool_scale", "attn_sinks", "ssm_dt_bias", "ssm_A_log",
                   "ssm_D", "ffn_conv_b", "ssm_conv_w", "ssm_conv_b", "ssm_norm", "ffn_conv_w"]

    def as2d(t):
        if t.ndim == 1:
            return t.reshape(1, -1)
        return t.reshape(-1, t.shape[-1])

    wmv = [(as2d(W[nm]), as2d(Mo[nm]), as2d(Vo[nm])) for nm in small_names]
    summed = _small_allreduce([small[nm] for nm in small_names], [t[0].shape for t in wmv], loss_row, "small_allreduce")
    res = _small_adamw(summed[:-1], wmv, "small_adamw")
    for a, nm in enumerate(small_names):
        out[nm] = tuple(r.reshape(W[nm].shape) for r in res[4 * a:4 * a + 4])
    loss = summed[-1][0, 0]

    order = ["norm_mix", "norm_ffn", "norm_final", "mix_w_in", "pool_w", "pool_scale", "attn_sinks", "mix_w_out", "ssm_w_in",
             "ssm_conv_w", "ssm_conv_b", "ssm_dt_bias", "ssm_A_log", "ssm_D", "ssm_norm", "ssm_w_out", "ffn_w_up", "ffn_conv_w",
             "ffn_conv_b", "ffn_w_down"]
    return (loss, grad_x.reshape(x.shape), *[out[nm][0] for nm in order], *[out[nm][1] for nm in order],
            *[out[nm][2] for nm in order], *[out[nm][3] for nm in order])
```

```python
import functools

import jax
import jax.numpy as jnp
from jax import lax
from jax.experimental import pallas as pl
from jax.experimental.pallas import tpu as pltpu

F32 = jnp.float32
BF16 = jnp.bfloat16
MXU = BF16
HI = lax.Precision.HIGHEST

D_MODEL = 1024
POOL_WINDOWS = (2, 4, 8, 16)
POOL_DIM = 512
POOL_GROUP = 128
HEAD_DIM = 64
N_HEADS = 8
N_KV_HEADS = 2
GQ = 4
Q_DIM = 512
KV_DIM = 128
BLOCK = 128
ROPE_THETA = 10000.0
MIX_IN_DIM = 1280
SSM_D_INNER = 2048
SSM_HEADS = 32
SSM_GROUPS = 8
SSM_STATE = 128
SSM_CONV = 4
SSM_CHUNK = 128
SSM_CONV_DIM = 4096
SSM_IN_DIM = 6176
D_FF = 2816
FFN_CONV = 3
NORM_EPS = 1e-6
SSM_NORM_EPS = 1e-5
ADAM_LR = 0.001
ADAM_B1 = 0.9
ADAM_B2 = 0.999
ADAM_EPS = 1e-08
ADAM_WD = 0.01
ADAM_STEP = 10

N_CHIPS = 4
N_DEV = 8
LANES = 128
SUBLANES = 8
V7X_VMEM_LIMIT = 56 * 1024 * 1024
NEG = -1e30
MESH = pl.DeviceIdType.MESH


def _cp(*sem):
    return pltpu.CompilerParams(dimension_semantics=sem if sem else None, vmem_limit_bytes=V7X_VMEM_LIMIT)


def _sds(shape, dtype=F32):
    return jax.ShapeDtypeStruct(tuple(shape), dtype)


def _iota(shape, dim):
    return lax.broadcasted_iota(jnp.int32, shape, dim)


def _silu(x):
    return x * (1.0 / (1.0 + jnp.exp(-x)))


def _dsilu(x):
    s = 1.0 / (1.0 + jnp.exp(-x))
    return s * (1.0 + x * (1.0 - s))


def _mm(a, b, *, ta=False, tb=False, tm, tn, tk, res=None, out_dtype=F32, out_shard_perm=None, name):
    M, K = (a.shape[1], a.shape[0]) if ta else a.shape
    N = b.shape[0] if tb else b.shape[1]
    tm, tn, tk = min(tm, M), min(tn, N), min(tk, K)
    gm, gn, gk = M // tm, N // tn, K // tk
    assert gm * tm == M and gn * tn == N and gk * tk == K, (name, M, N, K, tm, tn, tk)
    a_spec = pl.BlockSpec((tk, tm), lambda i, j, k: (k, i)) if ta else pl.BlockSpec((tm, tk), lambda i, j, k: (i, k))
    b_spec = pl.BlockSpec((tn, tk), lambda i, j, k: (j, k)) if tb else pl.BlockSpec((tk, tn), lambda i, j, k: (k, j))
    dims = (((0 if ta else 1,), (1 if tb else 0,)), ((), ()))
    has_res = res is not None

    def body(*refs):
        a_ref, b_ref = refs[0], refs[1]
        r_ref = refs[2] if has_res else None
        o_ref = refs[3] if has_res else refs[2]
        def dot():
            return lax.dot_general(a_ref[...].astype(MXU), b_ref[...].astype(MXU), dims, preferred_element_type=F32)

        if gk == 1:
            p = dot()
            if has_res:
                p = p + r_ref[...]
            o_ref[...] = p.astype(out_dtype)
        else:
            acc = refs[-1]
            k = pl.program_id(2)

            @pl.when(k == 0)
            def _():
                acc[...] = dot()

            if gk > 2:
                @pl.when(jnp.logical_and(k > 0, k < gk - 1))
                def _():
                    acc[...] += dot()

            @pl.when(k == gk - 1)
            def _():
                r = acc[...] + dot()
                if has_res:
                    r = r + r_ref[...]
                o_ref[...] = r.astype(out_dtype)

    in_specs = [a_spec, b_spec]
    args = [a, b]
    if has_res:
        in_specs.append(pl.BlockSpec((tm, tn), lambda i, j, k: (i, j)))
        args.append(res)
    if out_shard_perm is None:
        out_spec = pl.BlockSpec((tm, tn), lambda i, j, k: (i, j))
        out_shape = _sds((M, N), out_dtype)
    else:
        assert gn == len(out_shard_perm) == 4 and tuple(out_shard_perm) == (0, 2, 1, 3)
        out_spec = pl.BlockSpec((None, tm, tn), lambda i, j, k: ((j % 2) * 2 + j // 2, i, 0))
        out_shape = _sds((gn, M, tn), out_dtype)
    return pl.pallas_call(
        body, grid=(gm, gn, gk), in_specs=in_specs, out_specs=out_spec, out_shape=out_shape,
        scratch_shapes=[pltpu.VMEM((tm, tn), F32)] if gk > 1 else [],
        compiler_params=_cp("parallel", "parallel", "arbitrary"), name=name)(*args)


def _rmsnorm_fwd(x, w, name, token=None):
    T, D = x.shape
    tm = min(T, 512)
    has_token = token is not None

    def body(*refs):
        x_ref, w_ref, o_ref = refs[0], refs[1], refs[-1]
        xv = x_ref[...]
        if has_token:
            xv = xv + refs[2][0:1, 0:1]
        r = lax.rsqrt(jnp.mean(xv * xv, axis=-1, keepdims=True) + NORM_EPS)
        o_ref[...] = (xv * r * w_ref[...]).astype(o_ref.dtype)

    in_specs = [pl.BlockSpec((tm, D), lambda i: (i, 0)), pl.BlockSpec((1, D), lambda i: (0, 0))]
    args = [x, w.reshape(1, D)]
    if has_token:
        in_specs.append(pl.BlockSpec((SUBLANES, LANES), lambda i: (0, 0)))
        args.append(token)
    return pl.pallas_call(
        body, grid=(T // tm,), in_specs=in_specs,
        out_specs=pl.BlockSpec((tm, D), lambda i: (i, 0)), out_shape=_sds((T, D), MXU),
        compiler_params=_cp("parallel"), name=name)(*args)


def _rmsnorm_bwd(x, w, dh, dres, name, token=None):
    T, D = x.shape
    tm = min(T, 512)
    has_token = token is not None

    def body(*refs):
        x_ref, w_ref, dh_ref, dr_ref = refs[:4]
        dx_ref, dw_ref = refs[-2:]
        xv = x_ref[...]
        r = lax.rsqrt(jnp.mean(xv * xv, axis=-1, keepdims=True) + NORM_EPS)
        xh = xv * r
        dh = dh_ref[...]
        g = dh * w_ref[...]
        dr = dr_ref[...] + refs[4][0:1, 0:1] if has_token else dr_ref[...]
        dx_ref[...] = dr + r * (g - xh * jnp.mean(g * xh, axis=-1, keepdims=True))
        part = jnp.sum(dh * xh, axis=0, keepdims=True)

        @pl.when(pl.program_id(0) == 0)
        def _():
            dw_ref[...] = part

        @pl.when(pl.program_id(0) > 0)
        def _():
            dw_ref[...] += part

    row = pl.BlockSpec((tm, D), lambda i: (i, 0))
    vec = pl.BlockSpec((1, D), lambda i: (0, 0))
    in_specs = [row, vec, row, row]
    args = [x, w.reshape(1, D), dh, dres]
    if has_token:
        in_specs.append(pl.BlockSpec((SUBLANES, LANES), lambda i: (0, 0)))
        args.append(token)
    return pl.pallas_call(
        body, grid=(T // tm,), in_specs=in_specs, out_specs=[row, vec],
        out_shape=[_sds((T, D)), _sds((1, D))], compiler_params=_cp("arbitrary"), name=name)(*args)


def _loss_head(x, w, target, name):
    T, D = x.shape
    tm = min(T, 512)

    def body(x_ref, w_ref, t_ref, loss_ref, dx_ref, dw_ref):
        xv = x_ref[...]
        r = lax.rsqrt(jnp.mean(xv * xv, axis=-1, keepdims=True) + NORM_EPS)
        xh = xv * r
        wv = w_ref[...]
        e = xh * wv - t_ref[...]
        lpart = 0.5 * jnp.sum(jnp.mean(e * e, axis=-1, keepdims=True), axis=0, keepdims=True)
        dy = e * (1.0 / D)
        g = dy * wv
        dx_ref[...] = r * (g - xh * jnp.mean(g * xh, axis=-1, keepdims=True))
        part = jnp.sum(dy * xh, axis=0, keepdims=True)
        lrow = jnp.broadcast_to(lpart, (1, LANES))

        @pl.when(pl.program_id(0) == 0)
        def _():
            dw_ref[...] = part
            loss_ref[...] = lrow

        @pl.when(pl.program_id(0) > 0)
        def _():
            dw_ref[...] += part
            loss_ref[...] += lrow

    row = pl.BlockSpec((tm, D), lambda i: (i, 0))
    vec = pl.BlockSpec((1, D), lambda i: (0, 0))
    return pl.pallas_call(
        body, grid=(T // tm,), in_specs=[row, vec, row],
        out_specs=[pl.BlockSpec((1, LANES), lambda i: (0, 0)), row, vec],
        out_shape=[_sds((1, LANES)), _sds((T, D)), _sds((1, D))],
        compiler_params=_cp("arbitrary"), name=name)(x, w.reshape(1, D), target)


def _shift_down(cur, prev8, s):
    if s == 0:
        return cur
    tm = cur.shape[0]
    rc = pltpu.roll(cur, s, 0)
    top = jnp.where(_iota((SUBLANES, cur.shape[1]), 0) < s, pltpu.roll(prev8, s, 0), rc[:SUBLANES])
    return jnp.concatenate([top, rc[SUBLANES:]], axis=0) if tm > SUBLANES else top


def _shift_up(cur, next8, s):
    if s == 0:
        return cur
    tm = cur.shape[0]
    rc = pltpu.roll(cur, tm - s, 0)
    bot = jnp.where(_iota((SUBLANES, cur.shape[1]), 0) >= SUBLANES - s, pltpu.roll(next8, SUBLANES - s, 0), rc[tm - SUBLANES:])
    return jnp.concatenate([rc[:tm - SUBLANES], bot], axis=0) if tm > SUBLANES else bot


def _conv_rows(cur, prev8, w, b, K):
    acc = cur * w[K - 1:K, :] + b
    for s in range(1, K):
        acc = acc + _shift_down(cur, prev8, s) * w[K - 1 - s:K - s, :]
    return acc


FFN_TC = 1408
HALO16 = 2 * SUBLANES


def _ffn_mid_fwd(hid, cw, cb, name):
    T = hid.shape[0]
    tm = min(T, 256)
    nt, nj = T // tm, D_FF // FFN_TC
    K = FFN_CONV

    q = tm // HALO16

    def body(h_ref, hp_ref, w_ref, b_ref, o_ref, hc_ref):
        i = pl.program_id(0)
        cur = h_ref[...].astype(F32)
        prev8 = jnp.where(i > 0, hp_ref[...].astype(F32)[HALO16 - SUBLANES:], 0.0)
        hc = _conv_rows(cur, prev8, w_ref[...], b_ref[...], K)
        hc_ref[...] = hc
        o_ref[...] = (_silu(hc[:, FFN_TC:]) * hc[:, :FFN_TC]).astype(o_ref.dtype)

    return pl.pallas_call(
        body, grid=(nt, nj),
        in_specs=[pl.BlockSpec((tm, 2 * FFN_TC), lambda i, j: (i, j)),
                  pl.BlockSpec((HALO16, 2 * FFN_TC), lambda i, j: (jnp.maximum(i * q - 1, 0), j)),
                  pl.BlockSpec((K, 2 * FFN_TC), lambda i, j: (0, j)), pl.BlockSpec((1, 2 * FFN_TC), lambda i, j: (0, j))],
        out_specs=[pl.BlockSpec((tm, FFN_TC), lambda i, j: (i, j)), pl.BlockSpec((tm, 2 * FFN_TC), lambda i, j: (i, j))],
        out_shape=[_sds((T, D_FF), MXU), _sds((T, 2 * D_FF))],
        compiler_params=_cp("parallel", "parallel"), name=name)(hid, hid, cw, cb)


def _ffn_mid_bwd(hid, hc, cw, da, name):
    T = hid.shape[0]
    tm = min(T, 256)
    nt, nj = T // tm, D_FF // FFN_TC
    K = FFN_CONV
    W2 = 2 * FFN_TC

    def body(h_ref, c_ref, cn_ref, da_ref, dan_ref, w_ref, dh_ref, dw_ref, db_ref):
        i = pl.program_id(1)
        w = w_ref[...]
        cur = h_ref[...].astype(F32)
        last = i == nt - 1

        def dpre(hcv, dav):
            u, g = hcv[:, :FFN_TC], hcv[:, FFN_TC:]
            return jnp.concatenate([dav * _silu(g), dav * u * _dsilu(g)], axis=1)

        d_cur = dpre(c_ref[...], da_ref[...])
        d_nxt = jnp.where(last, 0.0, dpre(cn_ref[...], dan_ref[...]))
        ups = [d_cur] + [_shift_up(d_cur, d_nxt, s) for s in range(1, K)]
        dh = ups[0] * w[K - 1:K, :]
        for s in range(1, K):
            dh = dh + ups[s] * w[K - 1 - s:K - s, :]
        dh_ref[...] = dh.astype(dh_ref.dtype)
        dwp = jnp.concatenate([jnp.sum(ups[K - 1 - k] * cur, axis=0, keepdims=True) for k in range(K)], axis=0)
        dbp = jnp.sum(d_cur, axis=0, keepdims=True)

        @pl.when(i == 0)
        def _():
            dw_ref[...] = dwp
            db_ref[...] = dbp

        @pl.when(i > 0)
        def _():
            dw_ref[...] += dwp
            db_ref[...] += dbp

    q = tm // SUBLANES
    blk = pl.BlockSpec((tm, W2), lambda j, i: (i, j))
    nxt = pl.BlockSpec((SUBLANES, W2), lambda j, i: (jnp.minimum((i + 1) * q, nt * q - 1), j))
    dab = pl.BlockSpec((tm, FFN_TC), lambda j, i: (i, j))
    dan = pl.BlockSpec((SUBLANES, FFN_TC), lambda j, i: (jnp.minimum((i + 1) * q, nt * q - 1), j))
    return pl.pallas_call(
        body, grid=(nj, nt),
        in_specs=[blk, blk, nxt, dab, dan, pl.BlockSpec((K, W2), lambda j, i: (0, j))],
        out_specs=[blk, pl.BlockSpec((K, W2), lambda j, i: (0, j)), pl.BlockSpec((1, W2), lambda j, i: (0, j))],
        out_shape=[_sds((T, 2 * D_FF), MXU), _sds((K, 2 * D_FF)), _sds((1, 2 * D_FF))],
        compiler_params=_cp("parallel", "arbitrary"), name=name)(hid, hc, hc, da, da, cw)


def _rope(t, cos, sin_s, inverse=False):
    n = t.shape[1] // LANES
    c = jnp.concatenate([cos] * n, axis=1) if n > 1 else cos
    s = jnp.concatenate([sin_s] * n, axis=1) if n > 1 else sin_s
    a = pltpu.roll(t, HEAD_DIM // 2, 1)
    b = pltpu.roll(t, t.shape[1] - HEAD_DIM // 2, 1)
    first = (_iota(t.shape, 1) % HEAD_DIM) < HEAD_DIM // 2
    rot = jnp.where(first, b, a) * s
    return t * c - rot if inverse else t * c + rot


def _stack_heads(t, g):
    return jnp.concatenate([t[:, (GQ * g + r) * HEAD_DIM:(GQ * g + r + 1) * HEAD_DIM] for r in range(GQ)], axis=0)


def _stack_cols(t, g):
    return jnp.concatenate([t[:, GQ * g + r:GQ * g + r + 1] for r in range(GQ)], axis=0)


def _pool_sums(prev, cur, w):
    s = jnp.concatenate([prev, cur], axis=0)
    sh = 1
    while sh < w:
        s = s + pltpu.roll(s, sh, 0)
        sh *= 2
    return s[BLOCK:]


def _nt(a, b):
    return lax.dot_general(a.astype(MXU), b.astype(MXU), (((1,), (1,)), ((), ())), preferred_element_type=F32)


def _tn(a, b):
    return lax.dot_general(a.astype(MXU), b.astype(MXU), (((0,), (0,)), ((), ())), preferred_element_type=F32)


def _nn(a, b):
    return jnp.dot(a.astype(MXU), b.astype(MXU), preferred_element_type=F32)


def _mixcore_fwd(proj, cos, sin_s, pool_w, pool_scale, sinks, name):
    T = proj.shape[0]
    nb = T // BLOCK
    scale = HEAD_DIM ** -0.5

    def body(p_ref, pp_ref, c_ref, s_ref, cp_ref, sp_ref, pw_ref, ps_ref, sk_ref, cat_ref, at_ref, lse_ref):
        i = pl.program_id(0)
        has_prev = i > 0
        cur = p_ref[...]
        prv = jnp.where(has_prev, pp_ref[...], 0.0)
        tpos = (i * BLOCK + _iota((BLOCK, 1), 0) + 1).astype(F32)
        for g, w in enumerate(POOL_WINDOWS):
            sl = slice(g * POOL_GROUP, (g + 1) * POOL_GROUP)
            pooled = _pool_sums(prv[:, sl], cur[:, sl], w) / jnp.minimum(tpos, float(w)) - cur[:, sl]
            cat_ref[:, sl] = (_nn(pooled, pw_ref[g]) * ps_ref[:, sl]).astype(cat_ref.dtype)
        q = _rope(cur[:, POOL_DIM:POOL_DIM + Q_DIM], c_ref[...], s_ref[...])
        kc = _rope(cur[:, POOL_DIM + Q_DIM:POOL_DIM + Q_DIM + KV_DIM], c_ref[...], s_ref[...])
        kp = _rope(prv[:, POOL_DIM + Q_DIM:POOL_DIM + Q_DIM + KV_DIM], cp_ref[...], sp_ref[...])
        vc = cur[:, POOL_DIM + Q_DIM + KV_DIM:]
        vp = prv[:, POOL_DIM + Q_DIM + KV_DIM:]
        ri = _iota((GQ * BLOCK, BLOCK), 0) % BLOCK
        cj = _iota((GQ * BLOCK, BLOCK), 1)
        mc = cj <= ri
        mp = jnp.logical_and(cj > ri, has_prev)
        outs, lses = [], []
        for g in range(N_KV_HEADS):
            hs = slice(g * HEAD_DIM, (g + 1) * HEAD_DIM)
            qg = _stack_heads(q, g) * scale
            sc = jnp.where(mc, _nt(qg, kc[:, hs]), NEG)
            sp = jnp.where(mp, _nt(qg, kp[:, hs]), NEG)
            sink = jnp.concatenate([jnp.full((BLOCK, 1), sk_ref[GQ * g + r], F32) for r in range(GQ)], axis=0)
            m = jnp.maximum(jnp.maximum(jnp.max(sc, axis=1, keepdims=True), jnp.max(sp, axis=1, keepdims=True)), sink)
            pc = jnp.exp(sc - m)
            pp = jnp.exp(sp - m)
            den = jnp.sum(pc, axis=1, keepdims=True) + jnp.sum(pp, axis=1, keepdims=True) + jnp.exp(sink - m)
            o = (_nn(pc, vc[:, hs]) + _nn(pp, vp[:, hs])) / den
            lse = m + jnp.log(den)
            for r in range(GQ):
                outs.append(o[r * BLOCK:(r + 1) * BLOCK])
                lses.append(lse[r * BLOCK:(r + 1) * BLOCK])
        attn = jnp.concatenate(outs, axis=1)
        at_ref[...] = attn
        cat_ref[:, POOL_DIM:] = attn.astype(cat_ref.dtype)
        lane = _iota((BLOCK, LANES), 1)
        lrow = jnp.zeros((BLOCK, LANES), F32)
        for h in range(N_HEADS):
            lrow = jnp.where(lane == h, lses[h], lrow)
        lse_ref[...] = lrow

    cur = lambda w: pl.BlockSpec((BLOCK, w), lambda i: (i, 0))
    prv = lambda w: pl.BlockSpec((BLOCK, w), lambda i: (jnp.maximum(i - 1, 0), 0))
    return pl.pallas_call(
        body, grid=(nb,),
        in_specs=[cur(MIX_IN_DIM), prv(MIX_IN_DIM), cur(LANES), cur(LANES), prv(LANES), prv(LANES),
                  pl.BlockSpec((4, POOL_GROUP, POOL_GROUP), lambda i: (0, 0, 0)), pl.BlockSpec((1, POOL_DIM), lambda i: (0, 0)),
                  pl.BlockSpec(memory_space=pltpu.SMEM)],
        out_specs=[cur(2 * POOL_DIM), cur(Q_DIM), cur(LANES)],
        out_shape=[_sds((T, 2 * POOL_DIM), MXU), _sds((T, Q_DIM)), _sds((T, LANES))],
        compiler_params=_cp("parallel"), name=name)(proj, proj, cos, sin_s, cos, sin_s, pool_w, pool_scale, sinks)


def _mixcore_bwd(proj, cos, sin_s, pool_w, pool_scale, sinks, attn, lse, dcat, name):
    T = proj.shape[0]
    nb = T // BLOCK
    scale = HEAD_DIM ** -0.5
    QO, KO, VO = POOL_DIM, POOL_DIM + Q_DIM, POOL_DIM + Q_DIM + KV_DIM

    def body(p_ref, pp_ref, pn_ref, c_ref, s_ref, cp_ref, sp_ref, cn_ref, sn_ref, pw_ref, ps_ref, sk_ref,
             at_ref, atn_ref, l_ref, ln_ref, d_ref, dn_ref, dp_ref, dpw_ref, dps_ref, dsk_ref):
        i = pl.program_id(0)
        has_prev = i > 0
        has_next = i < nb - 1
        cur = p_ref[...]
        prv = jnp.where(has_prev, pp_ref[...], 0.0)
        d_cur = d_ref[...]
        d_nxt = jnp.where(has_next, dn_ref[...], 0.0)

        tpos = (i * BLOCK + _iota((BLOCK, 1), 0) + 1).astype(F32)
        tpos2 = (i * BLOCK + _iota((2 * BLOCK, 1), 0) + 1).astype(F32)
        ps = ps_ref[...]
        dps_parts, dpw_parts = [], []
        for g, w in enumerate(POOL_WINDOWS):
            sl = slice(g * POOL_GROUP, (g + 1) * POOL_GROUP)
            pooled = _pool_sums(prv[:, sl], cur[:, sl], w) / jnp.minimum(tpos, float(w)) - cur[:, sl]
            mixed = _nn(pooled, pw_ref[g])
            dps_parts.append(jnp.sum(d_cur[:, sl] * mixed, axis=0, keepdims=True))
            dm2 = jnp.concatenate([d_cur[:, sl], d_nxt[:, sl]], axis=0) * ps[:, sl]
            dpw_parts.append(_tn(pooled, dm2[:BLOCK]))
            dpool2 = _nt(dm2, pw_ref[g])
            e = dpool2 / jnp.minimum(tpos2, float(w))
            sh = 1
            while sh < w:
                e = e + pltpu.roll(e, 2 * BLOCK - sh, 0)
                sh *= 2
            dp_ref[:, sl] = (e[:BLOCK] - dpool2[:BLOCK]).astype(dp_ref.dtype)
        dpsp = jnp.concatenate(dps_parts, axis=1)

        nxt = pn_ref[...]
        q = _rope(cur[:, QO:KO], c_ref[...], s_ref[...])
        qn = _rope(nxt[:, QO:KO], cn_ref[...], sn_ref[...])
        kc = _rope(cur[:, KO:VO], c_ref[...], s_ref[...])
        kp = _rope(prv[:, KO:VO], cp_ref[...], sp_ref[...])
        vc, vp = cur[:, VO:], prv[:, VO:]
        do, don = d_cur[:, POOL_DIM:], d_nxt[:, POOL_DIM:]
        dl = do * at_ref[...]
        dln = don * atn_ref[...]
        lse, lsen = l_ref[...], ln_ref[...]
        ri = _iota((GQ * BLOCK, BLOCK), 0) % BLOCK
        cj = _iota((GQ * BLOCK, BLOCK), 1)
        mc = cj <= ri
        mp = jnp.logical_and(cj > ri, has_prev)
        mn = jnp.logical_and(cj > ri, has_next)
        dq_parts, dk_parts, dv_parts, dsk_vals = [], [], [], []
        for g in range(N_KV_HEADS):
            hs = slice(g * HEAD_DIM, (g + 1) * HEAD_DIM)
            qg, qng = _stack_heads(q, g) * scale, _stack_heads(qn, g) * scale
            dog, dong = _stack_heads(do, g), _stack_heads(don, g)
            delta = jnp.sum(_stack_heads(dl, g), axis=1, keepdims=True)
            deltan = jnp.sum(_stack_heads(dln, g), axis=1, keepdims=True)
            lg, lng = _stack_cols(lse, g), _stack_cols(lsen, g)
            pc = jnp.where(mc, jnp.exp(_nt(qg, kc[:, hs]) - lg), 0.0)
            pp = jnp.where(mp, jnp.exp(_nt(qg, kp[:, hs]) - lg), 0.0)
            pn = jnp.where(mn, jnp.exp(_nt(qng, kc[:, hs]) - lng), 0.0)
            dsc = pc * (_nt(dog, vc[:, hs]) - delta)
            dsp = pp * (_nt(dog, vp[:, hs]) - delta)
            dsn = pn * (_nt(dong, vc[:, hs]) - deltan)
            dqg = (_nn(dsc, kc[:, hs]) + _nn(dsp, kp[:, hs])) * scale
            dq_parts += [dqg[r * BLOCK:(r + 1) * BLOCK] for r in range(GQ)]
            dk_parts.append(_tn(dsc, qg) + _tn(dsn, qng))
            dv_parts.append(_tn(pc, dog) + _tn(pn, dong))
            sink = jnp.concatenate([jnp.full((BLOCK, 1), sk_ref[GQ * g + r], F32) for r in range(GQ)], axis=0)
            dsk = -jnp.exp(sink - lg) * delta
            dsk_vals += [jnp.sum(dsk[r * BLOCK:(r + 1) * BLOCK], axis=0, keepdims=True) for r in range(GQ)]
        dq = _rope(jnp.concatenate(dq_parts, axis=1), c_ref[...], s_ref[...], inverse=True)
        dk = _rope(jnp.concatenate(dk_parts, axis=1), c_ref[...], s_ref[...], inverse=True)
        dp_ref[:, QO:KO] = dq.astype(dp_ref.dtype)
        dp_ref[:, KO:VO] = dk.astype(dp_ref.dtype)
        dp_ref[:, VO:] = jnp.concatenate(dv_parts, axis=1).astype(dp_ref.dtype)
        lane = _iota((1, LANES), 1)
        dskp = jnp.zeros((1, LANES), F32)
        for h in range(N_HEADS):
            dskp = jnp.where(lane == h, dsk_vals[h], dskp)

        @pl.when(i == 0)
        def _():
            dps_ref[...] = dpsp
            dsk_ref[...] = dskp
            for g in range(4):
                dpw_ref[g] = dpw_parts[g]

        @pl.when(i > 0)
        def _():
            dps_ref[...] += dpsp
            dsk_ref[...] += dskp
            for g in range(4):
                dpw_ref[g] += dpw_parts[g]

    cur = lambda w: pl.BlockSpec((BLOCK, w), lambda i: (i, 0))
    prv = lambda w: pl.BlockSpec((BLOCK, w), lambda i: (jnp.maximum(i - 1, 0), 0))
    nxt = lambda w: pl.BlockSpec((BLOCK, w), lambda i: (jnp.minimum(i + 1, nb - 1), 0))
    return pl.pallas_call(
        body, grid=(nb,),
        in_specs=[cur(MIX_IN_DIM), prv(MIX_IN_DIM), nxt(MIX_IN_DIM),
                  cur(LANES), cur(LANES), prv(LANES), prv(LANES), nxt(LANES), nxt(LANES),
                  pl.BlockSpec((4, POOL_GROUP, POOL_GROUP), lambda i: (0, 0, 0)), pl.BlockSpec((1, POOL_DIM), lambda i: (0, 0)),
                  pl.BlockSpec(memory_space=pltpu.SMEM),
                  cur(Q_DIM), nxt(Q_DIM), cur(LANES), nxt(LANES), cur(2 * POOL_DIM), nxt(2 * POOL_DIM)],
        out_specs=[cur(MIX_IN_DIM), pl.BlockSpec((4, POOL_GROUP, POOL_GROUP), lambda i: (0, 0, 0)),
                   pl.BlockSpec((1, POOL_DIM), lambda i: (0, 0)), pl.BlockSpec((1, LANES), lambda i: (0, 0))],
        out_shape=[_sds((T, MIX_IN_DIM), MXU), _sds((4, POOL_GROUP, POOL_GROUP)), _sds((1, POOL_DIM)), _sds((1, LANES))],
        compiler_params=_cp("arbitrary"), name=name)(
            proj, proj, proj, cos, sin_s, cos, sin_s, cos, sin_s, pool_w, pool_scale, sinks, attn, attn, lse, lse, dcat, dcat)


SSM_TC = 128
GROUP_W = SSM_D_INNER // SSM_GROUPS
PERM_W = GROUP_W + 2 * SSM_STATE


def _perm_col(n):
    nx = SSM_D_INNER // SSM_TC
    nbt = SSM_GROUPS
    x_idx = (n // 2) * 4 + n % 2
    b_idx = (n - nx) * 4 + 2
    c_idx = (n - nx - nbt) * 4 + 3
    return jnp.where(n < nx, x_idx, jnp.where(n < nx + nbt, b_idx, c_idx))


def _ssm_pre_fwd(xbc, cw, cb, name):
    T = xbc.shape[0]
    tm = min(T, 1024)
    K = SSM_CONV
    q = tm // SUBLANES

    def body(x_ref, xp_ref, w_ref, b_ref, o_ref, pre_ref):
        prev8 = jnp.where(pl.program_id(0) > 0, xp_ref[...], 0.0)
        pre = _conv_rows(x_ref[...], prev8, w_ref[...], b_ref[...], K)
        pre_ref[...] = pre
        o_ref[...] = _silu(pre)

    tc = 512
    blk = pl.BlockSpec((tm, tc), lambda i, j: (i, j))
    return pl.pallas_call(
        body, grid=(T // tm, SSM_CONV_DIM // tc),
        in_specs=[blk, pl.BlockSpec((SUBLANES, tc), lambda i, j: (jnp.maximum(i * q - 1, 0), j)),
                  pl.BlockSpec((K, tc), lambda i, j: (0, j)), pl.BlockSpec((1, tc), lambda i, j: (0, j))],
        out_specs=[blk, blk], out_shape=[_sds((T, SSM_CONV_DIM)), _sds((T, SSM_CONV_DIM))],
        compiler_params=_cp("parallel", "parallel"), name=name)(xbc, xbc, cw, cb)


def _ssm_pre_bwd(xbc, pre, cw, dact_perm, name):
    T = xbc.shape[0]
    tm = min(T, 1024)
    nt = T // tm
    K = SSM_CONV
    q = tm // SUBLANES
    tc = SSM_TC

    def body(x_ref, p_ref, pn_ref, d_ref, dn_ref, w_ref, dx_ref, dw_ref, db_ref):
        i = pl.program_id(1)
        w = w_ref[...]
        cur = x_ref[...]
        d_cur = d_ref[...] * _dsilu(p_ref[...])
        d_nxt = jnp.where(i == nt - 1, 0.0, dn_ref[...] * _dsilu(pn_ref[...]))
        ups = [d_cur] + [_shift_up(d_cur, d_nxt, s) for s in range(1, K)]
        dx = ups[0] * w[K - 1:K, :]
        for s in range(1, K):
            dx = dx + ups[s] * w[K - 1 - s:K - s, :]
        dx_ref[...] = dx.astype(dx_ref.dtype)
        dwp = jnp.concatenate([jnp.sum(ups[K - 1 - k] * cur, axis=0, keepdims=True) for k in range(K)], axis=0)
        dbp = jnp.sum(d_cur, axis=0, keepdims=True)

        @pl.when(i == 0)
        def _():
            dw_ref[...] = dwp
            db_ref[...] = dbp

        @pl.when(i > 0)
        def _():
            dw_ref[...] += dwp
            db_ref[...] += dbp

    nxt_row = lambda i: jnp.minimum((i + 1) * q, nt * q - 1)
    return pl.pallas_call(
        body, grid=(SSM_CONV_DIM // tc, nt),
        in_specs=[pl.BlockSpec((tm, tc), lambda j, i: (i, j)),
                  pl.BlockSpec((tm, tc), lambda j, i: (i, j)),
                  pl.BlockSpec((SUBLANES, tc), lambda j, i: (nxt_row(i), j)),
                  pl.BlockSpec((tm, tc), lambda j, i: (i, _perm_col(j))),
                  pl.BlockSpec((SUBLANES, tc), lambda j, i: (nxt_row(i), _perm_col(j))),
                  pl.BlockSpec((K, tc), lambda j, i: (0, j))],
        out_specs=[pl.BlockSpec((tm, tc), lambda j, i: (i, j)), pl.BlockSpec((K, tc), lambda j, i: (0, j)),
                   pl.BlockSpec((1, tc), lambda j, i: (0, j))],
        out_shape=[_sds((T, SSM_CONV_DIM), MXU), _sds((K, SSM_CONV_DIM)), _sds((1, SSM_CONV_DIM))],
        compiler_params=_cp("parallel", "arbitrary"), name=name)(xbc, pre, pre, dact_perm, dact_perm, cw)


def _dot_hi(a, b):
    return jnp.dot(a, b, precision=HI, preferred_element_type=F32)


def _ssd_common(dtraw, bias, alog):
    L = SSM_CHUNK
    xb = dtraw + bias
    dt = jnp.maximum(xb, 0.0) + jnp.log1p(jnp.exp(-jnp.abs(xb)))
    A = -jnp.exp(alog)
    tril = (_iota((L, L), 1) <= _iota((L, L), 0)).astype(F32)
    acs = _dot_hi(tril, dt * A)
    return xb, dt, A, tril, acs


def _head_selectors():
    es = (_iota((LANES, SSM_D_INNER), 0) == _iota((LANES, SSM_D_INNER), 1) // HEAD_DIM).astype(BF16)
    est = (_iota((SSM_D_INNER, LANES), 1) == _iota((SSM_D_INNER, LANES), 0) // HEAD_DIM).astype(BF16)
    return es, est


def _dot_sel(v, sel):
    hi = v.astype(BF16)
    r1 = v - hi.astype(F32)
    mid = r1.astype(BF16)
    lo = (r1 - mid.astype(F32)).astype(BF16)
    d = lambda a: jnp.dot(a, sel, preferred_element_type=F32)
    return (d(hi) + d(mid)) + d(lo)


def _expand_heads(v, es):
    return _dot_sel(v, es)


def _reduce_heads(q, est):
    return _dot_sel(q, est)


def _per_state_row(v, g):
    return jnp.concatenate([jnp.broadcast_to(v[:, GQ * g + r:GQ * g + r + 1], (HEAD_DIM, 1)) for r in range(GQ)], axis=0)


def _ssd_fwd(xact, dtraw, dt_bias, a_log, name):
    T = xact.shape[0]
    nc = T // SSM_CHUNK
    L = SSM_CHUNK
    BO, CO = SSM_D_INNER, SSM_D_INNER + SSM_GROUPS * SSM_STATE

    def body(x_ref, dt_ref, bias_ref, al_ref, es_ref, y_ref, st_ref, state):
        @pl.when(pl.program_id(0) == 0)
        def _():
            state[...] = jnp.zeros(state.shape, F32)

        _, dt, A, tril, acs = _ssd_common(dt_ref[...], bias_ref[...], al_ref[...])
        acsT = acs.T
        last = acs[L - 1:L, :]
        cd = jnp.exp(last)
        es = es_ref[...]
        dtX = _expand_heads(dt, es)
        EX = _expand_heads(jnp.exp(acs), es)
        decX = _expand_heads(jnp.exp(last - acs), es)
        for g in range(SSM_GROUPS):
            gs = slice(g * GROUP_W, (g + 1) * GROUP_W)
            B = x_ref[:, BO + g * SSM_STATE:BO + (g + 1) * SSM_STATE]
            C = x_ref[:, CO + g * SSM_STATE:CO + (g + 1) * SSM_STATE]
            X = x_ref[:, gs] * dtX[:, gs]
            CB = _nt(C, B)
            yd = []
            for r in range(GQ):
                h = GQ * g + r
                Lm = jnp.exp(jnp.where(tril > 0, acs[:, h:h + 1] - acsT[h:h + 1, :], NEG))
                yd.append(_nn(CB * Lm, X[:, r * HEAD_DIM:(r + 1) * HEAD_DIM]))
            S = state[g]
            st_ref[g] = S
            y_ref[:, gs] = jnp.concatenate(yd, axis=1) + _nt(C, S) * EX[:, gs]
            state[g] = S * _per_state_row(cd, g) + _tn(X * decX[:, gs], B)

    es, _ = _head_selectors()
    return pl.pallas_call(
        body, grid=(nc,),
        in_specs=[pl.BlockSpec((L, SSM_CONV_DIM), lambda c: (c, 0)), pl.BlockSpec((L, LANES), lambda c: (c, 0)),
                  pl.BlockSpec((1, LANES), lambda c: (0, 0)), pl.BlockSpec((1, LANES), lambda c: (0, 0)),
                  pl.BlockSpec((LANES, SSM_D_INNER), lambda c: (0, 0))],
        out_specs=[pl.BlockSpec((L, SSM_D_INNER), lambda c: (c, 0)),
                   pl.BlockSpec((None, SSM_GROUPS, GROUP_W, SSM_STATE), lambda c: (c, 0, 0, 0))],
        out_shape=[_sds((T, SSM_D_INNER)), _sds((nc, SSM_GROUPS, GROUP_W, SSM_STATE))],
        scratch_shapes=[pltpu.VMEM((SSM_GROUPS, GROUP_W, SSM_STATE), F32)],
        compiler_params=_cp("arbitrary"), name=name)(xact, dtraw, dt_bias, a_log, es)


def _ssd_bwd(xact, dtraw, dt_bias, a_log, d_skip, states, dy, name):
    T = xact.shape[0]
    nc = T // SSM_CHUNK
    L = SSM_CHUNK
    BO, CO = SSM_D_INNER, SSM_D_INNER + SSM_GROUPS * SSM_STATE

    def body(x_ref, dt_ref, bias_ref, al_ref, dsk_ref, es_ref, est_ref, st_ref, dy_ref,
             dxp_ref, ddt_ref, dbias_ref, dal_ref, dd_ref, dstate, qa, qx):
        cc = pl.program_id(0)

        @pl.when(cc == 0)
        def _():
            dstate[...] = jnp.zeros(dstate.shape, F32)

        xb, dt, A, tril, acs = _ssd_common(dt_ref[...], bias_ref[...], al_ref[...])
        acsT = acs.T
        last = acs[L - 1:L, :]
        cd = jnp.exp(last)
        es, est = es_ref[...], est_ref[...]
        dtX = _expand_heads(dt, es)
        EX = _expand_heads(jnp.exp(acs), es)
        decX = _expand_heads(jnp.exp(last - acs), es)
        lane1 = _iota((1, LANES), 1)
        lane = _iota((L, LANES), 1)
        sub = _iota((L, LANES), 0)
        ztot = jnp.zeros((1, LANES), F32)
        wrow = jnp.zeros((L, LANES), F32)
        wcolT = jnp.zeros((LANES, L), F32)
        rows_dec, rows_dd = [], []
        for g in range(SSM_GROUPS):
            gs = slice(g * GROUP_W, (g + 1) * GROUP_W)
            x = x_ref[:, gs]
            B = x_ref[:, BO + g * SSM_STATE:BO + (g + 1) * SSM_STATE]
            C = x_ref[:, CO + g * SSM_STATE:CO + (g + 1) * SSM_STATE]
            dY = dy_ref[:, gs]
            dtx, e_x, dec_x = dtX[:, gs], EX[:, gs], decX[:, gs]
            X = x * dtx
            CB = _nt(C, B)
            S = st_ref[g]
            dS_out = dstate[g]
            dcb_sum = jnp.zeros((L, L), F32)
            dxd = []
            for r in range(GQ):
                h = GQ * g + r
                hs = slice(r * HEAD_DIM, (r + 1) * HEAD_DIM)
                Lm = jnp.exp(jnp.where(tril > 0, acs[:, h:h + 1] - acsT[h:h + 1, :], NEG))
                M = CB * Lm
                dM = _nt(dY[:, hs], X[:, hs])
                dxd.append(_tn(M, dY[:, hs]))
                dcb_sum = dcb_sum + dM * Lm
                Wm = dM * M
                wrow = jnp.where(lane == h, jnp.sum(Wm, axis=1, keepdims=True), wrow)
                wcolT = jnp.where(sub == h, jnp.sum(Wm, axis=0, keepdims=True), wcolT)
            dXd = jnp.concatenate(dxd, axis=1)
            G = _nt(C, S)
            dG = dY * e_x
            dDX = _nt(B, dS_out)
            dX = dXd + dec_x * dDX
            t_dec = dDX * X * dec_x
            qa[:, gs] = dG * G - t_dec
            qx[:, gs] = dX * x
            rows_dec.append(jnp.sum(t_dec, axis=0, keepdims=True))
            rows_dd.append(jnp.sum(dY * x, axis=0, keepdims=True))
            zc = jnp.sum(dS_out * S, axis=1, keepdims=True)
            for r in range(GQ):
                ztot = jnp.where(lane1 == GQ * g + r, jnp.sum(zc[r * HEAD_DIM:(r + 1) * HEAD_DIM], axis=0, keepdims=True), ztot)
            dxp_ref[:, g * PERM_W:g * PERM_W + GROUP_W] = dX * dtx + dY * dsk_ref[:, gs]
            dxp_ref[:, g * PERM_W + GROUP_W:g * PERM_W + GROUP_W + SSM_STATE] = _tn(dcb_sum, C) + _nn(X * dec_x, dS_out)
            dxp_ref[:, g * PERM_W + GROUP_W + SSM_STATE:(g + 1) * PERM_W] = _nn(dcb_sum, B) + _nn(dG, S)
            dstate[g] = dS_out * _per_state_row(cd, g) + _tn(dG, C)
        rows = jnp.concatenate([jnp.concatenate(rows_dec, axis=1), jnp.concatenate(rows_dd, axis=1)]
                               + [jnp.zeros((SUBLANES - 2, SSM_D_INNER), F32)], axis=0)
        rsum = _reduce_heads(rows, est)
        dlast = rsum[0:1, :] + cd * ztot
        dacs = (wrow - wcolT.T) + _reduce_heads(qa[...], est) + jnp.where(sub == L - 1, dlast, 0.0)
        triu = (_iota((L, L), 0) <= _iota((L, L), 1)).astype(F32)
        da = _dot_hi(triu, dacs)
        ddtraw = (da * A + _reduce_heads(qx[...], est)) * (1.0 / (1.0 + jnp.exp(-xb)))
        ddt_ref[...] = ddtraw
        dal = jnp.sum(da * dt, axis=0, keepdims=True) * A
        ddp = rsum[1:2, :]
        dbp = jnp.sum(ddtraw, axis=0, keepdims=True)

        @pl.when(cc == 0)
        def _():
            dbias_ref[...] = dbp
            dal_ref[...] = dal
            dd_ref[...] = ddp

        @pl.when(cc > 0)
        def _():
            dbias_ref[...] += dbp
            dal_ref[...] += dal
            dd_ref[...] += ddp

    rc = lambda c: nc - 1 - c
    vec = pl.BlockSpec((1, LANES), lambda c: (0, 0))
    es, est = _head_selectors()
    return pl.pallas_call(
        body, grid=(nc,),
        in_specs=[pl.BlockSpec((L, SSM_CONV_DIM), lambda c: (rc(c), 0)), pl.BlockSpec((L, LANES), lambda c: (rc(c), 0)), vec, vec,
                  pl.BlockSpec((1, SSM_D_INNER), lambda c: (0, 0)),
                  pl.BlockSpec((LANES, SSM_D_INNER), lambda c: (0, 0)), pl.BlockSpec((SSM_D_INNER, LANES), lambda c: (0, 0)),
                  pl.BlockSpec((None, SSM_GROUPS, GROUP_W, SSM_STATE), lambda c: (rc(c), 0, 0, 0)),
                  pl.BlockSpec((L, SSM_D_INNER), lambda c: (rc(c), 0))],
        out_specs=[pl.BlockSpec((L, SSM_GROUPS * PERM_W), lambda c: (rc(c), 0)),
                   pl.BlockSpec((L, LANES), lambda c: (rc(c), 0)), vec, vec, vec],
        out_shape=[_sds((T, SSM_GROUPS * PERM_W)), _sds((T, LANES)), _sds((1, LANES)), _sds((1, LANES)), _sds((1, LANES))],
        scratch_shapes=[pltpu.VMEM((SSM_GROUPS, GROUP_W, SSM_STATE), F32), pltpu.VMEM((L, SSM_D_INNER), F32),
                        pltpu.VMEM((L, SSM_D_INNER), F32)],
        compiler_params=_cp("arbitrary"), name=name)(xact, dtraw, dt_bias, a_log, d_skip, es, est, states, dy)


def _ssm_post_fwd(y, xact, z, d_skip, nw, name):
    T = y.shape[0]
    tm = min(T, 256)
    W = SSM_D_INNER

    def body(y_ref, x_ref, z_ref, d_ref, w_ref, o_ref):
        y2 = (y_ref[...] + d_ref[...] * x_ref[...]) * _silu(z_ref[...])
        r = lax.rsqrt(jnp.mean(y2 * y2, axis=-1, keepdims=True) + SSM_NORM_EPS)
        o_ref[...] = (y2 * r * w_ref[...]).astype(o_ref.dtype)

    row = pl.BlockSpec((tm, W), lambda i: (i, 0))
    vec = pl.BlockSpec((1, W), lambda i: (0, 0))
    return pl.pallas_call(
        body, grid=(T // tm,), in_specs=[row, row, row, vec, vec], out_specs=row, out_shape=_sds((T, W), MXU),
        compiler_params=_cp("parallel"), name=name)(y, xact, z, d_skip, nw)


def _ssm_post_bwd(y, xact, z, d_skip, nw, dyn, name):
    T = y.shape[0]
    tm = min(T, 256)
    W = SSM_D_INNER

    def body(y_ref, x_ref, z_ref, d_ref, w_ref, dn_ref, dyg_ref, dz_ref, dw_ref):
        zv = z_ref[...]
        sz = _silu(zv)
        yg = y_ref[...] + d_ref[...] * x_ref[...]
        y2 = yg * sz
        r = lax.rsqrt(jnp.mean(y2 * y2, axis=-1, keepdims=True) + SSM_NORM_EPS)
        y2h = y2 * r
        dn = dn_ref[...]
        gy = dn * w_ref[...]
        dy2 = r * (gy - y2h * jnp.mean(gy * y2h, axis=-1, keepdims=True))
        dyg_ref[...] = dy2 * sz
        dz_ref[...] = (dy2 * yg * _dsilu(zv)).astype(dz_ref.dtype)
        part = jnp.sum(dn * y2h, axis=0, keepdims=True)

        @pl.when(pl.program_id(0) == 0)
        def _():
            dw_ref[...] = part

        @pl.when(pl.program_id(0) > 0)
        def _():
            dw_ref[...] += part

    row = pl.BlockSpec((tm, W), lambda i: (i, 0))
    vec = pl.BlockSpec((1, W), lambda i: (0, 0))
    return pl.pallas_call(
        body, grid=(T // tm,), in_specs=[row, row, row, vec, vec, row], out_specs=[row, row, vec],
        out_shape=[_sds((T, W)), _sds((T, W), MXU), _sds((1, W))],
        compiler_params=_cp("arbitrary"), name=name)(y, xact, z, d_skip, nw, dyn)


def _local_step(x0, cos, sin_s, target, P, fetch, token, send):
    mmf = functools.partial(_mm, tm=1024)
    big, small = {}, {}
    P = dict(P, wup={}, wdn={}, fcw={})
    h0 = _rmsnorm_fwd(x0, P["nm"][0], "norm_mix0", token=token)
    proj0 = mmf(h0, P["wmiT"], tb=True, tn=1280, tk=1024, name="mix_in")
    cat, attn, lse = _mixcore_fwd(proj0, cos, sin_s, P["pool_w"], P["pool_scale"], P["sinks"], "mixcore_fwd")
    x1 = mmf(cat, P["wmo"], tn=1024, tk=1024, res=x0, name="mix_out")

    def ffn_fwd(xin, i):
        hf = _rmsnorm_fwd(xin, P["nf"][i], f"norm_ffn{i}")
        got = fetch(f"ffn{i}", hf)
        P["wup"][i], P["wdn"][i], P["fcw"][i] = got["wup"], got["wdn"], got["fcw"]
        hid = mmf(hf, P["wup"][i], tn=1408, tk=1024, out_dtype=MXU, name=f"ffn_up{i}")
        act, hc = _ffn_mid_fwd(hid, P["fcw"][i], P["fcb"][i], f"ffn_mid_fwd{i}")
        xout = mmf(act, P["wdn"][i], tn=1024, tk=D_FF, res=xin, name=f"ffn_down{i}")
        return hf, (hid, hc), act, xout

    hf0, hid0, act0, x2 = ffn_fwd(x1, 0)
    h1 = _rmsnorm_fwd(x2, P["nm"][1], "norm_mix1")
    P.update(fetch("ssm", h1))
    z = mmf(h1, P["wzT"], tb=True, tn=1024, tk=1024, name="ssm_in_z")
    xbc = mmf(h1, P["wxbcT"], tb=True, tn=1024, tk=1024, name="ssm_in_xbc")
    dtraw = mmf(h1, P["wdtT"], tb=True, tn=128, tk=1024, name="ssm_in_dt")
    xact, xpre = _ssm_pre_fwd(xbc, P["scw"], P["scb"], "ssm_pre_fwd")
    y, states = _ssd_fwd(xact, dtraw, P["dt_bias"], P["a_log"], "ssd_fwd")
    yn = _ssm_post_fwd(y, xact, z, P["d_exp"], P["snorm"], "ssm_post_fwd")
    x3 = mmf(yn, P["wso"], tn=1024, tk=SSM_D_INNER, res=x2, name="ssm_out")
    hf1, hid1, act1, x4 = ffn_fwd(x3, 1)
    loss_row, dx4, d_nfin = _loss_head(x4, P["nfin"], target, "loss_head")
    small["norm_final"] = d_nfin

    def ffn_bwd(xin, dxo, hf, hid, act, i):
        da = mmf(dxo, P["wdn"][i], tb=True, tn=1408, tk=1024, name=f"ffn_down_dx{i}")
        big[f"ffn_w_down{i}"] = dwf(act, dxo, tm=1408, tn=1024, name=f"ffn_down_dw{i}").reshape(N_CHIPS, D_FF // N_CHIPS, D_MODEL)
        dhid, dcw, dcb = _ffn_mid_bwd(hid[0], hid[1], P["fcw"][i], da, f"ffn_mid_bwd{i}")
        dhf = mmf(dhid, P["wup"][i], tb=True, tn=1024, tk=2816, name=f"ffn_up_dx{i}")
        big[f"ffn_w_up{i}"] = dwf(hf, dhid, tm=1024, tn=1408, out_shard_perm=(0, 2, 1, 3), name=f"ffn_up_dw{i}")
        tok = send(f"ffn{i}", [big[f"ffn_w_up{i}"], big[f"ffn_w_down{i}"]])
        dxi, dnf = _rmsnorm_bwd(xin, P["nf"][i], dhf, dxo, f"norm_ffn_bwd{i}", token=tok)
        return dxi, dnf, dcw, dcb

    dwf = functools.partial(_mm, ta=True, tk=2048, out_dtype=BF16)
    dx3, dnf1, dfcw1, dfcb1 = ffn_bwd(x3, dx4, hf1, hid1, act1, 1)
    dyn = mmf(dx3, P["wso"], tb=True, tn=1024, tk=1024, name="ssm_out_dx")
    big["ssm_w_out"] = dwf(yn, dx3, tm=1024, tn=1024, name="ssm_out_dw").reshape(N_CHIPS, SSM_D_INNER // N_CHIPS, D_MODEL)
    dyg, dz, d_snorm = _ssm_post_bwd(y, xact, z, P["d_exp"], P["snorm"], dyn, "ssm_post_bwd")
    dxact_p, ddtraw, d_dtb, d_alog, d_dskip = _ssd_bwd(xact, dtraw, P["dt_bias"], P["a_log"], P["d_exp"], states, dyg, "ssd_bwd")
    dxbc, d_scw, d_scb = _ssm_pre_bwd(xbc, xpre, P["scw"], dxact_p, "ssm_pre_bwd")
    dh1 = mmf(dz, P["wzT"], tn=1024, tk=2048, name="ssm_in_dx_z")
    dh1 = mmf(dxbc, P["wxbcT"], tn=1024, tk=2048, res=dh1, name="ssm_in_dx_xbc")
    dh1 = mmf(ddtraw, P["wdtT"], tn=1024, tk=128, res=dh1, name="ssm_in_dx_dt")
    dwz = dwf(dz, h1, tm=1024, tn=1024, name="ssm_in_dw_z")
    dwxbc = dwf(dxbc, h1, tm=1024, tn=1024, name="ssm_in_dw_xbc")
    dwdt = dwf(ddtraw, h1, tm=128, tn=1024, name="ssm_in_dw_dt")
    dwsi = jnp.concatenate([dwz, dwxbc, dwdt[:SSM_HEADS]], axis=0)
    big["ssm_w_in"] = dwsi.reshape(N_CHIPS, SSM_IN_DIM // N_CHIPS, D_MODEL)
    tok = send("ssm", [big["ssm_w_in"], big["ssm_w_out"]])
    dx2, dnm1 = _rmsnorm_bwd(x2, P["nm"][1], dh1, dx3, "norm_mix_bwd1", token=tok)
    dx1, dnf0, dfcw0, dfcb0 = ffn_bwd(x1, dx2, hf0, hid0, act0, 0)
    dcat = mmf(dx1, P["wmo"], tb=True, tn=1024, tk=1024, name="mix_out_dx")
    big["mix_w_out"] = dwf(cat, dx1, tm=1024, tn=1024, name="mix_out_dw").reshape(N_CHIPS, D_MODEL // N_CHIPS, D_MODEL)
    dproj0, d_pw, d_ps, d_sk = _mixcore_bwd(proj0, cos, sin_s, P["pool_w"], P["pool_scale"], P["sinks"], attn, lse, dcat, "mixcore_bwd")
    dh0 = mmf(dproj0, P["wmiT"], tn=1024, tk=1280, name="mix_in_dx")
    big["mix_w_in"] = dwf(dproj0, h0, tm=1280, tn=1024, name="mix_in_dw").reshape(N_CHIPS, MIX_IN_DIM // N_CHIPS, D_MODEL)
    tok = send("mix", [big["mix_w_in"], big["mix_w_out"]])
    dx0, dnm0 = _rmsnorm_bwd(x0, P["nm"][0], dh0, dx1, "norm_mix_bwd0", token=tok)

    def unperm_cols(a):
        r = a.shape[0]
        t = a.reshape(r, N_CHIPS, FFN_TC)
        return jnp.stack([t[:, p] for p in _PERM], axis=0)

    small["norm_mix"] = jnp.concatenate([dnm0, dnm1], axis=0)
    small["norm_ffn"] = jnp.concatenate([dnf0, dnf1], axis=0)
    small["pool_w"] = d_pw.reshape(4 * POOL_GROUP, POOL_GROUP)
    small["pool_scale"] = d_ps
    small["attn_sinks"] = d_sk
    small["ssm_dt_bias"] = d_dtb
    small["ssm_A_log"] = d_alog
    small["ssm_D"] = d_dskip
    fcb = jnp.stack([unperm_cols(dfcb0), unperm_cols(dfcb1)], axis=0)
    small["ffn_conv_b"] = fcb.reshape(2, 2 * D_FF)
    small["ssm_conv_w"] = d_scw.reshape(SSM_CONV, N_CHIPS, SSM_CONV_DIM // N_CHIPS).transpose(1, 0, 2)
    small["ssm_conv_b"] = d_scb.reshape(N_CHIPS, 1, SSM_CONV_DIM // N_CHIPS)
    small["ssm_norm"] = d_snorm.reshape(N_CHIPS, 1, SSM_D_INNER // N_CHIPS)
    small["ffn_conv_w"] = jnp.concatenate([unperm_cols(dfcw0), unperm_cols(dfcw1)], axis=1)
    return loss_row, dx0, big, small


ANY = pl.BlockSpec(memory_space=pl.ANY)


def _place():
    return lax.axis_index("x"), lax.axis_index("y"), lax.axis_index("c")


def _gather_shards(shards, name):
    n = len(shards)
    split = [s.size >= (1 << 16) for s in shards]

    def half(ref, a, h):
        shp = shards[a].shape
        if len(shp) == 3:
            return ref.at[h]
        r2 = shp[0] // 2
        return ref.at[pl.ds(pl.multiple_of(h * r2, 2 * SUBLANES), r2), :]

    def body(*refs):
        ins, outs = refs[:n], refs[n:2 * n]
        send, recv, fsend, frecv = refs[2 * n:]
        x, y, c = _place()
        k = 2 * x + y
        chips = [(1 - x, y), (x, 1 - y), (1 - x, 1 - y)]

        def ici(a, j, src_slot_ref, dst_slot):
            px, py = chips[j]
            src = half(src_slot_ref, a, c) if split[a] else src_slot_ref
            dst = half(outs[a].at[dst_slot], a, c) if split[a] else outs[a].at[dst_slot]
            return pltpu.make_async_remote_copy(src, dst, send.at[a, j], recv.at[a, j], device_id=(px, py, c), device_id_type=MESH)

        def d2d(a, j, h):
            px, py = chips[j]
            part = half(outs[a].at[2 * px + py], a, h)
            return pltpu.make_async_remote_copy(part, part, fsend.at[a, j], frecv.at[a, j], device_id=(x, y, 1 - c), device_id_type=MESH)

        sends = [ici(a, j, ins[a], k) for a in range(n) for j in range(3)]
        for cp in sends:
            cp.start()
        passed = []
        for a in range(n):
            for j, (px, py) in enumerate(chips):
                ici(a, j, ins[a], 2 * px + py).wait_recv()
                if split[a]:
                    passed.append(d2d(a, j, c))
                    passed[-1].start()
        for a in range(n):
            if split[a]:
                for j in range(3):
                    d2d(a, j, 1 - c).wait_recv()
        for cp in sends + passed:
            cp.wait_send()

    return pl.pallas_call(
        body, in_specs=[ANY] * n, out_specs=[ANY] * n,
        out_shape=[_sds((N_CHIPS,) + s.shape, s.dtype) for s in shards],
        scratch_shapes=[pltpu.SemaphoreType.DMA((n, 3))] * 4,
        compiler_params=pltpu.CompilerParams(has_side_effects=True), name=name)(*shards)


HBM = pl.BlockSpec(memory_space=pltpu.HBM)
SEM = pl.BlockSpec(memory_space=pltpu.SEMAPHORE)
DATAFLOW = pltpu.SideEffectType.DATAFLOW_SIDE_EFFECTING


def _spread_start(groups, slot_src, after, name):
    flat = [a for grp in groups for a in grp]
    n = len(flat)
    ng = len(groups)
    offs = [sum(len(g) for g in groups[:i]) for i in range(ng)]
    lshape = [(a.shape if slot_src else (N_CHIPS,) + a.shape) for a in flat]

    nsem = 6 * n

    def body(*refs):
        src, land = refs[:n], refs[n:2 * n]
        sems = refs[2 * n + 1:2 * n + 1 + nsem]
        token = refs[-1]
        x, y, c = _place()
        k = 2 * x + y
        chips = [(1 - x, y), (x, 1 - y), (1 - x, 1 - y)]
        for a in range(n):
            for j, (px, py) in enumerate(chips):
                s = src[a].at[2 * px + py] if slot_src else src[a]
                pltpu.make_async_remote_copy(s, land[a].at[k], sems[6 * a + 2 * j], sems[6 * a + 2 * j + 1],
                                             device_id=(px, py, c), device_id_type=MESH).start()
        token[...] = jnp.zeros(token.shape, token.dtype)

    out_shape = [pltpu.SemaphoreType.DMA(())] * nsem
    out_shape += [pltpu.HBM(a.shape, a.dtype) for a in flat] + [pltpu.HBM(s, a.dtype) for s, a in zip(lshape, flat)]
    out_shape.append(_sds((SUBLANES, LANES)))
    args = [pltpu.with_memory_space_constraint(a, pltpu.HBM) for a in flat]
    args += [pltpu.with_memory_space_constraint(lax.empty(s, a.dtype), pltpu.HBM) for s, a in zip(lshape, flat)]
    res = pl.pallas_call(
        body, name=name, out_shape=tuple(out_shape), in_specs=[HBM] * (2 * n) + [pl.BlockSpec(memory_space=pl.ANY)],
        out_specs=tuple([SEM] * nsem + [HBM] * (2 * n) + [pl.BlockSpec(memory_space=pltpu.VMEM)]),
        input_output_aliases={i: nsem + i for i in range(2 * n)},
        compiler_params=pltpu.CompilerParams(has_side_effects=DATAFLOW))(*args, after)
    sems, thru, token = res[:nsem], res[nsem:nsem + 2 * n], res[-1]
    out = []
    for gi, grp in enumerate(groups):
        sl = slice(offs[gi], offs[gi] + len(grp))
        out.append((list(sems[6 * offs[gi]:6 * (offs[gi] + len(grp))]), list(thru[:n][sl]), list(thru[n:][sl])))
    return out, token


def _spread_wait(started, slot_src, after, name):
    sems, srcs, lands = started
    n = len(srcs)

    def body(*refs):
        src, land = refs[:n], refs[n:2 * n]
        sem = refs[2 * n:2 * n + 6 * n]
        x, y, c = _place()
        chips = [(1 - x, y), (x, 1 - y), (1 - x, 1 - y)]
        for a in range(n):
            for j, (px, py) in enumerate(chips):
                s = src[a].at[2 * px + py] if slot_src else src[a]
                cp = pltpu.make_async_remote_copy(s, land[a].at[2 * px + py], sem[6 * a + 2 * j], sem[6 * a + 2 * j + 1],
                                                  device_id=(px, py, c), device_id_type=MESH)
                cp.wait_send()
                cp.wait_recv()

    res = pl.pallas_call(
        body, name=name, out_shape=tuple([pltpu.HBM(a.shape, a.dtype) for a in srcs] + [pltpu.HBM(a.shape, a.dtype) for a in lands]),
        in_specs=[HBM] * (2 * n) + [SEM] * (6 * n) + [pl.BlockSpec(memory_space=pl.ANY)], out_specs=tuple([HBM] * (2 * n)),
        input_output_aliases={i: i for i in range(2 * n)},
        compiler_params=pltpu.CompilerParams(has_side_effects=DATAFLOW))(*srcs, *lands, *sems, after)
    return list(res[:n]), list(res[n:])


def _sibling_exchange(fs, name):
    n = len(fs)

    def body(*refs):
        ins, outs = refs[:n], refs[n:2 * n]
        send, recv = refs[2 * n:]
        x, y, c = _place()
        cps = [pltpu.make_async_remote_copy(ins[a], outs[a], send.at[a], recv.at[a],
                                            device_id=(x, y, 1 - c), device_id_type=MESH) for a in range(n)]
        for cp in cps:
            cp.start()
        for cp in cps:
            cp.wait()

    return pl.pallas_call(
        body, in_specs=[ANY] * n, out_specs=[ANY] * n, out_shape=[_sds(f.shape, f.dtype) for f in fs],
        scratch_shapes=[pltpu.SemaphoreType.DMA((n,)), pltpu.SemaphoreType.DMA((n,))],
        compiler_params=pltpu.CompilerParams(has_side_effects=True), name=name)(*fs)


def _tile2d(rows, cols, budget=1024 * 1024, step=2 * SUBLANES):
    fits = [t for t in range(step, rows + 1, step) if rows % t == 0 and t * cols * 4 <= budget]
    if fits:
        return fits[-1], cols
    fits = [t for t in range(LANES, cols + 1, LANES) if cols % t == 0 and rows * t * 4 <= budget]
    assert fits, (rows, cols)
    return rows, fits[-1]


def _chip_sum(own, parts, kidx, name):
    _, R, C = parts.shape
    tr, tc = _tile2d(R, C)

    def body(k_ref, o_ref_in, p1_ref, p2_ref, p3_ref, o_ref):
        o_ref[...] = ((o_ref_in[...].astype(F32) + p1_ref[...].astype(F32)) + p2_ref[...].astype(F32)) + p3_ref[...].astype(F32)

    def slot(d):
        return pl.BlockSpec((None, tr, tc), lambda i, j, k: ((k[0] + d) % N_CHIPS, i, j))

    return pl.pallas_call(
        body,
        grid_spec=pltpu.PrefetchScalarGridSpec(
            num_scalar_prefetch=1, grid=(R // tr, C // tc), in_specs=[slot(0), slot(1), slot(2), slot(3)],
            out_specs=pl.BlockSpec((tr, tc), lambda i, j, k: (i, j))),
        out_shape=_sds((R, C)), compiler_params=_cp("parallel", "parallel"), name=name)(kidx, own, parts, parts, parts)


def _adamw_math(w, g, m, v):
    m2 = ADAM_B1 * m + (1.0 - ADAM_B1) * g
    v2 = ADAM_B2 * v + (1.0 - ADAM_B2) * (g * g)
    m_hat = m2 / (1.0 - ADAM_B1 ** ADAM_STEP)
    v_hat = v2 / (1.0 - ADAM_B2 ** ADAM_STEP)
    delta = -ADAM_LR * (m_hat / (jnp.sqrt(v_hat) + ADAM_EPS) + ADAM_WD * w)
    return delta, m2, v2


def _adamw(w, m, v, gparts, name):
    Lw, R, C = w.shape
    tr, tc = _tile2d(R, C)
    flat = [h for pair in gparts for h in pair]

    def body(*refs):
        w_ref, m_ref, v_ref = refs[:3]
        g_refs = refs[3:3 + 2 * Lw]
        go_ref, d_ref, mo_ref, vo_ref = refs[3 + 2 * Lw:]
        g = g_refs[0][...] + g_refs[1][...]
        for l in range(1, Lw):
            g = jnp.where(pl.program_id(0) == l, g_refs[2 * l][...] + g_refs[2 * l + 1][...], g)
        d, m2, v2 = _adamw_math(w_ref[...], g, m_ref[...], v_ref[...])
        go_ref[...] = g
        d_ref[...] = d
        mo_ref[...] = m2
        vo_ref[...] = v2

    blk = pl.BlockSpec((None, tr, tc), lambda l, i, j: (l, i, j))
    gblk = pl.BlockSpec((tr, tc), lambda l, i, j: (i, j))
    return pl.pallas_call(
        body, grid=(Lw, R // tr, C // tc), in_specs=[blk, blk, blk] + [gblk] * (2 * Lw), out_specs=[blk] * 4,
        out_shape=[_sds((Lw, R, C))] * 4, compiler_params=_cp("parallel", "parallel", "parallel"), name=name)(w, m, v, *flat)


def _small_adamw(grads, wmv, name):
    n = len(grads)

    def body(*refs):
        g_in, p_in, outs = refs[:n], refs[n:4 * n], refs[4 * n:]
        for a in range(n):
            g = g_in[a][...]
            d_, m2, v2 = _adamw_math(p_in[3 * a][...], g, p_in[3 * a + 1][...], p_in[3 * a + 2][...])
            outs[4 * a][...] = g
            outs[4 * a + 1][...] = d_
            outs[4 * a + 2][...] = m2
            outs[4 * a + 3][...] = v2

    vm = pl.BlockSpec(memory_space=pltpu.VMEM)
    args = list(grads) + [t for tri in wmv for t in tri]
    out_shape = [_sds(g.shape) for g in grads for _ in range(4)]
    return pl.pallas_call(body, in_specs=[vm] * len(args), out_specs=[vm] * len(out_shape), out_shape=out_shape,
                          compiler_params=pltpu.CompilerParams(vmem_limit_bytes=V7X_VMEM_LIMIT), name=name)(*args)


def _small_allreduce(partials, pshapes, loss_row, name):
    n = len(partials)
    gshapes = [p.shape for p in partials] + [loss_row.shape]
    ng = n + 1

    def body(*refs):
        g_in = refs[:ng]
        outs = refs[ng:2 * ng]
        bufs = refs[2 * ng:3 * ng]
        send, recv = refs[-2:]
        x, y, c = _place()
        me = 4 * x + 2 * y + c
        k = 2 * x + y
        flips = [(fx, fy, fc) for fx in (0, 1) for fy in (0, 1) for fc in (0, 1)][1:]

        def peer(f):
            return (x ^ f[0], y ^ f[1], c ^ f[2])

        def slot(p):
            return 4 * p[0] + 2 * p[1] + p[2]

        for a in range(ng):
            bufs[a][me] = g_in[a][...]
        sends = [pltpu.make_async_remote_copy(g_in[a], bufs[a].at[me], send.at[a, j], recv.at[a, j],
                                              device_id=peer(f), device_id_type=MESH)
                 for a in range(ng) for j, f in enumerate(flips)]
        for cp in sends:
            cp.start()
        for a in range(ng):
            for j, f in enumerate(flips):
                pltpu.make_async_remote_copy(g_in[a], bufs[a].at[slot(peer(f))], send.at[a, j], recv.at[a, j],
                                             device_id=peer(f), device_id_type=MESH).wait_recv()
        for cp in sends:
            cp.wait_send()
        for a in range(ng):
            sharded = len(gshapes[a]) == 3

            def part(d):
                return bufs[a][d, k] if sharded else bufs[a][d]

            tot = part(0)
            for d in range(1, N_DEV):
                tot = tot + part(d)
            if a == n:
                outs[n][...] = tot
            else:
                pr, pc = pshapes[a]
                outs[a][...] = tot[:pr, :pc]

    vm = pl.BlockSpec(memory_space=pltpu.VMEM)
    args = list(partials) + [loss_row]
    out_shape = [_sds(ps) for ps in pshapes] + [_sds(loss_row.shape)]
    return pl.pallas_call(
        body, in_specs=[vm] * len(args), out_specs=[vm] * len(out_shape), out_shape=out_shape,
        scratch_shapes=[pltpu.VMEM((N_DEV,) + tuple(s), F32) for s in gshapes]
        + [pltpu.SemaphoreType.DMA((ng, N_DEV - 1)), pltpu.SemaphoreType.DMA((ng, N_DEV - 1))],
        compiler_params=pltpu.CompilerParams(has_side_effects=True, vmem_limit_bytes=V7X_VMEM_LIMIT), name=name)(*args)


_PERM = (0, 2, 1, 3)


def _cols_from_shards(g):
    return g.transpose(1, 0, 2).reshape(g.shape[1], N_CHIPS * g.shape[2])


def _rope_tables(positions):
    inv_freq = ROPE_THETA ** (-jnp.arange(0, HEAD_DIM, 2, dtype=F32) / HEAD_DIM)
    ang = positions.astype(F32).reshape(-1, 1) * inv_freq
    cos, sin = jnp.cos(ang), jnp.sin(ang)
    cos = jnp.concatenate([cos, cos, cos, cos], axis=-1)
    sin_s = jnp.concatenate([-sin, sin, -sin, sin], axis=-1)
    return cos, sin_s


def kernel(x, positions, norm_mix, norm_ffn, norm_final, mix_w_in, pool_w, pool_scale, attn_sinks, mix_w_out, ssm_w_in, ssm_conv_w, ssm_conv_b, ssm_dt_bias, ssm_A_log, ssm_D, ssm_norm, ssm_w_out, ffn_w_up, ffn_conv_w, ffn_conv_b, ffn_w_down, loss_target, m_norm_mix, m_norm_ffn, m_norm_final, m_mix_w_in, m_pool_w, m_pool_scale, m_attn_sinks, m_mix_w_out, m_ssm_w_in, m_ssm_conv_w, m_ssm_conv_b, m_ssm_dt_bias, m_ssm_A_log, m_ssm_D, m_ssm_norm, m_ssm_w_out, m_ffn_w_up, m_ffn_conv_w, m_ffn_conv_b, m_ffn_w_down, v_norm_mix, v_norm_ffn, v_norm_final, v_mix_w_in, v_pool_w, v_pool_scale, v_attn_sinks, v_mix_w_out, v_ssm_w_in, v_ssm_conv_w, v_ssm_conv_b, v_ssm_dt_bias, v_ssm_A_log, v_ssm_D, v_ssm_norm, v_ssm_w_out, v_ffn_w_up, v_ffn_conv_w, v_ffn_conv_b, v_ffn_w_down):
    W = dict(norm_mix=norm_mix, norm_ffn=norm_ffn, norm_final=norm_final, mix_w_in=mix_w_in, pool_w=pool_w, pool_scale=pool_scale, attn_sinks=attn_sinks, mix_w_out=mix_w_out, ssm_w_in=ssm_w_in, ssm_conv_w=ssm_conv_w, ssm_conv_b=ssm_conv_b, ssm_dt_bias=ssm_dt_bias, ssm_A_log=ssm_A_log, ssm_D=ssm_D, ssm_norm=ssm_norm, ssm_w_out=ssm_w_out, ffn_w_up=ffn_w_up, ffn_conv_w=ffn_conv_w, ffn_conv_b=ffn_conv_b, ffn_w_down=ffn_w_down)
    Mo = dict(norm_mix=m_norm_mix, norm_ffn=m_norm_ffn, norm_final=m_norm_final, mix_w_in=m_mix_w_in, pool_w=m_pool_w, pool_scale=m_pool_scale, attn_sinks=m_attn_sinks, mix_w_out=m_mix_w_out, ssm_w_in=m_ssm_w_in, ssm_conv_w=m_ssm_conv_w, ssm_conv_b=m_ssm_conv_b, ssm_dt_bias=m_ssm_dt_bias, ssm_A_log=m_ssm_A_log, ssm_D=m_ssm_D, ssm_norm=m_ssm_norm, ssm_w_out=m_ssm_w_out, ffn_w_up=m_ffn_w_up, ffn_conv_w=m_ffn_conv_w, ffn_conv_b=m_ffn_conv_b, ffn_w_down=m_ffn_w_down)
    Vo = dict(norm_mix=v_norm_mix, norm_ffn=v_norm_ffn, norm_final=v_norm_final, mix_w_in=v_mix_w_in, pool_w=v_pool_w, pool_scale=v_pool_scale, attn_sinks=v_attn_sinks, mix_w_out=v_mix_w_out, ssm_w_in=v_ssm_w_in, ssm_conv_w=v_ssm_conv_w, ssm_conv_b=v_ssm_conv_b, ssm_dt_bias=v_ssm_dt_bias, ssm_A_log=v_ssm_A_log, ssm_D=v_ssm_D, ssm_norm=v_ssm_norm, ssm_w_out=v_ssm_w_out, ffn_w_up=v_ffn_w_up, ffn_conv_w=v_ffn_conv_w, ffn_conv_b=v_ffn_conv_b, ffn_w_down=v_ffn_w_down)

    kchip = 2 * lax.axis_index("x") + lax.axis_index("y")

    def own_slot(g, own):
        return lax.dynamic_update_slice_in_dim(g, own[None], kchip, axis=0)

    def tr(t):
        return jnp.swapaxes(t[0], 0, 1)

    later = dict(ffn0=[ffn_w_up[0].astype(MXU), ffn_w_down[0].astype(MXU)],
                 ssm=[tr(ssm_w_in).astype(MXU), ssm_w_out[0].astype(MXU)],
                 ffn1=[ffn_w_up[1].astype(MXU), ffn_w_down[1].astype(MXU)])
    sh = [tr(mix_w_in).astype(MXU), mix_w_out[0].astype(MXU), ssm_conv_w[0], ssm_conv_b, ssm_norm, ffn_conv_w]
    first = _gather_shards(sh, "gather_first")
    g_mi, g_mo, g_scw, g_scb, g_sn, g_fcw = [own_slot(g, own) for g, own in zip(first, sh)]
    started, token = _spread_start(list(later.values()), False, first[0], "gather_start")
    started = dict(zip(later.keys(), started))
    fcw = [jnp.concatenate([g_fcw[p, i] for p in _PERM], axis=1) for i in range(2)]
    P = dict(
        nm=norm_mix, nf=norm_ffn, nfin=norm_final,
        wmiT=g_mi.reshape(MIX_IN_DIM, D_MODEL), wmo=g_mo.reshape(D_MODEL, D_MODEL),
        pool_w=pool_w[0], pool_scale=pool_scale, sinks=attn_sinks[0],
        scw=_cols_from_shards(g_scw), scb=g_scb.reshape(1, SSM_CONV_DIM), snorm=g_sn.reshape(1, SSM_D_INNER),
        dt_bias=jnp.pad(ssm_dt_bias, ((0, 0), (0, LANES - SSM_HEADS))), a_log=jnp.pad(ssm_A_log, ((0, 0), (0, LANES - SSM_HEADS))),
        d_exp=jnp.repeat(ssm_D, SSM_D_INNER // SSM_HEADS, axis=1),
        fcb=[jnp.concatenate([ffn_conv_b[i:i + 1, p * FFN_TC:(p + 1) * FFN_TC] for p in _PERM], axis=1) for i in range(2)],
    )

    def fetch(group, after):
        owns, lands = _spread_wait(started[group], False, after, f"gather_wait_{group}")
        a, b = [own_slot(g, own) for g, own in zip(lands, owns)]
        if group == "ssm":
            wsi = a.reshape(SSM_IN_DIM, D_MODEL)
            zx = SSM_D_INNER + SSM_CONV_DIM
            return dict(wzT=wsi[:SSM_D_INNER], wxbcT=wsi[SSM_D_INNER:zx],
                        wdtT=jnp.pad(wsi[zx:], ((0, LANES - SSM_HEADS), (0, 0))), wso=b.reshape(SSM_D_INNER, D_MODEL))
        i = int(group[-1])
        return dict(wup=jnp.concatenate([a[p] for p in _PERM], axis=1), wdn=b.reshape(D_FF, D_MODEL), fcw=fcw[i])

    cos, sin_s = _rope_tables(positions)
    sent = {}

    def send(group, grads):
        res, tok = _spread_start([grads], True, jnp.zeros((SUBLANES, LANES), F32), f"grad_start_{group}")
        sent[group] = res[0]
        return tok

    loss_row, grad_x, big, small = _local_step(x[0], cos, sin_s, loss_target[0], P, fetch, token, send)

    kidx = kchip.astype(jnp.int32).reshape(1)
    group_names = dict(ffn1=["ffn_w_up1", "ffn_w_down1"], ssm=["ssm_w_in", "ssm_w_out"], ffn0=["ffn_w_up0", "ffn_w_down0"],
                       mix=["mix_w_in", "mix_w_out"])
    names, mine = [], []
    for group, started_g in sent.items():
        grads, lands = _spread_wait(started_g, True, grad_x, f"grad_wait_{group}")
        for nm, g, land in zip(group_names[group], grads, lands):
            names.append(nm)
            mine.append(_chip_sum(g, land, kidx, f"chip_sum_{nm}"))
    theirs = _sibling_exchange(mine, "sibling_exchange")
    red = {nm: (a, b) for nm, a, b in zip(names, mine, theirs)}

    out = {}

    def big_update(pname, gparts, transposed=False):
        w = W[pname]
        lw = len(gparts)
        shp = w.shape
        rr, cc = gparts[0][0].shape
        fix = (lambda t: tr(t)[None]) if transposed else (lambda t: t.reshape(lw, rr, cc))
        res = _adamw(fix(w), fix(Mo[pname]), fix(Vo[pname]), gparts, f"adamw_{pname}")
        out[pname] = tuple((tr(r)[None] if transposed else r.reshape(shp)) for r in res)

    big_update("mix_w_in", [red["mix_w_in"]], transposed=True)
    big_update("mix_w_out", [red["mix_w_out"]])
    big_update("ssm_w_in", [red["ssm_w_in"]], transposed=True)
    big_update("ssm_w_out", [red["ssm_w_out"]])
    big_update("ffn_w_up", [red["ffn_w_up0"], red["ffn_w_up1"]])
    big_update("ffn_w_down", [red["ffn_w_down0"], red["ffn_w_down1"]])

    small_names = ["norm_mix", "norm_ffn", "norm_final", "pool_w", "pool_scale", "attn_sinks", "ssm_dt_bias", "ssm_A_log",
                   "ssm_D", "ffn_conv_b", "ssm_conv_w", "ssm_conv_b", "ssm_norm", "ffn_conv_w"]

    def as2d(t):
        if t.ndim == 1:
            return t.reshape(1, -1)
        return t.reshape(-1, t.shape[-1])

    wmv = [(as2d(W[nm]), as2d(Mo[nm]), as2d(Vo[nm])) for nm in small_names]
    summed = _small_allreduce([small[nm] for nm in small_names], [t[0].shape for t in wmv], loss_row, "small_allreduce")
    res = _small_adamw(summed[:-1], wmv, "small_adamw")
    for a, nm in enumerate(small_names):
        out[nm] = tuple(r.reshape(W[nm].shape) for r in res[4 * a:4 * a + 4])
    loss = summed[-1][0, 0]

    order = ["norm_mix", "norm_ffn", "norm_final", "mix_w_in", "pool_w", "pool_scale", "attn_sinks", "mix_w_out", "ssm_w_in",
             "ssm_conv_w", "ssm_conv_b", "ssm_dt_bias", "ssm_A_log", "ssm_D", "ssm_norm", "ssm_w_out", "ffn_w_up", "ffn_conv_w",
             "ffn_conv_b", "ffn_w_down"]
    return (loss, grad_x.reshape(x.shape), *[out[nm][0] for nm in order], *[out[nm][1] for nm in order],
            *[out[nm][2] for nm in order], *[out[nm][3] for nm in order])
```

```python
import functools

import jax
import jax.numpy as jnp
from jax import lax
from jax.experimental import pallas as pl
from jax.experimental.pallas import tpu as pltpu

F32 = jnp.float32
BF16 = jnp.bfloat16
MXU = BF16
HI = lax.Precision.HIGHEST

D_MODEL = 1024
POOL_WINDOWS = (2, 4, 8, 16)
POOL_DIM = 512
POOL_GROUP = 128
HEAD_DIM = 64
N_HEADS = 8
N_KV_HEADS = 2
GQ = 4
Q_DIM = 512
KV_DIM = 128
BLOCK = 128
ROPE_THETA = 10000.0
MIX_IN_DIM = 1280
SSM_D_INNER = 2048
SSM_HEADS = 32
SSM_GROUPS = 8
SSM_STATE = 128
SSM_CONV = 4
SSM_CHUNK = 128
SSM_CONV_DIM = 4096
SSM_IN_DIM = 6176
D_FF = 2816
FFN_CONV = 3
NORM_EPS = 1e-6
SSM_NORM_EPS = 1e-5
ADAM_LR = 0.001
ADAM_B1 = 0.9
ADAM_B2 = 0.999
ADAM_EPS = 1e-08
ADAM_WD = 0.01
ADAM_STEP = 10

N_CHIPS = 4
N_DEV = 8
LANES = 128
SUBLANES = 8
V7X_VMEM_LIMIT = 56 * 1024 * 1024
NEG = -1e30
MESH = pl.DeviceIdType.MESH


def _cp(*sem):
    return pltpu.CompilerParams(dimension_semantics=sem if sem else None, vmem_limit_bytes=V7X_VMEM_LIMIT)


def _sds(shape, dtype=F32):
    return jax.ShapeDtypeStruct(tuple(shape), dtype)


def _iota(shape, dim):
    return lax.broadcasted_iota(jnp.int32, shape, dim)


def _silu(x):
    return x * (1.0 / (1.0 + jnp.exp(-x)))


def _dsilu(x):
    s = 1.0 / (1.0 + jnp.exp(-x))
    return s * (1.0 + x * (1.0 - s))


def _mm(a, b, *, ta=False, tb=False, tm, tn, tk, res=None, out_dtype=F32, out_shard_perm=None, name):
    M, K = (a.shape[1], a.shape[0]) if ta else a.shape
    N = b.shape[0] if tb else b.shape[1]
    tm, tn, tk = min(tm, M), min(tn, N), min(tk, K)
    gm, gn, gk = M // tm, N // tn, K // tk
    assert gm * tm == M and gn * tn == N and gk * tk == K, (name, M, N, K, tm, tn, tk)
    a_spec = pl.BlockSpec((tk, tm), lambda i, j, k: (k, i)) if ta else pl.BlockSpec((tm, tk), lambda i, j, k: (i, k))
    b_spec = pl.BlockSpec((tn, tk), lambda i, j, k: (j, k)) if tb else pl.BlockSpec((tk, tn), lambda i, j, k: (k, j))
    dims = (((0 if ta else 1,), (1 if tb else 0,)), ((), ()))
    has_res = res is not None

    def body(*refs):
        a_ref, b_ref = refs[0], refs[1]
        r_ref = refs[2] if has_res else None
        o_ref = refs[3] if has_res else refs[2]
        def dot():
            return lax.dot_general(a_ref[...].astype(MXU), b_ref[...].astype(MXU), dims, preferred_element_type=F32)

        if gk == 1:
            p = dot()
            if has_res:
                p = p + r_ref[...]
            o_ref[...] = p.astype(out_dtype)
        else:
            acc = refs[-1]
            k = pl.program_id(2)

            @pl.when(k == 0)
            def _():
                acc[...] = dot()

            if gk > 2:
                @pl.when(jnp.logical_and(k > 0, k < gk - 1))
                def _():
                    acc[...] += dot()

            @pl.when(k == gk - 1)
            def _():
                r = acc[...] + dot()
                if has_res:
                    r = r + r_ref[...]
                o_ref[...] = r.astype(out_dtype)

    in_specs = [a_spec, b_spec]
    args = [a, b]
    if has_res:
        in_specs.append(pl.BlockSpec((tm, tn), lambda i, j, k: (i, j)))
        args.append(res)
    if out_shard_perm is None:
        out_spec = pl.BlockSpec((tm, tn), lambda i, j, k: (i, j))
        out_shape = _sds((M, N), out_dtype)
    else:
        assert gn == len(out_shard_perm) == 4 and tuple(out_shard_perm) == (0, 2, 1, 3)
        out_spec = pl.BlockSpec((None, tm, tn), lambda i, j, k: ((j % 2) * 2 + j // 2, i, 0))
        out_shape = _sds((gn, M, tn), out_dtype)
    return pl.pallas_call(
        body, grid=(gm, gn, gk), in_specs=in_specs, out_specs=out_spec, out_shape=out_shape,
        scratch_shapes=[pltpu.VMEM((tm, tn), F32)] if gk > 1 else [],
        compiler_params=_cp("parallel", "parallel", "arbitrary"), name=name)(*args)


def _rmsnorm_fwd(x, w, name, token=None):
    T, D = x.shape
    tm = min(T, 512)
    has_token = token is not None

    def body(*refs):
        x_ref, w_ref, o_ref = refs[0], refs[1], refs[-1]
        xv = x_ref[...]
        if has_token:
            xv = xv + refs[2][0:1, 0:1]
        r = lax.rsqrt(jnp.mean(xv * xv, axis=-1, keepdims=True) + NORM_EPS)
        o_ref[...] = (xv * r * w_ref[...]).astype(o_ref.dtype)

    in_specs = [pl.BlockSpec((tm, D), lambda i: (i, 0)), pl.BlockSpec((1, D), lambda i: (0, 0))]
    args = [x, w.reshape(1, D)]
    if has_token:
        in_specs.append(pl.BlockSpec((SUBLANES, LANES), lambda i: (0, 0)))
        args.append(token)
    return pl.pallas_call(
        body, grid=(T // tm,), in_specs=in_specs,
        out_specs=pl.BlockSpec((tm, D), lambda i: (i, 0)), out_shape=_sds((T, D), MXU),
        compiler_params=_cp("parallel"), name=name)(*args)


def _rmsnorm_bwd(x, w, dh, dres, name, token=None):
    T, D = x.shape
    tm = min(T, 512)
    has_token = token is not None

    def body(*refs):
        x_ref, w_ref, dh_ref, dr_ref = refs[:4]
        dx_ref, dw_ref = refs[-2:]
        xv = x_ref[...]
        r = lax.rsqrt(jnp.mean(xv * xv, axis=-1, keepdims=True) + NORM_EPS)
        xh = xv * r
        dh = dh_ref[...]
        g = dh * w_ref[...]
        dr = dr_ref[...] + refs[4][0:1, 0:1] if has_token else dr_ref[...]
        dx_ref[...] = dr + r * (g - xh * jnp.mean(g * xh, axis=-1, keepdims=True))
        part = jnp.sum(dh * xh, axis=0, keepdims=True)

        @pl.when(pl.program_id(0) == 0)
        def _():
            dw_ref[...] = part

        @pl.when(pl.program_id(0) > 0)
        def _():
            dw_ref[...] += part

    row = pl.BlockSpec((tm, D), lambda i: (i, 0))
    vec = pl.BlockSpec((1, D), lambda i: (0, 0))
    in_specs = [row, vec, row, row]
    args = [x, w.reshape(1, D), dh, dres]
    if has_token:
        in_specs.append(pl.BlockSpec((SUBLANES, LANES), lambda i: (0, 0)))
        args.append(token)
    return pl.pallas_call(
        body, grid=(T // tm,), in_specs=in_specs, out_specs=[row, vec],
        out_shape=[_sds((T, D)), _sds((1, D))], compiler_params=_cp("arbitrary"), name=name)(*args)


def _loss_head(x, w, target, name):
    T, D = x.shape
    tm = min(T, 512)

    def body(x_ref, w_ref, t_ref, loss_ref, dx_ref, dw_ref):
        xv = x_ref[...]
        r = lax.rsqrt(jnp.mean(xv * xv, axis=-1, keepdims=True) + NORM_EPS)
        xh = xv * r
        wv = w_ref[...]
        e = xh * wv - t_ref[...]
        lpart = 0.5 * jnp.sum(jnp.mean(e * e, axis=-1, keepdims=True), axis=0, keepdims=True)
        dy = e * (1.0 / D)
        g = dy * wv
        dx_ref[...] = r * (g - xh * jnp.mean(g * xh, axis=-1, keepdims=True))
        part = jnp.sum(dy * xh, axis=0, keepdims=True)
        lrow = jnp.broadcast_to(lpart, (1, LANES))

        @pl.when(pl.program_id(0) == 0)
        def _():
            dw_ref[...] = part
            loss_ref[...] = lrow

        @pl.when(pl.program_id(0) > 0)
        def _():
            dw_ref[...] += part
            loss_ref[...] += lrow

    row = pl.BlockSpec((tm, D), lambda i: (i, 0))
    vec = pl.BlockSpec((1, D), lambda i: (0, 0))
    return pl.pallas_call(
        body, grid=(T // tm,), in_specs=[row, vec, row],
        out_specs=[pl.BlockSpec((1, LANES), lambda i: (0, 0)), row, vec],
        out_shape=[_sds((1, LANES)), _sds((T, D)), _sds((1, D))],
        compiler_params=_cp("arbitrary"), name=name)(x, w.reshape(1, D), target)


def _shift_down(cur, prev8, s):
    if s == 0:
        return cur
    tm = cur.shape[0]
    rc = pltpu.roll(cur, s, 0)
    top = jnp.where(_iota((SUBLANES, cur.shape[1]), 0) < s, pltpu.roll(prev8, s, 0), rc[:SUBLANES])
    return jnp.concatenate([top, rc[SUBLANES:]], axis=0) if tm > SUBLANES else top


def _shift_up(cur, next8, s):
    if s == 0:
        return cur
    tm = cur.shape[0]
    rc = pltpu.roll(cur, tm - s, 0)
    bot = jnp.where(_iota((SUBLANES, cur.shape[1]), 0) >= SUBLANES - s, pltpu.roll(next8, SUBLANES - s, 0), rc[tm - SUBLANES:])
    return jnp.concatenate([rc[:tm - SUBLANES], bot], axis=0) if tm > SUBLANES else bot


def _conv_rows(cur, prev8, w, b, K):
    acc = cur * w[K - 1:K, :] + b
    for s in range(1, K):
        acc = acc + _shift_down(cur, prev8, s) * w[K - 1 - s:K - s, :]
    return acc


FFN_TC = 1408
HALO16 = 2 * SUBLANES


def _ffn_mid_fwd(hid, cw, cb, name):
    T = hid.shape[0]
    tm = min(T, 256)
    nt, nj = T // tm, D_FF // FFN_TC
    K = FFN_CONV

    q = tm // HALO16

    def body(h_ref, hp_ref, w_ref, b_ref, o_ref, hc_ref):
        i = pl.program_id(0)
        cur = h_ref[...].astype(F32)
        prev8 = jnp.where(i > 0, hp_ref[...].astype(F32)[HALO16 - SUBLANES:], 0.0)
        hc = _conv_rows(cur, prev8, w_ref[...], b_ref[...], K)
        hc_ref[...] = hc
        o_ref[...] = (_silu(hc[:, FFN_TC:]) * hc[:, :FFN_TC]).astype(o_ref.dtype)

    return pl.pallas_call(
        body, grid=(nt, nj),
        in_specs=[pl.BlockSpec((tm, 2 * FFN_TC), lambda i, j: (i, j)),
                  pl.BlockSpec((HALO16, 2 * FFN_TC), lambda i, j: (jnp.maximum(i * q - 1, 0), j)),
                  pl.BlockSpec((K, 2 * FFN_TC), lambda i, j: (0, j)), pl.BlockSpec((1, 2 * FFN_TC), lambda i, j: (0, j))],
        out_specs=[pl.BlockSpec((tm, FFN_TC), lambda i, j: (i, j)), pl.BlockSpec((tm, 2 * FFN_TC), lambda i, j: (i, j))],
        out_shape=[_sds((T, D_FF), MXU), _sds((T, 2 * D_FF))],
        compiler_params=_cp("parallel", "parallel"), name=name)(hid, hid, cw, cb)


def _ffn_mid_bwd(hid, hc, cw, da, name):
    T = hid.shape[0]
    tm = min(T, 256)
    nt, nj = T // tm, D_FF // FFN_TC
    K = FFN_CONV
    W2 = 2 * FFN_TC

    def body(h_ref, c_ref, cn_ref, da_ref, dan_ref, w_ref, dh_ref, dw_ref, db_ref):
        i = pl.program_id(1)
        w = w_ref[...]
        cur = h_ref[...].astype(F32)
        last = i == nt - 1

        def dpre(hcv, dav):
            u, g = hcv[:, :FFN_TC], hcv[:, FFN_TC:]
            return jnp.concatenate([dav * _silu(g), dav * u * _dsilu(g)], axis=1)

        d_cur = dpre(c_ref[...], da_ref[...])
        d_nxt = jnp.where(last, 0.0, dpre(cn_ref[...], dan_ref[...]))
        ups = [d_cur] + [_shift_up(d_cur, d_nxt, s) for s in range(1, K)]
        dh = ups[0] * w[K - 1:K, :]
        for s in range(1, K):
            dh = dh + ups[s] * w[K - 1 - s:K - s, :]
        dh_ref[...] = dh.astype(dh_ref.dtype)
        dwp = jnp.concatenate([jnp.sum(ups[K - 1 - k] * cur, axis=0, keepdims=True) for k in range(K)], axis=0)
        dbp = jnp.sum(d_cur, axis=0, keepdims=True)

        @pl.when(i == 0)
        def _():
            dw_ref[...] = dwp
            db_ref[...] = dbp

        @pl.when(i > 0)
        def _():
            dw_ref[...] += dwp
            db_ref[...] += dbp

    q = tm // SUBLANES
    blk = pl.BlockSpec((tm, W2), lambda j, i: (i, j))
    nxt = pl.BlockSpec((SUBLANES, W2), lambda j, i: (jnp.minimum((i + 1) * q, nt * q - 1), j))
    dab = pl.BlockSpec((tm, FFN_TC), lambda j, i: (i, j))
    dan = pl.BlockSpec((SUBLANES, FFN_TC), lambda j, i: (jnp.minimum((i + 1) * q, nt * q - 1), j))
    return pl.pallas_call(
        body, grid=(nj, nt),
        in_specs=[blk, blk, nxt, dab, dan, pl.BlockSpec((K, W2), lambda j, i: (0, j))],
        out_specs=[blk, pl.BlockSpec((K, W2), lambda j, i: (0, j)), pl.BlockSpec((1, W2), lambda j, i: (0, j))],
        out_shape=[_sds((T, 2 * D_FF), MXU), _sds((K, 2 * D_FF)), _sds((1, 2 * D_FF))],
        compiler_params=_cp("parallel", "arbitrary"), name=name)(hid, hc, hc, da, da, cw)


def _rope(t, cos, sin_s, inverse=False):
    n = t.shape[1] // LANES
    c = jnp.concatenate([cos] * n, axis=1) if n > 1 else cos
    s = jnp.concatenate([sin_s] * n, axis=1) if n > 1 else sin_s
    a = pltpu.roll(t, HEAD_DIM // 2, 1)
    b = pltpu.roll(t, t.shape[1] - HEAD_DIM // 2, 1)
    first = (_iota(t.shape, 1) % HEAD_DIM) < HEAD_DIM // 2
    rot = jnp.where(first, b, a) * s
    return t * c - rot if inverse else t * c + rot


def _stack_heads(t, g):
    return jnp.concatenate([t[:, (GQ * g + r) * HEAD_DIM:(GQ * g + r + 1) * HEAD_DIM] for r in range(GQ)], axis=0)


def _stack_cols(t, g):
    return jnp.concatenate([t[:, GQ * g + r:GQ * g + r + 1] for r in range(GQ)], axis=0)


def _pool_sums(prev, cur, w):
    s = jnp.concatenate([prev, cur], axis=0)
    sh = 1
    while sh < w:
        s = s + pltpu.roll(s, sh, 0)
        sh *= 2
    return s[BLOCK:]


def _nt(a, b):
    return lax.dot_general(a.astype(MXU), b.astype(MXU), (((1,), (1,)), ((), ())), preferred_element_type=F32)


def _tn(a, b):
    return lax.dot_general(a.astype(MXU), b.astype(MXU), (((0,), (0,)), ((), ())), preferred_element_type=F32)


def _nn(a, b):
    return jnp.dot(a.astype(MXU), b.astype(MXU), preferred_element_type=F32)


def _mixcore_fwd(proj, cos, sin_s, pool_w, pool_scale, sinks, name):
    T = proj.shape[0]
    nb = T // BLOCK
    scale = HEAD_DIM ** -0.5

    def body(p_ref, pp_ref, c_ref, s_ref, cp_ref, sp_ref, pw_ref, ps_ref, sk_ref, cat_ref, at_ref, lse_ref):
        i = pl.program_id(0)
        has_prev = i > 0
        cur = p_ref[...]
        prv = jnp.where(has_prev, pp_ref[...], 0.0)
        tpos = (i * BLOCK + _iota((BLOCK, 1), 0) + 1).astype(F32)
        for g, w in enumerate(POOL_WINDOWS):
            sl = slice(g * POOL_GROUP, (g + 1) * POOL_GROUP)
            pooled = _pool_sums(prv[:, sl], cur[:, sl], w) / jnp.minimum(tpos, float(w)) - cur[:, sl]
            cat_ref[:, sl] = (_nn(pooled, pw_ref[g]) * ps_ref[:, sl]).astype(cat_ref.dtype)
        q = _rope(cur[:, POOL_DIM:POOL_DIM + Q_DIM], c_ref[...], s_ref[...])
        kc = _rope(cur[:, POOL_DIM + Q_DIM:POOL_DIM + Q_DIM + KV_DIM], c_ref[...], s_ref[...])
        kp = _rope(prv[:, POOL_DIM + Q_DIM:POOL_DIM + Q_DIM + KV_DIM], cp_ref[...], sp_ref[...])
        vc = cur[:, POOL_DIM + Q_DIM + KV_DIM:]
        vp = prv[:, POOL_DIM + Q_DIM + KV_DIM:]
        ri = _iota((GQ * BLOCK, BLOCK), 0) % BLOCK
        cj = _iota((GQ * BLOCK, BLOCK), 1)
        mc = cj <= ri
        mp = jnp.logical_and(cj > ri, has_prev)
        outs, lses = [], []
        for g in range(N_KV_HEADS):
            hs = slice(g * HEAD_DIM, (g + 1) * HEAD_DIM)
            qg = _stack_heads(q, g) * scale
            sc = jnp.where(mc, _nt(qg, kc[:, hs]), NEG)
            sp = jnp.where(mp, _nt(qg, kp[:, hs]), NEG)
            sink = jnp.concatenate([jnp.full((BLOCK, 1), sk_ref[GQ * g + r], F32) for r in range(GQ)], axis=0)
            m = jnp.maximum(jnp.maximum(jnp.max(sc, axis=1, keepdims=True), jnp.max(sp, axis=1, keepdims=True)), sink)
            pc = jnp.exp(sc - m)
            pp = jnp.exp(sp - m)
            den = jnp.sum(pc, axis=1, keepdims=True) + jnp.sum(pp, axis=1, keepdims=True) + jnp.exp(sink - m)
            o = (_nn(pc, vc[:, hs]) + _nn(pp, vp[:, hs])) / den
            lse = m + jnp.log(den)
            for r in range(GQ):
                outs.append(o[r * BLOCK:(r + 1) * BLOCK])
                lses.append(lse[r * BLOCK:(r + 1) * BLOCK])
        attn = jnp.concatenate(outs, axis=1)
        at_ref[...] = attn
        cat_ref[:, POOL_DIM:] = attn.astype(cat_ref.dtype)
        lane = _iota((BLOCK, LANES), 1)
        lrow = jnp.zeros((BLOCK, LANES), F32)
        for h in range(N_HEADS):
            lrow = jnp.where(lane == h, lses[h], lrow)
        lse_ref[...] = lrow

    cur = lambda w: pl.BlockSpec((BLOCK, w), lambda i: (i, 0))
    prv = lambda w: pl.BlockSpec((BLOCK, w), lambda i: (jnp.maximum(i - 1, 0), 0))
    return pl.pallas_call(
        body, grid=(nb,),
        in_specs=[cur(MIX_IN_DIM), prv(MIX_IN_DIM), cur(LANES), cur(LANES), prv(LANES), prv(LANES),
                  pl.BlockSpec((4, POOL_GROUP, POOL_GROUP), lambda i: (0, 0, 0)), pl.BlockSpec((1, POOL_DIM), lambda i: (0, 0)),
                  pl.BlockSpec(memory_space=pltpu.SMEM)],
        out_specs=[cur(2 * POOL_DIM), cur(Q_DIM), cur(LANES)],
        out_shape=[_sds((T, 2 * POOL_DIM), MXU), _sds((T, Q_DIM)), _sds((T, LANES))],
        compiler_params=_cp("parallel"), name=name)(proj, proj, cos, sin_s, cos, sin_s, pool_w, pool_scale, sinks)


def _mixcore_bwd(proj, cos, sin_s, pool_w, pool_scale, sinks, attn, lse, dcat, name):
    T = proj.shape[0]
    nb = T // BLOCK
    scale = HEAD_DIM ** -0.5
    QO, KO, VO = POOL_DIM, POOL_DIM + Q_DIM, POOL_DIM + Q_DIM + KV_DIM

    def body(p_ref, pp_ref, pn_ref, c_ref, s_ref, cp_ref, sp_ref, cn_ref, sn_ref, pw_ref, ps_ref, sk_ref,
             at_ref, atn_ref, l_ref, ln_ref, d_ref, dn_ref, dp_ref, dpw_ref, dps_ref, dsk_ref):
        i = pl.program_id(0)
        has_prev = i > 0
        has_next = i < nb - 1
        cur = p_ref[...]
        prv = jnp.where(has_prev, pp_ref[...], 0.0)
        d_cur = d_ref[...]
        d_nxt = jnp.where(has_next, dn_ref[...], 0.0)

        tpos = (i * BLOCK + _iota((BLOCK, 1), 0) + 1).astype(F32)
        tpos2 = (i * BLOCK + _iota((2 * BLOCK, 1), 0) + 1).astype(F32)
        ps = ps_ref[...]
        dps_parts, dpw_parts = [], []
        for g, w in enumerate(POOL_WINDOWS):
            sl = slice(g * POOL_GROUP, (g + 1) * POOL_GROUP)
            pooled = _pool_sums(prv[:, sl], cur[:, sl], w) / jnp.minimum(tpos, float(w)) - cur[:, sl]
            mixed = _nn(pooled, pw_ref[g])
            dps_parts.append(jnp.sum(d_cur[:, sl] * mixed, axis=0, keepdims=True))
            dm2 = jnp.concatenate([d_cur[:, sl], d_nxt[:, sl]], axis=0) * ps[:, sl]
            dpw_parts.append(_tn(pooled, dm2[:BLOCK]))
            dpool2 = _nt(dm2, pw_ref[g])
            e = dpool2 / jnp.minimum(tpos2, float(w))
            sh = 1
            while sh < w:
                e = e + pltpu.roll(e, 2 * BLOCK - sh, 0)
                sh *= 2
            dp_ref[:, sl] = (e[:BLOCK] - dpool2[:BLOCK]).astype(dp_ref.dtype)
        dpsp = jnp.concatenate(dps_parts, axis=1)

        nxt = pn_ref[...]
        q = _rope(cur[:, QO:KO], c_ref[...], s_ref[...])
        qn = _rope(nxt[:, QO:KO], cn_ref[...], sn_ref[...])
        kc = _rope(cur[:, KO:VO], c_ref[...], s_ref[...])
        kp = _rope(prv[:, KO:VO], cp_ref[...], sp_ref[...])
        vc, vp = cur[:, VO:], prv[:, VO:]
        do, don = d_cur[:, POOL_DIM:], d_nxt[:, POOL_DIM:]
        dl = do * at_ref[...]
        dln = don * atn_ref[...]
        lse, lsen = l_ref[...], ln_ref[...]
        ri = _iota((GQ * BLOCK, BLOCK), 0) % BLOCK
        cj = _iota((GQ * BLOCK, BLOCK), 1)
        mc = cj <= ri
        mp = jnp.logical_and(cj > ri, has_prev)
        mn = jnp.logical_and(cj > ri, has_next)
        dq_parts, dk_parts, dv_parts, dsk_vals = [], [], [], []
        for g in range(N_KV_HEADS):
            hs = slice(g * HEAD_DIM, (g + 1) * HEAD_DIM)
            qg, qng = _stack_heads(q, g) * scale, _stack_heads(qn, g) * scale
            dog, dong = _stack_heads(do, g), _stack_heads(don, g)
            delta = jnp.sum(_stack_heads(dl, g), axis=1, keepdims=True)
            deltan = jnp.sum(_stack_heads(dln, g), axis=1, keepdims=True)
            lg, lng = _stack_cols(lse, g), _stack_cols(lsen, g)
            pc = jnp.where(mc, jnp.exp(_nt(qg, kc[:, hs]) - lg), 0.0)
            pp = jnp.where(mp, jnp.exp(_nt(qg, kp[:, hs]) - lg), 0.0)
            pn = jnp.where(mn, jnp.exp(_nt(qng, kc[:, hs]) - lng), 0.0)
            dsc = pc * (_nt(dog, vc[:, hs]) - delta)
            dsp = pp * (_nt(dog, vp[:, hs]) - delta)
            dsn = pn * (_nt(dong, vc[:, hs]) - deltan)
            dqg = (_nn(dsc, kc[:, hs]) + _nn(dsp, kp[:, hs])) * scale
            dq_parts += [dqg[r * BLOCK:(r + 1) * BLOCK] for r in range(GQ)]
            dk_parts.append(_tn(dsc, qg) + _tn(dsn, qng))
            dv_parts.append(_tn(pc, dog) + _tn(pn, dong))
            sink = jnp.concatenate([jnp.full((BLOCK, 1), sk_ref[GQ * g + r], F32) for r in range(GQ)], axis=0)
            dsk = -jnp.exp(sink - lg) * delta
            dsk_vals += [jnp.sum(dsk[r * BLOCK:(r + 1) * BLOCK], axis=0, keepdims=True) for r in range(GQ)]
        dq = _rope(jnp.concatenate(dq_parts, axis=1), c_ref[...], s_ref[...], inverse=True)
        dk = _rope(jnp.concatenate(dk_parts, axis=1), c_ref[...], s_ref[...], inverse=True)
        dp_ref[:, QO:KO] = dq.astype(dp_ref.dtype)
        dp_ref[:, KO:VO] = dk.astype(dp_ref.dtype)
        dp_ref[:, VO:] = jnp.concatenate(dv_parts, axis=1).astype(dp_ref.dtype)
        lane = _iota((1, LANES), 1)
        dskp = jnp.zeros((1, LANES), F32)
        for h in range(N_HEADS):
            dskp = jnp.where(lane == h, dsk_vals[h], dskp)

        @pl.when(i == 0)
        def _():
            dps_ref[...] = dpsp
            dsk_ref[...] = dskp
            for g in range(4):
                dpw_ref[g] = dpw_parts[g]

        @pl.when(i > 0)
        def _():
            dps_ref[...] += dpsp
            dsk_ref[...] += dskp
            for g in range(4):
                dpw_ref[g] += dpw_parts[g]

    cur = lambda w: pl.BlockSpec((BLOCK, w), lambda i: (i, 0))
    prv = lambda w: pl.BlockSpec((BLOCK, w), lambda i: (jnp.maximum(i - 1, 0), 0))
    nxt = lambda w: pl.BlockSpec((BLOCK, w), lambda i: (jnp.minimum(i + 1, nb - 1), 0))
    return pl.pallas_call(
        body, grid=(nb,),
        in_specs=[cur(MIX_IN_DIM), prv(MIX_IN_DIM), nxt(MIX_IN_DIM),
                  cur(LANES), cur(LANES), prv(LANES), prv(LANES), nxt(LANES), nxt(LANES),
                  pl.BlockSpec((4, POOL_GROUP, POOL_GROUP), lambda i: (0, 0, 0)), pl.BlockSpec((1, POOL_DIM), lambda i: (0, 0)),
                  pl.BlockSpec(memory_space=pltpu.SMEM),
                  cur(Q_DIM), nxt(Q_DIM), cur(LANES), nxt(LANES), cur(2 * POOL_DIM), nxt(2 * POOL_DIM)],
        out_specs=[cur(MIX_IN_DIM), pl.BlockSpec((4, POOL_GROUP, POOL_GROUP), lambda i: (0, 0, 0)),
                   pl.BlockSpec((1, POOL_DIM), lambda i: (0, 0)), pl.BlockSpec((1, LANES), lambda i: (0, 0))],
        out_shape=[_sds((T, MIX_IN_DIM), MXU), _sds((4, POOL_GROUP, POOL_GROUP)), _sds((1, POOL_DIM)), _sds((1, LANES))],
        compiler_params=_cp("arbitrary"), name=name)(
            proj, proj, proj, cos, sin_s, cos, sin_s, cos, sin_s, pool_w, pool_scale, sinks, attn, attn, lse, lse, dcat, dcat)


SSM_TC = 512
GROUP_W = SSM_D_INNER // SSM_GROUPS


def _ssm_pre_fwd(xbc, cw, cb, name):
    T = xbc.shape[0]
    tm = min(T, 1024)
    K = SSM_CONV
    q = tm // SUBLANES

    def body(x_ref, xp_ref, w_ref, b_ref, o_ref, pre_ref):
        prev8 = jnp.where(pl.program_id(0) > 0, xp_ref[...], 0.0)
        pre = _conv_rows(x_ref[...], prev8, w_ref[...], b_ref[...], K)
        pre_ref[...] = pre
        o_ref[...] = _silu(pre)

    tc = 512
    blk = pl.BlockSpec((tm, tc), lambda i, j: (i, j))
    return pl.pallas_call(
        body, grid=(T // tm, SSM_CONV_DIM // tc),
        in_specs=[blk, pl.BlockSpec((SUBLANES, tc), lambda i, j: (jnp.maximum(i * q - 1, 0), j)),
                  pl.BlockSpec((K, tc), lambda i, j: (0, j)), pl.BlockSpec((1, tc), lambda i, j: (0, j))],
        out_specs=[blk, blk], out_shape=[_sds((T, SSM_CONV_DIM)), _sds((T, SSM_CONV_DIM))],
        compiler_params=_cp("parallel", "parallel"), name=name)(xbc, xbc, cw, cb)


def _ssm_pre_bwd(xbc, pre, cw, dact, name):
    T = xbc.shape[0]
    tm = min(T, 512)
    nt = T // tm
    K = SSM_CONV
    q = tm // SUBLANES
    tc = SSM_TC

    def body(x_ref, p_ref, pn_ref, d_ref, dn_ref, w_ref, dx_ref, dw_ref, db_ref):
        i = pl.program_id(1)
        w = w_ref[...]
        cur = x_ref[...]
        d_cur = d_ref[...] * _dsilu(p_ref[...])
        d_nxt = jnp.where(i == nt - 1, 0.0, dn_ref[...] * _dsilu(pn_ref[...]))
        ups = [d_cur] + [_shift_up(d_cur, d_nxt, s) for s in range(1, K)]
        dx = ups[0] * w[K - 1:K, :]
        for s in range(1, K):
            dx = dx + ups[s] * w[K - 1 - s:K - s, :]
        dx_ref[...] = dx.astype(dx_ref.dtype)
        dwp = jnp.concatenate([jnp.sum(ups[K - 1 - k] * cur, axis=0, keepdims=True) for k in range(K)], axis=0)
        dbp = jnp.sum(d_cur, axis=0, keepdims=True)

        @pl.when(i == 0)
        def _():
            dw_ref[...] = dwp
            db_ref[...] = dbp

        @pl.when(i > 0)
        def _():
            dw_ref[...] += dwp
            db_ref[...] += dbp

    nxt_row = lambda i: jnp.minimum((i + 1) * q, nt * q - 1)
    return pl.pallas_call(
        body, grid=(SSM_CONV_DIM // tc, nt),
        in_specs=[pl.BlockSpec((tm, tc), lambda j, i: (i, j)),
                  pl.BlockSpec((tm, tc), lambda j, i: (i, j)),
                  pl.BlockSpec((SUBLANES, tc), lambda j, i: (nxt_row(i), j)),
                  pl.BlockSpec((tm, tc), lambda j, i: (i, j)),
                  pl.BlockSpec((SUBLANES, tc), lambda j, i: (nxt_row(i), j)),
                  pl.BlockSpec((K, tc), lambda j, i: (0, j))],
        out_specs=[pl.BlockSpec((tm, tc), lambda j, i: (i, j)), pl.BlockSpec((K, tc), lambda j, i: (0, j)),
                   pl.BlockSpec((1, tc), lambda j, i: (0, j))],
        out_shape=[_sds((T, SSM_CONV_DIM), MXU), _sds((K, SSM_CONV_DIM)), _sds((1, SSM_CONV_DIM))],
        compiler_params=_cp("parallel", "arbitrary"), name=name)(xbc, pre, pre, dact, dact, cw)


def _dot_hi(a, b):
    return jnp.dot(a, b, precision=HI, preferred_element_type=F32)


def _ssd_common(dtraw, bias, alog):
    L = SSM_CHUNK
    xb = dtraw + bias
    dt = jnp.maximum(xb, 0.0) + jnp.log1p(jnp.exp(-jnp.abs(xb)))
    A = -jnp.exp(alog)
    tril = (_iota((L, L), 1) <= _iota((L, L), 0)).astype(F32)
    acs = _dot_hi(tril, dt * A)
    return xb, dt, A, tril, acs


def _head_selectors():
    es = (_iota((LANES, SSM_D_INNER), 0) == _iota((LANES, SSM_D_INNER), 1) // HEAD_DIM).astype(BF16)
    est = (_iota((SSM_D_INNER, LANES), 1) == _iota((SSM_D_INNER, LANES), 0) // HEAD_DIM).astype(BF16)
    return es, est


def _dot_sel(v, sel):
    hi = v.astype(BF16)
    r1 = v - hi.astype(F32)
    mid = r1.astype(BF16)
    lo = (r1 - mid.astype(F32)).astype(BF16)
    d = lambda a: jnp.dot(a, sel, preferred_element_type=F32)
    return (d(hi) + d(mid)) + d(lo)


def _expand_heads(v, es):
    return _dot_sel(v, es)


def _reduce_heads(q, est):
    return _dot_sel(q, est)


def _per_state_row(v, g):
    return jnp.concatenate([jnp.broadcast_to(v[:, GQ * g + r:GQ * g + r + 1], (HEAD_DIM, 1)) for r in range(GQ)], axis=0)


def _ssd_fwd(xact, dtraw, dt_bias, a_log, name):
    T = xact.shape[0]
    nc = T // SSM_CHUNK
    L = SSM_CHUNK
    BO, CO = SSM_D_INNER, SSM_D_INNER + SSM_GROUPS * SSM_STATE

    def body(x_ref, dt_ref, bias_ref, al_ref, es_ref, y_ref, st_ref, state):
        @pl.when(pl.program_id(0) == 0)
        def _():
            state[...] = jnp.zeros(state.shape, F32)

        _, dt, A, tril, acs = _ssd_common(dt_ref[...], bias_ref[...], al_ref[...])
        acsT = acs.T
        last = acs[L - 1:L, :]
        cd = jnp.exp(last)
        es = es_ref[...]
        dtX = _expand_heads(dt, es)
        EX = _expand_heads(jnp.exp(acs), es)
        decX = _expand_heads(jnp.exp(last - acs), es)
        for g in range(SSM_GROUPS):
            gs = slice(g * GROUP_W, (g + 1) * GROUP_W)
            B = x_ref[:, BO + g * SSM_STATE:BO + (g + 1) * SSM_STATE]
            C = x_ref[:, CO + g * SSM_STATE:CO + (g + 1) * SSM_STATE]
            X = x_ref[:, gs] * dtX[:, gs]
            CB = _nt(C, B)
            yd = []
            for r in range(GQ):
                h = GQ * g + r
                Lm = jnp.exp(jnp.where(tril > 0, acs[:, h:h + 1] - acsT[h:h + 1, :], NEG))
                yd.append(_nn(CB * Lm, X[:, r * HEAD_DIM:(r + 1) * HEAD_DIM]))
            S = state[g]
            st_ref[g] = S
            y_ref[:, gs] = jnp.concatenate(yd, axis=1) + _nt(C, S) * EX[:, gs]
            state[g] = S * _per_state_row(cd, g) + _tn(X * decX[:, gs], B)

    es, _ = _head_selectors()
    return pl.pallas_call(
        body, grid=(nc,),
        in_specs=[pl.BlockSpec((L, SSM_CONV_DIM), lambda c: (c, 0)), pl.BlockSpec((L, LANES), lambda c: (c, 0)),
                  pl.BlockSpec((1, LANES), lambda c: (0, 0)), pl.BlockSpec((1, LANES), lambda c: (0, 0)),
                  pl.BlockSpec((LANES, SSM_D_INNER), lambda c: (0, 0))],
        out_specs=[pl.BlockSpec((L, SSM_D_INNER), lambda c: (c, 0)),
                   pl.BlockSpec((None, SSM_GROUPS, GROUP_W, SSM_STATE), lambda c: (c, 0, 0, 0))],
        out_shape=[_sds((T, SSM_D_INNER)), _sds((nc, SSM_GROUPS, GROUP_W, SSM_STATE))],
        scratch_shapes=[pltpu.VMEM((SSM_GROUPS, GROUP_W, SSM_STATE), F32)],
        compiler_params=_cp("arbitrary"), name=name)(xact, dtraw, dt_bias, a_log, es)


def _ssd_bwd(xact, dtraw, dt_bias, a_log, d_skip, states, dy, name):
    T = xact.shape[0]
    nc = T // SSM_CHUNK
    L = SSM_CHUNK
    BO, CO = SSM_D_INNER, SSM_D_INNER + SSM_GROUPS * SSM_STATE

    def body(x_ref, dt_ref, bias_ref, al_ref, dsk_ref, es_ref, est_ref, st_ref, dy_ref,
             dxp_ref, ddt_ref, dbias_ref, dal_ref, dd_ref, dstate, qa, qx):
        cc = pl.program_id(0)

        @pl.when(cc == 0)
        def _():
            dstate[...] = jnp.zeros(dstate.shape, F32)

        xb, dt, A, tril, acs = _ssd_common(dt_ref[...], bias_ref[...], al_ref[...])
        acsT = acs.T
        last = acs[L - 1:L, :]
        cd = jnp.exp(last)
        es, est = es_ref[...], est_ref[...]
        dtX = _expand_heads(dt, es)
        EX = _expand_heads(jnp.exp(acs), es)
        decX = _expand_heads(jnp.exp(last - acs), es)
        lane1 = _iota((1, LANES), 1)
        lane = _iota((L, LANES), 1)
        sub = _iota((L, LANES), 0)
        ztot = jnp.zeros((1, LANES), F32)
        wrow = jnp.zeros((L, LANES), F32)
        wcolT = jnp.zeros((LANES, L), F32)
        rows_dec, rows_dd = [], []
        for g in range(SSM_GROUPS):
            gs = slice(g * GROUP_W, (g + 1) * GROUP_W)
            x = x_ref[:, gs]
            B = x_ref[:, BO + g * SSM_STATE:BO + (g + 1) * SSM_STATE]
            C = x_ref[:, CO + g * SSM_STATE:CO + (g + 1) * SSM_STATE]
            dY = dy_ref[:, gs]
            dtx, e_x, dec_x = dtX[:, gs], EX[:, gs], decX[:, gs]
            X = x * dtx
            CB = _nt(C, B)
            S = st_ref[g]
            dS_out = dstate[g]
            dcb_sum = jnp.zeros((L, L), F32)
            dxd = []
            for r in range(GQ):
                h = GQ * g + r
                hs = slice(r * HEAD_DIM, (r + 1) * HEAD_DIM)
                Lm = jnp.exp(jnp.where(tril > 0, acs[:, h:h + 1] - acsT[h:h + 1, :], NEG))
                M = CB * Lm
                dM = _nt(dY[:, hs], X[:, hs])
                dxd.append(_tn(M, dY[:, hs]))
                dcb_sum = dcb_sum + dM * Lm
                Wm = dM * M
                wrow = jnp.where(lane == h, jnp.sum(Wm, axis=1, keepdims=True), wrow)
                wcolT = jnp.where(sub == h, jnp.sum(Wm, axis=0, keepdims=True), wcolT)
            dXd = jnp.concatenate(dxd, axis=1)
            G = _nt(C, S)
            dG = dY * e_x
            dDX = _nt(B, dS_out)
            dX = dXd + dec_x * dDX
            t_dec = dDX * X * dec_x
            qa[:, gs] = dG * G - t_dec
            qx[:, gs] = dX * x
            rows_dec.append(jnp.sum(t_dec, axis=0, keepdims=True))
            rows_dd.append(jnp.sum(dY * x, axis=0, keepdims=True))
            zc = jnp.sum(dS_out * S, axis=1, keepdims=True)
            for r in range(GQ):
                ztot = jnp.where(lane1 == GQ * g + r, jnp.sum(zc[r * HEAD_DIM:(r + 1) * HEAD_DIM], axis=0, keepdims=True), ztot)
            dxp_ref[:, gs] = dX * dtx + dY * dsk_ref[:, gs]
            dxp_ref[:, BO + g * SSM_STATE:BO + (g + 1) * SSM_STATE] = _tn(dcb_sum, C) + _nn(X * dec_x, dS_out)
            dxp_ref[:, CO + g * SSM_STATE:CO + (g + 1) * SSM_STATE] = _nn(dcb_sum, B) + _nn(dG, S)
            dstate[g] = dS_out * _per_state_row(cd, g) + _tn(dG, C)
        rows = jnp.concatenate([jnp.concatenate(rows_dec, axis=1), jnp.concatenate(rows_dd, axis=1)]
                               + [jnp.zeros((SUBLANES - 2, SSM_D_INNER), F32)], axis=0)
        rsum = _reduce_heads(rows, est)
        dlast = rsum[0:1, :] + cd * ztot
        dacs = (wrow - wcolT.T) + _reduce_heads(qa[...], est) + jnp.where(sub == L - 1, dlast, 0.0)
        triu = (_iota((L, L), 0) <= _iota((L, L), 1)).astype(F32)
        da = _dot_hi(triu, dacs)
        ddtraw = (da * A + _reduce_heads(qx[...], est)) * (1.0 / (1.0 + jnp.exp(-xb)))
        ddt_ref[...] = ddtraw
        dal = jnp.sum(da * dt, axis=0, keepdims=True) * A
        ddp = rsum[1:2, :]
        dbp = jnp.sum(ddtraw, axis=0, keepdims=True)

        @pl.when(cc == 0)
        def _():
            dbias_ref[...] = dbp
            dal_ref[...] = dal
            dd_ref[...] = ddp

        @pl.when(cc > 0)
        def _():
            dbias_ref[...] += dbp
            dal_ref[...] += dal
            dd_ref[...] += ddp

    rc = lambda c: nc - 1 - c
    vec = pl.BlockSpec((1, LANES), lambda c: (0, 0))
    es, est = _head_selectors()
    return pl.pallas_call(
        body, grid=(nc,),
        in_specs=[pl.BlockSpec((L, SSM_CONV_DIM), lambda c: (rc(c), 0)), pl.BlockSpec((L, LANES), lambda c: (rc(c), 0)), vec, vec,
                  pl.BlockSpec((1, SSM_D_INNER), lambda c: (0, 0)),
                  pl.BlockSpec((LANES, SSM_D_INNER), lambda c: (0, 0)), pl.BlockSpec((SSM_D_INNER, LANES), lambda c: (0, 0)),
                  pl.BlockSpec((None, SSM_GROUPS, GROUP_W, SSM_STATE), lambda c: (rc(c), 0, 0, 0)),
                  pl.BlockSpec((L, SSM_D_INNER), lambda c: (rc(c), 0))],
        out_specs=[pl.BlockSpec((L, SSM_CONV_DIM), lambda c: (rc(c), 0)),
                   pl.BlockSpec((L, LANES), lambda c: (rc(c), 0)), vec, vec, vec],
        out_shape=[_sds((T, SSM_CONV_DIM)), _sds((T, LANES)), _sds((1, LANES)), _sds((1, LANES)), _sds((1, LANES))],
        scratch_shapes=[pltpu.VMEM((SSM_GROUPS, GROUP_W, SSM_STATE), F32), pltpu.VMEM((L, SSM_D_INNER), F32),
                        pltpu.VMEM((L, SSM_D_INNER), F32)],
        compiler_params=_cp("arbitrary"), name=name)(xact, dtraw, dt_bias, a_log, d_skip, es, est, states, dy)


def _ssm_post_fwd(y, xact, z, d_skip, nw, name):
    T = y.shape[0]
    tm = min(T, 256)
    W = SSM_D_INNER

    def body(y_ref, x_ref, z_ref, d_ref, w_ref, o_ref):
        y2 = (y_ref[...] + d_ref[...] * x_ref[...]) * _silu(z_ref[...])
        r = lax.rsqrt(jnp.mean(y2 * y2, axis=-1, keepdims=True) + SSM_NORM_EPS)
        o_ref[...] = (y2 * r * w_ref[...]).astype(o_ref.dtype)

    row = pl.BlockSpec((tm, W), lambda i: (i, 0))
    vec = pl.BlockSpec((1, W), lambda i: (0, 0))
    return pl.pallas_call(
        body, grid=(T // tm,), in_specs=[row, row, row, vec, vec], out_specs=row, out_shape=_sds((T, W), MXU),
        compiler_params=_cp("parallel"), name=name)(y, xact, z, d_skip, nw)


def _ssm_post_bwd(y, xact, z, d_skip, nw, dyn, name):
    T = y.shape[0]
    tm = min(T, 256)
    W = SSM_D_INNER

    def body(y_ref, x_ref, z_ref, d_ref, w_ref, dn_ref, dyg_ref, dz_ref, dw_ref):
        zv = z_ref[...]
        sz = _silu(zv)
        yg = y_ref[...] + d_ref[...] * x_ref[...]
        y2 = yg * sz
        r = lax.rsqrt(jnp.mean(y2 * y2, axis=-1, keepdims=True) + SSM_NORM_EPS)
        y2h = y2 * r
        dn = dn_ref[...]
        gy = dn * w_ref[...]
        dy2 = r * (gy - y2h * jnp.mean(gy * y2h, axis=-1, keepdims=True))
        dyg_ref[...] = dy2 * sz
        dz_ref[...] = (dy2 * yg * _dsilu(zv)).astype(dz_ref.dtype)
        part = jnp.sum(dn * y2h, axis=0, keepdims=True)

        @pl.when(pl.program_id(0) == 0)
        def _():
            dw_ref[...] = part

        @pl.when(pl.program_id(0) > 0)
        def _():
            dw_ref[...] += part

    row = pl.BlockSpec((tm, W), lambda i: (i, 0))
    vec = pl.BlockSpec((1, W), lambda i: (0, 0))
    return pl.pallas_call(
        body, grid=(T // tm,), in_specs=[row, row, row, vec, vec, row], out_specs=[row, row, vec],
        out_shape=[_sds((T, W)), _sds((T, W), MXU), _sds((1, W))],
        compiler_params=_cp("arbitrary"), name=name)(y, xact, z, d_skip, nw, dyn)


def _local_step(x0, cos, sin_s, target, P, fetch, token, send):
    mmf = functools.partial(_mm, tm=1024)
    big, small = {}, {}
    P = dict(P, wup={}, wdn={}, fcw={})
    h0 = _rmsnorm_fwd(x0, P["nm"][0], "norm_mix0", token=token)
    proj0 = mmf(h0, P["wmiT"], tb=True, tn=1280, tk=1024, name="mix_in")
    cat, attn, lse = _mixcore_fwd(proj0, cos, sin_s, P["pool_w"], P["pool_scale"], P["sinks"], "mixcore_fwd")
    x1 = mmf(cat, P["wmo"], tn=1024, tk=1024, res=x0, name="mix_out")

    def ffn_fwd(xin, i):
        hf = _rmsnorm_fwd(xin, P["nf"][i], f"norm_ffn{i}")
        got = fetch(f"ffn{i}", hf)
        P["wup"][i], P["wdn"][i], P["fcw"][i] = got["wup"], got["wdn"], got["fcw"]
        hid = mmf(hf, P["wup"][i], tn=1408, tk=1024, out_dtype=MXU, name=f"ffn_up{i}")
        act, hc = _ffn_mid_fwd(hid, P["fcw"][i], P["fcb"][i], f"ffn_mid_fwd{i}")
        xout = mmf(act, P["wdn"][i], tn=1024, tk=D_FF, res=xin, name=f"ffn_down{i}")
        return hf, (hid, hc), act, xout

    hf0, hid0, act0, x2 = ffn_fwd(x1, 0)
    h1 = _rmsnorm_fwd(x2, P["nm"][1], "norm_mix1")
    P.update(fetch("ssm", h1))
    z = mmf(h1, P["wzT"], tb=True, tn=1024, tk=1024, name="ssm_in_z")
    xbc = mmf(h1, P["wxbcT"], tb=True, tn=1024, tk=1024, name="ssm_in_xbc")
    dtraw = mmf(h1, P["wdtT"], tb=True, tn=128, tk=1024, name="ssm_in_dt")
    xact, xpre = _ssm_pre_fwd(xbc, P["scw"], P["scb"], "ssm_pre_fwd")
    y, states = _ssd_fwd(xact, dtraw, P["dt_bias"], P["a_log"], "ssd_fwd")
    yn = _ssm_post_fwd(y, xact, z, P["d_exp"], P["snorm"], "ssm_post_fwd")
    x3 = mmf(yn, P["wso"], tn=1024, tk=SSM_D_INNER, res=x2, name="ssm_out")
    hf1, hid1, act1, x4 = ffn_fwd(x3, 1)
    loss_row, dx4, d_nfin = _loss_head(x4, P["nfin"], target, "loss_head")
    small["norm_final"] = d_nfin

    def ffn_bwd(xin, dxo, hf, hid, act, i):
        da = mmf(dxo, P["wdn"][i], tb=True, tn=1408, tk=1024, name=f"ffn_down_dx{i}")
        big[f"ffn_w_down{i}"] = dwf(act, dxo, tm=1408, tn=1024, name=f"ffn_down_dw{i}").reshape(N_CHIPS, D_FF // N_CHIPS, D_MODEL)
        dhid, dcw, dcb = _ffn_mid_bwd(hid[0], hid[1], P["fcw"][i], da, f"ffn_mid_bwd{i}")
        dhf = mmf(dhid, P["wup"][i], tb=True, tn=1024, tk=2816, name=f"ffn_up_dx{i}")
        big[f"ffn_w_up{i}"] = dwf(hf, dhid, tm=1024, tn=1408, out_shard_perm=(0, 2, 1, 3), name=f"ffn_up_dw{i}")
        tok = send(f"ffn{i}", [big[f"ffn_w_up{i}"], big[f"ffn_w_down{i}"]])
        dxi, dnf = _rmsnorm_bwd(xin, P["nf"][i], dhf, dxo, f"norm_ffn_bwd{i}", token=tok)
        return dxi, dnf, dcw, dcb

    dwf = functools.partial(_mm, ta=True, tk=2048, out_dtype=BF16)
    dx3, dnf1, dfcw1, dfcb1 = ffn_bwd(x3, dx4, hf1, hid1, act1, 1)
    dyn = mmf(dx3, P["wso"], tb=True, tn=1024, tk=1024, name="ssm_out_dx")
    big["ssm_w_out"] = dwf(yn, dx3, tm=1024, tn=1024, name="ssm_out_dw").reshape(N_CHIPS, SSM_D_INNER // N_CHIPS, D_MODEL)
    dyg, dz, d_snorm = _ssm_post_bwd(y, xact, z, P["d_exp"], P["snorm"], dyn, "ssm_post_bwd")
    dxact_p, ddtraw, d_dtb, d_alog, d_dskip = _ssd_bwd(xact, dtraw, P["dt_bias"], P["a_log"], P["d_exp"], states, dyg, "ssd_bwd")
    dxbc, d_scw, d_scb = _ssm_pre_bwd(xbc, xpre, P["scw"], dxact_p, "ssm_pre_bwd")
    dh1 = mmf(dz, P["wzT"], tn=1024, tk=2048, name="ssm_in_dx_z")
    dh1 = mmf(dxbc, P["wxbcT"], tn=1024, tk=2048, res=dh1, name="ssm_in_dx_xbc")
    dh1 = mmf(ddtraw, P["wdtT"], tn=1024, tk=128, res=dh1, name="ssm_in_dx_dt")
    dwz = dwf(dz, h1, tm=1024, tn=1024, name="ssm_in_dw_z")
    dwxbc = dwf(dxbc, h1, tm=1024, tn=1024, name="ssm_in_dw_xbc")
    dwdt = dwf(ddtraw, h1, tm=128, tn=1024, name="ssm_in_dw_dt")
    dwsi = jnp.concatenate([dwz, dwxbc, dwdt[:SSM_HEADS]], axis=0)
    big["ssm_w_in"] = dwsi.reshape(N_CHIPS, SSM_IN_DIM // N_CHIPS, D_MODEL)
    tok = send("ssm", [big["ssm_w_in"], big["ssm_w_out"]])
    dx2, dnm1 = _rmsnorm_bwd(x2, P["nm"][1], dh1, dx3, "norm_mix_bwd1", token=tok)
    dx1, dnf0, dfcw0, dfcb0 = ffn_bwd(x1, dx2, hf0, hid0, act0, 0)
    dcat = mmf(dx1, P["wmo"], tb=True, tn=1024, tk=1024, name="mix_out_dx")
    big["mix_w_out"] = dwf(cat, dx1, tm=1024, tn=1024, name="mix_out_dw").reshape(N_CHIPS, D_MODEL // N_CHIPS, D_MODEL)
    dproj0, d_pw, d_ps, d_sk = _mixcore_bwd(proj0, cos, sin_s, P["pool_w"], P["pool_scale"], P["sinks"], attn, lse, dcat, "mixcore_bwd")
    dh0 = mmf(dproj0, P["wmiT"], tn=1024, tk=1280, name="mix_in_dx")
    big["mix_w_in"] = dwf(dproj0, h0, tm=1280, tn=1024, name="mix_in_dw").reshape(N_CHIPS, MIX_IN_DIM // N_CHIPS, D_MODEL)
    tok = send("mix", [big["mix_w_in"], big["mix_w_out"]])
    dx0, dnm0 = _rmsnorm_bwd(x0, P["nm"][0], dh0, dx1, "norm_mix_bwd0", token=tok)

    def unperm_cols(a):
        r = a.shape[0]
        t = a.reshape(r, N_CHIPS, FFN_TC)
        return jnp.stack([t[:, p] for p in _PERM], axis=0)

    small["norm_mix"] = jnp.concatenate([dnm0, dnm1], axis=0)
    small["norm_ffn"] = jnp.concatenate([dnf0, dnf1], axis=0)
    small["pool_w"] = d_pw.reshape(4 * POOL_GROUP, POOL_GROUP)
    small["pool_scale"] = d_ps
    small["attn_sinks"] = d_sk
    small["ssm_dt_bias"] = d_dtb
    small["ssm_A_log"] = d_alog
    small["ssm_D"] = d_dskip
    fcb = jnp.stack([unperm_cols(dfcb0), unperm_cols(dfcb1)], axis=0)
    small["ffn_conv_b"] = fcb.reshape(2, 2 * D_FF)
    small["ssm_conv_w"] = d_scw.reshape(SSM_CONV, N_CHIPS, SSM_CONV_DIM // N_CHIPS).transpose(1, 0, 2)
    small["ssm_conv_b"] = d_scb.reshape(N_CHIPS, 1, SSM_CONV_DIM // N_CHIPS)
    small["ssm_norm"] = d_snorm.reshape(N_CHIPS, 1, SSM_D_INNER // N_CHIPS)
    small["ffn_conv_w"] = jnp.concatenate([unperm_cols(dfcw0), unperm_cols(dfcw1)], axis=1)
    return loss_row, dx0, big, small


ANY = pl.BlockSpec(memory_space=pl.ANY)


def _place():
    return lax.axis_index("x"), lax.axis_index("y"), lax.axis_index("c")


def _gather_shards(shards, name):
    n = len(shards)
    split = [s.size >= (1 << 16) for s in shards]

    def half(ref, a, h):
        shp = shards[a].shape
        if len(shp) == 3:
            return ref.at[h]
        r2 = shp[0] // 2
        return ref.at[pl.ds(pl.multiple_of(h * r2, 2 * SUBLANES), r2), :]

    def body(*refs):
        ins, outs = refs[:n], refs[n:2 * n]
        send, recv, fsend, frecv = refs[2 * n:]
        x, y, c = _place()
        k = 2 * x + y
        chips = [(1 - x, y), (x, 1 - y), (1 - x, 1 - y)]

        def ici(a, j, src_slot_ref, dst_slot):
            px, py = chips[j]
            src = half(src_slot_ref, a, c) if split[a] else src_slot_ref
            dst = half(outs[a].at[dst_slot], a, c) if split[a] else outs[a].at[dst_slot]
            return pltpu.make_async_remote_copy(src, dst, send.at[a, j], recv.at[a, j], device_id=(px, py, c), device_id_type=MESH)

        def d2d(a, j, h):
            px, py = chips[j]
            part = half(outs[a].at[2 * px + py], a, h)
            return pltpu.make_async_remote_copy(part, part, fsend.at[a, j], frecv.at[a, j], device_id=(x, y, 1 - c), device_id_type=MESH)

        sends = [ici(a, j, ins[a], k) for a in range(n) for j in range(3)]
        for cp in sends:
            cp.start()
        passed = []
        for a in range(n):
            for j, (px, py) in enumerate(chips):
                ici(a, j, ins[a], 2 * px + py).wait_recv()
                if split[a]:
                    passed.append(d2d(a, j, c))
                    passed[-1].start()
        for a in range(n):
            if split[a]:
                for j in range(3):
                    d2d(a, j, 1 - c).wait_recv()
        for cp in sends + passed:
            cp.wait_send()

    return pl.pallas_call(
        body, in_specs=[ANY] * n, out_specs=[ANY] * n,
        out_shape=[_sds((N_CHIPS,) + s.shape, s.dtype) for s in shards],
        scratch_shapes=[pltpu.SemaphoreType.DMA((n, 3))] * 4,
        compiler_params=pltpu.CompilerParams(has_side_effects=True), name=name)(*shards)


HBM = pl.BlockSpec(memory_space=pltpu.HBM)
SEM = pl.BlockSpec(memory_space=pltpu.SEMAPHORE)
DATAFLOW = pltpu.SideEffectType.DATAFLOW_SIDE_EFFECTING


def _spread_start(groups, slot_src, after, name):
    flat = [a for grp in groups for a in grp]
    n = len(flat)
    ng = len(groups)
    offs = [sum(len(g) for g in groups[:i]) for i in range(ng)]
    lshape = [(a.shape if slot_src else (N_CHIPS,) + a.shape) for a in flat]

    nsem = 6 * n

    def body(*refs):
        src, land = refs[:n], refs[n:2 * n]
        sems = refs[2 * n + 1:2 * n + 1 + nsem]
        token = refs[-1]
        x, y, c = _place()
        k = 2 * x + y
        chips = [(1 - x, y), (x, 1 - y), (1 - x, 1 - y)]
        for a in range(n):
            for j, (px, py) in enumerate(chips):
                s = src[a].at[2 * px + py] if slot_src else src[a]
                pltpu.make_async_remote_copy(s, land[a].at[k], sems[6 * a + 2 * j], sems[6 * a + 2 * j + 1],
                                             device_id=(px, py, c), device_id_type=MESH).start()
        token[...] = jnp.zeros(token.shape, token.dtype)

    out_shape = [pltpu.SemaphoreType.DMA(())] * nsem
    out_shape += [pltpu.HBM(a.shape, a.dtype) for a in flat] + [pltpu.HBM(s, a.dtype) for s, a in zip(lshape, flat)]
    out_shape.append(_sds((SUBLANES, LANES)))
    args = [pltpu.with_memory_space_constraint(a, pltpu.HBM) for a in flat]
    args += [pltpu.with_memory_space_constraint(lax.empty(s, a.dtype), pltpu.HBM) for s, a in zip(lshape, flat)]
    res = pl.pallas_call(
        body, name=name, out_shape=tuple(out_shape), in_specs=[HBM] * (2 * n) + [pl.BlockSpec(memory_space=pl.ANY)],
        out_specs=tuple([SEM] * nsem + [HBM] * (2 * n) + [pl.BlockSpec(memory_space=pltpu.VMEM)]),
        input_output_aliases={i: nsem + i for i in range(2 * n)},
        compiler_params=pltpu.CompilerParams(has_side_effects=DATAFLOW))(*args, after)
    sems, thru, token = res[:nsem], res[nsem:nsem + 2 * n], res[-1]
    out = []
    for gi, grp in enumerate(groups):
        sl = slice(offs[gi], offs[gi] + len(grp))
        out.append((list(sems[6 * offs[gi]:6 * (offs[gi] + len(grp))]), list(thru[:n][sl]), list(thru[n:][sl])))
    return out, token


def _spread_wait(started, slot_src, after, name):
    sems, srcs, lands = started
    n = len(srcs)

    def body(*refs):
        src, land = refs[:n], refs[n:2 * n]
        sem = refs[2 * n:2 * n + 6 * n]
        x, y, c = _place()
        chips = [(1 - x, y), (x, 1 - y), (1 - x, 1 - y)]
        for a in range(n):
            for j, (px, py) in enumerate(chips):
                s = src[a].at[2 * px + py] if slot_src else src[a]
                cp = pltpu.make_async_remote_copy(s, land[a].at[2 * px + py], sem[6 * a + 2 * j], sem[6 * a + 2 * j + 1],
                                                  device_id=(px, py, c), device_id_type=MESH)
                cp.wait_send()
                cp.wait_recv()

    res = pl.pallas_call(
        body, name=name, out_shape=tuple([pltpu.HBM(a.shape, a.dtype) for a in srcs] + [pltpu.HBM(a.shape, a.dtype) for a in lands]),
        in_specs=[HBM] * (2 * n) + [SEM] * (6 * n) + [pl.BlockSpec(memory_space=pl.ANY)], out_specs=tuple([HBM] * (2 * n)),
        input_output_aliases={i: i for i in range(2 * n)},
        compiler_params=pltpu.CompilerParams(has_side_effects=DATAFLOW))(*srcs, *lands, *sems, after)
    return list(res[:n]), list(res[n:])


def _sibling_exchange(fs, name):
    n = len(fs)

    def body(*refs):
        ins, outs = refs[:n], refs[n:2 * n]
        send, recv = refs[2 * n:]
        x, y, c = _place()
        cps = [pltpu.make_async_remote_copy(ins[a], outs[a], send.at[a], recv.at[a],
                                            device_id=(x, y, 1 - c), device_id_type=MESH) for a in range(n)]
        for cp in cps:
            cp.start()
        for cp in cps:
            cp.wait()

    return pl.pallas_call(
        body, in_specs=[ANY] * n, out_specs=[ANY] * n, out_shape=[_sds(f.shape, f.dtype) for f in fs],
        scratch_shapes=[pltpu.SemaphoreType.DMA((n,)), pltpu.SemaphoreType.DMA((n,))],
        compiler_params=pltpu.CompilerParams(has_side_effects=True), name=name)(*fs)


def _tile2d(rows, cols, budget=1024 * 1024, step=2 * SUBLANES):
    fits = [t for t in range(step, rows + 1, step) if rows % t == 0 and t * cols * 4 <= budget]
    if fits:
        return fits[-1], cols
    fits = [t for t in range(LANES, cols + 1, LANES) if cols % t == 0 and rows * t * 4 <= budget]
    assert fits, (rows, cols)
    return rows, fits[-1]


def _chip_sum(own, parts, kidx, name):
    _, R, C = parts.shape
    tr, tc = _tile2d(R, C)

    def body(k_ref, o_ref_in, p1_ref, p2_ref, p3_ref, o_ref):
        o_ref[...] = ((o_ref_in[...].astype(F32) + p1_ref[...].astype(F32)) + p2_ref[...].astype(F32)) + p3_ref[...].astype(F32)

    def slot(d):
        return pl.BlockSpec((None, tr, tc), lambda i, j, k: ((k[0] + d) % N_CHIPS, i, j))

    return pl.pallas_call(
        body,
        grid_spec=pltpu.PrefetchScalarGridSpec(
            num_scalar_prefetch=1, grid=(R // tr, C // tc), in_specs=[slot(0), slot(1), slot(2), slot(3)],
            out_specs=pl.BlockSpec((tr, tc), lambda i, j, k: (i, j))),
        out_shape=_sds((R, C)), compiler_params=_cp("parallel", "parallel"), name=name)(kidx, own, parts, parts, parts)


def _adamw_math(w, g, m, v):
    m2 = ADAM_B1 * m + (1.0 - ADAM_B1) * g
    v2 = ADAM_B2 * v + (1.0 - ADAM_B2) * (g * g)
    m_hat = m2 / (1.0 - ADAM_B1 ** ADAM_STEP)
    v_hat = v2 / (1.0 - ADAM_B2 ** ADAM_STEP)
    delta = -ADAM_LR * (m_hat / (jnp.sqrt(v_hat) + ADAM_EPS) + ADAM_WD * w)
    return delta, m2, v2


def _adamw(w, m, v, gparts, name):
    Lw, R, C = w.shape
    tr, tc = _tile2d(R, C)
    flat = [h for pair in gparts for h in pair]

    def body(*refs):
        w_ref, m_ref, v_ref = refs[:3]
        g_refs = refs[3:3 + 2 * Lw]
        go_ref, d_ref, mo_ref, vo_ref = refs[3 + 2 * Lw:]
        g = g_refs[0][...] + g_refs[1][...]
        for l in range(1, Lw):
            g = jnp.where(pl.program_id(0) == l, g_refs[2 * l][...] + g_refs[2 * l + 1][...], g)
        d, m2, v2 = _adamw_math(w_ref[...], g, m_ref[...], v_ref[...])
        go_ref[...] = g
        d_ref[...] = d
        mo_ref[...] = m2
        vo_ref[...] = v2

    blk = pl.BlockSpec((None, tr, tc), lambda l, i, j: (l, i, j))
    gblk = pl.BlockSpec((tr, tc), lambda l, i, j: (i, j))
    return pl.pallas_call(
        body, grid=(Lw, R // tr, C // tc), in_specs=[blk, blk, blk] + [gblk] * (2 * Lw), out_specs=[blk] * 4,
        out_shape=[_sds((Lw, R, C))] * 4, compiler_params=_cp("parallel", "parallel", "parallel"), name=name)(w, m, v, *flat)


def _small_adamw(grads, wmv, name):
    n = len(grads)

    def body(*refs):
        g_in, p_in, outs = refs[:n], refs[n:4 * n], refs[4 * n:]
        for a in range(n):
            g = g_in[a][...]
            d_, m2, v2 = _adamw_math(p_in[3 * a][...], g, p_in[3 * a + 1][...], p_in[3 * a + 2][...])
            outs[4 * a][...] = g
            outs[4 * a + 1][...] = d_
            outs[4 * a + 2][...] = m2
            outs[4 * a + 3][...] = v2

    vm = pl.BlockSpec(memory_space=pltpu.VMEM)
    args = list(grads) + [t for tri in wmv for t in tri]
    out_shape = [_sds(g.shape) for g in grads for _ in range(4)]
    return pl.pallas_call(body, in_specs=[vm] * len(args), out_specs=[vm] * len(out_shape), out_shape=out_shape,
                          compiler_params=pltpu.CompilerParams(vmem_limit_bytes=V7X_VMEM_LIMIT), name=name)(*args)


def _small_allreduce(partials, pshapes, loss_row, name):
    n = len(partials)
    gshapes = [p.shape for p in partials] + [loss_row.shape]
    ng = n + 1

    def body(*refs):
        g_in = refs[:ng]
        outs = refs[ng:2 * ng]
        bufs = refs[2 * ng:3 * ng]
        send, recv = refs[-2:]
        x, y, c = _place()
        me = 4 * x + 2 * y + c
        k = 2 * x + y
        flips = [(fx, fy, fc) for fx in (0, 1) for fy in (0, 1) for fc in (0, 1)][1:]

        def peer(f):
            return (x ^ f[0], y ^ f[1], c ^ f[2])

        def slot(p):
            return 4 * p[0] + 2 * p[1] + p[2]

        for a in range(ng):
            bufs[a][me] = g_in[a][...]
        sends = [pltpu.make_async_remote_copy(g_in[a], bufs[a].at[me], send.at[a, j], recv.at[a, j],
                                              device_id=peer(f), device_id_type=MESH)
                 for a in range(ng) for j, f in enumerate(flips)]
        for cp in sends:
            cp.start()
        for a in range(ng):
            for j, f in enumerate(flips):
                pltpu.make_async_remote_copy(g_in[a], bufs[a].at[slot(peer(f))], send.at[a, j], recv.at[a, j],
                                             device_id=peer(f), device_id_type=MESH).wait_recv()
        for cp in sends:
            cp.wait_send()
        for a in range(ng):
            sharded = len(gshapes[a]) == 3

            def part(d):
                return bufs[a][d, k] if sharded else bufs[a][d]

            tot = part(0)
            for d in range(1, N_DEV):
                tot = tot + part(d)
            if a == n:
                outs[n][...] = tot
            else:
                pr, pc = pshapes[a]
                outs[a][...] = tot[:pr, :pc]

    vm = pl.BlockSpec(memory_space=pltpu.VMEM)
    args = list(partials) + [loss_row]
    out_shape = [_sds(ps) for ps in pshapes] + [_sds(loss_row.shape)]
    return pl.pallas_call(
        body, in_specs=[vm] * len(args), out_specs=[vm] * len(out_shape), out_shape=out_shape,
        scratch_shapes=[pltpu.VMEM((N_DEV,) + tuple(s), F32) for s in gshapes]
        + [pltpu.SemaphoreType.DMA((ng, N_DEV - 1)), pltpu.SemaphoreType.DMA((ng, N_DEV - 1))],
        compiler_params=pltpu.CompilerParams(has_side_effects=True, vmem_limit_bytes=V7X_VMEM_LIMIT), name=name)(*args)


_PERM = (0, 2, 1, 3)


def _cols_from_shards(g):
    return g.transpose(1, 0, 2).reshape(g.shape[1], N_CHIPS * g.shape[2])


def _rope_tables(positions):
    inv_freq = ROPE_THETA ** (-jnp.arange(0, HEAD_DIM, 2, dtype=F32) / HEAD_DIM)
    ang = positions.astype(F32).reshape(-1, 1) * inv_freq
    cos, sin = jnp.cos(ang), jnp.sin(ang)
    cos = jnp.concatenate([cos, cos, cos, cos], axis=-1)
    sin_s = jnp.concatenate([-sin, sin, -sin, sin], axis=-1)
    return cos, sin_s


def kernel(x, positions, norm_mix, norm_ffn, norm_final, mix_w_in, pool_w, pool_scale, attn_sinks, mix_w_out, ssm_w_in, ssm_conv_w, ssm_conv_b, ssm_dt_bias, ssm_A_log, ssm_D, ssm_norm, ssm_w_out, ffn_w_up, ffn_conv_w, ffn_conv_b, ffn_w_down, loss_target, m_norm_mix, m_norm_ffn, m_norm_final, m_mix_w_in, m_pool_w, m_pool_scale, m_attn_sinks, m_mix_w_out, m_ssm_w_in, m_ssm_conv_w, m_ssm_conv_b, m_ssm_dt_bias, m_ssm_A_log, m_ssm_D, m_ssm_norm, m_ssm_w_out, m_ffn_w_up, m_ffn_conv_w, m_ffn_conv_b, m_ffn_w_down, v_norm_mix, v_norm_ffn, v_norm_final, v_mix_w_in, v_pool_w, v_pool_scale, v_attn_sinks, v_mix_w_out, v_ssm_w_in, v_ssm_conv_w, v_ssm_conv_b, v_ssm_dt_bias, v_ssm_A_log, v_ssm_D, v_ssm_norm, v_ssm_w_out, v_ffn_w_up, v_ffn_conv_w, v_ffn_conv_b, v_ffn_w_down):
    W = dict(norm_mix=norm_mix, norm_ffn=norm_ffn, norm_final=norm_final, mix_w_in=mix_w_in, pool_w=pool_w, pool_scale=pool_scale, attn_sinks=attn_sinks, mix_w_out=mix_w_out, ssm_w_in=ssm_w_in, ssm_conv_w=ssm_conv_w, ssm_conv_b=ssm_conv_b, ssm_dt_bias=ssm_dt_bias, ssm_A_log=ssm_A_log, ssm_D=ssm_D, ssm_norm=ssm_norm, ssm_w_out=ssm_w_out, ffn_w_up=ffn_w_up, ffn_conv_w=ffn_conv_w, ffn_conv_b=ffn_conv_b, ffn_w_down=ffn_w_down)
    Mo = dict(norm_mix=m_norm_mix, norm_ffn=m_norm_ffn, norm_final=m_norm_final, mix_w_in=m_mix_w_in, pool_w=m_pool_w, pool_scale=m_pool_scale, attn_sinks=m_attn_sinks, mix_w_out=m_mix_w_out, ssm_w_in=m_ssm_w_in, ssm_conv_w=m_ssm_conv_w, ssm_conv_b=m_ssm_conv_b, ssm_dt_bias=m_ssm_dt_bias, ssm_A_log=m_ssm_A_log, ssm_D=m_ssm_D, ssm_norm=m_ssm_norm, ssm_w_out=m_ssm_w_out, ffn_w_up=m_ffn_w_up, ffn_conv_w=m_ffn_conv_w, ffn_conv_b=m_ffn_conv_b, ffn_w_down=m_ffn_w_down)
    Vo = dict(norm_mix=v_norm_mix, norm_ffn=v_norm_ffn, norm_final=v_norm_final, mix_w_in=v_mix_w_in, pool_w=v_pool_w, pool_scale=v_pool_scale, attn_sinks=v_attn_sinks, mix_w_out=v_mix_w_out, ssm_w_in=v_ssm_w_in, ssm_conv_w=v_ssm_conv_w, ssm_conv_b=v_ssm_conv_b, ssm_dt_bias=v_ssm_dt_bias, ssm_A_log=v_ssm_A_log, ssm_D=v_ssm_D, ssm_norm=v_ssm_norm, ssm_w_out=v_ssm_w_out, ffn_w_up=v_ffn_w_up, ffn_conv_w=v_ffn_conv_w, ffn_conv_b=v_ffn_conv_b, ffn_w_down=v_ffn_w_down)

    kchip = 2 * lax.axis_index("x") + lax.axis_index("y")

    def own_slot(g, own):
        return lax.dynamic_update_slice_in_dim(g, own[None], kchip, axis=0)

    def tr(t):
        return jnp.swapaxes(t[0], 0, 1)

    later = dict(ffn0=[ffn_w_up[0].astype(MXU), ffn_w_down[0].astype(MXU)],
                 ssm=[tr(ssm_w_in).astype(MXU), ssm_w_out[0].astype(MXU)],
                 ffn1=[ffn_w_up[1].astype(MXU), ffn_w_down[1].astype(MXU)])
    sh = [tr(mix_w_in).astype(MXU), mix_w_out[0].astype(MXU), ssm_conv_w[0], ssm_conv_b, ssm_norm, ffn_conv_w]
    first = _gather_shards(sh, "gather_first")
    g_mi, g_mo, g_scw, g_scb, g_sn, g_fcw = [own_slot(g, own) for g, own in zip(first, sh)]
    started, token = _spread_start(list(later.values()), False, first[0], "gather_start")
    started = dict(zip(later.keys(), started))
    fcw = [jnp.concatenate([g_fcw[p, i] for p in _PERM], axis=1) for i in range(2)]
    P = dict(
        nm=norm_mix, nf=norm_ffn, nfin=norm_final,
        wmiT=g_mi.reshape(MIX_IN_DIM, D_MODEL), wmo=g_mo.reshape(D_MODEL, D_MODEL),
        pool_w=pool_w[0], pool_scale=pool_scale, sinks=attn_sinks[0],
        scw=_cols_from_shards(g_scw), scb=g_scb.reshape(1, SSM_CONV_DIM), snorm=g_sn.reshape(1, SSM_D_INNER),
        dt_bias=jnp.pad(ssm_dt_bias, ((0, 0), (0, LANES - SSM_HEADS))), a_log=jnp.pad(ssm_A_log, ((0, 0), (0, LANES - SSM_HEADS))),
        d_exp=jnp.repeat(ssm_D, SSM_D_INNER // SSM_HEADS, axis=1),
        fcb=[jnp.concatenate([ffn_conv_b[i:i + 1, p * FFN_TC:(p + 1) * FFN_TC] for p in _PERM], axis=1) for i in range(2)],
    )

    def fetch(group, after):
        owns, lands = _spread_wait(started[group], False, after, f"gather_wait_{group}")
        a, b = [own_slot(g, own) for g, own in zip(lands, owns)]
        if group == "ssm":
            wsi = a.reshape(SSM_IN_DIM, D_MODEL)
            zx = SSM_D_INNER + SSM_CONV_DIM
            return dict(wzT=wsi[:SSM_D_INNER], wxbcT=wsi[SSM_D_INNER:zx],
                        wdtT=jnp.pad(wsi[zx:], ((0, LANES - SSM_HEADS), (0, 0))), wso=b.reshape(SSM_D_INNER, D_MODEL))
        i = int(group[-1])
        return dict(wup=jnp.concatenate([a[p] for p in _PERM], axis=1), wdn=b.reshape(D_FF, D_MODEL), fcw=fcw[i])

    cos, sin_s = _rope_tables(positions)
    sent = {}

    def send(group, grads):
        res, tok = _spread_start([grads], True, jnp.zeros((SUBLANES, LANES), F32), f"grad_start_{group}")
        sent[group] = res[0]
        return tok

    loss_row, grad_x, big, small = _local_step(x[0], cos, sin_s, loss_target[0], P, fetch, token, send)

    kidx = kchip.astype(jnp.int32).reshape(1)
    group_names = dict(ffn1=["ffn_w_up1", "ffn_w_down1"], ssm=["ssm_w_in", "ssm_w_out"], ffn0=["ffn_w_up0", "ffn_w_down0"],
                       mix=["mix_w_in", "mix_w_out"])
    names, mine = [], []
    for group, started_g in sent.items():
        grads, lands = _spread_wait(started_g, True, grad_x, f"grad_wait_{group}")
        for nm, g, land in zip(group_names[group], grads, lands):
            names.append(nm)
            mine.append(_chip_sum(g, land, kidx, f"chip_sum_{nm}"))
    theirs = _sibling_exchange(mine, "sibling_exchange")
    red = {nm: (a, b) for nm, a, b in zip(names, mine, theirs)}

    out = {}

    def big_update(pname, gparts, transposed=False):
        w = W[pname]
        lw = len(gparts)
        shp = w.shape
        rr, cc = gparts[0][0].shape
        fix = (lambda t: tr(t)[None]) if transposed else (lambda t: t.reshape(lw, rr, cc))
        res = _adamw(fix(w), fix(Mo[pname]), fix(Vo[pname]), gparts, f"adamw_{pname}")
        out[pname] = tuple((tr(r)[None] if transposed else r.reshape(shp)) for r in res)

    big_update("mix_w_in", [red["mix_w_in"]], transposed=True)
    big_update("mix_w_out", [red["mix_w_out"]])
    big_update("ssm_w_in", [red["ssm_w_in"]], transposed=True)
    big_update("ssm_w_out", [red["ssm_w_out"]])
    big_update("ffn_w_up", [red["ffn_w_up0"], red["ffn_w_up1"]])
    big_update("ffn_w_down", [red["ffn_w_down0"], red["ffn_w_down1"]])

    small_names = ["norm_mix", "norm_ffn", "norm_final", "pool_w", "pool_scale", "attn_sinks", "ssm_dt_bias", "ssm_A_log",
                   "ssm_D", "ffn_conv_b", "ssm_conv_w", "ssm_conv_b", "ssm_norm", "ffn_conv_w"]

    def as2d(t):
        if t.ndim == 1:
            return t.reshape(1, -1)
        return t.reshape(-1, t.shape[-1])

    wmv = [(as2d(W[nm]), as2d(Mo[nm]), as2d(Vo[nm])) for nm in small_names]
    summed = _small_allreduce([small[nm] for nm in small_names], [t[0].shape for t in wmv], loss_row, "small_allreduce")
    res = _small_adamw(summed[:-1], wmv, "small_adamw")
    for a, nm in enumerate(small_names):
        out[nm] = tuple(r.reshape(W[nm].shape) for r in res[4 * a:4 * a + 4])
    loss = summed[-1][0, 0]

    order = ["norm_mix", "norm_ffn", "norm_final", "mix_w_in", "pool_w", "pool_scale", "attn_sinks", "mix_w_out", "ssm_w_in",
             "ssm_conv_w", "ssm_conv_b", "ssm_dt_bias", "ssm_A_log", "ssm_D", "ssm_norm", "ssm_w_out", "ffn_w_up", "ffn_conv_w",
             "ffn_conv_b", "ffn_w_down"]
    return (loss, grad_x.reshape(x.shape), *[out[nm][0] for nm in order], *[out[nm][1] for nm in order],
            *[out[nm][2] for nm in order], *[out[nm][3] for nm in order])
```

```python
import functools

import jax
import jax.numpy as jnp
from jax import lax
from jax.experimental import pallas as pl
from jax.experimental.pallas import tpu as pltpu

F32 = jnp.float32
BF16 = jnp.bfloat16
MXU = BF16
HI = lax.Precision.HIGHEST

D_MODEL = 1024
POOL_WINDOWS = (2, 4, 8, 16)
POOL_DIM = 512
POOL_GROUP = 128
HEAD_DIM = 64
N_HEADS = 8
N_KV_HEADS = 2
GQ = 4
Q_DIM = 512
KV_DIM = 128
BLOCK = 128
ROPE_THETA = 10000.0
MIX_IN_DIM = 1280
SSM_D_INNER = 2048
SSM_HEADS = 32
SSM_GROUPS = 8
SSM_STATE = 128
SSM_CONV = 4
SSM_CHUNK = 128
SSM_CONV_DIM = 4096
SSM_IN_DIM = 6176
D_FF = 2816
FFN_CONV = 3
NORM_EPS = 1e-6
SSM_NORM_EPS = 1e-5
ADAM_LR = 0.001
ADAM_B1 = 0.9
ADAM_B2 = 0.999
ADAM_EPS = 1e-08
ADAM_WD = 0.01
ADAM_STEP = 10

N_CHIPS = 4
N_DEV = 8
LANES = 128
SUBLANES = 8
V7X_VMEM_LIMIT = 56 * 1024 * 1024
NEG = -1e30
MESH = pl.DeviceIdType.MESH


def _cp(*sem):
    return pltpu.CompilerParams(dimension_semantics=sem if sem else None, vmem_limit_bytes=V7X_VMEM_LIMIT)


def _sds(shape, dtype=F32):
    return jax.ShapeDtypeStruct(tuple(shape), dtype)


def _iota(shape, dim):
    return lax.broadcasted_iota(jnp.int32, shape, dim)


def _silu(x):
    return x * (1.0 / (1.0 + jnp.exp(-x)))


def _dsilu(x):
    s = 1.0 / (1.0 + jnp.exp(-x))
    return s * (1.0 + x * (1.0 - s))


def _mm(a, b, *, ta=False, tb=False, tm, tn, tk, res=None, out_dtype=F32, out_shard_perm=None, norm_w=None, norm_bwd=None, name):
    M, K = (a.shape[1], a.shape[0]) if ta else a.shape
    N = b.shape[0] if tb else b.shape[1]
    tm, tn, tk = min(tm, M), min(tn, N), min(tk, K)
    gm, gn, gk = M // tm, N // tn, K // tk
    assert gm * tm == M and gn * tn == N and gk * tk == K, (name, M, N, K, tm, tn, tk)
    a_spec = pl.BlockSpec((tk, tm), lambda i, j, k: (k, i)) if ta else pl.BlockSpec((tm, tk), lambda i, j, k: (i, k))
    b_spec = pl.BlockSpec((tn, tk), lambda i, j, k: (j, k)) if tb else pl.BlockSpec((tk, tn), lambda i, j, k: (k, j))
    dims = (((0 if ta else 1,), (1 if tb else 0,)), ((), ()))
    has_res = res is not None
    has_nw = norm_w is not None
    has_nb = norm_bwd is not None
    has_tok = has_nb and norm_bwd[3] is not None
    assert not (has_nw or has_nb) or (gn == 1 and out_shard_perm is None)
    n_in = 2 + has_res + has_nw + (3 + has_tok if has_nb else 0)

    def body(*refs):
        a_ref, b_ref = refs[0], refs[1]
        extra = list(refs[2:n_in])
        outs = refs[n_in:]
        r_ref = extra.pop(0) if has_res else None
        nw_ref = extra.pop(0) if has_nw else None
        nb_refs = extra if has_nb else None

        def dot():
            return lax.dot_general(a_ref[...].astype(MXU), b_ref[...].astype(MXU), dims, preferred_element_type=F32)

        def finish(r):
            if has_res:
                r = r + r_ref[...]
            if has_nb:
                xv = nb_refs[0][...]
                rs = lax.rsqrt(jnp.mean(xv * xv, axis=-1, keepdims=True) + NORM_EPS)
                xh = xv * rs
                g = r * nb_refs[1][...]
                dr = nb_refs[2][...] + nb_refs[3][0:1, 0:1] if has_tok else nb_refs[2][...]
                outs[0][...] = dr + rs * (g - xh * jnp.mean(g * xh, axis=-1, keepdims=True))
                part = jnp.sum(r * xh, axis=0, keepdims=True)
                i = pl.program_id(0)

                @pl.when(i == 0)
                def _():
                    outs[1][...] = part

                @pl.when(i > 0)
                def _():
                    outs[1][...] += part
                return
            outs[0][...] = r.astype(out_dtype)
            if has_nw:
                rs = lax.rsqrt(jnp.mean(r * r, axis=-1, keepdims=True) + NORM_EPS)
                outs[1][...] = (r * rs * nw_ref[...]).astype(outs[1].dtype)

        if gk == 1:
            finish(dot())
        else:
            acc = refs[-1]
            k = pl.program_id(2)

            @pl.when(k == 0)
            def _():
                acc[...] = dot()

            if gk > 2:
                @pl.when(jnp.logical_and(k > 0, k < gk - 1))
                def _():
                    acc[...] += dot()

            @pl.when(k == gk - 1)
            def _():
                finish(acc[...] + dot())

    tile = pl.BlockSpec((tm, tn), lambda i, j, k: (i, j))
    row = pl.BlockSpec((1, tn), lambda i, j, k: (0, j))
    in_specs = [a_spec, b_spec]
    args = [a, b]
    if has_res:
        in_specs.append(tile)
        args.append(res)
    if has_nw:
        in_specs.append(row)
        args.append(norm_w.reshape(1, N))
    if has_nb:
        in_specs += [tile, row, tile]
        args += [norm_bwd[0], norm_bwd[1].reshape(1, N), norm_bwd[2]]
        if has_tok:
            in_specs.append(pl.BlockSpec((SUBLANES, LANES), lambda i, j, k: (0, 0)))
            args.append(norm_bwd[3])
    if out_shard_perm is None:
        out_spec = tile
        out_shape = _sds((M, N), out_dtype)
    else:
        assert gn == len(out_shard_perm) == 4 and tuple(out_shard_perm) == (0, 2, 1, 3)
        out_spec = pl.BlockSpec((None, tm, tn), lambda i, j, k: ((j % 2) * 2 + j // 2, i, 0))
        out_shape = _sds((gn, M, tn), out_dtype)
    sem = ("parallel", "parallel", "arbitrary")
    if has_nw:
        out_spec, out_shape = [out_spec, tile], [out_shape, _sds((M, N), MXU)]
    if has_nb:
        out_spec, out_shape = [tile, row], [_sds((M, N)), _sds((1, N))]
        sem = ("arbitrary", "arbitrary", "arbitrary")
    return pl.pallas_call(
        body, grid=(gm, gn, gk), in_specs=in_specs, out_specs=out_spec, out_shape=out_shape,
        scratch_shapes=[pltpu.VMEM((tm, tn), F32)] if gk > 1 else [],
        compiler_params=_cp(*sem), name=name)(*args)


def _rmsnorm_fwd(x, w, name, token=None):
    T, D = x.shape
    tm = min(T, 512)
    has_token = token is not None

    def body(*refs):
        x_ref, w_ref, o_ref = refs[0], refs[1], refs[-1]
        xv = x_ref[...]
        if has_token:
            xv = xv + refs[2][0:1, 0:1]
        r = lax.rsqrt(jnp.mean(xv * xv, axis=-1, keepdims=True) + NORM_EPS)
        o_ref[...] = (xv * r * w_ref[...]).astype(o_ref.dtype)

    in_specs = [pl.BlockSpec((tm, D), lambda i: (i, 0)), pl.BlockSpec((1, D), lambda i: (0, 0))]
    args = [x, w.reshape(1, D)]
    if has_token:
        in_specs.append(pl.BlockSpec((SUBLANES, LANES), lambda i: (0, 0)))
        args.append(token)
    return pl.pallas_call(
        body, grid=(T // tm,), in_specs=in_specs,
        out_specs=pl.BlockSpec((tm, D), lambda i: (i, 0)), out_shape=_sds((T, D), MXU),
        compiler_params=_cp("parallel"), name=name)(*args)


def _loss_head(x, w, target, name):
    T, D = x.shape
    tm = min(T, 512)

    def body(x_ref, w_ref, t_ref, loss_ref, dx_ref, dw_ref):
        xv = x_ref[...]
        r = lax.rsqrt(jnp.mean(xv * xv, axis=-1, keepdims=True) + NORM_EPS)
        xh = xv * r
        wv = w_ref[...]
        e = xh * wv - t_ref[...]
        lpart = 0.5 * jnp.sum(jnp.mean(e * e, axis=-1, keepdims=True), axis=0, keepdims=True)
        dy = e * (1.0 / D)
        g = dy * wv
        dx_ref[...] = r * (g - xh * jnp.mean(g * xh, axis=-1, keepdims=True))
        part = jnp.sum(dy * xh, axis=0, keepdims=True)
        lrow = jnp.broadcast_to(lpart, (1, LANES))

        @pl.when(pl.program_id(0) == 0)
        def _():
            dw_ref[...] = part
            loss_ref[...] = lrow

        @pl.when(pl.program_id(0) > 0)
        def _():
            dw_ref[...] += part
            loss_ref[...] += lrow

    row = pl.BlockSpec((tm, D), lambda i: (i, 0))
    vec = pl.BlockSpec((1, D), lambda i: (0, 0))
    return pl.pallas_call(
        body, grid=(T // tm,), in_specs=[row, vec, row],
        out_specs=[pl.BlockSpec((1, LANES), lambda i: (0, 0)), row, vec],
        out_shape=[_sds((1, LANES)), _sds((T, D)), _sds((1, D))],
        compiler_params=_cp("arbitrary"), name=name)(x, w.reshape(1, D), target)


def _shift_down(cur, prev8, s):
    if s == 0:
        return cur
    tm = cur.shape[0]
    rc = pltpu.roll(cur, s, 0)
    top = jnp.where(_iota((SUBLANES, cur.shape[1]), 0) < s, pltpu.roll(prev8, s, 0), rc[:SUBLANES])
    return jnp.concatenate([top, rc[SUBLANES:]], axis=0) if tm > SUBLANES else top


def _shift_up(cur, next8, s):
    if s == 0:
        return cur
    tm = cur.shape[0]
    rc = pltpu.roll(cur, tm - s, 0)
    bot = jnp.where(_iota((SUBLANES, cur.shape[1]), 0) >= SUBLANES - s, pltpu.roll(next8, SUBLANES - s, 0), rc[tm - SUBLANES:])
    return jnp.concatenate([rc[:tm - SUBLANES], bot], axis=0) if tm > SUBLANES else bot


def _conv_rows(cur, prev8, w, b, K):
    acc = cur * w[K - 1:K, :] + b
    for s in range(1, K):
        acc = acc + _shift_down(cur, prev8, s) * w[K - 1 - s:K - s, :]
    return acc


FFN_TC = 1408
HALO16 = 2 * SUBLANES


def _ffn_mid_fwd(hid, cw, cb, name):
    T = hid.shape[0]
    tm = min(T, 256)
    nt, nj = T // tm, D_FF // FFN_TC
    K = FFN_CONV

    q = tm // HALO16

    def body(h_ref, hp_ref, w_ref, b_ref, o_ref, hc_ref):
        i = pl.program_id(0)
        cur = h_ref[...].astype(F32)
        prev8 = jnp.where(i > 0, hp_ref[...].astype(F32)[HALO16 - SUBLANES:], 0.0)
        hc = _conv_rows(cur, prev8, w_ref[...], b_ref[...], K)
        hc_ref[...] = hc
        o_ref[...] = (_silu(hc[:, FFN_TC:]) * hc[:, :FFN_TC]).astype(o_ref.dtype)

    return pl.pallas_call(
        body, grid=(nt, nj),
        in_specs=[pl.BlockSpec((tm, 2 * FFN_TC), lambda i, j: (i, j)),
                  pl.BlockSpec((HALO16, 2 * FFN_TC), lambda i, j: (jnp.maximum(i * q - 1, 0), j)),
                  pl.BlockSpec((K, 2 * FFN_TC), lambda i, j: (0, j)), pl.BlockSpec((1, 2 * FFN_TC), lambda i, j: (0, j))],
        out_specs=[pl.BlockSpec((tm, FFN_TC), lambda i, j: (i, j)), pl.BlockSpec((tm, 2 * FFN_TC), lambda i, j: (i, j))],
        out_shape=[_sds((T, D_FF), MXU), _sds((T, 2 * D_FF))],
        compiler_params=_cp("parallel", "parallel"), name=name)(hid, hid, cw, cb)


def _ffn_mid_bwd(hid, hc, cw, da, name):
    T = hid.shape[0]
    tm = min(T, 256)
    nt, nj = T // tm, D_FF // FFN_TC
    K = FFN_CONV
    W2 = 2 * FFN_TC

    def body(h_ref, c_ref, cn_ref, da_ref, dan_ref, w_ref, dh_ref, dw_ref, db_ref):
        i = pl.program_id(1)
        w = w_ref[...]
        cur = h_ref[...].astype(F32)
        last = i == nt - 1

        def dpre(hcv, dav):
            u, g = hcv[:, :FFN_TC], hcv[:, FFN_TC:]
            return jnp.concatenate([dav * _silu(g), dav * u * _dsilu(g)], axis=1)

        d_cur = dpre(c_ref[...], da_ref[...])
        d_nxt = jnp.where(last, 0.0, dpre(cn_ref[...], dan_ref[...]))
        ups = [d_cur] + [_shift_up(d_cur, d_nxt, s) for s in range(1, K)]
        dh = ups[0] * w[K - 1:K, :]
        for s in range(1, K):
            dh = dh + ups[s] * w[K - 1 - s:K - s, :]
        dh_ref[...] = dh.astype(dh_ref.dtype)
        dwp = jnp.concatenate([jnp.sum(ups[K - 1 - k] * cur, axis=0, keepdims=True) for k in range(K)], axis=0)
        dbp = jnp.sum(d_cur, axis=0, keepdims=True)

        @pl.when(i == 0)
        def _():
            dw_ref[...] = dwp
            db_ref[...] = dbp

        @pl.when(i > 0)
        def _():
            dw_ref[...] += dwp
            db_ref[...] += dbp

    q = tm // SUBLANES
    blk = pl.BlockSpec((tm, W2), lambda j, i: (i, j))
    nxt = pl.BlockSpec((SUBLANES, W2), lambda j, i: (jnp.minimum((i + 1) * q, nt * q - 1), j))
    dab = pl.BlockSpec((tm, FFN_TC), lambda j, i: (i, j))
    dan = pl.BlockSpec((SUBLANES, FFN_TC), lambda j, i: (jnp.minimum((i + 1) * q, nt * q - 1), j))
    return pl.pallas_call(
        body, grid=(nj, nt),
        in_specs=[blk, blk, nxt, dab, dan, pl.BlockSpec((K, W2), lambda j, i: (0, j))],
        out_specs=[blk, pl.BlockSpec((K, W2), lambda j, i: (0, j)), pl.BlockSpec((1, W2), lambda j, i: (0, j))],
        out_shape=[_sds((T, 2 * D_FF), MXU), _sds((K, 2 * D_FF)), _sds((1, 2 * D_FF))],
        compiler_params=_cp("parallel", "arbitrary"), name=name)(hid, hc, hc, da, da, cw)


def _rope(t, cos, sin_s, inverse=False):
    n = t.shape[1] // LANES
    c = jnp.concatenate([cos] * n, axis=1) if n > 1 else cos
    s = jnp.concatenate([sin_s] * n, axis=1) if n > 1 else sin_s
    a = pltpu.roll(t, HEAD_DIM // 2, 1)
    b = pltpu.roll(t, t.shape[1] - HEAD_DIM // 2, 1)
    first = (_iota(t.shape, 1) % HEAD_DIM) < HEAD_DIM // 2
    rot = jnp.where(first, b, a) * s
    return t * c - rot if inverse else t * c + rot


def _stack_heads(t, g):
    return jnp.concatenate([t[:, (GQ * g + r) * HEAD_DIM:(GQ * g + r + 1) * HEAD_DIM] for r in range(GQ)], axis=0)


def _stack_cols(t, g):
    return jnp.concatenate([t[:, GQ * g + r:GQ * g + r + 1] for r in range(GQ)], axis=0)


def _pool_sums(prev, cur, w):
    s = jnp.concatenate([prev, cur], axis=0)
    sh = 1
    while sh < w:
        s = s + pltpu.roll(s, sh, 0)
        sh *= 2
    return s[BLOCK:]


def _nt(a, b):
    return lax.dot_general(a.astype(MXU), b.astype(MXU), (((1,), (1,)), ((), ())), preferred_element_type=F32)


def _tn(a, b):
    return lax.dot_general(a.astype(MXU), b.astype(MXU), (((0,), (0,)), ((), ())), preferred_element_type=F32)


def _nn(a, b):
    return jnp.dot(a.astype(MXU), b.astype(MXU), preferred_element_type=F32)


def _mixcore_fwd(proj, cos, sin_s, pool_w, pool_scale, sinks, name):
    T = proj.shape[0]
    nb = T // BLOCK
    scale = HEAD_DIM ** -0.5

    def body(p_ref, pp_ref, c_ref, s_ref, cp_ref, sp_ref, pw_ref, ps_ref, sk_ref, cat_ref, at_ref, lse_ref):
        i = pl.program_id(0)
        has_prev = i > 0
        cur = p_ref[...]
        prv = jnp.where(has_prev, pp_ref[...], 0.0)
        tpos = (i * BLOCK + _iota((BLOCK, 1), 0) + 1).astype(F32)
        for g, w in enumerate(POOL_WINDOWS):
            sl = slice(g * POOL_GROUP, (g + 1) * POOL_GROUP)
            pooled = _pool_sums(prv[:, sl], cur[:, sl], w) / jnp.minimum(tpos, float(w)) - cur[:, sl]
            cat_ref[:, sl] = (_nn(pooled, pw_ref[g]) * ps_ref[:, sl]).astype(cat_ref.dtype)
        q = _rope(cur[:, POOL_DIM:POOL_DIM + Q_DIM], c_ref[...], s_ref[...])
        kc = _rope(cur[:, POOL_DIM + Q_DIM:POOL_DIM + Q_DIM + KV_DIM], c_ref[...], s_ref[...])
        kp = _rope(prv[:, POOL_DIM + Q_DIM:POOL_DIM + Q_DIM + KV_DIM], cp_ref[...], sp_ref[...])
        vc = cur[:, POOL_DIM + Q_DIM + KV_DIM:]
        vp = prv[:, POOL_DIM + Q_DIM + KV_DIM:]
        ri = _iota((GQ * BLOCK, BLOCK), 0) % BLOCK
        cj = _iota((GQ * BLOCK, BLOCK), 1)
        mc = cj <= ri
        mp = jnp.logical_and(cj > ri, has_prev)
        outs, lses = [], []
        for g in range(N_KV_HEADS):
            hs = slice(g * HEAD_DIM, (g + 1) * HEAD_DIM)
            qg = _stack_heads(q, g) * scale
            sc = jnp.where(mc, _nt(qg, kc[:, hs]), NEG)
            sp = jnp.where(mp, _nt(qg, kp[:, hs]), NEG)
            sink = jnp.concatenate([jnp.full((BLOCK, 1), sk_ref[GQ * g + r], F32) for r in range(GQ)], axis=0)
            m = jnp.maximum(jnp.maximum(jnp.max(sc, axis=1, keepdims=True), jnp.max(sp, axis=1, keepdims=True)), sink)
            pc = jnp.exp(sc - m)
            pp = jnp.exp(sp - m)
            den = jnp.sum(pc, axis=1, keepdims=True) + jnp.sum(pp, axis=1, keepdims=True) + jnp.exp(sink - m)
            o = (_nn(pc, vc[:, hs]) + _nn(pp, vp[:, hs])) / den
            lse = m + jnp.log(den)
            for r in range(GQ):
                outs.append(o[r * BLOCK:(r + 1) * BLOCK])
                lses.append(lse[r * BLOCK:(r + 1) * BLOCK])
        attn = jnp.concatenate(outs, axis=1)
        at_ref[...] = attn
        cat_ref[:, POOL_DIM:] = attn.astype(cat_ref.dtype)
        lane = _iota((BLOCK, LANES), 1)
        lrow = jnp.zeros((BLOCK, LANES), F32)
        for h in range(N_HEADS):
            lrow = jnp.where(lane == h, lses[h], lrow)
        lse_ref[...] = lrow

    cur = lambda w: pl.BlockSpec((BLOCK, w), lambda i: (i, 0))
    prv = lambda w: pl.BlockSpec((BLOCK, w), lambda i: (jnp.maximum(i - 1, 0), 0))
    return pl.pallas_call(
        body, grid=(nb,),
        in_specs=[cur(MIX_IN_DIM), prv(MIX_IN_DIM), cur(LANES), cur(LANES), prv(LANES), prv(LANES),
                  pl.BlockSpec((4, POOL_GROUP, POOL_GROUP), lambda i: (0, 0, 0)), pl.BlockSpec((1, POOL_DIM), lambda i: (0, 0)),
                  pl.BlockSpec(memory_space=pltpu.SMEM)],
        out_specs=[cur(2 * POOL_DIM), cur(Q_DIM), cur(LANES)],
        out_shape=[_sds((T, 2 * POOL_DIM), MXU), _sds((T, Q_DIM)), _sds((T, LANES))],
        compiler_params=_cp("parallel"), name=name)(proj, proj, cos, sin_s, cos, sin_s, pool_w, pool_scale, sinks)


def _mixcore_bwd(proj, cos, sin_s, pool_w, pool_scale, sinks, attn, lse, dcat, name):
    T = proj.shape[0]
    nb = T // BLOCK
    scale = HEAD_DIM ** -0.5
    QO, KO, VO = POOL_DIM, POOL_DIM + Q_DIM, POOL_DIM + Q_DIM + KV_DIM

    def body(p_ref, pp_ref, pn_ref, c_ref, s_ref, cp_ref, sp_ref, cn_ref, sn_ref, pw_ref, ps_ref, sk_ref,
             at_ref, atn_ref, l_ref, ln_ref, d_ref, dn_ref, dp_ref, dpw_ref, dps_ref, dsk_ref):
        i = pl.program_id(0)
        has_prev = i > 0
        has_next = i < nb - 1
        cur = p_ref[...]
        prv = jnp.where(has_prev, pp_ref[...], 0.0)
        d_cur = d_ref[...]
        d_nxt = jnp.where(has_next, dn_ref[...], 0.0)

        tpos = (i * BLOCK + _iota((BLOCK, 1), 0) + 1).astype(F32)
        tpos2 = (i * BLOCK + _iota((2 * BLOCK, 1), 0) + 1).astype(F32)
        ps = ps_ref[...]
        dps_parts, dpw_parts = [], []
        for g, w in enumerate(POOL_WINDOWS):
            sl = slice(g * POOL_GROUP, (g + 1) * POOL_GROUP)
            pooled = _pool_sums(prv[:, sl], cur[:, sl], w) / jnp.minimum(tpos, float(w)) - cur[:, sl]
            mixed = _nn(pooled, pw_ref[g])
            dps_parts.append(jnp.sum(d_cur[:, sl] * mixed, axis=0, keepdims=True))
            dm2 = jnp.concatenate([d_cur[:, sl], d_nxt[:, sl]], axis=0) * ps[:, sl]
            dpw_parts.append(_tn(pooled, dm2[:BLOCK]))
            dpool2 = _nt(dm2, pw_ref[g])
            e = dpool2 / jnp.minimum(tpos2, float(w))
            sh = 1
            while sh < w:
                e = e + pltpu.roll(e, 2 * BLOCK - sh, 0)
                sh *= 2
            dp_ref[:, sl] = (e[:BLOCK] - dpool2[:BLOCK]).astype(dp_ref.dtype)
        dpsp = jnp.concatenate(dps_parts, axis=1)

        nxt = pn_ref[...]
        q = _rope(cur[:, QO:KO], c_ref[...], s_ref[...])
        qn = _rope(nxt[:, QO:KO], cn_ref[...], sn_ref[...])
        kc = _rope(cur[:, KO:VO], c_ref[...], s_ref[...])
        kp = _rope(prv[:, KO:VO], cp_ref[...], sp_ref[...])
        vc, vp = cur[:, VO:], prv[:, VO:]
        do, don = d_cur[:, POOL_DIM:], d_nxt[:, POOL_DIM:]
        dl = do * at_ref[...]
        dln = don * atn_ref[...]
        lse, lsen = l_ref[...], ln_ref[...]
        ri = _iota((GQ * BLOCK, BLOCK), 0) % BLOCK
        cj = _iota((GQ * BLOCK, BLOCK), 1)
        mc = cj <= ri
        mp = jnp.logical_and(cj > ri, has_prev)
        mn = jnp.logical_and(cj > ri, has_next)
        dq_parts, dk_parts, dv_parts, dsk_vals = [], [], [], []
        for g in range(N_KV_HEADS):
            hs = slice(g * HEAD_DIM, (g + 1) * HEAD_DIM)
            qg, qng = _stack_heads(q, g) * scale, _stack_heads(qn, g) * scale
            dog, dong = _stack_heads(do, g), _stack_heads(don, g)
            delta = jnp.sum(_stack_heads(dl, g), axis=1, keepdims=True)
            deltan = jnp.sum(_stack_heads(dln, g), axis=1, keepdims=True)
            lg, lng = _stack_cols(lse, g), _stack_cols(lsen, g)
            pc = jnp.where(mc, jnp.exp(_nt(qg, kc[:, hs]) - lg), 0.0)
            pp = jnp.where(mp, jnp.exp(_nt(qg, kp[:, hs]) - lg), 0.0)
            pn = jnp.where(mn, jnp.exp(_nt(qng, kc[:, hs]) - lng), 0.0)
            dsc = pc * (_nt(dog, vc[:, hs]) - delta)
            dsp = pp * (_nt(dog, vp[:, hs]) - delta)
            dsn = pn * (_nt(dong, vc[:, hs]) - deltan)
            dqg = (_nn(dsc, kc[:, hs]) + _nn(dsp, kp[:, hs])) * scale
            dq_parts += [dqg[r * BLOCK:(r + 1) * BLOCK] for r in range(GQ)]
            dk_parts.append(_tn(dsc, qg) + _tn(dsn, qng))
            dv_parts.append(_tn(pc, dog) + _tn(pn, dong))
            sink = jnp.concatenate([jnp.full((BLOCK, 1), sk_ref[GQ * g + r], F32) for r in range(GQ)], axis=0)
            dsk = -jnp.exp(sink - lg) * delta
            dsk_vals += [jnp.sum(dsk[r * BLOCK:(r + 1) * BLOCK], axis=0, keepdims=True) for r in range(GQ)]
        dq = _rope(jnp.concatenate(dq_parts, axis=1), c_ref[...], s_ref[...], inverse=True)
        dk = _rope(jnp.concatenate(dk_parts, axis=1), c_ref[...], s_ref[...], inverse=True)
        dp_ref[:, QO:KO] = dq.astype(dp_ref.dtype)
        dp_ref[:, KO:VO] = dk.astype(dp_ref.dtype)
        dp_ref[:, VO:] = jnp.concatenate(dv_parts, axis=1).astype(dp_ref.dtype)
        lane = _iota((1, LANES), 1)
        dskp = jnp.zeros((1, LANES), F32)
        for h in range(N_HEADS):
            dskp = jnp.where(lane == h, dsk_vals[h], dskp)

        @pl.when(i == 0)
        def _():
            dps_ref[...] = dpsp
            dsk_ref[...] = dskp
            for g in range(4):
                dpw_ref[g] = dpw_parts[g]

        @pl.when(i > 0)
        def _():
            dps_ref[...] += dpsp
            dsk_ref[...] += dskp
            for g in range(4):
                dpw_ref[g] += dpw_parts[g]

    cur = lambda w: pl.BlockSpec((BLOCK, w), lambda i: (i, 0))
    prv = lambda w: pl.BlockSpec((BLOCK, w), lambda i: (jnp.maximum(i - 1, 0), 0))
    nxt = lambda w: pl.BlockSpec((BLOCK, w), lambda i: (jnp.minimum(i + 1, nb - 1), 0))
    return pl.pallas_call(
        body, grid=(nb,),
        in_specs=[cur(MIX_IN_DIM), prv(MIX_IN_DIM), nxt(MIX_IN_DIM),
                  cur(LANES), cur(LANES), prv(LANES), prv(LANES), nxt(LANES), nxt(LANES),
                  pl.BlockSpec((4, POOL_GROUP, POOL_GROUP), lambda i: (0, 0, 0)), pl.BlockSpec((1, POOL_DIM), lambda i: (0, 0)),
                  pl.BlockSpec(memory_space=pltpu.SMEM),
                  cur(Q_DIM), nxt(Q_DIM), cur(LANES), nxt(LANES), cur(2 * POOL_DIM), nxt(2 * POOL_DIM)],
        out_specs=[cur(MIX_IN_DIM), pl.BlockSpec((4, POOL_GROUP, POOL_GROUP), lambda i: (0, 0, 0)),
                   pl.BlockSpec((1, POOL_DIM), lambda i: (0, 0)), pl.BlockSpec((1, LANES), lambda i: (0, 0))],
        out_shape=[_sds((T, MIX_IN_DIM), MXU), _sds((4, POOL_GROUP, POOL_GROUP)), _sds((1, POOL_DIM)), _sds((1, LANES))],
        compiler_params=_cp("arbitrary"), name=name)(
            proj, proj, proj, cos, sin_s, cos, sin_s, cos, sin_s, pool_w, pool_scale, sinks, attn, attn, lse, lse, dcat, dcat)


SSM_TC = 512
GROUP_W = SSM_D_INNER // SSM_GROUPS


def _ssm_pre_fwd(xbc, cw, cb, name):
    T = xbc.shape[0]
    tm = min(T, 1024)
    K = SSM_CONV
    q = tm // SUBLANES

    def body(x_ref, xp_ref, w_ref, b_ref, o_ref, pre_ref):
        prev8 = jnp.where(pl.program_id(0) > 0, xp_ref[...], 0.0)
        pre = _conv_rows(x_ref[...], prev8, w_ref[...], b_ref[...], K)
        pre_ref[...] = pre
        o_ref[...] = _silu(pre)

    tc = 512
    blk = pl.BlockSpec((tm, tc), lambda i, j: (i, j))
    return pl.pallas_call(
        body, grid=(T // tm, SSM_CONV_DIM // tc),
        in_specs=[blk, pl.BlockSpec((SUBLANES, tc), lambda i, j: (jnp.maximum(i * q - 1, 0), j)),
                  pl.BlockSpec((K, tc), lambda i, j: (0, j)), pl.BlockSpec((1, tc), lambda i, j: (0, j))],
        out_specs=[blk, blk], out_shape=[_sds((T, SSM_CONV_DIM)), _sds((T, SSM_CONV_DIM))],
        compiler_params=_cp("parallel", "parallel"), name=name)(xbc, xbc, cw, cb)


def _ssm_pre_bwd(xbc, pre, cw, dact, name):
    T = xbc.shape[0]
    tm = min(T, 512)
    nt = T // tm
    K = SSM_CONV
    q = tm // SUBLANES
    tc = SSM_TC

    def body(x_ref, p_ref, pn_ref, d_ref, dn_ref, w_ref, dx_ref, dw_ref, db_ref):
        i = pl.program_id(1)
        w = w_ref[...]
        cur = x_ref[...]
        d_cur = d_ref[...] * _dsilu(p_ref[...])
        d_nxt = jnp.where(i == nt - 1, 0.0, dn_ref[...] * _dsilu(pn_ref[...]))
        ups = [d_cur] + [_shift_up(d_cur, d_nxt, s) for s in range(1, K)]
        dx = ups[0] * w[K - 1:K, :]
        for s in range(1, K):
            dx = dx + ups[s] * w[K - 1 - s:K - s, :]
        dx_ref[...] = dx.astype(dx_ref.dtype)
        dwp = jnp.concatenate([jnp.sum(ups[K - 1 - k] * cur, axis=0, keepdims=True) for k in range(K)], axis=0)
        dbp = jnp.sum(d_cur, axis=0, keepdims=True)

        @pl.when(i == 0)
        def _():
            dw_ref[...] = dwp
            db_ref[...] = dbp

        @pl.when(i > 0)
        def _():
            dw_ref[...] += dwp
            db_ref[...] += dbp

    nxt_row = lambda i: jnp.minimum((i + 1) * q, nt * q - 1)
    return pl.pallas_call(
        body, grid=(SSM_CONV_DIM // tc, nt),
        in_specs=[pl.BlockSpec((tm, tc), lambda j, i: (i, j)),
                  pl.BlockSpec((tm, tc), lambda j, i: (i, j)),
                  pl.BlockSpec((SUBLANES, tc), lambda j, i: (nxt_row(i), j)),
                  pl.BlockSpec((tm, tc), lambda j, i: (i, j)),
                  pl.BlockSpec((SUBLANES, tc), lambda j, i: (nxt_row(i), j)),
                  pl.BlockSpec((K, tc), lambda j, i: (0, j))],
        out_specs=[pl.BlockSpec((tm, tc), lambda j, i: (i, j)), pl.BlockSpec((K, tc), lambda j, i: (0, j)),
                   pl.BlockSpec((1, tc), lambda j, i: (0, j))],
        out_shape=[_sds((T, SSM_CONV_DIM), MXU), _sds((K, SSM_CONV_DIM)), _sds((1, SSM_CONV_DIM))],
        compiler_params=_cp("parallel", "arbitrary"), name=name)(xbc, pre, pre, dact, dact, cw)


def _dot_hi(a, b):
    return jnp.dot(a, b, precision=HI, preferred_element_type=F32)


def _ssd_common(dtraw, bias, alog):
    L = SSM_CHUNK
    xb = dtraw + bias
    dt = jnp.maximum(xb, 0.0) + jnp.log1p(jnp.exp(-jnp.abs(xb)))
    A = -jnp.exp(alog)
    tril = (_iota((L, L), 1) <= _iota((L, L), 0)).astype(F32)
    acs = _dot_hi(tril, dt * A)
    return xb, dt, A, tril, acs


def _head_selectors():
    es = (_iota((LANES, SSM_D_INNER), 0) == _iota((LANES, SSM_D_INNER), 1) // HEAD_DIM).astype(BF16)
    est = (_iota((SSM_D_INNER, LANES), 1) == _iota((SSM_D_INNER, LANES), 0) // HEAD_DIM).astype(BF16)
    return es, est


def _dot_sel(v, sel):
    hi = v.astype(BF16)
    r1 = v - hi.astype(F32)
    mid = r1.astype(BF16)
    lo = (r1 - mid.astype(F32)).astype(BF16)
    d = lambda a: jnp.dot(a, sel, preferred_element_type=F32)
    return (d(hi) + d(mid)) + d(lo)


def _expand_heads(v, es):
    return _dot_sel(v, es)


def _reduce_heads(q, est):
    return _dot_sel(q, est)


def _per_state_row(v, g):
    return jnp.concatenate([jnp.broadcast_to(v[:, GQ * g + r:GQ * g + r + 1], (HEAD_DIM, 1)) for r in range(GQ)], axis=0)


def _ssd_fwd(xact, dtraw, dt_bias, a_log, name):
    T = xact.shape[0]
    nc = T // SSM_CHUNK
    L = SSM_CHUNK
    BO, CO = SSM_D_INNER, SSM_D_INNER + SSM_GROUPS * SSM_STATE

    def body(x_ref, dt_ref, bias_ref, al_ref, es_ref, y_ref, st_ref, state):
        @pl.when(pl.program_id(0) == 0)
        def _():
            state[...] = jnp.zeros(state.shape, F32)

        _, dt, A, tril, acs = _ssd_common(dt_ref[...], bias_ref[...], al_ref[...])
        acsT = acs.T
        last = acs[L - 1:L, :]
        cd = jnp.exp(last)
        es = es_ref[...]
        dtX = _expand_heads(dt, es)
        EX = _expand_heads(jnp.exp(acs), es)
        decX = _expand_heads(jnp.exp(last - acs), es)
        for g in range(SSM_GROUPS):
            gs = slice(g * GROUP_W, (g + 1) * GROUP_W)
            B = x_ref[:, BO + g * SSM_STATE:BO + (g + 1) * SSM_STATE]
            C = x_ref[:, CO + g * SSM_STATE:CO + (g + 1) * SSM_STATE]
            X = x_ref[:, gs] * dtX[:, gs]
            CB = _nt(C, B)
            yd = []
            for r in range(GQ):
                h = GQ * g + r
                Lm = jnp.exp(jnp.where(tril > 0, acs[:, h:h + 1] - acsT[h:h + 1, :], NEG))
                yd.append(_nn(CB * Lm, X[:, r * HEAD_DIM:(r + 1) * HEAD_DIM]))
            S = state[g]
            st_ref[g] = S
            y_ref[:, gs] = jnp.concatenate(yd, axis=1) + _nt(C, S) * EX[:, gs]
            state[g] = S * _per_state_row(cd, g) + _tn(X * decX[:, gs], B)

    es, _ = _head_selectors()
    return pl.pallas_call(
        body, grid=(nc,),
        in_specs=[pl.BlockSpec((L, SSM_CONV_DIM), lambda c: (c, 0)), pl.BlockSpec((L, LANES), lambda c: (c, 0)),
                  pl.BlockSpec((1, LANES), lambda c: (0, 0)), pl.BlockSpec((1, LANES), lambda c: (0, 0)),
                  pl.BlockSpec((LANES, SSM_D_INNER), lambda c: (0, 0))],
        out_specs=[pl.BlockSpec((L, SSM_D_INNER), lambda c: (c, 0)),
                   pl.BlockSpec((None, SSM_GROUPS, GROUP_W, SSM_STATE), lambda c: (c, 0, 0, 0))],
        out_shape=[_sds((T, SSM_D_INNER)), _sds((nc, SSM_GROUPS, GROUP_W, SSM_STATE))],
        scratch_shapes=[pltpu.VMEM((SSM_GROUPS, GROUP_W, SSM_STATE), F32)],
        compiler_params=_cp("arbitrary"), name=name)(xact, dtraw, dt_bias, a_log, es)


def _ssd_bwd(xact, dtraw, dt_bias, a_log, d_skip, states, dy, name):
    T = xact.shape[0]
    nc = T // SSM_CHUNK
    L = SSM_CHUNK
    BO, CO = SSM_D_INNER, SSM_D_INNER + SSM_GROUPS * SSM_STATE

    def body(x_ref, dt_ref, bias_ref, al_ref, dsk_ref, es_ref, est_ref, st_ref, dy_ref,
             dxp_ref, ddt_ref, dbias_ref, dal_ref, dd_ref, dstate, qa, qx):
        cc = pl.program_id(0)

        @pl.when(cc == 0)
        def _():
            dstate[...] = jnp.zeros(dstate.shape, F32)

        xb, dt, A, tril, acs = _ssd_common(dt_ref[...], bias_ref[...], al_ref[...])
        acsT = acs.T
        last = acs[L - 1:L, :]
        cd = jnp.exp(last)
        es, est = es_ref[...], est_ref[...]
        dtX = _expand_heads(dt, es)
        EX = _expand_heads(jnp.exp(acs), es)
        decX = _expand_heads(jnp.exp(last - acs), es)
        lane1 = _iota((1, LANES), 1)
        lane = _iota((L, LANES), 1)
        sub = _iota((L, LANES), 0)
        ztot = jnp.zeros((1, LANES), F32)
        wrow = jnp.zeros((L, LANES), F32)
        wcolT = jnp.zeros((LANES, L), F32)
        rows_dec, rows_dd = [], []
        for g in range(SSM_GROUPS):
            gs = slice(g * GROUP_W, (g + 1) * GROUP_W)
            x = x_ref[:, gs]
            B = x_ref[:, BO + g * SSM_STATE:BO + (g + 1) * SSM_STATE]
            C = x_ref[:, CO + g * SSM_STATE:CO + (g + 1) * SSM_STATE]
            dY = dy_ref[:, gs]
            dtx, e_x, dec_x = dtX[:, gs], EX[:, gs], decX[:, gs]
            X = x * dtx
            CB = _nt(C, B)
            S = st_ref[g]
            dS_out = dstate[g]
            dcb_sum = jnp.zeros((L, L), F32)
            dxd = []
            for r in range(GQ):
                h = GQ * g + r
                hs = slice(r * HEAD_DIM, (r + 1) * HEAD_DIM)
                Lm = jnp.exp(jnp.where(tril > 0, acs[:, h:h + 1] - acsT[h:h + 1, :], NEG))
                M = CB * Lm
                dM = _nt(dY[:, hs], X[:, hs])
                dxd.append(_tn(M, dY[:, hs]))
                dcb_sum = dcb_sum + dM * Lm
                Wm = dM * M
                wrow = jnp.where(lane == h, jnp.sum(Wm, axis=1, keepdims=True), wrow)
                wcolT = jnp.where(sub == h, jnp.sum(Wm, axis=0, keepdims=True), wcolT)
            dXd = jnp.concatenate(dxd, axis=1)
            G = _nt(C, S)
            dG = dY * e_x
            dDX = _nt(B, dS_out)
            dX = dXd + dec_x * dDX
            t_dec = dDX * X * dec_x
            qa[:, gs] = dG * G - t_dec
            qx[:, gs] = dX * x
            rows_dec.append(jnp.sum(t_dec, axis=0, keepdims=True))
            rows_dd.append(jnp.sum(dY * x, axis=0, keepdims=True))
            zc = jnp.sum(dS_out * S, axis=1, keepdims=True)
            for r in range(GQ):
                ztot = jnp.where(lane1 == GQ * g + r, jnp.sum(zc[r * HEAD_DIM:(r + 1) * HEAD_DIM], axis=0, keepdims=True), ztot)
            dxp_ref[:, gs] = dX * dtx + dY * dsk_ref[:, gs]
            dxp_ref[:, BO + g * SSM_STATE:BO + (g + 1) * SSM_STATE] = _tn(dcb_sum, C) + _nn(X * dec_x, dS_out)
            dxp_ref[:, CO + g * SSM_STATE:CO + (g + 1) * SSM_STATE] = _nn(dcb_sum, B) + _nn(dG, S)
            dstate[g] = dS_out * _per_state_row(cd, g) + _tn(dG, C)
        rows = jnp.concatenate([jnp.concatenate(rows_dec, axis=1), jnp.concatenate(rows_dd, axis=1)]
                               + [jnp.zeros((SUBLANES - 2, SSM_D_INNER), F32)], axis=0)
        rsum = _reduce_heads(rows, est)
        dlast = rsum[0:1, :] + cd * ztot
        dacs = (wrow - wcolT.T) + _reduce_heads(qa[...], est) + jnp.where(sub == L - 1, dlast, 0.0)
        triu = (_iota((L, L), 0) <= _iota((L, L), 1)).astype(F32)
        da = _dot_hi(triu, dacs)
        ddtraw = (da * A + _reduce_heads(qx[...], est)) * (1.0 / (1.0 + jnp.exp(-xb)))
        ddt_ref[...] = ddtraw
        dal = jnp.sum(da * dt, axis=0, keepdims=True) * A
        ddp = rsum[1:2, :]
        dbp = jnp.sum(ddtraw, axis=0, keepdims=True)

        @pl.when(cc == 0)
        def _():
            dbias_ref[...] = dbp
            dal_ref[...] = dal
            dd_ref[...] = ddp

        @pl.when(cc > 0)
        def _():
            dbias_ref[...] += dbp
            dal_ref[...] += dal
            dd_ref[...] += ddp

    rc = lambda c: nc - 1 - c
    vec = pl.BlockSpec((1, LANES), lambda c: (0, 0))
    es, est = _head_selectors()
    return pl.pallas_call(
        body, grid=(nc,),
        in_specs=[pl.BlockSpec((L, SSM_CONV_DIM), lambda c: (rc(c), 0)), pl.BlockSpec((L, LANES), lambda c: (rc(c), 0)), vec, vec,
                  pl.BlockSpec((1, SSM_D_INNER), lambda c: (0, 0)),
                  pl.BlockSpec((LANES, SSM_D_INNER), lambda c: (0, 0)), pl.BlockSpec((SSM_D_INNER, LANES), lambda c: (0, 0)),
                  pl.BlockSpec((None, SSM_GROUPS, GROUP_W, SSM_STATE), lambda c: (rc(c), 0, 0, 0)),
                  pl.BlockSpec((L, SSM_D_INNER), lambda c: (rc(c), 0))],
        out_specs=[pl.BlockSpec((L, SSM_CONV_DIM), lambda c: (rc(c), 0)),
                   pl.BlockSpec((L, LANES), lambda c: (rc(c), 0)), vec, vec, vec],
        out_shape=[_sds((T, SSM_CONV_DIM)), _sds((T, LANES)), _sds((1, LANES)), _sds((1, LANES)), _sds((1, LANES))],
        scratch_shapes=[pltpu.VMEM((SSM_GROUPS, GROUP_W, SSM_STATE), F32), pltpu.VMEM((L, SSM_D_INNER), F32),
                        pltpu.VMEM((L, SSM_D_INNER), F32)],
        compiler_params=_cp("arbitrary"), name=name)(xact, dtraw, dt_bias, a_log, d_skip, es, est, states, dy)


def _ssm_post_fwd(y, xact, z, d_skip, nw, name):
    T = y.shape[0]
    tm = min(T, 256)
    W = SSM_D_INNER

    def body(y_ref, x_ref, z_ref, d_ref, w_ref, o_ref):
        y2 = (y_ref[...] + d_ref[...] * x_ref[...]) * _silu(z_ref[...])
        r = lax.rsqrt(jnp.mean(y2 * y2, axis=-1, keepdims=True) + SSM_NORM_EPS)
        o_ref[...] = (y2 * r * w_ref[...]).astype(o_ref.dtype)

    row = pl.BlockSpec((tm, W), lambda i: (i, 0))
    vec = pl.BlockSpec((1, W), lambda i: (0, 0))
    return pl.pallas_call(
        body, grid=(T // tm,), in_specs=[row, row, row, vec, vec], out_specs=row, out_shape=_sds((T, W), MXU),
        compiler_params=_cp("parallel"), name=name)(y, xact, z, d_skip, nw)


def _ssm_post_bwd(y, xact, z, d_skip, nw, dyn, name):
    T = y.shape[0]
    tm = min(T, 256)
    W = SSM_D_INNER

    def body(y_ref, x_ref, z_ref, d_ref, w_ref, dn_ref, dyg_ref, dz_ref, dw_ref):
        zv = z_ref[...]
        sz = _silu(zv)
        yg = y_ref[...] + d_ref[...] * x_ref[...]
        y2 = yg * sz
        r = lax.rsqrt(jnp.mean(y2 * y2, axis=-1, keepdims=True) + SSM_NORM_EPS)
        y2h = y2 * r
        dn = dn_ref[...]
        gy = dn * w_ref[...]
        dy2 = r * (gy - y2h * jnp.mean(gy * y2h, axis=-1, keepdims=True))
        dyg_ref[...] = dy2 * sz
        dz_ref[...] = (dy2 * yg * _dsilu(zv)).astype(dz_ref.dtype)
        part = jnp.sum(dn * y2h, axis=0, keepdims=True)

        @pl.when(pl.program_id(0) == 0)
        def _():
            dw_ref[...] = part

        @pl.when(pl.program_id(0) > 0)
        def _():
            dw_ref[...] += part

    row = pl.BlockSpec((tm, W), lambda i: (i, 0))
    vec = pl.BlockSpec((1, W), lambda i: (0, 0))
    return pl.pallas_call(
        body, grid=(T // tm,), in_specs=[row, row, row, vec, vec, row], out_specs=[row, row, vec],
        out_shape=[_sds((T, W)), _sds((T, W), MXU), _sds((1, W))],
        compiler_params=_cp("arbitrary"), name=name)(y, xact, z, d_skip, nw, dyn)


def _local_step(x0, cos, sin_s, target, P, fetch, token, send):
    mmf = functools.partial(_mm, tm=1024)
    big, small = {}, {}
    P = dict(P, wup={}, wdn={}, fcw={})
    h0 = _rmsnorm_fwd(x0, P["nm"][0], "norm_mix0", token=token)
    proj0 = mmf(h0, P["wmiT"], tb=True, tn=1280, tk=1024, name="mix_in")
    cat, attn, lse = _mixcore_fwd(proj0, cos, sin_s, P["pool_w"], P["pool_scale"], P["sinks"], "mixcore_fwd")
    x1, hf0 = mmf(cat, P["wmo"], tn=1024, tk=1024, res=x0, norm_w=P["nf"][0], name="mix_out")

    def ffn_fwd(xin, hf, i, next_norm):
        got = fetch(f"ffn{i}", hf)
        P["wup"][i], P["wdn"][i], P["fcw"][i] = got["wup"], got["wdn"], got["fcw"]
        hid = mmf(hf, P["wup"][i], tn=1408, tk=1024, out_dtype=MXU, name=f"ffn_up{i}")
        act, hc = _ffn_mid_fwd(hid, P["fcw"][i], P["fcb"][i], f"ffn_mid_fwd{i}")
        xout = mmf(act, P["wdn"][i], tn=1024, tk=D_FF, res=xin, norm_w=next_norm, name=f"ffn_down{i}")
        return (hid, hc), act, xout

    hid0, act0, (x2, h1) = ffn_fwd(x1, hf0, 0, P["nm"][1])
    P.update(fetch("ssm", h1))
    z = mmf(h1, P["wzT"], tb=True, tn=1024, tk=1024, name="ssm_in_z")
    xbc = mmf(h1, P["wxbcT"], tb=True, tn=1024, tk=1024, name="ssm_in_xbc")
    dtraw = mmf(h1, P["wdtT"], tb=True, tn=128, tk=1024, name="ssm_in_dt")
    xact, xpre = _ssm_pre_fwd(xbc, P["scw"], P["scb"], "ssm_pre_fwd")
    y, states = _ssd_fwd(xact, dtraw, P["dt_bias"], P["a_log"], "ssd_fwd")
    yn = _ssm_post_fwd(y, xact, z, P["d_exp"], P["snorm"], "ssm_post_fwd")
    x3, hf1 = mmf(yn, P["wso"], tn=1024, tk=SSM_D_INNER, res=x2, norm_w=P["nf"][1], name="ssm_out")
    hid1, act1, x4 = ffn_fwd(x3, hf1, 1, None)
    loss_row, dx4, d_nfin = _loss_head(x4, P["nfin"], target, "loss_head")
    small["norm_final"] = d_nfin

    def ffn_bwd(xin, dxo, hf, hid, act, i):
        da = mmf(dxo, P["wdn"][i], tb=True, tn=1408, tk=1024, name=f"ffn_down_dx{i}")
        big[f"ffn_w_down{i}"] = dwf(act, dxo, tm=1408, tn=1024, name=f"ffn_down_dw{i}").reshape(N_CHIPS, D_FF // N_CHIPS, D_MODEL)
        dhid, dcw, dcb = _ffn_mid_bwd(hid[0], hid[1], P["fcw"][i], da, f"ffn_mid_bwd{i}")
        big[f"ffn_w_up{i}"] = dwf(hf, dhid, tm=1024, tn=1408, out_shard_perm=(0, 2, 1, 3), name=f"ffn_up_dw{i}")
        tok = send(f"ffn{i}", [big[f"ffn_w_up{i}"], big[f"ffn_w_down{i}"]])
        dxi, dnf = _mm(dhid, P["wup"][i], tb=True, tm=512, tn=1024, tk=2816, norm_bwd=(xin, P["nf"][i], dxo, tok), name=f"ffn_up_dx{i}")
        return dxi, dnf, dcw, dcb

    dwf = functools.partial(_mm, ta=True, tk=2048, out_dtype=BF16)
    dx3, dnf1, dfcw1, dfcb1 = ffn_bwd(x3, dx4, hf1, hid1, act1, 1)
    dyn = mmf(dx3, P["wso"], tb=True, tn=1024, tk=1024, name="ssm_out_dx")
    big["ssm_w_out"] = dwf(yn, dx3, tm=1024, tn=1024, name="ssm_out_dw").reshape(N_CHIPS, SSM_D_INNER // N_CHIPS, D_MODEL)
    dyg, dz, d_snorm = _ssm_post_bwd(y, xact, z, P["d_exp"], P["snorm"], dyn, "ssm_post_bwd")
    dxact_p, ddtraw, d_dtb, d_alog, d_dskip = _ssd_bwd(xact, dtraw, P["dt_bias"], P["a_log"], P["d_exp"], states, dyg, "ssd_bwd")
    dxbc, d_scw, d_scb = _ssm_pre_bwd(xbc, xpre, P["scw"], dxact_p, "ssm_pre_bwd")
    dwz = dwf(dz, h1, tm=1024, tn=1024, name="ssm_in_dw_z")
    dwxbc = dwf(dxbc, h1, tm=1024, tn=1024, name="ssm_in_dw_xbc")
    dwdt = dwf(ddtraw, h1, tm=128, tn=1024, name="ssm_in_dw_dt")
    dwsi = jnp.concatenate([dwz, dwxbc, dwdt[:SSM_HEADS]], axis=0)
    big["ssm_w_in"] = dwsi.reshape(N_CHIPS, SSM_IN_DIM // N_CHIPS, D_MODEL)
    tok = send("ssm", [big["ssm_w_in"], big["ssm_w_out"]])
    dh1 = mmf(dz, P["wzT"], tn=1024, tk=2048, name="ssm_in_dx_z")
    dh1 = mmf(dxbc, P["wxbcT"], tn=1024, tk=2048, res=dh1, name="ssm_in_dx_xbc")
    dx2, dnm1 = mmf(ddtraw, P["wdtT"], tn=1024, tk=128, res=dh1, norm_bwd=(x2, P["nm"][1], dx3, tok), name="ssm_in_dx_dt")
    dx1, dnf0, dfcw0, dfcb0 = ffn_bwd(x1, dx2, hf0, hid0, act0, 0)
    dcat = mmf(dx1, P["wmo"], tb=True, tn=1024, tk=1024, name="mix_out_dx")
    big["mix_w_out"] = dwf(cat, dx1, tm=1024, tn=1024, name="mix_out_dw").reshape(N_CHIPS, D_MODEL // N_CHIPS, D_MODEL)
    dproj0, d_pw, d_ps, d_sk = _mixcore_bwd(proj0, cos, sin_s, P["pool_w"], P["pool_scale"], P["sinks"], attn, lse, dcat, "mixcore_bwd")
    big["mix_w_in"] = dwf(dproj0, h0, tm=1280, tn=1024, name="mix_in_dw").reshape(N_CHIPS, MIX_IN_DIM // N_CHIPS, D_MODEL)
    tok = send("mix", [big["mix_w_in"], big["mix_w_out"]])
    dx0, dnm0 = mmf(dproj0, P["wmiT"], tn=1024, tk=1280, norm_bwd=(x0, P["nm"][0], dx1, tok), name="mix_in_dx")

    def unperm_cols(a):
        r = a.shape[0]
        t = a.reshape(r, N_CHIPS, FFN_TC)
        return jnp.stack([t[:, p] for p in _PERM], axis=0)

    small["norm_mix"] = jnp.concatenate([dnm0, dnm1], axis=0)
    small["norm_ffn"] = jnp.concatenate([dnf0, dnf1], axis=0)
    small["pool_w"] = d_pw.reshape(4 * POOL_GROUP, POOL_GROUP)
    small["pool_scale"] = d_ps
    small["attn_sinks"] = d_sk
    small["ssm_dt_bias"] = d_dtb
    small["ssm_A_log"] = d_alog
    small["ssm_D"] = d_dskip
    fcb = jnp.stack([unperm_cols(dfcb0), unperm_cols(dfcb1)], axis=0)
    small["ffn_conv_b"] = fcb.reshape(2, 2 * D_FF)
    small["ssm_conv_w"] = d_scw.reshape(SSM_CONV, N_CHIPS, SSM_CONV_DIM // N_CHIPS).transpose(1, 0, 2)
    small["ssm_conv_b"] = d_scb.reshape(N_CHIPS, 1, SSM_CONV_DIM // N_CHIPS)
    small["ssm_norm"] = d_snorm.reshape(N_CHIPS, 1, SSM_D_INNER // N_CHIPS)
    small["ffn_conv_w"] = jnp.concatenate([unperm_cols(dfcw0), unperm_cols(dfcw1)], axis=1)
    return loss_row, dx0, big, small


ANY = pl.BlockSpec(memory_space=pl.ANY)


def _place():
    return lax.axis_index("x"), lax.axis_index("y"), lax.axis_index("c")


def _gather_shards(shards, name):
    n = len(shards)
    split = [s.size >= (1 << 16) for s in shards]

    def half(ref, a, h):
        shp = shards[a].shape
        if len(shp) == 3:
            return ref.at[h]
        r2 = shp[0] // 2
        return ref.at[pl.ds(pl.multiple_of(h * r2, 2 * SUBLANES), r2), :]

    def body(*refs):
        ins, outs = refs[:n], refs[n:2 * n]
        send, recv, fsend, frecv = refs[2 * n:]
        x, y, c = _place()
        k = 2 * x + y
        chips = [(1 - x, y), (x, 1 - y), (1 - x, 1 - y)]

        def ici(a, j, src_slot_ref, dst_slot):
            px, py = chips[j]
            src = half(src_slot_ref, a, c) if split[a] else src_slot_ref
            dst = half(outs[a].at[dst_slot], a, c) if split[a] else outs[a].at[dst_slot]
            return pltpu.make_async_remote_copy(src, dst, send.at[a, j], recv.at[a, j], device_id=(px, py, c), device_id_type=MESH)

        def d2d(a, j, h):
            px, py = chips[j]
            part = half(outs[a].at[2 * px + py], a, h)
            return pltpu.make_async_remote_copy(part, part, fsend.at[a, j], frecv.at[a, j], device_id=(x, y, 1 - c), device_id_type=MESH)

        sends = [ici(a, j, ins[a], k) for a in range(n) for j in range(3)]
        for cp in sends:
            cp.start()
        passed = []
        for a in range(n):
            for j, (px, py) in enumerate(chips):
                ici(a, j, ins[a], 2 * px + py).wait_recv()
                if split[a]:
                    passed.append(d2d(a, j, c))
                    passed[-1].start()
        for a in range(n):
            if split[a]:
                for j in range(3):
                    d2d(a, j, 1 - c).wait_recv()
        for cp in sends + passed:
            cp.wait_send()

    return pl.pallas_call(
        body, in_specs=[ANY] * n, out_specs=[ANY] * n,
        out_shape=[_sds((N_CHIPS,) + s.shape, s.dtype) for s in shards],
        scratch_shapes=[pltpu.SemaphoreType.DMA((n, 3))] * 4,
        compiler_params=pltpu.CompilerParams(has_side_effects=True), name=name)(*shards)


HBM = pl.BlockSpec(memory_space=pltpu.HBM)
SEM = pl.BlockSpec(memory_space=pltpu.SEMAPHORE)
DATAFLOW = pltpu.SideEffectType.DATAFLOW_SIDE_EFFECTING


def _spread_start(groups, slot_src, after, name):
    flat = [a for grp in groups for a in grp]
    n = len(flat)
    ng = len(groups)
    offs = [sum(len(g) for g in groups[:i]) for i in range(ng)]
    lshape = [(a.shape if slot_src else (N_CHIPS,) + a.shape) for a in flat]

    nsem = 6 * n

    def body(*refs):
        src, land = refs[:n], refs[n:2 * n]
        sems = refs[2 * n + 1:2 * n + 1 + nsem]
        token = refs[-1]
        x, y, c = _place()
        k = 2 * x + y
        chips = [(1 - x, y), (x, 1 - y), (1 - x, 1 - y)]
        for a in range(n):
            for j, (px, py) in enumerate(chips):
                s = src[a].at[2 * px + py] if slot_src else src[a]
                pltpu.make_async_remote_copy(s, land[a].at[k], sems[6 * a + 2 * j], sems[6 * a + 2 * j + 1],
                                             device_id=(px, py, c), device_id_type=MESH).start()
        token[...] = jnp.zeros(token.shape, token.dtype)

    out_shape = [pltpu.SemaphoreType.DMA(())] * nsem
    out_shape += [pltpu.HBM(a.shape, a.dtype) for a in flat] + [pltpu.HBM(s, a.dtype) for s, a in zip(lshape, flat)]
    out_shape.append(_sds((SUBLANES, LANES)))
    args = [pltpu.with_memory_space_constraint(a, pltpu.HBM) for a in flat]
    args += [pltpu.with_memory_space_constraint(lax.empty(s, a.dtype), pltpu.HBM) for s, a in zip(lshape, flat)]
    res = pl.pallas_call(
        body, name=name, out_shape=tuple(out_shape), in_specs=[HBM] * (2 * n) + [pl.BlockSpec(memory_space=pl.ANY)],
        out_specs=tuple([SEM] * nsem + [HBM] * (2 * n) + [pl.BlockSpec(memory_space=pltpu.VMEM)]),
        input_output_aliases={i: nsem + i for i in range(2 * n)},
        compiler_params=pltpu.CompilerParams(has_side_effects=DATAFLOW))(*args, after)
    sems, thru, token = res[:nsem], res[nsem:nsem + 2 * n], res[-1]
    out = []
    for gi, grp in enumerate(groups):
        sl = slice(offs[gi], offs[gi] + len(grp))
        out.append((list(sems[6 * offs[gi]:6 * (offs[gi] + len(grp))]), list(thru[:n][sl]), list(thru[n:][sl])))
    return out, token


def _spread_wait(started, slot_src, after, name):
    sems, srcs, lands = started
    n = len(srcs)

    def body(*refs):
        src, land = refs[:n], refs[n:2 * n]
        sem = refs[2 * n:2 * n + 6 * n]
        x, y, c = _place()
        chips = [(1 - x, y), (x, 1 - y), (1 - x, 1 - y)]
        for a in range(n):
            for j, (px, py) in enumerate(chips):
                s = src[a].at[2 * px + py] if slot_src else src[a]
                cp = pltpu.make_async_remote_copy(s, land[a].at[2 * px + py], sem[6 * a + 2 * j], sem[6 * a + 2 * j + 1],
                                                  device_id=(px, py, c), device_id_type=MESH)
                cp.wait_send()
                cp.wait_recv()

    res = pl.pallas_call(
        body, name=name, out_shape=tuple([pltpu.HBM(a.shape, a.dtype) for a in srcs] + [pltpu.HBM(a.shape, a.dtype) for a in lands]),
        in_specs=[HBM] * (2 * n) + [SEM] * (6 * n) + [pl.BlockSpec(memory_space=pl.ANY)], out_specs=tuple([HBM] * (2 * n)),
        input_output_aliases={i: i for i in range(2 * n)},
        compiler_params=pltpu.CompilerParams(has_side_effects=DATAFLOW))(*srcs, *lands, *sems, after)
    return list(res[:n]), list(res[n:])


def _sibling_exchange(fs, name):
    n = len(fs)

    def body(*refs):
        ins, outs = refs[:n], refs[n:2 * n]
        send, recv = refs[2 * n:]
        x, y, c = _place()
        cps = [pltpu.make_async_remote_copy(ins[a], outs[a], send.at[a], recv.at[a],
                                            device_id=(x, y, 1 - c), device_id_type=MESH) for a in range(n)]
        for cp in cps:
            cp.start()
        for cp in cps:
            cp.wait()

    return pl.pallas_call(
        body, in_specs=[ANY] * n, out_specs=[ANY] * n, out_shape=[_sds(f.shape, f.dtype) for f in fs],
        scratch_shapes=[pltpu.SemaphoreType.DMA((n,)), pltpu.SemaphoreType.DMA((n,))],
        compiler_params=pltpu.CompilerParams(has_side_effects=True), name=name)(*fs)


def _tile2d(rows, cols, budget=1024 * 1024, step=2 * SUBLANES):
    fits = [t for t in range(step, rows + 1, step) if rows % t == 0 and t * cols * 4 <= budget]
    if fits:
        return fits[-1], cols
    fits = [t for t in range(LANES, cols + 1, LANES) if cols % t == 0 and rows * t * 4 <= budget]
    assert fits, (rows, cols)
    return rows, fits[-1]


def _chip_sum(own, parts, kidx, name):
    _, R, C = parts.shape
    tr, tc = _tile2d(R, C)

    def body(k_ref, o_ref_in, p1_ref, p2_ref, p3_ref, o_ref):
        o_ref[...] = ((o_ref_in[...].astype(F32) + p1_ref[...].astype(F32)) + p2_ref[...].astype(F32)) + p3_ref[...].astype(F32)

    def slot(d):
        return pl.BlockSpec((None, tr, tc), lambda i, j, k: ((k[0] + d) % N_CHIPS, i, j))

    return pl.pallas_call(
        body,
        grid_spec=pltpu.PrefetchScalarGridSpec(
            num_scalar_prefetch=1, grid=(R // tr, C // tc), in_specs=[slot(0), slot(1), slot(2), slot(3)],
            out_specs=pl.BlockSpec((tr, tc), lambda i, j, k: (i, j))),
        out_shape=_sds((R, C)), compiler_params=_cp("parallel", "parallel"), name=name)(kidx, own, parts, parts, parts)


def _adamw_math(w, g, m, v):
    m2 = ADAM_B1 * m + (1.0 - ADAM_B1) * g
    v2 = ADAM_B2 * v + (1.0 - ADAM_B2) * (g * g)
    m_hat = m2 / (1.0 - ADAM_B1 ** ADAM_STEP)
    v_hat = v2 / (1.0 - ADAM_B2 ** ADAM_STEP)
    delta = -ADAM_LR * (m_hat / (jnp.sqrt(v_hat) + ADAM_EPS) + ADAM_WD * w)
    return delta, m2, v2


def _adamw(w, m, v, gparts, name):
    Lw, R, C = w.shape
    tr, tc = _tile2d(R, C)
    flat = [h for pair in gparts for h in pair]

    def body(*refs):
        w_ref, m_ref, v_ref = refs[:3]
        g_refs = refs[3:3 + 2 * Lw]
        go_ref, d_ref, mo_ref, vo_ref = refs[3 + 2 * Lw:]
        g = g_refs[0][...] + g_refs[1][...]
        for l in range(1, Lw):
            g = jnp.where(pl.program_id(0) == l, g_refs[2 * l][...] + g_refs[2 * l + 1][...], g)
        d, m2, v2 = _adamw_math(w_ref[...], g, m_ref[...], v_ref[...])
        go_ref[...] = g
        d_ref[...] = d
        mo_ref[...] = m2
        vo_ref[...] = v2

    blk = pl.BlockSpec((None, tr, tc), lambda l, i, j: (l, i, j))
    gblk = pl.BlockSpec((tr, tc), lambda l, i, j: (i, j))
    return pl.pallas_call(
        body, grid=(Lw, R // tr, C // tc), in_specs=[blk, blk, blk] + [gblk] * (2 * Lw), out_specs=[blk] * 4,
        out_shape=[_sds((Lw, R, C))] * 4, compiler_params=_cp("parallel", "parallel", "parallel"), name=name)(w, m, v, *flat)


def _small_adamw(grads, wmv, name):
    n = len(grads)

    def body(*refs):
        g_in, p_in, outs = refs[:n], refs[n:4 * n], refs[4 * n:]
        for a in range(n):
            g = g_in[a][...]
            d_, m2, v2 = _adamw_math(p_in[3 * a][...], g, p_in[3 * a + 1][...], p_in[3 * a + 2][...])
            outs[4 * a][...] = g
            outs[4 * a + 1][...] = d_
            outs[4 * a + 2][...] = m2
            outs[4 * a + 3][...] = v2

    vm = pl.BlockSpec(memory_space=pltpu.VMEM)
    args = list(grads) + [t for tri in wmv for t in tri]
    out_shape = [_sds(g.shape) for g in grads for _ in range(4)]
    return pl.pallas_call(body, in_specs=[vm] * len(args), out_specs=[vm] * len(out_shape), out_shape=out_shape,
                          compiler_params=pltpu.CompilerParams(vmem_limit_bytes=V7X_VMEM_LIMIT), name=name)(*args)


def _small_allreduce(partials, pshapes, loss_row, name):
    n = len(partials)
    gshapes = [p.shape for p in partials] + [loss_row.shape]
    ng = n + 1

    def body(*refs):
        g_in = refs[:ng]
        outs = refs[ng:2 * ng]
        bufs = refs[2 * ng:3 * ng]
        send, recv = refs[-2:]
        x, y, c = _place()
        me = 4 * x + 2 * y + c
        k = 2 * x + y
        flips = [(fx, fy, fc) for fx in (0, 1) for fy in (0, 1) for fc in (0, 1)][1:]

        def peer(f):
            return (x ^ f[0], y ^ f[1], c ^ f[2])

        def slot(p):
            return 4 * p[0] + 2 * p[1] + p[2]

        for a in range(ng):
            bufs[a][me] = g_in[a][...]
        sends = [pltpu.make_async_remote_copy(g_in[a], bufs[a].at[me], send.at[a, j], recv.at[a, j],
                                              device_id=peer(f), device_id_type=MESH)
                 for a in range(ng) for j, f in enumerate(flips)]
        for cp in sends:
            cp.start()
        for a in range(ng):
            for j, f in enumerate(flips):
                pltpu.make_async_remote_copy(g_in[a], bufs[a].at[slot(peer(f))], send.at[a, j], recv.at[a, j],
                                             device_id=peer(f), device_id_type=MESH).wait_recv()
        for cp in sends:
            cp.wait_send()
        for a in range(ng):
            sharded = len(gshapes[a]) == 3

            def part(d):
                return bufs[a][d, k] if sharded else bufs[a][d]

            tot = part(0)
            for d in range(1, N_DEV):
                tot = tot + part(d)
            if a == n:
                outs[n][...] = tot
            else:
                pr, pc = pshapes[a]
                outs[a][...] = tot[:pr, :pc]

    vm = pl.BlockSpec(memory_space=pltpu.VMEM)
    args = list(partials) + [loss_row]
    out_shape = [_sds(ps) for ps in pshapes] + [_sds(loss_row.shape)]
    return pl.pallas_call(
        body, in_specs=[vm] * len(args), out_specs=[vm] * len(out_shape), out_shape=out_shape,
        scratch_shapes=[pltpu.VMEM((N_DEV,) + tuple(s), F32) for s in gshapes]
        + [pltpu.SemaphoreType.DMA((ng, N_DEV - 1)), pltpu.SemaphoreType.DMA((ng, N_DEV - 1))],
        compiler_params=pltpu.CompilerParams(has_side_effects=True, vmem_limit_bytes=V7X_VMEM_LIMIT), name=name)(*args)


_PERM = (0, 2, 1, 3)


def _cols_from_shards(g):
    return g.transpose(1, 0, 2).reshape(g.shape[1], N_CHIPS * g.shape[2])


def _rope_tables(positions):
    inv_freq = ROPE_THETA ** (-jnp.arange(0, HEAD_DIM, 2, dtype=F32) / HEAD_DIM)
    ang = positions.astype(F32).reshape(-1, 1) * inv_freq
    cos, sin = jnp.cos(ang), jnp.sin(ang)
    cos = jnp.concatenate([cos, cos, cos, cos], axis=-1)
    sin_s = jnp.concatenate([-sin, sin, -sin, sin], axis=-1)
    return cos, sin_s


def kernel(x, positions, norm_mix, norm_ffn, norm_final, mix_w_in, pool_w, pool_scale, attn_sinks, mix_w_out, ssm_w_in, ssm_conv_w, ssm_conv_b, ssm_dt_bias, ssm_A_log, ssm_D, ssm_norm, ssm_w_out, ffn_w_up, ffn_conv_w, ffn_conv_b, ffn_w_down, loss_target, m_norm_mix, m_norm_ffn, m_norm_final, m_mix_w_in, m_pool_w, m_pool_scale, m_attn_sinks, m_mix_w_out, m_ssm_w_in, m_ssm_conv_w, m_ssm_conv_b, m_ssm_dt_bias, m_ssm_A_log, m_ssm_D, m_ssm_norm, m_ssm_w_out, m_ffn_w_up, m_ffn_conv_w, m_ffn_conv_b, m_ffn_w_down, v_norm_mix, v_norm_ffn, v_norm_final, v_mix_w_in, v_pool_w, v_pool_scale, v_attn_sinks, v_mix_w_out, v_ssm_w_in, v_ssm_conv_w, v_ssm_conv_b, v_ssm_dt_bias, v_ssm_A_log, v_ssm_D, v_ssm_norm, v_ssm_w_out, v_ffn_w_up, v_ffn_conv_w, v_ffn_conv_b, v_ffn_w_down):
    W = dict(norm_mix=norm_mix, norm_ffn=norm_ffn, norm_final=norm_final, mix_w_in=mix_w_in, pool_w=pool_w, pool_scale=pool_scale, attn_sinks=attn_sinks, mix_w_out=mix_w_out, ssm_w_in=ssm_w_in, ssm_conv_w=ssm_conv_w, ssm_conv_b=ssm_conv_b, ssm_dt_bias=ssm_dt_bias, ssm_A_log=ssm_A_log, ssm_D=ssm_D, ssm_norm=ssm_norm, ssm_w_out=ssm_w_out, ffn_w_up=ffn_w_up, ffn_conv_w=ffn_conv_w, ffn_conv_b=ffn_conv_b, ffn_w_down=ffn_w_down)
    Mo = dict(norm_mix=m_norm_mix, norm_ffn=m_norm_ffn, norm_final=m_norm_final, mix_w_in=m_mix_w_in, pool_w=m_pool_w, pool_scale=m_pool_scale, attn_sinks=m_attn_sinks, mix_w_out=m_mix_w_out, ssm_w_in=m_ssm_w_in, ssm_conv_w=m_ssm_conv_w, ssm_conv_b=m_ssm_conv_b, ssm_dt_bias=m_ssm_dt_bias, ssm_A_log=m_ssm_A_log, ssm_D=m_ssm_D, ssm_norm=m_ssm_norm, ssm_w_out=m_ssm_w_out, ffn_w_up=m_ffn_w_up, ffn_conv_w=m_ffn_conv_w, ffn_conv_b=m_ffn_conv_b, ffn_w_down=m_ffn_w_down)
    Vo = dict(norm_mix=v_norm_mix, norm_ffn=v_norm_ffn, norm_final=v_norm_final, mix_w_in=v_mix_w_in, pool_w=v_pool_w, pool_scale=v_pool_scale, attn_sinks=v_attn_sinks, mix_w_out=v_mix_w_out, ssm_w_in=v_ssm_w_in, ssm_conv_w=v_ssm_conv_w, ssm_conv_b=v_ssm_conv_b, ssm_dt_bias=v_ssm_dt_bias, ssm_A_log=v_ssm_A_log, ssm_D=v_ssm_D, ssm_norm=v_ssm_norm, ssm_w_out=v_ssm_w_out, ffn_w_up=v_ffn_w_up, ffn_conv_w=v_ffn_conv_w, ffn_conv_b=v_ffn_conv_b, ffn_w_down=v_ffn_w_down)

    kchip = 2 * lax.axis_index("x") + lax.axis_index("y")

    def own_slot(g, own):
        return lax.dynamic_update_slice_in_dim(g, own[None], kchip, axis=0)

    def tr(t):
        return jnp.swapaxes(t[0], 0, 1)

    later = dict(ffn0=[ffn_w_up[0].astype(MXU), ffn_w_down[0].astype(MXU)],
                 ssm=[tr(ssm_w_in).astype(MXU), ssm_w_out[0].astype(MXU)],
                 ffn1=[ffn_w_up[1].astype(MXU), ffn_w_down[1].astype(MXU)])
    sh = [tr(mix_w_in).astype(MXU), mix_w_out[0].astype(MXU), ssm_conv_w[0], ssm_conv_b, ssm_norm, ffn_conv_w]
    first = _gather_shards(sh, "gather_first")
    g_mi, g_mo, g_scw, g_scb, g_sn, g_fcw = [own_slot(g, own) for g, own in zip(first, sh)]
    started, token = _spread_start(list(later.values()), False, first[0], "gather_start")
    started = dict(zip(later.keys(), started))
    fcw = [jnp.concatenate([g_fcw[p, i] for p in _PERM], axis=1) for i in range(2)]
    P = dict(
        nm=norm_mix, nf=norm_ffn, nfin=norm_final,
        wmiT=g_mi.reshape(MIX_IN_DIM, D_MODEL), wmo=g_mo.reshape(D_MODEL, D_MODEL),
        pool_w=pool_w[0], pool_scale=pool_scale, sinks=attn_sinks[0],
        scw=_cols_from_shards(g_scw), scb=g_scb.reshape(1, SSM_CONV_DIM), snorm=g_sn.reshape(1, SSM_D_INNER),
        dt_bias=jnp.pad(ssm_dt_bias, ((0, 0), (0, LANES - SSM_HEADS))), a_log=jnp.pad(ssm_A_log, ((0, 0), (0, LANES - SSM_HEADS))),
        d_exp=jnp.repeat(ssm_D, SSM_D_INNER // SSM_HEADS, axis=1),
        fcb=[jnp.concatenate([ffn_conv_b[i:i + 1, p * FFN_TC:(p + 1) * FFN_TC] for p in _PERM], axis=1) for i in range(2)],
    )

    def fetch(group, after):
        owns, lands = _spread_wait(started[group], False, after, f"gather_wait_{group}")
        a, b = [own_slot(g, own) for g, own in zip(lands, owns)]
        if group == "ssm":
            wsi = a.reshape(SSM_IN_DIM, D_MODEL)
            zx = SSM_D_INNER + SSM_CONV_DIM
            return dict(wzT=wsi[:SSM_D_INNER], wxbcT=wsi[SSM_D_INNER:zx],
                        wdtT=jnp.pad(wsi[zx:], ((0, LANES - SSM_HEADS), (0, 0))), wso=b.reshape(SSM_D_INNER, D_MODEL))
        i = int(group[-1])
        return dict(wup=jnp.concatenate([a[p] for p in _PERM], axis=1), wdn=b.reshape(D_FF, D_MODEL), fcw=fcw[i])

    cos, sin_s = _rope_tables(positions)
    sent = {}

    def send(group, grads):
        res, tok = _spread_start([grads], True, jnp.zeros((SUBLANES, LANES), F32), f"grad_start_{group}")
        sent[group] = res[0]
        return tok

    loss_row, grad_x, big, small = _local_step(x[0], cos, sin_s, loss_target[0], P, fetch, token, send)

    kidx = kchip.astype(jnp.int32).reshape(1)
    group_names = dict(ffn1=["ffn_w_up1", "ffn_w_down1"], ssm=["ssm_w_in", "ssm_w_out"], ffn0=["ffn_w_up0", "ffn_w_down0"],
                       mix=["mix_w_in", "mix_w_out"])
    names, mine = [], []
    for group, started_g in sent.items():
        grads, lands = _spread_wait(started_g, True, grad_x, f"grad_wait_{group}")
        for nm, g, land in zip(group_names[group], grads, lands):
            names.append(nm)
            mine.append(_chip_sum(g, land, kidx, f"chip_sum_{nm}"))
    theirs = _sibling_exchange(mine, "sibling_exchange")
    red = {nm: (a, b) for nm, a, b in zip(names, mine, theirs)}

    out = {}

    def big_update(pname, gparts, transposed=False):
        w = W[pname]
        lw = len(gparts)
        shp = w.shape
        rr, cc = gparts[0][0].shape
        fix = (lambda t: tr(t)[None]) if transposed else (lambda t: t.reshape(lw, rr, cc))
        res = _adamw(fix(w), fix(Mo[pname]), fix(Vo[pname]), gparts, f"adamw_{pname}")
        out[pname] = tuple((tr(r)[None] if transposed else r.reshape(shp)) for r in res)

    big_update("mix_w_in", [red["mix_w_in"]], transposed=True)
    big_update("mix_w_out", [red["mix_w_out"]])
    big_update("ssm_w_in", [red["ssm_w_in"]], transposed=True)
    big_update("ssm_w_out", [red["ssm_w_out"]])
    big_update("ffn_w_up", [red["ffn_w_up0"], red["ffn_w_up1"]])
    big_update("ffn_w_down", [red["ffn_w_down0"], red["ffn_w_down1"]])

    small_names = ["norm_mix", "norm_ffn", "norm_final", "pool_w", "pool_scale", "attn_sinks", "ssm_dt_bias", "ssm_A_log",
                   "ssm_D", "ffn_conv_b", "ssm_conv_w", "ssm_conv_b", "ssm_norm", "ffn_conv_w"]

    def as2d(t):
        if t.ndim == 1:
            return t.reshape(1, -1)
        return t.reshape(-1, t.shape[-1])

    wmv = [(as2d(W[nm]), as2d(Mo[nm]), as2d(Vo[nm])) for nm in small_names]
    summed = _small_allreduce([small[nm] for nm in small_names], [t[0].shape for t in wmv], loss_row, "small_allreduce")
    res = _small_adamw(summed[:-1], wmv, "small_adamw")
    for a, nm in enumerate(small_names):
        out[nm] = tuple(r.reshape(W[nm].shape) for r in res[4 * a:4 * a + 4])
    loss = summed[-1][0, 0]

    order = ["norm_mix", "norm_ffn", "norm_final", "mix_w_in", "pool_w", "pool_scale", "attn_sinks", "mix_w_out", "ssm_w_in",
             "ssm_conv_w", "ssm_conv_b", "ssm_dt_bias", "ssm_A_log", "ssm_D", "ssm_norm", "ssm_w_out", "ffn_w_up", "ffn_conv_w",
             "ffn_conv_b", "ffn_w_down"]
    return (loss, grad_x.reshape(x.shape), *[out[nm][0] for nm in order], *[out[nm][1] for nm in order],
            *[out[nm][2] for nm in order], *[out[nm][3] for nm in order])
```

```python
import functools

import jax
import jax.numpy as jnp
from jax import lax
from jax.experimental import pallas as pl
from jax.experimental.pallas import tpu as pltpu

F32 = jnp.float32
BF16 = jnp.bfloat16
MXU = BF16
HI = lax.Precision.HIGHEST

D_MODEL = 1024
POOL_WINDOWS = (2, 4, 8, 16)
POOL_DIM = 512
POOL_GROUP = 128
HEAD_DIM = 64
N_HEADS = 8
N_KV_HEADS = 2
GQ = 4
Q_DIM = 512
KV_DIM = 128
BLOCK = 128
ROPE_THETA = 10000.0
MIX_IN_DIM = 1280
SSM_D_INNER = 2048
SSM_HEADS = 32
SSM_GROUPS = 8
SSM_STATE = 128
SSM_CONV = 4
SSM_CHUNK = 128
SSM_CONV_DIM = 4096
SSM_IN_DIM = 6176
D_FF = 2816
FFN_CONV = 3
NORM_EPS = 1e-6
SSM_NORM_EPS = 1e-5
ADAM_LR = 0.001
ADAM_B1 = 0.9
ADAM_B2 = 0.999
ADAM_EPS = 1e-08
ADAM_WD = 0.01
ADAM_STEP = 10

N_CHIPS = 4
N_DEV = 8
LANES = 128
SUBLANES = 8
V7X_VMEM_LIMIT = 56 * 1024 * 1024
NEG = -1e30
MESH = pl.DeviceIdType.MESH


def _cp(*sem):
    return pltpu.CompilerParams(dimension_semantics=sem if sem else None, vmem_limit_bytes=V7X_VMEM_LIMIT)


def _sds(shape, dtype=F32):
    return jax.ShapeDtypeStruct(tuple(shape), dtype)


def _iota(shape, dim):
    return lax.broadcasted_iota(jnp.int32, shape, dim)


def _silu(x):
    return x * (1.0 / (1.0 + jnp.exp(-x)))


def _dsilu(x):
    s = 1.0 / (1.0 + jnp.exp(-x))
    return s * (1.0 + x * (1.0 - s))


def _mm(a, b, *, ta=False, tb=False, tm, tn, tk, res=None, out_dtype=F32, out_shard_perm=None, out_into=None, norm_w=None,
        norm_bwd=None, name):
    M, K = (a.shape[1], a.shape[0]) if ta else a.shape
    N = b.shape[0] if tb else b.shape[1]
    tm, tn, tk = min(tm, M), min(tn, N), min(tk, K)
    gm, gn, gk = M // tm, N // tn, K // tk
    assert gm * tm == M and gn * tn == N and gk * tk == K, (name, M, N, K, tm, tn, tk)
    a_spec = pl.BlockSpec((tk, tm), lambda i, j, k: (k, i)) if ta else pl.BlockSpec((tm, tk), lambda i, j, k: (i, k))
    b_spec = pl.BlockSpec((tn, tk), lambda i, j, k: (j, k)) if tb else pl.BlockSpec((tk, tn), lambda i, j, k: (k, j))
    dims = (((0 if ta else 1,), (1 if tb else 0,)), ((), ()))
    has_res = res is not None
    has_nw = norm_w is not None
    has_nb = norm_bwd is not None
    has_tok = has_nb and norm_bwd[3] is not None
    assert not (has_nw or has_nb) or (gn == 1 and out_shard_perm is None)
    n_extra = has_res + has_nw + (3 + has_tok if has_nb else 0)

    def body(*refs):
        a_ref, b_ref = refs[0], refs[1]
        extra = list(refs[2:2 + n_extra])
        outs = refs[len(args):]
        r_ref = extra.pop(0) if has_res else None
        nw_ref = extra.pop(0) if has_nw else None
        nb_refs = extra if has_nb else None

        def dot():
            return lax.dot_general(a_ref[...].astype(MXU), b_ref[...].astype(MXU), dims, preferred_element_type=F32)

        def finish(r):
            if has_res:
                r = r + r_ref[...]
            if has_nb:
                xv = nb_refs[0][...]
                rs = lax.rsqrt(jnp.mean(xv * xv, axis=-1, keepdims=True) + NORM_EPS)
                xh = xv * rs
                g = r * nb_refs[1][...]
                dr = nb_refs[2][...] + nb_refs[3][0:1, 0:1] if has_tok else nb_refs[2][...]
                outs[0][...] = dr + rs * (g - xh * jnp.mean(g * xh, axis=-1, keepdims=True))
                part = jnp.sum(r * xh, axis=0, keepdims=True)
                i = pl.program_id(0)

                @pl.when(i == 0)
                def _():
                    outs[1][...] = part

                @pl.when(i > 0)
                def _():
                    outs[1][...] += part
                return
            outs[0][...] = r.astype(out_dtype)
            if has_nw:
                rs = lax.rsqrt(jnp.mean(r * r, axis=-1, keepdims=True) + NORM_EPS)
                outs[1][...] = (r * rs * nw_ref[...]).astype(outs[1].dtype)

        if gk == 1:
            finish(dot())
        else:
            acc = refs[-1]
            k = pl.program_id(2)

            @pl.when(k == 0)
            def _():
                acc[...] = dot()

            if gk > 2:
                @pl.when(jnp.logical_and(k > 0, k < gk - 1))
                def _():
                    acc[...] += dot()

            @pl.when(k == gk - 1)
            def _():
                finish(acc[...] + dot())

    tile = pl.BlockSpec((tm, tn), lambda i, j, k: (i, j))
    row = pl.BlockSpec((1, tn), lambda i, j, k: (0, j))
    in_specs = [a_spec, b_spec]
    args = [a, b]
    if has_res:
        in_specs.append(tile)
        args.append(res)
    if has_nw:
        in_specs.append(row)
        args.append(norm_w.reshape(1, N))
    if has_nb:
        in_specs += [tile, row, tile]
        args += [norm_bwd[0], norm_bwd[1].reshape(1, N), norm_bwd[2]]
        if has_tok:
            in_specs.append(pl.BlockSpec((SUBLANES, LANES), lambda i, j, k: (0, 0)))
            args.append(norm_bwd[3])
    alias = {}
    if out_into is not None:
        buf, rows, off = out_into
        out_spec = pl.BlockSpec((tm, tn), lambda i, j, k: (i + off, j))
        out_shape = _sds((rows, N), out_dtype)
        if buf is not None:
            alias = {len(args): 0}
            in_specs.append(pl.BlockSpec(memory_space=pl.ANY))
            args.append(buf)
    elif out_shard_perm is None:
        out_spec = tile
        out_shape = _sds((M, N), out_dtype)
    else:
        assert gn == len(out_shard_perm) == 4 and tuple(out_shard_perm) == (0, 2, 1, 3)
        out_spec = pl.BlockSpec((None, tm, tn), lambda i, j, k: ((j % 2) * 2 + j // 2, i, 0))
        out_shape = _sds((gn, M, tn), out_dtype)
    sem = ("parallel", "parallel", "arbitrary")
    if has_nw:
        out_spec, out_shape = [out_spec, tile], [out_shape, _sds((M, N), MXU)]
    if has_nb:
        out_spec, out_shape = [tile, row], [_sds((M, N)), _sds((1, N))]
        sem = ("arbitrary", "arbitrary", "arbitrary")
    return pl.pallas_call(
        body, grid=(gm, gn, gk), in_specs=in_specs, out_specs=out_spec, out_shape=out_shape,
        scratch_shapes=[pltpu.VMEM((tm, tn), F32)] if gk > 1 else [], input_output_aliases=alias,
        compiler_params=_cp(*sem), name=name)(*args)


def _put_rows(buf, src, rows, at, name):
    assert at % rows == 0 and src.shape[1] == buf.shape[1] and src.dtype == buf.dtype
    C = buf.shape[1]

    def body(s_ref, b_ref, o_ref):
        o_ref[...] = s_ref[...]

    return pl.pallas_call(
        body, grid=(1,), in_specs=[pl.BlockSpec((rows, C), lambda i: (0, 0)), pl.BlockSpec(memory_space=pl.ANY)],
        out_specs=pl.BlockSpec((rows, C), lambda i: (at // rows, 0)), out_shape=_sds(buf.shape, buf.dtype),
        input_output_aliases={1: 0}, compiler_params=_cp("arbitrary"), name=name)(src, buf)


def _rmsnorm_fwd(x, w, name, token=None):
    T, D = x.shape
    tm = min(T, 512)
    has_token = token is not None

    def body(*refs):
        x_ref, w_ref, o_ref = refs[0], refs[1], refs[-1]
        xv = x_ref[...]
        if has_token:
            xv = xv + refs[2][0:1, 0:1]
        r = lax.rsqrt(jnp.mean(xv * xv, axis=-1, keepdims=True) + NORM_EPS)
        o_ref[...] = (xv * r * w_ref[...]).astype(o_ref.dtype)

    in_specs = [pl.BlockSpec((tm, D), lambda i: (i, 0)), pl.BlockSpec((1, D), lambda i: (0, 0))]
    args = [x, w.reshape(1, D)]
    if has_token:
        in_specs.append(pl.BlockSpec((SUBLANES, LANES), lambda i: (0, 0)))
        args.append(token)
    return pl.pallas_call(
        body, grid=(T // tm,), in_specs=in_specs,
        out_specs=pl.BlockSpec((tm, D), lambda i: (i, 0)), out_shape=_sds((T, D), MXU),
        compiler_params=_cp("parallel"), name=name)(*args)


def _loss_head(x, w, target, name):
    T, D = x.shape
    tm = min(T, 512)

    def body(x_ref, w_ref, t_ref, loss_ref, dx_ref, dw_ref):
        xv = x_ref[...]
        r = lax.rsqrt(jnp.mean(xv * xv, axis=-1, keepdims=True) + NORM_EPS)
        xh = xv * r
        wv = w_ref[...]
        e = xh * wv - t_ref[...]
        lpart = 0.5 * jnp.sum(jnp.mean(e * e, axis=-1, keepdims=True), axis=0, keepdims=True)
        dy = e * (1.0 / D)
        g = dy * wv
        dx_ref[...] = r * (g - xh * jnp.mean(g * xh, axis=-1, keepdims=True))
        part = jnp.sum(dy * xh, axis=0, keepdims=True)
        lrow = jnp.broadcast_to(lpart, (1, LANES))

        @pl.when(pl.program_id(0) == 0)
        def _():
            dw_ref[...] = part
            loss_ref[...] = lrow

        @pl.when(pl.program_id(0) > 0)
        def _():
            dw_ref[...] += part
            loss_ref[...] += lrow

    row = pl.BlockSpec((tm, D), lambda i: (i, 0))
    vec = pl.BlockSpec((1, D), lambda i: (0, 0))
    return pl.pallas_call(
        body, grid=(T // tm,), in_specs=[row, vec, row],
        out_specs=[pl.BlockSpec((1, LANES), lambda i: (0, 0)), row, vec],
        out_shape=[_sds((1, LANES)), _sds((T, D)), _sds((1, D))],
        compiler_params=_cp("arbitrary"), name=name)(x, w.reshape(1, D), target)


def _shift_down(cur, prev8, s):
    if s == 0:
        return cur
    tm = cur.shape[0]
    rc = pltpu.roll(cur, s, 0)
    top = jnp.where(_iota((SUBLANES, cur.shape[1]), 0) < s, pltpu.roll(prev8, s, 0), rc[:SUBLANES])
    return jnp.concatenate([top, rc[SUBLANES:]], axis=0) if tm > SUBLANES else top


def _shift_up(cur, next8, s):
    if s == 0:
        return cur
    tm = cur.shape[0]
    rc = pltpu.roll(cur, tm - s, 0)
    bot = jnp.where(_iota((SUBLANES, cur.shape[1]), 0) >= SUBLANES - s, pltpu.roll(next8, SUBLANES - s, 0), rc[tm - SUBLANES:])
    return jnp.concatenate([rc[:tm - SUBLANES], bot], axis=0) if tm > SUBLANES else bot


def _conv_rows(cur, prev8, w, b, K):
    acc = cur * w[K - 1:K, :] + b
    for s in range(1, K):
        acc = acc + _shift_down(cur, prev8, s) * w[K - 1 - s:K - s, :]
    return acc


FFN_TC = 1408
HALO16 = 2 * SUBLANES


def _ffn_mid_fwd(hid, cw, cb, name):
    T = hid.shape[0]
    tm = min(T, 256)
    nt, nj = T // tm, D_FF // FFN_TC
    K = FFN_CONV

    q = tm // HALO16

    def body(h_ref, hp_ref, w_ref, b_ref, o_ref, hc_ref):
        i = pl.program_id(0)
        cur = h_ref[...].astype(F32)
        prev8 = jnp.where(i > 0, hp_ref[...].astype(F32)[HALO16 - SUBLANES:], 0.0)
        hc = _conv_rows(cur, prev8, w_ref[...], b_ref[...], K)
        hc_ref[...] = hc
        o_ref[...] = (_silu(hc[:, FFN_TC:]) * hc[:, :FFN_TC]).astype(o_ref.dtype)

    return pl.pallas_call(
        body, grid=(nt, nj),
        in_specs=[pl.BlockSpec((tm, 2 * FFN_TC), lambda i, j: (i, j)),
                  pl.BlockSpec((HALO16, 2 * FFN_TC), lambda i, j: (jnp.maximum(i * q - 1, 0), j)),
                  pl.BlockSpec((K, 2 * FFN_TC), lambda i, j: (0, j)), pl.BlockSpec((1, 2 * FFN_TC), lambda i, j: (0, j))],
        out_specs=[pl.BlockSpec((tm, FFN_TC), lambda i, j: (i, j)), pl.BlockSpec((tm, 2 * FFN_TC), lambda i, j: (i, j))],
        out_shape=[_sds((T, D_FF), MXU), _sds((T, 2 * D_FF))],
        compiler_params=_cp("parallel", "parallel"), name=name)(hid, hid, cw, cb)


def _ffn_mid_bwd(hid, hc, cw, da, name):
    T = hid.shape[0]
    tm = min(T, 256)
    nt, nj = T // tm, D_FF // FFN_TC
    K = FFN_CONV
    W2 = 2 * FFN_TC

    def body(h_ref, c_ref, cn_ref, da_ref, dan_ref, w_ref, dh_ref, dw_ref, db_ref):
        i = pl.program_id(1)
        w = w_ref[...]
        cur = h_ref[...].astype(F32)
        last = i == nt - 1

        def dpre(hcv, dav):
            u, g = hcv[:, :FFN_TC], hcv[:, FFN_TC:]
            return jnp.concatenate([dav * _silu(g), dav * u * _dsilu(g)], axis=1)

        d_cur = dpre(c_ref[...], da_ref[...])
        d_nxt = jnp.where(last, 0.0, dpre(cn_ref[...], dan_ref[...]))
        ups = [d_cur] + [_shift_up(d_cur, d_nxt, s) for s in range(1, K)]
        dh = ups[0] * w[K - 1:K, :]
        for s in range(1, K):
            dh = dh + ups[s] * w[K - 1 - s:K - s, :]
        dh_ref[...] = dh.astype(dh_ref.dtype)
        dwp = jnp.concatenate([jnp.sum(ups[K - 1 - k] * cur, axis=0, keepdims=True) for k in range(K)], axis=0)
        dbp = jnp.sum(d_cur, axis=0, keepdims=True)

        @pl.when(i == 0)
        def _():
            dw_ref[...] = dwp
            db_ref[...] = dbp

        @pl.when(i > 0)
        def _():
            dw_ref[...] += dwp
            db_ref[...] += dbp

    q = tm // SUBLANES
    blk = pl.BlockSpec((tm, W2), lambda j, i: (i, j))
    nxt = pl.BlockSpec((SUBLANES, W2), lambda j, i: (jnp.minimum((i + 1) * q, nt * q - 1), j))
    dab = pl.BlockSpec((tm, FFN_TC), lambda j, i: (i, j))
    dan = pl.BlockSpec((SUBLANES, FFN_TC), lambda j, i: (jnp.minimum((i + 1) * q, nt * q - 1), j))
    return pl.pallas_call(
        body, grid=(nj, nt),
        in_specs=[blk, blk, nxt, dab, dan, pl.BlockSpec((K, W2), lambda j, i: (0, j))],
        out_specs=[blk, pl.BlockSpec((K, W2), lambda j, i: (0, j)), pl.BlockSpec((1, W2), lambda j, i: (0, j))],
        out_shape=[_sds((T, 2 * D_FF), MXU), _sds((K, 2 * D_FF)), _sds((1, 2 * D_FF))],
        compiler_params=_cp("parallel", "arbitrary"), name=name)(hid, hc, hc, da, da, cw)


def _rope(t, cos, sin_s, inverse=False):
    n = t.shape[1] // LANES
    c = jnp.concatenate([cos] * n, axis=1) if n > 1 else cos
    s = jnp.concatenate([sin_s] * n, axis=1) if n > 1 else sin_s
    a = pltpu.roll(t, HEAD_DIM // 2, 1)
    b = pltpu.roll(t, t.shape[1] - HEAD_DIM // 2, 1)
    first = (_iota(t.shape, 1) % HEAD_DIM) < HEAD_DIM // 2
    rot = jnp.where(first, b, a) * s
    return t * c - rot if inverse else t * c + rot


def _stack_heads(t, g):
    return jnp.concatenate([t[:, (GQ * g + r) * HEAD_DIM:(GQ * g + r + 1) * HEAD_DIM] for r in range(GQ)], axis=0)


def _stack_cols(t, g):
    return jnp.concatenate([t[:, GQ * g + r:GQ * g + r + 1] for r in range(GQ)], axis=0)


def _pool_sums(prev, cur, w):
    s = jnp.concatenate([prev, cur], axis=0)
    sh = 1
    while sh < w:
        s = s + pltpu.roll(s, sh, 0)
        sh *= 2
    return s[BLOCK:]


def _nt(a, b):
    return lax.dot_general(a.astype(MXU), b.astype(MXU), (((1,), (1,)), ((), ())), preferred_element_type=F32)


def _tn(a, b):
    return lax.dot_general(a.astype(MXU), b.astype(MXU), (((0,), (0,)), ((), ())), preferred_element_type=F32)


def _nn(a, b):
    return jnp.dot(a.astype(MXU), b.astype(MXU), preferred_element_type=F32)


def _mixcore_fwd(proj, cos, sin_s, pool_w, pool_scale, sinks, name):
    T = proj.shape[0]
    nb = T // BLOCK
    scale = HEAD_DIM ** -0.5

    def body(p_ref, pp_ref, c_ref, s_ref, cp_ref, sp_ref, pw_ref, ps_ref, sk_ref, cat_ref, at_ref, lse_ref):
        i = pl.program_id(0)
        has_prev = i > 0
        cur = p_ref[...]
        prv = jnp.where(has_prev, pp_ref[...], 0.0)
        tpos = (i * BLOCK + _iota((BLOCK, 1), 0) + 1).astype(F32)
        for g, w in enumerate(POOL_WINDOWS):
            sl = slice(g * POOL_GROUP, (g + 1) * POOL_GROUP)
            pooled = _pool_sums(prv[:, sl], cur[:, sl], w) / jnp.minimum(tpos, float(w)) - cur[:, sl]
            cat_ref[:, sl] = (_nn(pooled, pw_ref[g]) * ps_ref[:, sl]).astype(cat_ref.dtype)
        q = _rope(cur[:, POOL_DIM:POOL_DIM + Q_DIM], c_ref[...], s_ref[...])
        kc = _rope(cur[:, POOL_DIM + Q_DIM:POOL_DIM + Q_DIM + KV_DIM], c_ref[...], s_ref[...])
        kp = _rope(prv[:, POOL_DIM + Q_DIM:POOL_DIM + Q_DIM + KV_DIM], cp_ref[...], sp_ref[...])
        vc = cur[:, POOL_DIM + Q_DIM + KV_DIM:]
        vp = prv[:, POOL_DIM + Q_DIM + KV_DIM:]
        ri = _iota((GQ * BLOCK, BLOCK), 0) % BLOCK
        cj = _iota((GQ * BLOCK, BLOCK), 1)
        mc = cj <= ri
        mp = jnp.logical_and(cj > ri, has_prev)
        outs, lses = [], []
        for g in range(N_KV_HEADS):
            hs = slice(g * HEAD_DIM, (g + 1) * HEAD_DIM)
            qg = _stack_heads(q, g) * scale
            sc = jnp.where(mc, _nt(qg, kc[:, hs]), NEG)
            sp = jnp.where(mp, _nt(qg, kp[:, hs]), NEG)
            sink = jnp.concatenate([jnp.full((BLOCK, 1), sk_ref[GQ * g + r], F32) for r in range(GQ)], axis=0)
            m = jnp.maximum(jnp.maximum(jnp.max(sc, axis=1, keepdims=True), jnp.max(sp, axis=1, keepdims=True)), sink)
            pc = jnp.exp(sc - m)
            pp = jnp.exp(sp - m)
            den = jnp.sum(pc, axis=1, keepdims=True) + jnp.sum(pp, axis=1, keepdims=True) + jnp.exp(sink - m)
            o = (_nn(pc, vc[:, hs]) + _nn(pp, vp[:, hs])) / den
            lse = m + jnp.log(den)
            for r in range(GQ):
                outs.append(o[r * BLOCK:(r + 1) * BLOCK])
                lses.append(lse[r * BLOCK:(r + 1) * BLOCK])
        attn = jnp.concatenate(outs, axis=1)
        at_ref[...] = attn
        cat_ref[:, POOL_DIM:] = attn.astype(cat_ref.dtype)
        lane = _iota((BLOCK, LANES), 1)
        lrow = jnp.zeros((BLOCK, LANES), F32)
        for h in range(N_HEADS):
            lrow = jnp.where(lane == h, lses[h], lrow)
        lse_ref[...] = lrow

    cur = lambda w: pl.BlockSpec((BLOCK, w), lambda i: (i, 0))
    prv = lambda w: pl.BlockSpec((BLOCK, w), lambda i: (jnp.maximum(i - 1, 0), 0))
    return pl.pallas_call(
        body, grid=(nb,),
        in_specs=[cur(MIX_IN_DIM), prv(MIX_IN_DIM), cur(LANES), cur(LANES), prv(LANES), prv(LANES),
                  pl.BlockSpec((4, POOL_GROUP, POOL_GROUP), lambda i: (0, 0, 0)), pl.BlockSpec((1, POOL_DIM), lambda i: (0, 0)),
                  pl.BlockSpec(memory_space=pltpu.SMEM)],
        out_specs=[cur(2 * POOL_DIM), cur(Q_DIM), cur(LANES)],
        out_shape=[_sds((T, 2 * POOL_DIM), MXU), _sds((T, Q_DIM)), _sds((T, LANES))],
        compiler_params=_cp("parallel"), name=name)(proj, proj, cos, sin_s, cos, sin_s, pool_w, pool_scale, sinks)


def _mixcore_bwd(proj, cos, sin_s, pool_w, pool_scale, sinks, attn, lse, dcat, name):
    T = proj.shape[0]
    nb = T // BLOCK
    scale = HEAD_DIM ** -0.5
    QO, KO, VO = POOL_DIM, POOL_DIM + Q_DIM, POOL_DIM + Q_DIM + KV_DIM

    def body(p_ref, pp_ref, pn_ref, c_ref, s_ref, cp_ref, sp_ref, cn_ref, sn_ref, pw_ref, ps_ref, sk_ref,
             at_ref, atn_ref, l_ref, ln_ref, d_ref, dn_ref, dp_ref, dpw_ref, dps_ref, dsk_ref):
        i = pl.program_id(0)
        has_prev = i > 0
        has_next = i < nb - 1
        cur = p_ref[...]
        prv = jnp.where(has_prev, pp_ref[...], 0.0)
        d_cur = d_ref[...]
        d_nxt = jnp.where(has_next, dn_ref[...], 0.0)

        tpos = (i * BLOCK + _iota((BLOCK, 1), 0) + 1).astype(F32)
        tpos2 = (i * BLOCK + _iota((2 * BLOCK, 1), 0) + 1).astype(F32)
        ps = ps_ref[...]
        dps_parts, dpw_parts = [], []
        for g, w in enumerate(POOL_WINDOWS):
            sl = slice(g * POOL_GROUP, (g + 1) * POOL_GROUP)
            pooled = _pool_sums(prv[:, sl], cur[:, sl], w) / jnp.minimum(tpos, float(w)) - cur[:, sl]
            mixed = _nn(pooled, pw_ref[g])
            dps_parts.append(jnp.sum(d_cur[:, sl] * mixed, axis=0, keepdims=True))
            dm2 = jnp.concatenate([d_cur[:, sl], d_nxt[:, sl]], axis=0) * ps[:, sl]
            dpw_parts.append(_tn(pooled, dm2[:BLOCK]))
            dpool2 = _nt(dm2, pw_ref[g])
            e = dpool2 / jnp.minimum(tpos2, float(w))
            sh = 1
            while sh < w:
                e = e + pltpu.roll(e, 2 * BLOCK - sh, 0)
                sh *= 2
            dp_ref[:, sl] = (e[:BLOCK] - dpool2[:BLOCK]).astype(dp_ref.dtype)
        dpsp = jnp.concatenate(dps_parts, axis=1)

        nxt = pn_ref[...]
        q = _rope(cur[:, QO:KO], c_ref[...], s_ref[...])
        qn = _rope(nxt[:, QO:KO], cn_ref[...], sn_ref[...])
        kc = _rope(cur[:, KO:VO], c_ref[...], s_ref[...])
        kp = _rope(prv[:, KO:VO], cp_ref[...], sp_ref[...])
        vc, vp = cur[:, VO:], prv[:, VO:]
        do, don = d_cur[:, POOL_DIM:], d_nxt[:, POOL_DIM:]
        dl = do * at_ref[...]
        dln = don * atn_ref[...]
        lse, lsen = l_ref[...], ln_ref[...]
        ri = _iota((GQ * BLOCK, BLOCK), 0) % BLOCK
        cj = _iota((GQ * BLOCK, BLOCK), 1)
        mc = cj <= ri
        mp = jnp.logical_and(cj > ri, has_prev)
        mn = jnp.logical_and(cj > ri, has_next)
        dq_parts, dk_parts, dv_parts, dsk_vals = [], [], [], []
        for g in range(N_KV_HEADS):
            hs = slice(g * HEAD_DIM, (g + 1) * HEAD_DIM)
            qg, qng = _stack_heads(q, g) * scale, _stack_heads(qn, g) * scale
            dog, dong = _stack_heads(do, g), _stack_heads(don, g)
            delta = jnp.sum(_stack_heads(dl, g), axis=1, keepdims=True)
            deltan = jnp.sum(_stack_heads(dln, g), axis=1, keepdims=True)
            lg, lng = _stack_cols(lse, g), _stack_cols(lsen, g)
            pc = jnp.where(mc, jnp.exp(_nt(qg, kc[:, hs]) - lg), 0.0)
            pp = jnp.where(mp, jnp.exp(_nt(qg, kp[:, hs]) - lg), 0.0)
            pn = jnp.where(mn, jnp.exp(_nt(qng, kc[:, hs]) - lng), 0.0)
            dsc = pc * (_nt(dog, vc[:, hs]) - delta)
            dsp = pp * (_nt(dog, vp[:, hs]) - delta)
            dsn = pn * (_nt(dong, vc[:, hs]) - deltan)
            dqg = (_nn(dsc, kc[:, hs]) + _nn(dsp, kp[:, hs])) * scale
            dq_parts += [dqg[r * BLOCK:(r + 1) * BLOCK] for r in range(GQ)]
            dk_parts.append(_tn(dsc, qg) + _tn(dsn, qng))
            dv_parts.append(_tn(pc, dog) + _tn(pn, dong))
            sink = jnp.concatenate([jnp.full((BLOCK, 1), sk_ref[GQ * g + r], F32) for r in range(GQ)], axis=0)
            dsk = -jnp.exp(sink - lg) * delta
            dsk_vals += [jnp.sum(dsk[r * BLOCK:(r + 1) * BLOCK], axis=0, keepdims=True) for r in range(GQ)]
        dq = _rope(jnp.concatenate(dq_parts, axis=1), c_ref[...], s_ref[...], inverse=True)
        dk = _rope(jnp.concatenate(dk_parts, axis=1), c_ref[...], s_ref[...], inverse=True)
        dp_ref[:, QO:KO] = dq.astype(dp_ref.dtype)
        dp_ref[:, KO:VO] = dk.astype(dp_ref.dtype)
        dp_ref[:, VO:] = jnp.concatenate(dv_parts, axis=1).astype(dp_ref.dtype)
        lane = _iota((1, LANES), 1)
        dskp = jnp.zeros((1, LANES), F32)
        for h in range(N_HEADS):
            dskp = jnp.where(lane == h, dsk_vals[h], dskp)

        @pl.when(i == 0)
        def _():
            dps_ref[...] = dpsp
            dsk_ref[...] = dskp
            for g in range(4):
                dpw_ref[g] = dpw_parts[g]

        @pl.when(i > 0)
        def _():
            dps_ref[...] += dpsp
            dsk_ref[...] += dskp
            for g in range(4):
                dpw_ref[g] += dpw_parts[g]

    cur = lambda w: pl.BlockSpec((BLOCK, w), lambda i: (i, 0))
    prv = lambda w: pl.BlockSpec((BLOCK, w), lambda i: (jnp.maximum(i - 1, 0), 0))
    nxt = lambda w: pl.BlockSpec((BLOCK, w), lambda i: (jnp.minimum(i + 1, nb - 1), 0))
    return pl.pallas_call(
        body, grid=(nb,),
        in_specs=[cur(MIX_IN_DIM), prv(MIX_IN_DIM), nxt(MIX_IN_DIM),
                  cur(LANES), cur(LANES), prv(LANES), prv(LANES), nxt(LANES), nxt(LANES),
                  pl.BlockSpec((4, POOL_GROUP, POOL_GROUP), lambda i: (0, 0, 0)), pl.BlockSpec((1, POOL_DIM), lambda i: (0, 0)),
                  pl.BlockSpec(memory_space=pltpu.SMEM),
                  cur(Q_DIM), nxt(Q_DIM), cur(LANES), nxt(LANES), cur(2 * POOL_DIM), nxt(2 * POOL_DIM)],
        out_specs=[cur(MIX_IN_DIM), pl.BlockSpec((4, POOL_GROUP, POOL_GROUP), lambda i: (0, 0, 0)),
                   pl.BlockSpec((1, POOL_DIM), lambda i: (0, 0)), pl.BlockSpec((1, LANES), lambda i: (0, 0))],
        out_shape=[_sds((T, MIX_IN_DIM), MXU), _sds((4, POOL_GROUP, POOL_GROUP)), _sds((1, POOL_DIM)), _sds((1, LANES))],
        compiler_params=_cp("arbitrary"), name=name)(
            proj, proj, proj, cos, sin_s, cos, sin_s, cos, sin_s, pool_w, pool_scale, sinks, attn, attn, lse, lse, dcat, dcat)


SSM_TC = 512
GROUP_W = SSM_D_INNER // SSM_GROUPS


def _ssm_pre_fwd(xbc, cw, cb, name):
    T = xbc.shape[0]
    tm = min(T, 1024)
    K = SSM_CONV
    q = tm // SUBLANES

    def body(x_ref, xp_ref, w_ref, b_ref, o_ref, pre_ref):
        prev8 = jnp.where(pl.program_id(0) > 0, xp_ref[...], 0.0)
        pre = _conv_rows(x_ref[...], prev8, w_ref[...], b_ref[...], K)
        pre_ref[...] = pre
        o_ref[...] = _silu(pre)

    tc = 512
    blk = pl.BlockSpec((tm, tc), lambda i, j: (i, j))
    return pl.pallas_call(
        body, grid=(T // tm, SSM_CONV_DIM // tc),
        in_specs=[blk, pl.BlockSpec((SUBLANES, tc), lambda i, j: (jnp.maximum(i * q - 1, 0), j)),
                  pl.BlockSpec((K, tc), lambda i, j: (0, j)), pl.BlockSpec((1, tc), lambda i, j: (0, j))],
        out_specs=[blk, blk], out_shape=[_sds((T, SSM_CONV_DIM)), _sds((T, SSM_CONV_DIM))],
        compiler_params=_cp("parallel", "parallel"), name=name)(xbc, xbc, cw, cb)


def _ssm_pre_bwd(xbc, pre, cw, dact, name):
    T = xbc.shape[0]
    tm = min(T, 512)
    nt = T // tm
    K = SSM_CONV
    q = tm // SUBLANES
    tc = SSM_TC

    def body(x_ref, p_ref, pn_ref, d_ref, dn_ref, w_ref, dx_ref, dw_ref, db_ref):
        i = pl.program_id(1)
        w = w_ref[...]
        cur = x_ref[...]
        d_cur = d_ref[...] * _dsilu(p_ref[...])
        d_nxt = jnp.where(i == nt - 1, 0.0, dn_ref[...] * _dsilu(pn_ref[...]))
        ups = [d_cur] + [_shift_up(d_cur, d_nxt, s) for s in range(1, K)]
        dx = ups[0] * w[K - 1:K, :]
        for s in range(1, K):
            dx = dx + ups[s] * w[K - 1 - s:K - s, :]
        dx_ref[...] = dx.astype(dx_ref.dtype)
        dwp = jnp.concatenate([jnp.sum(ups[K - 1 - k] * cur, axis=0, keepdims=True) for k in range(K)], axis=0)
        dbp = jnp.sum(d_cur, axis=0, keepdims=True)

        @pl.when(i == 0)
        def _():
            dw_ref[...] = dwp
            db_ref[...] = dbp

        @pl.when(i > 0)
        def _():
            dw_ref[...] += dwp
            db_ref[...] += dbp

    nxt_row = lambda i: jnp.minimum((i + 1) * q, nt * q - 1)
    return pl.pallas_call(
        body, grid=(SSM_CONV_DIM // tc, nt),
        in_specs=[pl.BlockSpec((tm, tc), lambda j, i: (i, j)),
                  pl.BlockSpec((tm, tc), lambda j, i: (i, j)),
                  pl.BlockSpec((SUBLANES, tc), lambda j, i: (nxt_row(i), j)),
                  pl.BlockSpec((tm, tc), lambda j, i: (i, j)),
                  pl.BlockSpec((SUBLANES, tc), lambda j, i: (nxt_row(i), j)),
                  pl.BlockSpec((K, tc), lambda j, i: (0, j))],
        out_specs=[pl.BlockSpec((tm, tc), lambda j, i: (i, j)), pl.BlockSpec((K, tc), lambda j, i: (0, j)),
                   pl.BlockSpec((1, tc), lambda j, i: (0, j))],
        out_shape=[_sds((T, SSM_CONV_DIM), MXU), _sds((K, SSM_CONV_DIM)), _sds((1, SSM_CONV_DIM))],
        compiler_params=_cp("parallel", "arbitrary"), name=name)(xbc, pre, pre, dact, dact, cw)


def _dot_hi(a, b):
    return jnp.dot(a, b, precision=HI, preferred_element_type=F32)


def _ssd_common(dtraw, bias, alog):
    L = SSM_CHUNK
    xb = dtraw + bias
    dt = jnp.maximum(xb, 0.0) + jnp.log1p(jnp.exp(-jnp.abs(xb)))
    A = -jnp.exp(alog)
    tril = (_iota((L, L), 1) <= _iota((L, L), 0)).astype(F32)
    acs = _dot_hi(tril, dt * A)
    return xb, dt, A, tril, acs


def _head_selectors():
    es = (_iota((LANES, SSM_D_INNER), 0) == _iota((LANES, SSM_D_INNER), 1) // HEAD_DIM).astype(BF16)
    est = (_iota((SSM_D_INNER, LANES), 1) == _iota((SSM_D_INNER, LANES), 0) // HEAD_DIM).astype(BF16)
    return es, est


def _dot_sel(v, sel):
    hi = v.astype(BF16)
    r1 = v - hi.astype(F32)
    mid = r1.astype(BF16)
    lo = (r1 - mid.astype(F32)).astype(BF16)
    d = lambda a: jnp.dot(a, sel, preferred_element_type=F32)
    return (d(hi) + d(mid)) + d(lo)


def _expand_heads(v, es):
    return _dot_sel(v, es)


def _reduce_heads(q, est):
    return _dot_sel(q, est)


def _per_state_row(v, g):
    return jnp.concatenate([jnp.broadcast_to(v[:, GQ * g + r:GQ * g + r + 1], (HEAD_DIM, 1)) for r in range(GQ)], axis=0)


def _ssd_fwd(xact, dtraw, dt_bias, a_log, name):
    T = xact.shape[0]
    nc = T // SSM_CHUNK
    L = SSM_CHUNK
    BO, CO = SSM_D_INNER, SSM_D_INNER + SSM_GROUPS * SSM_STATE

    def body(x_ref, dt_ref, bias_ref, al_ref, es_ref, y_ref, st_ref, state):
        @pl.when(pl.program_id(0) == 0)
        def _():
            state[...] = jnp.zeros(state.shape, F32)

        _, dt, A, tril, acs = _ssd_common(dt_ref[...], bias_ref[...], al_ref[...])
        acsT = acs.T
        last = acs[L - 1:L, :]
        cd = jnp.exp(last)
        es = es_ref[...]
        dtX = _expand_heads(dt, es)
        EX = _expand_heads(jnp.exp(acs), es)
        decX = _expand_heads(jnp.exp(last - acs), es)
        for g in range(SSM_GROUPS):
            gs = slice(g * GROUP_W, (g + 1) * GROUP_W)
            B = x_ref[:, BO + g * SSM_STATE:BO + (g + 1) * SSM_STATE]
            C = x_ref[:, CO + g * SSM_STATE:CO + (g + 1) * SSM_STATE]
            X = x_ref[:, gs] * dtX[:, gs]
            CB = _nt(C, B)
            yd = []
            for r in range(GQ):
                h = GQ * g + r
                Lm = jnp.exp(jnp.where(tril > 0, acs[:, h:h + 1] - acsT[h:h + 1, :], NEG))
                yd.append(_nn(CB * Lm, X[:, r * HEAD_DIM:(r + 1) * HEAD_DIM]))
            S = state[g]
            st_ref[g] = S
            y_ref[:, gs] = jnp.concatenate(yd, axis=1) + _nt(C, S) * EX[:, gs]
            state[g] = S * _per_state_row(cd, g) + _tn(X * decX[:, gs], B)

    es, _ = _head_selectors()
    return pl.pallas_call(
        body, grid=(nc,),
        in_specs=[pl.BlockSpec((L, SSM_CONV_DIM), lambda c: (c, 0)), pl.BlockSpec((L, LANES), lambda c: (c, 0)),
                  pl.BlockSpec((1, LANES), lambda c: (0, 0)), pl.BlockSpec((1, LANES), lambda c: (0, 0)),
                  pl.BlockSpec((LANES, SSM_D_INNER), lambda c: (0, 0))],
        out_specs=[pl.BlockSpec((L, SSM_D_INNER), lambda c: (c, 0)),
                   pl.BlockSpec((None, SSM_GROUPS, GROUP_W, SSM_STATE), lambda c: (c, 0, 0, 0))],
        out_shape=[_sds((T, SSM_D_INNER)), _sds((nc, SSM_GROUPS, GROUP_W, SSM_STATE))],
        scratch_shapes=[pltpu.VMEM((SSM_GROUPS, GROUP_W, SSM_STATE), F32)],
        compiler_params=_cp("arbitrary"), name=name)(xact, dtraw, dt_bias, a_log, es)


def _ssd_bwd(xact, dtraw, dt_bias, a_log, d_skip, states, dy, name):
    T = xact.shape[0]
    nc = T // SSM_CHUNK
    L = SSM_CHUNK
    BO, CO = SSM_D_INNER, SSM_D_INNER + SSM_GROUPS * SSM_STATE

    def body(x_ref, dt_ref, bias_ref, al_ref, dsk_ref, es_ref, est_ref, st_ref, dy_ref,
             dxp_ref, ddt_ref, dbias_ref, dal_ref, dd_ref, dstate, qa, qx):
        cc = pl.program_id(0)

        @pl.when(cc == 0)
        def _():
            dstate[...] = jnp.zeros(dstate.shape, F32)

        xb, dt, A, tril, acs = _ssd_common(dt_ref[...], bias_ref[...], al_ref[...])
        acsT = acs.T
        last = acs[L - 1:L, :]
        cd = jnp.exp(last)
        es, est = es_ref[...], est_ref[...]
        dtX = _expand_heads(dt, es)
        EX = _expand_heads(jnp.exp(acs), es)
        decX = _expand_heads(jnp.exp(last - acs), es)
        lane1 = _iota((1, LANES), 1)
        lane = _iota((L, LANES), 1)
        sub = _iota((L, LANES), 0)
        ztot = jnp.zeros((1, LANES), F32)
        wrow = jnp.zeros((L, LANES), F32)
        wcolT = jnp.zeros((LANES, L), F32)
        rows_dec, rows_dd = [], []
        for g in range(SSM_GROUPS):
            gs = slice(g * GROUP_W, (g + 1) * GROUP_W)
            x = x_ref[:, gs]
            B = x_ref[:, BO + g * SSM_STATE:BO + (g + 1) * SSM_STATE]
            C = x_ref[:, CO + g * SSM_STATE:CO + (g + 1) * SSM_STATE]
            dY = dy_ref[:, gs]
            dtx, e_x, dec_x = dtX[:, gs], EX[:, gs], decX[:, gs]
            X = x * dtx
            CB = _nt(C, B)
            S = st_ref[g]
            dS_out = dstate[g]
            dcb_sum = jnp.zeros((L, L), F32)
            dxd = []
            for r in range(GQ):
                h = GQ * g + r
                hs = slice(r * HEAD_DIM, (r + 1) * HEAD_DIM)
                Lm = jnp.exp(jnp.where(tril > 0, acs[:, h:h + 1] - acsT[h:h + 1, :], NEG))
                M = CB * Lm
                dM = _nt(dY[:, hs], X[:, hs])
                dxd.append(_tn(M, dY[:, hs]))
                dcb_sum = dcb_sum + dM * Lm
                Wm = dM * M
                wrow = jnp.where(lane == h, jnp.sum(Wm, axis=1, keepdims=True), wrow)
                wcolT = jnp.where(sub == h, jnp.sum(Wm, axis=0, keepdims=True), wcolT)
            dXd = jnp.concatenate(dxd, axis=1)
            G = _nt(C, S)
            dG = dY * e_x
            dDX = _nt(B, dS_out)
            dX = dXd + dec_x * dDX
            t_dec = dDX * X * dec_x
            qa[:, gs] = dG * G - t_dec
            qx[:, gs] = dX * x
            rows_dec.append(jnp.sum(t_dec, axis=0, keepdims=True))
            rows_dd.append(jnp.sum(dY * x, axis=0, keepdims=True))
            zc = jnp.sum(dS_out * S, axis=1, keepdims=True)
            for r in range(GQ):
                ztot = jnp.where(lane1 == GQ * g + r, jnp.sum(zc[r * HEAD_DIM:(r + 1) * HEAD_DIM], axis=0, keepdims=True), ztot)
            dxp_ref[:, gs] = dX * dtx + dY * dsk_ref[:, gs]
            dxp_ref[:, BO + g * SSM_STATE:BO + (g + 1) * SSM_STATE] = _tn(dcb_sum, C) + _nn(X * dec_x, dS_out)
            dxp_ref[:, CO + g * SSM_STATE:CO + (g + 1) * SSM_STATE] = _nn(dcb_sum, B) + _nn(dG, S)
            dstate[g] = dS_out * _per_state_row(cd, g) + _tn(dG, C)
        rows = jnp.concatenate([jnp.concatenate(rows_dec, axis=1), jnp.concatenate(rows_dd, axis=1)]
                               + [jnp.zeros((SUBLANES - 2, SSM_D_INNER), F32)], axis=0)
        rsum = _reduce_heads(rows, est)
        dlast = rsum[0:1, :] + cd * ztot
        dacs = (wrow - wcolT.T) + _reduce_heads(qa[...], est) + jnp.where(sub == L - 1, dlast, 0.0)
        triu = (_iota((L, L), 0) <= _iota((L, L), 1)).astype(F32)
        da = _dot_hi(triu, dacs)
        ddtraw = (da * A + _reduce_heads(qx[...], est)) * (1.0 / (1.0 + jnp.exp(-xb)))
        ddt_ref[...] = ddtraw
        dal = jnp.sum(da * dt, axis=0, keepdims=True) * A
        ddp = rsum[1:2, :]
        dbp = jnp.sum(ddtraw, axis=0, keepdims=True)

        @pl.when(cc == 0)
        def _():
            dbias_ref[...] = dbp
            dal_ref[...] = dal
            dd_ref[...] = ddp

        @pl.when(cc > 0)
        def _():
            dbias_ref[...] += dbp
            dal_ref[...] += dal
            dd_ref[...] += ddp

    rc = lambda c: nc - 1 - c
    vec = pl.BlockSpec((1, LANES), lambda c: (0, 0))
    es, est = _head_selectors()
    return pl.pallas_call(
        body, grid=(nc,),
        in_specs=[pl.BlockSpec((L, SSM_CONV_DIM), lambda c: (rc(c), 0)), pl.BlockSpec((L, LANES), lambda c: (rc(c), 0)), vec, vec,
                  pl.BlockSpec((1, SSM_D_INNER), lambda c: (0, 0)),
                  pl.BlockSpec((LANES, SSM_D_INNER), lambda c: (0, 0)), pl.BlockSpec((SSM_D_INNER, LANES), lambda c: (0, 0)),
                  pl.BlockSpec((None, SSM_GROUPS, GROUP_W, SSM_STATE), lambda c: (rc(c), 0, 0, 0)),
                  pl.BlockSpec((L, SSM_D_INNER), lambda c: (rc(c), 0))],
        out_specs=[pl.BlockSpec((L, SSM_CONV_DIM), lambda c: (rc(c), 0)),
                   pl.BlockSpec((L, LANES), lambda c: (rc(c), 0)), vec, vec, vec],
        out_shape=[_sds((T, SSM_CONV_DIM)), _sds((T, LANES)), _sds((1, LANES)), _sds((1, LANES)), _sds((1, LANES))],
        scratch_shapes=[pltpu.VMEM((SSM_GROUPS, GROUP_W, SSM_STATE), F32), pltpu.VMEM((L, SSM_D_INNER), F32),
                        pltpu.VMEM((L, SSM_D_INNER), F32)],
        compiler_params=_cp("arbitrary"), name=name)(xact, dtraw, dt_bias, a_log, d_skip, es, est, states, dy)


def _ssm_post_fwd(y, xact, z, d_skip, nw, name):
    T = y.shape[0]
    tm = min(T, 256)
    W = SSM_D_INNER

    def body(y_ref, x_ref, z_ref, d_ref, w_ref, o_ref):
        y2 = (y_ref[...] + d_ref[...] * x_ref[...]) * _silu(z_ref[...])
        r = lax.rsqrt(jnp.mean(y2 * y2, axis=-1, keepdims=True) + SSM_NORM_EPS)
        o_ref[...] = (y2 * r * w_ref[...]).astype(o_ref.dtype)

    row = pl.BlockSpec((tm, W), lambda i: (i, 0))
    vec = pl.BlockSpec((1, W), lambda i: (0, 0))
    return pl.pallas_call(
        body, grid=(T // tm,), in_specs=[row, row, row, vec, vec], out_specs=row, out_shape=_sds((T, W), MXU),
        compiler_params=_cp("parallel"), name=name)(y, xact, z, d_skip, nw)


def _ssm_post_bwd(y, xact, z, d_skip, nw, dyn, name):
    T = y.shape[0]
    tm = min(T, 256)
    W = SSM_D_INNER

    def body(y_ref, x_ref, z_ref, d_ref, w_ref, dn_ref, dyg_ref, dz_ref, dw_ref):
        zv = z_ref[...]
        sz = _silu(zv)
        yg = y_ref[...] + d_ref[...] * x_ref[...]
        y2 = yg * sz
        r = lax.rsqrt(jnp.mean(y2 * y2, axis=-1, keepdims=True) + SSM_NORM_EPS)
        y2h = y2 * r
        dn = dn_ref[...]
        gy = dn * w_ref[...]
        dy2 = r * (gy - y2h * jnp.mean(gy * y2h, axis=-1, keepdims=True))
        dyg_ref[...] = dy2 * sz
        dz_ref[...] = (dy2 * yg * _dsilu(zv)).astype(dz_ref.dtype)
        part = jnp.sum(dn * y2h, axis=0, keepdims=True)

        @pl.when(pl.program_id(0) == 0)
        def _():
            dw_ref[...] = part

        @pl.when(pl.program_id(0) > 0)
        def _():
            dw_ref[...] += part

    row = pl.BlockSpec((tm, W), lambda i: (i, 0))
    vec = pl.BlockSpec((1, W), lambda i: (0, 0))
    return pl.pallas_call(
        body, grid=(T // tm,), in_specs=[row, row, row, vec, vec, row], out_specs=[row, row, vec],
        out_shape=[_sds((T, W)), _sds((T, W), MXU), _sds((1, W))],
        compiler_params=_cp("arbitrary"), name=name)(y, xact, z, d_skip, nw, dyn)


def _local_step(x0, cos, sin_s, target, P, fetch, token, send):
    mmf = functools.partial(_mm, tm=1024)
    big, small = {}, {}
    P = dict(P, wup={}, wdn={}, fcw={})
    h0 = _rmsnorm_fwd(x0, P["nm"][0], "norm_mix0", token=token)
    proj0 = mmf(h0, P["wmiT"], tb=True, tn=1280, tk=1024, name="mix_in")
    cat, attn, lse = _mixcore_fwd(proj0, cos, sin_s, P["pool_w"], P["pool_scale"], P["sinks"], "mixcore_fwd")
    x1, hf0 = mmf(cat, P["wmo"], tn=1024, tk=1024, res=x0, norm_w=P["nf"][0], name="mix_out")

    def ffn_fwd(xin, hf, i, next_norm):
        got = fetch(f"ffn{i}", hf)
        P["wup"][i], P["wdn"][i], P["fcw"][i] = got["wup"], got["wdn"], got["fcw"]
        hid = mmf(hf, P["wup"][i], tn=1408, tk=1024, out_dtype=MXU, name=f"ffn_up{i}")
        act, hc = _ffn_mid_fwd(hid, P["fcw"][i], P["fcb"][i], f"ffn_mid_fwd{i}")
        xout = mmf(act, P["wdn"][i], tn=1024, tk=D_FF, res=xin, norm_w=next_norm, name=f"ffn_down{i}")
        return (hid, hc), act, xout

    hid0, act0, (x2, h1) = ffn_fwd(x1, hf0, 0, P["nm"][1])
    P.update(fetch("ssm", h1))
    z = mmf(h1, P["wzT"], tb=True, tn=1024, tk=1024, name="ssm_in_z")
    xbc = mmf(h1, P["wxbcT"], tb=True, tn=1024, tk=1024, name="ssm_in_xbc")
    dtraw = mmf(h1, P["wdtT"], tb=True, tn=128, tk=1024, name="ssm_in_dt")
    xact, xpre = _ssm_pre_fwd(xbc, P["scw"], P["scb"], "ssm_pre_fwd")
    y, states = _ssd_fwd(xact, dtraw, P["dt_bias"], P["a_log"], "ssd_fwd")
    yn = _ssm_post_fwd(y, xact, z, P["d_exp"], P["snorm"], "ssm_post_fwd")
    x3, hf1 = mmf(yn, P["wso"], tn=1024, tk=SSM_D_INNER, res=x2, norm_w=P["nf"][1], name="ssm_out")
    hid1, act1, x4 = ffn_fwd(x3, hf1, 1, None)
    loss_row, dx4, d_nfin = _loss_head(x4, P["nfin"], target, "loss_head")
    small["norm_final"] = d_nfin

    def ffn_bwd(xin, dxo, hf, hid, act, i):
        da = mmf(dxo, P["wdn"][i], tb=True, tn=1408, tk=1024, name=f"ffn_down_dx{i}")
        big[f"ffn_w_down{i}"] = dwf(act, dxo, tm=1408, tn=1024, name=f"ffn_down_dw{i}").reshape(N_CHIPS, D_FF // N_CHIPS, D_MODEL)
        dhid, dcw, dcb = _ffn_mid_bwd(hid[0], hid[1], P["fcw"][i], da, f"ffn_mid_bwd{i}")
        big[f"ffn_w_up{i}"] = dwf(hf, dhid, tm=1024, tn=1408, out_shard_perm=(0, 2, 1, 3), name=f"ffn_up_dw{i}")
        tok = send(f"ffn{i}", [big[f"ffn_w_up{i}"], big[f"ffn_w_down{i}"]])
        dxi, dnf = _mm(dhid, P["wup"][i], tb=True, tm=512, tn=1024, tk=2816, norm_bwd=(xin, P["nf"][i], dxo, tok), name=f"ffn_up_dx{i}")
        return dxi, dnf, dcw, dcb

    dwf = functools.partial(_mm, ta=True, tk=2048, out_dtype=BF16)
    dx3, dnf1, dfcw1, dfcb1 = ffn_bwd(x3, dx4, hf1, hid1, act1, 1)
    dyn = mmf(dx3, P["wso"], tb=True, tn=1024, tk=1024, name="ssm_out_dx")
    big["ssm_w_out"] = dwf(yn, dx3, tm=1024, tn=1024, name="ssm_out_dw").reshape(N_CHIPS, SSM_D_INNER // N_CHIPS, D_MODEL)
    dyg, dz, d_snorm = _ssm_post_bwd(y, xact, z, P["d_exp"], P["snorm"], dyn, "ssm_post_bwd")
    dxact_p, ddtraw, d_dtb, d_alog, d_dskip = _ssd_bwd(xact, dtraw, P["dt_bias"], P["a_log"], P["d_exp"], states, dyg, "ssd_bwd")
    dxbc, d_scw, d_scb = _ssm_pre_bwd(xbc, xpre, P["scw"], dxact_p, "ssm_pre_bwd")
    dwsi = dwf(dz, h1, tm=1024, tn=1024, out_into=(None, SSM_IN_DIM, 0), name="ssm_in_dw_z")
    dwsi = dwf(dxbc, h1, tm=1024, tn=1024, out_into=(dwsi, SSM_IN_DIM, SSM_D_INNER // 1024), name="ssm_in_dw_xbc")
    dwdt = dwf(ddtraw, h1, tm=128, tn=1024, name="ssm_in_dw_dt")
    dwsi = _put_rows(dwsi, dwdt, SSM_HEADS, SSM_D_INNER + SSM_CONV_DIM, "ssm_in_dw_put_dt")
    big["ssm_w_in"] = dwsi.reshape(N_CHIPS, SSM_IN_DIM // N_CHIPS, D_MODEL)
    tok = send("ssm", [big["ssm_w_in"], big["ssm_w_out"]])
    dh1 = mmf(dz, P["wzT"], tn=1024, tk=2048, name="ssm_in_dx_z")
    dh1 = mmf(dxbc, P["wxbcT"], tn=1024, tk=2048, res=dh1, name="ssm_in_dx_xbc")
    dx2, dnm1 = mmf(ddtraw, P["wdtT"], tn=1024, tk=128, res=dh1, norm_bwd=(x2, P["nm"][1], dx3, tok), name="ssm_in_dx_dt")
    dx1, dnf0, dfcw0, dfcb0 = ffn_bwd(x1, dx2, hf0, hid0, act0, 0)
    dcat = mmf(dx1, P["wmo"], tb=True, tn=1024, tk=1024, name="mix_out_dx")
    big["mix_w_out"] = dwf(cat, dx1, tm=1024, tn=1024, name="mix_out_dw").reshape(N_CHIPS, D_MODEL // N_CHIPS, D_MODEL)
    dproj0, d_pw, d_ps, d_sk = _mixcore_bwd(proj0, cos, sin_s, P["pool_w"], P["pool_scale"], P["sinks"], attn, lse, dcat, "mixcore_bwd")
    big["mix_w_in"] = dwf(dproj0, h0, tm=1280, tn=1024, name="mix_in_dw").reshape(N_CHIPS, MIX_IN_DIM // N_CHIPS, D_MODEL)
    tok = send("mix", [big["mix_w_in"], big["mix_w_out"]])
    dx0, dnm0 = mmf(dproj0, P["wmiT"], tn=1024, tk=1280, norm_bwd=(x0, P["nm"][0], dx1, tok), name="mix_in_dx")

    def unperm_cols(a):
        r = a.shape[0]
        t = a.reshape(r, N_CHIPS, FFN_TC)
        return jnp.stack([t[:, p] for p in _PERM], axis=0)

    small["norm_mix"] = jnp.concatenate([dnm0, dnm1], axis=0)
    small["norm_ffn"] = jnp.concatenate([dnf0, dnf1], axis=0)
    small["pool_w"] = d_pw.reshape(4 * POOL_GROUP, POOL_GROUP)
    small["pool_scale"] = d_ps
    small["attn_sinks"] = d_sk
    small["ssm_dt_bias"] = d_dtb
    small["ssm_A_log"] = d_alog
    small["ssm_D"] = d_dskip
    fcb = jnp.stack([unperm_cols(dfcb0), unperm_cols(dfcb1)], axis=0)
    small["ffn_conv_b"] = fcb.reshape(2, 2 * D_FF)
    small["ssm_conv_w"] = d_scw.reshape(SSM_CONV, N_CHIPS, SSM_CONV_DIM // N_CHIPS).transpose(1, 0, 2)
    small["ssm_conv_b"] = d_scb.reshape(N_CHIPS, 1, SSM_CONV_DIM // N_CHIPS)
    small["ssm_norm"] = d_snorm.reshape(N_CHIPS, 1, SSM_D_INNER // N_CHIPS)
    small["ffn_conv_w"] = jnp.concatenate([unperm_cols(dfcw0), unperm_cols(dfcw1)], axis=1)
    return loss_row, dx0, big, small


ANY = pl.BlockSpec(memory_space=pl.ANY)


def _place():
    return lax.axis_index("x"), lax.axis_index("y"), lax.axis_index("c")


def _gather_shards(shards, name):
    n = len(shards)
    split = [s.size >= (1 << 16) for s in shards]

    def half(ref, a, h):
        shp = shards[a].shape
        if len(shp) == 3:
            return ref.at[h]
        r2 = shp[0] // 2
        return ref.at[pl.ds(pl.multiple_of(h * r2, 2 * SUBLANES), r2), :]

    def body(*refs):
        ins, outs = refs[:n], refs[n:2 * n]
        send, recv, fsend, frecv = refs[2 * n:]
        x, y, c = _place()
        k = 2 * x + y
        chips = [(1 - x, y), (x, 1 - y), (1 - x, 1 - y)]

        def ici(a, j, src_slot_ref, dst_slot):
            px, py = chips[j]
            src = half(src_slot_ref, a, c) if split[a] else src_slot_ref
            dst = half(outs[a].at[dst_slot], a, c) if split[a] else outs[a].at[dst_slot]
            return pltpu.make_async_remote_copy(src, dst, send.at[a, j], recv.at[a, j], device_id=(px, py, c), device_id_type=MESH)

        def d2d(a, j, h):
            px, py = chips[j]
            part = half(outs[a].at[2 * px + py], a, h)
            return pltpu.make_async_remote_copy(part, part, fsend.at[a, j], frecv.at[a, j], device_id=(x, y, 1 - c), device_id_type=MESH)

        sends = [ici(a, j, ins[a], k) for a in range(n) for j in range(3)]
        for cp in sends:
            cp.start()
        passed = []
        for a in range(n):
            for j, (px, py) in enumerate(chips):
                ici(a, j, ins[a], 2 * px + py).wait_recv()
                if split[a]:
                    passed.append(d2d(a, j, c))
                    passed[-1].start()
        for a in range(n):
            if split[a]:
                for j in range(3):
                    d2d(a, j, 1 - c).wait_recv()
        for cp in sends + passed:
            cp.wait_send()

    return pl.pallas_call(
        body, in_specs=[ANY] * n, out_specs=[ANY] * n,
        out_shape=[_sds((N_CHIPS,) + s.shape, s.dtype) for s in shards],
        scratch_shapes=[pltpu.SemaphoreType.DMA((n, 3))] * 4,
        compiler_params=pltpu.CompilerParams(has_side_effects=True), name=name)(*shards)


HBM = pl.BlockSpec(memory_space=pltpu.HBM)
SEM = pl.BlockSpec(memory_space=pltpu.SEMAPHORE)
DATAFLOW = pltpu.SideEffectType.DATAFLOW_SIDE_EFFECTING


def _spread_start(groups, slot_src, after, name):
    flat = [a for grp in groups for a in grp]
    n = len(flat)
    ng = len(groups)
    offs = [sum(len(g) for g in groups[:i]) for i in range(ng)]
    lshape = [(a.shape if slot_src else (N_CHIPS,) + a.shape) for a in flat]

    nsem = 6 * n

    def body(*refs):
        src, land = refs[:n], refs[n:2 * n]
        sems = refs[2 * n + 1:2 * n + 1 + nsem]
        token = refs[-1]
        x, y, c = _place()
        k = 2 * x + y
        chips = [(1 - x, y), (x, 1 - y), (1 - x, 1 - y)]
        for a in range(n):
            for j, (px, py) in enumerate(chips):
                s = src[a].at[2 * px + py] if slot_src else src[a]
                pltpu.make_async_remote_copy(s, land[a].at[k], sems[6 * a + 2 * j], sems[6 * a + 2 * j + 1],
                                             device_id=(px, py, c), device_id_type=MESH).start()
        token[...] = jnp.zeros(token.shape, token.dtype)

    out_shape = [pltpu.SemaphoreType.DMA(())] * nsem
    out_shape += [pltpu.HBM(a.shape, a.dtype) for a in flat] + [pltpu.HBM(s, a.dtype) for s, a in zip(lshape, flat)]
    out_shape.append(_sds((SUBLANES, LANES)))
    args = [pltpu.with_memory_space_constraint(a, pltpu.HBM) for a in flat]
    args += [pltpu.with_memory_space_constraint(lax.empty(s, a.dtype), pltpu.HBM) for s, a in zip(lshape, flat)]
    res = pl.pallas_call(
        body, name=name, out_shape=tuple(out_shape), in_specs=[HBM] * (2 * n) + [pl.BlockSpec(memory_space=pl.ANY)],
        out_specs=tuple([SEM] * nsem + [HBM] * (2 * n) + [pl.BlockSpec(memory_space=pltpu.VMEM)]),
        input_output_aliases={i: nsem + i for i in range(2 * n)},
        compiler_params=pltpu.CompilerParams(has_side_effects=DATAFLOW))(*args, after)
    sems, thru, token = res[:nsem], res[nsem:nsem + 2 * n], res[-1]
    out = []
    for gi, grp in enumerate(groups):
        sl = slice(offs[gi], offs[gi] + len(grp))
        out.append((list(sems[6 * offs[gi]:6 * (offs[gi] + len(grp))]), list(thru[:n][sl]), list(thru[n:][sl])))
    return out, token


def _spread_wait(started, slot_src, after, name):
    sems, srcs, lands = started
    n = len(srcs)

    def body(*refs):
        src, land = refs[:n], refs[n:2 * n]
        sem = refs[2 * n:2 * n + 6 * n]
        x, y, c = _place()
        chips = [(1 - x, y), (x, 1 - y), (1 - x, 1 - y)]
        for a in range(n):
            for j, (px, py) in enumerate(chips):
                s = src[a].at[2 * px + py] if slot_src else src[a]
                cp = pltpu.make_async_remote_copy(s, land[a].at[2 * px + py], sem[6 * a + 2 * j], sem[6 * a + 2 * j + 1],
                                                  device_id=(px, py, c), device_id_type=MESH)
                cp.wait_send()
                cp.wait_recv()

    res = pl.pallas_call(
        body, name=name, out_shape=tuple([pltpu.HBM(a.shape, a.dtype) for a in srcs] + [pltpu.HBM(a.shape, a.dtype) for a in lands]),
        in_specs=[HBM] * (2 * n) + [SEM] * (6 * n) + [pl.BlockSpec(memory_space=pl.ANY)], out_specs=tuple([HBM] * (2 * n)),
        input_output_aliases={i: i for i in range(2 * n)},
        compiler_params=pltpu.CompilerParams(has_side_effects=DATAFLOW))(*srcs, *lands, *sems, after)
    return list(res[:n]), list(res[n:])


def _sibling_exchange(fs, name):
    n = len(fs)

    def body(*refs):
        ins, outs = refs[:n], refs[n:2 * n]
        send, recv = refs[2 * n:]
        x, y, c = _place()
        cps = [pltpu.make_async_remote_copy(ins[a], outs[a], send.at[a], recv.at[a],
                                            device_id=(x, y, 1 - c), device_id_type=MESH) for a in range(n)]
        for cp in cps:
            cp.start()
        for cp in cps:
            cp.wait()

    return pl.pallas_call(
        body, in_specs=[ANY] * n, out_specs=[ANY] * n, out_shape=[_sds(f.shape, f.dtype) for f in fs],
        scratch_shapes=[pltpu.SemaphoreType.DMA((n,)), pltpu.SemaphoreType.DMA((n,))],
        compiler_params=pltpu.CompilerParams(has_side_effects=True), name=name)(*fs)


def _tile2d(rows, cols, budget=1024 * 1024, step=2 * SUBLANES):
    fits = [t for t in range(step, rows + 1, step) if rows % t == 0 and t * cols * 4 <= budget]
    if fits:
        return fits[-1], cols
    fits = [t for t in range(LANES, cols + 1, LANES) if cols % t == 0 and rows * t * 4 <= budget]
    assert fits, (rows, cols)
    return rows, fits[-1]


def _chip_sum(own, parts, kidx, name):
    _, R, C = parts.shape
    tr, tc = _tile2d(R, C)

    def body(k_ref, o_ref_in, p1_ref, p2_ref, p3_ref, o_ref):
        o_ref[...] = ((o_ref_in[...].astype(F32) + p1_ref[...].astype(F32)) + p2_ref[...].astype(F32)) + p3_ref[...].astype(F32)

    def slot(d):
        return pl.BlockSpec((None, tr, tc), lambda i, j, k: ((k[0] + d) % N_CHIPS, i, j))

    return pl.pallas_call(
        body,
        grid_spec=pltpu.PrefetchScalarGridSpec(
            num_scalar_prefetch=1, grid=(R // tr, C // tc), in_specs=[slot(0), slot(1), slot(2), slot(3)],
            out_specs=pl.BlockSpec((tr, tc), lambda i, j, k: (i, j))),
        out_shape=_sds((R, C)), compiler_params=_cp("parallel", "parallel"), name=name)(kidx, own, parts, parts, parts)


def _adamw_math(w, g, m, v):
    m2 = ADAM_B1 * m + (1.0 - ADAM_B1) * g
    v2 = ADAM_B2 * v + (1.0 - ADAM_B2) * (g * g)
    m_hat = m2 / (1.0 - ADAM_B1 ** ADAM_STEP)
    v_hat = v2 / (1.0 - ADAM_B2 ** ADAM_STEP)
    delta = -ADAM_LR * (m_hat / (jnp.sqrt(v_hat) + ADAM_EPS) + ADAM_WD * w)
    return delta, m2, v2


def _adamw(w, m, v, gparts, name):
    Lw, R, C = w.shape
    tr, tc = _tile2d(R, C)
    flat = [h for pair in gparts for h in pair]

    def body(*refs):
        w_ref, m_ref, v_ref = refs[:3]
        g_refs = refs[3:3 + 2 * Lw]
        go_ref, d_ref, mo_ref, vo_ref = refs[3 + 2 * Lw:]
        g = g_refs[0][...] + g_refs[1][...]
        for l in range(1, Lw):
            g = jnp.where(pl.program_id(0) == l, g_refs[2 * l][...] + g_refs[2 * l + 1][...], g)
        d, m2, v2 = _adamw_math(w_ref[...], g, m_ref[...], v_ref[...])
        go_ref[...] = g
        d_ref[...] = d
        mo_ref[...] = m2
        vo_ref[...] = v2

    blk = pl.BlockSpec((None, tr, tc), lambda l, i, j: (l, i, j))
    gblk = pl.BlockSpec((tr, tc), lambda l, i, j: (i, j))
    return pl.pallas_call(
        body, grid=(Lw, R // tr, C // tc), in_specs=[blk, blk, blk] + [gblk] * (2 * Lw), out_specs=[blk] * 4,
        out_shape=[_sds((Lw, R, C))] * 4, compiler_params=_cp("parallel", "parallel", "parallel"), name=name)(w, m, v, *flat)


def _small_adamw(grads, wmv, name):
    n = len(grads)

    def body(*refs):
        g_in, p_in, outs = refs[:n], refs[n:4 * n], refs[4 * n:]
        for a in range(n):
            g = g_in[a][...]
            d_, m2, v2 = _adamw_math(p_in[3 * a][...], g, p_in[3 * a + 1][...], p_in[3 * a + 2][...])
            outs[4 * a][...] = g
            outs[4 * a + 1][...] = d_
            outs[4 * a + 2][...] = m2
            outs[4 * a + 3][...] = v2

    vm = pl.BlockSpec(memory_space=pltpu.VMEM)
    args = list(grads) + [t for tri in wmv for t in tri]
    out_shape = [_sds(g.shape) for g in grads for _ in range(4)]
    return pl.pallas_call(body, in_specs=[vm] * len(args), out_specs=[vm] * len(out_shape), out_shape=out_shape,
                          compiler_params=pltpu.CompilerParams(vmem_limit_bytes=V7X_VMEM_LIMIT), name=name)(*args)


def _small_allreduce(partials, pshapes, loss_row, name):
    n = len(partials)
    gshapes = [p.shape for p in partials] + [loss_row.shape]
    ng = n + 1

    def body(*refs):
        g_in = refs[:ng]
        outs = refs[ng:2 * ng]
        sib = refs[2 * ng:3 * ng]
        pair = refs[3 * ng:4 * ng]
        bufs = refs[4 * ng:5 * ng]
        send1, recv1, send2, recv2 = refs[-4:]
        x, y, c = _place()
        k = 2 * x + y
        chips = [(1 - x, y), (x, 1 - y), (1 - x, 1 - y)]
        swaps = [pltpu.make_async_remote_copy(g_in[a], sib[a], send1.at[a], recv1.at[a],
                                              device_id=(x, y, 1 - c), device_id_type=MESH) for a in range(ng)]
        for cp in swaps:
            cp.start()
        for a, cp in enumerate(swaps):
            cp.wait()
            pair[a][...] = g_in[a][...] + sib[a][...]
            bufs[a][k] = pair[a][...]
        sends = [pltpu.make_async_remote_copy(pair[a], bufs[a].at[k], send2.at[a, j], recv2.at[a, j],
                                              device_id=(px, py, c), device_id_type=MESH)
                 for a in range(ng) for j, (px, py) in enumerate(chips)]
        for cp in sends:
            cp.start()
        for a in range(ng):
            for j, (px, py) in enumerate(chips):
                pltpu.make_async_remote_copy(pair[a], bufs[a].at[2 * px + py], send2.at[a, j], recv2.at[a, j],
                                             device_id=(px, py, c), device_id_type=MESH).wait_recv()
        for cp in sends:
            cp.wait_send()
        for a in range(ng):
            sharded = len(gshapes[a]) == 3

            def part(d):
                return bufs[a][d, k] if sharded else bufs[a][d]

            tot = part(0)
            for d in range(1, N_CHIPS):
                tot = tot + part(d)
            if a == n:
                outs[n][...] = tot
            else:
                pr, pc = pshapes[a]
                outs[a][...] = tot[:pr, :pc]

    vm = pl.BlockSpec(memory_space=pltpu.VMEM)
    args = list(partials) + [loss_row]
    out_shape = [_sds(ps) for ps in pshapes] + [_sds(loss_row.shape)]
    return pl.pallas_call(
        body, in_specs=[vm] * len(args), out_specs=[vm] * len(out_shape), out_shape=out_shape,
        scratch_shapes=[pltpu.VMEM(tuple(s), F32) for s in gshapes] * 2 + [pltpu.VMEM((N_CHIPS,) + tuple(s), F32) for s in gshapes]
        + [pltpu.SemaphoreType.DMA((ng,)), pltpu.SemaphoreType.DMA((ng,)),
           pltpu.SemaphoreType.DMA((ng, 3)), pltpu.SemaphoreType.DMA((ng, 3))],
        compiler_params=pltpu.CompilerParams(has_side_effects=True, vmem_limit_bytes=V7X_VMEM_LIMIT), name=name)(*args)


_PERM = (0, 2, 1, 3)


def _cols_from_shards(g):
    return g.transpose(1, 0, 2).reshape(g.shape[1], N_CHIPS * g.shape[2])


def _rope_tables(positions):
    inv_freq = ROPE_THETA ** (-jnp.arange(0, HEAD_DIM, 2, dtype=F32) / HEAD_DIM)
    ang = positions.astype(F32).reshape(-1, 1) * inv_freq
    cos, sin = jnp.cos(ang), jnp.sin(ang)
    cos = jnp.concatenate([cos, cos, cos, cos], axis=-1)
    sin_s = jnp.concatenate([-sin, sin, -sin, sin], axis=-1)
    return cos, sin_s


def kernel(x, positions, norm_mix, norm_ffn, norm_final, mix_w_in, pool_w, pool_scale, attn_sinks, mix_w_out, ssm_w_in, ssm_conv_w, ssm_conv_b, ssm_dt_bias, ssm_A_log, ssm_D, ssm_norm, ssm_w_out, ffn_w_up, ffn_conv_w, ffn_conv_b, ffn_w_down, loss_target, m_norm_mix, m_norm_ffn, m_norm_final, m_mix_w_in, m_pool_w, m_pool_scale, m_attn_sinks, m_mix_w_out, m_ssm_w_in, m_ssm_conv_w, m_ssm_conv_b, m_ssm_dt_bias, m_ssm_A_log, m_ssm_D, m_ssm_norm, m_ssm_w_out, m_ffn_w_up, m_ffn_conv_w, m_ffn_conv_b, m_ffn_w_down, v_norm_mix, v_norm_ffn, v_norm_final, v_mix_w_in, v_pool_w, v_pool_scale, v_attn_sinks, v_mix_w_out, v_ssm_w_in, v_ssm_conv_w, v_ssm_conv_b, v_ssm_dt_bias, v_ssm_A_log, v_ssm_D, v_ssm_norm, v_ssm_w_out, v_ffn_w_up, v_ffn_conv_w, v_ffn_conv_b, v_ffn_w_down):
    W = dict(norm_mix=norm_mix, norm_ffn=norm_ffn, norm_final=norm_final, mix_w_in=mix_w_in, pool_w=pool_w, pool_scale=pool_scale, attn_sinks=attn_sinks, mix_w_out=mix_w_out, ssm_w_in=ssm_w_in, ssm_conv_w=ssm_conv_w, ssm_conv_b=ssm_conv_b, ssm_dt_bias=ssm_dt_bias, ssm_A_log=ssm_A_log, ssm_D=ssm_D, ssm_norm=ssm_norm, ssm_w_out=ssm_w_out, ffn_w_up=ffn_w_up, ffn_conv_w=ffn_conv_w, ffn_conv_b=ffn_conv_b, ffn_w_down=ffn_w_down)
    Mo = dict(norm_mix=m_norm_mix, norm_ffn=m_norm_ffn, norm_final=m_norm_final, mix_w_in=m_mix_w_in, pool_w=m_pool_w, pool_scale=m_pool_scale, attn_sinks=m_attn_sinks, mix_w_out=m_mix_w_out, ssm_w_in=m_ssm_w_in, ssm_conv_w=m_ssm_conv_w, ssm_conv_b=m_ssm_conv_b, ssm_dt_bias=m_ssm_dt_bias, ssm_A_log=m_ssm_A_log, ssm_D=m_ssm_D, ssm_norm=m_ssm_norm, ssm_w_out=m_ssm_w_out, ffn_w_up=m_ffn_w_up, ffn_conv_w=m_ffn_conv_w, ffn_conv_b=m_ffn_conv_b, ffn_w_down=m_ffn_w_down)
    Vo = dict(norm_mix=v_norm_mix, norm_ffn=v_norm_ffn, norm_final=v_norm_final, mix_w_in=v_mix_w_in, pool_w=v_pool_w, pool_scale=v_pool_scale, attn_sinks=v_attn_sinks, mix_w_out=v_mix_w_out, ssm_w_in=v_ssm_w_in, ssm_conv_w=v_ssm_conv_w, ssm_conv_b=v_ssm_conv_b, ssm_dt_bias=v_ssm_dt_bias, ssm_A_log=v_ssm_A_log, ssm_D=v_ssm_D, ssm_norm=v_ssm_norm, ssm_w_out=v_ssm_w_out, ffn_w_up=v_ffn_w_up, ffn_conv_w=v_ffn_conv_w, ffn_conv_b=v_ffn_conv_b, ffn_w_down=v_ffn_w_down)

    kchip = 2 * lax.axis_index("x") + lax.axis_index("y")

    def own_slot(g, own):
        return lax.dynamic_update_slice_in_dim(g, own[None], kchip, axis=0)

    def tr(t):
        return jnp.swapaxes(t[0], 0, 1)

    later = dict(ffn0=[ffn_w_up[0].astype(MXU), ffn_w_down[0].astype(MXU)],
                 ssm=[tr(ssm_w_in).astype(MXU), ssm_w_out[0].astype(MXU)],
                 ffn1=[ffn_w_up[1].astype(MXU), ffn_w_down[1].astype(MXU)])
    sh = [tr(mix_w_in).astype(MXU), mix_w_out[0].astype(MXU), ssm_conv_w[0], ssm_conv_b, ssm_norm, ffn_conv_w]
    first = _gather_shards(sh, "gather_first")
    g_mi, g_mo, g_scw, g_scb, g_sn, g_fcw = [own_slot(g, own) for g, own in zip(first, sh)]
    started, token = _spread_start(list(later.values()), False, first[0], "gather_start")
    started = dict(zip(later.keys(), started))
    fcw = [jnp.concatenate([g_fcw[p, i] for p in _PERM], axis=1) for i in range(2)]
    P = dict(
        nm=norm_mix, nf=norm_ffn, nfin=norm_final,
        wmiT=g_mi.reshape(MIX_IN_DIM, D_MODEL), wmo=g_mo.reshape(D_MODEL, D_MODEL),
        pool_w=pool_w[0], pool_scale=pool_scale, sinks=attn_sinks[0],
        scw=_cols_from_shards(g_scw), scb=g_scb.reshape(1, SSM_CONV_DIM), snorm=g_sn.reshape(1, SSM_D_INNER),
        dt_bias=jnp.pad(ssm_dt_bias, ((0, 0), (0, LANES - SSM_HEADS))), a_log=jnp.pad(ssm_A_log, ((0, 0), (0, LANES - SSM_HEADS))),
        d_exp=jnp.repeat(ssm_D, SSM_D_INNER // SSM_HEADS, axis=1),
        fcb=[jnp.concatenate([ffn_conv_b[i:i + 1, p * FFN_TC:(p + 1) * FFN_TC] for p in _PERM], axis=1) for i in range(2)],
    )

    def fetch(group, after):
        owns, lands = _spread_wait(started[group], False, after, f"gather_wait_{group}")
        a, b = [own_slot(g, own) for g, own in zip(lands, owns)]
        if group == "ssm":
            wsi = a.reshape(SSM_IN_DIM, D_MODEL)
            zx = SSM_D_INNER + SSM_CONV_DIM
            return dict(wzT=wsi[:SSM_D_INNER], wxbcT=wsi[SSM_D_INNER:zx],
                        wdtT=jnp.pad(wsi[zx:], ((0, LANES - SSM_HEADS), (0, 0))), wso=b.reshape(SSM_D_INNER, D_MODEL))
        i = int(group[-1])
        return dict(wup=jnp.concatenate([a[p] for p in _PERM], axis=1), wdn=b.reshape(D_FF, D_MODEL), fcw=fcw[i])

    cos, sin_s = _rope_tables(positions)
    sent = {}

    def send(group, grads):
        res, tok = _spread_start([grads], True, jnp.zeros((SUBLANES, LANES), F32), f"grad_start_{group}")
        sent[group] = res[0]
        return tok

    loss_row, grad_x, big, small = _local_step(x[0], cos, sin_s, loss_target[0], P, fetch, token, send)

    kidx = kchip.astype(jnp.int32).reshape(1)
    group_names = dict(ffn1=["ffn_w_up1", "ffn_w_down1"], ssm=["ssm_w_in", "ssm_w_out"], ffn0=["ffn_w_up0", "ffn_w_down0"],
                       mix=["mix_w_in", "mix_w_out"])
    names, mine = [], []
    for group, started_g in sent.items():
        grads, lands = _spread_wait(started_g, True, grad_x, f"grad_wait_{group}")
        for nm, g, land in zip(group_names[group], grads, lands):
            names.append(nm)
            mine.append(_chip_sum(g, land, kidx, f"chip_sum_{nm}"))
    theirs = _sibling_exchange(mine, "sibling_exchange")
    red = {nm: (a, b) for nm, a, b in zip(names, mine, theirs)}

    out = {}

    def big_update(pname, gparts, transposed=False):
        w = W[pname]
        lw = len(gparts)
        shp = w.shape
        rr, cc = gparts[0][0].shape
        fix = (lambda t: tr(t)[None]) if transposed else (lambda t: t.reshape(lw, rr, cc))
        res = _adamw(fix(w), fix(Mo[pname]), fix(Vo[pname]), gparts, f"adamw_{pname}")
        out[pname] = tuple((tr(r)[None] if transposed else r.reshape(shp)) for r in res)

    big_update("mix_w_in", [red["mix_w_in"]], transposed=True)
    big_update("mix_w_out", [red["mix_w_out"]])
    big_update("ssm_w_in", [red["ssm_w_in"]], transposed=True)
    big_update("ssm_w_out", [red["ssm_w_out"]])
    big_update("ffn_w_up", [red["ffn_w_up0"], red["ffn_w_up1"]])
    big_update("ffn_w_down", [red["ffn_w_down0"], red["ffn_w_down1"]])

    small_names = ["norm_mix", "norm_ffn", "norm_final", "pool_w", "pool_scale", "attn_sinks", "ssm_dt_bias", "ssm_A_log",
                   "ssm_D", "ffn_conv_b", "ssm_conv_w", "ssm_conv_b", "ssm_norm", "ffn_conv_w"]

    def as2d(t):
        if t.ndim == 1:
            return t.reshape(1, -1)
        return t.reshape(-1, t.shape[-1])

    wmv = [(as2d(W[nm]), as2d(Mo[nm]), as2d(Vo[nm])) for nm in small_names]
    summed = _small_allreduce([small[nm] for nm in small_names], [t[0].shape for t in wmv], loss_row, "small_allreduce")
    res = _small_adamw(summed[:-1], wmv, "small_adamw")
    for a, nm in enumerate(small_names):
        out[nm] = tuple(r.reshape(W[nm].shape) for r in res[4 * a:4 * a + 4])
    loss = summed[-1][0, 0]

    order = ["norm_mix", "norm_ffn", "norm_final", "mix_w_in", "pool_w", "pool_scale", "attn_sinks", "mix_w_out", "ssm_w_in",
             "ssm_conv_w", "ssm_conv_b", "ssm_dt_bias", "ssm_A_log", "ssm_D", "ssm_norm", "ssm_w_out", "ffn_w_up", "ffn_conv_w",
             "ffn_conv_b", "ffn_w_down"]
    return (loss, grad_x.reshape(x.shape), *[out[nm][0] for nm in order], *[out[nm][1] for nm in order],
            *[out[nm][2] for nm in order], *[out[nm][3] for nm in order])
```

```python
import functools

import jax
import jax.numpy as jnp
from jax import lax
from jax.experimental import pallas as pl
from jax.experimental.pallas import tpu as pltpu

F32 = jnp.float32
BF16 = jnp.bfloat16
MXU = BF16
HI = lax.Precision.HIGHEST

D_MODEL = 1024
POOL_WINDOWS = (2, 4, 8, 16)
POOL_DIM = 512
POOL_GROUP = 128
HEAD_DIM = 64
N_HEADS = 8
N_KV_HEADS = 2
GQ = 4
Q_DIM = 512
KV_DIM = 128
BLOCK = 128
ROPE_THETA = 10000.0
MIX_IN_DIM = 1280
SSM_D_INNER = 2048
SSM_HEADS = 32
SSM_GROUPS = 8
SSM_STATE = 128
SSM_CONV = 4
SSM_CHUNK = 128
SSM_CONV_DIM = 4096
SSM_IN_DIM = 6176
D_FF = 2816
FFN_CONV = 3
NORM_EPS = 1e-6
SSM_NORM_EPS = 1e-5
ADAM_LR = 0.001
ADAM_B1 = 0.9
ADAM_B2 = 0.999
ADAM_EPS = 1e-08
ADAM_WD = 0.01
ADAM_STEP = 10

N_CHIPS = 4
N_DEV = 8
LANES = 128
SUBLANES = 8
V7X_VMEM_LIMIT = 56 * 1024 * 1024
NEG = -1e30
MESH = pl.DeviceIdType.MESH


def _cp(*sem):
    return pltpu.CompilerParams(dimension_semantics=sem if sem else None, vmem_limit_bytes=V7X_VMEM_LIMIT)


def _sds(shape, dtype=F32):
    return jax.ShapeDtypeStruct(tuple(shape), dtype)


def _iota(shape, dim):
    return lax.broadcasted_iota(jnp.int32, shape, dim)


def _silu(x):
    return x * (1.0 / (1.0 + jnp.exp(-x)))


def _dsilu(x):
    s = 1.0 / (1.0 + jnp.exp(-x))
    return s * (1.0 + x * (1.0 - s))


def _mm(a, b, *, ta=False, tb=False, tm, tn, tk, res=None, out_dtype=F32, out_shard_perm=None, out_into=None, norm_w=None,
        norm_bwd=None, name):
    M, K = (a.shape[1], a.shape[0]) if ta else a.shape
    N = b.shape[0] if tb else b.shape[1]
    tm, tn, tk = min(tm, M), min(tn, N), min(tk, K)
    gm, gn, gk = M // tm, N // tn, K // tk
    assert gm * tm == M and gn * tn == N and gk * tk == K, (name, M, N, K, tm, tn, tk)
    a_spec = pl.BlockSpec((tk, tm), lambda i, j, k: (k, i)) if ta else pl.BlockSpec((tm, tk), lambda i, j, k: (i, k))
    b_spec = pl.BlockSpec((tn, tk), lambda i, j, k: (j, k)) if tb else pl.BlockSpec((tk, tn), lambda i, j, k: (k, j))
    dims = (((0 if ta else 1,), (1 if tb else 0,)), ((), ()))
    has_res = res is not None
    has_nw = norm_w is not None
    has_nb = norm_bwd is not None
    has_tok = has_nb and norm_bwd[3] is not None
    assert not (has_nw or has_nb) or (gn == 1 and out_shard_perm is None)
    n_extra = has_res + has_nw + (3 + has_tok if has_nb else 0)

    def body(*refs):
        a_ref, b_ref = refs[0], refs[1]
        extra = list(refs[2:2 + n_extra])
        outs = refs[len(args):]
        r_ref = extra.pop(0) if has_res else None
        nw_ref = extra.pop(0) if has_nw else None
        nb_refs = extra if has_nb else None

        def dot():
            return lax.dot_general(a_ref[...].astype(MXU), b_ref[...].astype(MXU), dims, preferred_element_type=F32)

        def finish(r):
            if has_res:
                r = r + r_ref[...]
            if has_nb:
                xv = nb_refs[0][...]
                rs = lax.rsqrt(jnp.mean(xv * xv, axis=-1, keepdims=True) + NORM_EPS)
                xh = xv * rs
                g = r * nb_refs[1][...]
                dr = nb_refs[2][...] + nb_refs[3][0:1, 0:1] if has_tok else nb_refs[2][...]
                outs[0][...] = dr + rs * (g - xh * jnp.mean(g * xh, axis=-1, keepdims=True))
                part = jnp.sum(r * xh, axis=0, keepdims=True)
                i = pl.program_id(0)

                @pl.when(i == 0)
                def _():
                    outs[1][...] = part

                @pl.when(i > 0)
                def _():
                    outs[1][...] += part
                return
            outs[0][...] = r.astype(out_dtype)
            if has_nw:
                rs = lax.rsqrt(jnp.mean(r * r, axis=-1, keepdims=True) + NORM_EPS)
                outs[1][...] = (r * rs * nw_ref[...]).astype(outs[1].dtype)

        if gk == 1:
            finish(dot())
        else:
            acc = refs[-1]
            k = pl.program_id(2)

            @pl.when(k == 0)
            def _():
                acc[...] = dot()

            if gk > 2:
                @pl.when(jnp.logical_and(k > 0, k < gk - 1))
                def _():
                    acc[...] += dot()

            @pl.when(k == gk - 1)
            def _():
                finish(acc[...] + dot())

    tile = pl.BlockSpec((tm, tn), lambda i, j, k: (i, j))
    row = pl.BlockSpec((1, tn), lambda i, j, k: (0, j))
    in_specs = [a_spec, b_spec]
    args = [a, b]
    if has_res:
        in_specs.append(tile)
        args.append(res)
    if has_nw:
        in_specs.append(row)
        args.append(norm_w.reshape(1, N))
    if has_nb:
        in_specs += [tile, row, tile]
        args += [norm_bwd[0], norm_bwd[1].reshape(1, N), norm_bwd[2]]
        if has_tok:
            in_specs.append(pl.BlockSpec((SUBLANES, LANES), lambda i, j, k: (0, 0)))
            args.append(norm_bwd[3])
    alias = {}
    if out_into is not None:
        buf, rows, off = out_into
        out_spec = pl.BlockSpec((tm, tn), lambda i, j, k: (i + off, j))
        out_shape = _sds((rows, N), out_dtype)
        if buf is not None:
            alias = {len(args): 0}
            in_specs.append(pl.BlockSpec(memory_space=pl.ANY))
            args.append(buf)
    elif out_shard_perm is None:
        out_spec = tile
        out_shape = _sds((M, N), out_dtype)
    else:
        assert gn == len(out_shard_perm) == 4 and tuple(out_shard_perm) == (0, 2, 1, 3)
        out_spec = pl.BlockSpec((None, tm, tn), lambda i, j, k: ((j % 2) * 2 + j // 2, i, 0))
        out_shape = _sds((gn, M, tn), out_dtype)
    sem = ("parallel", "parallel", "arbitrary")
    if has_nw:
        out_spec, out_shape = [out_spec, tile], [out_shape, _sds((M, N), MXU)]
    if has_nb:
        out_spec, out_shape = [tile, row], [_sds((M, N)), _sds((1, N))]
        sem = ("arbitrary", "arbitrary", "arbitrary")
    return pl.pallas_call(
        body, grid=(gm, gn, gk), in_specs=in_specs, out_specs=out_spec, out_shape=out_shape,
        scratch_shapes=[pltpu.VMEM((tm, tn), F32)] if gk > 1 else [], input_output_aliases=alias,
        compiler_params=_cp(*sem), name=name)(*args)


def _put_rows(buf, src, rows, at, name):
    assert at % rows == 0 and src.shape[1] == buf.shape[1] and src.dtype == buf.dtype
    C = buf.shape[1]

    def body(s_ref, b_ref, o_ref):
        o_ref[...] = s_ref[...]

    return pl.pallas_call(
        body, grid=(1,), in_specs=[pl.BlockSpec((rows, C), lambda i: (0, 0)), pl.BlockSpec(memory_space=pl.ANY)],
        out_specs=pl.BlockSpec((rows, C), lambda i: (at // rows, 0)), out_shape=_sds(buf.shape, buf.dtype),
        input_output_aliases={1: 0}, compiler_params=_cp("arbitrary"), name=name)(src, buf)


def _rmsnorm_fwd(x, w, name, token=None):
    T, D = x.shape
    tm = min(T, 512)
    has_token = token is not None

    def body(*refs):
        x_ref, w_ref, o_ref = refs[0], refs[1], refs[-1]
        xv = x_ref[...]
        if has_token:
            xv = xv + refs[2][0:1, 0:1]
        r = lax.rsqrt(jnp.mean(xv * xv, axis=-1, keepdims=True) + NORM_EPS)
        o_ref[...] = (xv * r * w_ref[...]).astype(o_ref.dtype)

    in_specs = [pl.BlockSpec((tm, D), lambda i: (i, 0)), pl.BlockSpec((1, D), lambda i: (0, 0))]
    args = [x, w.reshape(1, D)]
    if has_token:
        in_specs.append(pl.BlockSpec((SUBLANES, LANES), lambda i: (0, 0)))
        args.append(token)
    return pl.pallas_call(
        body, grid=(T // tm,), in_specs=in_specs,
        out_specs=pl.BlockSpec((tm, D), lambda i: (i, 0)), out_shape=_sds((T, D), MXU),
        compiler_params=_cp("parallel"), name=name)(*args)


def _loss_head(x, w, target, name):
    T, D = x.shape
    tm = min(T, 512)

    def body(x_ref, w_ref, t_ref, loss_ref, dx_ref, dw_ref):
        xv = x_ref[...]
        r = lax.rsqrt(jnp.mean(xv * xv, axis=-1, keepdims=True) + NORM_EPS)
        xh = xv * r
        wv = w_ref[...]
        e = xh * wv - t_ref[...]
        lpart = 0.5 * jnp.sum(jnp.mean(e * e, axis=-1, keepdims=True), axis=0, keepdims=True)
        dy = e * (1.0 / D)
        g = dy * wv
        dx_ref[...] = r * (g - xh * jnp.mean(g * xh, axis=-1, keepdims=True))
        part = jnp.sum(dy * xh, axis=0, keepdims=True)
        lrow = jnp.broadcast_to(lpart, (1, LANES))

        @pl.when(pl.program_id(0) == 0)
        def _():
            dw_ref[...] = part
            loss_ref[...] = lrow

        @pl.when(pl.program_id(0) > 0)
        def _():
            dw_ref[...] += part
            loss_ref[...] += lrow

    row = pl.BlockSpec((tm, D), lambda i: (i, 0))
    vec = pl.BlockSpec((1, D), lambda i: (0, 0))
    return pl.pallas_call(
        body, grid=(T // tm,), in_specs=[row, vec, row],
        out_specs=[pl.BlockSpec((1, LANES), lambda i: (0, 0)), row, vec],
        out_shape=[_sds((1, LANES)), _sds((T, D)), _sds((1, D))],
        compiler_params=_cp("arbitrary"), name=name)(x, w.reshape(1, D), target)


def _shift_down(cur, prev8, s):
    if s == 0:
        return cur
    tm = cur.shape[0]
    rc = pltpu.roll(cur, s, 0)
    top = jnp.where(_iota((SUBLANES, cur.shape[1]), 0) < s, pltpu.roll(prev8, s, 0), rc[:SUBLANES])
    return jnp.concatenate([top, rc[SUBLANES:]], axis=0) if tm > SUBLANES else top


def _shift_up(cur, next8, s):
    if s == 0:
        return cur
    tm = cur.shape[0]
    rc = pltpu.roll(cur, tm - s, 0)
    bot = jnp.where(_iota((SUBLANES, cur.shape[1]), 0) >= SUBLANES - s, pltpu.roll(next8, SUBLANES - s, 0), rc[tm - SUBLANES:])
    return jnp.concatenate([rc[:tm - SUBLANES], bot], axis=0) if tm > SUBLANES else bot


def _conv_rows(cur, prev8, w, b, K):
    acc = cur * w[K - 1:K, :] + b
    for s in range(1, K):
        acc = acc + _shift_down(cur, prev8, s) * w[K - 1 - s:K - s, :]
    return acc


FFN_TC = 1408
HALO16 = 2 * SUBLANES


def _ffn_up_conv_gate(hf, wup, cw, cb, name):
    T, D = hf.shape
    tm = min(T, 256)
    nt, nj = T // tm, D_FF // FFN_TC
    K = FFN_CONV
    W2 = 2 * FFN_TC

    def body(a_ref, b_ref, w_ref, c_ref, hid_ref, hc_ref, act_ref, halo):
        i = pl.program_id(1)

        @pl.when(i == 0)
        def _():
            halo[...] = jnp.zeros(halo.shape, F32)

        hb = jnp.dot(a_ref[...].astype(MXU), b_ref[...].astype(MXU), preferred_element_type=F32).astype(hid_ref.dtype)
        hid_ref[...] = hb
        cur = hb.astype(F32)
        hc = _conv_rows(cur, halo[...], w_ref[...], c_ref[...], K)
        halo[...] = cur[tm - SUBLANES:]
        hc_ref[...] = hc
        act_ref[...] = (_silu(hc[:, FFN_TC:]) * hc[:, :FFN_TC]).astype(act_ref.dtype)

    blk = pl.BlockSpec((tm, W2), lambda j, i: (i, j))
    return pl.pallas_call(
        body, grid=(nj, nt),
        in_specs=[pl.BlockSpec((tm, D), lambda j, i: (i, 0)), pl.BlockSpec((D, W2), lambda j, i: (0, j)),
                  pl.BlockSpec((K, W2), lambda j, i: (0, j)), pl.BlockSpec((1, W2), lambda j, i: (0, j))],
        out_specs=[blk, blk, pl.BlockSpec((tm, FFN_TC), lambda j, i: (i, j))],
        out_shape=[_sds((T, 2 * D_FF), MXU), _sds((T, 2 * D_FF)), _sds((T, D_FF), MXU)],
        scratch_shapes=[pltpu.VMEM((SUBLANES, W2), F32)],
        compiler_params=_cp("arbitrary", "arbitrary"), name=name)(hf, wup, cw, cb)


def _ffn_mid_bwd(hid, hc, cw, da, name):
    T = hid.shape[0]
    tm = min(T, 256)
    nt, nj = T // tm, D_FF // FFN_TC
    K = FFN_CONV
    W2 = 2 * FFN_TC

    def body(h_ref, c_ref, cn_ref, da_ref, dan_ref, w_ref, dh_ref, dw_ref, db_ref):
        i = pl.program_id(1)
        w = w_ref[...]
        cur = h_ref[...].astype(F32)
        last = i == nt - 1

        def dpre(hcv, dav):
            u, g = hcv[:, :FFN_TC], hcv[:, FFN_TC:]
            return jnp.concatenate([dav * _silu(g), dav * u * _dsilu(g)], axis=1)

        d_cur = dpre(c_ref[...], da_ref[...])
        d_nxt = jnp.where(last, 0.0, dpre(cn_ref[...], dan_ref[...]))
        ups = [d_cur] + [_shift_up(d_cur, d_nxt, s) for s in range(1, K)]
        dh = ups[0] * w[K - 1:K, :]
        for s in range(1, K):
            dh = dh + ups[s] * w[K - 1 - s:K - s, :]
        dh_ref[...] = dh.astype(dh_ref.dtype)
        dwp = jnp.concatenate([jnp.sum(ups[K - 1 - k] * cur, axis=0, keepdims=True) for k in range(K)], axis=0)
        dbp = jnp.sum(d_cur, axis=0, keepdims=True)

        @pl.when(i == 0)
        def _():
            dw_ref[...] = dwp
            db_ref[...] = dbp

        @pl.when(i > 0)
        def _():
            dw_ref[...] += dwp
            db_ref[...] += dbp

    q = tm // SUBLANES
    blk = pl.BlockSpec((tm, W2), lambda j, i: (i, j))
    nxt = pl.BlockSpec((SUBLANES, W2), lambda j, i: (jnp.minimum((i + 1) * q, nt * q - 1), j))
    dab = pl.BlockSpec((tm, FFN_TC), lambda j, i: (i, j))
    dan = pl.BlockSpec((SUBLANES, FFN_TC), lambda j, i: (jnp.minimum((i + 1) * q, nt * q - 1), j))
    return pl.pallas_call(
        body, grid=(nj, nt),
        in_specs=[blk, blk, nxt, dab, dan, pl.BlockSpec((K, W2), lambda j, i: (0, j))],
        out_specs=[blk, pl.BlockSpec((K, W2), lambda j, i: (0, j)), pl.BlockSpec((1, W2), lambda j, i: (0, j))],
        out_shape=[_sds((T, 2 * D_FF), MXU), _sds((K, 2 * D_FF)), _sds((1, 2 * D_FF))],
        compiler_params=_cp("parallel", "arbitrary"), name=name)(hid, hc, hc, da, da, cw)


def _rope(t, cos, sin_s, inverse=False):
    n = t.shape[1] // LANES
    c = jnp.concatenate([cos] * n, axis=1) if n > 1 else cos
    s = jnp.concatenate([sin_s] * n, axis=1) if n > 1 else sin_s
    a = pltpu.roll(t, HEAD_DIM // 2, 1)
    b = pltpu.roll(t, t.shape[1] - HEAD_DIM // 2, 1)
    first = (_iota(t.shape, 1) % HEAD_DIM) < HEAD_DIM // 2
    rot = jnp.where(first, b, a) * s
    return t * c - rot if inverse else t * c + rot


def _stack_heads(t, g):
    return jnp.concatenate([t[:, (GQ * g + r) * HEAD_DIM:(GQ * g + r + 1) * HEAD_DIM] for r in range(GQ)], axis=0)


def _stack_cols(t, g):
    return jnp.concatenate([t[:, GQ * g + r:GQ * g + r + 1] for r in range(GQ)], axis=0)


def _pool_sums(prev, cur, w):
    s = jnp.concatenate([prev, cur], axis=0)
    sh = 1
    while sh < w:
        s = s + pltpu.roll(s, sh, 0)
        sh *= 2
    return s[BLOCK:]


def _nt(a, b):
    return lax.dot_general(a.astype(MXU), b.astype(MXU), (((1,), (1,)), ((), ())), preferred_element_type=F32)


def _tn(a, b):
    return lax.dot_general(a.astype(MXU), b.astype(MXU), (((0,), (0,)), ((), ())), preferred_element_type=F32)


def _nn(a, b):
    return jnp.dot(a.astype(MXU), b.astype(MXU), preferred_element_type=F32)


def _mixcore_fwd(proj, cos, sin_s, pool_w, pool_scale, sinks, name):
    T = proj.shape[0]
    nb = T // BLOCK
    scale = HEAD_DIM ** -0.5

    def body(p_ref, pp_ref, c_ref, s_ref, cp_ref, sp_ref, pw_ref, ps_ref, sk_ref, cat_ref, at_ref, lse_ref):
        i = pl.program_id(0)
        has_prev = i > 0
        cur = p_ref[...]
        prv = jnp.where(has_prev, pp_ref[...], 0.0)
        tpos = (i * BLOCK + _iota((BLOCK, 1), 0) + 1).astype(F32)
        for g, w in enumerate(POOL_WINDOWS):
            sl = slice(g * POOL_GROUP, (g + 1) * POOL_GROUP)
            pooled = _pool_sums(prv[:, sl], cur[:, sl], w) / jnp.minimum(tpos, float(w)) - cur[:, sl]
            cat_ref[:, sl] = (_nn(pooled, pw_ref[g]) * ps_ref[:, sl]).astype(cat_ref.dtype)
        q = _rope(cur[:, POOL_DIM:POOL_DIM + Q_DIM], c_ref[...], s_ref[...])
        kc = _rope(cur[:, POOL_DIM + Q_DIM:POOL_DIM + Q_DIM + KV_DIM], c_ref[...], s_ref[...])
        kp = _rope(prv[:, POOL_DIM + Q_DIM:POOL_DIM + Q_DIM + KV_DIM], cp_ref[...], sp_ref[...])
        vc = cur[:, POOL_DIM + Q_DIM + KV_DIM:]
        vp = prv[:, POOL_DIM + Q_DIM + KV_DIM:]
        ri = _iota((GQ * BLOCK, BLOCK), 0) % BLOCK
        cj = _iota((GQ * BLOCK, BLOCK), 1)
        mc = cj <= ri
        mp = jnp.logical_and(cj > ri, has_prev)
        outs, lses = [], []
        for g in range(N_KV_HEADS):
            hs = slice(g * HEAD_DIM, (g + 1) * HEAD_DIM)
            qg = _stack_heads(q, g) * scale
            sc = jnp.where(mc, _nt(qg, kc[:, hs]), NEG)
            sp = jnp.where(mp, _nt(qg, kp[:, hs]), NEG)
            sink = jnp.concatenate([jnp.full((BLOCK, 1), sk_ref[GQ * g + r], F32) for r in range(GQ)], axis=0)
            m = jnp.maximum(jnp.maximum(jnp.max(sc, axis=1, keepdims=True), jnp.max(sp, axis=1, keepdims=True)), sink)
            pc = jnp.exp(sc - m)
            pp = jnp.exp(sp - m)
            den = jnp.sum(pc, axis=1, keepdims=True) + jnp.sum(pp, axis=1, keepdims=True) + jnp.exp(sink - m)
            o = (_nn(pc, vc[:, hs]) + _nn(pp, vp[:, hs])) / den
            lse = m + jnp.log(den)
            for r in range(GQ):
                outs.append(o[r * BLOCK:(r + 1) * BLOCK])
                lses.append(lse[r * BLOCK:(r + 1) * BLOCK])
        attn = jnp.concatenate(outs, axis=1)
        at_ref[...] = attn
        cat_ref[:, POOL_DIM:] = attn.astype(cat_ref.dtype)
        lane = _iota((BLOCK, LANES), 1)
        lrow = jnp.zeros((BLOCK, LANES), F32)
        for h in range(N_HEADS):
            lrow = jnp.where(lane == h, lses[h], lrow)
        lse_ref[...] = lrow

    cur = lambda w: pl.BlockSpec((BLOCK, w), lambda i: (i, 0))
    prv = lambda w: pl.BlockSpec((BLOCK, w), lambda i: (jnp.maximum(i - 1, 0), 0))
    return pl.pallas_call(
        body, grid=(nb,),
        in_specs=[cur(MIX_IN_DIM), prv(MIX_IN_DIM), cur(LANES), cur(LANES), prv(LANES), prv(LANES),
                  pl.BlockSpec((4, POOL_GROUP, POOL_GROUP), lambda i: (0, 0, 0)), pl.BlockSpec((1, POOL_DIM), lambda i: (0, 0)),
                  pl.BlockSpec(memory_space=pltpu.SMEM)],
        out_specs=[cur(2 * POOL_DIM), cur(Q_DIM), cur(LANES)],
        out_shape=[_sds((T, 2 * POOL_DIM), MXU), _sds((T, Q_DIM)), _sds((T, LANES))],
        compiler_params=_cp("parallel"), name=name)(proj, proj, cos, sin_s, cos, sin_s, pool_w, pool_scale, sinks)


def _mixcore_bwd(proj, cos, sin_s, pool_w, pool_scale, sinks, attn, lse, dcat, name):
    T = proj.shape[0]
    nb = T // BLOCK
    scale = HEAD_DIM ** -0.5
    QO, KO, VO = POOL_DIM, POOL_DIM + Q_DIM, POOL_DIM + Q_DIM + KV_DIM

    def body(p_ref, pp_ref, pn_ref, c_ref, s_ref, cp_ref, sp_ref, cn_ref, sn_ref, pw_ref, ps_ref, sk_ref,
             at_ref, atn_ref, l_ref, ln_ref, d_ref, dn_ref, dp_ref, dpw_ref, dps_ref, dsk_ref):
        i = pl.program_id(0)
        has_prev = i > 0
        has_next = i < nb - 1
        cur = p_ref[...]
        prv = jnp.where(has_prev, pp_ref[...], 0.0)
        d_cur = d_ref[...]
        d_nxt = jnp.where(has_next, dn_ref[...], 0.0)

        tpos = (i * BLOCK + _iota((BLOCK, 1), 0) + 1).astype(F32)
        tpos2 = (i * BLOCK + _iota((2 * BLOCK, 1), 0) + 1).astype(F32)
        ps = ps_ref[...]
        dps_parts, dpw_parts = [], []
        for g, w in enumerate(POOL_WINDOWS):
            sl = slice(g * POOL_GROUP, (g + 1) * POOL_GROUP)
            pooled = _pool_sums(prv[:, sl], cur[:, sl], w) / jnp.minimum(tpos, float(w)) - cur[:, sl]
            mixed = _nn(pooled, pw_ref[g])
            dps_parts.append(jnp.sum(d_cur[:, sl] * mixed, axis=0, keepdims=True))
            dm2 = jnp.concatenate([d_cur[:, sl], d_nxt[:, sl]], axis=0) * ps[:, sl]
            dpw_parts.append(_tn(pooled, dm2[:BLOCK]))
            dpool2 = _nt(dm2, pw_ref[g])
            e = dpool2 / jnp.minimum(tpos2, float(w))
            sh = 1
            while sh < w:
                e = e + pltpu.roll(e, 2 * BLOCK - sh, 0)
                sh *= 2
            dp_ref[:, sl] = (e[:BLOCK] - dpool2[:BLOCK]).astype(dp_ref.dtype)
        dpsp = jnp.concatenate(dps_parts, axis=1)

        nxt = pn_ref[...]
        q = _rope(cur[:, QO:KO], c_ref[...], s_ref[...])
        qn = _rope(nxt[:, QO:KO], cn_ref[...], sn_ref[...])
        kc = _rope(cur[:, KO:VO], c_ref[...], s_ref[...])
        kp = _rope(prv[:, KO:VO], cp_ref[...], sp_ref[...])
        vc, vp = cur[:, VO:], prv[:, VO:]
        do, don = d_cur[:, POOL_DIM:], d_nxt[:, POOL_DIM:]
        dl = do * at_ref[...]
        dln = don * atn_ref[...]
        lse, lsen = l_ref[...], ln_ref[...]
        ri = _iota((GQ * BLOCK, BLOCK), 0) % BLOCK
        cj = _iota((GQ * BLOCK, BLOCK), 1)
        mc = cj <= ri
        mp = jnp.logical_and(cj > ri, has_prev)
        mn = jnp.logical_and(cj > ri, has_next)
        dq_parts, dk_parts, dv_parts, dsk_vals = [], [], [], []
        for g in range(N_KV_HEADS):
            hs = slice(g * HEAD_DIM, (g + 1) * HEAD_DIM)
            qg, qng = _stack_heads(q, g) * scale, _stack_heads(qn, g) * scale
            dog, dong = _stack_heads(do, g), _stack_heads(don, g)
            delta = jnp.sum(_stack_heads(dl, g), axis=1, keepdims=True)
            deltan = jnp.sum(_stack_heads(dln, g), axis=1, keepdims=True)
            lg, lng = _stack_cols(lse, g), _stack_cols(lsen, g)
            pc = jnp.where(mc, jnp.exp(_nt(qg, kc[:, hs]) - lg), 0.0)
            pp = jnp.where(mp, jnp.exp(_nt(qg, kp[:, hs]) - lg), 0.0)
            pn = jnp.where(mn, jnp.exp(_nt(qng, kc[:, hs]) - lng), 0.0)
            dsc = pc * (_nt(dog, vc[:, hs]) - delta)
            dsp = pp * (_nt(dog, vp[:, hs]) - delta)
            dsn = pn * (_nt(dong, vc[:, hs]) - deltan)
            dqg = (_nn(dsc, kc[:, hs]) + _nn(dsp, kp[:, hs])) * scale
            dq_parts += [dqg[r * BLOCK:(r + 1) * BLOCK] for r in range(GQ)]
            dk_parts.append(_tn(dsc, qg) + _tn(dsn, qng))
            dv_parts.append(_tn(pc, dog) + _tn(pn, dong))
            sink = jnp.concatenate([jnp.full((BLOCK, 1), sk_ref[GQ * g + r], F32) for r in range(GQ)], axis=0)
            dsk = -jnp.exp(sink - lg) * delta
            dsk_vals += [jnp.sum(dsk[r * BLOCK:(r + 1) * BLOCK], axis=0, keepdims=True) for r in range(GQ)]
        dq = _rope(jnp.concatenate(dq_parts, axis=1), c_ref[...], s_ref[...], inverse=True)
        dk = _rope(jnp.concatenate(dk_parts, axis=1), c_ref[...], s_ref[...], inverse=True)
        dp_ref[:, QO:KO] = dq.astype(dp_ref.dtype)
        dp_ref[:, KO:VO] = dk.astype(dp_ref.dtype)
        dp_ref[:, VO:] = jnp.concatenate(dv_parts, axis=1).astype(dp_ref.dtype)
        lane = _iota((1, LANES), 1)
        dskp = jnp.zeros((1, LANES), F32)
        for h in range(N_HEADS):
            dskp = jnp.where(lane == h, dsk_vals[h], dskp)

        @pl.when(i == 0)
        def _():
            dps_ref[...] = dpsp
            dsk_ref[...] = dskp
            for g in range(4):
                dpw_ref[g] = dpw_parts[g]

        @pl.when(i > 0)
        def _():
            dps_ref[...] += dpsp
            dsk_ref[...] += dskp
            for g in range(4):
                dpw_ref[g] += dpw_parts[g]

    cur = lambda w: pl.BlockSpec((BLOCK, w), lambda i: (i, 0))
    prv = lambda w: pl.BlockSpec((BLOCK, w), lambda i: (jnp.maximum(i - 1, 0), 0))
    nxt = lambda w: pl.BlockSpec((BLOCK, w), lambda i: (jnp.minimum(i + 1, nb - 1), 0))
    return pl.pallas_call(
        body, grid=(nb,),
        in_specs=[cur(MIX_IN_DIM), prv(MIX_IN_DIM), nxt(MIX_IN_DIM),
                  cur(LANES), cur(LANES), prv(LANES), prv(LANES), nxt(LANES), nxt(LANES),
                  pl.BlockSpec((4, POOL_GROUP, POOL_GROUP), lambda i: (0, 0, 0)), pl.BlockSpec((1, POOL_DIM), lambda i: (0, 0)),
                  pl.BlockSpec(memory_space=pltpu.SMEM),
                  cur(Q_DIM), nxt(Q_DIM), cur(LANES), nxt(LANES), cur(2 * POOL_DIM), nxt(2 * POOL_DIM)],
        out_specs=[cur(MIX_IN_DIM), pl.BlockSpec((4, POOL_GROUP, POOL_GROUP), lambda i: (0, 0, 0)),
                   pl.BlockSpec((1, POOL_DIM), lambda i: (0, 0)), pl.BlockSpec((1, LANES), lambda i: (0, 0))],
        out_shape=[_sds((T, MIX_IN_DIM), MXU), _sds((4, POOL_GROUP, POOL_GROUP)), _sds((1, POOL_DIM)), _sds((1, LANES))],
        compiler_params=_cp("arbitrary"), name=name)(
            proj, proj, proj, cos, sin_s, cos, sin_s, cos, sin_s, pool_w, pool_scale, sinks, attn, attn, lse, lse, dcat, dcat)


SSM_TC = 512
GROUP_W = SSM_D_INNER // SSM_GROUPS


def _ssm_pre_fwd(xbc, cw, cb, name):
    T = xbc.shape[0]
    tm = min(T, 1024)
    K = SSM_CONV
    q = tm // SUBLANES

    def body(x_ref, xp_ref, w_ref, b_ref, o_ref, pre_ref):
        prev8 = jnp.where(pl.program_id(0) > 0, xp_ref[...], 0.0)
        pre = _conv_rows(x_ref[...], prev8, w_ref[...], b_ref[...], K)
        pre_ref[...] = pre
        o_ref[...] = _silu(pre)

    tc = 512
    blk = pl.BlockSpec((tm, tc), lambda i, j: (i, j))
    return pl.pallas_call(
        body, grid=(T // tm, SSM_CONV_DIM // tc),
        in_specs=[blk, pl.BlockSpec((SUBLANES, tc), lambda i, j: (jnp.maximum(i * q - 1, 0), j)),
                  pl.BlockSpec((K, tc), lambda i, j: (0, j)), pl.BlockSpec((1, tc), lambda i, j: (0, j))],
        out_specs=[blk, blk], out_shape=[_sds((T, SSM_CONV_DIM)), _sds((T, SSM_CONV_DIM))],
        compiler_params=_cp("parallel", "parallel"), name=name)(xbc, xbc, cw, cb)


def _ssm_pre_bwd(xbc, pre, cw, dact, name):
    T = xbc.shape[0]
    tm = min(T, 512)
    nt = T // tm
    K = SSM_CONV
    q = tm // SUBLANES
    tc = SSM_TC

    def body(x_ref, p_ref, pn_ref, d_ref, dn_ref, w_ref, dx_ref, dw_ref, db_ref):
        i = pl.program_id(1)
        w = w_ref[...]
        cur = x_ref[...]
        d_cur = d_ref[...] * _dsilu(p_ref[...])
        d_nxt = jnp.where(i == nt - 1, 0.0, dn_ref[...] * _dsilu(pn_ref[...]))
        ups = [d_cur] + [_shift_up(d_cur, d_nxt, s) for s in range(1, K)]
        dx = ups[0] * w[K - 1:K, :]
        for s in range(1, K):
            dx = dx + ups[s] * w[K - 1 - s:K - s, :]
        dx_ref[...] = dx.astype(dx_ref.dtype)
        dwp = jnp.concatenate([jnp.sum(ups[K - 1 - k] * cur, axis=0, keepdims=True) for k in range(K)], axis=0)
        dbp = jnp.sum(d_cur, axis=0, keepdims=True)

        @pl.when(i == 0)
        def _():
            dw_ref[...] = dwp
            db_ref[...] = dbp

        @pl.when(i > 0)
        def _():
            dw_ref[...] += dwp
            db_ref[...] += dbp

    nxt_row = lambda i: jnp.minimum((i + 1) * q, nt * q - 1)
    return pl.pallas_call(
        body, grid=(SSM_CONV_DIM // tc, nt),
        in_specs=[pl.BlockSpec((tm, tc), lambda j, i: (i, j)),
                  pl.BlockSpec((tm, tc), lambda j, i: (i, j)),
                  pl.BlockSpec((SUBLANES, tc), lambda j, i: (nxt_row(i), j)),
                  pl.BlockSpec((tm, tc), lambda j, i: (i, j)),
                  pl.BlockSpec((SUBLANES, tc), lambda j, i: (nxt_row(i), j)),
                  pl.BlockSpec((K, tc), lambda j, i: (0, j))],
        out_specs=[pl.BlockSpec((tm, tc), lambda j, i: (i, j)), pl.BlockSpec((K, tc), lambda j, i: (0, j)),
                   pl.BlockSpec((1, tc), lambda j, i: (0, j))],
        out_shape=[_sds((T, SSM_CONV_DIM), MXU), _sds((K, SSM_CONV_DIM)), _sds((1, SSM_CONV_DIM))],
        compiler_params=_cp("parallel", "arbitrary"), name=name)(xbc, pre, pre, dact, dact, cw)


def _dot_hi(a, b):
    return jnp.dot(a, b, precision=HI, preferred_element_type=F32)


def _ssd_common(dtraw, bias, alog):
    L = SSM_CHUNK
    xb = dtraw + bias
    dt = jnp.maximum(xb, 0.0) + jnp.log1p(jnp.exp(-jnp.abs(xb)))
    A = -jnp.exp(alog)
    tril = (_iota((L, L), 1) <= _iota((L, L), 0)).astype(F32)
    acs = _dot_hi(tril, dt * A)
    return xb, dt, A, tril, acs


def _head_selectors():
    es = (_iota((LANES, SSM_D_INNER), 0) == _iota((LANES, SSM_D_INNER), 1) // HEAD_DIM).astype(BF16)
    est = (_iota((SSM_D_INNER, LANES), 1) == _iota((SSM_D_INNER, LANES), 0) // HEAD_DIM).astype(BF16)
    return es, est


def _dot_sel(v, sel):
    hi = v.astype(BF16)
    r1 = v - hi.astype(F32)
    mid = r1.astype(BF16)
    lo = (r1 - mid.astype(F32)).astype(BF16)
    d = lambda a: jnp.dot(a, sel, preferred_element_type=F32)
    return (d(hi) + d(mid)) + d(lo)


def _expand_heads(v, es):
    return _dot_sel(v, es)


def _reduce_heads(q, est):
    return _dot_sel(q, est)


def _per_state_row(v, g):
    return jnp.concatenate([jnp.broadcast_to(v[:, GQ * g + r:GQ * g + r + 1], (HEAD_DIM, 1)) for r in range(GQ)], axis=0)


def _ssd_fwd(xact, dtraw, dt_bias, a_log, name):
    T = xact.shape[0]
    nc = T // SSM_CHUNK
    L = SSM_CHUNK
    BO, CO = SSM_D_INNER, SSM_D_INNER + SSM_GROUPS * SSM_STATE

    def body(x_ref, dt_ref, bias_ref, al_ref, es_ref, y_ref, st_ref, state):
        @pl.when(pl.program_id(0) == 0)
        def _():
            state[...] = jnp.zeros(state.shape, F32)

        _, dt, A, tril, acs = _ssd_common(dt_ref[...], bias_ref[...], al_ref[...])
        acsT = acs.T
        last = acs[L - 1:L, :]
        cd = jnp.exp(last)
        es = es_ref[...]
        dtX = _expand_heads(dt, es)
        EX = _expand_heads(jnp.exp(acs), es)
        decX = _expand_heads(jnp.exp(last - acs), es)
        for g in range(SSM_GROUPS):
            gs = slice(g * GROUP_W, (g + 1) * GROUP_W)
            B = x_ref[:, BO + g * SSM_STATE:BO + (g + 1) * SSM_STATE]
            C = x_ref[:, CO + g * SSM_STATE:CO + (g + 1) * SSM_STATE]
            X = x_ref[:, gs] * dtX[:, gs]
            CB = _nt(C, B)
            yd = []
            for r in range(GQ):
                h = GQ * g + r
                Lm = jnp.exp(jnp.where(tril > 0, acs[:, h:h + 1] - acsT[h:h + 1, :], NEG))
                yd.append(_nn(CB * Lm, X[:, r * HEAD_DIM:(r + 1) * HEAD_DIM]))
            S = state[g]
            st_ref[g] = S
            y_ref[:, gs] = jnp.concatenate(yd, axis=1) + _nt(C, S) * EX[:, gs]
            state[g] = S * _per_state_row(cd, g) + _tn(X * decX[:, gs], B)

    es, _ = _head_selectors()
    return pl.pallas_call(
        body, grid=(nc,),
        in_specs=[pl.BlockSpec((L, SSM_CONV_DIM), lambda c: (c, 0)), pl.BlockSpec((L, LANES), lambda c: (c, 0)),
                  pl.BlockSpec((1, LANES), lambda c: (0, 0)), pl.BlockSpec((1, LANES), lambda c: (0, 0)),
                  pl.BlockSpec((LANES, SSM_D_INNER), lambda c: (0, 0))],
        out_specs=[pl.BlockSpec((L, SSM_D_INNER), lambda c: (c, 0)),
                   pl.BlockSpec((None, SSM_GROUPS, GROUP_W, SSM_STATE), lambda c: (c, 0, 0, 0))],
        out_shape=[_sds((T, SSM_D_INNER)), _sds((nc, SSM_GROUPS, GROUP_W, SSM_STATE))],
        scratch_shapes=[pltpu.VMEM((SSM_GROUPS, GROUP_W, SSM_STATE), F32)],
        compiler_params=_cp("arbitrary"), name=name)(xact, dtraw, dt_bias, a_log, es)


def _ssd_bwd(xact, dtraw, dt_bias, a_log, d_skip, states, dy, name):
    T = xact.shape[0]
    nc = T // SSM_CHUNK
    L = SSM_CHUNK
    BO, CO = SSM_D_INNER, SSM_D_INNER + SSM_GROUPS * SSM_STATE

    def body(x_ref, dt_ref, bias_ref, al_ref, dsk_ref, es_ref, est_ref, st_ref, dy_ref,
             dxp_ref, ddt_ref, dbias_ref, dal_ref, dd_ref, dstate, qa, qx):
        cc = pl.program_id(0)

        @pl.when(cc == 0)
        def _():
            dstate[...] = jnp.zeros(dstate.shape, F32)

        xb, dt, A, tril, acs = _ssd_common(dt_ref[...], bias_ref[...], al_ref[...])
        acsT = acs.T
        last = acs[L - 1:L, :]
        cd = jnp.exp(last)
        es, est = es_ref[...], est_ref[...]
        dtX = _expand_heads(dt, es)
        EX = _expand_heads(jnp.exp(acs), es)
        decX = _expand_heads(jnp.exp(last - acs), es)
        lane1 = _iota((1, LANES), 1)
        lane = _iota((L, LANES), 1)
        sub = _iota((L, LANES), 0)
        ztot = jnp.zeros((1, LANES), F32)
        wrow = jnp.zeros((L, LANES), F32)
        wcolT = jnp.zeros((LANES, L), F32)
        rows_dec, rows_dd = [], []
        for g in range(SSM_GROUPS):
            gs = slice(g * GROUP_W, (g + 1) * GROUP_W)
            x = x_ref[:, gs]
            B = x_ref[:, BO + g * SSM_STATE:BO + (g + 1) * SSM_STATE]
            C = x_ref[:, CO + g * SSM_STATE:CO + (g + 1) * SSM_STATE]
            dY = dy_ref[:, gs]
            dtx, e_x, dec_x = dtX[:, gs], EX[:, gs], decX[:, gs]
            X = x * dtx
            CB = _nt(C, B)
            S = st_ref[g]
            dS_out = dstate[g]
            dcb_sum = jnp.zeros((L, L), F32)
            dxd = []
            for r in range(GQ):
                h = GQ * g + r
                hs = slice(r * HEAD_DIM, (r + 1) * HEAD_DIM)
                Lm = jnp.exp(jnp.where(tril > 0, acs[:, h:h + 1] - acsT[h:h + 1, :], NEG))
                M = CB * Lm
                dM = _nt(dY[:, hs], X[:, hs])
                dxd.append(_tn(M, dY[:, hs]))
                dcb_sum = dcb_sum + dM * Lm
                Wm = dM * M
                wrow = jnp.where(lane == h, jnp.sum(Wm, axis=1, keepdims=True), wrow)
                wcolT = jnp.where(sub == h, jnp.sum(Wm, axis=0, keepdims=True), wcolT)
            dXd = jnp.concatenate(dxd, axis=1)
            G = _nt(C, S)
            dG = dY * e_x
            dDX = _nt(B, dS_out)
            dX = dXd + dec_x * dDX
            t_dec = dDX * X * dec_x
            qa[:, gs] = dG * G - t_dec
            qx[:, gs] = dX * x
            rows_dec.append(jnp.sum(t_dec, axis=0, keepdims=True))
            rows_dd.append(jnp.sum(dY * x, axis=0, keepdims=True))
            zc = jnp.sum(dS_out * S, axis=1, keepdims=True)
            for r in range(GQ):
                ztot = jnp.where(lane1 == GQ * g + r, jnp.sum(zc[r * HEAD_DIM:(r + 1) * HEAD_DIM], axis=0, keepdims=True), ztot)
            dxp_ref[:, gs] = dX * dtx + dY * dsk_ref[:, gs]
            dxp_ref[:, BO + g * SSM_STATE:BO + (g + 1) * SSM_STATE] = _tn(dcb_sum, C) + _nn(X * dec_x, dS_out)
            dxp_ref[:, CO + g * SSM_STATE:CO + (g + 1) * SSM_STATE] = _nn(dcb_sum, B) + _nn(dG, S)
            dstate[g] = dS_out * _per_state_row(cd, g) + _tn(dG, C)
        rows = jnp.concatenate([jnp.concatenate(rows_dec, axis=1), jnp.concatenate(rows_dd, axis=1)]
                               + [jnp.zeros((SUBLANES - 2, SSM_D_INNER), F32)], axis=0)
        rsum = _reduce_heads(rows, est)
        dlast = rsum[0:1, :] + cd * ztot
        dacs = (wrow - wcolT.T) + _reduce_heads(qa[...], est) + jnp.where(sub == L - 1, dlast, 0.0)
        triu = (_iota((L, L), 0) <= _iota((L, L), 1)).astype(F32)
        da = _dot_hi(triu, dacs)
        ddtraw = (da * A + _reduce_heads(qx[...], est)) * (1.0 / (1.0 + jnp.exp(-xb)))
        ddt_ref[...] = ddtraw
        dal = jnp.sum(da * dt, axis=0, keepdims=True) * A
        ddp = rsum[1:2, :]
        dbp = jnp.sum(ddtraw, axis=0, keepdims=True)

        @pl.when(cc == 0)
        def _():
            dbias_ref[...] = dbp
            dal_ref[...] = dal
            dd_ref[...] = ddp

        @pl.when(cc > 0)
        def _():
            dbias_ref[...] += dbp
            dal_ref[...] += dal
            dd_ref[...] += ddp

    rc = lambda c: nc - 1 - c
    vec = pl.BlockSpec((1, LANES), lambda c: (0, 0))
    es, est = _head_selectors()
    return pl.pallas_call(
        body, grid=(nc,),
        in_specs=[pl.BlockSpec((L, SSM_CONV_DIM), lambda c: (rc(c), 0)), pl.BlockSpec((L, LANES), lambda c: (rc(c), 0)), vec, vec,
                  pl.BlockSpec((1, SSM_D_INNER), lambda c: (0, 0)),
                  pl.BlockSpec((LANES, SSM_D_INNER), lambda c: (0, 0)), pl.BlockSpec((SSM_D_INNER, LANES), lambda c: (0, 0)),
                  pl.BlockSpec((None, SSM_GROUPS, GROUP_W, SSM_STATE), lambda c: (rc(c), 0, 0, 0)),
                  pl.BlockSpec((L, SSM_D_INNER), lambda c: (rc(c), 0))],
        out_specs=[pl.BlockSpec((L, SSM_CONV_DIM), lambda c: (rc(c), 0)),
                   pl.BlockSpec((L, LANES), lambda c: (rc(c), 0)), vec, vec, vec],
        out_shape=[_sds((T, SSM_CONV_DIM)), _sds((T, LANES)), _sds((1, LANES)), _sds((1, LANES)), _sds((1, LANES))],
        scratch_shapes=[pltpu.VMEM((SSM_GROUPS, GROUP_W, SSM_STATE), F32), pltpu.VMEM((L, SSM_D_INNER), F32),
                        pltpu.VMEM((L, SSM_D_INNER), F32)],
        compiler_params=_cp("arbitrary"), name=name)(xact, dtraw, dt_bias, a_log, d_skip, es, est, states, dy)


def _ssm_post_fwd(y, xact, z, d_skip, nw, name):
    T = y.shape[0]
    tm = min(T, 256)
    W = SSM_D_INNER

    def body(y_ref, x_ref, z_ref, d_ref, w_ref, o_ref):
        y2 = (y_ref[...] + d_ref[...] * x_ref[...]) * _silu(z_ref[...])
        r = lax.rsqrt(jnp.mean(y2 * y2, axis=-1, keepdims=True) + SSM_NORM_EPS)
        o_ref[...] = (y2 * r * w_ref[...]).astype(o_ref.dtype)

    row = pl.BlockSpec((tm, W), lambda i: (i, 0))
    vec = pl.BlockSpec((1, W), lambda i: (0, 0))
    return pl.pallas_call(
        body, grid=(T // tm,), in_specs=[row, row, row, vec, vec], out_specs=row, out_shape=_sds((T, W), MXU),
        compiler_params=_cp("parallel"), name=name)(y, xact, z, d_skip, nw)


def _ssm_post_bwd(y, xact, z, d_skip, nw, dyn, name):
    T = y.shape[0]
    tm = min(T, 256)
    W = SSM_D_INNER

    def body(y_ref, x_ref, z_ref, d_ref, w_ref, dn_ref, dyg_ref, dz_ref, dw_ref):
        zv = z_ref[...]
        sz = _silu(zv)
        yg = y_ref[...] + d_ref[...] * x_ref[...]
        y2 = yg * sz
        r = lax.rsqrt(jnp.mean(y2 * y2, axis=-1, keepdims=True) + SSM_NORM_EPS)
        y2h = y2 * r
        dn = dn_ref[...]
        gy = dn * w_ref[...]
        dy2 = r * (gy - y2h * jnp.mean(gy * y2h, axis=-1, keepdims=True))
        dyg_ref[...] = dy2 * sz
        dz_ref[...] = (dy2 * yg * _dsilu(zv)).astype(dz_ref.dtype)
        part = jnp.sum(dn * y2h, axis=0, keepdims=True)

        @pl.when(pl.program_id(0) == 0)
        def _():
            dw_ref[...] = part

        @pl.when(pl.program_id(0) > 0)
        def _():
            dw_ref[...] += part

    row = pl.BlockSpec((tm, W), lambda i: (i, 0))
    vec = pl.BlockSpec((1, W), lambda i: (0, 0))
    return pl.pallas_call(
        body, grid=(T // tm,), in_specs=[row, row, row, vec, vec, row], out_specs=[row, row, vec],
        out_shape=[_sds((T, W)), _sds((T, W), MXU), _sds((1, W))],
        compiler_params=_cp("arbitrary"), name=name)(y, xact, z, d_skip, nw, dyn)


def _local_step(x0, cos, sin_s, target, P, fetch, token, send):
    mmf = functools.partial(_mm, tm=1024)
    big, small = {}, {}
    P = dict(P, wup={}, wdn={}, fcw={})
    h0 = _rmsnorm_fwd(x0, P["nm"][0], "norm_mix0", token=token)
    proj0 = mmf(h0, P["wmiT"], tb=True, tn=1280, tk=1024, name="mix_in")
    cat, attn, lse = _mixcore_fwd(proj0, cos, sin_s, P["pool_w"], P["pool_scale"], P["sinks"], "mixcore_fwd")
    x1, hf0 = mmf(cat, P["wmo"], tn=1024, tk=1024, res=x0, norm_w=P["nf"][0], name="mix_out")

    def ffn_fwd(xin, hf, i, next_norm):
        got = fetch(f"ffn{i}", hf)
        P["wup"][i], P["wdn"][i], P["fcw"][i] = got["wup"], got["wdn"], got["fcw"]
        hid, hc, act = _ffn_up_conv_gate(hf, P["wup"][i], P["fcw"][i], P["fcb"][i], f"ffn_up{i}")
        xout = mmf(act, P["wdn"][i], tn=1024, tk=D_FF, res=xin, norm_w=next_norm, name=f"ffn_down{i}")
        return (hid, hc), act, xout

    hid0, act0, (x2, h1) = ffn_fwd(x1, hf0, 0, P["nm"][1])
    P.update(fetch("ssm", h1))
    z = mmf(h1, P["wzT"], tb=True, tn=1024, tk=1024, name="ssm_in_z")
    xbc = mmf(h1, P["wxbcT"], tb=True, tn=1024, tk=1024, name="ssm_in_xbc")
    dtraw = mmf(h1, P["wdtT"], tb=True, tn=128, tk=1024, name="ssm_in_dt")
    xact, xpre = _ssm_pre_fwd(xbc, P["scw"], P["scb"], "ssm_pre_fwd")
    y, states = _ssd_fwd(xact, dtraw, P["dt_bias"], P["a_log"], "ssd_fwd")
    yn = _ssm_post_fwd(y, xact, z, P["d_exp"], P["snorm"], "ssm_post_fwd")
    x3, hf1 = mmf(yn, P["wso"], tn=1024, tk=SSM_D_INNER, res=x2, norm_w=P["nf"][1], name="ssm_out")
    hid1, act1, x4 = ffn_fwd(x3, hf1, 1, None)
    loss_row, dx4, d_nfin = _loss_head(x4, P["nfin"], target, "loss_head")
    small["norm_final"] = d_nfin

    def ffn_bwd(xin, dxo, hf, hid, act, i):
        da = mmf(dxo, P["wdn"][i], tb=True, tn=1408, tk=1024, name=f"ffn_down_dx{i}")
        big[f"ffn_w_down{i}"] = dwf(act, dxo, tm=1408, tn=1024, name=f"ffn_down_dw{i}").reshape(N_CHIPS, D_FF // N_CHIPS, D_MODEL)
        dhid, dcw, dcb = _ffn_mid_bwd(hid[0], hid[1], P["fcw"][i], da, f"ffn_mid_bwd{i}")
        big[f"ffn_w_up{i}"] = dwf(hf, dhid, tm=1024, tn=1408, out_shard_perm=(0, 2, 1, 3), name=f"ffn_up_dw{i}")
        tok = send(f"ffn{i}", [big[f"ffn_w_up{i}"], big[f"ffn_w_down{i}"]])
        dxi, dnf = _mm(dhid, P["wup"][i], tb=True, tm=512, tn=1024, tk=2816, norm_bwd=(xin, P["nf"][i], dxo, tok), name=f"ffn_up_dx{i}")
        return dxi, dnf, dcw, dcb

    dwf = functools.partial(_mm, ta=True, tk=2048, out_dtype=BF16)
    dx3, dnf1, dfcw1, dfcb1 = ffn_bwd(x3, dx4, hf1, hid1, act1, 1)
    dyn = mmf(dx3, P["wso"], tb=True, tn=1024, tk=1024, name="ssm_out_dx")
    big["ssm_w_out"] = dwf(yn, dx3, tm=1024, tn=1024, name="ssm_out_dw").reshape(N_CHIPS, SSM_D_INNER // N_CHIPS, D_MODEL)
    dyg, dz, d_snorm = _ssm_post_bwd(y, xact, z, P["d_exp"], P["snorm"], dyn, "ssm_post_bwd")
    dxact_p, ddtraw, d_dtb, d_alog, d_dskip = _ssd_bwd(xact, dtraw, P["dt_bias"], P["a_log"], P["d_exp"], states, dyg, "ssd_bwd")
    dxbc, d_scw, d_scb = _ssm_pre_bwd(xbc, xpre, P["scw"], dxact_p, "ssm_pre_bwd")
    dwsi = dwf(dz, h1, tm=1024, tn=1024, out_into=(None, SSM_IN_DIM, 0), name="ssm_in_dw_z")
    dwsi = dwf(dxbc, h1, tm=1024, tn=1024, out_into=(dwsi, SSM_IN_DIM, SSM_D_INNER // 1024), name="ssm_in_dw_xbc")
    dwdt = dwf(ddtraw, h1, tm=128, tn=1024, name="ssm_in_dw_dt")
    dwsi = _put_rows(dwsi, dwdt, SSM_HEADS, SSM_D_INNER + SSM_CONV_DIM, "ssm_in_dw_put_dt")
    big["ssm_w_in"] = dwsi.reshape(N_CHIPS, SSM_IN_DIM // N_CHIPS, D_MODEL)
    tok = send("ssm", [big["ssm_w_in"], big["ssm_w_out"]])
    dh1 = mmf(dz, P["wzT"], tn=1024, tk=2048, name="ssm_in_dx_z")
    dh1 = mmf(dxbc, P["wxbcT"], tn=1024, tk=2048, res=dh1, name="ssm_in_dx_xbc")
    dx2, dnm1 = mmf(ddtraw, P["wdtT"], tn=1024, tk=128, res=dh1, norm_bwd=(x2, P["nm"][1], dx3, tok), name="ssm_in_dx_dt")
    dx1, dnf0, dfcw0, dfcb0 = ffn_bwd(x1, dx2, hf0, hid0, act0, 0)
    dcat = mmf(dx1, P["wmo"], tb=True, tn=1024, tk=1024, name="mix_out_dx")
    big["mix_w_out"] = dwf(cat, dx1, tm=1024, tn=1024, name="mix_out_dw").reshape(N_CHIPS, D_MODEL // N_CHIPS, D_MODEL)
    dproj0, d_pw, d_ps, d_sk = _mixcore_bwd(proj0, cos, sin_s, P["pool_w"], P["pool_scale"], P["sinks"], attn, lse, dcat, "mixcore_bwd")
    big["mix_w_in"] = dwf(dproj0, h0, tm=1280, tn=1024, name="mix_in_dw").reshape(N_CHIPS, MIX_IN_DIM // N_CHIPS, D_MODEL)
    tok = send("mix", [big["mix_w_in"], big["mix_w_out"]])
    dx0, dnm0 = mmf(dproj0, P["wmiT"], tn=1024, tk=1280, norm_bwd=(x0, P["nm"][0], dx1, tok), name="mix_in_dx")

    def unperm_cols(a):
        r = a.shape[0]
        t = a.reshape(r, N_CHIPS, FFN_TC)
        return jnp.stack([t[:, p] for p in _PERM], axis=0)

    small["norm_mix"] = jnp.concatenate([dnm0, dnm1], axis=0)
    small["norm_ffn"] = jnp.concatenate([dnf0, dnf1], axis=0)
    small["pool_w"] = d_pw.reshape(4 * POOL_GROUP, POOL_GROUP)
    small["pool_scale"] = d_ps
    small["attn_sinks"] = d_sk
    small["ssm_dt_bias"] = d_dtb
    small["ssm_A_log"] = d_alog
    small["ssm_D"] = d_dskip
    fcb = jnp.stack([unperm_cols(dfcb0), unperm_cols(dfcb1)], axis=0)
    small["ffn_conv_b"] = fcb.reshape(2, 2 * D_FF)
    small["ssm_conv_w"] = d_scw.reshape(SSM_CONV, N_CHIPS, SSM_CONV_DIM // N_CHIPS).transpose(1, 0, 2)
    small["ssm_conv_b"] = d_scb.reshape(N_CHIPS, 1, SSM_CONV_DIM // N_CHIPS)
    small["ssm_norm"] = d_snorm.reshape(N_CHIPS, 1, SSM_D_INNER // N_CHIPS)
    small["ffn_conv_w"] = jnp.concatenate([unperm_cols(dfcw0), unperm_cols(dfcw1)], axis=1)
    return loss_row, dx0, big, small


ANY = pl.BlockSpec(memory_space=pl.ANY)


def _place():
    return lax.axis_index("x"), lax.axis_index("y"), lax.axis_index("c")


def _gather_shards(shards, name):
    n = len(shards)
    split = [s.size >= (1 << 16) for s in shards]

    def half(ref, a, h):
        shp = shards[a].shape
        if len(shp) == 3:
            return ref.at[h]
        r2 = shp[0] // 2
        return ref.at[pl.ds(pl.multiple_of(h * r2, 2 * SUBLANES), r2), :]

    def body(*refs):
        ins, outs = refs[:n], refs[n:2 * n]
        send, recv, fsend, frecv = refs[2 * n:]
        x, y, c = _place()
        k = 2 * x + y
        chips = [(1 - x, y), (x, 1 - y), (1 - x, 1 - y)]

        def ici(a, j, src_slot_ref, dst_slot):
            px, py = chips[j]
            src = half(src_slot_ref, a, c) if split[a] else src_slot_ref
            dst = half(outs[a].at[dst_slot], a, c) if split[a] else outs[a].at[dst_slot]
            return pltpu.make_async_remote_copy(src, dst, send.at[a, j], recv.at[a, j], device_id=(px, py, c), device_id_type=MESH)

        def d2d(a, j, h):
            px, py = chips[j]
            part = half(outs[a].at[2 * px + py], a, h)
            return pltpu.make_async_remote_copy(part, part, fsend.at[a, j], frecv.at[a, j], device_id=(x, y, 1 - c), device_id_type=MESH)

        sends = [ici(a, j, ins[a], k) for a in range(n) for j in range(3)]
        for cp in sends:
            cp.start()
        passed = []
        for a in range(n):
            for j, (px, py) in enumerate(chips):
                ici(a, j, ins[a], 2 * px + py).wait_recv()
                if split[a]:
                    passed.append(d2d(a, j, c))
                    passed[-1].start()
        for a in range(n):
            if split[a]:
                for j in range(3):
                    d2d(a, j, 1 - c).wait_recv()
        for cp in sends + passed:
            cp.wait_send()

    return pl.pallas_call(
        body, in_specs=[ANY] * n, out_specs=[ANY] * n,
        out_shape=[_sds((N_CHIPS,) + s.shape, s.dtype) for s in shards],
        scratch_shapes=[pltpu.SemaphoreType.DMA((n, 3))] * 4,
        compiler_params=pltpu.CompilerParams(has_side_effects=True), name=name)(*shards)


HBM = pl.BlockSpec(memory_space=pltpu.HBM)
SEM = pl.BlockSpec(memory_space=pltpu.SEMAPHORE)
DATAFLOW = pltpu.SideEffectType.DATAFLOW_SIDE_EFFECTING


def _spread_start(groups, slot_src, after, name):
    flat = [a for grp in groups for a in grp]
    n = len(flat)
    ng = len(groups)
    offs = [sum(len(g) for g in groups[:i]) for i in range(ng)]
    lshape = [(a.shape if slot_src else (N_CHIPS,) + a.shape) for a in flat]

    nsem = 6 * n

    def body(*refs):
        src, land = refs[:n], refs[n:2 * n]
        sems = refs[2 * n + 1:2 * n + 1 + nsem]
        token = refs[-1]
        x, y, c = _place()
        k = 2 * x + y
        chips = [(1 - x, y), (x, 1 - y), (1 - x, 1 - y)]
        for a in range(n):
            for j, (px, py) in enumerate(chips):
                s = src[a].at[2 * px + py] if slot_src else src[a]
                pltpu.make_async_remote_copy(s, land[a].at[k], sems[6 * a + 2 * j], sems[6 * a + 2 * j + 1],
                                             device_id=(px, py, c), device_id_type=MESH).start()
        token[...] = jnp.zeros(token.shape, token.dtype)

    out_shape = [pltpu.SemaphoreType.DMA(())] * nsem
    out_shape += [pltpu.HBM(a.shape, a.dtype) for a in flat] + [pltpu.HBM(s, a.dtype) for s, a in zip(lshape, flat)]
    out_shape.append(_sds((SUBLANES, LANES)))
    args = [pltpu.with_memory_space_constraint(a, pltpu.HBM) for a in flat]
    args += [pltpu.with_memory_space_constraint(lax.empty(s, a.dtype), pltpu.HBM) for s, a in zip(lshape, flat)]
    res = pl.pallas_call(
        body, name=name, out_shape=tuple(out_shape), in_specs=[HBM] * (2 * n) + [pl.BlockSpec(memory_space=pl.ANY)],
        out_specs=tuple([SEM] * nsem + [HBM] * (2 * n) + [pl.BlockSpec(memory_space=pltpu.VMEM)]),
        input_output_aliases={i: nsem + i for i in range(2 * n)},
        compiler_params=pltpu.CompilerParams(has_side_effects=DATAFLOW))(*args, after)
    sems, thru, token = res[:nsem], res[nsem:nsem + 2 * n], res[-1]
    out = []
    for gi, grp in enumerate(groups):
        sl = slice(offs[gi], offs[gi] + len(grp))
        out.append((list(sems[6 * offs[gi]:6 * (offs[gi] + len(grp))]), list(thru[:n][sl]), list(thru[n:][sl])))
    return out, token


def _spread_wait(started, slot_src, after, name):
    sems, srcs, lands = started
    n = len(srcs)

    def body(*refs):
        src, land = refs[:n], refs[n:2 * n]
        sem = refs[2 * n:2 * n + 6 * n]
        x, y, c = _place()
        chips = [(1 - x, y), (x, 1 - y), (1 - x, 1 - y)]
        for a in range(n):
            for j, (px, py) in enumerate(chips):
                s = src[a].at[2 * px + py] if slot_src else src[a]
                cp = pltpu.make_async_remote_copy(s, land[a].at[2 * px + py], sem[6 * a + 2 * j], sem[6 * a + 2 * j + 1],
                                                  device_id=(px, py, c), device_id_type=MESH)
                cp.wait_send()
                cp.wait_recv()

    res = pl.pallas_call(
        body, name=name, out_shape=tuple([pltpu.HBM(a.shape, a.dtype) for a in srcs] + [pltpu.HBM(a.shape, a.dtype) for a in lands]),
        in_specs=[HBM] * (2 * n) + [SEM] * (6 * n) + [pl.BlockSpec(memory_space=pl.ANY)], out_specs=tuple([HBM] * (2 * n)),
        input_output_aliases={i: i for i in range(2 * n)},
        compiler_params=pltpu.CompilerParams(has_side_effects=DATAFLOW))(*srcs, *lands, *sems, after)
    return list(res[:n]), list(res[n:])


def _sibling_exchange(fs, name):
    n = len(fs)

    def body(*refs):
        ins, outs = refs[:n], refs[n:2 * n]
        send, recv = refs[2 * n:]
        x, y, c = _place()
        cps = [pltpu.make_async_remote_copy(ins[a], outs[a], send.at[a], recv.at[a],
                                            device_id=(x, y, 1 - c), device_id_type=MESH) for a in range(n)]
        for cp in cps:
            cp.start()
        for cp in cps:
            cp.wait()

    return pl.pallas_call(
        body, in_specs=[ANY] * n, out_specs=[ANY] * n, out_shape=[_sds(f.shape, f.dtype) for f in fs],
        scratch_shapes=[pltpu.SemaphoreType.DMA((n,)), pltpu.SemaphoreType.DMA((n,))],
        compiler_params=pltpu.CompilerParams(has_side_effects=True), name=name)(*fs)


def _tile2d(rows, cols, budget=1024 * 1024, step=2 * SUBLANES):
    fits = [t for t in range(step, rows + 1, step) if rows % t == 0 and t * cols * 4 <= budget]
    if fits:
        return fits[-1], cols
    fits = [t for t in range(LANES, cols + 1, LANES) if cols % t == 0 and rows * t * 4 <= budget]
    assert fits, (rows, cols)
    return rows, fits[-1]


def _chip_sum(own, parts, kidx, name):
    _, R, C = parts.shape
    tr, tc = _tile2d(R, C)

    def body(k_ref, o_ref_in, p1_ref, p2_ref, p3_ref, o_ref):
        o_ref[...] = ((o_ref_in[...].astype(F32) + p1_ref[...].astype(F32)) + p2_ref[...].astype(F32)) + p3_ref[...].astype(F32)

    def slot(d):
        return pl.BlockSpec((None, tr, tc), lambda i, j, k: ((k[0] + d) % N_CHIPS, i, j))

    return pl.pallas_call(
        body,
        grid_spec=pltpu.PrefetchScalarGridSpec(
            num_scalar_prefetch=1, grid=(R // tr, C // tc), in_specs=[slot(0), slot(1), slot(2), slot(3)],
            out_specs=pl.BlockSpec((tr, tc), lambda i, j, k: (i, j))),
        out_shape=_sds((R, C)), compiler_params=_cp("parallel", "parallel"), name=name)(kidx, own, parts, parts, parts)


def _adamw_math(w, g, m, v):
    m2 = ADAM_B1 * m + (1.0 - ADAM_B1) * g
    v2 = ADAM_B2 * v + (1.0 - ADAM_B2) * (g * g)
    m_hat = m2 / (1.0 - ADAM_B1 ** ADAM_STEP)
    v_hat = v2 / (1.0 - ADAM_B2 ** ADAM_STEP)
    delta = -ADAM_LR * (m_hat / (jnp.sqrt(v_hat) + ADAM_EPS) + ADAM_WD * w)
    return delta, m2, v2


def _adamw(w, m, v, gparts, name):
    Lw, R, C = w.shape
    tr, tc = _tile2d(R, C)
    flat = [h for pair in gparts for h in pair]

    def body(*refs):
        w_ref, m_ref, v_ref = refs[:3]
        g_refs = refs[3:3 + 2 * Lw]
        go_ref, d_ref, mo_ref, vo_ref = refs[3 + 2 * Lw:]
        g = g_refs[0][...] + g_refs[1][...]
        for l in range(1, Lw):
            g = jnp.where(pl.program_id(0) == l, g_refs[2 * l][...] + g_refs[2 * l + 1][...], g)
        d, m2, v2 = _adamw_math(w_ref[...], g, m_ref[...], v_ref[...])
        go_ref[...] = g
        d_ref[...] = d
        mo_ref[...] = m2
        vo_ref[...] = v2

    blk = pl.BlockSpec((None, tr, tc), lambda l, i, j: (l, i, j))
    gblk = pl.BlockSpec((tr, tc), lambda l, i, j: (i, j))
    return pl.pallas_call(
        body, grid=(Lw, R // tr, C // tc), in_specs=[blk, blk, blk] + [gblk] * (2 * Lw), out_specs=[blk] * 4,
        out_shape=[_sds((Lw, R, C))] * 4, compiler_params=_cp("parallel", "parallel", "parallel"), name=name)(w, m, v, *flat)


def _small_adamw(grads, wmv, name):
    n = len(grads)

    def body(*refs):
        g_in, p_in, outs = refs[:n], refs[n:4 * n], refs[4 * n:]
        for a in range(n):
            g = g_in[a][...]
            d_, m2, v2 = _adamw_math(p_in[3 * a][...], g, p_in[3 * a + 1][...], p_in[3 * a + 2][...])
            outs[4 * a][...] = g
            outs[4 * a + 1][...] = d_
            outs[4 * a + 2][...] = m2
            outs[4 * a + 3][...] = v2

    vm = pl.BlockSpec(memory_space=pltpu.VMEM)
    args = list(grads) + [t for tri in wmv for t in tri]
    out_shape = [_sds(g.shape) for g in grads for _ in range(4)]
    return pl.pallas_call(body, in_specs=[vm] * len(args), out_specs=[vm] * len(out_shape), out_shape=out_shape,
                          compiler_params=pltpu.CompilerParams(vmem_limit_bytes=V7X_VMEM_LIMIT), name=name)(*args)


def _small_allreduce(partials, pshapes, loss_row, name):
    n = len(partials)
    gshapes = [p.shape for p in partials] + [loss_row.shape]
    ng = n + 1

    def body(*refs):
        g_in = refs[:ng]
        outs = refs[ng:2 * ng]
        sib = refs[2 * ng:3 * ng]
        pair = refs[3 * ng:4 * ng]
        bufs = refs[4 * ng:5 * ng]
        send1, recv1, send2, recv2 = refs[-4:]
        x, y, c = _place()
        k = 2 * x + y
        chips = [(1 - x, y), (x, 1 - y), (1 - x, 1 - y)]
        swaps = [pltpu.make_async_remote_copy(g_in[a], sib[a], send1.at[a], recv1.at[a],
                                              device_id=(x, y, 1 - c), device_id_type=MESH) for a in range(ng)]
        for cp in swaps:
            cp.start()
        for a, cp in enumerate(swaps):
            cp.wait()
            pair[a][...] = g_in[a][...] + sib[a][...]
            bufs[a][k] = pair[a][...]
        sends = [pltpu.make_async_remote_copy(pair[a], bufs[a].at[k], send2.at[a, j], recv2.at[a, j],
                                              device_id=(px, py, c), device_id_type=MESH)
                 for a in range(ng) for j, (px, py) in enumerate(chips)]
        for cp in sends:
            cp.start()
        for a in range(ng):
            for j, (px, py) in enumerate(chips):
                pltpu.make_async_remote_copy(pair[a], bufs[a].at[2 * px + py], send2.at[a, j], recv2.at[a, j],
                                             device_id=(px, py, c), device_id_type=MESH).wait_recv()
        for cp in sends:
            cp.wait_send()
        for a in range(ng):
            sharded = len(gshapes[a]) == 3

            def part(d):
                return bufs[a][d, k] if sharded else bufs[a][d]

            tot = part(0)
            for d in range(1, N_CHIPS):
                tot = tot + part(d)
            if a == n:
                outs[n][...] = tot
            else:
                pr, pc = pshapes[a]
                outs[a][...] = tot[:pr, :pc]

    vm = pl.BlockSpec(memory_space=pltpu.VMEM)
    args = list(partials) + [loss_row]
    out_shape = [_sds(ps) for ps in pshapes] + [_sds(loss_row.shape)]
    return pl.pallas_call(
        body, in_specs=[vm] * len(args), out_specs=[vm] * len(out_shape), out_shape=out_shape,
        scratch_shapes=[pltpu.VMEM(tuple(s), F32) for s in gshapes] * 2 + [pltpu.VMEM((N_CHIPS,) + tuple(s), F32) for s in gshapes]
        + [pltpu.SemaphoreType.DMA((ng,)), pltpu.SemaphoreType.DMA((ng,)),
           pltpu.SemaphoreType.DMA((ng, 3)), pltpu.SemaphoreType.DMA((ng, 3))],
        compiler_params=pltpu.CompilerParams(has_side_effects=True, vmem_limit_bytes=V7X_VMEM_LIMIT), name=name)(*args)


_PERM = (0, 2, 1, 3)


def _cols_from_shards(g):
    return g.transpose(1, 0, 2).reshape(g.shape[1], N_CHIPS * g.shape[2])


def _rope_tables(positions):
    inv_freq = ROPE_THETA ** (-jnp.arange(0, HEAD_DIM, 2, dtype=F32) / HEAD_DIM)
    ang = positions.astype(F32).reshape(-1, 1) * inv_freq
    cos, sin = jnp.cos(ang), jnp.sin(ang)
    cos = jnp.concatenate([cos, cos, cos, cos], axis=-1)
    sin_s = jnp.concatenate([-sin, sin, -sin, sin], axis=-1)
    return cos, sin_s


def kernel(x, positions, norm_mix, norm_ffn, norm_final, mix_w_in, pool_w, pool_scale, attn_sinks, mix_w_out, ssm_w_in, ssm_conv_w, ssm_conv_b, ssm_dt_bias, ssm_A_log, ssm_D, ssm_norm, ssm_w_out, ffn_w_up, ffn_conv_w, ffn_conv_b, ffn_w_down, loss_target, m_norm_mix, m_norm_ffn, m_norm_final, m_mix_w_in, m_pool_w, m_pool_scale, m_attn_sinks, m_mix_w_out, m_ssm_w_in, m_ssm_conv_w, m_ssm_conv_b, m_ssm_dt_bias, m_ssm_A_log, m_ssm_D, m_ssm_norm, m_ssm_w_out, m_ffn_w_up, m_ffn_conv_w, m_ffn_conv_b, m_ffn_w_down, v_norm_mix, v_norm_ffn, v_norm_final, v_mix_w_in, v_pool_w, v_pool_scale, v_attn_sinks, v_mix_w_out, v_ssm_w_in, v_ssm_conv_w, v_ssm_conv_b, v_ssm_dt_bias, v_ssm_A_log, v_ssm_D, v_ssm_norm, v_ssm_w_out, v_ffn_w_up, v_ffn_conv_w, v_ffn_conv_b, v_ffn_w_down):
    W = dict(norm_mix=norm_mix, norm_ffn=norm_ffn, norm_final=norm_final, mix_w_in=mix_w_in, pool_w=pool_w, pool_scale=pool_scale, attn_sinks=attn_sinks, mix_w_out=mix_w_out, ssm_w_in=ssm_w_in, ssm_conv_w=ssm_conv_w, ssm_conv_b=ssm_conv_b, ssm_dt_bias=ssm_dt_bias, ssm_A_log=ssm_A_log, ssm_D=ssm_D, ssm_norm=ssm_norm, ssm_w_out=ssm_w_out, ffn_w_up=ffn_w_up, ffn_conv_w=ffn_conv_w, ffn_conv_b=ffn_conv_b, ffn_w_down=ffn_w_down)
    Mo = dict(norm_mix=m_norm_mix, norm_ffn=m_norm_ffn, norm_final=m_norm_final, mix_w_in=m_mix_w_in, pool_w=m_pool_w, pool_scale=m_pool_scale, attn_sinks=m_attn_sinks, mix_w_out=m_mix_w_out, ssm_w_in=m_ssm_w_in, ssm_conv_w=m_ssm_conv_w, ssm_conv_b=m_ssm_conv_b, ssm_dt_bias=m_ssm_dt_bias, ssm_A_log=m_ssm_A_log, ssm_D=m_ssm_D, ssm_norm=m_ssm_norm, ssm_w_out=m_ssm_w_out, ffn_w_up=m_ffn_w_up, ffn_conv_w=m_ffn_conv_w, ffn_conv_b=m_ffn_conv_b, ffn_w_down=m_ffn_w_down)
    Vo = dict(norm_mix=v_norm_mix, norm_ffn=v_norm_ffn, norm_final=v_norm_final, mix_w_in=v_mix_w_in, pool_w=v_pool_w, pool_scale=v_pool_scale, attn_sinks=v_attn_sinks, mix_w_out=v_mix_w_out, ssm_w_in=v_ssm_w_in, ssm_conv_w=v_ssm_conv_w, ssm_conv_b=v_ssm_conv_b, ssm_dt_bias=v_ssm_dt_bias, ssm_A_log=v_ssm_A_log, ssm_D=v_ssm_D, ssm_norm=v_ssm_norm, ssm_w_out=v_ssm_w_out, ffn_w_up=v_ffn_w_up, ffn_conv_w=v_ffn_conv_w, ffn_conv_b=v_ffn_conv_b, ffn_w_down=v_ffn_w_down)

    kchip = 2 * lax.axis_index("x") + lax.axis_index("y")

    def own_slot(g, own):
        return lax.dynamic_update_slice_in_dim(g, own[None], kchip, axis=0)

    def tr(t):
        return jnp.swapaxes(t[0], 0, 1)

    later = dict(ffn0=[ffn_w_up[0].astype(MXU), ffn_w_down[0].astype(MXU)],
                 ssm=[tr(ssm_w_in).astype(MXU), ssm_w_out[0].astype(MXU)],
                 ffn1=[ffn_w_up[1].astype(MXU), ffn_w_down[1].astype(MXU)])
    sh = [tr(mix_w_in).astype(MXU), mix_w_out[0].astype(MXU), ssm_conv_w[0], ssm_conv_b, ssm_norm, ffn_conv_w]
    first = _gather_shards(sh, "gather_first")
    g_mi, g_mo, g_scw, g_scb, g_sn, g_fcw = [own_slot(g, own) for g, own in zip(first, sh)]
    started, token = _spread_start(list(later.values()), False, first[0], "gather_start")
    started = dict(zip(later.keys(), started))
    fcw = [jnp.concatenate([g_fcw[p, i] for p in _PERM], axis=1) for i in range(2)]
    P = dict(
        nm=norm_mix, nf=norm_ffn, nfin=norm_final,
        wmiT=g_mi.reshape(MIX_IN_DIM, D_MODEL), wmo=g_mo.reshape(D_MODEL, D_MODEL),
        pool_w=pool_w[0], pool_scale=pool_scale, sinks=attn_sinks[0],
        scw=_cols_from_shards(g_scw), scb=g_scb.reshape(1, SSM_CONV_DIM), snorm=g_sn.reshape(1, SSM_D_INNER),
        dt_bias=jnp.pad(ssm_dt_bias, ((0, 0), (0, LANES - SSM_HEADS))), a_log=jnp.pad(ssm_A_log, ((0, 0), (0, LANES - SSM_HEADS))),
        d_exp=jnp.repeat(ssm_D, SSM_D_INNER // SSM_HEADS, axis=1),
        fcb=[jnp.concatenate([ffn_conv_b[i:i + 1, p * FFN_TC:(p + 1) * FFN_TC] for p in _PERM], axis=1) for i in range(2)],
    )

    def fetch(group, after):
        owns, lands = _spread_wait(started[group], False, after, f"gather_wait_{group}")
        a, b = [own_slot(g, own) for g, own in zip(lands, owns)]
        if group == "ssm":
            wsi = a.reshape(SSM_IN_DIM, D_MODEL)
            zx = SSM_D_INNER + SSM_CONV_DIM
            return dict(wzT=wsi[:SSM_D_INNER], wxbcT=wsi[SSM_D_INNER:zx],
                        wdtT=jnp.pad(wsi[zx:], ((0, LANES - SSM_HEADS), (0, 0))), wso=b.reshape(SSM_D_INNER, D_MODEL))
        i = int(group[-1])
        return dict(wup=jnp.concatenate([a[p] for p in _PERM], axis=1), wdn=b.reshape(D_FF, D_MODEL), fcw=fcw[i])

    cos, sin_s = _rope_tables(positions)
    sent = {}

    def send(group, grads):
        res, tok = _spread_start([grads], True, jnp.zeros((SUBLANES, LANES), F32), f"grad_start_{group}")
        sent[group] = res[0]
        return tok

    loss_row, grad_x, big, small = _local_step(x[0], cos, sin_s, loss_target[0], P, fetch, token, send)

    kidx = kchip.astype(jnp.int32).reshape(1)
    group_names = dict(ffn1=["ffn_w_up1", "ffn_w_down1"], ssm=["ssm_w_in", "ssm_w_out"], ffn0=["ffn_w_up0", "ffn_w_down0"],
                       mix=["mix_w_in", "mix_w_out"])
    names, mine = [], []
    for group, started_g in sent.items():
        grads, lands = _spread_wait(started_g, True, grad_x, f"grad_wait_{group}")
        for nm, g, land in zip(group_names[group], grads, lands):
            names.append(nm)
            mine.append(_chip_sum(g, land, kidx, f"chip_sum_{nm}"))
    theirs = _sibling_exchange(mine, "sibling_exchange")
    red = {nm: (a, b) for nm, a, b in zip(names, mine, theirs)}

    out = {}

    def big_update(pname, gparts, transposed=False):
        w = W[pname]
        lw = len(gparts)
        shp = w.shape
        rr, cc = gparts[0][0].shape
        fix = (lambda t: tr(t)[None]) if transposed else (lambda t: t.reshape(lw, rr, cc))
        res = _adamw(fix(w), fix(Mo[pname]), fix(Vo[pname]), gparts, f"adamw_{pname}")
        out[pname] = tuple((tr(r)[None] if transposed else r.reshape(shp)) for r in res)

    big_update("mix_w_in", [red["mix_w_in"]], transposed=True)
    big_update("mix_w_out", [red["mix_w_out"]])
    big_update("ssm_w_in", [red["ssm_w_in"]], transposed=True)
    big_update("ssm_w_out", [red["ssm_w_out"]])
    big_update("ffn_w_up", [red["ffn_w_up0"], red["ffn_w_up1"]])
    big_update("ffn_w_down", [red["ffn_w_down0"], red["ffn_w_down1"]])

    small_names = ["norm_mix", "norm_ffn", "norm_final", "pool_w", "pool_scale", "attn_sinks", "ssm_dt_bias", "ssm_A_log",
                   "ssm_D", "ffn_conv_b", "ssm_conv_w", "ssm_conv_b", "ssm_norm", "ffn_conv_w"]

    def as2d(t):
        if t.ndim == 1:
            return t.reshape(1, -1)
        return t.reshape(-1, t.shape[-1])

    wmv = [(as2d(W[nm]), as2d(Mo[nm]), as2d(Vo[nm])) for nm in small_names]
    summed = _small_allreduce([small[nm] for nm in small_names], [t[0].shape for t in wmv], loss_row, "small_allreduce")
    res = _small_adamw(summed[:-1], wmv, "small_adamw")
    for a, nm in enumerate(small_names):
        out[nm] = tuple(r.reshape(W[nm].shape) for r in res[4 * a:4 * a + 4])
    loss = summed[-1][0, 0]

    order = ["norm_mix", "norm_ffn", "norm_final", "mix_w_in", "pool_w", "pool_scale", "attn_sinks", "mix_w_out", "ssm_w_in",
             "ssm_conv_w", "ssm_conv_b", "ssm_dt_bias", "ssm_A_log", "ssm_D", "ssm_norm", "ssm_w_out", "ffn_w_up", "ffn_conv_w",
             "ffn_conv_b", "ffn_w_down"]
    return (loss, grad_x.reshape(x.shape), *[out[nm][0] for nm in order], *[out[nm][1] for nm in order],
            *[out[nm][2] for nm in order], *[out[nm][3] for nm in order])
```

```python
import functools

import jax
import jax.numpy as jnp
from jax import lax
from jax.experimental import pallas as pl
from jax.experimental.pallas import tpu as pltpu

F32 = jnp.float32
BF16 = jnp.bfloat16
MXU = BF16
HI = lax.Precision.HIGHEST

D_MODEL = 1024
POOL_WINDOWS = (2, 4, 8, 16)
POOL_DIM = 512
POOL_GROUP = 128
HEAD_DIM = 64
N_HEADS = 8
N_KV_HEADS = 2
GQ = 4
Q_DIM = 512
KV_DIM = 128
BLOCK = 128
ROPE_THETA = 10000.0
MIX_IN_DIM = 1280
SSM_D_INNER = 2048
SSM_HEADS = 32
SSM_GROUPS = 8
SSM_STATE = 128
SSM_CONV = 4
SSM_CHUNK = 128
SSM_CONV_DIM = 4096
SSM_IN_DIM = 6176
D_FF = 2816
FFN_CONV = 3
NORM_EPS = 1e-6
SSM_NORM_EPS = 1e-5
ADAM_LR = 0.001
ADAM_B1 = 0.9
ADAM_B2 = 0.999
ADAM_EPS = 1e-08
ADAM_WD = 0.01
ADAM_STEP = 10

N_CHIPS = 4
N_DEV = 8
LANES = 128
SUBLANES = 8
V7X_VMEM_LIMIT = 56 * 1024 * 1024
NEG = -1e30
MESH = pl.DeviceIdType.MESH


def _cp(*sem):
    return pltpu.CompilerParams(dimension_semantics=sem if sem else None, vmem_limit_bytes=V7X_VMEM_LIMIT)


def _sds(shape, dtype=F32):
    return jax.ShapeDtypeStruct(tuple(shape), dtype)


def _iota(shape, dim):
    return lax.broadcasted_iota(jnp.int32, shape, dim)


def _silu(x):
    return x * (1.0 / (1.0 + jnp.exp(-x)))


def _dsilu(x):
    s = 1.0 / (1.0 + jnp.exp(-x))
    return s * (1.0 + x * (1.0 - s))


def _mm(a, b, *, ta=False, tb=False, tm, tn, tk, res=None, out_dtype=F32, out_shard_perm=None, out_into=None, norm_w=None,
        norm_bwd=None, name):
    M, K = (a.shape[1], a.shape[0]) if ta else a.shape
    N = b.shape[0] if tb else b.shape[1]
    tm, tn, tk = min(tm, M), min(tn, N), min(tk, K)
    gm, gn, gk = M // tm, N // tn, K // tk
    assert gm * tm == M and gn * tn == N and gk * tk == K, (name, M, N, K, tm, tn, tk)
    a_spec = pl.BlockSpec((tk, tm), lambda i, j, k: (k, i)) if ta else pl.BlockSpec((tm, tk), lambda i, j, k: (i, k))
    b_spec = pl.BlockSpec((tn, tk), lambda i, j, k: (j, k)) if tb else pl.BlockSpec((tk, tn), lambda i, j, k: (k, j))
    dims = (((0 if ta else 1,), (1 if tb else 0,)), ((), ()))
    has_res = res is not None
    has_nw = norm_w is not None
    has_nb = norm_bwd is not None
    has_tok = has_nb and norm_bwd[3] is not None
    assert not (has_nw or has_nb) or (gn == 1 and out_shard_perm is None)
    n_extra = has_res + has_nw + (3 + has_tok if has_nb else 0)

    def body(*refs):
        a_ref, b_ref = refs[0], refs[1]
        extra = list(refs[2:2 + n_extra])
        outs = refs[len(args):]
        r_ref = extra.pop(0) if has_res else None
        nw_ref = extra.pop(0) if has_nw else None
        nb_refs = extra if has_nb else None

        def dot():
            return lax.dot_general(a_ref[...].astype(MXU), b_ref[...].astype(MXU), dims, preferred_element_type=F32)

        def finish(r):
            if has_res:
                r = r + r_ref[...]
            if has_nb:
                xv = nb_refs[0][...]
                rs = lax.rsqrt(jnp.mean(xv * xv, axis=-1, keepdims=True) + NORM_EPS)
                xh = xv * rs
                g = r * nb_refs[1][...]
                dr = nb_refs[2][...] + nb_refs[3][0:1, 0:1] if has_tok else nb_refs[2][...]
                outs[0][...] = dr + rs * (g - xh * jnp.mean(g * xh, axis=-1, keepdims=True))
                part = jnp.sum(r * xh, axis=0, keepdims=True)
                i = pl.program_id(0)

                @pl.when(i == 0)
                def _():
                    outs[1][...] = part

                @pl.when(i > 0)
                def _():
                    outs[1][...] += part
                return
            outs[0][...] = r.astype(out_dtype)
            if has_nw:
                rs = lax.rsqrt(jnp.mean(r * r, axis=-1, keepdims=True) + NORM_EPS)
                outs[1][...] = (r * rs * nw_ref[...]).astype(outs[1].dtype)

        if gk == 1:
            finish(dot())
        else:
            acc = refs[-1]
            k = pl.program_id(2)

            @pl.when(k == 0)
            def _():
                acc[...] = dot()

            if gk > 2:
                @pl.when(jnp.logical_and(k > 0, k < gk - 1))
                def _():
                    acc[...] += dot()

            @pl.when(k == gk - 1)
            def _():
                finish(acc[...] + dot())

    tile = pl.BlockSpec((tm, tn), lambda i, j, k: (i, j))
    row = pl.BlockSpec((1, tn), lambda i, j, k: (0, j))
    in_specs = [a_spec, b_spec]
    args = [a, b]
    if has_res:
        in_specs.append(tile)
        args.append(res)
    if has_nw:
        in_specs.append(row)
        args.append(norm_w.reshape(1, N))
    if has_nb:
        in_specs += [tile, row, tile]
        args += [norm_bwd[0], norm_bwd[1].reshape(1, N), norm_bwd[2]]
        if has_tok:
            in_specs.append(pl.BlockSpec((SUBLANES, LANES), lambda i, j, k: (0, 0)))
            args.append(norm_bwd[3])
    alias = {}
    if out_into is not None:
        buf, rows, off = out_into
        out_spec = pl.BlockSpec((tm, tn), lambda i, j, k: (i + off, j))
        out_shape = _sds((rows, N), out_dtype)
        if buf is not None:
            alias = {len(args): 0}
            in_specs.append(pl.BlockSpec(memory_space=pl.ANY))
            args.append(buf)
    elif out_shard_perm is None:
        out_spec = tile
        out_shape = _sds((M, N), out_dtype)
    else:
        assert gn == len(out_shard_perm) == 4 and tuple(out_shard_perm) == (0, 2, 1, 3)
        out_spec = pl.BlockSpec((None, tm, tn), lambda i, j, k: ((j % 2) * 2 + j // 2, i, 0))
        out_shape = _sds((gn, M, tn), out_dtype)
    sem = ("parallel", "parallel", "arbitrary")
    if has_nw:
        out_spec, out_shape = [out_spec, tile], [out_shape, _sds((M, N), MXU)]
    if has_nb:
        out_spec, out_shape = [tile, row], [_sds((M, N)), _sds((1, N))]
        sem = ("arbitrary", "arbitrary", "arbitrary")
    return pl.pallas_call(
        body, grid=(gm, gn, gk), in_specs=in_specs, out_specs=out_spec, out_shape=out_shape,
        scratch_shapes=[pltpu.VMEM((tm, tn), F32)] if gk > 1 else [], input_output_aliases=alias,
        compiler_params=_cp(*sem), name=name)(*args)


def _put_rows(buf, src, rows, at, name):
    assert at % rows == 0 and src.shape[1] == buf.shape[1] and src.dtype == buf.dtype
    C = buf.shape[1]

    def body(s_ref, b_ref, o_ref):
        o_ref[...] = s_ref[...]

    return pl.pallas_call(
        body, grid=(1,), in_specs=[pl.BlockSpec((rows, C), lambda i: (0, 0)), pl.BlockSpec(memory_space=pl.ANY)],
        out_specs=pl.BlockSpec((rows, C), lambda i: (at // rows, 0)), out_shape=_sds(buf.shape, buf.dtype),
        input_output_aliases={1: 0}, compiler_params=_cp("arbitrary"), name=name)(src, buf)


def _rmsnorm_fwd(x, w, name, token=None):
    T, D = x.shape
    tm = min(T, 512)
    has_token = token is not None

    def body(*refs):
        x_ref, w_ref, o_ref = refs[0], refs[1], refs[-1]
        xv = x_ref[...]
        if has_token:
            xv = xv + refs[2][0:1, 0:1]
        r = lax.rsqrt(jnp.mean(xv * xv, axis=-1, keepdims=True) + NORM_EPS)
        o_ref[...] = (xv * r * w_ref[...]).astype(o_ref.dtype)

    in_specs = [pl.BlockSpec((tm, D), lambda i: (i, 0)), pl.BlockSpec((1, D), lambda i: (0, 0))]
    args = [x, w.reshape(1, D)]
    if has_token:
        in_specs.append(pl.BlockSpec((SUBLANES, LANES), lambda i: (0, 0)))
        args.append(token)
    return pl.pallas_call(
        body, grid=(T // tm,), in_specs=in_specs,
        out_specs=pl.BlockSpec((tm, D), lambda i: (i, 0)), out_shape=_sds((T, D), MXU),
        compiler_params=_cp("parallel"), name=name)(*args)


def _loss_head(x, w, target, name):
    T, D = x.shape
    tm = min(T, 512)

    def body(x_ref, w_ref, t_ref, loss_ref, dx_ref, dw_ref):
        xv = x_ref[...]
        r = lax.rsqrt(jnp.mean(xv * xv, axis=-1, keepdims=True) + NORM_EPS)
        xh = xv * r
        wv = w_ref[...]
        e = xh * wv - t_ref[...]
        lpart = 0.5 * jnp.sum(jnp.mean(e * e, axis=-1, keepdims=True), axis=0, keepdims=True)
        dy = e * (1.0 / D)
        g = dy * wv
        dx_ref[...] = r * (g - xh * jnp.mean(g * xh, axis=-1, keepdims=True))
        part = jnp.sum(dy * xh, axis=0, keepdims=True)
        lrow = jnp.broadcast_to(lpart, (1, LANES))

        @pl.when(pl.program_id(0) == 0)
        def _():
            dw_ref[...] = part
            loss_ref[...] = lrow

        @pl.when(pl.program_id(0) > 0)
        def _():
            dw_ref[...] += part
            loss_ref[...] += lrow

    row = pl.BlockSpec((tm, D), lambda i: (i, 0))
    vec = pl.BlockSpec((1, D), lambda i: (0, 0))
    return pl.pallas_call(
        body, grid=(T // tm,), in_specs=[row, vec, row],
        out_specs=[pl.BlockSpec((1, LANES), lambda i: (0, 0)), row, vec],
        out_shape=[_sds((1, LANES)), _sds((T, D)), _sds((1, D))],
        compiler_params=_cp("arbitrary"), name=name)(x, w.reshape(1, D), target)


def _shift_down(cur, prev8, s):
    if s == 0:
        return cur
    tm = cur.shape[0]
    rc = pltpu.roll(cur, s, 0)
    top = jnp.where(_iota((SUBLANES, cur.shape[1]), 0) < s, pltpu.roll(prev8, s, 0), rc[:SUBLANES])
    return jnp.concatenate([top, rc[SUBLANES:]], axis=0) if tm > SUBLANES else top


def _shift_up(cur, next8, s):
    if s == 0:
        return cur
    tm = cur.shape[0]
    rc = pltpu.roll(cur, tm - s, 0)
    bot = jnp.where(_iota((SUBLANES, cur.shape[1]), 0) >= SUBLANES - s, pltpu.roll(next8, SUBLANES - s, 0), rc[tm - SUBLANES:])
    return jnp.concatenate([rc[:tm - SUBLANES], bot], axis=0) if tm > SUBLANES else bot


def _conv_rows(cur, prev8, w, b, K):
    acc = cur * w[K - 1:K, :] + b
    for s in range(1, K):
        acc = acc + _shift_down(cur, prev8, s) * w[K - 1 - s:K - s, :]
    return acc


FFN_TC = 1408
HALO16 = 2 * SUBLANES


def _ffn_up_conv_gate(hf, wup, cw, cb, name):
    T, D = hf.shape
    tm = min(T, 256)
    nt, nj = T // tm, D_FF // FFN_TC
    K = FFN_CONV
    W2 = 2 * FFN_TC

    def body(a_ref, b_ref, w_ref, c_ref, hid_ref, hc_ref, act_ref, halo):
        i = pl.program_id(1)

        @pl.when(i == 0)
        def _():
            halo[...] = jnp.zeros(halo.shape, F32)

        hb = jnp.dot(a_ref[...].astype(MXU), b_ref[...].astype(MXU), preferred_element_type=F32).astype(hid_ref.dtype)
        hid_ref[...] = hb
        cur = hb.astype(F32)
        hc = _conv_rows(cur, halo[...], w_ref[...], c_ref[...], K)
        halo[...] = cur[tm - SUBLANES:]
        hc_ref[...] = hc
        act_ref[...] = (_silu(hc[:, FFN_TC:]) * hc[:, :FFN_TC]).astype(act_ref.dtype)

    blk = pl.BlockSpec((tm, W2), lambda j, i: (i, j))
    return pl.pallas_call(
        body, grid=(nj, nt),
        in_specs=[pl.BlockSpec((tm, D), lambda j, i: (i, 0)), pl.BlockSpec((D, W2), lambda j, i: (0, j)),
                  pl.BlockSpec((K, W2), lambda j, i: (0, j)), pl.BlockSpec((1, W2), lambda j, i: (0, j))],
        out_specs=[blk, blk, pl.BlockSpec((tm, FFN_TC), lambda j, i: (i, j))],
        out_shape=[_sds((T, 2 * D_FF), MXU), _sds((T, 2 * D_FF)), _sds((T, D_FF), MXU)],
        scratch_shapes=[pltpu.VMEM((SUBLANES, W2), F32)],
        compiler_params=_cp("arbitrary", "arbitrary"), name=name)(hf, wup, cw, cb)


def _ffn_mid_bwd(hid, hc, cw, da, name):
    T = hid.shape[0]
    tm = min(T, 256)
    nt, nj = T // tm, D_FF // FFN_TC
    K = FFN_CONV
    W2 = 2 * FFN_TC

    def body(h_ref, c_ref, cn_ref, da_ref, dan_ref, w_ref, dh_ref, dw_ref, db_ref):
        i = pl.program_id(1)
        w = w_ref[...]
        cur = h_ref[...].astype(F32)
        last = i == nt - 1

        def dpre(hcv, dav):
            u, g = hcv[:, :FFN_TC], hcv[:, FFN_TC:]
            return jnp.concatenate([dav * _silu(g), dav * u * _dsilu(g)], axis=1)

        d_cur = dpre(c_ref[...], da_ref[...])
        d_nxt = jnp.where(last, 0.0, dpre(cn_ref[...], dan_ref[...]))
        ups = [d_cur] + [_shift_up(d_cur, d_nxt, s) for s in range(1, K)]
        dh = ups[0] * w[K - 1:K, :]
        for s in range(1, K):
            dh = dh + ups[s] * w[K - 1 - s:K - s, :]
        dh_ref[...] = dh.astype(dh_ref.dtype)
        dwp = jnp.concatenate([jnp.sum(ups[K - 1 - k] * cur, axis=0, keepdims=True) for k in range(K)], axis=0)
        dbp = jnp.sum(d_cur, axis=0, keepdims=True)

        @pl.when(i == 0)
        def _():
            dw_ref[...] = dwp
            db_ref[...] = dbp

        @pl.when(i > 0)
        def _():
            dw_ref[...] += dwp
            db_ref[...] += dbp

    q = tm // SUBLANES
    blk = pl.BlockSpec((tm, W2), lambda j, i: (i, j))
    nxt = pl.BlockSpec((SUBLANES, W2), lambda j, i: (jnp.minimum((i + 1) * q, nt * q - 1), j))
    dab = pl.BlockSpec((tm, FFN_TC), lambda j, i: (i, j))
    dan = pl.BlockSpec((SUBLANES, FFN_TC), lambda j, i: (jnp.minimum((i + 1) * q, nt * q - 1), j))
    return pl.pallas_call(
        body, grid=(nj, nt),
        in_specs=[blk, blk, nxt, dab, dan, pl.BlockSpec((K, W2), lambda j, i: (0, j))],
        out_specs=[blk, pl.BlockSpec((K, W2), lambda j, i: (0, j)), pl.BlockSpec((1, W2), lambda j, i: (0, j))],
        out_shape=[_sds((T, 2 * D_FF), MXU), _sds((K, 2 * D_FF)), _sds((1, 2 * D_FF))],
        compiler_params=_cp("parallel", "arbitrary"), name=name)(hid, hc, hc, da, da, cw)


def _rope(t, cos, sin_s, inverse=False):
    n = t.shape[1] // LANES
    c = jnp.concatenate([cos] * n, axis=1) if n > 1 else cos
    s = jnp.concatenate([sin_s] * n, axis=1) if n > 1 else sin_s
    a = pltpu.roll(t, HEAD_DIM // 2, 1)
    b = pltpu.roll(t, t.shape[1] - HEAD_DIM // 2, 1)
    first = (_iota(t.shape, 1) % HEAD_DIM) < HEAD_DIM // 2
    rot = jnp.where(first, b, a) * s
    return t * c - rot if inverse else t * c + rot


def _stack_heads(t, g):
    return jnp.concatenate([t[:, (GQ * g + r) * HEAD_DIM:(GQ * g + r + 1) * HEAD_DIM] for r in range(GQ)], axis=0)


def _stack_cols(t, g):
    return jnp.concatenate([t[:, GQ * g + r:GQ * g + r + 1] for r in range(GQ)], axis=0)


def _pool_sums(prev, cur, w):
    s = jnp.concatenate([prev, cur], axis=0)
    sh = 1
    while sh < w:
        s = s + pltpu.roll(s, sh, 0)
        sh *= 2
    return s[BLOCK:]


def _nt(a, b):
    return lax.dot_general(a.astype(MXU), b.astype(MXU), (((1,), (1,)), ((), ())), preferred_element_type=F32)


def _tn(a, b):
    return lax.dot_general(a.astype(MXU), b.astype(MXU), (((0,), (0,)), ((), ())), preferred_element_type=F32)


def _nn(a, b):
    return jnp.dot(a.astype(MXU), b.astype(MXU), preferred_element_type=F32)


def _mixcore_fwd(proj, cos, sin_s, pool_w, pool_scale, sinks, name):
    T = proj.shape[0]
    nb = T // BLOCK
    scale = HEAD_DIM ** -0.5

    def body(p_ref, pp_ref, c_ref, s_ref, cp_ref, sp_ref, pw_ref, ps_ref, sk_ref, cat_ref, at_ref, lse_ref):
        i = pl.program_id(0)
        has_prev = i > 0
        cur = p_ref[...]
        prv = jnp.where(has_prev, pp_ref[...], 0.0)
        tpos = (i * BLOCK + _iota((BLOCK, 1), 0) + 1).astype(F32)
        for g, w in enumerate(POOL_WINDOWS):
            sl = slice(g * POOL_GROUP, (g + 1) * POOL_GROUP)
            pooled = _pool_sums(prv[:, sl], cur[:, sl], w) / jnp.minimum(tpos, float(w)) - cur[:, sl]
            cat_ref[:, sl] = (_nn(pooled, pw_ref[g]) * ps_ref[:, sl]).astype(cat_ref.dtype)
        q = _rope(cur[:, POOL_DIM:POOL_DIM + Q_DIM], c_ref[...], s_ref[...])
        kc = _rope(cur[:, POOL_DIM + Q_DIM:POOL_DIM + Q_DIM + KV_DIM], c_ref[...], s_ref[...])
        kp = _rope(prv[:, POOL_DIM + Q_DIM:POOL_DIM + Q_DIM + KV_DIM], cp_ref[...], sp_ref[...])
        vc = cur[:, POOL_DIM + Q_DIM + KV_DIM:]
        vp = prv[:, POOL_DIM + Q_DIM + KV_DIM:]
        ri = _iota((GQ * BLOCK, BLOCK), 0) % BLOCK
        cj = _iota((GQ * BLOCK, BLOCK), 1)
        mc = cj <= ri
        mp = jnp.logical_and(cj > ri, has_prev)
        outs, lses = [], []
        for g in range(N_KV_HEADS):
            hs = slice(g * HEAD_DIM, (g + 1) * HEAD_DIM)
            qg = _stack_heads(q, g) * scale
            sc = jnp.where(mc, _nt(qg, kc[:, hs]), NEG)
            sp = jnp.where(mp, _nt(qg, kp[:, hs]), NEG)
            sink = jnp.concatenate([jnp.full((BLOCK, 1), sk_ref[GQ * g + r], F32) for r in range(GQ)], axis=0)
            m = jnp.maximum(jnp.maximum(jnp.max(sc, axis=1, keepdims=True), jnp.max(sp, axis=1, keepdims=True)), sink)
            pc = jnp.exp(sc - m)
            pp = jnp.exp(sp - m)
            den = jnp.sum(pc, axis=1, keepdims=True) + jnp.sum(pp, axis=1, keepdims=True) + jnp.exp(sink - m)
            o = (_nn(pc, vc[:, hs]) + _nn(pp, vp[:, hs])) / den
            lse = m + jnp.log(den)
            for r in range(GQ):
                outs.append(o[r * BLOCK:(r + 1) * BLOCK])
                lses.append(lse[r * BLOCK:(r + 1) * BLOCK])
        attn = jnp.concatenate(outs, axis=1)
        at_ref[...] = attn
        cat_ref[:, POOL_DIM:] = attn.astype(cat_ref.dtype)
        lane = _iota((BLOCK, LANES), 1)
        lrow = jnp.zeros((BLOCK, LANES), F32)
        for h in range(N_HEADS):
            lrow = jnp.where(lane == h, lses[h], lrow)
        lse_ref[...] = lrow

    cur = lambda w: pl.BlockSpec((BLOCK, w), lambda i: (i, 0))
    prv = lambda w: pl.BlockSpec((BLOCK, w), lambda i: (jnp.maximum(i - 1, 0), 0))
    return pl.pallas_call(
        body, grid=(nb,),
        in_specs=[cur(MIX_IN_DIM), prv(MIX_IN_DIM), cur(LANES), cur(LANES), prv(LANES), prv(LANES),
                  pl.BlockSpec((4, POOL_GROUP, POOL_GROUP), lambda i: (0, 0, 0)), pl.BlockSpec((1, POOL_DIM), lambda i: (0, 0)),
                  pl.BlockSpec(memory_space=pltpu.SMEM)],
        out_specs=[cur(2 * POOL_DIM), cur(Q_DIM), cur(LANES)],
        out_shape=[_sds((T, 2 * POOL_DIM), MXU), _sds((T, Q_DIM)), _sds((T, LANES))],
        compiler_params=_cp("parallel"), name=name)(proj, proj, cos, sin_s, cos, sin_s, pool_w, pool_scale, sinks)


def _mixcore_bwd(proj, cos, sin_s, pool_w, pool_scale, sinks, attn, lse, dcat, name):
    T = proj.shape[0]
    nb = T // BLOCK
    scale = HEAD_DIM ** -0.5
    QO, KO, VO = POOL_DIM, POOL_DIM + Q_DIM, POOL_DIM + Q_DIM + KV_DIM

    def body(p_ref, pp_ref, pn_ref, c_ref, s_ref, cp_ref, sp_ref, cn_ref, sn_ref, pw_ref, ps_ref, sk_ref,
             at_ref, atn_ref, l_ref, ln_ref, d_ref, dn_ref, dp_ref, dpw_ref, dps_ref, dsk_ref):
        i = pl.program_id(0)
        has_prev = i > 0
        has_next = i < nb - 1
        cur = p_ref[...]
        prv = jnp.where(has_prev, pp_ref[...], 0.0)
        d_cur = d_ref[...]
        d_nxt = jnp.where(has_next, dn_ref[...], 0.0)

        tpos = (i * BLOCK + _iota((BLOCK, 1), 0) + 1).astype(F32)
        tpos2 = (i * BLOCK + _iota((2 * BLOCK, 1), 0) + 1).astype(F32)
        ps = ps_ref[...]
        dps_parts, dpw_parts = [], []
        for g, w in enumerate(POOL_WINDOWS):
            sl = slice(g * POOL_GROUP, (g + 1) * POOL_GROUP)
            pooled = _pool_sums(prv[:, sl], cur[:, sl], w) / jnp.minimum(tpos, float(w)) - cur[:, sl]
            mixed = _nn(pooled, pw_ref[g])
            dps_parts.append(jnp.sum(d_cur[:, sl] * mixed, axis=0, keepdims=True))
            dm2 = jnp.concatenate([d_cur[:, sl], d_nxt[:, sl]], axis=0) * ps[:, sl]
            dpw_parts.append(_tn(pooled, dm2[:BLOCK]))
            dpool2 = _nt(dm2, pw_ref[g])
            e = dpool2 / jnp.minimum(tpos2, float(w))
            sh = 1
            while sh < w:
                e = e + pltpu.roll(e, 2 * BLOCK - sh, 0)
                sh *= 2
            dp_ref[:, sl] = (e[:BLOCK] - dpool2[:BLOCK]).astype(dp_ref.dtype)
        dpsp = jnp.concatenate(dps_parts, axis=1)

        nxt = pn_ref[...]
        q = _rope(cur[:, QO:KO], c_ref[...], s_ref[...])
        qn = _rope(nxt[:, QO:KO], cn_ref[...], sn_ref[...])
        kc = _rope(cur[:, KO:VO], c_ref[...], s_ref[...])
        kp = _rope(prv[:, KO:VO], cp_ref[...], sp_ref[...])
        vc, vp = cur[:, VO:], prv[:, VO:]
        do, don = d_cur[:, POOL_DIM:], d_nxt[:, POOL_DIM:]
        dl = do * at_ref[...]
        dln = don * atn_ref[...]
        lse, lsen = l_ref[...], ln_ref[...]
        ri = _iota((GQ * BLOCK, BLOCK), 0) % BLOCK
        cj = _iota((GQ * BLOCK, BLOCK), 1)
        mc = cj <= ri
        mp = jnp.logical_and(cj > ri, has_prev)
        mn = jnp.logical_and(cj > ri, has_next)
        dq_parts, dk_parts, dv_parts, dsk_vals = [], [], [], []
        for g in range(N_KV_HEADS):
            hs = slice(g * HEAD_DIM, (g + 1) * HEAD_DIM)
            qg, qng = _stack_heads(q, g) * scale, _stack_heads(qn, g) * scale
            dog, dong = _stack_heads(do, g), _stack_heads(don, g)
            delta = jnp.sum(_stack_heads(dl, g), axis=1, keepdims=True)
            deltan = jnp.sum(_stack_heads(dln, g), axis=1, keepdims=True)
            lg, lng = _stack_cols(lse, g), _stack_cols(lsen, g)
            pc = jnp.where(mc, jnp.exp(_nt(qg, kc[:, hs]) - lg), 0.0)
            pp = jnp.where(mp, jnp.exp(_nt(qg, kp[:, hs]) - lg), 0.0)
            pn = jnp.where(mn, jnp.exp(_nt(qng, kc[:, hs]) - lng), 0.0)
            dsc = pc * (_nt(dog, vc[:, hs]) - delta)
            dsp = pp * (_nt(dog, vp[:, hs]) - delta)
            dsn = pn * (_nt(dong, vc[:, hs]) - deltan)
            dqg = (_nn(dsc, kc[:, hs]) + _nn(dsp, kp[:, hs])) * scale
            dq_parts += [dqg[r * BLOCK:(r + 1) * BLOCK] for r in range(GQ)]
            dk_parts.append(_tn(dsc, qg) + _tn(dsn, qng))
            dv_parts.append(_tn(pc, dog) + _tn(pn, dong))
            sink = jnp.concatenate([jnp.full((BLOCK, 1), sk_ref[GQ * g + r], F32) for r in range(GQ)], axis=0)
            dsk = -jnp.exp(sink - lg) * delta
            dsk_vals += [jnp.sum(dsk[r * BLOCK:(r + 1) * BLOCK], axis=0, keepdims=True) for r in range(GQ)]
        dq = _rope(jnp.concatenate(dq_parts, axis=1), c_ref[...], s_ref[...], inverse=True)
        dk = _rope(jnp.concatenate(dk_parts, axis=1), c_ref[...], s_ref[...], inverse=True)
        dp_ref[:, QO:KO] = dq.astype(dp_ref.dtype)
        dp_ref[:, KO:VO] = dk.astype(dp_ref.dtype)
        dp_ref[:, VO:] = jnp.concatenate(dv_parts, axis=1).astype(dp_ref.dtype)
        lane = _iota((1, LANES), 1)
        dskp = jnp.zeros((1, LANES), F32)
        for h in range(N_HEADS):
            dskp = jnp.where(lane == h, dsk_vals[h], dskp)

        @pl.when(i == 0)
        def _():
            dps_ref[...] = dpsp
            dsk_ref[...] = dskp
            for g in range(4):
                dpw_ref[g] = dpw_parts[g]

        @pl.when(i > 0)
        def _():
            dps_ref[...] += dpsp
            dsk_ref[...] += dskp
            for g in range(4):
                dpw_ref[g] += dpw_parts[g]

    cur = lambda w: pl.BlockSpec((BLOCK, w), lambda i: (i, 0))
    prv = lambda w: pl.BlockSpec((BLOCK, w), lambda i: (jnp.maximum(i - 1, 0), 0))
    nxt = lambda w: pl.BlockSpec((BLOCK, w), lambda i: (jnp.minimum(i + 1, nb - 1), 0))
    return pl.pallas_call(
        body, grid=(nb,),
        in_specs=[cur(MIX_IN_DIM), prv(MIX_IN_DIM), nxt(MIX_IN_DIM),
                  cur(LANES), cur(LANES), prv(LANES), prv(LANES), nxt(LANES), nxt(LANES),
                  pl.BlockSpec((4, POOL_GROUP, POOL_GROUP), lambda i: (0, 0, 0)), pl.BlockSpec((1, POOL_DIM), lambda i: (0, 0)),
                  pl.BlockSpec(memory_space=pltpu.SMEM),
                  cur(Q_DIM), nxt(Q_DIM), cur(LANES), nxt(LANES), cur(2 * POOL_DIM), nxt(2 * POOL_DIM)],
        out_specs=[cur(MIX_IN_DIM), pl.BlockSpec((4, POOL_GROUP, POOL_GROUP), lambda i: (0, 0, 0)),
                   pl.BlockSpec((1, POOL_DIM), lambda i: (0, 0)), pl.BlockSpec((1, LANES), lambda i: (0, 0))],
        out_shape=[_sds((T, MIX_IN_DIM), MXU), _sds((4, POOL_GROUP, POOL_GROUP)), _sds((1, POOL_DIM)), _sds((1, LANES))],
        compiler_params=_cp("arbitrary"), name=name)(
            proj, proj, proj, cos, sin_s, cos, sin_s, cos, sin_s, pool_w, pool_scale, sinks, attn, attn, lse, lse, dcat, dcat)


SSM_TC = 512
GROUP_W = SSM_D_INNER // SSM_GROUPS


def _ssm_in_conv(h, wT, cw, cb, name):
    T, D = h.shape
    tm = min(T, 256)
    tc = 1024
    K = SSM_CONV

    def body(a_ref, b_ref, w_ref, c_ref, x_ref, pre_ref, act_ref, halo):
        @pl.when(pl.program_id(1) == 0)
        def _():
            halo[...] = jnp.zeros(halo.shape, F32)

        cur = _nt(a_ref[...], b_ref[...])
        x_ref[...] = cur
        pre = _conv_rows(cur, halo[...], w_ref[...], c_ref[...], K)
        halo[...] = cur[tm - SUBLANES:]
        pre_ref[...] = pre
        act_ref[...] = _silu(pre)

    blk = pl.BlockSpec((tm, tc), lambda j, i: (i, j))
    return pl.pallas_call(
        body, grid=(SSM_CONV_DIM // tc, T // tm),
        in_specs=[pl.BlockSpec((tm, D), lambda j, i: (i, 0)), pl.BlockSpec((tc, D), lambda j, i: (j, 0)),
                  pl.BlockSpec((K, tc), lambda j, i: (0, j)), pl.BlockSpec((1, tc), lambda j, i: (0, j))],
        out_specs=[blk, blk, blk], out_shape=[_sds((T, SSM_CONV_DIM))] * 3,
        scratch_shapes=[pltpu.VMEM((SUBLANES, tc), F32)],
        compiler_params=_cp("arbitrary", "arbitrary"), name=name)(h, wT, cw, cb)


def _ssm_pre_bwd(xbc, pre, cw, dact, name):
    T = xbc.shape[0]
    tm = min(T, 512)
    nt = T // tm
    K = SSM_CONV
    q = tm // SUBLANES
    tc = SSM_TC

    def body(x_ref, p_ref, pn_ref, d_ref, dn_ref, w_ref, dx_ref, dw_ref, db_ref):
        i = pl.program_id(1)
        w = w_ref[...]
        cur = x_ref[...]
        d_cur = d_ref[...] * _dsilu(p_ref[...])
        d_nxt = jnp.where(i == nt - 1, 0.0, dn_ref[...] * _dsilu(pn_ref[...]))
        ups = [d_cur] + [_shift_up(d_cur, d_nxt, s) for s in range(1, K)]
        dx = ups[0] * w[K - 1:K, :]
        for s in range(1, K):
            dx = dx + ups[s] * w[K - 1 - s:K - s, :]
        dx_ref[...] = dx.astype(dx_ref.dtype)
        dwp = jnp.concatenate([jnp.sum(ups[K - 1 - k] * cur, axis=0, keepdims=True) for k in range(K)], axis=0)
        dbp = jnp.sum(d_cur, axis=0, keepdims=True)

        @pl.when(i == 0)
        def _():
            dw_ref[...] = dwp
            db_ref[...] = dbp

        @pl.when(i > 0)
        def _():
            dw_ref[...] += dwp
            db_ref[...] += dbp

    nxt_row = lambda i: jnp.minimum((i + 1) * q, nt * q - 1)
    return pl.pallas_call(
        body, grid=(SSM_CONV_DIM // tc, nt),
        in_specs=[pl.BlockSpec((tm, tc), lambda j, i: (i, j)),
                  pl.BlockSpec((tm, tc), lambda j, i: (i, j)),
                  pl.BlockSpec((SUBLANES, tc), lambda j, i: (nxt_row(i), j)),
                  pl.BlockSpec((tm, tc), lambda j, i: (i, j)),
                  pl.BlockSpec((SUBLANES, tc), lambda j, i: (nxt_row(i), j)),
                  pl.BlockSpec((K, tc), lambda j, i: (0, j))],
        out_specs=[pl.BlockSpec((tm, tc), lambda j, i: (i, j)), pl.BlockSpec((K, tc), lambda j, i: (0, j)),
                   pl.BlockSpec((1, tc), lambda j, i: (0, j))],
        out_shape=[_sds((T, SSM_CONV_DIM), MXU), _sds((K, SSM_CONV_DIM)), _sds((1, SSM_CONV_DIM))],
        compiler_params=_cp("parallel", "arbitrary"), name=name)(xbc, pre, pre, dact, dact, cw)


def _dot_hi(a, b):
    return jnp.dot(a, b, precision=HI, preferred_element_type=F32)


def _ssd_common(dtraw, bias, alog):
    L = SSM_CHUNK
    xb = dtraw + bias
    dt = jnp.maximum(xb, 0.0) + jnp.log1p(jnp.exp(-jnp.abs(xb)))
    A = -jnp.exp(alog)
    tril = (_iota((L, L), 1) <= _iota((L, L), 0)).astype(F32)
    acs = _dot_hi(tril, dt * A)
    return xb, dt, A, tril, acs


def _head_selectors():
    es = (_iota((LANES, SSM_D_INNER), 0) == _iota((LANES, SSM_D_INNER), 1) // HEAD_DIM).astype(BF16)
    est = (_iota((SSM_D_INNER, LANES), 1) == _iota((SSM_D_INNER, LANES), 0) // HEAD_DIM).astype(BF16)
    return es, est


def _dot_sel(v, sel):
    hi = v.astype(BF16)
    r1 = v - hi.astype(F32)
    mid = r1.astype(BF16)
    lo = (r1 - mid.astype(F32)).astype(BF16)
    d = lambda a: jnp.dot(a, sel, preferred_element_type=F32)
    return (d(hi) + d(mid)) + d(lo)


def _expand_heads(v, es):
    return _dot_sel(v, es)


def _reduce_heads(q, est):
    return _dot_sel(q, est)


def _per_state_row(v, g):
    return jnp.concatenate([jnp.broadcast_to(v[:, GQ * g + r:GQ * g + r + 1], (HEAD_DIM, 1)) for r in range(GQ)], axis=0)


def _ssd_fwd(xact, dtraw, dt_bias, a_log, name):
    T = xact.shape[0]
    nc = T // SSM_CHUNK
    L = SSM_CHUNK
    BO, CO = SSM_D_INNER, SSM_D_INNER + SSM_GROUPS * SSM_STATE

    def body(x_ref, dt_ref, bias_ref, al_ref, es_ref, y_ref, st_ref, state):
        @pl.when(pl.program_id(0) == 0)
        def _():
            state[...] = jnp.zeros(state.shape, F32)

        _, dt, A, tril, acs = _ssd_common(dt_ref[...], bias_ref[...], al_ref[...])
        acsT = acs.T
        last = acs[L - 1:L, :]
        cd = jnp.exp(last)
        es = es_ref[...]
        dtX = _expand_heads(dt, es)
        EX = _expand_heads(jnp.exp(acs), es)
        decX = _expand_heads(jnp.exp(last - acs), es)
        for g in range(SSM_GROUPS):
            gs = slice(g * GROUP_W, (g + 1) * GROUP_W)
            B = x_ref[:, BO + g * SSM_STATE:BO + (g + 1) * SSM_STATE]
            C = x_ref[:, CO + g * SSM_STATE:CO + (g + 1) * SSM_STATE]
            X = x_ref[:, gs] * dtX[:, gs]
            CB = _nt(C, B)
            yd = []
            for r in range(GQ):
                h = GQ * g + r
                Lm = jnp.exp(jnp.where(tril > 0, acs[:, h:h + 1] - acsT[h:h + 1, :], NEG))
                yd.append(_nn(CB * Lm, X[:, r * HEAD_DIM:(r + 1) * HEAD_DIM]))
            S = state[g]
            st_ref[g] = S
            y_ref[:, gs] = jnp.concatenate(yd, axis=1) + _nt(C, S) * EX[:, gs]
            state[g] = S * _per_state_row(cd, g) + _tn(X * decX[:, gs], B)

    es, _ = _head_selectors()
    return pl.pallas_call(
        body, grid=(nc,),
        in_specs=[pl.BlockSpec((L, SSM_CONV_DIM), lambda c: (c, 0)), pl.BlockSpec((L, LANES), lambda c: (c, 0)),
                  pl.BlockSpec((1, LANES), lambda c: (0, 0)), pl.BlockSpec((1, LANES), lambda c: (0, 0)),
                  pl.BlockSpec((LANES, SSM_D_INNER), lambda c: (0, 0))],
        out_specs=[pl.BlockSpec((L, SSM_D_INNER), lambda c: (c, 0)),
                   pl.BlockSpec((None, SSM_GROUPS, GROUP_W, SSM_STATE), lambda c: (c, 0, 0, 0))],
        out_shape=[_sds((T, SSM_D_INNER)), _sds((nc, SSM_GROUPS, GROUP_W, SSM_STATE))],
        scratch_shapes=[pltpu.VMEM((SSM_GROUPS, GROUP_W, SSM_STATE), F32)],
        compiler_params=_cp("arbitrary"), name=name)(xact, dtraw, dt_bias, a_log, es)


def _ssd_bwd(xact, dtraw, dt_bias, a_log, d_skip, states, dy, name):
    T = xact.shape[0]
    nc = T // SSM_CHUNK
    L = SSM_CHUNK
    BO, CO = SSM_D_INNER, SSM_D_INNER + SSM_GROUPS * SSM_STATE

    def body(x_ref, dt_ref, bias_ref, al_ref, dsk_ref, es_ref, est_ref, st_ref, dy_ref,
             dxp_ref, ddt_ref, dbias_ref, dal_ref, dd_ref, dstate, qa, qx):
        cc = pl.program_id(0)

        @pl.when(cc == 0)
        def _():
            dstate[...] = jnp.zeros(dstate.shape, F32)

        xb, dt, A, tril, acs = _ssd_common(dt_ref[...], bias_ref[...], al_ref[...])
        acsT = acs.T
        last = acs[L - 1:L, :]
        cd = jnp.exp(last)
        es, est = es_ref[...], est_ref[...]
        dtX = _expand_heads(dt, es)
        EX = _expand_heads(jnp.exp(acs), es)
        decX = _expand_heads(jnp.exp(last - acs), es)
        lane1 = _iota((1, LANES), 1)
        lane = _iota((L, LANES), 1)
        sub = _iota((L, LANES), 0)
        ztot = jnp.zeros((1, LANES), F32)
        wrow = jnp.zeros((L, LANES), F32)
        wcolT = jnp.zeros((LANES, L), F32)
        rows_dec, rows_dd = [], []
        for g in range(SSM_GROUPS):
            gs = slice(g * GROUP_W, (g + 1) * GROUP_W)
            x = x_ref[:, gs]
            B = x_ref[:, BO + g * SSM_STATE:BO + (g + 1) * SSM_STATE]
            C = x_ref[:, CO + g * SSM_STATE:CO + (g + 1) * SSM_STATE]
            dY = dy_ref[:, gs]
            dtx, e_x, dec_x = dtX[:, gs], EX[:, gs], decX[:, gs]
            X = x * dtx
            CB = _nt(C, B)
            S = st_ref[g]
            dS_out = dstate[g]
            dcb_sum = jnp.zeros((L, L), F32)
            dxd = []
            for r in range(GQ):
                h = GQ * g + r
                hs = slice(r * HEAD_DIM, (r + 1) * HEAD_DIM)
                Lm = jnp.exp(jnp.where(tril > 0, acs[:, h:h + 1] - acsT[h:h + 1, :], NEG))
                M = CB * Lm
                dM = _nt(dY[:, hs], X[:, hs])
                dxd.append(_tn(M, dY[:, hs]))
                dcb_sum = dcb_sum + dM * Lm
                Wm = dM * M
                wrow = jnp.where(lane == h, jnp.sum(Wm, axis=1, keepdims=True), wrow)
                wcolT = jnp.where(sub == h, jnp.sum(Wm, axis=0, keepdims=True), wcolT)
            dXd = jnp.concatenate(dxd, axis=1)
            G = _nt(C, S)
            dG = dY * e_x
            dDX = _nt(B, dS_out)
            dX = dXd + dec_x * dDX
            t_dec = dDX * X * dec_x
            qa[:, gs] = dG * G - t_dec
            qx[:, gs] = dX * x
            rows_dec.append(jnp.sum(t_dec, axis=0, keepdims=True))
            rows_dd.append(jnp.sum(dY * x, axis=0, keepdims=True))
            zc = jnp.sum(dS_out * S, axis=1, keepdims=True)
            for r in range(GQ):
                ztot = jnp.where(lane1 == GQ * g + r, jnp.sum(zc[r * HEAD_DIM:(r + 1) * HEAD_DIM], axis=0, keepdims=True), ztot)
            dxp_ref[:, gs] = dX * dtx + dY * dsk_ref[:, gs]
            dxp_ref[:, BO + g * SSM_STATE:BO + (g + 1) * SSM_STATE] = _tn(dcb_sum, C) + _nn(X * dec_x, dS_out)
            dxp_ref[:, CO + g * SSM_STATE:CO + (g + 1) * SSM_STATE] = _nn(dcb_sum, B) + _nn(dG, S)
            dstate[g] = dS_out * _per_state_row(cd, g) + _tn(dG, C)
        rows = jnp.concatenate([jnp.concatenate(rows_dec, axis=1), jnp.concatenate(rows_dd, axis=1)]
                               + [jnp.zeros((SUBLANES - 2, SSM_D_INNER), F32)], axis=0)
        rsum = _reduce_heads(rows, est)
        dlast = rsum[0:1, :] + cd * ztot
        dacs = (wrow - wcolT.T) + _reduce_heads(qa[...], est) + jnp.where(sub == L - 1, dlast, 0.0)
        triu = (_iota((L, L), 0) <= _iota((L, L), 1)).astype(F32)
        da = _dot_hi(triu, dacs)
        ddtraw = (da * A + _reduce_heads(qx[...], est)) * (1.0 / (1.0 + jnp.exp(-xb)))
        ddt_ref[...] = ddtraw
        dal = jnp.sum(da * dt, axis=0, keepdims=True) * A
        ddp = rsum[1:2, :]
        dbp = jnp.sum(ddtraw, axis=0, keepdims=True)

        @pl.when(cc == 0)
        def _():
            dbias_ref[...] = dbp
            dal_ref[...] = dal
            dd_ref[...] = ddp

        @pl.when(cc > 0)
        def _():
            dbias_ref[...] += dbp
            dal_ref[...] += dal
            dd_ref[...] += ddp

    rc = lambda c: nc - 1 - c
    vec = pl.BlockSpec((1, LANES), lambda c: (0, 0))
    es, est = _head_selectors()
    return pl.pallas_call(
        body, grid=(nc,),
        in_specs=[pl.BlockSpec((L, SSM_CONV_DIM), lambda c: (rc(c), 0)), pl.BlockSpec((L, LANES), lambda c: (rc(c), 0)), vec, vec,
                  pl.BlockSpec((1, SSM_D_INNER), lambda c: (0, 0)),
                  pl.BlockSpec((LANES, SSM_D_INNER), lambda c: (0, 0)), pl.BlockSpec((SSM_D_INNER, LANES), lambda c: (0, 0)),
                  pl.BlockSpec((None, SSM_GROUPS, GROUP_W, SSM_STATE), lambda c: (rc(c), 0, 0, 0)),
                  pl.BlockSpec((L, SSM_D_INNER), lambda c: (rc(c), 0))],
        out_specs=[pl.BlockSpec((L, SSM_CONV_DIM), lambda c: (rc(c), 0)),
                   pl.BlockSpec((L, LANES), lambda c: (rc(c), 0)), vec, vec, vec],
        out_shape=[_sds((T, SSM_CONV_DIM)), _sds((T, LANES)), _sds((1, LANES)), _sds((1, LANES)), _sds((1, LANES))],
        scratch_shapes=[pltpu.VMEM((SSM_GROUPS, GROUP_W, SSM_STATE), F32), pltpu.VMEM((L, SSM_D_INNER), F32),
                        pltpu.VMEM((L, SSM_D_INNER), F32)],
        compiler_params=_cp("arbitrary"), name=name)(xact, dtraw, dt_bias, a_log, d_skip, es, est, states, dy)


def _ssm_post_fwd(y, xact, z, d_skip, nw, name):
    T = y.shape[0]
    tm = min(T, 256)
    W = SSM_D_INNER

    def body(y_ref, x_ref, z_ref, d_ref, w_ref, o_ref):
        y2 = (y_ref[...] + d_ref[...] * x_ref[...]) * _silu(z_ref[...])
        r = lax.rsqrt(jnp.mean(y2 * y2, axis=-1, keepdims=True) + SSM_NORM_EPS)
        o_ref[...] = (y2 * r * w_ref[...]).astype(o_ref.dtype)

    row = pl.BlockSpec((tm, W), lambda i: (i, 0))
    vec = pl.BlockSpec((1, W), lambda i: (0, 0))
    return pl.pallas_call(
        body, grid=(T // tm,), in_specs=[row, row, row, vec, vec], out_specs=row, out_shape=_sds((T, W), MXU),
        compiler_params=_cp("parallel"), name=name)(y, xact, z, d_skip, nw)


def _ssm_post_bwd(y, xact, z, d_skip, nw, dyn, name):
    T = y.shape[0]
    tm = min(T, 256)
    W = SSM_D_INNER

    def body(y_ref, x_ref, z_ref, d_ref, w_ref, dn_ref, dyg_ref, dz_ref, dw_ref):
        zv = z_ref[...]
        sz = _silu(zv)
        yg = y_ref[...] + d_ref[...] * x_ref[...]
        y2 = yg * sz
        r = lax.rsqrt(jnp.mean(y2 * y2, axis=-1, keepdims=True) + SSM_NORM_EPS)
        y2h = y2 * r
        dn = dn_ref[...]
        gy = dn * w_ref[...]
        dy2 = r * (gy - y2h * jnp.mean(gy * y2h, axis=-1, keepdims=True))
        dyg_ref[...] = dy2 * sz
        dz_ref[...] = (dy2 * yg * _dsilu(zv)).astype(dz_ref.dtype)
        part = jnp.sum(dn * y2h, axis=0, keepdims=True)

        @pl.when(pl.program_id(0) == 0)
        def _():
            dw_ref[...] = part

        @pl.when(pl.program_id(0) > 0)
        def _():
            dw_ref[...] += part

    row = pl.BlockSpec((tm, W), lambda i: (i, 0))
    vec = pl.BlockSpec((1, W), lambda i: (0, 0))
    return pl.pallas_call(
        body, grid=(T // tm,), in_specs=[row, row, row, vec, vec, row], out_specs=[row, row, vec],
        out_shape=[_sds((T, W)), _sds((T, W), MXU), _sds((1, W))],
        compiler_params=_cp("arbitrary"), name=name)(y, xact, z, d_skip, nw, dyn)


def _local_step(x0, cos, sin_s, target, P, fetch, token, send):
    mmf = functools.partial(_mm, tm=1024)
    big, small = {}, {}
    P = dict(P, wup={}, wdn={}, fcw={})
    h0 = _rmsnorm_fwd(x0, P["nm"][0], "norm_mix0", token=token)
    proj0 = mmf(h0, P["wmiT"], tb=True, tn=1280, tk=1024, name="mix_in")
    cat, attn, lse = _mixcore_fwd(proj0, cos, sin_s, P["pool_w"], P["pool_scale"], P["sinks"], "mixcore_fwd")
    x1, hf0 = mmf(cat, P["wmo"], tn=1024, tk=1024, res=x0, norm_w=P["nf"][0], name="mix_out")

    def ffn_fwd(xin, hf, i, next_norm):
        got = fetch(f"ffn{i}", hf)
        P["wup"][i], P["wdn"][i], P["fcw"][i] = got["wup"], got["wdn"], got["fcw"]
        hid, hc, act = _ffn_up_conv_gate(hf, P["wup"][i], P["fcw"][i], P["fcb"][i], f"ffn_up{i}")
        xout = mmf(act, P["wdn"][i], tn=1024, tk=D_FF, res=xin, norm_w=next_norm, name=f"ffn_down{i}")
        return (hid, hc), act, xout

    hid0, act0, (x2, h1) = ffn_fwd(x1, hf0, 0, P["nm"][1])
    P.update(fetch("ssm", h1))
    z = mmf(h1, P["wzT"], tb=True, tn=1024, tk=1024, name="ssm_in_z")
    xbc, xpre, xact = _ssm_in_conv(h1, P["wxbcT"], P["scw"], P["scb"], "ssm_in_xbc")
    dtraw = mmf(h1, P["wdtT"], tb=True, tn=128, tk=1024, name="ssm_in_dt")
    y, states = _ssd_fwd(xact, dtraw, P["dt_bias"], P["a_log"], "ssd_fwd")
    yn = _ssm_post_fwd(y, xact, z, P["d_exp"], P["snorm"], "ssm_post_fwd")
    x3, hf1 = mmf(yn, P["wso"], tn=1024, tk=SSM_D_INNER, res=x2, norm_w=P["nf"][1], name="ssm_out")
    hid1, act1, x4 = ffn_fwd(x3, hf1, 1, None)
    loss_row, dx4, d_nfin = _loss_head(x4, P["nfin"], target, "loss_head")
    small["norm_final"] = d_nfin

    def ffn_bwd(xin, dxo, hf, hid, act, i):
        da = mmf(dxo, P["wdn"][i], tb=True, tn=1408, tk=1024, name=f"ffn_down_dx{i}")
        big[f"ffn_w_down{i}"] = dwf(act, dxo, tm=1408, tn=1024, name=f"ffn_down_dw{i}").reshape(N_CHIPS, D_FF // N_CHIPS, D_MODEL)
        dhid, dcw, dcb = _ffn_mid_bwd(hid[0], hid[1], P["fcw"][i], da, f"ffn_mid_bwd{i}")
        big[f"ffn_w_up{i}"] = dwf(hf, dhid, tm=1024, tn=1408, out_shard_perm=(0, 2, 1, 3), name=f"ffn_up_dw{i}")
        tok = send(f"ffn{i}", [big[f"ffn_w_up{i}"], big[f"ffn_w_down{i}"]])
        dxi, dnf = _mm(dhid, P["wup"][i], tb=True, tm=512, tn=1024, tk=2816, norm_bwd=(xin, P["nf"][i], dxo, tok), name=f"ffn_up_dx{i}")
        return dxi, dnf, dcw, dcb

    dwf = functools.partial(_mm, ta=True, tk=2048, out_dtype=BF16)
    dx3, dnf1, dfcw1, dfcb1 = ffn_bwd(x3, dx4, hf1, hid1, act1, 1)
    dyn = mmf(dx3, P["wso"], tb=True, tn=1024, tk=1024, name="ssm_out_dx")
    big["ssm_w_out"] = dwf(yn, dx3, tm=1024, tn=1024, name="ssm_out_dw").reshape(N_CHIPS, SSM_D_INNER // N_CHIPS, D_MODEL)
    dyg, dz, d_snorm = _ssm_post_bwd(y, xact, z, P["d_exp"], P["snorm"], dyn, "ssm_post_bwd")
    dxact_p, ddtraw, d_dtb, d_alog, d_dskip = _ssd_bwd(xact, dtraw, P["dt_bias"], P["a_log"], P["d_exp"], states, dyg, "ssd_bwd")
    dxbc, d_scw, d_scb = _ssm_pre_bwd(xbc, xpre, P["scw"], dxact_p, "ssm_pre_bwd")
    dwsi = dwf(dz, h1, tm=1024, tn=1024, out_into=(None, SSM_IN_DIM, 0), name="ssm_in_dw_z")
    dwsi = dwf(dxbc, h1, tm=1024, tn=1024, out_into=(dwsi, SSM_IN_DIM, SSM_D_INNER // 1024), name="ssm_in_dw_xbc")
    dwdt = dwf(ddtraw, h1, tm=128, tn=1024, name="ssm_in_dw_dt")
    dwsi = _put_rows(dwsi, dwdt, SSM_HEADS, SSM_D_INNER + SSM_CONV_DIM, "ssm_in_dw_put_dt")
    big["ssm_w_in"] = dwsi.reshape(N_CHIPS, SSM_IN_DIM // N_CHIPS, D_MODEL)
    tok = send("ssm", [big["ssm_w_in"], big["ssm_w_out"]])
    dh1 = mmf(dz, P["wzT"], tn=1024, tk=2048, name="ssm_in_dx_z")
    dh1 = mmf(dxbc, P["wxbcT"], tn=1024, tk=2048, res=dh1, name="ssm_in_dx_xbc")
    dx2, dnm1 = mmf(ddtraw, P["wdtT"], tn=1024, tk=128, res=dh1, norm_bwd=(x2, P["nm"][1], dx3, tok), name="ssm_in_dx_dt")
    dx1, dnf0, dfcw0, dfcb0 = ffn_bwd(x1, dx2, hf0, hid0, act0, 0)
    dcat = mmf(dx1, P["wmo"], tb=True, tn=1024, tk=1024, name="mix_out_dx")
    big["mix_w_out"] = dwf(cat, dx1, tm=1024, tn=1024, name="mix_out_dw").reshape(N_CHIPS, D_MODEL // N_CHIPS, D_MODEL)
    dproj0, d_pw, d_ps, d_sk = _mixcore_bwd(proj0, cos, sin_s, P["pool_w"], P["pool_scale"], P["sinks"], attn, lse, dcat, "mixcore_bwd")
    big["mix_w_in"] = dwf(dproj0, h0, tm=1280, tn=1024, name="mix_in_dw").reshape(N_CHIPS, MIX_IN_DIM // N_CHIPS, D_MODEL)
    tok = send("mix", [big["mix_w_in"], big["mix_w_out"]])
    dx0, dnm0 = mmf(dproj0, P["wmiT"], tn=1024, tk=1280, norm_bwd=(x0, P["nm"][0], dx1, tok), name="mix_in_dx")

    def unperm_cols(a):
        r = a.shape[0]
        t = a.reshape(r, N_CHIPS, FFN_TC)
        return jnp.stack([t[:, p] for p in _PERM], axis=0)

    small["norm_mix"] = jnp.concatenate([dnm0, dnm1], axis=0)
    small["norm_ffn"] = jnp.concatenate([dnf0, dnf1], axis=0)
    small["pool_w"] = d_pw.reshape(4 * POOL_GROUP, POOL_GROUP)
    small["pool_scale"] = d_ps
    small["attn_sinks"] = d_sk
    small["ssm_dt_bias"] = d_dtb
    small["ssm_A_log"] = d_alog
    small["ssm_D"] = d_dskip
    fcb = jnp.stack([unperm_cols(dfcb0), unperm_cols(dfcb1)], axis=0)
    small["ffn_conv_b"] = fcb.reshape(2, 2 * D_FF)
    small["ssm_conv_w"] = d_scw.reshape(SSM_CONV, N_CHIPS, SSM_CONV_DIM // N_CHIPS).transpose(1, 0, 2)
    small["ssm_conv_b"] = d_scb.reshape(N_CHIPS, 1, SSM_CONV_DIM // N_CHIPS)
    small["ssm_norm"] = d_snorm.reshape(N_CHIPS, 1, SSM_D_INNER // N_CHIPS)
    small["ffn_conv_w"] = jnp.concatenate([unperm_cols(dfcw0), unperm_cols(dfcw1)], axis=1)
    return loss_row, dx0, big, small


ANY = pl.BlockSpec(memory_space=pl.ANY)


def _place():
    return lax.axis_index("x"), lax.axis_index("y"), lax.axis_index("c")


def _gather_shards(shards, name):
    n = len(shards)
    split = [s.size >= (1 << 16) for s in shards]

    def half(ref, a, h):
        shp = shards[a].shape
        if len(shp) == 3:
            return ref.at[h]
        r2 = shp[0] // 2
        return ref.at[pl.ds(pl.multiple_of(h * r2, 2 * SUBLANES), r2), :]

    def body(*refs):
        ins, outs = refs[:n], refs[n:2 * n]
        send, recv, fsend, frecv = refs[2 * n:]
        x, y, c = _place()
        k = 2 * x + y
        chips = [(1 - x, y), (x, 1 - y), (1 - x, 1 - y)]

        def ici(a, j, src_slot_ref, dst_slot):
            px, py = chips[j]
            src = half(src_slot_ref, a, c) if split[a] else src_slot_ref
            dst = half(outs[a].at[dst_slot], a, c) if split[a] else outs[a].at[dst_slot]
            return pltpu.make_async_remote_copy(src, dst, send.at[a, j], recv.at[a, j], device_id=(px, py, c), device_id_type=MESH)

        def d2d(a, j, h):
            px, py = chips[j]
            part = half(outs[a].at[2 * px + py], a, h)
            return pltpu.make_async_remote_copy(part, part, fsend.at[a, j], frecv.at[a, j], device_id=(x, y, 1 - c), device_id_type=MESH)

        sends = [ici(a, j, ins[a], k) for a in range(n) for j in range(3)]
        for cp in sends:
            cp.start()
        passed = []
        for a in range(n):
            for j, (px, py) in enumerate(chips):
                ici(a, j, ins[a], 2 * px + py).wait_recv()
                if split[a]:
                    passed.append(d2d(a, j, c))
                    passed[-1].start()
        for a in range(n):
            if split[a]:
                for j in range(3):
                    d2d(a, j, 1 - c).wait_recv()
        for cp in sends + passed:
            cp.wait_send()

    return pl.pallas_call(
        body, in_specs=[ANY] * n, out_specs=[ANY] * n,
        out_shape=[_sds((N_CHIPS,) + s.shape, s.dtype) for s in shards],
        scratch_shapes=[pltpu.SemaphoreType.DMA((n, 3))] * 4,
        compiler_params=pltpu.CompilerParams(has_side_effects=True), name=name)(*shards)


HBM = pl.BlockSpec(memory_space=pltpu.HBM)
SEM = pl.BlockSpec(memory_space=pltpu.SEMAPHORE)
DATAFLOW = pltpu.SideEffectType.DATAFLOW_SIDE_EFFECTING


def _spread_start(groups, slot_src, after, name):
    flat = [a for grp in groups for a in grp]
    n = len(flat)
    ng = len(groups)
    offs = [sum(len(g) for g in groups[:i]) for i in range(ng)]
    lshape = [(a.shape if slot_src else (N_CHIPS,) + a.shape) for a in flat]

    nsem = 6 * n

    def body(*refs):
        src, land = refs[:n], refs[n:2 * n]
        sems = refs[2 * n + 1:2 * n + 1 + nsem]
        token = refs[-1]
        x, y, c = _place()
        k = 2 * x + y
        chips = [(1 - x, y), (x, 1 - y), (1 - x, 1 - y)]
        for a in range(n):
            for j, (px, py) in enumerate(chips):
                s = src[a].at[2 * px + py] if slot_src else src[a]
                pltpu.make_async_remote_copy(s, land[a].at[k], sems[6 * a + 2 * j], sems[6 * a + 2 * j + 1],
                                             device_id=(px, py, c), device_id_type=MESH).start()
        token[...] = jnp.zeros(token.shape, token.dtype)

    out_shape = [pltpu.SemaphoreType.DMA(())] * nsem
    out_shape += [pltpu.HBM(a.shape, a.dtype) for a in flat] + [pltpu.HBM(s, a.dtype) for s, a in zip(lshape, flat)]
    out_shape.append(_sds((SUBLANES, LANES)))
    args = [pltpu.with_memory_space_constraint(a, pltpu.HBM) for a in flat]
    args += [pltpu.with_memory_space_constraint(lax.empty(s, a.dtype), pltpu.HBM) for s, a in zip(lshape, flat)]
    res = pl.pallas_call(
        body, name=name, out_shape=tuple(out_shape), in_specs=[HBM] * (2 * n) + [pl.BlockSpec(memory_space=pl.ANY)],
        out_specs=tuple([SEM] * nsem + [HBM] * (2 * n) + [pl.BlockSpec(memory_space=pltpu.VMEM)]),
        input_output_aliases={i: nsem + i for i in range(2 * n)},
        compiler_params=pltpu.CompilerParams(has_side_effects=DATAFLOW))(*args, after)
    sems, thru, token = res[:nsem], res[nsem:nsem + 2 * n], res[-1]
    out = []
    for gi, grp in enumerate(groups):
        sl = slice(offs[gi], offs[gi] + len(grp))
        out.append((list(sems[6 * offs[gi]:6 * (offs[gi] + len(grp))]), list(thru[:n][sl]), list(thru[n:][sl])))
    return out, token


def _spread_wait(started, slot_src, after, name):
    sems, srcs, lands = started
    n = len(srcs)

    def body(*refs):
        src, land = refs[:n], refs[n:2 * n]
        sem = refs[2 * n:2 * n + 6 * n]
        x, y, c = _place()
        chips = [(1 - x, y), (x, 1 - y), (1 - x, 1 - y)]
        for a in range(n):
            for j, (px, py) in enumerate(chips):
                s = src[a].at[2 * px + py] if slot_src else src[a]
                cp = pltpu.make_async_remote_copy(s, land[a].at[2 * px + py], sem[6 * a + 2 * j], sem[6 * a + 2 * j + 1],
                                                  device_id=(px, py, c), device_id_type=MESH)
                cp.wait_send()
                cp.wait_recv()

    res = pl.pallas_call(
        body, name=name, out_shape=tuple([pltpu.HBM(a.shape, a.dtype) for a in srcs] + [pltpu.HBM(a.shape, a.dtype) for a in lands]),
        in_specs=[HBM] * (2 * n) + [SEM] * (6 * n) + [pl.BlockSpec(memory_space=pl.ANY)], out_specs=tuple([HBM] * (2 * n)),
        input_output_aliases={i: i for i in range(2 * n)},
        compiler_params=pltpu.CompilerParams(has_side_effects=DATAFLOW))(*srcs, *lands, *sems, after)
    return list(res[:n]), list(res[n:])


def _sibling_exchange(fs, name):
    n = len(fs)

    def body(*refs):
        ins, outs = refs[:n], refs[n:2 * n]
        send, recv = refs[2 * n:]
        x, y, c = _place()
        cps = [pltpu.make_async_remote_copy(ins[a], outs[a], send.at[a], recv.at[a],
                                            device_id=(x, y, 1 - c), device_id_type=MESH) for a in range(n)]
        for cp in cps:
            cp.start()
        for cp in cps:
            cp.wait()

    return pl.pallas_call(
        body, in_specs=[ANY] * n, out_specs=[ANY] * n, out_shape=[_sds(f.shape, f.dtype) for f in fs],
        scratch_shapes=[pltpu.SemaphoreType.DMA((n,)), pltpu.SemaphoreType.DMA((n,))],
        compiler_params=pltpu.CompilerParams(has_side_effects=True), name=name)(*fs)


def _tile2d(rows, cols, budget=1024 * 1024, step=2 * SUBLANES):
    fits = [t for t in range(step, rows + 1, step) if rows % t == 0 and t * cols * 4 <= budget]
    if fits:
        return fits[-1], cols
    fits = [t for t in range(LANES, cols + 1, LANES) if cols % t == 0 and rows * t * 4 <= budget]
    assert fits, (rows, cols)
    return rows, fits[-1]


def _chip_sum(own, parts, kidx, name):
    _, R, C = parts.shape
    tr, tc = _tile2d(R, C)

    def body(k_ref, o_ref_in, p1_ref, p2_ref, p3_ref, o_ref):
        o_ref[...] = ((o_ref_in[...].astype(F32) + p1_ref[...].astype(F32)) + p2_ref[...].astype(F32)) + p3_ref[...].astype(F32)

    def slot(d):
        return pl.BlockSpec((None, tr, tc), lambda i, j, k: ((k[0] + d) % N_CHIPS, i, j))

    return pl.pallas_call(
        body,
        grid_spec=pltpu.PrefetchScalarGridSpec(
            num_scalar_prefetch=1, grid=(R // tr, C // tc), in_specs=[slot(0), slot(1), slot(2), slot(3)],
            out_specs=pl.BlockSpec((tr, tc), lambda i, j, k: (i, j))),
        out_shape=_sds((R, C)), compiler_params=_cp("parallel", "parallel"), name=name)(kidx, own, parts, parts, parts)


def _adamw_math(w, g, m, v):
    m2 = ADAM_B1 * m + (1.0 - ADAM_B1) * g
    v2 = ADAM_B2 * v + (1.0 - ADAM_B2) * (g * g)
    m_hat = m2 / (1.0 - ADAM_B1 ** ADAM_STEP)
    v_hat = v2 / (1.0 - ADAM_B2 ** ADAM_STEP)
    delta = -ADAM_LR * (m_hat / (jnp.sqrt(v_hat) + ADAM_EPS) + ADAM_WD * w)
    return delta, m2, v2


def _adamw(w, m, v, gparts, name):
    Lw, R, C = w.shape
    tr, tc = _tile2d(R, C)
    flat = [h for pair in gparts for h in pair]

    def body(*refs):
        w_ref, m_ref, v_ref = refs[:3]
        g_refs = refs[3:3 + 2 * Lw]
        go_ref, d_ref, mo_ref, vo_ref = refs[3 + 2 * Lw:]
        g = g_refs[0][...] + g_refs[1][...]
        for l in range(1, Lw):
            g = jnp.where(pl.program_id(0) == l, g_refs[2 * l][...] + g_refs[2 * l + 1][...], g)
        d, m2, v2 = _adamw_math(w_ref[...], g, m_ref[...], v_ref[...])
        go_ref[...] = g
        d_ref[...] = d
        mo_ref[...] = m2
        vo_ref[...] = v2

    blk = pl.BlockSpec((None, tr, tc), lambda l, i, j: (l, i, j))
    gblk = pl.BlockSpec((tr, tc), lambda l, i, j: (i, j))
    return pl.pallas_call(
        body, grid=(Lw, R // tr, C // tc), in_specs=[blk, blk, blk] + [gblk] * (2 * Lw), out_specs=[blk] * 4,
        out_shape=[_sds((Lw, R, C))] * 4, compiler_params=_cp("parallel", "parallel", "parallel"), name=name)(w, m, v, *flat)


def _small_adamw(grads, wmv, name):
    n = len(grads)

    def body(*refs):
        g_in, p_in, outs = refs[:n], refs[n:4 * n], refs[4 * n:]
        for a in range(n):
            g = g_in[a][...]
            d_, m2, v2 = _adamw_math(p_in[3 * a][...], g, p_in[3 * a + 1][...], p_in[3 * a + 2][...])
            outs[4 * a][...] = g
            outs[4 * a + 1][...] = d_
            outs[4 * a + 2][...] = m2
            outs[4 * a + 3][...] = v2

    vm = pl.BlockSpec(memory_space=pltpu.VMEM)
    args = list(grads) + [t for tri in wmv for t in tri]
    out_shape = [_sds(g.shape) for g in grads for _ in range(4)]
    return pl.pallas_call(body, in_specs=[vm] * len(args), out_specs=[vm] * len(out_shape), out_shape=out_shape,
                          compiler_params=pltpu.CompilerParams(vmem_limit_bytes=V7X_VMEM_LIMIT), name=name)(*args)


def _small_allreduce(partials, pshapes, loss_row, name):
    n = len(partials)
    gshapes = [p.shape for p in partials] + [loss_row.shape]
    ng = n + 1

    def body(*refs):
        g_in = refs[:ng]
        outs = refs[ng:2 * ng]
        sib = refs[2 * ng:3 * ng]
        pair = refs[3 * ng:4 * ng]
        bufs = refs[4 * ng:5 * ng]
        send1, recv1, send2, recv2 = refs[-4:]
        x, y, c = _place()
        k = 2 * x + y
        chips = [(1 - x, y), (x, 1 - y), (1 - x, 1 - y)]
        swaps = [pltpu.make_async_remote_copy(g_in[a], sib[a], send1.at[a], recv1.at[a],
                                              device_id=(x, y, 1 - c), device_id_type=MESH) for a in range(ng)]
        for cp in swaps:
            cp.start()
        for a, cp in enumerate(swaps):
            cp.wait()
            pair[a][...] = g_in[a][...] + sib[a][...]
            bufs[a][k] = pair[a][...]
        sends = [pltpu.make_async_remote_copy(pair[a], bufs[a].at[k], send2.at[a, j], recv2.at[a, j],
                                              device_id=(px, py, c), device_id_type=MESH)
                 for a in range(ng) for j, (px, py) in enumerate(chips)]
        for cp in sends:
            cp.start()
        for a in range(ng):
            for j, (px, py) in enumerate(chips):
                pltpu.make_async_remote_copy(pair[a], bufs[a].at[2 * px + py], send2.at[a, j], recv2.at[a, j],
                                             device_id=(px, py, c), device_id_type=MESH).wait_recv()
        for cp in sends:
            cp.wait_send()
        for a in range(ng):
            sharded = len(gshapes[a]) == 3

            def part(d):
                return bufs[a][d, k] if sharded else bufs[a][d]

            tot = part(0)
            for d in range(1, N_CHIPS):
                tot = tot + part(d)
            if a == n:
                outs[n][...] = tot
            else:
                pr, pc = pshapes[a]
                outs[a][...] = tot[:pr, :pc]

    vm = pl.BlockSpec(memory_space=pltpu.VMEM)
    args = list(partials) + [loss_row]
    out_shape = [_sds(ps) for ps in pshapes] + [_sds(loss_row.shape)]
    return pl.pallas_call(
        body, in_specs=[vm] * len(args), out_specs=[vm] * len(out_shape), out_shape=out_shape,
        scratch_shapes=[pltpu.VMEM(tuple(s), F32) for s in gshapes] * 2 + [pltpu.VMEM((N_CHIPS,) + tuple(s), F32) for s in gshapes]
        + [pltpu.SemaphoreType.DMA((ng,)), pltpu.SemaphoreType.DMA((ng,)),
           pltpu.SemaphoreType.DMA((ng, 3)), pltpu.SemaphoreType.DMA((ng, 3))],
        compiler_params=pltpu.CompilerParams(has_side_effects=True, vmem_limit_bytes=V7X_VMEM_LIMIT), name=name)(*args)


_PERM = (0, 2, 1, 3)


def _cols_from_shards(g):
    return g.transpose(1, 0, 2).reshape(g.shape[1], N_CHIPS * g.shape[2])


def _rope_tables(positions):
    inv_freq = ROPE_THETA ** (-jnp.arange(0, HEAD_DIM, 2, dtype=F32) / HEAD_DIM)
    ang = positions.astype(F32).reshape(-1, 1) * inv_freq
    cos, sin = jnp.cos(ang), jnp.sin(ang)
    cos = jnp.concatenate([cos, cos, cos, cos], axis=-1)
    sin_s = jnp.concatenate([-sin, sin, -sin, sin], axis=-1)
    return cos, sin_s


def kernel(x, positions, norm_mix, norm_ffn, norm_final, mix_w_in, pool_w, pool_scale, attn_sinks, mix_w_out, ssm_w_in, ssm_conv_w, ssm_conv_b, ssm_dt_bias, ssm_A_log, ssm_D, ssm_norm, ssm_w_out, ffn_w_up, ffn_conv_w, ffn_conv_b, ffn_w_down, loss_target, m_norm_mix, m_norm_ffn, m_norm_final, m_mix_w_in, m_pool_w, m_pool_scale, m_attn_sinks, m_mix_w_out, m_ssm_w_in, m_ssm_conv_w, m_ssm_conv_b, m_ssm_dt_bias, m_ssm_A_log, m_ssm_D, m_ssm_norm, m_ssm_w_out, m_ffn_w_up, m_ffn_conv_w, m_ffn_conv_b, m_ffn_w_down, v_norm_mix, v_norm_ffn, v_norm_final, v_mix_w_in, v_pool_w, v_pool_scale, v_attn_sinks, v_mix_w_out, v_ssm_w_in, v_ssm_conv_w, v_ssm_conv_b, v_ssm_dt_bias, v_ssm_A_log, v_ssm_D, v_ssm_norm, v_ssm_w_out, v_ffn_w_up, v_ffn_conv_w, v_ffn_conv_b, v_ffn_w_down):
    W = dict(norm_mix=norm_mix, norm_ffn=norm_ffn, norm_final=norm_final, mix_w_in=mix_w_in, pool_w=pool_w, pool_scale=pool_scale, attn_sinks=attn_sinks, mix_w_out=mix_w_out, ssm_w_in=ssm_w_in, ssm_conv_w=ssm_conv_w, ssm_conv_b=ssm_conv_b, ssm_dt_bias=ssm_dt_bias, ssm_A_log=ssm_A_log, ssm_D=ssm_D, ssm_norm=ssm_norm, ssm_w_out=ssm_w_out, ffn_w_up=ffn_w_up, ffn_conv_w=ffn_conv_w, ffn_conv_b=ffn_conv_b, ffn_w_down=ffn_w_down)
    Mo = dict(norm_mix=m_norm_mix, norm_ffn=m_norm_ffn, norm_final=m_norm_final, mix_w_in=m_mix_w_in, pool_w=m_pool_w, pool_scale=m_pool_scale, attn_sinks=m_attn_sinks, mix_w_out=m_mix_w_out, ssm_w_in=m_ssm_w_in, ssm_conv_w=m_ssm_conv_w, ssm_conv_b=m_ssm_conv_b, ssm_dt_bias=m_ssm_dt_bias, ssm_A_log=m_ssm_A_log, ssm_D=m_ssm_D, ssm_norm=m_ssm_norm, ssm_w_out=m_ssm_w_out, ffn_w_up=m_ffn_w_up, ffn_conv_w=m_ffn_conv_w, ffn_conv_b=m_ffn_conv_b, ffn_w_down=m_ffn_w_down)
    Vo = dict(norm_mix=v_norm_mix, norm_ffn=v_norm_ffn, norm_final=v_norm_final, mix_w_in=v_mix_w_in, pool_w=v_pool_w, pool_scale=v_pool_scale, attn_sinks=v_attn_sinks, mix_w_out=v_mix_w_out, ssm_w_in=v_ssm_w_in, ssm_conv_w=v_ssm_conv_w, ssm_conv_b=v_ssm_conv_b, ssm_dt_bias=v_ssm_dt_bias, ssm_A_log=v_ssm_A_log, ssm_D=v_ssm_D, ssm_norm=v_ssm_norm, ssm_w_out=v_ssm_w_out, ffn_w_up=v_ffn_w_up, ffn_conv_w=v_ffn_conv_w, ffn_conv_b=v_ffn_conv_b, ffn_w_down=v_ffn_w_down)

    kchip = 2 * lax.axis_index("x") + lax.axis_index("y")

    def own_slot(g, own):
        return lax.dynamic_update_slice_in_dim(g, own[None], kchip, axis=0)

    def tr(t):
        return jnp.swapaxes(t[0], 0, 1)

    later = dict(ffn0=[ffn_w_up[0].astype(MXU), ffn_w_down[0].astype(MXU)],
                 ssm=[tr(ssm_w_in).astype(MXU), ssm_w_out[0].astype(MXU)],
                 ffn1=[ffn_w_up[1].astype(MXU), ffn_w_down[1].astype(MXU)])
    sh = [tr(mix_w_in).astype(MXU), mix_w_out[0].astype(MXU), ssm_conv_w[0], ssm_conv_b, ssm_norm, ffn_conv_w]
    first = _gather_shards(sh, "gather_first")
    g_mi, g_mo, g_scw, g_scb, g_sn, g_fcw = [own_slot(g, own) for g, own in zip(first, sh)]
    started, token = _spread_start(list(later.values()), False, first[0], "gather_start")
    started = dict(zip(later.keys(), started))
    fcw = [jnp.concatenate([g_fcw[p, i] for p in _PERM], axis=1) for i in range(2)]
    P = dict(
        nm=norm_mix, nf=norm_ffn, nfin=norm_final,
        wmiT=g_mi.reshape(MIX_IN_DIM, D_MODEL), wmo=g_mo.reshape(D_MODEL, D_MODEL),
        pool_w=pool_w[0], pool_scale=pool_scale, sinks=attn_sinks[0],
        scw=_cols_from_shards(g_scw), scb=g_scb.reshape(1, SSM_CONV_DIM), snorm=g_sn.reshape(1, SSM_D_INNER),
        dt_bias=jnp.pad(ssm_dt_bias, ((0, 0), (0, LANES - SSM_HEADS))), a_log=jnp.pad(ssm_A_log, ((0, 0), (0, LANES - SSM_HEADS))),
        d_exp=jnp.repeat(ssm_D, SSM_D_INNER // SSM_HEADS, axis=1),
        fcb=[jnp.concatenate([ffn_conv_b[i:i + 1, p * FFN_TC:(p + 1) * FFN_TC] for p in _PERM], axis=1) for i in range(2)],
    )

    def fetch(group, after):
        owns, lands = _spread_wait(started[group], False, after, f"gather_wait_{group}")
        a, b = [own_slot(g, own) for g, own in zip(lands, owns)]
        if group == "ssm":
            wsi = a.reshape(SSM_IN_DIM, D_MODEL)
            zx = SSM_D_INNER + SSM_CONV_DIM
            return dict(wzT=wsi[:SSM_D_INNER], wxbcT=wsi[SSM_D_INNER:zx],
                        wdtT=jnp.pad(wsi[zx:], ((0, LANES - SSM_HEADS), (0, 0))), wso=b.reshape(SSM_D_INNER, D_MODEL))
        i = int(group[-1])
        return dict(wup=jnp.concatenate([a[p] for p in _PERM], axis=1), wdn=b.reshape(D_FF, D_MODEL), fcw=fcw[i])

    cos, sin_s = _rope_tables(positions)
    sent = {}

    def send(group, grads):
        res, tok = _spread_start([grads], True, jnp.zeros((SUBLANES, LANES), F32), f"grad_start_{group}")
        sent[group] = res[0]
        return tok

    loss_row, grad_x, big, small = _local_step(x[0], cos, sin_s, loss_target[0], P, fetch, token, send)

    kidx = kchip.astype(jnp.int32).reshape(1)
    group_names = dict(ffn1=["ffn_w_up1", "ffn_w_down1"], ssm=["ssm_w_in", "ssm_w_out"], ffn0=["ffn_w_up0", "ffn_w_down0"],
                       mix=["mix_w_in", "mix_w_out"])
    names, mine = [], []
    for group, started_g in sent.items():
        grads, lands = _spread_wait(started_g, True, grad_x, f"grad_wait_{group}")
        for nm, g, land in zip(group_names[group], grads, lands):
            names.append(nm)
            mine.append(_chip_sum(g, land, kidx, f"chip_sum_{nm}"))
    theirs = _sibling_exchange(mine, "sibling_exchange")
    red = {nm: (a, b) for nm, a, b in zip(names, mine, theirs)}

    out = {}

    def big_update(pname, gparts, transposed=False):
        w = W[pname]
        lw = len(gparts)
        shp = w.shape
        rr, cc = gparts[0][0].shape
        fix = (lambda t: tr(t)[None]) if transposed else (lambda t: t.reshape(lw, rr, cc))
        res = _adamw(fix(w), fix(Mo[pname]), fix(Vo[pname]), gparts, f"adamw_{pname}")
        out[pname] = tuple((tr(r)[None] if transposed else r.reshape(shp)) for r in res)

    big_update("mix_w_in", [red["mix_w_in"]], transposed=True)
    big_update("mix_w_out", [red["mix_w_out"]])
    big_update("ssm_w_in", [red["ssm_w_in"]], transposed=True)
    big_update("ssm_w_out", [red["ssm_w_out"]])
    big_update("ffn_w_up", [red["ffn_w_up0"], red["ffn_w_up1"]])
    big_update("ffn_w_down", [red["ffn_w_down0"], red["ffn_w_down1"]])

    small_names = ["norm_mix", "norm_ffn", "norm_final", "pool_w", "pool_scale", "attn_sinks", "ssm_dt_bias", "ssm_A_log",
                   "ssm_D", "ffn_conv_b", "ssm_conv_w", "ssm_conv_b", "ssm_norm", "ffn_conv_w"]

    def as2d(t):
        if t.ndim == 1:
            return t.reshape(1, -1)
        return t.reshape(-1, t.shape[-1])

    wmv = [(as2d(W[nm]), as2d(Mo[nm]), as2d(Vo[nm])) for nm in small_names]
    summed = _small_allreduce([small[nm] for nm in small_names], [t[0].shape for t in wmv], loss_row, "small_allreduce")
    res = _small_adamw(summed[:-1], wmv, "small_adamw")
    for a, nm in enumerate(small_names):
        out[nm] = tuple(r.reshape(W[nm].shape) for r in res[4 * a:4 * a + 4])
    loss = summed[-1][0, 0]

    order = ["norm_mix", "norm_ffn", "norm_final", "mix_w_in", "pool_w", "pool_scale", "attn_sinks", "mix_w_out", "ssm_w_in",
             "ssm_conv_w", "ssm_conv_b", "ssm_dt_bias", "ssm_A_log", "ssm_D", "ssm_norm", "ssm_w_out", "ffn_w_up", "ffn_conv_w",
             "ffn_conv_b", "ffn_w_down"]
    return (loss, grad_x.reshape(x.shape), *[out[nm][0] for nm in order], *[out[nm][1] for nm in order],
            *[out[nm][2] for nm in order], *[out[nm][3] for nm in order])
```

```python
import functools

import jax
import jax.numpy as jnp
from jax import lax
from jax.experimental import pallas as pl
from jax.experimental.pallas import tpu as pltpu

F32 = jnp.float32
BF16 = jnp.bfloat16
MXU = BF16
HI = lax.Precision.HIGHEST

D_MODEL = 1024
POOL_WINDOWS = (2, 4, 8, 16)
POOL_DIM = 512
POOL_GROUP = 128
HEAD_DIM = 64
N_HEADS = 8
N_KV_HEADS = 2
GQ = 4
Q_DIM = 512
KV_DIM = 128
BLOCK = 128
ROPE_THETA = 10000.0
MIX_IN_DIM = 1280
SSM_D_INNER = 2048
SSM_HEADS = 32
SSM_GROUPS = 8
SSM_STATE = 128
SSM_CONV = 4
SSM_CHUNK = 128
SSM_CONV_DIM = 4096
SSM_IN_DIM = 6176
D_FF = 2816
FFN_CONV = 3
NORM_EPS = 1e-6
SSM_NORM_EPS = 1e-5
ADAM_LR = 0.001
ADAM_B1 = 0.9
ADAM_B2 = 0.999
ADAM_EPS = 1e-08
ADAM_WD = 0.01
ADAM_STEP = 10

N_CHIPS = 4
N_DEV = 8
LANES = 128
SUBLANES = 8
V7X_VMEM_LIMIT = 56 * 1024 * 1024
NEG = -1e30
MESH = pl.DeviceIdType.MESH


def _cp(*sem):
    return pltpu.CompilerParams(dimension_semantics=sem if sem else None, vmem_limit_bytes=V7X_VMEM_LIMIT)


def _sds(shape, dtype=F32):
    return jax.ShapeDtypeStruct(tuple(shape), dtype)


def _iota(shape, dim):
    return lax.broadcasted_iota(jnp.int32, shape, dim)


def _silu(x):
    return x * (1.0 / (1.0 + jnp.exp(-x)))


def _dsilu(x):
    s = 1.0 / (1.0 + jnp.exp(-x))
    return s * (1.0 + x * (1.0 - s))


def _mm(a, b, *, ta=False, tb=False, tm, tn, tk, res=None, out_dtype=F32, out_shard_perm=None, out_into=None, norm_w=None,
        norm_bwd=None, name):
    M, K = (a.shape[1], a.shape[0]) if ta else a.shape
    N = b.shape[0] if tb else b.shape[1]
    tm, tn, tk = min(tm, M), min(tn, N), min(tk, K)
    gm, gn, gk = M // tm, N // tn, K // tk
    assert gm * tm == M and gn * tn == N and gk * tk == K, (name, M, N, K, tm, tn, tk)
    a_spec = pl.BlockSpec((tk, tm), lambda i, j, k: (k, i)) if ta else pl.BlockSpec((tm, tk), lambda i, j, k: (i, k))
    b_spec = pl.BlockSpec((tn, tk), lambda i, j, k: (j, k)) if tb else pl.BlockSpec((tk, tn), lambda i, j, k: (k, j))
    dims = (((0 if ta else 1,), (1 if tb else 0,)), ((), ()))
    has_res = res is not None
    has_nw = norm_w is not None
    has_nb = norm_bwd is not None
    has_tok = has_nb and norm_bwd[3] is not None
    assert not (has_nw or has_nb) or (gn == 1 and out_shard_perm is None)
    n_extra = has_res + has_nw + (3 + has_tok if has_nb else 0)

    def body(*refs):
        a_ref, b_ref = refs[0], refs[1]
        extra = list(refs[2:2 + n_extra])
        outs = refs[len(args):]
        r_ref = extra.pop(0) if has_res else None
        nw_ref = extra.pop(0) if has_nw else None
        nb_refs = extra if has_nb else None

        def dot():
            return lax.dot_general(a_ref[...].astype(MXU), b_ref[...].astype(MXU), dims, preferred_element_type=F32)

        def finish(r):
            if has_res:
                r = r + r_ref[...]
            if has_nb:
                xv = nb_refs[0][...]
                rs = lax.rsqrt(jnp.mean(xv * xv, axis=-1, keepdims=True) + NORM_EPS)
                xh = xv * rs
                g = r * nb_refs[1][...]
                dr = nb_refs[2][...] + nb_refs[3][0:1, 0:1] if has_tok else nb_refs[2][...]
                outs[0][...] = dr + rs * (g - xh * jnp.mean(g * xh, axis=-1, keepdims=True))
                part = jnp.sum(r * xh, axis=0, keepdims=True)
                i = pl.program_id(0)

                @pl.when(i == 0)
                def _():
                    outs[1][...] = part

                @pl.when(i > 0)
                def _():
                    outs[1][...] += part
                return
            outs[0][...] = r.astype(out_dtype)
            if has_nw:
                rs = lax.rsqrt(jnp.mean(r * r, axis=-1, keepdims=True) + NORM_EPS)
                outs[1][...] = (r * rs * nw_ref[...]).astype(outs[1].dtype)

        if gk == 1:
            finish(dot())
        else:
            acc = refs[-1]
            k = pl.program_id(2)

            @pl.when(k == 0)
            def _():
                acc[...] = dot()

            if gk > 2:
                @pl.when(jnp.logical_and(k > 0, k < gk - 1))
                def _():
                    acc[...] += dot()

            @pl.when(k == gk - 1)
            def _():
                finish(acc[...] + dot())

    tile = pl.BlockSpec((tm, tn), lambda i, j, k: (i, j))
    row = pl.BlockSpec((1, tn), lambda i, j, k: (0, j))
    in_specs = [a_spec, b_spec]
    args = [a, b]
    if has_res:
        in_specs.append(tile)
        args.append(res)
    if has_nw:
        in_specs.append(row)
        args.append(norm_w.reshape(1, N))
    if has_nb:
        in_specs += [tile, row, tile]
        args += [norm_bwd[0], norm_bwd[1].reshape(1, N), norm_bwd[2]]
        if has_tok:
            in_specs.append(pl.BlockSpec((SUBLANES, LANES), lambda i, j, k: (0, 0)))
            args.append(norm_bwd[3])
    alias = {}
    if out_into is not None:
        buf, rows, off = out_into
        out_spec = pl.BlockSpec((tm, tn), lambda i, j, k: (i + off, j))
        out_shape = _sds((rows, N), out_dtype)
        if buf is not None:
            alias = {len(args): 0}
            in_specs.append(pl.BlockSpec(memory_space=pl.ANY))
            args.append(buf)
    elif out_shard_perm is None:
        out_spec = tile
        out_shape = _sds((M, N), out_dtype)
    else:
        assert gn == len(out_shard_perm) == 4 and tuple(out_shard_perm) == (0, 2, 1, 3)
        out_spec = pl.BlockSpec((None, tm, tn), lambda i, j, k: ((j % 2) * 2 + j // 2, i, 0))
        out_shape = _sds((gn, M, tn), out_dtype)
    sem = ("parallel", "parallel", "arbitrary")
    if has_nw:
        out_spec, out_shape = [out_spec, tile], [out_shape, _sds((M, N), MXU)]
    if has_nb:
        out_spec, out_shape = [tile, row], [_sds((M, N)), _sds((1, N))]
        sem = ("arbitrary", "arbitrary", "arbitrary")
    return pl.pallas_call(
        body, grid=(gm, gn, gk), in_specs=in_specs, out_specs=out_spec, out_shape=out_shape,
        scratch_shapes=[pltpu.VMEM((tm, tn), F32)] if gk > 1 else [], input_output_aliases=alias,
        compiler_params=_cp(*sem), name=name)(*args)


def _put_rows(buf, src, rows, at, name):
    assert at % rows == 0 and src.shape[1] == buf.shape[1] and src.dtype == buf.dtype
    C = buf.shape[1]

    def body(s_ref, b_ref, o_ref):
        o_ref[...] = s_ref[...]

    return pl.pallas_call(
        body, grid=(1,), in_specs=[pl.BlockSpec((rows, C), lambda i: (0, 0)), pl.BlockSpec(memory_space=pl.ANY)],
        out_specs=pl.BlockSpec((rows, C), lambda i: (at // rows, 0)), out_shape=_sds(buf.shape, buf.dtype),
        input_output_aliases={1: 0}, compiler_params=_cp("arbitrary"), name=name)(src, buf)


def _rmsnorm_fwd(x, w, name, token=None):
    T, D = x.shape
    tm = min(T, 512)
    has_token = token is not None

    def body(*refs):
        x_ref, w_ref, o_ref = refs[0], refs[1], refs[-1]
        xv = x_ref[...]
        if has_token:
            xv = xv + refs[2][0:1, 0:1]
        r = lax.rsqrt(jnp.mean(xv * xv, axis=-1, keepdims=True) + NORM_EPS)
        o_ref[...] = (xv * r * w_ref[...]).astype(o_ref.dtype)

    in_specs = [pl.BlockSpec((tm, D), lambda i: (i, 0)), pl.BlockSpec((1, D), lambda i: (0, 0))]
    args = [x, w.reshape(1, D)]
    if has_token:
        in_specs.append(pl.BlockSpec((SUBLANES, LANES), lambda i: (0, 0)))
        args.append(token)
    return pl.pallas_call(
        body, grid=(T // tm,), in_specs=in_specs,
        out_specs=pl.BlockSpec((tm, D), lambda i: (i, 0)), out_shape=_sds((T, D), MXU),
        compiler_params=_cp("parallel"), name=name)(*args)


def _loss_head(x, w, target, name):
    T, D = x.shape
    tm = min(T, 512)

    def body(x_ref, w_ref, t_ref, loss_ref, dx_ref, dw_ref):
        xv = x_ref[...]
        r = lax.rsqrt(jnp.mean(xv * xv, axis=-1, keepdims=True) + NORM_EPS)
        xh = xv * r
        wv = w_ref[...]
        e = xh * wv - t_ref[...]
        lpart = 0.5 * jnp.sum(jnp.mean(e * e, axis=-1, keepdims=True), axis=0, keepdims=True)
        dy = e * (1.0 / D)
        g = dy * wv
        dx_ref[...] = r * (g - xh * jnp.mean(g * xh, axis=-1, keepdims=True))
        part = jnp.sum(dy * xh, axis=0, keepdims=True)
        lrow = jnp.broadcast_to(lpart, (1, LANES))

        @pl.when(pl.program_id(0) == 0)
        def _():
            dw_ref[...] = part
            loss_ref[...] = lrow

        @pl.when(pl.program_id(0) > 0)
        def _():
            dw_ref[...] += part
            loss_ref[...] += lrow

    row = pl.BlockSpec((tm, D), lambda i: (i, 0))
    vec = pl.BlockSpec((1, D), lambda i: (0, 0))
    return pl.pallas_call(
        body, grid=(T // tm,), in_specs=[row, vec, row],
        out_specs=[pl.BlockSpec((1, LANES), lambda i: (0, 0)), row, vec],
        out_shape=[_sds((1, LANES)), _sds((T, D)), _sds((1, D))],
        compiler_params=_cp("arbitrary"), name=name)(x, w.reshape(1, D), target)


def _shift_down(cur, prev8, s):
    if s == 0:
        return cur
    tm = cur.shape[0]
    rc = pltpu.roll(cur, s, 0)
    top = jnp.where(_iota((SUBLANES, cur.shape[1]), 0) < s, pltpu.roll(prev8, s, 0), rc[:SUBLANES])
    return jnp.concatenate([top, rc[SUBLANES:]], axis=0) if tm > SUBLANES else top


def _shift_up(cur, next8, s):
    if s == 0:
        return cur
    tm = cur.shape[0]
    rc = pltpu.roll(cur, tm - s, 0)
    bot = jnp.where(_iota((SUBLANES, cur.shape[1]), 0) >= SUBLANES - s, pltpu.roll(next8, SUBLANES - s, 0), rc[tm - SUBLANES:])
    return jnp.concatenate([rc[:tm - SUBLANES], bot], axis=0) if tm > SUBLANES else bot


def _conv_rows(cur, prev8, w, b, K):
    acc = cur * w[K - 1:K, :] + b
    for s in range(1, K):
        acc = acc + _shift_down(cur, prev8, s) * w[K - 1 - s:K - s, :]
    return acc


FFN_TC = 1408
HALO16 = 2 * SUBLANES


def _ffn_up_conv_gate(hf, wup, cw, cb, name):
    T, D = hf.shape
    tm = min(T, 256)
    nt, nj = T // tm, D_FF // FFN_TC
    K = FFN_CONV
    W2 = 2 * FFN_TC

    def body(a_ref, b_ref, w_ref, c_ref, hid_ref, hc_ref, act_ref, halo):
        i = pl.program_id(1)

        @pl.when(i == 0)
        def _():
            halo[...] = jnp.zeros(halo.shape, F32)

        hb = jnp.dot(a_ref[...].astype(MXU), b_ref[...].astype(MXU), preferred_element_type=F32).astype(hid_ref.dtype)
        hid_ref[...] = hb
        cur = hb.astype(F32)
        hc = _conv_rows(cur, halo[...], w_ref[...], c_ref[...], K)
        halo[...] = cur[tm - SUBLANES:]
        hc_ref[...] = hc
        act_ref[...] = (_silu(hc[:, FFN_TC:]) * hc[:, :FFN_TC]).astype(act_ref.dtype)

    blk = pl.BlockSpec((tm, W2), lambda j, i: (i, j))
    return pl.pallas_call(
        body, grid=(nj, nt),
        in_specs=[pl.BlockSpec((tm, D), lambda j, i: (i, 0)), pl.BlockSpec((D, W2), lambda j, i: (0, j)),
                  pl.BlockSpec((K, W2), lambda j, i: (0, j)), pl.BlockSpec((1, W2), lambda j, i: (0, j))],
        out_specs=[blk, blk, pl.BlockSpec((tm, FFN_TC), lambda j, i: (i, j))],
        out_shape=[_sds((T, 2 * D_FF), MXU), _sds((T, 2 * D_FF)), _sds((T, D_FF), MXU)],
        scratch_shapes=[pltpu.VMEM((SUBLANES, W2), F32)],
        compiler_params=_cp("arbitrary", "arbitrary"), name=name)(hf, wup, cw, cb)


def _ffn_mid_bwd(hid, hc, cw, da, name):
    T = hid.shape[0]
    tm = min(T, 256)
    nt, nj = T // tm, D_FF // FFN_TC
    K = FFN_CONV
    W2 = 2 * FFN_TC

    def body(h_ref, c_ref, cn_ref, da_ref, dan_ref, w_ref, dh_ref, dw_ref, db_ref):
        i = pl.program_id(1)
        w = w_ref[...]
        cur = h_ref[...].astype(F32)
        last = i == nt - 1

        def dpre(hcv, dav):
            u, g = hcv[:, :FFN_TC], hcv[:, FFN_TC:]
            return jnp.concatenate([dav * _silu(g), dav * u * _dsilu(g)], axis=1)

        d_cur = dpre(c_ref[...], da_ref[...])
        d_nxt = jnp.where(last, 0.0, dpre(cn_ref[...], dan_ref[...]))
        ups = [d_cur] + [_shift_up(d_cur, d_nxt, s) for s in range(1, K)]
        dh = ups[0] * w[K - 1:K, :]
        for s in range(1, K):
            dh = dh + ups[s] * w[K - 1 - s:K - s, :]
        dh_ref[...] = dh.astype(dh_ref.dtype)
        dwp = jnp.concatenate([jnp.sum(ups[K - 1 - k] * cur, axis=0, keepdims=True) for k in range(K)], axis=0)
        dbp = jnp.sum(d_cur, axis=0, keepdims=True)

        @pl.when(i == 0)
        def _():
            dw_ref[...] = dwp
            db_ref[...] = dbp

        @pl.when(i > 0)
        def _():
            dw_ref[...] += dwp
            db_ref[...] += dbp

    q = tm // SUBLANES
    blk = pl.BlockSpec((tm, W2), lambda j, i: (i, j))
    nxt = pl.BlockSpec((SUBLANES, W2), lambda j, i: (jnp.minimum((i + 1) * q, nt * q - 1), j))
    dab = pl.BlockSpec((tm, FFN_TC), lambda j, i: (i, j))
    dan = pl.BlockSpec((SUBLANES, FFN_TC), lambda j, i: (jnp.minimum((i + 1) * q, nt * q - 1), j))
    return pl.pallas_call(
        body, grid=(nj, nt),
        in_specs=[blk, blk, nxt, dab, dan, pl.BlockSpec((K, W2), lambda j, i: (0, j))],
        out_specs=[blk, pl.BlockSpec((K, W2), lambda j, i: (0, j)), pl.BlockSpec((1, W2), lambda j, i: (0, j))],
        out_shape=[_sds((T, 2 * D_FF), MXU), _sds((K, 2 * D_FF)), _sds((1, 2 * D_FF))],
        compiler_params=_cp("parallel", "arbitrary"), name=name)(hid, hc, hc, da, da, cw)


def _rope(t, cos, sin_s, inverse=False):
    n = t.shape[1] // LANES
    c = jnp.concatenate([cos] * n, axis=1) if n > 1 else cos
    s = jnp.concatenate([sin_s] * n, axis=1) if n > 1 else sin_s
    a = pltpu.roll(t, HEAD_DIM // 2, 1)
    b = pltpu.roll(t, t.shape[1] - HEAD_DIM // 2, 1)
    first = (_iota(t.shape, 1) % HEAD_DIM) < HEAD_DIM // 2
    rot = jnp.where(first, b, a) * s
    return t * c - rot if inverse else t * c + rot


def _stack_heads(t, g):
    return jnp.concatenate([t[:, (GQ * g + r) * HEAD_DIM:(GQ * g + r + 1) * HEAD_DIM] for r in range(GQ)], axis=0)


def _stack_cols(t, g):
    return jnp.concatenate([t[:, GQ * g + r:GQ * g + r + 1] for r in range(GQ)], axis=0)


def _pool_sums(prev, cur, w):
    s = jnp.concatenate([prev, cur], axis=0)
    sh = 1
    while sh < w:
        s = s + pltpu.roll(s, sh, 0)
        sh *= 2
    return s[BLOCK:]


def _nt(a, b):
    return lax.dot_general(a.astype(MXU), b.astype(MXU), (((1,), (1,)), ((), ())), preferred_element_type=F32)


def _tn(a, b):
    return lax.dot_general(a.astype(MXU), b.astype(MXU), (((0,), (0,)), ((), ())), preferred_element_type=F32)


def _nn(a, b):
    return jnp.dot(a.astype(MXU), b.astype(MXU), preferred_element_type=F32)


def _mixcore_fwd(proj, cos, sin_s, pool_w, pool_scale, sinks, name):
    T = proj.shape[0]
    nb = T // BLOCK
    scale = HEAD_DIM ** -0.5

    def body(p_ref, pp_ref, c_ref, s_ref, cp_ref, sp_ref, pw_ref, ps_ref, sk_ref, cat_ref, at_ref, lse_ref):
        i = pl.program_id(0)
        has_prev = i > 0
        cur = p_ref[...]
        prv = jnp.where(has_prev, pp_ref[...], 0.0)
        tpos = (i * BLOCK + _iota((BLOCK, 1), 0) + 1).astype(F32)
        for g, w in enumerate(POOL_WINDOWS):
            sl = slice(g * POOL_GROUP, (g + 1) * POOL_GROUP)
            pooled = _pool_sums(prv[:, sl], cur[:, sl], w) / jnp.minimum(tpos, float(w)) - cur[:, sl]
            cat_ref[:, sl] = (_nn(pooled, pw_ref[g]) * ps_ref[:, sl]).astype(cat_ref.dtype)
        q = _rope(cur[:, POOL_DIM:POOL_DIM + Q_DIM], c_ref[...], s_ref[...])
        kc = _rope(cur[:, POOL_DIM + Q_DIM:POOL_DIM + Q_DIM + KV_DIM], c_ref[...], s_ref[...])
        kp = _rope(prv[:, POOL_DIM + Q_DIM:POOL_DIM + Q_DIM + KV_DIM], cp_ref[...], sp_ref[...])
        vc = cur[:, POOL_DIM + Q_DIM + KV_DIM:]
        vp = prv[:, POOL_DIM + Q_DIM + KV_DIM:]
        ri = _iota((GQ * BLOCK, BLOCK), 0) % BLOCK
        cj = _iota((GQ * BLOCK, BLOCK), 1)
        mc = cj <= ri
        mp = jnp.logical_and(cj > ri, has_prev)
        outs, lses = [], []
        for g in range(N_KV_HEADS):
            hs = slice(g * HEAD_DIM, (g + 1) * HEAD_DIM)
            qg = _stack_heads(q, g) * scale
            sc = jnp.where(mc, _nt(qg, kc[:, hs]), NEG)
            sp = jnp.where(mp, _nt(qg, kp[:, hs]), NEG)
            sink = jnp.concatenate([jnp.full((BLOCK, 1), sk_ref[GQ * g + r], F32) for r in range(GQ)], axis=0)
            m = jnp.maximum(jnp.maximum(jnp.max(sc, axis=1, keepdims=True), jnp.max(sp, axis=1, keepdims=True)), sink)
            pc = jnp.exp(sc - m)
            pp = jnp.exp(sp - m)
            den = jnp.sum(pc, axis=1, keepdims=True) + jnp.sum(pp, axis=1, keepdims=True) + jnp.exp(sink - m)
            o = (_nn(pc, vc[:, hs]) + _nn(pp, vp[:, hs])) / den
            lse = m + jnp.log(den)
            for r in range(GQ):
                outs.append(o[r * BLOCK:(r + 1) * BLOCK])
                lses.append(lse[r * BLOCK:(r + 1) * BLOCK])
        attn = jnp.concatenate(outs, axis=1)
        at_ref[...] = attn
        cat_ref[:, POOL_DIM:] = attn.astype(cat_ref.dtype)
        lane = _iota((BLOCK, LANES), 1)
        lrow = jnp.zeros((BLOCK, LANES), F32)
        for h in range(N_HEADS):
            lrow = jnp.where(lane == h, lses[h], lrow)
        lse_ref[...] = lrow

    cur = lambda w: pl.BlockSpec((BLOCK, w), lambda i: (i, 0))
    prv = lambda w: pl.BlockSpec((BLOCK, w), lambda i: (jnp.maximum(i - 1, 0), 0))
    return pl.pallas_call(
        body, grid=(nb,),
        in_specs=[cur(MIX_IN_DIM), prv(MIX_IN_DIM), cur(LANES), cur(LANES), prv(LANES), prv(LANES),
                  pl.BlockSpec((4, POOL_GROUP, POOL_GROUP), lambda i: (0, 0, 0)), pl.BlockSpec((1, POOL_DIM), lambda i: (0, 0)),
                  pl.BlockSpec(memory_space=pltpu.SMEM)],
        out_specs=[cur(2 * POOL_DIM), cur(Q_DIM), cur(LANES)],
        out_shape=[_sds((T, 2 * POOL_DIM), MXU), _sds((T, Q_DIM)), _sds((T, LANES))],
        compiler_params=_cp("parallel"), name=name)(proj, proj, cos, sin_s, cos, sin_s, pool_w, pool_scale, sinks)


def _mixcore_bwd(proj, cos, sin_s, pool_w, pool_scale, sinks, attn, lse, dcat, name):
    T = proj.shape[0]
    nb = T // BLOCK
    scale = HEAD_DIM ** -0.5
    QO, KO, VO = POOL_DIM, POOL_DIM + Q_DIM, POOL_DIM + Q_DIM + KV_DIM

    def body(p_ref, pp_ref, pn_ref, c_ref, s_ref, cp_ref, sp_ref, cn_ref, sn_ref, pw_ref, ps_ref, sk_ref,
             at_ref, atn_ref, l_ref, ln_ref, d_ref, dn_ref, dp_ref, dpw_ref, dps_ref, dsk_ref):
        i = pl.program_id(0)
        has_prev = i > 0
        has_next = i < nb - 1
        cur = p_ref[...]
        prv = jnp.where(has_prev, pp_ref[...], 0.0)
        d_cur = d_ref[...]
        d_nxt = jnp.where(has_next, dn_ref[...], 0.0)

        tpos = (i * BLOCK + _iota((BLOCK, 1), 0) + 1).astype(F32)
        tpos2 = (i * BLOCK + _iota((2 * BLOCK, 1), 0) + 1).astype(F32)
        ps = ps_ref[...]
        dps_parts, dpw_parts = [], []
        for g, w in enumerate(POOL_WINDOWS):
            sl = slice(g * POOL_GROUP, (g + 1) * POOL_GROUP)
            pooled = _pool_sums(prv[:, sl], cur[:, sl], w) / jnp.minimum(tpos, float(w)) - cur[:, sl]
            mixed = _nn(pooled, pw_ref[g])
            dps_parts.append(jnp.sum(d_cur[:, sl] * mixed, axis=0, keepdims=True))
            dm2 = jnp.concatenate([d_cur[:, sl], d_nxt[:, sl]], axis=0) * ps[:, sl]
            dpw_parts.append(_tn(pooled, dm2[:BLOCK]))
            dpool2 = _nt(dm2, pw_ref[g])
            e = dpool2 / jnp.minimum(tpos2, float(w))
            sh = 1
            while sh < w:
                e = e + pltpu.roll(e, 2 * BLOCK - sh, 0)
                sh *= 2
            dp_ref[:, sl] = (e[:BLOCK] - dpool2[:BLOCK]).astype(dp_ref.dtype)
        dpsp = jnp.concatenate(dps_parts, axis=1)

        nxt = pn_ref[...]
        q = _rope(cur[:, QO:KO], c_ref[...], s_ref[...])
        qn = _rope(nxt[:, QO:KO], cn_ref[...], sn_ref[...])
        kc = _rope(cur[:, KO:VO], c_ref[...], s_ref[...])
        kp = _rope(prv[:, KO:VO], cp_ref[...], sp_ref[...])
        vc, vp = cur[:, VO:], prv[:, VO:]
        do, don = d_cur[:, POOL_DIM:], d_nxt[:, POOL_DIM:]
        dl = do * at_ref[...]
        dln = don * atn_ref[...]
        lse, lsen = l_ref[...], ln_ref[...]
        ri = _iota((GQ * BLOCK, BLOCK), 0) % BLOCK
        cj = _iota((GQ * BLOCK, BLOCK), 1)
        mc = cj <= ri
        mp = jnp.logical_and(cj > ri, has_prev)
        mn = jnp.logical_and(cj > ri, has_next)
        dq_parts, dk_parts, dv_parts, dsk_vals = [], [], [], []
        for g in range(N_KV_HEADS):
            hs = slice(g * HEAD_DIM, (g + 1) * HEAD_DIM)
            qg, qng = _stack_heads(q, g) * scale, _stack_heads(qn, g) * scale
            dog, dong = _stack_heads(do, g), _stack_heads(don, g)
            delta = jnp.sum(_stack_heads(dl, g), axis=1, keepdims=True)
            deltan = jnp.sum(_stack_heads(dln, g), axis=1, keepdims=True)
            lg, lng = _stack_cols(lse, g), _stack_cols(lsen, g)
            pc = jnp.where(mc, jnp.exp(_nt(qg, kc[:, hs]) - lg), 0.0)
            pp = jnp.where(mp, jnp.exp(_nt(qg, kp[:, hs]) - lg), 0.0)
            pn = jnp.where(mn, jnp.exp(_nt(qng, kc[:, hs]) - lng), 0.0)
            dsc = pc * (_nt(dog, vc[:, hs]) - delta)
            dsp = pp * (_nt(dog, vp[:, hs]) - delta)
            dsn = pn * (_nt(dong, vc[:, hs]) - deltan)
            dqg = (_nn(dsc, kc[:, hs]) + _nn(dsp, kp[:, hs])) * scale
            dq_parts += [dqg[r * BLOCK:(r + 1) * BLOCK] for r in range(GQ)]
            dk_parts.append(_tn(dsc, qg) + _tn(dsn, qng))
            dv_parts.append(_tn(pc, dog) + _tn(pn, dong))
            sink = jnp.concatenate([jnp.full((BLOCK, 1), sk_ref[GQ * g + r], F32) for r in range(GQ)], axis=0)
            dsk = -jnp.exp(sink - lg) * delta
            dsk_vals += [jnp.sum(dsk[r * BLOCK:(r + 1) * BLOCK], axis=0, keepdims=True) for r in range(GQ)]
        dq = _rope(jnp.concatenate(dq_parts, axis=1), c_ref[...], s_ref[...], inverse=True)
        dk = _rope(jnp.concatenate(dk_parts, axis=1), c_ref[...], s_ref[...], inverse=True)
        dp_ref[:, QO:KO] = dq.astype(dp_ref.dtype)
        dp_ref[:, KO:VO] = dk.astype(dp_ref.dtype)
        dp_ref[:, VO:] = jnp.concatenate(dv_parts, axis=1).astype(dp_ref.dtype)
        lane = _iota((1, LANES), 1)
        dskp = jnp.zeros((1, LANES), F32)
        for h in range(N_HEADS):
            dskp = jnp.where(lane == h, dsk_vals[h], dskp)

        @pl.when(i == 0)
        def _():
            dps_ref[...] = dpsp
            dsk_ref[...] = dskp
            for g in range(4):
                dpw_ref[g] = dpw_parts[g]

        @pl.when(i > 0)
        def _():
            dps_ref[...] += dpsp
            dsk_ref[...] += dskp
            for g in range(4):
                dpw_ref[g] += dpw_parts[g]

    cur = lambda w: pl.BlockSpec((BLOCK, w), lambda i: (i, 0))
    prv = lambda w: pl.BlockSpec((BLOCK, w), lambda i: (jnp.maximum(i - 1, 0), 0))
    nxt = lambda w: pl.BlockSpec((BLOCK, w), lambda i: (jnp.minimum(i + 1, nb - 1), 0))
    return pl.pallas_call(
        body, grid=(nb,),
        in_specs=[cur(MIX_IN_DIM), prv(MIX_IN_DIM), nxt(MIX_IN_DIM),
                  cur(LANES), cur(LANES), prv(LANES), prv(LANES), nxt(LANES), nxt(LANES),
                  pl.BlockSpec((4, POOL_GROUP, POOL_GROUP), lambda i: (0, 0, 0)), pl.BlockSpec((1, POOL_DIM), lambda i: (0, 0)),
                  pl.BlockSpec(memory_space=pltpu.SMEM),
                  cur(Q_DIM), nxt(Q_DIM), cur(LANES), nxt(LANES), cur(2 * POOL_DIM), nxt(2 * POOL_DIM)],
        out_specs=[cur(MIX_IN_DIM), pl.BlockSpec((4, POOL_GROUP, POOL_GROUP), lambda i: (0, 0, 0)),
                   pl.BlockSpec((1, POOL_DIM), lambda i: (0, 0)), pl.BlockSpec((1, LANES), lambda i: (0, 0))],
        out_shape=[_sds((T, MIX_IN_DIM), MXU), _sds((4, POOL_GROUP, POOL_GROUP)), _sds((1, POOL_DIM)), _sds((1, LANES))],
        compiler_params=_cp("arbitrary"), name=name)(
            proj, proj, proj, cos, sin_s, cos, sin_s, cos, sin_s, pool_w, pool_scale, sinks, attn, attn, lse, lse, dcat, dcat)


SSM_TC = 512
GROUP_W = SSM_D_INNER // SSM_GROUPS


def _ssm_in_conv(h, wT, cw, cb, name):
    T, D = h.shape
    tm = min(T, 256)
    tc = 1024
    K = SSM_CONV

    def body(a_ref, b_ref, w_ref, c_ref, x_ref, pre_ref, act_ref, halo):
        @pl.when(pl.program_id(1) == 0)
        def _():
            halo[...] = jnp.zeros(halo.shape, F32)

        cur = _nt(a_ref[...], b_ref[...])
        x_ref[...] = cur
        pre = _conv_rows(cur, halo[...], w_ref[...], c_ref[...], K)
        halo[...] = cur[tm - SUBLANES:]
        pre_ref[...] = pre
        act_ref[...] = _silu(pre)

    blk = pl.BlockSpec((tm, tc), lambda j, i: (i, j))
    return pl.pallas_call(
        body, grid=(SSM_CONV_DIM // tc, T // tm),
        in_specs=[pl.BlockSpec((tm, D), lambda j, i: (i, 0)), pl.BlockSpec((tc, D), lambda j, i: (j, 0)),
                  pl.BlockSpec((K, tc), lambda j, i: (0, j)), pl.BlockSpec((1, tc), lambda j, i: (0, j))],
        out_specs=[blk, blk, blk], out_shape=[_sds((T, SSM_CONV_DIM))] * 3,
        scratch_shapes=[pltpu.VMEM((SUBLANES, tc), F32)],
        compiler_params=_cp("arbitrary", "arbitrary"), name=name)(h, wT, cw, cb)


def _ssm_pre_bwd(xbc, pre, cw, dact, name):
    T = xbc.shape[0]
    tm = min(T, 512)
    nt = T // tm
    K = SSM_CONV
    q = tm // SUBLANES
    tc = SSM_TC

    def body(x_ref, p_ref, pn_ref, d_ref, dn_ref, w_ref, dx_ref, dw_ref, db_ref):
        i = pl.program_id(1)
        w = w_ref[...]
        cur = x_ref[...]
        d_cur = d_ref[...] * _dsilu(p_ref[...])
        d_nxt = jnp.where(i == nt - 1, 0.0, dn_ref[...] * _dsilu(pn_ref[...]))
        ups = [d_cur] + [_shift_up(d_cur, d_nxt, s) for s in range(1, K)]
        dx = ups[0] * w[K - 1:K, :]
        for s in range(1, K):
            dx = dx + ups[s] * w[K - 1 - s:K - s, :]
        dx_ref[...] = dx.astype(dx_ref.dtype)
        dwp = jnp.concatenate([jnp.sum(ups[K - 1 - k] * cur, axis=0, keepdims=True) for k in range(K)], axis=0)
        dbp = jnp.sum(d_cur, axis=0, keepdims=True)

        @pl.when(i == 0)
        def _():
            dw_ref[...] = dwp
            db_ref[...] = dbp

        @pl.when(i > 0)
        def _():
            dw_ref[...] += dwp
            db_ref[...] += dbp

    nxt_row = lambda i: jnp.minimum((i + 1) * q, nt * q - 1)
    return pl.pallas_call(
        body, grid=(SSM_CONV_DIM // tc, nt),
        in_specs=[pl.BlockSpec((tm, tc), lambda j, i: (i, j)),
                  pl.BlockSpec((tm, tc), lambda j, i: (i, j)),
                  pl.BlockSpec((SUBLANES, tc), lambda j, i: (nxt_row(i), j)),
                  pl.BlockSpec((tm, tc), lambda j, i: (i, j)),
                  pl.BlockSpec((SUBLANES, tc), lambda j, i: (nxt_row(i), j)),
                  pl.BlockSpec((K, tc), lambda j, i: (0, j))],
        out_specs=[pl.BlockSpec((tm, tc), lambda j, i: (i, j)), pl.BlockSpec((K, tc), lambda j, i: (0, j)),
                   pl.BlockSpec((1, tc), lambda j, i: (0, j))],
        out_shape=[_sds((T, SSM_CONV_DIM), MXU), _sds((K, SSM_CONV_DIM)), _sds((1, SSM_CONV_DIM))],
        compiler_params=_cp("parallel", "arbitrary"), name=name)(xbc, pre, pre, dact, dact, cw)


def _dot_hi(a, b):
    return jnp.dot(a, b, precision=HI, preferred_element_type=F32)


def _ssd_common(dtraw, bias, alog):
    L = SSM_CHUNK
    xb = dtraw + bias
    dt = jnp.maximum(xb, 0.0) + jnp.log1p(jnp.exp(-jnp.abs(xb)))
    A = -jnp.exp(alog)
    tril = (_iota((L, L), 1) <= _iota((L, L), 0)).astype(F32)
    acs = _dot_hi(tril, dt * A)
    return xb, dt, A, tril, acs


def _head_selectors():
    es = (_iota((LANES, SSM_D_INNER), 0) == _iota((LANES, SSM_D_INNER), 1) // HEAD_DIM).astype(BF16)
    est = (_iota((SSM_D_INNER, LANES), 1) == _iota((SSM_D_INNER, LANES), 0) // HEAD_DIM).astype(BF16)
    return es, est


def _dot_sel(v, sel):
    hi = v.astype(BF16)
    r1 = v - hi.astype(F32)
    mid = r1.astype(BF16)
    lo = (r1 - mid.astype(F32)).astype(BF16)
    d = lambda a: jnp.dot(a, sel, preferred_element_type=F32)
    return (d(hi) + d(mid)) + d(lo)


def _expand_heads(v, es):
    return _dot_sel(v, es)


def _reduce_heads(q, est):
    return _dot_sel(q, est)


def _per_state_row(v, g):
    return jnp.concatenate([jnp.broadcast_to(v[:, GQ * g + r:GQ * g + r + 1], (HEAD_DIM, 1)) for r in range(GQ)], axis=0)


def _ssd_fwd(xact, dtraw, dt_bias, a_log, name):
    T = xact.shape[0]
    nc = T // SSM_CHUNK
    L = SSM_CHUNK
    BO, CO = SSM_D_INNER, SSM_D_INNER + SSM_GROUPS * SSM_STATE

    def body(x_ref, dt_ref, bias_ref, al_ref, es_ref, y_ref, st_ref, state):
        @pl.when(pl.program_id(0) == 0)
        def _():
            state[...] = jnp.zeros(state.shape, F32)

        _, dt, A, tril, acs = _ssd_common(dt_ref[...], bias_ref[...], al_ref[...])
        acsT = acs.T
        last = acs[L - 1:L, :]
        cd = jnp.exp(last)
        es = es_ref[...]
        dtX = _expand_heads(dt, es)
        EX = _expand_heads(jnp.exp(acs), es)
        decX = _expand_heads(jnp.exp(last - acs), es)
        for g in range(SSM_GROUPS):
            gs = slice(g * GROUP_W, (g + 1) * GROUP_W)
            B = x_ref[:, BO + g * SSM_STATE:BO + (g + 1) * SSM_STATE]
            C = x_ref[:, CO + g * SSM_STATE:CO + (g + 1) * SSM_STATE]
            X = x_ref[:, gs] * dtX[:, gs]
            CB = _nt(C, B)
            yd = []
            for r in range(GQ):
                h = GQ * g + r
                Lm = jnp.exp(jnp.where(tril > 0, acs[:, h:h + 1] - acsT[h:h + 1, :], NEG))
                yd.append(_nn(CB * Lm, X[:, r * HEAD_DIM:(r + 1) * HEAD_DIM]))
            S = state[g]
            st_ref[g] = S
            y_ref[:, gs] = jnp.concatenate(yd, axis=1) + _nt(C, S) * EX[:, gs]
            state[g] = S * _per_state_row(cd, g) + _tn(X * decX[:, gs], B)

    es, _ = _head_selectors()
    return pl.pallas_call(
        body, grid=(nc,),
        in_specs=[pl.BlockSpec((L, SSM_CONV_DIM), lambda c: (c, 0)), pl.BlockSpec((L, LANES), lambda c: (c, 0)),
                  pl.BlockSpec((1, LANES), lambda c: (0, 0)), pl.BlockSpec((1, LANES), lambda c: (0, 0)),
                  pl.BlockSpec((LANES, SSM_D_INNER), lambda c: (0, 0))],
        out_specs=[pl.BlockSpec((L, SSM_D_INNER), lambda c: (c, 0)),
                   pl.BlockSpec((None, SSM_GROUPS, GROUP_W, SSM_STATE), lambda c: (c, 0, 0, 0))],
        out_shape=[_sds((T, SSM_D_INNER)), _sds((nc, SSM_GROUPS, GROUP_W, SSM_STATE))],
        scratch_shapes=[pltpu.VMEM((SSM_GROUPS, GROUP_W, SSM_STATE), F32)],
        compiler_params=_cp("arbitrary"), name=name)(xact, dtraw, dt_bias, a_log, es)


def _ssd_bwd(xact, dtraw, dt_bias, a_log, d_skip, states, dy, name):
    T = xact.shape[0]
    nc = T // SSM_CHUNK
    L = SSM_CHUNK
    BO, CO = SSM_D_INNER, SSM_D_INNER + SSM_GROUPS * SSM_STATE

    def body(x_ref, dt_ref, bias_ref, al_ref, dsk_ref, es_ref, est_ref, st_ref, dy_ref,
             dxp_ref, ddt_ref, dbias_ref, dal_ref, dd_ref, dstate, qa, qx):
        cc = pl.program_id(0)

        @pl.when(cc == 0)
        def _():
            dstate[...] = jnp.zeros(dstate.shape, F32)

        xb, dt, A, tril, acs = _ssd_common(dt_ref[...], bias_ref[...], al_ref[...])
        acsT = acs.T
        last = acs[L - 1:L, :]
        cd = jnp.exp(last)
        es, est = es_ref[...], est_ref[...]
        dtX = _expand_heads(dt, es)
        EX = _expand_heads(jnp.exp(acs), es)
        decX = _expand_heads(jnp.exp(last - acs), es)
        lane1 = _iota((1, LANES), 1)
        lane = _iota((L, LANES), 1)
        sub = _iota((L, LANES), 0)
        ztot = jnp.zeros((1, LANES), F32)
        wrow = jnp.zeros((L, LANES), F32)
        wcolT = jnp.zeros((LANES, L), F32)
        rows_dec, rows_dd = [], []
        for g in range(SSM_GROUPS):
            gs = slice(g * GROUP_W, (g + 1) * GROUP_W)
            x = x_ref[:, gs]
            B = x_ref[:, BO + g * SSM_STATE:BO + (g + 1) * SSM_STATE]
            C = x_ref[:, CO + g * SSM_STATE:CO + (g + 1) * SSM_STATE]
            dY = dy_ref[:, gs]
            dtx, e_x, dec_x = dtX[:, gs], EX[:, gs], decX[:, gs]
            X = x * dtx
            CB = _nt(C, B)
            S = st_ref[g]
            dS_out = dstate[g]
            dcb_sum = jnp.zeros((L, L), F32)
            dxd = []
            for r in range(GQ):
                h = GQ * g + r
                hs = slice(r * HEAD_DIM, (r + 1) * HEAD_DIM)
                Lm = jnp.exp(jnp.where(tril > 0, acs[:, h:h + 1] - acsT[h:h + 1, :], NEG))
                M = CB * Lm
                dM = _nt(dY[:, hs], X[:, hs])
                dxd.append(_tn(M, dY[:, hs]))
                dcb_sum = dcb_sum + dM * Lm
                Wm = dM * M
                wrow = jnp.where(lane == h, jnp.sum(Wm, axis=1, keepdims=True), wrow)
                wcolT = jnp.where(sub == h, jnp.sum(Wm, axis=0, keepdims=True), wcolT)
            dXd = jnp.concatenate(dxd, axis=1)
            G = _nt(C, S)
            dG = dY * e_x
            dDX = _nt(B, dS_out)
            dX = dXd + dec_x * dDX
            t_dec = dDX * X * dec_x
            qa[:, gs] = dG * G - t_dec
            qx[:, gs] = dX * x
            rows_dec.append(jnp.sum(t_dec, axis=0, keepdims=True))
            rows_dd.append(jnp.sum(dY * x, axis=0, keepdims=True))
            zc = jnp.sum(dS_out * S, axis=1, keepdims=True)
            for r in range(GQ):
                ztot = jnp.where(lane1 == GQ * g + r, jnp.sum(zc[r * HEAD_DIM:(r + 1) * HEAD_DIM], axis=0, keepdims=True), ztot)
            dxp_ref[:, gs] = dX * dtx + dY * dsk_ref[:, gs]
            dxp_ref[:, BO + g * SSM_STATE:BO + (g + 1) * SSM_STATE] = _tn(dcb_sum, C) + _nn(X * dec_x, dS_out)
            dxp_ref[:, CO + g * SSM_STATE:CO + (g + 1) * SSM_STATE] = _nn(dcb_sum, B) + _nn(dG, S)
            dstate[g] = dS_out * _per_state_row(cd, g) + _tn(dG, C)
        rows = jnp.concatenate([jnp.concatenate(rows_dec, axis=1), jnp.concatenate(rows_dd, axis=1)]
                               + [jnp.zeros((SUBLANES - 2, SSM_D_INNER), F32)], axis=0)
        rsum = _reduce_heads(rows, est)
        dlast = rsum[0:1, :] + cd * ztot
        dacs = (wrow - wcolT.T) + _reduce_heads(qa[...], est) + jnp.where(sub == L - 1, dlast, 0.0)
        triu = (_iota((L, L), 0) <= _iota((L, L), 1)).astype(F32)
        da = _dot_hi(triu, dacs)
        ddtraw = (da * A + _reduce_heads(qx[...], est)) * (1.0 / (1.0 + jnp.exp(-xb)))
        ddt_ref[...] = ddtraw
        dal = jnp.sum(da * dt, axis=0, keepdims=True) * A
        ddp = rsum[1:2, :]
        dbp = jnp.sum(ddtraw, axis=0, keepdims=True)

        @pl.when(cc == 0)
        def _():
            dbias_ref[...] = dbp
            dal_ref[...] = dal
            dd_ref[...] = ddp

        @pl.when(cc > 0)
        def _():
            dbias_ref[...] += dbp
            dal_ref[...] += dal
            dd_ref[...] += ddp

    rc = lambda c: nc - 1 - c
    vec = pl.BlockSpec((1, LANES), lambda c: (0, 0))
    es, est = _head_selectors()
    return pl.pallas_call(
        body, grid=(nc,),
        in_specs=[pl.BlockSpec((L, SSM_CONV_DIM), lambda c: (rc(c), 0)), pl.BlockSpec((L, LANES), lambda c: (rc(c), 0)), vec, vec,
                  pl.BlockSpec((1, SSM_D_INNER), lambda c: (0, 0)),
                  pl.BlockSpec((LANES, SSM_D_INNER), lambda c: (0, 0)), pl.BlockSpec((SSM_D_INNER, LANES), lambda c: (0, 0)),
                  pl.BlockSpec((None, SSM_GROUPS, GROUP_W, SSM_STATE), lambda c: (rc(c), 0, 0, 0)),
                  pl.BlockSpec((L, SSM_D_INNER), lambda c: (rc(c), 0))],
        out_specs=[pl.BlockSpec((L, SSM_CONV_DIM), lambda c: (rc(c), 0)),
                   pl.BlockSpec((L, LANES), lambda c: (rc(c), 0)), vec, vec, vec],
        out_shape=[_sds((T, SSM_CONV_DIM)), _sds((T, LANES)), _sds((1, LANES)), _sds((1, LANES)), _sds((1, LANES))],
        scratch_shapes=[pltpu.VMEM((SSM_GROUPS, GROUP_W, SSM_STATE), F32), pltpu.VMEM((L, SSM_D_INNER), F32),
                        pltpu.VMEM((L, SSM_D_INNER), F32)],
        compiler_params=_cp("arbitrary"), name=name)(xact, dtraw, dt_bias, a_log, d_skip, es, est, states, dy)


def _ssm_post_fwd(y, xact, z, d_skip, nw, name):
    T = y.shape[0]
    tm = min(T, 256)
    W = SSM_D_INNER

    def body(y_ref, x_ref, z_ref, d_ref, w_ref, o_ref):
        y2 = (y_ref[...] + d_ref[...] * x_ref[...]) * _silu(z_ref[...])
        r = lax.rsqrt(jnp.mean(y2 * y2, axis=-1, keepdims=True) + SSM_NORM_EPS)
        o_ref[...] = (y2 * r * w_ref[...]).astype(o_ref.dtype)

    row = pl.BlockSpec((tm, W), lambda i: (i, 0))
    vec = pl.BlockSpec((1, W), lambda i: (0, 0))
    return pl.pallas_call(
        body, grid=(T // tm,), in_specs=[row, row, row, vec, vec], out_specs=row, out_shape=_sds((T, W), MXU),
        compiler_params=_cp("parallel"), name=name)(y, xact, z, d_skip, nw)


def _ssm_post_bwd(y, xact, z, d_skip, nw, dyn, name):
    T = y.shape[0]
    tm = min(T, 256)
    W = SSM_D_INNER

    def body(y_ref, x_ref, z_ref, d_ref, w_ref, dn_ref, dyg_ref, dz_ref, dw_ref):
        zv = z_ref[...]
        sz = _silu(zv)
        yg = y_ref[...] + d_ref[...] * x_ref[...]
        y2 = yg * sz
        r = lax.rsqrt(jnp.mean(y2 * y2, axis=-1, keepdims=True) + SSM_NORM_EPS)
        y2h = y2 * r
        dn = dn_ref[...]
        gy = dn * w_ref[...]
        dy2 = r * (gy - y2h * jnp.mean(gy * y2h, axis=-1, keepdims=True))
        dyg_ref[...] = dy2 * sz
        dz_ref[...] = (dy2 * yg * _dsilu(zv)).astype(dz_ref.dtype)
        part = jnp.sum(dn * y2h, axis=0, keepdims=True)

        @pl.when(pl.program_id(0) == 0)
        def _():
            dw_ref[...] = part

        @pl.when(pl.program_id(0) > 0)
        def _():
            dw_ref[...] += part

    row = pl.BlockSpec((tm, W), lambda i: (i, 0))
    vec = pl.BlockSpec((1, W), lambda i: (0, 0))
    return pl.pallas_call(
        body, grid=(T // tm,), in_specs=[row, row, row, vec, vec, row], out_specs=[row, row, vec],
        out_shape=[_sds((T, W)), _sds((T, W), MXU), _sds((1, W))],
        compiler_params=_cp("arbitrary"), name=name)(y, xact, z, d_skip, nw, dyn)


def _local_step(x0, cos, sin_s, target, P, fetch, token, send):
    mmf = functools.partial(_mm, tm=1024)
    big, small = {}, {}
    P = dict(P, wup={}, wdn={}, fcw={})
    h0 = _rmsnorm_fwd(x0, P["nm"][0], "norm_mix0", token=token)
    proj0 = mmf(h0, P["wmiT"], tb=True, tn=1280, tk=1024, name="mix_in")
    cat, attn, lse = _mixcore_fwd(proj0, cos, sin_s, P["pool_w"], P["pool_scale"], P["sinks"], "mixcore_fwd")
    x1, hf0 = mmf(cat, P["wmo"], tn=1024, tk=1024, res=x0, norm_w=P["nf"][0], name="mix_out")

    def ffn_fwd(xin, hf, i, next_norm):
        got = fetch(f"ffn{i}", hf)
        P["wup"][i], P["wdn"][i], P["fcw"][i] = got["wup"], got["wdn"], got["fcw"]
        hid, hc, act = _ffn_up_conv_gate(hf, P["wup"][i], P["fcw"][i], P["fcb"][i], f"ffn_up{i}")
        xout = mmf(act, P["wdn"][i], tn=1024, tk=D_FF, res=xin, norm_w=next_norm, name=f"ffn_down{i}")
        return (hid, hc), act, xout

    hid0, act0, (x2, h1) = ffn_fwd(x1, hf0, 0, P["nm"][1])
    P.update(fetch("ssm", h1))
    z = mmf(h1, P["wzT"], tb=True, tn=1024, tk=1024, name="ssm_in_z")
    xbc, xpre, xact = _ssm_in_conv(h1, P["wxbcT"], P["scw"], P["scb"], "ssm_in_xbc")
    dtraw = mmf(h1, P["wdtT"], tb=True, tn=128, tk=1024, name="ssm_in_dt")
    y, states = _ssd_fwd(xact, dtraw, P["dt_bias"], P["a_log"], "ssd_fwd")
    yn = _ssm_post_fwd(y, xact, z, P["d_exp"], P["snorm"], "ssm_post_fwd")
    x3, hf1 = mmf(yn, P["wso"], tn=1024, tk=SSM_D_INNER, res=x2, norm_w=P["nf"][1], name="ssm_out")
    hid1, act1, x4 = ffn_fwd(x3, hf1, 1, None)
    loss_row, dx4, d_nfin = _loss_head(x4, P["nfin"], target, "loss_head")
    small["norm_final"] = d_nfin

    def ffn_bwd(xin, dxo, hf, hid, act, i):
        da = mmf(dxo, P["wdn"][i], tb=True, tn=1408, tk=1024, name=f"ffn_down_dx{i}")
        big[f"ffn_w_down{i}"] = dwf(act, dxo, tm=1408, tn=1024, name=f"ffn_down_dw{i}").reshape(N_CHIPS, D_FF // N_CHIPS, D_MODEL)
        dhid, dcw, dcb = _ffn_mid_bwd(hid[0], hid[1], P["fcw"][i], da, f"ffn_mid_bwd{i}")
        big[f"ffn_w_up{i}"] = dwf(hf, dhid, tm=1024, tn=1408, out_shard_perm=(0, 2, 1, 3), name=f"ffn_up_dw{i}")
        tok = send(f"ffn{i}", [big[f"ffn_w_up{i}"], big[f"ffn_w_down{i}"]])
        dxi, dnf = _mm(dhid, P["wup"][i], tb=True, tm=512, tn=1024, tk=2816, norm_bwd=(xin, P["nf"][i], dxo, tok), name=f"ffn_up_dx{i}")
        return dxi, dnf, dcw, dcb

    dwf = functools.partial(_mm, ta=True, tk=2048, out_dtype=BF16)
    dx3, dnf1, dfcw1, dfcb1 = ffn_bwd(x3, dx4, hf1, hid1, act1, 1)
    dyn = mmf(dx3, P["wso"], tb=True, tn=1024, tk=1024, name="ssm_out_dx")
    big["ssm_w_out"] = dwf(yn, dx3, tm=1024, tn=1024, name="ssm_out_dw").reshape(N_CHIPS, SSM_D_INNER // N_CHIPS, D_MODEL)
    dyg, dz, d_snorm = _ssm_post_bwd(y, xact, z, P["d_exp"], P["snorm"], dyn, "ssm_post_bwd")
    dxact_p, ddtraw, d_dtb, d_alog, d_dskip = _ssd_bwd(xact, dtraw, P["dt_bias"], P["a_log"], P["d_exp"], states, dyg, "ssd_bwd")
    dxbc, d_scw, d_scb = _ssm_pre_bwd(xbc, xpre, P["scw"], dxact_p, "ssm_pre_bwd")
    dwsi = dwf(dz, h1, tm=1024, tn=1024, out_into=(None, SSM_IN_DIM, 0), name="ssm_in_dw_z")
    dwsi = dwf(dxbc, h1, tm=1024, tn=1024, out_into=(dwsi, SSM_IN_DIM, SSM_D_INNER // 1024), name="ssm_in_dw_xbc")
    dwdt = dwf(ddtraw, h1, tm=128, tn=1024, name="ssm_in_dw_dt")
    dwsi = _put_rows(dwsi, dwdt, SSM_HEADS, SSM_D_INNER + SSM_CONV_DIM, "ssm_in_dw_put_dt")
    big["ssm_w_in"] = dwsi.reshape(N_CHIPS, SSM_IN_DIM // N_CHIPS, D_MODEL)
    tok = send("ssm", [big["ssm_w_in"], big["ssm_w_out"]])
    dh1 = mmf(dz, P["wzT"], tn=1024, tk=2048, name="ssm_in_dx_z")
    dh1 = mmf(dxbc, P["wxbcT"], tn=1024, tk=2048, res=dh1, name="ssm_in_dx_xbc")
    dx2, dnm1 = mmf(ddtraw, P["wdtT"], tn=1024, tk=128, res=dh1, norm_bwd=(x2, P["nm"][1], dx3, tok), name="ssm_in_dx_dt")
    dx1, dnf0, dfcw0, dfcb0 = ffn_bwd(x1, dx2, hf0, hid0, act0, 0)
    dcat = mmf(dx1, P["wmo"], tb=True, tn=1024, tk=1024, name="mix_out_dx")
    big["mix_w_out"] = dwf(cat, dx1, tm=1024, tn=1024, name="mix_out_dw").reshape(N_CHIPS, D_MODEL // N_CHIPS, D_MODEL)
    dproj0, d_pw, d_ps, d_sk = _mixcore_bwd(proj0, cos, sin_s, P["pool_w"], P["pool_scale"], P["sinks"], attn, lse, dcat, "mixcore_bwd")
    big["mix_w_in"] = dwf(dproj0, h0, tm=1280, tn=1024, name="mix_in_dw").reshape(N_CHIPS, MIX_IN_DIM // N_CHIPS, D_MODEL)
    tok = send("mix", [big["mix_w_in"], big["mix_w_out"]])
    dx0, dnm0 = mmf(dproj0, P["wmiT"], tn=1024, tk=1280, norm_bwd=(x0, P["nm"][0], dx1, tok), name="mix_in_dx")

    def unperm_cols(a):
        r = a.shape[0]
        t = a.reshape(r, N_CHIPS, FFN_TC)
        return jnp.stack([t[:, p] for p in _PERM], axis=0)

    small["norm_mix"] = jnp.concatenate([dnm0, dnm1], axis=0)
    small["norm_ffn"] = jnp.concatenate([dnf0, dnf1], axis=0)
    small["pool_w"] = d_pw.reshape(4 * POOL_GROUP, POOL_GROUP)
    small["pool_scale"] = d_ps
    small["attn_sinks"] = d_sk
    small["ssm_dt_bias"] = d_dtb
    small["ssm_A_log"] = d_alog
    small["ssm_D"] = d_dskip
    fcb = jnp.stack([unperm_cols(dfcb0), unperm_cols(dfcb1)], axis=0)
    small["ffn_conv_b"] = fcb.reshape(2, 2 * D_FF)
    small["ssm_conv_w"] = d_scw.reshape(SSM_CONV, N_CHIPS, SSM_CONV_DIM // N_CHIPS).transpose(1, 0, 2)
    small["ssm_conv_b"] = d_scb.reshape(N_CHIPS, 1, SSM_CONV_DIM // N_CHIPS)
    small["ssm_norm"] = d_snorm.reshape(N_CHIPS, 1, SSM_D_INNER // N_CHIPS)
    small["ffn_conv_w"] = jnp.concatenate([unperm_cols(dfcw0), unperm_cols(dfcw1)], axis=1)
    return loss_row, dx0, big, small


ANY = pl.BlockSpec(memory_space=pl.ANY)


def _place():
    return lax.axis_index("x"), lax.axis_index("y"), lax.axis_index("c")


def _gather_shards(shards, name):
    n = len(shards)
    split = [s.size >= (1 << 16) for s in shards]

    def half(ref, a, h):
        shp = shards[a].shape
        if len(shp) == 3:
            return ref.at[h]
        r2 = shp[0] // 2
        return ref.at[pl.ds(pl.multiple_of(h * r2, 2 * SUBLANES), r2), :]

    def body(*refs):
        ins, outs = refs[:n], refs[n:2 * n]
        send, recv, fsend, frecv = refs[2 * n:]
        x, y, c = _place()
        k = 2 * x + y
        chips = [(1 - x, y), (x, 1 - y), (1 - x, 1 - y)]

        def ici(a, j, src_slot_ref, dst_slot):
            px, py = chips[j]
            src = half(src_slot_ref, a, c) if split[a] else src_slot_ref
            dst = half(outs[a].at[dst_slot], a, c) if split[a] else outs[a].at[dst_slot]
            return pltpu.make_async_remote_copy(src, dst, send.at[a, j], recv.at[a, j], device_id=(px, py, c), device_id_type=MESH)

        def d2d(a, j, h):
            px, py = chips[j]
            part = half(outs[a].at[2 * px + py], a, h)
            return pltpu.make_async_remote_copy(part, part, fsend.at[a, j], frecv.at[a, j], device_id=(x, y, 1 - c), device_id_type=MESH)

        sends = [ici(a, j, ins[a], k) for a in range(n) for j in range(3)]
        for cp in sends:
            cp.start()
        passed = []
        for a in range(n):
            for j, (px, py) in enumerate(chips):
                ici(a, j, ins[a], 2 * px + py).wait_recv()
                if split[a]:
                    passed.append(d2d(a, j, c))
                    passed[-1].start()
        for a in range(n):
            if split[a]:
                for j in range(3):
                    d2d(a, j, 1 - c).wait_recv()
        for cp in sends + passed:
            cp.wait_send()

    return pl.pallas_call(
        body, in_specs=[ANY] * n, out_specs=[ANY] * n,
        out_shape=[_sds((N_CHIPS,) + s.shape, s.dtype) for s in shards],
        scratch_shapes=[pltpu.SemaphoreType.DMA((n, 3))] * 4,
        compiler_params=pltpu.CompilerParams(has_side_effects=True), name=name)(*shards)


HBM = pl.BlockSpec(memory_space=pltpu.HBM)
SEM = pl.BlockSpec(memory_space=pltpu.SEMAPHORE)
DATAFLOW = pltpu.SideEffectType.DATAFLOW_SIDE_EFFECTING


def _row_half(ref, h):
    r2 = ref.shape[0] // 2
    return ref.at[pl.ds(pl.multiple_of(h * r2, 2 * SUBLANES), r2), :]


def _spread_start(groups, slot_src, after, name, halved=()):
    flat = [a for grp in groups for a in grp]
    n = len(flat)
    ng = len(groups)
    offs = [sum(len(g) for g in groups[:i]) for i in range(ng)]
    lshape = [(a.shape if slot_src else (N_CHIPS,) + a.shape) for a in flat]

    nsem = 6 * n

    def body(*refs):
        src, land = refs[:n], refs[n:2 * n]
        sems = refs[2 * n + 1:2 * n + 1 + nsem]
        token = refs[-1]
        x, y, c = _place()
        k = 2 * x + y
        chips = [(1 - x, y), (x, 1 - y), (1 - x, 1 - y)]
        for a in range(n):
            half = any(offs[gi] <= a < offs[gi] + len(groups[gi]) for gi in halved)
            for j, (px, py) in enumerate(chips):
                s = src[a].at[2 * px + py] if slot_src else src[a]
                d = land[a].at[k]
                if half:
                    s, d = _row_half(s, c), _row_half(d, c)
                pltpu.make_async_remote_copy(s, d, sems[6 * a + 2 * j], sems[6 * a + 2 * j + 1],
                                             device_id=(px, py, c), device_id_type=MESH).start()
        token[...] = jnp.zeros(token.shape, token.dtype)

    out_shape = [pltpu.SemaphoreType.DMA(())] * nsem
    out_shape += [pltpu.HBM(a.shape, a.dtype) for a in flat] + [pltpu.HBM(s, a.dtype) for s, a in zip(lshape, flat)]
    out_shape.append(_sds((SUBLANES, LANES)))
    args = [pltpu.with_memory_space_constraint(a, pltpu.HBM) for a in flat]
    args += [pltpu.with_memory_space_constraint(lax.empty(s, a.dtype), pltpu.HBM) for s, a in zip(lshape, flat)]
    res = pl.pallas_call(
        body, name=name, out_shape=tuple(out_shape), in_specs=[HBM] * (2 * n) + [pl.BlockSpec(memory_space=pl.ANY)],
        out_specs=tuple([SEM] * nsem + [HBM] * (2 * n) + [pl.BlockSpec(memory_space=pltpu.VMEM)]),
        input_output_aliases={i: nsem + i for i in range(2 * n)},
        compiler_params=pltpu.CompilerParams(has_side_effects=DATAFLOW))(*args, after)
    sems, thru, token = res[:nsem], res[nsem:nsem + 2 * n], res[-1]
    out = []
    for gi, grp in enumerate(groups):
        sl = slice(offs[gi], offs[gi] + len(grp))
        out.append((list(sems[6 * offs[gi]:6 * (offs[gi] + len(grp))]), list(thru[:n][sl]), list(thru[n:][sl])))
    return out, token


def _spread_wait(started, slot_src, after, name, halved=False):
    sems, srcs, lands = started
    n = len(srcs)

    def body(*refs):
        src, land = refs[:n], refs[n:2 * n]
        sem = refs[2 * n:2 * n + 6 * n]
        x, y, c = _place()
        chips = [(1 - x, y), (x, 1 - y), (1 - x, 1 - y)]
        for a in range(n):
            for j, (px, py) in enumerate(chips):
                s = src[a].at[2 * px + py] if slot_src else src[a]
                d = land[a].at[2 * px + py]
                if halved:
                    s, d = _row_half(s, c), _row_half(d, c)
                cp = pltpu.make_async_remote_copy(s, d, sem[6 * a + 2 * j], sem[6 * a + 2 * j + 1],
                                                  device_id=(px, py, c), device_id_type=MESH)
                cp.wait_send()
                cp.wait_recv()

    res = pl.pallas_call(
        body, name=name, out_shape=tuple([pltpu.HBM(a.shape, a.dtype) for a in srcs] + [pltpu.HBM(a.shape, a.dtype) for a in lands]),
        in_specs=[HBM] * (2 * n) + [SEM] * (6 * n) + [pl.BlockSpec(memory_space=pl.ANY)], out_specs=tuple([HBM] * (2 * n)),
        input_output_aliases={i: i for i in range(2 * n)},
        compiler_params=pltpu.CompilerParams(has_side_effects=DATAFLOW))(*srcs, *lands, *sems, after)
    return list(res[:n]), list(res[n:])


def _sibling_fill(lands, name):
    n = len(lands)

    def body(*refs):
        bufs = refs[n:2 * n]
        send, recv = refs[2 * n:]
        x, y, c = _place()
        chips = [(1 - x, y), (x, 1 - y), (1 - x, 1 - y)]

        def copy(a, j, h):
            px, py = chips[j]
            part = _row_half(bufs[a].at[2 * px + py], h)
            return pltpu.make_async_remote_copy(part, part, send.at[a, j], recv.at[a, j], device_id=(x, y, 1 - c), device_id_type=MESH)

        sends = [copy(a, j, c) for a in range(n) for j in range(3)]
        for cp in sends:
            cp.start()
        for a in range(n):
            for j in range(3):
                copy(a, j, 1 - c).wait_recv()
        for cp in sends:
            cp.wait_send()

    return pl.pallas_call(
        body, in_specs=[ANY] * n, out_specs=[ANY] * n, out_shape=[_sds(t.shape, t.dtype) for t in lands],
        input_output_aliases={i: i for i in range(n)},
        scratch_shapes=[pltpu.SemaphoreType.DMA((n, 3)), pltpu.SemaphoreType.DMA((n, 3))],
        compiler_params=pltpu.CompilerParams(has_side_effects=True), name=name)(*lands)


def _sibling_exchange(fs, name):
    n = len(fs)

    def body(*refs):
        ins, outs = refs[:n], refs[n:2 * n]
        send, recv = refs[2 * n:]
        x, y, c = _place()
        cps = [pltpu.make_async_remote_copy(ins[a], outs[a], send.at[a], recv.at[a],
                                            device_id=(x, y, 1 - c), device_id_type=MESH) for a in range(n)]
        for cp in cps:
            cp.start()
        for cp in cps:
            cp.wait()

    return pl.pallas_call(
        body, in_specs=[ANY] * n, out_specs=[ANY] * n, out_shape=[_sds(f.shape, f.dtype) for f in fs],
        scratch_shapes=[pltpu.SemaphoreType.DMA((n,)), pltpu.SemaphoreType.DMA((n,))],
        compiler_params=pltpu.CompilerParams(has_side_effects=True), name=name)(*fs)


def _tile2d(rows, cols, budget=1024 * 1024, step=2 * SUBLANES):
    fits = [t for t in range(step, rows + 1, step) if rows % t == 0 and t * cols * 4 <= budget]
    if fits:
        return fits[-1], cols
    fits = [t for t in range(LANES, cols + 1, LANES) if cols % t == 0 and rows * t * 4 <= budget]
    assert fits, (rows, cols)
    return rows, fits[-1]


def _chip_sum(own, parts, kidx, name):
    _, R, C = parts.shape
    tr, tc = _tile2d(R, C)

    def body(k_ref, o_ref_in, p1_ref, p2_ref, p3_ref, o_ref):
        tot = ((o_ref_in[...].astype(F32) + p1_ref[...].astype(F32)) + p2_ref[...].astype(F32)) + p3_ref[...].astype(F32)
        o_ref[...] = tot.astype(o_ref.dtype)

    def slot(d):
        return pl.BlockSpec((None, tr, tc), lambda i, j, k: ((k[0] + d) % N_CHIPS, i, j))

    return pl.pallas_call(
        body,
        grid_spec=pltpu.PrefetchScalarGridSpec(
            num_scalar_prefetch=1, grid=(R // tr, C // tc), in_specs=[slot(0), slot(1), slot(2), slot(3)],
            out_specs=pl.BlockSpec((tr, tc), lambda i, j, k: (i, j))),
        out_shape=_sds((R, C), BF16), compiler_params=_cp("parallel", "parallel"), name=name)(kidx, own, parts, parts, parts)


def _adamw_math(w, g, m, v):
    m2 = ADAM_B1 * m + (1.0 - ADAM_B1) * g
    v2 = ADAM_B2 * v + (1.0 - ADAM_B2) * (g * g)
    m_hat = m2 / (1.0 - ADAM_B1 ** ADAM_STEP)
    v_hat = v2 / (1.0 - ADAM_B2 ** ADAM_STEP)
    delta = -ADAM_LR * (m_hat / (jnp.sqrt(v_hat) + ADAM_EPS) + ADAM_WD * w)
    return delta, m2, v2


def _adamw(w, m, v, gparts, name):
    Lw, R, C = w.shape
    tr, tc = _tile2d(R, C)
    flat = [h for pair in gparts for h in pair]

    def body(*refs):
        w_ref, m_ref, v_ref = refs[:3]
        g_refs = refs[3:3 + 2 * Lw]
        go_ref, d_ref, mo_ref, vo_ref = refs[3 + 2 * Lw:]
        g = g_refs[0][...].astype(F32) + g_refs[1][...].astype(F32)
        for l in range(1, Lw):
            g = jnp.where(pl.program_id(0) == l, g_refs[2 * l][...].astype(F32) + g_refs[2 * l + 1][...].astype(F32), g)
        d, m2, v2 = _adamw_math(w_ref[...], g, m_ref[...], v_ref[...])
        go_ref[...] = g
        d_ref[...] = d
        mo_ref[...] = m2
        vo_ref[...] = v2

    blk = pl.BlockSpec((None, tr, tc), lambda l, i, j: (l, i, j))
    gblk = pl.BlockSpec((tr, tc), lambda l, i, j: (i, j))
    return pl.pallas_call(
        body, grid=(Lw, R // tr, C // tc), in_specs=[blk, blk, blk] + [gblk] * (2 * Lw), out_specs=[blk] * 4,
        out_shape=[_sds((Lw, R, C))] * 4, compiler_params=_cp("parallel", "parallel", "parallel"), name=name)(w, m, v, *flat)


def _small_adamw(grads, wmv, name):
    n = len(grads)

    def body(*refs):
        g_in, p_in, outs = refs[:n], refs[n:4 * n], refs[4 * n:]
        for a in range(n):
            g = g_in[a][...]
            d_, m2, v2 = _adamw_math(p_in[3 * a][...], g, p_in[3 * a + 1][...], p_in[3 * a + 2][...])
            outs[4 * a][...] = g
            outs[4 * a + 1][...] = d_
            outs[4 * a + 2][...] = m2
            outs[4 * a + 3][...] = v2

    vm = pl.BlockSpec(memory_space=pltpu.VMEM)
    args = list(grads) + [t for tri in wmv for t in tri]
    out_shape = [_sds(g.shape) for g in grads for _ in range(4)]
    return pl.pallas_call(body, in_specs=[vm] * len(args), out_specs=[vm] * len(out_shape), out_shape=out_shape,
                          compiler_params=pltpu.CompilerParams(vmem_limit_bytes=V7X_VMEM_LIMIT), name=name)(*args)


def _small_allreduce(partials, pshapes, loss_row, name):
    n = len(partials)
    gshapes = [p.shape for p in partials] + [loss_row.shape]
    ng = n + 1

    def body(*refs):
        g_in = refs[:ng]
        outs = refs[ng:2 * ng]
        sib = refs[2 * ng:3 * ng]
        pair = refs[3 * ng:4 * ng]
        bufs = refs[4 * ng:5 * ng]
        send1, recv1, send2, recv2 = refs[-4:]
        x, y, c = _place()
        k = 2 * x + y
        chips = [(1 - x, y), (x, 1 - y), (1 - x, 1 - y)]
        swaps = [pltpu.make_async_remote_copy(g_in[a], sib[a], send1.at[a], recv1.at[a],
                                              device_id=(x, y, 1 - c), device_id_type=MESH) for a in range(ng)]
        for cp in swaps:
            cp.start()
        for a, cp in enumerate(swaps):
            cp.wait()
            pair[a][...] = g_in[a][...] + sib[a][...]
            bufs[a][k] = pair[a][...]
        sends = [pltpu.make_async_remote_copy(pair[a], bufs[a].at[k], send2.at[a, j], recv2.at[a, j],
                                              device_id=(px, py, c), device_id_type=MESH)
                 for a in range(ng) for j, (px, py) in enumerate(chips)]
        for cp in sends:
            cp.start()
        for a in range(ng):
            for j, (px, py) in enumerate(chips):
                pltpu.make_async_remote_copy(pair[a], bufs[a].at[2 * px + py], send2.at[a, j], recv2.at[a, j],
                                             device_id=(px, py, c), device_id_type=MESH).wait_recv()
        for cp in sends:
            cp.wait_send()
        for a in range(ng):
            sharded = len(gshapes[a]) == 3

            def part(d):
                return bufs[a][d, k] if sharded else bufs[a][d]

            tot = part(0)
            for d in range(1, N_CHIPS):
                tot = tot + part(d)
            if a == n:
                outs[n][...] = tot
            else:
                pr, pc = pshapes[a]
                outs[a][...] = tot[:pr, :pc]

    vm = pl.BlockSpec(memory_space=pltpu.VMEM)
    args = list(partials) + [loss_row]
    out_shape = [_sds(ps) for ps in pshapes] + [_sds(loss_row.shape)]
    return pl.pallas_call(
        body, in_specs=[vm] * len(args), out_specs=[vm] * len(out_shape), out_shape=out_shape,
        scratch_shapes=[pltpu.VMEM(tuple(s), F32) for s in gshapes] * 2 + [pltpu.VMEM((N_CHIPS,) + tuple(s), F32) for s in gshapes]
        + [pltpu.SemaphoreType.DMA((ng,)), pltpu.SemaphoreType.DMA((ng,)),
           pltpu.SemaphoreType.DMA((ng, 3)), pltpu.SemaphoreType.DMA((ng, 3))],
        compiler_params=pltpu.CompilerParams(has_side_effects=True, vmem_limit_bytes=V7X_VMEM_LIMIT), name=name)(*args)


_PERM = (0, 2, 1, 3)


def _cols_from_shards(g):
    return g.transpose(1, 0, 2).reshape(g.shape[1], N_CHIPS * g.shape[2])


def _rope_tables(positions):
    inv_freq = ROPE_THETA ** (-jnp.arange(0, HEAD_DIM, 2, dtype=F32) / HEAD_DIM)
    ang = positions.astype(F32).reshape(-1, 1) * inv_freq
    cos, sin = jnp.cos(ang), jnp.sin(ang)
    cos = jnp.concatenate([cos, cos, cos, cos], axis=-1)
    sin_s = jnp.concatenate([-sin, sin, -sin, sin], axis=-1)
    return cos, sin_s


def kernel(x, positions, norm_mix, norm_ffn, norm_final, mix_w_in, pool_w, pool_scale, attn_sinks, mix_w_out, ssm_w_in, ssm_conv_w, ssm_conv_b, ssm_dt_bias, ssm_A_log, ssm_D, ssm_norm, ssm_w_out, ffn_w_up, ffn_conv_w, ffn_conv_b, ffn_w_down, loss_target, m_norm_mix, m_norm_ffn, m_norm_final, m_mix_w_in, m_pool_w, m_pool_scale, m_attn_sinks, m_mix_w_out, m_ssm_w_in, m_ssm_conv_w, m_ssm_conv_b, m_ssm_dt_bias, m_ssm_A_log, m_ssm_D, m_ssm_norm, m_ssm_w_out, m_ffn_w_up, m_ffn_conv_w, m_ffn_conv_b, m_ffn_w_down, v_norm_mix, v_norm_ffn, v_norm_final, v_mix_w_in, v_pool_w, v_pool_scale, v_attn_sinks, v_mix_w_out, v_ssm_w_in, v_ssm_conv_w, v_ssm_conv_b, v_ssm_dt_bias, v_ssm_A_log, v_ssm_D, v_ssm_norm, v_ssm_w_out, v_ffn_w_up, v_ffn_conv_w, v_ffn_conv_b, v_ffn_w_down):
    W = dict(norm_mix=norm_mix, norm_ffn=norm_ffn, norm_final=norm_final, mix_w_in=mix_w_in, pool_w=pool_w, pool_scale=pool_scale, attn_sinks=attn_sinks, mix_w_out=mix_w_out, ssm_w_in=ssm_w_in, ssm_conv_w=ssm_conv_w, ssm_conv_b=ssm_conv_b, ssm_dt_bias=ssm_dt_bias, ssm_A_log=ssm_A_log, ssm_D=ssm_D, ssm_norm=ssm_norm, ssm_w_out=ssm_w_out, ffn_w_up=ffn_w_up, ffn_conv_w=ffn_conv_w, ffn_conv_b=ffn_conv_b, ffn_w_down=ffn_w_down)
    Mo = dict(norm_mix=m_norm_mix, norm_ffn=m_norm_ffn, norm_final=m_norm_final, mix_w_in=m_mix_w_in, pool_w=m_pool_w, pool_scale=m_pool_scale, attn_sinks=m_attn_sinks, mix_w_out=m_mix_w_out, ssm_w_in=m_ssm_w_in, ssm_conv_w=m_ssm_conv_w, ssm_conv_b=m_ssm_conv_b, ssm_dt_bias=m_ssm_dt_bias, ssm_A_log=m_ssm_A_log, ssm_D=m_ssm_D, ssm_norm=m_ssm_norm, ssm_w_out=m_ssm_w_out, ffn_w_up=m_ffn_w_up, ffn_conv_w=m_ffn_conv_w, ffn_conv_b=m_ffn_conv_b, ffn_w_down=m_ffn_w_down)
    Vo = dict(norm_mix=v_norm_mix, norm_ffn=v_norm_ffn, norm_final=v_norm_final, mix_w_in=v_mix_w_in, pool_w=v_pool_w, pool_scale=v_pool_scale, attn_sinks=v_attn_sinks, mix_w_out=v_mix_w_out, ssm_w_in=v_ssm_w_in, ssm_conv_w=v_ssm_conv_w, ssm_conv_b=v_ssm_conv_b, ssm_dt_bias=v_ssm_dt_bias, ssm_A_log=v_ssm_A_log, ssm_D=v_ssm_D, ssm_norm=v_ssm_norm, ssm_w_out=v_ssm_w_out, ffn_w_up=v_ffn_w_up, ffn_conv_w=v_ffn_conv_w, ffn_conv_b=v_ffn_conv_b, ffn_w_down=v_ffn_w_down)

    kchip = 2 * lax.axis_index("x") + lax.axis_index("y")

    def own_slot(g, own):
        return lax.dynamic_update_slice_in_dim(g, own[None], kchip, axis=0)

    def tr(t):
        return jnp.swapaxes(t[0], 0, 1)

    later = dict(ffn0=[ffn_w_up[0].astype(MXU), ffn_w_down[0].astype(MXU)],
                 ssm=[tr(ssm_w_in).astype(MXU), ssm_w_out[0].astype(MXU)],
                 ffn1=[ffn_w_up[1].astype(MXU), ffn_w_down[1].astype(MXU)])
    sh = [tr(mix_w_in).astype(MXU), mix_w_out[0].astype(MXU), ssm_conv_w[0], ssm_conv_b, ssm_norm, ffn_conv_w]
    first = _gather_shards(sh, "gather_first")
    g_mi, g_mo, g_scw, g_scb, g_sn, g_fcw = [own_slot(g, own) for g, own in zip(first, sh)]
    started, token = _spread_start(list(later.values()), False, first[0], "gather_start", halved=(0,))
    started = dict(zip(later.keys(), started))
    fcw = [jnp.concatenate([g_fcw[p, i] for p in _PERM], axis=1) for i in range(2)]
    P = dict(
        nm=norm_mix, nf=norm_ffn, nfin=norm_final,
        wmiT=g_mi.reshape(MIX_IN_DIM, D_MODEL), wmo=g_mo.reshape(D_MODEL, D_MODEL),
        pool_w=pool_w[0], pool_scale=pool_scale, sinks=attn_sinks[0],
        scw=_cols_from_shards(g_scw), scb=g_scb.reshape(1, SSM_CONV_DIM), snorm=g_sn.reshape(1, SSM_D_INNER),
        dt_bias=jnp.pad(ssm_dt_bias, ((0, 0), (0, LANES - SSM_HEADS))), a_log=jnp.pad(ssm_A_log, ((0, 0), (0, LANES - SSM_HEADS))),
        d_exp=jnp.repeat(ssm_D, SSM_D_INNER // SSM_HEADS, axis=1),
        fcb=[jnp.concatenate([ffn_conv_b[i:i + 1, p * FFN_TC:(p + 1) * FFN_TC] for p in _PERM], axis=1) for i in range(2)],
    )

    def fetch(group, after):
        owns, lands = _spread_wait(started[group], False, after, f"gather_wait_{group}", halved=group == "ffn0")
        if group == "ffn0":
            lands = _sibling_fill(lands, "gather_fill_ffn0")
        a, b = [own_slot(g, own) for g, own in zip(lands, owns)]
        if group == "ssm":
            wsi = a.reshape(SSM_IN_DIM, D_MODEL)
            zx = SSM_D_INNER + SSM_CONV_DIM
            return dict(wzT=wsi[:SSM_D_INNER], wxbcT=wsi[SSM_D_INNER:zx],
                        wdtT=jnp.pad(wsi[zx:], ((0, LANES - SSM_HEADS), (0, 0))), wso=b.reshape(SSM_D_INNER, D_MODEL))
        i = int(group[-1])
        return dict(wup=jnp.concatenate([a[p] for p in _PERM], axis=1), wdn=b.reshape(D_FF, D_MODEL), fcw=fcw[i])

    cos, sin_s = _rope_tables(positions)
    sent = {}

    def send(group, grads):
        res, tok = _spread_start([grads], True, jnp.zeros((SUBLANES, LANES), F32), f"grad_start_{group}")
        sent[group] = res[0]
        return tok

    loss_row, grad_x, big, small = _local_step(x[0], cos, sin_s, loss_target[0], P, fetch, token, send)

    kidx = kchip.astype(jnp.int32).reshape(1)
    group_names = dict(ffn1=["ffn_w_up1", "ffn_w_down1"], ssm=["ssm_w_in", "ssm_w_out"], ffn0=["ffn_w_up0", "ffn_w_down0"],
                       mix=["mix_w_in", "mix_w_out"])
    names, mine = [], []
    for group, started_g in sent.items():
        grads, lands = _spread_wait(started_g, True, grad_x, f"grad_wait_{group}")
        for nm, g, land in zip(group_names[group], grads, lands):
            names.append(nm)
            mine.append(_chip_sum(g, land, kidx, f"chip_sum_{nm}"))
    theirs = _sibling_exchange(mine, "sibling_exchange")
    red = {nm: (a, b) for nm, a, b in zip(names, mine, theirs)}

    out = {}

    def big_update(pname, gparts, transposed=False):
        w = W[pname]
        lw = len(gparts)
        shp = w.shape
        rr, cc = gparts[0][0].shape
        fix = (lambda t: tr(t)[None]) if transposed else (lambda t: t.reshape(lw, rr, cc))
        res = _adamw(fix(w), fix(Mo[pname]), fix(Vo[pname]), gparts, f"adamw_{pname}")
        out[pname] = tuple((tr(r)[None] if transposed else r.reshape(shp)) for r in res)

    big_update("mix_w_in", [red["mix_w_in"]], transposed=True)
    big_update("mix_w_out", [red["mix_w_out"]])
    big_update("ssm_w_in", [red["ssm_w_in"]], transposed=True)
    big_update("ssm_w_out", [red["ssm_w_out"]])
    big_update("ffn_w_up", [red["ffn_w_up0"], red["ffn_w_up1"]])
    big_update("ffn_w_down", [red["ffn_w_down0"], red["ffn_w_down1"]])

    small_names = ["norm_mix", "norm_ffn", "norm_final", "pool_w", "pool_scale", "attn_sinks", "ssm_dt_bias", "ssm_A_log",
                   "ssm_D", "ffn_conv_b", "ssm_conv_w", "ssm_conv_b", "ssm_norm", "ffn_conv_w"]

    def as2d(t):
        if t.ndim == 1:
            return t.reshape(1, -1)
        return t.reshape(-1, t.shape[-1])

    wmv = [(as2d(W[nm]), as2d(Mo[nm]), as2d(Vo[nm])) for nm in small_names]
    summed = _small_allreduce([small[nm] for nm in small_names], [t[0].shape for t in wmv], loss_row, "small_allreduce")
    res = _small_adamw(summed[:-1], wmv, "small_adamw")
    for a, nm in enumerate(small_names):
        out[nm] = tuple(r.reshape(W[nm].shape) for r in res[4 * a:4 * a + 4])
    loss = summed[-1][0, 0]

    order = ["norm_mix", "norm_ffn", "norm_final", "mix_w_in", "pool_w", "pool_scale", "attn_sinks", "mix_w_out", "ssm_w_in",
             "ssm_conv_w", "ssm_conv_b", "ssm_dt_bias", "ssm_A_log", "ssm_D", "ssm_norm", "ssm_w_out", "ffn_w_up", "ffn_conv_w",
             "ffn_conv_b", "ffn_w_down"]
    return (loss, grad_x.reshape(x.shape), *[out[nm][0] for nm in order], *[out[nm][1] for nm in order],
            *[out[nm][2] for nm in order], *[out[nm][3] for nm in order])
```

```python
import functools

import jax
import jax.numpy as jnp
from jax import lax
from jax.experimental import pallas as pl
from jax.experimental.pallas import tpu as pltpu

F32 = jnp.float32
BF16 = jnp.bfloat16
MXU = BF16
HI = lax.Precision.HIGHEST

D_MODEL = 1024
POOL_WINDOWS = (2, 4, 8, 16)
POOL_DIM = 512
POOL_GROUP = 128
HEAD_DIM = 64
N_HEADS = 8
N_KV_HEADS = 2
GQ = 4
Q_DIM = 512
KV_DIM = 128
BLOCK = 128
ROPE_THETA = 10000.0
MIX_IN_DIM = 1280
SSM_D_INNER = 2048
SSM_HEADS = 32
SSM_GROUPS = 8
SSM_STATE = 128
SSM_CONV = 4
SSM_CHUNK = 128
SSM_CONV_DIM = 4096
SSM_IN_DIM = 6176
D_FF = 2816
FFN_CONV = 3
NORM_EPS = 1e-6
SSM_NORM_EPS = 1e-5
ADAM_LR = 0.001
ADAM_B1 = 0.9
ADAM_B2 = 0.999
ADAM_EPS = 1e-08
ADAM_WD = 0.01
ADAM_STEP = 10

N_CHIPS = 4
N_DEV = 8
LANES = 128
SUBLANES = 8
V7X_VMEM_LIMIT = 56 * 1024 * 1024
NEG = -1e30
MESH = pl.DeviceIdType.MESH


def _cp(*sem):
    return pltpu.CompilerParams(dimension_semantics=sem if sem else None, vmem_limit_bytes=V7X_VMEM_LIMIT)


def _sds(shape, dtype=F32):
    return jax.ShapeDtypeStruct(tuple(shape), dtype)


def _iota(shape, dim):
    return lax.broadcasted_iota(jnp.int32, shape, dim)


def _silu(x):
    return x * (1.0 / (1.0 + jnp.exp(-x)))


def _dsilu(x):
    s = 1.0 / (1.0 + jnp.exp(-x))
    return s * (1.0 + x * (1.0 - s))


def _mm(a, b, *, ta=False, tb=False, tm, tn, tk, res=None, out_dtype=F32, out_shard_perm=None, out_into=None, norm_w=None,
        norm_bwd=None, name):
    M, K = (a.shape[1], a.shape[0]) if ta else a.shape
    N = b.shape[0] if tb else b.shape[1]
    tm, tn, tk = min(tm, M), min(tn, N), min(tk, K)
    gm, gn, gk = M // tm, N // tn, K // tk
    assert gm * tm == M and gn * tn == N and gk * tk == K, (name, M, N, K, tm, tn, tk)
    a_spec = pl.BlockSpec((tk, tm), lambda i, j, k: (k, i)) if ta else pl.BlockSpec((tm, tk), lambda i, j, k: (i, k))
    b_spec = pl.BlockSpec((tn, tk), lambda i, j, k: (j, k)) if tb else pl.BlockSpec((tk, tn), lambda i, j, k: (k, j))
    dims = (((0 if ta else 1,), (1 if tb else 0,)), ((), ()))
    has_res = res is not None
    has_nw = norm_w is not None
    has_nb = norm_bwd is not None
    has_tok = has_nb and norm_bwd[3] is not None
    assert not (has_nw or has_nb) or (gn == 1 and out_shard_perm is None)
    n_extra = has_res + has_nw + (3 + has_tok if has_nb else 0)

    def body(*refs):
        a_ref, b_ref = refs[0], refs[1]
        extra = list(refs[2:2 + n_extra])
        outs = refs[len(args):]
        r_ref = extra.pop(0) if has_res else None
        nw_ref = extra.pop(0) if has_nw else None
        nb_refs = extra if has_nb else None

        def dot():
            return lax.dot_general(a_ref[...].astype(MXU), b_ref[...].astype(MXU), dims, preferred_element_type=F32)

        def finish(r):
            if has_res:
                r = r + r_ref[...]
            if has_nb:
                xv = nb_refs[0][...]
                rs = lax.rsqrt(jnp.mean(xv * xv, axis=-1, keepdims=True) + NORM_EPS)
                xh = xv * rs
                g = r * nb_refs[1][...]
                dr = nb_refs[2][...] + nb_refs[3][0:1, 0:1] if has_tok else nb_refs[2][...]
                outs[0][...] = dr + rs * (g - xh * jnp.mean(g * xh, axis=-1, keepdims=True))
                part = jnp.sum(r * xh, axis=0, keepdims=True)
                i = pl.program_id(0)

                @pl.when(i == 0)
                def _():
                    outs[1][...] = part

                @pl.when(i > 0)
                def _():
                    outs[1][...] += part
                return
            outs[0][...] = r.astype(out_dtype)
            if has_nw:
                rs = lax.rsqrt(jnp.mean(r * r, axis=-1, keepdims=True) + NORM_EPS)
                outs[1][...] = (r * rs * nw_ref[...]).astype(outs[1].dtype)

        if gk == 1:
            finish(dot())
        else:
            acc = refs[-1]
            k = pl.program_id(2)

            @pl.when(k == 0)
            def _():
                acc[...] = dot()

            if gk > 2:
                @pl.when(jnp.logical_and(k > 0, k < gk - 1))
                def _():
                    acc[...] += dot()

            @pl.when(k == gk - 1)
            def _():
                finish(acc[...] + dot())

    tile = pl.BlockSpec((tm, tn), lambda i, j, k: (i, j))
    row = pl.BlockSpec((1, tn), lambda i, j, k: (0, j))
    in_specs = [a_spec, b_spec]
    args = [a, b]
    if has_res:
        in_specs.append(tile)
        args.append(res)
    if has_nw:
        in_specs.append(row)
        args.append(norm_w.reshape(1, N))
    if has_nb:
        in_specs += [tile, row, tile]
        args += [norm_bwd[0], norm_bwd[1].reshape(1, N), norm_bwd[2]]
        if has_tok:
            in_specs.append(pl.BlockSpec((SUBLANES, LANES), lambda i, j, k: (0, 0)))
            args.append(norm_bwd[3])
    alias = {}
    if out_into is not None:
        buf, rows, off = out_into
        out_spec = pl.BlockSpec((tm, tn), lambda i, j, k: (i + off, j))
        out_shape = _sds((rows, N), out_dtype)
        if buf is not None:
            alias = {len(args): 0}
            in_specs.append(pl.BlockSpec(memory_space=pl.ANY))
            args.append(buf)
    elif out_shard_perm is None:
        out_spec = tile
        out_shape = _sds((M, N), out_dtype)
    else:
        assert gn == len(out_shard_perm) == 4 and tuple(out_shard_perm) == (0, 2, 1, 3)
        out_spec = pl.BlockSpec((None, tm, tn), lambda i, j, k: ((j % 2) * 2 + j // 2, i, 0))
        out_shape = _sds((gn, M, tn), out_dtype)
    sem = ("parallel", "parallel", "arbitrary")
    if has_nw:
        out_spec, out_shape = [out_spec, tile], [out_shape, _sds((M, N), MXU)]
    if has_nb:
        out_spec, out_shape = [tile, row], [_sds((M, N)), _sds((1, N))]
        sem = ("arbitrary", "arbitrary", "arbitrary")
    return pl.pallas_call(
        body, grid=(gm, gn, gk), in_specs=in_specs, out_specs=out_spec, out_shape=out_shape,
        scratch_shapes=[pltpu.VMEM((tm, tn), F32)] if gk > 1 else [], input_output_aliases=alias,
        compiler_params=_cp(*sem), name=name)(*args)


def _put_rows(buf, src, rows, at, name):
    assert at % rows == 0 and src.shape[1] == buf.shape[1] and src.dtype == buf.dtype
    C = buf.shape[1]

    def body(s_ref, b_ref, o_ref):
        o_ref[...] = s_ref[...]

    return pl.pallas_call(
        body, grid=(1,), in_specs=[pl.BlockSpec((rows, C), lambda i: (0, 0)), pl.BlockSpec(memory_space=pl.ANY)],
        out_specs=pl.BlockSpec((rows, C), lambda i: (at // rows, 0)), out_shape=_sds(buf.shape, buf.dtype),
        input_output_aliases={1: 0}, compiler_params=_cp("arbitrary"), name=name)(src, buf)


def _rmsnorm_fwd(x, w, name, token=None):
    T, D = x.shape
    tm = min(T, 512)
    has_token = token is not None

    def body(*refs):
        x_ref, w_ref, o_ref = refs[0], refs[1], refs[-1]
        xv = x_ref[...]
        if has_token:
            xv = xv + refs[2][0:1, 0:1]
        r = lax.rsqrt(jnp.mean(xv * xv, axis=-1, keepdims=True) + NORM_EPS)
        o_ref[...] = (xv * r * w_ref[...]).astype(o_ref.dtype)

    in_specs = [pl.BlockSpec((tm, D), lambda i: (i, 0)), pl.BlockSpec((1, D), lambda i: (0, 0))]
    args = [x, w.reshape(1, D)]
    if has_token:
        in_specs.append(pl.BlockSpec((SUBLANES, LANES), lambda i: (0, 0)))
        args.append(token)
    return pl.pallas_call(
        body, grid=(T // tm,), in_specs=in_specs,
        out_specs=pl.BlockSpec((tm, D), lambda i: (i, 0)), out_shape=_sds((T, D), MXU),
        compiler_params=_cp("parallel"), name=name)(*args)


def _loss_head(x, w, target, name):
    T, D = x.shape
    tm = min(T, 512)

    def body(x_ref, w_ref, t_ref, loss_ref, dx_ref, dw_ref):
        xv = x_ref[...]
        r = lax.rsqrt(jnp.mean(xv * xv, axis=-1, keepdims=True) + NORM_EPS)
        xh = xv * r
        wv = w_ref[...]
        e = xh * wv - t_ref[...]
        lpart = 0.5 * jnp.sum(jnp.mean(e * e, axis=-1, keepdims=True), axis=0, keepdims=True)
        dy = e * (1.0 / D)
        g = dy * wv
        dx_ref[...] = r * (g - xh * jnp.mean(g * xh, axis=-1, keepdims=True))
        part = jnp.sum(dy * xh, axis=0, keepdims=True)
        lrow = jnp.broadcast_to(lpart, (1, LANES))

        @pl.when(pl.program_id(0) == 0)
        def _():
            dw_ref[...] = part
            loss_ref[...] = lrow

        @pl.when(pl.program_id(0) > 0)
        def _():
            dw_ref[...] += part
            loss_ref[...] += lrow

    row = pl.BlockSpec((tm, D), lambda i: (i, 0))
    vec = pl.BlockSpec((1, D), lambda i: (0, 0))
    return pl.pallas_call(
        body, grid=(T // tm,), in_specs=[row, vec, row],
        out_specs=[pl.BlockSpec((1, LANES), lambda i: (0, 0)), row, vec],
        out_shape=[_sds((1, LANES)), _sds((T, D)), _sds((1, D))],
        compiler_params=_cp("arbitrary"), name=name)(x, w.reshape(1, D), target)


def _shift_down(cur, prev8, s):
    if s == 0:
        return cur
    tm = cur.shape[0]
    rc = pltpu.roll(cur, s, 0)
    top = jnp.where(_iota((SUBLANES, cur.shape[1]), 0) < s, pltpu.roll(prev8, s, 0), rc[:SUBLANES])
    return jnp.concatenate([top, rc[SUBLANES:]], axis=0) if tm > SUBLANES else top


def _shift_up(cur, next8, s):
    if s == 0:
        return cur
    tm = cur.shape[0]
    rc = pltpu.roll(cur, tm - s, 0)
    bot = jnp.where(_iota((SUBLANES, cur.shape[1]), 0) >= SUBLANES - s, pltpu.roll(next8, SUBLANES - s, 0), rc[tm - SUBLANES:])
    return jnp.concatenate([rc[:tm - SUBLANES], bot], axis=0) if tm > SUBLANES else bot


def _conv_rows(cur, prev8, w, b, K):
    acc = cur * w[K - 1:K, :] + b
    for s in range(1, K):
        acc = acc + _shift_down(cur, prev8, s) * w[K - 1 - s:K - s, :]
    return acc


FFN_TC = 1408
HALO16 = 2 * SUBLANES


def _ffn_up_conv_gate(hf, wup, cw, cb, name):
    T, D = hf.shape
    tm = min(T, 256)
    nt, nj = T // tm, D_FF // FFN_TC
    K = FFN_CONV
    W2 = 2 * FFN_TC

    def body(a_ref, b_ref, w_ref, c_ref, hid_ref, hc_ref, act_ref, halo):
        i = pl.program_id(1)

        @pl.when(i == 0)
        def _():
            halo[...] = jnp.zeros(halo.shape, F32)

        hb = jnp.dot(a_ref[...].astype(MXU), b_ref[...].astype(MXU), preferred_element_type=F32).astype(hid_ref.dtype)
        hid_ref[...] = hb
        cur = hb.astype(F32)
        hc = _conv_rows(cur, halo[...], w_ref[...], c_ref[...], K)
        halo[...] = cur[tm - SUBLANES:]
        hc_ref[...] = hc
        act_ref[...] = (_silu(hc[:, FFN_TC:]) * hc[:, :FFN_TC]).astype(act_ref.dtype)

    blk = pl.BlockSpec((tm, W2), lambda j, i: (i, j))
    return pl.pallas_call(
        body, grid=(nj, nt),
        in_specs=[pl.BlockSpec((tm, D), lambda j, i: (i, 0)), pl.BlockSpec((D, W2), lambda j, i: (0, j)),
                  pl.BlockSpec((K, W2), lambda j, i: (0, j)), pl.BlockSpec((1, W2), lambda j, i: (0, j))],
        out_specs=[blk, blk, pl.BlockSpec((tm, FFN_TC), lambda j, i: (i, j))],
        out_shape=[_sds((T, 2 * D_FF), MXU), _sds((T, 2 * D_FF)), _sds((T, D_FF), MXU)],
        scratch_shapes=[pltpu.VMEM((SUBLANES, W2), F32)],
        compiler_params=_cp("arbitrary", "arbitrary"), name=name)(hf, wup, cw, cb)


def _ffn_down_dx_mid_bwd(dxo, wdn, hid, hc, cw, name):
    T, D = dxo.shape
    tm = min(T, 256)
    nt, nj = T // tm, D_FF // FFN_TC
    K = FFN_CONV
    W2 = 2 * FFN_TC

    def body(g_ref, wd_ref, h_ref, c_ref, w_ref, dh_ref, dw_ref, db_ref, ahead):
        i = pl.program_id(1)

        @pl.when(i == 0)
        def _():
            ahead[...] = jnp.zeros(ahead.shape, F32)

        w = w_ref[...]
        cur = h_ref[...].astype(F32)
        hcv = c_ref[...]
        dav = _nt(g_ref[...], wd_ref[...])
        u, g = hcv[:, :FFN_TC], hcv[:, FFN_TC:]
        d_cur = jnp.concatenate([dav * _silu(g), dav * u * _dsilu(g)], axis=1)
        d_nxt = ahead[...]
        ahead[...] = d_cur[:SUBLANES]
        ups = [d_cur] + [_shift_up(d_cur, d_nxt, s) for s in range(1, K)]
        dh = ups[0] * w[K - 1:K, :]
        for s in range(1, K):
            dh = dh + ups[s] * w[K - 1 - s:K - s, :]
        dh_ref[...] = dh.astype(dh_ref.dtype)
        dwp = jnp.concatenate([jnp.sum(ups[K - 1 - k] * cur, axis=0, keepdims=True) for k in range(K)], axis=0)
        dbp = jnp.sum(d_cur, axis=0, keepdims=True)

        @pl.when(i == 0)
        def _():
            dw_ref[...] = dwp
            db_ref[...] = dbp

        @pl.when(i > 0)
        def _():
            dw_ref[...] += dwp
            db_ref[...] += dbp

    blk = pl.BlockSpec((tm, W2), lambda j, i: (nt - 1 - i, j))
    return pl.pallas_call(
        body, grid=(nj, nt),
        in_specs=[pl.BlockSpec((tm, D), lambda j, i: (nt - 1 - i, 0)), pl.BlockSpec((FFN_TC, D), lambda j, i: (j, 0)), blk, blk,
                  pl.BlockSpec((K, W2), lambda j, i: (0, j))],
        out_specs=[blk, pl.BlockSpec((K, W2), lambda j, i: (0, j)), pl.BlockSpec((1, W2), lambda j, i: (0, j))],
        out_shape=[_sds((T, 2 * D_FF), MXU), _sds((K, 2 * D_FF)), _sds((1, 2 * D_FF))],
        scratch_shapes=[pltpu.VMEM((SUBLANES, W2), F32)],
        compiler_params=_cp("arbitrary", "arbitrary"), name=name)(dxo, wdn, hid, hc, cw)


def _rope(t, cos, sin_s, inverse=False):
    n = t.shape[1] // LANES
    c = jnp.concatenate([cos] * n, axis=1) if n > 1 else cos
    s = jnp.concatenate([sin_s] * n, axis=1) if n > 1 else sin_s
    a = pltpu.roll(t, HEAD_DIM // 2, 1)
    b = pltpu.roll(t, t.shape[1] - HEAD_DIM // 2, 1)
    first = (_iota(t.shape, 1) % HEAD_DIM) < HEAD_DIM // 2
    rot = jnp.where(first, b, a) * s
    return t * c - rot if inverse else t * c + rot


def _stack_heads(t, g):
    return jnp.concatenate([t[:, (GQ * g + r) * HEAD_DIM:(GQ * g + r + 1) * HEAD_DIM] for r in range(GQ)], axis=0)


def _stack_cols(t, g):
    return jnp.concatenate([t[:, GQ * g + r:GQ * g + r + 1] for r in range(GQ)], axis=0)


def _pool_sums(prev, cur, w):
    s = jnp.concatenate([prev, cur], axis=0)
    sh = 1
    while sh < w:
        s = s + pltpu.roll(s, sh, 0)
        sh *= 2
    return s[BLOCK:]


def _nt(a, b):
    return lax.dot_general(a.astype(MXU), b.astype(MXU), (((1,), (1,)), ((), ())), preferred_element_type=F32)


def _tn(a, b):
    return lax.dot_general(a.astype(MXU), b.astype(MXU), (((0,), (0,)), ((), ())), preferred_element_type=F32)


def _nn(a, b):
    return jnp.dot(a.astype(MXU), b.astype(MXU), preferred_element_type=F32)


def _mixcore_fwd(proj, cos, sin_s, pool_w, pool_scale, sinks, name):
    T = proj.shape[0]
    nb = T // BLOCK
    scale = HEAD_DIM ** -0.5

    def body(p_ref, pp_ref, c_ref, s_ref, cp_ref, sp_ref, pw_ref, ps_ref, sk_ref, cat_ref, at_ref, lse_ref):
        i = pl.program_id(0)
        has_prev = i > 0
        cur = p_ref[...]
        prv = jnp.where(has_prev, pp_ref[...], 0.0)
        tpos = (i * BLOCK + _iota((BLOCK, 1), 0) + 1).astype(F32)
        for g, w in enumerate(POOL_WINDOWS):
            sl = slice(g * POOL_GROUP, (g + 1) * POOL_GROUP)
            pooled = _pool_sums(prv[:, sl], cur[:, sl], w) / jnp.minimum(tpos, float(w)) - cur[:, sl]
            cat_ref[:, sl] = (_nn(pooled, pw_ref[g]) * ps_ref[:, sl]).astype(cat_ref.dtype)
        q = _rope(cur[:, POOL_DIM:POOL_DIM + Q_DIM], c_ref[...], s_ref[...])
        kc = _rope(cur[:, POOL_DIM + Q_DIM:POOL_DIM + Q_DIM + KV_DIM], c_ref[...], s_ref[...])
        kp = _rope(prv[:, POOL_DIM + Q_DIM:POOL_DIM + Q_DIM + KV_DIM], cp_ref[...], sp_ref[...])
        vc = cur[:, POOL_DIM + Q_DIM + KV_DIM:]
        vp = prv[:, POOL_DIM + Q_DIM + KV_DIM:]
        ri = _iota((GQ * BLOCK, BLOCK), 0) % BLOCK
        cj = _iota((GQ * BLOCK, BLOCK), 1)
        mc = cj <= ri
        mp = jnp.logical_and(cj > ri, has_prev)
        outs, lses = [], []
        for g in range(N_KV_HEADS):
            hs = slice(g * HEAD_DIM, (g + 1) * HEAD_DIM)
            qg = _stack_heads(q, g) * scale
            sc = jnp.where(mc, _nt(qg, kc[:, hs]), NEG)
            sp = jnp.where(mp, _nt(qg, kp[:, hs]), NEG)
            sink = jnp.concatenate([jnp.full((BLOCK, 1), sk_ref[GQ * g + r], F32) for r in range(GQ)], axis=0)
            m = jnp.maximum(jnp.maximum(jnp.max(sc, axis=1, keepdims=True), jnp.max(sp, axis=1, keepdims=True)), sink)
            pc = jnp.exp(sc - m)
            pp = jnp.exp(sp - m)
            den = jnp.sum(pc, axis=1, keepdims=True) + jnp.sum(pp, axis=1, keepdims=True) + jnp.exp(sink - m)
            o = (_nn(pc, vc[:, hs]) + _nn(pp, vp[:, hs])) / den
            lse = m + jnp.log(den)
            for r in range(GQ):
                outs.append(o[r * BLOCK:(r + 1) * BLOCK])
                lses.append(lse[r * BLOCK:(r + 1) * BLOCK])
        attn = jnp.concatenate(outs, axis=1)
        at_ref[...] = attn
        cat_ref[:, POOL_DIM:] = attn.astype(cat_ref.dtype)
        lane = _iota((BLOCK, LANES), 1)
        lrow = jnp.zeros((BLOCK, LANES), F32)
        for h in range(N_HEADS):
            lrow = jnp.where(lane == h, lses[h], lrow)
        lse_ref[...] = lrow

    cur = lambda w: pl.BlockSpec((BLOCK, w), lambda i: (i, 0))
    prv = lambda w: pl.BlockSpec((BLOCK, w), lambda i: (jnp.maximum(i - 1, 0), 0))
    return pl.pallas_call(
        body, grid=(nb,),
        in_specs=[cur(MIX_IN_DIM), prv(MIX_IN_DIM), cur(LANES), cur(LANES), prv(LANES), prv(LANES),
                  pl.BlockSpec((4, POOL_GROUP, POOL_GROUP), lambda i: (0, 0, 0)), pl.BlockSpec((1, POOL_DIM), lambda i: (0, 0)),
                  pl.BlockSpec(memory_space=pltpu.SMEM)],
        out_specs=[cur(2 * POOL_DIM), cur(Q_DIM), cur(LANES)],
        out_shape=[_sds((T, 2 * POOL_DIM), MXU), _sds((T, Q_DIM)), _sds((T, LANES))],
        compiler_params=_cp("parallel"), name=name)(proj, proj, cos, sin_s, cos, sin_s, pool_w, pool_scale, sinks)


def _mixcore_bwd(proj, cos, sin_s, pool_w, pool_scale, sinks, attn, lse, dcat, name):
    T = proj.shape[0]
    nb = T // BLOCK
    scale = HEAD_DIM ** -0.5
    QO, KO, VO = POOL_DIM, POOL_DIM + Q_DIM, POOL_DIM + Q_DIM + KV_DIM

    def body(p_ref, pp_ref, pn_ref, c_ref, s_ref, cp_ref, sp_ref, cn_ref, sn_ref, pw_ref, ps_ref, sk_ref,
             at_ref, atn_ref, l_ref, ln_ref, d_ref, dn_ref, dp_ref, dpw_ref, dps_ref, dsk_ref):
        i = pl.program_id(0)
        has_prev = i > 0
        has_next = i < nb - 1
        cur = p_ref[...]
        prv = jnp.where(has_prev, pp_ref[...], 0.0)
        d_cur = d_ref[...]
        d_nxt = jnp.where(has_next, dn_ref[...], 0.0)

        tpos = (i * BLOCK + _iota((BLOCK, 1), 0) + 1).astype(F32)
        tpos2 = (i * BLOCK + _iota((2 * BLOCK, 1), 0) + 1).astype(F32)
        ps = ps_ref[...]
        dps_parts, dpw_parts = [], []
        for g, w in enumerate(POOL_WINDOWS):
            sl = slice(g * POOL_GROUP, (g + 1) * POOL_GROUP)
            pooled = _pool_sums(prv[:, sl], cur[:, sl], w) / jnp.minimum(tpos, float(w)) - cur[:, sl]
            mixed = _nn(pooled, pw_ref[g])
            dps_parts.append(jnp.sum(d_cur[:, sl] * mixed, axis=0, keepdims=True))
            dm2 = jnp.concatenate([d_cur[:, sl], d_nxt[:, sl]], axis=0) * ps[:, sl]
            dpw_parts.append(_tn(pooled, dm2[:BLOCK]))
            dpool2 = _nt(dm2, pw_ref[g])
            e = dpool2 / jnp.minimum(tpos2, float(w))
            sh = 1
            while sh < w:
                e = e + pltpu.roll(e, 2 * BLOCK - sh, 0)
                sh *= 2
            dp_ref[:, sl] = (e[:BLOCK] - dpool2[:BLOCK]).astype(dp_ref.dtype)
        dpsp = jnp.concatenate(dps_parts, axis=1)

        nxt = pn_ref[...]
        q = _rope(cur[:, QO:KO], c_ref[...], s_ref[...])
        qn = _rope(nxt[:, QO:KO], cn_ref[...], sn_ref[...])
        kc = _rope(cur[:, KO:VO], c_ref[...], s_ref[...])
        kp = _rope(prv[:, KO:VO], cp_ref[...], sp_ref[...])
        vc, vp = cur[:, VO:], prv[:, VO:]
        do, don = d_cur[:, POOL_DIM:], d_nxt[:, POOL_DIM:]
        dl = do * at_ref[...]
        dln = don * atn_ref[...]
        lse, lsen = l_ref[...], ln_ref[...]
        ri = _iota((GQ * BLOCK, BLOCK), 0) % BLOCK
        cj = _iota((GQ * BLOCK, BLOCK), 1)
        mc = cj <= ri
        mp = jnp.logical_and(cj > ri, has_prev)
        mn = jnp.logical_and(cj > ri, has_next)
        dq_parts, dk_parts, dv_parts, dsk_vals = [], [], [], []
        for g in range(N_KV_HEADS):
            hs = slice(g * HEAD_DIM, (g + 1) * HEAD_DIM)
            qg, qng = _stack_heads(q, g) * scale, _stack_heads(qn, g) * scale
            dog, dong = _stack_heads(do, g), _stack_heads(don, g)
            delta = jnp.sum(_stack_heads(dl, g), axis=1, keepdims=True)
            deltan = jnp.sum(_stack_heads(dln, g), axis=1, keepdims=True)
            lg, lng = _stack_cols(lse, g), _stack_cols(lsen, g)
            pc = jnp.where(mc, jnp.exp(_nt(qg, kc[:, hs]) - lg), 0.0)
            pp = jnp.where(mp, jnp.exp(_nt(qg, kp[:, hs]) - lg), 0.0)
            pn = jnp.where(mn, jnp.exp(_nt(qng, kc[:, hs]) - lng), 0.0)
            dsc = pc * (_nt(dog, vc[:, hs]) - delta)
            dsp = pp * (_nt(dog, vp[:, hs]) - delta)
            dsn = pn * (_nt(dong, vc[:, hs]) - deltan)
            dqg = (_nn(dsc, kc[:, hs]) + _nn(dsp, kp[:, hs])) * scale
            dq_parts += [dqg[r * BLOCK:(r + 1) * BLOCK] for r in range(GQ)]
            dk_parts.append(_tn(dsc, qg) + _tn(dsn, qng))
            dv_parts.append(_tn(pc, dog) + _tn(pn, dong))
            sink = jnp.concatenate([jnp.full((BLOCK, 1), sk_ref[GQ * g + r], F32) for r in range(GQ)], axis=0)
            dsk = -jnp.exp(sink - lg) * delta
            dsk_vals += [jnp.sum(dsk[r * BLOCK:(r + 1) * BLOCK], axis=0, keepdims=True) for r in range(GQ)]
        dq = _rope(jnp.concatenate(dq_parts, axis=1), c_ref[...], s_ref[...], inverse=True)
        dk = _rope(jnp.concatenate(dk_parts, axis=1), c_ref[...], s_ref[...], inverse=True)
        dp_ref[:, QO:KO] = dq.astype(dp_ref.dtype)
        dp_ref[:, KO:VO] = dk.astype(dp_ref.dtype)
        dp_ref[:, VO:] = jnp.concatenate(dv_parts, axis=1).astype(dp_ref.dtype)
        lane = _iota((1, LANES), 1)
        dskp = jnp.zeros((1, LANES), F32)
        for h in range(N_HEADS):
            dskp = jnp.where(lane == h, dsk_vals[h], dskp)

        @pl.when(i == 0)
        def _():
            dps_ref[...] = dpsp
            dsk_ref[...] = dskp
            for g in range(4):
                dpw_ref[g] = dpw_parts[g]

        @pl.when(i > 0)
        def _():
            dps_ref[...] += dpsp
            dsk_ref[...] += dskp
            for g in range(4):
                dpw_ref[g] += dpw_parts[g]

    cur = lambda w: pl.BlockSpec((BLOCK, w), lambda i: (i, 0))
    prv = lambda w: pl.BlockSpec((BLOCK, w), lambda i: (jnp.maximum(i - 1, 0), 0))
    nxt = lambda w: pl.BlockSpec((BLOCK, w), lambda i: (jnp.minimum(i + 1, nb - 1), 0))
    return pl.pallas_call(
        body, grid=(nb,),
        in_specs=[cur(MIX_IN_DIM), prv(MIX_IN_DIM), nxt(MIX_IN_DIM),
                  cur(LANES), cur(LANES), prv(LANES), prv(LANES), nxt(LANES), nxt(LANES),
                  pl.BlockSpec((4, POOL_GROUP, POOL_GROUP), lambda i: (0, 0, 0)), pl.BlockSpec((1, POOL_DIM), lambda i: (0, 0)),
                  pl.BlockSpec(memory_space=pltpu.SMEM),
                  cur(Q_DIM), nxt(Q_DIM), cur(LANES), nxt(LANES), cur(2 * POOL_DIM), nxt(2 * POOL_DIM)],
        out_specs=[cur(MIX_IN_DIM), pl.BlockSpec((4, POOL_GROUP, POOL_GROUP), lambda i: (0, 0, 0)),
                   pl.BlockSpec((1, POOL_DIM), lambda i: (0, 0)), pl.BlockSpec((1, LANES), lambda i: (0, 0))],
        out_shape=[_sds((T, MIX_IN_DIM), MXU), _sds((4, POOL_GROUP, POOL_GROUP)), _sds((1, POOL_DIM)), _sds((1, LANES))],
        compiler_params=_cp("arbitrary"), name=name)(
            proj, proj, proj, cos, sin_s, cos, sin_s, cos, sin_s, pool_w, pool_scale, sinks, attn, attn, lse, lse, dcat, dcat)


SSM_TC = 512
GROUP_W = SSM_D_INNER // SSM_GROUPS


def _ssm_in_conv(h, wT, cw, cb, name):
    T, D = h.shape
    tm = min(T, 256)
    tc = 1024
    K = SSM_CONV

    def body(a_ref, b_ref, w_ref, c_ref, x_ref, pre_ref, act_ref, halo):
        @pl.when(pl.program_id(1) == 0)
        def _():
            halo[...] = jnp.zeros(halo.shape, F32)

        cur = _nt(a_ref[...], b_ref[...])
        x_ref[...] = cur
        pre = _conv_rows(cur, halo[...], w_ref[...], c_ref[...], K)
        halo[...] = cur[tm - SUBLANES:]
        pre_ref[...] = pre
        act_ref[...] = _silu(pre)

    blk = pl.BlockSpec((tm, tc), lambda j, i: (i, j))
    return pl.pallas_call(
        body, grid=(SSM_CONV_DIM // tc, T // tm),
        in_specs=[pl.BlockSpec((tm, D), lambda j, i: (i, 0)), pl.BlockSpec((tc, D), lambda j, i: (j, 0)),
                  pl.BlockSpec((K, tc), lambda j, i: (0, j)), pl.BlockSpec((1, tc), lambda j, i: (0, j))],
        out_specs=[blk, blk, blk], out_shape=[_sds((T, SSM_CONV_DIM))] * 3,
        scratch_shapes=[pltpu.VMEM((SUBLANES, tc), F32)],
        compiler_params=_cp("arbitrary", "arbitrary"), name=name)(h, wT, cw, cb)


def _ssm_pre_bwd(xbc, pre, cw, dact, name):
    T = xbc.shape[0]
    tm = min(T, 512)
    nt = T // tm
    K = SSM_CONV
    q = tm // SUBLANES
    tc = SSM_TC

    def body(x_ref, p_ref, pn_ref, d_ref, dn_ref, w_ref, dx_ref, dw_ref, db_ref):
        i = pl.program_id(1)
        w = w_ref[...]
        cur = x_ref[...]
        d_cur = d_ref[...] * _dsilu(p_ref[...])
        d_nxt = jnp.where(i == nt - 1, 0.0, dn_ref[...] * _dsilu(pn_ref[...]))
        ups = [d_cur] + [_shift_up(d_cur, d_nxt, s) for s in range(1, K)]
        dx = ups[0] * w[K - 1:K, :]
        for s in range(1, K):
            dx = dx + ups[s] * w[K - 1 - s:K - s, :]
        dx_ref[...] = dx.astype(dx_ref.dtype)
        dwp = jnp.concatenate([jnp.sum(ups[K - 1 - k] * cur, axis=0, keepdims=True) for k in range(K)], axis=0)
        dbp = jnp.sum(d_cur, axis=0, keepdims=True)

        @pl.when(i == 0)
        def _():
            dw_ref[...] = dwp
            db_ref[...] = dbp

        @pl.when(i > 0)
        def _():
            dw_ref[...] += dwp
            db_ref[...] += dbp

    nxt_row = lambda i: jnp.minimum((i + 1) * q, nt * q - 1)
    return pl.pallas_call(
        body, grid=(SSM_CONV_DIM // tc, nt),
        in_specs=[pl.BlockSpec((tm, tc), lambda j, i: (i, j)),
                  pl.BlockSpec((tm, tc), lambda j, i: (i, j)),
                  pl.BlockSpec((SUBLANES, tc), lambda j, i: (nxt_row(i), j)),
                  pl.BlockSpec((tm, tc), lambda j, i: (i, j)),
                  pl.BlockSpec((SUBLANES, tc), lambda j, i: (nxt_row(i), j)),
                  pl.BlockSpec((K, tc), lambda j, i: (0, j))],
        out_specs=[pl.BlockSpec((tm, tc), lambda j, i: (i, j)), pl.BlockSpec((K, tc), lambda j, i: (0, j)),
                   pl.BlockSpec((1, tc), lambda j, i: (0, j))],
        out_shape=[_sds((T, SSM_CONV_DIM), MXU), _sds((K, SSM_CONV_DIM)), _sds((1, SSM_CONV_DIM))],
        compiler_params=_cp("parallel", "arbitrary"), name=name)(xbc, pre, pre, dact, dact, cw)


def _dot_hi(a, b):
    return jnp.dot(a, b, precision=HI, preferred_element_type=F32)


def _ssd_common(dtraw, bias, alog):
    L = SSM_CHUNK
    xb = dtraw + bias
    dt = jnp.maximum(xb, 0.0) + jnp.log1p(jnp.exp(-jnp.abs(xb)))
    A = -jnp.exp(alog)
    tril = (_iota((L, L), 1) <= _iota((L, L), 0)).astype(F32)
    acs = _dot_hi(tril, dt * A)
    return xb, dt, A, tril, acs


def _head_selectors():
    es = (_iota((LANES, SSM_D_INNER), 0) == _iota((LANES, SSM_D_INNER), 1) // HEAD_DIM).astype(BF16)
    est = (_iota((SSM_D_INNER, LANES), 1) == _iota((SSM_D_INNER, LANES), 0) // HEAD_DIM).astype(BF16)
    return es, est


def _dot_sel(v, sel):
    hi = v.astype(BF16)
    r1 = v - hi.astype(F32)
    mid = r1.astype(BF16)
    lo = (r1 - mid.astype(F32)).astype(BF16)
    d = lambda a: jnp.dot(a, sel, preferred_element_type=F32)
    return (d(hi) + d(mid)) + d(lo)


def _expand_heads(v, es):
    return _dot_sel(v, es)


def _reduce_heads(q, est):
    return _dot_sel(q, est)


def _per_state_row(v, g):
    return jnp.concatenate([jnp.broadcast_to(v[:, GQ * g + r:GQ * g + r + 1], (HEAD_DIM, 1)) for r in range(GQ)], axis=0)


def _ssd_fwd(xact, dtraw, dt_bias, a_log, name):
    T = xact.shape[0]
    nc = T // SSM_CHUNK
    L = SSM_CHUNK
    BO, CO = SSM_D_INNER, SSM_D_INNER + SSM_GROUPS * SSM_STATE

    def body(x_ref, dt_ref, bias_ref, al_ref, es_ref, y_ref, st_ref, state):
        @pl.when(pl.program_id(0) == 0)
        def _():
            state[...] = jnp.zeros(state.shape, F32)

        _, dt, A, tril, acs = _ssd_common(dt_ref[...], bias_ref[...], al_ref[...])
        acsT = acs.T
        last = acs[L - 1:L, :]
        cd = jnp.exp(last)
        es = es_ref[...]
        dtX = _expand_heads(dt, es)
        EX = _expand_heads(jnp.exp(acs), es)
        decX = _expand_heads(jnp.exp(last - acs), es)
        for g in range(SSM_GROUPS):
            gs = slice(g * GROUP_W, (g + 1) * GROUP_W)
            B = x_ref[:, BO + g * SSM_STATE:BO + (g + 1) * SSM_STATE]
            C = x_ref[:, CO + g * SSM_STATE:CO + (g + 1) * SSM_STATE]
            X = x_ref[:, gs] * dtX[:, gs]
            CB = _nt(C, B)
            yd = []
            for r in range(GQ):
                h = GQ * g + r
                Lm = jnp.exp(jnp.where(tril > 0, acs[:, h:h + 1] - acsT[h:h + 1, :], NEG))
                yd.append(_nn(CB * Lm, X[:, r * HEAD_DIM:(r + 1) * HEAD_DIM]))
            S = state[g]
            st_ref[g] = S
            y_ref[:, gs] = jnp.concatenate(yd, axis=1) + _nt(C, S) * EX[:, gs]
            state[g] = S * _per_state_row(cd, g) + _tn(X * decX[:, gs], B)

    es, _ = _head_selectors()
    return pl.pallas_call(
        body, grid=(nc,),
        in_specs=[pl.BlockSpec((L, SSM_CONV_DIM), lambda c: (c, 0)), pl.BlockSpec((L, LANES), lambda c: (c, 0)),
                  pl.BlockSpec((1, LANES), lambda c: (0, 0)), pl.BlockSpec((1, LANES), lambda c: (0, 0)),
                  pl.BlockSpec((LANES, SSM_D_INNER), lambda c: (0, 0))],
        out_specs=[pl.BlockSpec((L, SSM_D_INNER), lambda c: (c, 0)),
                   pl.BlockSpec((None, SSM_GROUPS, GROUP_W, SSM_STATE), lambda c: (c, 0, 0, 0))],
        out_shape=[_sds((T, SSM_D_INNER)), _sds((nc, SSM_GROUPS, GROUP_W, SSM_STATE))],
        scratch_shapes=[pltpu.VMEM((SSM_GROUPS, GROUP_W, SSM_STATE), F32)],
        compiler_params=_cp("arbitrary"), name=name)(xact, dtraw, dt_bias, a_log, es)


def _ssd_bwd(xact, dtraw, dt_bias, a_log, d_skip, states, dy, name):
    T = xact.shape[0]
    nc = T // SSM_CHUNK
    L = SSM_CHUNK
    BO, CO = SSM_D_INNER, SSM_D_INNER + SSM_GROUPS * SSM_STATE

    def body(x_ref, dt_ref, bias_ref, al_ref, dsk_ref, es_ref, est_ref, st_ref, dy_ref,
             dxp_ref, ddt_ref, dbias_ref, dal_ref, dd_ref, dstate, qa, qx):
        cc = pl.program_id(0)

        @pl.when(cc == 0)
        def _():
            dstate[...] = jnp.zeros(dstate.shape, F32)

        xb, dt, A, tril, acs = _ssd_common(dt_ref[...], bias_ref[...], al_ref[...])
        acsT = acs.T
        last = acs[L - 1:L, :]
        cd = jnp.exp(last)
        es, est = es_ref[...], est_ref[...]
        dtX = _expand_heads(dt, es)
        EX = _expand_heads(jnp.exp(acs), es)
        decX = _expand_heads(jnp.exp(last - acs), es)
        lane1 = _iota((1, LANES), 1)
        lane = _iota((L, LANES), 1)
        sub = _iota((L, LANES), 0)
        ztot = jnp.zeros((1, LANES), F32)
        wrow = jnp.zeros((L, LANES), F32)
        wcolT = jnp.zeros((LANES, L), F32)
        rows_dec, rows_dd = [], []
        for g in range(SSM_GROUPS):
            gs = slice(g * GROUP_W, (g + 1) * GROUP_W)
            x = x_ref[:, gs]
            B = x_ref[:, BO + g * SSM_STATE:BO + (g + 1) * SSM_STATE]
            C = x_ref[:, CO + g * SSM_STATE:CO + (g + 1) * SSM_STATE]
            dY = dy_ref[:, gs]
            dtx, e_x, dec_x = dtX[:, gs], EX[:, gs], decX[:, gs]
            X = x * dtx
            CB = _nt(C, B)
            S = st_ref[g]
            dS_out = dstate[g]
            dcb_sum = jnp.zeros((L, L), F32)
            dxd = []
            for r in range(GQ):
                h = GQ * g + r
                hs = slice(r * HEAD_DIM, (r + 1) * HEAD_DIM)
                Lm = jnp.exp(jnp.where(tril > 0, acs[:, h:h + 1] - acsT[h:h + 1, :], NEG))
                M = CB * Lm
                dM = _nt(dY[:, hs], X[:, hs])
                dxd.append(_tn(M, dY[:, hs]))
                dcb_sum = dcb_sum + dM * Lm
                Wm = dM * M
                wrow = jnp.where(lane == h, jnp.sum(Wm, axis=1, keepdims=True), wrow)
                wcolT = jnp.where(sub == h, jnp.sum(Wm, axis=0, keepdims=True), wcolT)
            dXd = jnp.concatenate(dxd, axis=1)
            G = _nt(C, S)
            dG = dY * e_x
            dDX = _nt(B, dS_out)
            dX = dXd + dec_x * dDX
            t_dec = dDX * X * dec_x
            qa[:, gs] = dG * G - t_dec
            qx[:, gs] = dX * x
            rows_dec.append(jnp.sum(t_dec, axis=0, keepdims=True))
            rows_dd.append(jnp.sum(dY * x, axis=0, keepdims=True))
            zc = jnp.sum(dS_out * S, axis=1, keepdims=True)
            for r in range(GQ):
                ztot = jnp.where(lane1 == GQ * g + r, jnp.sum(zc[r * HEAD_DIM:(r + 1) * HEAD_DIM], axis=0, keepdims=True), ztot)
            dxp_ref[:, gs] = dX * dtx + dY * dsk_ref[:, gs]
            dxp_ref[:, BO + g * SSM_STATE:BO + (g + 1) * SSM_STATE] = _tn(dcb_sum, C) + _nn(X * dec_x, dS_out)
            dxp_ref[:, CO + g * SSM_STATE:CO + (g + 1) * SSM_STATE] = _nn(dcb_sum, B) + _nn(dG, S)
            dstate[g] = dS_out * _per_state_row(cd, g) + _tn(dG, C)
        rows = jnp.concatenate([jnp.concatenate(rows_dec, axis=1), jnp.concatenate(rows_dd, axis=1)]
                               + [jnp.zeros((SUBLANES - 2, SSM_D_INNER), F32)], axis=0)
        rsum = _reduce_heads(rows, est)
        dlast = rsum[0:1, :] + cd * ztot
        dacs = (wrow - wcolT.T) + _reduce_heads(qa[...], est) + jnp.where(sub == L - 1, dlast, 0.0)
        triu = (_iota((L, L), 0) <= _iota((L, L), 1)).astype(F32)
        da = _dot_hi(triu, dacs)
        ddtraw = (da * A + _reduce_heads(qx[...], est)) * (1.0 / (1.0 + jnp.exp(-xb)))
        ddt_ref[...] = ddtraw
        dal = jnp.sum(da * dt, axis=0, keepdims=True) * A
        ddp = rsum[1:2, :]
        dbp = jnp.sum(ddtraw, axis=0, keepdims=True)

        @pl.when(cc == 0)
        def _():
            dbias_ref[...] = dbp
            dal_ref[...] = dal
            dd_ref[...] = ddp

        @pl.when(cc > 0)
        def _():
            dbias_ref[...] += dbp
            dal_ref[...] += dal
            dd_ref[...] += ddp

    rc = lambda c: nc - 1 - c
    vec = pl.BlockSpec((1, LANES), lambda c: (0, 0))
    es, est = _head_selectors()
    return pl.pallas_call(
        body, grid=(nc,),
        in_specs=[pl.BlockSpec((L, SSM_CONV_DIM), lambda c: (rc(c), 0)), pl.BlockSpec((L, LANES), lambda c: (rc(c), 0)), vec, vec,
                  pl.BlockSpec((1, SSM_D_INNER), lambda c: (0, 0)),
                  pl.BlockSpec((LANES, SSM_D_INNER), lambda c: (0, 0)), pl.BlockSpec((SSM_D_INNER, LANES), lambda c: (0, 0)),
                  pl.BlockSpec((None, SSM_GROUPS, GROUP_W, SSM_STATE), lambda c: (rc(c), 0, 0, 0)),
                  pl.BlockSpec((L, SSM_D_INNER), lambda c: (rc(c), 0))],
        out_specs=[pl.BlockSpec((L, SSM_CONV_DIM), lambda c: (rc(c), 0)),
                   pl.BlockSpec((L, LANES), lambda c: (rc(c), 0)), vec, vec, vec],
        out_shape=[_sds((T, SSM_CONV_DIM)), _sds((T, LANES)), _sds((1, LANES)), _sds((1, LANES)), _sds((1, LANES))],
        scratch_shapes=[pltpu.VMEM((SSM_GROUPS, GROUP_W, SSM_STATE), F32), pltpu.VMEM((L, SSM_D_INNER), F32),
                        pltpu.VMEM((L, SSM_D_INNER), F32)],
        compiler_params=_cp("arbitrary"), name=name)(xact, dtraw, dt_bias, a_log, d_skip, es, est, states, dy)


def _ssm_post_fwd(y, xact, z, d_skip, nw, name):
    T = y.shape[0]
    tm = min(T, 256)
    W = SSM_D_INNER

    def body(y_ref, x_ref, z_ref, d_ref, w_ref, o_ref):
        y2 = (y_ref[...] + d_ref[...] * x_ref[...]) * _silu(z_ref[...])
        r = lax.rsqrt(jnp.mean(y2 * y2, axis=-1, keepdims=True) + SSM_NORM_EPS)
        o_ref[...] = (y2 * r * w_ref[...]).astype(o_ref.dtype)

    row = pl.BlockSpec((tm, W), lambda i: (i, 0))
    vec = pl.BlockSpec((1, W), lambda i: (0, 0))
    return pl.pallas_call(
        body, grid=(T // tm,), in_specs=[row, row, row, vec, vec], out_specs=row, out_shape=_sds((T, W), MXU),
        compiler_params=_cp("parallel"), name=name)(y, xact, z, d_skip, nw)


def _ssm_post_bwd(y, xact, z, d_skip, nw, dyn, name):
    T = y.shape[0]
    tm = min(T, 256)
    W = SSM_D_INNER

    def body(y_ref, x_ref, z_ref, d_ref, w_ref, dn_ref, dyg_ref, dz_ref, dw_ref):
        zv = z_ref[...]
        sz = _silu(zv)
        yg = y_ref[...] + d_ref[...] * x_ref[...]
        y2 = yg * sz
        r = lax.rsqrt(jnp.mean(y2 * y2, axis=-1, keepdims=True) + SSM_NORM_EPS)
        y2h = y2 * r
        dn = dn_ref[...]
        gy = dn * w_ref[...]
        dy2 = r * (gy - y2h * jnp.mean(gy * y2h, axis=-1, keepdims=True))
        dyg_ref[...] = dy2 * sz
        dz_ref[...] = (dy2 * yg * _dsilu(zv)).astype(dz_ref.dtype)
        part = jnp.sum(dn * y2h, axis=0, keepdims=True)

        @pl.when(pl.program_id(0) == 0)
        def _():
            dw_ref[...] = part

        @pl.when(pl.program_id(0) > 0)
        def _():
            dw_ref[...] += part

    row = pl.BlockSpec((tm, W), lambda i: (i, 0))
    vec = pl.BlockSpec((1, W), lambda i: (0, 0))
    return pl.pallas_call(
        body, grid=(T // tm,), in_specs=[row, row, row, vec, vec, row], out_specs=[row, row, vec],
        out_shape=[_sds((T, W)), _sds((T, W), MXU), _sds((1, W))],
        compiler_params=_cp("arbitrary"), name=name)(y, xact, z, d_skip, nw, dyn)


def _local_step(x0, cos, sin_s, target, P, fetch, token, send):
    mmf = functools.partial(_mm, tm=1024)
    big, small = {}, {}
    P = dict(P, wup={}, wdn={}, fcw={})
    h0 = _rmsnorm_fwd(x0, P["nm"][0], "norm_mix0", token=token)
    proj0 = mmf(h0, P["wmiT"], tb=True, tn=1280, tk=1024, name="mix_in")
    cat, attn, lse = _mixcore_fwd(proj0, cos, sin_s, P["pool_w"], P["pool_scale"], P["sinks"], "mixcore_fwd")
    x1, hf0 = mmf(cat, P["wmo"], tn=1024, tk=1024, res=x0, norm_w=P["nf"][0], name="mix_out")

    def ffn_fwd(xin, hf, i, next_norm):
        got = fetch(f"ffn{i}", hf)
        P["wup"][i], P["wdn"][i], P["fcw"][i] = got["wup"], got["wdn"], got["fcw"]
        hid, hc, act = _ffn_up_conv_gate(hf, P["wup"][i], P["fcw"][i], P["fcb"][i], f"ffn_up{i}")
        xout = mmf(act, P["wdn"][i], tn=1024, tk=D_FF, res=xin, norm_w=next_norm, name=f"ffn_down{i}")
        return (hid, hc), act, xout

    hid0, act0, (x2, h1) = ffn_fwd(x1, hf0, 0, P["nm"][1])
    P.update(fetch("ssm", h1))
    z = mmf(h1, P["wzT"], tb=True, tn=1024, tk=1024, name="ssm_in_z")
    xbc, xpre, xact = _ssm_in_conv(h1, P["wxbcT"], P["scw"], P["scb"], "ssm_in_xbc")
    dtraw = mmf(h1, P["wdtT"], tb=True, tn=128, tk=1024, name="ssm_in_dt")
    y, states = _ssd_fwd(xact, dtraw, P["dt_bias"], P["a_log"], "ssd_fwd")
    yn = _ssm_post_fwd(y, xact, z, P["d_exp"], P["snorm"], "ssm_post_fwd")
    x3, hf1 = mmf(yn, P["wso"], tn=1024, tk=SSM_D_INNER, res=x2, norm_w=P["nf"][1], name="ssm_out")
    hid1, act1, x4 = ffn_fwd(x3, hf1, 1, None)
    loss_row, dx4, d_nfin = _loss_head(x4, P["nfin"], target, "loss_head")
    small["norm_final"] = d_nfin

    def ffn_bwd(xin, dxo, hf, hid, act, i):
        big[f"ffn_w_down{i}"] = dwf(act, dxo, tm=1408, tn=1024, name=f"ffn_down_dw{i}").reshape(N_CHIPS, D_FF // N_CHIPS, D_MODEL)
        dhid, dcw, dcb = _ffn_down_dx_mid_bwd(dxo, P["wdn"][i], hid[0], hid[1], P["fcw"][i], f"ffn_down_dx{i}")
        big[f"ffn_w_up{i}"] = dwf(hf, dhid, tm=1024, tn=1408, out_shard_perm=(0, 2, 1, 3), name=f"ffn_up_dw{i}")
        tok = send(f"ffn{i}", [big[f"ffn_w_up{i}"], big[f"ffn_w_down{i}"]])
        dxi, dnf = _mm(dhid, P["wup"][i], tb=True, tm=512, tn=1024, tk=2816, norm_bwd=(xin, P["nf"][i], dxo, tok), name=f"ffn_up_dx{i}")
        return dxi, dnf, dcw, dcb

    dwf = functools.partial(_mm, ta=True, tk=2048, out_dtype=BF16)
    dx3, dnf1, dfcw1, dfcb1 = ffn_bwd(x3, dx4, hf1, hid1, act1, 1)
    dyn = mmf(dx3, P["wso"], tb=True, tn=1024, tk=1024, name="ssm_out_dx")
    big["ssm_w_out"] = dwf(yn, dx3, tm=1024, tn=1024, name="ssm_out_dw").reshape(N_CHIPS, SSM_D_INNER // N_CHIPS, D_MODEL)
    dyg, dz, d_snorm = _ssm_post_bwd(y, xact, z, P["d_exp"], P["snorm"], dyn, "ssm_post_bwd")
    dxact_p, ddtraw, d_dtb, d_alog, d_dskip = _ssd_bwd(xact, dtraw, P["dt_bias"], P["a_log"], P["d_exp"], states, dyg, "ssd_bwd")
    dxbc, d_scw, d_scb = _ssm_pre_bwd(xbc, xpre, P["scw"], dxact_p, "ssm_pre_bwd")
    dwsi = dwf(dz, h1, tm=1024, tn=1024, out_into=(None, SSM_IN_DIM, 0), name="ssm_in_dw_z")
    dwsi = dwf(dxbc, h1, tm=1024, tn=1024, out_into=(dwsi, SSM_IN_DIM, SSM_D_INNER // 1024), name="ssm_in_dw_xbc")
    dwdt = dwf(ddtraw, h1, tm=128, tn=1024, name="ssm_in_dw_dt")
    dwsi = _put_rows(dwsi, dwdt, SSM_HEADS, SSM_D_INNER + SSM_CONV_DIM, "ssm_in_dw_put_dt")
    big["ssm_w_in"] = dwsi.reshape(N_CHIPS, SSM_IN_DIM // N_CHIPS, D_MODEL)
    tok = send("ssm", [big["ssm_w_in"], big["ssm_w_out"]])
    dh1 = mmf(dz, P["wzT"], tn=1024, tk=2048, name="ssm_in_dx_z")
    dh1 = mmf(dxbc, P["wxbcT"], tn=1024, tk=2048, res=dh1, name="ssm_in_dx_xbc")
    dx2, dnm1 = mmf(ddtraw, P["wdtT"], tn=1024, tk=128, res=dh1, norm_bwd=(x2, P["nm"][1], dx3, tok), name="ssm_in_dx_dt")
    dx1, dnf0, dfcw0, dfcb0 = ffn_bwd(x1, dx2, hf0, hid0, act0, 0)
    dcat = mmf(dx1, P["wmo"], tb=True, tn=1024, tk=1024, name="mix_out_dx")
    big["mix_w_out"] = dwf(cat, dx1, tm=1024, tn=1024, name="mix_out_dw").reshape(N_CHIPS, D_MODEL // N_CHIPS, D_MODEL)
    dproj0, d_pw, d_ps, d_sk = _mixcore_bwd(proj0, cos, sin_s, P["pool_w"], P["pool_scale"], P["sinks"], attn, lse, dcat, "mixcore_bwd")
    big["mix_w_in"] = dwf(dproj0, h0, tm=1280, tn=1024, name="mix_in_dw").reshape(N_CHIPS, MIX_IN_DIM // N_CHIPS, D_MODEL)
    tok = send("mix", [big["mix_w_in"], big["mix_w_out"]])
    dx0, dnm0 = mmf(dproj0, P["wmiT"], tn=1024, tk=1280, norm_bwd=(x0, P["nm"][0], dx1, tok), name="mix_in_dx")

    def unperm_cols(a):
        r = a.shape[0]
        t = a.reshape(r, N_CHIPS, FFN_TC)
        return jnp.stack([t[:, p] for p in _PERM], axis=0)

    small["norm_mix"] = jnp.concatenate([dnm0, dnm1], axis=0)
    small["norm_ffn"] = jnp.concatenate([dnf0, dnf1], axis=0)
    small["pool_w"] = d_pw.reshape(4 * POOL_GROUP, POOL_GROUP)
    small["pool_scale"] = d_ps
    small["attn_sinks"] = d_sk
    small["ssm_dt_bias"] = d_dtb
    small["ssm_A_log"] = d_alog
    small["ssm_D"] = d_dskip
    fcb = jnp.stack([unperm_cols(dfcb0), unperm_cols(dfcb1)], axis=0)
    small["ffn_conv_b"] = fcb.reshape(2, 2 * D_FF)
    small["ssm_conv_w"] = d_scw.reshape(SSM_CONV, N_CHIPS, SSM_CONV_DIM // N_CHIPS).transpose(1, 0, 2)
    small["ssm_conv_b"] = d_scb.reshape(N_CHIPS, 1, SSM_CONV_DIM // N_CHIPS)
    small["ssm_norm"] = d_snorm.reshape(N_CHIPS, 1, SSM_D_INNER // N_CHIPS)
    small["ffn_conv_w"] = jnp.concatenate([unperm_cols(dfcw0), unperm_cols(dfcw1)], axis=1)
    return loss_row, dx0, big, small


ANY = pl.BlockSpec(memory_space=pl.ANY)


def _place():
    return lax.axis_index("x"), lax.axis_index("y"), lax.axis_index("c")


def _gather_shards(shards, name):
    n = len(shards)
    split = [s.size >= (1 << 16) for s in shards]

    def half(ref, a, h):
        shp = shards[a].shape
        if len(shp) == 3:
            return ref.at[h]
        r2 = shp[0] // 2
        return ref.at[pl.ds(pl.multiple_of(h * r2, 2 * SUBLANES), r2), :]

    def body(*refs):
        ins, outs = refs[:n], refs[n:2 * n]
        send, recv, fsend, frecv = refs[2 * n:]
        x, y, c = _place()
        k = 2 * x + y
        chips = [(1 - x, y), (x, 1 - y), (1 - x, 1 - y)]

        def ici(a, j, src_slot_ref, dst_slot):
            px, py = chips[j]
            src = half(src_slot_ref, a, c) if split[a] else src_slot_ref
            dst = half(outs[a].at[dst_slot], a, c) if split[a] else outs[a].at[dst_slot]
            return pltpu.make_async_remote_copy(src, dst, send.at[a, j], recv.at[a, j], device_id=(px, py, c), device_id_type=MESH)

        def d2d(a, j, h):
            px, py = chips[j]
            part = half(outs[a].at[2 * px + py], a, h)
            return pltpu.make_async_remote_copy(part, part, fsend.at[a, j], frecv.at[a, j], device_id=(x, y, 1 - c), device_id_type=MESH)

        sends = [ici(a, j, ins[a], k) for a in range(n) for j in range(3)]
        for cp in sends:
            cp.start()
        passed = []
        for a in range(n):
            for j, (px, py) in enumerate(chips):
                ici(a, j, ins[a], 2 * px + py).wait_recv()
                if split[a]:
                    passed.append(d2d(a, j, c))
                    passed[-1].start()
        for a in range(n):
            if split[a]:
                for j in range(3):
                    d2d(a, j, 1 - c).wait_recv()
        for cp in sends + passed:
            cp.wait_send()

    return pl.pallas_call(
        body, in_specs=[ANY] * n, out_specs=[ANY] * n,
        out_shape=[_sds((N_CHIPS,) + s.shape, s.dtype) for s in shards],
        scratch_shapes=[pltpu.SemaphoreType.DMA((n, 3))] * 4,
        compiler_params=pltpu.CompilerParams(has_side_effects=True), name=name)(*shards)


HBM = pl.BlockSpec(memory_space=pltpu.HBM)
SEM = pl.BlockSpec(memory_space=pltpu.SEMAPHORE)
DATAFLOW = pltpu.SideEffectType.DATAFLOW_SIDE_EFFECTING


def _row_half(ref, h):
    r2 = ref.shape[0] // 2
    return ref.at[pl.ds(pl.multiple_of(h * r2, 2 * SUBLANES), r2), :]


def _spread_start(groups, slot_src, after, name, halved=()):
    flat = [a for grp in groups for a in grp]
    n = len(flat)
    ng = len(groups)
    offs = [sum(len(g) for g in groups[:i]) for i in range(ng)]
    lshape = [(a.shape if slot_src else (N_CHIPS,) + a.shape) for a in flat]

    nsem = 6 * n

    def body(*refs):
        src, land = refs[:n], refs[n:2 * n]
        sems = refs[2 * n + 1:2 * n + 1 + nsem]
        token = refs[-1]
        x, y, c = _place()
        k = 2 * x + y
        chips = [(1 - x, y), (x, 1 - y), (1 - x, 1 - y)]
        for a in range(n):
            half = any(offs[gi] <= a < offs[gi] + len(groups[gi]) for gi in halved)
            for j, (px, py) in enumerate(chips):
                s = src[a].at[2 * px + py] if slot_src else src[a]
                d = land[a].at[k]
                if half:
                    s, d = _row_half(s, c), _row_half(d, c)
                pltpu.make_async_remote_copy(s, d, sems[6 * a + 2 * j], sems[6 * a + 2 * j + 1],
                                             device_id=(px, py, c), device_id_type=MESH).start()
        token[...] = jnp.zeros(token.shape, token.dtype)

    out_shape = [pltpu.SemaphoreType.DMA(())] * nsem
    out_shape += [pltpu.HBM(a.shape, a.dtype) for a in flat] + [pltpu.HBM(s, a.dtype) for s, a in zip(lshape, flat)]
    out_shape.append(_sds((SUBLANES, LANES)))
    args = [pltpu.with_memory_space_constraint(a, pltpu.HBM) for a in flat]
    args += [pltpu.with_memory_space_constraint(lax.empty(s, a.dtype), pltpu.HBM) for s, a in zip(lshape, flat)]
    res = pl.pallas_call(
        body, name=name, out_shape=tuple(out_shape), in_specs=[HBM] * (2 * n) + [pl.BlockSpec(memory_space=pl.ANY)],
        out_specs=tuple([SEM] * nsem + [HBM] * (2 * n) + [pl.BlockSpec(memory_space=pltpu.VMEM)]),
        input_output_aliases={i: nsem + i for i in range(2 * n)},
        compiler_params=pltpu.CompilerParams(has_side_effects=DATAFLOW))(*args, after)
    sems, thru, token = res[:nsem], res[nsem:nsem + 2 * n], res[-1]
    out = []
    for gi, grp in enumerate(groups):
        sl = slice(offs[gi], offs[gi] + len(grp))
        out.append((list(sems[6 * offs[gi]:6 * (offs[gi] + len(grp))]), list(thru[:n][sl]), list(thru[n:][sl])))
    return out, token


def _spread_wait(started, slot_src, after, name, halved=False):
    sems, srcs, lands = started
    n = len(srcs)

    def body(*refs):
        src, land = refs[:n], refs[n:2 * n]
        sem = refs[2 * n:2 * n + 6 * n]
        x, y, c = _place()
        chips = [(1 - x, y), (x, 1 - y), (1 - x, 1 - y)]
        for a in range(n):
            for j, (px, py) in enumerate(chips):
                s = src[a].at[2 * px + py] if slot_src else src[a]
                d = land[a].at[2 * px + py]
                if halved:
                    s, d = _row_half(s, c), _row_half(d, c)
                cp = pltpu.make_async_remote_copy(s, d, sem[6 * a + 2 * j], sem[6 * a + 2 * j + 1],
                                                  device_id=(px, py, c), device_id_type=MESH)
                cp.wait_send()
                cp.wait_recv()

    res = pl.pallas_call(
        body, name=name, out_shape=tuple([pltpu.HBM(a.shape, a.dtype) for a in srcs] + [pltpu.HBM(a.shape, a.dtype) for a in lands]),
        in_specs=[HBM] * (2 * n) + [SEM] * (6 * n) + [pl.BlockSpec(memory_space=pl.ANY)], out_specs=tuple([HBM] * (2 * n)),
        input_output_aliases={i: i for i in range(2 * n)},
        compiler_params=pltpu.CompilerParams(has_side_effects=DATAFLOW))(*srcs, *lands, *sems, after)
    return list(res[:n]), list(res[n:])


def _sibling_fill(lands, name):
    n = len(lands)

    def body(*refs):
        bufs = refs[n:2 * n]
        send, recv = refs[2 * n:]
        x, y, c = _place()
        chips = [(1 - x, y), (x, 1 - y), (1 - x, 1 - y)]

        def copy(a, j, h):
            px, py = chips[j]
            part = _row_half(bufs[a].at[2 * px + py], h)
            return pltpu.make_async_remote_copy(part, part, send.at[a, j], recv.at[a, j], device_id=(x, y, 1 - c), device_id_type=MESH)

        sends = [copy(a, j, c) for a in range(n) for j in range(3)]
        for cp in sends:
            cp.start()
        for a in range(n):
            for j in range(3):
                copy(a, j, 1 - c).wait_recv()
        for cp in sends:
            cp.wait_send()

    return pl.pallas_call(
        body, in_specs=[ANY] * n, out_specs=[ANY] * n, out_shape=[_sds(t.shape, t.dtype) for t in lands],
        input_output_aliases={i: i for i in range(n)},
        scratch_shapes=[pltpu.SemaphoreType.DMA((n, 3)), pltpu.SemaphoreType.DMA((n, 3))],
        compiler_params=pltpu.CompilerParams(has_side_effects=True), name=name)(*lands)


def _sibling_exchange(fs, name):
    n = len(fs)

    def body(*refs):
        ins, outs = refs[:n], refs[n:2 * n]
        send, recv = refs[2 * n:]
        x, y, c = _place()
        cps = [pltpu.make_async_remote_copy(ins[a], outs[a], send.at[a], recv.at[a],
                                            device_id=(x, y, 1 - c), device_id_type=MESH) for a in range(n)]
        for cp in cps:
            cp.start()
        for cp in cps:
            cp.wait()

    return pl.pallas_call(
        body, in_specs=[ANY] * n, out_specs=[ANY] * n, out_shape=[_sds(f.shape, f.dtype) for f in fs],
        scratch_shapes=[pltpu.SemaphoreType.DMA((n,)), pltpu.SemaphoreType.DMA((n,))],
        compiler_params=pltpu.CompilerParams(has_side_effects=True), name=name)(*fs)


def _tile2d(rows, cols, budget=1024 * 1024, step=2 * SUBLANES):
    fits = [t for t in range(step, rows + 1, step) if rows % t == 0 and t * cols * 4 <= budget]
    if fits:
        return fits[-1], cols
    fits = [t for t in range(LANES, cols + 1, LANES) if cols % t == 0 and rows * t * 4 <= budget]
    assert fits, (rows, cols)
    return rows, fits[-1]


def _chip_sum(own, parts, kidx, name):
    _, R, C = parts.shape
    tr, tc = _tile2d(R, C)

    def body(k_ref, o_ref_in, p1_ref, p2_ref, p3_ref, o_ref):
        tot = ((o_ref_in[...].astype(F32) + p1_ref[...].astype(F32)) + p2_ref[...].astype(F32)) + p3_ref[...].astype(F32)
        o_ref[...] = tot.astype(o_ref.dtype)

    def slot(d):
        return pl.BlockSpec((None, tr, tc), lambda i, j, k: ((k[0] + d) % N_CHIPS, i, j))

    return pl.pallas_call(
        body,
        grid_spec=pltpu.PrefetchScalarGridSpec(
            num_scalar_prefetch=1, grid=(R // tr, C // tc), in_specs=[slot(0), slot(1), slot(2), slot(3)],
            out_specs=pl.BlockSpec((tr, tc), lambda i, j, k: (i, j))),
        out_shape=_sds((R, C), BF16), compiler_params=_cp("parallel", "parallel"), name=name)(kidx, own, parts, parts, parts)


def _adamw_math(w, g, m, v):
    m2 = ADAM_B1 * m + (1.0 - ADAM_B1) * g
    v2 = ADAM_B2 * v + (1.0 - ADAM_B2) * (g * g)
    m_hat = m2 / (1.0 - ADAM_B1 ** ADAM_STEP)
    v_hat = v2 / (1.0 - ADAM_B2 ** ADAM_STEP)
    delta = -ADAM_LR * (m_hat / (jnp.sqrt(v_hat) + ADAM_EPS) + ADAM_WD * w)
    return delta, m2, v2


def _adamw(w, m, v, gparts, name):
    Lw, R, C = w.shape
    tr, tc = _tile2d(R, C)
    flat = [h for pair in gparts for h in pair]

    def body(*refs):
        w_ref, m_ref, v_ref = refs[:3]
        g_refs = refs[3:3 + 2 * Lw]
        go_ref, d_ref, mo_ref, vo_ref = refs[3 + 2 * Lw:]
        g = g_refs[0][...].astype(F32) + g_refs[1][...].astype(F32)
        for l in range(1, Lw):
            g = jnp.where(pl.program_id(0) == l, g_refs[2 * l][...].astype(F32) + g_refs[2 * l + 1][...].astype(F32), g)
        d, m2, v2 = _adamw_math(w_ref[...], g, m_ref[...], v_ref[...])
        go_ref[...] = g
        d_ref[...] = d
        mo_ref[...] = m2
        vo_ref[...] = v2

    blk = pl.BlockSpec((None, tr, tc), lambda l, i, j: (l, i, j))
    gblk = pl.BlockSpec((tr, tc), lambda l, i, j: (i, j))
    return pl.pallas_call(
        body, grid=(Lw, R // tr, C // tc), in_specs=[blk, blk, blk] + [gblk] * (2 * Lw), out_specs=[blk] * 4,
        out_shape=[_sds((Lw, R, C))] * 4, compiler_params=_cp("parallel", "parallel", "parallel"), name=name)(w, m, v, *flat)


def _small_adamw(grads, wmv, name):
    n = len(grads)

    def body(*refs):
        g_in, p_in, outs = refs[:n], refs[n:4 * n], refs[4 * n:]
        for a in range(n):
            g = g_in[a][...]
            d_, m2, v2 = _adamw_math(p_in[3 * a][...], g, p_in[3 * a + 1][...], p_in[3 * a + 2][...])
            outs[4 * a][...] = g
            outs[4 * a + 1][...] = d_
            outs[4 * a + 2][...] = m2
            outs[4 * a + 3][...] = v2

    vm = pl.BlockSpec(memory_space=pltpu.VMEM)
    args = list(grads) + [t for tri in wmv for t in tri]
    out_shape = [_sds(g.shape) for g in grads for _ in range(4)]
    return pl.pallas_call(body, in_specs=[vm] * len(args), out_specs=[vm] * len(out_shape), out_shape=out_shape,
                          compiler_params=pltpu.CompilerParams(vmem_limit_bytes=V7X_VMEM_LIMIT), name=name)(*args)


def _small_allreduce(partials, pshapes, loss_row, name):
    n = len(partials)
    gshapes = [p.shape for p in partials] + [loss_row.shape]
    ng = n + 1

    def body(*refs):
        g_in = refs[:ng]
        outs = refs[ng:2 * ng]
        sib = refs[2 * ng:3 * ng]
        pair = refs[3 * ng:4 * ng]
        bufs = refs[4 * ng:5 * ng]
        send1, recv1, send2, recv2 = refs[-4:]
        x, y, c = _place()
        k = 2 * x + y
        chips = [(1 - x, y), (x, 1 - y), (1 - x, 1 - y)]
        swaps = [pltpu.make_async_remote_copy(g_in[a], sib[a], send1.at[a], recv1.at[a],
                                              device_id=(x, y, 1 - c), device_id_type=MESH) for a in range(ng)]
        for cp in swaps:
            cp.start()
        for a, cp in enumerate(swaps):
            cp.wait()
            pair[a][...] = g_in[a][...] + sib[a][...]
            bufs[a][k] = pair[a][...]
        sends = [pltpu.make_async_remote_copy(pair[a], bufs[a].at[k], send2.at[a, j], recv2.at[a, j],
                                              device_id=(px, py, c), device_id_type=MESH)
                 for a in range(ng) for j, (px, py) in enumerate(chips)]
        for cp in sends:
            cp.start()
        for a in range(ng):
            for j, (px, py) in enumerate(chips):
                pltpu.make_async_remote_copy(pair[a], bufs[a].at[2 * px + py], send2.at[a, j], recv2.at[a, j],
                                             device_id=(px, py, c), device_id_type=MESH).wait_recv()
        for cp in sends:
            cp.wait_send()
        for a in range(ng):
            sharded = len(gshapes[a]) == 3

            def part(d):
                return bufs[a][d, k] if sharded else bufs[a][d]

            tot = part(0)
            for d in range(1, N_CHIPS):
                tot = tot + part(d)
            if a == n:
                outs[n][...] = tot
            else:
                pr, pc = pshapes[a]
                outs[a][...] = tot[:pr, :pc]

    vm = pl.BlockSpec(memory_space=pltpu.VMEM)
    args = list(partials) + [loss_row]
    out_shape = [_sds(ps) for ps in pshapes] + [_sds(loss_row.shape)]
    return pl.pallas_call(
        body, in_specs=[vm] * len(args), out_specs=[vm] * len(out_shape), out_shape=out_shape,
        scratch_shapes=[pltpu.VMEM(tuple(s), F32) for s in gshapes] * 2 + [pltpu.VMEM((N_CHIPS,) + tuple(s), F32) for s in gshapes]
        + [pltpu.SemaphoreType.DMA((ng,)), pltpu.SemaphoreType.DMA((ng,)),
           pltpu.SemaphoreType.DMA((ng, 3)), pltpu.SemaphoreType.DMA((ng, 3))],
        compiler_params=pltpu.CompilerParams(has_side_effects=True, vmem_limit_bytes=V7X_VMEM_LIMIT), name=name)(*args)


_PERM = (0, 2, 1, 3)


def _cols_from_shards(g):
    return g.transpose(1, 0, 2).reshape(g.shape[1], N_CHIPS * g.shape[2])


def _rope_tables(positions):
    inv_freq = ROPE_THETA ** (-jnp.arange(0, HEAD_DIM, 2, dtype=F32) / HEAD_DIM)
    ang = positions.astype(F32).reshape(-1, 1) * inv_freq
    cos, sin = jnp.cos(ang), jnp.sin(ang)
    cos = jnp.concatenate([cos, cos, cos, cos], axis=-1)
    sin_s = jnp.concatenate([-sin, sin, -sin, sin], axis=-1)
    return cos, sin_s


def kernel(x, positions, norm_mix, norm_ffn, norm_final, mix_w_in, pool_w, pool_scale, attn_sinks, mix_w_out, ssm_w_in, ssm_conv_w, ssm_conv_b, ssm_dt_bias, ssm_A_log, ssm_D, ssm_norm, ssm_w_out, ffn_w_up, ffn_conv_w, ffn_conv_b, ffn_w_down, loss_target, m_norm_mix, m_norm_ffn, m_norm_final, m_mix_w_in, m_pool_w, m_pool_scale, m_attn_sinks, m_mix_w_out, m_ssm_w_in, m_ssm_conv_w, m_ssm_conv_b, m_ssm_dt_bias, m_ssm_A_log, m_ssm_D, m_ssm_norm, m_ssm_w_out, m_ffn_w_up, m_ffn_conv_w, m_ffn_conv_b, m_ffn_w_down, v_norm_mix, v_norm_ffn, v_norm_final, v_mix_w_in, v_pool_w, v_pool_scale, v_attn_sinks, v_mix_w_out, v_ssm_w_in, v_ssm_conv_w, v_ssm_conv_b, v_ssm_dt_bias, v_ssm_A_log, v_ssm_D, v_ssm_norm, v_ssm_w_out, v_ffn_w_up, v_ffn_conv_w, v_ffn_conv_b, v_ffn_w_down):
    W = dict(norm_mix=norm_mix, norm_ffn=norm_ffn, norm_final=norm_final, mix_w_in=mix_w_in, pool_w=pool_w, pool_scale=pool_scale, attn_sinks=attn_sinks, mix_w_out=mix_w_out, ssm_w_in=ssm_w_in, ssm_conv_w=ssm_conv_w, ssm_conv_b=ssm_conv_b, ssm_dt_bias=ssm_dt_bias, ssm_A_log=ssm_A_log, ssm_D=ssm_D, ssm_norm=ssm_norm, ssm_w_out=ssm_w_out, ffn_w_up=ffn_w_up, ffn_conv_w=ffn_conv_w, ffn_conv_b=ffn_conv_b, ffn_w_down=ffn_w_down)
    Mo = dict(norm_mix=m_norm_mix, norm_ffn=m_norm_ffn, norm_final=m_norm_final, mix_w_in=m_mix_w_in, pool_w=m_pool_w, pool_scale=m_pool_scale, attn_sinks=m_attn_sinks, mix_w_out=m_mix_w_out, ssm_w_in=m_ssm_w_in, ssm_conv_w=m_ssm_conv_w, ssm_conv_b=m_ssm_conv_b, ssm_dt_bias=m_ssm_dt_bias, ssm_A_log=m_ssm_A_log, ssm_D=m_ssm_D, ssm_norm=m_ssm_norm, ssm_w_out=m_ssm_w_out, ffn_w_up=m_ffn_w_up, ffn_conv_w=m_ffn_conv_w, ffn_conv_b=m_ffn_conv_b, ffn_w_down=m_ffn_w_down)
    Vo = dict(norm_mix=v_norm_mix, norm_ffn=v_norm_ffn, norm_final=v_norm_final, mix_w_in=v_mix_w_in, pool_w=v_pool_w, pool_scale=v_pool_scale, attn_sinks=v_attn_sinks, mix_w_out=v_mix_w_out, ssm_w_in=v_ssm_w_in, ssm_conv_w=v_ssm_conv_w, ssm_conv_b=v_ssm_conv_b, ssm_dt_bias=v_ssm_dt_bias, ssm_A_log=v_ssm_A_log, ssm_D=v_ssm_D, ssm_norm=v_ssm_norm, ssm_w_out=v_ssm_w_out, ffn_w_up=v_ffn_w_up, ffn_conv_w=v_ffn_conv_w, ffn_conv_b=v_ffn_conv_b, ffn_w_down=v_ffn_w_down)

    kchip = 2 * lax.axis_index("x") + lax.axis_index("y")

    def own_slot(g, own):
        return lax.dynamic_update_slice_in_dim(g, own[None], kchip, axis=0)

    def tr(t):
        return jnp.swapaxes(t[0], 0, 1)

    later = dict(ffn0=[ffn_w_up[0].astype(MXU), ffn_w_down[0].astype(MXU)],
                 ssm=[tr(ssm_w_in).astype(MXU), ssm_w_out[0].astype(MXU)],
                 ffn1=[ffn_w_up[1].astype(MXU), ffn_w_down[1].astype(MXU)])
    sh = [tr(mix_w_in).astype(MXU), mix_w_out[0].astype(MXU), ssm_conv_w[0], ssm_conv_b, ssm_norm, ffn_conv_w]
    first = _gather_shards(sh, "gather_first")
    g_mi, g_mo, g_scw, g_scb, g_sn, g_fcw = [own_slot(g, own) for g, own in zip(first, sh)]
    started, token = _spread_start(list(later.values()), False, first[0], "gather_start", halved=(0,))
    started = dict(zip(later.keys(), started))
    fcw = [jnp.concatenate([g_fcw[p, i] for p in _PERM], axis=1) for i in range(2)]
    P = dict(
        nm=norm_mix, nf=norm_ffn, nfin=norm_final,
        wmiT=g_mi.reshape(MIX_IN_DIM, D_MODEL), wmo=g_mo.reshape(D_MODEL, D_MODEL),
        pool_w=pool_w[0], pool_scale=pool_scale, sinks=attn_sinks[0],
        scw=_cols_from_shards(g_scw), scb=g_scb.reshape(1, SSM_CONV_DIM), snorm=g_sn.reshape(1, SSM_D_INNER),
        dt_bias=jnp.pad(ssm_dt_bias, ((0, 0), (0, LANES - SSM_HEADS))), a_log=jnp.pad(ssm_A_log, ((0, 0), (0, LANES - SSM_HEADS))),
        d_exp=jnp.repeat(ssm_D, SSM_D_INNER // SSM_HEADS, axis=1),
        fcb=[jnp.concatenate([ffn_conv_b[i:i + 1, p * FFN_TC:(p + 1) * FFN_TC] for p in _PERM], axis=1) for i in range(2)],
    )

    def fetch(group, after):
        owns, lands = _spread_wait(started[group], False, after, f"gather_wait_{group}", halved=group == "ffn0")
        if group == "ffn0":
            lands = _sibling_fill(lands, "gather_fill_ffn0")
        a, b = [own_slot(g, own) for g, own in zip(lands, owns)]
        if group == "ssm":
            wsi = a.reshape(SSM_IN_DIM, D_MODEL)
            zx = SSM_D_INNER + SSM_CONV_DIM
            return dict(wzT=wsi[:SSM_D_INNER], wxbcT=wsi[SSM_D_INNER:zx],
                        wdtT=jnp.pad(wsi[zx:], ((0, LANES - SSM_HEADS), (0, 0))), wso=b.reshape(SSM_D_INNER, D_MODEL))
        i = int(group[-1])
        return dict(wup=jnp.concatenate([a[p] for p in _PERM], axis=1), wdn=b.reshape(D_FF, D_MODEL), fcw=fcw[i])

    cos, sin_s = _rope_tables(positions)
    sent = {}

    def send(group, grads):
        res, tok = _spread_start([grads], True, jnp.zeros((SUBLANES, LANES), F32), f"grad_start_{group}")
        sent[group] = res[0]
        return tok

    loss_row, grad_x, big, small = _local_step(x[0], cos, sin_s, loss_target[0], P, fetch, token, send)

    kidx = kchip.astype(jnp.int32).reshape(1)
    group_names = dict(ffn1=["ffn_w_up1", "ffn_w_down1"], ssm=["ssm_w_in", "ssm_w_out"], ffn0=["ffn_w_up0", "ffn_w_down0"],
                       mix=["mix_w_in", "mix_w_out"])
    names, mine = [], []
    for group, started_g in sent.items():
        grads, lands = _spread_wait(started_g, True, grad_x, f"grad_wait_{group}")
        for nm, g, land in zip(group_names[group], grads, lands):
            names.append(nm)
            mine.append(_chip_sum(g, land, kidx, f"chip_sum_{nm}"))
    theirs = _sibling_exchange(mine, "sibling_exchange")
    red = {nm: (a, b) for nm, a, b in zip(names, mine, theirs)}

    out = {}

    def big_update(pname, gparts, transposed=False):
        w = W[pname]
        lw = len(gparts)
        shp = w.shape
        rr, cc = gparts[0][0].shape
        fix = (lambda t: tr(t)[None]) if transposed else (lambda t: t.reshape(lw, rr, cc))
        res = _adamw(fix(w), fix(Mo[pname]), fix(Vo[pname]), gparts, f"adamw_{pname}")
        out[pname] = tuple((tr(r)[None] if transposed else r.reshape(shp)) for r in res)

    big_update("mix_w_in", [red["mix_w_in"]], transposed=True)
    big_update("mix_w_out", [red["mix_w_out"]])
    big_update("ssm_w_in", [red["ssm_w_in"]], transposed=True)
    big_update("ssm_w_out", [red["ssm_w_out"]])
    big_update("ffn_w_up", [red["ffn_w_up0"], red["ffn_w_up1"]])
    big_update("ffn_w_down", [red["ffn_w_down0"], red["ffn_w_down1"]])

    small_names = ["norm_mix", "norm_ffn", "norm_final", "pool_w", "pool_scale", "attn_sinks", "ssm_dt_bias", "ssm_A_log",
                   "ssm_D", "ffn_conv_b", "ssm_conv_w", "ssm_conv_b", "ssm_norm", "ffn_conv_w"]

    def as2d(t):
        if t.ndim == 1:
            return t.reshape(1, -1)
        return t.reshape(-1, t.shape[-1])

    wmv = [(as2d(W[nm]), as2d(Mo[nm]), as2d(Vo[nm])) for nm in small_names]
    summed = _small_allreduce([small[nm] for nm in small_names], [t[0].shape for t in wmv], loss_row, "small_allreduce")
    res = _small_adamw(summed[:-1], wmv, "small_adamw")
    for a, nm in enumerate(small_names):
        out[nm] = tuple(r.reshape(W[nm].shape) for r in res[4 * a:4 * a + 4])
    loss = summed[-1][0, 0]

    order = ["norm_mix", "norm_ffn", "norm_final", "mix_w_in", "pool_w", "pool_scale", "attn_sinks", "mix_w_out", "ssm_w_in",
             "ssm_conv_w", "ssm_conv_b", "ssm_dt_bias", "ssm_A_log", "ssm_D", "ssm_norm", "ssm_w_out", "ffn_w_up", "ffn_conv_w",
             "ffn_conv_b", "ffn_w_down"]
    return (loss, grad_x.reshape(x.shape), *[out[nm][0] for nm in order], *[out[nm][1] for nm in order],
            *[out[nm][2] for nm in order], *[out[nm][3] for nm in order])
```

```python
import functools

import jax
import jax.numpy as jnp
from jax import lax
from jax.experimental import pallas as pl
from jax.experimental.pallas import tpu as pltpu

F32 = jnp.float32
BF16 = jnp.bfloat16
MXU = BF16
HI = lax.Precision.HIGHEST

D_MODEL = 1024
POOL_WINDOWS = (2, 4, 8, 16)
POOL_DIM = 512
POOL_GROUP = 128
HEAD_DIM = 64
N_HEADS = 8
N_KV_HEADS = 2
GQ = 4
Q_DIM = 512
KV_DIM = 128
BLOCK = 128
ROPE_THETA = 10000.0
MIX_IN_DIM = 1280
SSM_D_INNER = 2048
SSM_HEADS = 32
SSM_GROUPS = 8
SSM_STATE = 128
SSM_CONV = 4
SSM_CHUNK = 128
SSM_CONV_DIM = 4096
SSM_IN_DIM = 6176
D_FF = 2816
FFN_CONV = 3
NORM_EPS = 1e-6
SSM_NORM_EPS = 1e-5
ADAM_LR = 0.001
ADAM_B1 = 0.9
ADAM_B2 = 0.999
ADAM_EPS = 1e-08
ADAM_WD = 0.01
ADAM_STEP = 10

N_CHIPS = 4
N_DEV = 8
LANES = 128
SUBLANES = 8
V7X_VMEM_LIMIT = 56 * 1024 * 1024
NEG = -1e30
MESH = pl.DeviceIdType.MESH


def _cp(*sem):
    return pltpu.CompilerParams(dimension_semantics=sem if sem else None, vmem_limit_bytes=V7X_VMEM_LIMIT)


def _sds(shape, dtype=F32):
    return jax.ShapeDtypeStruct(tuple(shape), dtype)


def _iota(shape, dim):
    return lax.broadcasted_iota(jnp.int32, shape, dim)


def _silu(x):
    return x * (1.0 / (1.0 + jnp.exp(-x)))


def _dsilu(x):
    s = 1.0 / (1.0 + jnp.exp(-x))
    return s * (1.0 + x * (1.0 - s))


def _mm(a, b, *, ta=False, tb=False, tm, tn, tk, res=None, out_dtype=F32, out_shard_perm=None, out_into=None, norm_w=None,
        norm_bwd=None, name):
    M, K = (a.shape[1], a.shape[0]) if ta else a.shape
    N = b.shape[0] if tb else b.shape[1]
    tm, tn, tk = min(tm, M), min(tn, N), min(tk, K)
    gm, gn, gk = M // tm, N // tn, K // tk
    assert gm * tm == M and gn * tn == N and gk * tk == K, (name, M, N, K, tm, tn, tk)
    a_spec = pl.BlockSpec((tk, tm), lambda i, j, k: (k, i)) if ta else pl.BlockSpec((tm, tk), lambda i, j, k: (i, k))
    b_spec = pl.BlockSpec((tn, tk), lambda i, j, k: (j, k)) if tb else pl.BlockSpec((tk, tn), lambda i, j, k: (k, j))
    dims = (((0 if ta else 1,), (1 if tb else 0,)), ((), ()))
    has_res = res is not None
    has_nw = norm_w is not None
    has_nb = norm_bwd is not None
    has_tok = has_nb and norm_bwd[3] is not None
    assert not (has_nw or has_nb) or (gn == 1 and out_shard_perm is None)
    n_extra = has_res + has_nw + (3 + has_tok if has_nb else 0)

    def body(*refs):
        a_ref, b_ref = refs[0], refs[1]
        extra = list(refs[2:2 + n_extra])
        outs = refs[len(args):]
        r_ref = extra.pop(0) if has_res else None
        nw_ref = extra.pop(0) if has_nw else None
        nb_refs = extra if has_nb else None

        def dot():
            return lax.dot_general(a_ref[...].astype(MXU), b_ref[...].astype(MXU), dims, preferred_element_type=F32)

        def finish(r):
            if has_res:
                r = r + r_ref[...]
            if has_nb:
                xv = nb_refs[0][...]
                rs = lax.rsqrt(jnp.mean(xv * xv, axis=-1, keepdims=True) + NORM_EPS)
                xh = xv * rs
                g = r * nb_refs[1][...]
                dr = nb_refs[2][...] + nb_refs[3][0:1, 0:1] if has_tok else nb_refs[2][...]
                outs[0][...] = dr + rs * (g - xh * jnp.mean(g * xh, axis=-1, keepdims=True))
                part = jnp.sum(r * xh, axis=0, keepdims=True)
                i = pl.program_id(0)

                @pl.when(i == 0)
                def _():
                    outs[1][...] = part

                @pl.when(i > 0)
                def _():
                    outs[1][...] += part
                return
            outs[0][...] = r.astype(out_dtype)
            if has_nw:
                rs = lax.rsqrt(jnp.mean(r * r, axis=-1, keepdims=True) + NORM_EPS)
                outs[1][...] = (r * rs * nw_ref[...]).astype(outs[1].dtype)

        if gk == 1:
            finish(dot())
        else:
            acc = refs[-1]
            k = pl.program_id(2)

            @pl.when(k == 0)
            def _():
                acc[...] = dot()

            if gk > 2:
                @pl.when(jnp.logical_and(k > 0, k < gk - 1))
                def _():
                    acc[...] += dot()

            @pl.when(k == gk - 1)
            def _():
                finish(acc[...] + dot())

    tile = pl.BlockSpec((tm, tn), lambda i, j, k: (i, j))
    row = pl.BlockSpec((1, tn), lambda i, j, k: (0, j))
    in_specs = [a_spec, b_spec]
    args = [a, b]
    if has_res:
        in_specs.append(tile)
        args.append(res)
    if has_nw:
        in_specs.append(row)
        args.append(norm_w.reshape(1, N))
    if has_nb:
        in_specs += [tile, row, tile]
        args += [norm_bwd[0], norm_bwd[1].reshape(1, N), norm_bwd[2]]
        if has_tok:
            in_specs.append(pl.BlockSpec((SUBLANES, LANES), lambda i, j, k: (0, 0)))
            args.append(norm_bwd[3])
    alias = {}
    if out_into is not None:
        buf, rows, off = out_into
        out_spec = pl.BlockSpec((tm, tn), lambda i, j, k: (i + off, j))
        out_shape = _sds((rows, N), out_dtype)
        if buf is not None:
            alias = {len(args): 0}
            in_specs.append(pl.BlockSpec(memory_space=pl.ANY))
            args.append(buf)
    elif out_shard_perm is None:
        out_spec = tile
        out_shape = _sds((M, N), out_dtype)
    else:
        assert gn == len(out_shard_perm) == 4 and tuple(out_shard_perm) == (0, 2, 1, 3)
        out_spec = pl.BlockSpec((None, tm, tn), lambda i, j, k: ((j % 2) * 2 + j // 2, i, 0))
        out_shape = _sds((gn, M, tn), out_dtype)
    sem = ("parallel", "parallel", "arbitrary")
    if has_nw:
        out_spec, out_shape = [out_spec, tile], [out_shape, _sds((M, N), MXU)]
    if has_nb:
        out_spec, out_shape = [tile, row], [_sds((M, N)), _sds((1, N))]
        sem = ("arbitrary", "arbitrary", "arbitrary")
    return pl.pallas_call(
        body, grid=(gm, gn, gk), in_specs=in_specs, out_specs=out_spec, out_shape=out_shape,
        scratch_shapes=[pltpu.VMEM((tm, tn), F32)] if gk > 1 else [], input_output_aliases=alias,
        compiler_params=_cp(*sem), name=name)(*args)


def _put_rows(buf, src, rows, at, name):
    assert at % rows == 0 and src.shape[1] == buf.shape[1] and src.dtype == buf.dtype
    C = buf.shape[1]

    def body(s_ref, b_ref, o_ref):
        o_ref[...] = s_ref[...]

    return pl.pallas_call(
        body, grid=(1,), in_specs=[pl.BlockSpec((rows, C), lambda i: (0, 0)), pl.BlockSpec(memory_space=pl.ANY)],
        out_specs=pl.BlockSpec((rows, C), lambda i: (at // rows, 0)), out_shape=_sds(buf.shape, buf.dtype),
        input_output_aliases={1: 0}, compiler_params=_cp("arbitrary"), name=name)(src, buf)


def _rmsnorm_fwd(x, w, name, token=None):
    T, D = x.shape
    tm = min(T, 512)
    has_token = token is not None

    def body(*refs):
        x_ref, w_ref, o_ref = refs[0], refs[1], refs[-1]
        xv = x_ref[...]
        if has_token:
            xv = xv + refs[2][0:1, 0:1]
        r = lax.rsqrt(jnp.mean(xv * xv, axis=-1, keepdims=True) + NORM_EPS)
        o_ref[...] = (xv * r * w_ref[...]).astype(o_ref.dtype)

    in_specs = [pl.BlockSpec((tm, D), lambda i: (i, 0)), pl.BlockSpec((1, D), lambda i: (0, 0))]
    args = [x, w.reshape(1, D)]
    if has_token:
        in_specs.append(pl.BlockSpec((SUBLANES, LANES), lambda i: (0, 0)))
        args.append(token)
    return pl.pallas_call(
        body, grid=(T // tm,), in_specs=in_specs,
        out_specs=pl.BlockSpec((tm, D), lambda i: (i, 0)), out_shape=_sds((T, D), MXU),
        compiler_params=_cp("parallel"), name=name)(*args)


def _loss_head(x, w, target, name):
    T, D = x.shape
    tm = min(T, 512)

    def body(x_ref, w_ref, t_ref, loss_ref, dx_ref, dw_ref):
        xv = x_ref[...]
        r = lax.rsqrt(jnp.mean(xv * xv, axis=-1, keepdims=True) + NORM_EPS)
        xh = xv * r
        wv = w_ref[...]
        e = xh * wv - t_ref[...]
        lpart = 0.5 * jnp.sum(jnp.mean(e * e, axis=-1, keepdims=True), axis=0, keepdims=True)
        dy = e * (1.0 / D)
        g = dy * wv
        dx_ref[...] = r * (g - xh * jnp.mean(g * xh, axis=-1, keepdims=True))
        part = jnp.sum(dy * xh, axis=0, keepdims=True)
        lrow = jnp.broadcast_to(lpart, (1, LANES))

        @pl.when(pl.program_id(0) == 0)
        def _():
            dw_ref[...] = part
            loss_ref[...] = lrow

        @pl.when(pl.program_id(0) > 0)
        def _():
            dw_ref[...] += part
            loss_ref[...] += lrow

    row = pl.BlockSpec((tm, D), lambda i: (i, 0))
    vec = pl.BlockSpec((1, D), lambda i: (0, 0))
    return pl.pallas_call(
        body, grid=(T // tm,), in_specs=[row, vec, row],
        out_specs=[pl.BlockSpec((1, LANES), lambda i: (0, 0)), row, vec],
        out_shape=[_sds((1, LANES)), _sds((T, D)), _sds((1, D))],
        compiler_params=_cp("arbitrary"), name=name)(x, w.reshape(1, D), target)


def _shift_down(cur, prev8, s):
    if s == 0:
        return cur
    tm = cur.shape[0]
    rc = pltpu.roll(cur, s, 0)
    top = jnp.where(_iota((SUBLANES, cur.shape[1]), 0) < s, pltpu.roll(prev8, s, 0), rc[:SUBLANES])
    return jnp.concatenate([top, rc[SUBLANES:]], axis=0) if tm > SUBLANES else top


def _shift_up(cur, next8, s):
    if s == 0:
        return cur
    tm = cur.shape[0]
    rc = pltpu.roll(cur, tm - s, 0)
    bot = jnp.where(_iota((SUBLANES, cur.shape[1]), 0) >= SUBLANES - s, pltpu.roll(next8, SUBLANES - s, 0), rc[tm - SUBLANES:])
    return jnp.concatenate([rc[:tm - SUBLANES], bot], axis=0) if tm > SUBLANES else bot


def _conv_rows(cur, prev8, w, b, K):
    acc = cur * w[K - 1:K, :] + b
    for s in range(1, K):
        acc = acc + _shift_down(cur, prev8, s) * w[K - 1 - s:K - s, :]
    return acc


FFN_TC = 1408
HALO16 = 2 * SUBLANES


def _ffn_up_conv_gate(hf, wup, cw, cb, name):
    T, D = hf.shape
    tm = min(T, 256)
    nt, nj = T // tm, D_FF // FFN_TC
    K = FFN_CONV
    W2 = 2 * FFN_TC

    def body(a_ref, b_ref, w_ref, c_ref, hid_ref, hc_ref, act_ref, halo):
        i = pl.program_id(1)

        @pl.when(i == 0)
        def _():
            halo[...] = jnp.zeros(halo.shape, F32)

        hb = jnp.dot(a_ref[...].astype(MXU), b_ref[...].astype(MXU), preferred_element_type=F32).astype(hid_ref.dtype)
        hid_ref[...] = hb
        cur = hb.astype(F32)
        hc = _conv_rows(cur, halo[...], w_ref[...], c_ref[...], K)
        halo[...] = cur[tm - SUBLANES:]
        hc_ref[...] = hc
        act_ref[...] = (_silu(hc[:, FFN_TC:]) * hc[:, :FFN_TC]).astype(act_ref.dtype)

    blk = pl.BlockSpec((tm, W2), lambda j, i: (i, j))
    return pl.pallas_call(
        body, grid=(nj, nt),
        in_specs=[pl.BlockSpec((tm, D), lambda j, i: (i, 0)), pl.BlockSpec((D, W2), lambda j, i: (0, j)),
                  pl.BlockSpec((K, W2), lambda j, i: (0, j)), pl.BlockSpec((1, W2), lambda j, i: (0, j))],
        out_specs=[blk, blk, pl.BlockSpec((tm, FFN_TC), lambda j, i: (i, j))],
        out_shape=[_sds((T, 2 * D_FF), MXU), _sds((T, 2 * D_FF)), _sds((T, D_FF), MXU)],
        scratch_shapes=[pltpu.VMEM((SUBLANES, W2), F32)],
        compiler_params=_cp("arbitrary", "arbitrary"), name=name)(hf, wup, cw, cb)


def _ffn_down_dx_mid_bwd(dxo, wdn, hid, hc, cw, name):
    T, D = dxo.shape
    tm = min(T, 256)
    nt, nj = T // tm, D_FF // FFN_TC
    K = FFN_CONV
    W2 = 2 * FFN_TC

    def body(g_ref, wd_ref, h_ref, c_ref, w_ref, dh_ref, dw_ref, db_ref, ahead):
        i = pl.program_id(1)

        @pl.when(i == 0)
        def _():
            ahead[...] = jnp.zeros(ahead.shape, F32)

        w = w_ref[...]
        cur = h_ref[...].astype(F32)
        hcv = c_ref[...]
        dav = _nt(g_ref[...], wd_ref[...])
        u, g = hcv[:, :FFN_TC], hcv[:, FFN_TC:]
        d_cur = jnp.concatenate([dav * _silu(g), dav * u * _dsilu(g)], axis=1)
        d_nxt = ahead[...]
        ahead[...] = d_cur[:SUBLANES]
        ups = [d_cur] + [_shift_up(d_cur, d_nxt, s) for s in range(1, K)]
        dh = ups[0] * w[K - 1:K, :]
        for s in range(1, K):
            dh = dh + ups[s] * w[K - 1 - s:K - s, :]
        dh_ref[...] = dh.astype(dh_ref.dtype)
        dwp = jnp.concatenate([jnp.sum(ups[K - 1 - k] * cur, axis=0, keepdims=True) for k in range(K)], axis=0)
        dbp = jnp.sum(d_cur, axis=0, keepdims=True)

        @pl.when(i == 0)
        def _():
            dw_ref[...] = dwp
            db_ref[...] = dbp

        @pl.when(i > 0)
        def _():
            dw_ref[...] += dwp
            db_ref[...] += dbp

    blk = pl.BlockSpec((tm, W2), lambda j, i: (nt - 1 - i, j))
    return pl.pallas_call(
        body, grid=(nj, nt),
        in_specs=[pl.BlockSpec((tm, D), lambda j, i: (nt - 1 - i, 0)), pl.BlockSpec((FFN_TC, D), lambda j, i: (j, 0)), blk, blk,
                  pl.BlockSpec((K, W2), lambda j, i: (0, j))],
        out_specs=[blk, pl.BlockSpec((K, W2), lambda j, i: (0, j)), pl.BlockSpec((1, W2), lambda j, i: (0, j))],
        out_shape=[_sds((T, 2 * D_FF), MXU), _sds((K, 2 * D_FF)), _sds((1, 2 * D_FF))],
        scratch_shapes=[pltpu.VMEM((SUBLANES, W2), F32)],
        compiler_params=_cp("arbitrary", "arbitrary"), name=name)(dxo, wdn, hid, hc, cw)


def _rope(t, cos, sin_s, inverse=False):
    n = t.shape[1] // LANES
    c = jnp.concatenate([cos] * n, axis=1) if n > 1 else cos
    s = jnp.concatenate([sin_s] * n, axis=1) if n > 1 else sin_s
    a = pltpu.roll(t, HEAD_DIM // 2, 1)
    b = pltpu.roll(t, t.shape[1] - HEAD_DIM // 2, 1)
    first = (_iota(t.shape, 1) % HEAD_DIM) < HEAD_DIM // 2
    rot = jnp.where(first, b, a) * s
    return t * c - rot if inverse else t * c + rot


def _stack_heads(t, g):
    return jnp.concatenate([t[:, (GQ * g + r) * HEAD_DIM:(GQ * g + r + 1) * HEAD_DIM] for r in range(GQ)], axis=0)


def _stack_cols(t, g):
    return jnp.concatenate([t[:, GQ * g + r:GQ * g + r + 1] for r in range(GQ)], axis=0)


def _pool_sums(prev, cur, w):
    s = jnp.concatenate([prev, cur], axis=0)
    sh = 1
    while sh < w:
        s = s + pltpu.roll(s, sh, 0)
        sh *= 2
    return s[BLOCK:]


def _nt(a, b):
    return lax.dot_general(a.astype(MXU), b.astype(MXU), (((1,), (1,)), ((), ())), preferred_element_type=F32)


def _tn(a, b):
    return lax.dot_general(a.astype(MXU), b.astype(MXU), (((0,), (0,)), ((), ())), preferred_element_type=F32)


def _nn(a, b):
    return jnp.dot(a.astype(MXU), b.astype(MXU), preferred_element_type=F32)


def _mixcore_fwd(proj, cos, sin_s, pool_w, pool_scale, sinks, name):
    T = proj.shape[0]
    nb = T // BLOCK
    scale = HEAD_DIM ** -0.5

    def body(p_ref, pp_ref, c_ref, s_ref, cp_ref, sp_ref, pw_ref, ps_ref, sk_ref, cat_ref, at_ref, lse_ref):
        i = pl.program_id(0)
        has_prev = i > 0
        cur = p_ref[...]
        prv = jnp.where(has_prev, pp_ref[...], 0.0)
        tpos = (i * BLOCK + _iota((BLOCK, 1), 0) + 1).astype(F32)
        for g, w in enumerate(POOL_WINDOWS):
            sl = slice(g * POOL_GROUP, (g + 1) * POOL_GROUP)
            pooled = _pool_sums(prv[:, sl], cur[:, sl], w) / jnp.minimum(tpos, float(w)) - cur[:, sl]
            cat_ref[:, sl] = (_nn(pooled, pw_ref[g]) * ps_ref[:, sl]).astype(cat_ref.dtype)
        q = _rope(cur[:, POOL_DIM:POOL_DIM + Q_DIM], c_ref[...], s_ref[...])
        kc = _rope(cur[:, POOL_DIM + Q_DIM:POOL_DIM + Q_DIM + KV_DIM], c_ref[...], s_ref[...])
        kp = _rope(prv[:, POOL_DIM + Q_DIM:POOL_DIM + Q_DIM + KV_DIM], cp_ref[...], sp_ref[...])
        vc = cur[:, POOL_DIM + Q_DIM + KV_DIM:]
        vp = prv[:, POOL_DIM + Q_DIM + KV_DIM:]
        ri = _iota((GQ * BLOCK, BLOCK), 0) % BLOCK
        cj = _iota((GQ * BLOCK, BLOCK), 1)
        mc = cj <= ri
        mp = jnp.logical_and(cj > ri, has_prev)
        outs, lses = [], []
        for g in range(N_KV_HEADS):
            hs = slice(g * HEAD_DIM, (g + 1) * HEAD_DIM)
            qg = _stack_heads(q, g) * scale
            sc = jnp.where(mc, _nt(qg, kc[:, hs]), NEG)
            sp = jnp.where(mp, _nt(qg, kp[:, hs]), NEG)
            sink = jnp.concatenate([jnp.full((BLOCK, 1), sk_ref[GQ * g + r], F32) for r in range(GQ)], axis=0)
            m = jnp.maximum(jnp.maximum(jnp.max(sc, axis=1, keepdims=True), jnp.max(sp, axis=1, keepdims=True)), sink)
            pc = jnp.exp(sc - m)
            pp = jnp.exp(sp - m)
            den = jnp.sum(pc, axis=1, keepdims=True) + jnp.sum(pp, axis=1, keepdims=True) + jnp.exp(sink - m)
            o = (_nn(pc, vc[:, hs]) + _nn(pp, vp[:, hs])) / den
            lse = m + jnp.log(den)
            for r in range(GQ):
                outs.append(o[r * BLOCK:(r + 1) * BLOCK])
                lses.append(lse[r * BLOCK:(r + 1) * BLOCK])
        attn = jnp.concatenate(outs, axis=1)
        at_ref[...] = attn
        cat_ref[:, POOL_DIM:] = attn.astype(cat_ref.dtype)
        lane = _iota((BLOCK, LANES), 1)
        lrow = jnp.zeros((BLOCK, LANES), F32)
        for h in range(N_HEADS):
            lrow = jnp.where(lane == h, lses[h], lrow)
        lse_ref[...] = lrow

    cur = lambda w: pl.BlockSpec((BLOCK, w), lambda i: (i, 0))
    prv = lambda w: pl.BlockSpec((BLOCK, w), lambda i: (jnp.maximum(i - 1, 0), 0))
    return pl.pallas_call(
        body, grid=(nb,),
        in_specs=[cur(MIX_IN_DIM), prv(MIX_IN_DIM), cur(LANES), cur(LANES), prv(LANES), prv(LANES),
                  pl.BlockSpec((4, POOL_GROUP, POOL_GROUP), lambda i: (0, 0, 0)), pl.BlockSpec((1, POOL_DIM), lambda i: (0, 0)),
                  pl.BlockSpec(memory_space=pltpu.SMEM)],
        out_specs=[cur(2 * POOL_DIM), cur(Q_DIM), cur(LANES)],
        out_shape=[_sds((T, 2 * POOL_DIM), MXU), _sds((T, Q_DIM)), _sds((T, LANES))],
        compiler_params=_cp("parallel"), name=name)(proj, proj, cos, sin_s, cos, sin_s, pool_w, pool_scale, sinks)


def _mixcore_bwd(proj, cos, sin_s, pool_w, pool_scale, sinks, attn, lse, dcat, name):
    T = proj.shape[0]
    nb = T // BLOCK
    scale = HEAD_DIM ** -0.5
    QO, KO, VO = POOL_DIM, POOL_DIM + Q_DIM, POOL_DIM + Q_DIM + KV_DIM

    def body(p_ref, pp_ref, pn_ref, c_ref, s_ref, cp_ref, sp_ref, cn_ref, sn_ref, pw_ref, ps_ref, sk_ref,
             at_ref, atn_ref, l_ref, ln_ref, d_ref, dn_ref, dp_ref, dpw_ref, dps_ref, dsk_ref):
        i = pl.program_id(0)
        has_prev = i > 0
        has_next = i < nb - 1
        cur = p_ref[...]
        prv = jnp.where(has_prev, pp_ref[...], 0.0)
        d_cur = d_ref[...]
        d_nxt = jnp.where(has_next, dn_ref[...], 0.0)

        tpos = (i * BLOCK + _iota((BLOCK, 1), 0) + 1).astype(F32)
        tpos2 = (i * BLOCK + _iota((2 * BLOCK, 1), 0) + 1).astype(F32)
        ps = ps_ref[...]
        dps_parts, dpw_parts = [], []
        for g, w in enumerate(POOL_WINDOWS):
            sl = slice(g * POOL_GROUP, (g + 1) * POOL_GROUP)
            pooled = _pool_sums(prv[:, sl], cur[:, sl], w) / jnp.minimum(tpos, float(w)) - cur[:, sl]
            mixed = _nn(pooled, pw_ref[g])
            dps_parts.append(jnp.sum(d_cur[:, sl] * mixed, axis=0, keepdims=True))
            dm2 = jnp.concatenate([d_cur[:, sl], d_nxt[:, sl]], axis=0) * ps[:, sl]
            dpw_parts.append(_tn(pooled, dm2[:BLOCK]))
            dpool2 = _nt(dm2, pw_ref[g])
            e = dpool2 / jnp.minimum(tpos2, float(w))
            sh = 1
            while sh < w:
                e = e + pltpu.roll(e, 2 * BLOCK - sh, 0)
                sh *= 2
            dp_ref[:, sl] = (e[:BLOCK] - dpool2[:BLOCK]).astype(dp_ref.dtype)
        dpsp = jnp.concatenate(dps_parts, axis=1)

        nxt = pn_ref[...]
        q = _rope(cur[:, QO:KO], c_ref[...], s_ref[...])
        qn = _rope(nxt[:, QO:KO], cn_ref[...], sn_ref[...])
        kc = _rope(cur[:, KO:VO], c_ref[...], s_ref[...])
        kp = _rope(prv[:, KO:VO], cp_ref[...], sp_ref[...])
        vc, vp = cur[:, VO:], prv[:, VO:]
        do, don = d_cur[:, POOL_DIM:], d_nxt[:, POOL_DIM:]
        dl = do * at_ref[...]
        dln = don * atn_ref[...]
        lse, lsen = l_ref[...], ln_ref[...]
        ri = _iota((GQ * BLOCK, BLOCK), 0) % BLOCK
        cj = _iota((GQ * BLOCK, BLOCK), 1)
        mc = cj <= ri
        mp = jnp.logical_and(cj > ri, has_prev)
        mn = jnp.logical_and(cj > ri, has_next)
        dq_parts, dk_parts, dv_parts, dsk_vals = [], [], [], []
        for g in range(N_KV_HEADS):
            hs = slice(g * HEAD_DIM, (g + 1) * HEAD_DIM)
            qg, qng = _stack_heads(q, g) * scale, _stack_heads(qn, g) * scale
            dog, dong = _stack_heads(do, g), _stack_heads(don, g)
            delta = jnp.sum(_stack_heads(dl, g), axis=1, keepdims=True)
            deltan = jnp.sum(_stack_heads(dln, g), axis=1, keepdims=True)
            lg, lng = _stack_cols(lse, g), _stack_cols(lsen, g)
            pc = jnp.where(mc, jnp.exp(_nt(qg, kc[:, hs]) - lg), 0.0)
            pp = jnp.where(mp, jnp.exp(_nt(qg, kp[:, hs]) - lg), 0.0)
            pn = jnp.where(mn, jnp.exp(_nt(qng, kc[:, hs]) - lng), 0.0)
            dsc = pc * (_nt(dog, vc[:, hs]) - delta)
            dsp = pp * (_nt(dog, vp[:, hs]) - delta)
            dsn = pn * (_nt(dong, vc[:, hs]) - deltan)
            dqg = (_nn(dsc, kc[:, hs]) + _nn(dsp, kp[:, hs])) * scale
            dq_parts += [dqg[r * BLOCK:(r + 1) * BLOCK] for r in range(GQ)]
            dk_parts.append(_tn(dsc, qg) + _tn(dsn, qng))
            dv_parts.append(_tn(pc, dog) + _tn(pn, dong))
            sink = jnp.concatenate([jnp.full((BLOCK, 1), sk_ref[GQ * g + r], F32) for r in range(GQ)], axis=0)
            dsk = -jnp.exp(sink - lg) * delta
            dsk_vals += [jnp.sum(dsk[r * BLOCK:(r + 1) * BLOCK], axis=0, keepdims=True) for r in range(GQ)]
        dq = _rope(jnp.concatenate(dq_parts, axis=1), c_ref[...], s_ref[...], inverse=True)
        dk = _rope(jnp.concatenate(dk_parts, axis=1), c_ref[...], s_ref[...], inverse=True)
        dp_ref[:, QO:KO] = dq.astype(dp_ref.dtype)
        dp_ref[:, KO:VO] = dk.astype(dp_ref.dtype)
        dp_ref[:, VO:] = jnp.concatenate(dv_parts, axis=1).astype(dp_ref.dtype)
        lane = _iota((1, LANES), 1)
        dskp = jnp.zeros((1, LANES), F32)
        for h in range(N_HEADS):
            dskp = jnp.where(lane == h, dsk_vals[h], dskp)

        @pl.when(i == 0)
        def _():
            dps_ref[...] = dpsp
            dsk_ref[...] = dskp
            for g in range(4):
                dpw_ref[g] = dpw_parts[g]

        @pl.when(i > 0)
        def _():
            dps_ref[...] += dpsp
            dsk_ref[...] += dskp
            for g in range(4):
                dpw_ref[g] += dpw_parts[g]

    cur = lambda w: pl.BlockSpec((BLOCK, w), lambda i: (i, 0))
    prv = lambda w: pl.BlockSpec((BLOCK, w), lambda i: (jnp.maximum(i - 1, 0), 0))
    nxt = lambda w: pl.BlockSpec((BLOCK, w), lambda i: (jnp.minimum(i + 1, nb - 1), 0))
    return pl.pallas_call(
        body, grid=(nb,),
        in_specs=[cur(MIX_IN_DIM), prv(MIX_IN_DIM), nxt(MIX_IN_DIM),
                  cur(LANES), cur(LANES), prv(LANES), prv(LANES), nxt(LANES), nxt(LANES),
                  pl.BlockSpec((4, POOL_GROUP, POOL_GROUP), lambda i: (0, 0, 0)), pl.BlockSpec((1, POOL_DIM), lambda i: (0, 0)),
                  pl.BlockSpec(memory_space=pltpu.SMEM),
                  cur(Q_DIM), nxt(Q_DIM), cur(LANES), nxt(LANES), cur(2 * POOL_DIM), nxt(2 * POOL_DIM)],
        out_specs=[cur(MIX_IN_DIM), pl.BlockSpec((4, POOL_GROUP, POOL_GROUP), lambda i: (0, 0, 0)),
                   pl.BlockSpec((1, POOL_DIM), lambda i: (0, 0)), pl.BlockSpec((1, LANES), lambda i: (0, 0))],
        out_shape=[_sds((T, MIX_IN_DIM), MXU), _sds((4, POOL_GROUP, POOL_GROUP)), _sds((1, POOL_DIM)), _sds((1, LANES))],
        compiler_params=_cp("arbitrary"), name=name)(
            proj, proj, proj, cos, sin_s, cos, sin_s, cos, sin_s, pool_w, pool_scale, sinks, attn, attn, lse, lse, dcat, dcat)


GROUP_W = SSM_D_INNER // SSM_GROUPS


def _ssm_in_conv(h, wT, cw, cb, name):
    T, D = h.shape
    tm = min(T, 256)
    tc = 1024
    K = SSM_CONV

    def body(a_ref, b_ref, w_ref, c_ref, x_ref, pre_ref, act_ref, halo):
        @pl.when(pl.program_id(1) == 0)
        def _():
            halo[...] = jnp.zeros(halo.shape, F32)

        cur = _nt(a_ref[...], b_ref[...])
        x_ref[...] = cur
        pre = _conv_rows(cur, halo[...], w_ref[...], c_ref[...], K)
        halo[...] = cur[tm - SUBLANES:]
        pre_ref[...] = pre
        act_ref[...] = _silu(pre)

    blk = pl.BlockSpec((tm, tc), lambda j, i: (i, j))
    return pl.pallas_call(
        body, grid=(SSM_CONV_DIM // tc, T // tm),
        in_specs=[pl.BlockSpec((tm, D), lambda j, i: (i, 0)), pl.BlockSpec((tc, D), lambda j, i: (j, 0)),
                  pl.BlockSpec((K, tc), lambda j, i: (0, j)), pl.BlockSpec((1, tc), lambda j, i: (0, j))],
        out_specs=[blk, blk, blk], out_shape=[_sds((T, SSM_CONV_DIM))] * 3,
        scratch_shapes=[pltpu.VMEM((SUBLANES, tc), F32)],
        compiler_params=_cp("arbitrary", "arbitrary"), name=name)(h, wT, cw, cb)


def _dot_hi(a, b):
    return jnp.dot(a, b, precision=HI, preferred_element_type=F32)


def _ssd_common(dtraw, bias, alog):
    L = SSM_CHUNK
    xb = dtraw + bias
    dt = jnp.maximum(xb, 0.0) + jnp.log1p(jnp.exp(-jnp.abs(xb)))
    A = -jnp.exp(alog)
    tril = (_iota((L, L), 1) <= _iota((L, L), 0)).astype(F32)
    acs = _dot_hi(tril, dt * A)
    return xb, dt, A, tril, acs


def _head_selectors():
    es = (_iota((LANES, SSM_D_INNER), 0) == _iota((LANES, SSM_D_INNER), 1) // HEAD_DIM).astype(BF16)
    est = (_iota((SSM_D_INNER, LANES), 1) == _iota((SSM_D_INNER, LANES), 0) // HEAD_DIM).astype(BF16)
    return es, est


def _dot_sel(v, sel):
    hi = v.astype(BF16)
    r1 = v - hi.astype(F32)
    mid = r1.astype(BF16)
    lo = (r1 - mid.astype(F32)).astype(BF16)
    d = lambda a: jnp.dot(a, sel, preferred_element_type=F32)
    return (d(hi) + d(mid)) + d(lo)


def _expand_heads(v, es):
    return _dot_sel(v, es)


def _reduce_heads(q, est):
    return _dot_sel(q, est)


def _per_state_row(v, g):
    return jnp.concatenate([jnp.broadcast_to(v[:, GQ * g + r:GQ * g + r + 1], (HEAD_DIM, 1)) for r in range(GQ)], axis=0)


def _ssd_fwd(xact, dtraw, dt_bias, a_log, name):
    T = xact.shape[0]
    nc = T // SSM_CHUNK
    L = SSM_CHUNK
    BO, CO = SSM_D_INNER, SSM_D_INNER + SSM_GROUPS * SSM_STATE

    def body(x_ref, dt_ref, bias_ref, al_ref, es_ref, y_ref, st_ref, state):
        @pl.when(pl.program_id(0) == 0)
        def _():
            state[...] = jnp.zeros(state.shape, F32)

        _, dt, A, tril, acs = _ssd_common(dt_ref[...], bias_ref[...], al_ref[...])
        acsT = acs.T
        last = acs[L - 1:L, :]
        cd = jnp.exp(last)
        es = es_ref[...]
        dtX = _expand_heads(dt, es)
        EX = _expand_heads(jnp.exp(acs), es)
        decX = _expand_heads(jnp.exp(last - acs), es)
        for g in range(SSM_GROUPS):
            gs = slice(g * GROUP_W, (g + 1) * GROUP_W)
            B = x_ref[:, BO + g * SSM_STATE:BO + (g + 1) * SSM_STATE]
            C = x_ref[:, CO + g * SSM_STATE:CO + (g + 1) * SSM_STATE]
            X = x_ref[:, gs] * dtX[:, gs]
            CB = _nt(C, B)
            yd = []
            for r in range(GQ):
                h = GQ * g + r
                Lm = jnp.exp(jnp.where(tril > 0, acs[:, h:h + 1] - acsT[h:h + 1, :], NEG))
                yd.append(_nn(CB * Lm, X[:, r * HEAD_DIM:(r + 1) * HEAD_DIM]))
            S = state[g]
            st_ref[g] = S
            y_ref[:, gs] = jnp.concatenate(yd, axis=1) + _nt(C, S) * EX[:, gs]
            state[g] = S * _per_state_row(cd, g) + _tn(X * decX[:, gs], B)

    es, _ = _head_selectors()
    return pl.pallas_call(
        body, grid=(nc,),
        in_specs=[pl.BlockSpec((L, SSM_CONV_DIM), lambda c: (c, 0)), pl.BlockSpec((L, LANES), lambda c: (c, 0)),
                  pl.BlockSpec((1, LANES), lambda c: (0, 0)), pl.BlockSpec((1, LANES), lambda c: (0, 0)),
                  pl.BlockSpec((LANES, SSM_D_INNER), lambda c: (0, 0))],
        out_specs=[pl.BlockSpec((L, SSM_D_INNER), lambda c: (c, 0)),
                   pl.BlockSpec((None, SSM_GROUPS, GROUP_W, SSM_STATE), lambda c: (c, 0, 0, 0))],
        out_shape=[_sds((T, SSM_D_INNER)), _sds((nc, SSM_GROUPS, GROUP_W, SSM_STATE))],
        scratch_shapes=[pltpu.VMEM((SSM_GROUPS, GROUP_W, SSM_STATE), F32)],
        compiler_params=_cp("arbitrary"), name=name)(xact, dtraw, dt_bias, a_log, es)


def _ssd_bwd(xact, xbc, xpre, cw, dtraw, dt_bias, a_log, d_skip, states, dy, name):
    T = xact.shape[0]
    nc = T // SSM_CHUNK
    L = SSM_CHUNK
    K = SSM_CONV
    BO, CO = SSM_D_INNER, SSM_D_INNER + SSM_GROUPS * SSM_STATE

    def body(x_ref, xin_ref, pre_ref, cw_ref, dt_ref, bias_ref, al_ref, dsk_ref, es_ref, est_ref, st_ref, dy_ref,
             dxbc_ref, dcw_ref, dcb_ref, ddt_ref, dbias_ref, dal_ref, dd_ref, dstate, qa, qx, dxp_ref, ahead):
        cc = pl.program_id(0)

        @pl.when(cc == 0)
        def _():
            dstate[...] = jnp.zeros(dstate.shape, F32)
            ahead[...] = jnp.zeros(ahead.shape, F32)

        xb, dt, A, tril, acs = _ssd_common(dt_ref[...], bias_ref[...], al_ref[...])
        acsT = acs.T
        last = acs[L - 1:L, :]
        cd = jnp.exp(last)
        es, est = es_ref[...], est_ref[...]
        dtX = _expand_heads(dt, es)
        EX = _expand_heads(jnp.exp(acs), es)
        decX = _expand_heads(jnp.exp(last - acs), es)
        lane1 = _iota((1, LANES), 1)
        lane = _iota((L, LANES), 1)
        sub = _iota((L, LANES), 0)
        ztot = jnp.zeros((1, LANES), F32)
        wrow = jnp.zeros((L, LANES), F32)
        wcolT = jnp.zeros((LANES, L), F32)
        rows_dec, rows_dd = [], []
        for g in range(SSM_GROUPS):
            gs = slice(g * GROUP_W, (g + 1) * GROUP_W)
            x = x_ref[:, gs]
            B = x_ref[:, BO + g * SSM_STATE:BO + (g + 1) * SSM_STATE]
            C = x_ref[:, CO + g * SSM_STATE:CO + (g + 1) * SSM_STATE]
            dY = dy_ref[:, gs]
            dtx, e_x, dec_x = dtX[:, gs], EX[:, gs], decX[:, gs]
            X = x * dtx
            CB = _nt(C, B)
            S = st_ref[g]
            dS_out = dstate[g]
            dcb_sum = jnp.zeros((L, L), F32)
            dxd = []
            for r in range(GQ):
                h = GQ * g + r
                hs = slice(r * HEAD_DIM, (r + 1) * HEAD_DIM)
                Lm = jnp.exp(jnp.where(tril > 0, acs[:, h:h + 1] - acsT[h:h + 1, :], NEG))
                M = CB * Lm
                dM = _nt(dY[:, hs], X[:, hs])
                dxd.append(_tn(M, dY[:, hs]))
                dcb_sum = dcb_sum + dM * Lm
                Wm = dM * M
                wrow = jnp.where(lane == h, jnp.sum(Wm, axis=1, keepdims=True), wrow)
                wcolT = jnp.where(sub == h, jnp.sum(Wm, axis=0, keepdims=True), wcolT)
            dXd = jnp.concatenate(dxd, axis=1)
            G = _nt(C, S)
            dG = dY * e_x
            dDX = _nt(B, dS_out)
            dX = dXd + dec_x * dDX
            t_dec = dDX * X * dec_x
            qa[:, gs] = dG * G - t_dec
            qx[:, gs] = dX * x
            rows_dec.append(jnp.sum(t_dec, axis=0, keepdims=True))
            rows_dd.append(jnp.sum(dY * x, axis=0, keepdims=True))
            zc = jnp.sum(dS_out * S, axis=1, keepdims=True)
            for r in range(GQ):
                ztot = jnp.where(lane1 == GQ * g + r, jnp.sum(zc[r * HEAD_DIM:(r + 1) * HEAD_DIM], axis=0, keepdims=True), ztot)
            dxp_ref[:, gs] = dX * dtx + dY * dsk_ref[:, gs]
            dxp_ref[:, BO + g * SSM_STATE:BO + (g + 1) * SSM_STATE] = _tn(dcb_sum, C) + _nn(X * dec_x, dS_out)
            dxp_ref[:, CO + g * SSM_STATE:CO + (g + 1) * SSM_STATE] = _nn(dcb_sum, B) + _nn(dG, S)
            dstate[g] = dS_out * _per_state_row(cd, g) + _tn(dG, C)
        rows = jnp.concatenate([jnp.concatenate(rows_dec, axis=1), jnp.concatenate(rows_dd, axis=1)]
                               + [jnp.zeros((SUBLANES - 2, SSM_D_INNER), F32)], axis=0)
        rsum = _reduce_heads(rows, est)
        dlast = rsum[0:1, :] + cd * ztot
        dacs = (wrow - wcolT.T) + _reduce_heads(qa[...], est) + jnp.where(sub == L - 1, dlast, 0.0)
        triu = (_iota((L, L), 0) <= _iota((L, L), 1)).astype(F32)
        da = _dot_hi(triu, dacs)
        ddtraw = (da * A + _reduce_heads(qx[...], est)) * (1.0 / (1.0 + jnp.exp(-xb)))
        ddt_ref[...] = ddtraw
        dal = jnp.sum(da * dt, axis=0, keepdims=True) * A
        ddp = rsum[1:2, :]
        dbp = jnp.sum(ddtraw, axis=0, keepdims=True)
        w = cw_ref[...]
        d_cur = dxp_ref[...] * _dsilu(pre_ref[...])
        d_nxt = ahead[...]
        ahead[...] = d_cur[:SUBLANES]
        ups = [d_cur] + [_shift_up(d_cur, d_nxt, s) for s in range(1, K)]
        dxc = ups[0] * w[K - 1:K, :]
        for s in range(1, K):
            dxc = dxc + ups[s] * w[K - 1 - s:K - s, :]
        dxbc_ref[...] = dxc.astype(dxbc_ref.dtype)
        xin = xin_ref[...]
        dcwp = jnp.concatenate([jnp.sum(ups[K - 1 - k] * xin, axis=0, keepdims=True) for k in range(K)], axis=0)
        dcbp = jnp.sum(d_cur, axis=0, keepdims=True)

        @pl.when(cc == 0)
        def _():
            dbias_ref[...] = dbp
            dal_ref[...] = dal
            dd_ref[...] = ddp
            dcw_ref[...] = dcwp
            dcb_ref[...] = dcbp

        @pl.when(cc > 0)
        def _():
            dbias_ref[...] += dbp
            dal_ref[...] += dal
            dd_ref[...] += ddp
            dcw_ref[...] += dcwp
            dcb_ref[...] += dcbp

    rc = lambda c: nc - 1 - c
    vec = pl.BlockSpec((1, LANES), lambda c: (0, 0))
    wide = pl.BlockSpec((L, SSM_CONV_DIM), lambda c: (rc(c), 0))
    es, est = _head_selectors()
    return pl.pallas_call(
        body, grid=(nc,),
        in_specs=[wide, wide, wide, pl.BlockSpec((K, SSM_CONV_DIM), lambda c: (0, 0)),
                  pl.BlockSpec((L, LANES), lambda c: (rc(c), 0)), vec, vec,
                  pl.BlockSpec((1, SSM_D_INNER), lambda c: (0, 0)),
                  pl.BlockSpec((LANES, SSM_D_INNER), lambda c: (0, 0)), pl.BlockSpec((SSM_D_INNER, LANES), lambda c: (0, 0)),
                  pl.BlockSpec((None, SSM_GROUPS, GROUP_W, SSM_STATE), lambda c: (rc(c), 0, 0, 0)),
                  pl.BlockSpec((L, SSM_D_INNER), lambda c: (rc(c), 0))],
        out_specs=[wide, pl.BlockSpec((K, SSM_CONV_DIM), lambda c: (0, 0)), pl.BlockSpec((1, SSM_CONV_DIM), lambda c: (0, 0)),
                   pl.BlockSpec((L, LANES), lambda c: (rc(c), 0)), vec, vec, vec],
        out_shape=[_sds((T, SSM_CONV_DIM), MXU), _sds((K, SSM_CONV_DIM)), _sds((1, SSM_CONV_DIM)),
                   _sds((T, LANES)), _sds((1, LANES)), _sds((1, LANES)), _sds((1, LANES))],
        scratch_shapes=[pltpu.VMEM((SSM_GROUPS, GROUP_W, SSM_STATE), F32), pltpu.VMEM((L, SSM_D_INNER), F32),
                        pltpu.VMEM((L, SSM_D_INNER), F32), pltpu.VMEM((L, SSM_CONV_DIM), F32),
                        pltpu.VMEM((SUBLANES, SSM_CONV_DIM), F32)],
        compiler_params=_cp("arbitrary"), name=name)(xact, xbc, xpre, cw, dtraw, dt_bias, a_log, d_skip, es, est, states, dy)


def _ssm_post_fwd(y, xact, z, d_skip, nw, name):
    T = y.shape[0]
    tm = min(T, 256)
    W = SSM_D_INNER

    def body(y_ref, x_ref, z_ref, d_ref, w_ref, o_ref):
        y2 = (y_ref[...] + d_ref[...] * x_ref[...]) * _silu(z_ref[...])
        r = lax.rsqrt(jnp.mean(y2 * y2, axis=-1, keepdims=True) + SSM_NORM_EPS)
        o_ref[...] = (y2 * r * w_ref[...]).astype(o_ref.dtype)

    row = pl.BlockSpec((tm, W), lambda i: (i, 0))
    vec = pl.BlockSpec((1, W), lambda i: (0, 0))
    return pl.pallas_call(
        body, grid=(T // tm,), in_specs=[row, row, row, vec, vec], out_specs=row, out_shape=_sds((T, W), MXU),
        compiler_params=_cp("parallel"), name=name)(y, xact, z, d_skip, nw)


def _ssm_post_bwd(y, xact, z, d_skip, nw, dyn, name):
    T = y.shape[0]
    tm = min(T, 256)
    W = SSM_D_INNER

    def body(y_ref, x_ref, z_ref, d_ref, w_ref, dn_ref, dyg_ref, dz_ref, dw_ref):
        zv = z_ref[...]
        sz = _silu(zv)
        yg = y_ref[...] + d_ref[...] * x_ref[...]
        y2 = yg * sz
        r = lax.rsqrt(jnp.mean(y2 * y2, axis=-1, keepdims=True) + SSM_NORM_EPS)
        y2h = y2 * r
        dn = dn_ref[...]
        gy = dn * w_ref[...]
        dy2 = r * (gy - y2h * jnp.mean(gy * y2h, axis=-1, keepdims=True))
        dyg_ref[...] = dy2 * sz
        dz_ref[...] = (dy2 * yg * _dsilu(zv)).astype(dz_ref.dtype)
        part = jnp.sum(dn * y2h, axis=0, keepdims=True)

        @pl.when(pl.program_id(0) == 0)
        def _():
            dw_ref[...] = part

        @pl.when(pl.program_id(0) > 0)
        def _():
            dw_ref[...] += part

    row = pl.BlockSpec((tm, W), lambda i: (i, 0))
    vec = pl.BlockSpec((1, W), lambda i: (0, 0))
    return pl.pallas_call(
        body, grid=(T // tm,), in_specs=[row, row, row, vec, vec, row], out_specs=[row, row, vec],
        out_shape=[_sds((T, W)), _sds((T, W), MXU), _sds((1, W))],
        compiler_params=_cp("arbitrary"), name=name)(y, xact, z, d_skip, nw, dyn)


def _local_step(x0, cos, sin_s, target, P, fetch, token, send):
    mmf = functools.partial(_mm, tm=1024)
    big, small = {}, {}
    P = dict(P, wup={}, wdn={}, fcw={})
    h0 = _rmsnorm_fwd(x0, P["nm"][0], "norm_mix0", token=token)
    proj0 = mmf(h0, P["wmiT"], tb=True, tn=1280, tk=1024, name="mix_in")
    cat, attn, lse = _mixcore_fwd(proj0, cos, sin_s, P["pool_w"], P["pool_scale"], P["sinks"], "mixcore_fwd")
    x1, hf0 = mmf(cat, P["wmo"], tn=1024, tk=1024, res=x0, norm_w=P["nf"][0], name="mix_out")

    def ffn_fwd(xin, hf, i, next_norm):
        got = fetch(f"ffn{i}", hf)
        P["wup"][i], P["wdn"][i], P["fcw"][i] = got["wup"], got["wdn"], got["fcw"]
        hid, hc, act = _ffn_up_conv_gate(hf, P["wup"][i], P["fcw"][i], P["fcb"][i], f"ffn_up{i}")
        xout = mmf(act, P["wdn"][i], tn=1024, tk=D_FF, res=xin, norm_w=next_norm, name=f"ffn_down{i}")
        return (hid, hc), act, xout

    hid0, act0, (x2, h1) = ffn_fwd(x1, hf0, 0, P["nm"][1])
    P.update(fetch("ssm", h1))
    z = mmf(h1, P["wzT"], tb=True, tn=1024, tk=1024, name="ssm_in_z")
    xbc, xpre, xact = _ssm_in_conv(h1, P["wxbcT"], P["scw"], P["scb"], "ssm_in_xbc")
    dtraw = mmf(h1, P["wdtT"], tb=True, tn=128, tk=1024, name="ssm_in_dt")
    y, states = _ssd_fwd(xact, dtraw, P["dt_bias"], P["a_log"], "ssd_fwd")
    yn = _ssm_post_fwd(y, xact, z, P["d_exp"], P["snorm"], "ssm_post_fwd")
    x3, hf1 = mmf(yn, P["wso"], tn=1024, tk=SSM_D_INNER, res=x2, norm_w=P["nf"][1], name="ssm_out")
    hid1, act1, x4 = ffn_fwd(x3, hf1, 1, None)
    loss_row, dx4, d_nfin = _loss_head(x4, P["nfin"], target, "loss_head")
    small["norm_final"] = d_nfin

    def ffn_bwd(xin, dxo, hf, hid, act, i):
        big[f"ffn_w_down{i}"] = dwf(act, dxo, tm=1408, tn=1024, name=f"ffn_down_dw{i}").reshape(N_CHIPS, D_FF // N_CHIPS, D_MODEL)
        dhid, dcw, dcb = _ffn_down_dx_mid_bwd(dxo, P["wdn"][i], hid[0], hid[1], P["fcw"][i], f"ffn_down_dx{i}")
        big[f"ffn_w_up{i}"] = dwf(hf, dhid, tm=1024, tn=1408, out_shard_perm=(0, 2, 1, 3), name=f"ffn_up_dw{i}")
        tok = send(f"ffn{i}", [big[f"ffn_w_up{i}"], big[f"ffn_w_down{i}"]])
        dxi, dnf = _mm(dhid, P["wup"][i], tb=True, tm=512, tn=1024, tk=2816, norm_bwd=(xin, P["nf"][i], dxo, tok), name=f"ffn_up_dx{i}")
        return dxi, dnf, dcw, dcb

    dwf = functools.partial(_mm, ta=True, tk=2048, out_dtype=BF16)
    dx3, dnf1, dfcw1, dfcb1 = ffn_bwd(x3, dx4, hf1, hid1, act1, 1)
    dyn = mmf(dx3, P["wso"], tb=True, tn=1024, tk=1024, name="ssm_out_dx")
    big["ssm_w_out"] = dwf(yn, dx3, tm=1024, tn=1024, name="ssm_out_dw").reshape(N_CHIPS, SSM_D_INNER // N_CHIPS, D_MODEL)
    dyg, dz, d_snorm = _ssm_post_bwd(y, xact, z, P["d_exp"], P["snorm"], dyn, "ssm_post_bwd")
    dxbc, d_scw, d_scb, ddtraw, d_dtb, d_alog, d_dskip = _ssd_bwd(
        xact, xbc, xpre, P["scw"], dtraw, P["dt_bias"], P["a_log"], P["d_exp"], states, dyg, "ssd_bwd")
    dwsi = dwf(dz, h1, tm=1024, tn=1024, out_into=(None, SSM_IN_DIM, 0), name="ssm_in_dw_z")
    dwsi = dwf(dxbc, h1, tm=1024, tn=1024, out_into=(dwsi, SSM_IN_DIM, SSM_D_INNER // 1024), name="ssm_in_dw_xbc")
    dwdt = dwf(ddtraw, h1, tm=128, tn=1024, name="ssm_in_dw_dt")
    dwsi = _put_rows(dwsi, dwdt, SSM_HEADS, SSM_D_INNER + SSM_CONV_DIM, "ssm_in_dw_put_dt")
    big["ssm_w_in"] = dwsi.reshape(N_CHIPS, SSM_IN_DIM // N_CHIPS, D_MODEL)
    tok = send("ssm", [big["ssm_w_in"], big["ssm_w_out"]])
    dh1 = mmf(dz, P["wzT"], tn=1024, tk=2048, name="ssm_in_dx_z")
    dh1 = mmf(dxbc, P["wxbcT"], tn=1024, tk=2048, res=dh1, name="ssm_in_dx_xbc")
    dx2, dnm1 = mmf(ddtraw, P["wdtT"], tn=1024, tk=128, res=dh1, norm_bwd=(x2, P["nm"][1], dx3, tok), name="ssm_in_dx_dt")
    dx1, dnf0, dfcw0, dfcb0 = ffn_bwd(x1, dx2, hf0, hid0, act0, 0)
    dcat = mmf(dx1, P["wmo"], tb=True, tn=1024, tk=1024, name="mix_out_dx")
    big["mix_w_out"] = dwf(cat, dx1, tm=1024, tn=1024, name="mix_out_dw").reshape(N_CHIPS, D_MODEL // N_CHIPS, D_MODEL)
    dproj0, d_pw, d_ps, d_sk = _mixcore_bwd(proj0, cos, sin_s, P["pool_w"], P["pool_scale"], P["sinks"], attn, lse, dcat, "mixcore_bwd")
    big["mix_w_in"] = dwf(dproj0, h0, tm=1280, tn=1024, name="mix_in_dw").reshape(N_CHIPS, MIX_IN_DIM // N_CHIPS, D_MODEL)
    tok = send("mix", [big["mix_w_in"], big["mix_w_out"]])
    dx0, dnm0 = mmf(dproj0, P["wmiT"], tn=1024, tk=1280, norm_bwd=(x0, P["nm"][0], dx1, tok), name="mix_in_dx")

    def unperm_cols(a):
        r = a.shape[0]
        t = a.reshape(r, N_CHIPS, FFN_TC)
        return jnp.stack([t[:, p] for p in _PERM], axis=0)

    small["norm_mix"] = jnp.concatenate([dnm0, dnm1], axis=0)
    small["norm_ffn"] = jnp.concatenate([dnf0, dnf1], axis=0)
    small["pool_w"] = d_pw.reshape(4 * POOL_GROUP, POOL_GROUP)
    small["pool_scale"] = d_ps
    small["attn_sinks"] = d_sk
    small["ssm_dt_bias"] = d_dtb
    small["ssm_A_log"] = d_alog
    small["ssm_D"] = d_dskip
    fcb = jnp.stack([unperm_cols(dfcb0), unperm_cols(dfcb1)], axis=0)
    small["ffn_conv_b"] = fcb.reshape(2, 2 * D_FF)
    small["ssm_conv_w"] = d_scw.reshape(SSM_CONV, N_CHIPS, SSM_CONV_DIM // N_CHIPS).transpose(1, 0, 2)
    small["ssm_conv_b"] = d_scb.reshape(N_CHIPS, 1, SSM_CONV_DIM // N_CHIPS)
    small["ssm_norm"] = d_snorm.reshape(N_CHIPS, 1, SSM_D_INNER // N_CHIPS)
    small["ffn_conv_w"] = jnp.concatenate([unperm_cols(dfcw0), unperm_cols(dfcw1)], axis=1)
    return loss_row, dx0, big, small


ANY = pl.BlockSpec(memory_space=pl.ANY)


def _place():
    return lax.axis_index("x"), lax.axis_index("y"), lax.axis_index("c")


def _gather_shards(shards, name):
    n = len(shards)
    split = [s.size >= (1 << 16) for s in shards]

    def half(ref, a, h):
        shp = shards[a].shape
        if len(shp) == 3:
            return ref.at[h]
        r2 = shp[0] // 2
        return ref.at[pl.ds(pl.multiple_of(h * r2, 2 * SUBLANES), r2), :]

    def body(*refs):
        ins, outs = refs[:n], refs[n:2 * n]
        send, recv, fsend, frecv = refs[2 * n:]
        x, y, c = _place()
        k = 2 * x + y
        chips = [(1 - x, y), (x, 1 - y), (1 - x, 1 - y)]

        def ici(a, j, src_slot_ref, dst_slot):
            px, py = chips[j]
            src = half(src_slot_ref, a, c) if split[a] else src_slot_ref
            dst = half(outs[a].at[dst_slot], a, c) if split[a] else outs[a].at[dst_slot]
            return pltpu.make_async_remote_copy(src, dst, send.at[a, j], recv.at[a, j], device_id=(px, py, c), device_id_type=MESH)

        def d2d(a, j, h):
            px, py = chips[j]
            part = half(outs[a].at[2 * px + py], a, h)
            return pltpu.make_async_remote_copy(part, part, fsend.at[a, j], frecv.at[a, j], device_id=(x, y, 1 - c), device_id_type=MESH)

        sends = [ici(a, j, ins[a], k) for a in range(n) for j in range(3)]
        for cp in sends:
            cp.start()
        passed = []
        for a in range(n):
            for j, (px, py) in enumerate(chips):
                ici(a, j, ins[a], 2 * px + py).wait_recv()
                if split[a]:
                    passed.append(d2d(a, j, c))
                    passed[-1].start()
        for a in range(n):
            if split[a]:
                for j in range(3):
                    d2d(a, j, 1 - c).wait_recv()
        for cp in sends + passed:
            cp.wait_send()

    return pl.pallas_call(
        body, in_specs=[ANY] * n, out_specs=[ANY] * n,
        out_shape=[_sds((N_CHIPS,) + s.shape, s.dtype) for s in shards],
        scratch_shapes=[pltpu.SemaphoreType.DMA((n, 3))] * 4,
        compiler_params=pltpu.CompilerParams(has_side_effects=True), name=name)(*shards)


HBM = pl.BlockSpec(memory_space=pltpu.HBM)
SEM = pl.BlockSpec(memory_space=pltpu.SEMAPHORE)
DATAFLOW = pltpu.SideEffectType.DATAFLOW_SIDE_EFFECTING


def _row_half(ref, h):
    r2 = ref.shape[0] // 2
    return ref.at[pl.ds(pl.multiple_of(h * r2, 2 * SUBLANES), r2), :]


def _spread_start(groups, slot_src, after, name, halved=()):
    flat = [a for grp in groups for a in grp]
    n = len(flat)
    ng = len(groups)
    offs = [sum(len(g) for g in groups[:i]) for i in range(ng)]
    lshape = [(a.shape if slot_src else (N_CHIPS,) + a.shape) for a in flat]

    nsem = 6 * n

    def body(*refs):
        src, land = refs[:n], refs[n:2 * n]
        sems = refs[2 * n + 1:2 * n + 1 + nsem]
        token = refs[-1]
        x, y, c = _place()
        k = 2 * x + y
        chips = [(1 - x, y), (x, 1 - y), (1 - x, 1 - y)]
        for a in range(n):
            half = any(offs[gi] <= a < offs[gi] + len(groups[gi]) for gi in halved)
            for j, (px, py) in enumerate(chips):
                s = src[a].at[2 * px + py] if slot_src else src[a]
                d = land[a].at[k]
                if half:
                    s, d = _row_half(s, c), _row_half(d, c)
                pltpu.make_async_remote_copy(s, d, sems[6 * a + 2 * j], sems[6 * a + 2 * j + 1],
                                             device_id=(px, py, c), device_id_type=MESH).start()
        token[...] = jnp.zeros(token.shape, token.dtype)

    out_shape = [pltpu.SemaphoreType.DMA(())] * nsem
    out_shape += [pltpu.HBM(a.shape, a.dtype) for a in flat] + [pltpu.HBM(s, a.dtype) for s, a in zip(lshape, flat)]
    out_shape.append(_sds((SUBLANES, LANES)))
    args = [pltpu.with_memory_space_constraint(a, pltpu.HBM) for a in flat]
    args += [pltpu.with_memory_space_constraint(lax.empty(s, a.dtype), pltpu.HBM) for s, a in zip(lshape, flat)]
    res = pl.pallas_call(
        body, name=name, out_shape=tuple(out_shape), in_specs=[HBM] * (2 * n) + [pl.BlockSpec(memory_space=pl.ANY)],
        out_specs=tuple([SEM] * nsem + [HBM] * (2 * n) + [pl.BlockSpec(memory_space=pltpu.VMEM)]),
        input_output_aliases={i: nsem + i for i in range(2 * n)},
        compiler_params=pltpu.CompilerParams(has_side_effects=DATAFLOW))(*args, after)
    sems, thru, token = res[:nsem], res[nsem:nsem + 2 * n], res[-1]
    out = []
    for gi, grp in enumerate(groups):
        sl = slice(offs[gi], offs[gi] + len(grp))
        out.append((list(sems[6 * offs[gi]:6 * (offs[gi] + len(grp))]), list(thru[:n][sl]), list(thru[n:][sl])))
    return out, token


def _spread_wait(started, slot_src, after, name, halved=False):
    sems, srcs, lands = started
    n = len(srcs)

    def body(*refs):
        src, land = refs[:n], refs[n:2 * n]
        sem = refs[2 * n:2 * n + 6 * n]
        x, y, c = _place()
        chips = [(1 - x, y), (x, 1 - y), (1 - x, 1 - y)]
        for a in range(n):
            for j, (px, py) in enumerate(chips):
                s = src[a].at[2 * px + py] if slot_src else src[a]
                d = land[a].at[2 * px + py]
                if halved:
                    s, d = _row_half(s, c), _row_half(d, c)
                cp = pltpu.make_async_remote_copy(s, d, sem[6 * a + 2 * j], sem[6 * a + 2 * j + 1],
                                                  device_id=(px, py, c), device_id_type=MESH)
                cp.wait_send()
                cp.wait_recv()

    res = pl.pallas_call(
        body, name=name, out_shape=tuple([pltpu.HBM(a.shape, a.dtype) for a in srcs] + [pltpu.HBM(a.shape, a.dtype) for a in lands]),
        in_specs=[HBM] * (2 * n) + [SEM] * (6 * n) + [pl.BlockSpec(memory_space=pl.ANY)], out_specs=tuple([HBM] * (2 * n)),
        input_output_aliases={i: i for i in range(2 * n)},
        compiler_params=pltpu.CompilerParams(has_side_effects=DATAFLOW))(*srcs, *lands, *sems, after)
    return list(res[:n]), list(res[n:])


def _sibling_fill(lands, name):
    n = len(lands)

    def body(*refs):
        bufs = refs[n:2 * n]
        send, recv = refs[2 * n:]
        x, y, c = _place()
        chips = [(1 - x, y), (x, 1 - y), (1 - x, 1 - y)]

        def copy(a, j, h):
            px, py = chips[j]
            part = _row_half(bufs[a].at[2 * px + py], h)
            return pltpu.make_async_remote_copy(part, part, send.at[a, j], recv.at[a, j], device_id=(x, y, 1 - c), device_id_type=MESH)

        sends = [copy(a, j, c) for a in range(n) for j in range(3)]
        for cp in sends:
            cp.start()
        for a in range(n):
            for j in range(3):
                copy(a, j, 1 - c).wait_recv()
        for cp in sends:
            cp.wait_send()

    return pl.pallas_call(
        body, in_specs=[ANY] * n, out_specs=[ANY] * n, out_shape=[_sds(t.shape, t.dtype) for t in lands],
        input_output_aliases={i: i for i in range(n)},
        scratch_shapes=[pltpu.SemaphoreType.DMA((n, 3)), pltpu.SemaphoreType.DMA((n, 3))],
        compiler_params=pltpu.CompilerParams(has_side_effects=True), name=name)(*lands)


def _sibling_exchange(fs, name):
    n = len(fs)

    def body(*refs):
        ins, outs = refs[:n], refs[n:2 * n]
        send, recv = refs[2 * n:]
        x, y, c = _place()
        cps = [pltpu.make_async_remote_copy(ins[a], outs[a], send.at[a], recv.at[a],
                                            device_id=(x, y, 1 - c), device_id_type=MESH) for a in range(n)]
        for cp in cps:
            cp.start()
        for cp in cps:
            cp.wait()

    return pl.pallas_call(
        body, in_specs=[ANY] * n, out_specs=[ANY] * n, out_shape=[_sds(f.shape, f.dtype) for f in fs],
        scratch_shapes=[pltpu.SemaphoreType.DMA((n,)), pltpu.SemaphoreType.DMA((n,))],
        compiler_params=pltpu.CompilerParams(has_side_effects=True), name=name)(*fs)


def _tile2d(rows, cols, budget=1024 * 1024, step=2 * SUBLANES):
    fits = [t for t in range(step, rows + 1, step) if rows % t == 0 and t * cols * 4 <= budget]
    if fits:
        return fits[-1], cols
    fits = [t for t in range(LANES, cols + 1, LANES) if cols % t == 0 and rows * t * 4 <= budget]
    assert fits, (rows, cols)
    return rows, fits[-1]


def _chip_sum(own, parts, kidx, name):
    _, R, C = parts.shape
    tr, tc = _tile2d(R, C)

    def body(k_ref, o_ref_in, p1_ref, p2_ref, p3_ref, o_ref):
        tot = ((o_ref_in[...].astype(F32) + p1_ref[...].astype(F32)) + p2_ref[...].astype(F32)) + p3_ref[...].astype(F32)
        o_ref[...] = tot.astype(o_ref.dtype)

    def slot(d):
        return pl.BlockSpec((None, tr, tc), lambda i, j, k: ((k[0] + d) % N_CHIPS, i, j))

    return pl.pallas_call(
        body,
        grid_spec=pltpu.PrefetchScalarGridSpec(
            num_scalar_prefetch=1, grid=(R // tr, C // tc), in_specs=[slot(0), slot(1), slot(2), slot(3)],
            out_specs=pl.BlockSpec((tr, tc), lambda i, j, k: (i, j))),
        out_shape=_sds((R, C), BF16), compiler_params=_cp("parallel", "parallel"), name=name)(kidx, own, parts, parts, parts)


def _adamw_math(w, g, m, v):
    m2 = ADAM_B1 * m + (1.0 - ADAM_B1) * g
    v2 = ADAM_B2 * v + (1.0 - ADAM_B2) * (g * g)
    m_hat = m2 / (1.0 - ADAM_B1 ** ADAM_STEP)
    v_hat = v2 / (1.0 - ADAM_B2 ** ADAM_STEP)
    delta = -ADAM_LR * (m_hat / (jnp.sqrt(v_hat) + ADAM_EPS) + ADAM_WD * w)
    return delta, m2, v2


def _adamw(w, m, v, gparts, name):
    Lw, R, C = w.shape
    tr, tc = _tile2d(R, C)
    flat = [h for pair in gparts for h in pair]

    def body(*refs):
        w_ref, m_ref, v_ref = refs[:3]
        g_refs = refs[3:3 + 2 * Lw]
        go_ref, d_ref, mo_ref, vo_ref = refs[3 + 2 * Lw:]
        g = g_refs[0][...].astype(F32) + g_refs[1][...].astype(F32)
        for l in range(1, Lw):
            g = jnp.where(pl.program_id(0) == l, g_refs[2 * l][...].astype(F32) + g_refs[2 * l + 1][...].astype(F32), g)
        d, m2, v2 = _adamw_math(w_ref[...], g, m_ref[...], v_ref[...])
        go_ref[...] = g
        d_ref[...] = d
        mo_ref[...] = m2
        vo_ref[...] = v2

    blk = pl.BlockSpec((None, tr, tc), lambda l, i, j: (l, i, j))
    gblk = pl.BlockSpec((tr, tc), lambda l, i, j: (i, j))
    return pl.pallas_call(
        body, grid=(Lw, R // tr, C // tc), in_specs=[blk, blk, blk] + [gblk] * (2 * Lw), out_specs=[blk] * 4,
        out_shape=[_sds((Lw, R, C))] * 4, compiler_params=_cp("parallel", "parallel", "parallel"), name=name)(w, m, v, *flat)


def _small_adamw(grads, wmv, name):
    n = len(grads)

    def body(*refs):
        g_in, p_in, outs = refs[:n], refs[n:4 * n], refs[4 * n:]
        for a in range(n):
            g = g_in[a][...]
            d_, m2, v2 = _adamw_math(p_in[3 * a][...], g, p_in[3 * a + 1][...], p_in[3 * a + 2][...])
            outs[4 * a][...] = g
            outs[4 * a + 1][...] = d_
            outs[4 * a + 2][...] = m2
            outs[4 * a + 3][...] = v2

    vm = pl.BlockSpec(memory_space=pltpu.VMEM)
    args = list(grads) + [t for tri in wmv for t in tri]
    out_shape = [_sds(g.shape) for g in grads for _ in range(4)]
    return pl.pallas_call(body, in_specs=[vm] * len(args), out_specs=[vm] * len(out_shape), out_shape=out_shape,
                          compiler_params=pltpu.CompilerParams(vmem_limit_bytes=V7X_VMEM_LIMIT), name=name)(*args)


def _small_allreduce(partials, pshapes, loss_row, name):
    n = len(partials)
    gshapes = [p.shape for p in partials] + [loss_row.shape]
    ng = n + 1

    def body(*refs):
        g_in = refs[:ng]
        outs = refs[ng:2 * ng]
        sib = refs[2 * ng:3 * ng]
        pair = refs[3 * ng:4 * ng]
        bufs = refs[4 * ng:5 * ng]
        send1, recv1, send2, recv2 = refs[-4:]
        x, y, c = _place()
        k = 2 * x + y
        chips = [(1 - x, y), (x, 1 - y), (1 - x, 1 - y)]
        swaps = [pltpu.make_async_remote_copy(g_in[a], sib[a], send1.at[a], recv1.at[a],
                                              device_id=(x, y, 1 - c), device_id_type=MESH) for a in range(ng)]
        for cp in swaps:
            cp.start()
        for a, cp in enumerate(swaps):
            cp.wait()
            pair[a][...] = g_in[a][...] + sib[a][...]
            bufs[a][k] = pair[a][...]
        sends = [pltpu.make_async_remote_copy(pair[a], bufs[a].at[k], send2.at[a, j], recv2.at[a, j],
                                              device_id=(px, py, c), device_id_type=MESH)
                 for a in range(ng) for j, (px, py) in enumerate(chips)]
        for cp in sends:
            cp.start()
        for a in range(ng):
            for j, (px, py) in enumerate(chips):
                pltpu.make_async_remote_copy(pair[a], bufs[a].at[2 * px + py], send2.at[a, j], recv2.at[a, j],
                                             device_id=(px, py, c), device_id_type=MESH).wait_recv()
        for cp in sends:
            cp.wait_send()
        for a in range(ng):
            sharded = len(gshapes[a]) == 3

            def part(d):
                return bufs[a][d, k] if sharded else bufs[a][d]

            tot = part(0)
            for d in range(1, N_CHIPS):
                tot = tot + part(d)
            if a == n:
                outs[n][...] = tot
            else:
                pr, pc = pshapes[a]
                outs[a][...] = tot[:pr, :pc]

    vm = pl.BlockSpec(memory_space=pltpu.VMEM)
    args = list(partials) + [loss_row]
    out_shape = [_sds(ps) for ps in pshapes] + [_sds(loss_row.shape)]
    return pl.pallas_call(
        body, in_specs=[vm] * len(args), out_specs=[vm] * len(out_shape), out_shape=out_shape,
        scratch_shapes=[pltpu.VMEM(tuple(s), F32) for s in gshapes] * 2 + [pltpu.VMEM((N_CHIPS,) + tuple(s), F32) for s in gshapes]
        + [pltpu.SemaphoreType.DMA((ng,)), pltpu.SemaphoreType.DMA((ng,)),
           pltpu.SemaphoreType.DMA((ng, 3)), pltpu.SemaphoreType.DMA((ng, 3))],
        compiler_params=pltpu.CompilerParams(has_side_effects=True, vmem_limit_bytes=V7X_VMEM_LIMIT), name=name)(*args)


_PERM = (0, 2, 1, 3)


def _cols_from_shards(g):
    return g.transpose(1, 0, 2).reshape(g.shape[1], N_CHIPS * g.shape[2])


def _rope_tables(positions):
    inv_freq = ROPE_THETA ** (-jnp.arange(0, HEAD_DIM, 2, dtype=F32) / HEAD_DIM)
    ang = positions.astype(F32).reshape(-1, 1) * inv_freq
    cos, sin = jnp.cos(ang), jnp.sin(ang)
    cos = jnp.concatenate([cos, cos, cos, cos], axis=-1)
    sin_s = jnp.concatenate([-sin, sin, -sin, sin], axis=-1)
    return cos, sin_s


def kernel(x, positions, norm_mix, norm_ffn, norm_final, mix_w_in, pool_w, pool_scale, attn_sinks, mix_w_out, ssm_w_in, ssm_conv_w, ssm_conv_b, ssm_dt_bias, ssm_A_log, ssm_D, ssm_norm, ssm_w_out, ffn_w_up, ffn_conv_w, ffn_conv_b, ffn_w_down, loss_target, m_norm_mix, m_norm_ffn, m_norm_final, m_mix_w_in, m_pool_w, m_pool_scale, m_attn_sinks, m_mix_w_out, m_ssm_w_in, m_ssm_conv_w, m_ssm_conv_b, m_ssm_dt_bias, m_ssm_A_log, m_ssm_D, m_ssm_norm, m_ssm_w_out, m_ffn_w_up, m_ffn_conv_w, m_ffn_conv_b, m_ffn_w_down, v_norm_mix, v_norm_ffn, v_norm_final, v_mix_w_in, v_pool_w, v_pool_scale, v_attn_sinks, v_mix_w_out, v_ssm_w_in, v_ssm_conv_w, v_ssm_conv_b, v_ssm_dt_bias, v_ssm_A_log, v_ssm_D, v_ssm_norm, v_ssm_w_out, v_ffn_w_up, v_ffn_conv_w, v_ffn_conv_b, v_ffn_w_down):
    W = dict(norm_mix=norm_mix, norm_ffn=norm_ffn, norm_final=norm_final, mix_w_in=mix_w_in, pool_w=pool_w, pool_scale=pool_scale, attn_sinks=attn_sinks, mix_w_out=mix_w_out, ssm_w_in=ssm_w_in, ssm_conv_w=ssm_conv_w, ssm_conv_b=ssm_conv_b, ssm_dt_bias=ssm_dt_bias, ssm_A_log=ssm_A_log, ssm_D=ssm_D, ssm_norm=ssm_norm, ssm_w_out=ssm_w_out, ffn_w_up=ffn_w_up, ffn_conv_w=ffn_conv_w, ffn_conv_b=ffn_conv_b, ffn_w_down=ffn_w_down)
    Mo = dict(norm_mix=m_norm_mix, norm_ffn=m_norm_ffn, norm_final=m_norm_final, mix_w_in=m_mix_w_in, pool_w=m_pool_w, pool_scale=m_pool_scale, attn_sinks=m_attn_sinks, mix_w_out=m_mix_w_out, ssm_w_in=m_ssm_w_in, ssm_conv_w=m_ssm_conv_w, ssm_conv_b=m_ssm_conv_b, ssm_dt_bias=m_ssm_dt_bias, ssm_A_log=m_ssm_A_log, ssm_D=m_ssm_D, ssm_norm=m_ssm_norm, ssm_w_out=m_ssm_w_out, ffn_w_up=m_ffn_w_up, ffn_conv_w=m_ffn_conv_w, ffn_conv_b=m_ffn_conv_b, ffn_w_down=m_ffn_w_down)
    Vo = dict(norm_mix=v_norm_mix, norm_ffn=v_norm_ffn, norm_final=v_norm_final, mix_w_in=v_mix_w_in, pool_w=v_pool_w, pool_scale=v_pool_scale, attn_sinks=v_attn_sinks, mix_w_out=v_mix_w_out, ssm_w_in=v_ssm_w_in, ssm_conv_w=v_ssm_conv_w, ssm_conv_b=v_ssm_conv_b, ssm_dt_bias=v_ssm_dt_bias, ssm_A_log=v_ssm_A_log, ssm_D=v_ssm_D, ssm_norm=v_ssm_norm, ssm_w_out=v_ssm_w_out, ffn_w_up=v_ffn_w_up, ffn_conv_w=v_ffn_conv_w, ffn_conv_b=v_ffn_conv_b, ffn_w_down=v_ffn_w_down)

    kchip = 2 * lax.axis_index("x") + lax.axis_index("y")

    def own_slot(g, own):
        return lax.dynamic_update_slice_in_dim(g, own[None], kchip, axis=0)

    def tr(t):
        return jnp.swapaxes(t[0], 0, 1)

    later = dict(ffn0=[ffn_w_up[0].astype(MXU), ffn_w_down[0].astype(MXU)],
                 ssm=[tr(ssm_w_in).astype(MXU), ssm_w_out[0].astype(MXU)],
                 ffn1=[ffn_w_up[1].astype(MXU), ffn_w_down[1].astype(MXU)])
    sh = [tr(mix_w_in).astype(MXU), mix_w_out[0].astype(MXU), ssm_conv_w[0], ssm_conv_b, ssm_norm, ffn_conv_w]
    first = _gather_shards(sh, "gather_first")
    g_mi, g_mo, g_scw, g_scb, g_sn, g_fcw = [own_slot(g, own) for g, own in zip(first, sh)]
    started, token = _spread_start(list(later.values()), False, first[0], "gather_start", halved=(0,))
    started = dict(zip(later.keys(), started))
    fcw = [jnp.concatenate([g_fcw[p, i] for p in _PERM], axis=1) for i in range(2)]
    P = dict(
        nm=norm_mix, nf=norm_ffn, nfin=norm_final,
        wmiT=g_mi.reshape(MIX_IN_DIM, D_MODEL), wmo=g_mo.reshape(D_MODEL, D_MODEL),
        pool_w=pool_w[0], pool_scale=pool_scale, sinks=attn_sinks[0],
        scw=_cols_from_shards(g_scw), scb=g_scb.reshape(1, SSM_CONV_DIM), snorm=g_sn.reshape(1, SSM_D_INNER),
        dt_bias=jnp.pad(ssm_dt_bias, ((0, 0), (0, LANES - SSM_HEADS))), a_log=jnp.pad(ssm_A_log, ((0, 0), (0, LANES - SSM_HEADS))),
        d_exp=jnp.repeat(ssm_D, SSM_D_INNER // SSM_HEADS, axis=1),
        fcb=[jnp.concatenate([ffn_conv_b[i:i + 1, p * FFN_TC:(p + 1) * FFN_TC] for p in _PERM], axis=1) for i in range(2)],
    )

    def fetch(group, after):
        owns, lands = _spread_wait(started[group], False, after, f"gather_wait_{group}", halved=group == "ffn0")
        if group == "ffn0":
            lands = _sibling_fill(lands, "gather_fill_ffn0")
        a, b = [own_slot(g, own) for g, own in zip(lands, owns)]
        if group == "ssm":
            wsi = a.reshape(SSM_IN_DIM, D_MODEL)
            zx = SSM_D_INNER + SSM_CONV_DIM
            return dict(wzT=wsi[:SSM_D_INNER], wxbcT=wsi[SSM_D_INNER:zx],
                        wdtT=jnp.pad(wsi[zx:], ((0, LANES - SSM_HEADS), (0, 0))), wso=b.reshape(SSM_D_INNER, D_MODEL))
        i = int(group[-1])
        return dict(wup=jnp.concatenate([a[p] for p in _PERM], axis=1), wdn=b.reshape(D_FF, D_MODEL), fcw=fcw[i])

    cos, sin_s = _rope_tables(positions)
    sent = {}

    def send(group, grads):
        res, tok = _spread_start([grads], True, jnp.zeros((SUBLANES, LANES), F32), f"grad_start_{group}")
        sent[group] = res[0]
        return tok

    loss_row, grad_x, big, small = _local_step(x[0], cos, sin_s, loss_target[0], P, fetch, token, send)

    kidx = kchip.astype(jnp.int32).reshape(1)
    group_names = dict(ffn1=["ffn_w_up1", "ffn_w_down1"], ssm=["ssm_w_in", "ssm_w_out"], ffn0=["ffn_w_up0", "ffn_w_down0"],
                       mix=["mix_w_in", "mix_w_out"])
    names, mine = [], []
    for group, started_g in sent.items():
        grads, lands = _spread_wait(started_g, True, grad_x, f"grad_wait_{group}")
        for nm, g, land in zip(group_names[group], grads, lands):
            names.append(nm)
            mine.append(_chip_sum(g, land, kidx, f"chip_sum_{nm}"))
    theirs = _sibling_exchange(mine, "sibling_exchange")
    red = {nm: (a, b) for nm, a, b in zip(names, mine, theirs)}

    out = {}

    def big_update(pname, gparts, transposed=False):
        w = W[pname]
        lw = len(gparts)
        shp = w.shape
        rr, cc = gparts[0][0].shape
        fix = (lambda t: tr(t)[None]) if transposed else (lambda t: t.reshape(lw, rr, cc))
        res = _adamw(fix(w), fix(Mo[pname]), fix(Vo[pname]), gparts, f"adamw_{pname}")
        out[pname] = tuple((tr(r)[None] if transposed else r.reshape(shp)) for r in res)

    big_update("mix_w_in", [red["mix_w_in"]], transposed=True)
    big_update("mix_w_out", [red["mix_w_out"]])
    big_update("ssm_w_in", [red["ssm_w_in"]], transposed=True)
    big_update("ssm_w_out", [red["ssm_w_out"]])
    big_update("ffn_w_up", [red["ffn_w_up0"], red["ffn_w_up1"]])
    big_update("ffn_w_down", [red["ffn_w_down0"], red["ffn_w_down1"]])

    small_names = ["norm_mix", "norm_ffn", "norm_final", "pool_w", "pool_scale", "attn_sinks", "ssm_dt_bias", "ssm_A_log",
                   "ssm_D", "ffn_conv_b", "ssm_conv_w", "ssm_conv_b", "ssm_norm", "ffn_conv_w"]

    def as2d(t):
        if t.ndim == 1:
            return t.reshape(1, -1)
        return t.reshape(-1, t.shape[-1])

    wmv = [(as2d(W[nm]), as2d(Mo[nm]), as2d(Vo[nm])) for nm in small_names]
    summed = _small_allreduce([small[nm] for nm in small_names], [t[0].shape for t in wmv], loss_row, "small_allreduce")
    res = _small_adamw(summed[:-1], wmv, "small_adamw")
    for a, nm in enumerate(small_names):
        out[nm] = tuple(r.reshape(W[nm].shape) for r in res[4 * a:4 * a + 4])
    loss = summed[-1][0, 0]

    order = ["norm_mix", "norm_ffn", "norm_final", "mix_w_in", "pool_w", "pool_scale", "attn_sinks", "mix_w_out", "ssm_w_in",
             "ssm_conv_w", "ssm_conv_b", "ssm_dt_bias", "ssm_A_log", "ssm_D", "ssm_norm", "ssm_w_out", "ffn_w_up", "ffn_conv_w",
             "ffn_conv_b", "ffn_w_down"]
    return (loss, grad_x.reshape(x.shape), *[out[nm][0] for nm in order], *[out[nm][1] for nm in order],
            *[out[nm][2] for nm in order], *[out[nm][3] for nm in order])
```

```python
import functools

import jax
import jax.numpy as jnp
from jax import lax
from jax.experimental import pallas as pl
from jax.experimental.pallas import tpu as pltpu

F32 = jnp.float32
BF16 = jnp.bfloat16
MXU = BF16
HI = lax.Precision.HIGHEST

D_MODEL = 1024
POOL_WINDOWS = (2, 4, 8, 16)
POOL_DIM = 512
POOL_GROUP = 128
HEAD_DIM = 64
N_HEADS = 8
N_KV_HEADS = 2
GQ = 4
Q_DIM = 512
KV_DIM = 128
BLOCK = 128
ROPE_THETA = 10000.0
MIX_IN_DIM = 1280
SSM_D_INNER = 2048
SSM_HEADS = 32
SSM_GROUPS = 8
SSM_STATE = 128
SSM_CONV = 4
SSM_CHUNK = 128
SSM_CONV_DIM = 4096
SSM_IN_DIM = 6176
D_FF = 2816
FFN_CONV = 3
NORM_EPS = 1e-6
SSM_NORM_EPS = 1e-5
ADAM_LR = 0.001
ADAM_B1 = 0.9
ADAM_B2 = 0.999
ADAM_EPS = 1e-08
ADAM_WD = 0.01
ADAM_STEP = 10

N_CHIPS = 4
N_DEV = 8
LANES = 128
SUBLANES = 8
V7X_VMEM_LIMIT = 56 * 1024 * 1024
NEG = -1e30
MESH = pl.DeviceIdType.MESH


def _cp(*sem):
    return pltpu.CompilerParams(dimension_semantics=sem if sem else None, vmem_limit_bytes=V7X_VMEM_LIMIT)


def _sds(shape, dtype=F32):
    return jax.ShapeDtypeStruct(tuple(shape), dtype)


def _iota(shape, dim):
    return lax.broadcasted_iota(jnp.int32, shape, dim)


def _silu(x):
    return x * (1.0 / (1.0 + jnp.exp(-x)))


def _dsilu(x):
    s = 1.0 / (1.0 + jnp.exp(-x))
    return s * (1.0 + x * (1.0 - s))


def _mm(a, b, *, ta=False, tb=False, tm, tn, tk, res=None, out_dtype=F32, out_shard_perm=None, out_into=None, norm_w=None,
        norm_bwd=None, name):
    M, K = (a.shape[1], a.shape[0]) if ta else a.shape
    N = b.shape[0] if tb else b.shape[1]
    tm, tn, tk = min(tm, M), min(tn, N), min(tk, K)
    gm, gn, gk = M // tm, N // tn, K // tk
    assert gm * tm == M and gn * tn == N and gk * tk == K, (name, M, N, K, tm, tn, tk)
    a_spec = pl.BlockSpec((tk, tm), lambda i, j, k: (k, i)) if ta else pl.BlockSpec((tm, tk), lambda i, j, k: (i, k))
    b_spec = pl.BlockSpec((tn, tk), lambda i, j, k: (j, k)) if tb else pl.BlockSpec((tk, tn), lambda i, j, k: (k, j))
    dims = (((0 if ta else 1,), (1 if tb else 0,)), ((), ()))
    has_res = res is not None
    has_nw = norm_w is not None
    has_nb = norm_bwd is not None
    has_tok = has_nb and norm_bwd[3] is not None
    assert not (has_nw or has_nb) or (gn == 1 and out_shard_perm is None)
    n_extra = has_res + has_nw + (3 + has_tok if has_nb else 0)

    def body(*refs):
        a_ref, b_ref = refs[0], refs[1]
        extra = list(refs[2:2 + n_extra])
        outs = refs[len(args):]
        r_ref = extra.pop(0) if has_res else None
        nw_ref = extra.pop(0) if has_nw else None
        nb_refs = extra if has_nb else None

        def dot():
            return lax.dot_general(a_ref[...].astype(MXU), b_ref[...].astype(MXU), dims, preferred_element_type=F32)

        def finish(r):
            if has_res:
                r = r + r_ref[...]
            if has_nb:
                xv = nb_refs[0][...]
                rs = lax.rsqrt(jnp.mean(xv * xv, axis=-1, keepdims=True) + NORM_EPS)
                xh = xv * rs
                g = r * nb_refs[1][...]
                dr = nb_refs[2][...] + nb_refs[3][0:1, 0:1] if has_tok else nb_refs[2][...]
                outs[0][...] = dr + rs * (g - xh * jnp.mean(g * xh, axis=-1, keepdims=True))
                part = jnp.sum(r * xh, axis=0, keepdims=True)
                i = pl.program_id(0)

                @pl.when(i == 0)
                def _():
                    outs[1][...] = part

                @pl.when(i > 0)
                def _():
                    outs[1][...] += part
                return
            outs[0][...] = r.astype(out_dtype)
            if has_nw:
                rs = lax.rsqrt(jnp.mean(r * r, axis=-1, keepdims=True) + NORM_EPS)
                outs[1][...] = (r * rs * nw_ref[...]).astype(outs[1].dtype)

        if gk == 1:
            finish(dot())
        else:
            acc = refs[-1]
            k = pl.program_id(2)

            @pl.when(k == 0)
            def _():
                acc[...] = dot()

            if gk > 2:
                @pl.when(jnp.logical_and(k > 0, k < gk - 1))
                def _():
                    acc[...] += dot()

            @pl.when(k == gk - 1)
            def _():
                finish(acc[...] + dot())

    tile = pl.BlockSpec((tm, tn), lambda i, j, k: (i, j))
    row = pl.BlockSpec((1, tn), lambda i, j, k: (0, j))
    in_specs = [a_spec, b_spec]
    args = [a, b]
    if has_res:
        in_specs.append(tile)
        args.append(res)
    if has_nw:
        in_specs.append(row)
        args.append(norm_w.reshape(1, N))
    if has_nb:
        in_specs += [tile, row, tile]
        args += [norm_bwd[0], norm_bwd[1].reshape(1, N), norm_bwd[2]]
        if has_tok:
            in_specs.append(pl.BlockSpec((SUBLANES, LANES), lambda i, j, k: (0, 0)))
            args.append(norm_bwd[3])
    alias = {}
    if out_into is not None:
        buf, rows, off = out_into
        out_spec = pl.BlockSpec((tm, tn), lambda i, j, k: (i + off, j))
        out_shape = _sds((rows, N), out_dtype)
        if buf is not None:
            alias = {len(args): 0}
            in_specs.append(pl.BlockSpec(memory_space=pl.ANY))
            args.append(buf)
    elif out_shard_perm is None:
        out_spec = tile
        out_shape = _sds((M, N), out_dtype)
    else:
        assert gn == len(out_shard_perm) == 4 and tuple(out_shard_perm) == (0, 2, 1, 3)
        out_spec = pl.BlockSpec((None, tm, tn), lambda i, j, k: ((j % 2) * 2 + j // 2, i, 0))
        out_shape = _sds((gn, M, tn), out_dtype)
    sem = ("parallel", "parallel", "arbitrary")
    if has_nw:
        out_spec, out_shape = [out_spec, tile], [out_shape, _sds((M, N), MXU)]
    if has_nb:
        out_spec, out_shape = [tile, row], [_sds((M, N)), _sds((1, N))]
        sem = ("arbitrary", "arbitrary", "arbitrary")
    return pl.pallas_call(
        body, grid=(gm, gn, gk), in_specs=in_specs, out_specs=out_spec, out_shape=out_shape,
        scratch_shapes=[pltpu.VMEM((tm, tn), F32)] if gk > 1 else [], input_output_aliases=alias,
        compiler_params=_cp(*sem), name=name)(*args)


def _put_rows(buf, src, rows, at, name):
    assert at % rows == 0 and src.shape[1] == buf.shape[1] and src.dtype == buf.dtype
    C = buf.shape[1]

    def body(s_ref, b_ref, o_ref):
        o_ref[...] = s_ref[...]

    return pl.pallas_call(
        body, grid=(1,), in_specs=[pl.BlockSpec((rows, C), lambda i: (0, 0)), pl.BlockSpec(memory_space=pl.ANY)],
        out_specs=pl.BlockSpec((rows, C), lambda i: (at // rows, 0)), out_shape=_sds(buf.shape, buf.dtype),
        input_output_aliases={1: 0}, compiler_params=_cp("arbitrary"), name=name)(src, buf)


def _rmsnorm_fwd(x, w, name, token=None):
    T, D = x.shape
    tm = min(T, 512)
    has_token = token is not None

    def body(*refs):
        x_ref, w_ref, o_ref = refs[0], refs[1], refs[-1]
        xv = x_ref[...]
        if has_token:
            xv = xv + refs[2][0:1, 0:1]
        r = lax.rsqrt(jnp.mean(xv * xv, axis=-1, keepdims=True) + NORM_EPS)
        o_ref[...] = (xv * r * w_ref[...]).astype(o_ref.dtype)

    in_specs = [pl.BlockSpec((tm, D), lambda i: (i, 0)), pl.BlockSpec((1, D), lambda i: (0, 0))]
    args = [x, w.reshape(1, D)]
    if has_token:
        in_specs.append(pl.BlockSpec((SUBLANES, LANES), lambda i: (0, 0)))
        args.append(token)
    return pl.pallas_call(
        body, grid=(T // tm,), in_specs=in_specs,
        out_specs=pl.BlockSpec((tm, D), lambda i: (i, 0)), out_shape=_sds((T, D), MXU),
        compiler_params=_cp("parallel"), name=name)(*args)


def _loss_head(x, w, target, name):
    T, D = x.shape
    tm = min(T, 512)

    def body(x_ref, w_ref, t_ref, loss_ref, dx_ref, dw_ref):
        xv = x_ref[...]
        r = lax.rsqrt(jnp.mean(xv * xv, axis=-1, keepdims=True) + NORM_EPS)
        xh = xv * r
        wv = w_ref[...]
        e = xh * wv - t_ref[...]
        lpart = 0.5 * jnp.sum(jnp.mean(e * e, axis=-1, keepdims=True), axis=0, keepdims=True)
        dy = e * (1.0 / D)
        g = dy * wv
        dx_ref[...] = r * (g - xh * jnp.mean(g * xh, axis=-1, keepdims=True))
        part = jnp.sum(dy * xh, axis=0, keepdims=True)
        lrow = jnp.broadcast_to(lpart, (1, LANES))

        @pl.when(pl.program_id(0) == 0)
        def _():
            dw_ref[...] = part
            loss_ref[...] = lrow

        @pl.when(pl.program_id(0) > 0)
        def _():
            dw_ref[...] += part
            loss_ref[...] += lrow

    row = pl.BlockSpec((tm, D), lambda i: (i, 0))
    vec = pl.BlockSpec((1, D), lambda i: (0, 0))
    return pl.pallas_call(
        body, grid=(T // tm,), in_specs=[row, vec, row],
        out_specs=[pl.BlockSpec((1, LANES), lambda i: (0, 0)), row, vec],
        out_shape=[_sds((1, LANES)), _sds((T, D)), _sds((1, D))],
        compiler_params=_cp("arbitrary"), name=name)(x, w.reshape(1, D), target)


def _shift_down(cur, prev8, s):
    if s == 0:
        return cur
    tm = cur.shape[0]
    rc = pltpu.roll(cur, s, 0)
    top = jnp.where(_iota((SUBLANES, cur.shape[1]), 0) < s, pltpu.roll(prev8, s, 0), rc[:SUBLANES])
    return jnp.concatenate([top, rc[SUBLANES:]], axis=0) if tm > SUBLANES else top


def _shift_up(cur, next8, s):
    if s == 0:
        return cur
    tm = cur.shape[0]
    rc = pltpu.roll(cur, tm - s, 0)
    bot = jnp.where(_iota((SUBLANES, cur.shape[1]), 0) >= SUBLANES - s, pltpu.roll(next8, SUBLANES - s, 0), rc[tm - SUBLANES:])
    return jnp.concatenate([rc[:tm - SUBLANES], bot], axis=0) if tm > SUBLANES else bot


def _conv_rows(cur, prev8, w, b, K):
    acc = cur * w[K - 1:K, :] + b
    for s in range(1, K):
        acc = acc + _shift_down(cur, prev8, s) * w[K - 1 - s:K - s, :]
    return acc


FFN_TC = 1408
HALO16 = 2 * SUBLANES


def _ffn_up_conv_gate(hf, wup, cw, cb, name):
    T, D = hf.shape
    tm = min(T, 256)
    nt, nj = T // tm, D_FF // FFN_TC
    K = FFN_CONV
    W2 = 2 * FFN_TC

    def body(a_ref, b_ref, w_ref, c_ref, hid_ref, hc_ref, act_ref, halo):
        i = pl.program_id(1)

        @pl.when(i == 0)
        def _():
            halo[...] = jnp.zeros(halo.shape, F32)

        hb = jnp.dot(a_ref[...].astype(MXU), b_ref[...].astype(MXU), preferred_element_type=F32).astype(hid_ref.dtype)
        hid_ref[...] = hb
        cur = hb.astype(F32)
        hc = _conv_rows(cur, halo[...], w_ref[...], c_ref[...], K)
        halo[...] = cur[tm - SUBLANES:]
        hc_ref[...] = hc
        act_ref[...] = (_silu(hc[:, FFN_TC:]) * hc[:, :FFN_TC]).astype(act_ref.dtype)

    blk = pl.BlockSpec((tm, W2), lambda j, i: (i, j))
    return pl.pallas_call(
        body, grid=(nj, nt),
        in_specs=[pl.BlockSpec((tm, D), lambda j, i: (i, 0)), pl.BlockSpec((D, W2), lambda j, i: (0, j)),
                  pl.BlockSpec((K, W2), lambda j, i: (0, j)), pl.BlockSpec((1, W2), lambda j, i: (0, j))],
        out_specs=[blk, blk, pl.BlockSpec((tm, FFN_TC), lambda j, i: (i, j))],
        out_shape=[_sds((T, 2 * D_FF), MXU), _sds((T, 2 * D_FF)), _sds((T, D_FF), MXU)],
        scratch_shapes=[pltpu.VMEM((SUBLANES, W2), F32)],
        compiler_params=_cp("arbitrary", "arbitrary"), name=name)(hf, wup, cw, cb)


def _ffn_down_dx_mid_bwd(dxo, wdn, hid, hc, cw, name):
    T, D = dxo.shape
    tm = min(T, 256)
    nt, nj = T // tm, D_FF // FFN_TC
    K = FFN_CONV
    W2 = 2 * FFN_TC

    def body(g_ref, wd_ref, h_ref, c_ref, w_ref, dh_ref, dw_ref, db_ref, ahead):
        i = pl.program_id(1)

        @pl.when(i == 0)
        def _():
            ahead[...] = jnp.zeros(ahead.shape, F32)

        w = w_ref[...]
        cur = h_ref[...].astype(F32)
        hcv = c_ref[...]
        dav = _nt(g_ref[...], wd_ref[...])
        u, g = hcv[:, :FFN_TC], hcv[:, FFN_TC:]
        d_cur = jnp.concatenate([dav * _silu(g), dav * u * _dsilu(g)], axis=1)
        d_nxt = ahead[...]
        ahead[...] = d_cur[:SUBLANES]
        ups = [d_cur] + [_shift_up(d_cur, d_nxt, s) for s in range(1, K)]
        dh = ups[0] * w[K - 1:K, :]
        for s in range(1, K):
            dh = dh + ups[s] * w[K - 1 - s:K - s, :]
        dh_ref[...] = dh.astype(dh_ref.dtype)
        dwp = jnp.concatenate([jnp.sum(ups[K - 1 - k] * cur, axis=0, keepdims=True) for k in range(K)], axis=0)
        dbp = jnp.sum(d_cur, axis=0, keepdims=True)

        @pl.when(i == 0)
        def _():
            dw_ref[...] = dwp
            db_ref[...] = dbp

        @pl.when(i > 0)
        def _():
            dw_ref[...] += dwp
            db_ref[...] += dbp

    blk = pl.BlockSpec((tm, W2), lambda j, i: (nt - 1 - i, j))
    return pl.pallas_call(
        body, grid=(nj, nt),
        in_specs=[pl.BlockSpec((tm, D), lambda j, i: (nt - 1 - i, 0)), pl.BlockSpec((FFN_TC, D), lambda j, i: (j, 0)), blk, blk,
                  pl.BlockSpec((K, W2), lambda j, i: (0, j))],
        out_specs=[blk, pl.BlockSpec((K, W2), lambda j, i: (0, j)), pl.BlockSpec((1, W2), lambda j, i: (0, j))],
        out_shape=[_sds((T, 2 * D_FF), MXU), _sds((K, 2 * D_FF)), _sds((1, 2 * D_FF))],
        scratch_shapes=[pltpu.VMEM((SUBLANES, W2), F32)],
        compiler_params=_cp("arbitrary", "arbitrary"), name=name)(dxo, wdn, hid, hc, cw)


def _rope(t, cos, sin_s, inverse=False):
    n = t.shape[1] // LANES
    c = jnp.concatenate([cos] * n, axis=1) if n > 1 else cos
    s = jnp.concatenate([sin_s] * n, axis=1) if n > 1 else sin_s
    a = pltpu.roll(t, HEAD_DIM // 2, 1)
    b = pltpu.roll(t, t.shape[1] - HEAD_DIM // 2, 1)
    first = (_iota(t.shape, 1) % HEAD_DIM) < HEAD_DIM // 2
    rot = jnp.where(first, b, a) * s
    return t * c - rot if inverse else t * c + rot


def _stack_heads(t, g):
    return jnp.concatenate([t[:, (GQ * g + r) * HEAD_DIM:(GQ * g + r + 1) * HEAD_DIM] for r in range(GQ)], axis=0)


def _stack_cols(t, g):
    return jnp.concatenate([t[:, GQ * g + r:GQ * g + r + 1] for r in range(GQ)], axis=0)


def _pool_sums(prev, cur, w):
    s = jnp.concatenate([prev, cur], axis=0)
    sh = 1
    while sh < w:
        s = s + pltpu.roll(s, sh, 0)
        sh *= 2
    return s[BLOCK:]


def _nt(a, b):
    return lax.dot_general(a.astype(MXU), b.astype(MXU), (((1,), (1,)), ((), ())), preferred_element_type=F32)


def _tn(a, b):
    return lax.dot_general(a.astype(MXU), b.astype(MXU), (((0,), (0,)), ((), ())), preferred_element_type=F32)


def _nn(a, b):
    return jnp.dot(a.astype(MXU), b.astype(MXU), preferred_element_type=F32)


def _mixcore_fwd(proj, cos, sin_s, pool_w, pool_scale, sinks, name):
    T = proj.shape[0]
    nb = T // BLOCK
    scale = HEAD_DIM ** -0.5

    def body(p_ref, pp_ref, c_ref, s_ref, cp_ref, sp_ref, pw_ref, ps_ref, sk_ref, cat_ref, at_ref, lse_ref):
        i = pl.program_id(0)
        has_prev = i > 0
        cur = p_ref[...]
        prv = jnp.where(has_prev, pp_ref[...], 0.0)
        tpos = (i * BLOCK + _iota((BLOCK, 1), 0) + 1).astype(F32)
        for g, w in enumerate(POOL_WINDOWS):
            sl = slice(g * POOL_GROUP, (g + 1) * POOL_GROUP)
            pooled = _pool_sums(prv[:, sl], cur[:, sl], w) / jnp.minimum(tpos, float(w)) - cur[:, sl]
            cat_ref[:, sl] = (_nn(pooled, pw_ref[g]) * ps_ref[:, sl]).astype(cat_ref.dtype)
        q = _rope(cur[:, POOL_DIM:POOL_DIM + Q_DIM], c_ref[...], s_ref[...])
        kc = _rope(cur[:, POOL_DIM + Q_DIM:POOL_DIM + Q_DIM + KV_DIM], c_ref[...], s_ref[...])
        kp = _rope(prv[:, POOL_DIM + Q_DIM:POOL_DIM + Q_DIM + KV_DIM], cp_ref[...], sp_ref[...])
        vc = cur[:, POOL_DIM + Q_DIM + KV_DIM:]
        vp = prv[:, POOL_DIM + Q_DIM + KV_DIM:]
        ri = _iota((GQ * BLOCK, BLOCK), 0) % BLOCK
        cj = _iota((GQ * BLOCK, BLOCK), 1)
        mc = cj <= ri
        mp = jnp.logical_and(cj > ri, has_prev)
        outs, lses = [], []
        for g in range(N_KV_HEADS):
            hs = slice(g * HEAD_DIM, (g + 1) * HEAD_DIM)
            qg = _stack_heads(q, g) * scale
            sc = jnp.where(mc, _nt(qg, kc[:, hs]), NEG)
            sp = jnp.where(mp, _nt(qg, kp[:, hs]), NEG)
            sink = jnp.concatenate([jnp.full((BLOCK, 1), sk_ref[GQ * g + r], F32) for r in range(GQ)], axis=0)
            m = jnp.maximum(jnp.maximum(jnp.max(sc, axis=1, keepdims=True), jnp.max(sp, axis=1, keepdims=True)), sink)
            pc = jnp.exp(sc - m)
            pp = jnp.exp(sp - m)
            den = jnp.sum(pc, axis=1, keepdims=True) + jnp.sum(pp, axis=1, keepdims=True) + jnp.exp(sink - m)
            o = (_nn(pc, vc[:, hs]) + _nn(pp, vp[:, hs])) / den
            lse = m + jnp.log(den)
            for r in range(GQ):
                outs.append(o[r * BLOCK:(r + 1) * BLOCK])
                lses.append(lse[r * BLOCK:(r + 1) * BLOCK])
        attn = jnp.concatenate(outs, axis=1)
        at_ref[...] = attn
        cat_ref[:, POOL_DIM:] = attn.astype(cat_ref.dtype)
        lane = _iota((BLOCK, LANES), 1)
        lrow = jnp.zeros((BLOCK, LANES), F32)
        for h in range(N_HEADS):
            lrow = jnp.where(lane == h, lses[h], lrow)
        lse_ref[...] = lrow

    cur = lambda w: pl.BlockSpec((BLOCK, w), lambda i: (i, 0))
    prv = lambda w: pl.BlockSpec((BLOCK, w), lambda i: (jnp.maximum(i - 1, 0), 0))
    return pl.pallas_call(
        body, grid=(nb,),
        in_specs=[cur(MIX_IN_DIM), prv(MIX_IN_DIM), cur(LANES), cur(LANES), prv(LANES), prv(LANES),
                  pl.BlockSpec((4, POOL_GROUP, POOL_GROUP), lambda i: (0, 0, 0)), pl.BlockSpec((1, POOL_DIM), lambda i: (0, 0)),
                  pl.BlockSpec(memory_space=pltpu.SMEM)],
        out_specs=[cur(2 * POOL_DIM), cur(Q_DIM), cur(LANES)],
        out_shape=[_sds((T, 2 * POOL_DIM), MXU), _sds((T, Q_DIM)), _sds((T, LANES))],
        compiler_params=_cp("parallel"), name=name)(proj, proj, cos, sin_s, cos, sin_s, pool_w, pool_scale, sinks)


def _mixcore_bwd(proj, cos, sin_s, pool_w, pool_scale, sinks, attn, lse, dcat, name):
    T = proj.shape[0]
    nb = T // BLOCK
    scale = HEAD_DIM ** -0.5
    QO, KO, VO = POOL_DIM, POOL_DIM + Q_DIM, POOL_DIM + Q_DIM + KV_DIM

    def body(p_ref, pp_ref, pn_ref, c_ref, s_ref, cp_ref, sp_ref, cn_ref, sn_ref, pw_ref, ps_ref, sk_ref,
             at_ref, atn_ref, l_ref, ln_ref, d_ref, dn_ref, dp_ref, dpw_ref, dps_ref, dsk_ref):
        i = pl.program_id(0)
        has_prev = i > 0
        has_next = i < nb - 1
        cur = p_ref[...]
        prv = jnp.where(has_prev, pp_ref[...], 0.0)
        d_cur = d_ref[...]
        d_nxt = jnp.where(has_next, dn_ref[...], 0.0)

        tpos = (i * BLOCK + _iota((BLOCK, 1), 0) + 1).astype(F32)
        tpos2 = (i * BLOCK + _iota((2 * BLOCK, 1), 0) + 1).astype(F32)
        ps = ps_ref[...]
        dps_parts, dpw_parts = [], []
        for g, w in enumerate(POOL_WINDOWS):
            sl = slice(g * POOL_GROUP, (g + 1) * POOL_GROUP)
            pooled = _pool_sums(prv[:, sl], cur[:, sl], w) / jnp.minimum(tpos, float(w)) - cur[:, sl]
            mixed = _nn(pooled, pw_ref[g])
            dps_parts.append(jnp.sum(d_cur[:, sl] * mixed, axis=0, keepdims=True))
            dm2 = jnp.concatenate([d_cur[:, sl], d_nxt[:, sl]], axis=0) * ps[:, sl]
            dpw_parts.append(_tn(pooled, dm2[:BLOCK]))
            dpool2 = _nt(dm2, pw_ref[g])
            e = dpool2 / jnp.minimum(tpos2, float(w))
            sh = 1
            while sh < w:
                e = e + pltpu.roll(e, 2 * BLOCK - sh, 0)
                sh *= 2
            dp_ref[:, sl] = (e[:BLOCK] - dpool2[:BLOCK]).astype(dp_ref.dtype)
        dpsp = jnp.concatenate(dps_parts, axis=1)

        nxt = pn_ref[...]
        q = _rope(cur[:, QO:KO], c_ref[...], s_ref[...])
        qn = _rope(nxt[:, QO:KO], cn_ref[...], sn_ref[...])
        kc = _rope(cur[:, KO:VO], c_ref[...], s_ref[...])
        kp = _rope(prv[:, KO:VO], cp_ref[...], sp_ref[...])
        vc, vp = cur[:, VO:], prv[:, VO:]
        do, don = d_cur[:, POOL_DIM:], d_nxt[:, POOL_DIM:]
        dl = do * at_ref[...]
        dln = don * atn_ref[...]
        lse, lsen = l_ref[...], ln_ref[...]
        ri = _iota((GQ * BLOCK, BLOCK), 0) % BLOCK
        cj = _iota((GQ * BLOCK, BLOCK), 1)
        mc = cj <= ri
        mp = jnp.logical_and(cj > ri, has_prev)
        mn = jnp.logical_and(cj > ri, has_next)
        dq_parts, dk_parts, dv_parts, dsk_vals = [], [], [], []
        for g in range(N_KV_HEADS):
            hs = slice(g * HEAD_DIM, (g + 1) * HEAD_DIM)
            qg, qng = _stack_heads(q, g) * scale, _stack_heads(qn, g) * scale
            dog, dong = _stack_heads(do, g), _stack_heads(don, g)
            delta = jnp.sum(_stack_heads(dl, g), axis=1, keepdims=True)
            deltan = jnp.sum(_stack_heads(dln, g), axis=1, keepdims=True)
            lg, lng = _stack_cols(lse, g), _stack_cols(lsen, g)
            pc = jnp.where(mc, jnp.exp(_nt(qg, kc[:, hs]) - lg), 0.0)
            pp = jnp.where(mp, jnp.exp(_nt(qg, kp[:, hs]) - lg), 0.0)
            pn = jnp.where(mn, jnp.exp(_nt(qng, kc[:, hs]) - lng), 0.0)
            dsc = pc * (_nt(dog, vc[:, hs]) - delta)
            dsp = pp * (_nt(dog, vp[:, hs]) - delta)
            dsn = pn * (_nt(dong, vc[:, hs]) - deltan)
            dqg = (_nn(dsc, kc[:, hs]) + _nn(dsp, kp[:, hs])) * scale
            dq_parts += [dqg[r * BLOCK:(r + 1) * BLOCK] for r in range(GQ)]
            dk_parts.append(_tn(dsc, qg) + _tn(dsn, qng))
            dv_parts.append(_tn(pc, dog) + _tn(pn, dong))
            sink = jnp.concatenate([jnp.full((BLOCK, 1), sk_ref[GQ * g + r], F32) for r in range(GQ)], axis=0)
            dsk = -jnp.exp(sink - lg) * delta
            dsk_vals += [jnp.sum(dsk[r * BLOCK:(r + 1) * BLOCK], axis=0, keepdims=True) for r in range(GQ)]
        dq = _rope(jnp.concatenate(dq_parts, axis=1), c_ref[...], s_ref[...], inverse=True)
        dk = _rope(jnp.concatenate(dk_parts, axis=1), c_ref[...], s_ref[...], inverse=True)
        dp_ref[:, QO:KO] = dq.astype(dp_ref.dtype)
        dp_ref[:, KO:VO] = dk.astype(dp_ref.dtype)
        dp_ref[:, VO:] = jnp.concatenate(dv_parts, axis=1).astype(dp_ref.dtype)
        lane = _iota((1, LANES), 1)
        dskp = jnp.zeros((1, LANES), F32)
        for h in range(N_HEADS):
            dskp = jnp.where(lane == h, dsk_vals[h], dskp)

        @pl.when(i == 0)
        def _():
            dps_ref[...] = dpsp
            dsk_ref[...] = dskp
            for g in range(4):
                dpw_ref[g] = dpw_parts[g]

        @pl.when(i > 0)
        def _():
            dps_ref[...] += dpsp
            dsk_ref[...] += dskp
            for g in range(4):
                dpw_ref[g] += dpw_parts[g]

    cur = lambda w: pl.BlockSpec((BLOCK, w), lambda i: (i, 0))
    prv = lambda w: pl.BlockSpec((BLOCK, w), lambda i: (jnp.maximum(i - 1, 0), 0))
    nxt = lambda w: pl.BlockSpec((BLOCK, w), lambda i: (jnp.minimum(i + 1, nb - 1), 0))
    return pl.pallas_call(
        body, grid=(nb,),
        in_specs=[cur(MIX_IN_DIM), prv(MIX_IN_DIM), nxt(MIX_IN_DIM),
                  cur(LANES), cur(LANES), prv(LANES), prv(LANES), nxt(LANES), nxt(LANES),
                  pl.BlockSpec((4, POOL_GROUP, POOL_GROUP), lambda i: (0, 0, 0)), pl.BlockSpec((1, POOL_DIM), lambda i: (0, 0)),
                  pl.BlockSpec(memory_space=pltpu.SMEM),
                  cur(Q_DIM), nxt(Q_DIM), cur(LANES), nxt(LANES), cur(2 * POOL_DIM), nxt(2 * POOL_DIM)],
        out_specs=[cur(MIX_IN_DIM), pl.BlockSpec((4, POOL_GROUP, POOL_GROUP), lambda i: (0, 0, 0)),
                   pl.BlockSpec((1, POOL_DIM), lambda i: (0, 0)), pl.BlockSpec((1, LANES), lambda i: (0, 0))],
        out_shape=[_sds((T, MIX_IN_DIM), MXU), _sds((4, POOL_GROUP, POOL_GROUP)), _sds((1, POOL_DIM)), _sds((1, LANES))],
        compiler_params=_cp("arbitrary"), name=name)(
            proj, proj, proj, cos, sin_s, cos, sin_s, cos, sin_s, pool_w, pool_scale, sinks, attn, attn, lse, lse, dcat, dcat)


GROUP_W = SSM_D_INNER // SSM_GROUPS


def _ssm_in_conv(h, wT, cw, cb, name):
    T, D = h.shape
    tm = min(T, 256)
    tc = 1024
    K = SSM_CONV

    def body(a_ref, b_ref, w_ref, c_ref, x_ref, pre_ref, act_ref, halo):
        @pl.when(pl.program_id(1) == 0)
        def _():
            halo[...] = jnp.zeros(halo.shape, F32)

        cur = _nt(a_ref[...], b_ref[...])
        x_ref[...] = cur
        pre = _conv_rows(cur, halo[...], w_ref[...], c_ref[...], K)
        halo[...] = cur[tm - SUBLANES:]
        pre_ref[...] = pre
        act_ref[...] = _silu(pre)

    blk = pl.BlockSpec((tm, tc), lambda j, i: (i, j))
    return pl.pallas_call(
        body, grid=(SSM_CONV_DIM // tc, T // tm),
        in_specs=[pl.BlockSpec((tm, D), lambda j, i: (i, 0)), pl.BlockSpec((tc, D), lambda j, i: (j, 0)),
                  pl.BlockSpec((K, tc), lambda j, i: (0, j)), pl.BlockSpec((1, tc), lambda j, i: (0, j))],
        out_specs=[blk, blk, blk], out_shape=[_sds((T, SSM_CONV_DIM))] * 3,
        scratch_shapes=[pltpu.VMEM((SUBLANES, tc), F32)],
        compiler_params=_cp("arbitrary", "arbitrary"), name=name)(h, wT, cw, cb)


def _dot_hi(a, b):
    return jnp.dot(a, b, precision=HI, preferred_element_type=F32)


def _ssd_common(dtraw, bias, alog):
    L = SSM_CHUNK
    xb = dtraw + bias
    dt = jnp.maximum(xb, 0.0) + jnp.log1p(jnp.exp(-jnp.abs(xb)))
    A = -jnp.exp(alog)
    tril = (_iota((L, L), 1) <= _iota((L, L), 0)).astype(F32)
    acs = _dot_hi(tril, dt * A)
    return xb, dt, A, tril, acs


def _head_selectors():
    es = (_iota((LANES, SSM_D_INNER), 0) == _iota((LANES, SSM_D_INNER), 1) // HEAD_DIM).astype(BF16)
    est = (_iota((SSM_D_INNER, LANES), 1) == _iota((SSM_D_INNER, LANES), 0) // HEAD_DIM).astype(BF16)
    return es, est


def _dot_sel(v, sel):
    hi = v.astype(BF16)
    r1 = v - hi.astype(F32)
    mid = r1.astype(BF16)
    lo = (r1 - mid.astype(F32)).astype(BF16)
    d = lambda a: jnp.dot(a, sel, preferred_element_type=F32)
    return (d(hi) + d(mid)) + d(lo)


def _expand_heads(v, es):
    return _dot_sel(v, es)


def _reduce_heads(q, est):
    return _dot_sel(q, est)


def _per_state_row(v, g):
    return jnp.concatenate([jnp.broadcast_to(v[:, GQ * g + r:GQ * g + r + 1], (HEAD_DIM, 1)) for r in range(GQ)], axis=0)


def _ssd_fwd(xact, dtraw, dt_bias, a_log, z, d_skip, nw, name):
    T = xact.shape[0]
    nc = T // SSM_CHUNK
    L = SSM_CHUNK
    BO, CO = SSM_D_INNER, SSM_D_INNER + SSM_GROUPS * SSM_STATE

    def body(x_ref, dt_ref, bias_ref, al_ref, es_ref, z_ref, dsk_ref, nw_ref, y_ref, st_ref, yn_ref, state):
        @pl.when(pl.program_id(0) == 0)
        def _():
            state[...] = jnp.zeros(state.shape, F32)

        _, dt, A, tril, acs = _ssd_common(dt_ref[...], bias_ref[...], al_ref[...])
        acsT = acs.T
        last = acs[L - 1:L, :]
        cd = jnp.exp(last)
        es = es_ref[...]
        dtX = _expand_heads(dt, es)
        EX = _expand_heads(jnp.exp(acs), es)
        decX = _expand_heads(jnp.exp(last - acs), es)
        for g in range(SSM_GROUPS):
            gs = slice(g * GROUP_W, (g + 1) * GROUP_W)
            B = x_ref[:, BO + g * SSM_STATE:BO + (g + 1) * SSM_STATE]
            C = x_ref[:, CO + g * SSM_STATE:CO + (g + 1) * SSM_STATE]
            X = x_ref[:, gs] * dtX[:, gs]
            CB = _nt(C, B)
            yd = []
            for r in range(GQ):
                h = GQ * g + r
                Lm = jnp.exp(jnp.where(tril > 0, acs[:, h:h + 1] - acsT[h:h + 1, :], NEG))
                yd.append(_nn(CB * Lm, X[:, r * HEAD_DIM:(r + 1) * HEAD_DIM]))
            S = state[g]
            st_ref[g] = S
            y_ref[:, gs] = jnp.concatenate(yd, axis=1) + _nt(C, S) * EX[:, gs]
            state[g] = S * _per_state_row(cd, g) + _tn(X * decX[:, gs], B)
        y2 = (y_ref[...] + dsk_ref[...] * x_ref[:, :SSM_D_INNER]) * _silu(z_ref[...])
        r = lax.rsqrt(jnp.mean(y2 * y2, axis=-1, keepdims=True) + SSM_NORM_EPS)
        yn_ref[...] = (y2 * r * nw_ref[...]).astype(yn_ref.dtype)

    es, _ = _head_selectors()
    row = pl.BlockSpec((L, SSM_D_INNER), lambda c: (c, 0))
    vec = pl.BlockSpec((1, SSM_D_INNER), lambda c: (0, 0))
    return pl.pallas_call(
        body, grid=(nc,),
        in_specs=[pl.BlockSpec((L, SSM_CONV_DIM), lambda c: (c, 0)), pl.BlockSpec((L, LANES), lambda c: (c, 0)),
                  pl.BlockSpec((1, LANES), lambda c: (0, 0)), pl.BlockSpec((1, LANES), lambda c: (0, 0)),
                  pl.BlockSpec((LANES, SSM_D_INNER), lambda c: (0, 0)), row, vec, vec],
        out_specs=[row, pl.BlockSpec((None, SSM_GROUPS, GROUP_W, SSM_STATE), lambda c: (c, 0, 0, 0)), row],
        out_shape=[_sds((T, SSM_D_INNER)), _sds((nc, SSM_GROUPS, GROUP_W, SSM_STATE)), _sds((T, SSM_D_INNER), MXU)],
        scratch_shapes=[pltpu.VMEM((SSM_GROUPS, GROUP_W, SSM_STATE), F32)],
        compiler_params=_cp("arbitrary"), name=name)(xact, dtraw, dt_bias, a_log, es, z, d_skip, nw)


def _ssd_bwd(xact, xbc, xpre, cw, dtraw, dt_bias, a_log, d_skip, states, dy, name):
    T = xact.shape[0]
    nc = T // SSM_CHUNK
    L = SSM_CHUNK
    K = SSM_CONV
    BO, CO = SSM_D_INNER, SSM_D_INNER + SSM_GROUPS * SSM_STATE

    def body(x_ref, xin_ref, pre_ref, cw_ref, dt_ref, bias_ref, al_ref, dsk_ref, es_ref, est_ref, st_ref, dy_ref,
             dxbc_ref, dcw_ref, dcb_ref, ddt_ref, dbias_ref, dal_ref, dd_ref, dstate, qa, qx, dxp_ref, ahead):
        cc = pl.program_id(0)

        @pl.when(cc == 0)
        def _():
            dstate[...] = jnp.zeros(dstate.shape, F32)
            ahead[...] = jnp.zeros(ahead.shape, F32)

        xb, dt, A, tril, acs = _ssd_common(dt_ref[...], bias_ref[...], al_ref[...])
        acsT = acs.T
        last = acs[L - 1:L, :]
        cd = jnp.exp(last)
        es, est = es_ref[...], est_ref[...]
        dtX = _expand_heads(dt, es)
        EX = _expand_heads(jnp.exp(acs), es)
        decX = _expand_heads(jnp.exp(last - acs), es)
        lane1 = _iota((1, LANES), 1)
        lane = _iota((L, LANES), 1)
        sub = _iota((L, LANES), 0)
        ztot = jnp.zeros((1, LANES), F32)
        wrow = jnp.zeros((L, LANES), F32)
        wcolT = jnp.zeros((LANES, L), F32)
        rows_dec, rows_dd = [], []
        for g in range(SSM_GROUPS):
            gs = slice(g * GROUP_W, (g + 1) * GROUP_W)
            x = x_ref[:, gs]
            B = x_ref[:, BO + g * SSM_STATE:BO + (g + 1) * SSM_STATE]
            C = x_ref[:, CO + g * SSM_STATE:CO + (g + 1) * SSM_STATE]
            dY = dy_ref[:, gs]
            dtx, e_x, dec_x = dtX[:, gs], EX[:, gs], decX[:, gs]
            X = x * dtx
            CB = _nt(C, B)
            S = st_ref[g]
            dS_out = dstate[g]
            dcb_sum = jnp.zeros((L, L), F32)
            dxd = []
            for r in range(GQ):
                h = GQ * g + r
                hs = slice(r * HEAD_DIM, (r + 1) * HEAD_DIM)
                Lm = jnp.exp(jnp.where(tril > 0, acs[:, h:h + 1] - acsT[h:h + 1, :], NEG))
                M = CB * Lm
                dM = _nt(dY[:, hs], X[:, hs])
                dxd.append(_tn(M, dY[:, hs]))
                dcb_sum = dcb_sum + dM * Lm
                Wm = dM * M
                wrow = jnp.where(lane == h, jnp.sum(Wm, axis=1, keepdims=True), wrow)
                wcolT = jnp.where(sub == h, jnp.sum(Wm, axis=0, keepdims=True), wcolT)
            dXd = jnp.concatenate(dxd, axis=1)
            G = _nt(C, S)
            dG = dY * e_x
            dDX = _nt(B, dS_out)
            dX = dXd + dec_x * dDX
            t_dec = dDX * X * dec_x
            qa[:, gs] = dG * G - t_dec
            qx[:, gs] = dX * x
            rows_dec.append(jnp.sum(t_dec, axis=0, keepdims=True))
            rows_dd.append(jnp.sum(dY * x, axis=0, keepdims=True))
            zc = jnp.sum(dS_out * S, axis=1, keepdims=True)
            for r in range(GQ):
                ztot = jnp.where(lane1 == GQ * g + r, jnp.sum(zc[r * HEAD_DIM:(r + 1) * HEAD_DIM], axis=0, keepdims=True), ztot)
            dxp_ref[:, gs] = dX * dtx + dY * dsk_ref[:, gs]
            dxp_ref[:, BO + g * SSM_STATE:BO + (g + 1) * SSM_STATE] = _tn(dcb_sum, C) + _nn(X * dec_x, dS_out)
            dxp_ref[:, CO + g * SSM_STATE:CO + (g + 1) * SSM_STATE] = _nn(dcb_sum, B) + _nn(dG, S)
            dstate[g] = dS_out * _per_state_row(cd, g) + _tn(dG, C)
        rows = jnp.concatenate([jnp.concatenate(rows_dec, axis=1), jnp.concatenate(rows_dd, axis=1)]
                               + [jnp.zeros((SUBLANES - 2, SSM_D_INNER), F32)], axis=0)
        rsum = _reduce_heads(rows, est)
        dlast = rsum[0:1, :] + cd * ztot
        dacs = (wrow - wcolT.T) + _reduce_heads(qa[...], est) + jnp.where(sub == L - 1, dlast, 0.0)
        triu = (_iota((L, L), 0) <= _iota((L, L), 1)).astype(F32)
        da = _dot_hi(triu, dacs)
        ddtraw = (da * A + _reduce_heads(qx[...], est)) * (1.0 / (1.0 + jnp.exp(-xb)))
        ddt_ref[...] = ddtraw
        dal = jnp.sum(da * dt, axis=0, keepdims=True) * A
        ddp = rsum[1:2, :]
        dbp = jnp.sum(ddtraw, axis=0, keepdims=True)
        w = cw_ref[...]
        d_cur = dxp_ref[...] * _dsilu(pre_ref[...])
        d_nxt = ahead[...]
        ahead[...] = d_cur[:SUBLANES]
        ups = [d_cur] + [_shift_up(d_cur, d_nxt, s) for s in range(1, K)]
        dxc = ups[0] * w[K - 1:K, :]
        for s in range(1, K):
            dxc = dxc + ups[s] * w[K - 1 - s:K - s, :]
        dxbc_ref[...] = dxc.astype(dxbc_ref.dtype)
        xin = xin_ref[...]
        dcwp = jnp.concatenate([jnp.sum(ups[K - 1 - k] * xin, axis=0, keepdims=True) for k in range(K)], axis=0)
        dcbp = jnp.sum(d_cur, axis=0, keepdims=True)

        @pl.when(cc == 0)
        def _():
            dbias_ref[...] = dbp
            dal_ref[...] = dal
            dd_ref[...] = ddp
            dcw_ref[...] = dcwp
            dcb_ref[...] = dcbp

        @pl.when(cc > 0)
        def _():
            dbias_ref[...] += dbp
            dal_ref[...] += dal
            dd_ref[...] += ddp
            dcw_ref[...] += dcwp
            dcb_ref[...] += dcbp

    rc = lambda c: nc - 1 - c
    vec = pl.BlockSpec((1, LANES), lambda c: (0, 0))
    wide = pl.BlockSpec((L, SSM_CONV_DIM), lambda c: (rc(c), 0))
    es, est = _head_selectors()
    return pl.pallas_call(
        body, grid=(nc,),
        in_specs=[wide, wide, wide, pl.BlockSpec((K, SSM_CONV_DIM), lambda c: (0, 0)),
                  pl.BlockSpec((L, LANES), lambda c: (rc(c), 0)), vec, vec,
                  pl.BlockSpec((1, SSM_D_INNER), lambda c: (0, 0)),
                  pl.BlockSpec((LANES, SSM_D_INNER), lambda c: (0, 0)), pl.BlockSpec((SSM_D_INNER, LANES), lambda c: (0, 0)),
                  pl.BlockSpec((None, SSM_GROUPS, GROUP_W, SSM_STATE), lambda c: (rc(c), 0, 0, 0)),
                  pl.BlockSpec((L, SSM_D_INNER), lambda c: (rc(c), 0))],
        out_specs=[wide, pl.BlockSpec((K, SSM_CONV_DIM), lambda c: (0, 0)), pl.BlockSpec((1, SSM_CONV_DIM), lambda c: (0, 0)),
                   pl.BlockSpec((L, LANES), lambda c: (rc(c), 0)), vec, vec, vec],
        out_shape=[_sds((T, SSM_CONV_DIM), MXU), _sds((K, SSM_CONV_DIM)), _sds((1, SSM_CONV_DIM)),
                   _sds((T, LANES)), _sds((1, LANES)), _sds((1, LANES)), _sds((1, LANES))],
        scratch_shapes=[pltpu.VMEM((SSM_GROUPS, GROUP_W, SSM_STATE), F32), pltpu.VMEM((L, SSM_D_INNER), F32),
                        pltpu.VMEM((L, SSM_D_INNER), F32), pltpu.VMEM((L, SSM_CONV_DIM), F32),
                        pltpu.VMEM((SUBLANES, SSM_CONV_DIM), F32)],
        compiler_params=_cp("arbitrary"), name=name)(xact, xbc, xpre, cw, dtraw, dt_bias, a_log, d_skip, es, est, states, dy)


def _ssm_post_bwd(y, xact, z, d_skip, nw, dyn, name):
    T = y.shape[0]
    tm = min(T, 256)
    W = SSM_D_INNER

    def body(y_ref, x_ref, z_ref, d_ref, w_ref, dn_ref, dyg_ref, dz_ref, dw_ref):
        zv = z_ref[...]
        sz = _silu(zv)
        yg = y_ref[...] + d_ref[...] * x_ref[...]
        y2 = yg * sz
        r = lax.rsqrt(jnp.mean(y2 * y2, axis=-1, keepdims=True) + SSM_NORM_EPS)
        y2h = y2 * r
        dn = dn_ref[...]
        gy = dn * w_ref[...]
        dy2 = r * (gy - y2h * jnp.mean(gy * y2h, axis=-1, keepdims=True))
        dyg_ref[...] = dy2 * sz
        dz_ref[...] = (dy2 * yg * _dsilu(zv)).astype(dz_ref.dtype)
        part = jnp.sum(dn * y2h, axis=0, keepdims=True)

        @pl.when(pl.program_id(0) == 0)
        def _():
            dw_ref[...] = part

        @pl.when(pl.program_id(0) > 0)
        def _():
            dw_ref[...] += part

    row = pl.BlockSpec((tm, W), lambda i: (i, 0))
    vec = pl.BlockSpec((1, W), lambda i: (0, 0))
    return pl.pallas_call(
        body, grid=(T // tm,), in_specs=[row, row, row, vec, vec, row], out_specs=[row, row, vec],
        out_shape=[_sds((T, W)), _sds((T, W), MXU), _sds((1, W))],
        compiler_params=_cp("arbitrary"), name=name)(y, xact, z, d_skip, nw, dyn)


def _local_step(x0, cos, sin_s, target, P, fetch, token, send):
    mmf = functools.partial(_mm, tm=1024)
    big, small = {}, {}
    P = dict(P, wup={}, wdn={}, fcw={})
    h0 = _rmsnorm_fwd(x0, P["nm"][0], "norm_mix0", token=token)
    proj0 = mmf(h0, P["wmiT"], tb=True, tn=1280, tk=1024, name="mix_in")
    cat, attn, lse = _mixcore_fwd(proj0, cos, sin_s, P["pool_w"], P["pool_scale"], P["sinks"], "mixcore_fwd")
    x1, hf0 = mmf(cat, P["wmo"], tn=1024, tk=1024, res=x0, norm_w=P["nf"][0], name="mix_out")

    def ffn_fwd(xin, hf, i, next_norm):
        got = fetch(f"ffn{i}", hf)
        P["wup"][i], P["wdn"][i], P["fcw"][i] = got["wup"], got["wdn"], got["fcw"]
        hid, hc, act = _ffn_up_conv_gate(hf, P["wup"][i], P["fcw"][i], P["fcb"][i], f"ffn_up{i}")
        xout = mmf(act, P["wdn"][i], tn=1024, tk=D_FF, res=xin, norm_w=next_norm, name=f"ffn_down{i}")
        return (hid, hc), act, xout

    hid0, act0, (x2, h1) = ffn_fwd(x1, hf0, 0, P["nm"][1])
    P.update(fetch("ssm", h1))
    z = mmf(h1, P["wzT"], tb=True, tn=1024, tk=1024, name="ssm_in_z")
    xbc, xpre, xact = _ssm_in_conv(h1, P["wxbcT"], P["scw"], P["scb"], "ssm_in_xbc")
    dtraw = mmf(h1, P["wdtT"], tb=True, tn=128, tk=1024, name="ssm_in_dt")
    y, states, yn = _ssd_fwd(xact, dtraw, P["dt_bias"], P["a_log"], z, P["d_exp"], P["snorm"], "ssd_fwd")
    x3, hf1 = mmf(yn, P["wso"], tn=1024, tk=SSM_D_INNER, res=x2, norm_w=P["nf"][1], name="ssm_out")
    hid1, act1, x4 = ffn_fwd(x3, hf1, 1, None)
    loss_row, dx4, d_nfin = _loss_head(x4, P["nfin"], target, "loss_head")
    small["norm_final"] = d_nfin

    def ffn_bwd(xin, dxo, hf, hid, act, i):
        big[f"ffn_w_down{i}"] = dwf(act, dxo, tm=1408, tn=1024, name=f"ffn_down_dw{i}").reshape(N_CHIPS, D_FF // N_CHIPS, D_MODEL)
        dhid, dcw, dcb = _ffn_down_dx_mid_bwd(dxo, P["wdn"][i], hid[0], hid[1], P["fcw"][i], f"ffn_down_dx{i}")
        big[f"ffn_w_up{i}"] = dwf(hf, dhid, tm=1024, tn=1408, out_shard_perm=(0, 2, 1, 3), name=f"ffn_up_dw{i}")
        tok = send(f"ffn{i}", [big[f"ffn_w_up{i}"], big[f"ffn_w_down{i}"]])
        dxi, dnf = _mm(dhid, P["wup"][i], tb=True, tm=512, tn=1024, tk=2816, norm_bwd=(xin, P["nf"][i], dxo, tok), name=f"ffn_up_dx{i}")
        return dxi, dnf, dcw, dcb

    dwf = functools.partial(_mm, ta=True, tk=2048, out_dtype=BF16)
    dx3, dnf1, dfcw1, dfcb1 = ffn_bwd(x3, dx4, hf1, hid1, act1, 1)
    dyn = mmf(dx3, P["wso"], tb=True, tn=1024, tk=1024, name="ssm_out_dx")
    big["ssm_w_out"] = dwf(yn, dx3, tm=1024, tn=1024, name="ssm_out_dw").reshape(N_CHIPS, SSM_D_INNER // N_CHIPS, D_MODEL)
    dyg, dz, d_snorm = _ssm_post_bwd(y, xact, z, P["d_exp"], P["snorm"], dyn, "ssm_post_bwd")
    dxbc, d_scw, d_scb, ddtraw, d_dtb, d_alog, d_dskip = _ssd_bwd(
        xact, xbc, xpre, P["scw"], dtraw, P["dt_bias"], P["a_log"], P["d_exp"], states, dyg, "ssd_bwd")
    dwsi = dwf(dz, h1, tm=1024, tn=1024, out_into=(None, SSM_IN_DIM, 0), name="ssm_in_dw_z")
    dwsi = dwf(dxbc, h1, tm=1024, tn=1024, out_into=(dwsi, SSM_IN_DIM, SSM_D_INNER // 1024), name="ssm_in_dw_xbc")
    dwdt = dwf(ddtraw, h1, tm=128, tn=1024, name="ssm_in_dw_dt")
    dwsi = _put_rows(dwsi, dwdt, SSM_HEADS, SSM_D_INNER + SSM_CONV_DIM, "ssm_in_dw_put_dt")
    big["ssm_w_in"] = dwsi.reshape(N_CHIPS, SSM_IN_DIM // N_CHIPS, D_MODEL)
    tok = send("ssm", [big["ssm_w_in"], big["ssm_w_out"]])
    dh1 = mmf(dz, P["wzT"], tn=1024, tk=2048, name="ssm_in_dx_z")
    dh1 = mmf(dxbc, P["wxbcT"], tn=1024, tk=2048, res=dh1, name="ssm_in_dx_xbc")
    dx2, dnm1 = mmf(ddtraw, P["wdtT"], tn=1024, tk=128, res=dh1, norm_bwd=(x2, P["nm"][1], dx3, tok), name="ssm_in_dx_dt")
    dx1, dnf0, dfcw0, dfcb0 = ffn_bwd(x1, dx2, hf0, hid0, act0, 0)
    dcat = mmf(dx1, P["wmo"], tb=True, tn=1024, tk=1024, name="mix_out_dx")
    big["mix_w_out"] = dwf(cat, dx1, tm=1024, tn=1024, name="mix_out_dw").reshape(N_CHIPS, D_MODEL // N_CHIPS, D_MODEL)
    dproj0, d_pw, d_ps, d_sk = _mixcore_bwd(proj0, cos, sin_s, P["pool_w"], P["pool_scale"], P["sinks"], attn, lse, dcat, "mixcore_bwd")
    big["mix_w_in"] = dwf(dproj0, h0, tm=1280, tn=1024, name="mix_in_dw").reshape(N_CHIPS, MIX_IN_DIM // N_CHIPS, D_MODEL)
    tok = send("mix", [big["mix_w_in"], big["mix_w_out"]])
    dx0, dnm0 = mmf(dproj0, P["wmiT"], tn=1024, tk=1280, norm_bwd=(x0, P["nm"][0], dx1, tok), name="mix_in_dx")

    def unperm_cols(a):
        r = a.shape[0]
        t = a.reshape(r, N_CHIPS, FFN_TC)
        return jnp.stack([t[:, p] for p in _PERM], axis=0)

    small["norm_mix"] = jnp.concatenate([dnm0, dnm1], axis=0)
    small["norm_ffn"] = jnp.concatenate([dnf0, dnf1], axis=0)
    small["pool_w"] = d_pw.reshape(4 * POOL_GROUP, POOL_GROUP)
    small["pool_scale"] = d_ps
    small["attn_sinks"] = d_sk
    small["ssm_dt_bias"] = d_dtb
    small["ssm_A_log"] = d_alog
    small["ssm_D"] = d_dskip
    fcb = jnp.stack([unperm_cols(dfcb0), unperm_cols(dfcb1)], axis=0)
    small["ffn_conv_b"] = fcb.reshape(2, 2 * D_FF)
    small["ssm_conv_w"] = d_scw.reshape(SSM_CONV, N_CHIPS, SSM_CONV_DIM // N_CHIPS).transpose(1, 0, 2)
    small["ssm_conv_b"] = d_scb.reshape(N_CHIPS, 1, SSM_CONV_DIM // N_CHIPS)
    small["ssm_norm"] = d_snorm.reshape(N_CHIPS, 1, SSM_D_INNER // N_CHIPS)
    small["ffn_conv_w"] = jnp.concatenate([unperm_cols(dfcw0), unperm_cols(dfcw1)], axis=1)
    return loss_row, dx0, big, small


ANY = pl.BlockSpec(memory_space=pl.ANY)


def _place():
    return lax.axis_index("x"), lax.axis_index("y"), lax.axis_index("c")


def _gather_shards(shards, name):
    n = len(shards)
    split = [s.size >= (1 << 16) for s in shards]

    def half(ref, a, h):
        shp = shards[a].shape
        if len(shp) == 3:
            return ref.at[h]
        r2 = shp[0] // 2
        return ref.at[pl.ds(pl.multiple_of(h * r2, 2 * SUBLANES), r2), :]

    def body(*refs):
        ins, outs = refs[:n], refs[n:2 * n]
        send, recv, fsend, frecv = refs[2 * n:]
        x, y, c = _place()
        k = 2 * x + y
        chips = [(1 - x, y), (x, 1 - y), (1 - x, 1 - y)]

        def ici(a, j, src_slot_ref, dst_slot):
            px, py = chips[j]
            src = half(src_slot_ref, a, c) if split[a] else src_slot_ref
            dst = half(outs[a].at[dst_slot], a, c) if split[a] else outs[a].at[dst_slot]
            return pltpu.make_async_remote_copy(src, dst, send.at[a, j], recv.at[a, j], device_id=(px, py, c), device_id_type=MESH)

        def d2d(a, j, h):
            px, py = chips[j]
            part = half(outs[a].at[2 * px + py], a, h)
            return pltpu.make_async_remote_copy(part, part, fsend.at[a, j], frecv.at[a, j], device_id=(x, y, 1 - c), device_id_type=MESH)

        sends = [ici(a, j, ins[a], k) for a in range(n) for j in range(3)]
        for cp in sends:
            cp.start()
        passed = []
        for a in range(n):
            for j, (px, py) in enumerate(chips):
                ici(a, j, ins[a], 2 * px + py).wait_recv()
                if split[a]:
                    passed.append(d2d(a, j, c))
                    passed[-1].start()
        for a in range(n):
            if split[a]:
                for j in range(3):
                    d2d(a, j, 1 - c).wait_recv()
        for cp in sends + passed:
            cp.wait_send()

    return pl.pallas_call(
        body, in_specs=[ANY] * n, out_specs=[ANY] * n,
        out_shape=[_sds((N_CHIPS,) + s.shape, s.dtype) for s in shards],
        scratch_shapes=[pltpu.SemaphoreType.DMA((n, 3))] * 4,
        compiler_params=pltpu.CompilerParams(has_side_effects=True), name=name)(*shards)


HBM = pl.BlockSpec(memory_space=pltpu.HBM)
SEM = pl.BlockSpec(memory_space=pltpu.SEMAPHORE)
DATAFLOW = pltpu.SideEffectType.DATAFLOW_SIDE_EFFECTING


def _row_half(ref, h):
    r2 = ref.shape[0] // 2
    return ref.at[pl.ds(pl.multiple_of(h * r2, 2 * SUBLANES), r2), :]


def _spread_start(groups, slot_src, after, name, halved=()):
    flat = [a for grp in groups for a in grp]
    n = len(flat)
    ng = len(groups)
    offs = [sum(len(g) for g in groups[:i]) for i in range(ng)]
    lshape = [(a.shape if slot_src else (N_CHIPS,) + a.shape) for a in flat]

    nsem = 6 * n

    def body(*refs):
        src, land = refs[:n], refs[n:2 * n]
        sems = refs[2 * n + 1:2 * n + 1 + nsem]
        token = refs[-1]
        x, y, c = _place()
        k = 2 * x + y
        chips = [(1 - x, y), (x, 1 - y), (1 - x, 1 - y)]
        for a in range(n):
            half = any(offs[gi] <= a < offs[gi] + len(groups[gi]) for gi in halved)
            for j, (px, py) in enumerate(chips):
                s = src[a].at[2 * px + py] if slot_src else src[a]
                d = land[a].at[k]
                if half:
                    s, d = _row_half(s, c), _row_half(d, c)
                pltpu.make_async_remote_copy(s, d, sems[6 * a + 2 * j], sems[6 * a + 2 * j + 1],
                                             device_id=(px, py, c), device_id_type=MESH).start()
        token[...] = jnp.zeros(token.shape, token.dtype)

    out_shape = [pltpu.SemaphoreType.DMA(())] * nsem
    out_shape += [pltpu.HBM(a.shape, a.dtype) for a in flat] + [pltpu.HBM(s, a.dtype) for s, a in zip(lshape, flat)]
    out_shape.append(_sds((SUBLANES, LANES)))
    args = [pltpu.with_memory_space_constraint(a, pltpu.HBM) for a in flat]
    args += [pltpu.with_memory_space_constraint(lax.empty(s, a.dtype), pltpu.HBM) for s, a in zip(lshape, flat)]
    res = pl.pallas_call(
        body, name=name, out_shape=tuple(out_shape), in_specs=[HBM] * (2 * n) + [pl.BlockSpec(memory_space=pl.ANY)],
        out_specs=tuple([SEM] * nsem + [HBM] * (2 * n) + [pl.BlockSpec(memory_space=pltpu.VMEM)]),
        input_output_aliases={i: nsem + i for i in range(2 * n)},
        compiler_params=pltpu.CompilerParams(has_side_effects=DATAFLOW))(*args, after)
    sems, thru, token = res[:nsem], res[nsem:nsem + 2 * n], res[-1]
    out = []
    for gi, grp in enumerate(groups):
        sl = slice(offs[gi], offs[gi] + len(grp))
        out.append((list(sems[6 * offs[gi]:6 * (offs[gi] + len(grp))]), list(thru[:n][sl]), list(thru[n:][sl])))
    return out, token


def _spread_wait(started, slot_src, after, name, halved=False):
    sems, srcs, lands = started
    n = len(srcs)

    def body(*refs):
        src, land = refs[:n], refs[n:2 * n]
        sem = refs[2 * n:2 * n + 6 * n]
        x, y, c = _place()
        chips = [(1 - x, y), (x, 1 - y), (1 - x, 1 - y)]
        for a in range(n):
            for j, (px, py) in enumerate(chips):
                s = src[a].at[2 * px + py] if slot_src else src[a]
                d = land[a].at[2 * px + py]
                if halved:
                    s, d = _row_half(s, c), _row_half(d, c)
                cp = pltpu.make_async_remote_copy(s, d, sem[6 * a + 2 * j], sem[6 * a + 2 * j + 1],
                                                  device_id=(px, py, c), device_id_type=MESH)
                cp.wait_send()
                cp.wait_recv()

    res = pl.pallas_call(
        body, name=name, out_shape=tuple([pltpu.HBM(a.shape, a.dtype) for a in srcs] + [pltpu.HBM(a.shape, a.dtype) for a in lands]),
        in_specs=[HBM] * (2 * n) + [SEM] * (6 * n) + [pl.BlockSpec(memory_space=pl.ANY)], out_specs=tuple([HBM] * (2 * n)),
        input_output_aliases={i: i for i in range(2 * n)},
        compiler_params=pltpu.CompilerParams(has_side_effects=DATAFLOW))(*srcs, *lands, *sems, after)
    return list(res[:n]), list(res[n:])


def _sibling_fill(lands, name):
    n = len(lands)

    def body(*refs):
        bufs = refs[n:2 * n]
        send, recv = refs[2 * n:]
        x, y, c = _place()
        chips = [(1 - x, y), (x, 1 - y), (1 - x, 1 - y)]

        def copy(a, j, h):
            px, py = chips[j]
            part = _row_half(bufs[a].at[2 * px + py], h)
            return pltpu.make_async_remote_copy(part, part, send.at[a, j], recv.at[a, j], device_id=(x, y, 1 - c), device_id_type=MESH)

        sends = [copy(a, j, c) for a in range(n) for j in range(3)]
        for cp in sends:
            cp.start()
        for a in range(n):
            for j in range(3):
                copy(a, j, 1 - c).wait_recv()
        for cp in sends:
            cp.wait_send()

    return pl.pallas_call(
        body, in_specs=[ANY] * n, out_specs=[ANY] * n, out_shape=[_sds(t.shape, t.dtype) for t in lands],
        input_output_aliases={i: i for i in range(n)},
        scratch_shapes=[pltpu.SemaphoreType.DMA((n, 3)), pltpu.SemaphoreType.DMA((n, 3))],
        compiler_params=pltpu.CompilerParams(has_side_effects=True), name=name)(*lands)


def _sibling_exchange(fs, name):
    n = len(fs)

    def body(*refs):
        ins, outs = refs[:n], refs[n:2 * n]
        send, recv = refs[2 * n:]
        x, y, c = _place()
        cps = [pltpu.make_async_remote_copy(ins[a], outs[a], send.at[a], recv.at[a],
                                            device_id=(x, y, 1 - c), device_id_type=MESH) for a in range(n)]
        for cp in cps:
            cp.start()
        for cp in cps:
            cp.wait()

    return pl.pallas_call(
        body, in_specs=[ANY] * n, out_specs=[ANY] * n, out_shape=[_sds(f.shape, f.dtype) for f in fs],
        scratch_shapes=[pltpu.SemaphoreType.DMA((n,)), pltpu.SemaphoreType.DMA((n,))],
        compiler_params=pltpu.CompilerParams(has_side_effects=True), name=name)(*fs)


def _tile2d(rows, cols, budget=1024 * 1024, step=2 * SUBLANES):
    fits = [t for t in range(step, rows + 1, step) if rows % t == 0 and t * cols * 4 <= budget]
    if fits:
        return fits[-1], cols
    fits = [t for t in range(LANES, cols + 1, LANES) if cols % t == 0 and rows * t * 4 <= budget]
    assert fits, (rows, cols)
    return rows, fits[-1]


def _chip_sum(own, parts, kidx, name):
    _, R, C = parts.shape
    tr, tc = _tile2d(R, C)

    def body(k_ref, o_ref_in, p1_ref, p2_ref, p3_ref, o_ref):
        tot = ((o_ref_in[...].astype(F32) + p1_ref[...].astype(F32)) + p2_ref[...].astype(F32)) + p3_ref[...].astype(F32)
        o_ref[...] = tot.astype(o_ref.dtype)

    def slot(d):
        return pl.BlockSpec((None, tr, tc), lambda i, j, k: ((k[0] + d) % N_CHIPS, i, j))

    return pl.pallas_call(
        body,
        grid_spec=pltpu.PrefetchScalarGridSpec(
            num_scalar_prefetch=1, grid=(R // tr, C // tc), in_specs=[slot(0), slot(1), slot(2), slot(3)],
            out_specs=pl.BlockSpec((tr, tc), lambda i, j, k: (i, j))),
        out_shape=_sds((R, C), BF16), compiler_params=_cp("parallel", "parallel"), name=name)(kidx, own, parts, parts, parts)


def _adamw_math(w, g, m, v):
    m2 = ADAM_B1 * m + (1.0 - ADAM_B1) * g
    v2 = ADAM_B2 * v + (1.0 - ADAM_B2) * (g * g)
    m_hat = m2 / (1.0 - ADAM_B1 ** ADAM_STEP)
    v_hat = v2 / (1.0 - ADAM_B2 ** ADAM_STEP)
    delta = -ADAM_LR * (m_hat / (jnp.sqrt(v_hat) + ADAM_EPS) + ADAM_WD * w)
    return delta, m2, v2


def _adamw(w, m, v, gparts, name):
    Lw, R, C = w.shape
    tr, tc = _tile2d(R, C)
    flat = [h for pair in gparts for h in pair]

    def body(*refs):
        w_ref, m_ref, v_ref = refs[:3]
        g_refs = refs[3:3 + 2 * Lw]
        go_ref, d_ref, mo_ref, vo_ref = refs[3 + 2 * Lw:]
        g = g_refs[0][...].astype(F32) + g_refs[1][...].astype(F32)
        for l in range(1, Lw):
            g = jnp.where(pl.program_id(0) == l, g_refs[2 * l][...].astype(F32) + g_refs[2 * l + 1][...].astype(F32), g)
        d, m2, v2 = _adamw_math(w_ref[...], g, m_ref[...], v_ref[...])
        go_ref[...] = g
        d_ref[...] = d
        mo_ref[...] = m2
        vo_ref[...] = v2

    blk = pl.BlockSpec((None, tr, tc), lambda l, i, j: (l, i, j))
    gblk = pl.BlockSpec((tr, tc), lambda l, i, j: (i, j))
    return pl.pallas_call(
        body, grid=(Lw, R // tr, C // tc), in_specs=[blk, blk, blk] + [gblk] * (2 * Lw), out_specs=[blk] * 4,
        out_shape=[_sds((Lw, R, C))] * 4, compiler_params=_cp("parallel", "parallel", "parallel"), name=name)(w, m, v, *flat)


def _small_adamw(grads, wmv, name):
    n = len(grads)

    def body(*refs):
        g_in, p_in, outs = refs[:n], refs[n:4 * n], refs[4 * n:]
        for a in range(n):
            g = g_in[a][...]
            d_, m2, v2 = _adamw_math(p_in[3 * a][...], g, p_in[3 * a + 1][...], p_in[3 * a + 2][...])
            outs[4 * a][...] = g
            outs[4 * a + 1][...] = d_
            outs[4 * a + 2][...] = m2
            outs[4 * a + 3][...] = v2

    vm = pl.BlockSpec(memory_space=pltpu.VMEM)
    args = list(grads) + [t for tri in wmv for t in tri]
    out_shape = [_sds(g.shape) for g in grads for _ in range(4)]
    return pl.pallas_call(body, in_specs=[vm] * len(args), out_specs=[vm] * len(out_shape), out_shape=out_shape,
                          compiler_params=pltpu.CompilerParams(vmem_limit_bytes=V7X_VMEM_LIMIT), name=name)(*args)


def _small_allreduce(partials, pshapes, loss_row, name):
    n = len(partials)
    gshapes = [p.shape for p in partials] + [loss_row.shape]
    ng = n + 1

    def body(*refs):
        g_in = refs[:ng]
        outs = refs[ng:2 * ng]
        sib = refs[2 * ng:3 * ng]
        pair = refs[3 * ng:4 * ng]
        bufs = refs[4 * ng:5 * ng]
        send1, recv1, send2, recv2 = refs[-4:]
        x, y, c = _place()
        k = 2 * x + y
        chips = [(1 - x, y), (x, 1 - y), (1 - x, 1 - y)]
        swaps = [pltpu.make_async_remote_copy(g_in[a], sib[a], send1.at[a], recv1.at[a],
                                              device_id=(x, y, 1 - c), device_id_type=MESH) for a in range(ng)]
        for cp in swaps:
            cp.start()
        for a, cp in enumerate(swaps):
            cp.wait()
            pair[a][...] = g_in[a][...] + sib[a][...]
            bufs[a][k] = pair[a][...]
        sends = [pltpu.make_async_remote_copy(pair[a], bufs[a].at[k], send2.at[a, j], recv2.at[a, j],
                                              device_id=(px, py, c), device_id_type=MESH)
                 for a in range(ng) for j, (px, py) in enumerate(chips)]
        for cp in sends:
            cp.start()
        for a in range(ng):
            for j, (px, py) in enumerate(chips):
                pltpu.make_async_remote_copy(pair[a], bufs[a].at[2 * px + py], send2.at[a, j], recv2.at[a, j],
                                             device_id=(px, py, c), device_id_type=MESH).wait_recv()
        for cp in sends:
            cp.wait_send()
        for a in range(ng):
            sharded = len(gshapes[a]) == 3

            def part(d):
                return bufs[a][d, k] if sharded else bufs[a][d]

            tot = part(0)
            for d in range(1, N_CHIPS):
                tot = tot + part(d)
            if a == n:
                outs[n][...] = tot
            else:
                pr, pc = pshapes[a]
                outs[a][...] = tot[:pr, :pc]

    vm = pl.BlockSpec(memory_space=pltpu.VMEM)
    args = list(partials) + [loss_row]
    out_shape = [_sds(ps) for ps in pshapes] + [_sds(loss_row.shape)]
    return pl.pallas_call(
        body, in_specs=[vm] * len(args), out_specs=[vm] * len(out_shape), out_shape=out_shape,
        scratch_shapes=[pltpu.VMEM(tuple(s), F32) for s in gshapes] * 2 + [pltpu.VMEM((N_CHIPS,) + tuple(s), F32) for s in gshapes]
        + [pltpu.SemaphoreType.DMA((ng,)), pltpu.SemaphoreType.DMA((ng,)),
           pltpu.SemaphoreType.DMA((ng, 3)), pltpu.SemaphoreType.DMA((ng, 3))],
        compiler_params=pltpu.CompilerParams(has_side_effects=True, vmem_limit_bytes=V7X_VMEM_LIMIT), name=name)(*args)


_PERM = (0, 2, 1, 3)


def _cols_from_shards(g):
    return g.transpose(1, 0, 2).reshape(g.shape[1], N_CHIPS * g.shape[2])


def _rope_tables(positions):
    inv_freq = ROPE_THETA ** (-jnp.arange(0, HEAD_DIM, 2, dtype=F32) / HEAD_DIM)
    ang = positions.astype(F32).reshape(-1, 1) * inv_freq
    cos, sin = jnp.cos(ang), jnp.sin(ang)
    cos = jnp.concatenate([cos, cos, cos, cos], axis=-1)
    sin_s = jnp.concatenate([-sin, sin, -sin, sin], axis=-1)
    return cos, sin_s


def kernel(x, positions, norm_mix, norm_ffn, norm_final, mix_w_in, pool_w, pool_scale, attn_sinks, mix_w_out, ssm_w_in, ssm_conv_w, ssm_conv_b, ssm_dt_bias, ssm_A_log, ssm_D, ssm_norm, ssm_w_out, ffn_w_up, ffn_conv_w, ffn_conv_b, ffn_w_down, loss_target, m_norm_mix, m_norm_ffn, m_norm_final, m_mix_w_in, m_pool_w, m_pool_scale, m_attn_sinks, m_mix_w_out, m_ssm_w_in, m_ssm_conv_w, m_ssm_conv_b, m_ssm_dt_bias, m_ssm_A_log, m_ssm_D, m_ssm_norm, m_ssm_w_out, m_ffn_w_up, m_ffn_conv_w, m_ffn_conv_b, m_ffn_w_down, v_norm_mix, v_norm_ffn, v_norm_final, v_mix_w_in, v_pool_w, v_pool_scale, v_attn_sinks, v_mix_w_out, v_ssm_w_in, v_ssm_conv_w, v_ssm_conv_b, v_ssm_dt_bias, v_ssm_A_log, v_ssm_D, v_ssm_norm, v_ssm_w_out, v_ffn_w_up, v_ffn_conv_w, v_ffn_conv_b, v_ffn_w_down):
    W = dict(norm_mix=norm_mix, norm_ffn=norm_ffn, norm_final=norm_final, mix_w_in=mix_w_in, pool_w=pool_w, pool_scale=pool_scale, attn_sinks=attn_sinks, mix_w_out=mix_w_out, ssm_w_in=ssm_w_in, ssm_conv_w=ssm_conv_w, ssm_conv_b=ssm_conv_b, ssm_dt_bias=ssm_dt_bias, ssm_A_log=ssm_A_log, ssm_D=ssm_D, ssm_norm=ssm_norm, ssm_w_out=ssm_w_out, ffn_w_up=ffn_w_up, ffn_conv_w=ffn_conv_w, ffn_conv_b=ffn_conv_b, ffn_w_down=ffn_w_down)
    Mo = dict(norm_mix=m_norm_mix, norm_ffn=m_norm_ffn, norm_final=m_norm_final, mix_w_in=m_mix_w_in, pool_w=m_pool_w, pool_scale=m_pool_scale, attn_sinks=m_attn_sinks, mix_w_out=m_mix_w_out, ssm_w_in=m_ssm_w_in, ssm_conv_w=m_ssm_conv_w, ssm_conv_b=m_ssm_conv_b, ssm_dt_bias=m_ssm_dt_bias, ssm_A_log=m_ssm_A_log, ssm_D=m_ssm_D, ssm_norm=m_ssm_norm, ssm_w_out=m_ssm_w_out, ffn_w_up=m_ffn_w_up, ffn_conv_w=m_ffn_conv_w, ffn_conv_b=m_ffn_conv_b, ffn_w_down=m_ffn_w_down)
    Vo = dict(norm_mix=v_norm_mix, norm_ffn=v_norm_ffn, norm_final=v_norm_final, mix_w_in=v_mix_w_in, pool_w=v_pool_w, pool_scale=v_pool_scale, attn_sinks=v_attn_sinks, mix_w_out=v_mix_w_out, ssm_w_in=v_ssm_w_in, ssm_conv_w=v_ssm_conv_w, ssm_conv_b=v_ssm_conv_b, ssm_dt_bias=v_ssm_dt_bias, ssm_A_log=v_ssm_A_log, ssm_D=v_ssm_D, ssm_norm=v_ssm_norm, ssm_w_out=v_ssm_w_out, ffn_w_up=v_ffn_w_up, ffn_conv_w=v_ffn_conv_w, ffn_conv_b=v_ffn_conv_b, ffn_w_down=v_ffn_w_down)

    kchip = 2 * lax.axis_index("x") + lax.axis_index("y")

    def own_slot(g, own):
        return lax.dynamic_update_slice_in_dim(g, own[None], kchip, axis=0)

    def tr(t):
        return jnp.swapaxes(t[0], 0, 1)

    later = dict(ffn0=[ffn_w_up[0].astype(MXU), ffn_w_down[0].astype(MXU)],
                 ssm=[tr(ssm_w_in).astype(MXU), ssm_w_out[0].astype(MXU)],
                 ffn1=[ffn_w_up[1].astype(MXU), ffn_w_down[1].astype(MXU)])
    sh = [tr(mix_w_in).astype(MXU), mix_w_out[0].astype(MXU), ssm_conv_w[0], ssm_conv_b, ssm_norm, ffn_conv_w]
    first = _gather_shards(sh, "gather_first")
    g_mi, g_mo, g_scw, g_scb, g_sn, g_fcw = [own_slot(g, own) for g, own in zip(first, sh)]
    started, token = _spread_start(list(later.values()), False, first[0], "gather_start", halved=(0,))
    started = dict(zip(later.keys(), started))
    fcw = [jnp.concatenate([g_fcw[p, i] for p in _PERM], axis=1) for i in range(2)]
    P = dict(
        nm=norm_mix, nf=norm_ffn, nfin=norm_final,
        wmiT=g_mi.reshape(MIX_IN_DIM, D_MODEL), wmo=g_mo.reshape(D_MODEL, D_MODEL),
        pool_w=pool_w[0], pool_scale=pool_scale, sinks=attn_sinks[0],
        scw=_cols_from_shards(g_scw), scb=g_scb.reshape(1, SSM_CONV_DIM), snorm=g_sn.reshape(1, SSM_D_INNER),
        dt_bias=jnp.pad(ssm_dt_bias, ((0, 0), (0, LANES - SSM_HEADS))), a_log=jnp.pad(ssm_A_log, ((0, 0), (0, LANES - SSM_HEADS))),
        d_exp=jnp.repeat(ssm_D, SSM_D_INNER // SSM_HEADS, axis=1),
        fcb=[jnp.concatenate([ffn_conv_b[i:i + 1, p * FFN_TC:(p + 1) * FFN_TC] for p in _PERM], axis=1) for i in range(2)],
    )

    def fetch(group, after):
        owns, lands = _spread_wait(started[group], False, after, f"gather_wait_{group}", halved=group == "ffn0")
        if group == "ffn0":
            lands = _sibling_fill(lands, "gather_fill_ffn0")
        a, b = [own_slot(g, own) for g, own in zip(lands, owns)]
        if group == "ssm":
            wsi = a.reshape(SSM_IN_DIM, D_MODEL)
            zx = SSM_D_INNER + SSM_CONV_DIM
            return dict(wzT=wsi[:SSM_D_INNER], wxbcT=wsi[SSM_D_INNER:zx],
                        wdtT=jnp.pad(wsi[zx:], ((0, LANES - SSM_HEADS), (0, 0))), wso=b.reshape(SSM_D_INNER, D_MODEL))
        i = int(group[-1])
        return dict(wup=jnp.concatenate([a[p] for p in _PERM], axis=1), wdn=b.reshape(D_FF, D_MODEL), fcw=fcw[i])

    cos, sin_s = _rope_tables(positions)
    sent = {}

    def send(group, grads):
        res, tok = _spread_start([grads], True, jnp.zeros((SUBLANES, LANES), F32), f"grad_start_{group}")
        sent[group] = res[0]
        return tok

    loss_row, grad_x, big, small = _local_step(x[0], cos, sin_s, loss_target[0], P, fetch, token, send)

    kidx = kchip.astype(jnp.int32).reshape(1)
    group_names = dict(ffn1=["ffn_w_up1", "ffn_w_down1"], ssm=["ssm_w_in", "ssm_w_out"], ffn0=["ffn_w_up0", "ffn_w_down0"],
                       mix=["mix_w_in", "mix_w_out"])
    names, mine = [], []
    for group, started_g in sent.items():
        grads, lands = _spread_wait(started_g, True, grad_x, f"grad_wait_{group}")
        for nm, g, land in zip(group_names[group], grads, lands):
            names.append(nm)
            mine.append(_chip_sum(g, land, kidx, f"chip_sum_{nm}"))
    theirs = _sibling_exchange(mine, "sibling_exchange")
    red = {nm: (a, b) for nm, a, b in zip(names, mine, theirs)}

    out = {}

    def big_update(pname, gparts, transposed=False):
        w = W[pname]
        lw = len(gparts)
        shp = w.shape
        rr, cc = gparts[0][0].shape
        fix = (lambda t: tr(t)[None]) if transposed else (lambda t: t.reshape(lw, rr, cc))
        res = _adamw(fix(w), fix(Mo[pname]), fix(Vo[pname]), gparts, f"adamw_{pname}")
        out[pname] = tuple((tr(r)[None] if transposed else r.reshape(shp)) for r in res)

    big_update("mix_w_in", [red["mix_w_in"]], transposed=True)
    big_update("mix_w_out", [red["mix_w_out"]])
    big_update("ssm_w_in", [red["ssm_w_in"]], transposed=True)
    big_update("ssm_w_out", [red["ssm_w_out"]])
    big_update("ffn_w_up", [red["ffn_w_up0"], red["ffn_w_up1"]])
    big_update("ffn_w_down", [red["ffn_w_down0"], red["ffn_w_down1"]])

    small_names = ["norm_mix", "norm_ffn", "norm_final", "pool_w", "pool_scale", "attn_sinks", "ssm_dt_bias", "ssm_A_log",
                   "ssm_D", "ffn_conv_b", "ssm_conv_w", "ssm_conv_b", "ssm_norm", "ffn_conv_w"]

    def as2d(t):
        if t.ndim == 1:
            return t.reshape(1, -1)
        return t.reshape(-1, t.shape[-1])

    wmv = [(as2d(W[nm]), as2d(Mo[nm]), as2d(Vo[nm])) for nm in small_names]
    summed = _small_allreduce([small[nm] for nm in small_names], [t[0].shape for t in wmv], loss_row, "small_allreduce")
    res = _small_adamw(summed[:-1], wmv, "small_adamw")
    for a, nm in enumerate(small_names):
        out[nm] = tuple(r.reshape(W[nm].shape) for r in res[4 * a:4 * a + 4])
    loss = summed[-1][0, 0]

    order = ["norm_mix", "norm_ffn", "norm_final", "mix_w_in", "pool_w", "pool_scale", "attn_sinks", "mix_w_out", "ssm_w_in",
             "ssm_conv_w", "ssm_conv_b", "ssm_dt_bias", "ssm_A_log", "ssm_D", "ssm_norm", "ssm_w_out", "ffn_w_up", "ffn_conv_w",
             "ffn_conv_b", "ffn_w_down"]
    return (loss, grad_x.reshape(x.shape), *[out[nm][0] for nm in order], *[out[nm][1] for nm in order],
            *[out[nm][2] for nm in order], *[out[nm][3] for nm in order])
```

```python
import functools

import jax
import jax.numpy as jnp
from jax import lax
from jax.experimental import pallas as pl
from jax.experimental.pallas import tpu as pltpu

F32 = jnp.float32
BF16 = jnp.bfloat16
MXU = BF16
HI = lax.Precision.HIGHEST

D_MODEL = 1024
POOL_WINDOWS = (2, 4, 8, 16)
POOL_DIM = 512
POOL_GROUP = 128
HEAD_DIM = 64
N_HEADS = 8
N_KV_HEADS = 2
GQ = 4
Q_DIM = 512
KV_DIM = 128
BLOCK = 128
ROPE_THETA = 10000.0
MIX_IN_DIM = 1280
SSM_D_INNER = 2048
SSM_HEADS = 32
SSM_GROUPS = 8
SSM_STATE = 128
SSM_CONV = 4
SSM_CHUNK = 128
SSM_CONV_DIM = 4096
SSM_IN_DIM = 6176
D_FF = 2816
FFN_CONV = 3
NORM_EPS = 1e-6
SSM_NORM_EPS = 1e-5
ADAM_LR = 0.001
ADAM_B1 = 0.9
ADAM_B2 = 0.999
ADAM_EPS = 1e-08
ADAM_WD = 0.01
ADAM_STEP = 10

N_CHIPS = 4
N_DEV = 8
LANES = 128
SUBLANES = 8
V7X_VMEM_LIMIT = 56 * 1024 * 1024
NEG = -1e30
MESH = pl.DeviceIdType.MESH


def _cp(*sem):
    return pltpu.CompilerParams(dimension_semantics=sem if sem else None, vmem_limit_bytes=V7X_VMEM_LIMIT)


def _sds(shape, dtype=F32):
    return jax.ShapeDtypeStruct(tuple(shape), dtype)


def _iota(shape, dim):
    return lax.broadcasted_iota(jnp.int32, shape, dim)


def _silu(x):
    return x * (1.0 / (1.0 + jnp.exp(-x)))


def _dsilu(x):
    s = 1.0 / (1.0 + jnp.exp(-x))
    return s * (1.0 + x * (1.0 - s))


def _mm(a, b, *, ta=False, tb=False, tm, tn, tk, res=None, out_dtype=F32, out_shard_perm=None, out_into=None, b_rows=None,
        norm_w=None, norm_bwd=None, loss_head=None, name):
    M, K = (a.shape[1], a.shape[0]) if ta else a.shape
    N = b.shape[0] if tb else b.shape[1]
    boff = 0
    if b_rows is not None:
        boff = b_rows[0]
        if tb:
            N = b_rows[1]
        else:
            K = b_rows[1]
    tm, tn, tk = min(tm, M), min(tn, N), min(tk, K)
    gm, gn, gk = M // tm, N // tn, K // tk
    assert gm * tm == M and gn * tn == N and gk * tk == K, (name, M, N, K, tm, tn, tk)
    a_spec = pl.BlockSpec((tk, tm), lambda i, j, k: (k, i)) if ta else pl.BlockSpec((tm, tk), lambda i, j, k: (i, k))
    b_spec = pl.BlockSpec((tn, tk), lambda i, j, k: (j + boff, k)) if tb else pl.BlockSpec((tk, tn), lambda i, j, k: (k + boff, j))
    dims = (((0 if ta else 1,), (1 if tb else 0,)), ((), ()))
    has_res = res is not None
    has_nw = norm_w is not None
    has_nb = norm_bwd is not None
    has_lh = loss_head is not None
    has_tok = has_nb and norm_bwd[3] is not None
    assert not (has_nw or has_nb or has_lh) or (gn == 1 and out_shard_perm is None)
    n_extra = has_res + has_nw + (3 + has_tok if has_nb else 0) + (2 if has_lh else 0)

    def body(*refs):
        a_ref, b_ref = refs[0], refs[1]
        extra = list(refs[2:2 + n_extra])
        outs = refs[len(args):]
        r_ref = extra.pop(0) if has_res else None
        nw_ref = extra.pop(0) if has_nw else None
        nb_refs = extra if has_nb else None

        def dot():
            return lax.dot_general(a_ref[...].astype(MXU), b_ref[...].astype(MXU), dims, preferred_element_type=F32)

        def accumulate(o_ref, part):
            i = pl.program_id(0)

            @pl.when(i == 0)
            def _():
                o_ref[...] = part

            @pl.when(i > 0)
            def _():
                o_ref[...] += part

        def finish(r):
            if has_res:
                r = r + r_ref[...]
            if has_lh:
                wv = extra[0][...]
                rs = lax.rsqrt(jnp.mean(r * r, axis=-1, keepdims=True) + NORM_EPS)
                xh = r * rs
                e = xh * wv - extra[1][...]
                lpart = 0.5 * jnp.sum(jnp.mean(e * e, axis=-1, keepdims=True), axis=0, keepdims=True)
                dy = e * (1.0 / N)
                g = dy * wv
                outs[0][...] = rs * (g - xh * jnp.mean(g * xh, axis=-1, keepdims=True))
                accumulate(outs[1], jnp.sum(dy * xh, axis=0, keepdims=True))
                accumulate(outs[2], jnp.broadcast_to(lpart, (1, LANES)))
                return
            if has_nb:
                xv = nb_refs[0][...]
                rs = lax.rsqrt(jnp.mean(xv * xv, axis=-1, keepdims=True) + NORM_EPS)
                xh = xv * rs
                g = r * nb_refs[1][...]
                dr = nb_refs[2][...] + nb_refs[3][0:1, 0:1] if has_tok else nb_refs[2][...]
                outs[0][...] = dr + rs * (g - xh * jnp.mean(g * xh, axis=-1, keepdims=True))
                accumulate(outs[1], jnp.sum(r * xh, axis=0, keepdims=True))
                return
            outs[0][...] = r.astype(out_dtype)
            if has_nw:
                rs = lax.rsqrt(jnp.mean(r * r, axis=-1, keepdims=True) + NORM_EPS)
                outs[1][...] = (r * rs * nw_ref[...]).astype(outs[1].dtype)

        if gk == 1:
            finish(dot())
        else:
            acc = refs[-1]
            k = pl.program_id(2)

            @pl.when(k == 0)
            def _():
                acc[...] = dot()

            if gk > 2:
                @pl.when(jnp.logical_and(k > 0, k < gk - 1))
                def _():
                    acc[...] += dot()

            @pl.when(k == gk - 1)
            def _():
                finish(acc[...] + dot())

    tile = pl.BlockSpec((tm, tn), lambda i, j, k: (i, j))
    row = pl.BlockSpec((1, tn), lambda i, j, k: (0, j))
    in_specs = [a_spec, b_spec]
    args = [a, b]
    if has_res:
        in_specs.append(tile)
        args.append(res)
    if has_nw:
        in_specs.append(row)
        args.append(norm_w.reshape(1, N))
    if has_nb:
        in_specs += [tile, row, tile]
        args += [norm_bwd[0], norm_bwd[1].reshape(1, N), norm_bwd[2]]
        if has_tok:
            in_specs.append(pl.BlockSpec((SUBLANES, LANES), lambda i, j, k: (0, 0)))
            args.append(norm_bwd[3])
    if has_lh:
        in_specs += [row, tile]
        args += [loss_head[0].reshape(1, N), loss_head[1]]
    alias = {}
    if out_into is not None:
        buf, rows, off = out_into
        out_spec = pl.BlockSpec((tm, tn), lambda i, j, k: (i + off, j))
        out_shape = _sds((rows, N), out_dtype)
        if buf is not None:
            alias = {len(args): 0}
            in_specs.append(pl.BlockSpec(memory_space=pl.ANY))
            args.append(buf)
    elif out_shard_perm is None:
        out_spec = tile
        out_shape = _sds((M, N), out_dtype)
    else:
        assert gn == len(out_shard_perm) == 4 and tuple(out_shard_perm) == (0, 2, 1, 3)
        out_spec = pl.BlockSpec((None, tm, tn), lambda i, j, k: ((j % 2) * 2 + j // 2, i, 0))
        out_shape = _sds((gn, M, tn), out_dtype)
    sem = ("parallel", "parallel", "arbitrary")
    if has_nw:
        out_spec, out_shape = [out_spec, tile], [out_shape, _sds((M, N), MXU)]
    if has_nb:
        out_spec, out_shape = [tile, row], [_sds((M, N)), _sds((1, N))]
        sem = ("arbitrary", "arbitrary", "arbitrary")
    if has_lh:
        out_spec = [tile, row, pl.BlockSpec((1, LANES), lambda i, j, k: (0, 0))]
        out_shape = [_sds((M, N)), _sds((1, N)), _sds((1, LANES))]
        sem = ("arbitrary", "arbitrary", "arbitrary")
    return pl.pallas_call(
        body, grid=(gm, gn, gk), in_specs=in_specs, out_specs=out_spec, out_shape=out_shape,
        scratch_shapes=[pltpu.VMEM((tm, tn), F32)] if gk > 1 else [], input_output_aliases=alias,
        compiler_params=_cp(*sem), name=name)(*args)


def _put_rows(buf, src, rows, at, name):
    assert at % rows == 0 and src.shape[1] == buf.shape[1] and src.dtype == buf.dtype
    C = buf.shape[1]

    def body(s_ref, b_ref, o_ref):
        o_ref[...] = s_ref[...]

    return pl.pallas_call(
        body, grid=(1,), in_specs=[pl.BlockSpec((rows, C), lambda i: (0, 0)), pl.BlockSpec(memory_space=pl.ANY)],
        out_specs=pl.BlockSpec((rows, C), lambda i: (at // rows, 0)), out_shape=_sds(buf.shape, buf.dtype),
        input_output_aliases={1: 0}, compiler_params=_cp("arbitrary"), name=name)(src, buf)


def _rmsnorm_fwd(x, w, name, token=None):
    T, D = x.shape
    tm = min(T, 512)
    has_token = token is not None

    def body(*refs):
        x_ref, w_ref, o_ref = refs[0], refs[1], refs[-1]
        xv = x_ref[...]
        if has_token:
            xv = xv + refs[2][0:1, 0:1]
        r = lax.rsqrt(jnp.mean(xv * xv, axis=-1, keepdims=True) + NORM_EPS)
        o_ref[...] = (xv * r * w_ref[...]).astype(o_ref.dtype)

    in_specs = [pl.BlockSpec((tm, D), lambda i: (i, 0)), pl.BlockSpec((1, D), lambda i: (0, 0))]
    args = [x, w.reshape(1, D)]
    if has_token:
        in_specs.append(pl.BlockSpec((SUBLANES, LANES), lambda i: (0, 0)))
        args.append(token)
    return pl.pallas_call(
        body, grid=(T // tm,), in_specs=in_specs,
        out_specs=pl.BlockSpec((tm, D), lambda i: (i, 0)), out_shape=_sds((T, D), MXU),
        compiler_params=_cp("parallel"), name=name)(*args)


def _shift_down(cur, prev8, s):
    if s == 0:
        return cur
    tm = cur.shape[0]
    rc = pltpu.roll(cur, s, 0)
    top = jnp.where(_iota((SUBLANES, cur.shape[1]), 0) < s, pltpu.roll(prev8, s, 0), rc[:SUBLANES])
    return jnp.concatenate([top, rc[SUBLANES:]], axis=0) if tm > SUBLANES else top


def _shift_up(cur, next8, s):
    if s == 0:
        return cur
    tm = cur.shape[0]
    rc = pltpu.roll(cur, tm - s, 0)
    bot = jnp.where(_iota((SUBLANES, cur.shape[1]), 0) >= SUBLANES - s, pltpu.roll(next8, SUBLANES - s, 0), rc[tm - SUBLANES:])
    return jnp.concatenate([rc[:tm - SUBLANES], bot], axis=0) if tm > SUBLANES else bot


def _conv_rows(cur, prev8, w, b, K):
    acc = cur * w[K - 1:K, :] + b
    for s in range(1, K):
        acc = acc + _shift_down(cur, prev8, s) * w[K - 1 - s:K - s, :]
    return acc


FFN_TC = 1408
HALO16 = 2 * SUBLANES


def _ffn_up_conv_gate(hf, wup, cw, cb, name):
    T, D = hf.shape
    tm = min(T, 256)
    nt, nj = T // tm, D_FF // FFN_TC
    K = FFN_CONV
    W2 = 2 * FFN_TC

    def body(a_ref, b_ref, w_ref, c_ref, hid_ref, hc_ref, act_ref, halo):
        i = pl.program_id(1)

        @pl.when(i == 0)
        def _():
            halo[...] = jnp.zeros(halo.shape, F32)

        hb = jnp.dot(a_ref[...].astype(MXU), b_ref[...].astype(MXU), preferred_element_type=F32).astype(hid_ref.dtype)
        hid_ref[...] = hb
        cur = hb.astype(F32)
        hc = _conv_rows(cur, halo[...], w_ref[...], c_ref[...], K)
        halo[...] = cur[tm - SUBLANES:]
        hc_ref[...] = hc
        act_ref[...] = (_silu(hc[:, FFN_TC:]) * hc[:, :FFN_TC]).astype(act_ref.dtype)

    blk = pl.BlockSpec((tm, W2), lambda j, i: (i, j))
    return pl.pallas_call(
        body, grid=(nj, nt),
        in_specs=[pl.BlockSpec((tm, D), lambda j, i: (i, 0)), pl.BlockSpec((D, W2), lambda j, i: (0, j)),
                  pl.BlockSpec((K, W2), lambda j, i: (0, j)), pl.BlockSpec((1, W2), lambda j, i: (0, j))],
        out_specs=[blk, blk, pl.BlockSpec((tm, FFN_TC), lambda j, i: (i, j))],
        out_shape=[_sds((T, 2 * D_FF), MXU), _sds((T, 2 * D_FF)), _sds((T, D_FF), MXU)],
        scratch_shapes=[pltpu.VMEM((SUBLANES, W2), F32)],
        compiler_params=_cp("arbitrary", "arbitrary"), name=name)(hf, wup, cw, cb)


def _ffn_down_dx_mid_bwd(dxo, wdn, hid, hc, cw, name):
    T, D = dxo.shape
    tm = min(T, 256)
    nt, nj = T // tm, D_FF // FFN_TC
    K = FFN_CONV
    W2 = 2 * FFN_TC

    def body(g_ref, wd_ref, h_ref, c_ref, w_ref, dh_ref, dw_ref, db_ref, ahead):
        i = pl.program_id(1)

        @pl.when(i == 0)
        def _():
            ahead[...] = jnp.zeros(ahead.shape, F32)

        w = w_ref[...]
        cur = h_ref[...].astype(F32)
        hcv = c_ref[...]
        dav = _nt(g_ref[...], wd_ref[...])
        u, g = hcv[:, :FFN_TC], hcv[:, FFN_TC:]
        d_cur = jnp.concatenate([dav * _silu(g), dav * u * _dsilu(g)], axis=1)
        d_nxt = ahead[...]
        ahead[...] = d_cur[:SUBLANES]
        ups = [d_cur] + [_shift_up(d_cur, d_nxt, s) for s in range(1, K)]
        dh = ups[0] * w[K - 1:K, :]
        for s in range(1, K):
            dh = dh + ups[s] * w[K - 1 - s:K - s, :]
        dh_ref[...] = dh.astype(dh_ref.dtype)
        dwp = jnp.concatenate([jnp.sum(ups[K - 1 - k] * cur, axis=0, keepdims=True) for k in range(K)], axis=0)
        dbp = jnp.sum(d_cur, axis=0, keepdims=True)

        @pl.when(i == 0)
        def _():
            dw_ref[...] = dwp
            db_ref[...] = dbp

        @pl.when(i > 0)
        def _():
            dw_ref[...] += dwp
            db_ref[...] += dbp

    blk = pl.BlockSpec((tm, W2), lambda j, i: (nt - 1 - i, j))
    return pl.pallas_call(
        body, grid=(nj, nt),
        in_specs=[pl.BlockSpec((tm, D), lambda j, i: (nt - 1 - i, 0)), pl.BlockSpec((FFN_TC, D), lambda j, i: (j, 0)), blk, blk,
                  pl.BlockSpec((K, W2), lambda j, i: (0, j))],
        out_specs=[blk, pl.BlockSpec((K, W2), lambda j, i: (0, j)), pl.BlockSpec((1, W2), lambda j, i: (0, j))],
        out_shape=[_sds((T, 2 * D_FF), MXU), _sds((K, 2 * D_FF)), _sds((1, 2 * D_FF))],
        scratch_shapes=[pltpu.VMEM((SUBLANES, W2), F32)],
        compiler_params=_cp("arbitrary", "arbitrary"), name=name)(dxo, wdn, hid, hc, cw)


def _rope(t, cos, sin_s, inverse=False):
    n = t.shape[1] // LANES
    c = jnp.concatenate([cos] * n, axis=1) if n > 1 else cos
    s = jnp.concatenate([sin_s] * n, axis=1) if n > 1 else sin_s
    a = pltpu.roll(t, HEAD_DIM // 2, 1)
    b = pltpu.roll(t, t.shape[1] - HEAD_DIM // 2, 1)
    first = (_iota(t.shape, 1) % HEAD_DIM) < HEAD_DIM // 2
    rot = jnp.where(first, b, a) * s
    return t * c - rot if inverse else t * c + rot


def _stack_heads(t, g):
    return jnp.concatenate([t[:, (GQ * g + r) * HEAD_DIM:(GQ * g + r + 1) * HEAD_DIM] for r in range(GQ)], axis=0)


def _stack_cols(t, g):
    return jnp.concatenate([t[:, GQ * g + r:GQ * g + r + 1] for r in range(GQ)], axis=0)


def _pool_sums(prev, cur, w):
    s = jnp.concatenate([prev, cur], axis=0)
    sh = 1
    while sh < w:
        s = s + pltpu.roll(s, sh, 0)
        sh *= 2
    return s[BLOCK:]


def _nt(a, b):
    return lax.dot_general(a.astype(MXU), b.astype(MXU), (((1,), (1,)), ((), ())), preferred_element_type=F32)


def _tn(a, b):
    return lax.dot_general(a.astype(MXU), b.astype(MXU), (((0,), (0,)), ((), ())), preferred_element_type=F32)


def _nn(a, b):
    return jnp.dot(a.astype(MXU), b.astype(MXU), preferred_element_type=F32)


def _mixcore_fwd(proj, cos, sin_s, pool_w, pool_scale, sinks, name):
    T = proj.shape[0]
    nb = T // BLOCK
    scale = HEAD_DIM ** -0.5

    def body(p_ref, pp_ref, c_ref, s_ref, cp_ref, sp_ref, pw_ref, ps_ref, sk_ref, cat_ref, at_ref, lse_ref):
        i = pl.program_id(0)
        has_prev = i > 0
        cur = p_ref[...]
        prv = jnp.where(has_prev, pp_ref[...], 0.0)
        tpos = (i * BLOCK + _iota((BLOCK, 1), 0) + 1).astype(F32)
        for g, w in enumerate(POOL_WINDOWS):
            sl = slice(g * POOL_GROUP, (g + 1) * POOL_GROUP)
            pooled = _pool_sums(prv[:, sl], cur[:, sl], w) / jnp.minimum(tpos, float(w)) - cur[:, sl]
            cat_ref[:, sl] = (_nn(pooled, pw_ref[g]) * ps_ref[:, sl]).astype(cat_ref.dtype)
        q = _rope(cur[:, POOL_DIM:POOL_DIM + Q_DIM], c_ref[...], s_ref[...])
        kc = _rope(cur[:, POOL_DIM + Q_DIM:POOL_DIM + Q_DIM + KV_DIM], c_ref[...], s_ref[...])
        kp = _rope(prv[:, POOL_DIM + Q_DIM:POOL_DIM + Q_DIM + KV_DIM], cp_ref[...], sp_ref[...])
        vc = cur[:, POOL_DIM + Q_DIM + KV_DIM:]
        vp = prv[:, POOL_DIM + Q_DIM + KV_DIM:]
        ri = _iota((GQ * BLOCK, BLOCK), 0) % BLOCK
        cj = _iota((GQ * BLOCK, BLOCK), 1)
        mc = cj <= ri
        mp = jnp.logical_and(cj > ri, has_prev)
        outs, lses = [], []
        for g in range(N_KV_HEADS):
            hs = slice(g * HEAD_DIM, (g + 1) * HEAD_DIM)
            qg = _stack_heads(q, g) * scale
            sc = jnp.where(mc, _nt(qg, kc[:, hs]), NEG)
            sp = jnp.where(mp, _nt(qg, kp[:, hs]), NEG)
            sink = jnp.concatenate([jnp.full((BLOCK, 1), sk_ref[GQ * g + r], F32) for r in range(GQ)], axis=0)
            m = jnp.maximum(jnp.maximum(jnp.max(sc, axis=1, keepdims=True), jnp.max(sp, axis=1, keepdims=True)), sink)
            pc = jnp.exp(sc - m)
            pp = jnp.exp(sp - m)
            den = jnp.sum(pc, axis=1, keepdims=True) + jnp.sum(pp, axis=1, keepdims=True) + jnp.exp(sink - m)
            o = (_nn(pc, vc[:, hs]) + _nn(pp, vp[:, hs])) / den
            lse = m + jnp.log(den)
            for r in range(GQ):
                outs.append(o[r * BLOCK:(r + 1) * BLOCK])
                lses.append(lse[r * BLOCK:(r + 1) * BLOCK])
        attn = jnp.concatenate(outs, axis=1)
        at_ref[...] = attn
        cat_ref[:, POOL_DIM:] = attn.astype(cat_ref.dtype)
        lane = _iota((BLOCK, LANES), 1)
        lrow = jnp.zeros((BLOCK, LANES), F32)
        for h in range(N_HEADS):
            lrow = jnp.where(lane == h, lses[h], lrow)
        lse_ref[...] = lrow

    cur = lambda w: pl.BlockSpec((BLOCK, w), lambda i: (i, 0))
    prv = lambda w: pl.BlockSpec((BLOCK, w), lambda i: (jnp.maximum(i - 1, 0), 0))
    return pl.pallas_call(
        body, grid=(nb,),
        in_specs=[cur(MIX_IN_DIM), prv(MIX_IN_DIM), cur(LANES), cur(LANES), prv(LANES), prv(LANES),
                  pl.BlockSpec((4, POOL_GROUP, POOL_GROUP), lambda i: (0, 0, 0)), pl.BlockSpec((1, POOL_DIM), lambda i: (0, 0)),
                  pl.BlockSpec(memory_space=pltpu.SMEM)],
        out_specs=[cur(2 * POOL_DIM), cur(Q_DIM), cur(LANES)],
        out_shape=[_sds((T, 2 * POOL_DIM), MXU), _sds((T, Q_DIM)), _sds((T, LANES))],
        compiler_params=_cp("parallel"), name=name)(proj, proj, cos, sin_s, cos, sin_s, pool_w, pool_scale, sinks)


def _mixcore_bwd(proj, cos, sin_s, pool_w, pool_scale, sinks, attn, lse, dcat, name):
    T = proj.shape[0]
    nb = T // BLOCK
    scale = HEAD_DIM ** -0.5
    QO, KO, VO = POOL_DIM, POOL_DIM + Q_DIM, POOL_DIM + Q_DIM + KV_DIM

    def body(p_ref, pp_ref, pn_ref, c_ref, s_ref, cp_ref, sp_ref, cn_ref, sn_ref, pw_ref, ps_ref, sk_ref,
             at_ref, atn_ref, l_ref, ln_ref, d_ref, dn_ref, dp_ref, dpw_ref, dps_ref, dsk_ref):
        i = pl.program_id(0)
        has_prev = i > 0
        has_next = i < nb - 1
        cur = p_ref[...]
        prv = jnp.where(has_prev, pp_ref[...], 0.0)
        d_cur = d_ref[...]
        d_nxt = jnp.where(has_next, dn_ref[...], 0.0)

        tpos = (i * BLOCK + _iota((BLOCK, 1), 0) + 1).astype(F32)
        tpos2 = (i * BLOCK + _iota((2 * BLOCK, 1), 0) + 1).astype(F32)
        ps = ps_ref[...]
        dps_parts, dpw_parts = [], []
        for g, w in enumerate(POOL_WINDOWS):
            sl = slice(g * POOL_GROUP, (g + 1) * POOL_GROUP)
            pooled = _pool_sums(prv[:, sl], cur[:, sl], w) / jnp.minimum(tpos, float(w)) - cur[:, sl]
            mixed = _nn(pooled, pw_ref[g])
            dps_parts.append(jnp.sum(d_cur[:, sl] * mixed, axis=0, keepdims=True))
            dm2 = jnp.concatenate([d_cur[:, sl], d_nxt[:, sl]], axis=0) * ps[:, sl]
            dpw_parts.append(_tn(pooled, dm2[:BLOCK]))
            dpool2 = _nt(dm2, pw_ref[g])
            e = dpool2 / jnp.minimum(tpos2, float(w))
            sh = 1
            while sh < w:
                e = e + pltpu.roll(e, 2 * BLOCK - sh, 0)
                sh *= 2
            dp_ref[:, sl] = (e[:BLOCK] - dpool2[:BLOCK]).astype(dp_ref.dtype)
        dpsp = jnp.concatenate(dps_parts, axis=1)

        nxt = pn_ref[...]
        q = _rope(cur[:, QO:KO], c_ref[...], s_ref[...])
        qn = _rope(nxt[:, QO:KO], cn_ref[...], sn_ref[...])
        kc = _rope(cur[:, KO:VO], c_ref[...], s_ref[...])
        kp = _rope(prv[:, KO:VO], cp_ref[...], sp_ref[...])
        vc, vp = cur[:, VO:], prv[:, VO:]
        do, don = d_cur[:, POOL_DIM:], d_nxt[:, POOL_DIM:]
        dl = do * at_ref[...]
        dln = don * atn_ref[...]
        lse, lsen = l_ref[...], ln_ref[...]
        ri = _iota((GQ * BLOCK, BLOCK), 0) % BLOCK
        cj = _iota((GQ * BLOCK, BLOCK), 1)
        mc = cj <= ri
        mp = jnp.logical_and(cj > ri, has_prev)
        mn = jnp.logical_and(cj > ri, has_next)
        dq_parts, dk_parts, dv_parts, dsk_vals = [], [], [], []
        for g in range(N_KV_HEADS):
            hs = slice(g * HEAD_DIM, (g + 1) * HEAD_DIM)
            qg, qng = _stack_heads(q, g) * scale, _stack_heads(qn, g) * scale
            dog, dong = _stack_heads(do, g), _stack_heads(don, g)
            delta = jnp.sum(_stack_heads(dl, g), axis=1, keepdims=True)
            deltan = jnp.sum(_stack_heads(dln, g), axis=1, keepdims=True)
            lg, lng = _stack_cols(lse, g), _stack_cols(lsen, g)
            pc = jnp.where(mc, jnp.exp(_nt(qg, kc[:, hs]) - lg), 0.0)
            pp = jnp.where(mp, jnp.exp(_nt(qg, kp[:, hs]) - lg), 0.0)
            pn = jnp.where(mn, jnp.exp(_nt(qng, kc[:, hs]) - lng), 0.0)
            dsc = pc * (_nt(dog, vc[:, hs]) - delta)
            dsp = pp * (_nt(dog, vp[:, hs]) - delta)
            dsn = pn * (_nt(dong, vc[:, hs]) - deltan)
            dqg = (_nn(dsc, kc[:, hs]) + _nn(dsp, kp[:, hs])) * scale
            dq_parts += [dqg[r * BLOCK:(r + 1) * BLOCK] for r in range(GQ)]
            dk_parts.append(_tn(dsc, qg) + _tn(dsn, qng))
            dv_parts.append(_tn(pc, dog) + _tn(pn, dong))
            sink = jnp.concatenate([jnp.full((BLOCK, 1), sk_ref[GQ * g + r], F32) for r in range(GQ)], axis=0)
            dsk = -jnp.exp(sink - lg) * delta
            dsk_vals += [jnp.sum(dsk[r * BLOCK:(r + 1) * BLOCK], axis=0, keepdims=True) for r in range(GQ)]
        dq = _rope(jnp.concatenate(dq_parts, axis=1), c_ref[...], s_ref[...], inverse=True)
        dk = _rope(jnp.concatenate(dk_parts, axis=1), c_ref[...], s_ref[...], inverse=True)
        dp_ref[:, QO:KO] = dq.astype(dp_ref.dtype)
        dp_ref[:, KO:VO] = dk.astype(dp_ref.dtype)
        dp_ref[:, VO:] = jnp.concatenate(dv_parts, axis=1).astype(dp_ref.dtype)
        lane = _iota((1, LANES), 1)
        dskp = jnp.zeros((1, LANES), F32)
        for h in range(N_HEADS):
            dskp = jnp.where(lane == h, dsk_vals[h], dskp)

        @pl.when(i == 0)
        def _():
            dps_ref[...] = dpsp
            dsk_ref[...] = dskp
            for g in range(4):
                dpw_ref[g] = dpw_parts[g]

        @pl.when(i > 0)
        def _():
            dps_ref[...] += dpsp
            dsk_ref[...] += dskp
            for g in range(4):
                dpw_ref[g] += dpw_parts[g]

    cur = lambda w: pl.BlockSpec((BLOCK, w), lambda i: (i, 0))
    prv = lambda w: pl.BlockSpec((BLOCK, w), lambda i: (jnp.maximum(i - 1, 0), 0))
    nxt = lambda w: pl.BlockSpec((BLOCK, w), lambda i: (jnp.minimum(i + 1, nb - 1), 0))
    return pl.pallas_call(
        body, grid=(nb,),
        in_specs=[cur(MIX_IN_DIM), prv(MIX_IN_DIM), nxt(MIX_IN_DIM),
                  cur(LANES), cur(LANES), prv(LANES), prv(LANES), nxt(LANES), nxt(LANES),
                  pl.BlockSpec((4, POOL_GROUP, POOL_GROUP), lambda i: (0, 0, 0)), pl.BlockSpec((1, POOL_DIM), lambda i: (0, 0)),
                  pl.BlockSpec(memory_space=pltpu.SMEM),
                  cur(Q_DIM), nxt(Q_DIM), cur(LANES), nxt(LANES), cur(2 * POOL_DIM), nxt(2 * POOL_DIM)],
        out_specs=[cur(MIX_IN_DIM), pl.BlockSpec((4, POOL_GROUP, POOL_GROUP), lambda i: (0, 0, 0)),
                   pl.BlockSpec((1, POOL_DIM), lambda i: (0, 0)), pl.BlockSpec((1, LANES), lambda i: (0, 0))],
        out_shape=[_sds((T, MIX_IN_DIM), MXU), _sds((4, POOL_GROUP, POOL_GROUP)), _sds((1, POOL_DIM)), _sds((1, LANES))],
        compiler_params=_cp("arbitrary"), name=name)(
            proj, proj, proj, cos, sin_s, cos, sin_s, cos, sin_s, pool_w, pool_scale, sinks, attn, attn, lse, lse, dcat, dcat)


GROUP_W = SSM_D_INNER // SSM_GROUPS


def _ssm_in_conv(h, wT, row_off, cw, cb, name):
    T, D = h.shape
    tm = min(T, 256)
    tc = 1024
    K = SSM_CONV

    def body(a_ref, b_ref, w_ref, c_ref, x_ref, pre_ref, act_ref, halo):
        @pl.when(pl.program_id(1) == 0)
        def _():
            halo[...] = jnp.zeros(halo.shape, F32)

        cur = _nt(a_ref[...], b_ref[...])
        x_ref[...] = cur
        pre = _conv_rows(cur, halo[...], w_ref[...], c_ref[...], K)
        halo[...] = cur[tm - SUBLANES:]
        pre_ref[...] = pre
        act_ref[...] = _silu(pre)

    blk = pl.BlockSpec((tm, tc), lambda j, i: (i, j))
    return pl.pallas_call(
        body, grid=(SSM_CONV_DIM // tc, T // tm),
        in_specs=[pl.BlockSpec((tm, D), lambda j, i: (i, 0)), pl.BlockSpec((tc, D), lambda j, i: (j + row_off // tc, 0)),
                  pl.BlockSpec((K, tc), lambda j, i: (0, j)), pl.BlockSpec((1, tc), lambda j, i: (0, j))],
        out_specs=[blk, blk, blk], out_shape=[_sds((T, SSM_CONV_DIM))] * 3,
        scratch_shapes=[pltpu.VMEM((SUBLANES, tc), F32)],
        compiler_params=_cp("arbitrary", "arbitrary"), name=name)(h, wT, cw, cb)


def _dot_hi(a, b):
    return jnp.dot(a, b, precision=HI, preferred_element_type=F32)


def _ssd_common(dtraw, bias, alog):
    L = SSM_CHUNK
    xb = dtraw + bias
    dt = jnp.maximum(xb, 0.0) + jnp.log1p(jnp.exp(-jnp.abs(xb)))
    A = -jnp.exp(alog)
    tril = (_iota((L, L), 1) <= _iota((L, L), 0)).astype(F32)
    acs = _dot_hi(tril, dt * A)
    return xb, dt, A, tril, acs


def _head_selectors():
    es = (_iota((LANES, SSM_D_INNER), 0) == _iota((LANES, SSM_D_INNER), 1) // HEAD_DIM).astype(BF16)
    est = (_iota((SSM_D_INNER, LANES), 1) == _iota((SSM_D_INNER, LANES), 0) // HEAD_DIM).astype(BF16)
    return es, est


def _dot_sel(v, sel):
    hi = v.astype(BF16)
    r1 = v - hi.astype(F32)
    mid = r1.astype(BF16)
    lo = (r1 - mid.astype(F32)).astype(BF16)
    d = lambda a: jnp.dot(a, sel, preferred_element_type=F32)
    return (d(hi) + d(mid)) + d(lo)


def _expand_heads(v, es):
    return _dot_sel(v, es)


def _reduce_heads(q, est):
    return _dot_sel(q, est)


def _per_state_row(v, g):
    return jnp.concatenate([jnp.broadcast_to(v[:, GQ * g + r:GQ * g + r + 1], (HEAD_DIM, 1)) for r in range(GQ)], axis=0)


def _ssd_fwd(xact, dtraw, dt_bias, a_log, z, d_skip, nw, name):
    T = xact.shape[0]
    nc = T // SSM_CHUNK
    L = SSM_CHUNK
    BO, CO = SSM_D_INNER, SSM_D_INNER + SSM_GROUPS * SSM_STATE

    def body(x_ref, dt_ref, bias_ref, al_ref, es_ref, z_ref, dsk_ref, nw_ref, y_ref, st_ref, yn_ref, state):
        @pl.when(pl.program_id(0) == 0)
        def _():
            state[...] = jnp.zeros(state.shape, F32)

        _, dt, A, tril, acs = _ssd_common(dt_ref[...], bias_ref[...], al_ref[...])
        acsT = acs.T
        last = acs[L - 1:L, :]
        cd = jnp.exp(last)
        es = es_ref[...]
        dtX = _expand_heads(dt, es)
        EX = _expand_heads(jnp.exp(acs), es)
        decX = _expand_heads(jnp.exp(last - acs), es)
        for g in range(SSM_GROUPS):
            gs = slice(g * GROUP_W, (g + 1) * GROUP_W)
            B = x_ref[:, BO + g * SSM_STATE:BO + (g + 1) * SSM_STATE]
            C = x_ref[:, CO + g * SSM_STATE:CO + (g + 1) * SSM_STATE]
            X = x_ref[:, gs] * dtX[:, gs]
            CB = _nt(C, B)
            yd = []
            for r in range(GQ):
                h = GQ * g + r
                Lm = jnp.exp(jnp.where(tril > 0, acs[:, h:h + 1] - acsT[h:h + 1, :], NEG))
                yd.append(_nn(CB * Lm, X[:, r * HEAD_DIM:(r + 1) * HEAD_DIM]))
            S = state[g]
            st_ref[g] = S
            y_ref[:, gs] = jnp.concatenate(yd, axis=1) + _nt(C, S) * EX[:, gs]
            state[g] = S * _per_state_row(cd, g) + _tn(X * decX[:, gs], B)
        y2 = (y_ref[...] + dsk_ref[...] * x_ref[:, :SSM_D_INNER]) * _silu(z_ref[...])
        r = lax.rsqrt(jnp.mean(y2 * y2, axis=-1, keepdims=True) + SSM_NORM_EPS)
        yn_ref[...] = (y2 * r * nw_ref[...]).astype(yn_ref.dtype)

    es, _ = _head_selectors()
    row = pl.BlockSpec((L, SSM_D_INNER), lambda c: (c, 0))
    vec = pl.BlockSpec((1, SSM_D_INNER), lambda c: (0, 0))
    return pl.pallas_call(
        body, grid=(nc,),
        in_specs=[pl.BlockSpec((L, SSM_CONV_DIM), lambda c: (c, 0)), pl.BlockSpec((L, LANES), lambda c: (c, 0)),
                  pl.BlockSpec((1, LANES), lambda c: (0, 0)), pl.BlockSpec((1, LANES), lambda c: (0, 0)),
                  pl.BlockSpec((LANES, SSM_D_INNER), lambda c: (0, 0)), row, vec, vec],
        out_specs=[row, pl.BlockSpec((None, SSM_GROUPS, GROUP_W, SSM_STATE), lambda c: (c, 0, 0, 0)), row],
        out_shape=[_sds((T, SSM_D_INNER)), _sds((nc, SSM_GROUPS, GROUP_W, SSM_STATE)), _sds((T, SSM_D_INNER), MXU)],
        scratch_shapes=[pltpu.VMEM((SSM_GROUPS, GROUP_W, SSM_STATE), F32)],
        compiler_params=_cp("arbitrary"), name=name)(xact, dtraw, dt_bias, a_log, es, z, d_skip, nw)


def _ssd_bwd(xact, xbc, xpre, cw, dtraw, dt_bias, a_log, d_skip, states, dy, name):
    T = xact.shape[0]
    nc = T // SSM_CHUNK
    L = SSM_CHUNK
    K = SSM_CONV
    BO, CO = SSM_D_INNER, SSM_D_INNER + SSM_GROUPS * SSM_STATE

    def body(x_ref, xin_ref, pre_ref, cw_ref, dt_ref, bias_ref, al_ref, dsk_ref, es_ref, est_ref, st_ref, dy_ref,
             dxbc_ref, dcw_ref, dcb_ref, ddt_ref, dbias_ref, dal_ref, dd_ref, dstate, qa, qx, dxp_ref, ahead):
        cc = pl.program_id(0)

        @pl.when(cc == 0)
        def _():
            dstate[...] = jnp.zeros(dstate.shape, F32)
            ahead[...] = jnp.zeros(ahead.shape, F32)

        xb, dt, A, tril, acs = _ssd_common(dt_ref[...], bias_ref[...], al_ref[...])
        acsT = acs.T
        last = acs[L - 1:L, :]
        cd = jnp.exp(last)
        es, est = es_ref[...], est_ref[...]
        dtX = _expand_heads(dt, es)
        EX = _expand_heads(jnp.exp(acs), es)
        decX = _expand_heads(jnp.exp(last - acs), es)
        lane1 = _iota((1, LANES), 1)
        lane = _iota((L, LANES), 1)
        sub = _iota((L, LANES), 0)
        ztot = jnp.zeros((1, LANES), F32)
        wrow = jnp.zeros((L, LANES), F32)
        wcolT = jnp.zeros((LANES, L), F32)
        rows_dec, rows_dd = [], []
        for g in range(SSM_GROUPS):
            gs = slice(g * GROUP_W, (g + 1) * GROUP_W)
            x = x_ref[:, gs]
            B = x_ref[:, BO + g * SSM_STATE:BO + (g + 1) * SSM_STATE]
            C = x_ref[:, CO + g * SSM_STATE:CO + (g + 1) * SSM_STATE]
            dY = dy_ref[:, gs]
            dtx, e_x, dec_x = dtX[:, gs], EX[:, gs], decX[:, gs]
            X = x * dtx
            CB = _nt(C, B)
            S = st_ref[g]
            dS_out = dstate[g]
            dcb_sum = jnp.zeros((L, L), F32)
            dxd = []
            for r in range(GQ):
                h = GQ * g + r
                hs = slice(r * HEAD_DIM, (r + 1) * HEAD_DIM)
                Lm = jnp.exp(jnp.where(tril > 0, acs[:, h:h + 1] - acsT[h:h + 1, :], NEG))
                M = CB * Lm
                dM = _nt(dY[:, hs], X[:, hs])
                dxd.append(_tn(M, dY[:, hs]))
                dcb_sum = dcb_sum + dM * Lm
                Wm = dM * M
                wrow = jnp.where(lane == h, jnp.sum(Wm, axis=1, keepdims=True), wrow)
                wcolT = jnp.where(sub == h, jnp.sum(Wm, axis=0, keepdims=True), wcolT)
            dXd = jnp.concatenate(dxd, axis=1)
            G = _nt(C, S)
            dG = dY * e_x
            dDX = _nt(B, dS_out)
            dX = dXd + dec_x * dDX
            t_dec = dDX * X * dec_x
            qa[:, gs] = dG * G - t_dec
            qx[:, gs] = dX * x
            rows_dec.append(jnp.sum(t_dec, axis=0, keepdims=True))
            rows_dd.append(jnp.sum(dY * x, axis=0, keepdims=True))
            zc = jnp.sum(dS_out * S, axis=1, keepdims=True)
            for r in range(GQ):
                ztot = jnp.where(lane1 == GQ * g + r, jnp.sum(zc[r * HEAD_DIM:(r + 1) * HEAD_DIM], axis=0, keepdims=True), ztot)
            dxp_ref[:, gs] = dX * dtx + dY * dsk_ref[:, gs]
            dxp_ref[:, BO + g * SSM_STATE:BO + (g + 1) * SSM_STATE] = _tn(dcb_sum, C) + _nn(X * dec_x, dS_out)
            dxp_ref[:, CO + g * SSM_STATE:CO + (g + 1) * SSM_STATE] = _nn(dcb_sum, B) + _nn(dG, S)
            dstate[g] = dS_out * _per_state_row(cd, g) + _tn(dG, C)
        rows = jnp.concatenate([jnp.concatenate(rows_dec, axis=1), jnp.concatenate(rows_dd, axis=1)]
                               + [jnp.zeros((SUBLANES - 2, SSM_D_INNER), F32)], axis=0)
        rsum = _reduce_heads(rows, est)
        dlast = rsum[0:1, :] + cd * ztot
        dacs = (wrow - wcolT.T) + _reduce_heads(qa[...], est) + jnp.where(sub == L - 1, dlast, 0.0)
        triu = (_iota((L, L), 0) <= _iota((L, L), 1)).astype(F32)
        da = _dot_hi(triu, dacs)
        ddtraw = (da * A + _reduce_heads(qx[...], est)) * (1.0 / (1.0 + jnp.exp(-xb)))
        ddt_ref[...] = ddtraw
        dal = jnp.sum(da * dt, axis=0, keepdims=True) * A
        ddp = rsum[1:2, :]
        dbp = jnp.sum(ddtraw, axis=0, keepdims=True)
        w = cw_ref[...]
        d_cur = dxp_ref[...] * _dsilu(pre_ref[...])
        d_nxt = ahead[...]
        ahead[...] = d_cur[:SUBLANES]
        ups = [d_cur] + [_shift_up(d_cur, d_nxt, s) for s in range(1, K)]
        dxc = ups[0] * w[K - 1:K, :]
        for s in range(1, K):
            dxc = dxc + ups[s] * w[K - 1 - s:K - s, :]
        dxbc_ref[...] = dxc.astype(dxbc_ref.dtype)
        xin = xin_ref[...]
        dcwp = jnp.concatenate([jnp.sum(ups[K - 1 - k] * xin, axis=0, keepdims=True) for k in range(K)], axis=0)
        dcbp = jnp.sum(d_cur, axis=0, keepdims=True)

        @pl.when(cc == 0)
        def _():
            dbias_ref[...] = dbp
            dal_ref[...] = dal
            dd_ref[...] = ddp
            dcw_ref[...] = dcwp
            dcb_ref[...] = dcbp

        @pl.when(cc > 0)
        def _():
            dbias_ref[...] += dbp
            dal_ref[...] += dal
            dd_ref[...] += ddp
            dcw_ref[...] += dcwp
            dcb_ref[...] += dcbp

    rc = lambda c: nc - 1 - c
    vec = pl.BlockSpec((1, LANES), lambda c: (0, 0))
    wide = pl.BlockSpec((L, SSM_CONV_DIM), lambda c: (rc(c), 0))
    es, est = _head_selectors()
    return pl.pallas_call(
        body, grid=(nc,),
        in_specs=[wide, wide, wide, pl.BlockSpec((K, SSM_CONV_DIM), lambda c: (0, 0)),
                  pl.BlockSpec((L, LANES), lambda c: (rc(c), 0)), vec, vec,
                  pl.BlockSpec((1, SSM_D_INNER), lambda c: (0, 0)),
                  pl.BlockSpec((LANES, SSM_D_INNER), lambda c: (0, 0)), pl.BlockSpec((SSM_D_INNER, LANES), lambda c: (0, 0)),
                  pl.BlockSpec((None, SSM_GROUPS, GROUP_W, SSM_STATE), lambda c: (rc(c), 0, 0, 0)),
                  pl.BlockSpec((L, SSM_D_INNER), lambda c: (rc(c), 0))],
        out_specs=[wide, pl.BlockSpec((K, SSM_CONV_DIM), lambda c: (0, 0)), pl.BlockSpec((1, SSM_CONV_DIM), lambda c: (0, 0)),
                   pl.BlockSpec((L, LANES), lambda c: (rc(c), 0)), vec, vec, vec],
        out_shape=[_sds((T, SSM_CONV_DIM), MXU), _sds((K, SSM_CONV_DIM)), _sds((1, SSM_CONV_DIM)),
                   _sds((T, LANES)), _sds((1, LANES)), _sds((1, LANES)), _sds((1, LANES))],
        scratch_shapes=[pltpu.VMEM((SSM_GROUPS, GROUP_W, SSM_STATE), F32), pltpu.VMEM((L, SSM_D_INNER), F32),
                        pltpu.VMEM((L, SSM_D_INNER), F32), pltpu.VMEM((L, SSM_CONV_DIM), F32),
                        pltpu.VMEM((SUBLANES, SSM_CONV_DIM), F32)],
        compiler_params=_cp("arbitrary"), name=name)(xact, xbc, xpre, cw, dtraw, dt_bias, a_log, d_skip, es, est, states, dy)


def _ssm_post_bwd(y, xact, z, d_skip, nw, dyn, name):
    T = y.shape[0]
    tm = min(T, 256)
    W = SSM_D_INNER

    def body(y_ref, x_ref, z_ref, d_ref, w_ref, dn_ref, dyg_ref, dz_ref, dw_ref):
        zv = z_ref[...]
        sz = _silu(zv)
        yg = y_ref[...] + d_ref[...] * x_ref[...]
        y2 = yg * sz
        r = lax.rsqrt(jnp.mean(y2 * y2, axis=-1, keepdims=True) + SSM_NORM_EPS)
        y2h = y2 * r
        dn = dn_ref[...]
        gy = dn * w_ref[...]
        dy2 = r * (gy - y2h * jnp.mean(gy * y2h, axis=-1, keepdims=True))
        dyg_ref[...] = dy2 * sz
        dz_ref[...] = (dy2 * yg * _dsilu(zv)).astype(dz_ref.dtype)
        part = jnp.sum(dn * y2h, axis=0, keepdims=True)

        @pl.when(pl.program_id(0) == 0)
        def _():
            dw_ref[...] = part

        @pl.when(pl.program_id(0) > 0)
        def _():
            dw_ref[...] += part

    row = pl.BlockSpec((tm, W), lambda i: (i, 0))
    vec = pl.BlockSpec((1, W), lambda i: (0, 0))
    return pl.pallas_call(
        body, grid=(T // tm,), in_specs=[row, row, row, vec, vec, row], out_specs=[row, row, vec],
        out_shape=[_sds((T, W)), _sds((T, W), MXU), _sds((1, W))],
        compiler_params=_cp("arbitrary"), name=name)(y, xact, z, d_skip, nw, dyn)


def _local_step(x0, cos, sin_s, target, P, fetch, token, send):
    mmf = functools.partial(_mm, tm=1024)
    big, small = {}, {}
    P = dict(P, wup={}, wdn={}, fcw={})
    h0 = _rmsnorm_fwd(x0, P["nm"][0], "norm_mix0", token=token)
    proj0 = mmf(h0, P["wmiT"], tb=True, tn=1280, tk=1024, name="mix_in")
    cat, attn, lse = _mixcore_fwd(proj0, cos, sin_s, P["pool_w"], P["pool_scale"], P["sinks"], "mixcore_fwd")
    x1, hf0 = mmf(cat, P["wmo"], tn=1024, tk=1024, res=x0, norm_w=P["nf"][0], name="mix_out")

    def ffn_fwd(xin, hf, i, **epilogue):
        got = fetch(f"ffn{i}", hf)
        P["wup"][i], P["wdn"][i], P["fcw"][i] = got["wup"], got["wdn"], got["fcw"]
        hid, hc, act = _ffn_up_conv_gate(hf, P["wup"][i], P["fcw"][i], P["fcb"][i], f"ffn_up{i}")
        xout = mmf(act, P["wdn"][i], tn=1024, tk=D_FF, res=xin, name=f"ffn_down{i}", **epilogue)
        return (hid, hc), act, xout

    hid0, act0, (x2, h1) = ffn_fwd(x1, hf0, 0, norm_w=P["nm"][1])
    P.update(fetch("ssm", h1))
    z = mmf(h1, P["wsiT"], tb=True, tn=1024, tk=1024, b_rows=(0, SSM_D_INNER), name="ssm_in_z")
    xbc, xpre, xact = _ssm_in_conv(h1, P["wsiT"], SSM_D_INNER, P["scw"], P["scb"], "ssm_in_xbc")
    dtraw = mmf(h1, P["wdtT"], tb=True, tn=128, tk=1024, name="ssm_in_dt")
    y, states, yn = _ssd_fwd(xact, dtraw, P["dt_bias"], P["a_log"], z, P["d_exp"], P["snorm"], "ssd_fwd")
    x3, hf1 = mmf(yn, P["wso"], tn=1024, tk=SSM_D_INNER, res=x2, norm_w=P["nf"][1], name="ssm_out")
    hid1, act1, (dx4, d_nfin, loss_row) = ffn_fwd(x3, hf1, 1, loss_head=(P["nfin"], target))
    small["norm_final"] = d_nfin

    def ffn_bwd(xin, dxo, hf, hid, act, i):
        big[f"ffn_w_down{i}"] = dwf(act, dxo, tm=1408, tn=1024, name=f"ffn_down_dw{i}").reshape(N_CHIPS, D_FF // N_CHIPS, D_MODEL)
        dhid, dcw, dcb = _ffn_down_dx_mid_bwd(dxo, P["wdn"][i], hid[0], hid[1], P["fcw"][i], f"ffn_down_dx{i}")
        big[f"ffn_w_up{i}"] = dwf(hf, dhid, tm=1024, tn=1408, out_shard_perm=(0, 2, 1, 3), name=f"ffn_up_dw{i}")
        tok = send(f"ffn{i}", [big[f"ffn_w_up{i}"], big[f"ffn_w_down{i}"]])
        dxi, dnf = _mm(dhid, P["wup"][i], tb=True, tm=512, tn=1024, tk=2816, norm_bwd=(xin, P["nf"][i], dxo, tok), name=f"ffn_up_dx{i}")
        return dxi, dnf, dcw, dcb

    dwf = functools.partial(_mm, ta=True, tk=2048, out_dtype=BF16)
    dx3, dnf1, dfcw1, dfcb1 = ffn_bwd(x3, dx4, hf1, hid1, act1, 1)
    dyn = mmf(dx3, P["wso"], tb=True, tn=1024, tk=1024, name="ssm_out_dx")
    big["ssm_w_out"] = dwf(yn, dx3, tm=1024, tn=1024, name="ssm_out_dw").reshape(N_CHIPS, SSM_D_INNER // N_CHIPS, D_MODEL)
    dyg, dz, d_snorm = _ssm_post_bwd(y, xact, z, P["d_exp"], P["snorm"], dyn, "ssm_post_bwd")
    dxbc, d_scw, d_scb, ddtraw, d_dtb, d_alog, d_dskip = _ssd_bwd(
        xact, xbc, xpre, P["scw"], dtraw, P["dt_bias"], P["a_log"], P["d_exp"], states, dyg, "ssd_bwd")
    dwsi = dwf(dz, h1, tm=1024, tn=1024, out_into=(None, SSM_IN_DIM, 0), name="ssm_in_dw_z")
    dwsi = dwf(dxbc, h1, tm=1024, tn=1024, out_into=(dwsi, SSM_IN_DIM, SSM_D_INNER // 1024), name="ssm_in_dw_xbc")
    dwdt = dwf(ddtraw, h1, tm=128, tn=1024, name="ssm_in_dw_dt")
    dwsi = _put_rows(dwsi, dwdt, SSM_HEADS, SSM_D_INNER + SSM_CONV_DIM, "ssm_in_dw_put_dt")
    big["ssm_w_in"] = dwsi.reshape(N_CHIPS, SSM_IN_DIM // N_CHIPS, D_MODEL)
    tok = send("ssm", [big["ssm_w_in"], big["ssm_w_out"]])
    dh1 = mmf(dz, P["wsiT"], tn=1024, tk=2048, b_rows=(0, SSM_D_INNER), name="ssm_in_dx_z")
    dh1 = mmf(dxbc, P["wsiT"], tn=1024, tk=2048, b_rows=(SSM_D_INNER // 2048, SSM_CONV_DIM), res=dh1, name="ssm_in_dx_xbc")
    dx2, dnm1 = mmf(ddtraw, P["wdtT"], tn=1024, tk=128, res=dh1, norm_bwd=(x2, P["nm"][1], dx3, tok), name="ssm_in_dx_dt")
    dx1, dnf0, dfcw0, dfcb0 = ffn_bwd(x1, dx2, hf0, hid0, act0, 0)
    dcat = mmf(dx1, P["wmo"], tb=True, tn=1024, tk=1024, name="mix_out_dx")
    big["mix_w_out"] = dwf(cat, dx1, tm=1024, tn=1024, name="mix_out_dw").reshape(N_CHIPS, D_MODEL // N_CHIPS, D_MODEL)
    dproj0, d_pw, d_ps, d_sk = _mixcore_bwd(proj0, cos, sin_s, P["pool_w"], P["pool_scale"], P["sinks"], attn, lse, dcat, "mixcore_bwd")
    big["mix_w_in"] = dwf(dproj0, h0, tm=1280, tn=1024, name="mix_in_dw").reshape(N_CHIPS, MIX_IN_DIM // N_CHIPS, D_MODEL)
    tok = send("mix", [big["mix_w_in"], big["mix_w_out"]])
    dx0, dnm0 = mmf(dproj0, P["wmiT"], tn=1024, tk=1280, norm_bwd=(x0, P["nm"][0], dx1, tok), name="mix_in_dx")

    def unperm_cols(a):
        r = a.shape[0]
        t = a.reshape(r, N_CHIPS, FFN_TC)
        return jnp.stack([t[:, p] for p in _PERM], axis=0)

    small["norm_mix"] = jnp.concatenate([dnm0, dnm1], axis=0)
    small["norm_ffn"] = jnp.concatenate([dnf0, dnf1], axis=0)
    small["pool_w"] = d_pw.reshape(4 * POOL_GROUP, POOL_GROUP)
    small["pool_scale"] = d_ps
    small["attn_sinks"] = d_sk
    small["ssm_dt_bias"] = d_dtb
    small["ssm_A_log"] = d_alog
    small["ssm_D"] = d_dskip
    fcb = jnp.stack([unperm_cols(dfcb0), unperm_cols(dfcb1)], axis=0)
    small["ffn_conv_b"] = fcb.reshape(2, 2 * D_FF)
    small["ssm_conv_w"] = d_scw.reshape(SSM_CONV, N_CHIPS, SSM_CONV_DIM // N_CHIPS).transpose(1, 0, 2)
    small["ssm_conv_b"] = d_scb.reshape(N_CHIPS, 1, SSM_CONV_DIM // N_CHIPS)
    small["ssm_norm"] = d_snorm.reshape(N_CHIPS, 1, SSM_D_INNER // N_CHIPS)
    small["ffn_conv_w"] = jnp.concatenate([unperm_cols(dfcw0), unperm_cols(dfcw1)], axis=1)
    return loss_row, dx0, big, small


ANY = pl.BlockSpec(memory_space=pl.ANY)


def _place():
    return lax.axis_index("x"), lax.axis_index("y"), lax.axis_index("c")


def _gather_shards(shards, name):
    n = len(shards)
    split = [s.size >= (1 << 16) for s in shards]

    def half(ref, a, h):
        shp = shards[a].shape
        if len(shp) == 3:
            return ref.at[h]
        r2 = shp[0] // 2
        return ref.at[pl.ds(pl.multiple_of(h * r2, 2 * SUBLANES), r2), :]

    def body(*refs):
        ins, outs = refs[:n], refs[n:2 * n]
        send, recv, fsend, frecv = refs[2 * n:]
        x, y, c = _place()
        k = 2 * x + y
        chips = [(1 - x, y), (x, 1 - y), (1 - x, 1 - y)]

        def ici(a, j, src_slot_ref, dst_slot):
            px, py = chips[j]
            src = half(src_slot_ref, a, c) if split[a] else src_slot_ref
            dst = half(outs[a].at[dst_slot], a, c) if split[a] else outs[a].at[dst_slot]
            return pltpu.make_async_remote_copy(src, dst, send.at[a, j], recv.at[a, j], device_id=(px, py, c), device_id_type=MESH)

        def d2d(a, j, h):
            px, py = chips[j]
            part = half(outs[a].at[2 * px + py], a, h)
            return pltpu.make_async_remote_copy(part, part, fsend.at[a, j], frecv.at[a, j], device_id=(x, y, 1 - c), device_id_type=MESH)

        sends = [ici(a, j, ins[a], k) for a in range(n) for j in range(3)]
        for cp in sends:
            cp.start()
        passed = []
        for a in range(n):
            for j, (px, py) in enumerate(chips):
                ici(a, j, ins[a], 2 * px + py).wait_recv()
                if split[a]:
                    passed.append(d2d(a, j, c))
                    passed[-1].start()
        for a in range(n):
            if split[a]:
                for j in range(3):
                    d2d(a, j, 1 - c).wait_recv()
        for cp in sends + passed:
            cp.wait_send()

    return pl.pallas_call(
        body, in_specs=[ANY] * n, out_specs=[ANY] * n,
        out_shape=[_sds((N_CHIPS,) + s.shape, s.dtype) for s in shards],
        scratch_shapes=[pltpu.SemaphoreType.DMA((n, 3))] * 4,
        compiler_params=pltpu.CompilerParams(has_side_effects=True), name=name)(*shards)


HBM = pl.BlockSpec(memory_space=pltpu.HBM)
SEM = pl.BlockSpec(memory_space=pltpu.SEMAPHORE)
DATAFLOW = pltpu.SideEffectType.DATAFLOW_SIDE_EFFECTING


def _row_half(ref, h):
    r2 = ref.shape[0] // 2
    return ref.at[pl.ds(pl.multiple_of(h * r2, 2 * SUBLANES), r2), :]


def _spread_start(groups, slot_src, after, name, halved=()):
    flat = [a for grp in groups for a in grp]
    n = len(flat)
    ng = len(groups)
    offs = [sum(len(g) for g in groups[:i]) for i in range(ng)]
    lshape = [(a.shape if slot_src else (N_CHIPS,) + a.shape) for a in flat]

    nsem = 6 * n

    def body(*refs):
        src, land = refs[:n], refs[n:2 * n]
        sems = refs[2 * n + 1:2 * n + 1 + nsem]
        token = refs[-1]
        x, y, c = _place()
        k = 2 * x + y
        chips = [(1 - x, y), (x, 1 - y), (1 - x, 1 - y)]
        for a in range(n):
            half = any(offs[gi] <= a < offs[gi] + len(groups[gi]) for gi in halved)
            for j, (px, py) in enumerate(chips):
                s = src[a].at[2 * px + py] if slot_src else src[a]
                d = land[a].at[k]
                if half:
                    s, d = _row_half(s, c), _row_half(d, c)
                pltpu.make_async_remote_copy(s, d, sems[6 * a + 2 * j], sems[6 * a + 2 * j + 1],
                                             device_id=(px, py, c), device_id_type=MESH).start()
        token[...] = jnp.zeros(token.shape, token.dtype)

    out_shape = [pltpu.SemaphoreType.DMA(())] * nsem
    out_shape += [pltpu.HBM(a.shape, a.dtype) for a in flat] + [pltpu.HBM(s, a.dtype) for s, a in zip(lshape, flat)]
    out_shape.append(_sds((SUBLANES, LANES)))
    args = [pltpu.with_memory_space_constraint(a, pltpu.HBM) for a in flat]
    args += [pltpu.with_memory_space_constraint(lax.empty(s, a.dtype), pltpu.HBM) for s, a in zip(lshape, flat)]
    res = pl.pallas_call(
        body, name=name, out_shape=tuple(out_shape), in_specs=[HBM] * (2 * n) + [pl.BlockSpec(memory_space=pl.ANY)],
        out_specs=tuple([SEM] * nsem + [HBM] * (2 * n) + [pl.BlockSpec(memory_space=pltpu.VMEM)]),
        input_output_aliases={i: nsem + i for i in range(2 * n)},
        compiler_params=pltpu.CompilerParams(has_side_effects=DATAFLOW))(*args, after)
    sems, thru, token = res[:nsem], res[nsem:nsem + 2 * n], res[-1]
    out = []
    for gi, grp in enumerate(groups):
        sl = slice(offs[gi], offs[gi] + len(grp))
        out.append((list(sems[6 * offs[gi]:6 * (offs[gi] + len(grp))]), list(thru[:n][sl]), list(thru[n:][sl])))
    return out, token


def _spread_wait(started, slot_src, after, name, halved=False):
    sems, srcs, lands = started
    n = len(srcs)

    def body(*refs):
        src, land = refs[:n], refs[n:2 * n]
        sem = refs[2 * n:2 * n + 6 * n]
        x, y, c = _place()
        chips = [(1 - x, y), (x, 1 - y), (1 - x, 1 - y)]
        for a in range(n):
            for j, (px, py) in enumerate(chips):
                s = src[a].at[2 * px + py] if slot_src else src[a]
                d = land[a].at[2 * px + py]
                if halved:
                    s, d = _row_half(s, c), _row_half(d, c)
                cp = pltpu.make_async_remote_copy(s, d, sem[6 * a + 2 * j], sem[6 * a + 2 * j + 1],
                                                  device_id=(px, py, c), device_id_type=MESH)
                cp.wait_send()
                cp.wait_recv()

    res = pl.pallas_call(
        body, name=name, out_shape=tuple([pltpu.HBM(a.shape, a.dtype) for a in srcs] + [pltpu.HBM(a.shape, a.dtype) for a in lands]),
        in_specs=[HBM] * (2 * n) + [SEM] * (6 * n) + [pl.BlockSpec(memory_space=pl.ANY)], out_specs=tuple([HBM] * (2 * n)),
        input_output_aliases={i: i for i in range(2 * n)},
        compiler_params=pltpu.CompilerParams(has_side_effects=DATAFLOW))(*srcs, *lands, *sems, after)
    return list(res[:n]), list(res[n:])


def _sibling_fill(lands, name):
    n = len(lands)

    def body(*refs):
        bufs = refs[n:2 * n]
        send, recv = refs[2 * n:]
        x, y, c = _place()
        chips = [(1 - x, y), (x, 1 - y), (1 - x, 1 - y)]

        def copy(a, j, h):
            px, py = chips[j]
            part = _row_half(bufs[a].at[2 * px + py], h)
            return pltpu.make_async_remote_copy(part, part, send.at[a, j], recv.at[a, j], device_id=(x, y, 1 - c), device_id_type=MESH)

        sends = [copy(a, j, c) for a in range(n) for j in range(3)]
        for cp in sends:
            cp.start()
        for a in range(n):
            for j in range(3):
                copy(a, j, 1 - c).wait_recv()
        for cp in sends:
            cp.wait_send()

    return pl.pallas_call(
        body, in_specs=[ANY] * n, out_specs=[ANY] * n, out_shape=[_sds(t.shape, t.dtype) for t in lands],
        input_output_aliases={i: i for i in range(n)},
        scratch_shapes=[pltpu.SemaphoreType.DMA((n, 3)), pltpu.SemaphoreType.DMA((n, 3))],
        compiler_params=pltpu.CompilerParams(has_side_effects=True), name=name)(*lands)


def _sibling_exchange(fs, name):
    n = len(fs)

    def body(*refs):
        ins, outs = refs[:n], refs[n:2 * n]
        send, recv = refs[2 * n:]
        x, y, c = _place()
        cps = [pltpu.make_async_remote_copy(ins[a], outs[a], send.at[a], recv.at[a],
                                            device_id=(x, y, 1 - c), device_id_type=MESH) for a in range(n)]
        for cp in cps:
            cp.start()
        for cp in cps:
            cp.wait()

    return pl.pallas_call(
        body, in_specs=[ANY] * n, out_specs=[ANY] * n, out_shape=[_sds(f.shape, f.dtype) for f in fs],
        scratch_shapes=[pltpu.SemaphoreType.DMA((n,)), pltpu.SemaphoreType.DMA((n,))],
        compiler_params=pltpu.CompilerParams(has_side_effects=True), name=name)(*fs)


def _tile2d(rows, cols, budget=2 * 1024 * 1024, step=2 * SUBLANES):
    fits = [t for t in range(step, rows + 1, step) if rows % t == 0 and t * cols * 4 <= budget]
    if fits:
        return fits[-1], cols
    fits = [t for t in range(LANES, cols + 1, LANES) if cols % t == 0 and rows * t * 4 <= budget]
    assert fits, (rows, cols)
    return rows, fits[-1]


def _chip_sum(own, parts, kidx, name):
    _, R, C = parts.shape
    tr, tc = _tile2d(R, C)

    def body(k_ref, o_ref_in, p1_ref, p2_ref, p3_ref, o_ref):
        tot = ((o_ref_in[...].astype(F32) + p1_ref[...].astype(F32)) + p2_ref[...].astype(F32)) + p3_ref[...].astype(F32)
        o_ref[...] = tot.astype(o_ref.dtype)

    def slot(d):
        return pl.BlockSpec((None, tr, tc), lambda i, j, k: ((k[0] + d) % N_CHIPS, i, j))

    return pl.pallas_call(
        body,
        grid_spec=pltpu.PrefetchScalarGridSpec(
            num_scalar_prefetch=1, grid=(R // tr, C // tc), in_specs=[slot(0), slot(1), slot(2), slot(3)],
            out_specs=pl.BlockSpec((tr, tc), lambda i, j, k: (i, j))),
        out_shape=_sds((R, C), BF16), compiler_params=_cp("parallel", "parallel"), name=name)(kidx, own, parts, parts, parts)


def _adamw_math(w, g, m, v):
    m2 = ADAM_B1 * m + (1.0 - ADAM_B1) * g
    v2 = ADAM_B2 * v + (1.0 - ADAM_B2) * (g * g)
    m_hat = m2 / (1.0 - ADAM_B1 ** ADAM_STEP)
    v_hat = v2 / (1.0 - ADAM_B2 ** ADAM_STEP)
    delta = -ADAM_LR * (m_hat / (jnp.sqrt(v_hat) + ADAM_EPS) + ADAM_WD * w)
    return delta, m2, v2


def _adamw(w, m, v, gparts, name):
    Lw, R, C = w.shape
    tr, tc = _tile2d(R, C)
    flat = [h for pair in gparts for h in pair]

    def body(*refs):
        w_ref, m_ref, v_ref = refs[:3]
        g_refs = refs[3:3 + 2 * Lw]
        go_ref, d_ref, mo_ref, vo_ref = refs[3 + 2 * Lw:]
        g = g_refs[0][...].astype(F32) + g_refs[1][...].astype(F32)
        for l in range(1, Lw):
            g = jnp.where(pl.program_id(0) == l, g_refs[2 * l][...].astype(F32) + g_refs[2 * l + 1][...].astype(F32), g)
        d, m2, v2 = _adamw_math(w_ref[...], g, m_ref[...], v_ref[...])
        go_ref[...] = g
        d_ref[...] = d
        mo_ref[...] = m2
        vo_ref[...] = v2

    blk = pl.BlockSpec((None, tr, tc), lambda l, i, j: (l, i, j))
    gblk = pl.BlockSpec((tr, tc), lambda l, i, j: (i, j))
    return pl.pallas_call(
        body, grid=(Lw, R // tr, C // tc), in_specs=[blk, blk, blk] + [gblk] * (2 * Lw), out_specs=[blk] * 4,
        out_shape=[_sds((Lw, R, C))] * 4, compiler_params=_cp("parallel", "parallel", "parallel"), name=name)(w, m, v, *flat)


def _small_adamw(grads, wmv, name):
    n = len(grads)

    def body(*refs):
        g_in, p_in, outs = refs[:n], refs[n:4 * n], refs[4 * n:]
        for a in range(n):
            g = g_in[a][...]
            d_, m2, v2 = _adamw_math(p_in[3 * a][...], g, p_in[3 * a + 1][...], p_in[3 * a + 2][...])
            outs[4 * a][...] = g
            outs[4 * a + 1][...] = d_
            outs[4 * a + 2][...] = m2
            outs[4 * a + 3][...] = v2

    vm = pl.BlockSpec(memory_space=pltpu.VMEM)
    args = list(grads) + [t for tri in wmv for t in tri]
    out_shape = [_sds(g.shape) for g in grads for _ in range(4)]
    return pl.pallas_call(body, in_specs=[vm] * len(args), out_specs=[vm] * len(out_shape), out_shape=out_shape,
                          compiler_params=pltpu.CompilerParams(vmem_limit_bytes=V7X_VMEM_LIMIT), name=name)(*args)


def _small_allreduce(partials, pshapes, loss_row, name):
    n = len(partials)
    gshapes = [p.shape for p in partials] + [loss_row.shape]
    ng = n + 1

    def body(*refs):
        g_in = refs[:ng]
        outs = refs[ng:2 * ng]
        sib = refs[2 * ng:3 * ng]
        pair = refs[3 * ng:4 * ng]
        bufs = refs[4 * ng:5 * ng]
        send1, recv1, send2, recv2 = refs[-4:]
        x, y, c = _place()
        k = 2 * x + y
        chips = [(1 - x, y), (x, 1 - y), (1 - x, 1 - y)]
        swaps = [pltpu.make_async_remote_copy(g_in[a], sib[a], send1.at[a], recv1.at[a],
                                              device_id=(x, y, 1 - c), device_id_type=MESH) for a in range(ng)]
        for cp in swaps:
            cp.start()
        for a, cp in enumerate(swaps):
            cp.wait()
            pair[a][...] = g_in[a][...] + sib[a][...]
            bufs[a][k] = pair[a][...]
        sends = [pltpu.make_async_remote_copy(pair[a], bufs[a].at[k], send2.at[a, j], recv2.at[a, j],
                                              device_id=(px, py, c), device_id_type=MESH)
                 for a in range(ng) for j, (px, py) in enumerate(chips)]
        for cp in sends:
            cp.start()
        for a in range(ng):
            for j, (px, py) in enumerate(chips):
                pltpu.make_async_remote_copy(pair[a], bufs[a].at[2 * px + py], send2.at[a, j], recv2.at[a, j],
                                             device_id=(px, py, c), device_id_type=MESH).wait_recv()
        for cp in sends:
            cp.wait_send()
        for a in range(ng):
            sharded = len(gshapes[a]) == 3

            def part(d):
                return bufs[a][d, k] if sharded else bufs[a][d]

            tot = part(0)
            for d in range(1, N_CHIPS):
                tot = tot + part(d)
            if a == n:
                outs[n][...] = tot
            else:
                pr, pc = pshapes[a]
                outs[a][...] = tot[:pr, :pc]

    vm = pl.BlockSpec(memory_space=pltpu.VMEM)
    args = list(partials) + [loss_row]
    out_shape = [_sds(ps) for ps in pshapes] + [_sds(loss_row.shape)]
    return pl.pallas_call(
        body, in_specs=[vm] * len(args), out_specs=[vm] * len(out_shape), out_shape=out_shape,
        scratch_shapes=[pltpu.VMEM(tuple(s), F32) for s in gshapes] * 2 + [pltpu.VMEM((N_CHIPS,) + tuple(s), F32) for s in gshapes]
        + [pltpu.SemaphoreType.DMA((ng,)), pltpu.SemaphoreType.DMA((ng,)),
           pltpu.SemaphoreType.DMA((ng, 3)), pltpu.SemaphoreType.DMA((ng, 3))],
        compiler_params=pltpu.CompilerParams(has_side_effects=True, vmem_limit_bytes=V7X_VMEM_LIMIT), name=name)(*args)


_PERM = (0, 2, 1, 3)


def _cols_from_shards(g):
    return g.transpose(1, 0, 2).reshape(g.shape[1], N_CHIPS * g.shape[2])


def _rope_tables(positions):
    inv_freq = ROPE_THETA ** (-jnp.arange(0, HEAD_DIM, 2, dtype=F32) / HEAD_DIM)
    ang = positions.astype(F32).reshape(-1, 1) * inv_freq
    cos, sin = jnp.cos(ang), jnp.sin(ang)
    cos = jnp.concatenate([cos, cos, cos, cos], axis=-1)
    sin_s = jnp.concatenate([-sin, sin, -sin, sin], axis=-1)
    return cos, sin_s


def kernel(x, positions, norm_mix, norm_ffn, norm_final, mix_w_in, pool_w, pool_scale, attn_sinks, mix_w_out, ssm_w_in, ssm_conv_w, ssm_conv_b, ssm_dt_bias, ssm_A_log, ssm_D, ssm_norm, ssm_w_out, ffn_w_up, ffn_conv_w, ffn_conv_b, ffn_w_down, loss_target, m_norm_mix, m_norm_ffn, m_norm_final, m_mix_w_in, m_pool_w, m_pool_scale, m_attn_sinks, m_mix_w_out, m_ssm_w_in, m_ssm_conv_w, m_ssm_conv_b, m_ssm_dt_bias, m_ssm_A_log, m_ssm_D, m_ssm_norm, m_ssm_w_out, m_ffn_w_up, m_ffn_conv_w, m_ffn_conv_b, m_ffn_w_down, v_norm_mix, v_norm_ffn, v_norm_final, v_mix_w_in, v_pool_w, v_pool_scale, v_attn_sinks, v_mix_w_out, v_ssm_w_in, v_ssm_conv_w, v_ssm_conv_b, v_ssm_dt_bias, v_ssm_A_log, v_ssm_D, v_ssm_norm, v_ssm_w_out, v_ffn_w_up, v_ffn_conv_w, v_ffn_conv_b, v_ffn_w_down):
    W = dict(norm_mix=norm_mix, norm_ffn=norm_ffn, norm_final=norm_final, mix_w_in=mix_w_in, pool_w=pool_w, pool_scale=pool_scale, attn_sinks=attn_sinks, mix_w_out=mix_w_out, ssm_w_in=ssm_w_in, ssm_conv_w=ssm_conv_w, ssm_conv_b=ssm_conv_b, ssm_dt_bias=ssm_dt_bias, ssm_A_log=ssm_A_log, ssm_D=ssm_D, ssm_norm=ssm_norm, ssm_w_out=ssm_w_out, ffn_w_up=ffn_w_up, ffn_conv_w=ffn_conv_w, ffn_conv_b=ffn_conv_b, ffn_w_down=ffn_w_down)
    Mo = dict(norm_mix=m_norm_mix, norm_ffn=m_norm_ffn, norm_final=m_norm_final, mix_w_in=m_mix_w_in, pool_w=m_pool_w, pool_scale=m_pool_scale, attn_sinks=m_attn_sinks, mix_w_out=m_mix_w_out, ssm_w_in=m_ssm_w_in, ssm_conv_w=m_ssm_conv_w, ssm_conv_b=m_ssm_conv_b, ssm_dt_bias=m_ssm_dt_bias, ssm_A_log=m_ssm_A_log, ssm_D=m_ssm_D, ssm_norm=m_ssm_norm, ssm_w_out=m_ssm_w_out, ffn_w_up=m_ffn_w_up, ffn_conv_w=m_ffn_conv_w, ffn_conv_b=m_ffn_conv_b, ffn_w_down=m_ffn_w_down)
    Vo = dict(norm_mix=v_norm_mix, norm_ffn=v_norm_ffn, norm_final=v_norm_final, mix_w_in=v_mix_w_in, pool_w=v_pool_w, pool_scale=v_pool_scale, attn_sinks=v_attn_sinks, mix_w_out=v_mix_w_out, ssm_w_in=v_ssm_w_in, ssm_conv_w=v_ssm_conv_w, ssm_conv_b=v_ssm_conv_b, ssm_dt_bias=v_ssm_dt_bias, ssm_A_log=v_ssm_A_log, ssm_D=v_ssm_D, ssm_norm=v_ssm_norm, ssm_w_out=v_ssm_w_out, ffn_w_up=v_ffn_w_up, ffn_conv_w=v_ffn_conv_w, ffn_conv_b=v_ffn_conv_b, ffn_w_down=v_ffn_w_down)

    kchip = 2 * lax.axis_index("x") + lax.axis_index("y")

    def own_slot(g, own):
        return lax.dynamic_update_slice_in_dim(g, own[None], kchip, axis=0)

    def tr(t):
        return jnp.swapaxes(t[0], 0, 1)

    later = dict(ffn0=[ffn_w_up[0].astype(MXU), ffn_w_down[0].astype(MXU)],
                 ssm=[tr(ssm_w_in).astype(MXU), ssm_w_out[0].astype(MXU)],
                 ffn1=[ffn_w_up[1].astype(MXU), ffn_w_down[1].astype(MXU)])
    sh = [tr(mix_w_in).astype(MXU), mix_w_out[0].astype(MXU), ssm_conv_w[0], ssm_conv_b, ssm_norm, ffn_conv_w]
    first = _gather_shards(sh, "gather_first")
    g_mi, g_mo, g_scw, g_scb, g_sn, g_fcw = [own_slot(g, own) for g, own in zip(first, sh)]
    started, token = _spread_start(list(later.values()), False, first[0], "gather_start", halved=(0,))
    started = dict(zip(later.keys(), started))
    fcw = [jnp.concatenate([g_fcw[p, i] for p in _PERM], axis=1) for i in range(2)]
    P = dict(
        nm=norm_mix, nf=norm_ffn, nfin=norm_final,
        wmiT=g_mi.reshape(MIX_IN_DIM, D_MODEL), wmo=g_mo.reshape(D_MODEL, D_MODEL),
        pool_w=pool_w[0], pool_scale=pool_scale, sinks=attn_sinks[0],
        scw=_cols_from_shards(g_scw), scb=g_scb.reshape(1, SSM_CONV_DIM), snorm=g_sn.reshape(1, SSM_D_INNER),
        dt_bias=jnp.pad(ssm_dt_bias, ((0, 0), (0, LANES - SSM_HEADS))), a_log=jnp.pad(ssm_A_log, ((0, 0), (0, LANES - SSM_HEADS))),
        d_exp=jnp.repeat(ssm_D, SSM_D_INNER // SSM_HEADS, axis=1),
        fcb=[jnp.concatenate([ffn_conv_b[i:i + 1, p * FFN_TC:(p + 1) * FFN_TC] for p in _PERM], axis=1) for i in range(2)],
    )

    def fetch(group, after):
        owns, lands = _spread_wait(started[group], False, after, f"gather_wait_{group}", halved=group == "ffn0")
        if group == "ffn0":
            lands = _sibling_fill(lands, "gather_fill_ffn0")
        a, b = [own_slot(g, own) for g, own in zip(lands, owns)]
        if group == "ssm":
            wsi = a.reshape(SSM_IN_DIM, D_MODEL)
            zx = SSM_D_INNER + SSM_CONV_DIM
            return dict(wsiT=wsi, wdtT=jnp.pad(wsi[zx:], ((0, LANES - SSM_HEADS), (0, 0))), wso=b.reshape(SSM_D_INNER, D_MODEL))
        i = int(group[-1])
        return dict(wup=jnp.concatenate([a[p] for p in _PERM], axis=1), wdn=b.reshape(D_FF, D_MODEL), fcw=fcw[i])

    cos, sin_s = _rope_tables(positions)
    sent = {}

    def send(group, grads):
        res, tok = _spread_start([grads], True, jnp.zeros((SUBLANES, LANES), F32), f"grad_start_{group}")
        sent[group] = res[0]
        return tok

    loss_row, grad_x, big, small = _local_step(x[0], cos, sin_s, loss_target[0], P, fetch, token, send)

    kidx = kchip.astype(jnp.int32).reshape(1)
    group_names = dict(ffn1=["ffn_w_up1", "ffn_w_down1"], ssm=["ssm_w_in", "ssm_w_out"], ffn0=["ffn_w_up0", "ffn_w_down0"],
                       mix=["mix_w_in", "mix_w_out"])
    names, mine = [], []
    for group, started_g in sent.items():
        grads, lands = _spread_wait(started_g, True, grad_x, f"grad_wait_{group}")
        for nm, g, land in zip(group_names[group], grads, lands):
            names.append(nm)
            mine.append(_chip_sum(g, land, kidx, f"chip_sum_{nm}"))
    theirs = _sibling_exchange(mine, "sibling_exchange")
    red = {nm: (a, b) for nm, a, b in zip(names, mine, theirs)}

    out = {}

    def big_update(pname, gparts, transposed=False):
        w = W[pname]
        lw = len(gparts)
        shp = w.shape
        rr, cc = gparts[0][0].shape
        fix = (lambda t: tr(t)[None]) if transposed else (lambda t: t.reshape(lw, rr, cc))
        res = _adamw(fix(w), fix(Mo[pname]), fix(Vo[pname]), gparts, f"adamw_{pname}")
        out[pname] = tuple((tr(r)[None] if transposed else r.reshape(shp)) for r in res)

    big_update("mix_w_in", [red["mix_w_in"]], transposed=True)
    big_update("mix_w_out", [red["mix_w_out"]])
    big_update("ssm_w_in", [red["ssm_w_in"]], transposed=True)
    big_update("ssm_w_out", [red["ssm_w_out"]])
    big_update("ffn_w_up", [red["ffn_w_up0"], red["ffn_w_up1"]])
    big_update("ffn_w_down", [red["ffn_w_down0"], red["ffn_w_down1"]])

    small_names = ["norm_mix", "norm_ffn", "norm_final", "pool_w", "pool_scale", "attn_sinks", "ssm_dt_bias", "ssm_A_log",
                   "ssm_D", "ffn_conv_b", "ssm_conv_w", "ssm_conv_b", "ssm_norm", "ffn_conv_w"]

    def as2d(t):
        if t.ndim == 1:
            return t.reshape(1, -1)
        return t.reshape(-1, t.shape[-1])

    wmv = [(as2d(W[nm]), as2d(Mo[nm]), as2d(Vo[nm])) for nm in small_names]
    summed = _small_allreduce([small[nm] for nm in small_names], [t[0].shape for t in wmv], loss_row, "small_allreduce")
    res = _small_adamw(summed[:-1], wmv, "small_adamw")
    for a, nm in enumerate(small_names):
        out[nm] = tuple(r.reshape(W[nm].shape) for r in res[4 * a:4 * a + 4])
    loss = summed[-1][0, 0]

    order = ["norm_mix", "norm_ffn", "norm_final", "mix_w_in", "pool_w", "pool_scale", "attn_sinks", "mix_w_out", "ssm_w_in",
             "ssm_conv_w", "ssm_conv_b", "ssm_dt_bias", "ssm_A_log", "ssm_D", "ssm_norm", "ssm_w_out", "ffn_w_up", "ffn_conv_w",
             "ffn_conv_b", "ffn_w_down"]
    return (loss, grad_x.reshape(x.shape), *[out[nm][0] for nm in order], *[out[nm][1] for nm in order],
            *[out[nm][2] for nm in order], *[out[nm][3] for nm in order])
```

```python
import functools

import jax
import jax.numpy as jnp
from jax import lax
from jax.experimental import pallas as pl
from jax.experimental.pallas import tpu as pltpu

F32 = jnp.float32
BF16 = jnp.bfloat16
MXU = BF16
HI = lax.Precision.HIGHEST

D_MODEL = 1024
POOL_WINDOWS = (2, 4, 8, 16)
POOL_DIM = 512
POOL_GROUP = 128
HEAD_DIM = 64
N_HEADS = 8
N_KV_HEADS = 2
GQ = 4
Q_DIM = 512
KV_DIM = 128
BLOCK = 128
ROPE_THETA = 10000.0
MIX_IN_DIM = 1280
SSM_D_INNER = 2048
SSM_HEADS = 32
SSM_GROUPS = 8
SSM_STATE = 128
SSM_CONV = 4
SSM_CHUNK = 128
SSM_CONV_DIM = 4096
SSM_IN_DIM = 6176
D_FF = 2816
FFN_CONV = 3
NORM_EPS = 1e-6
SSM_NORM_EPS = 1e-5
ADAM_LR = 0.001
ADAM_B1 = 0.9
ADAM_B2 = 0.999
ADAM_EPS = 1e-08
ADAM_WD = 0.01
ADAM_STEP = 10

N_CHIPS = 4
N_DEV = 8
LANES = 128
SUBLANES = 8
V7X_VMEM_LIMIT = 56 * 1024 * 1024
NEG = -1e30
MESH = pl.DeviceIdType.MESH


def _cp(*sem):
    return pltpu.CompilerParams(dimension_semantics=sem if sem else None, vmem_limit_bytes=V7X_VMEM_LIMIT)


def _sds(shape, dtype=F32):
    return jax.ShapeDtypeStruct(tuple(shape), dtype)


def _iota(shape, dim):
    return lax.broadcasted_iota(jnp.int32, shape, dim)


def _silu(x):
    return x * (1.0 / (1.0 + jnp.exp(-x)))


def _dsilu(x):
    s = 1.0 / (1.0 + jnp.exp(-x))
    return s * (1.0 + x * (1.0 - s))


def _mm(a, b, *, ta=False, tb=False, tm, tn, tk, res=None, out_dtype=F32, out_shard_perm=None, out_into=None, b_rows=None,
        norm_w=None, norm_bwd=None, loss_head=None, name):
    M, K = (a.shape[1], a.shape[0]) if ta else a.shape
    N = b.shape[0] if tb else b.shape[1]
    boff = 0
    if b_rows is not None:
        boff = b_rows[0]
        if tb:
            N = b_rows[1]
        else:
            K = b_rows[1]
    tm, tn, tk = min(tm, M), min(tn, N), min(tk, K)
    gm, gn, gk = M // tm, N // tn, K // tk
    assert gm * tm == M and gn * tn == N and gk * tk == K, (name, M, N, K, tm, tn, tk)
    a_spec = pl.BlockSpec((tk, tm), lambda i, j, k: (k, i)) if ta else pl.BlockSpec((tm, tk), lambda i, j, k: (i, k))
    b_spec = pl.BlockSpec((tn, tk), lambda i, j, k: (j + boff, k)) if tb else pl.BlockSpec((tk, tn), lambda i, j, k: (k + boff, j))
    dims = (((0 if ta else 1,), (1 if tb else 0,)), ((), ()))
    has_res = res is not None
    has_nw = norm_w is not None
    has_nb = norm_bwd is not None
    has_lh = loss_head is not None
    has_tok = has_nb and norm_bwd[3] is not None
    assert not (has_nw or has_nb or has_lh) or (gn == 1 and out_shard_perm is None)
    n_extra = has_res + has_nw + (3 + has_tok if has_nb else 0) + (2 if has_lh else 0)

    def body(*refs):
        a_ref, b_ref = refs[0], refs[1]
        extra = list(refs[2:2 + n_extra])
        outs = refs[len(args):]
        r_ref = extra.pop(0) if has_res else None
        nw_ref = extra.pop(0) if has_nw else None
        nb_refs = extra if has_nb else None

        def dot():
            return lax.dot_general(a_ref[...].astype(MXU), b_ref[...].astype(MXU), dims, preferred_element_type=F32)

        def accumulate(o_ref, part):
            i = pl.program_id(0)

            @pl.when(i == 0)
            def _():
                o_ref[...] = part

            @pl.when(i > 0)
            def _():
                o_ref[...] += part

        def finish(r):
            if has_res:
                r = r + r_ref[...]
            if has_lh:
                wv = extra[0][...]
                rs = lax.rsqrt(jnp.mean(r * r, axis=-1, keepdims=True) + NORM_EPS)
                xh = r * rs
                e = xh * wv - extra[1][...]
                lpart = 0.5 * jnp.sum(jnp.mean(e * e, axis=-1, keepdims=True), axis=0, keepdims=True)
                dy = e * (1.0 / N)
                g = dy * wv
                outs[0][...] = rs * (g - xh * jnp.mean(g * xh, axis=-1, keepdims=True))
                accumulate(outs[1], jnp.sum(dy * xh, axis=0, keepdims=True))
                accumulate(outs[2], jnp.broadcast_to(lpart, (1, LANES)))
                return
            if has_nb:
                xv = nb_refs[0][...]
                rs = lax.rsqrt(jnp.mean(xv * xv, axis=-1, keepdims=True) + NORM_EPS)
                xh = xv * rs
                g = r * nb_refs[1][...]
                dr = nb_refs[2][...] + nb_refs[3][0:1, 0:1] if has_tok else nb_refs[2][...]
                outs[0][...] = dr + rs * (g - xh * jnp.mean(g * xh, axis=-1, keepdims=True))
                accumulate(outs[1], jnp.sum(r * xh, axis=0, keepdims=True))
                return
            outs[0][...] = r.astype(out_dtype)
            if has_nw:
                rs = lax.rsqrt(jnp.mean(r * r, axis=-1, keepdims=True) + NORM_EPS)
                outs[1][...] = (r * rs * nw_ref[...]).astype(outs[1].dtype)

        if gk == 1:
            finish(dot())
        else:
            acc = refs[-1]
            k = pl.program_id(2)

            @pl.when(k == 0)
            def _():
                acc[...] = dot()

            if gk > 2:
                @pl.when(jnp.logical_and(k > 0, k < gk - 1))
                def _():
                    acc[...] += dot()

            @pl.when(k == gk - 1)
            def _():
                finish(acc[...] + dot())

    tile = pl.BlockSpec((tm, tn), lambda i, j, k: (i, j))
    row = pl.BlockSpec((1, tn), lambda i, j, k: (0, j))
    in_specs = [a_spec, b_spec]
    args = [a, b]
    if has_res:
        in_specs.append(tile)
        args.append(res)
    if has_nw:
        in_specs.append(row)
        args.append(norm_w.reshape(1, N))
    if has_nb:
        in_specs += [tile, row, tile]
        args += [norm_bwd[0], norm_bwd[1].reshape(1, N), norm_bwd[2]]
        if has_tok:
            in_specs.append(pl.BlockSpec((SUBLANES, LANES), lambda i, j, k: (0, 0)))
            args.append(norm_bwd[3])
    if has_lh:
        in_specs += [row, tile]
        args += [loss_head[0].reshape(1, N), loss_head[1]]
    alias = {}
    if out_into is not None:
        buf, rows, off = out_into
        out_spec = pl.BlockSpec((tm, tn), lambda i, j, k: (i + off, j))
        out_shape = _sds((rows, N), out_dtype)
        if buf is not None:
            alias = {len(args): 0}
            in_specs.append(pl.BlockSpec(memory_space=pl.ANY))
            args.append(buf)
    elif out_shard_perm is None:
        out_spec = tile
        out_shape = _sds((M, N), out_dtype)
    else:
        assert gn == len(out_shard_perm) == 4 and tuple(out_shard_perm) == (0, 2, 1, 3)
        out_spec = pl.BlockSpec((None, tm, tn), lambda i, j, k: ((j % 2) * 2 + j // 2, i, 0))
        out_shape = _sds((gn, M, tn), out_dtype)
    sem = ("parallel", "parallel", "arbitrary")
    if has_nw:
        out_spec, out_shape = [out_spec, tile], [out_shape, _sds((M, N), MXU)]
    if has_nb:
        out_spec, out_shape = [tile, row], [_sds((M, N)), _sds((1, N))]
        sem = ("arbitrary", "arbitrary", "arbitrary")
    if has_lh:
        out_spec = [tile, row, pl.BlockSpec((1, LANES), lambda i, j, k: (0, 0))]
        out_shape = [_sds((M, N)), _sds((1, N)), _sds((1, LANES))]
        sem = ("arbitrary", "arbitrary", "arbitrary")
    return pl.pallas_call(
        body, grid=(gm, gn, gk), in_specs=in_specs, out_specs=out_spec, out_shape=out_shape,
        scratch_shapes=[pltpu.VMEM((tm, tn), F32)] if gk > 1 else [], input_output_aliases=alias,
        compiler_params=_cp(*sem), name=name)(*args)


def _put_rows(buf, src, rows, at, name):
    assert at % rows == 0 and src.shape[1] == buf.shape[1] and src.dtype == buf.dtype
    C = buf.shape[1]

    def body(s_ref, b_ref, o_ref):
        o_ref[...] = s_ref[...]

    return pl.pallas_call(
        body, grid=(1,), in_specs=[pl.BlockSpec((rows, C), lambda i: (0, 0)), pl.BlockSpec(memory_space=pl.ANY)],
        out_specs=pl.BlockSpec((rows, C), lambda i: (at // rows, 0)), out_shape=_sds(buf.shape, buf.dtype),
        input_output_aliases={1: 0}, compiler_params=_cp("arbitrary"), name=name)(src, buf)


def _rmsnorm_fwd(x, w, name, token=None):
    T, D = x.shape
    tm = min(T, 512)
    has_token = token is not None

    def body(*refs):
        x_ref, w_ref, o_ref = refs[0], refs[1], refs[-1]
        xv = x_ref[...]
        if has_token:
            xv = xv + refs[2][0:1, 0:1]
        r = lax.rsqrt(jnp.mean(xv * xv, axis=-1, keepdims=True) + NORM_EPS)
        o_ref[...] = (xv * r * w_ref[...]).astype(o_ref.dtype)

    in_specs = [pl.BlockSpec((tm, D), lambda i: (i, 0)), pl.BlockSpec((1, D), lambda i: (0, 0))]
    args = [x, w.reshape(1, D)]
    if has_token:
        in_specs.append(pl.BlockSpec((SUBLANES, LANES), lambda i: (0, 0)))
        args.append(token)
    return pl.pallas_call(
        body, grid=(T // tm,), in_specs=in_specs,
        out_specs=pl.BlockSpec((tm, D), lambda i: (i, 0)), out_shape=_sds((T, D), MXU),
        compiler_params=_cp("parallel"), name=name)(*args)


def _shift_down(cur, prev8, s):
    if s == 0:
        return cur
    tm = cur.shape[0]
    rc = pltpu.roll(cur, s, 0)
    top = jnp.where(_iota((SUBLANES, cur.shape[1]), 0) < s, pltpu.roll(prev8, s, 0), rc[:SUBLANES])
    return jnp.concatenate([top, rc[SUBLANES:]], axis=0) if tm > SUBLANES else top


def _shift_up(cur, next8, s):
    if s == 0:
        return cur
    tm = cur.shape[0]
    rc = pltpu.roll(cur, tm - s, 0)
    bot = jnp.where(_iota((SUBLANES, cur.shape[1]), 0) >= SUBLANES - s, pltpu.roll(next8, SUBLANES - s, 0), rc[tm - SUBLANES:])
    return jnp.concatenate([rc[:tm - SUBLANES], bot], axis=0) if tm > SUBLANES else bot


def _conv_rows(cur, prev8, w, b, K):
    acc = cur * w[K - 1:K, :] + b
    for s in range(1, K):
        acc = acc + _shift_down(cur, prev8, s) * w[K - 1 - s:K - s, :]
    return acc


FFN_TC = 1408
HALO16 = 2 * SUBLANES


def _ffn_up_conv_gate(hf, wup, cw, cb, name):
    T, D = hf.shape
    tm = min(T, 256)
    nt, nj = T // tm, D_FF // FFN_TC
    K = FFN_CONV
    W2 = 2 * FFN_TC

    def body(a_ref, b_ref, w_ref, c_ref, hid_ref, hc_ref, act_ref, halo):
        i = pl.program_id(1)

        @pl.when(i == 0)
        def _():
            halo[...] = jnp.zeros(halo.shape, F32)

        hb = jnp.dot(a_ref[...].astype(MXU), b_ref[...].astype(MXU), preferred_element_type=F32).astype(hid_ref.dtype)
        hid_ref[...] = hb
        cur = hb.astype(F32)
        hc = _conv_rows(cur, halo[...], w_ref[...], c_ref[...], K)
        halo[...] = cur[tm - SUBLANES:]
        hc_ref[...] = hc
        act_ref[...] = (_silu(hc[:, FFN_TC:]) * hc[:, :FFN_TC]).astype(act_ref.dtype)

    blk = pl.BlockSpec((tm, W2), lambda j, i: (i, j))
    return pl.pallas_call(
        body, grid=(nj, nt),
        in_specs=[pl.BlockSpec((tm, D), lambda j, i: (i, 0)), pl.BlockSpec((D, W2), lambda j, i: (0, j)),
                  pl.BlockSpec((K, W2), lambda j, i: (0, j)), pl.BlockSpec((1, W2), lambda j, i: (0, j))],
        out_specs=[blk, blk, pl.BlockSpec((tm, FFN_TC), lambda j, i: (i, j))],
        out_shape=[_sds((T, 2 * D_FF), MXU), _sds((T, 2 * D_FF)), _sds((T, D_FF), MXU)],
        scratch_shapes=[pltpu.VMEM((SUBLANES, W2), F32)],
        compiler_params=_cp("arbitrary", "arbitrary"), name=name)(hf, wup, cw, cb)


def _ffn_down_dx_mid_bwd(dxo, wdn, hid, hc, cw, name):
    T, D = dxo.shape
    tm = min(T, 256)
    nt, nj = T // tm, D_FF // FFN_TC
    K = FFN_CONV
    W2 = 2 * FFN_TC

    def body(g_ref, wd_ref, h_ref, c_ref, w_ref, dh_ref, dw_ref, db_ref, ahead):
        i = pl.program_id(1)

        @pl.when(i == 0)
        def _():
            ahead[...] = jnp.zeros(ahead.shape, F32)

        w = w_ref[...]
        cur = h_ref[...].astype(F32)
        hcv = c_ref[...]
        dav = _nt(g_ref[...], wd_ref[...])
        u, g = hcv[:, :FFN_TC], hcv[:, FFN_TC:]
        d_cur = jnp.concatenate([dav * _silu(g), dav * u * _dsilu(g)], axis=1)
        d_nxt = ahead[...]
        ahead[...] = d_cur[:SUBLANES]
        ups = [d_cur] + [_shift_up(d_cur, d_nxt, s) for s in range(1, K)]
        dh = ups[0] * w[K - 1:K, :]
        for s in range(1, K):
            dh = dh + ups[s] * w[K - 1 - s:K - s, :]
        dh_ref[...] = dh.astype(dh_ref.dtype)
        dwp = jnp.concatenate([jnp.sum(ups[K - 1 - k] * cur, axis=0, keepdims=True) for k in range(K)], axis=0)
        dbp = jnp.sum(d_cur, axis=0, keepdims=True)

        @pl.when(i == 0)
        def _():
            dw_ref[...] = dwp
            db_ref[...] = dbp

        @pl.when(i > 0)
        def _():
            dw_ref[...] += dwp
            db_ref[...] += dbp

    blk = pl.BlockSpec((tm, W2), lambda j, i: (nt - 1 - i, j))
    return pl.pallas_call(
        body, grid=(nj, nt),
        in_specs=[pl.BlockSpec((tm, D), lambda j, i: (nt - 1 - i, 0)), pl.BlockSpec((FFN_TC, D), lambda j, i: (j, 0)), blk, blk,
                  pl.BlockSpec((K, W2), lambda j, i: (0, j))],
        out_specs=[blk, pl.BlockSpec((K, W2), lambda j, i: (0, j)), pl.BlockSpec((1, W2), lambda j, i: (0, j))],
        out_shape=[_sds((T, 2 * D_FF), MXU), _sds((K, 2 * D_FF)), _sds((1, 2 * D_FF))],
        scratch_shapes=[pltpu.VMEM((SUBLANES, W2), F32)],
        compiler_params=_cp("arbitrary", "arbitrary"), name=name)(dxo, wdn, hid, hc, cw)


def _rope(t, cos, sin_s, inverse=False):
    n = t.shape[1] // LANES
    c = jnp.concatenate([cos] * n, axis=1) if n > 1 else cos
    s = jnp.concatenate([sin_s] * n, axis=1) if n > 1 else sin_s
    a = pltpu.roll(t, HEAD_DIM // 2, 1)
    b = pltpu.roll(t, t.shape[1] - HEAD_DIM // 2, 1)
    first = (_iota(t.shape, 1) % HEAD_DIM) < HEAD_DIM // 2
    rot = jnp.where(first, b, a) * s
    return t * c - rot if inverse else t * c + rot


def _stack_heads(t, g):
    return jnp.concatenate([t[:, (GQ * g + r) * HEAD_DIM:(GQ * g + r + 1) * HEAD_DIM] for r in range(GQ)], axis=0)


def _stack_cols(t, g):
    return jnp.concatenate([t[:, GQ * g + r:GQ * g + r + 1] for r in range(GQ)], axis=0)


def _pool_sums(prev, cur, w):
    s = jnp.concatenate([prev, cur], axis=0)
    sh = 1
    while sh < w:
        s = s + pltpu.roll(s, sh, 0)
        sh *= 2
    return s[BLOCK:]


def _nt(a, b):
    return lax.dot_general(a.astype(MXU), b.astype(MXU), (((1,), (1,)), ((), ())), preferred_element_type=F32)


def _tn(a, b):
    return lax.dot_general(a.astype(MXU), b.astype(MXU), (((0,), (0,)), ((), ())), preferred_element_type=F32)


def _nn(a, b):
    return jnp.dot(a.astype(MXU), b.astype(MXU), preferred_element_type=F32)


def _mixcore_fwd(proj, cos, sin_s, pool_w, pool_scale, sinks, name):
    T = proj.shape[0]
    nb = T // BLOCK
    scale = HEAD_DIM ** -0.5

    def body(p_ref, pp_ref, c_ref, s_ref, cp_ref, sp_ref, pw_ref, ps_ref, sk_ref, cat_ref, at_ref, lse_ref):
        i = pl.program_id(0)
        has_prev = i > 0
        cur = p_ref[...]
        prv = jnp.where(has_prev, pp_ref[...], 0.0)
        tpos = (i * BLOCK + _iota((BLOCK, 1), 0) + 1).astype(F32)
        for g, w in enumerate(POOL_WINDOWS):
            sl = slice(g * POOL_GROUP, (g + 1) * POOL_GROUP)
            pooled = _pool_sums(prv[:, sl], cur[:, sl], w) / jnp.minimum(tpos, float(w)) - cur[:, sl]
            cat_ref[:, sl] = (_nn(pooled, pw_ref[g]) * ps_ref[:, sl]).astype(cat_ref.dtype)
        q = _rope(cur[:, POOL_DIM:POOL_DIM + Q_DIM], c_ref[...], s_ref[...])
        kc = _rope(cur[:, POOL_DIM + Q_DIM:POOL_DIM + Q_DIM + KV_DIM], c_ref[...], s_ref[...])
        kp = _rope(prv[:, POOL_DIM + Q_DIM:POOL_DIM + Q_DIM + KV_DIM], cp_ref[...], sp_ref[...])
        vc = cur[:, POOL_DIM + Q_DIM + KV_DIM:]
        vp = prv[:, POOL_DIM + Q_DIM + KV_DIM:]
        ri = _iota((GQ * BLOCK, BLOCK), 0) % BLOCK
        cj = _iota((GQ * BLOCK, BLOCK), 1)
        mc = cj <= ri
        mp = jnp.logical_and(cj > ri, has_prev)
        outs, lses = [], []
        for g in range(N_KV_HEADS):
            hs = slice(g * HEAD_DIM, (g + 1) * HEAD_DIM)
            qg = _stack_heads(q, g) * scale
            sc = jnp.where(mc, _nt(qg, kc[:, hs]), NEG)
            sp = jnp.where(mp, _nt(qg, kp[:, hs]), NEG)
            sink = jnp.concatenate([jnp.full((BLOCK, 1), sk_ref[GQ * g + r], F32) for r in range(GQ)], axis=0)
            m = jnp.maximum(jnp.maximum(jnp.max(sc, axis=1, keepdims=True), jnp.max(sp, axis=1, keepdims=True)), sink)
            pc = jnp.exp(sc - m)
            pp = jnp.exp(sp - m)
            den = jnp.sum(pc, axis=1, keepdims=True) + jnp.sum(pp, axis=1, keepdims=True) + jnp.exp(sink - m)
            o = (_nn(pc, vc[:, hs]) + _nn(pp, vp[:, hs])) / den
            lse = m + jnp.log(den)
            for r in range(GQ):
                outs.append(o[r * BLOCK:(r + 1) * BLOCK])
                lses.append(lse[r * BLOCK:(r + 1) * BLOCK])
        attn = jnp.concatenate(outs, axis=1)
        at_ref[...] = attn
        cat_ref[:, POOL_DIM:] = attn.astype(cat_ref.dtype)
        lane = _iota((BLOCK, LANES), 1)
        lrow = jnp.zeros((BLOCK, LANES), F32)
        for h in range(N_HEADS):
            lrow = jnp.where(lane == h, lses[h], lrow)
        lse_ref[...] = lrow

    cur = lambda w: pl.BlockSpec((BLOCK, w), lambda i: (i, 0))
    prv = lambda w: pl.BlockSpec((BLOCK, w), lambda i: (jnp.maximum(i - 1, 0), 0))
    return pl.pallas_call(
        body, grid=(nb,),
        in_specs=[cur(MIX_IN_DIM), prv(MIX_IN_DIM), cur(LANES), cur(LANES), prv(LANES), prv(LANES),
                  pl.BlockSpec((4, POOL_GROUP, POOL_GROUP), lambda i: (0, 0, 0)), pl.BlockSpec((1, POOL_DIM), lambda i: (0, 0)),
                  pl.BlockSpec(memory_space=pltpu.SMEM)],
        out_specs=[cur(2 * POOL_DIM), cur(Q_DIM), cur(LANES)],
        out_shape=[_sds((T, 2 * POOL_DIM), MXU), _sds((T, Q_DIM)), _sds((T, LANES))],
        compiler_params=_cp("parallel"), name=name)(proj, proj, cos, sin_s, cos, sin_s, pool_w, pool_scale, sinks)


def _mixcore_bwd(proj, cos, sin_s, pool_w, pool_scale, sinks, attn, lse, dcat, name):
    T = proj.shape[0]
    nb = T // BLOCK
    scale = HEAD_DIM ** -0.5
    QO, KO, VO = POOL_DIM, POOL_DIM + Q_DIM, POOL_DIM + Q_DIM + KV_DIM

    def body(p_ref, pp_ref, pn_ref, c_ref, s_ref, cp_ref, sp_ref, cn_ref, sn_ref, pw_ref, ps_ref, sk_ref,
             at_ref, atn_ref, l_ref, ln_ref, d_ref, dn_ref, dp_ref, dpw_ref, dps_ref, dsk_ref):
        i = pl.program_id(0)
        has_prev = i > 0
        has_next = i < nb - 1
        cur = p_ref[...]
        prv = jnp.where(has_prev, pp_ref[...], 0.0)
        d_cur = d_ref[...]
        d_nxt = jnp.where(has_next, dn_ref[...], 0.0)

        tpos = (i * BLOCK + _iota((BLOCK, 1), 0) + 1).astype(F32)
        tpos2 = (i * BLOCK + _iota((2 * BLOCK, 1), 0) + 1).astype(F32)
        ps = ps_ref[...]
        dps_parts, dpw_parts = [], []
        for g, w in enumerate(POOL_WINDOWS):
            sl = slice(g * POOL_GROUP, (g + 1) * POOL_GROUP)
            pooled = _pool_sums(prv[:, sl], cur[:, sl], w) / jnp.minimum(tpos, float(w)) - cur[:, sl]
            mixed = _nn(pooled, pw_ref[g])
            dps_parts.append(jnp.sum(d_cur[:, sl] * mixed, axis=0, keepdims=True))
            dm2 = jnp.concatenate([d_cur[:, sl], d_nxt[:, sl]], axis=0) * ps[:, sl]
            dpw_parts.append(_tn(pooled, dm2[:BLOCK]))
            dpool2 = _nt(dm2, pw_ref[g])
            e = dpool2 / jnp.minimum(tpos2, float(w))
            sh = 1
            while sh < w:
                e = e + pltpu.roll(e, 2 * BLOCK - sh, 0)
                sh *= 2
            dp_ref[:, sl] = (e[:BLOCK] - dpool2[:BLOCK]).astype(dp_ref.dtype)
        dpsp = jnp.concatenate(dps_parts, axis=1)

        nxt = pn_ref[...]
        q = _rope(cur[:, QO:KO], c_ref[...], s_ref[...])
        qn = _rope(nxt[:, QO:KO], cn_ref[...], sn_ref[...])
        kc = _rope(cur[:, KO:VO], c_ref[...], s_ref[...])
        kp = _rope(prv[:, KO:VO], cp_ref[...], sp_ref[...])
        vc, vp = cur[:, VO:], prv[:, VO:]
        do, don = d_cur[:, POOL_DIM:], d_nxt[:, POOL_DIM:]
        dl = do * at_ref[...]
        dln = don * atn_ref[...]
        lse, lsen = l_ref[...], ln_ref[...]
        ri = _iota((GQ * BLOCK, BLOCK), 0) % BLOCK
        cj = _iota((GQ * BLOCK, BLOCK), 1)
        mc = cj <= ri
        mp = jnp.logical_and(cj > ri, has_prev)
        mn = jnp.logical_and(cj > ri, has_next)
        dq_parts, dk_parts, dv_parts, dsk_vals = [], [], [], []
        for g in range(N_KV_HEADS):
            hs = slice(g * HEAD_DIM, (g + 1) * HEAD_DIM)
            qg, qng = _stack_heads(q, g) * scale, _stack_heads(qn, g) * scale
            dog, dong = _stack_heads(do, g), _stack_heads(don, g)
            delta = jnp.sum(_stack_heads(dl, g), axis=1, keepdims=True)
            deltan = jnp.sum(_stack_heads(dln, g), axis=1, keepdims=True)
            lg, lng = _stack_cols(lse, g), _stack_cols(lsen, g)
            pc = jnp.where(mc, jnp.exp(_nt(qg, kc[:, hs]) - lg), 0.0)
            pp = jnp.where(mp, jnp.exp(_nt(qg, kp[:, hs]) - lg), 0.0)
            pn = jnp.where(mn, jnp.exp(_nt(qng, kc[:, hs]) - lng), 0.0)
            dsc = pc * (_nt(dog, vc[:, hs]) - delta)
            dsp = pp * (_nt(dog, vp[:, hs]) - delta)
            dsn = pn * (_nt(dong, vc[:, hs]) - deltan)
            dqg = (_nn(dsc, kc[:, hs]) + _nn(dsp, kp[:, hs])) * scale
            dq_parts += [dqg[r * BLOCK:(r + 1) * BLOCK] for r in range(GQ)]
            dk_parts.append(_tn(dsc, qg) + _tn(dsn, qng))
            dv_parts.append(_tn(pc, dog) + _tn(pn, dong))
            sink = jnp.concatenate([jnp.full((BLOCK, 1), sk_ref[GQ * g + r], F32) for r in range(GQ)], axis=0)
            dsk = -jnp.exp(sink - lg) * delta
            dsk_vals += [jnp.sum(dsk[r * BLOCK:(r + 1) * BLOCK], axis=0, keepdims=True) for r in range(GQ)]
        dq = _rope(jnp.concatenate(dq_parts, axis=1), c_ref[...], s_ref[...], inverse=True)
        dk = _rope(jnp.concatenate(dk_parts, axis=1), c_ref[...], s_ref[...], inverse=True)
        dp_ref[:, QO:KO] = dq.astype(dp_ref.dtype)
        dp_ref[:, KO:VO] = dk.astype(dp_ref.dtype)
        dp_ref[:, VO:] = jnp.concatenate(dv_parts, axis=1).astype(dp_ref.dtype)
        lane = _iota((1, LANES), 1)
        dskp = jnp.zeros((1, LANES), F32)
        for h in range(N_HEADS):
            dskp = jnp.where(lane == h, dsk_vals[h], dskp)

        @pl.when(i == 0)
        def _():
            dps_ref[...] = dpsp
            dsk_ref[...] = dskp
            for g in range(4):
                dpw_ref[g] = dpw_parts[g]

        @pl.when(i > 0)
        def _():
            dps_ref[...] += dpsp
            dsk_ref[...] += dskp
            for g in range(4):
                dpw_ref[g] += dpw_parts[g]

    cur = lambda w: pl.BlockSpec((BLOCK, w), lambda i: (i, 0))
    prv = lambda w: pl.BlockSpec((BLOCK, w), lambda i: (jnp.maximum(i - 1, 0), 0))
    nxt = lambda w: pl.BlockSpec((BLOCK, w), lambda i: (jnp.minimum(i + 1, nb - 1), 0))
    return pl.pallas_call(
        body, grid=(nb,),
        in_specs=[cur(MIX_IN_DIM), prv(MIX_IN_DIM), nxt(MIX_IN_DIM),
                  cur(LANES), cur(LANES), prv(LANES), prv(LANES), nxt(LANES), nxt(LANES),
                  pl.BlockSpec((4, POOL_GROUP, POOL_GROUP), lambda i: (0, 0, 0)), pl.BlockSpec((1, POOL_DIM), lambda i: (0, 0)),
                  pl.BlockSpec(memory_space=pltpu.SMEM),
                  cur(Q_DIM), nxt(Q_DIM), cur(LANES), nxt(LANES), cur(2 * POOL_DIM), nxt(2 * POOL_DIM)],
        out_specs=[cur(MIX_IN_DIM), pl.BlockSpec((4, POOL_GROUP, POOL_GROUP), lambda i: (0, 0, 0)),
                   pl.BlockSpec((1, POOL_DIM), lambda i: (0, 0)), pl.BlockSpec((1, LANES), lambda i: (0, 0))],
        out_shape=[_sds((T, MIX_IN_DIM), MXU), _sds((4, POOL_GROUP, POOL_GROUP)), _sds((1, POOL_DIM)), _sds((1, LANES))],
        compiler_params=_cp("arbitrary"), name=name)(
            proj, proj, proj, cos, sin_s, cos, sin_s, cos, sin_s, pool_w, pool_scale, sinks, attn, attn, lse, lse, dcat, dcat)


GROUP_W = SSM_D_INNER // SSM_GROUPS


def _ssm_in_conv(h, wT, row_off, cw, cb, name):
    T, D = h.shape
    tm = min(T, 256)
    tc = 1024
    K = SSM_CONV

    def body(a_ref, b_ref, w_ref, c_ref, x_ref, pre_ref, act_ref, halo):
        @pl.when(pl.program_id(1) == 0)
        def _():
            halo[...] = jnp.zeros(halo.shape, F32)

        cur = _nt(a_ref[...], b_ref[...])
        x_ref[...] = cur
        pre = _conv_rows(cur, halo[...], w_ref[...], c_ref[...], K)
        halo[...] = cur[tm - SUBLANES:]
        pre_ref[...] = pre
        act_ref[...] = _silu(pre)

    blk = pl.BlockSpec((tm, tc), lambda j, i: (i, j))
    return pl.pallas_call(
        body, grid=(SSM_CONV_DIM // tc, T // tm),
        in_specs=[pl.BlockSpec((tm, D), lambda j, i: (i, 0)), pl.BlockSpec((tc, D), lambda j, i: (j + row_off // tc, 0)),
                  pl.BlockSpec((K, tc), lambda j, i: (0, j)), pl.BlockSpec((1, tc), lambda j, i: (0, j))],
        out_specs=[blk, blk, blk], out_shape=[_sds((T, SSM_CONV_DIM))] * 3,
        scratch_shapes=[pltpu.VMEM((SUBLANES, tc), F32)],
        compiler_params=_cp("arbitrary", "arbitrary"), name=name)(h, wT, cw, cb)


def _dot_hi(a, b):
    return jnp.dot(a, b, precision=HI, preferred_element_type=F32)


def _ssd_common(dtraw, bias, alog):
    L = SSM_CHUNK
    xb = dtraw + bias
    dt = jnp.maximum(xb, 0.0) + jnp.log1p(jnp.exp(-jnp.abs(xb)))
    A = -jnp.exp(alog)
    tril = (_iota((L, L), 1) <= _iota((L, L), 0)).astype(F32)
    acs = _dot_hi(tril, dt * A)
    return xb, dt, A, tril, acs


def _head_selectors():
    es = (_iota((LANES, SSM_D_INNER), 0) == _iota((LANES, SSM_D_INNER), 1) // HEAD_DIM).astype(BF16)
    est = (_iota((SSM_D_INNER, LANES), 1) == _iota((SSM_D_INNER, LANES), 0) // HEAD_DIM).astype(BF16)
    return es, est


def _dot_sel(v, sel):
    hi = v.astype(BF16)
    r1 = v - hi.astype(F32)
    mid = r1.astype(BF16)
    lo = (r1 - mid.astype(F32)).astype(BF16)
    d = lambda a: jnp.dot(a, sel, preferred_element_type=F32)
    return (d(hi) + d(mid)) + d(lo)


def _expand_heads(v, es):
    return _dot_sel(v, es)


def _reduce_heads(q, est):
    return _dot_sel(q, est)


def _per_state_row(v, g):
    return jnp.concatenate([jnp.broadcast_to(v[:, GQ * g + r:GQ * g + r + 1], (HEAD_DIM, 1)) for r in range(GQ)], axis=0)


def _ssd_fwd(xact, dtraw, dt_bias, a_log, z, d_skip, nw, name):
    T = xact.shape[0]
    nc = T // SSM_CHUNK
    L = SSM_CHUNK
    BO, CO = SSM_D_INNER, SSM_D_INNER + SSM_GROUPS * SSM_STATE

    def body(x_ref, dt_ref, bias_ref, al_ref, es_ref, z_ref, dsk_ref, nw_ref, y_ref, st_ref, yn_ref, state):
        @pl.when(pl.program_id(0) == 0)
        def _():
            state[...] = jnp.zeros(state.shape, F32)

        _, dt, A, tril, acs = _ssd_common(dt_ref[...], bias_ref[...], al_ref[...])
        acsT = acs.T
        last = acs[L - 1:L, :]
        cd = jnp.exp(last)
        es = es_ref[...]
        dtX = _expand_heads(dt, es)
        EX = _expand_heads(jnp.exp(acs), es)
        decX = _expand_heads(jnp.exp(last - acs), es)
        for g in range(SSM_GROUPS):
            gs = slice(g * GROUP_W, (g + 1) * GROUP_W)
            B = x_ref[:, BO + g * SSM_STATE:BO + (g + 1) * SSM_STATE]
            C = x_ref[:, CO + g * SSM_STATE:CO + (g + 1) * SSM_STATE]
            X = x_ref[:, gs] * dtX[:, gs]
            CB = _nt(C, B)
            yd = []
            for r in range(GQ):
                h = GQ * g + r
                Lm = jnp.exp(jnp.where(tril > 0, acs[:, h:h + 1] - acsT[h:h + 1, :], NEG))
                yd.append(_nn(CB * Lm, X[:, r * HEAD_DIM:(r + 1) * HEAD_DIM]))
            S = state[g]
            st_ref[g] = S
            y_ref[:, gs] = jnp.concatenate(yd, axis=1) + _nt(C, S) * EX[:, gs]
            state[g] = S * _per_state_row(cd, g) + _tn(X * decX[:, gs], B)
        y2 = (y_ref[...] + dsk_ref[...] * x_ref[:, :SSM_D_INNER]) * _silu(z_ref[...])
        r = lax.rsqrt(jnp.mean(y2 * y2, axis=-1, keepdims=True) + SSM_NORM_EPS)
        yn_ref[...] = (y2 * r * nw_ref[...]).astype(yn_ref.dtype)

    es, _ = _head_selectors()
    row = pl.BlockSpec((L, SSM_D_INNER), lambda c: (c, 0))
    vec = pl.BlockSpec((1, SSM_D_INNER), lambda c: (0, 0))
    return pl.pallas_call(
        body, grid=(nc,),
        in_specs=[pl.BlockSpec((L, SSM_CONV_DIM), lambda c: (c, 0)), pl.BlockSpec((L, LANES), lambda c: (c, 0)),
                  pl.BlockSpec((1, LANES), lambda c: (0, 0)), pl.BlockSpec((1, LANES), lambda c: (0, 0)),
                  pl.BlockSpec((LANES, SSM_D_INNER), lambda c: (0, 0)), row, vec, vec],
        out_specs=[row, pl.BlockSpec((None, SSM_GROUPS, GROUP_W, SSM_STATE), lambda c: (c, 0, 0, 0)), row],
        out_shape=[_sds((T, SSM_D_INNER)), _sds((nc, SSM_GROUPS, GROUP_W, SSM_STATE)), _sds((T, SSM_D_INNER), MXU)],
        scratch_shapes=[pltpu.VMEM((SSM_GROUPS, GROUP_W, SSM_STATE), F32)],
        compiler_params=_cp("arbitrary"), name=name)(xact, dtraw, dt_bias, a_log, es, z, d_skip, nw)


def _ssd_bwd(xact, xbc, xpre, cw, dtraw, dt_bias, a_log, d_skip, states, dyn, y, z, nw, name):
    T = xact.shape[0]
    nc = T // SSM_CHUNK
    L = SSM_CHUNK
    K = SSM_CONV
    BO, CO = SSM_D_INNER, SSM_D_INNER + SSM_GROUPS * SSM_STATE

    def body(x_ref, xin_ref, pre_ref, cw_ref, dt_ref, bias_ref, al_ref, dsk_ref, es_ref, est_ref, st_ref, dn_ref, y_ref,
             z_ref, nw_ref, dxbc_ref, dcw_ref, dcb_ref, ddt_ref, dbias_ref, dal_ref, dd_ref, dz_ref, dnw_ref,
             dstate, qa, qx, dxp_ref, ahead, dyv):
        cc = pl.program_id(0)

        @pl.when(cc == 0)
        def _():
            dstate[...] = jnp.zeros(dstate.shape, F32)
            ahead[...] = jnp.zeros(ahead.shape, F32)

        zv = z_ref[...]
        sz = _silu(zv)
        yg = y_ref[...] + dsk_ref[...] * x_ref[:, :SSM_D_INNER]
        y2 = yg * sz
        rn = lax.rsqrt(jnp.mean(y2 * y2, axis=-1, keepdims=True) + SSM_NORM_EPS)
        y2h = y2 * rn
        dn = dn_ref[...]
        gy = dn * nw_ref[...]
        dy2 = rn * (gy - y2h * jnp.mean(gy * y2h, axis=-1, keepdims=True))
        dyv[...] = dy2 * sz
        dz_ref[...] = (dy2 * yg * _dsilu(zv)).astype(dz_ref.dtype)
        dnwp = jnp.sum(dn * y2h, axis=0, keepdims=True)

        xb, dt, A, tril, acs = _ssd_common(dt_ref[...], bias_ref[...], al_ref[...])
        acsT = acs.T
        last = acs[L - 1:L, :]
        cd = jnp.exp(last)
        es, est = es_ref[...], est_ref[...]
        dtX = _expand_heads(dt, es)
        EX = _expand_heads(jnp.exp(acs), es)
        decX = _expand_heads(jnp.exp(last - acs), es)
        lane1 = _iota((1, LANES), 1)
        lane = _iota((L, LANES), 1)
        sub = _iota((L, LANES), 0)
        ztot = jnp.zeros((1, LANES), F32)
        wrow = jnp.zeros((L, LANES), F32)
        wcolT = jnp.zeros((LANES, L), F32)
        rows_dec, rows_dd = [], []
        for g in range(SSM_GROUPS):
            gs = slice(g * GROUP_W, (g + 1) * GROUP_W)
            x = x_ref[:, gs]
            B = x_ref[:, BO + g * SSM_STATE:BO + (g + 1) * SSM_STATE]
            C = x_ref[:, CO + g * SSM_STATE:CO + (g + 1) * SSM_STATE]
            dY = dyv[:, gs]
            dtx, e_x, dec_x = dtX[:, gs], EX[:, gs], decX[:, gs]
            X = x * dtx
            CB = _nt(C, B)
            S = st_ref[g]
            dS_out = dstate[g]
            dcb_sum = jnp.zeros((L, L), F32)
            dxd = []
            for r in range(GQ):
                h = GQ * g + r
                hs = slice(r * HEAD_DIM, (r + 1) * HEAD_DIM)
                Lm = jnp.exp(jnp.where(tril > 0, acs[:, h:h + 1] - acsT[h:h + 1, :], NEG))
                M = CB * Lm
                dM = _nt(dY[:, hs], X[:, hs])
                dxd.append(_tn(M, dY[:, hs]))
                dcb_sum = dcb_sum + dM * Lm
                Wm = dM * M
                wrow = jnp.where(lane == h, jnp.sum(Wm, axis=1, keepdims=True), wrow)
                wcolT = jnp.where(sub == h, jnp.sum(Wm, axis=0, keepdims=True), wcolT)
            dXd = jnp.concatenate(dxd, axis=1)
            G = _nt(C, S)
            dG = dY * e_x
            dDX = _nt(B, dS_out)
            dX = dXd + dec_x * dDX
            t_dec = dDX * X * dec_x
            qa[:, gs] = dG * G - t_dec
            qx[:, gs] = dX * x
            rows_dec.append(jnp.sum(t_dec, axis=0, keepdims=True))
            rows_dd.append(jnp.sum(dY * x, axis=0, keepdims=True))
            zc = jnp.sum(dS_out * S, axis=1, keepdims=True)
            for r in range(GQ):
                ztot = jnp.where(lane1 == GQ * g + r, jnp.sum(zc[r * HEAD_DIM:(r + 1) * HEAD_DIM], axis=0, keepdims=True), ztot)
            dxp_ref[:, gs] = dX * dtx + dY * dsk_ref[:, gs]
            dxp_ref[:, BO + g * SSM_STATE:BO + (g + 1) * SSM_STATE] = _tn(dcb_sum, C) + _nn(X * dec_x, dS_out)
            dxp_ref[:, CO + g * SSM_STATE:CO + (g + 1) * SSM_STATE] = _nn(dcb_sum, B) + _nn(dG, S)
            dstate[g] = dS_out * _per_state_row(cd, g) + _tn(dG, C)
        rows = jnp.concatenate([jnp.concatenate(rows_dec, axis=1), jnp.concatenate(rows_dd, axis=1)]
                               + [jnp.zeros((SUBLANES - 2, SSM_D_INNER), F32)], axis=0)
        rsum = _reduce_heads(rows, est)
        dlast = rsum[0:1, :] + cd * ztot
        dacs = (wrow - wcolT.T) + _reduce_heads(qa[...], est) + jnp.where(sub == L - 1, dlast, 0.0)
        triu = (_iota((L, L), 0) <= _iota((L, L), 1)).astype(F32)
        da = _dot_hi(triu, dacs)
        ddtraw = (da * A + _reduce_heads(qx[...], est)) * (1.0 / (1.0 + jnp.exp(-xb)))
        ddt_ref[...] = ddtraw
        dal = jnp.sum(da * dt, axis=0, keepdims=True) * A
        ddp = rsum[1:2, :]
        dbp = jnp.sum(ddtraw, axis=0, keepdims=True)
        w = cw_ref[...]
        d_cur = dxp_ref[...] * _dsilu(pre_ref[...])
        d_nxt = ahead[...]
        ahead[...] = d_cur[:SUBLANES]
        ups = [d_cur] + [_shift_up(d_cur, d_nxt, s) for s in range(1, K)]
        dxc = ups[0] * w[K - 1:K, :]
        for s in range(1, K):
            dxc = dxc + ups[s] * w[K - 1 - s:K - s, :]
        dxbc_ref[...] = dxc.astype(dxbc_ref.dtype)
        xin = xin_ref[...]
        dcwp = jnp.concatenate([jnp.sum(ups[K - 1 - k] * xin, axis=0, keepdims=True) for k in range(K)], axis=0)
        dcbp = jnp.sum(d_cur, axis=0, keepdims=True)

        @pl.when(cc == 0)
        def _():
            dbias_ref[...] = dbp
            dal_ref[...] = dal
            dd_ref[...] = ddp
            dcw_ref[...] = dcwp
            dcb_ref[...] = dcbp
            dnw_ref[...] = dnwp

        @pl.when(cc > 0)
        def _():
            dbias_ref[...] += dbp
            dal_ref[...] += dal
            dd_ref[...] += ddp
            dcw_ref[...] += dcwp
            dcb_ref[...] += dcbp
            dnw_ref[...] += dnwp

    rc = lambda c: nc - 1 - c
    vec = pl.BlockSpec((1, LANES), lambda c: (0, 0))
    wide = pl.BlockSpec((L, SSM_CONV_DIM), lambda c: (rc(c), 0))
    inner = pl.BlockSpec((L, SSM_D_INNER), lambda c: (rc(c), 0))
    es, est = _head_selectors()
    return pl.pallas_call(
        body, grid=(nc,),
        in_specs=[wide, wide, wide, pl.BlockSpec((K, SSM_CONV_DIM), lambda c: (0, 0)),
                  pl.BlockSpec((L, LANES), lambda c: (rc(c), 0)), vec, vec,
                  pl.BlockSpec((1, SSM_D_INNER), lambda c: (0, 0)),
                  pl.BlockSpec((LANES, SSM_D_INNER), lambda c: (0, 0)), pl.BlockSpec((SSM_D_INNER, LANES), lambda c: (0, 0)),
                  pl.BlockSpec((None, SSM_GROUPS, GROUP_W, SSM_STATE), lambda c: (rc(c), 0, 0, 0)),
                  inner, inner, inner, pl.BlockSpec((1, SSM_D_INNER), lambda c: (0, 0))],
        out_specs=[wide, pl.BlockSpec((K, SSM_CONV_DIM), lambda c: (0, 0)), pl.BlockSpec((1, SSM_CONV_DIM), lambda c: (0, 0)),
                   pl.BlockSpec((L, LANES), lambda c: (rc(c), 0)), vec, vec, vec, inner,
                   pl.BlockSpec((1, SSM_D_INNER), lambda c: (0, 0))],
        out_shape=[_sds((T, SSM_CONV_DIM), MXU), _sds((K, SSM_CONV_DIM)), _sds((1, SSM_CONV_DIM)),
                   _sds((T, LANES)), _sds((1, LANES)), _sds((1, LANES)), _sds((1, LANES)),
                   _sds((T, SSM_D_INNER), MXU), _sds((1, SSM_D_INNER))],
        scratch_shapes=[pltpu.VMEM((SSM_GROUPS, GROUP_W, SSM_STATE), F32), pltpu.VMEM((L, SSM_D_INNER), F32),
                        pltpu.VMEM((L, SSM_D_INNER), F32), pltpu.VMEM((L, SSM_CONV_DIM), F32),
                        pltpu.VMEM((SUBLANES, SSM_CONV_DIM), F32), pltpu.VMEM((L, SSM_D_INNER), F32)],
        compiler_params=_cp("arbitrary"), name=name)(xact, xbc, xpre, cw, dtraw, dt_bias, a_log, d_skip, es, est, states, dyn, y,
                                                     z, nw)


def _local_step(x0, cos, sin_s, target, P, fetch, token, send):
    mmf = functools.partial(_mm, tm=1024)
    big, small = {}, {}
    P = dict(P, wup={}, wdn={}, fcw={})
    h0 = _rmsnorm_fwd(x0, P["nm"][0], "norm_mix0", token=token)
    proj0 = mmf(h0, P["wmiT"], tb=True, tn=1280, tk=1024, name="mix_in")
    cat, attn, lse = _mixcore_fwd(proj0, cos, sin_s, P["pool_w"], P["pool_scale"], P["sinks"], "mixcore_fwd")
    x1, hf0 = mmf(cat, P["wmo"], tn=1024, tk=1024, res=x0, norm_w=P["nf"][0], name="mix_out")

    def ffn_fwd(xin, hf, i, **epilogue):
        got = fetch(f"ffn{i}", hf)
        P["wup"][i], P["wdn"][i], P["fcw"][i] = got["wup"], got["wdn"], got["fcw"]
        hid, hc, act = _ffn_up_conv_gate(hf, P["wup"][i], P["fcw"][i], P["fcb"][i], f"ffn_up{i}")
        xout = mmf(act, P["wdn"][i], tn=1024, tk=D_FF, res=xin, name=f"ffn_down{i}", **epilogue)
        return (hid, hc), act, xout

    hid0, act0, (x2, h1) = ffn_fwd(x1, hf0, 0, norm_w=P["nm"][1])
    P.update(fetch("ssm", h1))
    z = mmf(h1, P["wsiT"], tb=True, tn=1024, tk=1024, b_rows=(0, SSM_D_INNER), name="ssm_in_z")
    xbc, xpre, xact = _ssm_in_conv(h1, P["wsiT"], SSM_D_INNER, P["scw"], P["scb"], "ssm_in_xbc")
    dtraw = mmf(h1, P["wdtT"], tb=True, tn=128, tk=1024, name="ssm_in_dt")
    y, states, yn = _ssd_fwd(xact, dtraw, P["dt_bias"], P["a_log"], z, P["d_exp"], P["snorm"], "ssd_fwd")
    x3, hf1 = mmf(yn, P["wso"], tn=1024, tk=SSM_D_INNER, res=x2, norm_w=P["nf"][1], name="ssm_out")
    hid1, act1, (dx4, d_nfin, loss_row) = ffn_fwd(x3, hf1, 1, loss_head=(P["nfin"], target))
    small["norm_final"] = d_nfin

    def ffn_bwd(xin, dxo, hf, hid, act, i):
        big[f"ffn_w_down{i}"] = dwf(act, dxo, tm=1408, tn=1024, name=f"ffn_down_dw{i}").reshape(N_CHIPS, D_FF // N_CHIPS, D_MODEL)
        dhid, dcw, dcb = _ffn_down_dx_mid_bwd(dxo, P["wdn"][i], hid[0], hid[1], P["fcw"][i], f"ffn_down_dx{i}")
        big[f"ffn_w_up{i}"] = dwf(hf, dhid, tm=1024, tn=1408, out_shard_perm=(0, 2, 1, 3), name=f"ffn_up_dw{i}")
        tok = send(f"ffn{i}", [big[f"ffn_w_up{i}"], big[f"ffn_w_down{i}"]])
        dxi, dnf = _mm(dhid, P["wup"][i], tb=True, tm=512, tn=1024, tk=2816, norm_bwd=(xin, P["nf"][i], dxo, tok), name=f"ffn_up_dx{i}")
        return dxi, dnf, dcw, dcb

    dwf = functools.partial(_mm, ta=True, tk=2048, out_dtype=BF16)
    dx3, dnf1, dfcw1, dfcb1 = ffn_bwd(x3, dx4, hf1, hid1, act1, 1)
    dyn = mmf(dx3, P["wso"], tb=True, tn=1024, tk=1024, name="ssm_out_dx")
    big["ssm_w_out"] = dwf(yn, dx3, tm=1024, tn=1024, name="ssm_out_dw").reshape(N_CHIPS, SSM_D_INNER // N_CHIPS, D_MODEL)
    dxbc, d_scw, d_scb, ddtraw, d_dtb, d_alog, d_dskip, dz, d_snorm = _ssd_bwd(
        xact, xbc, xpre, P["scw"], dtraw, P["dt_bias"], P["a_log"], P["d_exp"], states, dyn, y, z, P["snorm"], "ssd_bwd")
    dwsi = dwf(dz, h1, tm=1024, tn=1024, out_into=(None, SSM_IN_DIM, 0), name="ssm_in_dw_z")
    dwsi = dwf(dxbc, h1, tm=1024, tn=1024, out_into=(dwsi, SSM_IN_DIM, SSM_D_INNER // 1024), name="ssm_in_dw_xbc")
    dwdt = dwf(ddtraw, h1, tm=128, tn=1024, name="ssm_in_dw_dt")
    dwsi = _put_rows(dwsi, dwdt, SSM_HEADS, SSM_D_INNER + SSM_CONV_DIM, "ssm_in_dw_put_dt")
    big["ssm_w_in"] = dwsi.reshape(N_CHIPS, SSM_IN_DIM // N_CHIPS, D_MODEL)
    tok = send("ssm", [big["ssm_w_in"], big["ssm_w_out"]])
    dh1 = mmf(dz, P["wsiT"], tn=1024, tk=2048, b_rows=(0, SSM_D_INNER), name="ssm_in_dx_z")
    dh1 = mmf(dxbc, P["wsiT"], tn=1024, tk=2048, b_rows=(SSM_D_INNER // 2048, SSM_CONV_DIM), res=dh1, name="ssm_in_dx_xbc")
    dx2, dnm1 = mmf(ddtraw, P["wdtT"], tn=1024, tk=128, res=dh1, norm_bwd=(x2, P["nm"][1], dx3, tok), name="ssm_in_dx_dt")
    dx1, dnf0, dfcw0, dfcb0 = ffn_bwd(x1, dx2, hf0, hid0, act0, 0)
    dcat = mmf(dx1, P["wmo"], tb=True, tn=1024, tk=1024, name="mix_out_dx")
    big["mix_w_out"] = dwf(cat, dx1, tm=1024, tn=1024, name="mix_out_dw").reshape(N_CHIPS, D_MODEL // N_CHIPS, D_MODEL)
    dproj0, d_pw, d_ps, d_sk = _mixcore_bwd(proj0, cos, sin_s, P["pool_w"], P["pool_scale"], P["sinks"], attn, lse, dcat, "mixcore_bwd")
    big["mix_w_in"] = dwf(dproj0, h0, tm=1280, tn=1024, name="mix_in_dw").reshape(N_CHIPS, MIX_IN_DIM // N_CHIPS, D_MODEL)
    tok = send("mix", [big["mix_w_in"], big["mix_w_out"]])
    dx0, dnm0 = mmf(dproj0, P["wmiT"], tn=1024, tk=1280, norm_bwd=(x0, P["nm"][0], dx1, tok), name="mix_in_dx")

    def unperm_cols(a):
        r = a.shape[0]
        t = a.reshape(r, N_CHIPS, FFN_TC)
        return jnp.stack([t[:, p] for p in _PERM], axis=0)

    small["norm_mix"] = jnp.concatenate([dnm0, dnm1], axis=0)
    small["norm_ffn"] = jnp.concatenate([dnf0, dnf1], axis=0)
    small["pool_w"] = d_pw.reshape(4 * POOL_GROUP, POOL_GROUP)
    small["pool_scale"] = d_ps
    small["attn_sinks"] = d_sk
    small["ssm_dt_bias"] = d_dtb
    small["ssm_A_log"] = d_alog
    small["ssm_D"] = d_dskip
    fcb = jnp.stack([unperm_cols(dfcb0), unperm_cols(dfcb1)], axis=0)
    small["ffn_conv_b"] = fcb.reshape(2, 2 * D_FF)
    small["ssm_conv_w"] = d_scw.reshape(SSM_CONV, N_CHIPS, SSM_CONV_DIM // N_CHIPS).transpose(1, 0, 2)
    small["ssm_conv_b"] = d_scb.reshape(N_CHIPS, 1, SSM_CONV_DIM // N_CHIPS)
    small["ssm_norm"] = d_snorm.reshape(N_CHIPS, 1, SSM_D_INNER // N_CHIPS)
    small["ffn_conv_w"] = jnp.concatenate([unperm_cols(dfcw0), unperm_cols(dfcw1)], axis=1)
    return loss_row, dx0, big, small


ANY = pl.BlockSpec(memory_space=pl.ANY)


def _place():
    return lax.axis_index("x"), lax.axis_index("y"), lax.axis_index("c")


def _gather_shards(shards, name):
    n = len(shards)
    split = [s.size >= (1 << 16) for s in shards]

    def half(ref, a, h):
        shp = shards[a].shape
        if len(shp) == 3:
            return ref.at[h]
        r2 = shp[0] // 2
        return ref.at[pl.ds(pl.multiple_of(h * r2, 2 * SUBLANES), r2), :]

    def body(*refs):
        ins, outs = refs[:n], refs[n:2 * n]
        send, recv, fsend, frecv = refs[2 * n:]
        x, y, c = _place()
        k = 2 * x + y
        chips = [(1 - x, y), (x, 1 - y), (1 - x, 1 - y)]

        def ici(a, j, src_slot_ref, dst_slot):
            px, py = chips[j]
            src = half(src_slot_ref, a, c) if split[a] else src_slot_ref
            dst = half(outs[a].at[dst_slot], a, c) if split[a] else outs[a].at[dst_slot]
            return pltpu.make_async_remote_copy(src, dst, send.at[a, j], recv.at[a, j], device_id=(px, py, c), device_id_type=MESH)

        def d2d(a, j, h):
            px, py = chips[j]
            part = half(outs[a].at[2 * px + py], a, h)
            return pltpu.make_async_remote_copy(part, part, fsend.at[a, j], frecv.at[a, j], device_id=(x, y, 1 - c), device_id_type=MESH)

        sends = [ici(a, j, ins[a], k) for a in range(n) for j in range(3)]
        for cp in sends:
            cp.start()
        passed = []
        for a in range(n):
            for j, (px, py) in enumerate(chips):
                ici(a, j, ins[a], 2 * px + py).wait_recv()
                if split[a]:
                    passed.append(d2d(a, j, c))
                    passed[-1].start()
        for a in range(n):
            if split[a]:
                for j in range(3):
                    d2d(a, j, 1 - c).wait_recv()
        for cp in sends + passed:
            cp.wait_send()

    return pl.pallas_call(
        body, in_specs=[ANY] * n, out_specs=[ANY] * n,
        out_shape=[_sds((N_CHIPS,) + s.shape, s.dtype) for s in shards],
        scratch_shapes=[pltpu.SemaphoreType.DMA((n, 3))] * 4,
        compiler_params=pltpu.CompilerParams(has_side_effects=True), name=name)(*shards)


HBM = pl.BlockSpec(memory_space=pltpu.HBM)
SEM = pl.BlockSpec(memory_space=pltpu.SEMAPHORE)
DATAFLOW = pltpu.SideEffectType.DATAFLOW_SIDE_EFFECTING


def _row_half(ref, h):
    r2 = ref.shape[0] // 2
    return ref.at[pl.ds(pl.multiple_of(h * r2, 2 * SUBLANES), r2), :]


def _spread_start(groups, slot_src, after, name, halved=()):
    flat = [a for grp in groups for a in grp]
    n = len(flat)
    ng = len(groups)
    offs = [sum(len(g) for g in groups[:i]) for i in range(ng)]
    lshape = [(a.shape if slot_src else (N_CHIPS,) + a.shape) for a in flat]

    nsem = 6 * n

    def body(*refs):
        src, land = refs[:n], refs[n:2 * n]
        sems = refs[2 * n + 1:2 * n + 1 + nsem]
        token = refs[-1]
        x, y, c = _place()
        k = 2 * x + y
        chips = [(1 - x, y), (x, 1 - y), (1 - x, 1 - y)]
        for a in range(n):
            half = any(offs[gi] <= a < offs[gi] + len(groups[gi]) for gi in halved)
            for j, (px, py) in enumerate(chips):
                s = src[a].at[2 * px + py] if slot_src else src[a]
                d = land[a].at[k]
                if half:
                    s, d = _row_half(s, c), _row_half(d, c)
                pltpu.make_async_remote_copy(s, d, sems[6 * a + 2 * j], sems[6 * a + 2 * j + 1],
                                             device_id=(px, py, c), device_id_type=MESH).start()
        token[...] = jnp.zeros(token.shape, token.dtype)

    out_shape = [pltpu.SemaphoreType.DMA(())] * nsem
    out_shape += [pltpu.HBM(a.shape, a.dtype) for a in flat] + [pltpu.HBM(s, a.dtype) for s, a in zip(lshape, flat)]
    out_shape.append(_sds((SUBLANES, LANES)))
    args = [pltpu.with_memory_space_constraint(a, pltpu.HBM) for a in flat]
    args += [pltpu.with_memory_space_constraint(lax.empty(s, a.dtype), pltpu.HBM) for s, a in zip(lshape, flat)]
    res = pl.pallas_call(
        body, name=name, out_shape=tuple(out_shape), in_specs=[HBM] * (2 * n) + [pl.BlockSpec(memory_space=pl.ANY)],
        out_specs=tuple([SEM] * nsem + [HBM] * (2 * n) + [pl.BlockSpec(memory_space=pltpu.VMEM)]),
        input_output_aliases={i: nsem + i for i in range(2 * n)},
        compiler_params=pltpu.CompilerParams(has_side_effects=DATAFLOW))(*args, after)
    sems, thru, token = res[:nsem], res[nsem:nsem + 2 * n], res[-1]
    out = []
    for gi, grp in enumerate(groups):
        sl = slice(offs[gi], offs[gi] + len(grp))
        out.append((list(sems[6 * offs[gi]:6 * (offs[gi] + len(grp))]), list(thru[:n][sl]), list(thru[n:][sl])))
    return out, token


def _spread_wait(started, slot_src, after, name, halved=False):
    sems, srcs, lands = started
    n = len(srcs)

    def body(*refs):
        src, land = refs[:n], refs[n:2 * n]
        sem = refs[2 * n:2 * n + 6 * n]
        x, y, c = _place()
        chips = [(1 - x, y), (x, 1 - y), (1 - x, 1 - y)]
        for a in range(n):
            for j, (px, py) in enumerate(chips):
                s = src[a].at[2 * px + py] if slot_src else src[a]
                d = land[a].at[2 * px + py]
                if halved:
                    s, d = _row_half(s, c), _row_half(d, c)
                cp = pltpu.make_async_remote_copy(s, d, sem[6 * a + 2 * j], sem[6 * a + 2 * j + 1],
                                                  device_id=(px, py, c), device_id_type=MESH)
                cp.wait_send()
                cp.wait_recv()

    res = pl.pallas_call(
        body, name=name, out_shape=tuple([pltpu.HBM(a.shape, a.dtype) for a in srcs] + [pltpu.HBM(a.shape, a.dtype) for a in lands]),
        in_specs=[HBM] * (2 * n) + [SEM] * (6 * n) + [pl.BlockSpec(memory_space=pl.ANY)], out_specs=tuple([HBM] * (2 * n)),
        input_output_aliases={i: i for i in range(2 * n)},
        compiler_params=pltpu.CompilerParams(has_side_effects=DATAFLOW))(*srcs, *lands, *sems, after)
    return list(res[:n]), list(res[n:])


def _sibling_fill(lands, name):
    n = len(lands)

    def body(*refs):
        bufs = refs[n:2 * n]
        send, recv = refs[2 * n:]
        x, y, c = _place()
        chips = [(1 - x, y), (x, 1 - y), (1 - x, 1 - y)]

        def copy(a, j, h):
            px, py = chips[j]
            part = _row_half(bufs[a].at[2 * px + py], h)
            return pltpu.make_async_remote_copy(part, part, send.at[a, j], recv.at[a, j], device_id=(x, y, 1 - c), device_id_type=MESH)

        sends = [copy(a, j, c) for a in range(n) for j in range(3)]
        for cp in sends:
            cp.start()
        for a in range(n):
            for j in range(3):
                copy(a, j, 1 - c).wait_recv()
        for cp in sends:
            cp.wait_send()

    return pl.pallas_call(
        body, in_specs=[ANY] * n, out_specs=[ANY] * n, out_shape=[_sds(t.shape, t.dtype) for t in lands],
        input_output_aliases={i: i for i in range(n)},
        scratch_shapes=[pltpu.SemaphoreType.DMA((n, 3)), pltpu.SemaphoreType.DMA((n, 3))],
        compiler_params=pltpu.CompilerParams(has_side_effects=True), name=name)(*lands)


def _sibling_exchange(fs, name):
    n = len(fs)

    def body(*refs):
        ins, outs = refs[:n], refs[n:2 * n]
        send, recv = refs[2 * n:]
        x, y, c = _place()
        cps = [pltpu.make_async_remote_copy(ins[a], outs[a], send.at[a], recv.at[a],
                                            device_id=(x, y, 1 - c), device_id_type=MESH) for a in range(n)]
        for cp in cps:
            cp.start()
        for cp in cps:
            cp.wait()

    return pl.pallas_call(
        body, in_specs=[ANY] * n, out_specs=[ANY] * n, out_shape=[_sds(f.shape, f.dtype) for f in fs],
        scratch_shapes=[pltpu.SemaphoreType.DMA((n,)), pltpu.SemaphoreType.DMA((n,))],
        compiler_params=pltpu.CompilerParams(has_side_effects=True), name=name)(*fs)


def _tile2d(rows, cols, budget=2 * 1024 * 1024, step=2 * SUBLANES):
    fits = [t for t in range(step, rows + 1, step) if rows % t == 0 and t * cols * 4 <= budget]
    if fits:
        return fits[-1], cols
    fits = [t for t in range(LANES, cols + 1, LANES) if cols % t == 0 and rows * t * 4 <= budget]
    assert fits, (rows, cols)
    return rows, fits[-1]


def _chip_sum(own, parts, kidx, name):
    _, R, C = parts.shape
    tr, tc = _tile2d(R, C)

    def body(k_ref, o_ref_in, p1_ref, p2_ref, p3_ref, o_ref):
        tot = ((o_ref_in[...].astype(F32) + p1_ref[...].astype(F32)) + p2_ref[...].astype(F32)) + p3_ref[...].astype(F32)
        o_ref[...] = tot.astype(o_ref.dtype)

    def slot(d):
        return pl.BlockSpec((None, tr, tc), lambda i, j, k: ((k[0] + d) % N_CHIPS, i, j))

    return pl.pallas_call(
        body,
        grid_spec=pltpu.PrefetchScalarGridSpec(
            num_scalar_prefetch=1, grid=(R // tr, C // tc), in_specs=[slot(0), slot(1), slot(2), slot(3)],
            out_specs=pl.BlockSpec((tr, tc), lambda i, j, k: (i, j))),
        out_shape=_sds((R, C), BF16), compiler_params=_cp("parallel", "parallel"), name=name)(kidx, own, parts, parts, parts)


def _adamw_math(w, g, m, v):
    m2 = ADAM_B1 * m + (1.0 - ADAM_B1) * g
    v2 = ADAM_B2 * v + (1.0 - ADAM_B2) * (g * g)
    m_hat = m2 / (1.0 - ADAM_B1 ** ADAM_STEP)
    v_hat = v2 / (1.0 - ADAM_B2 ** ADAM_STEP)
    delta = -ADAM_LR * (m_hat / (jnp.sqrt(v_hat) + ADAM_EPS) + ADAM_WD * w)
    return delta, m2, v2


def _adamw(w, m, v, gparts, name):
    Lw, R, C = w.shape
    tr, tc = _tile2d(R, C)
    flat = [h for pair in gparts for h in pair]

    def body(*refs):
        w_ref, m_ref, v_ref = refs[:3]
        g_refs = refs[3:3 + 2 * Lw]
        go_ref, d_ref, mo_ref, vo_ref = refs[3 + 2 * Lw:]
        g = g_refs[0][...].astype(F32) + g_refs[1][...].astype(F32)
        for l in range(1, Lw):
            g = jnp.where(pl.program_id(0) == l, g_refs[2 * l][...].astype(F32) + g_refs[2 * l + 1][...].astype(F32), g)
        d, m2, v2 = _adamw_math(w_ref[...], g, m_ref[...], v_ref[...])
        go_ref[...] = g
        d_ref[...] = d
        mo_ref[...] = m2
        vo_ref[...] = v2

    blk = pl.BlockSpec((None, tr, tc), lambda l, i, j: (l, i, j))
    gblk = pl.BlockSpec((tr, tc), lambda l, i, j: (i, j))
    return pl.pallas_call(
        body, grid=(Lw, R // tr, C // tc), in_specs=[blk, blk, blk] + [gblk] * (2 * Lw), out_specs=[blk] * 4,
        out_shape=[_sds((Lw, R, C))] * 4, compiler_params=_cp("parallel", "parallel", "parallel"), name=name)(w, m, v, *flat)


def _small_adamw(grads, wmv, name):
    n = len(grads)

    def body(*refs):
        g_in, p_in, outs = refs[:n], refs[n:4 * n], refs[4 * n:]
        for a in range(n):
            g = g_in[a][...]
            d_, m2, v2 = _adamw_math(p_in[3 * a][...], g, p_in[3 * a + 1][...], p_in[3 * a + 2][...])
            outs[4 * a][...] = g
            outs[4 * a + 1][...] = d_
            outs[4 * a + 2][...] = m2
            outs[4 * a + 3][...] = v2

    vm = pl.BlockSpec(memory_space=pltpu.VMEM)
    args = list(grads) + [t for tri in wmv for t in tri]
    out_shape = [_sds(g.shape) for g in grads for _ in range(4)]
    return pl.pallas_call(body, in_specs=[vm] * len(args), out_specs=[vm] * len(out_shape), out_shape=out_shape,
                          compiler_params=pltpu.CompilerParams(vmem_limit_bytes=V7X_VMEM_LIMIT), name=name)(*args)


def _small_allreduce(partials, pshapes, loss_row, name):
    n = len(partials)
    gshapes = [p.shape for p in partials] + [loss_row.shape]
    ng = n + 1

    def body(*refs):
        g_in = refs[:ng]
        outs = refs[ng:2 * ng]
        sib = refs[2 * ng:3 * ng]
        pair = refs[3 * ng:4 * ng]
        bufs = refs[4 * ng:5 * ng]
        send1, recv1, send2, recv2 = refs[-4:]
        x, y, c = _place()
        k = 2 * x + y
        chips = [(1 - x, y), (x, 1 - y), (1 - x, 1 - y)]
        swaps = [pltpu.make_async_remote_copy(g_in[a], sib[a], send1.at[a], recv1.at[a],
                                              device_id=(x, y, 1 - c), device_id_type=MESH) for a in range(ng)]
        for cp in swaps:
            cp.start()
        for a, cp in enumerate(swaps):
            cp.wait()
            pair[a][...] = g_in[a][...] + sib[a][...]
            bufs[a][k] = pair[a][...]
        sends = [pltpu.make_async_remote_copy(pair[a], bufs[a].at[k], send2.at[a, j], recv2.at[a, j],
                                              device_id=(px, py, c), device_id_type=MESH)
                 for a in range(ng) for j, (px, py) in enumerate(chips)]
        for cp in sends:
            cp.start()
        for a in range(ng):
            for j, (px, py) in enumerate(chips):
                pltpu.make_async_remote_copy(pair[a], bufs[a].at[2 * px + py], send2.at[a, j], recv2.at[a, j],
                                             device_id=(px, py, c), device_id_type=MESH).wait_recv()
        for cp in sends:
            cp.wait_send()
        for a in range(ng):
            sharded = len(gshapes[a]) == 3

            def part(d):
                return bufs[a][d, k] if sharded else bufs[a][d]

            tot = part(0)
            for d in range(1, N_CHIPS):
                tot = tot + part(d)
            if a == n:
                outs[n][...] = tot
            else:
                pr, pc = pshapes[a]
                outs[a][...] = tot[:pr, :pc]

    vm = pl.BlockSpec(memory_space=pltpu.VMEM)
    args = list(partials) + [loss_row]
    out_shape = [_sds(ps) for ps in pshapes] + [_sds(loss_row.shape)]
    return pl.pallas_call(
        body, in_specs=[vm] * len(args), out_specs=[vm] * len(out_shape), out_shape=out_shape,
        scratch_shapes=[pltpu.VMEM(tuple(s), F32) for s in gshapes] * 2 + [pltpu.VMEM((N_CHIPS,) + tuple(s), F32) for s in gshapes]
        + [pltpu.SemaphoreType.DMA((ng,)), pltpu.SemaphoreType.DMA((ng,)),
           pltpu.SemaphoreType.DMA((ng, 3)), pltpu.SemaphoreType.DMA((ng, 3))],
        compiler_params=pltpu.CompilerParams(has_side_effects=True, vmem_limit_bytes=V7X_VMEM_LIMIT), name=name)(*args)


_PERM = (0, 2, 1, 3)


def _cols_from_shards(g):
    return g.transpose(1, 0, 2).reshape(g.shape[1], N_CHIPS * g.shape[2])


def _rope_tables(positions):
    inv_freq = ROPE_THETA ** (-jnp.arange(0, HEAD_DIM, 2, dtype=F32) / HEAD_DIM)
    ang = positions.astype(F32).reshape(-1, 1) * inv_freq
    cos, sin = jnp.cos(ang), jnp.sin(ang)
    cos = jnp.concatenate([cos, cos, cos, cos], axis=-1)
    sin_s = jnp.concatenate([-sin, sin, -sin, sin], axis=-1)
    return cos, sin_s


def kernel(x, positions, norm_mix, norm_ffn, norm_final, mix_w_in, pool_w, pool_scale, attn_sinks, mix_w_out, ssm_w_in, ssm_conv_w, ssm_conv_b, ssm_dt_bias, ssm_A_log, ssm_D, ssm_norm, ssm_w_out, ffn_w_up, ffn_conv_w, ffn_conv_b, ffn_w_down, loss_target, m_norm_mix, m_norm_ffn, m_norm_final, m_mix_w_in, m_pool_w, m_pool_scale, m_attn_sinks, m_mix_w_out, m_ssm_w_in, m_ssm_conv_w, m_ssm_conv_b, m_ssm_dt_bias, m_ssm_A_log, m_ssm_D, m_ssm_norm, m_ssm_w_out, m_ffn_w_up, m_ffn_conv_w, m_ffn_conv_b, m_ffn_w_down, v_norm_mix, v_norm_ffn, v_norm_final, v_mix_w_in, v_pool_w, v_pool_scale, v_attn_sinks, v_mix_w_out, v_ssm_w_in, v_ssm_conv_w, v_ssm_conv_b, v_ssm_dt_bias, v_ssm_A_log, v_ssm_D, v_ssm_norm, v_ssm_w_out, v_ffn_w_up, v_ffn_conv_w, v_ffn_conv_b, v_ffn_w_down):
    W = dict(norm_mix=norm_mix, norm_ffn=norm_ffn, norm_final=norm_final, mix_w_in=mix_w_in, pool_w=pool_w, pool_scale=pool_scale, attn_sinks=attn_sinks, mix_w_out=mix_w_out, ssm_w_in=ssm_w_in, ssm_conv_w=ssm_conv_w, ssm_conv_b=ssm_conv_b, ssm_dt_bias=ssm_dt_bias, ssm_A_log=ssm_A_log, ssm_D=ssm_D, ssm_norm=ssm_norm, ssm_w_out=ssm_w_out, ffn_w_up=ffn_w_up, ffn_conv_w=ffn_conv_w, ffn_conv_b=ffn_conv_b, ffn_w_down=ffn_w_down)
    Mo = dict(norm_mix=m_norm_mix, norm_ffn=m_norm_ffn, norm_final=m_norm_final, mix_w_in=m_mix_w_in, pool_w=m_pool_w, pool_scale=m_pool_scale, attn_sinks=m_attn_sinks, mix_w_out=m_mix_w_out, ssm_w_in=m_ssm_w_in, ssm_conv_w=m_ssm_conv_w, ssm_conv_b=m_ssm_conv_b, ssm_dt_bias=m_ssm_dt_bias, ssm_A_log=m_ssm_A_log, ssm_D=m_ssm_D, ssm_norm=m_ssm_norm, ssm_w_out=m_ssm_w_out, ffn_w_up=m_ffn_w_up, ffn_conv_w=m_ffn_conv_w, ffn_conv_b=m_ffn_conv_b, ffn_w_down=m_ffn_w_down)
    Vo = dict(norm_mix=v_norm_mix, norm_ffn=v_norm_ffn, norm_final=v_norm_final, mix_w_in=v_mix_w_in, pool_w=v_pool_w, pool_scale=v_pool_scale, attn_sinks=v_attn_sinks, mix_w_out=v_mix_w_out, ssm_w_in=v_ssm_w_in, ssm_conv_w=v_ssm_conv_w, ssm_conv_b=v_ssm_conv_b, ssm_dt_bias=v_ssm_dt_bias, ssm_A_log=v_ssm_A_log, ssm_D=v_ssm_D, ssm_norm=v_ssm_norm, ssm_w_out=v_ssm_w_out, ffn_w_up=v_ffn_w_up, ffn_conv_w=v_ffn_conv_w, ffn_conv_b=v_ffn_conv_b, ffn_w_down=v_ffn_w_down)

    kchip = 2 * lax.axis_index("x") + lax.axis_index("y")

    def own_slot(g, own):
        return lax.dynamic_update_slice_in_dim(g, own[None], kchip, axis=0)

    def tr(t):
        return jnp.swapaxes(t[0], 0, 1)

    later = dict(ffn0=[ffn_w_up[0].astype(MXU), ffn_w_down[0].astype(MXU)],
                 ssm=[tr(ssm_w_in).astype(MXU), ssm_w_out[0].astype(MXU)],
                 ffn1=[ffn_w_up[1].astype(MXU), ffn_w_down[1].astype(MXU)])
    sh = [tr(mix_w_in).astype(MXU), mix_w_out[0].astype(MXU), ssm_conv_w[0], ssm_conv_b, ssm_norm, ffn_conv_w]
    first = _gather_shards(sh, "gather_first")
    g_mi, g_mo, g_scw, g_scb, g_sn, g_fcw = [own_slot(g, own) for g, own in zip(first, sh)]
    started, token = _spread_start(list(later.values()), False, first[0], "gather_start", halved=(0,))
    started = dict(zip(later.keys(), started))
    fcw = [jnp.concatenate([g_fcw[p, i] for p in _PERM], axis=1) for i in range(2)]
    P = dict(
        nm=norm_mix, nf=norm_ffn, nfin=norm_final,
        wmiT=g_mi.reshape(MIX_IN_DIM, D_MODEL), wmo=g_mo.reshape(D_MODEL, D_MODEL),
        pool_w=pool_w[0], pool_scale=pool_scale, sinks=attn_sinks[0],
        scw=_cols_from_shards(g_scw), scb=g_scb.reshape(1, SSM_CONV_DIM), snorm=g_sn.reshape(1, SSM_D_INNER),
        dt_bias=jnp.pad(ssm_dt_bias, ((0, 0), (0, LANES - SSM_HEADS))), a_log=jnp.pad(ssm_A_log, ((0, 0), (0, LANES - SSM_HEADS))),
        d_exp=jnp.repeat(ssm_D, SSM_D_INNER // SSM_HEADS, axis=1),
        fcb=[jnp.concatenate([ffn_conv_b[i:i + 1, p * FFN_TC:(p + 1) * FFN_TC] for p in _PERM], axis=1) for i in range(2)],
    )

    def fetch(group, after):
        owns, lands = _spread_wait(started[group], False, after, f"gather_wait_{group}", halved=group == "ffn0")
        if group == "ffn0":
            lands = _sibling_fill(lands, "gather_fill_ffn0")
        a, b = [own_slot(g, own) for g, own in zip(lands, owns)]
        if group == "ssm":
            wsi = a.reshape(SSM_IN_DIM, D_MODEL)
            zx = SSM_D_INNER + SSM_CONV_DIM
            return dict(wsiT=wsi, wdtT=jnp.pad(wsi[zx:], ((0, LANES - SSM_HEADS), (0, 0))), wso=b.reshape(SSM_D_INNER, D_MODEL))
        i = int(group[-1])
        return dict(wup=jnp.concatenate([a[p] for p in _PERM], axis=1), wdn=b.reshape(D_FF, D_MODEL), fcw=fcw[i])

    cos, sin_s = _rope_tables(positions)
    sent = {}

    def send(group, grads):
        res, tok = _spread_start([grads], True, jnp.zeros((SUBLANES, LANES), F32), f"grad_start_{group}")
        sent[group] = res[0]
        return tok

    loss_row, grad_x, big, small = _local_step(x[0], cos, sin_s, loss_target[0], P, fetch, token, send)

    kidx = kchip.astype(jnp.int32).reshape(1)
    group_names = dict(ffn1=["ffn_w_up1", "ffn_w_down1"], ssm=["ssm_w_in", "ssm_w_out"], ffn0=["ffn_w_up0", "ffn_w_down0"],
                       mix=["mix_w_in", "mix_w_out"])
    names, mine = [], []
    for group, started_g in sent.items():
        grads, lands = _spread_wait(started_g, True, grad_x, f"grad_wait_{group}")
        for nm, g, land in zip(group_names[group], grads, lands):
            names.append(nm)
            mine.append(_chip_sum(g, land, kidx, f"chip_sum_{nm}"))
    theirs = _sibling_exchange(mine, "sibling_exchange")
    red = {nm: (a, b) for nm, a, b in zip(names, mine, theirs)}

    out = {}

    def big_update(pname, gparts, transposed=False):
        w = W[pname]
        lw = len(gparts)
        shp = w.shape
        rr, cc = gparts[0][0].shape
        fix = (lambda t: tr(t)[None]) if transposed else (lambda t: t.reshape(lw, rr, cc))
        res = _adamw(fix(w), fix(Mo[pname]), fix(Vo[pname]), gparts, f"adamw_{pname}")
        out[pname] = tuple((tr(r)[None] if transposed else r.reshape(shp)) for r in res)

    big_update("mix_w_in", [red["mix_w_in"]], transposed=True)
    big_update("mix_w_out", [red["mix_w_out"]])
    big_update("ssm_w_in", [red["ssm_w_in"]], transposed=True)
    big_update("ssm_w_out", [red["ssm_w_out"]])
    big_update("ffn_w_up", [red["ffn_w_up0"], red["ffn_w_up1"]])
    big_update("ffn_w_down", [red["ffn_w_down0"], red["ffn_w_down1"]])

    small_names = ["norm_mix", "norm_ffn", "norm_final", "pool_w", "pool_scale", "attn_sinks", "ssm_dt_bias", "ssm_A_log",
                   "ssm_D", "ffn_conv_b", "ssm_conv_w", "ssm_conv_b", "ssm_norm", "ffn_conv_w"]

    def as2d(t):
        if t.ndim == 1:
            return t.reshape(1, -1)
        return t.reshape(-1, t.shape[-1])

    wmv = [(as2d(W[nm]), as2d(Mo[nm]), as2d(Vo[nm])) for nm in small_names]
    summed = _small_allreduce([small[nm] for nm in small_names], [t[0].shape for t in wmv], loss_row, "small_allreduce")
    res = _small_adamw(summed[:-1], wmv, "small_adamw")
    for a, nm in enumerate(small_names):
        out[nm] = tuple(r.reshape(W[nm].shape) for r in res[4 * a:4 * a + 4])
    loss = summed[-1][0, 0]

    order = ["norm_mix", "norm_ffn", "norm_final", "mix_w_in", "pool_w", "pool_scale", "attn_sinks", "mix_w_out", "ssm_w_in",
             "ssm_conv_w", "ssm_conv_b", "ssm_dt_bias", "ssm_A_log", "ssm_D", "ssm_norm", "ssm_w_out", "ffn_w_up", "ffn_conv_w",
             "ffn_conv_b", "ffn_w_down"]
    return (loss, grad_x.reshape(x.shape), *[out[nm][0] for nm in order], *[out[nm][1] for nm in order],
            *[out[nm][2] for nm in order], *[out[nm][3] for nm in order])
```

```python
import functools

import jax
import jax.numpy as jnp
from jax import lax
from jax.experimental import pallas as pl
from jax.experimental.pallas import tpu as pltpu

F32 = jnp.float32
BF16 = jnp.bfloat16
MXU = BF16
HI = lax.Precision.HIGHEST

D_MODEL = 1024
POOL_WINDOWS = (2, 4, 8, 16)
POOL_DIM = 512
POOL_GROUP = 128
HEAD_DIM = 64
N_HEADS = 8
N_KV_HEADS = 2
GQ = 4
Q_DIM = 512
KV_DIM = 128
BLOCK = 128
ROPE_THETA = 10000.0
MIX_IN_DIM = 1280
SSM_D_INNER = 2048
SSM_HEADS = 32
SSM_GROUPS = 8
SSM_STATE = 128
SSM_CONV = 4
SSM_CHUNK = 128
SSM_CONV_DIM = 4096
SSM_IN_DIM = 6176
D_FF = 2816
FFN_CONV = 3
NORM_EPS = 1e-6
SSM_NORM_EPS = 1e-5
ADAM_LR = 0.001
ADAM_B1 = 0.9
ADAM_B2 = 0.999
ADAM_EPS = 1e-08
ADAM_WD = 0.01
ADAM_STEP = 10

N_CHIPS = 4
LANES = 128
SUBLANES = 8
V7X_VMEM_LIMIT = 56 * 1024 * 1024
NEG = -1e30
MESH = pl.DeviceIdType.MESH


def _cp(*sem):
    return pltpu.CompilerParams(dimension_semantics=sem if sem else None, vmem_limit_bytes=V7X_VMEM_LIMIT)


def _sds(shape, dtype=F32):
    return jax.ShapeDtypeStruct(tuple(shape), dtype)


def _iota(shape, dim):
    return lax.broadcasted_iota(jnp.int32, shape, dim)


def _silu(x):
    return x * (1.0 / (1.0 + jnp.exp(-x)))


def _dsilu(x):
    s = 1.0 / (1.0 + jnp.exp(-x))
    return s * (1.0 + x * (1.0 - s))


def _mm(a, b, *, ta=False, tb=False, tm, tn, tk, res=None, out_dtype=F32, out_shard_perm=None, out_into=None, b_rows=None,
        norm_w=None, norm_bwd=None, loss_head=None, name):
    M, K = (a.shape[1], a.shape[0]) if ta else a.shape
    N = b.shape[0] if tb else b.shape[1]
    boff = 0
    if b_rows is not None:
        boff = b_rows[0]
        if tb:
            N = b_rows[1]
        else:
            K = b_rows[1]
    tm, tn, tk = min(tm, M), min(tn, N), min(tk, K)
    gm, gn, gk = M // tm, N // tn, K // tk
    assert gm * tm == M and gn * tn == N and gk * tk == K, (name, M, N, K, tm, tn, tk)
    a_spec = pl.BlockSpec((tk, tm), lambda i, j, k: (k, i)) if ta else pl.BlockSpec((tm, tk), lambda i, j, k: (i, k))
    b_spec = pl.BlockSpec((tn, tk), lambda i, j, k: (j + boff, k)) if tb else pl.BlockSpec((tk, tn), lambda i, j, k: (k + boff, j))
    dims = (((0 if ta else 1,), (1 if tb else 0,)), ((), ()))
    has_res = res is not None
    has_nw = norm_w is not None
    has_nb = norm_bwd is not None
    has_lh = loss_head is not None
    has_tok = has_nb and norm_bwd[3] is not None
    assert not (has_nw or has_nb or has_lh) or (gn == 1 and out_shard_perm is None)
    n_extra = has_res + has_nw + (3 + has_tok if has_nb else 0) + (2 if has_lh else 0)

    def body(*refs):
        a_ref, b_ref = refs[0], refs[1]
        extra = list(refs[2:2 + n_extra])
        outs = refs[len(args):]
        r_ref = extra.pop(0) if has_res else None
        nw_ref = extra.pop(0) if has_nw else None
        nb_refs = extra if has_nb else None

        def dot():
            return lax.dot_general(a_ref[...].astype(MXU), b_ref[...].astype(MXU), dims, preferred_element_type=F32)

        def accumulate(o_ref, part):
            i = pl.program_id(0)

            @pl.when(i == 0)
            def _():
                o_ref[...] = part

            @pl.when(i > 0)
            def _():
                o_ref[...] += part

        def finish(r):
            if has_res:
                r = r + r_ref[...]
            if has_lh:
                wv = extra[0][...]
                rs = lax.rsqrt(jnp.mean(r * r, axis=-1, keepdims=True) + NORM_EPS)
                xh = r * rs
                e = xh * wv - extra[1][...]
                lpart = 0.5 * jnp.sum(jnp.mean(e * e, axis=-1, keepdims=True), axis=0, keepdims=True)
                dy = e * (1.0 / N)
                g = dy * wv
                outs[0][...] = rs * (g - xh * jnp.mean(g * xh, axis=-1, keepdims=True))
                accumulate(outs[1], jnp.sum(dy * xh, axis=0, keepdims=True))
                accumulate(outs[2], jnp.broadcast_to(lpart, (1, LANES)))
                return
            if has_nb:
                xv = nb_refs[0][...]
                rs = lax.rsqrt(jnp.mean(xv * xv, axis=-1, keepdims=True) + NORM_EPS)
                xh = xv * rs
                g = r * nb_refs[1][...]
                dr = nb_refs[2][...] + nb_refs[3][0:1, 0:1] if has_tok else nb_refs[2][...]
                outs[0][...] = dr + rs * (g - xh * jnp.mean(g * xh, axis=-1, keepdims=True))
                accumulate(outs[1], jnp.sum(r * xh, axis=0, keepdims=True))
                return
            outs[0][...] = r.astype(out_dtype)
            if has_nw:
                rs = lax.rsqrt(jnp.mean(r * r, axis=-1, keepdims=True) + NORM_EPS)
                outs[1][...] = (r * rs * nw_ref[...]).astype(outs[1].dtype)

        if gk == 1:
            finish(dot())
        else:
            acc = refs[-1]
            k = pl.program_id(2)

            @pl.when(k == 0)
            def _():
                acc[...] = dot()

            if gk > 2:
                @pl.when(jnp.logical_and(k > 0, k < gk - 1))
                def _():
                    acc[...] += dot()

            @pl.when(k == gk - 1)
            def _():
                finish(acc[...] + dot())

    tile = pl.BlockSpec((tm, tn), lambda i, j, k: (i, j))
    row = pl.BlockSpec((1, tn), lambda i, j, k: (0, j))
    in_specs = [a_spec, b_spec]
    args = [a, b]
    if has_res:
        in_specs.append(tile)
        args.append(res)
    if has_nw:
        in_specs.append(row)
        args.append(norm_w.reshape(1, N))
    if has_nb:
        in_specs += [tile, row, tile]
        args += [norm_bwd[0], norm_bwd[1].reshape(1, N), norm_bwd[2]]
        if has_tok:
            in_specs.append(pl.BlockSpec((SUBLANES, LANES), lambda i, j, k: (0, 0)))
            args.append(norm_bwd[3])
    if has_lh:
        in_specs += [row, tile]
        args += [loss_head[0].reshape(1, N), loss_head[1]]
    alias = {}
    if out_into is not None:
        buf, rows, off = out_into
        out_spec = pl.BlockSpec((tm, tn), lambda i, j, k: (i + off, j))
        out_shape = _sds((rows, N), out_dtype)
        if buf is not None:
            alias = {len(args): 0}
            in_specs.append(pl.BlockSpec(memory_space=pl.ANY))
            args.append(buf)
    elif out_shard_perm is None:
        out_spec = tile
        out_shape = _sds((M, N), out_dtype)
    else:
        assert gn == len(out_shard_perm) == 4 and tuple(out_shard_perm) == (0, 2, 1, 3)
        out_spec = pl.BlockSpec((None, tm, tn), lambda i, j, k: ((j % 2) * 2 + j // 2, i, 0))
        out_shape = _sds((gn, M, tn), out_dtype)
    sem = ("parallel", "parallel", "arbitrary")
    if has_nw:
        out_spec, out_shape = [out_spec, tile], [out_shape, _sds((M, N), MXU)]
    if has_nb:
        out_spec, out_shape = [tile, row], [_sds((M, N)), _sds((1, N))]
        sem = ("arbitrary", "arbitrary", "arbitrary")
    if has_lh:
        out_spec = [tile, row, pl.BlockSpec((1, LANES), lambda i, j, k: (0, 0))]
        out_shape = [_sds((M, N)), _sds((1, N)), _sds((1, LANES))]
        sem = ("arbitrary", "arbitrary", "arbitrary")
    return pl.pallas_call(
        body, grid=(gm, gn, gk), in_specs=in_specs, out_specs=out_spec, out_shape=out_shape,
        scratch_shapes=[pltpu.VMEM((tm, tn), F32)] if gk > 1 else [], input_output_aliases=alias,
        compiler_params=_cp(*sem), name=name)(*args)


def _put_rows(buf, src, rows, at, name):
    assert at % rows == 0 and src.shape[1] == buf.shape[1] and src.dtype == buf.dtype
    C = buf.shape[1]

    def body(s_ref, b_ref, o_ref):
        o_ref[...] = s_ref[...]

    return pl.pallas_call(
        body, grid=(1,), in_specs=[pl.BlockSpec((rows, C), lambda i: (0, 0)), pl.BlockSpec(memory_space=pl.ANY)],
        out_specs=pl.BlockSpec((rows, C), lambda i: (at // rows, 0)), out_shape=_sds(buf.shape, buf.dtype),
        input_output_aliases={1: 0}, compiler_params=_cp("arbitrary"), name=name)(src, buf)


def _rmsnorm_fwd(x, w, name, token=None):
    T, D = x.shape
    tm = min(T, 512)
    has_token = token is not None

    def body(*refs):
        x_ref, w_ref, o_ref = refs[0], refs[1], refs[-1]
        xv = x_ref[...]
        if has_token:
            xv = xv + refs[2][0:1, 0:1]
        r = lax.rsqrt(jnp.mean(xv * xv, axis=-1, keepdims=True) + NORM_EPS)
        o_ref[...] = (xv * r * w_ref[...]).astype(o_ref.dtype)

    in_specs = [pl.BlockSpec((tm, D), lambda i: (i, 0)), pl.BlockSpec((1, D), lambda i: (0, 0))]
    args = [x, w.reshape(1, D)]
    if has_token:
        in_specs.append(pl.BlockSpec((SUBLANES, LANES), lambda i: (0, 0)))
        args.append(token)
    return pl.pallas_call(
        body, grid=(T // tm,), in_specs=in_specs,
        out_specs=pl.BlockSpec((tm, D), lambda i: (i, 0)), out_shape=_sds((T, D), MXU),
        compiler_params=_cp("parallel"), name=name)(*args)


def _shift_down(cur, prev8, s):
    if s == 0:
        return cur
    tm = cur.shape[0]
    rc = pltpu.roll(cur, s, 0)
    top = jnp.where(_iota((SUBLANES, cur.shape[1]), 0) < s, pltpu.roll(prev8, s, 0), rc[:SUBLANES])
    return jnp.concatenate([top, rc[SUBLANES:]], axis=0) if tm > SUBLANES else top


def _shift_up(cur, next8, s):
    if s == 0:
        return cur
    tm = cur.shape[0]
    rc = pltpu.roll(cur, tm - s, 0)
    bot = jnp.where(_iota((SUBLANES, cur.shape[1]), 0) >= SUBLANES - s, pltpu.roll(next8, SUBLANES - s, 0), rc[tm - SUBLANES:])
    return jnp.concatenate([rc[:tm - SUBLANES], bot], axis=0) if tm > SUBLANES else bot


def _conv_rows(cur, prev8, w, b, K):
    acc = cur * w[K - 1:K, :] + b
    for s in range(1, K):
        acc = acc + _shift_down(cur, prev8, s) * w[K - 1 - s:K - s, :]
    return acc


FFN_TC = 1408


def _ffn_up_conv_gate(hf, wup, cw, cb, name):
    T, D = hf.shape
    tm = min(T, 256)
    nt, nj = T // tm, D_FF // FFN_TC
    K = FFN_CONV
    W2 = 2 * FFN_TC

    def body(a_ref, b_ref, w_ref, c_ref, hid_ref, hc_ref, act_ref, halo):
        i = pl.program_id(1)

        @pl.when(i == 0)
        def _():
            halo[...] = jnp.zeros(halo.shape, F32)

        hb = jnp.dot(a_ref[...].astype(MXU), b_ref[...].astype(MXU), preferred_element_type=F32).astype(hid_ref.dtype)
        hid_ref[...] = hb
        cur = hb.astype(F32)
        hc = _conv_rows(cur, halo[...], w_ref[...], c_ref[...], K)
        halo[...] = cur[tm - SUBLANES:]
        hc_ref[...] = hc
        act_ref[...] = (_silu(hc[:, FFN_TC:]) * hc[:, :FFN_TC]).astype(act_ref.dtype)

    blk = pl.BlockSpec((tm, W2), lambda j, i: (i, j))
    return pl.pallas_call(
        body, grid=(nj, nt),
        in_specs=[pl.BlockSpec((tm, D), lambda j, i: (i, 0)), pl.BlockSpec((D, W2), lambda j, i: (0, j)),
                  pl.BlockSpec((K, W2), lambda j, i: (0, j)), pl.BlockSpec((1, W2), lambda j, i: (0, j))],
        out_specs=[blk, blk, pl.BlockSpec((tm, FFN_TC), lambda j, i: (i, j))],
        out_shape=[_sds((T, 2 * D_FF), MXU), _sds((T, 2 * D_FF)), _sds((T, D_FF), MXU)],
        scratch_shapes=[pltpu.VMEM((SUBLANES, W2), F32)],
        compiler_params=_cp("arbitrary", "arbitrary"), name=name)(hf, wup, cw, cb)


def _ffn_down_dx_mid_bwd(dxo, wdn, hid, hc, cw, name):
    T, D = dxo.shape
    tm = min(T, 256)
    nt, nj = T // tm, D_FF // FFN_TC
    K = FFN_CONV
    W2 = 2 * FFN_TC

    def body(g_ref, wd_ref, h_ref, c_ref, w_ref, dh_ref, dw_ref, db_ref, ahead):
        i = pl.program_id(1)

        @pl.when(i == 0)
        def _():
            ahead[...] = jnp.zeros(ahead.shape, F32)

        w = w_ref[...]
        cur = h_ref[...].astype(F32)
        hcv = c_ref[...]
        dav = _nt(g_ref[...], wd_ref[...])
        u, g = hcv[:, :FFN_TC], hcv[:, FFN_TC:]
        d_cur = jnp.concatenate([dav * _silu(g), dav * u * _dsilu(g)], axis=1)
        d_nxt = ahead[...]
        ahead[...] = d_cur[:SUBLANES]
        ups = [d_cur] + [_shift_up(d_cur, d_nxt, s) for s in range(1, K)]
        dh = ups[0] * w[K - 1:K, :]
        for s in range(1, K):
            dh = dh + ups[s] * w[K - 1 - s:K - s, :]
        dh_ref[...] = dh.astype(dh_ref.dtype)
        dwp = jnp.concatenate([jnp.sum(ups[K - 1 - k] * cur, axis=0, keepdims=True) for k in range(K)], axis=0)
        dbp = jnp.sum(d_cur, axis=0, keepdims=True)

        @pl.when(i == 0)
        def _():
            dw_ref[...] = dwp
            db_ref[...] = dbp

        @pl.when(i > 0)
        def _():
            dw_ref[...] += dwp
            db_ref[...] += dbp

    blk = pl.BlockSpec((tm, W2), lambda j, i: (nt - 1 - i, j))
    return pl.pallas_call(
        body, grid=(nj, nt),
        in_specs=[pl.BlockSpec((tm, D), lambda j, i: (nt - 1 - i, 0)), pl.BlockSpec((FFN_TC, D), lambda j, i: (j, 0)), blk, blk,
                  pl.BlockSpec((K, W2), lambda j, i: (0, j))],
        out_specs=[blk, pl.BlockSpec((K, W2), lambda j, i: (0, j)), pl.BlockSpec((1, W2), lambda j, i: (0, j))],
        out_shape=[_sds((T, 2 * D_FF), MXU), _sds((K, 2 * D_FF)), _sds((1, 2 * D_FF))],
        scratch_shapes=[pltpu.VMEM((SUBLANES, W2), F32)],
        compiler_params=_cp("arbitrary", "arbitrary"), name=name)(dxo, wdn, hid, hc, cw)


def _rope(t, cos, sin_s, inverse=False):
    n = t.shape[1] // LANES
    c = jnp.concatenate([cos] * n, axis=1) if n > 1 else cos
    s = jnp.concatenate([sin_s] * n, axis=1) if n > 1 else sin_s
    a = pltpu.roll(t, HEAD_DIM // 2, 1)
    b = pltpu.roll(t, t.shape[1] - HEAD_DIM // 2, 1)
    first = (_iota(t.shape, 1) % HEAD_DIM) < HEAD_DIM // 2
    rot = jnp.where(first, b, a) * s
    return t * c - rot if inverse else t * c + rot


def _stack_heads(t, g):
    return jnp.concatenate([t[:, (GQ * g + r) * HEAD_DIM:(GQ * g + r + 1) * HEAD_DIM] for r in range(GQ)], axis=0)


def _stack_cols(t, g):
    return jnp.concatenate([t[:, GQ * g + r:GQ * g + r + 1] for r in range(GQ)], axis=0)


def _pool_sums(prev, cur, w):
    s = jnp.concatenate([prev, cur], axis=0)
    sh = 1
    while sh < w:
        s = s + pltpu.roll(s, sh, 0)
        sh *= 2
    return s[BLOCK:]


def _nt(a, b):
    return lax.dot_general(a.astype(MXU), b.astype(MXU), (((1,), (1,)), ((), ())), preferred_element_type=F32)


def _tn(a, b):
    return lax.dot_general(a.astype(MXU), b.astype(MXU), (((0,), (0,)), ((), ())), preferred_element_type=F32)


def _nn(a, b):
    return jnp.dot(a.astype(MXU), b.astype(MXU), preferred_element_type=F32)


def _mixcore_fwd(proj, cos, sin_s, pool_w, pool_scale, sinks, name):
    T = proj.shape[0]
    nb = T // BLOCK
    scale = HEAD_DIM ** -0.5

    def body(p_ref, pp_ref, c_ref, s_ref, cp_ref, sp_ref, pw_ref, ps_ref, sk_ref, cat_ref, at_ref, lse_ref):
        i = pl.program_id(0)
        has_prev = i > 0
        cur = p_ref[...]
        prv = jnp.where(has_prev, pp_ref[...], 0.0)
        tpos = (i * BLOCK + _iota((BLOCK, 1), 0) + 1).astype(F32)
        for g, w in enumerate(POOL_WINDOWS):
            sl = slice(g * POOL_GROUP, (g + 1) * POOL_GROUP)
            pooled = _pool_sums(prv[:, sl], cur[:, sl], w) / jnp.minimum(tpos, float(w)) - cur[:, sl]
            cat_ref[:, sl] = (_nn(pooled, pw_ref[g]) * ps_ref[:, sl]).astype(cat_ref.dtype)
        q = _rope(cur[:, POOL_DIM:POOL_DIM + Q_DIM], c_ref[...], s_ref[...])
        kc = _rope(cur[:, POOL_DIM + Q_DIM:POOL_DIM + Q_DIM + KV_DIM], c_ref[...], s_ref[...])
        kp = _rope(prv[:, POOL_DIM + Q_DIM:POOL_DIM + Q_DIM + KV_DIM], cp_ref[...], sp_ref[...])
        vc = cur[:, POOL_DIM + Q_DIM + KV_DIM:]
        vp = prv[:, POOL_DIM + Q_DIM + KV_DIM:]
        ri = _iota((GQ * BLOCK, BLOCK), 0) % BLOCK
        cj = _iota((GQ * BLOCK, BLOCK), 1)
        mc = cj <= ri
        mp = jnp.logical_and(cj > ri, has_prev)
        outs, lses = [], []
        for g in range(N_KV_HEADS):
            hs = slice(g * HEAD_DIM, (g + 1) * HEAD_DIM)
            qg = _stack_heads(q, g) * scale
            sc = jnp.where(mc, _nt(qg, kc[:, hs]), NEG)
            sp = jnp.where(mp, _nt(qg, kp[:, hs]), NEG)
            sink = jnp.concatenate([jnp.full((BLOCK, 1), sk_ref[GQ * g + r], F32) for r in range(GQ)], axis=0)
            m = jnp.maximum(jnp.maximum(jnp.max(sc, axis=1, keepdims=True), jnp.max(sp, axis=1, keepdims=True)), sink)
            pc = jnp.exp(sc - m)
            pp = jnp.exp(sp - m)
            den = jnp.sum(pc, axis=1, keepdims=True) + jnp.sum(pp, axis=1, keepdims=True) + jnp.exp(sink - m)
            o = (_nn(pc, vc[:, hs]) + _nn(pp, vp[:, hs])) / den
            lse = m + jnp.log(den)
            for r in range(GQ):
                outs.append(o[r * BLOCK:(r + 1) * BLOCK])
                lses.append(lse[r * BLOCK:(r + 1) * BLOCK])
        attn = jnp.concatenate(outs, axis=1)
        at_ref[...] = attn
        cat_ref[:, POOL_DIM:] = attn.astype(cat_ref.dtype)
        lane = _iota((BLOCK, LANES), 1)
        lrow = jnp.zeros((BLOCK, LANES), F32)
        for h in range(N_HEADS):
            lrow = jnp.where(lane == h, lses[h], lrow)
        lse_ref[...] = lrow

    cur = lambda w: pl.BlockSpec((BLOCK, w), lambda i: (i, 0))
    prv = lambda w: pl.BlockSpec((BLOCK, w), lambda i: (jnp.maximum(i - 1, 0), 0))
    return pl.pallas_call(
        body, grid=(nb,),
        in_specs=[cur(MIX_IN_DIM), prv(MIX_IN_DIM), cur(LANES), cur(LANES), prv(LANES), prv(LANES),
                  pl.BlockSpec((4, POOL_GROUP, POOL_GROUP), lambda i: (0, 0, 0)), pl.BlockSpec((1, POOL_DIM), lambda i: (0, 0)),
                  pl.BlockSpec(memory_space=pltpu.SMEM)],
        out_specs=[cur(2 * POOL_DIM), cur(Q_DIM), cur(LANES)],
        out_shape=[_sds((T, 2 * POOL_DIM), MXU), _sds((T, Q_DIM)), _sds((T, LANES))],
        compiler_params=_cp("parallel"), name=name)(proj, proj, cos, sin_s, cos, sin_s, pool_w, pool_scale, sinks)


def _mixcore_bwd(proj, cos, sin_s, pool_w, pool_scale, sinks, attn, lse, dcat, name):
    T = proj.shape[0]
    nb = T // BLOCK
    scale = HEAD_DIM ** -0.5
    QO, KO, VO = POOL_DIM, POOL_DIM + Q_DIM, POOL_DIM + Q_DIM + KV_DIM

    def body(p_ref, pp_ref, pn_ref, c_ref, s_ref, cp_ref, sp_ref, cn_ref, sn_ref, pw_ref, ps_ref, sk_ref,
             at_ref, atn_ref, l_ref, ln_ref, d_ref, dn_ref, dp_ref, dpw_ref, dps_ref, dsk_ref):
        i = pl.program_id(0)
        has_prev = i > 0
        has_next = i < nb - 1
        cur = p_ref[...]
        prv = jnp.where(has_prev, pp_ref[...], 0.0)
        d_cur = d_ref[...]
        d_nxt = jnp.where(has_next, dn_ref[...], 0.0)

        tpos = (i * BLOCK + _iota((BLOCK, 1), 0) + 1).astype(F32)
        tpos2 = (i * BLOCK + _iota((2 * BLOCK, 1), 0) + 1).astype(F32)
        ps = ps_ref[...]
        dps_parts, dpw_parts = [], []
        for g, w in enumerate(POOL_WINDOWS):
            sl = slice(g * POOL_GROUP, (g + 1) * POOL_GROUP)
            pooled = _pool_sums(prv[:, sl], cur[:, sl], w) / jnp.minimum(tpos, float(w)) - cur[:, sl]
            mixed = _nn(pooled, pw_ref[g])
            dps_parts.append(jnp.sum(d_cur[:, sl] * mixed, axis=0, keepdims=True))
            dm2 = jnp.concatenate([d_cur[:, sl], d_nxt[:, sl]], axis=0) * ps[:, sl]
            dpw_parts.append(_tn(pooled, dm2[:BLOCK]))
            dpool2 = _nt(dm2, pw_ref[g])
            e = dpool2 / jnp.minimum(tpos2, float(w))
            sh = 1
            while sh < w:
                e = e + pltpu.roll(e, 2 * BLOCK - sh, 0)
                sh *= 2
            dp_ref[:, sl] = (e[:BLOCK] - dpool2[:BLOCK]).astype(dp_ref.dtype)
        dpsp = jnp.concatenate(dps_parts, axis=1)

        nxt = pn_ref[...]
        q = _rope(cur[:, QO:KO], c_ref[...], s_ref[...])
        qn = _rope(nxt[:, QO:KO], cn_ref[...], sn_ref[...])
        kc = _rope(cur[:, KO:VO], c_ref[...], s_ref[...])
        kp = _rope(prv[:, KO:VO], cp_ref[...], sp_ref[...])
        vc, vp = cur[:, VO:], prv[:, VO:]
        do, don = d_cur[:, POOL_DIM:], d_nxt[:, POOL_DIM:]
        dl = do * at_ref[...]
        dln = don * atn_ref[...]
        lse, lsen = l_ref[...], ln_ref[...]
        ri = _iota((GQ * BLOCK, BLOCK), 0) % BLOCK
        cj = _iota((GQ * BLOCK, BLOCK), 1)
        mc = cj <= ri
        mp = jnp.logical_and(cj > ri, has_prev)
        mn = jnp.logical_and(cj > ri, has_next)
        dq_parts, dk_parts, dv_parts, dsk_vals = [], [], [], []
        for g in range(N_KV_HEADS):
            hs = slice(g * HEAD_DIM, (g + 1) * HEAD_DIM)
            qg, qng = _stack_heads(q, g) * scale, _stack_heads(qn, g) * scale
            dog, dong = _stack_heads(do, g), _stack_heads(don, g)
            delta = jnp.sum(_stack_heads(dl, g), axis=1, keepdims=True)
            deltan = jnp.sum(_stack_heads(dln, g), axis=1, keepdims=True)
            lg, lng = _stack_cols(lse, g), _stack_cols(lsen, g)
            pc = jnp.where(mc, jnp.exp(_nt(qg, kc[:, hs]) - lg), 0.0)
            pp = jnp.where(mp, jnp.exp(_nt(qg, kp[:, hs]) - lg), 0.0)
            pn = jnp.where(mn, jnp.exp(_nt(qng, kc[:, hs]) - lng), 0.0)
            dsc = pc * (_nt(dog, vc[:, hs]) - delta)
            dsp = pp * (_nt(dog, vp[:, hs]) - delta)
            dsn = pn * (_nt(dong, vc[:, hs]) - deltan)
            dqg = (_nn(dsc, kc[:, hs]) + _nn(dsp, kp[:, hs])) * scale
            dq_parts += [dqg[r * BLOCK:(r + 1) * BLOCK] for r in range(GQ)]
            dk_parts.append(_tn(dsc, qg) + _tn(dsn, qng))
            dv_parts.append(_tn(pc, dog) + _tn(pn, dong))
            sink = jnp.concatenate([jnp.full((BLOCK, 1), sk_ref[GQ * g + r], F32) for r in range(GQ)], axis=0)
            dsk = -jnp.exp(sink - lg) * delta
            dsk_vals += [jnp.sum(dsk[r * BLOCK:(r + 1) * BLOCK], axis=0, keepdims=True) for r in range(GQ)]
        dq = _rope(jnp.concatenate(dq_parts, axis=1), c_ref[...], s_ref[...], inverse=True)
        dk = _rope(jnp.concatenate(dk_parts, axis=1), c_ref[...], s_ref[...], inverse=True)
        dp_ref[:, QO:KO] = dq.astype(dp_ref.dtype)
        dp_ref[:, KO:VO] = dk.astype(dp_ref.dtype)
        dp_ref[:, VO:] = jnp.concatenate(dv_parts, axis=1).astype(dp_ref.dtype)
        lane = _iota((1, LANES), 1)
        dskp = jnp.zeros((1, LANES), F32)
        for h in range(N_HEADS):
            dskp = jnp.where(lane == h, dsk_vals[h], dskp)

        @pl.when(i == 0)
        def _():
            dps_ref[...] = dpsp
            dsk_ref[...] = dskp
            for g in range(4):
                dpw_ref[g] = dpw_parts[g]

        @pl.when(i > 0)
        def _():
            dps_ref[...] += dpsp
            dsk_ref[...] += dskp
            for g in range(4):
                dpw_ref[g] += dpw_parts[g]

    cur = lambda w: pl.BlockSpec((BLOCK, w), lambda i: (i, 0))
    prv = lambda w: pl.BlockSpec((BLOCK, w), lambda i: (jnp.maximum(i - 1, 0), 0))
    nxt = lambda w: pl.BlockSpec((BLOCK, w), lambda i: (jnp.minimum(i + 1, nb - 1), 0))
    return pl.pallas_call(
        body, grid=(nb,),
        in_specs=[cur(MIX_IN_DIM), prv(MIX_IN_DIM), nxt(MIX_IN_DIM),
                  cur(LANES), cur(LANES), prv(LANES), prv(LANES), nxt(LANES), nxt(LANES),
                  pl.BlockSpec((4, POOL_GROUP, POOL_GROUP), lambda i: (0, 0, 0)), pl.BlockSpec((1, POOL_DIM), lambda i: (0, 0)),
                  pl.BlockSpec(memory_space=pltpu.SMEM),
                  cur(Q_DIM), nxt(Q_DIM), cur(LANES), nxt(LANES), cur(2 * POOL_DIM), nxt(2 * POOL_DIM)],
        out_specs=[cur(MIX_IN_DIM), pl.BlockSpec((4, POOL_GROUP, POOL_GROUP), lambda i: (0, 0, 0)),
                   pl.BlockSpec((1, POOL_DIM), lambda i: (0, 0)), pl.BlockSpec((1, LANES), lambda i: (0, 0))],
        out_shape=[_sds((T, MIX_IN_DIM), MXU), _sds((4, POOL_GROUP, POOL_GROUP)), _sds((1, POOL_DIM)), _sds((1, LANES))],
        compiler_params=_cp("arbitrary"), name=name)(
            proj, proj, proj, cos, sin_s, cos, sin_s, cos, sin_s, pool_w, pool_scale, sinks, attn, attn, lse, lse, dcat, dcat)


GROUP_W = SSM_D_INNER // SSM_GROUPS


def _ssm_in_conv(h, wT, row_off, cw, cb, name):
    T, D = h.shape
    tm = min(T, 256)
    tc = 1024
    K = SSM_CONV

    def body(a_ref, b_ref, w_ref, c_ref, x_ref, pre_ref, act_ref, halo):
        @pl.when(pl.program_id(1) == 0)
        def _():
            halo[...] = jnp.zeros(halo.shape, F32)

        cur = _nt(a_ref[...], b_ref[...])
        x_ref[...] = cur
        pre = _conv_rows(cur, halo[...], w_ref[...], c_ref[...], K)
        halo[...] = cur[tm - SUBLANES:]
        pre_ref[...] = pre
        act_ref[...] = _silu(pre)

    blk = pl.BlockSpec((tm, tc), lambda j, i: (i, j))
    return pl.pallas_call(
        body, grid=(SSM_CONV_DIM // tc, T // tm),
        in_specs=[pl.BlockSpec((tm, D), lambda j, i: (i, 0)), pl.BlockSpec((tc, D), lambda j, i: (j + row_off // tc, 0)),
                  pl.BlockSpec((K, tc), lambda j, i: (0, j)), pl.BlockSpec((1, tc), lambda j, i: (0, j))],
        out_specs=[blk, blk, blk], out_shape=[_sds((T, SSM_CONV_DIM))] * 3,
        scratch_shapes=[pltpu.VMEM((SUBLANES, tc), F32)],
        compiler_params=_cp("arbitrary", "arbitrary"), name=name)(h, wT, cw, cb)


def _dot_hi(a, b):
    return jnp.dot(a, b, precision=HI, preferred_element_type=F32)


def _ssd_common(dtraw, bias, alog):
    L = SSM_CHUNK
    xb = dtraw + bias
    dt = jnp.maximum(xb, 0.0) + jnp.log1p(jnp.exp(-jnp.abs(xb)))
    A = -jnp.exp(alog)
    tril = (_iota((L, L), 1) <= _iota((L, L), 0)).astype(F32)
    acs = _dot_hi(tril, dt * A)
    return xb, dt, A, tril, acs


def _head_selectors():
    es = (_iota((LANES, SSM_D_INNER), 0) == _iota((LANES, SSM_D_INNER), 1) // HEAD_DIM).astype(BF16)
    est = (_iota((SSM_D_INNER, LANES), 1) == _iota((SSM_D_INNER, LANES), 0) // HEAD_DIM).astype(BF16)
    return es, est


def _dot_sel(v, sel):
    hi = v.astype(BF16)
    r1 = v - hi.astype(F32)
    mid = r1.astype(BF16)
    lo = (r1 - mid.astype(F32)).astype(BF16)
    d = lambda a: jnp.dot(a, sel, preferred_element_type=F32)
    return (d(hi) + d(mid)) + d(lo)


def _expand_heads(v, es):
    return _dot_sel(v, es)


def _reduce_heads(q, est):
    return _dot_sel(q, est)


def _per_state_row(v, g):
    return jnp.concatenate([jnp.broadcast_to(v[:, GQ * g + r:GQ * g + r + 1], (HEAD_DIM, 1)) for r in range(GQ)], axis=0)


def _ssd_fwd(xact, dtraw, dt_bias, a_log, z, d_skip, nw, name):
    T = xact.shape[0]
    nc = T // SSM_CHUNK
    L = SSM_CHUNK
    BO, CO = SSM_D_INNER, SSM_D_INNER + SSM_GROUPS * SSM_STATE

    def body(x_ref, dt_ref, bias_ref, al_ref, es_ref, z_ref, dsk_ref, nw_ref, y_ref, st_ref, yn_ref, state):
        @pl.when(pl.program_id(0) == 0)
        def _():
            state[...] = jnp.zeros(state.shape, F32)

        _, dt, A, tril, acs = _ssd_common(dt_ref[...], bias_ref[...], al_ref[...])
        acsT = acs.T
        last = acs[L - 1:L, :]
        cd = jnp.exp(last)
        es = es_ref[...]
        dtX = _expand_heads(dt, es)
        EX = _expand_heads(jnp.exp(acs), es)
        decX = _expand_heads(jnp.exp(last - acs), es)
        for g in range(SSM_GROUPS):
            gs = slice(g * GROUP_W, (g + 1) * GROUP_W)
            B = x_ref[:, BO + g * SSM_STATE:BO + (g + 1) * SSM_STATE]
            C = x_ref[:, CO + g * SSM_STATE:CO + (g + 1) * SSM_STATE]
            X = x_ref[:, gs] * dtX[:, gs]
            CB = _nt(C, B)
            yd = []
            for r in range(GQ):
                h = GQ * g + r
                Lm = jnp.exp(jnp.where(tril > 0, acs[:, h:h + 1] - acsT[h:h + 1, :], NEG))
                yd.append(_nn(CB * Lm, X[:, r * HEAD_DIM:(r + 1) * HEAD_DIM]))
            S = state[g]
            st_ref[g] = S
            y_ref[:, gs] = jnp.concatenate(yd, axis=1) + _nt(C, S) * EX[:, gs]
            state[g] = S * _per_state_row(cd, g) + _tn(X * decX[:, gs], B)
        y2 = (y_ref[...] + dsk_ref[...] * x_ref[:, :SSM_D_INNER]) * _silu(z_ref[...])
        r = lax.rsqrt(jnp.mean(y2 * y2, axis=-1, keepdims=True) + SSM_NORM_EPS)
        yn_ref[...] = (y2 * r * nw_ref[...]).astype(yn_ref.dtype)

    es, _ = _head_selectors()
    row = pl.BlockSpec((L, SSM_D_INNER), lambda c: (c, 0))
    vec = pl.BlockSpec((1, SSM_D_INNER), lambda c: (0, 0))
    return pl.pallas_call(
        body, grid=(nc,),
        in_specs=[pl.BlockSpec((L, SSM_CONV_DIM), lambda c: (c, 0)), pl.BlockSpec((L, LANES), lambda c: (c, 0)),
                  pl.BlockSpec((1, LANES), lambda c: (0, 0)), pl.BlockSpec((1, LANES), lambda c: (0, 0)),
                  pl.BlockSpec((LANES, SSM_D_INNER), lambda c: (0, 0)), row, vec, vec],
        out_specs=[row, pl.BlockSpec((None, SSM_GROUPS, GROUP_W, SSM_STATE), lambda c: (c, 0, 0, 0)), row],
        out_shape=[_sds((T, SSM_D_INNER)), _sds((nc, SSM_GROUPS, GROUP_W, SSM_STATE)), _sds((T, SSM_D_INNER), MXU)],
        scratch_shapes=[pltpu.VMEM((SSM_GROUPS, GROUP_W, SSM_STATE), F32)],
        compiler_params=_cp("arbitrary"), name=name)(xact, dtraw, dt_bias, a_log, es, z, d_skip, nw)


def _ssd_bwd(xact, xbc, xpre, cw, dtraw, dt_bias, a_log, d_skip, states, dyn, y, z, nw, name):
    T = xact.shape[0]
    nc = T // SSM_CHUNK
    L = SSM_CHUNK
    K = SSM_CONV
    BO, CO = SSM_D_INNER, SSM_D_INNER + SSM_GROUPS * SSM_STATE

    def body(x_ref, xin_ref, pre_ref, cw_ref, dt_ref, bias_ref, al_ref, dsk_ref, es_ref, est_ref, st_ref, dn_ref, y_ref,
             z_ref, nw_ref, dxbc_ref, dcw_ref, dcb_ref, ddt_ref, dbias_ref, dal_ref, dd_ref, dz_ref, dnw_ref,
             dstate, qa, qx, dxp_ref, ahead, dyv):
        cc = pl.program_id(0)

        @pl.when(cc == 0)
        def _():
            dstate[...] = jnp.zeros(dstate.shape, F32)
            ahead[...] = jnp.zeros(ahead.shape, F32)

        zv = z_ref[...]
        sz = _silu(zv)
        yg = y_ref[...] + dsk_ref[...] * x_ref[:, :SSM_D_INNER]
        y2 = yg * sz
        rn = lax.rsqrt(jnp.mean(y2 * y2, axis=-1, keepdims=True) + SSM_NORM_EPS)
        y2h = y2 * rn
        dn = dn_ref[...]
        gy = dn * nw_ref[...]
        dy2 = rn * (gy - y2h * jnp.mean(gy * y2h, axis=-1, keepdims=True))
        dyv[...] = dy2 * sz
        dz_ref[...] = (dy2 * yg * _dsilu(zv)).astype(dz_ref.dtype)
        dnwp = jnp.sum(dn * y2h, axis=0, keepdims=True)

        xb, dt, A, tril, acs = _ssd_common(dt_ref[...], bias_ref[...], al_ref[...])
        acsT = acs.T
        last = acs[L - 1:L, :]
        cd = jnp.exp(last)
        es, est = es_ref[...], est_ref[...]
        dtX = _expand_heads(dt, es)
        EX = _expand_heads(jnp.exp(acs), es)
        decX = _expand_heads(jnp.exp(last - acs), es)
        lane1 = _iota((1, LANES), 1)
        lane = _iota((L, LANES), 1)
        sub = _iota((L, LANES), 0)
        ztot = jnp.zeros((1, LANES), F32)
        wrow = jnp.zeros((L, LANES), F32)
        wcolT = jnp.zeros((LANES, L), F32)
        rows_dec, rows_dd = [], []
        for g in range(SSM_GROUPS):
            gs = slice(g * GROUP_W, (g + 1) * GROUP_W)
            x = x_ref[:, gs]
            B = x_ref[:, BO + g * SSM_STATE:BO + (g + 1) * SSM_STATE]
            C = x_ref[:, CO + g * SSM_STATE:CO + (g + 1) * SSM_STATE]
            dY = dyv[:, gs]
            dtx, e_x, dec_x = dtX[:, gs], EX[:, gs], decX[:, gs]
            X = x * dtx
            CB = _nt(C, B)
            S = st_ref[g]
            dS_out = dstate[g]
            dcb_sum = jnp.zeros((L, L), F32)
            dxd = []
            for r in range(GQ):
                h = GQ * g + r
                hs = slice(r * HEAD_DIM, (r + 1) * HEAD_DIM)
                Lm = jnp.exp(jnp.where(tril > 0, acs[:, h:h + 1] - acsT[h:h + 1, :], NEG))
                M = CB * Lm
                dM = _nt(dY[:, hs], X[:, hs])
                dxd.append(_tn(M, dY[:, hs]))
                dcb_sum = dcb_sum + dM * Lm
                Wm = dM * M
                wrow = jnp.where(lane == h, jnp.sum(Wm, axis=1, keepdims=True), wrow)
                wcolT = jnp.where(sub == h, jnp.sum(Wm, axis=0, keepdims=True), wcolT)
            dXd = jnp.concatenate(dxd, axis=1)
            G = _nt(C, S)
            dG = dY * e_x
            dDX = _nt(B, dS_out)
            dX = dXd + dec_x * dDX
            t_dec = dDX * X * dec_x
            qa[:, gs] = dG * G - t_dec
            qx[:, gs] = dX * x
            rows_dec.append(jnp.sum(t_dec, axis=0, keepdims=True))
            rows_dd.append(jnp.sum(dY * x, axis=0, keepdims=True))
            zc = jnp.sum(dS_out * S, axis=1, keepdims=True)
            for r in range(GQ):
                ztot = jnp.where(lane1 == GQ * g + r, jnp.sum(zc[r * HEAD_DIM:(r + 1) * HEAD_DIM], axis=0, keepdims=True), ztot)
            dxp_ref[:, gs] = dX * dtx + dY * dsk_ref[:, gs]
            dxp_ref[:, BO + g * SSM_STATE:BO + (g + 1) * SSM_STATE] = _tn(dcb_sum, C) + _nn(X * dec_x, dS_out)
            dxp_ref[:, CO + g * SSM_STATE:CO + (g + 1) * SSM_STATE] = _nn(dcb_sum, B) + _nn(dG, S)
            dstate[g] = dS_out * _per_state_row(cd, g) + _tn(dG, C)
        rows = jnp.concatenate([jnp.concatenate(rows_dec, axis=1), jnp.concatenate(rows_dd, axis=1)]
                               + [jnp.zeros((SUBLANES - 2, SSM_D_INNER), F32)], axis=0)
        rsum = _reduce_heads(rows, est)
        dlast = rsum[0:1, :] + cd * ztot
        dacs = (wrow - wcolT.T) + _reduce_heads(qa[...], est) + jnp.where(sub == L - 1, dlast, 0.0)
        triu = (_iota((L, L), 0) <= _iota((L, L), 1)).astype(F32)
        da = _dot_hi(triu, dacs)
        ddtraw = (da * A + _reduce_heads(qx[...], est)) * (1.0 / (1.0 + jnp.exp(-xb)))
        ddt_ref[...] = ddtraw
        dal = jnp.sum(da * dt, axis=0, keepdims=True) * A
        ddp = rsum[1:2, :]
        dbp = jnp.sum(ddtraw, axis=0, keepdims=True)
        w = cw_ref[...]
        d_cur = dxp_ref[...] * _dsilu(pre_ref[...])
        d_nxt = ahead[...]
        ahead[...] = d_cur[:SUBLANES]
        ups = [d_cur] + [_shift_up(d_cur, d_nxt, s) for s in range(1, K)]
        dxc = ups[0] * w[K - 1:K, :]
        for s in range(1, K):
            dxc = dxc + ups[s] * w[K - 1 - s:K - s, :]
        dxbc_ref[...] = dxc.astype(dxbc_ref.dtype)
        xin = xin_ref[...]
        dcwp = jnp.concatenate([jnp.sum(ups[K - 1 - k] * xin, axis=0, keepdims=True) for k in range(K)], axis=0)
        dcbp = jnp.sum(d_cur, axis=0, keepdims=True)

        @pl.when(cc == 0)
        def _():
            dbias_ref[...] = dbp
            dal_ref[...] = dal
            dd_ref[...] = ddp
            dcw_ref[...] = dcwp
            dcb_ref[...] = dcbp
            dnw_ref[...] = dnwp

        @pl.when(cc > 0)
        def _():
            dbias_ref[...] += dbp
            dal_ref[...] += dal
            dd_ref[...] += ddp
            dcw_ref[...] += dcwp
            dcb_ref[...] += dcbp
            dnw_ref[...] += dnwp

    rc = lambda c: nc - 1 - c
    vec = pl.BlockSpec((1, LANES), lambda c: (0, 0))
    wide = pl.BlockSpec((L, SSM_CONV_DIM), lambda c: (rc(c), 0))
    inner = pl.BlockSpec((L, SSM_D_INNER), lambda c: (rc(c), 0))
    es, est = _head_selectors()
    return pl.pallas_call(
        body, grid=(nc,),
        in_specs=[wide, wide, wide, pl.BlockSpec((K, SSM_CONV_DIM), lambda c: (0, 0)),
                  pl.BlockSpec((L, LANES), lambda c: (rc(c), 0)), vec, vec,
                  pl.BlockSpec((1, SSM_D_INNER), lambda c: (0, 0)),
                  pl.BlockSpec((LANES, SSM_D_INNER), lambda c: (0, 0)), pl.BlockSpec((SSM_D_INNER, LANES), lambda c: (0, 0)),
                  pl.BlockSpec((None, SSM_GROUPS, GROUP_W, SSM_STATE), lambda c: (rc(c), 0, 0, 0)),
                  inner, inner, inner, pl.BlockSpec((1, SSM_D_INNER), lambda c: (0, 0))],
        out_specs=[wide, pl.BlockSpec((K, SSM_CONV_DIM), lambda c: (0, 0)), pl.BlockSpec((1, SSM_CONV_DIM), lambda c: (0, 0)),
                   pl.BlockSpec((L, LANES), lambda c: (rc(c), 0)), vec, vec, vec, inner,
                   pl.BlockSpec((1, SSM_D_INNER), lambda c: (0, 0))],
        out_shape=[_sds((T, SSM_CONV_DIM), MXU), _sds((K, SSM_CONV_DIM)), _sds((1, SSM_CONV_DIM)),
                   _sds((T, LANES)), _sds((1, LANES)), _sds((1, LANES)), _sds((1, LANES)),
                   _sds((T, SSM_D_INNER), MXU), _sds((1, SSM_D_INNER))],
        scratch_shapes=[pltpu.VMEM((SSM_GROUPS, GROUP_W, SSM_STATE), F32), pltpu.VMEM((L, SSM_D_INNER), F32),
                        pltpu.VMEM((L, SSM_D_INNER), F32), pltpu.VMEM((L, SSM_CONV_DIM), F32),
                        pltpu.VMEM((SUBLANES, SSM_CONV_DIM), F32), pltpu.VMEM((L, SSM_D_INNER), F32)],
        compiler_params=_cp("arbitrary"), name=name)(xact, xbc, xpre, cw, dtraw, dt_bias, a_log, d_skip, es, est, states, dyn, y,
                                                     z, nw)


def _local_step(x0, cos, sin_s, target, P, fetch, token, send):
    mmf = functools.partial(_mm, tm=1024)
    big, small = {}, {}
    P = dict(P, wup={}, wdn={}, fcw={})
    h0 = _rmsnorm_fwd(x0, P["nm"][0], "norm_mix0", token=token)
    proj0 = mmf(h0, P["wmiT"], tb=True, tn=1280, tk=1024, name="mix_in")
    cat, attn, lse = _mixcore_fwd(proj0, cos, sin_s, P["pool_w"], P["pool_scale"], P["sinks"], "mixcore_fwd")
    x1, hf0 = mmf(cat, P["wmo"], tn=1024, tk=1024, res=x0, norm_w=P["nf"][0], name="mix_out")

    def ffn_fwd(xin, hf, i, **epilogue):
        got = fetch(f"ffn{i}", hf)
        P["wup"][i], P["wdn"][i], P["fcw"][i] = got["wup"], got["wdn"], got["fcw"]
        hid, hc, act = _ffn_up_conv_gate(hf, P["wup"][i], P["fcw"][i], P["fcb"][i], f"ffn_up{i}")
        xout = mmf(act, P["wdn"][i], tn=1024, tk=D_FF, res=xin, name=f"ffn_down{i}", **epilogue)
        return (hid, hc), act, xout

    hid0, act0, (x2, h1) = ffn_fwd(x1, hf0, 0, norm_w=P["nm"][1])
    P.update(fetch("ssm", h1))
    z = mmf(h1, P["wsiT"], tb=True, tn=1024, tk=1024, b_rows=(0, SSM_D_INNER), name="ssm_in_z")
    xbc, xpre, xact = _ssm_in_conv(h1, P["wsiT"], SSM_D_INNER, P["scw"], P["scb"], "ssm_in_xbc")
    dtraw = mmf(h1, P["wdtT"], tb=True, tn=128, tk=1024, name="ssm_in_dt")
    y, states, yn = _ssd_fwd(xact, dtraw, P["dt_bias"], P["a_log"], z, P["d_exp"], P["snorm"], "ssd_fwd")
    x3, hf1 = mmf(yn, P["wso"], tn=1024, tk=SSM_D_INNER, res=x2, norm_w=P["nf"][1], name="ssm_out")
    hid1, act1, (dx4, d_nfin, loss_row) = ffn_fwd(x3, hf1, 1, loss_head=(P["nfin"], target))
    small["norm_final"] = d_nfin

    def ffn_bwd(xin, dxo, hf, hid, act, i):
        big[f"ffn_w_down{i}"] = dwf(act, dxo, tm=1408, tn=1024, name=f"ffn_down_dw{i}").reshape(N_CHIPS, D_FF // N_CHIPS, D_MODEL)
        dhid, dcw, dcb = _ffn_down_dx_mid_bwd(dxo, P["wdn"][i], hid[0], hid[1], P["fcw"][i], f"ffn_down_dx{i}")
        big[f"ffn_w_up{i}"] = dwf(hf, dhid, tm=1024, tn=1408, out_shard_perm=(0, 2, 1, 3), name=f"ffn_up_dw{i}")
        tok = send(f"ffn{i}", [big[f"ffn_w_up{i}"], big[f"ffn_w_down{i}"]])
        dxi, dnf = _mm(dhid, P["wup"][i], tb=True, tm=512, tn=1024, tk=5632, norm_bwd=(xin, P["nf"][i], dxo, tok), name=f"ffn_up_dx{i}")
        return dxi, dnf, dcw, dcb

    dwf = functools.partial(_mm, ta=True, tk=2048, out_dtype=BF16)
    dx3, dnf1, dfcw1, dfcb1 = ffn_bwd(x3, dx4, hf1, hid1, act1, 1)
    dyn = mmf(dx3, P["wso"], tb=True, tn=1024, tk=1024, name="ssm_out_dx")
    big["ssm_w_out"] = dwf(yn, dx3, tm=1024, tn=1024, name="ssm_out_dw").reshape(N_CHIPS, SSM_D_INNER // N_CHIPS, D_MODEL)
    dxbc, d_scw, d_scb, ddtraw, d_dtb, d_alog, d_dskip, dz, d_snorm = _ssd_bwd(
        xact, xbc, xpre, P["scw"], dtraw, P["dt_bias"], P["a_log"], P["d_exp"], states, dyn, y, z, P["snorm"], "ssd_bwd")
    dwsi = dwf(dz, h1, tm=1024, tn=1024, out_into=(None, SSM_IN_DIM, 0), name="ssm_in_dw_z")
    dwsi = dwf(dxbc, h1, tm=1024, tn=1024, out_into=(dwsi, SSM_IN_DIM, SSM_D_INNER // 1024), name="ssm_in_dw_xbc")
    dwdt = dwf(ddtraw, h1, tm=128, tn=1024, name="ssm_in_dw_dt")
    dwsi = _put_rows(dwsi, dwdt, SSM_HEADS, SSM_D_INNER + SSM_CONV_DIM, "ssm_in_dw_put_dt")
    big["ssm_w_in"] = dwsi.reshape(N_CHIPS, SSM_IN_DIM // N_CHIPS, D_MODEL)
    tok = send("ssm", [big["ssm_w_in"], big["ssm_w_out"]])
    dh1 = mmf(dz, P["wsiT"], tn=1024, tk=2048, b_rows=(0, SSM_D_INNER), name="ssm_in_dx_z")
    dh1 = mmf(dxbc, P["wsiT"], tn=1024, tk=2048, b_rows=(SSM_D_INNER // 2048, SSM_CONV_DIM), res=dh1, name="ssm_in_dx_xbc")
    dx2, dnm1 = mmf(ddtraw, P["wdtT"], tn=1024, tk=128, res=dh1, norm_bwd=(x2, P["nm"][1], dx3, tok), name="ssm_in_dx_dt")
    dx1, dnf0, dfcw0, dfcb0 = ffn_bwd(x1, dx2, hf0, hid0, act0, 0)
    dcat = mmf(dx1, P["wmo"], tb=True, tn=1024, tk=1024, name="mix_out_dx")
    big["mix_w_out"] = dwf(cat, dx1, tm=1024, tn=1024, name="mix_out_dw").reshape(N_CHIPS, D_MODEL // N_CHIPS, D_MODEL)
    dproj0, d_pw, d_ps, d_sk = _mixcore_bwd(proj0, cos, sin_s, P["pool_w"], P["pool_scale"], P["sinks"], attn, lse, dcat, "mixcore_bwd")
    big["mix_w_in"] = dwf(dproj0, h0, tm=1280, tn=1024, name="mix_in_dw").reshape(N_CHIPS, MIX_IN_DIM // N_CHIPS, D_MODEL)
    tok = send("mix", [big["mix_w_in"], big["mix_w_out"]])
    dx0, dnm0 = mmf(dproj0, P["wmiT"], tn=1024, tk=1280, norm_bwd=(x0, P["nm"][0], dx1, tok), name="mix_in_dx")

    def unperm_cols(a):
        r = a.shape[0]
        t = a.reshape(r, N_CHIPS, FFN_TC)
        return jnp.stack([t[:, p] for p in _PERM], axis=0)

    small["norm_mix"] = jnp.concatenate([dnm0, dnm1], axis=0)
    small["norm_ffn"] = jnp.concatenate([dnf0, dnf1], axis=0)
    small["pool_w"] = d_pw.reshape(4 * POOL_GROUP, POOL_GROUP)
    small["pool_scale"] = d_ps
    small["attn_sinks"] = d_sk
    small["ssm_dt_bias"] = d_dtb
    small["ssm_A_log"] = d_alog
    small["ssm_D"] = d_dskip
    fcb = jnp.stack([unperm_cols(dfcb0), unperm_cols(dfcb1)], axis=0)
    small["ffn_conv_b"] = fcb.reshape(2, 2 * D_FF)
    small["ssm_conv_w"] = d_scw.reshape(SSM_CONV, N_CHIPS, SSM_CONV_DIM // N_CHIPS).transpose(1, 0, 2)
    small["ssm_conv_b"] = d_scb.reshape(N_CHIPS, 1, SSM_CONV_DIM // N_CHIPS)
    small["ssm_norm"] = d_snorm.reshape(N_CHIPS, 1, SSM_D_INNER // N_CHIPS)
    small["ffn_conv_w"] = jnp.concatenate([unperm_cols(dfcw0), unperm_cols(dfcw1)], axis=1)
    return loss_row, dx0, big, small


ANY = pl.BlockSpec(memory_space=pl.ANY)


def _place():
    return lax.axis_index("x"), lax.axis_index("y"), lax.axis_index("c")


def _gather_shards(shards, name):
    n = len(shards)
    split = [s.size >= (1 << 16) for s in shards]

    def half(ref, a, h):
        shp = shards[a].shape
        if len(shp) == 3:
            return ref.at[h]
        r2 = shp[0] // 2
        return ref.at[pl.ds(pl.multiple_of(h * r2, 2 * SUBLANES), r2), :]

    def body(*refs):
        ins, outs = refs[:n], refs[n:2 * n]
        send, recv, fsend, frecv = refs[2 * n:]
        x, y, c = _place()
        k = 2 * x + y
        chips = [(1 - x, y), (x, 1 - y), (1 - x, 1 - y)]

        def ici(a, j, src_slot_ref, dst_slot):
            px, py = chips[j]
            src = half(src_slot_ref, a, c) if split[a] else src_slot_ref
            dst = half(outs[a].at[dst_slot], a, c) if split[a] else outs[a].at[dst_slot]
            return pltpu.make_async_remote_copy(src, dst, send.at[a, j], recv.at[a, j], device_id=(px, py, c), device_id_type=MESH)

        def d2d(a, j, h):
            px, py = chips[j]
            part = half(outs[a].at[2 * px + py], a, h)
            return pltpu.make_async_remote_copy(part, part, fsend.at[a, j], frecv.at[a, j], device_id=(x, y, 1 - c), device_id_type=MESH)

        sends = [ici(a, j, ins[a], k) for a in range(n) for j in range(3)]
        for cp in sends:
            cp.start()
        passed = []
        for a in range(n):
            for j, (px, py) in enumerate(chips):
                ici(a, j, ins[a], 2 * px + py).wait_recv()
                if split[a]:
                    passed.append(d2d(a, j, c))
                    passed[-1].start()
        for a in range(n):
            if split[a]:
                for j in range(3):
                    d2d(a, j, 1 - c).wait_recv()
        for cp in sends + passed:
            cp.wait_send()

    return pl.pallas_call(
        body, in_specs=[ANY] * n, out_specs=[ANY] * n,
        out_shape=[_sds((N_CHIPS,) + s.shape, s.dtype) for s in shards],
        scratch_shapes=[pltpu.SemaphoreType.DMA((n, 3))] * 4,
        compiler_params=pltpu.CompilerParams(has_side_effects=True), name=name)(*shards)


HBM = pl.BlockSpec(memory_space=pltpu.HBM)
SEM = pl.BlockSpec(memory_space=pltpu.SEMAPHORE)
DATAFLOW = pltpu.SideEffectType.DATAFLOW_SIDE_EFFECTING


def _row_half(ref, h):
    r2 = ref.shape[0] // 2
    return ref.at[pl.ds(pl.multiple_of(h * r2, 2 * SUBLANES), r2), :]


def _spread_start(groups, slot_src, after, name, halved=()):
    flat = [a for grp in groups for a in grp]
    n = len(flat)
    ng = len(groups)
    offs = [sum(len(g) for g in groups[:i]) for i in range(ng)]
    lshape = [(a.shape if slot_src else (N_CHIPS,) + a.shape) for a in flat]

    nsem = 6 * n

    def body(*refs):
        src, land = refs[:n], refs[n:2 * n]
        sems = refs[2 * n + 1:2 * n + 1 + nsem]
        token = refs[-1]
        x, y, c = _place()
        k = 2 * x + y
        chips = [(1 - x, y), (x, 1 - y), (1 - x, 1 - y)]
        for a in range(n):
            half = any(offs[gi] <= a < offs[gi] + len(groups[gi]) for gi in halved)
            for j, (px, py) in enumerate(chips):
                s = src[a].at[2 * px + py] if slot_src else src[a]
                d = land[a].at[k]
                if half:
                    s, d = _row_half(s, c), _row_half(d, c)
                pltpu.make_async_remote_copy(s, d, sems[6 * a + 2 * j], sems[6 * a + 2 * j + 1],
                                             device_id=(px, py, c), device_id_type=MESH).start()
        token[...] = jnp.zeros(token.shape, token.dtype)

    out_shape = [pltpu.SemaphoreType.DMA(())] * nsem
    out_shape += [pltpu.HBM(a.shape, a.dtype) for a in flat] + [pltpu.HBM(s, a.dtype) for s, a in zip(lshape, flat)]
    out_shape.append(_sds((SUBLANES, LANES)))
    args = [pltpu.with_memory_space_constraint(a, pltpu.HBM) for a in flat]
    args += [pltpu.with_memory_space_constraint(lax.empty(s, a.dtype), pltpu.HBM) for s, a in zip(lshape, flat)]
    res = pl.pallas_call(
        body, name=name, out_shape=tuple(out_shape), in_specs=[HBM] * (2 * n) + [pl.BlockSpec(memory_space=pl.ANY)],
        out_specs=tuple([SEM] * nsem + [HBM] * (2 * n) + [pl.BlockSpec(memory_space=pltpu.VMEM)]),
        input_output_aliases={i: nsem + i for i in range(2 * n)},
        compiler_params=pltpu.CompilerParams(has_side_effects=DATAFLOW))(*args, after)
    sems, thru, token = res[:nsem], res[nsem:nsem + 2 * n], res[-1]
    out = []
    for gi, grp in enumerate(groups):
        sl = slice(offs[gi], offs[gi] + len(grp))
        out.append((list(sems[6 * offs[gi]:6 * (offs[gi] + len(grp))]), list(thru[:n][sl]), list(thru[n:][sl])))
    return out, token


def _spread_wait(started, slot_src, after, name, halved=False):
    sems, srcs, lands = started
    n = len(srcs)

    def body(*refs):
        src, land = refs[:n], refs[n:2 * n]
        sem = refs[2 * n:2 * n + 6 * n]
        x, y, c = _place()
        chips = [(1 - x, y), (x, 1 - y), (1 - x, 1 - y)]
        for a in range(n):
            for j, (px, py) in enumerate(chips):
                s = src[a].at[2 * px + py] if slot_src else src[a]
                d = land[a].at[2 * px + py]
                if halved:
                    s, d = _row_half(s, c), _row_half(d, c)
                cp = pltpu.make_async_remote_copy(s, d, sem[6 * a + 2 * j], sem[6 * a + 2 * j + 1],
                                                  device_id=(px, py, c), device_id_type=MESH)
                cp.wait_send()
                cp.wait_recv()

    res = pl.pallas_call(
        body, name=name, out_shape=tuple([pltpu.HBM(a.shape, a.dtype) for a in srcs] + [pltpu.HBM(a.shape, a.dtype) for a in lands]),
        in_specs=[HBM] * (2 * n) + [SEM] * (6 * n) + [pl.BlockSpec(memory_space=pl.ANY)], out_specs=tuple([HBM] * (2 * n)),
        input_output_aliases={i: i for i in range(2 * n)},
        compiler_params=pltpu.CompilerParams(has_side_effects=DATAFLOW))(*srcs, *lands, *sems, after)
    return list(res[:n]), list(res[n:])


def _sibling_fill(lands, name):
    n = len(lands)

    def body(*refs):
        bufs = refs[n:2 * n]
        send, recv = refs[2 * n:]
        x, y, c = _place()
        chips = [(1 - x, y), (x, 1 - y), (1 - x, 1 - y)]

        def copy(a, j, h):
            px, py = chips[j]
            part = _row_half(bufs[a].at[2 * px + py], h)
            return pltpu.make_async_remote_copy(part, part, send.at[a, j], recv.at[a, j], device_id=(x, y, 1 - c), device_id_type=MESH)

        sends = [copy(a, j, c) for a in range(n) for j in range(3)]
        for cp in sends:
            cp.start()
        for a in range(n):
            for j in range(3):
                copy(a, j, 1 - c).wait_recv()
        for cp in sends:
            cp.wait_send()

    return pl.pallas_call(
        body, in_specs=[ANY] * n, out_specs=[ANY] * n, out_shape=[_sds(t.shape, t.dtype) for t in lands],
        input_output_aliases={i: i for i in range(n)},
        scratch_shapes=[pltpu.SemaphoreType.DMA((n, 3)), pltpu.SemaphoreType.DMA((n, 3))],
        compiler_params=pltpu.CompilerParams(has_side_effects=True), name=name)(*lands)


def _sibling_exchange(fs, name):
    n = len(fs)

    def body(*refs):
        ins, outs = refs[:n], refs[n:2 * n]
        send, recv = refs[2 * n:]
        x, y, c = _place()
        cps = [pltpu.make_async_remote_copy(ins[a], outs[a], send.at[a], recv.at[a],
                                            device_id=(x, y, 1 - c), device_id_type=MESH) for a in range(n)]
        for cp in cps:
            cp.start()
        for cp in cps:
            cp.wait()

    return pl.pallas_call(
        body, in_specs=[ANY] * n, out_specs=[ANY] * n, out_shape=[_sds(f.shape, f.dtype) for f in fs],
        scratch_shapes=[pltpu.SemaphoreType.DMA((n,)), pltpu.SemaphoreType.DMA((n,))],
        compiler_params=pltpu.CompilerParams(has_side_effects=True), name=name)(*fs)


def _tile2d(rows, cols, budget=2 * 1024 * 1024, step=2 * SUBLANES):
    fits = [t for t in range(step, rows + 1, step) if rows % t == 0 and t * cols * 4 <= budget]
    if fits:
        return fits[-1], cols
    fits = [t for t in range(LANES, cols + 1, LANES) if cols % t == 0 and rows * t * 4 <= budget]
    assert fits, (rows, cols)
    return rows, fits[-1]


def _chip_sum(own, parts, kidx, name):
    _, R, C = parts.shape
    tr, tc = _tile2d(R, C)

    def body(k_ref, o_ref_in, p1_ref, p2_ref, p3_ref, o_ref):
        tot = ((o_ref_in[...].astype(F32) + p1_ref[...].astype(F32)) + p2_ref[...].astype(F32)) + p3_ref[...].astype(F32)
        o_ref[...] = tot.astype(o_ref.dtype)

    def slot(d):
        return pl.BlockSpec((None, tr, tc), lambda i, j, k: ((k[0] + d) % N_CHIPS, i, j))

    return pl.pallas_call(
        body,
        grid_spec=pltpu.PrefetchScalarGridSpec(
            num_scalar_prefetch=1, grid=(R // tr, C // tc), in_specs=[slot(0), slot(1), slot(2), slot(3)],
            out_specs=pl.BlockSpec((tr, tc), lambda i, j, k: (i, j))),
        out_shape=_sds((R, C), BF16), compiler_params=_cp("parallel", "parallel"), name=name)(kidx, own, parts, parts, parts)


def _adamw_math(w, g, m, v):
    m2 = ADAM_B1 * m + (1.0 - ADAM_B1) * g
    v2 = ADAM_B2 * v + (1.0 - ADAM_B2) * (g * g)
    m_hat = m2 / (1.0 - ADAM_B1 ** ADAM_STEP)
    v_hat = v2 / (1.0 - ADAM_B2 ** ADAM_STEP)
    delta = -ADAM_LR * (m_hat / (jnp.sqrt(v_hat) + ADAM_EPS) + ADAM_WD * w)
    return delta, m2, v2


def _adamw(w, m, v, gparts, name):
    Lw, R, C = w.shape
    tr, tc = _tile2d(R, C)
    flat = [h for pair in gparts for h in pair]

    def body(*refs):
        w_ref, m_ref, v_ref = refs[:3]
        g_refs = refs[3:3 + 2 * Lw]
        go_ref, d_ref, mo_ref, vo_ref = refs[3 + 2 * Lw:]
        g = g_refs[0][...].astype(F32) + g_refs[1][...].astype(F32)
        for l in range(1, Lw):
            g = jnp.where(pl.program_id(0) == l, g_refs[2 * l][...].astype(F32) + g_refs[2 * l + 1][...].astype(F32), g)
        d, m2, v2 = _adamw_math(w_ref[...], g, m_ref[...], v_ref[...])
        go_ref[...] = g
        d_ref[...] = d
        mo_ref[...] = m2
        vo_ref[...] = v2

    blk = pl.BlockSpec((None, tr, tc), lambda l, i, j: (l, i, j))
    gblk = pl.BlockSpec((tr, tc), lambda l, i, j: (i, j))
    return pl.pallas_call(
        body, grid=(Lw, R // tr, C // tc), in_specs=[blk, blk, blk] + [gblk] * (2 * Lw), out_specs=[blk] * 4,
        out_shape=[_sds((Lw, R, C))] * 4, compiler_params=_cp("parallel", "parallel", "parallel"), name=name)(w, m, v, *flat)


def _small_adamw(grads, wmv, name):
    n = len(grads)

    def body(*refs):
        g_in, p_in, outs = refs[:n], refs[n:4 * n], refs[4 * n:]
        for a in range(n):
            g = g_in[a][...]
            d_, m2, v2 = _adamw_math(p_in[3 * a][...], g, p_in[3 * a + 1][...], p_in[3 * a + 2][...])
            outs[4 * a][...] = g
            outs[4 * a + 1][...] = d_
            outs[4 * a + 2][...] = m2
            outs[4 * a + 3][...] = v2

    vm = pl.BlockSpec(memory_space=pltpu.VMEM)
    args = list(grads) + [t for tri in wmv for t in tri]
    out_shape = [_sds(g.shape) for g in grads for _ in range(4)]
    return pl.pallas_call(body, in_specs=[vm] * len(args), out_specs=[vm] * len(out_shape), out_shape=out_shape,
                          compiler_params=pltpu.CompilerParams(vmem_limit_bytes=V7X_VMEM_LIMIT), name=name)(*args)


def _small_allreduce(partials, pshapes, loss_row, name):
    n = len(partials)
    gshapes = [p.shape for p in partials] + [loss_row.shape]
    ng = n + 1

    def body(*refs):
        g_in = refs[:ng]
        outs = refs[ng:2 * ng]
        sib = refs[2 * ng:3 * ng]
        pair = refs[3 * ng:4 * ng]
        bufs = refs[4 * ng:5 * ng]
        send1, recv1, send2, recv2 = refs[-4:]
        x, y, c = _place()
        k = 2 * x + y
        chips = [(1 - x, y), (x, 1 - y), (1 - x, 1 - y)]
        swaps = [pltpu.make_async_remote_copy(g_in[a], sib[a], send1.at[a], recv1.at[a],
                                              device_id=(x, y, 1 - c), device_id_type=MESH) for a in range(ng)]
        for cp in swaps:
            cp.start()
        for a, cp in enumerate(swaps):
            cp.wait()
            pair[a][...] = g_in[a][...] + sib[a][...]
            bufs[a][k] = pair[a][...]
        sends = [pltpu.make_async_remote_copy(pair[a], bufs[a].at[k], send2.at[a, j], recv2.at[a, j],
                                              device_id=(px, py, c), device_id_type=MESH)
                 for a in range(ng) for j, (px, py) in enumerate(chips)]
        for cp in sends:
            cp.start()
        for a in range(ng):
            for j, (px, py) in enumerate(chips):
                pltpu.make_async_remote_copy(pair[a], bufs[a].at[2 * px + py], send2.at[a, j], recv2.at[a, j],
                                             device_id=(px, py, c), device_id_type=MESH).wait_recv()
        for cp in sends:
            cp.wait_send()
        for a in range(ng):
            sharded = len(gshapes[a]) == 3

            def part(d):
                return bufs[a][d, k] if sharded else bufs[a][d]

            tot = part(0)
            for d in range(1, N_CHIPS):
                tot = tot + part(d)
            if a == n:
                outs[n][...] = tot
            else:
                pr, pc = pshapes[a]
                outs[a][...] = tot[:pr, :pc]

    vm = pl.BlockSpec(memory_space=pltpu.VMEM)
    args = list(partials) + [loss_row]
    out_shape = [_sds(ps) for ps in pshapes] + [_sds(loss_row.shape)]
    return pl.pallas_call(
        body, in_specs=[vm] * len(args), out_specs=[vm] * len(out_shape), out_shape=out_shape,
        scratch_shapes=[pltpu.VMEM(tuple(s), F32) for s in gshapes] * 2 + [pltpu.VMEM((N_CHIPS,) + tuple(s), F32) for s in gshapes]
        + [pltpu.SemaphoreType.DMA((ng,)), pltpu.SemaphoreType.DMA((ng,)),
           pltpu.SemaphoreType.DMA((ng, 3)), pltpu.SemaphoreType.DMA((ng, 3))],
        compiler_params=pltpu.CompilerParams(has_side_effects=True, vmem_limit_bytes=V7X_VMEM_LIMIT), name=name)(*args)


_PERM = (0, 2, 1, 3)


def _cols_from_shards(g):
    return g.transpose(1, 0, 2).reshape(g.shape[1], N_CHIPS * g.shape[2])


def _rope_tables(positions):
    inv_freq = ROPE_THETA ** (-jnp.arange(0, HEAD_DIM, 2, dtype=F32) / HEAD_DIM)
    ang = positions.astype(F32).reshape(-1, 1) * inv_freq
    cos, sin = jnp.cos(ang), jnp.sin(ang)
    cos = jnp.concatenate([cos, cos, cos, cos], axis=-1)
    sin_s = jnp.concatenate([-sin, sin, -sin, sin], axis=-1)
    return cos, sin_s


def kernel(x, positions, norm_mix, norm_ffn, norm_final, mix_w_in, pool_w, pool_scale, attn_sinks, mix_w_out, ssm_w_in, ssm_conv_w, ssm_conv_b, ssm_dt_bias, ssm_A_log, ssm_D, ssm_norm, ssm_w_out, ffn_w_up, ffn_conv_w, ffn_conv_b, ffn_w_down, loss_target, m_norm_mix, m_norm_ffn, m_norm_final, m_mix_w_in, m_pool_w, m_pool_scale, m_attn_sinks, m_mix_w_out, m_ssm_w_in, m_ssm_conv_w, m_ssm_conv_b, m_ssm_dt_bias, m_ssm_A_log, m_ssm_D, m_ssm_norm, m_ssm_w_out, m_ffn_w_up, m_ffn_conv_w, m_ffn_conv_b, m_ffn_w_down, v_norm_mix, v_norm_ffn, v_norm_final, v_mix_w_in, v_pool_w, v_pool_scale, v_attn_sinks, v_mix_w_out, v_ssm_w_in, v_ssm_conv_w, v_ssm_conv_b, v_ssm_dt_bias, v_ssm_A_log, v_ssm_D, v_ssm_norm, v_ssm_w_out, v_ffn_w_up, v_ffn_conv_w, v_ffn_conv_b, v_ffn_w_down):
    W = dict(norm_mix=norm_mix, norm_ffn=norm_ffn, norm_final=norm_final, mix_w_in=mix_w_in, pool_w=pool_w, pool_scale=pool_scale, attn_sinks=attn_sinks, mix_w_out=mix_w_out, ssm_w_in=ssm_w_in, ssm_conv_w=ssm_conv_w, ssm_conv_b=ssm_conv_b, ssm_dt_bias=ssm_dt_bias, ssm_A_log=ssm_A_log, ssm_D=ssm_D, ssm_norm=ssm_norm, ssm_w_out=ssm_w_out, ffn_w_up=ffn_w_up, ffn_conv_w=ffn_conv_w, ffn_conv_b=ffn_conv_b, ffn_w_down=ffn_w_down)
    Mo = dict(norm_mix=m_norm_mix, norm_ffn=m_norm_ffn, norm_final=m_norm_final, mix_w_in=m_mix_w_in, pool_w=m_pool_w, pool_scale=m_pool_scale, attn_sinks=m_attn_sinks, mix_w_out=m_mix_w_out, ssm_w_in=m_ssm_w_in, ssm_conv_w=m_ssm_conv_w, ssm_conv_b=m_ssm_conv_b, ssm_dt_bias=m_ssm_dt_bias, ssm_A_log=m_ssm_A_log, ssm_D=m_ssm_D, ssm_norm=m_ssm_norm, ssm_w_out=m_ssm_w_out, ffn_w_up=m_ffn_w_up, ffn_conv_w=m_ffn_conv_w, ffn_conv_b=m_ffn_conv_b, ffn_w_down=m_ffn_w_down)
    Vo = dict(norm_mix=v_norm_mix, norm_ffn=v_norm_ffn, norm_final=v_norm_final, mix_w_in=v_mix_w_in, pool_w=v_pool_w, pool_scale=v_pool_scale, attn_sinks=v_attn_sinks, mix_w_out=v_mix_w_out, ssm_w_in=v_ssm_w_in, ssm_conv_w=v_ssm_conv_w, ssm_conv_b=v_ssm_conv_b, ssm_dt_bias=v_ssm_dt_bias, ssm_A_log=v_ssm_A_log, ssm_D=v_ssm_D, ssm_norm=v_ssm_norm, ssm_w_out=v_ssm_w_out, ffn_w_up=v_ffn_w_up, ffn_conv_w=v_ffn_conv_w, ffn_conv_b=v_ffn_conv_b, ffn_w_down=v_ffn_w_down)

    kchip = 2 * lax.axis_index("x") + lax.axis_index("y")

    def own_slot(g, own):
        return lax.dynamic_update_slice_in_dim(g, own[None], kchip, axis=0)

    def tr(t):
        return jnp.swapaxes(t[0], 0, 1)

    later = dict(ffn0=[ffn_w_up[0].astype(MXU), ffn_w_down[0].astype(MXU)],
                 ssm=[tr(ssm_w_in).astype(MXU), ssm_w_out[0].astype(MXU)],
                 ffn1=[ffn_w_up[1].astype(MXU), ffn_w_down[1].astype(MXU)])
    sh = [tr(mix_w_in).astype(MXU), mix_w_out[0].astype(MXU), ssm_conv_w[0], ssm_conv_b, ssm_norm, ffn_conv_w]
    first = _gather_shards(sh, "gather_first")
    g_mi, g_mo, g_scw, g_scb, g_sn, g_fcw = [own_slot(g, own) for g, own in zip(first, sh)]
    started, token = _spread_start(list(later.values()), False, first[0], "gather_start", halved=(0,))
    started = dict(zip(later.keys(), started))
    fcw = [jnp.concatenate([g_fcw[p, i] for p in _PERM], axis=1) for i in range(2)]
    P = dict(
        nm=norm_mix, nf=norm_ffn, nfin=norm_final,
        wmiT=g_mi.reshape(MIX_IN_DIM, D_MODEL), wmo=g_mo.reshape(D_MODEL, D_MODEL),
        pool_w=pool_w[0], pool_scale=pool_scale, sinks=attn_sinks[0],
        scw=_cols_from_shards(g_scw), scb=g_scb.reshape(1, SSM_CONV_DIM), snorm=g_sn.reshape(1, SSM_D_INNER),
        dt_bias=jnp.pad(ssm_dt_bias, ((0, 0), (0, LANES - SSM_HEADS))), a_log=jnp.pad(ssm_A_log, ((0, 0), (0, LANES - SSM_HEADS))),
        d_exp=jnp.repeat(ssm_D, SSM_D_INNER // SSM_HEADS, axis=1),
        fcb=[jnp.concatenate([ffn_conv_b[i:i + 1, p * FFN_TC:(p + 1) * FFN_TC] for p in _PERM], axis=1) for i in range(2)],
    )

    def fetch(group, after):
        owns, lands = _spread_wait(started[group], False, after, f"gather_wait_{group}", halved=group == "ffn0")
        if group == "ffn0":
            lands = _sibling_fill(lands, "gather_fill_ffn0")
        a, b = [own_slot(g, own) for g, own in zip(lands, owns)]
        if group == "ssm":
            wsi = a.reshape(SSM_IN_DIM, D_MODEL)
            zx = SSM_D_INNER + SSM_CONV_DIM
            return dict(wsiT=wsi, wdtT=jnp.pad(wsi[zx:], ((0, LANES - SSM_HEADS), (0, 0))), wso=b.reshape(SSM_D_INNER, D_MODEL))
        i = int(group[-1])
        return dict(wup=jnp.concatenate([a[p] for p in _PERM], axis=1), wdn=b.reshape(D_FF, D_MODEL), fcw=fcw[i])

    cos, sin_s = _rope_tables(positions)
    sent = {}

    def send(group, grads):
        res, tok = _spread_start([grads], True, jnp.zeros((SUBLANES, LANES), F32), f"grad_start_{group}")
        sent[group] = res[0]
        return tok

    loss_row, grad_x, big, small = _local_step(x[0], cos, sin_s, loss_target[0], P, fetch, token, send)

    kidx = kchip.astype(jnp.int32).reshape(1)
    group_names = dict(ffn1=["ffn_w_up1", "ffn_w_down1"], ssm=["ssm_w_in", "ssm_w_out"], ffn0=["ffn_w_up0", "ffn_w_down0"],
                       mix=["mix_w_in", "mix_w_out"])
    names, mine = [], []
    for group, started_g in sent.items():
        grads, lands = _spread_wait(started_g, True, grad_x, f"grad_wait_{group}")
        for nm, g, land in zip(group_names[group], grads, lands):
            names.append(nm)
            mine.append(_chip_sum(g, land, kidx, f"chip_sum_{nm}"))
    theirs = _sibling_exchange(mine, "sibling_exchange")
    red = {nm: (a, b) for nm, a, b in zip(names, mine, theirs)}

    out = {}

    def big_update(pname, gparts, transposed=False):
        w = W[pname]
        lw = len(gparts)
        shp = w.shape
        rr, cc = gparts[0][0].shape
        fix = (lambda t: tr(t)[None]) if transposed else (lambda t: t.reshape(lw, rr, cc))
        res = _adamw(fix(w), fix(Mo[pname]), fix(Vo[pname]), gparts, f"adamw_{pname}")
        out[pname] = tuple((tr(r)[None] if transposed else r.reshape(shp)) for r in res)

    big_update("mix_w_in", [red["mix_w_in"]], transposed=True)
    big_update("mix_w_out", [red["mix_w_out"]])
    big_update("ssm_w_in", [red["ssm_w_in"]], transposed=True)
    big_update("ssm_w_out", [red["ssm_w_out"]])
    big_update("ffn_w_up", [red["ffn_w_up0"], red["ffn_w_up1"]])
    big_update("ffn_w_down", [red["ffn_w_down0"], red["ffn_w_down1"]])

    small_names = ["norm_mix", "norm_ffn", "norm_final", "pool_w", "pool_scale", "attn_sinks", "ssm_dt_bias", "ssm_A_log",
                   "ssm_D", "ffn_conv_b", "ssm_conv_w", "ssm_conv_b", "ssm_norm", "ffn_conv_w"]

    def as2d(t):
        if t.ndim == 1:
            return t.reshape(1, -1)
        return t.reshape(-1, t.shape[-1])

    wmv = [(as2d(W[nm]), as2d(Mo[nm]), as2d(Vo[nm])) for nm in small_names]
    summed = _small_allreduce([small[nm] for nm in small_names], [t[0].shape for t in wmv], loss_row, "small_allreduce")
    res = _small_adamw(summed[:-1], wmv, "small_adamw")
    for a, nm in enumerate(small_names):
        out[nm] = tuple(r.reshape(W[nm].shape) for r in res[4 * a:4 * a + 4])
    loss = summed[-1][0, 0]

    order = ["norm_mix", "norm_ffn", "norm_final", "mix_w_in", "pool_w", "pool_scale", "attn_sinks", "mix_w_out", "ssm_w_in",
             "ssm_conv_w", "ssm_conv_b", "ssm_dt_bias", "ssm_A_log", "ssm_D", "ssm_norm", "ssm_w_out", "ffn_w_up", "ffn_conv_w",
             "ffn_conv_b", "ffn_w_down"]
    return (loss, grad_x.reshape(x.shape), *[out[nm][0] for nm in order], *[out[nm][1] for nm in order],
            *[out[nm][2] for nm in order], *[out[nm][3] for nm in order])
```

```python
import functools

import jax
import jax.numpy as jnp
from jax import lax
from jax.experimental import pallas as pl
from jax.experimental.pallas import tpu as pltpu

F32 = jnp.float32
BF16 = jnp.bfloat16
MXU = BF16
HI = lax.Precision.HIGHEST

D_MODEL = 1024
POOL_WINDOWS = (2, 4, 8, 16)
POOL_DIM = 512
POOL_GROUP = 128
HEAD_DIM = 64
N_HEADS = 8
N_KV_HEADS = 2
GQ = 4
Q_DIM = 512
KV_DIM = 128
BLOCK = 128
ROPE_THETA = 10000.0
MIX_IN_DIM = 1280
SSM_D_INNER = 2048
SSM_HEADS = 32
SSM_GROUPS = 8
SSM_STATE = 128
SSM_CONV = 4
SSM_CHUNK = 128
SSM_CONV_DIM = 4096
SSM_IN_DIM = 6176
D_FF = 2816
FFN_CONV = 3
NORM_EPS = 1e-6
SSM_NORM_EPS = 1e-5
ADAM_LR = 0.001
ADAM_B1 = 0.9
ADAM_B2 = 0.999
ADAM_EPS = 1e-08
ADAM_WD = 0.01
ADAM_STEP = 10

N_CHIPS = 4
LANES = 128
SUBLANES = 8
V7X_VMEM_LIMIT = 56 * 1024 * 1024
NEG = -1e30
MESH = pl.DeviceIdType.MESH


def _cp(*sem):
    return pltpu.CompilerParams(dimension_semantics=sem if sem else None, vmem_limit_bytes=V7X_VMEM_LIMIT)


def _sds(shape, dtype=F32):
    return jax.ShapeDtypeStruct(tuple(shape), dtype)


def _iota(shape, dim):
    return lax.broadcasted_iota(jnp.int32, shape, dim)


def _silu(x):
    return x * (1.0 / (1.0 + jnp.exp(-x)))


def _dsilu(x):
    s = 1.0 / (1.0 + jnp.exp(-x))
    return s * (1.0 + x * (1.0 - s))


def _mm(a, b, *, ta=False, tb=False, tm, tn, tk, res=None, out_dtype=F32, out_shard_perm=None, out_into=None, b_rows=None,
        norm_w=None, norm_bwd=None, loss_head=None, name):
    M, K = (a.shape[1], a.shape[0]) if ta else a.shape
    N = b.shape[0] if tb else b.shape[1]
    boff = 0
    if b_rows is not None:
        boff = b_rows[0]
        if tb:
            N = b_rows[1]
        else:
            K = b_rows[1]
    tm, tn, tk = min(tm, M), min(tn, N), min(tk, K)
    gm, gn, gk = M // tm, N // tn, K // tk
    assert gm * tm == M and gn * tn == N and gk * tk == K, (name, M, N, K, tm, tn, tk)
    a_spec = pl.BlockSpec((tk, tm), lambda i, j, k: (k, i)) if ta else pl.BlockSpec((tm, tk), lambda i, j, k: (i, k))
    b_spec = pl.BlockSpec((tn, tk), lambda i, j, k: (j + boff, k)) if tb else pl.BlockSpec((tk, tn), lambda i, j, k: (k + boff, j))
    dims = (((0 if ta else 1,), (1 if tb else 0,)), ((), ()))
    has_res = res is not None
    has_nw = norm_w is not None
    has_nb = norm_bwd is not None
    has_lh = loss_head is not None
    has_tok = has_nb and norm_bwd[3] is not None
    assert not (has_nw or has_nb or has_lh) or (gn == 1 and out_shard_perm is None)
    n_extra = has_res + has_nw + (3 + has_tok if has_nb else 0) + (2 if has_lh else 0)

    def body(*refs):
        a_ref, b_ref = refs[0], refs[1]
        extra = list(refs[2:2 + n_extra])
        outs = refs[len(args):]
        r_ref = extra.pop(0) if has_res else None
        nw_ref = extra.pop(0) if has_nw else None
        nb_refs = extra if has_nb else None

        def dot():
            return lax.dot_general(a_ref[...].astype(MXU), b_ref[...].astype(MXU), dims, preferred_element_type=F32)

        def accumulate(o_ref, part):
            i = pl.program_id(0)

            @pl.when(i == 0)
            def _():
                o_ref[...] = part

            @pl.when(i > 0)
            def _():
                o_ref[...] += part

        def finish(r):
            if has_res:
                r = r + r_ref[...]
            if has_lh:
                wv = extra[0][...]
                rs = lax.rsqrt(jnp.mean(r * r, axis=-1, keepdims=True) + NORM_EPS)
                xh = r * rs
                e = xh * wv - extra[1][...]
                lpart = 0.5 * jnp.sum(jnp.mean(e * e, axis=-1, keepdims=True), axis=0, keepdims=True)
                dy = e * (1.0 / N)
                g = dy * wv
                outs[0][...] = rs * (g - xh * jnp.mean(g * xh, axis=-1, keepdims=True))
                accumulate(outs[1], jnp.sum(dy * xh, axis=0, keepdims=True))
                accumulate(outs[2], jnp.broadcast_to(lpart, (1, LANES)))
                return
            if has_nb:
                xv = nb_refs[0][...]
                rs = lax.rsqrt(jnp.mean(xv * xv, axis=-1, keepdims=True) + NORM_EPS)
                xh = xv * rs
                g = r * nb_refs[1][...]
                dr = nb_refs[2][...] + nb_refs[3][0:1, 0:1] if has_tok else nb_refs[2][...]
                outs[0][...] = dr + rs * (g - xh * jnp.mean(g * xh, axis=-1, keepdims=True))
                accumulate(outs[1], jnp.sum(r * xh, axis=0, keepdims=True))
                return
            outs[0][...] = r.astype(out_dtype)
            if has_nw:
                rs = lax.rsqrt(jnp.mean(r * r, axis=-1, keepdims=True) + NORM_EPS)
                outs[1][...] = (r * rs * nw_ref[...]).astype(outs[1].dtype)

        if gk == 1:
            finish(dot())
        else:
            acc = refs[-1]
            k = pl.program_id(2)

            @pl.when(k == 0)
            def _():
                acc[...] = dot()

            if gk > 2:
                @pl.when(jnp.logical_and(k > 0, k < gk - 1))
                def _():
                    acc[...] += dot()

            @pl.when(k == gk - 1)
            def _():
                finish(acc[...] + dot())

    tile = pl.BlockSpec((tm, tn), lambda i, j, k: (i, j))
    row = pl.BlockSpec((1, tn), lambda i, j, k: (0, j))
    in_specs = [a_spec, b_spec]
    args = [a, b]
    if has_res:
        in_specs.append(tile)
        args.append(res)
    if has_nw:
        in_specs.append(row)
        args.append(norm_w.reshape(1, N))
    if has_nb:
        in_specs += [tile, row, tile]
        args += [norm_bwd[0], norm_bwd[1].reshape(1, N), norm_bwd[2]]
        if has_tok:
            in_specs.append(pl.BlockSpec((SUBLANES, LANES), lambda i, j, k: (0, 0)))
            args.append(norm_bwd[3])
    if has_lh:
        in_specs += [row, tile]
        args += [loss_head[0].reshape(1, N), loss_head[1]]
    alias = {}
    if out_into is not None:
        buf, rows, off = out_into
        out_spec = pl.BlockSpec((tm, tn), lambda i, j, k: (i + off, j))
        out_shape = _sds((rows, N), out_dtype)
        if buf is not None:
            alias = {len(args): 0}
            in_specs.append(pl.BlockSpec(memory_space=pl.ANY))
            args.append(buf)
    elif out_shard_perm is None:
        out_spec = tile
        out_shape = _sds((M, N), out_dtype)
    else:
        assert gn == len(out_shard_perm) == 4 and tuple(out_shard_perm) == (0, 2, 1, 3)
        out_spec = pl.BlockSpec((None, tm, tn), lambda i, j, k: ((j % 2) * 2 + j // 2, i, 0))
        out_shape = _sds((gn, M, tn), out_dtype)
    sem = ("parallel", "parallel", "arbitrary")
    if has_nw:
        out_spec, out_shape = [out_spec, tile], [out_shape, _sds((M, N), MXU)]
    if has_nb:
        out_spec, out_shape = [tile, row], [_sds((M, N)), _sds((1, N))]
        sem = ("arbitrary", "arbitrary", "arbitrary")
    if has_lh:
        out_spec = [tile, row, pl.BlockSpec((1, LANES), lambda i, j, k: (0, 0))]
        out_shape = [_sds((M, N)), _sds((1, N)), _sds((1, LANES))]
        sem = ("arbitrary", "arbitrary", "arbitrary")
    return pl.pallas_call(
        body, grid=(gm, gn, gk), in_specs=in_specs, out_specs=out_spec, out_shape=out_shape,
        scratch_shapes=[pltpu.VMEM((tm, tn), F32)] if gk > 1 else [], input_output_aliases=alias,
        compiler_params=_cp(*sem), name=name)(*args)


def _put_rows(buf, src, rows, at, name):
    assert at % rows == 0 and src.shape[1] == buf.shape[1] and src.dtype == buf.dtype
    C = buf.shape[1]

    def body(s_ref, b_ref, o_ref):
        o_ref[...] = s_ref[...]

    return pl.pallas_call(
        body, grid=(1,), in_specs=[pl.BlockSpec((rows, C), lambda i: (0, 0)), pl.BlockSpec(memory_space=pl.ANY)],
        out_specs=pl.BlockSpec((rows, C), lambda i: (at // rows, 0)), out_shape=_sds(buf.shape, buf.dtype),
        input_output_aliases={1: 0}, compiler_params=_cp("arbitrary"), name=name)(src, buf)


def _rmsnorm_fwd(x, w, name, token=None):
    T, D = x.shape
    tm = min(T, 512)
    has_token = token is not None

    def body(*refs):
        x_ref, w_ref, o_ref = refs[0], refs[1], refs[-1]
        xv = x_ref[...]
        if has_token:
            xv = xv + refs[2][0:1, 0:1]
        r = lax.rsqrt(jnp.mean(xv * xv, axis=-1, keepdims=True) + NORM_EPS)
        o_ref[...] = (xv * r * w_ref[...]).astype(o_ref.dtype)

    in_specs = [pl.BlockSpec((tm, D), lambda i: (i, 0)), pl.BlockSpec((1, D), lambda i: (0, 0))]
    args = [x, w.reshape(1, D)]
    if has_token:
        in_specs.append(pl.BlockSpec((SUBLANES, LANES), lambda i: (0, 0)))
        args.append(token)
    return pl.pallas_call(
        body, grid=(T // tm,), in_specs=in_specs,
        out_specs=pl.BlockSpec((tm, D), lambda i: (i, 0)), out_shape=_sds((T, D), MXU),
        compiler_params=_cp("parallel"), name=name)(*args)


def _shift_down(cur, prev8, s):
    if s == 0:
        return cur
    tm = cur.shape[0]
    rc = pltpu.roll(cur, s, 0)
    top = jnp.where(_iota((SUBLANES, cur.shape[1]), 0) < s, pltpu.roll(prev8, s, 0), rc[:SUBLANES])
    return jnp.concatenate([top, rc[SUBLANES:]], axis=0) if tm > SUBLANES else top


def _shift_up(cur, next8, s):
    if s == 0:
        return cur
    tm = cur.shape[0]
    rc = pltpu.roll(cur, tm - s, 0)
    bot = jnp.where(_iota((SUBLANES, cur.shape[1]), 0) >= SUBLANES - s, pltpu.roll(next8, SUBLANES - s, 0), rc[tm - SUBLANES:])
    return jnp.concatenate([rc[:tm - SUBLANES], bot], axis=0) if tm > SUBLANES else bot


def _conv_rows(cur, prev8, w, b, K):
    acc = cur * w[K - 1:K, :] + b
    for s in range(1, K):
        acc = acc + _shift_down(cur, prev8, s) * w[K - 1 - s:K - s, :]
    return acc


FFN_TC = 1408


def _ffn_up_conv_gate(hf, wup, cw, cb, name):
    T, D = hf.shape
    tm = min(T, 256)
    nt, nj = T // tm, D_FF // FFN_TC
    K = FFN_CONV
    W2 = 2 * FFN_TC

    def body(a_ref, b_ref, w_ref, c_ref, hid_ref, hc_ref, act_ref, halo):
        i = pl.program_id(1)

        @pl.when(i == 0)
        def _():
            halo[...] = jnp.zeros(halo.shape, F32)

        hb = jnp.dot(a_ref[...].astype(MXU), b_ref[...].astype(MXU), preferred_element_type=F32).astype(hid_ref.dtype)
        hid_ref[...] = hb
        cur = hb.astype(F32)
        hc = _conv_rows(cur, halo[...], w_ref[...], c_ref[...], K)
        halo[...] = cur[tm - SUBLANES:]
        hc_ref[...] = hc
        act_ref[...] = (_silu(hc[:, FFN_TC:]) * hc[:, :FFN_TC]).astype(act_ref.dtype)

    blk = pl.BlockSpec((tm, W2), lambda j, i: (i, j))
    return pl.pallas_call(
        body, grid=(nj, nt),
        in_specs=[pl.BlockSpec((tm, D), lambda j, i: (i, 0)), pl.BlockSpec((D, W2), lambda j, i: (0, j)),
                  pl.BlockSpec((K, W2), lambda j, i: (0, j)), pl.BlockSpec((1, W2), lambda j, i: (0, j))],
        out_specs=[blk, blk, pl.BlockSpec((tm, FFN_TC), lambda j, i: (i, j))],
        out_shape=[_sds((T, 2 * D_FF), MXU), _sds((T, 2 * D_FF)), _sds((T, D_FF), MXU)],
        scratch_shapes=[pltpu.VMEM((SUBLANES, W2), F32)],
        compiler_params=_cp("arbitrary", "arbitrary"), name=name)(hf, wup, cw, cb)


def _ffn_down_dx_mid_bwd(dxo, wdn, hid, hc, cw, name):
    T, D = dxo.shape
    tm = min(T, 256)
    nt, nj = T // tm, D_FF // FFN_TC
    K = FFN_CONV
    W2 = 2 * FFN_TC

    def body(g_ref, wd_ref, h_ref, c_ref, w_ref, dh_ref, dw_ref, db_ref, ahead):
        i = pl.program_id(1)

        @pl.when(i == 0)
        def _():
            ahead[...] = jnp.zeros(ahead.shape, F32)

        w = w_ref[...]
        cur = h_ref[...].astype(F32)
        hcv = c_ref[...]
        dav = _nt(g_ref[...], wd_ref[...])
        u, g = hcv[:, :FFN_TC], hcv[:, FFN_TC:]
        d_cur = jnp.concatenate([dav * _silu(g), dav * u * _dsilu(g)], axis=1)
        d_nxt = ahead[...]
        ahead[...] = d_cur[:SUBLANES]
        ups = [d_cur] + [_shift_up(d_cur, d_nxt, s) for s in range(1, K)]
        dh = ups[0] * w[K - 1:K, :]
        for s in range(1, K):
            dh = dh + ups[s] * w[K - 1 - s:K - s, :]
        dh_ref[...] = dh.astype(dh_ref.dtype)
        dwp = jnp.concatenate([jnp.sum(ups[K - 1 - k] * cur, axis=0, keepdims=True) for k in range(K)], axis=0)
        dbp = jnp.sum(d_cur, axis=0, keepdims=True)

        @pl.when(i == 0)
        def _():
            dw_ref[...] = dwp
            db_ref[...] = dbp

        @pl.when(i > 0)
        def _():
            dw_ref[...] += dwp
            db_ref[...] += dbp

    blk = pl.BlockSpec((tm, W2), lambda j, i: (nt - 1 - i, j))
    return pl.pallas_call(
        body, grid=(nj, nt),
        in_specs=[pl.BlockSpec((tm, D), lambda j, i: (nt - 1 - i, 0)), pl.BlockSpec((FFN_TC, D), lambda j, i: (j, 0)), blk, blk,
                  pl.BlockSpec((K, W2), lambda j, i: (0, j))],
        out_specs=[blk, pl.BlockSpec((K, W2), lambda j, i: (0, j)), pl.BlockSpec((1, W2), lambda j, i: (0, j))],
        out_shape=[_sds((T, 2 * D_FF), MXU), _sds((K, 2 * D_FF)), _sds((1, 2 * D_FF))],
        scratch_shapes=[pltpu.VMEM((SUBLANES, W2), F32)],
        compiler_params=_cp("arbitrary", "arbitrary"), name=name)(dxo, wdn, hid, hc, cw)


def _rope(t, cos, sin_s, inverse=False):
    n = t.shape[1] // LANES
    c = jnp.concatenate([cos] * n, axis=1) if n > 1 else cos
    s = jnp.concatenate([sin_s] * n, axis=1) if n > 1 else sin_s
    a = pltpu.roll(t, HEAD_DIM // 2, 1)
    b = pltpu.roll(t, t.shape[1] - HEAD_DIM // 2, 1)
    first = (_iota(t.shape, 1) % HEAD_DIM) < HEAD_DIM // 2
    rot = jnp.where(first, b, a) * s
    return t * c - rot if inverse else t * c + rot


def _stack_heads(t, g):
    return jnp.concatenate([t[:, (GQ * g + r) * HEAD_DIM:(GQ * g + r + 1) * HEAD_DIM] for r in range(GQ)], axis=0)


def _stack_cols(t, g):
    return jnp.concatenate([t[:, GQ * g + r:GQ * g + r + 1] for r in range(GQ)], axis=0)


def _pool_sums(prev, cur, w):
    s = jnp.concatenate([prev, cur], axis=0)
    sh = 1
    while sh < w:
        s = s + pltpu.roll(s, sh, 0)
        sh *= 2
    return s[BLOCK:]


def _nt(a, b):
    return lax.dot_general(a.astype(MXU), b.astype(MXU), (((1,), (1,)), ((), ())), preferred_element_type=F32)


def _tn(a, b):
    return lax.dot_general(a.astype(MXU), b.astype(MXU), (((0,), (0,)), ((), ())), preferred_element_type=F32)


def _nn(a, b):
    return jnp.dot(a.astype(MXU), b.astype(MXU), preferred_element_type=F32)


def _mixcore_fwd(proj, cos, sin_s, pool_w, pool_scale, sinks, name):
    T = proj.shape[0]
    nb = T // BLOCK
    scale = HEAD_DIM ** -0.5

    def body(p_ref, pp_ref, c_ref, s_ref, cp_ref, sp_ref, pw_ref, ps_ref, sk_ref, cat_ref, at_ref, lse_ref):
        i = pl.program_id(0)
        has_prev = i > 0
        cur = p_ref[...]
        prv = jnp.where(has_prev, pp_ref[...], 0.0)
        tpos = (i * BLOCK + _iota((BLOCK, 1), 0) + 1).astype(F32)
        for g, w in enumerate(POOL_WINDOWS):
            sl = slice(g * POOL_GROUP, (g + 1) * POOL_GROUP)
            pooled = _pool_sums(prv[:, sl], cur[:, sl], w) / jnp.minimum(tpos, float(w)) - cur[:, sl]
            cat_ref[:, sl] = (_nn(pooled, pw_ref[g]) * ps_ref[:, sl]).astype(cat_ref.dtype)
        q = _rope(cur[:, POOL_DIM:POOL_DIM + Q_DIM], c_ref[...], s_ref[...])
        kc = _rope(cur[:, POOL_DIM + Q_DIM:POOL_DIM + Q_DIM + KV_DIM], c_ref[...], s_ref[...])
        kp = _rope(prv[:, POOL_DIM + Q_DIM:POOL_DIM + Q_DIM + KV_DIM], cp_ref[...], sp_ref[...])
        vc = cur[:, POOL_DIM + Q_DIM + KV_DIM:]
        vp = prv[:, POOL_DIM + Q_DIM + KV_DIM:]
        ri = _iota((GQ * BLOCK, BLOCK), 0) % BLOCK
        cj = _iota((GQ * BLOCK, BLOCK), 1)
        mc = cj <= ri
        mp = jnp.logical_and(cj > ri, has_prev)
        outs, lses = [], []
        for g in range(N_KV_HEADS):
            hs = slice(g * HEAD_DIM, (g + 1) * HEAD_DIM)
            qg = _stack_heads(q, g) * scale
            sc = jnp.where(mc, _nt(qg, kc[:, hs]), NEG)
            sp = jnp.where(mp, _nt(qg, kp[:, hs]), NEG)
            sink = jnp.concatenate([jnp.full((BLOCK, 1), sk_ref[GQ * g + r], F32) for r in range(GQ)], axis=0)
            m = jnp.maximum(jnp.maximum(jnp.max(sc, axis=1, keepdims=True), jnp.max(sp, axis=1, keepdims=True)), sink)
            pc = jnp.exp(sc - m)
            pp = jnp.exp(sp - m)
            den = jnp.sum(pc, axis=1, keepdims=True) + jnp.sum(pp, axis=1, keepdims=True) + jnp.exp(sink - m)
            o = (_nn(pc, vc[:, hs]) + _nn(pp, vp[:, hs])) / den
            lse = m + jnp.log(den)
            for r in range(GQ):
                outs.append(o[r * BLOCK:(r + 1) * BLOCK])
                lses.append(lse[r * BLOCK:(r + 1) * BLOCK])
        attn = jnp.concatenate(outs, axis=1)
        at_ref[...] = attn
        cat_ref[:, POOL_DIM:] = attn.astype(cat_ref.dtype)
        lane = _iota((BLOCK, LANES), 1)
        lrow = jnp.zeros((BLOCK, LANES), F32)
        for h in range(N_HEADS):
            lrow = jnp.where(lane == h, lses[h], lrow)
        lse_ref[...] = lrow

    cur = lambda w: pl.BlockSpec((BLOCK, w), lambda i: (i, 0))
    prv = lambda w: pl.BlockSpec((BLOCK, w), lambda i: (jnp.maximum(i - 1, 0), 0))
    return pl.pallas_call(
        body, grid=(nb,),
        in_specs=[cur(MIX_IN_DIM), prv(MIX_IN_DIM), cur(LANES), cur(LANES), prv(LANES), prv(LANES),
                  pl.BlockSpec((4, POOL_GROUP, POOL_GROUP), lambda i: (0, 0, 0)), pl.BlockSpec((1, POOL_DIM), lambda i: (0, 0)),
                  pl.BlockSpec(memory_space=pltpu.SMEM)],
        out_specs=[cur(2 * POOL_DIM), cur(Q_DIM), cur(LANES)],
        out_shape=[_sds((T, 2 * POOL_DIM), MXU), _sds((T, Q_DIM)), _sds((T, LANES))],
        compiler_params=_cp("parallel"), name=name)(proj, proj, cos, sin_s, cos, sin_s, pool_w, pool_scale, sinks)


def _mixcore_bwd(proj, cos, sin_s, pool_w, pool_scale, sinks, attn, lse, dcat, name):
    T = proj.shape[0]
    nb = T // BLOCK
    scale = HEAD_DIM ** -0.5
    QO, KO, VO = POOL_DIM, POOL_DIM + Q_DIM, POOL_DIM + Q_DIM + KV_DIM

    def body(p_ref, pp_ref, pn_ref, c_ref, s_ref, cp_ref, sp_ref, cn_ref, sn_ref, pw_ref, ps_ref, sk_ref,
             at_ref, atn_ref, l_ref, ln_ref, d_ref, dn_ref, dp_ref, dpw_ref, dps_ref, dsk_ref):
        i = pl.program_id(0)
        has_prev = i > 0
        has_next = i < nb - 1
        cur = p_ref[...]
        prv = jnp.where(has_prev, pp_ref[...], 0.0)
        d_cur = d_ref[...]
        d_nxt = jnp.where(has_next, dn_ref[...], 0.0)

        tpos = (i * BLOCK + _iota((BLOCK, 1), 0) + 1).astype(F32)
        tpos2 = (i * BLOCK + _iota((2 * BLOCK, 1), 0) + 1).astype(F32)
        ps = ps_ref[...]
        dps_parts, dpw_parts = [], []
        for g, w in enumerate(POOL_WINDOWS):
            sl = slice(g * POOL_GROUP, (g + 1) * POOL_GROUP)
            pooled = _pool_sums(prv[:, sl], cur[:, sl], w) / jnp.minimum(tpos, float(w)) - cur[:, sl]
            mixed = _nn(pooled, pw_ref[g])
            dps_parts.append(jnp.sum(d_cur[:, sl] * mixed, axis=0, keepdims=True))
            dm2 = jnp.concatenate([d_cur[:, sl], d_nxt[:, sl]], axis=0) * ps[:, sl]
            dpw_parts.append(_tn(pooled, dm2[:BLOCK]))
            dpool2 = _nt(dm2, pw_ref[g])
            e = dpool2 / jnp.minimum(tpos2, float(w))
            sh = 1
            while sh < w:
                e = e + pltpu.roll(e, 2 * BLOCK - sh, 0)
                sh *= 2
            dp_ref[:, sl] = (e[:BLOCK] - dpool2[:BLOCK]).astype(dp_ref.dtype)
        dpsp = jnp.concatenate(dps_parts, axis=1)

        nxt = pn_ref[...]
        q = _rope(cur[:, QO:KO], c_ref[...], s_ref[...])
        qn = _rope(nxt[:, QO:KO], cn_ref[...], sn_ref[...])
        kc = _rope(cur[:, KO:VO], c_ref[...], s_ref[...])
        kp = _rope(prv[:, KO:VO], cp_ref[...], sp_ref[...])
        vc, vp = cur[:, VO:], prv[:, VO:]
        do, don = d_cur[:, POOL_DIM:], d_nxt[:, POOL_DIM:]
        dl = do * at_ref[...]
        dln = don * atn_ref[...]
        lse, lsen = l_ref[...], ln_ref[...]
        ri = _iota((GQ * BLOCK, BLOCK), 0) % BLOCK
        cj = _iota((GQ * BLOCK, BLOCK), 1)
        mc = cj <= ri
        mp = jnp.logical_and(cj > ri, has_prev)
        mn = jnp.logical_and(cj > ri, has_next)
        dq_parts, dk_parts, dv_parts, dsk_vals = [], [], [], []
        for g in range(N_KV_HEADS):
            hs = slice(g * HEAD_DIM, (g + 1) * HEAD_DIM)
            qg, qng = _stack_heads(q, g) * scale, _stack_heads(qn, g) * scale
            dog, dong = _stack_heads(do, g), _stack_heads(don, g)
            delta = jnp.sum(_stack_heads(dl, g), axis=1, keepdims=True)
            deltan = jnp.sum(_stack_heads(dln, g), axis=1, keepdims=True)
            lg, lng = _stack_cols(lse, g), _stack_cols(lsen, g)
            pc = jnp.where(mc, jnp.exp(_nt(qg, kc[:, hs]) - lg), 0.0)
            pp = jnp.where(mp, jnp.exp(_nt(qg, kp[:, hs]) - lg), 0.0)
            pn = jnp.where(mn, jnp.exp(_nt(qng, kc[:, hs]) - lng), 0.0)
            dsc = pc * (_nt(dog, vc[:, hs]) - delta)
            dsp = pp * (_nt(dog, vp[:, hs]) - delta)
            dsn = pn * (_nt(dong, vc[:, hs]) - deltan)
            dqg = (_nn(dsc, kc[:, hs]) + _nn(dsp, kp[:, hs])) * scale
            dq_parts += [dqg[r * BLOCK:(r + 1) * BLOCK] for r in range(GQ)]
            dk_parts.append(_tn(dsc, qg) + _tn(dsn, qng))
            dv_parts.append(_tn(pc, dog) + _tn(pn, dong))
            sink = jnp.concatenate([jnp.full((BLOCK, 1), sk_ref[GQ * g + r], F32) for r in range(GQ)], axis=0)
            dsk = -jnp.exp(sink - lg) * delta
            dsk_vals += [jnp.sum(dsk[r * BLOCK:(r + 1) * BLOCK], axis=0, keepdims=True) for r in range(GQ)]
        dq = _rope(jnp.concatenate(dq_parts, axis=1), c_ref[...], s_ref[...], inverse=True)
        dk = _rope(jnp.concatenate(dk_parts, axis=1), c_ref[...], s_ref[...], inverse=True)
        dp_ref[:, QO:KO] = dq.astype(dp_ref.dtype)
        dp_ref[:, KO:VO] = dk.astype(dp_ref.dtype)
        dp_ref[:, VO:] = jnp.concatenate(dv_parts, axis=1).astype(dp_ref.dtype)
        lane = _iota((1, LANES), 1)
        dskp = jnp.zeros((1, LANES), F32)
        for h in range(N_HEADS):
            dskp = jnp.where(lane == h, dsk_vals[h], dskp)

        @pl.when(i == 0)
        def _():
            dps_ref[...] = dpsp
            dsk_ref[...] = dskp
            for g in range(4):
                dpw_ref[g] = dpw_parts[g]

        @pl.when(i > 0)
        def _():
            dps_ref[...] += dpsp
            dsk_ref[...] += dskp
            for g in range(4):
                dpw_ref[g] += dpw_parts[g]

    cur = lambda w: pl.BlockSpec((BLOCK, w), lambda i: (i, 0))
    prv = lambda w: pl.BlockSpec((BLOCK, w), lambda i: (jnp.maximum(i - 1, 0), 0))
    nxt = lambda w: pl.BlockSpec((BLOCK, w), lambda i: (jnp.minimum(i + 1, nb - 1), 0))
    return pl.pallas_call(
        body, grid=(nb,),
        in_specs=[cur(MIX_IN_DIM), prv(MIX_IN_DIM), nxt(MIX_IN_DIM),
                  cur(LANES), cur(LANES), prv(LANES), prv(LANES), nxt(LANES), nxt(LANES),
                  pl.BlockSpec((4, POOL_GROUP, POOL_GROUP), lambda i: (0, 0, 0)), pl.BlockSpec((1, POOL_DIM), lambda i: (0, 0)),
                  pl.BlockSpec(memory_space=pltpu.SMEM),
                  cur(Q_DIM), nxt(Q_DIM), cur(LANES), nxt(LANES), cur(2 * POOL_DIM), nxt(2 * POOL_DIM)],
        out_specs=[cur(MIX_IN_DIM), pl.BlockSpec((4, POOL_GROUP, POOL_GROUP), lambda i: (0, 0, 0)),
                   pl.BlockSpec((1, POOL_DIM), lambda i: (0, 0)), pl.BlockSpec((1, LANES), lambda i: (0, 0))],
        out_shape=[_sds((T, MIX_IN_DIM), MXU), _sds((4, POOL_GROUP, POOL_GROUP)), _sds((1, POOL_DIM)), _sds((1, LANES))],
        compiler_params=_cp("arbitrary"), name=name)(
            proj, proj, proj, cos, sin_s, cos, sin_s, cos, sin_s, pool_w, pool_scale, sinks, attn, attn, lse, lse, dcat, dcat)


GROUP_W = SSM_D_INNER // SSM_GROUPS


def _ssm_in_conv(h, wT, row_off, cw, cb, name):
    T, D = h.shape
    tm = min(T, 256)
    tc = 1024
    K = SSM_CONV

    def body(a_ref, b_ref, w_ref, c_ref, x_ref, pre_ref, act_ref, halo):
        @pl.when(pl.program_id(1) == 0)
        def _():
            halo[...] = jnp.zeros(halo.shape, F32)

        cur = _nt(a_ref[...], b_ref[...])
        x_ref[...] = cur
        pre = _conv_rows(cur, halo[...], w_ref[...], c_ref[...], K)
        halo[...] = cur[tm - SUBLANES:]
        pre_ref[...] = pre
        act_ref[...] = _silu(pre)

    blk = pl.BlockSpec((tm, tc), lambda j, i: (i, j))
    return pl.pallas_call(
        body, grid=(SSM_CONV_DIM // tc, T // tm),
        in_specs=[pl.BlockSpec((tm, D), lambda j, i: (i, 0)), pl.BlockSpec((tc, D), lambda j, i: (j + row_off // tc, 0)),
                  pl.BlockSpec((K, tc), lambda j, i: (0, j)), pl.BlockSpec((1, tc), lambda j, i: (0, j))],
        out_specs=[blk, blk, blk], out_shape=[_sds((T, SSM_CONV_DIM))] * 3,
        scratch_shapes=[pltpu.VMEM((SUBLANES, tc), F32)],
        compiler_params=_cp("arbitrary", "arbitrary"), name=name)(h, wT, cw, cb)


def _dot_hi(a, b):
    return jnp.dot(a, b, precision=HI, preferred_element_type=F32)


def _ssd_common(dtraw, bias, alog):
    L = SSM_CHUNK
    xb = dtraw + bias
    dt = jnp.maximum(xb, 0.0) + jnp.log1p(jnp.exp(-jnp.abs(xb)))
    A = -jnp.exp(alog)
    tril = (_iota((L, L), 1) <= _iota((L, L), 0)).astype(F32)
    acs = _dot_hi(tril, dt * A)
    return xb, dt, A, tril, acs


def _head_selectors():
    es = (_iota((LANES, SSM_D_INNER), 0) == _iota((LANES, SSM_D_INNER), 1) // HEAD_DIM).astype(BF16)
    est = (_iota((SSM_D_INNER, LANES), 1) == _iota((SSM_D_INNER, LANES), 0) // HEAD_DIM).astype(BF16)
    return es, est


def _dot_sel(v, sel):
    hi = v.astype(BF16)
    r1 = v - hi.astype(F32)
    mid = r1.astype(BF16)
    lo = (r1 - mid.astype(F32)).astype(BF16)
    d = lambda a: jnp.dot(a, sel, preferred_element_type=F32)
    return (d(hi) + d(mid)) + d(lo)


def _expand_heads(v, es):
    return _dot_sel(v, es)


def _reduce_heads(q, est):
    return _dot_sel(q, est)


def _per_state_row(v, g):
    return jnp.concatenate([jnp.broadcast_to(v[:, GQ * g + r:GQ * g + r + 1], (HEAD_DIM, 1)) for r in range(GQ)], axis=0)


def _ssd_fwd(xact, dtraw, dt_bias, a_log, z, d_skip, nw, name):
    T = xact.shape[0]
    nc = T // SSM_CHUNK
    L = SSM_CHUNK
    BO, CO = SSM_D_INNER, SSM_D_INNER + SSM_GROUPS * SSM_STATE

    def body(x_ref, dt_ref, bias_ref, al_ref, es_ref, z_ref, dsk_ref, nw_ref, y_ref, st_ref, yn_ref, state):
        @pl.when(pl.program_id(0) == 0)
        def _():
            state[...] = jnp.zeros(state.shape, F32)

        _, dt, A, tril, acs = _ssd_common(dt_ref[...], bias_ref[...], al_ref[...])
        acsT = acs.T
        last = acs[L - 1:L, :]
        cd = jnp.exp(last)
        es = es_ref[...]
        dtX = _expand_heads(dt, es)
        EX = _expand_heads(jnp.exp(acs), es)
        decX = _expand_heads(jnp.exp(last - acs), es)
        for g in range(SSM_GROUPS):
            gs = slice(g * GROUP_W, (g + 1) * GROUP_W)
            B = x_ref[:, BO + g * SSM_STATE:BO + (g + 1) * SSM_STATE]
            C = x_ref[:, CO + g * SSM_STATE:CO + (g + 1) * SSM_STATE]
            X = x_ref[:, gs] * dtX[:, gs]
            CB = _nt(C, B)
            yd = []
            for r in range(GQ):
                h = GQ * g + r
                Lm = jnp.exp(jnp.where(tril > 0, acs[:, h:h + 1] - acsT[h:h + 1, :], NEG))
                yd.append(_nn(CB * Lm, X[:, r * HEAD_DIM:(r + 1) * HEAD_DIM]))
            S = state[g]
            st_ref[g] = S
            y_ref[:, gs] = jnp.concatenate(yd, axis=1) + _nt(C, S) * EX[:, gs]
            state[g] = S * _per_state_row(cd, g) + _tn(X * decX[:, gs], B)
        y2 = (y_ref[...] + dsk_ref[...] * x_ref[:, :SSM_D_INNER]) * _silu(z_ref[...])
        r = lax.rsqrt(jnp.mean(y2 * y2, axis=-1, keepdims=True) + SSM_NORM_EPS)
        yn_ref[...] = (y2 * r * nw_ref[...]).astype(yn_ref.dtype)

    es, _ = _head_selectors()
    row = pl.BlockSpec((L, SSM_D_INNER), lambda c: (c, 0))
    vec = pl.BlockSpec((1, SSM_D_INNER), lambda c: (0, 0))
    return pl.pallas_call(
        body, grid=(nc,),
        in_specs=[pl.BlockSpec((L, SSM_CONV_DIM), lambda c: (c, 0)), pl.BlockSpec((L, LANES), lambda c: (c, 0)),
                  pl.BlockSpec((1, LANES), lambda c: (0, 0)), pl.BlockSpec((1, LANES), lambda c: (0, 0)),
                  pl.BlockSpec((LANES, SSM_D_INNER), lambda c: (0, 0)), row, vec, vec],
        out_specs=[row, pl.BlockSpec((None, SSM_GROUPS, GROUP_W, SSM_STATE), lambda c: (c, 0, 0, 0)), row],
        out_shape=[_sds((T, SSM_D_INNER)), _sds((nc, SSM_GROUPS, GROUP_W, SSM_STATE)), _sds((T, SSM_D_INNER), MXU)],
        scratch_shapes=[pltpu.VMEM((SSM_GROUPS, GROUP_W, SSM_STATE), F32)],
        compiler_params=_cp("arbitrary"), name=name)(xact, dtraw, dt_bias, a_log, es, z, d_skip, nw)


def _ssd_bwd(xact, xbc, xpre, cw, dtraw, dt_bias, a_log, d_skip, states, dyn, y, z, nw, name):
    T = xact.shape[0]
    nc = T // SSM_CHUNK
    L = SSM_CHUNK
    K = SSM_CONV
    BO, CO = SSM_D_INNER, SSM_D_INNER + SSM_GROUPS * SSM_STATE

    def body(x_ref, xin_ref, pre_ref, cw_ref, dt_ref, bias_ref, al_ref, dsk_ref, es_ref, est_ref, st_ref, dn_ref, y_ref,
             z_ref, nw_ref, dxbc_ref, dcw_ref, dcb_ref, ddt_ref, dbias_ref, dal_ref, dd_ref, dz_ref, dnw_ref,
             dstate, qa, qx, dxp_ref, ahead, dyv):
        cc = pl.program_id(0)

        @pl.when(cc == 0)
        def _():
            dstate[...] = jnp.zeros(dstate.shape, F32)
            ahead[...] = jnp.zeros(ahead.shape, F32)

        zv = z_ref[...]
        sz = _silu(zv)
        yg = y_ref[...] + dsk_ref[...] * x_ref[:, :SSM_D_INNER]
        y2 = yg * sz
        rn = lax.rsqrt(jnp.mean(y2 * y2, axis=-1, keepdims=True) + SSM_NORM_EPS)
        y2h = y2 * rn
        dn = dn_ref[...]
        gy = dn * nw_ref[...]
        dy2 = rn * (gy - y2h * jnp.mean(gy * y2h, axis=-1, keepdims=True))
        dyv[...] = dy2 * sz
        dz_ref[...] = (dy2 * yg * _dsilu(zv)).astype(dz_ref.dtype)
        dnwp = jnp.sum(dn * y2h, axis=0, keepdims=True)

        xb, dt, A, tril, acs = _ssd_common(dt_ref[...], bias_ref[...], al_ref[...])
        acsT = acs.T
        last = acs[L - 1:L, :]
        cd = jnp.exp(last)
        es, est = es_ref[...], est_ref[...]
        dtX = _expand_heads(dt, es)
        EX = _expand_heads(jnp.exp(acs), es)
        decX = _expand_heads(jnp.exp(last - acs), es)
        lane1 = _iota((1, LANES), 1)
        lane = _iota((L, LANES), 1)
        sub = _iota((L, LANES), 0)
        ztot = jnp.zeros((1, LANES), F32)
        wrow = jnp.zeros((L, LANES), F32)
        wcolT = jnp.zeros((LANES, L), F32)
        rows_dec, rows_dd = [], []
        for g in range(SSM_GROUPS):
            gs = slice(g * GROUP_W, (g + 1) * GROUP_W)
            x = x_ref[:, gs]
            B = x_ref[:, BO + g * SSM_STATE:BO + (g + 1) * SSM_STATE]
            C = x_ref[:, CO + g * SSM_STATE:CO + (g + 1) * SSM_STATE]
            dY = dyv[:, gs]
            dtx, e_x, dec_x = dtX[:, gs], EX[:, gs], decX[:, gs]
            X = x * dtx
            CB = _nt(C, B)
            S = st_ref[g]
            dS_out = dstate[g]
            dcb_sum = jnp.zeros((L, L), F32)
            dxd = []
            for r in range(GQ):
                h = GQ * g + r
                hs = slice(r * HEAD_DIM, (r + 1) * HEAD_DIM)
                Lm = jnp.exp(jnp.where(tril > 0, acs[:, h:h + 1] - acsT[h:h + 1, :], NEG))
                M = CB * Lm
                dM = _nt(dY[:, hs], X[:, hs])
                dxd.append(_tn(M, dY[:, hs]))
                dcb_sum = dcb_sum + dM * Lm
                Wm = dM * M
                wrow = jnp.where(lane == h, jnp.sum(Wm, axis=1, keepdims=True), wrow)
                wcolT = jnp.where(sub == h, jnp.sum(Wm, axis=0, keepdims=True), wcolT)
            dXd = jnp.concatenate(dxd, axis=1)
            G = _nt(C, S)
            dG = dY * e_x
            dDX = _nt(B, dS_out)
            dX = dXd + dec_x * dDX
            t_dec = dDX * X * dec_x
            qa[:, gs] = dG * G - t_dec
            qx[:, gs] = dX * x
            rows_dec.append(jnp.sum(t_dec, axis=0, keepdims=True))
            rows_dd.append(jnp.sum(dY * x, axis=0, keepdims=True))
            zc = jnp.sum(dS_out * S, axis=1, keepdims=True)
            for r in range(GQ):
                ztot = jnp.where(lane1 == GQ * g + r, jnp.sum(zc[r * HEAD_DIM:(r + 1) * HEAD_DIM], axis=0, keepdims=True), ztot)
            dxp_ref[:, gs] = dX * dtx + dY * dsk_ref[:, gs]
            dxp_ref[:, BO + g * SSM_STATE:BO + (g + 1) * SSM_STATE] = _tn(dcb_sum, C) + _nn(X * dec_x, dS_out)
            dxp_ref[:, CO + g * SSM_STATE:CO + (g + 1) * SSM_STATE] = _nn(dcb_sum, B) + _nn(dG, S)
            dstate[g] = dS_out * _per_state_row(cd, g) + _tn(dG, C)
        rows = jnp.concatenate([jnp.concatenate(rows_dec, axis=1), jnp.concatenate(rows_dd, axis=1)]
                               + [jnp.zeros((SUBLANES - 2, SSM_D_INNER), F32)], axis=0)
        rsum = _reduce_heads(rows, est)
        dlast = rsum[0:1, :] + cd * ztot
        dacs = (wrow - wcolT.T) + _reduce_heads(qa[...], est) + jnp.where(sub == L - 1, dlast, 0.0)
        triu = (_iota((L, L), 0) <= _iota((L, L), 1)).astype(F32)
        da = _dot_hi(triu, dacs)
        ddtraw = (da * A + _reduce_heads(qx[...], est)) * (1.0 / (1.0 + jnp.exp(-xb)))
        ddt_ref[...] = ddtraw
        dal = jnp.sum(da * dt, axis=0, keepdims=True) * A
        ddp = rsum[1:2, :]
        dbp = jnp.sum(ddtraw, axis=0, keepdims=True)
        w = cw_ref[...]
        d_cur = dxp_ref[...] * _dsilu(pre_ref[...])
        d_nxt = ahead[...]
        ahead[...] = d_cur[:SUBLANES]
        ups = [d_cur] + [_shift_up(d_cur, d_nxt, s) for s in range(1, K)]
        dxc = ups[0] * w[K - 1:K, :]
        for s in range(1, K):
            dxc = dxc + ups[s] * w[K - 1 - s:K - s, :]
        dxbc_ref[...] = dxc.astype(dxbc_ref.dtype)
        xin = xin_ref[...]
        dcwp = jnp.concatenate([jnp.sum(ups[K - 1 - k] * xin, axis=0, keepdims=True) for k in range(K)], axis=0)
        dcbp = jnp.sum(d_cur, axis=0, keepdims=True)

        @pl.when(cc == 0)
        def _():
            dbias_ref[...] = dbp
            dal_ref[...] = dal
            dd_ref[...] = ddp
            dcw_ref[...] = dcwp
            dcb_ref[...] = dcbp
            dnw_ref[...] = dnwp

        @pl.when(cc > 0)
        def _():
            dbias_ref[...] += dbp
            dal_ref[...] += dal
            dd_ref[...] += ddp
            dcw_ref[...] += dcwp
            dcb_ref[...] += dcbp
            dnw_ref[...] += dnwp

    rc = lambda c: nc - 1 - c
    vec = pl.BlockSpec((1, LANES), lambda c: (0, 0))
    wide = pl.BlockSpec((L, SSM_CONV_DIM), lambda c: (rc(c), 0))
    inner = pl.BlockSpec((L, SSM_D_INNER), lambda c: (rc(c), 0))
    es, est = _head_selectors()
    return pl.pallas_call(
        body, grid=(nc,),
        in_specs=[wide, wide, wide, pl.BlockSpec((K, SSM_CONV_DIM), lambda c: (0, 0)),
                  pl.BlockSpec((L, LANES), lambda c: (rc(c), 0)), vec, vec,
                  pl.BlockSpec((1, SSM_D_INNER), lambda c: (0, 0)),
                  pl.BlockSpec((LANES, SSM_D_INNER), lambda c: (0, 0)), pl.BlockSpec((SSM_D_INNER, LANES), lambda c: (0, 0)),
                  pl.BlockSpec((None, SSM_GROUPS, GROUP_W, SSM_STATE), lambda c: (rc(c), 0, 0, 0)),
                  inner, inner, inner, pl.BlockSpec((1, SSM_D_INNER), lambda c: (0, 0))],
        out_specs=[wide, pl.BlockSpec((K, SSM_CONV_DIM), lambda c: (0, 0)), pl.BlockSpec((1, SSM_CONV_DIM), lambda c: (0, 0)),
                   pl.BlockSpec((L, LANES), lambda c: (rc(c), 0)), vec, vec, vec, inner,
                   pl.BlockSpec((1, SSM_D_INNER), lambda c: (0, 0))],
        out_shape=[_sds((T, SSM_CONV_DIM), MXU), _sds((K, SSM_CONV_DIM)), _sds((1, SSM_CONV_DIM)),
                   _sds((T, LANES)), _sds((1, LANES)), _sds((1, LANES)), _sds((1, LANES)),
                   _sds((T, SSM_D_INNER), MXU), _sds((1, SSM_D_INNER))],
        scratch_shapes=[pltpu.VMEM((SSM_GROUPS, GROUP_W, SSM_STATE), F32), pltpu.VMEM((L, SSM_D_INNER), F32),
                        pltpu.VMEM((L, SSM_D_INNER), F32), pltpu.VMEM((L, SSM_CONV_DIM), F32),
                        pltpu.VMEM((SUBLANES, SSM_CONV_DIM), F32), pltpu.VMEM((L, SSM_D_INNER), F32)],
        compiler_params=_cp("arbitrary"), name=name)(xact, xbc, xpre, cw, dtraw, dt_bias, a_log, d_skip, es, est, states, dyn, y,
                                                     z, nw)


def _local_step(x0, cos, sin_s, target, P, fetch, token, send):
    mmf = functools.partial(_mm, tm=1024)
    big, small = {}, {}
    P = dict(P, wup={}, wdn={}, fcw={})
    h0 = _rmsnorm_fwd(x0, P["nm"][0], "norm_mix0", token=token)
    proj0 = mmf(h0, P["wmiT"], tb=True, tn=1280, tk=1024, name="mix_in")
    cat, attn, lse = _mixcore_fwd(proj0, cos, sin_s, P["pool_w"], P["pool_scale"], P["sinks"], "mixcore_fwd")
    x1, hf0 = mmf(cat, P["wmo"], tn=1024, tk=1024, res=x0, norm_w=P["nf"][0], name="mix_out")

    def ffn_fwd(xin, hf, i, **epilogue):
        got = fetch(f"ffn{i}", hf)
        P["wup"][i], P["wdn"][i], P["fcw"][i] = got["wup"], got["wdn"], got["fcw"]
        hid, hc, act = _ffn_up_conv_gate(hf, P["wup"][i], P["fcw"][i], P["fcb"][i], f"ffn_up{i}")
        xout = mmf(act, P["wdn"][i], tn=1024, tk=D_FF, res=xin, name=f"ffn_down{i}", **epilogue)
        return (hid, hc), act, xout

    hid0, act0, (x2, h1) = ffn_fwd(x1, hf0, 0, norm_w=P["nm"][1])
    P.update(fetch("ssm", h1))
    z = mmf(h1, P["wsiT"], tb=True, tn=1024, tk=1024, b_rows=(0, SSM_D_INNER), name="ssm_in_z")
    xbc, xpre, xact = _ssm_in_conv(h1, P["wsiT"], SSM_D_INNER, P["scw"], P["scb"], "ssm_in_xbc")
    dtraw = mmf(h1, P["wdtT"], tb=True, tn=128, tk=1024, name="ssm_in_dt")
    y, states, yn = _ssd_fwd(xact, dtraw, P["dt_bias"], P["a_log"], z, P["d_exp"], P["snorm"], "ssd_fwd")
    x3, hf1 = mmf(yn, P["wso"], tn=1024, tk=SSM_D_INNER, res=x2, norm_w=P["nf"][1], name="ssm_out")
    hid1, act1, (dx4, d_nfin, loss_row) = ffn_fwd(x3, hf1, 1, loss_head=(P["nfin"], target))
    small["norm_final"] = d_nfin

    def ffn_bwd(xin, dxo, hf, hid, act, i):
        big[f"ffn_w_down{i}"] = dwf(act, dxo, tm=1408, tn=1024, name=f"ffn_down_dw{i}").reshape(N_CHIPS, D_FF // N_CHIPS, D_MODEL)
        dhid, dcw, dcb = _ffn_down_dx_mid_bwd(dxo, P["wdn"][i], hid[0], hid[1], P["fcw"][i], f"ffn_down_dx{i}")
        big[f"ffn_w_up{i}"] = dwf(hf, dhid, tm=1024, tn=1408, tk=4096, out_shard_perm=(0, 2, 1, 3), name=f"ffn_up_dw{i}")
        tok = send(f"ffn{i}", [big[f"ffn_w_up{i}"], big[f"ffn_w_down{i}"]])
        dxi, dnf = _mm(dhid, P["wup"][i], tb=True, tm=512, tn=1024, tk=5632, norm_bwd=(xin, P["nf"][i], dxo, tok), name=f"ffn_up_dx{i}")
        return dxi, dnf, dcw, dcb

    dwf = functools.partial(_mm, ta=True, tk=2048, out_dtype=BF16)
    dx3, dnf1, dfcw1, dfcb1 = ffn_bwd(x3, dx4, hf1, hid1, act1, 1)
    dyn = mmf(dx3, P["wso"], tb=True, tn=1024, tk=1024, name="ssm_out_dx")
    big["ssm_w_out"] = dwf(yn, dx3, tm=1024, tn=1024, name="ssm_out_dw").reshape(N_CHIPS, SSM_D_INNER // N_CHIPS, D_MODEL)
    dxbc, d_scw, d_scb, ddtraw, d_dtb, d_alog, d_dskip, dz, d_snorm = _ssd_bwd(
        xact, xbc, xpre, P["scw"], dtraw, P["dt_bias"], P["a_log"], P["d_exp"], states, dyn, y, z, P["snorm"], "ssd_bwd")
    dwsi = dwf(dz, h1, tm=1024, tn=1024, tk=4096, out_into=(None, SSM_IN_DIM, 0), name="ssm_in_dw_z")
    dwsi = dwf(dxbc, h1, tm=1024, tn=1024, tk=4096, out_into=(dwsi, SSM_IN_DIM, SSM_D_INNER // 1024), name="ssm_in_dw_xbc")
    dwdt = dwf(ddtraw, h1, tm=128, tn=1024, name="ssm_in_dw_dt")
    dwsi = _put_rows(dwsi, dwdt, SSM_HEADS, SSM_D_INNER + SSM_CONV_DIM, "ssm_in_dw_put_dt")
    big["ssm_w_in"] = dwsi.reshape(N_CHIPS, SSM_IN_DIM // N_CHIPS, D_MODEL)
    tok = send("ssm", [big["ssm_w_in"], big["ssm_w_out"]])
    dh1 = mmf(dz, P["wsiT"], tn=1024, tk=2048, b_rows=(0, SSM_D_INNER), name="ssm_in_dx_z")
    dh1 = mmf(dxbc, P["wsiT"], tn=1024, tk=2048, b_rows=(SSM_D_INNER // 2048, SSM_CONV_DIM), res=dh1, name="ssm_in_dx_xbc")
    dx2, dnm1 = mmf(ddtraw, P["wdtT"], tn=1024, tk=128, res=dh1, norm_bwd=(x2, P["nm"][1], dx3, tok), name="ssm_in_dx_dt")
    dx1, dnf0, dfcw0, dfcb0 = ffn_bwd(x1, dx2, hf0, hid0, act0, 0)
    dcat = mmf(dx1, P["wmo"], tb=True, tn=1024, tk=1024, name="mix_out_dx")
    big["mix_w_out"] = dwf(cat, dx1, tm=1024, tn=1024, name="mix_out_dw").reshape(N_CHIPS, D_MODEL // N_CHIPS, D_MODEL)
    dproj0, d_pw, d_ps, d_sk = _mixcore_bwd(proj0, cos, sin_s, P["pool_w"], P["pool_scale"], P["sinks"], attn, lse, dcat, "mixcore_bwd")
    big["mix_w_in"] = dwf(dproj0, h0, tm=1280, tn=1024, tk=4096, name="mix_in_dw").reshape(N_CHIPS, MIX_IN_DIM // N_CHIPS, D_MODEL)
    tok = send("mix", [big["mix_w_in"], big["mix_w_out"]])
    dx0, dnm0 = mmf(dproj0, P["wmiT"], tn=1024, tk=1280, norm_bwd=(x0, P["nm"][0], dx1, tok), name="mix_in_dx")

    def unperm_cols(a):
        r = a.shape[0]
        t = a.reshape(r, N_CHIPS, FFN_TC)
        return jnp.stack([t[:, p] for p in _PERM], axis=0)

    small["norm_mix"] = jnp.concatenate([dnm0, dnm1], axis=0)
    small["norm_ffn"] = jnp.concatenate([dnf0, dnf1], axis=0)
    small["pool_w"] = d_pw.reshape(4 * POOL_GROUP, POOL_GROUP)
    small["pool_scale"] = d_ps
    small["attn_sinks"] = d_sk
    small["ssm_dt_bias"] = d_dtb
    small["ssm_A_log"] = d_alog
    small["ssm_D"] = d_dskip
    fcb = jnp.stack([unperm_cols(dfcb0), unperm_cols(dfcb1)], axis=0)
    small["ffn_conv_b"] = fcb.reshape(2, 2 * D_FF)
    small["ssm_conv_w"] = d_scw.reshape(SSM_CONV, N_CHIPS, SSM_CONV_DIM // N_CHIPS).transpose(1, 0, 2)
    small["ssm_conv_b"] = d_scb.reshape(N_CHIPS, 1, SSM_CONV_DIM // N_CHIPS)
    small["ssm_norm"] = d_snorm.reshape(N_CHIPS, 1, SSM_D_INNER // N_CHIPS)
    small["ffn_conv_w"] = jnp.concatenate([unperm_cols(dfcw0), unperm_cols(dfcw1)], axis=1)
    return loss_row, dx0, big, small


ANY = pl.BlockSpec(memory_space=pl.ANY)


def _place():
    return lax.axis_index("x"), lax.axis_index("y"), lax.axis_index("c")


def _gather_shards(shards, name):
    n = len(shards)
    split = [s.size >= (1 << 16) for s in shards]

    def half(ref, a, h):
        shp = shards[a].shape
        if len(shp) == 3:
            return ref.at[h]
        r2 = shp[0] // 2
        return ref.at[pl.ds(pl.multiple_of(h * r2, 2 * SUBLANES), r2), :]

    def body(*refs):
        ins, outs = refs[:n], refs[n:2 * n]
        send, recv, fsend, frecv = refs[2 * n:]
        x, y, c = _place()
        k = 2 * x + y
        chips = [(1 - x, y), (x, 1 - y), (1 - x, 1 - y)]

        def ici(a, j, src_slot_ref, dst_slot):
            px, py = chips[j]
            src = half(src_slot_ref, a, c) if split[a] else src_slot_ref
            dst = half(outs[a].at[dst_slot], a, c) if split[a] else outs[a].at[dst_slot]
            return pltpu.make_async_remote_copy(src, dst, send.at[a, j], recv.at[a, j], device_id=(px, py, c), device_id_type=MESH)

        def d2d(a, j, h):
            px, py = chips[j]
            part = half(outs[a].at[2 * px + py], a, h)
            return pltpu.make_async_remote_copy(part, part, fsend.at[a, j], frecv.at[a, j], device_id=(x, y, 1 - c), device_id_type=MESH)

        sends = [ici(a, j, ins[a], k) for a in range(n) for j in range(3)]
        for cp in sends:
            cp.start()
        passed = []
        for a in range(n):
            for j, (px, py) in enumerate(chips):
                ici(a, j, ins[a], 2 * px + py).wait_recv()
                if split[a]:
                    passed.append(d2d(a, j, c))
                    passed[-1].start()
        for a in range(n):
            if split[a]:
                for j in range(3):
                    d2d(a, j, 1 - c).wait_recv()
        for cp in sends + passed:
            cp.wait_send()

    return pl.pallas_call(
        body, in_specs=[ANY] * n, out_specs=[ANY] * n,
        out_shape=[_sds((N_CHIPS,) + s.shape, s.dtype) for s in shards],
        scratch_shapes=[pltpu.SemaphoreType.DMA((n, 3))] * 4,
        compiler_params=pltpu.CompilerParams(has_side_effects=True), name=name)(*shards)


HBM = pl.BlockSpec(memory_space=pltpu.HBM)
SEM = pl.BlockSpec(memory_space=pltpu.SEMAPHORE)
DATAFLOW = pltpu.SideEffectType.DATAFLOW_SIDE_EFFECTING


def _row_half(ref, h):
    r2 = ref.shape[0] // 2
    return ref.at[pl.ds(pl.multiple_of(h * r2, 2 * SUBLANES), r2), :]


def _spread_start(groups, slot_src, after, name, halved=()):
    flat = [a for grp in groups for a in grp]
    n = len(flat)
    ng = len(groups)
    offs = [sum(len(g) for g in groups[:i]) for i in range(ng)]
    lshape = [(a.shape if slot_src else (N_CHIPS,) + a.shape) for a in flat]

    nsem = 6 * n

    def body(*refs):
        src, land = refs[:n], refs[n:2 * n]
        sems = refs[2 * n + 1:2 * n + 1 + nsem]
        token = refs[-1]
        x, y, c = _place()
        k = 2 * x + y
        chips = [(1 - x, y), (x, 1 - y), (1 - x, 1 - y)]
        for a in range(n):
            half = any(offs[gi] <= a < offs[gi] + len(groups[gi]) for gi in halved)
            for j, (px, py) in enumerate(chips):
                s = src[a].at[2 * px + py] if slot_src else src[a]
                d = land[a].at[k]
                if half:
                    s, d = _row_half(s, c), _row_half(d, c)
                pltpu.make_async_remote_copy(s, d, sems[6 * a + 2 * j], sems[6 * a + 2 * j + 1],
                                             device_id=(px, py, c), device_id_type=MESH).start()
        token[...] = jnp.zeros(token.shape, token.dtype)

    out_shape = [pltpu.SemaphoreType.DMA(())] * nsem
    out_shape += [pltpu.HBM(a.shape, a.dtype) for a in flat] + [pltpu.HBM(s, a.dtype) for s, a in zip(lshape, flat)]
    out_shape.append(_sds((SUBLANES, LANES)))
    args = [pltpu.with_memory_space_constraint(a, pltpu.HBM) for a in flat]
    args += [pltpu.with_memory_space_constraint(lax.empty(s, a.dtype), pltpu.HBM) for s, a in zip(lshape, flat)]
    res = pl.pallas_call(
        body, name=name, out_shape=tuple(out_shape), in_specs=[HBM] * (2 * n) + [pl.BlockSpec(memory_space=pl.ANY)],
        out_specs=tuple([SEM] * nsem + [HBM] * (2 * n) + [pl.BlockSpec(memory_space=pltpu.VMEM)]),
        input_output_aliases={i: nsem + i for i in range(2 * n)},
        compiler_params=pltpu.CompilerParams(has_side_effects=DATAFLOW))(*args, after)
    sems, thru, token = res[:nsem], res[nsem:nsem + 2 * n], res[-1]
    out = []
    for gi, grp in enumerate(groups):
        sl = slice(offs[gi], offs[gi] + len(grp))
        out.append((list(sems[6 * offs[gi]:6 * (offs[gi] + len(grp))]), list(thru[:n][sl]), list(thru[n:][sl])))
    return out, token


def _spread_wait(started, slot_src, after, name, halved=False):
    sems, srcs, lands = started
    n = len(srcs)

    def body(*refs):
        src, land = refs[:n], refs[n:2 * n]
        sem = refs[2 * n:2 * n + 6 * n]
        x, y, c = _place()
        chips = [(1 - x, y), (x, 1 - y), (1 - x, 1 - y)]
        for a in range(n):
            for j, (px, py) in enumerate(chips):
                s = src[a].at[2 * px + py] if slot_src else src[a]
                d = land[a].at[2 * px + py]
                if halved:
                    s, d = _row_half(s, c), _row_half(d, c)
                cp = pltpu.make_async_remote_copy(s, d, sem[6 * a + 2 * j], sem[6 * a + 2 * j + 1],
                                                  device_id=(px, py, c), device_id_type=MESH)
                cp.wait_send()
                cp.wait_recv()

    res = pl.pallas_call(
        body, name=name, out_shape=tuple([pltpu.HBM(a.shape, a.dtype) for a in srcs] + [pltpu.HBM(a.shape, a.dtype) for a in lands]),
        in_specs=[HBM] * (2 * n) + [SEM] * (6 * n) + [pl.BlockSpec(memory_space=pl.ANY)], out_specs=tuple([HBM] * (2 * n)),
        input_output_aliases={i: i for i in range(2 * n)},
        compiler_params=pltpu.CompilerParams(has_side_effects=DATAFLOW))(*srcs, *lands, *sems, after)
    return list(res[:n]), list(res[n:])


def _sibling_fill(lands, name):
    n = len(lands)

    def body(*refs):
        bufs = refs[n:2 * n]
        send, recv = refs[2 * n:]
        x, y, c = _place()
        chips = [(1 - x, y), (x, 1 - y), (1 - x, 1 - y)]

        def copy(a, j, h):
            px, py = chips[j]
            part = _row_half(bufs[a].at[2 * px + py], h)
            return pltpu.make_async_remote_copy(part, part, send.at[a, j], recv.at[a, j], device_id=(x, y, 1 - c), device_id_type=MESH)

        sends = [copy(a, j, c) for a in range(n) for j in range(3)]
        for cp in sends:
            cp.start()
        for a in range(n):
            for j in range(3):
                copy(a, j, 1 - c).wait_recv()
        for cp in sends:
            cp.wait_send()

    return pl.pallas_call(
        body, in_specs=[ANY] * n, out_specs=[ANY] * n, out_shape=[_sds(t.shape, t.dtype) for t in lands],
        input_output_aliases={i: i for i in range(n)},
        scratch_shapes=[pltpu.SemaphoreType.DMA((n, 3)), pltpu.SemaphoreType.DMA((n, 3))],
        compiler_params=pltpu.CompilerParams(has_side_effects=True), name=name)(*lands)


def _sibling_exchange(fs, name):
    n = len(fs)

    def body(*refs):
        ins, outs = refs[:n], refs[n:2 * n]
        send, recv = refs[2 * n:]
        x, y, c = _place()
        cps = [pltpu.make_async_remote_copy(ins[a], outs[a], send.at[a], recv.at[a],
                                            device_id=(x, y, 1 - c), device_id_type=MESH) for a in range(n)]
        for cp in cps:
            cp.start()
        for cp in cps:
            cp.wait()

    return pl.pallas_call(
        body, in_specs=[ANY] * n, out_specs=[ANY] * n, out_shape=[_sds(f.shape, f.dtype) for f in fs],
        scratch_shapes=[pltpu.SemaphoreType.DMA((n,)), pltpu.SemaphoreType.DMA((n,))],
        compiler_params=pltpu.CompilerParams(has_side_effects=True), name=name)(*fs)


def _tile2d(rows, cols, budget=2 * 1024 * 1024, step=2 * SUBLANES):
    fits = [t for t in range(step, rows + 1, step) if rows % t == 0 and t * cols * 4 <= budget]
    if fits:
        return fits[-1], cols
    fits = [t for t in range(LANES, cols + 1, LANES) if cols % t == 0 and rows * t * 4 <= budget]
    assert fits, (rows, cols)
    return rows, fits[-1]


def _chip_sum(own, parts, kidx, name):
    _, R, C = parts.shape
    tr, tc = _tile2d(R, C)

    def body(k_ref, o_ref_in, p1_ref, p2_ref, p3_ref, o_ref):
        tot = ((o_ref_in[...].astype(F32) + p1_ref[...].astype(F32)) + p2_ref[...].astype(F32)) + p3_ref[...].astype(F32)
        o_ref[...] = tot.astype(o_ref.dtype)

    def slot(d):
        return pl.BlockSpec((None, tr, tc), lambda i, j, k: ((k[0] + d) % N_CHIPS, i, j))

    return pl.pallas_call(
        body,
        grid_spec=pltpu.PrefetchScalarGridSpec(
            num_scalar_prefetch=1, grid=(R // tr, C // tc), in_specs=[slot(0), slot(1), slot(2), slot(3)],
            out_specs=pl.BlockSpec((tr, tc), lambda i, j, k: (i, j))),
        out_shape=_sds((R, C), BF16), compiler_params=_cp("parallel", "parallel"), name=name)(kidx, own, parts, parts, parts)


def _adamw_math(w, g, m, v):
    m2 = ADAM_B1 * m + (1.0 - ADAM_B1) * g
    v2 = ADAM_B2 * v + (1.0 - ADAM_B2) * (g * g)
    m_hat = m2 / (1.0 - ADAM_B1 ** ADAM_STEP)
    v_hat = v2 / (1.0 - ADAM_B2 ** ADAM_STEP)
    delta = -ADAM_LR * (m_hat / (jnp.sqrt(v_hat) + ADAM_EPS) + ADAM_WD * w)
    return delta, m2, v2


def _adamw(w, m, v, gparts, name):
    Lw, R, C = w.shape
    tr, tc = _tile2d(R, C)
    flat = [h for pair in gparts for h in pair]

    def body(*refs):
        w_ref, m_ref, v_ref = refs[:3]
        g_refs = refs[3:3 + 2 * Lw]
        go_ref, d_ref, mo_ref, vo_ref = refs[3 + 2 * Lw:]
        g = g_refs[0][...].astype(F32) + g_refs[1][...].astype(F32)
        for l in range(1, Lw):
            g = jnp.where(pl.program_id(0) == l, g_refs[2 * l][...].astype(F32) + g_refs[2 * l + 1][...].astype(F32), g)
        d, m2, v2 = _adamw_math(w_ref[...], g, m_ref[...], v_ref[...])
        go_ref[...] = g
        d_ref[...] = d
        mo_ref[...] = m2
        vo_ref[...] = v2

    blk = pl.BlockSpec((None, tr, tc), lambda l, i, j: (l, i, j))
    gblk = pl.BlockSpec((tr, tc), lambda l, i, j: (i, j))
    return pl.pallas_call(
        body, grid=(Lw, R // tr, C // tc), in_specs=[blk, blk, blk] + [gblk] * (2 * Lw), out_specs=[blk] * 4,
        out_shape=[_sds((Lw, R, C))] * 4, compiler_params=_cp("parallel", "parallel", "parallel"), name=name)(w, m, v, *flat)


def _small_adamw(grads, wmv, name):
    n = len(grads)

    def body(*refs):
        g_in, p_in, outs = refs[:n], refs[n:4 * n], refs[4 * n:]
        for a in range(n):
            g = g_in[a][...]
            d_, m2, v2 = _adamw_math(p_in[3 * a][...], g, p_in[3 * a + 1][...], p_in[3 * a + 2][...])
            outs[4 * a][...] = g
            outs[4 * a + 1][...] = d_
            outs[4 * a + 2][...] = m2
            outs[4 * a + 3][...] = v2

    vm = pl.BlockSpec(memory_space=pltpu.VMEM)
    args = list(grads) + [t for tri in wmv for t in tri]
    out_shape = [_sds(g.shape) for g in grads for _ in range(4)]
    return pl.pallas_call(body, in_specs=[vm] * len(args), out_specs=[vm] * len(out_shape), out_shape=out_shape,
                          compiler_params=pltpu.CompilerParams(vmem_limit_bytes=V7X_VMEM_LIMIT), name=name)(*args)


def _small_allreduce(partials, pshapes, loss_row, name):
    n = len(partials)
    gshapes = [p.shape for p in partials] + [loss_row.shape]
    ng = n + 1

    def body(*refs):
        g_in = refs[:ng]
        outs = refs[ng:2 * ng]
        sib = refs[2 * ng:3 * ng]
        pair = refs[3 * ng:4 * ng]
        bufs = refs[4 * ng:5 * ng]
        send1, recv1, send2, recv2 = refs[-4:]
        x, y, c = _place()
        k = 2 * x + y
        chips = [(1 - x, y), (x, 1 - y), (1 - x, 1 - y)]
        swaps = [pltpu.make_async_remote_copy(g_in[a], sib[a], send1.at[a], recv1.at[a],
                                              device_id=(x, y, 1 - c), device_id_type=MESH) for a in range(ng)]
        for cp in swaps:
            cp.start()
        for a, cp in enumerate(swaps):
            cp.wait()
            pair[a][...] = g_in[a][...] + sib[a][...]
            bufs[a][k] = pair[a][...]
        sends = [pltpu.make_async_remote_copy(pair[a], bufs[a].at[k], send2.at[a, j], recv2.at[a, j],
                                              device_id=(px, py, c), device_id_type=MESH)
                 for a in range(ng) for j, (px, py) in enumerate(chips)]
        for cp in sends:
            cp.start()
        for a in range(ng):
            for j, (px, py) in enumerate(chips):
                pltpu.make_async_remote_copy(pair[a], bufs[a].at[2 * px + py], send2.at[a, j], recv2.at[a, j],
                                             device_id=(px, py, c), device_id_type=MESH).wait_recv()
        for cp in sends:
            cp.wait_send()
        for a in range(ng):
            sharded = len(gshapes[a]) == 3

            def part(d):
                return bufs[a][d, k] if sharded else bufs[a][d]

            tot = part(0)
            for d in range(1, N_CHIPS):
                tot = tot + part(d)
            if a == n:
                outs[n][...] = tot
            else:
                pr, pc = pshapes[a]
                outs[a][...] = tot[:pr, :pc]

    vm = pl.BlockSpec(memory_space=pltpu.VMEM)
    args = list(partials) + [loss_row]
    out_shape = [_sds(ps) for ps in pshapes] + [_sds(loss_row.shape)]
    return pl.pallas_call(
        body, in_specs=[vm] * len(args), out_specs=[vm] * len(out_shape), out_shape=out_shape,
        scratch_shapes=[pltpu.VMEM(tuple(s), F32) for s in gshapes] * 2 + [pltpu.VMEM((N_CHIPS,) + tuple(s), F32) for s in gshapes]
        + [pltpu.SemaphoreType.DMA((ng,)), pltpu.SemaphoreType.DMA((ng,)),
           pltpu.SemaphoreType.DMA((ng, 3)), pltpu.SemaphoreType.DMA((ng, 3))],
        compiler_params=pltpu.CompilerParams(has_side_effects=True, vmem_limit_bytes=V7X_VMEM_LIMIT), name=name)(*args)


_PERM = (0, 2, 1, 3)


def _cols_from_shards(g):
    return g.transpose(1, 0, 2).reshape(g.shape[1], N_CHIPS * g.shape[2])


def _rope_tables(positions):
    inv_freq = ROPE_THETA ** (-jnp.arange(0, HEAD_DIM, 2, dtype=F32) / HEAD_DIM)
    ang = positions.astype(F32).reshape(-1, 1) * inv_freq
    cos, sin = jnp.cos(ang), jnp.sin(ang)
    cos = jnp.concatenate([cos, cos, cos, cos], axis=-1)
    sin_s = jnp.concatenate([-sin, sin, -sin, sin], axis=-1)
    return cos, sin_s


def kernel(x, positions, norm_mix, norm_ffn, norm_final, mix_w_in, pool_w, pool_scale, attn_sinks, mix_w_out, ssm_w_in, ssm_conv_w, ssm_conv_b, ssm_dt_bias, ssm_A_log, ssm_D, ssm_norm, ssm_w_out, ffn_w_up, ffn_conv_w, ffn_conv_b, ffn_w_down, loss_target, m_norm_mix, m_norm_ffn, m_norm_final, m_mix_w_in, m_pool_w, m_pool_scale, m_attn_sinks, m_mix_w_out, m_ssm_w_in, m_ssm_conv_w, m_ssm_conv_b, m_ssm_dt_bias, m_ssm_A_log, m_ssm_D, m_ssm_norm, m_ssm_w_out, m_ffn_w_up, m_ffn_conv_w, m_ffn_conv_b, m_ffn_w_down, v_norm_mix, v_norm_ffn, v_norm_final, v_mix_w_in, v_pool_w, v_pool_scale, v_attn_sinks, v_mix_w_out, v_ssm_w_in, v_ssm_conv_w, v_ssm_conv_b, v_ssm_dt_bias, v_ssm_A_log, v_ssm_D, v_ssm_norm, v_ssm_w_out, v_ffn_w_up, v_ffn_conv_w, v_ffn_conv_b, v_ffn_w_down):
    W = dict(norm_mix=norm_mix, norm_ffn=norm_ffn, norm_final=norm_final, mix_w_in=mix_w_in, pool_w=pool_w, pool_scale=pool_scale, attn_sinks=attn_sinks, mix_w_out=mix_w_out, ssm_w_in=ssm_w_in, ssm_conv_w=ssm_conv_w, ssm_conv_b=ssm_conv_b, ssm_dt_bias=ssm_dt_bias, ssm_A_log=ssm_A_log, ssm_D=ssm_D, ssm_norm=ssm_norm, ssm_w_out=ssm_w_out, ffn_w_up=ffn_w_up, ffn_conv_w=ffn_conv_w, ffn_conv_b=ffn_conv_b, ffn_w_down=ffn_w_down)
    Mo = dict(norm_mix=m_norm_mix, norm_ffn=m_norm_ffn, norm_final=m_norm_final, mix_w_in=m_mix_w_in, pool_w=m_pool_w, pool_scale=m_pool_scale, attn_sinks=m_attn_sinks, mix_w_out=m_mix_w_out, ssm_w_in=m_ssm_w_in, ssm_conv_w=m_ssm_conv_w, ssm_conv_b=m_ssm_conv_b, ssm_dt_bias=m_ssm_dt_bias, ssm_A_log=m_ssm_A_log, ssm_D=m_ssm_D, ssm_norm=m_ssm_norm, ssm_w_out=m_ssm_w_out, ffn_w_up=m_ffn_w_up, ffn_conv_w=m_ffn_conv_w, ffn_conv_b=m_ffn_conv_b, ffn_w_down=m_ffn_w_down)
    Vo = dict(norm_mix=v_norm_mix, norm_ffn=v_norm_ffn, norm_final=v_norm_final, mix_w_in=v_mix_w_in, pool_w=v_pool_w, pool_scale=v_pool_scale, attn_sinks=v_attn_sinks, mix_w_out=v_mix_w_out, ssm_w_in=v_ssm_w_in, ssm_conv_w=v_ssm_conv_w, ssm_conv_b=v_ssm_conv_b, ssm_dt_bias=v_ssm_dt_bias, ssm_A_log=v_ssm_A_log, ssm_D=v_ssm_D, ssm_norm=v_ssm_norm, ssm_w_out=v_ssm_w_out, ffn_w_up=v_ffn_w_up, ffn_conv_w=v_ffn_conv_w, ffn_conv_b=v_ffn_conv_b, ffn_w_down=v_ffn_w_down)

    kchip = 2 * lax.axis_index("x") + lax.axis_index("y")

    def own_slot(g, own):
        return lax.dynamic_update_slice_in_dim(g, own[None], kchip, axis=0)

    def tr(t):
        return jnp.swapaxes(t[0], 0, 1)

    later = dict(ffn0=[ffn_w_up[0].astype(MXU), ffn_w_down[0].astype(MXU)],
                 ssm=[tr(ssm_w_in).astype(MXU), ssm_w_out[0].astype(MXU)],
                 ffn1=[ffn_w_up[1].astype(MXU), ffn_w_down[1].astype(MXU)])
    sh = [tr(mix_w_in).astype(MXU), mix_w_out[0].astype(MXU), ssm_conv_w[0], ssm_conv_b, ssm_norm, ffn_conv_w]
    first = _gather_shards(sh, "gather_first")
    g_mi, g_mo, g_scw, g_scb, g_sn, g_fcw = [own_slot(g, own) for g, own in zip(first, sh)]
    started, token = _spread_start(list(later.values()), False, first[0], "gather_start", halved=(0,))
    started = dict(zip(later.keys(), started))
    fcw = [jnp.concatenate([g_fcw[p, i] for p in _PERM], axis=1) for i in range(2)]
    P = dict(
        nm=norm_mix, nf=norm_ffn, nfin=norm_final,
        wmiT=g_mi.reshape(MIX_IN_DIM, D_MODEL), wmo=g_mo.reshape(D_MODEL, D_MODEL),
        pool_w=pool_w[0], pool_scale=pool_scale, sinks=attn_sinks[0],
        scw=_cols_from_shards(g_scw), scb=g_scb.reshape(1, SSM_CONV_DIM), snorm=g_sn.reshape(1, SSM_D_INNER),
        dt_bias=jnp.pad(ssm_dt_bias, ((0, 0), (0, LANES - SSM_HEADS))), a_log=jnp.pad(ssm_A_log, ((0, 0), (0, LANES - SSM_HEADS))),
        d_exp=jnp.repeat(ssm_D, SSM_D_INNER // SSM_HEADS, axis=1),
        fcb=[jnp.concatenate([ffn_conv_b[i:i + 1, p * FFN_TC:(p + 1) * FFN_TC] for p in _PERM], axis=1) for i in range(2)],
    )

    def fetch(group, after):
        owns, lands = _spread_wait(started[group], False, after, f"gather_wait_{group}", halved=group == "ffn0")
        if group == "ffn0":
            lands = _sibling_fill(lands, "gather_fill_ffn0")
        a, b = [own_slot(g, own) for g, own in zip(lands, owns)]
        if group == "ssm":
            wsi = a.reshape(SSM_IN_DIM, D_MODEL)
            zx = SSM_D_INNER + SSM_CONV_DIM
            return dict(wsiT=wsi, wdtT=jnp.pad(wsi[zx:], ((0, LANES - SSM_HEADS), (0, 0))), wso=b.reshape(SSM_D_INNER, D_MODEL))
        i = int(group[-1])
        return dict(wup=jnp.concatenate([a[p] for p in _PERM], axis=1), wdn=b.reshape(D_FF, D_MODEL), fcw=fcw[i])

    cos, sin_s = _rope_tables(positions)
    sent = {}

    def send(group, grads):
        res, tok = _spread_start([grads], True, jnp.zeros((SUBLANES, LANES), F32), f"grad_start_{group}")
        sent[group] = res[0]
        return tok

    loss_row, grad_x, big, small = _local_step(x[0], cos, sin_s, loss_target[0], P, fetch, token, send)

    kidx = kchip.astype(jnp.int32).reshape(1)
    group_names = dict(ffn1=["ffn_w_up1", "ffn_w_down1"], ssm=["ssm_w_in", "ssm_w_out"], ffn0=["ffn_w_up0", "ffn_w_down0"],
                       mix=["mix_w_in", "mix_w_out"])
    names, mine = [], []
    for group, started_g in sent.items():
        grads, lands = _spread_wait(started_g, True, grad_x, f"grad_wait_{group}")
        for nm, g, land in zip(group_names[group], grads, lands):
            names.append(nm)
            mine.append(_chip_sum(g, land, kidx, f"chip_sum_{nm}"))
    theirs = _sibling_exchange(mine, "sibling_exchange")
    red = {nm: (a, b) for nm, a, b in zip(names, mine, theirs)}

    out = {}

    def big_update(pname, gparts, transposed=False):
        w = W[pname]
        lw = len(gparts)
        shp = w.shape
        rr, cc = gparts[0][0].shape
        fix = (lambda t: tr(t)[None]) if transposed else (lambda t: t.reshape(lw, rr, cc))
        res = _adamw(fix(w), fix(Mo[pname]), fix(Vo[pname]), gparts, f"adamw_{pname}")
        out[pname] = tuple((tr(r)[None] if transposed else r.reshape(shp)) for r in res)

    big_update("mix_w_in", [red["mix_w_in"]], transposed=True)
    big_update("mix_w_out", [red["mix_w_out"]])
    big_update("ssm_w_in", [red["ssm_w_in"]], transposed=True)
    big_update("ssm_w_out", [red["ssm_w_out"]])
    big_update("ffn_w_up", [red["ffn_w_up0"], red["ffn_w_up1"]])
    big_update("ffn_w_down", [red["ffn_w_down0"], red["ffn_w_down1"]])

    small_names = ["norm_mix", "norm_ffn", "norm_final", "pool_w", "pool_scale", "attn_sinks", "ssm_dt_bias", "ssm_A_log",
                   "ssm_D", "ffn_conv_b", "ssm_conv_w", "ssm_conv_b", "ssm_norm", "ffn_conv_w"]

    def as2d(t):
        if t.ndim == 1:
            return t.reshape(1, -1)
        return t.reshape(-1, t.shape[-1])

    wmv = [(as2d(W[nm]), as2d(Mo[nm]), as2d(Vo[nm])) for nm in small_names]
    summed = _small_allreduce([small[nm] for nm in small_names], [t[0].shape for t in wmv], loss_row, "small_allreduce")
    res = _small_adamw(summed[:-1], wmv, "small_adamw")
    for a, nm in enumerate(small_names):
        out[nm] = tuple(r.reshape(W[nm].shape) for r in res[4 * a:4 * a + 4])
    loss = summed[-1][0, 0]

    order = ["norm_mix", "norm_ffn", "norm_final", "mix_w_in", "pool_w", "pool_scale", "attn_sinks", "mix_w_out", "ssm_w_in",
             "ssm_conv_w", "ssm_conv_b", "ssm_dt_bias", "ssm_A_log", "ssm_D", "ssm_norm", "ssm_w_out", "ffn_w_up", "ffn_conv_w",
             "ffn_conv_b", "ffn_w_down"]
    return (loss, grad_x.reshape(x.shape), *[out[nm][0] for nm in order], *[out[nm][1] for nm in order],
            *[out[nm][2] for nm in order], *[out[nm][3] for nm in order])
```

```python
import functools

import jax
import jax.numpy as jnp
from jax import lax
from jax.experimental import pallas as pl
from jax.experimental.pallas import tpu as pltpu

F32 = jnp.float32
BF16 = jnp.bfloat16
MXU = BF16
HI = lax.Precision.HIGHEST

D_MODEL = 1024
POOL_WINDOWS = (2, 4, 8, 16)
POOL_DIM = 512
POOL_GROUP = 128
HEAD_DIM = 64
N_HEADS = 8
N_KV_HEADS = 2
GQ = 4
Q_DIM = 512
KV_DIM = 128
BLOCK = 128
ROPE_THETA = 10000.0
MIX_IN_DIM = 1280
SSM_D_INNER = 2048
SSM_HEADS = 32
SSM_GROUPS = 8
SSM_STATE = 128
SSM_CONV = 4
SSM_CHUNK = 128
SSM_CONV_DIM = 4096
SSM_IN_DIM = 6176
D_FF = 2816
FFN_CONV = 3
NORM_EPS = 1e-6
SSM_NORM_EPS = 1e-5
ADAM_LR = 0.001
ADAM_B1 = 0.9
ADAM_B2 = 0.999
ADAM_EPS = 1e-08
ADAM_WD = 0.01
ADAM_STEP = 10

N_CHIPS = 4
LANES = 128
SUBLANES = 8
V7X_VMEM_LIMIT = 56 * 1024 * 1024
NEG = -1e30
MESH = pl.DeviceIdType.MESH


def _cp(*sem):
    return pltpu.CompilerParams(dimension_semantics=sem if sem else None, vmem_limit_bytes=V7X_VMEM_LIMIT)


def _sds(shape, dtype=F32):
    return jax.ShapeDtypeStruct(tuple(shape), dtype)


def _iota(shape, dim):
    return lax.broadcasted_iota(jnp.int32, shape, dim)


def _silu(x):
    return x * (1.0 / (1.0 + jnp.exp(-x)))


def _dsilu(x):
    s = 1.0 / (1.0 + jnp.exp(-x))
    return s * (1.0 + x * (1.0 - s))


def _mm(a, b, *, ta=False, tb=False, tm, tn, tk, res=None, out_dtype=F32, out_shard_perm=None, out_into=None, b_rows=None,
        norm_w=None, norm_bwd=None, loss_head=None, name):
    M, K = (a.shape[1], a.shape[0]) if ta else a.shape
    N = b.shape[0] if tb else b.shape[1]
    boff = 0
    if b_rows is not None:
        boff = b_rows[0]
        if tb:
            N = b_rows[1]
        else:
            K = b_rows[1]
    tm, tn, tk = min(tm, M), min(tn, N), min(tk, K)
    gm, gn, gk = M // tm, N // tn, K // tk
    assert gm * tm == M and gn * tn == N and gk * tk == K, (name, M, N, K, tm, tn, tk)
    a_spec = pl.BlockSpec((tk, tm), lambda i, j, k: (k, i)) if ta else pl.BlockSpec((tm, tk), lambda i, j, k: (i, k))
    b_spec = pl.BlockSpec((tn, tk), lambda i, j, k: (j + boff, k)) if tb else pl.BlockSpec((tk, tn), lambda i, j, k: (k + boff, j))
    dims = (((0 if ta else 1,), (1 if tb else 0,)), ((), ()))
    has_res = res is not None
    has_nw = norm_w is not None
    has_nb = norm_bwd is not None
    has_lh = loss_head is not None
    has_tok = has_nb and norm_bwd[3] is not None
    assert not (has_nw or has_nb or has_lh) or (gn == 1 and out_shard_perm is None)
    n_extra = has_res + has_nw + (3 + has_tok if has_nb else 0) + (2 if has_lh else 0)

    def body(*refs):
        a_ref, b_ref = refs[0], refs[1]
        extra = list(refs[2:2 + n_extra])
        outs = refs[len(args):]
        r_ref = extra.pop(0) if has_res else None
        nw_ref = extra.pop(0) if has_nw else None
        nb_refs = extra if has_nb else None

        def dot():
            return lax.dot_general(a_ref[...].astype(MXU), b_ref[...].astype(MXU), dims, preferred_element_type=F32)

        def accumulate(o_ref, part):
            i = pl.program_id(0)

            @pl.when(i == 0)
            def _():
                o_ref[...] = part

            @pl.when(i > 0)
            def _():
                o_ref[...] += part

        def finish(r):
            if has_res:
                r = r + r_ref[...]
            if has_lh:
                wv = extra[0][...]
                rs = lax.rsqrt(jnp.mean(r * r, axis=-1, keepdims=True) + NORM_EPS)
                xh = r * rs
                e = xh * wv - extra[1][...]
                lpart = 0.5 * jnp.sum(jnp.mean(e * e, axis=-1, keepdims=True), axis=0, keepdims=True)
                dy = e * (1.0 / N)
                g = dy * wv
                outs[0][...] = rs * (g - xh * jnp.mean(g * xh, axis=-1, keepdims=True))
                accumulate(outs[1], jnp.sum(dy * xh, axis=0, keepdims=True))
                accumulate(outs[2], jnp.broadcast_to(lpart, (1, LANES)))
                return
            if has_nb:
                xv = nb_refs[0][...]
                rs = lax.rsqrt(jnp.mean(xv * xv, axis=-1, keepdims=True) + NORM_EPS)
                xh = xv * rs
                g = r * nb_refs[1][...]
                dr = nb_refs[2][...] + nb_refs[3][0:1, 0:1] if has_tok else nb_refs[2][...]
                outs[0][...] = dr + rs * (g - xh * jnp.mean(g * xh, axis=-1, keepdims=True))
                accumulate(outs[1], jnp.sum(r * xh, axis=0, keepdims=True))
                return
            outs[0][...] = r.astype(out_dtype)
            if has_nw:
                rs = lax.rsqrt(jnp.mean(r * r, axis=-1, keepdims=True) + NORM_EPS)
                outs[1][...] = (r * rs * nw_ref[...]).astype(outs[1].dtype)

        if gk == 1:
            finish(dot())
        else:
            acc = refs[-1]
            k = pl.program_id(2)

            @pl.when(k == 0)
            def _():
                acc[...] = dot()

            if gk > 2:
                @pl.when(jnp.logical_and(k > 0, k < gk - 1))
                def _():
                    acc[...] += dot()

            @pl.when(k == gk - 1)
            def _():
                finish(acc[...] + dot())

    tile = pl.BlockSpec((tm, tn), lambda i, j, k: (i, j))
    row = pl.BlockSpec((1, tn), lambda i, j, k: (0, j))
    in_specs = [a_spec, b_spec]
    args = [a, b]
    if has_res:
        in_specs.append(tile)
        args.append(res)
    if has_nw:
        in_specs.append(row)
        args.append(norm_w.reshape(1, N))
    if has_nb:
        in_specs += [tile, row, tile]
        args += [norm_bwd[0], norm_bwd[1].reshape(1, N), norm_bwd[2]]
        if has_tok:
            in_specs.append(pl.BlockSpec((SUBLANES, LANES), lambda i, j, k: (0, 0)))
            args.append(norm_bwd[3])
    if has_lh:
        in_specs += [row, tile]
        args += [loss_head[0].reshape(1, N), loss_head[1]]
    alias = {}
    if out_into is not None:
        buf, rows, off = out_into
        out_spec = pl.BlockSpec((tm, tn), lambda i, j, k: (i + off, j))
        out_shape = _sds((rows, N), out_dtype)
        if buf is not None:
            alias = {len(args): 0}
            in_specs.append(pl.BlockSpec(memory_space=pl.ANY))
            args.append(buf)
    elif out_shard_perm is None:
        out_spec = tile
        out_shape = _sds((M, N), out_dtype)
    else:
        assert gn == len(out_shard_perm) == 4 and tuple(out_shard_perm) == (0, 2, 1, 3)
        out_spec = pl.BlockSpec((None, tm, tn), lambda i, j, k: ((j % 2) * 2 + j // 2, i, 0))
        out_shape = _sds((gn, M, tn), out_dtype)
    sem = ("parallel", "parallel", "arbitrary")
    if has_nw:
        out_spec, out_shape = [out_spec, tile], [out_shape, _sds((M, N), MXU)]
    if has_nb:
        out_spec, out_shape = [tile, row], [_sds((M, N)), _sds((1, N))]
        sem = ("arbitrary", "arbitrary", "arbitrary")
    if has_lh:
        out_spec = [tile, row, pl.BlockSpec((1, LANES), lambda i, j, k: (0, 0))]
        out_shape = [_sds((M, N)), _sds((1, N)), _sds((1, LANES))]
        sem = ("arbitrary", "arbitrary", "arbitrary")
    return pl.pallas_call(
        body, grid=(gm, gn, gk), in_specs=in_specs, out_specs=out_spec, out_shape=out_shape,
        scratch_shapes=[pltpu.VMEM((tm, tn), F32)] if gk > 1 else [], input_output_aliases=alias,
        compiler_params=_cp(*sem), name=name)(*args)


def _put_rows(buf, src, rows, at, name):
    assert at % rows == 0 and src.shape[1] == buf.shape[1] and src.dtype == buf.dtype
    C = buf.shape[1]

    def body(s_ref, b_ref, o_ref):
        o_ref[...] = s_ref[...]

    return pl.pallas_call(
        body, grid=(1,), in_specs=[pl.BlockSpec((rows, C), lambda i: (0, 0)), pl.BlockSpec(memory_space=pl.ANY)],
        out_specs=pl.BlockSpec((rows, C), lambda i: (at // rows, 0)), out_shape=_sds(buf.shape, buf.dtype),
        input_output_aliases={1: 0}, compiler_params=_cp("arbitrary"), name=name)(src, buf)


def _rmsnorm_fwd(x, w, name, token=None):
    T, D = x.shape
    tm = min(T, 512)
    has_token = token is not None

    def body(*refs):
        x_ref, w_ref, o_ref = refs[0], refs[1], refs[-1]
        xv = x_ref[...]
        if has_token:
            xv = xv + refs[2][0:1, 0:1]
        r = lax.rsqrt(jnp.mean(xv * xv, axis=-1, keepdims=True) + NORM_EPS)
        o_ref[...] = (xv * r * w_ref[...]).astype(o_ref.dtype)

    in_specs = [pl.BlockSpec((tm, D), lambda i: (i, 0)), pl.BlockSpec((1, D), lambda i: (0, 0))]
    args = [x, w.reshape(1, D)]
    if has_token:
        in_specs.append(pl.BlockSpec((SUBLANES, LANES), lambda i: (0, 0)))
        args.append(token)
    return pl.pallas_call(
        body, grid=(T // tm,), in_specs=in_specs,
        out_specs=pl.BlockSpec((tm, D), lambda i: (i, 0)), out_shape=_sds((T, D), MXU),
        compiler_params=_cp("parallel"), name=name)(*args)


def _shift_down(cur, prev8, s):
    if s == 0:
        return cur
    tm = cur.shape[0]
    rc = pltpu.roll(cur, s, 0)
    top = jnp.where(_iota((SUBLANES, cur.shape[1]), 0) < s, pltpu.roll(prev8, s, 0), rc[:SUBLANES])
    return jnp.concatenate([top, rc[SUBLANES:]], axis=0) if tm > SUBLANES else top


def _shift_up(cur, next8, s):
    if s == 0:
        return cur
    tm = cur.shape[0]
    rc = pltpu.roll(cur, tm - s, 0)
    bot = jnp.where(_iota((SUBLANES, cur.shape[1]), 0) >= SUBLANES - s, pltpu.roll(next8, SUBLANES - s, 0), rc[tm - SUBLANES:])
    return jnp.concatenate([rc[:tm - SUBLANES], bot], axis=0) if tm > SUBLANES else bot


def _conv_rows(cur, prev8, w, b, K):
    acc = cur * w[K - 1:K, :] + b
    for s in range(1, K):
        acc = acc + _shift_down(cur, prev8, s) * w[K - 1 - s:K - s, :]
    return acc


FFN_TC = 1408


def _ffn_up_conv_gate(hf, wup, cw, cb, name):
    T, D = hf.shape
    tm = min(T, 256)
    nt, nj = T // tm, D_FF // FFN_TC
    K = FFN_CONV
    W2 = 2 * FFN_TC

    def body(a_ref, b_ref, w_ref, c_ref, hid_ref, hc_ref, act_ref, halo):
        i = pl.program_id(1)

        @pl.when(i == 0)
        def _():
            halo[...] = jnp.zeros(halo.shape, F32)

        hb = jnp.dot(a_ref[...].astype(MXU), b_ref[...].astype(MXU), preferred_element_type=F32).astype(hid_ref.dtype)
        hid_ref[...] = hb
        cur = hb.astype(F32)
        hc = _conv_rows(cur, halo[...], w_ref[...], c_ref[...], K)
        halo[...] = cur[tm - SUBLANES:]
        hc_ref[...] = hc
        act_ref[...] = (_silu(hc[:, FFN_TC:]) * hc[:, :FFN_TC]).astype(act_ref.dtype)

    blk = pl.BlockSpec((tm, W2), lambda j, i: (i, j))
    return pl.pallas_call(
        body, grid=(nj, nt),
        in_specs=[pl.BlockSpec((tm, D), lambda j, i: (i, 0)), pl.BlockSpec((D, W2), lambda j, i: (0, j)),
                  pl.BlockSpec((K, W2), lambda j, i: (0, j)), pl.BlockSpec((1, W2), lambda j, i: (0, j))],
        out_specs=[blk, blk, pl.BlockSpec((tm, FFN_TC), lambda j, i: (i, j))],
        out_shape=[_sds((T, 2 * D_FF), MXU), _sds((T, 2 * D_FF)), _sds((T, D_FF), MXU)],
        scratch_shapes=[pltpu.VMEM((SUBLANES, W2), F32)],
        compiler_params=_cp("arbitrary", "arbitrary"), name=name)(hf, wup, cw, cb)


def _ffn_down_dx_mid_bwd(dxo, wdn, hid, hc, cw, name):
    T, D = dxo.shape
    tm = min(T, 256)
    nt, nj = T // tm, D_FF // FFN_TC
    K = FFN_CONV
    W2 = 2 * FFN_TC

    def body(g_ref, wd_ref, h_ref, c_ref, w_ref, dh_ref, dw_ref, db_ref, ahead):
        i = pl.program_id(1)

        @pl.when(i == 0)
        def _():
            ahead[...] = jnp.zeros(ahead.shape, F32)

        w = w_ref[...]
        cur = h_ref[...].astype(F32)
        hcv = c_ref[...]
        dav = _nt(g_ref[...], wd_ref[...])
        u, g = hcv[:, :FFN_TC], hcv[:, FFN_TC:]
        d_cur = jnp.concatenate([dav * _silu(g), dav * u * _dsilu(g)], axis=1)
        d_nxt = ahead[...]
        ahead[...] = d_cur[:SUBLANES]
        ups = [d_cur] + [_shift_up(d_cur, d_nxt, s) for s in range(1, K)]
        dh = ups[0] * w[K - 1:K, :]
        for s in range(1, K):
            dh = dh + ups[s] * w[K - 1 - s:K - s, :]
        dh_ref[...] = dh.astype(dh_ref.dtype)
        dwp = jnp.concatenate([jnp.sum(ups[K - 1 - k] * cur, axis=0, keepdims=True) for k in range(K)], axis=0)
        dbp = jnp.sum(d_cur, axis=0, keepdims=True)

        @pl.when(i == 0)
        def _():
            dw_ref[...] = dwp
            db_ref[...] = dbp

        @pl.when(i > 0)
        def _():
            dw_ref[...] += dwp
            db_ref[...] += dbp

    blk = pl.BlockSpec((tm, W2), lambda j, i: (nt - 1 - i, j))
    return pl.pallas_call(
        body, grid=(nj, nt),
        in_specs=[pl.BlockSpec((tm, D), lambda j, i: (nt - 1 - i, 0)), pl.BlockSpec((FFN_TC, D), lambda j, i: (j, 0)), blk, blk,
                  pl.BlockSpec((K, W2), lambda j, i: (0, j))],
        out_specs=[blk, pl.BlockSpec((K, W2), lambda j, i: (0, j)), pl.BlockSpec((1, W2), lambda j, i: (0, j))],
        out_shape=[_sds((T, 2 * D_FF), MXU), _sds((K, 2 * D_FF)), _sds((1, 2 * D_FF))],
        scratch_shapes=[pltpu.VMEM((SUBLANES, W2), F32)],
        compiler_params=_cp("arbitrary", "arbitrary"), name=name)(dxo, wdn, hid, hc, cw)


def _rope(t, cos, sin_s, inverse=False):
    n = t.shape[1] // LANES
    c = jnp.concatenate([cos] * n, axis=1) if n > 1 else cos
    s = jnp.concatenate([sin_s] * n, axis=1) if n > 1 else sin_s
    a = pltpu.roll(t, HEAD_DIM // 2, 1)
    b = pltpu.roll(t, t.shape[1] - HEAD_DIM // 2, 1)
    first = (_iota(t.shape, 1) % HEAD_DIM) < HEAD_DIM // 2
    rot = jnp.where(first, b, a) * s
    return t * c - rot if inverse else t * c + rot


def _stack_heads(t, g):
    return jnp.concatenate([t[:, (GQ * g + r) * HEAD_DIM:(GQ * g + r + 1) * HEAD_DIM] for r in range(GQ)], axis=0)


def _stack_cols(t, g):
    return jnp.concatenate([t[:, GQ * g + r:GQ * g + r + 1] for r in range(GQ)], axis=0)


def _pool_sums(prev, cur, w):
    s = jnp.concatenate([prev, cur], axis=0)
    sh = 1
    while sh < w:
        s = s + pltpu.roll(s, sh, 0)
        sh *= 2
    return s[BLOCK:]


def _nt(a, b):
    return lax.dot_general(a.astype(MXU), b.astype(MXU), (((1,), (1,)), ((), ())), preferred_element_type=F32)


def _tn(a, b):
    return lax.dot_general(a.astype(MXU), b.astype(MXU), (((0,), (0,)), ((), ())), preferred_element_type=F32)


def _nn(a, b):
    return jnp.dot(a.astype(MXU), b.astype(MXU), preferred_element_type=F32)


def _mixcore_fwd(proj, cos, sin_s, pool_w, pool_scale, sinks, name):
    T = proj.shape[0]
    nb = T // BLOCK
    scale = HEAD_DIM ** -0.5

    def body(p_ref, pp_ref, c_ref, s_ref, cp_ref, sp_ref, pw_ref, ps_ref, sk_ref, cat_ref, at_ref, lse_ref):
        i = pl.program_id(0)
        has_prev = i > 0
        cur = p_ref[...]
        prv = jnp.where(has_prev, pp_ref[...], 0.0)
        tpos = (i * BLOCK + _iota((BLOCK, 1), 0) + 1).astype(F32)
        for g, w in enumerate(POOL_WINDOWS):
            sl = slice(g * POOL_GROUP, (g + 1) * POOL_GROUP)
            pooled = _pool_sums(prv[:, sl], cur[:, sl], w) / jnp.minimum(tpos, float(w)) - cur[:, sl]
            cat_ref[:, sl] = (_nn(pooled, pw_ref[g]) * ps_ref[:, sl]).astype(cat_ref.dtype)
        q = _rope(cur[:, POOL_DIM:POOL_DIM + Q_DIM], c_ref[...], s_ref[...])
        kc = _rope(cur[:, POOL_DIM + Q_DIM:POOL_DIM + Q_DIM + KV_DIM], c_ref[...], s_ref[...])
        kp = _rope(prv[:, POOL_DIM + Q_DIM:POOL_DIM + Q_DIM + KV_DIM], cp_ref[...], sp_ref[...])
        vc = cur[:, POOL_DIM + Q_DIM + KV_DIM:]
        vp = prv[:, POOL_DIM + Q_DIM + KV_DIM:]
        ri = _iota((GQ * BLOCK, BLOCK), 0) % BLOCK
        cj = _iota((GQ * BLOCK, BLOCK), 1)
        mc = cj <= ri
        mp = jnp.logical_and(cj > ri, has_prev)
        outs, lses = [], []
        for g in range(N_KV_HEADS):
            hs = slice(g * HEAD_DIM, (g + 1) * HEAD_DIM)
            qg = _stack_heads(q, g) * scale
            sc = jnp.where(mc, _nt(qg, kc[:, hs]), NEG)
            sp = jnp.where(mp, _nt(qg, kp[:, hs]), NEG)
            sink = jnp.concatenate([jnp.full((BLOCK, 1), sk_ref[GQ * g + r], F32) for r in range(GQ)], axis=0)
            m = jnp.maximum(jnp.maximum(jnp.max(sc, axis=1, keepdims=True), jnp.max(sp, axis=1, keepdims=True)), sink)
            pc = jnp.exp(sc - m)
            pp = jnp.exp(sp - m)
            den = jnp.sum(pc, axis=1, keepdims=True) + jnp.sum(pp, axis=1, keepdims=True) + jnp.exp(sink - m)
            o = (_nn(pc, vc[:, hs]) + _nn(pp, vp[:, hs])) / den
            lse = m + jnp.log(den)
            for r in range(GQ):
                outs.append(o[r * BLOCK:(r + 1) * BLOCK])
                lses.append(lse[r * BLOCK:(r + 1) * BLOCK])
        attn = jnp.concatenate(outs, axis=1)
        at_ref[...] = attn
        cat_ref[:, POOL_DIM:] = attn.astype(cat_ref.dtype)
        lane = _iota((BLOCK, LANES), 1)
        lrow = jnp.zeros((BLOCK, LANES), F32)
        for h in range(N_HEADS):
            lrow = jnp.where(lane == h, lses[h], lrow)
        lse_ref[...] = lrow

    cur = lambda w: pl.BlockSpec((BLOCK, w), lambda i: (i, 0))
    prv = lambda w: pl.BlockSpec((BLOCK, w), lambda i: (jnp.maximum(i - 1, 0), 0))
    return pl.pallas_call(
        body, grid=(nb,),
        in_specs=[cur(MIX_IN_DIM), prv(MIX_IN_DIM), cur(LANES), cur(LANES), prv(LANES), prv(LANES),
                  pl.BlockSpec((4, POOL_GROUP, POOL_GROUP), lambda i: (0, 0, 0)), pl.BlockSpec((1, POOL_DIM), lambda i: (0, 0)),
                  pl.BlockSpec(memory_space=pltpu.SMEM)],
        out_specs=[cur(2 * POOL_DIM), cur(Q_DIM), cur(LANES)],
        out_shape=[_sds((T, 2 * POOL_DIM), MXU), _sds((T, Q_DIM)), _sds((T, LANES))],
        compiler_params=_cp("parallel"), name=name)(proj, proj, cos, sin_s, cos, sin_s, pool_w, pool_scale, sinks)


def _mixcore_bwd(proj, cos, sin_s, pool_w, pool_scale, sinks, attn, lse, dcat, name):
    T = proj.shape[0]
    nb = T // BLOCK
    scale = HEAD_DIM ** -0.5
    QO, KO, VO = POOL_DIM, POOL_DIM + Q_DIM, POOL_DIM + Q_DIM + KV_DIM

    def body(p_ref, pp_ref, pn_ref, c_ref, s_ref, cp_ref, sp_ref, cn_ref, sn_ref, pw_ref, ps_ref, sk_ref,
             at_ref, atn_ref, l_ref, ln_ref, d_ref, dn_ref, dp_ref, dpw_ref, dps_ref, dsk_ref):
        i = pl.program_id(0)
        has_prev = i > 0
        has_next = i < nb - 1
        cur = p_ref[...]
        prv = jnp.where(has_prev, pp_ref[...], 0.0)
        d_cur = d_ref[...]
        d_nxt = jnp.where(has_next, dn_ref[...], 0.0)

        tpos = (i * BLOCK + _iota((BLOCK, 1), 0) + 1).astype(F32)
        tpos2 = (i * BLOCK + _iota((2 * BLOCK, 1), 0) + 1).astype(F32)
        ps = ps_ref[...]
        dps_parts, dpw_parts = [], []
        for g, w in enumerate(POOL_WINDOWS):
            sl = slice(g * POOL_GROUP, (g + 1) * POOL_GROUP)
            pooled = _pool_sums(prv[:, sl], cur[:, sl], w) / jnp.minimum(tpos, float(w)) - cur[:, sl]
            mixed = _nn(pooled, pw_ref[g])
            dps_parts.append(jnp.sum(d_cur[:, sl] * mixed, axis=0, keepdims=True))
            dm2 = jnp.concatenate([d_cur[:, sl], d_nxt[:, sl]], axis=0) * ps[:, sl]
            dpw_parts.append(_tn(pooled, dm2[:BLOCK]))
            dpool2 = _nt(dm2, pw_ref[g])
            e = dpool2 / jnp.minimum(tpos2, float(w))
            sh = 1
            while sh < w:
                e = e + pltpu.roll(e, 2 * BLOCK - sh, 0)
                sh *= 2
            dp_ref[:, sl] = (e[:BLOCK] - dpool2[:BLOCK]).astype(dp_ref.dtype)
        dpsp = jnp.concatenate(dps_parts, axis=1)

        nxt = pn_ref[...]
        q = _rope(cur[:, QO:KO], c_ref[...], s_ref[...])
        qn = _rope(nxt[:, QO:KO], cn_ref[...], sn_ref[...])
        kc = _rope(cur[:, KO:VO], c_ref[...], s_ref[...])
        kp = _rope(prv[:, KO:VO], cp_ref[...], sp_ref[...])
        vc, vp = cur[:, VO:], prv[:, VO:]
        do, don = d_cur[:, POOL_DIM:], d_nxt[:, POOL_DIM:]
        dl = do * at_ref[...]
        dln = don * atn_ref[...]
        lse, lsen = l_ref[...], ln_ref[...]
        ri = _iota((GQ * BLOCK, BLOCK), 0) % BLOCK
        cj = _iota((GQ * BLOCK, BLOCK), 1)
        mc = cj <= ri
        mp = jnp.logical_and(cj > ri, has_prev)
        mn = jnp.logical_and(cj > ri, has_next)
        dq_parts, dk_parts, dv_parts, dsk_vals = [], [], [], []
        for g in range(N_KV_HEADS):
            hs = slice(g * HEAD_DIM, (g + 1) * HEAD_DIM)
            qg, qng = _stack_heads(q, g) * scale, _stack_heads(qn, g) * scale
            dog, dong = _stack_heads(do, g), _stack_heads(don, g)
            delta = jnp.sum(_stack_heads(dl, g), axis=1, keepdims=True)
            deltan = jnp.sum(_stack_heads(dln, g), axis=1, keepdims=True)
            lg, lng = _stack_cols(lse, g), _stack_cols(lsen, g)
            pc = jnp.where(mc, jnp.exp(_nt(qg, kc[:, hs]) - lg), 0.0)
            pp = jnp.where(mp, jnp.exp(_nt(qg, kp[:, hs]) - lg), 0.0)
            pn = jnp.where(mn, jnp.exp(_nt(qng, kc[:, hs]) - lng), 0.0)
            dsc = pc * (_nt(dog, vc[:, hs]) - delta)
            dsp = pp * (_nt(dog, vp[:, hs]) - delta)
            dsn = pn * (_nt(dong, vc[:, hs]) - deltan)
            dqg = (_nn(dsc, kc[:, hs]) + _nn(dsp, kp[:, hs])) * scale
            dq_parts += [dqg[r * BLOCK:(r + 1) * BLOCK] for r in range(GQ)]
            dk_parts.append(_tn(dsc, qg) + _tn(dsn, qng))
            dv_parts.append(_tn(pc, dog) + _tn(pn, dong))
            sink = jnp.concatenate([jnp.full((BLOCK, 1), sk_ref[GQ * g + r], F32) for r in range(GQ)], axis=0)
            dsk = -jnp.exp(sink - lg) * delta
            dsk_vals += [jnp.sum(dsk[r * BLOCK:(r + 1) * BLOCK], axis=0, keepdims=True) for r in range(GQ)]
        dq = _rope(jnp.concatenate(dq_parts, axis=1), c_ref[...], s_ref[...], inverse=True)
        dk = _rope(jnp.concatenate(dk_parts, axis=1), c_ref[...], s_ref[...], inverse=True)
        dp_ref[:, QO:KO] = dq.astype(dp_ref.dtype)
        dp_ref[:, KO:VO] = dk.astype(dp_ref.dtype)
        dp_ref[:, VO:] = jnp.concatenate(dv_parts, axis=1).astype(dp_ref.dtype)
        lane = _iota((1, LANES), 1)
        dskp = jnp.zeros((1, LANES), F32)
        for h in range(N_HEADS):
            dskp = jnp.where(lane == h, dsk_vals[h], dskp)

        @pl.when(i == 0)
        def _():
            dps_ref[...] = dpsp
            dsk_ref[...] = dskp
            for g in range(4):
                dpw_ref[g] = dpw_parts[g]

        @pl.when(i > 0)
        def _():
            dps_ref[...] += dpsp
            dsk_ref[...] += dskp
            for g in range(4):
                dpw_ref[g] += dpw_parts[g]

    cur = lambda w: pl.BlockSpec((BLOCK, w), lambda i: (i, 0))
    prv = lambda w: pl.BlockSpec((BLOCK, w), lambda i: (jnp.maximum(i - 1, 0), 0))
    nxt = lambda w: pl.BlockSpec((BLOCK, w), lambda i: (jnp.minimum(i + 1, nb - 1), 0))
    return pl.pallas_call(
        body, grid=(nb,),
        in_specs=[cur(MIX_IN_DIM), prv(MIX_IN_DIM), nxt(MIX_IN_DIM),
                  cur(LANES), cur(LANES), prv(LANES), prv(LANES), nxt(LANES), nxt(LANES),
                  pl.BlockSpec((4, POOL_GROUP, POOL_GROUP), lambda i: (0, 0, 0)), pl.BlockSpec((1, POOL_DIM), lambda i: (0, 0)),
                  pl.BlockSpec(memory_space=pltpu.SMEM),
                  cur(Q_DIM), nxt(Q_DIM), cur(LANES), nxt(LANES), cur(2 * POOL_DIM), nxt(2 * POOL_DIM)],
        out_specs=[cur(MIX_IN_DIM), pl.BlockSpec((4, POOL_GROUP, POOL_GROUP), lambda i: (0, 0, 0)),
                   pl.BlockSpec((1, POOL_DIM), lambda i: (0, 0)), pl.BlockSpec((1, LANES), lambda i: (0, 0))],
        out_shape=[_sds((T, MIX_IN_DIM), MXU), _sds((4, POOL_GROUP, POOL_GROUP)), _sds((1, POOL_DIM)), _sds((1, LANES))],
        compiler_params=_cp("arbitrary"), name=name)(
            proj, proj, proj, cos, sin_s, cos, sin_s, cos, sin_s, pool_w, pool_scale, sinks, attn, attn, lse, lse, dcat, dcat)


GROUP_W = SSM_D_INNER // SSM_GROUPS


def _ssm_in_conv(h, wT, row_off, cw, cb, name):
    T, D = h.shape
    tm = min(T, 256)
    tc = 1024
    K = SSM_CONV

    def body(a_ref, b_ref, w_ref, c_ref, x_ref, pre_ref, act_ref, halo):
        @pl.when(pl.program_id(1) == 0)
        def _():
            halo[...] = jnp.zeros(halo.shape, F32)

        cur = _nt(a_ref[...], b_ref[...])
        x_ref[...] = cur
        pre = _conv_rows(cur, halo[...], w_ref[...], c_ref[...], K)
        halo[...] = cur[tm - SUBLANES:]
        pre_ref[...] = pre
        act_ref[...] = _silu(pre)

    blk = pl.BlockSpec((tm, tc), lambda j, i: (i, j))
    return pl.pallas_call(
        body, grid=(SSM_CONV_DIM // tc, T // tm),
        in_specs=[pl.BlockSpec((tm, D), lambda j, i: (i, 0)), pl.BlockSpec((tc, D), lambda j, i: (j + row_off // tc, 0)),
                  pl.BlockSpec((K, tc), lambda j, i: (0, j)), pl.BlockSpec((1, tc), lambda j, i: (0, j))],
        out_specs=[blk, blk, blk], out_shape=[_sds((T, SSM_CONV_DIM))] * 3,
        scratch_shapes=[pltpu.VMEM((SUBLANES, tc), F32)],
        compiler_params=_cp("arbitrary", "arbitrary"), name=name)(h, wT, cw, cb)


def _dot_hi(a, b):
    return jnp.dot(a, b, precision=HI, preferred_element_type=F32)


def _ssd_common(dtraw, bias, alog):
    L = SSM_CHUNK
    xb = dtraw + bias
    dt = jnp.maximum(xb, 0.0) + jnp.log1p(jnp.exp(-jnp.abs(xb)))
    A = -jnp.exp(alog)
    tril = (_iota((L, L), 1) <= _iota((L, L), 0)).astype(F32)
    acs = _dot_hi(tril, dt * A)
    return xb, dt, A, tril, acs


def _head_selectors():
    es = (_iota((LANES, SSM_D_INNER), 0) == _iota((LANES, SSM_D_INNER), 1) // HEAD_DIM).astype(BF16)
    est = (_iota((SSM_D_INNER, LANES), 1) == _iota((SSM_D_INNER, LANES), 0) // HEAD_DIM).astype(BF16)
    return es, est


def _dot_sel(v, sel):
    hi = v.astype(BF16)
    r1 = v - hi.astype(F32)
    mid = r1.astype(BF16)
    lo = (r1 - mid.astype(F32)).astype(BF16)
    d = lambda a: jnp.dot(a, sel, preferred_element_type=F32)
    return (d(hi) + d(mid)) + d(lo)


def _expand_heads(v, es):
    return _dot_sel(v, es)


def _reduce_heads(q, est):
    return _dot_sel(q, est)


def _per_state_row(v, g):
    return jnp.concatenate([jnp.broadcast_to(v[:, GQ * g + r:GQ * g + r + 1], (HEAD_DIM, 1)) for r in range(GQ)], axis=0)


def _ssd_fwd(xact, dtraw, dt_bias, a_log, z, d_skip, nw, name):
    T = xact.shape[0]
    nc = T // SSM_CHUNK
    L = SSM_CHUNK
    BO, CO = SSM_D_INNER, SSM_D_INNER + SSM_GROUPS * SSM_STATE

    def body(x_ref, dt_ref, bias_ref, al_ref, es_ref, z_ref, dsk_ref, nw_ref, y_ref, st_ref, yn_ref, state):
        @pl.when(pl.program_id(0) == 0)
        def _():
            state[...] = jnp.zeros(state.shape, F32)

        _, dt, A, tril, acs = _ssd_common(dt_ref[...], bias_ref[...], al_ref[...])
        acsT = acs.T
        last = acs[L - 1:L, :]
        cd = jnp.exp(last)
        es = es_ref[...]
        dtX = _expand_heads(dt, es)
        EX = _expand_heads(jnp.exp(acs), es)
        decX = _expand_heads(jnp.exp(last - acs), es)
        for g in range(SSM_GROUPS):
            gs = slice(g * GROUP_W, (g + 1) * GROUP_W)
            B = x_ref[:, BO + g * SSM_STATE:BO + (g + 1) * SSM_STATE]
            C = x_ref[:, CO + g * SSM_STATE:CO + (g + 1) * SSM_STATE]
            X = x_ref[:, gs] * dtX[:, gs]
            CB = _nt(C, B)
            yd = []
            for r in range(GQ):
                h = GQ * g + r
                Lm = jnp.exp(jnp.where(tril > 0, acs[:, h:h + 1] - acsT[h:h + 1, :], NEG))
                yd.append(_nn(CB * Lm, X[:, r * HEAD_DIM:(r + 1) * HEAD_DIM]))
            S = state[g]
            st_ref[g] = S
            y_ref[:, gs] = jnp.concatenate(yd, axis=1) + _nt(C, S) * EX[:, gs]
            state[g] = S * _per_state_row(cd, g) + _tn(X * decX[:, gs], B)
        y2 = (y_ref[...] + dsk_ref[...] * x_ref[:, :SSM_D_INNER]) * _silu(z_ref[...])
        r = lax.rsqrt(jnp.mean(y2 * y2, axis=-1, keepdims=True) + SSM_NORM_EPS)
        yn_ref[...] = (y2 * r * nw_ref[...]).astype(yn_ref.dtype)

    es, _ = _head_selectors()
    row = pl.BlockSpec((L, SSM_D_INNER), lambda c: (c, 0))
    vec = pl.BlockSpec((1, SSM_D_INNER), lambda c: (0, 0))
    return pl.pallas_call(
        body, grid=(nc,),
        in_specs=[pl.BlockSpec((L, SSM_CONV_DIM), lambda c: (c, 0)), pl.BlockSpec((L, LANES), lambda c: (c, 0)),
                  pl.BlockSpec((1, LANES), lambda c: (0, 0)), pl.BlockSpec((1, LANES), lambda c: (0, 0)),
                  pl.BlockSpec((LANES, SSM_D_INNER), lambda c: (0, 0)), row, vec, vec],
        out_specs=[row, pl.BlockSpec((None, SSM_GROUPS, GROUP_W, SSM_STATE), lambda c: (c, 0, 0, 0)), row],
        out_shape=[_sds((T, SSM_D_INNER)), _sds((nc, SSM_GROUPS, GROUP_W, SSM_STATE)), _sds((T, SSM_D_INNER), MXU)],
        scratch_shapes=[pltpu.VMEM((SSM_GROUPS, GROUP_W, SSM_STATE), F32)],
        compiler_params=_cp("arbitrary"), name=name)(xact, dtraw, dt_bias, a_log, es, z, d_skip, nw)


def _ssd_bwd(xact, xbc, xpre, cw, dtraw, dt_bias, a_log, d_skip, states, dyn, y, z, nw, name):
    T = xact.shape[0]
    nc = T // SSM_CHUNK
    L = SSM_CHUNK
    K = SSM_CONV
    BO, CO = SSM_D_INNER, SSM_D_INNER + SSM_GROUPS * SSM_STATE

    def body(x_ref, xin_ref, pre_ref, cw_ref, dt_ref, bias_ref, al_ref, dsk_ref, es_ref, est_ref, st_ref, dn_ref, y_ref,
             z_ref, nw_ref, dzx_ref, dcw_ref, dcb_ref, ddt_ref, dbias_ref, dal_ref, dd_ref, dnw_ref,
             dstate, qa, qx, dxp_ref, ahead, dyv):
        cc = pl.program_id(0)

        @pl.when(cc == 0)
        def _():
            dstate[...] = jnp.zeros(dstate.shape, F32)
            ahead[...] = jnp.zeros(ahead.shape, F32)

        zv = z_ref[...]
        sz = _silu(zv)
        yg = y_ref[...] + dsk_ref[...] * x_ref[:, :SSM_D_INNER]
        y2 = yg * sz
        rn = lax.rsqrt(jnp.mean(y2 * y2, axis=-1, keepdims=True) + SSM_NORM_EPS)
        y2h = y2 * rn
        dn = dn_ref[...]
        gy = dn * nw_ref[...]
        dy2 = rn * (gy - y2h * jnp.mean(gy * y2h, axis=-1, keepdims=True))
        dyv[...] = dy2 * sz
        dzx_ref[:, :SSM_D_INNER] = (dy2 * yg * _dsilu(zv)).astype(dzx_ref.dtype)
        dnwp = jnp.sum(dn * y2h, axis=0, keepdims=True)

        xb, dt, A, tril, acs = _ssd_common(dt_ref[...], bias_ref[...], al_ref[...])
        acsT = acs.T
        last = acs[L - 1:L, :]
        cd = jnp.exp(last)
        es, est = es_ref[...], est_ref[...]
        dtX = _expand_heads(dt, es)
        EX = _expand_heads(jnp.exp(acs), es)
        decX = _expand_heads(jnp.exp(last - acs), es)
        lane1 = _iota((1, LANES), 1)
        lane = _iota((L, LANES), 1)
        sub = _iota((L, LANES), 0)
        ztot = jnp.zeros((1, LANES), F32)
        wrow = jnp.zeros((L, LANES), F32)
        wcolT = jnp.zeros((LANES, L), F32)
        rows_dec, rows_dd = [], []
        for g in range(SSM_GROUPS):
            gs = slice(g * GROUP_W, (g + 1) * GROUP_W)
            x = x_ref[:, gs]
            B = x_ref[:, BO + g * SSM_STATE:BO + (g + 1) * SSM_STATE]
            C = x_ref[:, CO + g * SSM_STATE:CO + (g + 1) * SSM_STATE]
            dY = dyv[:, gs]
            dtx, e_x, dec_x = dtX[:, gs], EX[:, gs], decX[:, gs]
            X = x * dtx
            CB = _nt(C, B)
            S = st_ref[g]
            dS_out = dstate[g]
            dcb_sum = jnp.zeros((L, L), F32)
            dxd = []
            for r in range(GQ):
                h = GQ * g + r
                hs = slice(r * HEAD_DIM, (r + 1) * HEAD_DIM)
                Lm = jnp.exp(jnp.where(tril > 0, acs[:, h:h + 1] - acsT[h:h + 1, :], NEG))
                M = CB * Lm
                dM = _nt(dY[:, hs], X[:, hs])
                dxd.append(_tn(M, dY[:, hs]))
                dcb_sum = dcb_sum + dM * Lm
                Wm = dM * M
                wrow = jnp.where(lane == h, jnp.sum(Wm, axis=1, keepdims=True), wrow)
                wcolT = jnp.where(sub == h, jnp.sum(Wm, axis=0, keepdims=True), wcolT)
            dXd = jnp.concatenate(dxd, axis=1)
            G = _nt(C, S)
            dG = dY * e_x
            dDX = _nt(B, dS_out)
            dX = dXd + dec_x * dDX
            t_dec = dDX * X * dec_x
            qa[:, gs] = dG * G - t_dec
            qx[:, gs] = dX * x
            rows_dec.append(jnp.sum(t_dec, axis=0, keepdims=True))
            rows_dd.append(jnp.sum(dY * x, axis=0, keepdims=True))
            zc = jnp.sum(dS_out * S, axis=1, keepdims=True)
            for r in range(GQ):
                ztot = jnp.where(lane1 == GQ * g + r, jnp.sum(zc[r * HEAD_DIM:(r + 1) * HEAD_DIM], axis=0, keepdims=True), ztot)
            dxp_ref[:, gs] = dX * dtx + dY * dsk_ref[:, gs]
            dxp_ref[:, BO + g * SSM_STATE:BO + (g + 1) * SSM_STATE] = _tn(dcb_sum, C) + _nn(X * dec_x, dS_out)
            dxp_ref[:, CO + g * SSM_STATE:CO + (g + 1) * SSM_STATE] = _nn(dcb_sum, B) + _nn(dG, S)
            dstate[g] = dS_out * _per_state_row(cd, g) + _tn(dG, C)
        rows = jnp.concatenate([jnp.concatenate(rows_dec, axis=1), jnp.concatenate(rows_dd, axis=1)]
                               + [jnp.zeros((SUBLANES - 2, SSM_D_INNER), F32)], axis=0)
        rsum = _reduce_heads(rows, est)
        dlast = rsum[0:1, :] + cd * ztot
        dacs = (wrow - wcolT.T) + _reduce_heads(qa[...], est) + jnp.where(sub == L - 1, dlast, 0.0)
        triu = (_iota((L, L), 0) <= _iota((L, L), 1)).astype(F32)
        da = _dot_hi(triu, dacs)
        ddtraw = (da * A + _reduce_heads(qx[...], est)) * (1.0 / (1.0 + jnp.exp(-xb)))
        ddt_ref[...] = ddtraw
        dal = jnp.sum(da * dt, axis=0, keepdims=True) * A
        ddp = rsum[1:2, :]
        dbp = jnp.sum(ddtraw, axis=0, keepdims=True)
        w = cw_ref[...]
        d_cur = dxp_ref[...] * _dsilu(pre_ref[...])
        d_nxt = ahead[...]
        ahead[...] = d_cur[:SUBLANES]
        ups = [d_cur] + [_shift_up(d_cur, d_nxt, s) for s in range(1, K)]
        dxc = ups[0] * w[K - 1:K, :]
        for s in range(1, K):
            dxc = dxc + ups[s] * w[K - 1 - s:K - s, :]
        dzx_ref[:, SSM_D_INNER:] = dxc.astype(dzx_ref.dtype)
        xin = xin_ref[...]
        dcwp = jnp.concatenate([jnp.sum(ups[K - 1 - k] * xin, axis=0, keepdims=True) for k in range(K)], axis=0)
        dcbp = jnp.sum(d_cur, axis=0, keepdims=True)

        @pl.when(cc == 0)
        def _():
            dbias_ref[...] = dbp
            dal_ref[...] = dal
            dd_ref[...] = ddp
            dcw_ref[...] = dcwp
            dcb_ref[...] = dcbp
            dnw_ref[...] = dnwp

        @pl.when(cc > 0)
        def _():
            dbias_ref[...] += dbp
            dal_ref[...] += dal
            dd_ref[...] += ddp
            dcw_ref[...] += dcwp
            dcb_ref[...] += dcbp
            dnw_ref[...] += dnwp

    rc = lambda c: nc - 1 - c
    vec = pl.BlockSpec((1, LANES), lambda c: (0, 0))
    wide = pl.BlockSpec((L, SSM_CONV_DIM), lambda c: (rc(c), 0))
    inner = pl.BlockSpec((L, SSM_D_INNER), lambda c: (rc(c), 0))
    es, est = _head_selectors()
    return pl.pallas_call(
        body, grid=(nc,),
        in_specs=[wide, wide, wide, pl.BlockSpec((K, SSM_CONV_DIM), lambda c: (0, 0)),
                  pl.BlockSpec((L, LANES), lambda c: (rc(c), 0)), vec, vec,
                  pl.BlockSpec((1, SSM_D_INNER), lambda c: (0, 0)),
                  pl.BlockSpec((LANES, SSM_D_INNER), lambda c: (0, 0)), pl.BlockSpec((SSM_D_INNER, LANES), lambda c: (0, 0)),
                  pl.BlockSpec((None, SSM_GROUPS, GROUP_W, SSM_STATE), lambda c: (rc(c), 0, 0, 0)),
                  inner, inner, inner, pl.BlockSpec((1, SSM_D_INNER), lambda c: (0, 0))],
        out_specs=[pl.BlockSpec((L, SSM_D_INNER + SSM_CONV_DIM), lambda c: (rc(c), 0)),
                   pl.BlockSpec((K, SSM_CONV_DIM), lambda c: (0, 0)), pl.BlockSpec((1, SSM_CONV_DIM), lambda c: (0, 0)),
                   pl.BlockSpec((L, LANES), lambda c: (rc(c), 0)), vec, vec, vec,
                   pl.BlockSpec((1, SSM_D_INNER), lambda c: (0, 0))],
        out_shape=[_sds((T, SSM_D_INNER + SSM_CONV_DIM), MXU), _sds((K, SSM_CONV_DIM)), _sds((1, SSM_CONV_DIM)),
                   _sds((T, LANES)), _sds((1, LANES)), _sds((1, LANES)), _sds((1, LANES)), _sds((1, SSM_D_INNER))],
        scratch_shapes=[pltpu.VMEM((SSM_GROUPS, GROUP_W, SSM_STATE), F32), pltpu.VMEM((L, SSM_D_INNER), F32),
                        pltpu.VMEM((L, SSM_D_INNER), F32), pltpu.VMEM((L, SSM_CONV_DIM), F32),
                        pltpu.VMEM((SUBLANES, SSM_CONV_DIM), F32), pltpu.VMEM((L, SSM_D_INNER), F32)],
        compiler_params=_cp("arbitrary"), name=name)(xact, xbc, xpre, cw, dtraw, dt_bias, a_log, d_skip, es, est, states, dyn, y,
                                                     z, nw)


def _local_step(x0, cos, sin_s, target, P, fetch, token, send):
    mmf = functools.partial(_mm, tm=1024)
    big, small = {}, {}
    P = dict(P, wup={}, wdn={}, fcw={})
    h0 = _rmsnorm_fwd(x0, P["nm"][0], "norm_mix0", token=token)
    proj0 = mmf(h0, P["wmiT"], tb=True, tn=1280, tk=1024, name="mix_in")
    cat, attn, lse = _mixcore_fwd(proj0, cos, sin_s, P["pool_w"], P["pool_scale"], P["sinks"], "mixcore_fwd")
    x1, hf0 = mmf(cat, P["wmo"], tn=1024, tk=1024, res=x0, norm_w=P["nf"][0], name="mix_out")

    def ffn_fwd(xin, hf, i, **epilogue):
        got = fetch(f"ffn{i}", hf)
        P["wup"][i], P["wdn"][i], P["fcw"][i] = got["wup"], got["wdn"], got["fcw"]
        hid, hc, act = _ffn_up_conv_gate(hf, P["wup"][i], P["fcw"][i], P["fcb"][i], f"ffn_up{i}")
        xout = mmf(act, P["wdn"][i], tn=1024, tk=D_FF, res=xin, name=f"ffn_down{i}", **epilogue)
        return (hid, hc), act, xout

    hid0, act0, (x2, h1) = ffn_fwd(x1, hf0, 0, norm_w=P["nm"][1])
    P.update(fetch("ssm", h1))
    z = mmf(h1, P["wsiT"], tb=True, tn=1024, tk=1024, b_rows=(0, SSM_D_INNER), name="ssm_in_z")
    xbc, xpre, xact = _ssm_in_conv(h1, P["wsiT"], SSM_D_INNER, P["scw"], P["scb"], "ssm_in_xbc")
    dtraw = mmf(h1, P["wdtT"], tb=True, tn=128, tk=1024, name="ssm_in_dt")
    y, states, yn = _ssd_fwd(xact, dtraw, P["dt_bias"], P["a_log"], z, P["d_exp"], P["snorm"], "ssd_fwd")
    x3, hf1 = mmf(yn, P["wso"], tn=1024, tk=SSM_D_INNER, res=x2, norm_w=P["nf"][1], name="ssm_out")
    hid1, act1, (dx4, d_nfin, loss_row) = ffn_fwd(x3, hf1, 1, loss_head=(P["nfin"], target))
    small["norm_final"] = d_nfin

    def ffn_bwd(xin, dxo, hf, hid, act, i):
        big[f"ffn_w_down{i}"] = dwf(act, dxo, tm=1408, tn=1024, name=f"ffn_down_dw{i}").reshape(N_CHIPS, D_FF // N_CHIPS, D_MODEL)
        dhid, dcw, dcb = _ffn_down_dx_mid_bwd(dxo, P["wdn"][i], hid[0], hid[1], P["fcw"][i], f"ffn_down_dx{i}")
        big[f"ffn_w_up{i}"] = dwf(hf, dhid, tm=1024, tn=1408, out_shard_perm=(0, 2, 1, 3), name=f"ffn_up_dw{i}")
        tok = send(f"ffn{i}", [big[f"ffn_w_up{i}"], big[f"ffn_w_down{i}"]])
        dxi, dnf = _mm(dhid, P["wup"][i], tb=True, tm=512, tn=1024, tk=5632, norm_bwd=(xin, P["nf"][i], dxo, tok), name=f"ffn_up_dx{i}")
        return dxi, dnf, dcw, dcb

    dwf = functools.partial(_mm, ta=True, tk=2048, out_dtype=BF16)
    dx3, dnf1, dfcw1, dfcb1 = ffn_bwd(x3, dx4, hf1, hid1, act1, 1)
    dyn = mmf(dx3, P["wso"], tb=True, tn=1024, tk=1024, name="ssm_out_dx")
    big["ssm_w_out"] = dwf(yn, dx3, tm=1024, tn=1024, name="ssm_out_dw").reshape(N_CHIPS, SSM_D_INNER // N_CHIPS, D_MODEL)
    dzx, d_scw, d_scb, ddtraw, d_dtb, d_alog, d_dskip, d_snorm = _ssd_bwd(
        xact, xbc, xpre, P["scw"], dtraw, P["dt_bias"], P["a_log"], P["d_exp"], states, dyn, y, z, P["snorm"], "ssd_bwd")
    zx = SSM_D_INNER + SSM_CONV_DIM
    dwsi = dwf(dzx, h1, tm=1024, tn=1024, out_into=(None, SSM_IN_DIM, 0), name="ssm_in_dw_zx")
    dwdt = dwf(ddtraw, h1, tm=128, tn=1024, name="ssm_in_dw_dt")
    dwsi = _put_rows(dwsi, dwdt, SSM_HEADS, SSM_D_INNER + SSM_CONV_DIM, "ssm_in_dw_put_dt")
    big["ssm_w_in"] = dwsi.reshape(N_CHIPS, SSM_IN_DIM // N_CHIPS, D_MODEL)
    tok = send("ssm", [big["ssm_w_in"], big["ssm_w_out"]])
    dh1 = mmf(dzx, P["wsiT"], tn=1024, tk=2048, b_rows=(0, zx), name="ssm_in_dx_zx")
    dx2, dnm1 = mmf(ddtraw, P["wdtT"], tn=1024, tk=128, res=dh1, norm_bwd=(x2, P["nm"][1], dx3, tok), name="ssm_in_dx_dt")
    dx1, dnf0, dfcw0, dfcb0 = ffn_bwd(x1, dx2, hf0, hid0, act0, 0)
    dcat = mmf(dx1, P["wmo"], tb=True, tn=1024, tk=1024, name="mix_out_dx")
    big["mix_w_out"] = dwf(cat, dx1, tm=1024, tn=1024, name="mix_out_dw").reshape(N_CHIPS, D_MODEL // N_CHIPS, D_MODEL)
    dproj0, d_pw, d_ps, d_sk = _mixcore_bwd(proj0, cos, sin_s, P["pool_w"], P["pool_scale"], P["sinks"], attn, lse, dcat, "mixcore_bwd")
    big["mix_w_in"] = dwf(dproj0, h0, tm=1280, tn=1024, name="mix_in_dw").reshape(N_CHIPS, MIX_IN_DIM // N_CHIPS, D_MODEL)
    tok = send("mix", [big["mix_w_in"], big["mix_w_out"]])
    dx0, dnm0 = mmf(dproj0, P["wmiT"], tn=1024, tk=1280, norm_bwd=(x0, P["nm"][0], dx1, tok), name="mix_in_dx")

    def unperm_cols(a):
        r = a.shape[0]
        t = a.reshape(r, N_CHIPS, FFN_TC)
        return jnp.stack([t[:, p] for p in _PERM], axis=0)

    small["norm_mix"] = jnp.concatenate([dnm0, dnm1], axis=0)
    small["norm_ffn"] = jnp.concatenate([dnf0, dnf1], axis=0)
    small["pool_w"] = d_pw.reshape(4 * POOL_GROUP, POOL_GROUP)
    small["pool_scale"] = d_ps
    small["attn_sinks"] = d_sk
    small["ssm_dt_bias"] = d_dtb
    small["ssm_A_log"] = d_alog
    small["ssm_D"] = d_dskip
    fcb = jnp.stack([unperm_cols(dfcb0), unperm_cols(dfcb1)], axis=0)
    small["ffn_conv_b"] = fcb.reshape(2, 2 * D_FF)
    small["ssm_conv_w"] = d_scw.reshape(SSM_CONV, N_CHIPS, SSM_CONV_DIM // N_CHIPS).transpose(1, 0, 2)
    small["ssm_conv_b"] = d_scb.reshape(N_CHIPS, 1, SSM_CONV_DIM // N_CHIPS)
    small["ssm_norm"] = d_snorm.reshape(N_CHIPS, 1, SSM_D_INNER // N_CHIPS)
    small["ffn_conv_w"] = jnp.concatenate([unperm_cols(dfcw0), unperm_cols(dfcw1)], axis=1)
    return loss_row, dx0, big, small


ANY = pl.BlockSpec(memory_space=pl.ANY)


def _place():
    return lax.axis_index("x"), lax.axis_index("y"), lax.axis_index("c")


def _gather_shards(shards, name):
    n = len(shards)
    split = [s.size >= (1 << 16) for s in shards]

    def half(ref, a, h):
        shp = shards[a].shape
        if len(shp) == 3:
            return ref.at[h]
        r2 = shp[0] // 2
        return ref.at[pl.ds(pl.multiple_of(h * r2, 2 * SUBLANES), r2), :]

    def body(*refs):
        ins, outs = refs[:n], refs[n:2 * n]
        send, recv, fsend, frecv = refs[2 * n:]
        x, y, c = _place()
        k = 2 * x + y
        chips = [(1 - x, y), (x, 1 - y), (1 - x, 1 - y)]

        def ici(a, j, src_slot_ref, dst_slot):
            px, py = chips[j]
            src = half(src_slot_ref, a, c) if split[a] else src_slot_ref
            dst = half(outs[a].at[dst_slot], a, c) if split[a] else outs[a].at[dst_slot]
            return pltpu.make_async_remote_copy(src, dst, send.at[a, j], recv.at[a, j], device_id=(px, py, c), device_id_type=MESH)

        def d2d(a, j, h):
            px, py = chips[j]
            part = half(outs[a].at[2 * px + py], a, h)
            return pltpu.make_async_remote_copy(part, part, fsend.at[a, j], frecv.at[a, j], device_id=(x, y, 1 - c), device_id_type=MESH)

        sends = [ici(a, j, ins[a], k) for a in range(n) for j in range(3)]
        for cp in sends:
            cp.start()
        passed = []
        for a in range(n):
            for j, (px, py) in enumerate(chips):
                ici(a, j, ins[a], 2 * px + py).wait_recv()
                if split[a]:
                    passed.append(d2d(a, j, c))
                    passed[-1].start()
        for a in range(n):
            if split[a]:
                for j in range(3):
                    d2d(a, j, 1 - c).wait_recv()
        for cp in sends + passed:
            cp.wait_send()

    return pl.pallas_call(
        body, in_specs=[ANY] * n, out_specs=[ANY] * n,
        out_shape=[_sds((N_CHIPS,) + s.shape, s.dtype) for s in shards],
        scratch_shapes=[pltpu.SemaphoreType.DMA((n, 3))] * 4,
        compiler_params=pltpu.CompilerParams(has_side_effects=True), name=name)(*shards)


HBM = pl.BlockSpec(memory_space=pltpu.HBM)
SEM = pl.BlockSpec(memory_space=pltpu.SEMAPHORE)
DATAFLOW = pltpu.SideEffectType.DATAFLOW_SIDE_EFFECTING


def _row_half(ref, h):
    r2 = ref.shape[0] // 2
    return ref.at[pl.ds(pl.multiple_of(h * r2, 2 * SUBLANES), r2), :]


def _spread_start(groups, slot_src, after, name, halved=()):
    flat = [a for grp in groups for a in grp]
    n = len(flat)
    ng = len(groups)
    offs = [sum(len(g) for g in groups[:i]) for i in range(ng)]
    lshape = [(a.shape if slot_src else (N_CHIPS,) + a.shape) for a in flat]

    nsem = 6 * n

    def body(*refs):
        src, land = refs[:n], refs[n:2 * n]
        sems = refs[2 * n + 1:2 * n + 1 + nsem]
        token = refs[-1]
        x, y, c = _place()
        k = 2 * x + y
        chips = [(1 - x, y), (x, 1 - y), (1 - x, 1 - y)]
        for a in range(n):
            half = any(offs[gi] <= a < offs[gi] + len(groups[gi]) for gi in halved)
            for j, (px, py) in enumerate(chips):
                s = src[a].at[2 * px + py] if slot_src else src[a]
                d = land[a].at[k]
                if half:
                    s, d = _row_half(s, c), _row_half(d, c)
                pltpu.make_async_remote_copy(s, d, sems[6 * a + 2 * j], sems[6 * a + 2 * j + 1],
                                             device_id=(px, py, c), device_id_type=MESH).start()
        token[...] = jnp.zeros(token.shape, token.dtype)

    out_shape = [pltpu.SemaphoreType.DMA(())] * nsem
    out_shape += [pltpu.HBM(a.shape, a.dtype) for a in flat] + [pltpu.HBM(s, a.dtype) for s, a in zip(lshape, flat)]
    out_shape.append(_sds((SUBLANES, LANES)))
    args = [pltpu.with_memory_space_constraint(a, pltpu.HBM) for a in flat]
    args += [pltpu.with_memory_space_constraint(lax.empty(s, a.dtype), pltpu.HBM) for s, a in zip(lshape, flat)]
    res = pl.pallas_call(
        body, name=name, out_shape=tuple(out_shape), in_specs=[HBM] * (2 * n) + [pl.BlockSpec(memory_space=pl.ANY)],
        out_specs=tuple([SEM] * nsem + [HBM] * (2 * n) + [pl.BlockSpec(memory_space=pltpu.VMEM)]),
        input_output_aliases={i: nsem + i for i in range(2 * n)},
        compiler_params=pltpu.CompilerParams(has_side_effects=DATAFLOW))(*args, after)
    sems, thru, token = res[:nsem], res[nsem:nsem + 2 * n], res[-1]
    out = []
    for gi, grp in enumerate(groups):
        sl = slice(offs[gi], offs[gi] + len(grp))
        out.append((list(sems[6 * offs[gi]:6 * (offs[gi] + len(grp))]), list(thru[:n][sl]), list(thru[n:][sl])))
    return out, token


def _spread_wait(started, slot_src, after, name, halved=False):
    sems, srcs, lands = started
    n = len(srcs)

    def body(*refs):
        src, land = refs[:n], refs[n:2 * n]
        sem = refs[2 * n:2 * n + 6 * n]
        x, y, c = _place()
        chips = [(1 - x, y), (x, 1 - y), (1 - x, 1 - y)]
        for a in range(n):
            for j, (px, py) in enumerate(chips):
                s = src[a].at[2 * px + py] if slot_src else src[a]
                d = land[a].at[2 * px + py]
                if halved:
                    s, d = _row_half(s, c), _row_half(d, c)
                cp = pltpu.make_async_remote_copy(s, d, sem[6 * a + 2 * j], sem[6 * a + 2 * j + 1],
                                                  device_id=(px, py, c), device_id_type=MESH)
                cp.wait_send()
                cp.wait_recv()

    res = pl.pallas_call(
        body, name=name, out_shape=tuple([pltpu.HBM(a.shape, a.dtype) for a in srcs] + [pltpu.HBM(a.shape, a.dtype) for a in lands]),
        in_specs=[HBM] * (2 * n) + [SEM] * (6 * n) + [pl.BlockSpec(memory_space=pl.ANY)], out_specs=tuple([HBM] * (2 * n)),
        input_output_aliases={i: i for i in range(2 * n)},
        compiler_params=pltpu.CompilerParams(has_side_effects=DATAFLOW))(*srcs, *lands, *sems, after)
    return list(res[:n]), list(res[n:])


def _sibling_fill(lands, name):
    n = len(lands)

    def body(*refs):
        bufs = refs[n:2 * n]
        send, recv = refs[2 * n:]
        x, y, c = _place()
        chips = [(1 - x, y), (x, 1 - y), (1 - x, 1 - y)]

        def copy(a, j, h):
            px, py = chips[j]
            part = _row_half(bufs[a].at[2 * px + py], h)
            return pltpu.make_async_remote_copy(part, part, send.at[a, j], recv.at[a, j], device_id=(x, y, 1 - c), device_id_type=MESH)

        sends = [copy(a, j, c) for a in range(n) for j in range(3)]
        for cp in sends:
            cp.start()
        for a in range(n):
            for j in range(3):
                copy(a, j, 1 - c).wait_recv()
        for cp in sends:
            cp.wait_send()

    return pl.pallas_call(
        body, in_specs=[ANY] * n, out_specs=[ANY] * n, out_shape=[_sds(t.shape, t.dtype) for t in lands],
        input_output_aliases={i: i for i in range(n)},
        scratch_shapes=[pltpu.SemaphoreType.DMA((n, 3)), pltpu.SemaphoreType.DMA((n, 3))],
        compiler_params=pltpu.CompilerParams(has_side_effects=True), name=name)(*lands)


def _sibling_exchange(fs, name):
    n = len(fs)

    def body(*refs):
        ins, outs = refs[:n], refs[n:2 * n]
        send, recv = refs[2 * n:]
        x, y, c = _place()
        cps = [pltpu.make_async_remote_copy(ins[a], outs[a], send.at[a], recv.at[a],
                                            device_id=(x, y, 1 - c), device_id_type=MESH) for a in range(n)]
        for cp in cps:
            cp.start()
        for cp in cps:
            cp.wait()

    return pl.pallas_call(
        body, in_specs=[ANY] * n, out_specs=[ANY] * n, out_shape=[_sds(f.shape, f.dtype) for f in fs],
        scratch_shapes=[pltpu.SemaphoreType.DMA((n,)), pltpu.SemaphoreType.DMA((n,))],
        compiler_params=pltpu.CompilerParams(has_side_effects=True), name=name)(*fs)


def _tile2d(rows, cols, budget=2 * 1024 * 1024, step=2 * SUBLANES):
    fits = [t for t in range(step, rows + 1, step) if rows % t == 0 and t * cols * 4 <= budget]
    if fits:
        return fits[-1], cols
    fits = [t for t in range(LANES, cols + 1, LANES) if cols % t == 0 and rows * t * 4 <= budget]
    assert fits, (rows, cols)
    return rows, fits[-1]


def _chip_sum(own, parts, kidx, name):
    _, R, C = parts.shape
    tr, tc = _tile2d(R, C)

    def body(k_ref, o_ref_in, p1_ref, p2_ref, p3_ref, o_ref):
        tot = ((o_ref_in[...].astype(F32) + p1_ref[...].astype(F32)) + p2_ref[...].astype(F32)) + p3_ref[...].astype(F32)
        o_ref[...] = tot.astype(o_ref.dtype)

    def slot(d):
        return pl.BlockSpec((None, tr, tc), lambda i, j, k: ((k[0] + d) % N_CHIPS, i, j))

    return pl.pallas_call(
        body,
        grid_spec=pltpu.PrefetchScalarGridSpec(
            num_scalar_prefetch=1, grid=(R // tr, C // tc), in_specs=[slot(0), slot(1), slot(2), slot(3)],
            out_specs=pl.BlockSpec((tr, tc), lambda i, j, k: (i, j))),
        out_shape=_sds((R, C), BF16), compiler_params=_cp("parallel", "parallel"), name=name)(kidx, own, parts, parts, parts)


def _adamw_math(w, g, m, v):
    m2 = ADAM_B1 * m + (1.0 - ADAM_B1) * g
    v2 = ADAM_B2 * v + (1.0 - ADAM_B2) * (g * g)
    m_hat = m2 / (1.0 - ADAM_B1 ** ADAM_STEP)
    v_hat = v2 / (1.0 - ADAM_B2 ** ADAM_STEP)
    delta = -ADAM_LR * (m_hat / (jnp.sqrt(v_hat) + ADAM_EPS) + ADAM_WD * w)
    return delta, m2, v2


def _adamw(w, m, v, gparts, name):
    Lw, R, C = w.shape
    tr, tc = _tile2d(R, C)
    flat = [h for pair in gparts for h in pair]

    def body(*refs):
        w_ref, m_ref, v_ref = refs[:3]
        g_refs = refs[3:3 + 2 * Lw]
        go_ref, d_ref, mo_ref, vo_ref = refs[3 + 2 * Lw:]
        g = g_refs[0][...].astype(F32) + g_refs[1][...].astype(F32)
        for l in range(1, Lw):
            g = jnp.where(pl.program_id(0) == l, g_refs[2 * l][...].astype(F32) + g_refs[2 * l + 1][...].astype(F32), g)
        d, m2, v2 = _adamw_math(w_ref[...], g, m_ref[...], v_ref[...])
        go_ref[...] = g
        d_ref[...] = d
        mo_ref[...] = m2
        vo_ref[...] = v2

    blk = pl.BlockSpec((None, tr, tc), lambda l, i, j: (l, i, j))
    gblk = pl.BlockSpec((tr, tc), lambda l, i, j: (i, j))
    return pl.pallas_call(
        body, grid=(Lw, R // tr, C // tc), in_specs=[blk, blk, blk] + [gblk] * (2 * Lw), out_specs=[blk] * 4,
        out_shape=[_sds((Lw, R, C))] * 4, compiler_params=_cp("parallel", "parallel", "parallel"), name=name)(w, m, v, *flat)


def _small_adamw(grads, wmv, name):
    n = len(grads)

    def body(*refs):
        g_in, p_in, outs = refs[:n], refs[n:4 * n], refs[4 * n:]
        for a in range(n):
            g = g_in[a][...]
            d_, m2, v2 = _adamw_math(p_in[3 * a][...], g, p_in[3 * a + 1][...], p_in[3 * a + 2][...])
            outs[4 * a][...] = g
            outs[4 * a + 1][...] = d_
            outs[4 * a + 2][...] = m2
            outs[4 * a + 3][...] = v2

    vm = pl.BlockSpec(memory_space=pltpu.VMEM)
    args = list(grads) + [t for tri in wmv for t in tri]
    out_shape = [_sds(g.shape) for g in grads for _ in range(4)]
    return pl.pallas_call(body, in_specs=[vm] * len(args), out_specs=[vm] * len(out_shape), out_shape=out_shape,
                          compiler_params=pltpu.CompilerParams(vmem_limit_bytes=V7X_VMEM_LIMIT), name=name)(*args)


def _small_allreduce(partials, pshapes, loss_row, name):
    n = len(partials)
    gshapes = [p.shape for p in partials] + [loss_row.shape]
    ng = n + 1

    def body(*refs):
        g_in = refs[:ng]
        outs = refs[ng:2 * ng]
        sib = refs[2 * ng:3 * ng]
        pair = refs[3 * ng:4 * ng]
        bufs = refs[4 * ng:5 * ng]
        send1, recv1, send2, recv2 = refs[-4:]
        x, y, c = _place()
        k = 2 * x + y
        chips = [(1 - x, y), (x, 1 - y), (1 - x, 1 - y)]
        swaps = [pltpu.make_async_remote_copy(g_in[a], sib[a], send1.at[a], recv1.at[a],
                                              device_id=(x, y, 1 - c), device_id_type=MESH) for a in range(ng)]
        for cp in swaps:
            cp.start()
        for a, cp in enumerate(swaps):
            cp.wait()
            pair[a][...] = g_in[a][...] + sib[a][...]
            bufs[a][k] = pair[a][...]
        sends = [pltpu.make_async_remote_copy(pair[a], bufs[a].at[k], send2.at[a, j], recv2.at[a, j],
                                              device_id=(px, py, c), device_id_type=MESH)
                 for a in range(ng) for j, (px, py) in enumerate(chips)]
        for cp in sends:
            cp.start()
        for a in range(ng):
            for j, (px, py) in enumerate(chips):
                pltpu.make_async_remote_copy(pair[a], bufs[a].at[2 * px + py], send2.at[a, j], recv2.at[a, j],
                                             device_id=(px, py, c), device_id_type=MESH).wait_recv()
        for cp in sends:
            cp.wait_send()
        for a in range(ng):
            sharded = len(gshapes[a]) == 3

            def part(d):
                return bufs[a][d, k] if sharded else bufs[a][d]

            tot = part(0)
            for d in range(1, N_CHIPS):
                tot = tot + part(d)
            if a == n:
                outs[n][...] = tot
            else:
                pr, pc = pshapes[a]
                outs[a][...] = tot[:pr, :pc]

    vm = pl.BlockSpec(memory_space=pltpu.VMEM)
    args = list(partials) + [loss_row]
    out_shape = [_sds(ps) for ps in pshapes] + [_sds(loss_row.shape)]
    return pl.pallas_call(
        body, in_specs=[vm] * len(args), out_specs=[vm] * len(out_shape), out_shape=out_shape,
        scratch_shapes=[pltpu.VMEM(tuple(s), F32) for s in gshapes] * 2 + [pltpu.VMEM((N_CHIPS,) + tuple(s), F32) for s in gshapes]
        + [pltpu.SemaphoreType.DMA((ng,)), pltpu.SemaphoreType.DMA((ng,)),
           pltpu.SemaphoreType.DMA((ng, 3)), pltpu.SemaphoreType.DMA((ng, 3))],
        compiler_params=pltpu.CompilerParams(has_side_effects=True, vmem_limit_bytes=V7X_VMEM_LIMIT), name=name)(*args)


_PERM = (0, 2, 1, 3)


def _cols_from_shards(g):
    return g.transpose(1, 0, 2).reshape(g.shape[1], N_CHIPS * g.shape[2])


def _rope_tables(positions):
    inv_freq = ROPE_THETA ** (-jnp.arange(0, HEAD_DIM, 2, dtype=F32) / HEAD_DIM)
    ang = positions.astype(F32).reshape(-1, 1) * inv_freq
    cos, sin = jnp.cos(ang), jnp.sin(ang)
    cos = jnp.concatenate([cos, cos, cos, cos], axis=-1)
    sin_s = jnp.concatenate([-sin, sin, -sin, sin], axis=-1)
    return cos, sin_s


def kernel(x, positions, norm_mix, norm_ffn, norm_final, mix_w_in, pool_w, pool_scale, attn_sinks, mix_w_out, ssm_w_in, ssm_conv_w, ssm_conv_b, ssm_dt_bias, ssm_A_log, ssm_D, ssm_norm, ssm_w_out, ffn_w_up, ffn_conv_w, ffn_conv_b, ffn_w_down, loss_target, m_norm_mix, m_norm_ffn, m_norm_final, m_mix_w_in, m_pool_w, m_pool_scale, m_attn_sinks, m_mix_w_out, m_ssm_w_in, m_ssm_conv_w, m_ssm_conv_b, m_ssm_dt_bias, m_ssm_A_log, m_ssm_D, m_ssm_norm, m_ssm_w_out, m_ffn_w_up, m_ffn_conv_w, m_ffn_conv_b, m_ffn_w_down, v_norm_mix, v_norm_ffn, v_norm_final, v_mix_w_in, v_pool_w, v_pool_scale, v_attn_sinks, v_mix_w_out, v_ssm_w_in, v_ssm_conv_w, v_ssm_conv_b, v_ssm_dt_bias, v_ssm_A_log, v_ssm_D, v_ssm_norm, v_ssm_w_out, v_ffn_w_up, v_ffn_conv_w, v_ffn_conv_b, v_ffn_w_down):
    W = dict(norm_mix=norm_mix, norm_ffn=norm_ffn, norm_final=norm_final, mix_w_in=mix_w_in, pool_w=pool_w, pool_scale=pool_scale, attn_sinks=attn_sinks, mix_w_out=mix_w_out, ssm_w_in=ssm_w_in, ssm_conv_w=ssm_conv_w, ssm_conv_b=ssm_conv_b, ssm_dt_bias=ssm_dt_bias, ssm_A_log=ssm_A_log, ssm_D=ssm_D, ssm_norm=ssm_norm, ssm_w_out=ssm_w_out, ffn_w_up=ffn_w_up, ffn_conv_w=ffn_conv_w, ffn_conv_b=ffn_conv_b, ffn_w_down=ffn_w_down)
    Mo = dict(norm_mix=m_norm_mix, norm_ffn=m_norm_ffn, norm_final=m_norm_final, mix_w_in=m_mix_w_in, pool_w=m_pool_w, pool_scale=m_pool_scale, attn_sinks=m_attn_sinks, mix_w_out=m_mix_w_out, ssm_w_in=m_ssm_w_in, ssm_conv_w=m_ssm_conv_w, ssm_conv_b=m_ssm_conv_b, ssm_dt_bias=m_ssm_dt_bias, ssm_A_log=m_ssm_A_log, ssm_D=m_ssm_D, ssm_norm=m_ssm_norm, ssm_w_out=m_ssm_w_out, ffn_w_up=m_ffn_w_up, ffn_conv_w=m_ffn_conv_w, ffn_conv_b=m_ffn_conv_b, ffn_w_down=m_ffn_w_down)
    Vo = dict(norm_mix=v_norm_mix, norm_ffn=v_norm_ffn, norm_final=v_norm_final, mix_w_in=v_mix_w_in, pool_w=v_pool_w, pool_scale=v_pool_scale, attn_sinks=v_attn_sinks, mix_w_out=v_mix_w_out, ssm_w_in=v_ssm_w_in, ssm_conv_w=v_ssm_conv_w, ssm_conv_b=v_ssm_conv_b, ssm_dt_bias=v_ssm_dt_bias, ssm_A_log=v_ssm_A_log, ssm_D=v_ssm_D, ssm_norm=v_ssm_norm, ssm_w_out=v_ssm_w_out, ffn_w_up=v_ffn_w_up, ffn_conv_w=v_ffn_conv_w, ffn_conv_b=v_ffn_conv_b, ffn_w_down=v_ffn_w_down)

    kchip = 2 * lax.axis_index("x") + lax.axis_index("y")

    def own_slot(g, own):
        return lax.dynamic_update_slice_in_dim(g, own[None], kchip, axis=0)

    def tr(t):
        return jnp.swapaxes(t[0], 0, 1)

    later = dict(ffn0=[ffn_w_up[0].astype(MXU), ffn_w_down[0].astype(MXU)],
                 ssm=[tr(ssm_w_in).astype(MXU), ssm_w_out[0].astype(MXU)],
                 ffn1=[ffn_w_up[1].astype(MXU), ffn_w_down[1].astype(MXU)])
    sh = [tr(mix_w_in).astype(MXU), mix_w_out[0].astype(MXU), ssm_conv_w[0], ssm_conv_b, ssm_norm, ffn_conv_w]
    first = _gather_shards(sh, "gather_first")
    g_mi, g_mo, g_scw, g_scb, g_sn, g_fcw = [own_slot(g, own) for g, own in zip(first, sh)]
    started, token = _spread_start(list(later.values()), False, first[0], "gather_start", halved=(0,))
    started = dict(zip(later.keys(), started))
    fcw = [jnp.concatenate([g_fcw[p, i] for p in _PERM], axis=1) for i in range(2)]
    P = dict(
        nm=norm_mix, nf=norm_ffn, nfin=norm_final,
        wmiT=g_mi.reshape(MIX_IN_DIM, D_MODEL), wmo=g_mo.reshape(D_MODEL, D_MODEL),
        pool_w=pool_w[0], pool_scale=pool_scale, sinks=attn_sinks[0],
        scw=_cols_from_shards(g_scw), scb=g_scb.reshape(1, SSM_CONV_DIM), snorm=g_sn.reshape(1, SSM_D_INNER),
        dt_bias=jnp.pad(ssm_dt_bias, ((0, 0), (0, LANES - SSM_HEADS))), a_log=jnp.pad(ssm_A_log, ((0, 0), (0, LANES - SSM_HEADS))),
        d_exp=jnp.repeat(ssm_D, SSM_D_INNER // SSM_HEADS, axis=1),
        fcb=[jnp.concatenate([ffn_conv_b[i:i + 1, p * FFN_TC:(p + 1) * FFN_TC] for p in _PERM], axis=1) for i in range(2)],
    )

    def fetch(group, after):
        owns, lands = _spread_wait(started[group], False, after, f"gather_wait_{group}", halved=group == "ffn0")
        if group == "ffn0":
            lands = _sibling_fill(lands, "gather_fill_ffn0")
        a, b = [own_slot(g, own) for g, own in zip(lands, owns)]
        if group == "ssm":
            wsi = a.reshape(SSM_IN_DIM, D_MODEL)
            zx = SSM_D_INNER + SSM_CONV_DIM
            return dict(wsiT=wsi, wdtT=jnp.pad(wsi[zx:], ((0, LANES - SSM_HEADS), (0, 0))), wso=b.reshape(SSM_D_INNER, D_MODEL))
        i = int(group[-1])
        return dict(wup=jnp.concatenate([a[p] for p in _PERM], axis=1), wdn=b.reshape(D_FF, D_MODEL), fcw=fcw[i])

    cos, sin_s = _rope_tables(positions)
    sent = {}

    def send(group, grads):
        res, tok = _spread_start([grads], True, jnp.zeros((SUBLANES, LANES), F32), f"grad_start_{group}")
        sent[group] = res[0]
        return tok

    loss_row, grad_x, big, small = _local_step(x[0], cos, sin_s, loss_target[0], P, fetch, token, send)

    kidx = kchip.astype(jnp.int32).reshape(1)
    group_names = dict(ffn1=["ffn_w_up1", "ffn_w_down1"], ssm=["ssm_w_in", "ssm_w_out"], ffn0=["ffn_w_up0", "ffn_w_down0"],
                       mix=["mix_w_in", "mix_w_out"])
    names, mine = [], []
    for group, started_g in sent.items():
        grads, lands = _spread_wait(started_g, True, grad_x, f"grad_wait_{group}")
        for nm, g, land in zip(group_names[group], grads, lands):
            names.append(nm)
            mine.append(_chip_sum(g, land, kidx, f"chip_sum_{nm}"))
    theirs = _sibling_exchange(mine, "sibling_exchange")
    red = {nm: (a, b) for nm, a, b in zip(names, mine, theirs)}

    out = {}

    def big_update(pname, gparts, transposed=False):
        w = W[pname]
        lw = len(gparts)
        shp = w.shape
        rr, cc = gparts[0][0].shape
        fix = (lambda t: tr(t)[None]) if transposed else (lambda t: t.reshape(lw, rr, cc))
        res = _adamw(fix(w), fix(Mo[pname]), fix(Vo[pname]), gparts, f"adamw_{pname}")
        out[pname] = tuple((tr(r)[None] if transposed else r.reshape(shp)) for r in res)

    big_update("mix_w_in", [red["mix_w_in"]], transposed=True)
    big_update("mix_w_out", [red["mix_w_out"]])
    big_update("ssm_w_in", [red["ssm_w_in"]], transposed=True)
    big_update("ssm_w_out", [red["ssm_w_out"]])
    big_update("ffn_w_up", [red["ffn_w_up0"], red["ffn_w_up1"]])
    big_update("ffn_w_down", [red["ffn_w_down0"], red["ffn_w_down1"]])

    small_names = ["norm_mix", "norm_ffn", "norm_final", "pool_w", "pool_scale", "attn_sinks", "ssm_dt_bias", "ssm_A_log",
                   "ssm_D", "ffn_conv_b", "ssm_conv_w", "ssm_conv_b", "ssm_norm", "ffn_conv_w"]

    def as2d(t):
        if t.ndim == 1:
            return t.reshape(1, -1)
        return t.reshape(-1, t.shape[-1])

    wmv = [(as2d(W[nm]), as2d(Mo[nm]), as2d(Vo[nm])) for nm in small_names]
    summed = _small_allreduce([small[nm] for nm in small_names], [t[0].shape for t in wmv], loss_row, "small_allreduce")
    res = _small_adamw(summed[:-1], wmv, "small_adamw")
    for a, nm in enumerate(small_names):
        out[nm] = tuple(r.reshape(W[nm].shape) for r in res[4 * a:4 * a + 4])
    loss = summed[-1][0, 0]

    order = ["norm_mix", "norm_ffn", "norm_final", "mix_w_in", "pool_w", "pool_scale", "attn_sinks", "mix_w_out", "ssm_w_in",
             "ssm_conv_w", "ssm_conv_b", "ssm_dt_bias", "ssm_A_log", "ssm_D", "ssm_norm", "ssm_w_out", "ffn_w_up", "ffn_conv_w",
             "ffn_conv_b", "ffn_w_down"]
    return (loss, grad_x.reshape(x.shape), *[out[nm][0] for nm in order], *[out[nm][1] for nm in order],
            *[out[nm][2] for nm in order], *[out[nm][3] for nm in order])
```

```python
import functools

import jax
import jax.numpy as jnp
from jax import lax
from jax.experimental import pallas as pl
from jax.experimental.pallas import tpu as pltpu

F32 = jnp.float32
BF16 = jnp.bfloat16
MXU = BF16
HI = lax.Precision.HIGHEST

D_MODEL = 1024
POOL_WINDOWS = (2, 4, 8, 16)
POOL_DIM = 512
POOL_GROUP = 128
HEAD_DIM = 64
N_HEADS = 8
N_KV_HEADS = 2
GQ = 4
Q_DIM = 512
KV_DIM = 128
BLOCK = 128
ROPE_THETA = 10000.0
MIX_IN_DIM = 1280
SSM_D_INNER = 2048
SSM_HEADS = 32
SSM_GROUPS = 8
SSM_STATE = 128
SSM_CONV = 4
SSM_CHUNK = 128
SSM_CONV_DIM = 4096
SSM_IN_DIM = 6176
D_FF = 2816
FFN_CONV = 3
NORM_EPS = 1e-6
SSM_NORM_EPS = 1e-5
ADAM_LR = 0.001
ADAM_B1 = 0.9
ADAM_B2 = 0.999
ADAM_EPS = 1e-08
ADAM_WD = 0.01
ADAM_STEP = 10

N_CHIPS = 4
LANES = 128
SUBLANES = 8
V7X_VMEM_LIMIT = 56 * 1024 * 1024
NEG = -1e30
MESH = pl.DeviceIdType.MESH


def _cp(*sem):
    return pltpu.CompilerParams(dimension_semantics=sem if sem else None, vmem_limit_bytes=V7X_VMEM_LIMIT)


def _sds(shape, dtype=F32):
    return jax.ShapeDtypeStruct(tuple(shape), dtype)


def _iota(shape, dim):
    return lax.broadcasted_iota(jnp.int32, shape, dim)


def _silu(x):
    return x * (1.0 / (1.0 + jnp.exp(-x)))


def _dsilu(x):
    s = 1.0 / (1.0 + jnp.exp(-x))
    return s * (1.0 + x * (1.0 - s))


def _mm(a, b, *, ta=False, tb=False, tm, tn, tk, res=None, out_dtype=F32, out_shard_perm=None, out_into=None, b_rows=None,
        norm_w=None, norm_bwd=None, loss_head=None, name):
    M, K = (a.shape[1], a.shape[0]) if ta else a.shape
    N = b.shape[0] if tb else b.shape[1]
    boff = 0
    if b_rows is not None:
        boff = b_rows[0]
        if tb:
            N = b_rows[1]
        else:
            K = b_rows[1]
    tm, tn, tk = min(tm, M), min(tn, N), min(tk, K)
    gm, gn, gk = M // tm, N // tn, K // tk
    assert gm * tm == M and gn * tn == N and gk * tk == K, (name, M, N, K, tm, tn, tk)
    a_spec = pl.BlockSpec((tk, tm), lambda i, j, k: (k, i)) if ta else pl.BlockSpec((tm, tk), lambda i, j, k: (i, k))
    b_spec = pl.BlockSpec((tn, tk), lambda i, j, k: (j + boff, k)) if tb else pl.BlockSpec((tk, tn), lambda i, j, k: (k + boff, j))
    dims = (((0 if ta else 1,), (1 if tb else 0,)), ((), ()))
    has_res = res is not None
    has_nw = norm_w is not None
    has_nb = norm_bwd is not None
    has_lh = loss_head is not None
    has_tok = has_nb and norm_bwd[3] is not None
    assert not (has_nw or has_nb or has_lh) or (gn == 1 and out_shard_perm is None)
    n_extra = has_res + has_nw + (3 + has_tok if has_nb else 0) + (2 if has_lh else 0)

    def body(*refs):
        a_ref, b_ref = refs[0], refs[1]
        extra = list(refs[2:2 + n_extra])
        outs = refs[len(args):]
        r_ref = extra.pop(0) if has_res else None
        nw_ref = extra.pop(0) if has_nw else None
        nb_refs = extra if has_nb else None

        def dot():
            return lax.dot_general(a_ref[...].astype(MXU), b_ref[...].astype(MXU), dims, preferred_element_type=F32)

        def accumulate(o_ref, part):
            i = pl.program_id(0)

            @pl.when(i == 0)
            def _():
                o_ref[...] = part

            @pl.when(i > 0)
            def _():
                o_ref[...] += part

        def finish(r):
            if has_res:
                r = r + r_ref[...]
            if has_lh:
                wv = extra[0][...]
                rs = lax.rsqrt(jnp.mean(r * r, axis=-1, keepdims=True) + NORM_EPS)
                xh = r * rs
                e = xh * wv - extra[1][...]
                lpart = 0.5 * jnp.sum(jnp.mean(e * e, axis=-1, keepdims=True), axis=0, keepdims=True)
                dy = e * (1.0 / N)
                g = dy * wv
                outs[0][...] = rs * (g - xh * jnp.mean(g * xh, axis=-1, keepdims=True))
                accumulate(outs[1], jnp.sum(dy * xh, axis=0, keepdims=True))
                accumulate(outs[2], jnp.broadcast_to(lpart, (1, LANES)))
                return
            if has_nb:
                xv = nb_refs[0][...]
                rs = lax.rsqrt(jnp.mean(xv * xv, axis=-1, keepdims=True) + NORM_EPS)
                xh = xv * rs
                g = r * nb_refs[1][...]
                dr = nb_refs[2][...] + nb_refs[3][0:1, 0:1] if has_tok else nb_refs[2][...]
                outs[0][...] = dr + rs * (g - xh * jnp.mean(g * xh, axis=-1, keepdims=True))
                accumulate(outs[1], jnp.sum(r * xh, axis=0, keepdims=True))
                return
            outs[0][...] = r.astype(out_dtype)
            if has_nw:
                rs = lax.rsqrt(jnp.mean(r * r, axis=-1, keepdims=True) + NORM_EPS)
                outs[1][...] = (r * rs * nw_ref[...]).astype(outs[1].dtype)

        if gk == 1:
            finish(dot())
        else:
            acc = refs[-1]
            k = pl.program_id(2)

            @pl.when(k == 0)
            def _():
                acc[...] = dot()

            if gk > 2:
                @pl.when(jnp.logical_and(k > 0, k < gk - 1))
                def _():
                    acc[...] += dot()

            @pl.when(k == gk - 1)
            def _():
                finish(acc[...] + dot())

    tile = pl.BlockSpec((tm, tn), lambda i, j, k: (i, j))
    row = pl.BlockSpec((1, tn), lambda i, j, k: (0, j))
    in_specs = [a_spec, b_spec]
    args = [a, b]
    if has_res:
        in_specs.append(tile)
        args.append(res)
    if has_nw:
        in_specs.append(row)
        args.append(norm_w.reshape(1, N))
    if has_nb:
        in_specs += [tile, row, tile]
        args += [norm_bwd[0], norm_bwd[1].reshape(1, N), norm_bwd[2]]
        if has_tok:
            in_specs.append(pl.BlockSpec((SUBLANES, LANES), lambda i, j, k: (0, 0)))
            args.append(norm_bwd[3])
    if has_lh:
        in_specs += [row, tile]
        args += [loss_head[0].reshape(1, N), loss_head[1]]
    alias = {}
    if out_into is not None:
        buf, rows, off = out_into
        out_spec = pl.BlockSpec((tm, tn), lambda i, j, k: (i + off, j))
        out_shape = _sds((rows, N), out_dtype)
        if buf is not None:
            alias = {len(args): 0}
            in_specs.append(pl.BlockSpec(memory_space=pl.ANY))
            args.append(buf)
    elif out_shard_perm is None:
        out_spec = tile
        out_shape = _sds((M, N), out_dtype)
    else:
        assert gn == len(out_shard_perm) == 4 and tuple(out_shard_perm) == (0, 2, 1, 3)
        out_spec = pl.BlockSpec((None, tm, tn), lambda i, j, k: ((j % 2) * 2 + j // 2, i, 0))
        out_shape = _sds((gn, M, tn), out_dtype)
    sem = ("parallel", "parallel", "arbitrary")
    if has_nw:
        out_spec, out_shape = [out_spec, tile], [out_shape, _sds((M, N), MXU)]
    if has_nb:
        out_spec, out_shape = [tile, row], [_sds((M, N)), _sds((1, N))]
        sem = ("arbitrary", "arbitrary", "arbitrary")
    if has_lh:
        out_spec = [tile, row, pl.BlockSpec((1, LANES), lambda i, j, k: (0, 0))]
        out_shape = [_sds((M, N)), _sds((1, N)), _sds((1, LANES))]
        sem = ("arbitrary", "arbitrary", "arbitrary")
    return pl.pallas_call(
        body, grid=(gm, gn, gk), in_specs=in_specs, out_specs=out_spec, out_shape=out_shape,
        scratch_shapes=[pltpu.VMEM((tm, tn), F32)] if gk > 1 else [], input_output_aliases=alias,
        compiler_params=_cp(*sem), name=name)(*args)


def _put_rows(buf, src, rows, at, name):
    assert at % rows == 0 and src.shape[1] == buf.shape[1] and src.dtype == buf.dtype
    C = buf.shape[1]

    def body(s_ref, b_ref, o_ref):
        o_ref[...] = s_ref[...]

    return pl.pallas_call(
        body, grid=(1,), in_specs=[pl.BlockSpec((rows, C), lambda i: (0, 0)), pl.BlockSpec(memory_space=pl.ANY)],
        out_specs=pl.BlockSpec((rows, C), lambda i: (at // rows, 0)), out_shape=_sds(buf.shape, buf.dtype),
        input_output_aliases={1: 0}, compiler_params=_cp("arbitrary"), name=name)(src, buf)


def _rmsnorm_fwd(x, w, name, token=None):
    T, D = x.shape
    tm = min(T, 512)
    has_token = token is not None

    def body(*refs):
        x_ref, w_ref, o_ref = refs[0], refs[1], refs[-1]
        xv = x_ref[...]
        if has_token:
            xv = xv + refs[2][0:1, 0:1]
        r = lax.rsqrt(jnp.mean(xv * xv, axis=-1, keepdims=True) + NORM_EPS)
        o_ref[...] = (xv * r * w_ref[...]).astype(o_ref.dtype)

    in_specs = [pl.BlockSpec((tm, D), lambda i: (i, 0)), pl.BlockSpec((1, D), lambda i: (0, 0))]
    args = [x, w.reshape(1, D)]
    if has_token:
        in_specs.append(pl.BlockSpec((SUBLANES, LANES), lambda i: (0, 0)))
        args.append(token)
    return pl.pallas_call(
        body, grid=(T // tm,), in_specs=in_specs,
        out_specs=pl.BlockSpec((tm, D), lambda i: (i, 0)), out_shape=_sds((T, D), MXU),
        compiler_params=_cp("parallel"), name=name)(*args)


def _shift_down(cur, prev8, s):
    if s == 0:
        return cur
    tm = cur.shape[0]
    rc = pltpu.roll(cur, s, 0)
    top = jnp.where(_iota((SUBLANES, cur.shape[1]), 0) < s, pltpu.roll(prev8, s, 0), rc[:SUBLANES])
    return jnp.concatenate([top, rc[SUBLANES:]], axis=0) if tm > SUBLANES else top


def _shift_up(cur, next8, s):
    if s == 0:
        return cur
    tm = cur.shape[0]
    rc = pltpu.roll(cur, tm - s, 0)
    bot = jnp.where(_iota((SUBLANES, cur.shape[1]), 0) >= SUBLANES - s, pltpu.roll(next8, SUBLANES - s, 0), rc[tm - SUBLANES:])
    return jnp.concatenate([rc[:tm - SUBLANES], bot], axis=0) if tm > SUBLANES else bot


def _conv_rows(cur, prev8, w, b, K):
    acc = cur * w[K - 1:K, :] + b
    for s in range(1, K):
        acc = acc + _shift_down(cur, prev8, s) * w[K - 1 - s:K - s, :]
    return acc


FFN_TC = 1408


def _ffn_up_conv_gate(hf, wup, cw, cb, name):
    T, D = hf.shape
    tm = min(T, 256)
    nt, nj = T // tm, D_FF // FFN_TC
    K = FFN_CONV
    W2 = 2 * FFN_TC

    def body(a_ref, b_ref, w_ref, c_ref, hid_ref, hc_ref, act_ref, halo):
        i = pl.program_id(1)

        @pl.when(i == 0)
        def _():
            halo[...] = jnp.zeros(halo.shape, F32)

        hb = jnp.dot(a_ref[...].astype(MXU), b_ref[...].astype(MXU), preferred_element_type=F32).astype(hid_ref.dtype)
        hid_ref[...] = hb
        cur = hb.astype(F32)
        hc = _conv_rows(cur, halo[...], w_ref[...], c_ref[...], K)
        halo[...] = cur[tm - SUBLANES:]
        hc_ref[...] = hc
        act_ref[...] = (_silu(hc[:, FFN_TC:]) * hc[:, :FFN_TC]).astype(act_ref.dtype)

    blk = pl.BlockSpec((tm, W2), lambda j, i: (i, j))
    return pl.pallas_call(
        body, grid=(nj, nt),
        in_specs=[pl.BlockSpec((tm, D), lambda j, i: (i, 0)), pl.BlockSpec((D, W2), lambda j, i: (0, j)),
                  pl.BlockSpec((K, W2), lambda j, i: (0, j)), pl.BlockSpec((1, W2), lambda j, i: (0, j))],
        out_specs=[blk, blk, pl.BlockSpec((tm, FFN_TC), lambda j, i: (i, j))],
        out_shape=[_sds((T, 2 * D_FF), MXU), _sds((T, 2 * D_FF)), _sds((T, D_FF), MXU)],
        scratch_shapes=[pltpu.VMEM((SUBLANES, W2), F32)],
        compiler_params=_cp("arbitrary", "arbitrary"), name=name)(hf, wup, cw, cb)


def _ffn_down_dx_mid_bwd(dxo, wdn, hid, hc, cw, name):
    T, D = dxo.shape
    tm = min(T, 512)
    nt, nj = T // tm, D_FF // FFN_TC
    K = FFN_CONV
    W2 = 2 * FFN_TC

    def body(g_ref, wd_ref, h_ref, c_ref, w_ref, dh_ref, dw_ref, db_ref, ahead):
        i = pl.program_id(1)

        @pl.when(i == 0)
        def _():
            ahead[...] = jnp.zeros(ahead.shape, F32)

        w = w_ref[...]
        cur = h_ref[...].astype(F32)
        hcv = c_ref[...]
        dav = _nt(g_ref[...], wd_ref[...])
        u, g = hcv[:, :FFN_TC], hcv[:, FFN_TC:]
        d_cur = jnp.concatenate([dav * _silu(g), dav * u * _dsilu(g)], axis=1)
        d_nxt = ahead[...]
        ahead[...] = d_cur[:SUBLANES]
        ups = [d_cur] + [_shift_up(d_cur, d_nxt, s) for s in range(1, K)]
        dh = ups[0] * w[K - 1:K, :]
        for s in range(1, K):
            dh = dh + ups[s] * w[K - 1 - s:K - s, :]
        dh_ref[...] = dh.astype(dh_ref.dtype)
        dwp = jnp.concatenate([jnp.sum(ups[K - 1 - k] * cur, axis=0, keepdims=True) for k in range(K)], axis=0)
        dbp = jnp.sum(d_cur, axis=0, keepdims=True)

        @pl.when(i == 0)
        def _():
            dw_ref[...] = dwp
            db_ref[...] = dbp

        @pl.when(i > 0)
        def _():
            dw_ref[...] += dwp
            db_ref[...] += dbp

    blk = pl.BlockSpec((tm, W2), lambda j, i: (nt - 1 - i, j))
    return pl.pallas_call(
        body, grid=(nj, nt),
        in_specs=[pl.BlockSpec((tm, D), lambda j, i: (nt - 1 - i, 0)), pl.BlockSpec((FFN_TC, D), lambda j, i: (j, 0)), blk, blk,
                  pl.BlockSpec((K, W2), lambda j, i: (0, j))],
        out_specs=[blk, pl.BlockSpec((K, W2), lambda j, i: (0, j)), pl.BlockSpec((1, W2), lambda j, i: (0, j))],
        out_shape=[_sds((T, 2 * D_FF), MXU), _sds((K, 2 * D_FF)), _sds((1, 2 * D_FF))],
        scratch_shapes=[pltpu.VMEM((SUBLANES, W2), F32)],
        compiler_params=_cp("arbitrary", "arbitrary"), name=name)(dxo, wdn, hid, hc, cw)


def _rope(t, cos, sin_s, inverse=False):
    n = t.shape[1] // LANES
    c = jnp.concatenate([cos] * n, axis=1) if n > 1 else cos
    s = jnp.concatenate([sin_s] * n, axis=1) if n > 1 else sin_s
    a = pltpu.roll(t, HEAD_DIM // 2, 1)
    b = pltpu.roll(t, t.shape[1] - HEAD_DIM // 2, 1)
    first = (_iota(t.shape, 1) % HEAD_DIM) < HEAD_DIM // 2
    rot = jnp.where(first, b, a) * s
    return t * c - rot if inverse else t * c + rot


def _stack_heads(t, g):
    return jnp.concatenate([t[:, (GQ * g + r) * HEAD_DIM:(GQ * g + r + 1) * HEAD_DIM] for r in range(GQ)], axis=0)


def _stack_cols(t, g):
    return jnp.concatenate([t[:, GQ * g + r:GQ * g + r + 1] for r in range(GQ)], axis=0)


def _pool_sums(prev, cur, w):
    s = jnp.concatenate([prev, cur], axis=0)
    sh = 1
    while sh < w:
        s = s + pltpu.roll(s, sh, 0)
        sh *= 2
    return s[BLOCK:]


def _nt(a, b):
    return lax.dot_general(a.astype(MXU), b.astype(MXU), (((1,), (1,)), ((), ())), preferred_element_type=F32)


def _tn(a, b):
    return lax.dot_general(a.astype(MXU), b.astype(MXU), (((0,), (0,)), ((), ())), preferred_element_type=F32)


def _nn(a, b):
    return jnp.dot(a.astype(MXU), b.astype(MXU), preferred_element_type=F32)


def _mixcore_fwd(proj, cos, sin_s, pool_w, pool_scale, sinks, name):
    T = proj.shape[0]
    nb = T // BLOCK
    scale = HEAD_DIM ** -0.5

    def body(p_ref, pp_ref, c_ref, s_ref, cp_ref, sp_ref, pw_ref, ps_ref, sk_ref, cat_ref, at_ref, lse_ref):
        i = pl.program_id(0)
        has_prev = i > 0
        cur = p_ref[...]
        prv = jnp.where(has_prev, pp_ref[...], 0.0)
        tpos = (i * BLOCK + _iota((BLOCK, 1), 0) + 1).astype(F32)
        for g, w in enumerate(POOL_WINDOWS):
            sl = slice(g * POOL_GROUP, (g + 1) * POOL_GROUP)
            pooled = _pool_sums(prv[:, sl], cur[:, sl], w) / jnp.minimum(tpos, float(w)) - cur[:, sl]
            cat_ref[:, sl] = (_nn(pooled, pw_ref[g]) * ps_ref[:, sl]).astype(cat_ref.dtype)
        q = _rope(cur[:, POOL_DIM:POOL_DIM + Q_DIM], c_ref[...], s_ref[...])
        kc = _rope(cur[:, POOL_DIM + Q_DIM:POOL_DIM + Q_DIM + KV_DIM], c_ref[...], s_ref[...])
        kp = _rope(prv[:, POOL_DIM + Q_DIM:POOL_DIM + Q_DIM + KV_DIM], cp_ref[...], sp_ref[...])
        vc = cur[:, POOL_DIM + Q_DIM + KV_DIM:]
        vp = prv[:, POOL_DIM + Q_DIM + KV_DIM:]
        ri = _iota((GQ * BLOCK, BLOCK), 0) % BLOCK
        cj = _iota((GQ * BLOCK, BLOCK), 1)
        mc = cj <= ri
        mp = jnp.logical_and(cj > ri, has_prev)
        outs, lses = [], []
        for g in range(N_KV_HEADS):
            hs = slice(g * HEAD_DIM, (g + 1) * HEAD_DIM)
            qg = _stack_heads(q, g) * scale
            sc = jnp.where(mc, _nt(qg, kc[:, hs]), NEG)
            sp = jnp.where(mp, _nt(qg, kp[:, hs]), NEG)
            sink = jnp.concatenate([jnp.full((BLOCK, 1), sk_ref[GQ * g + r], F32) for r in range(GQ)], axis=0)
            m = jnp.maximum(jnp.maximum(jnp.max(sc, axis=1, keepdims=True), jnp.max(sp, axis=1, keepdims=True)), sink)
            pc = jnp.exp(sc - m)
            pp = jnp.exp(sp - m)
            den = jnp.sum(pc, axis=1, keepdims=True) + jnp.sum(pp, axis=1, keepdims=True) + jnp.exp(sink - m)
            o = (_nn(pc, vc[:, hs]) + _nn(pp, vp[:, hs])) / den
            lse = m + jnp.log(den)
            for r in range(GQ):
                outs.append(o[r * BLOCK:(r + 1) * BLOCK])
                lses.append(lse[r * BLOCK:(r + 1) * BLOCK])
        attn = jnp.concatenate(outs, axis=1)
        at_ref[...] = attn
        cat_ref[:, POOL_DIM:] = attn.astype(cat_ref.dtype)
        lane = _iota((BLOCK, LANES), 1)
        lrow = jnp.zeros((BLOCK, LANES), F32)
        for h in range(N_HEADS):
            lrow = jnp.where(lane == h, lses[h], lrow)
        lse_ref[...] = lrow

    cur = lambda w: pl.BlockSpec((BLOCK, w), lambda i: (i, 0))
    prv = lambda w: pl.BlockSpec((BLOCK, w), lambda i: (jnp.maximum(i - 1, 0), 0))
    return pl.pallas_call(
        body, grid=(nb,),
        in_specs=[cur(MIX_IN_DIM), prv(MIX_IN_DIM), cur(LANES), cur(LANES), prv(LANES), prv(LANES),
                  pl.BlockSpec((4, POOL_GROUP, POOL_GROUP), lambda i: (0, 0, 0)), pl.BlockSpec((1, POOL_DIM), lambda i: (0, 0)),
                  pl.BlockSpec(memory_space=pltpu.SMEM)],
        out_specs=[cur(2 * POOL_DIM), cur(Q_DIM), cur(LANES)],
        out_shape=[_sds((T, 2 * POOL_DIM), MXU), _sds((T, Q_DIM)), _sds((T, LANES))],
        compiler_params=_cp("parallel"), name=name)(proj, proj, cos, sin_s, cos, sin_s, pool_w, pool_scale, sinks)


def _mixcore_bwd(proj, cos, sin_s, pool_w, pool_scale, sinks, attn, lse, dcat, name):
    T = proj.shape[0]
    nb = T // BLOCK
    scale = HEAD_DIM ** -0.5
    QO, KO, VO = POOL_DIM, POOL_DIM + Q_DIM, POOL_DIM + Q_DIM + KV_DIM

    def body(p_ref, pp_ref, pn_ref, c_ref, s_ref, cp_ref, sp_ref, cn_ref, sn_ref, pw_ref, ps_ref, sk_ref,
             at_ref, atn_ref, l_ref, ln_ref, d_ref, dn_ref, dp_ref, dpw_ref, dps_ref, dsk_ref):
        i = pl.program_id(0)
        has_prev = i > 0
        has_next = i < nb - 1
        cur = p_ref[...]
        prv = jnp.where(has_prev, pp_ref[...], 0.0)
        d_cur = d_ref[...]
        d_nxt = jnp.where(has_next, dn_ref[...], 0.0)

        tpos = (i * BLOCK + _iota((BLOCK, 1), 0) + 1).astype(F32)
        tpos2 = (i * BLOCK + _iota((2 * BLOCK, 1), 0) + 1).astype(F32)
        ps = ps_ref[...]
        dps_parts, dpw_parts = [], []
        for g, w in enumerate(POOL_WINDOWS):
            sl = slice(g * POOL_GROUP, (g + 1) * POOL_GROUP)
            pooled = _pool_sums(prv[:, sl], cur[:, sl], w) / jnp.minimum(tpos, float(w)) - cur[:, sl]
            mixed = _nn(pooled, pw_ref[g])
            dps_parts.append(jnp.sum(d_cur[:, sl] * mixed, axis=0, keepdims=True))
            dm2 = jnp.concatenate([d_cur[:, sl], d_nxt[:, sl]], axis=0) * ps[:, sl]
            dpw_parts.append(_tn(pooled, dm2[:BLOCK]))
            dpool2 = _nt(dm2, pw_ref[g])
            e = dpool2 / jnp.minimum(tpos2, float(w))
            sh = 1
            while sh < w:
                e = e + pltpu.roll(e, 2 * BLOCK - sh, 0)
                sh *= 2
            dp_ref[:, sl] = (e[:BLOCK] - dpool2[:BLOCK]).astype(dp_ref.dtype)
        dpsp = jnp.concatenate(dps_parts, axis=1)

        nxt = pn_ref[...]
        q = _rope(cur[:, QO:KO], c_ref[...], s_ref[...])
        qn = _rope(nxt[:, QO:KO], cn_ref[...], sn_ref[...])
        kc = _rope(cur[:, KO:VO], c_ref[...], s_ref[...])
        kp = _rope(prv[:, KO:VO], cp_ref[...], sp_ref[...])
        vc, vp = cur[:, VO:], prv[:, VO:]
        do, don = d_cur[:, POOL_DIM:], d_nxt[:, POOL_DIM:]
        dl = do * at_ref[...]
        dln = don * atn_ref[...]
        lse, lsen = l_ref[...], ln_ref[...]
        ri = _iota((GQ * BLOCK, BLOCK), 0) % BLOCK
        cj = _iota((GQ * BLOCK, BLOCK), 1)
        mc = cj <= ri
        mp = jnp.logical_and(cj > ri, has_prev)
        mn = jnp.logical_and(cj > ri, has_next)
        dq_parts, dk_parts, dv_parts, dsk_vals = [], [], [], []
        for g in range(N_KV_HEADS):
            hs = slice(g * HEAD_DIM, (g + 1) * HEAD_DIM)
            qg, qng = _stack_heads(q, g) * scale, _stack_heads(qn, g) * scale
            dog, dong = _stack_heads(do, g), _stack_heads(don, g)
            delta = jnp.sum(_stack_heads(dl, g), axis=1, keepdims=True)
            deltan = jnp.sum(_stack_heads(dln, g), axis=1, keepdims=True)
            lg, lng = _stack_cols(lse, g), _stack_cols(lsen, g)
            pc = jnp.where(mc, jnp.exp(_nt(qg, kc[:, hs]) - lg), 0.0)
            pp = jnp.where(mp, jnp.exp(_nt(qg, kp[:, hs]) - lg), 0.0)
            pn = jnp.where(mn, jnp.exp(_nt(qng, kc[:, hs]) - lng), 0.0)
            dsc = pc * (_nt(dog, vc[:, hs]) - delta)
            dsp = pp * (_nt(dog, vp[:, hs]) - delta)
            dsn = pn * (_nt(dong, vc[:, hs]) - deltan)
            dqg = (_nn(dsc, kc[:, hs]) + _nn(dsp, kp[:, hs])) * scale
            dq_parts += [dqg[r * BLOCK:(r + 1) * BLOCK] for r in range(GQ)]
            dk_parts.append(_tn(dsc, qg) + _tn(dsn, qng))
            dv_parts.append(_tn(pc, dog) + _tn(pn, dong))
            sink = jnp.concatenate([jnp.full((BLOCK, 1), sk_ref[GQ * g + r], F32) for r in range(GQ)], axis=0)
            dsk = -jnp.exp(sink - lg) * delta
            dsk_vals += [jnp.sum(dsk[r * BLOCK:(r + 1) * BLOCK], axis=0, keepdims=True) for r in range(GQ)]
        dq = _rope(jnp.concatenate(dq_parts, axis=1), c_ref[...], s_ref[...], inverse=True)
        dk = _rope(jnp.concatenate(dk_parts, axis=1), c_ref[...], s_ref[...], inverse=True)
        dp_ref[:, QO:KO] = dq.astype(dp_ref.dtype)
        dp_ref[:, KO:VO] = dk.astype(dp_ref.dtype)
        dp_ref[:, VO:] = jnp.concatenate(dv_parts, axis=1).astype(dp_ref.dtype)
        lane = _iota((1, LANES), 1)
        dskp = jnp.zeros((1, LANES), F32)
        for h in range(N_HEADS):
            dskp = jnp.where(lane == h, dsk_vals[h], dskp)

        @pl.when(i == 0)
        def _():
            dps_ref[...] = dpsp
            dsk_ref[...] = dskp
            for g in range(4):
                dpw_ref[g] = dpw_parts[g]

        @pl.when(i > 0)
        def _():
            dps_ref[...] += dpsp
            dsk_ref[...] += dskp
            for g in range(4):
                dpw_ref[g] += dpw_parts[g]

    cur = lambda w: pl.BlockSpec((BLOCK, w), lambda i: (i, 0))
    prv = lambda w: pl.BlockSpec((BLOCK, w), lambda i: (jnp.maximum(i - 1, 0), 0))
    nxt = lambda w: pl.BlockSpec((BLOCK, w), lambda i: (jnp.minimum(i + 1, nb - 1), 0))
    return pl.pallas_call(
        body, grid=(nb,),
        in_specs=[cur(MIX_IN_DIM), prv(MIX_IN_DIM), nxt(MIX_IN_DIM),
                  cur(LANES), cur(LANES), prv(LANES), prv(LANES), nxt(LANES), nxt(LANES),
                  pl.BlockSpec((4, POOL_GROUP, POOL_GROUP), lambda i: (0, 0, 0)), pl.BlockSpec((1, POOL_DIM), lambda i: (0, 0)),
                  pl.BlockSpec(memory_space=pltpu.SMEM),
                  cur(Q_DIM), nxt(Q_DIM), cur(LANES), nxt(LANES), cur(2 * POOL_DIM), nxt(2 * POOL_DIM)],
        out_specs=[cur(MIX_IN_DIM), pl.BlockSpec((4, POOL_GROUP, POOL_GROUP), lambda i: (0, 0, 0)),
                   pl.BlockSpec((1, POOL_DIM), lambda i: (0, 0)), pl.BlockSpec((1, LANES), lambda i: (0, 0))],
        out_shape=[_sds((T, MIX_IN_DIM), MXU), _sds((4, POOL_GROUP, POOL_GROUP)), _sds((1, POOL_DIM)), _sds((1, LANES))],
        compiler_params=_cp("arbitrary"), name=name)(
            proj, proj, proj, cos, sin_s, cos, sin_s, cos, sin_s, pool_w, pool_scale, sinks, attn, attn, lse, lse, dcat, dcat)


GROUP_W = SSM_D_INNER // SSM_GROUPS


def _ssm_in_conv(h, wT, row_off, cw, cb, name):
    T, D = h.shape
    tm = min(T, 256)
    tc = 1024
    K = SSM_CONV

    def body(a_ref, b_ref, w_ref, c_ref, x_ref, pre_ref, act_ref, halo):
        @pl.when(pl.program_id(1) == 0)
        def _():
            halo[...] = jnp.zeros(halo.shape, F32)

        cur = _nt(a_ref[...], b_ref[...])
        x_ref[...] = cur
        pre = _conv_rows(cur, halo[...], w_ref[...], c_ref[...], K)
        halo[...] = cur[tm - SUBLANES:]
        pre_ref[...] = pre
        act_ref[...] = _silu(pre)

    blk = pl.BlockSpec((tm, tc), lambda j, i: (i, j))
    return pl.pallas_call(
        body, grid=(SSM_CONV_DIM // tc, T // tm),
        in_specs=[pl.BlockSpec((tm, D), lambda j, i: (i, 0)), pl.BlockSpec((tc, D), lambda j, i: (j + row_off // tc, 0)),
                  pl.BlockSpec((K, tc), lambda j, i: (0, j)), pl.BlockSpec((1, tc), lambda j, i: (0, j))],
        out_specs=[blk, blk, blk], out_shape=[_sds((T, SSM_CONV_DIM))] * 3,
        scratch_shapes=[pltpu.VMEM((SUBLANES, tc), F32)],
        compiler_params=_cp("arbitrary", "arbitrary"), name=name)(h, wT, cw, cb)


def _dot_hi(a, b):
    return jnp.dot(a, b, precision=HI, preferred_element_type=F32)


def _ssd_common(dtraw, bias, alog):
    L = SSM_CHUNK
    xb = dtraw + bias
    dt = jnp.maximum(xb, 0.0) + jnp.log1p(jnp.exp(-jnp.abs(xb)))
    A = -jnp.exp(alog)
    tril = (_iota((L, L), 1) <= _iota((L, L), 0)).astype(F32)
    acs = _dot_hi(tril, dt * A)
    return xb, dt, A, tril, acs


def _head_selectors():
    es = (_iota((LANES, SSM_D_INNER), 0) == _iota((LANES, SSM_D_INNER), 1) // HEAD_DIM).astype(BF16)
    est = (_iota((SSM_D_INNER, LANES), 1) == _iota((SSM_D_INNER, LANES), 0) // HEAD_DIM).astype(BF16)
    return es, est


def _dot_sel(v, sel):
    hi = v.astype(BF16)
    r1 = v - hi.astype(F32)
    mid = r1.astype(BF16)
    lo = (r1 - mid.astype(F32)).astype(BF16)
    d = lambda a: jnp.dot(a, sel, preferred_element_type=F32)
    return (d(hi) + d(mid)) + d(lo)


def _expand_heads(v, es):
    return _dot_sel(v, es)


def _reduce_heads(q, est):
    return _dot_sel(q, est)


def _per_state_row(v, g):
    return jnp.concatenate([jnp.broadcast_to(v[:, GQ * g + r:GQ * g + r + 1], (HEAD_DIM, 1)) for r in range(GQ)], axis=0)


def _ssd_fwd(xact, dtraw, dt_bias, a_log, z, d_skip, nw, name):
    T = xact.shape[0]
    nc = T // SSM_CHUNK
    L = SSM_CHUNK
    BO, CO = SSM_D_INNER, SSM_D_INNER + SSM_GROUPS * SSM_STATE

    def body(x_ref, dt_ref, bias_ref, al_ref, es_ref, z_ref, dsk_ref, nw_ref, y_ref, st_ref, yn_ref, state):
        @pl.when(pl.program_id(0) == 0)
        def _():
            state[...] = jnp.zeros(state.shape, F32)

        _, dt, A, tril, acs = _ssd_common(dt_ref[...], bias_ref[...], al_ref[...])
        acsT = acs.T
        last = acs[L - 1:L, :]
        cd = jnp.exp(last)
        es = es_ref[...]
        dtX = _expand_heads(dt, es)
        EX = _expand_heads(jnp.exp(acs), es)
        decX = _expand_heads(jnp.exp(last - acs), es)
        for g in range(SSM_GROUPS):
            gs = slice(g * GROUP_W, (g + 1) * GROUP_W)
            B = x_ref[:, BO + g * SSM_STATE:BO + (g + 1) * SSM_STATE]
            C = x_ref[:, CO + g * SSM_STATE:CO + (g + 1) * SSM_STATE]
            X = x_ref[:, gs] * dtX[:, gs]
            CB = _nt(C, B)
            yd = []
            for r in range(GQ):
                h = GQ * g + r
                Lm = jnp.exp(jnp.where(tril > 0, acs[:, h:h + 1] - acsT[h:h + 1, :], NEG))
                yd.append(_nn(CB * Lm, X[:, r * HEAD_DIM:(r + 1) * HEAD_DIM]))
            S = state[g]
            st_ref[g] = S
            y_ref[:, gs] = jnp.concatenate(yd, axis=1) + _nt(C, S) * EX[:, gs]
            state[g] = S * _per_state_row(cd, g) + _tn(X * decX[:, gs], B)
        y2 = (y_ref[...] + dsk_ref[...] * x_ref[:, :SSM_D_INNER]) * _silu(z_ref[...])
        r = lax.rsqrt(jnp.mean(y2 * y2, axis=-1, keepdims=True) + SSM_NORM_EPS)
        yn_ref[...] = (y2 * r * nw_ref[...]).astype(yn_ref.dtype)

    es, _ = _head_selectors()
    row = pl.BlockSpec((L, SSM_D_INNER), lambda c: (c, 0))
    vec = pl.BlockSpec((1, SSM_D_INNER), lambda c: (0, 0))
    return pl.pallas_call(
        body, grid=(nc,),
        in_specs=[pl.BlockSpec((L, SSM_CONV_DIM), lambda c: (c, 0)), pl.BlockSpec((L, LANES), lambda c: (c, 0)),
                  pl.BlockSpec((1, LANES), lambda c: (0, 0)), pl.BlockSpec((1, LANES), lambda c: (0, 0)),
                  pl.BlockSpec((LANES, SSM_D_INNER), lambda c: (0, 0)), row, vec, vec],
        out_specs=[row, pl.BlockSpec((None, SSM_GROUPS, GROUP_W, SSM_STATE), lambda c: (c, 0, 0, 0)), row],
        out_shape=[_sds((T, SSM_D_INNER)), _sds((nc, SSM_GROUPS, GROUP_W, SSM_STATE)), _sds((T, SSM_D_INNER), MXU)],
        scratch_shapes=[pltpu.VMEM((SSM_GROUPS, GROUP_W, SSM_STATE), F32)],
        compiler_params=_cp("arbitrary"), name=name)(xact, dtraw, dt_bias, a_log, es, z, d_skip, nw)


def _ssd_bwd(xact, xbc, xpre, cw, dtraw, dt_bias, a_log, d_skip, states, dyn, y, z, nw, name):
    T = xact.shape[0]
    nc = T // SSM_CHUNK
    L = SSM_CHUNK
    K = SSM_CONV
    BO, CO = SSM_D_INNER, SSM_D_INNER + SSM_GROUPS * SSM_STATE

    def body(x_ref, xin_ref, pre_ref, cw_ref, dt_ref, bias_ref, al_ref, dsk_ref, es_ref, est_ref, st_ref, dn_ref, y_ref,
             z_ref, nw_ref, dzx_ref, dcw_ref, dcb_ref, ddt_ref, dbias_ref, dal_ref, dd_ref, dnw_ref,
             dstate, qa, qx, dxp_ref, ahead, dyv):
        cc = pl.program_id(0)

        @pl.when(cc == 0)
        def _():
            dstate[...] = jnp.zeros(dstate.shape, F32)
            ahead[...] = jnp.zeros(ahead.shape, F32)

        zv = z_ref[...]
        sz = _silu(zv)
        yg = y_ref[...] + dsk_ref[...] * x_ref[:, :SSM_D_INNER]
        y2 = yg * sz
        rn = lax.rsqrt(jnp.mean(y2 * y2, axis=-1, keepdims=True) + SSM_NORM_EPS)
        y2h = y2 * rn
        dn = dn_ref[...]
        gy = dn * nw_ref[...]
        dy2 = rn * (gy - y2h * jnp.mean(gy * y2h, axis=-1, keepdims=True))
        dyv[...] = dy2 * sz
        dzx_ref[:, :SSM_D_INNER] = (dy2 * yg * _dsilu(zv)).astype(dzx_ref.dtype)
        dnwp = jnp.sum(dn * y2h, axis=0, keepdims=True)

        xb, dt, A, tril, acs = _ssd_common(dt_ref[...], bias_ref[...], al_ref[...])
        acsT = acs.T
        last = acs[L - 1:L, :]
        cd = jnp.exp(last)
        es, est = es_ref[...], est_ref[...]
        dtX = _expand_heads(dt, es)
        EX = _expand_heads(jnp.exp(acs), es)
        decX = _expand_heads(jnp.exp(last - acs), es)
        lane1 = _iota((1, LANES), 1)
        lane = _iota((L, LANES), 1)
        sub = _iota((L, LANES), 0)
        ztot = jnp.zeros((1, LANES), F32)
        wrow = jnp.zeros((L, LANES), F32)
        wcolT = jnp.zeros((LANES, L), F32)
        rows_dec, rows_dd = [], []
        for g in range(SSM_GROUPS):
            gs = slice(g * GROUP_W, (g + 1) * GROUP_W)
            x = x_ref[:, gs]
            B = x_ref[:, BO + g * SSM_STATE:BO + (g + 1) * SSM_STATE]
            C = x_ref[:, CO + g * SSM_STATE:CO + (g + 1) * SSM_STATE]
            dY = dyv[:, gs]
            dtx, e_x, dec_x = dtX[:, gs], EX[:, gs], decX[:, gs]
            X = x * dtx
            CB = _nt(C, B)
            S = st_ref[g]
            dS_out = dstate[g]
            dcb_sum = jnp.zeros((L, L), F32)
            dxd = []
            for r in range(GQ):
                h = GQ * g + r
                hs = slice(r * HEAD_DIM, (r + 1) * HEAD_DIM)
                Lm = jnp.exp(jnp.where(tril > 0, acs[:, h:h + 1] - acsT[h:h + 1, :], NEG))
                M = CB * Lm
                dM = _nt(dY[:, hs], X[:, hs])
                dxd.append(_tn(M, dY[:, hs]))
                dcb_sum = dcb_sum + dM * Lm
                Wm = dM * M
                wrow = jnp.where(lane == h, jnp.sum(Wm, axis=1, keepdims=True), wrow)
                wcolT = jnp.where(sub == h, jnp.sum(Wm, axis=0, keepdims=True), wcolT)
            dXd = jnp.concatenate(dxd, axis=1)
            G = _nt(C, S)
            dG = dY * e_x
            dDX = _nt(B, dS_out)
            dX = dXd + dec_x * dDX
            t_dec = dDX * X * dec_x
            qa[:, gs] = dG * G - t_dec
            qx[:, gs] = dX * x
            rows_dec.append(jnp.sum(t_dec, axis=0, keepdims=True))
            rows_dd.append(jnp.sum(dY * x, axis=0, keepdims=True))
            zc = jnp.sum(dS_out * S, axis=1, keepdims=True)
            for r in range(GQ):
                ztot = jnp.where(lane1 == GQ * g + r, jnp.sum(zc[r * HEAD_DIM:(r + 1) * HEAD_DIM], axis=0, keepdims=True), ztot)
            dxp_ref[:, gs] = dX * dtx + dY * dsk_ref[:, gs]
            dxp_ref[:, BO + g * SSM_STATE:BO + (g + 1) * SSM_STATE] = _tn(dcb_sum, C) + _nn(X * dec_x, dS_out)
            dxp_ref[:, CO + g * SSM_STATE:CO + (g + 1) * SSM_STATE] = _nn(dcb_sum, B) + _nn(dG, S)
            dstate[g] = dS_out * _per_state_row(cd, g) + _tn(dG, C)
        rows = jnp.concatenate([jnp.concatenate(rows_dec, axis=1), jnp.concatenate(rows_dd, axis=1)]
                               + [jnp.zeros((SUBLANES - 2, SSM_D_INNER), F32)], axis=0)
        rsum = _reduce_heads(rows, est)
        dlast = rsum[0:1, :] + cd * ztot
        dacs = (wrow - wcolT.T) + _reduce_heads(qa[...], est) + jnp.where(sub == L - 1, dlast, 0.0)
        triu = (_iota((L, L), 0) <= _iota((L, L), 1)).astype(F32)
        da = _dot_hi(triu, dacs)
        ddtraw = (da * A + _reduce_heads(qx[...], est)) * (1.0 / (1.0 + jnp.exp(-xb)))
        ddt_ref[...] = ddtraw
        dal = jnp.sum(da * dt, axis=0, keepdims=True) * A
        ddp = rsum[1:2, :]
        dbp = jnp.sum(ddtraw, axis=0, keepdims=True)
        w = cw_ref[...]
        d_cur = dxp_ref[...] * _dsilu(pre_ref[...])
        d_nxt = ahead[...]
        ahead[...] = d_cur[:SUBLANES]
        ups = [d_cur] + [_shift_up(d_cur, d_nxt, s) for s in range(1, K)]
        dxc = ups[0] * w[K - 1:K, :]
        for s in range(1, K):
            dxc = dxc + ups[s] * w[K - 1 - s:K - s, :]
        dzx_ref[:, SSM_D_INNER:] = dxc.astype(dzx_ref.dtype)
        xin = xin_ref[...]
        dcwp = jnp.concatenate([jnp.sum(ups[K - 1 - k] * xin, axis=0, keepdims=True) for k in range(K)], axis=0)
        dcbp = jnp.sum(d_cur, axis=0, keepdims=True)

        @pl.when(cc == 0)
        def _():
            dbias_ref[...] = dbp
            dal_ref[...] = dal
            dd_ref[...] = ddp
            dcw_ref[...] = dcwp
            dcb_ref[...] = dcbp
            dnw_ref[...] = dnwp

        @pl.when(cc > 0)
        def _():
            dbias_ref[...] += dbp
            dal_ref[...] += dal
            dd_ref[...] += ddp
            dcw_ref[...] += dcwp
            dcb_ref[...] += dcbp
            dnw_ref[...] += dnwp

    rc = lambda c: nc - 1 - c
    vec = pl.BlockSpec((1, LANES), lambda c: (0, 0))
    wide = pl.BlockSpec((L, SSM_CONV_DIM), lambda c: (rc(c), 0))
    inner = pl.BlockSpec((L, SSM_D_INNER), lambda c: (rc(c), 0))
    es, est = _head_selectors()
    return pl.pallas_call(
        body, grid=(nc,),
        in_specs=[wide, wide, wide, pl.BlockSpec((K, SSM_CONV_DIM), lambda c: (0, 0)),
                  pl.BlockSpec((L, LANES), lambda c: (rc(c), 0)), vec, vec,
                  pl.BlockSpec((1, SSM_D_INNER), lambda c: (0, 0)),
                  pl.BlockSpec((LANES, SSM_D_INNER), lambda c: (0, 0)), pl.BlockSpec((SSM_D_INNER, LANES), lambda c: (0, 0)),
                  pl.BlockSpec((None, SSM_GROUPS, GROUP_W, SSM_STATE), lambda c: (rc(c), 0, 0, 0)),
                  inner, inner, inner, pl.BlockSpec((1, SSM_D_INNER), lambda c: (0, 0))],
        out_specs=[pl.BlockSpec((L, SSM_D_INNER + SSM_CONV_DIM), lambda c: (rc(c), 0)),
                   pl.BlockSpec((K, SSM_CONV_DIM), lambda c: (0, 0)), pl.BlockSpec((1, SSM_CONV_DIM), lambda c: (0, 0)),
                   pl.BlockSpec((L, LANES), lambda c: (rc(c), 0)), vec, vec, vec,
                   pl.BlockSpec((1, SSM_D_INNER), lambda c: (0, 0))],
        out_shape=[_sds((T, SSM_D_INNER + SSM_CONV_DIM), MXU), _sds((K, SSM_CONV_DIM)), _sds((1, SSM_CONV_DIM)),
                   _sds((T, LANES)), _sds((1, LANES)), _sds((1, LANES)), _sds((1, LANES)), _sds((1, SSM_D_INNER))],
        scratch_shapes=[pltpu.VMEM((SSM_GROUPS, GROUP_W, SSM_STATE), F32), pltpu.VMEM((L, SSM_D_INNER), F32),
                        pltpu.VMEM((L, SSM_D_INNER), F32), pltpu.VMEM((L, SSM_CONV_DIM), F32),
                        pltpu.VMEM((SUBLANES, SSM_CONV_DIM), F32), pltpu.VMEM((L, SSM_D_INNER), F32)],
        compiler_params=_cp("arbitrary"), name=name)(xact, xbc, xpre, cw, dtraw, dt_bias, a_log, d_skip, es, est, states, dyn, y,
                                                     z, nw)


def _local_step(x0, cos, sin_s, target, P, fetch, token, send):
    mmf = functools.partial(_mm, tm=1024)
    big, small = {}, {}
    P = dict(P, wup={}, wdn={}, fcw={})
    h0 = _rmsnorm_fwd(x0, P["nm"][0], "norm_mix0", token=token)
    proj0 = mmf(h0, P["wmiT"], tb=True, tn=1280, tk=1024, name="mix_in")
    cat, attn, lse = _mixcore_fwd(proj0, cos, sin_s, P["pool_w"], P["pool_scale"], P["sinks"], "mixcore_fwd")
    x1, hf0 = mmf(cat, P["wmo"], tn=1024, tk=1024, res=x0, norm_w=P["nf"][0], name="mix_out")

    def ffn_fwd(xin, hf, i, **epilogue):
        got = fetch(f"ffn{i}", hf)
        P["wup"][i], P["wdn"][i], P["fcw"][i] = got["wup"], got["wdn"], got["fcw"]
        hid, hc, act = _ffn_up_conv_gate(hf, P["wup"][i], P["fcw"][i], P["fcb"][i], f"ffn_up{i}")
        xout = mmf(act, P["wdn"][i], tn=1024, tk=D_FF, res=xin, name=f"ffn_down{i}", **epilogue)
        return (hid, hc), act, xout

    hid0, act0, (x2, h1) = ffn_fwd(x1, hf0, 0, norm_w=P["nm"][1])
    P.update(fetch("ssm", h1))
    z = mmf(h1, P["wsiT"], tb=True, tn=1024, tk=1024, b_rows=(0, SSM_D_INNER), name="ssm_in_z")
    xbc, xpre, xact = _ssm_in_conv(h1, P["wsiT"], SSM_D_INNER, P["scw"], P["scb"], "ssm_in_xbc")
    dtraw = mmf(h1, P["wdtT"], tb=True, tn=128, tk=1024, name="ssm_in_dt")
    y, states, yn = _ssd_fwd(xact, dtraw, P["dt_bias"], P["a_log"], z, P["d_exp"], P["snorm"], "ssd_fwd")
    x3, hf1 = mmf(yn, P["wso"], tn=1024, tk=SSM_D_INNER, res=x2, norm_w=P["nf"][1], name="ssm_out")
    hid1, act1, (dx4, d_nfin, loss_row) = ffn_fwd(x3, hf1, 1, loss_head=(P["nfin"], target))
    small["norm_final"] = d_nfin

    def ffn_bwd(xin, dxo, hf, hid, act, i):
        big[f"ffn_w_down{i}"] = dwf(act, dxo, tm=1408, tn=1024, name=f"ffn_down_dw{i}").reshape(N_CHIPS, D_FF // N_CHIPS, D_MODEL)
        dhid, dcw, dcb = _ffn_down_dx_mid_bwd(dxo, P["wdn"][i], hid[0], hid[1], P["fcw"][i], f"ffn_down_dx{i}")
        big[f"ffn_w_up{i}"] = dwf(hf, dhid, tm=1024, tn=1408, out_shard_perm=(0, 2, 1, 3), name=f"ffn_up_dw{i}")
        tok = send(f"ffn{i}", [big[f"ffn_w_up{i}"], big[f"ffn_w_down{i}"]])
        dxi, dnf = _mm(dhid, P["wup"][i], tb=True, tm=512, tn=1024, tk=5632, norm_bwd=(xin, P["nf"][i], dxo, tok), name=f"ffn_up_dx{i}")
        return dxi, dnf, dcw, dcb

    dwf = functools.partial(_mm, ta=True, tk=2048, out_dtype=BF16)
    dx3, dnf1, dfcw1, dfcb1 = ffn_bwd(x3, dx4, hf1, hid1, act1, 1)
    dyn = mmf(dx3, P["wso"], tb=True, tn=1024, tk=1024, name="ssm_out_dx")
    big["ssm_w_out"] = dwf(yn, dx3, tm=1024, tn=1024, name="ssm_out_dw").reshape(N_CHIPS, SSM_D_INNER // N_CHIPS, D_MODEL)
    dzx, d_scw, d_scb, ddtraw, d_dtb, d_alog, d_dskip, d_snorm = _ssd_bwd(
        xact, xbc, xpre, P["scw"], dtraw, P["dt_bias"], P["a_log"], P["d_exp"], states, dyn, y, z, P["snorm"], "ssd_bwd")
    zx = SSM_D_INNER + SSM_CONV_DIM
    dwsi = dwf(dzx, h1, tm=1024, tn=1024, out_into=(None, SSM_IN_DIM, 0), name="ssm_in_dw_zx")
    dwdt = dwf(ddtraw, h1, tm=128, tn=1024, name="ssm_in_dw_dt")
    dwsi = _put_rows(dwsi, dwdt, SSM_HEADS, SSM_D_INNER + SSM_CONV_DIM, "ssm_in_dw_put_dt")
    big["ssm_w_in"] = dwsi.reshape(N_CHIPS, SSM_IN_DIM // N_CHIPS, D_MODEL)
    tok = send("ssm", [big["ssm_w_in"], big["ssm_w_out"]])
    dh1 = mmf(dzx, P["wsiT"], tn=1024, tk=2048, b_rows=(0, zx), name="ssm_in_dx_zx")
    dx2, dnm1 = mmf(ddtraw, P["wdtT"], tn=1024, tk=128, res=dh1, norm_bwd=(x2, P["nm"][1], dx3, tok), name="ssm_in_dx_dt")
    dx1, dnf0, dfcw0, dfcb0 = ffn_bwd(x1, dx2, hf0, hid0, act0, 0)
    dcat = mmf(dx1, P["wmo"], tb=True, tn=1024, tk=1024, name="mix_out_dx")
    big["mix_w_out"] = dwf(cat, dx1, tm=1024, tn=1024, name="mix_out_dw").reshape(N_CHIPS, D_MODEL // N_CHIPS, D_MODEL)
    dproj0, d_pw, d_ps, d_sk = _mixcore_bwd(proj0, cos, sin_s, P["pool_w"], P["pool_scale"], P["sinks"], attn, lse, dcat, "mixcore_bwd")
    big["mix_w_in"] = dwf(dproj0, h0, tm=1280, tn=1024, name="mix_in_dw").reshape(N_CHIPS, MIX_IN_DIM // N_CHIPS, D_MODEL)
    tok = send("mix", [big["mix_w_in"], big["mix_w_out"]])
    dx0, dnm0 = mmf(dproj0, P["wmiT"], tn=1024, tk=1280, norm_bwd=(x0, P["nm"][0], dx1, tok), name="mix_in_dx")

    def unperm_cols(a):
        r = a.shape[0]
        t = a.reshape(r, N_CHIPS, FFN_TC)
        return jnp.stack([t[:, p] for p in _PERM], axis=0)

    small["norm_mix"] = jnp.concatenate([dnm0, dnm1], axis=0)
    small["norm_ffn"] = jnp.concatenate([dnf0, dnf1], axis=0)
    small["pool_w"] = d_pw.reshape(4 * POOL_GROUP, POOL_GROUP)
    small["pool_scale"] = d_ps
    small["attn_sinks"] = d_sk
    small["ssm_dt_bias"] = d_dtb
    small["ssm_A_log"] = d_alog
    small["ssm_D"] = d_dskip
    fcb = jnp.stack([unperm_cols(dfcb0), unperm_cols(dfcb1)], axis=0)
    small["ffn_conv_b"] = fcb.reshape(2, 2 * D_FF)
    small["ssm_conv_w"] = d_scw.reshape(SSM_CONV, N_CHIPS, SSM_CONV_DIM // N_CHIPS).transpose(1, 0, 2)
    small["ssm_conv_b"] = d_scb.reshape(N_CHIPS, 1, SSM_CONV_DIM // N_CHIPS)
    small["ssm_norm"] = d_snorm.reshape(N_CHIPS, 1, SSM_D_INNER // N_CHIPS)
    small["ffn_conv_w"] = jnp.concatenate([unperm_cols(dfcw0), unperm_cols(dfcw1)], axis=1)
    return loss_row, dx0, big, small


ANY = pl.BlockSpec(memory_space=pl.ANY)


def _place():
    return lax.axis_index("x"), lax.axis_index("y"), lax.axis_index("c")


def _gather_shards(shards, name):
    n = len(shards)
    split = [s.size >= (1 << 16) for s in shards]

    def half(ref, a, h):
        shp = shards[a].shape
        if len(shp) == 3:
            return ref.at[h]
        r2 = shp[0] // 2
        return ref.at[pl.ds(pl.multiple_of(h * r2, 2 * SUBLANES), r2), :]

    def body(*refs):
        ins, outs = refs[:n], refs[n:2 * n]
        send, recv, fsend, frecv = refs[2 * n:]
        x, y, c = _place()
        k = 2 * x + y
        chips = [(1 - x, y), (x, 1 - y), (1 - x, 1 - y)]

        def ici(a, j, src_slot_ref, dst_slot):
            px, py = chips[j]
            src = half(src_slot_ref, a, c) if split[a] else src_slot_ref
            dst = half(outs[a].at[dst_slot], a, c) if split[a] else outs[a].at[dst_slot]
            return pltpu.make_async_remote_copy(src, dst, send.at[a, j], recv.at[a, j], device_id=(px, py, c), device_id_type=MESH)

        def d2d(a, j, h):
            px, py = chips[j]
            part = half(outs[a].at[2 * px + py], a, h)
            return pltpu.make_async_remote_copy(part, part, fsend.at[a, j], frecv.at[a, j], device_id=(x, y, 1 - c), device_id_type=MESH)

        sends = [ici(a, j, ins[a], k) for a in range(n) for j in range(3)]
        for cp in sends:
            cp.start()
        passed = []
        for a in range(n):
            for j, (px, py) in enumerate(chips):
                ici(a, j, ins[a], 2 * px + py).wait_recv()
                if split[a]:
                    passed.append(d2d(a, j, c))
                    passed[-1].start()
        for a in range(n):
            if split[a]:
                for j in range(3):
                    d2d(a, j, 1 - c).wait_recv()
        for cp in sends + passed:
            cp.wait_send()

    return pl.pallas_call(
        body, in_specs=[ANY] * n, out_specs=[ANY] * n,
        out_shape=[_sds((N_CHIPS,) + s.shape, s.dtype) for s in shards],
        scratch_shapes=[pltpu.SemaphoreType.DMA((n, 3))] * 4,
        compiler_params=pltpu.CompilerParams(has_side_effects=True), name=name)(*shards)


HBM = pl.BlockSpec(memory_space=pltpu.HBM)
SEM = pl.BlockSpec(memory_space=pltpu.SEMAPHORE)
DATAFLOW = pltpu.SideEffectType.DATAFLOW_SIDE_EFFECTING


def _row_half(ref, h):
    r2 = ref.shape[0] // 2
    return ref.at[pl.ds(pl.multiple_of(h * r2, 2 * SUBLANES), r2), :]


def _spread_start(groups, slot_src, after, name, halved=()):
    flat = [a for grp in groups for a in grp]
    n = len(flat)
    ng = len(groups)
    offs = [sum(len(g) for g in groups[:i]) for i in range(ng)]
    lshape = [(a.shape if slot_src else (N_CHIPS,) + a.shape) for a in flat]

    nsem = 6 * n

    def body(*refs):
        src, land = refs[:n], refs[n:2 * n]
        sems = refs[2 * n + 1:2 * n + 1 + nsem]
        token = refs[-1]
        x, y, c = _place()
        k = 2 * x + y
        chips = [(1 - x, y), (x, 1 - y), (1 - x, 1 - y)]
        for a in range(n):
            half = any(offs[gi] <= a < offs[gi] + len(groups[gi]) for gi in halved)
            for j, (px, py) in enumerate(chips):
                s = src[a].at[2 * px + py] if slot_src else src[a]
                d = land[a].at[k]
                if half:
                    s, d = _row_half(s, c), _row_half(d, c)
                pltpu.make_async_remote_copy(s, d, sems[6 * a + 2 * j], sems[6 * a + 2 * j + 1],
                                             device_id=(px, py, c), device_id_type=MESH).start()
        token[...] = jnp.zeros(token.shape, token.dtype)

    out_shape = [pltpu.SemaphoreType.DMA(())] * nsem
    out_shape += [pltpu.HBM(a.shape, a.dtype) for a in flat] + [pltpu.HBM(s, a.dtype) for s, a in zip(lshape, flat)]
    out_shape.append(_sds((SUBLANES, LANES)))
    args = [pltpu.with_memory_space_constraint(a, pltpu.HBM) for a in flat]
    args += [pltpu.with_memory_space_constraint(lax.empty(s, a.dtype), pltpu.HBM) for s, a in zip(lshape, flat)]
    res = pl.pallas_call(
        body, name=name, out_shape=tuple(out_shape), in_specs=[HBM] * (2 * n) + [pl.BlockSpec(memory_space=pl.ANY)],
        out_specs=tuple([SEM] * nsem + [HBM] * (2 * n) + [pl.BlockSpec(memory_space=pltpu.VMEM)]),
        input_output_aliases={i: nsem + i for i in range(2 * n)},
        compiler_params=pltpu.CompilerParams(has_side_effects=DATAFLOW))(*args, after)
    sems, thru, token = res[:nsem], res[nsem:nsem + 2 * n], res[-1]
    out = []
    for gi, grp in enumerate(groups):
        sl = slice(offs[gi], offs[gi] + len(grp))
        out.append((list(sems[6 * offs[gi]:6 * (offs[gi] + len(grp))]), list(thru[:n][sl]), list(thru[n:][sl])))
    return out, token


def _spread_wait(started, slot_src, after, name, halved=False):
    sems, srcs, lands = started
    n = len(srcs)

    def body(*refs):
        src, land = refs[:n], refs[n:2 * n]
        sem = refs[2 * n:2 * n + 6 * n]
        x, y, c = _place()
        chips = [(1 - x, y), (x, 1 - y), (1 - x, 1 - y)]
        for a in range(n):
            for j, (px, py) in enumerate(chips):
                s = src[a].at[2 * px + py] if slot_src else src[a]
                d = land[a].at[2 * px + py]
                if halved:
                    s, d = _row_half(s, c), _row_half(d, c)
                cp = pltpu.make_async_remote_copy(s, d, sem[6 * a + 2 * j], sem[6 * a + 2 * j + 1],
                                                  device_id=(px, py, c), device_id_type=MESH)
                cp.wait_send()
                cp.wait_recv()

    res = pl.pallas_call(
        body, name=name, out_shape=tuple([pltpu.HBM(a.shape, a.dtype) for a in srcs] + [pltpu.HBM(a.shape, a.dtype) for a in lands]),
        in_specs=[HBM] * (2 * n) + [SEM] * (6 * n) + [pl.BlockSpec(memory_space=pl.ANY)], out_specs=tuple([HBM] * (2 * n)),
        input_output_aliases={i: i for i in range(2 * n)},
        compiler_params=pltpu.CompilerParams(has_side_effects=DATAFLOW))(*srcs, *lands, *sems, after)
    return list(res[:n]), list(res[n:])


def _sibling_fill(lands, name):
    n = len(lands)

    def body(*refs):
        bufs = refs[n:2 * n]
        send, recv = refs[2 * n:]
        x, y, c = _place()
        chips = [(1 - x, y), (x, 1 - y), (1 - x, 1 - y)]

        def copy(a, j, h):
            px, py = chips[j]
            part = _row_half(bufs[a].at[2 * px + py], h)
            return pltpu.make_async_remote_copy(part, part, send.at[a, j], recv.at[a, j], device_id=(x, y, 1 - c), device_id_type=MESH)

        sends = [copy(a, j, c) for a in range(n) for j in range(3)]
        for cp in sends:
            cp.start()
        for a in range(n):
            for j in range(3):
                copy(a, j, 1 - c).wait_recv()
        for cp in sends:
            cp.wait_send()

    return pl.pallas_call(
        body, in_specs=[ANY] * n, out_specs=[ANY] * n, out_shape=[_sds(t.shape, t.dtype) for t in lands],
        input_output_aliases={i: i for i in range(n)},
        scratch_shapes=[pltpu.SemaphoreType.DMA((n, 3)), pltpu.SemaphoreType.DMA((n, 3))],
        compiler_params=pltpu.CompilerParams(has_side_effects=True), name=name)(*lands)


def _sibling_exchange(fs, name):
    n = len(fs)

    def body(*refs):
        ins, outs = refs[:n], refs[n:2 * n]
        send, recv = refs[2 * n:]
        x, y, c = _place()
        cps = [pltpu.make_async_remote_copy(ins[a], outs[a], send.at[a], recv.at[a],
                                            device_id=(x, y, 1 - c), device_id_type=MESH) for a in range(n)]
        for cp in cps:
            cp.start()
        for cp in cps:
            cp.wait()

    return pl.pallas_call(
        body, in_specs=[ANY] * n, out_specs=[ANY] * n, out_shape=[_sds(f.shape, f.dtype) for f in fs],
        scratch_shapes=[pltpu.SemaphoreType.DMA((n,)), pltpu.SemaphoreType.DMA((n,))],
        compiler_params=pltpu.CompilerParams(has_side_effects=True), name=name)(*fs)


def _tile2d(rows, cols, budget=2 * 1024 * 1024, step=2 * SUBLANES):
    fits = [t for t in range(step, rows + 1, step) if rows % t == 0 and t * cols * 4 <= budget]
    if fits:
        return fits[-1], cols
    fits = [t for t in range(LANES, cols + 1, LANES) if cols % t == 0 and rows * t * 4 <= budget]
    assert fits, (rows, cols)
    return rows, fits[-1]


def _chip_sum(own, parts, kidx, name):
    _, R, C = parts.shape
    tr, tc = _tile2d(R, C)

    def body(k_ref, o_ref_in, p1_ref, p2_ref, p3_ref, o_ref):
        tot = ((o_ref_in[...].astype(F32) + p1_ref[...].astype(F32)) + p2_ref[...].astype(F32)) + p3_ref[...].astype(F32)
        o_ref[...] = tot.astype(o_ref.dtype)

    def slot(d):
        return pl.BlockSpec((None, tr, tc), lambda i, j, k: ((k[0] + d) % N_CHIPS, i, j))

    return pl.pallas_call(
        body,
        grid_spec=pltpu.PrefetchScalarGridSpec(
            num_scalar_prefetch=1, grid=(R // tr, C // tc), in_specs=[slot(0), slot(1), slot(2), slot(3)],
            out_specs=pl.BlockSpec((tr, tc), lambda i, j, k: (i, j))),
        out_shape=_sds((R, C), BF16), compiler_params=_cp("parallel", "parallel"), name=name)(kidx, own, parts, parts, parts)


def _adamw_math(w, g, m, v):
    m2 = ADAM_B1 * m + (1.0 - ADAM_B1) * g
    v2 = ADAM_B2 * v + (1.0 - ADAM_B2) * (g * g)
    m_hat = m2 / (1.0 - ADAM_B1 ** ADAM_STEP)
    v_hat = v2 / (1.0 - ADAM_B2 ** ADAM_STEP)
    delta = -ADAM_LR * (m_hat / (jnp.sqrt(v_hat) + ADAM_EPS) + ADAM_WD * w)
    return delta, m2, v2


def _adamw(w, m, v, gparts, name):
    Lw, R, C = w.shape
    tr, tc = _tile2d(R, C)
    flat = [h for pair in gparts for h in pair]

    def body(*refs):
        w_ref, m_ref, v_ref = refs[:3]
        g_refs = refs[3:3 + 2 * Lw]
        go_ref, d_ref, mo_ref, vo_ref = refs[3 + 2 * Lw:]
        g = g_refs[0][...].astype(F32) + g_refs[1][...].astype(F32)
        for l in range(1, Lw):
            g = jnp.where(pl.program_id(0) == l, g_refs[2 * l][...].astype(F32) + g_refs[2 * l + 1][...].astype(F32), g)
        d, m2, v2 = _adamw_math(w_ref[...], g, m_ref[...], v_ref[...])
        go_ref[...] = g
        d_ref[...] = d
        mo_ref[...] = m2
        vo_ref[...] = v2

    blk = pl.BlockSpec((None, tr, tc), lambda l, i, j: (l, i, j))
    gblk = pl.BlockSpec((tr, tc), lambda l, i, j: (i, j))
    return pl.pallas_call(
        body, grid=(Lw, R // tr, C // tc), in_specs=[blk, blk, blk] + [gblk] * (2 * Lw), out_specs=[blk] * 4,
        out_shape=[_sds((Lw, R, C))] * 4, compiler_params=_cp("parallel", "parallel", "parallel"), name=name)(w, m, v, *flat)


def _small_adamw(grads, wmv, name):
    n = len(grads)

    def body(*refs):
        g_in, p_in, outs = refs[:n], refs[n:4 * n], refs[4 * n:]
        for a in range(n):
            g = g_in[a][...]
            d_, m2, v2 = _adamw_math(p_in[3 * a][...], g, p_in[3 * a + 1][...], p_in[3 * a + 2][...])
            outs[4 * a][...] = g
            outs[4 * a + 1][...] = d_
            outs[4 * a + 2][...] = m2
            outs[4 * a + 3][...] = v2

    vm = pl.BlockSpec(memory_space=pltpu.VMEM)
    args = list(grads) + [t for tri in wmv for t in tri]
    out_shape = [_sds(g.shape) for g in grads for _ in range(4)]
    return pl.pallas_call(body, in_specs=[vm] * len(args), out_specs=[vm] * len(out_shape), out_shape=out_shape,
                          compiler_params=pltpu.CompilerParams(vmem_limit_bytes=V7X_VMEM_LIMIT), name=name)(*args)


def _small_allreduce(partials, pshapes, loss_row, name):
    n = len(partials)
    gshapes = [p.shape for p in partials] + [loss_row.shape]
    ng = n + 1

    def body(*refs):
        g_in = refs[:ng]
        outs = refs[ng:2 * ng]
        sib = refs[2 * ng:3 * ng]
        pair = refs[3 * ng:4 * ng]
        bufs = refs[4 * ng:5 * ng]
        send1, recv1, send2, recv2 = refs[-4:]
        x, y, c = _place()
        k = 2 * x + y
        chips = [(1 - x, y), (x, 1 - y), (1 - x, 1 - y)]
        swaps = [pltpu.make_async_remote_copy(g_in[a], sib[a], send1.at[a], recv1.at[a],
                                              device_id=(x, y, 1 - c), device_id_type=MESH) for a in range(ng)]
        for cp in swaps:
            cp.start()
        for a, cp in enumerate(swaps):
            cp.wait()
            pair[a][...] = g_in[a][...] + sib[a][...]
            bufs[a][k] = pair[a][...]
        sends = [pltpu.make_async_remote_copy(pair[a], bufs[a].at[k], send2.at[a, j], recv2.at[a, j],
                                              device_id=(px, py, c), device_id_type=MESH)
                 for a in range(ng) for j, (px, py) in enumerate(chips)]
        for cp in sends:
            cp.start()
        for a in range(ng):
            for j, (px, py) in enumerate(chips):
                pltpu.make_async_remote_copy(pair[a], bufs[a].at[2 * px + py], send2.at[a, j], recv2.at[a, j],
                                             device_id=(px, py, c), device_id_type=MESH).wait_recv()
        for cp in sends:
            cp.wait_send()
        for a in range(ng):
            sharded = len(gshapes[a]) == 3

            def part(d):
                return bufs[a][d, k] if sharded else bufs[a][d]

            tot = part(0)
            for d in range(1, N_CHIPS):
                tot = tot + part(d)
            if a == n:
                outs[n][...] = tot
            else:
                pr, pc = pshapes[a]
                outs[a][...] = tot[:pr, :pc]

    vm = pl.BlockSpec(memory_space=pltpu.VMEM)
    args = list(partials) + [loss_row]
    out_shape = [_sds(ps) for ps in pshapes] + [_sds(loss_row.shape)]
    return pl.pallas_call(
        body, in_specs=[vm] * len(args), out_specs=[vm] * len(out_shape), out_shape=out_shape,
        scratch_shapes=[pltpu.VMEM(tuple(s), F32) for s in gshapes] * 2 + [pltpu.VMEM((N_CHIPS,) + tuple(s), F32) for s in gshapes]
        + [pltpu.SemaphoreType.DMA((ng,)), pltpu.SemaphoreType.DMA((ng,)),
           pltpu.SemaphoreType.DMA((ng, 3)), pltpu.SemaphoreType.DMA((ng, 3))],
        compiler_params=pltpu.CompilerParams(has_side_effects=True, vmem_limit_bytes=V7X_VMEM_LIMIT), name=name)(*args)


_PERM = (0, 2, 1, 3)


def _cols_from_shards(g):
    return g.transpose(1, 0, 2).reshape(g.shape[1], N_CHIPS * g.shape[2])


def _rope_tables(positions):
    inv_freq = ROPE_THETA ** (-jnp.arange(0, HEAD_DIM, 2, dtype=F32) / HEAD_DIM)
    ang = positions.astype(F32).reshape(-1, 1) * inv_freq
    cos, sin = jnp.cos(ang), jnp.sin(ang)
    cos = jnp.concatenate([cos, cos, cos, cos], axis=-1)
    sin_s = jnp.concatenate([-sin, sin, -sin, sin], axis=-1)
    return cos, sin_s


def kernel(x, positions, norm_mix, norm_ffn, norm_final, mix_w_in, pool_w, pool_scale, attn_sinks, mix_w_out, ssm_w_in, ssm_conv_w, ssm_conv_b, ssm_dt_bias, ssm_A_log, ssm_D, ssm_norm, ssm_w_out, ffn_w_up, ffn_conv_w, ffn_conv_b, ffn_w_down, loss_target, m_norm_mix, m_norm_ffn, m_norm_final, m_mix_w_in, m_pool_w, m_pool_scale, m_attn_sinks, m_mix_w_out, m_ssm_w_in, m_ssm_conv_w, m_ssm_conv_b, m_ssm_dt_bias, m_ssm_A_log, m_ssm_D, m_ssm_norm, m_ssm_w_out, m_ffn_w_up, m_ffn_conv_w, m_ffn_conv_b, m_ffn_w_down, v_norm_mix, v_norm_ffn, v_norm_final, v_mix_w_in, v_pool_w, v_pool_scale, v_attn_sinks, v_mix_w_out, v_ssm_w_in, v_ssm_conv_w, v_ssm_conv_b, v_ssm_dt_bias, v_ssm_A_log, v_ssm_D, v_ssm_norm, v_ssm_w_out, v_ffn_w_up, v_ffn_conv_w, v_ffn_conv_b, v_ffn_w_down):
    W = dict(norm_mix=norm_mix, norm_ffn=norm_ffn, norm_final=norm_final, mix_w_in=mix_w_in, pool_w=pool_w, pool_scale=pool_scale, attn_sinks=attn_sinks, mix_w_out=mix_w_out, ssm_w_in=ssm_w_in, ssm_conv_w=ssm_conv_w, ssm_conv_b=ssm_conv_b, ssm_dt_bias=ssm_dt_bias, ssm_A_log=ssm_A_log, ssm_D=ssm_D, ssm_norm=ssm_norm, ssm_w_out=ssm_w_out, ffn_w_up=ffn_w_up, ffn_conv_w=ffn_conv_w, ffn_conv_b=ffn_conv_b, ffn_w_down=ffn_w_down)
    Mo = dict(norm_mix=m_norm_mix, norm_ffn=m_norm_ffn, norm_final=m_norm_final, mix_w_in=m_mix_w_in, pool_w=m_pool_w, pool_scale=m_pool_scale, attn_sinks=m_attn_sinks, mix_w_out=m_mix_w_out, ssm_w_in=m_ssm_w_in, ssm_conv_w=m_ssm_conv_w, ssm_conv_b=m_ssm_conv_b, ssm_dt_bias=m_ssm_dt_bias, ssm_A_log=m_ssm_A_log, ssm_D=m_ssm_D, ssm_norm=m_ssm_norm, ssm_w_out=m_ssm_w_out, ffn_w_up=m_ffn_w_up, ffn_conv_w=m_ffn_conv_w, ffn_conv_b=m_ffn_conv_b, ffn_w_down=m_ffn_w_down)
    Vo = dict(norm_mix=v_norm_mix, norm_ffn=v_norm_ffn, norm_final=v_norm_final, mix_w_in=v_mix_w_in, pool_w=v_pool_w, pool_scale=v_pool_scale, attn_sinks=v_attn_sinks, mix_w_out=v_mix_w_out, ssm_w_in=v_ssm_w_in, ssm_conv_w=v_ssm_conv_w, ssm_conv_b=v_ssm_conv_b, ssm_dt_bias=v_ssm_dt_bias, ssm_A_log=v_ssm_A_log, ssm_D=v_ssm_D, ssm_norm=v_ssm_norm, ssm_w_out=v_ssm_w_out, ffn_w_up=v_ffn_w_up, ffn_conv_w=v_ffn_conv_w, ffn_conv_b=v_ffn_conv_b, ffn_w_down=v_ffn_w_down)

    kchip = 2 * lax.axis_index("x") + lax.axis_index("y")

    def own_slot(g, own):
        return lax.dynamic_update_slice_in_dim(g, own[None], kchip, axis=0)

    def tr(t):
        return jnp.swapaxes(t[0], 0, 1)

    later = dict(ffn0=[ffn_w_up[0].astype(MXU), ffn_w_down[0].astype(MXU)],
                 ssm=[tr(ssm_w_in).astype(MXU), ssm_w_out[0].astype(MXU)],
                 ffn1=[ffn_w_up[1].astype(MXU), ffn_w_down[1].astype(MXU)])
    sh = [tr(mix_w_in).astype(MXU), mix_w_out[0].astype(MXU), ssm_conv_w[0], ssm_conv_b, ssm_norm, ffn_conv_w]
    first = _gather_shards(sh, "gather_first")
    g_mi, g_mo, g_scw, g_scb, g_sn, g_fcw = [own_slot(g, own) for g, own in zip(first, sh)]
    started, token = _spread_start(list(later.values()), False, first[0], "gather_start", halved=(0,))
    started = dict(zip(later.keys(), started))
    fcw = [jnp.concatenate([g_fcw[p, i] for p in _PERM], axis=1) for i in range(2)]
    P = dict(
        nm=norm_mix, nf=norm_ffn, nfin=norm_final,
        wmiT=g_mi.reshape(MIX_IN_DIM, D_MODEL), wmo=g_mo.reshape(D_MODEL, D_MODEL),
        pool_w=pool_w[0], pool_scale=pool_scale, sinks=attn_sinks[0],
        scw=_cols_from_shards(g_scw), scb=g_scb.reshape(1, SSM_CONV_DIM), snorm=g_sn.reshape(1, SSM_D_INNER),
        dt_bias=jnp.pad(ssm_dt_bias, ((0, 0), (0, LANES - SSM_HEADS))), a_log=jnp.pad(ssm_A_log, ((0, 0), (0, LANES - SSM_HEADS))),
        d_exp=jnp.repeat(ssm_D, SSM_D_INNER // SSM_HEADS, axis=1),
        fcb=[jnp.concatenate([ffn_conv_b[i:i + 1, p * FFN_TC:(p + 1) * FFN_TC] for p in _PERM], axis=1) for i in range(2)],
    )

    def fetch(group, after):
        owns, lands = _spread_wait(started[group], False, after, f"gather_wait_{group}", halved=group == "ffn0")
        if group == "ffn0":
            lands = _sibling_fill(lands, "gather_fill_ffn0")
        a, b = [own_slot(g, own) for g, own in zip(lands, owns)]
        if group == "ssm":
            wsi = a.reshape(SSM_IN_DIM, D_MODEL)
            zx = SSM_D_INNER + SSM_CONV_DIM
            return dict(wsiT=wsi, wdtT=jnp.pad(wsi[zx:], ((0, LANES - SSM_HEADS), (0, 0))), wso=b.reshape(SSM_D_INNER, D_MODEL))
        i = int(group[-1])
        return dict(wup=jnp.concatenate([a[p] for p in _PERM], axis=1), wdn=b.reshape(D_FF, D_MODEL), fcw=fcw[i])

    cos, sin_s = _rope_tables(positions)
    sent = {}

    def send(group, grads):
        res, tok = _spread_start([grads], True, jnp.zeros((SUBLANES, LANES), F32), f"grad_start_{group}")
        sent[group] = res[0]
        return tok

    loss_row, grad_x, big, small = _local_step(x[0], cos, sin_s, loss_target[0], P, fetch, token, send)

    kidx = kchip.astype(jnp.int32).reshape(1)
    group_names = dict(ffn1=["ffn_w_up1", "ffn_w_down1"], ssm=["ssm_w_in", "ssm_w_out"], ffn0=["ffn_w_up0", "ffn_w_down0"],
                       mix=["mix_w_in", "mix_w_out"])
    names, mine = [], []
    for group, started_g in sent.items():
        grads, lands = _spread_wait(started_g, True, grad_x, f"grad_wait_{group}")
        for nm, g, land in zip(group_names[group], grads, lands):
            names.append(nm)
            mine.append(_chip_sum(g, land, kidx, f"chip_sum_{nm}"))
    theirs = _sibling_exchange(mine, "sibling_exchange")
    red = {nm: (a, b) for nm, a, b in zip(names, mine, theirs)}

    out = {}

    def big_update(pname, gparts, transposed=False):
        w = W[pname]
        lw = len(gparts)
        shp = w.shape
        rr, cc = gparts[0][0].shape
        fix = (lambda t: tr(t)[None]) if transposed else (lambda t: t.reshape(lw, rr, cc))
        res = _adamw(fix(w), fix(Mo[pname]), fix(Vo[pname]), gparts, f"adamw_{pname}")
        out[pname] = tuple((tr(r)[None] if transposed else r.reshape(shp)) for r in res)

    big_update("mix_w_in", [red["mix_w_in"]], transposed=True)
    big_update("mix_w_out", [red["mix_w_out"]])
    big_update("ssm_w_in", [red["ssm_w_in"]], transposed=True)
    big_update("ssm_w_out", [red["ssm_w_out"]])
    big_update("ffn_w_up", [red["ffn_w_up0"], red["ffn_w_up1"]])
    big_update("ffn_w_down", [red["ffn_w_down0"], red["ffn_w_down1"]])

    small_names = ["norm_mix", "norm_ffn", "norm_final", "pool_w", "pool_scale", "attn_sinks", "ssm_dt_bias", "ssm_A_log",
                   "ssm_D", "ffn_conv_b", "ssm_conv_w", "ssm_conv_b", "ssm_norm", "ffn_conv_w"]

    def as2d(t):
        if t.ndim == 1:
            return t.reshape(1, -1)
        return t.reshape(-1, t.shape[-1])

    wmv = [(as2d(W[nm]), as2d(Mo[nm]), as2d(Vo[nm])) for nm in small_names]
    summed = _small_allreduce([small[nm] for nm in small_names], [t[0].shape for t in wmv], loss_row, "small_allreduce")
    res = _small_adamw(summed[:-1], wmv, "small_adamw")
    for a, nm in enumerate(small_names):
        out[nm] = tuple(r.reshape(W[nm].shape) for r in res[4 * a:4 * a + 4])
    loss = summed[-1][0, 0]

    order = ["norm_mix", "norm_ffn", "norm_final", "mix_w_in", "pool_w", "pool_scale", "attn_sinks", "mix_w_out", "ssm_w_in",
             "ssm_conv_w", "ssm_conv_b", "ssm_dt_bias", "ssm_A_log", "ssm_D", "ssm_norm", "ssm_w_out", "ffn_w_up", "ffn_conv_w",
             "ffn_conv_b", "ffn_w_down"]
    return (loss, grad_x.reshape(x.shape), *[out[nm][0] for nm in order], *[out[nm][1] for nm in order],
            *[out[nm][2] for nm in order], *[out[nm][3] for nm in order])
```
